```python
import math
import jax
import jax.numpy as jnp
from jax import lax
import numpy as np

D_MODEL = 1024
BATCH = 8
SEQ = 8192
DEPTH = 1

D_MIX = D_MODEL
GDN_HEADS = 4
GDN_HEAD_DIM = 128
GDN_WIDTH = GDN_HEADS * GDN_HEAD_DIM
CONV_WIDTH = 5
CHUNK = 64
SWA_HEADS = 8
SWA_HEAD_DIM = 64
SWA_WIDTH = SWA_HEADS * SWA_HEAD_DIM
DILATION_PATTERNS = ((128, 1), (512, 4), (2048, 16))
BAND_BLOCK = 64
REL_BUCKETS = 32
REL_MAX_DISTANCE = 1024
D_FF = 2816
EPS = 1e-6
NEG_BIG = -1e30
SPLITS = (3 * GDN_WIDTH, GDN_WIDTH, 2 * GDN_HEADS, 2 * GDN_HEADS, 3 * SWA_WIDTH)
N_IN = sum(SPLITS)

kernel_name = "hybrid_gdn_dilated_swa_macaron"


def rms_norm(x, w):
    xf = x.astype(jnp.float32)
    y = xf * lax.rsqrt(jnp.mean(xf * xf, axis=-1, keepdims=True) + EPS)
    return (y * w.astype(jnp.float32)).astype(x.dtype)


def l2_normalize(x):
    return x * lax.rsqrt(jnp.sum(x * x, axis=-1, keepdims=True) + EPS)


def swiglu(x, w_gate, w_up, w_down):
    return (jax.nn.silu(x @ w_gate) * (x @ w_up)) @ w_down


def t5_bucket(rel):
    nb = REL_BUCKETS // 2
    bucket = (rel > 0).astype(np.int32) * nb
    n = np.abs(rel)
    max_exact = nb // 2
    large = max_exact + (np.log(np.maximum(n, 1) / max_exact)
                         / math.log(REL_MAX_DISTANCE / max_exact) * (nb - max_exact)).astype(np.int32)
    large = np.minimum(large, nb - 1)
    return (bucket + np.where(n < max_exact, n, large)).astype(np.int32)


def short_conv(x, w):
    c, kw = w.shape
    rhs = jnp.transpose(w).astype(x.dtype)[:, None, :]
    return lax.conv_general_dilated(x, rhs, window_strides=(1,),
                                    padding=((kw // 2, kw // 2),),
                                    dimension_numbers=('NWC', 'WIO', 'NWC'),
                                    feature_group_count=c)


def gated_delta_chunked(q, k, v, g, beta):
    b, h, t, dk = q.shape
    dv = v.shape[-1]
    nc = t // CHUNK
    q = q * (dk ** -0.5)

    def chunks(a):
        return a.reshape(b, h, nc, CHUNK, *a.shape[3:])

    q, k, v, g, beta = chunks(q), chunks(k), chunks(v), chunks(g), chunks(beta)
    g = jnp.cumsum(g, axis=-1)
    incl = jnp.tril(jnp.ones((CHUNK, CHUNK), dtype=bool))
    strict = jnp.tril(jnp.ones((CHUNK, CHUNK), dtype=bool), -1)
    diff = g[..., :, None] - g[..., None, :]
    decay = jnp.where(incl, jnp.exp(jnp.where(incl, diff, 0.0)), 0.0)
    kb = k * beta[..., None]
    lmat = jnp.where(strict, jnp.einsum('bhncd,bhnjd->bhncj', kb, k) * decay, 0.0)
    eye = jnp.eye(CHUNK, dtype=jnp.float32)
    tmat = lax.linalg.triangular_solve(eye + lmat, jnp.broadcast_to(eye, lmat.shape),
                                       left_side=True, lower=True, unit_diagonal=True)
    u = tmat @ (v * beta[..., None])
    w = tmat @ (kb * jnp.exp(g)[..., None])
    intra = jnp.einsum('bhncd,bhnjd->bhncj', q, k) * decay
    qg = q * jnp.exp(g)[..., None]
    g_last = g[..., -1]
    kdec = k * jnp.exp(g_last[..., None] - g)[..., None]

    def step(state, inp):
        u_c, w_c, qg_c, intra_c, kdec_c, gl_c = inp
        v_new = u_c - w_c @ state
        o_c = qg_c @ state + intra_c @ v_new
        state = state * jnp.exp(gl_c)[..., None, None] + jnp.swapaxes(kdec_c, -1, -2) @ v_new
        return state, o_c

    xs = tuple(jnp.moveaxis(a, 2, 0) for a in (u, w, qg, intra, kdec, g_last))
    state0 = jnp.zeros((b, h, dk, dv), jnp.float32)
    _, o = lax.scan(step, state0, xs)
    return jnp.moveaxis(o, 0, 2).reshape(b, h, t, dv)


def _reverse(a):
    return jnp.flip(a, axis=2)


def gdn_mixer(qkv, z, a, beta_logit, conv_w, a_log, dt_bias, norm_w):
    b, s, _ = qkv.shape
    f32 = jnp.float32
    qkv = jax.nn.silu(short_conv(qkv, conv_w)).astype(f32)
    qkv = qkv.reshape(b, s, 3, GDN_HEADS, GDN_HEAD_DIM).transpose(2, 0, 3, 1, 4)
    q, k, v = l2_normalize(qkv[0]), l2_normalize(qkv[1]), qkv[2]
    a = a.astype(f32).reshape(b, s, 2, GDN_HEADS)
    g = -jnp.exp(a_log.astype(f32)) * jax.nn.softplus(a + dt_bias.astype(f32))
    beta = jax.nn.sigmoid(beta_logit.astype(f32).reshape(b, s, 2, GDN_HEADS))
    g = g.transpose(2, 0, 3, 1)
    beta = beta.transpose(2, 0, 3, 1)
    o_fwd = gated_delta_chunked(q, k, v, g[0], beta[0])
    o_bwd = _reverse(gated_delta_chunked(_reverse(q), _reverse(k), _reverse(v),
                                         _reverse(g[1]), _reverse(beta[1])))
    o = (o_fwd + o_bwd).transpose(0, 2, 1, 3)
    zg = jax.nn.silu(z.astype(f32).reshape(b, s, GDN_HEADS, GDN_HEAD_DIM))
    o = rms_norm(o, norm_w) * zg
    return o.reshape(b, s, GDN_WIDTH).astype(z.dtype)


def dilated_band_attention(q, k, v, rel_bias, window, dilation):
    b, s, h, dh = q.shape
    radius = window // (2 * dilation)
    blk = BAND_BLOCK
    length = s // dilation
    nb = -(-length // blk)
    lp = nb * blk

    def to_blocks(a):
        a = a.reshape(b, length, dilation, h, dh).transpose(0, 2, 3, 1, 4)
        a = jnp.pad(a, ((0, 0), (0, 0), (0, 0), (0, lp - length), (0, 0)))
        return a.reshape(b, dilation, h, nb, blk, dh)

    def band(a):
        ap = jnp.pad(a, ((0, 0), (0, 0), (0, 0), (1, 1), (0, 0), (0, 0)))
        return jnp.concatenate([ap[:, :, :, :-2], ap[:, :, :, 1:-1], ap[:, :, :, 2:]], axis=-2)

    qb = to_blocks(q)
    kw = band(to_blocks(k))
    vw = band(to_blocks(v))
    qi = np.arange(blk)[:, None]
    ki = np.arange(3 * blk)[None, :]
    rel = ki - blk - qi
    key_t = (np.arange(nb)[:, None, None] - 1) * blk + ki[None]
    valid = (np.abs(rel)[None] <= radius) & (key_t >= 0) & (key_t < length)
    bias = jnp.transpose(rel_bias[t5_bucket(rel * dilation)], (2, 0, 1))[:, None]
    logits = jnp.einsum('bdhnqe,bdhnke->bdhnqk', qb, kw, preferred_element_type=jnp.float32)
    logits = jnp.where(valid, logits + bias.astype(jnp.float32), NEG_BIG)
    m = jnp.max(logits, axis=-1, keepdims=True)
    p = jnp.exp(logits - m)
    den = jnp.sum(p, axis=-1, keepdims=True)
    o = jnp.einsum('bdhnqk,bdhnke->bdhnqe', p, vw.astype(jnp.float32)) / den
    lse = (m + jnp.log(den))[..., 0]
    o = o.reshape(b, dilation, h, lp, dh)[:, :, :, :length].transpose(0, 3, 1, 2, 4).reshape(b, s, h, dh)
    lse = lse.reshape(b, dilation, h, lp)[..., :length].transpose(0, 3, 1, 2).reshape(b, s, h)
    return o, lse


def dilated_mixer(qkv, q_norm_w, k_norm_w, rel_bias):
    b, s, _ = qkv.shape
    qkv = qkv.reshape(b, s, 3, SWA_HEADS, SWA_HEAD_DIM)
    q = rms_norm(qkv[:, :, 0], q_norm_w) * (SWA_HEAD_DIM ** -0.5)
    k = rms_norm(qkv[:, :, 1], k_norm_w)
    v = qkv[:, :, 2]
    outs, lses = [], []
    for window, dilation in DILATION_PATTERNS:
        o_p, lse_p = dilated_band_attention(q, k, v, rel_bias, window, dilation)
        outs.append(o_p)
        lses.append(lse_p)
    wts = jax.nn.softmax(jnp.stack(lses, axis=0), axis=0)
    o = jnp.sum(wts[..., None] * jnp.stack(outs, axis=0), axis=0)
    return o.reshape(b, s, SWA_WIDTH).astype(qkv.dtype)


def _fwd_setup_inputs(seed: int = 0) -> dict:
    key = jax.random.key(seed)
    ks = jax.random.split(key, 24)
    f32 = jnp.float32

    def dense(k, shape, fan_in):
        return jax.random.normal(k, shape, f32) * (fan_in ** -0.5)

    def gain(k, shape):
        return 1.0 + 0.02 * jax.random.normal(k, shape, f32)

    x = jax.random.normal(ks[0], (BATCH, SEQ, D_MODEL), f32)
    ffn1_norm = gain(ks[1], (DEPTH, D_MODEL))
    ffn1_w_gate = dense(ks[2], (DEPTH, D_MODEL, D_FF), D_MODEL)
    ffn1_w_up = dense(ks[3], (DEPTH, D_MODEL, D_FF), D_MODEL)
    ffn1_w_down = dense(ks[4], (DEPTH, D_FF, D_MODEL), D_FF)
    mix_norm = gain(ks[5], (DEPTH, D_MODEL))
    w_in = dense(ks[6], (DEPTH, D_MODEL, N_IN), D_MODEL)
    conv_w = dense(ks[7], (DEPTH, 3 * GDN_WIDTH, CONV_WIDTH), CONV_WIDTH)
    a_log = jnp.log(jax.random.uniform(ks[8], (DEPTH, 2, GDN_HEADS), f32, 1.0, 16.0))
    dt = jnp.exp(jax.random.uniform(ks[9], (DEPTH, 2, GDN_HEADS), f32, math.log(1e-3), math.log(1e-1)))
    dt_bias = dt + jnp.log(-jnp.expm1(-dt))
    gdn_norm_w = gain(ks[10], (DEPTH, GDN_HEAD_DIM))
    q_norm_w = gain(ks[11], (DEPTH, SWA_HEAD_DIM))
    k_norm_w = gain(ks[12], (DEPTH, SWA_HEAD_DIM))
    rel_bias = 0.2 * jax.random.normal(ks[13], (REL_BUCKETS, SWA_HEADS), f32)
    w_out = dense(ks[14], (DEPTH, D_MIX, D_MODEL), D_MIX)
    ffn2_norm = gain(ks[15], (DEPTH, D_MODEL))
    ffn2_w_gate = dense(ks[16], (DEPTH, D_MODEL, D_FF), D_MODEL)
    ffn2_w_up = dense(ks[17], (DEPTH, D_MODEL, D_FF), D_MODEL)
    ffn2_w_down = dense(ks[18], (DEPTH, D_FF, D_MODEL), D_FF)
    final_norm = gain(ks[19], (DEPTH, D_MODEL))
    return {"x": x, "ffn1_norm": ffn1_norm, "ffn1_w_gate": ffn1_w_gate, "ffn1_w_up": ffn1_w_up,
            "ffn1_w_down": ffn1_w_down, "mix_norm": mix_norm, "w_in": w_in, "conv_w": conv_w,
            "a_log": a_log, "dt_bias": dt_bias, "gdn_norm_w": gdn_norm_w, "q_norm_w": q_norm_w,
            "k_norm_w": k_norm_w, "rel_bias": rel_bias, "w_out": w_out, "ffn2_norm": ffn2_norm,
            "ffn2_w_gate": ffn2_w_gate, "ffn2_w_up": ffn2_w_up, "ffn2_w_down": ffn2_w_down,
            "final_norm": final_norm}


def _fwd_reference(x, ffn1_norm, ffn1_w_gate, ffn1_w_up, ffn1_w_down, mix_norm, w_in, conv_w,
              a_log, dt_bias, gdn_norm_w, q_norm_w, k_norm_w, rel_bias, w_out, ffn2_norm,
              ffn2_w_gate, ffn2_w_up, ffn2_w_down, final_norm):
    split_at = np.cumsum(SPLITS)[:-1].tolist()
    for l in range(DEPTH):
        x = x + 0.5 * swiglu(rms_norm(x, ffn1_norm[l]), ffn1_w_gate[l], ffn1_w_up[l], ffn1_w_down[l])
        h = rms_norm(x, mix_norm[l])
        proj = h @ w_in[l]
        qkv_a, z_a, a_a, b_a, qkv_b = jnp.split(proj, split_at, axis=-1)
        o_a = gdn_mixer(qkv_a, z_a, a_a, b_a, conv_w[l], a_log[l], dt_bias[l], gdn_norm_w[l])
        o_b = dilated_mixer(qkv_b, q_norm_w[l], k_norm_w[l], rel_bias)
        x = x + jnp.concatenate([o_a, o_b], axis=-1) @ w_out[l]
        x = x + 0.5 * swiglu(rms_norm(x, ffn2_norm[l]), ffn2_w_gate[l], ffn2_w_up[l], ffn2_w_down[l])
        x = rms_norm(x, final_norm[l])
    return x


import jax as _jax
import jax.numpy as _jnp

TWIN_FORMAT = 'train_step'
FWD_PARAMS = ['x', 'ffn1_norm', 'ffn1_w_gate', 'ffn1_w_up', 'ffn1_w_down', 'mix_norm', 'w_in', 'conv_w', 'a_log', 'dt_bias', 'gdn_norm_w', 'q_norm_w', 'k_norm_w', 'rel_bias', 'w_out', 'ffn2_norm', 'ffn2_w_gate', 'ffn2_w_up', 'ffn2_w_down', 'final_norm']
TWIN_WEIGHTS = ['ffn1_norm', 'ffn1_w_gate', 'ffn1_w_up', 'ffn1_w_down', 'mix_norm', 'w_in', 'conv_w', 'a_log', 'dt_bias', 'gdn_norm_w', 'q_norm_w', 'k_norm_w', 'rel_bias', 'w_out', 'ffn2_norm', 'ffn2_w_gate', 'ffn2_w_up', 'ffn2_w_down', 'final_norm']
TWIN_DIFF_INPUT = 'x'
TWIN_INPUTS = ['x', 'ffn1_norm', 'ffn1_w_gate', 'ffn1_w_up', 'ffn1_w_down', 'mix_norm', 'w_in', 'conv_w', 'a_log', 'dt_bias', 'gdn_norm_w', 'q_norm_w', 'k_norm_w', 'rel_bias', 'w_out', 'ffn2_norm', 'ffn2_w_gate', 'ffn2_w_up', 'ffn2_w_down', 'final_norm', 'loss_target', 'm_ffn1_norm', 'm_ffn1_w_gate', 'm_ffn1_w_up', 'm_ffn1_w_down', 'm_mix_norm', 'm_w_in', 'm_conv_w', 'm_a_log', 'm_dt_bias', 'm_gdn_norm_w', 'm_q_norm_w', 'm_k_norm_w', 'm_rel_bias', 'm_w_out', 'm_ffn2_norm', 'm_ffn2_w_gate', 'm_ffn2_w_up', 'm_ffn2_w_down', 'm_final_norm', 'v_ffn1_norm', 'v_ffn1_w_gate', 'v_ffn1_w_up', 'v_ffn1_w_down', 'v_mix_norm', 'v_w_in', 'v_conv_w', 'v_a_log', 'v_dt_bias', 'v_gdn_norm_w', 'v_q_norm_w', 'v_k_norm_w', 'v_rel_bias', 'v_w_out', 'v_ffn2_norm', 'v_ffn2_w_gate', 'v_ffn2_w_up', 'v_ffn2_w_down', 'v_final_norm']
TWIN_OUTPUTS = ['loss', 'grad_x', 'grad_ffn1_norm', 'grad_ffn1_w_gate', 'grad_ffn1_w_up', 'grad_ffn1_w_down', 'grad_mix_norm', 'grad_w_in', 'grad_conv_w', 'grad_a_log', 'grad_dt_bias', 'grad_gdn_norm_w', 'grad_q_norm_w', 'grad_k_norm_w', 'grad_rel_bias', 'grad_w_out', 'grad_ffn2_norm', 'grad_ffn2_w_gate', 'grad_ffn2_w_up', 'grad_ffn2_w_down', 'grad_final_norm', 'delta_ffn1_norm', 'delta_ffn1_w_gate', 'delta_ffn1_w_up', 'delta_ffn1_w_down', 'delta_mix_norm', 'delta_w_in', 'delta_conv_w', 'delta_a_log', 'delta_dt_bias', 'delta_gdn_norm_w', 'delta_q_norm_w', 'delta_k_norm_w', 'delta_rel_bias', 'delta_w_out', 'delta_ffn2_norm', 'delta_ffn2_w_gate', 'delta_ffn2_w_up', 'delta_ffn2_w_down', 'delta_final_norm', 'new_m_ffn1_norm', 'new_m_ffn1_w_gate', 'new_m_ffn1_w_up', 'new_m_ffn1_w_down', 'new_m_mix_norm', 'new_m_w_in', 'new_m_conv_w', 'new_m_a_log', 'new_m_dt_bias', 'new_m_gdn_norm_w', 'new_m_q_norm_w', 'new_m_k_norm_w', 'new_m_rel_bias', 'new_m_w_out', 'new_m_ffn2_norm', 'new_m_ffn2_w_gate', 'new_m_ffn2_w_up', 'new_m_ffn2_w_down', 'new_m_final_norm', 'new_v_ffn1_norm', 'new_v_ffn1_w_gate', 'new_v_ffn1_w_up', 'new_v_ffn1_w_down', 'new_v_mix_norm', 'new_v_w_in', 'new_v_conv_w', 'new_v_a_log', 'new_v_dt_bias', 'new_v_gdn_norm_w', 'new_v_q_norm_w', 'new_v_k_norm_w', 'new_v_rel_bias', 'new_v_w_out', 'new_v_ffn2_norm', 'new_v_ffn2_w_gate', 'new_v_ffn2_w_up', 'new_v_ffn2_w_down', 'new_v_final_norm']
TWIN_LEAF_KINDS = {'loss': 'loss', 'grad_x': 'grad_x', 'grad_ffn1_norm': 'grad_w', 'grad_ffn1_w_gate': 'grad_w', 'grad_ffn1_w_up': 'grad_w', 'grad_ffn1_w_down': 'grad_w', 'grad_mix_norm': 'grad_w', 'grad_w_in': 'grad_w', 'grad_conv_w': 'grad_w', 'grad_a_log': 'grad_w', 'grad_dt_bias': 'grad_w', 'grad_gdn_norm_w': 'grad_w', 'grad_q_norm_w': 'grad_w', 'grad_k_norm_w': 'grad_w', 'grad_rel_bias': 'grad_w', 'grad_w_out': 'grad_w', 'grad_ffn2_norm': 'grad_w', 'grad_ffn2_w_gate': 'grad_w', 'grad_ffn2_w_up': 'grad_w', 'grad_ffn2_w_down': 'grad_w', 'grad_final_norm': 'grad_w', 'delta_ffn1_norm': 'delta_w', 'delta_ffn1_w_gate': 'delta_w', 'delta_ffn1_w_up': 'delta_w', 'delta_ffn1_w_down': 'delta_w', 'delta_mix_norm': 'delta_w', 'delta_w_in': 'delta_w', 'delta_conv_w': 'delta_w', 'delta_a_log': 'delta_w', 'delta_dt_bias': 'delta_w', 'delta_gdn_norm_w': 'delta_w', 'delta_q_norm_w': 'delta_w', 'delta_k_norm_w': 'delta_w', 'delta_rel_bias': 'delta_w', 'delta_w_out': 'delta_w', 'delta_ffn2_norm': 'delta_w', 'delta_ffn2_w_gate': 'delta_w', 'delta_ffn2_w_up': 'delta_w', 'delta_ffn2_w_down': 'delta_w', 'delta_final_norm': 'delta_w', 'new_m_ffn1_norm': 'new_m', 'new_m_ffn1_w_gate': 'new_m', 'new_m_ffn1_w_up': 'new_m', 'new_m_ffn1_w_down': 'new_m', 'new_m_mix_norm': 'new_m', 'new_m_w_in': 'new_m', 'new_m_conv_w': 'new_m', 'new_m_a_log': 'new_m', 'new_m_dt_bias': 'new_m', 'new_m_gdn_norm_w': 'new_m', 'new_m_q_norm_w': 'new_m', 'new_m_k_norm_w': 'new_m', 'new_m_rel_bias': 'new_m', 'new_m_w_out': 'new_m', 'new_m_ffn2_norm': 'new_m', 'new_m_ffn2_w_gate': 'new_m', 'new_m_ffn2_w_up': 'new_m', 'new_m_ffn2_w_down': 'new_m', 'new_m_final_norm': 'new_m', 'new_v_ffn1_norm': 'new_v', 'new_v_ffn1_w_gate': 'new_v', 'new_v_ffn1_w_up': 'new_v', 'new_v_ffn1_w_down': 'new_v', 'new_v_mix_norm': 'new_v', 'new_v_w_in': 'new_v', 'new_v_conv_w': 'new_v', 'new_v_a_log': 'new_v', 'new_v_dt_bias': 'new_v', 'new_v_gdn_norm_w': 'new_v', 'new_v_q_norm_w': 'new_v', 'new_v_k_norm_w': 'new_v', 'new_v_rel_bias': 'new_v', 'new_v_w_out': 'new_v', 'new_v_ffn2_norm': 'new_v', 'new_v_ffn2_w_gate': 'new_v', 'new_v_ffn2_w_up': 'new_v', 'new_v_ffn2_w_down': 'new_v', 'new_v_final_norm': 'new_v'}


def _forward(args):
    return _fwd_reference(*[args[k] for k in FWD_PARAMS])


def _output_shape():
    def fwd():
        inp = _fwd_setup_inputs(0)
        return _fwd_reference(*[inp[k] for k in FWD_PARAMS])
    out = _jax.eval_shape(fwd)
    return out.shape, out.dtype

N_MICROBATCH = 1
ADAM_LR = 0.001
ADAM_B1 = 0.9
ADAM_B2 = 0.999
ADAM_EPS = 1e-08
ADAM_WD = 0.01
ADAM_STEP = 10
PER_EXAMPLE_BATCH_AXIS = {'x': 0, 'loss_target': 0}
SHARED_INPUTS = []
_WEIGHT_DTYPES = {'ffn1_norm': _jnp.float32, 'ffn1_w_gate': _jnp.float32, 'ffn1_w_up': _jnp.float32, 'ffn1_w_down': _jnp.float32, 'mix_norm': _jnp.float32, 'w_in': _jnp.float32, 'conv_w': _jnp.float32, 'a_log': _jnp.float32, 'dt_bias': _jnp.float32, 'gdn_norm_w': _jnp.float32, 'q_norm_w': _jnp.float32, 'k_norm_w': _jnp.float32, 'rel_bias': _jnp.float32, 'w_out': _jnp.float32, 'ffn2_norm': _jnp.float32, 'ffn2_w_gate': _jnp.float32, 'ffn2_w_up': _jnp.float32, 'ffn2_w_down': _jnp.float32, 'final_norm': _jnp.float32}
MOMENT_SCALE = {'ffn1_norm': 1.203094e-01, 'ffn1_w_gate': 5.094935e-02, 'ffn1_w_up': 4.941951e-02, 'ffn1_w_down': 8.196639e-02, 'mix_norm': 1.671396e-01, 'w_in': 8.413506e-02, 'conv_w': 1.017612e-01, 'a_log': 5.225557e-01, 'dt_bias': 4.892924e-01, 'gdn_norm_w': 3.208200e-01, 'q_norm_w': 9.374401e-02, 'k_norm_w': 9.317276e-02, 'rel_bias': 4.294118e-02, 'w_out': 9.832471e-02, 'ffn2_norm': 9.434777e-02, 'ffn2_w_gate': 3.979578e-02, 'ffn2_w_up': 3.883627e-02, 'ffn2_w_down': 6.404988e-02, 'final_norm': 6.393961e+01}


def _to_microbatches(a, axis):
    t = _jnp.moveaxis(a, axis, 0)
    t = t.reshape((N_MICROBATCH, t.shape[0] // N_MICROBATCH) + t.shape[1:])
    return _jnp.moveaxis(t, 1, axis + 1)


def setup_inputs(seed: int = 0) -> dict:
    inp = _fwd_setup_inputs(seed)
    key = _jax.random.fold_in(_jax.random.key(seed), 7919)
    shape, _ = _output_shape()
    out = dict(inp)
    out["loss_target"] = _jax.random.normal(_jax.random.fold_in(key, 0), shape, _jnp.float32)
    for i, name in enumerate(TWIN_WEIGHTS):
        w = inp[name].astype(_jnp.float32)
        if MOMENT_SCALE is None:
            s = _jnp.sqrt(_jnp.mean(_jnp.square(w)) + 1e-30)
        else:
            s = MOMENT_SCALE[name]
        km, kv = _jax.random.split(_jax.random.fold_in(key, i + 1))
        out[name] = w
        out["m_" + name] = s * _jax.random.normal(km, w.shape, _jnp.float32)
        out["v_" + name] = (s * s) * _jax.random.uniform(kv, w.shape, _jnp.float32, 0.5, 1.5)
    if N_MICROBATCH > 1:
        for name, axis in PER_EXAMPLE_BATCH_AXIS.items():
            out[name] = _to_microbatches(out[name], axis)
    return {'x': out['x'], 'ffn1_norm': out['ffn1_norm'], 'ffn1_w_gate': out['ffn1_w_gate'], 'ffn1_w_up': out['ffn1_w_up'], 'ffn1_w_down': out['ffn1_w_down'], 'mix_norm': out['mix_norm'], 'w_in': out['w_in'], 'conv_w': out['conv_w'], 'a_log': out['a_log'], 'dt_bias': out['dt_bias'], 'gdn_norm_w': out['gdn_norm_w'], 'q_norm_w': out['q_norm_w'], 'k_norm_w': out['k_norm_w'], 'rel_bias': out['rel_bias'], 'w_out': out['w_out'], 'ffn2_norm': out['ffn2_norm'], 'ffn2_w_gate': out['ffn2_w_gate'], 'ffn2_w_up': out['ffn2_w_up'], 'ffn2_w_down': out['ffn2_w_down'], 'final_norm': out['final_norm'], 'loss_target': out['loss_target'], 'm_ffn1_norm': out['m_ffn1_norm'], 'm_ffn1_w_gate': out['m_ffn1_w_gate'], 'm_ffn1_w_up': out['m_ffn1_w_up'], 'm_ffn1_w_down': out['m_ffn1_w_down'], 'm_mix_norm': out['m_mix_norm'], 'm_w_in': out['m_w_in'], 'm_conv_w': out['m_conv_w'], 'm_a_log': out['m_a_log'], 'm_dt_bias': out['m_dt_bias'], 'm_gdn_norm_w': out['m_gdn_norm_w'], 'm_q_norm_w': out['m_q_norm_w'], 'm_k_norm_w': out['m_k_norm_w'], 'm_rel_bias': out['m_rel_bias'], 'm_w_out': out['m_w_out'], 'm_ffn2_norm': out['m_ffn2_norm'], 'm_ffn2_w_gate': out['m_ffn2_w_gate'], 'm_ffn2_w_up': out['m_ffn2_w_up'], 'm_ffn2_w_down': out['m_ffn2_w_down'], 'm_final_norm': out['m_final_norm'], 'v_ffn1_norm': out['v_ffn1_norm'], 'v_ffn1_w_gate': out['v_ffn1_w_gate'], 'v_ffn1_w_up': out['v_ffn1_w_up'], 'v_ffn1_w_down': out['v_ffn1_w_down'], 'v_mix_norm': out['v_mix_norm'], 'v_w_in': out['v_w_in'], 'v_conv_w': out['v_conv_w'], 'v_a_log': out['v_a_log'], 'v_dt_bias': out['v_dt_bias'], 'v_gdn_norm_w': out['v_gdn_norm_w'], 'v_q_norm_w': out['v_q_norm_w'], 'v_k_norm_w': out['v_k_norm_w'], 'v_rel_bias': out['v_rel_bias'], 'v_w_out': out['v_w_out'], 'v_ffn2_norm': out['v_ffn2_norm'], 'v_ffn2_w_gate': out['v_ffn2_w_gate'], 'v_ffn2_w_up': out['v_ffn2_w_up'], 'v_ffn2_w_down': out['v_ffn2_w_down'], 'v_final_norm': out['v_final_norm']}


def _loss(weights, diff, rest, loss_target):
    with _jax.named_scope("forward"):
        args = {**rest, TWIN_DIFF_INPUT: diff, **{k: w.astype(_WEIGHT_DTYPES[k]) for k, w in weights.items()}}
        y = _forward(args)
    with _jax.named_scope("loss_head"):
        err = _jnp.square(y.astype(_jnp.float32) - loss_target)
        return 0.5 * _jnp.sum(_jnp.mean(err, axis=-1)) if err.ndim else 0.5 * err


def _adamw(w, g, m, v):
    m = ADAM_B1 * m + (1.0 - ADAM_B1) * g
    v = ADAM_B2 * v + (1.0 - ADAM_B2) * _jnp.square(g)
    m_hat = m / (1.0 - ADAM_B1 ** ADAM_STEP)
    v_hat = v / (1.0 - ADAM_B2 ** ADAM_STEP)
    delta = -ADAM_LR * (m_hat / (_jnp.sqrt(v_hat) + ADAM_EPS) + ADAM_WD * w)
    return delta, m, v


def reference(x, ffn1_norm, ffn1_w_gate, ffn1_w_up, ffn1_w_down, mix_norm, w_in, conv_w, a_log, dt_bias, gdn_norm_w, q_norm_w, k_norm_w, rel_bias, w_out, ffn2_norm, ffn2_w_gate, ffn2_w_up, ffn2_w_down, final_norm, loss_target, m_ffn1_norm, m_ffn1_w_gate, m_ffn1_w_up, m_ffn1_w_down, m_mix_norm, m_w_in, m_conv_w, m_a_log, m_dt_bias, m_gdn_norm_w, m_q_norm_w, m_k_norm_w, m_rel_bias, m_w_out, m_ffn2_norm, m_ffn2_w_gate, m_ffn2_w_up, m_ffn2_w_down, m_final_norm, v_ffn1_norm, v_ffn1_w_gate, v_ffn1_w_up, v_ffn1_w_down, v_mix_norm, v_w_in, v_conv_w, v_a_log, v_dt_bias, v_gdn_norm_w, v_q_norm_w, v_k_norm_w, v_rel_bias, v_w_out, v_ffn2_norm, v_ffn2_w_gate, v_ffn2_w_up, v_ffn2_w_down, v_final_norm):
    given = dict(x=x, ffn1_norm=ffn1_norm, ffn1_w_gate=ffn1_w_gate, ffn1_w_up=ffn1_w_up, ffn1_w_down=ffn1_w_down, mix_norm=mix_norm, w_in=w_in, conv_w=conv_w, a_log=a_log, dt_bias=dt_bias, gdn_norm_w=gdn_norm_w, q_norm_w=q_norm_w, k_norm_w=k_norm_w, rel_bias=rel_bias, w_out=w_out, ffn2_norm=ffn2_norm, ffn2_w_gate=ffn2_w_gate, ffn2_w_up=ffn2_w_up, ffn2_w_down=ffn2_w_down, final_norm=final_norm, loss_target=loss_target, m_ffn1_norm=m_ffn1_norm, m_ffn1_w_gate=m_ffn1_w_gate, m_ffn1_w_up=m_ffn1_w_up, m_ffn1_w_down=m_ffn1_w_down, m_mix_norm=m_mix_norm, m_w_in=m_w_in, m_conv_w=m_conv_w, m_a_log=m_a_log, m_dt_bias=m_dt_bias, m_gdn_norm_w=m_gdn_norm_w, m_q_norm_w=m_q_norm_w, m_k_norm_w=m_k_norm_w, m_rel_bias=m_rel_bias, m_w_out=m_w_out, m_ffn2_norm=m_ffn2_norm, m_ffn2_w_gate=m_ffn2_w_gate, m_ffn2_w_up=m_ffn2_w_up, m_ffn2_w_down=m_ffn2_w_down, m_final_norm=m_final_norm, v_ffn1_norm=v_ffn1_norm, v_ffn1_w_gate=v_ffn1_w_gate, v_ffn1_w_up=v_ffn1_w_up, v_ffn1_w_down=v_ffn1_w_down, v_mix_norm=v_mix_norm, v_w_in=v_w_in, v_conv_w=v_conv_w, v_a_log=v_a_log, v_dt_bias=v_dt_bias, v_gdn_norm_w=v_gdn_norm_w, v_q_norm_w=v_q_norm_w, v_k_norm_w=v_k_norm_w, v_rel_bias=v_rel_bias, v_w_out=v_w_out, v_ffn2_norm=v_ffn2_norm, v_ffn2_w_gate=v_ffn2_w_gate, v_ffn2_w_up=v_ffn2_w_up, v_ffn2_w_down=v_ffn2_w_down, v_final_norm=v_final_norm)
    weights = {n: given[n] for n in TWIN_WEIGHTS}
    shared = {n: given[n] for n in SHARED_INPUTS}
    per_example = {n: given[n] for n in ['x']}
    grad_fn = _jax.value_and_grad(_loss, argnums=(0, 1))

    def one_microbatch(ex, loss_target):
        ex = dict(ex)
        diff = ex.pop(TWIN_DIFF_INPUT)
        return grad_fn(weights, diff, {**shared, **ex}, loss_target)

    if N_MICROBATCH == 1:
        loss, (grad_w, grad_x) = one_microbatch(per_example, given["loss_target"])
    else:
        def body(carry, xs):
            loss_sum, grad_sum = carry
            l_k, (gw_k, gx_k) = one_microbatch(xs[0], xs[1])
            with _jax.named_scope("update"):
                return (loss_sum + l_k, _jax.tree.map(_jnp.add, grad_sum, gw_k)), gx_k

        init = (_jnp.zeros((), _jnp.float32), _jax.tree.map(_jnp.zeros_like, weights))
        (loss, grad_w), grad_x = _jax.lax.scan(body, init, (per_example, given["loss_target"]))
    with _jax.named_scope("update"):
        delta_w, new_m, new_v = {}, {}, {}
        for n in TWIN_WEIGHTS:
            delta_w[n], new_m[n], new_v[n] = _adamw(weights[n], grad_w[n], given["m_" + n], given["v_" + n])
    return (loss, grad_x, *[grad_w[n] for n in TWIN_WEIGHTS], *[delta_w[n] for n in TWIN_WEIGHTS],
            *[new_m[n] for n in TWIN_WEIGHTS], *[new_v[n] for n in TWIN_WEIGHTS])
```

```python
import functools
import math

import numpy as np
import jax
import jax.numpy as jnp
from jax import lax
from jax.experimental import pallas as pl
from jax.experimental.pallas import tpu as pltpu

F32 = jnp.float32
BF16 = jnp.bfloat16
HIGHEST = lax.Precision.HIGHEST
MESH = pl.DeviceIdType.MESH

EPS = 1e-6
NEG_BIG = -1e30
GDN_HEADS = 4
GDN_DIM = 128
CHUNK = 64
SWA_HEADS = 8
SWA_DIM = 64
PATTERNS = ((128, 1), (512, 4), (2048, 16))
RADIUS = 64
REL_BUCKETS = 32
REL_MAX_DISTANCE = 1024
CONV_TAPS = 5
N_SHARDS = 4
LANES = 128
VMEM_LIMIT = 56 * 1024 * 1024

ADAM_LR, ADAM_B1, ADAM_B2, ADAM_EPS, ADAM_WD, ADAM_STEP = 0.001, 0.9, 0.999, 1e-08, 0.01, 10


def _params(sem=None, vmem=None):
    return pltpu.CompilerParams(dimension_semantics=sem, vmem_limit_bytes=vmem)


def _resident(shape):
    nd = len(shape)
    return pl.BlockSpec(shape, lambda *_: (0,) * nd, pipeline_mode=pl.Buffered(1))


def _dot(a, b):
    return jnp.dot(a.astype(BF16), b.astype(BF16), preferred_element_type=F32)


def _dot_nt(a, b):
    return lax.dot_general(a.astype(BF16), b.astype(BF16), (((1,), (1,)), ((), ())), preferred_element_type=F32)


def _dot_tn(a, b):
    return lax.dot_general(a.astype(BF16), b.astype(BF16), (((0,), (0,)), ((), ())), preferred_element_type=F32)


def _dot_hi(a, b):
    return jnp.dot(a, b, preferred_element_type=F32, precision=HIGHEST)


def _sigmoid(x):
    return 1.0 / (1.0 + jnp.exp(-x))


def _rstd(xf):
    return lax.rsqrt(jnp.mean(xf * xf, axis=-1, keepdims=True) + EPS)


def _rms_bwd(xf, r, nw, dxn):
    xhat = xf * r
    dxh = dxn * nw
    dx = r * (dxh - xhat * jnp.mean(dxh * xhat, axis=-1, keepdims=True))
    return dx, jnp.sum(dxn * xhat, axis=0, keepdims=True)


def _ffn_fwd(x, nw, wg, wu, wd, tm, name):
    t, d = x.shape
    nj, _, fs = wg.shape

    def body(x_ref, nw_ref, wg_ref, wu_ref, wd_ref, y_ref, xn_ref, g_ref, u_ref):
        xf = x_ref[...]
        xn = (xf * _rstd(xf) * nw_ref[...]).astype(BF16)
        xn_ref[...] = xn
        acc = jnp.zeros((tm, d), F32)
        for j in range(nj):
            g = jnp.dot(xn, wg_ref[j], preferred_element_type=F32)
            u = jnp.dot(xn, wu_ref[j], preferred_element_type=F32)
            h = (g * _sigmoid(g) * u).astype(BF16)
            acc = acc + jnp.dot(h, wd_ref[j], preferred_element_type=F32)
            g_ref[j] = g.astype(BF16)
            u_ref[j] = u.astype(BF16)
        y_ref[...] = xf + 0.5 * acc

    row = pl.BlockSpec((tm, d), lambda i: (i, 0))
    act = pl.BlockSpec((nj, tm, fs), lambda i: (0, i, 0))
    return pl.pallas_call(
        body, name=name, grid=(t // tm,),
        in_specs=[row, _resident((1, d)), _resident(wg.shape), _resident(wu.shape), _resident(wd.shape)],
        out_specs=[row, row, act, act],
        out_shape=[jax.ShapeDtypeStruct((t, d), F32), jax.ShapeDtypeStruct((t, d), BF16),
                   jax.ShapeDtypeStruct((nj, t, fs), BF16), jax.ShapeDtypeStruct((nj, t, fs), BF16)],
        compiler_params=_params(("arbitrary",), VMEM_LIMIT),
    )(x, nw, wg, wu, wd)


def _ffn_bwd_dx(dy, x, nw, g, u, wg, wu, wd, tm, name):
    t, d = x.shape
    nj, _, fs = wg.shape

    def body(dy_ref, x_ref, nw_ref, g_ref, u_ref, wg_ref, wu_ref, wd_ref,
             dx_ref, dyh_ref, dg_ref, du_ref, h_ref, dnw_ref):
        @pl.when(pl.program_id(0) == 0)
        def _():
            dnw_ref[...] = jnp.zeros_like(dnw_ref)

        dyv = dy_ref[...]
        dyh = (0.5 * dyv).astype(BF16)
        dyh_ref[...] = dyh
        dxn = jnp.zeros((tm, d), F32)
        for j in range(nj):
            gv = g_ref[j].astype(F32)
            uv = u_ref[j].astype(F32)
            dh = _dot_nt(dyh, wd_ref[j])
            sg = _sigmoid(gv)
            si = gv * sg
            dg = (dh * uv * (sg * (1.0 + gv * (1.0 - sg)))).astype(BF16)
            du = (dh * si).astype(BF16)
            h_ref[j] = (si * uv).astype(BF16)
            dg_ref[j] = dg
            du_ref[j] = du
            dxn = dxn + _dot_nt(dg, wg_ref[j]) + _dot_nt(du, wu_ref[j])
        xf = x_ref[...]
        dxr, dnw = _rms_bwd(xf, _rstd(xf), nw_ref[...], dxn)
        dx_ref[...] = dyv + dxr
        dnw_ref[...] += dnw

    row = pl.BlockSpec((tm, d), lambda i: (i, 0))
    act = pl.BlockSpec((nj, tm, fs), lambda i: (0, i, 0))
    act_shape = jax.ShapeDtypeStruct((nj, t, fs), BF16)
    return pl.pallas_call(
        body, name=name, grid=(t // tm,),
        in_specs=[row, row, _resident((1, d)), act, act, _resident(wg.shape), _resident(wu.shape), _resident(wd.shape)],
        out_specs=[row, row, act, act, act, pl.BlockSpec((1, d), lambda i: (0, 0))],
        out_shape=[jax.ShapeDtypeStruct((t, d), F32), jax.ShapeDtypeStruct((t, d), BF16),
                   act_shape, act_shape, act_shape, jax.ShapeDtypeStruct((1, d), F32)],
        compiler_params=_params(("arbitrary",), VMEM_LIMIT),
    )(dy, x, nw, g, u, wg, wu, wd)


def _matmul_tn(a, b, tk, name):
    a3, b3 = a.ndim == 3, b.ndim == 3
    nj = a.shape[0] if a3 else (b.shape[0] if b3 else 1)
    t, m = a.shape[-2:]
    n = b.shape[-1]
    nt = t // tk

    def body(a_ref, b_ref, o_ref, acc_ref):
        k = pl.program_id(1)

        @pl.when(k == 0)
        def _():
            acc_ref[...] = jnp.zeros_like(acc_ref)

        acc_ref[...] += lax.dot_general(a_ref[...], b_ref[...], (((0,), (0,)), ((), ())),
                                        preferred_element_type=F32)

        @pl.when(k == nt - 1)
        def _():
            o_ref[...] = acc_ref[...].astype(o_ref.dtype)

    a_spec = (pl.BlockSpec((None, tk, m), lambda j, k: (j, k, 0)) if a3
              else pl.BlockSpec((tk, m), lambda j, k: (k, 0)))
    b_spec = (pl.BlockSpec((None, tk, n), lambda j, k: (j, k, 0)) if b3
              else pl.BlockSpec((tk, n), lambda j, k: (k, 0)))
    return pl.pallas_call(
        body, name=name, grid=(nj, nt),
        in_specs=[a_spec, b_spec],
        out_specs=pl.BlockSpec((None, m, n), lambda j, k: (j, 0, 0)),
        out_shape=jax.ShapeDtypeStruct((nj, m, n), BF16),
        scratch_shapes=[pltpu.VMEM((m, n), F32)],
        compiler_params=_params(("arbitrary", "arbitrary"), VMEM_LIMIT),
    )(a, b)


P_QKVA, P_Z, P_AB, P_QKVB = (0, 1536), (1536, 2048), (2048, 2176), (2176, 3712)
P_PIECES = (P_QKVA, P_Z, P_AB, P_QKVB)
P_COLS = 3712


def _mix_in_fwd(x1, nw, wp, tm):
    t, d = x1.shape

    def body(x_ref, nw_ref, w_ref, hn_ref, *outs):
        xf = x_ref[...]
        xn = (xf * _rstd(xf) * nw_ref[...]).astype(BF16)
        hn_ref[...] = xn
        for (a, b), o_ref in zip(P_PIECES, outs):
            o_ref[...] = jnp.dot(xn, w_ref[:, a:b], preferred_element_type=F32)

    row = pl.BlockSpec((tm, d), lambda i: (i, 0))
    return pl.pallas_call(
        body, name="mix_in_fwd", grid=(t // tm,),
        in_specs=[row, _resident((1, d)), _resident(wp.shape)],
        out_specs=[row] + [pl.BlockSpec((tm, b - a), lambda i: (i, 0)) for a, b in P_PIECES],
        out_shape=[jax.ShapeDtypeStruct((t, d), BF16)]
                  + [jax.ShapeDtypeStruct((t, b - a), F32) for a, b in P_PIECES],
        compiler_params=_params(("arbitrary",), VMEM_LIMIT),
    )(x1, nw, wp)


def _mix_in_bwd_dx(dx, x1, nw, dpieces, wp, tm):
    t, d = x1.shape

    def body(dx_ref, x_ref, nw_ref, p0, p1, p2, p3, w_ref, o_ref, dnw_ref):
        @pl.when(pl.program_id(0) == 0)
        def _():
            dnw_ref[...] = jnp.zeros_like(dnw_ref)

        dh = jnp.zeros((tm, d), F32)
        for (a, b), p_ref in zip(P_PIECES, (p0, p1, p2, p3)):
            dh = dh + _dot_nt(p_ref[...], w_ref[:, a:b])
        xf = x_ref[...]
        dxr, dnw = _rms_bwd(xf, _rstd(xf), nw_ref[...], dh)
        o_ref[...] = dx_ref[...] + dxr
        dnw_ref[...] += dnw

    row = pl.BlockSpec((tm, d), lambda i: (i, 0))
    return pl.pallas_call(
        body, name="mix_in_bwd_dx", grid=(t // tm,),
        in_specs=[row, row, _resident((1, d))]
                 + [pl.BlockSpec((tm, b - a), lambda i: (i, 0)) for a, b in P_PIECES] + [_resident(wp.shape)],
        out_specs=[row, pl.BlockSpec((1, d), lambda i: (0, 0))],
        out_shape=[jax.ShapeDtypeStruct((t, d), F32), jax.ShapeDtypeStruct((1, d), F32)],
        compiler_params=_params(("arbitrary",), VMEM_LIMIT),
    )(dx, x1, nw, *dpieces, wp)


def _mix_out_fwd(x1, oa, ob, w, tm):
    t, d = x1.shape
    half = oa.shape[1]

    def body(x_ref, oa_ref, ob_ref, w_ref, o_ref):
        o_ref[...] = (x_ref[...] + _dot(oa_ref[...], w_ref[0:half, :]) + _dot(ob_ref[...], w_ref[half:2 * half, :]))

    row = pl.BlockSpec((tm, d), lambda i: (i, 0))
    hrow = pl.BlockSpec((tm, half), lambda i: (i, 0))
    return pl.pallas_call(
        body, name="mix_out_fwd", grid=(t // tm,),
        in_specs=[row, hrow, hrow, _resident(w.shape)],
        out_specs=row, out_shape=jax.ShapeDtypeStruct((t, d), F32),
        compiler_params=_params(("arbitrary",), VMEM_LIMIT),
    )(x1, oa, ob, w)


def _mix_out_bwd(dx2, w, tm):
    t, d = dx2.shape
    half = w.shape[0] // 2

    def body(dx_ref, w_ref, doa_ref, dob_ref, dxb_ref):
        dxb = dx_ref[...].astype(BF16)
        dxb_ref[...] = dxb
        doa_ref[...] = _dot_nt(dxb, w_ref[0:half, :])
        dob_ref[...] = _dot_nt(dxb, w_ref[half:2 * half, :])

    row = pl.BlockSpec((tm, d), lambda i: (i, 0))
    hrow = pl.BlockSpec((tm, half), lambda i: (i, 0))
    return pl.pallas_call(
        body, name="mix_out_bwd", grid=(t // tm,),
        in_specs=[row, _resident(w.shape)],
        out_specs=[hrow, hrow, row],
        out_shape=[jax.ShapeDtypeStruct((t, half), F32), jax.ShapeDtypeStruct((t, half), F32),
                   jax.ShapeDtypeStruct((t, d), BF16)],
        compiler_params=_params(("arbitrary",), VMEM_LIMIT),
    )(dx2, w)


def _final_loss(x3, fw, target, tm):
    t, d = x3.shape

    def body(x_ref, w_ref, t_ref, dx_ref, loss_ref, dw_ref):
        @pl.when(pl.program_id(0) == 0)
        def _():
            loss_ref[...] = jnp.zeros_like(loss_ref)
            dw_ref[...] = jnp.zeros_like(dw_ref)

        xf = x_ref[...]
        r = _rstd(xf)
        err = xf * r * w_ref[...] - t_ref[...]
        loss_ref[...] += 0.5 * jnp.sum(jnp.mean(err * err, axis=-1, keepdims=True), axis=0, keepdims=True)
        dxr, dw = _rms_bwd(xf, r, w_ref[...], err * (1.0 / d))
        dx_ref[...] = dxr
        dw_ref[...] += dw

    row = pl.BlockSpec((tm, d), lambda i: (i, 0))
    return pl.pallas_call(
        body, name="final_loss", grid=(t // tm,),
        in_specs=[row, _resident((1, d)), row],
        out_specs=[row, pl.BlockSpec((1, LANES), lambda i: (0, 0)), pl.BlockSpec((1, d), lambda i: (0, 0))],
        out_shape=[jax.ShapeDtypeStruct((t, d), F32), jax.ShapeDtypeStruct((1, LANES), F32),
                   jax.ShapeDtypeStruct((1, d), F32)],
        compiler_params=_params(("arbitrary",), VMEM_LIMIT),
    )(x3, fw, target)


HALO = 8


def _halo_row_specs(tr, cols, nrow8):
    per = tr // HALO
    return [pl.BlockSpec((tr, cols), lambda i: (i, 0)),
            pl.BlockSpec((HALO, cols), lambda i: (jnp.maximum(i * per - 1, 0), 0)),
            pl.BlockSpec((HALO, cols), lambda i: (jnp.minimum((i + 1) * per, nrow8 - 1), 0))]


def _conv_window(xm, xp, xn, first, last, cols):
    prev = jnp.where(first, 0.0, xp[:, cols])
    nxt = jnp.where(last, 0.0, xn[:, cols])
    return jnp.concatenate([prev, xm[:, cols], nxt], axis=0)


def _shift_rows(xw, off):
    n = xw.shape[0]
    sh = (-off) % n
    return xw if sh == 0 else pltpu.roll(xw, sh, 0)


def _conv_pre(xw, cw_ref, cols):
    acc = None
    for j in range(CONV_TAPS):
        term = _shift_rows(xw, j - CONV_TAPS // 2) * cw_ref[j:j + 1, cols]
        acc = term if acc is None else acc + term
    return acc


def _softplus(x):
    u = jnp.exp(-jnp.abs(x))
    w = 1.0 + u
    log1p = jnp.where(w == 1.0, u, jnp.log(w) * u / jnp.where(w == 1.0, 1.0, w - 1.0))
    return jnp.maximum(x, 0.0) + log1p


def _gdn_prep_fwd(qkva, cw, ab, gp, tr):
    t, c = qkva.shape
    nt = t // tr
    ncb = c // LANES

    def body(xm, xp, xn, cw_ref, ab_ref, gp_ref, o_ref, gb_ref):
        i = pl.program_id(0)
        first, last = i == 0, i == nt - 1
        for cb in range(ncb):
            cols = slice(cb * LANES, (cb + 1) * LANES)
            xw = _conv_window(xm, xp, xn, first, last, cols)
            pre = _conv_pre(xw, cw_ref, cols)[HALO:HALO + tr]
            y = pre * _sigmoid(pre)
            if cb < 2 * GDN_HEADS:
                y = y * lax.rsqrt(jnp.sum(y * y, axis=-1, keepdims=True) + EPS)
            if cb < GDN_HEADS:
                y = y * (GDN_DIM ** -0.5)
            o_ref[:, cols] = y
        abv = ab_ref[...]
        lane = lax.broadcasted_iota(jnp.int32, abv.shape, 1)
        g = -jnp.exp(gp_ref[0:1, :]) * _softplus(abv + gp_ref[1:2, :])
        gb_ref[...] = jnp.where(lane < 8, g, jnp.where(lane < 16, _sigmoid(abv), 0.0))

    return pl.pallas_call(
        body, name="gdn_prep_fwd", grid=(nt,),
        in_specs=_halo_row_specs(tr, c, t // HALO)
                 + [_resident(cw.shape), pl.BlockSpec((tr, LANES), lambda i: (i, 0)), _resident(gp.shape)],
        out_specs=[pl.BlockSpec((tr, c), lambda i: (i, 0)), pl.BlockSpec((tr, LANES), lambda i: (i, 0))],
        out_shape=[jax.ShapeDtypeStruct((t, c), F32), jax.ShapeDtypeStruct((t, LANES), F32)],
        compiler_params=_params(("arbitrary",), VMEM_LIMIT),
    )(qkva, qkva, qkva, cw, ab, gp)


def _gdn_prep_bwd(qkva, cw, ab, gp, dyf, dyb, dgbf, dgbb, tr):
    t, c = qkva.shape
    nt = t // tr
    ncb = c // LANES

    def body(xm, xp, xn, fm, fp, fn, bm, bp, bn, cw_ref, ab_ref, gp_ref, gf_ref, gbk_ref,
             dx_ref, dab_ref, dcw_ref, dgp_ref):
        i = pl.program_id(0)
        first, last = i == 0, i == nt - 1

        @pl.when(first)
        def _():
            dcw_ref[...] = jnp.zeros_like(dcw_ref)
            dgp_ref[...] = jnp.zeros_like(dgp_ref)

        sub8 = lax.broadcasted_iota(jnp.int32, (8, LANES), 0)
        for cb in range(ncb):
            cols = slice(cb * LANES, (cb + 1) * LANES)
            xw = _conv_window(xm, xp, xn, first, last, cols)
            dyw = (_conv_window(fm, fp, fn, first, last, cols) + _conv_window(bm, bp, bn, first, last, cols))
            pre = _conv_pre(xw, cw_ref, cols)
            sg = _sigmoid(pre)
            s = pre * sg
            if cb < 2 * GDN_HEADS:
                scale = (GDN_DIM ** -0.5) if cb < GDN_HEADS else 1.0
                r = lax.rsqrt(jnp.sum(s * s, axis=-1, keepdims=True) + EPS)
                dn = dyw * scale
                ds = r * dn - s * (r * r * r) * jnp.sum(dn * s, axis=-1, keepdims=True)
            else:
                ds = dyw
            dpre = ds * (sg * (1.0 + pre * (1.0 - sg)))
            dx = None
            dcw = jnp.zeros((8, LANES), F32)
            for j in range(CONV_TAPS):
                off = j - CONV_TAPS // 2
                term = _shift_rows(dpre, -off)[HALO:HALO + tr] * cw_ref[j:j + 1, cols]
                dx = term if dx is None else dx + term
                tap = jnp.sum(dpre[HALO:HALO + tr] * _shift_rows(xw, off)[HALO:HALO + tr], axis=0, keepdims=True)
                dcw = dcw + jnp.where(sub8 == j, tap, 0.0)
            dx_ref[:, cols] = dx.astype(BF16)
            dcw_ref[:, cols] += dcw

        abv = ab_ref[...]
        dgb = gf_ref[...] + gbk_ref[...]
        lane = lax.broadcasted_iota(jnp.int32, abv.shape, 1)
        nea = -jnp.exp(gp_ref[0:1, :])
        xs = abv + gp_ref[1:2, :]
        g = nea * _softplus(xs)
        beta = _sigmoid(abv)
        da = dgb * nea * _sigmoid(xs)
        dab = jnp.where(lane < 8, da, jnp.where(lane < 16, dgb * beta * (1.0 - beta), 0.0))
        dab_ref[...] = dab.astype(BF16)
        keep = lane[0:1, :] < 8
        dalog = jnp.where(keep, jnp.sum(dgb * g, axis=0, keepdims=True), 0.0)
        ddtb = jnp.where(keep, jnp.sum(da, axis=0, keepdims=True), 0.0)
        dgp_ref[...] += jnp.where(sub8 == 0, dalog, 0.0) + jnp.where(sub8 == 1, ddtb, 0.0)

    lrow = pl.BlockSpec((tr, LANES), lambda i: (i, 0))
    halo = _halo_row_specs(tr, c, t // HALO)
    return pl.pallas_call(
        body, name="gdn_prep_bwd", grid=(nt,),
        in_specs=halo + halo + halo + [_resident(cw.shape), lrow, _resident(gp.shape), lrow, lrow],
        out_specs=[pl.BlockSpec((tr, c), lambda i: (i, 0)), lrow,
                   pl.BlockSpec(cw.shape, lambda i: (0, 0)), pl.BlockSpec(gp.shape, lambda i: (0, 0))],
        out_shape=[jax.ShapeDtypeStruct((t, c), BF16), jax.ShapeDtypeStruct((t, LANES), BF16),
                   jax.ShapeDtypeStruct(cw.shape, F32), jax.ShapeDtypeStruct(gp.shape, F32)],
        compiler_params=_params(("arbitrary",), VMEM_LIMIT),
    )(qkva, qkva, qkva, dyf, dyf, dyf, dyb, dyb, dyb, cw, ab, gp, dgbf, dgbb)


def _chunk_masks(lower):
    ii = lax.broadcasted_iota(jnp.int32, (CHUNK, CHUNK), 0)
    jj = lax.broadcasted_iota(jnp.int32, (CHUNK, CHUNK), 1)
    incl = (ii >= jj) if lower else (ii <= jj)
    strict = (ii > jj) if lower else (ii < jj)
    return ii, jj, incl, strict


def _tri_inv(lmat, ii, jj):
    m16 = (ii // 16) == (jj // 16)
    m32 = (ii // 32) == (jj // 32)
    l16 = jnp.where(m16, lmat, 0.0)
    p2 = _dot_hi(l16, l16)
    p4 = _dot_hi(p2, p2)
    p8 = _dot_hi(p4, p4)
    x = jnp.where(ii == jj, 1.0, 0.0) - l16
    x = x + _dot_hi(x, p2)
    x = x + _dot_hi(x, p4)
    x = x + _dot_hi(x, p8)
    c32 = jnp.where(m32 & jnp.logical_not(m16), lmat, 0.0)
    x = x - _dot_hi(_dot_hi(x, c32), x)
    c64 = jnp.where(m32, 0.0, lmat)
    x = x - _dot_hi(_dot_hi(x, c64), x)
    return x


def _col_to_row(col, ii, jj):
    return jnp.sum(jnp.where(ii == jj, col, 0.0), axis=0, keepdims=True)


def _row_to_col(row, ii, jj):
    return jnp.sum(jnp.where(ii == jj, row, 0.0), axis=1, keepdims=True)


def _chain_common(q, k, v, graw_col, graw_row, bcol, masks):
    ii, jj, incl, strict = masks
    inclt = jnp.logical_not(strict)
    gcol = jnp.sum(jnp.where(incl, graw_row, 0.0), axis=1, keepdims=True)
    grow = jnp.sum(jnp.where(inclt, graw_col, 0.0), axis=0, keepdims=True)
    glast = jnp.sum(graw_row, axis=1, keepdims=True)
    decay = jnp.where(incl, jnp.exp(jnp.where(incl, gcol - grow, 0.0)), 0.0)
    kb = k * bcol
    vb = v * bcol
    eg = jnp.exp(gcol)
    ek = jnp.exp(glast - gcol)
    kbg = kb * eg
    amat = _dot_nt(kb, k)
    qk = _dot_nt(q, k)
    return dict(gcol=gcol, glast=glast, decay=decay, kb=kb, vb=vb, eg=eg, ek=ek, kbg=kbg, amat=amat, qk=qk,
                intra=qk * decay, qg=q * eg, kdec=k * ek)


def _gdn_fwd(qkvc, gb, gbt):
    t = qkvc.shape[0]
    nc = t // CHUNK
    hd = GDN_HEADS * GDN_DIM

    def body(xf_ref, xb_ref, gf_ref, gb_ref, gtf_ref, gtb_ref,
             of_ref, ob_ref, sf_ref, sb_ref, tf_ref, tb_ref, state):
        @pl.when(pl.program_id(0) == 0)
        def _():
            state[...] = jnp.zeros_like(state)

        for d, (x_ref, g_ref, gt_ref, o_ref, s_ref, t_ref) in enumerate(
                ((xf_ref, gf_ref, gtf_ref, of_ref, sf_ref, tf_ref), (xb_ref, gb_ref, gtb_ref, ob_ref, sb_ref, tb_ref))):
            masks = _chunk_masks(d == 0)
            ii, jj, incl, strict = masks
            for h in range(GDN_HEADS):
                ch = d * GDN_HEADS + h
                q = x_ref[:, h * GDN_DIM:(h + 1) * GDN_DIM]
                k = x_ref[:, hd + h * GDN_DIM:hd + (h + 1) * GDN_DIM]
                v = x_ref[:, 2 * hd + h * GDN_DIM:2 * hd + (h + 1) * GDN_DIM]
                cm = _chain_common(q, k, v, g_ref[:, ch:ch + 1], gt_ref[0, ch:ch + 1, :],
                                   g_ref[:, 8 + ch:9 + ch], masks)
                tm = _tri_inv(jnp.where(strict, cm["amat"] * cm["decay"], 0.0), ii, jj)
                s = state[ch]
                s_ref[0, h] = s
                t_ref[0, h] = tm
                u = _dot(tm, cm["vb"])
                w = _dot(tm, cm["kbg"])
                vnew = u - _dot(w, s)
                o_ref[:, h * GDN_DIM:(h + 1) * GDN_DIM] = _dot(cm["qg"], s) + _dot(cm["intra"], vnew)
                state[ch] = s * jnp.exp(cm["glast"]) + _dot_tn(cm["kdec"], vnew)

    fwd = lambda n: n
    bwd = lambda n: nc - 1 - n
    xs = lambda f: pl.BlockSpec((CHUNK, 3 * hd), lambda n: (f(n), 0))
    gs = lambda f: pl.BlockSpec((CHUNK, LANES), lambda n: (f(n), 0))
    gts = lambda f: pl.BlockSpec((1, 16, CHUNK), lambda n: (f(n), 0, 0))
    os_ = lambda f: pl.BlockSpec((CHUNK, hd), lambda n: (f(n), 0))
    ss = lambda f: pl.BlockSpec((1, GDN_HEADS, GDN_DIM, GDN_DIM), lambda n: (f(n), 0, 0, 0))
    ts = lambda f: pl.BlockSpec((1, GDN_HEADS, CHUNK, CHUNK), lambda n: (f(n), 0, 0, 0))
    s_shape = jax.ShapeDtypeStruct((nc, GDN_HEADS, GDN_DIM, GDN_DIM), F32)
    t_shape = jax.ShapeDtypeStruct((nc, GDN_HEADS, CHUNK, CHUNK), F32)
    return pl.pallas_call(
        body, name="gdn_fwd", grid=(nc,),
        in_specs=[xs(fwd), xs(bwd), gs(fwd), gs(bwd), gts(fwd), gts(bwd)],
        out_specs=[os_(fwd), os_(bwd), ss(fwd), ss(bwd), ts(fwd), ts(bwd)],
        out_shape=[jax.ShapeDtypeStruct((t, hd), F32), jax.ShapeDtypeStruct((t, hd), F32),
                   s_shape, s_shape, t_shape, t_shape],
        scratch_shapes=[pltpu.VMEM((2 * GDN_HEADS, GDN_DIM, GDN_DIM), F32)],
        compiler_params=_params(("arbitrary",), VMEM_LIMIT),
    )(qkvc, qkvc, gb, gb, gbt, gbt)


def _gdn_bwd(qkvc, gb, gbt, do, sf, sb, tf, tb):
    t = qkvc.shape[0]
    nc = t // CHUNK
    hd = GDN_HEADS * GDN_DIM

    def body(xf_ref, xb_ref, gf_ref, gb_ref, gtf_ref, gtb_ref, dof_ref, dob_ref, sf_ref, sb_ref, tf_ref, tb_ref,
             dxf_ref, dxb_ref, dgf_ref, dgb_ref, dstate):
        @pl.when(pl.program_id(0) == 0)
        def _():
            dstate[...] = jnp.zeros_like(dstate)

        lane = lax.broadcasted_iota(jnp.int32, (CHUNK, LANES), 1)
        for d, (x_ref, g_ref, gt_ref, do_ref, s_ref, t_ref, dx_ref, dg_ref) in enumerate(
                ((xf_ref, gf_ref, gtf_ref, dof_ref, sf_ref, tf_ref, dxf_ref, dgf_ref),
                 (xb_ref, gb_ref, gtb_ref, dob_ref, sb_ref, tb_ref, dxb_ref, dgb_ref))):
            masks = _chunk_masks(d == 0)
            ii, jj, incl, strict = masks
            dgates = jnp.zeros((CHUNK, LANES), F32)
            for h in range(GDN_HEADS):
                ch = d * GDN_HEADS + h
                q = x_ref[:, h * GDN_DIM:(h + 1) * GDN_DIM]
                k = x_ref[:, hd + h * GDN_DIM:hd + (h + 1) * GDN_DIM]
                v = x_ref[:, 2 * hd + h * GDN_DIM:2 * hd + (h + 1) * GDN_DIM]
                bcol = g_ref[:, 8 + ch:9 + ch]
                cm = _chain_common(q, k, v, g_ref[:, ch:ch + 1], gt_ref[0, ch:ch + 1, :], bcol, masks)
                decay, eg, ek, kb, kbg, vb = cm["decay"], cm["eg"], cm["ek"], cm["kb"], cm["kbg"], cm["vb"]
                s = s_ref[0, h]
                tm = t_ref[0, h]
                dov = do_ref[:, h * GDN_DIM:(h + 1) * GDN_DIM]
                ds = dstate[ch]
                w = _dot(tm, kbg)
                vnew = _dot(tm, vb) - _dot(w, s)
                egl = jnp.exp(cm["glast"])

                dintra = jnp.where(incl, _dot_nt(dov, vnew), 0.0)
                dvnew = _dot_tn(cm["intra"], dov) + _dot(cm["kdec"], ds)
                dqg = _dot_nt(dov, s)
                dstate[ch] = _dot_tn(cm["qg"], dov) + egl * ds - _dot_tn(w, dvnew)
                dkdec = _dot_nt(vnew, ds)
                dglast = egl * jnp.sum(jnp.sum(s * ds, axis=1, keepdims=True), axis=0, keepdims=True)
                dw = -_dot_nt(dvnew, s)
                dt = _dot_nt(dvnew, vb) + _dot_nt(dw, kbg)
                dvb = _dot_tn(tm, dvnew)
                dkbg = _dot_tn(tm, dw)
                tdt = lax.dot_general(dt, tm, (((1,), (1,)), ((), ())), precision=HIGHEST,
                                      preferred_element_type=F32)
                dl = jnp.where(strict, -lax.dot_general(tm, tdt, (((0,), (0,)), ((), ())), precision=HIGHEST,
                                                        preferred_element_type=F32), 0.0)
                da = dl * decay
                dqk = dintra * decay
                mm = (dl * cm["amat"] + dintra * cm["qk"]) * decay
                dkb = _dot(da, k) + dkbg * eg
                dk = _dot_tn(da, kb) + _dot_tn(dqk, q) + dkdec * ek + dkb * bcol
                dq = _dot(dqk, k) + dqg * eg
                dv = dvb * bcol
                dbeta = jnp.sum(dkb * k, axis=1, keepdims=True) + jnp.sum(dvb * v, axis=1, keepdims=True)
                kd2 = jnp.sum(dkdec * cm["kdec"], axis=1, keepdims=True)
                dgc = (jnp.sum(mm, axis=1, keepdims=True) - _row_to_col(jnp.sum(mm, axis=0, keepdims=True), ii, jj)
                       + jnp.sum(dqg * cm["qg"], axis=1, keepdims=True) - kd2
                       + jnp.sum(dkbg * kbg, axis=1, keepdims=True))
                dgl = dglast + jnp.sum(kd2, axis=0, keepdims=True)
                draw = jnp.sum(jnp.where(jnp.logical_not(strict), _col_to_row(dgc, ii, jj), 0.0),
                               axis=1, keepdims=True)
                draw = draw + dgl
                dx_ref[:, h * GDN_DIM:(h + 1) * GDN_DIM] = dq
                dx_ref[:, hd + h * GDN_DIM:hd + (h + 1) * GDN_DIM] = dk
                dx_ref[:, 2 * hd + h * GDN_DIM:2 * hd + (h + 1) * GDN_DIM] = dv
                dgates = dgates + jnp.where(lane == ch, draw, 0.0) + jnp.where(lane == 8 + ch, dbeta, 0.0)
            dg_ref[...] = dgates

    fwd = lambda n: nc - 1 - n
    bwd = lambda n: n
    xs = lambda f: pl.BlockSpec((CHUNK, 3 * hd), lambda n: (f(n), 0))
    gs = lambda f: pl.BlockSpec((CHUNK, LANES), lambda n: (f(n), 0))
    gts = lambda f: pl.BlockSpec((1, 16, CHUNK), lambda n: (f(n), 0, 0))
    os_ = lambda f: pl.BlockSpec((CHUNK, hd), lambda n: (f(n), 0))
    ss = lambda f: pl.BlockSpec((1, GDN_HEADS, GDN_DIM, GDN_DIM), lambda n: (f(n), 0, 0, 0))
    ts = lambda f: pl.BlockSpec((1, GDN_HEADS, CHUNK, CHUNK), lambda n: (f(n), 0, 0, 0))
    return pl.pallas_call(
        body, name="gdn_bwd", grid=(nc,),
        in_specs=[xs(fwd), xs(bwd), gs(fwd), gs(bwd), gts(fwd), gts(bwd), os_(fwd), os_(bwd),
                  ss(fwd), ss(bwd), ts(fwd), ts(bwd)],
        out_specs=[xs(fwd), xs(bwd), gs(fwd), gs(bwd)],
        out_shape=[jax.ShapeDtypeStruct((t, 3 * hd), F32), jax.ShapeDtypeStruct((t, 3 * hd), F32),
                   jax.ShapeDtypeStruct((t, LANES), F32), jax.ShapeDtypeStruct((t, LANES), F32)],
        scratch_shapes=[pltpu.VMEM((2 * GDN_HEADS, GDN_DIM, GDN_DIM), F32)],
        compiler_params=_params(("arbitrary",), VMEM_LIMIT),
    )(qkvc, qkvc, gb, gb, gbt, gbt, do, do, sf, sb, tf, tb)


def _gdn_post_fwd(of, ob, z, gw, tm):
    t, hd = of.shape

    def body(of_ref, ob_ref, z_ref, w_ref, o_ref):
        for h in range(GDN_HEADS):
            cols = slice(h * GDN_DIM, (h + 1) * GDN_DIM)
            o = of_ref[:, cols] + ob_ref[:, cols]
            zv = z_ref[:, cols]
            o_ref[:, cols] = (o * _rstd(o) * w_ref[...] * (zv * _sigmoid(zv))).astype(BF16)

    row = pl.BlockSpec((tm, hd), lambda i: (i, 0))
    return pl.pallas_call(
        body, name="gdn_post_fwd", grid=(t // tm,),
        in_specs=[row, row, row, _resident((1, GDN_DIM))],
        out_specs=row, out_shape=jax.ShapeDtypeStruct((t, hd), BF16),
        compiler_params=_params(("arbitrary",), VMEM_LIMIT),
    )(of, ob, z, gw)


def _gdn_post_bwd(doa, of, ob, z, gw, tm):
    t, hd = of.shape

    def body(d_ref, of_ref, ob_ref, z_ref, w_ref, do_ref, dz_ref, dw_ref):
        @pl.when(pl.program_id(0) == 0)
        def _():
            dw_ref[...] = jnp.zeros_like(dw_ref)

        dw = jnp.zeros((1, GDN_DIM), F32)
        for h in range(GDN_HEADS):
            cols = slice(h * GDN_DIM, (h + 1) * GDN_DIM)
            o = of_ref[:, cols] + ob_ref[:, cols]
            zv = z_ref[:, cols]
            dv = d_ref[:, cols]
            r = _rstd(o)
            sg = _sigmoid(zv)
            on = o * r * w_ref[...]
            dz_ref[:, cols] = (dv * on * (sg * (1.0 + zv * (1.0 - sg)))).astype(BF16)
            dxr, dwh = _rms_bwd(o, r, w_ref[...], dv * (zv * sg))
            do_ref[:, cols] = dxr
            dw = dw + dwh
        dw_ref[...] += dw

    row = pl.BlockSpec((tm, hd), lambda i: (i, 0))
    return pl.pallas_call(
        body, name="gdn_post_bwd", grid=(t // tm,),
        in_specs=[row, row, row, row, _resident((1, GDN_DIM))],
        out_specs=[row, row, pl.BlockSpec((1, GDN_DIM), lambda i: (0, 0))],
        out_shape=[jax.ShapeDtypeStruct((t, hd), F32), jax.ShapeDtypeStruct((t, hd), BF16),
                   jax.ShapeDtypeStruct((1, GDN_DIM), F32)],
        compiler_params=_params(("arbitrary",), VMEM_LIMIT),
    )(doa, of, ob, z, gw)


SWA_W = SWA_HEADS * SWA_DIM
QBLK = 128
KWIN = QBLK + 2 * RADIUS
WIN_OFFSETS = (0, RADIUS, 2 * RADIUS)


def _t5_bucket(rel):
    nb = REL_BUCKETS // 2
    bucket = (rel > 0).astype(np.int32) * nb
    n = np.abs(rel)
    max_exact = nb // 2
    large = max_exact + (np.log(np.maximum(n, 1) / max_exact)
                         / math.log(REL_MAX_DISTANCE / max_exact) * (nb - max_exact)).astype(np.int32)
    large = np.minimum(large, nb - 1)
    return (bucket + np.where(n < max_exact, n, large)).astype(np.int32)


def _band_tables(dilation):
    a = np.arange(QBLK)
    b = np.arange(KWIN)
    rel_a = np.stack([b[None, :] - w0 - a[:, None] for w0 in WIN_OFFSETS])
    rel_b = np.stack([a[None, :] + w0 - b[:, None] for w0 in WIN_OFFSETS])
    return ((_t5_bucket(rel_a * dilation), np.abs(rel_a) <= RADIUS),
            (_t5_bucket(rel_b * dilation), np.abs(rel_b) <= RADIUS))


def _bias_table(rel_bias, idx, valid):
    onehot = (jnp.arange(REL_BUCKETS, dtype=jnp.int32)[:, None] == jnp.asarray(idx.reshape(1, -1))).astype(F32)
    tab = jnp.dot(rel_bias.T, onehot, precision=HIGHEST)
    tab = jnp.where(jnp.asarray(valid.reshape(1, -1)), tab, NEG_BIG)
    tab = tab.reshape((SWA_HEADS,) + idx.shape)
    return jnp.transpose(tab, (1, 0, 2, 3)), onehot


def _head_mean(x2, bd_ref):
    return _dot_hi(x2, bd_ref[...])


def _swa_prep_fwd(qkvb, qw, kw, bd, tm):
    t = qkvb.shape[0]

    def body(x_ref, qw_ref, kw_ref, bd_ref, q_ref, k_ref, v_ref):
        for gidx in range(SWA_W // LANES):
            cols = slice(gidx * LANES, (gidx + 1) * LANES)
            xq = x_ref[:, cols]
            q_ref[:, cols] = (xq * lax.rsqrt(_head_mean(xq * xq, bd_ref) + EPS) * qw_ref[:, cols]
                              * (SWA_DIM ** -0.5)).astype(BF16)
            xk = x_ref[:, SWA_W + gidx * LANES:SWA_W + (gidx + 1) * LANES]
            k_ref[:, cols] = (xk * lax.rsqrt(_head_mean(xk * xk, bd_ref) + EPS) * kw_ref[:, cols]).astype(BF16)
        v_ref[...] = x_ref[:, 2 * SWA_W:3 * SWA_W].astype(BF16)

    hrow = pl.BlockSpec((tm, SWA_W), lambda i: (i, 0))
    shp = jax.ShapeDtypeStruct((t, SWA_W), BF16)
    return pl.pallas_call(
        body, name="swa_prep_fwd", grid=(t // tm,),
        in_specs=[pl.BlockSpec((tm, 3 * SWA_W), lambda i: (i, 0)), _resident((1, SWA_W)), _resident((1, SWA_W)),
                  _resident((LANES, LANES))],
        out_specs=[hrow, hrow, hrow], out_shape=[shp, shp, shp],
        compiler_params=_params(("arbitrary",), VMEM_LIMIT),
    )(qkvb, qw, kw, bd)


def _swa_prep_bwd(qkvb, qw, kw, bd, dqs, dks, dvs, tm):
    t = qkvb.shape[0]

    def body(x_ref, qw_ref, kw_ref, bd_ref, q0, q1, q2, k0, k1, k2, v0, v1, v2, dx_ref, dqw_ref, dkw_ref):
        @pl.when(pl.program_id(0) == 0)
        def _():
            dqw_ref[...] = jnp.zeros_like(dqw_ref)
            dkw_ref[...] = jnp.zeros_like(dkw_ref)

        for gidx in range(SWA_W // LANES):
            cols = slice(gidx * LANES, (gidx + 1) * LANES)
            for base, w_ref, parts, dw_ref, scale in ((0, qw_ref, (q0, q1, q2), dqw_ref, SWA_DIM ** -0.5),
                                                      (SWA_W, kw_ref, (k0, k1, k2), dkw_ref, 1.0)):
                xv = x_ref[:, base + gidx * LANES:base + (gidx + 1) * LANES]
                dy = (parts[0][:, cols] + parts[1][:, cols] + parts[2][:, cols]) * scale
                r = lax.rsqrt(_head_mean(xv * xv, bd_ref) + EPS)
                xhat = xv * r
                dxh = dy * w_ref[:, cols]
                dx = r * (dxh - xhat * _head_mean(dxh * xhat, bd_ref))
                dx_ref[:, base + gidx * LANES:base + (gidx + 1) * LANES] = dx.astype(BF16)
                dw_ref[:, cols] += jnp.sum(dy * xhat, axis=0, keepdims=True)
        dx_ref[:, 2 * SWA_W:3 * SWA_W] = (v0[...] + v1[...] + v2[...]).astype(BF16)

    hrow = pl.BlockSpec((tm, SWA_W), lambda i: (i, 0))
    wrow = pl.BlockSpec((1, SWA_W), lambda i: (0, 0))
    return pl.pallas_call(
        body, name="swa_prep_bwd", grid=(t // tm,),
        in_specs=[pl.BlockSpec((tm, 3 * SWA_W), lambda i: (i, 0)), _resident((1, SWA_W)), _resident((1, SWA_W)),
                  _resident((LANES, LANES))] + [hrow] * 9,
        out_specs=[pl.BlockSpec((tm, 3 * SWA_W), lambda i: (i, 0)), wrow, wrow],
        out_shape=[jax.ShapeDtypeStruct((t, 3 * SWA_W), BF16), jax.ShapeDtypeStruct((1, SWA_W), F32),
                   jax.ShapeDtypeStruct((1, SWA_W), F32)],
        compiler_params=_params(("arbitrary",), VMEM_LIMIT),
    )(qkvb, qw, kw, bd, *dqs, *dks, *dvs)


def _aligned(v, m):
    return v if isinstance(v, int) else pl.multiple_of(v, m)


def _band_loop(nsub, length, step):
    step(0, 0, 0)
    if nsub > 2:
        def inner(s, carry):
            step(s, pl.multiple_of(s * QBLK - RADIUS, RADIUS), 1)
            return carry
        lax.fori_loop(1, nsub - 1, inner, 0)
    step(nsub - 1, length - KWIN, 2)


def _head_select(lane, a0, a1):
    return jnp.where(lane < SWA_DIM, a0, a1)


def _swa_fwd(qv, kv, vv, bias, dilation, name):
    length = qv.shape[0]
    nsub = length // QBLK
    assert nsub >= 2 and length % QBLK == 0

    def body(q_ref, k_ref, v_ref, b_ref, o_ref, l_ref):
        lane = lax.broadcasted_iota(jnp.int32, (QBLK, LANES), 1)

        def step(s, ws, var):
            rows = pl.ds(_aligned(s * QBLK, QBLK), QBLK)
            q = q_ref[rows, :]
            kk = k_ref[pl.ds(ws, KWIN), :]
            vw = v_ref[pl.ds(ws, KWIN), :]
            outs, lses = [], []
            for hh in range(2):
                qh = jnp.where((lane < SWA_DIM) == (hh == 0), q, jnp.zeros_like(q))
                lg = _dot_nt(qh, kk) + b_ref[var, hh]
                m = jnp.max(lg, axis=-1, keepdims=True)
                p = jnp.exp(lg - m)
                den = jnp.sum(p, axis=-1, keepdims=True)
                outs.append(_dot(p, vw) / den)
                lses.append(m + jnp.log(den))
            o_ref[rows, :] = _head_select(lane, outs[0], outs[1])
            l_ref[rows, :] = _head_select(lane, lses[0], lses[1])

        _band_loop(nsub, length, step)

    blk = pl.BlockSpec((length, LANES), lambda hp, r: (0, r * (SWA_W // LANES) + hp))
    shp = jax.ShapeDtypeStruct(qv.shape, F32)
    return pl.pallas_call(
        body, name=name, grid=(SWA_W // LANES, dilation),
        in_specs=[blk, blk, blk, pl.BlockSpec((3, 2, QBLK, KWIN), lambda hp, r: (0, hp, 0, 0))],
        out_specs=[blk, blk], out_shape=[shp, shp],
        compiler_params=_params(("arbitrary", "arbitrary"), VMEM_LIMIT),
    )(qv, kv, vv, bias)


def _swa_combine(os_, ls_, tm):
    t = os_[0].shape[0]

    def body(o0, o1, o2, l0, l1, l2, o_ref, ob_ref, l_ref):
        la, lb, lc = l0[...], l1[...], l2[...]
        m = jnp.maximum(jnp.maximum(la, lb), lc)
        tot = m + jnp.log(jnp.exp(la - m) + jnp.exp(lb - m) + jnp.exp(lc - m))
        o = jnp.exp(la - tot) * o0[...] + jnp.exp(lb - tot) * o1[...] + jnp.exp(lc - tot) * o2[...]
        o_ref[...] = o
        ob_ref[...] = o.astype(BF16)
        l_ref[...] = tot

    hrow = pl.BlockSpec((tm, SWA_W), lambda i: (i, 0))
    return pl.pallas_call(
        body, name="swa_combine", grid=(t // tm,), in_specs=[hrow] * 6, out_specs=[hrow] * 3,
        out_shape=[jax.ShapeDtypeStruct((t, SWA_W), F32), jax.ShapeDtypeStruct((t, SWA_W), BF16),
                   jax.ShapeDtypeStruct((t, SWA_W), F32)],
        compiler_params=_params(("arbitrary",), VMEM_LIMIT),
    )(*os_, *ls_)


def _swa_bwd_prep(do, o, bd, tm):
    t = do.shape[0]

    def body(d_ref, o_ref, bd_ref, dd_ref, db_ref):
        for gidx in range(SWA_W // LANES):
            cols = slice(gidx * LANES, (gidx + 1) * LANES)
            dd_ref[:, cols] = _head_mean(d_ref[:, cols] * o_ref[:, cols], bd_ref) * float(SWA_DIM)
        db_ref[...] = d_ref[...].astype(BF16)

    hrow = pl.BlockSpec((tm, SWA_W), lambda i: (i, 0))
    return pl.pallas_call(
        body, name="swa_bwd_prep", grid=(t // tm,), in_specs=[hrow, hrow, _resident((LANES, LANES))],
        out_specs=[hrow, hrow],
        out_shape=[jax.ShapeDtypeStruct((t, SWA_W), F32), jax.ShapeDtypeStruct((t, SWA_W), BF16)],
        compiler_params=_params(("arbitrary",), VMEM_LIMIT),
    )(do, o, bd)


def _swa_bwd(qv, kv, vv, dov, lv, ddv, bias_a, bias_b, dilation, name):
    length = qv.shape[0]
    nsub = length // QBLK
    single = pl.Buffered(1) if dilation == 1 else None

    def body(q_ref, k_ref, v_ref, do_ref, l_ref, dd_ref, ba_ref, bb_ref, dq_ref, dk_ref, dv_ref, db_ref):
        @pl.when(pl.program_id(1) == 0)
        def _():
            db_ref[...] = jnp.zeros_like(db_ref)

        lane = lax.broadcasted_iota(jnp.int32, (QBLK, LANES), 1)
        lanew = lax.broadcasted_iota(jnp.int32, (KWIN, LANES), 1)

        def step_q(s, ws, var):
            rows = pl.ds(_aligned(s * QBLK, QBLK), QBLK)
            win = pl.ds(ws, KWIN)
            q, dov_ = q_ref[rows, :], do_ref[rows, :]
            kk, vw = k_ref[win, :], v_ref[win, :]
            lse, dd = l_ref[rows, :], dd_ref[rows, :]
            dqs = []
            for hh in range(2):
                mine = (lane < SWA_DIM) == (hh == 0)
                col = slice(hh * SWA_DIM, hh * SWA_DIM + 1)
                lg = _dot_nt(jnp.where(mine, q, jnp.zeros_like(q)), kk) + ba_ref[var, hh]
                p = jnp.exp(lg - lse[:, col])
                dp = _dot_nt(jnp.where(mine, dov_, jnp.zeros_like(dov_)), vw)
                ds = p * (dp - dd[:, col])
                dqs.append(_dot(ds, kk))
                db_ref[var, hh] += ds
            dq_ref[rows, :] = _head_select(lane, dqs[0], dqs[1])

        def step_k(s, ws, var):
            rows = pl.ds(_aligned(s * QBLK, QBLK), QBLK)
            win = pl.ds(ws, KWIN)
            kk, vw = k_ref[rows, :], v_ref[rows, :]
            qw, dow = q_ref[win, :], do_ref[win, :]
            lse, dd = l_ref[win, :], dd_ref[win, :]
            dks, dvs = [], []
            for hh in range(2):
                mine = (lanew < SWA_DIM) == (hh == 0)
                col = slice(hh * SWA_DIM, hh * SWA_DIM + 1)
                lg = _dot_nt(jnp.where(mine, qw, jnp.zeros_like(qw)), kk) + bb_ref[var, hh]
                p = jnp.exp(lg - lse[:, col])
                dp = _dot_nt(jnp.where(mine, dow, jnp.zeros_like(dow)), vw)
                ds = p * (dp - dd[:, col])
                dks.append(_dot_tn(ds, qw))
                dvs.append(_dot_tn(p, dow))
            dk_ref[rows, :] = _head_select(lane, dks[0], dks[1])
            dv_ref[rows, :] = _head_select(lane, dvs[0], dvs[1])

        _band_loop(nsub, length, step_q)
        _band_loop(nsub, length, step_k)

    imap = lambda hp, r: (0, r * (SWA_W // LANES) + hp)
    blk_in = pl.BlockSpec((length, LANES), imap, pipeline_mode=single)
    blk_out = pl.BlockSpec((length, LANES), imap)
    shp = jax.ShapeDtypeStruct(qv.shape, F32)
    return pl.pallas_call(
        body, name=name, grid=(SWA_W // LANES, dilation),
        in_specs=[blk_in] * 6 + [pl.BlockSpec((3, 2, QBLK, KWIN), lambda hp, r: (0, hp, 0, 0)),
                                 pl.BlockSpec((3, 2, KWIN, QBLK), lambda hp, r: (0, hp, 0, 0))],
        out_specs=[blk_out, blk_out, blk_out, pl.BlockSpec((3, 2, QBLK, KWIN), lambda hp, r: (0, hp, 0, 0))],
        out_shape=[shp, shp, shp, jax.ShapeDtypeStruct((3, SWA_HEADS, QBLK, KWIN), F32)],
        compiler_params=_params(("arbitrary", "arbitrary"), VMEM_LIMIT),
    )(qv, kv, vv, dov, lv, ddv, bias_a, bias_b)


def _bias_grad(ds2, onehot, tk):
    n = ds2.shape[1]
    nk = n // tk

    def body(a_ref, b_ref, o_ref):
        @pl.when(pl.program_id(0) == 0)
        def _():
            o_ref[...] = jnp.zeros_like(o_ref)

        o_ref[...] += lax.dot_general(a_ref[...], b_ref[...], (((1,), (1,)), ((), ())), precision=HIGHEST,
                                      preferred_element_type=F32)

    return pl.pallas_call(
        body, name="bias_grad", grid=(nk,),
        in_specs=[pl.BlockSpec((SWA_HEADS, tk), lambda k: (0, k)), pl.BlockSpec((REL_BUCKETS, tk), lambda k: (0, k))],
        out_specs=pl.BlockSpec((SWA_HEADS, REL_BUCKETS), lambda k: (0, 0)),
        out_shape=jax.ShapeDtypeStruct((SWA_HEADS, REL_BUCKETS), F32),
        compiler_params=_params(("arbitrary",), VMEM_LIMIT),
    )(ds2, onehot)


def _swa_branch_fwd(qkvb, qw_t, kw_t, rel_bias, bd, tm):
    t = qkvb.shape[0]
    q, k, v = _swa_prep_fwd(qkvb, qw_t, kw_t, bd, tm)
    os_, ls_, tabs = [], [], []
    for _, d in PATTERNS:
        (idx_a, val_a), (idx_b, val_b) = _band_tables(d)
        bias_a, onehot_a = _bias_table(rel_bias, idx_a, val_a)
        bias_b, _ = _bias_table(rel_bias, idx_b, val_b)
        view = lambda a: a.reshape(t // d, d * SWA_W)
        o_p, l_p = _swa_fwd(view(q), view(k), view(v), bias_a, d, f"swa_fwd_d{d}")
        os_.append(o_p.reshape(t, SWA_W))
        ls_.append(l_p.reshape(t, SWA_W))
        tabs.append((bias_a, bias_b, onehot_a))
    o, o16, lse = _swa_combine(os_, ls_, tm)
    return o, o16, (q, k, v, lse, tabs)


def _swa_branch_bwd(do, o, saved, qkvb, qw_t, kw_t, bd, tm):
    t = qkvb.shape[0]
    q, k, v, lse, tabs = saved
    dd, do16 = _swa_bwd_prep(do, o, bd, tm)
    dqs, dks, dvs, dss, ohs = [], [], [], [], []
    for (_, d), (bias_a, bias_b, onehot_a) in zip(PATTERNS, tabs):
        view = lambda a: a.reshape(t // d, d * SWA_W)
        dq, dk, dv, ds = _swa_bwd(view(q), view(k), view(v), view(do16), view(lse), view(dd),
                                  bias_a, bias_b, d, f"swa_bwd_d{d}")
        dqs.append(dq.reshape(t, SWA_W))
        dks.append(dk.reshape(t, SWA_W))
        dvs.append(dv.reshape(t, SWA_W))
        dss.append(jnp.transpose(ds, (1, 0, 2, 3)).reshape(SWA_HEADS, -1))
        ohs.append(onehot_a)
    dqkvb, dqw, dkw = _swa_prep_bwd(qkvb, qw_t, kw_t, bd, dqs, dks, dvs, tm)
    dbias = _bias_grad(jnp.concatenate(dss, axis=1), jnp.concatenate(ohs, axis=1), 8192)
    fold = lambda w: jnp.sum(w.reshape(SWA_HEADS, SWA_DIM), axis=0)
    return dqkvb, fold(dqw), fold(dkw), dbias.T


ANY = pl.BlockSpec(memory_space=pl.ANY)


def _mesh_pos():
    return lax.axis_index("x"), lax.axis_index("y"), lax.axis_index("c")


def _other_chips(x, y):
    return [(1 - x, y), (x, 1 - y), (1 - x, 1 - y)]


def _remote(src, dst, send_sem, recv_sem, device):
    return pltpu.make_async_remote_copy(src_ref=src, dst_ref=dst, send_sem=send_sem, recv_sem=recv_sem,
                                        device_id=device, device_id_type=MESH)


def _all_gather(xs):
    n = len(xs)

    def body(*refs):
        ins, outs = refs[:n], refs[n:2 * n]
        send_sems, recv_sems, local_sems = refs[2 * n:]
        x, y, c = _mesh_pos()
        me = 2 * x + y
        chips = _other_chips(x, y)
        halves = []
        locals_, sends = [], []
        for a in range(n):
            h = ins[a].shape[0] // 2
            mine, other = pl.ds(c * h, h), pl.ds((1 - c) * h, h)
            halves.append((mine, other))
            loc = pltpu.make_async_copy(ins[a], outs[a].at[me], local_sems.at[a])
            loc.start()
            locals_.append(loc)
            for j, chip in enumerate(chips):
                cp = _remote(ins[a].at[mine], outs[a].at[me, mine], send_sems.at[a, j], recv_sems.at[a, j], (*chip, c))
                cp.start()
                sends.append(cp)
        for a in range(n):
            mine, _ = halves[a]
            for j, chip in enumerate(chips):
                src = 2 * chip[0] + chip[1]
                landed = outs[a].at[src, mine]
                _remote(landed, landed, send_sems.at[a, j], recv_sems.at[a, j], (x, y, c)).wait_recv()
                fwd = _remote(landed, landed, send_sems.at[a, 3 + j], recv_sems.at[a, 3 + j], (x, y, 1 - c))
                fwd.start()
                sends.append(fwd)
        for a in range(n):
            _, other = halves[a]
            for j, chip in enumerate(chips):
                src = 2 * chip[0] + chip[1]
                landed = outs[a].at[src, other]
                _remote(landed, landed, send_sems.at[a, 3 + j], recv_sems.at[a, 3 + j], (x, y, c)).wait_recv()
        for cp in sends:
            cp.wait_send()
        for loc in locals_:
            loc.wait()

    return pl.pallas_call(
        body, name="all_gather_weights",
        in_specs=[ANY] * n, out_specs=[ANY] * n,
        out_shape=[jax.ShapeDtypeStruct((N_SHARDS,) + a.shape, a.dtype) for a in xs],
        scratch_shapes=[pltpu.SemaphoreType.DMA((n, 6)), pltpu.SemaphoreType.DMA((n, 6)),
                        pltpu.SemaphoreType.DMA((n,))],
    )(*xs)


def _rs_pair(gs):
    n = len(gs)

    def body(*refs):
        ins, owns, lands = refs[:n], refs[n:2 * n], refs[2 * n:3 * n]
        send_sems, recv_sems, local_sems = refs[3 * n:]
        x, y, c = _mesh_pos()
        cps = []
        for a in range(n):
            h = ins[a].shape[1] // 2
            loc = pltpu.make_async_copy(ins[a].at[:, pl.ds(c * h, h), :], owns[a], local_sems.at[a])
            loc.start()
            cp = _remote(ins[a].at[:, pl.ds((1 - c) * h, h), :], lands[a], send_sems.at[a], recv_sems.at[a],
                         (x, y, 1 - c))
            cp.start()
            cps.append((loc, cp))
        for loc, cp in cps:
            cp.wait()
            loc.wait()

    half = [jax.ShapeDtypeStruct((N_SHARDS, g.shape[1] // 2, g.shape[2]), g.dtype) for g in gs]
    return pl.pallas_call(
        body, name="rs_pair", in_specs=[ANY] * n, out_specs=[ANY] * (2 * n), out_shape=half + half,
        scratch_shapes=[pltpu.SemaphoreType.DMA((n,)), pltpu.SemaphoreType.DMA((n,)), pltpu.SemaphoreType.DMA((n,))],
    )(*gs)


def _rs_chips(ss):
    n = len(ss)

    def body(*refs):
        ins, outs = refs[:n], refs[n:2 * n]
        send_sems, recv_sems, local_sems = refs[2 * n:]
        x, y, c = _mesh_pos()
        me = 2 * x + y
        chips = _other_chips(x, y)
        cps, locs = [], []
        for a in range(n):
            loc = pltpu.make_async_copy(ins[a].at[me], outs[a].at[me], local_sems.at[a])
            loc.start()
            locs.append(loc)
            for j, chip in enumerate(chips):
                dst_chip = 2 * chip[0] + chip[1]
                cp = _remote(ins[a].at[dst_chip], outs[a].at[me], send_sems.at[a, j], recv_sems.at[a, j], (*chip, c))
                cp.start()
                cps.append(cp)
        for a in range(n):
            for j, chip in enumerate(chips):
                src = 2 * chip[0] + chip[1]
                _remote(outs[a].at[src], outs[a].at[src], send_sems.at[a, j], recv_sems.at[a, j], (x, y, c)).wait_recv()
        for cp in cps:
            cp.wait_send()
        for loc in locs:
            loc.wait()

    return pl.pallas_call(
        body, name="rs_chips", in_specs=[ANY] * n, out_specs=[ANY] * n,
        out_shape=[jax.ShapeDtypeStruct(s.shape, s.dtype) for s in ss],
        scratch_shapes=[pltpu.SemaphoreType.DMA((n, 3)), pltpu.SemaphoreType.DMA((n, 3)),
                        pltpu.SemaphoreType.DMA((n,))],
    )(*ss)


def _rs_join(fs):
    n = len(fs)

    def body(*refs):
        ins, outs = refs[:n], refs[n:2 * n]
        send_sems, recv_sems, local_sems = refs[2 * n:]
        x, y, c = _mesh_pos()
        cps = []
        for a in range(n):
            h = ins[a].shape[0]
            mine = pl.ds(c * h, h)
            loc = pltpu.make_async_copy(ins[a], outs[a].at[mine], local_sems.at[a])
            loc.start()
            cp = _remote(ins[a], outs[a].at[mine], send_sems.at[a], recv_sems.at[a], (x, y, 1 - c))
            cp.start()
            cps.append((loc, cp))
        for loc, cp in cps:
            cp.wait()
            loc.wait()

    return pl.pallas_call(
        body, name="rs_join", in_specs=[ANY] * n, out_specs=[ANY] * n,
        out_shape=[jax.ShapeDtypeStruct((2 * f.shape[0], f.shape[1]), f.dtype) for f in fs],
        scratch_shapes=[pltpu.SemaphoreType.DMA((n,)), pltpu.SemaphoreType.DMA((n,)), pltpu.SemaphoreType.DMA((n,))],
    )(*fs)


def _add_pair(a, b, name):
    nj, h, c = a.shape

    def body(a_ref, b_ref, o_ref):
        o_ref[...] = (a_ref[...].astype(F32) + b_ref[...].astype(F32)).astype(BF16)

    blk = pl.BlockSpec((1, h, c), lambda j: (j, 0, 0))
    return pl.pallas_call(body, name=name, grid=(nj,), in_specs=[blk, blk], out_specs=blk,
                          out_shape=jax.ShapeDtypeStruct(a.shape, BF16),
                          compiler_params=_params(("arbitrary",), VMEM_LIMIT))(a, b)


def _sum_slots(l2, name):
    nj, h, c = l2.shape
    th = h // 2 if h % 32 == 0 else h

    def body(i_ref, o_ref):
        acc = i_ref[0].astype(F32)
        for s in range(1, nj):
            acc = acc + i_ref[s].astype(F32)
        o_ref[...] = acc

    return pl.pallas_call(body, name=name, grid=(h // th,),
                          in_specs=[pl.BlockSpec((nj, th, c), lambda i: (0, i, 0))],
                          out_specs=pl.BlockSpec((th, c), lambda i: (i, 0)),
                          out_shape=jax.ShapeDtypeStruct((h, c), F32),
                          compiler_params=_params(("arbitrary",), VMEM_LIMIT))(l2)


def _all_reduce_small(p):
    r = p.shape[0]

    def body(p_ref, o_ref, buf, send_sems, recv_sems):
        x, y, c = _mesh_pos()
        me = 4 * x + 2 * y + c
        buf[me] = p_ref[...]
        cps = []
        k = 0
        for fx in range(2):
            for fy in range(2):
                for fc in range(2):
                    if fx + fy + fc == 0:
                        continue
                    peer = (1 - x if fx else x, 1 - y if fy else y, 1 - c if fc else c)
                    peer_id = 4 * peer[0] + 2 * peer[1] + peer[2]
                    cp = _remote(p_ref, buf.at[me], send_sems.at[k], recv_sems.at[k], peer)
                    cp.start()
                    cps.append((cp, peer_id, k))
                    k += 1
        for cp, peer_id, k in cps:
            _remote(p_ref, buf.at[peer_id], send_sems.at[k], recv_sems.at[k], (x, y, c)).wait_recv()
        for cp, _, _ in cps:
            cp.wait_send()
        acc = buf[0]
        for s in range(1, 8):
            acc = acc + buf[s]
        o_ref[...] = acc

    vm = pl.BlockSpec(memory_space=pltpu.VMEM)
    return pl.pallas_call(
        body, name="all_reduce_small", in_specs=[vm], out_specs=vm,
        out_shape=jax.ShapeDtypeStruct(p.shape, F32),
        scratch_shapes=[pltpu.VMEM((8, r, LANES), F32), pltpu.SemaphoreType.DMA((7,)), pltpu.SemaphoreType.DMA((7,))],
    )(p)


def _adamw(w, g, m, v, name):
    r, c = w.shape
    tr = max(d for d in range(8, min(r, 256) + 1, 8) if r % d == 0)
    c1 = 1.0 / (1.0 - ADAM_B1 ** ADAM_STEP)
    c2 = 1.0 / (1.0 - ADAM_B2 ** ADAM_STEP)

    def body(w_ref, g_ref, m_ref, v_ref, d_ref, nm_ref, nv_ref):
        gv = g_ref[...]
        nm = ADAM_B1 * m_ref[...] + (1.0 - ADAM_B1) * gv
        nv = ADAM_B2 * v_ref[...] + (1.0 - ADAM_B2) * (gv * gv)
        d_ref[...] = -ADAM_LR * ((nm * c1) / (jnp.sqrt(nv * c2) + ADAM_EPS) + ADAM_WD * w_ref[...])
        nm_ref[...] = nm
        nv_ref[...] = nv

    blk = pl.BlockSpec((tr, c), lambda i: (i, 0))
    shp = jax.ShapeDtypeStruct((r, c), F32)
    return pl.pallas_call(body, name=name, grid=(r // tr,), in_specs=[blk] * 4, out_specs=[blk] * 3,
                          out_shape=[shp, shp, shp], compiler_params=_params(("arbitrary",), VMEM_LIMIT))(w, g, m, v)


PACK_UNIT = 8 * LANES


def _pack(arrs):
    parts = []
    for a in arrs:
        f = a.reshape(-1).astype(F32)
        parts.append(jnp.pad(f, (0, (-f.shape[0]) % PACK_UNIT)).reshape(-1, LANES))
    return jnp.concatenate(parts, axis=0)


def _unpack(m, shapes):
    outs, row = [], 0
    for s in shapes:
        n = int(np.prod(s))
        rows = -(-n // PACK_UNIT) * 8
        outs.append(m[row:row + rows].reshape(-1)[:n].reshape(s))
        row += rows
    return outs


WEIGHTS = ["ffn1_norm", "ffn1_w_gate", "ffn1_w_up", "ffn1_w_down", "mix_norm", "w_in", "conv_w", "a_log", "dt_bias",
           "gdn_norm_w", "q_norm_w", "k_norm_w", "rel_bias", "w_out", "ffn2_norm", "ffn2_w_gate", "ffn2_w_up",
           "ffn2_w_down", "final_norm"]
BIG = ["ffn1_w_gate", "ffn1_w_up", "ffn1_w_down", "w_in", "w_out", "ffn2_w_gate", "ffn2_w_up", "ffn2_w_down"]
SMALL = [n for n in WEIGHTS if n not in BIG]
N_IN_COLS = 3600
TM = 256
TE = 512


def kernel(x, ffn1_norm, ffn1_w_gate, ffn1_w_up, ffn1_w_down, mix_norm, w_in, conv_w, a_log, dt_bias, gdn_norm_w, q_norm_w, k_norm_w, rel_bias, w_out, ffn2_norm, ffn2_w_gate, ffn2_w_up, ffn2_w_down, final_norm, loss_target, m_ffn1_norm, m_ffn1_w_gate, m_ffn1_w_up, m_ffn1_w_down, m_mix_norm, m_w_in, m_conv_w, m_a_log, m_dt_bias, m_gdn_norm_w, m_q_norm_w, m_k_norm_w, m_rel_bias, m_w_out, m_ffn2_norm, m_ffn2_w_gate, m_ffn2_w_up, m_ffn2_w_down, m_final_norm, v_ffn1_norm, v_ffn1_w_gate, v_ffn1_w_up, v_ffn1_w_down, v_mix_norm, v_w_in, v_conv_w, v_a_log, v_dt_bias, v_gdn_norm_w, v_q_norm_w, v_k_norm_w, v_rel_bias, v_w_out, v_ffn2_norm, v_ffn2_w_gate, v_ffn2_w_up, v_ffn2_w_down, v_final_norm):
    p = dict(locals())
    xs, target = x[0], loss_target[0]
    t, d = xs.shape
    nc = t // CHUNK
    me = 2 * lax.axis_index("x") + lax.axis_index("y")

    gathered = _all_gather([p[n][0].astype(BF16) for n in BIG] + [conv_w[0]])
    gw = dict(zip(BIG + ["conv_w"], gathered))
    w_in_full = jnp.transpose(gw["w_in"], (1, 0, 2)).reshape(d, N_IN_COLS)
    wp = jnp.concatenate([w_in_full[:, :2048], jnp.pad(w_in_full[:, 2048:2064], ((0, 0), (0, LANES - 16))),
                          w_in_full[:, 2064:]], axis=1)
    w_out_full = gw["w_out"].reshape(d, d)
    conv_rows = conv_w.shape[1]
    cw = jnp.pad(gw["conv_w"].reshape(N_SHARDS * conv_rows, CONV_TAPS).T, ((0, 8 - CONV_TAPS), (0, 0)))
    gp = jnp.pad(jnp.stack([a_log.reshape(8), dt_bias.reshape(8)]), ((0, 6), (0, LANES - 8)))
    gdn_w = gdn_norm_w.reshape(1, GDN_DIM)
    qw_t = jnp.tile(q_norm_w.reshape(1, SWA_DIM), (1, SWA_HEADS))
    kw_t = jnp.tile(k_norm_w.reshape(1, SWA_DIM), (1, SWA_HEADS))
    bd = jnp.asarray(np.kron(np.eye(2), np.full((SWA_DIM, SWA_DIM), 1.0 / SWA_DIM)), F32)
    f1 = (gw["ffn1_w_gate"], gw["ffn1_w_up"], gw["ffn1_w_down"])
    f2 = (gw["ffn2_w_gate"], gw["ffn2_w_up"], gw["ffn2_w_down"])

    x1, xn1, g1, u1 = _ffn_fwd(xs, ffn1_norm, *f1, TM, "ffn1_fwd")
    hn, qkva, z, ab, qkvb = _mix_in_fwd(x1, mix_norm, wp, TM)
    qkvc, gb = _gdn_prep_fwd(qkva, cw, ab, gp, TM)
    gbt = jnp.transpose(gb[:, :16].reshape(nc, CHUNK, 16), (0, 2, 1))
    o_f, o_b, s_f, s_b, t_f, t_b = _gdn_fwd(qkvc, gb, gbt)
    oa = _gdn_post_fwd(o_f, o_b, z, gdn_w, TE)
    o_swa, o_swa16, swa_saved = _swa_branch_fwd(qkvb, qw_t, kw_t, rel_bias, bd, TE)
    x2 = _mix_out_fwd(x1, oa, o_swa, w_out_full, TM)
    x3, xn2, g2, u2 = _ffn_fwd(x2, ffn2_norm, *f2, TM, "ffn2_fwd")
    dx3, loss_part, d_final = _final_loss(x3, final_norm, target, TE)

    dx2, dyh2, dg2, du2, h2, d_nw2 = _ffn_bwd_dx(dx3, x2, ffn2_norm, g2, u2, *f2, TM, "ffn2_bwd_dx")
    dwg2 = _matmul_tn(xn2, dg2, TE, "ffn2_dwg")
    dwu2 = _matmul_tn(xn2, du2, TE, "ffn2_dwu")
    dwd2 = _matmul_tn(h2, dyh2, TE, "ffn2_dwd")
    doa, dob, dx2b = _mix_out_bwd(dx2, w_out_full, TM)
    dwo = jnp.concatenate([_matmul_tn(oa, dx2b, TE, "w_out_dw_a")[0], _matmul_tn(o_swa16, dx2b, TE, "w_out_dw_b")[0]],
                          axis=0).reshape(N_SHARDS, d // N_SHARDS, d)
    do_g, dz, d_gdnw = _gdn_post_bwd(doa, o_f, o_b, z, gdn_w, TE)
    dxf, dxb, dgf, dgk = _gdn_bwd(qkvc, gb, gbt, do_g, s_f, s_b, t_f, t_b)
    dqkva, dab, dcw, dgp = _gdn_prep_bwd(qkva, cw, ab, gp, dxf, dxb, dgf, dgk, TM)
    dqkvb, d_qw, d_kw, d_rel = _swa_branch_bwd(dob, o_swa, swa_saved, qkvb, qw_t, kw_t, bd, TE)
    dpieces = (dqkva, dz, dab, dqkvb)
    dx1, d_mixnw = _mix_in_bwd_dx(dx2, x1, mix_norm, dpieces, wp, TM)
    dwp = [_matmul_tn(hn, dp, TE, f"w_in_dw_{i}")[0] for i, dp in enumerate(dpieces)]
    dw_in = jnp.concatenate([dwp[0], dwp[1], dwp[2][:, :16], dwp[3]], axis=1)
    dw_in = jnp.transpose(dw_in.reshape(d, N_SHARDS, N_IN_COLS // N_SHARDS), (1, 0, 2))
    gx, dyh1, dg1, du1, h1, d_nw1 = _ffn_bwd_dx(dx1, xs, ffn1_norm, g1, u1, *f1, TM, "ffn1_bwd_dx")
    dwg1 = _matmul_tn(xn1, dg1, TE, "ffn1_dwg")
    dwu1 = _matmul_tn(xn1, du1, TE, "ffn1_dwu")
    dwd1 = _matmul_tn(h1, dyh1, TE, "ffn1_dwd")

    partial = [dwg1, dwu1, dwd1, dw_in, dwo, dwg2, dwu2, dwd2]
    nb = len(partial)
    pair = _rs_pair(partial)
    chip_sums = [_add_pair(pair[i], pair[nb + i], f"rs_add_{i}") for i in range(nb)]
    slots = _rs_chips(chip_sums)
    halves = [_sum_slots(s, f"rs_sum_{i}") for i, s in enumerate(slots)]
    g_big = dict(zip(BIG, _rs_join(halves)))

    small_partial = {"ffn1_norm": d_nw1, "mix_norm": d_mixnw, "a_log": dgp[0, 0:8], "dt_bias": dgp[1, 0:8],
                     "gdn_norm_w": d_gdnw, "q_norm_w": d_qw, "k_norm_w": d_kw, "rel_bias": d_rel,
                     "ffn2_norm": d_nw2, "final_norm": d_final, "conv_w": dcw[0:CONV_TAPS].T}
    red = _all_reduce_small(_pack([small_partial[n] for n in SMALL] + [loss_part[0, 0:1]]))
    full_shapes = [p[n].shape if n != "conv_w" else (N_SHARDS * conv_rows, CONV_TAPS) for n in SMALL]
    red_parts = _unpack(red, full_shapes + [(1,)])
    loss = red_parts[-1].reshape(())
    g_small = dict(zip(SMALL, red_parts[:-1]))
    g_small["conv_w"] = lax.dynamic_slice_in_dim(g_small["conv_w"], me * conv_rows, conv_rows, 0).reshape(conv_w.shape)

    grads, deltas, new_m, new_v = {}, {}, {}, {}
    for n in BIG:
        grads[n] = g_big[n][None]
        dl, nm, nv = _adamw(p[n][0], g_big[n], p["m_" + n][0], p["v_" + n][0], "adamw_" + n)
        deltas[n], new_m[n], new_v[n] = dl[None], nm[None], nv[None]
    packed = [_pack([src[n] for n in SMALL]) for src in
              ({n: p[n] for n in SMALL}, g_small, {n: p["m_" + n] for n in SMALL}, {n: p["v_" + n] for n in SMALL})]
    small_shapes = [p[n].shape for n in SMALL]
    for dst, arr in zip((deltas, new_m, new_v), _adamw(*packed, "adamw_small")):
        dst.update(zip(SMALL, _unpack(arr, small_shapes)))
    grads.update(g_small)

    return (loss, gx[None], *[grads[n] for n in WEIGHTS], *[deltas[n] for n in WEIGHTS],
            *[new_m[n] for n in WEIGHTS], *[new_v[n] for n in WEIGHTS])
```

```python
import functools
import math

import numpy as np
import jax
import jax.numpy as jnp
from jax import lax
from jax.experimental import pallas as pl
from jax.experimental.pallas import tpu as pltpu

F32 = jnp.float32
BF16 = jnp.bfloat16
HIGHEST = lax.Precision.HIGHEST
MESH = pl.DeviceIdType.MESH

EPS = 1e-6
NEG_BIG = -1e30
GDN_HEADS = 4
GDN_DIM = 128
CHUNK = 64
SWA_HEADS = 8
SWA_DIM = 64
PATTERNS = ((128, 1), (512, 4), (2048, 16))
RADIUS = 64
REL_BUCKETS = 32
REL_MAX_DISTANCE = 1024
CONV_TAPS = 5
N_SHARDS = 4
LANES = 128
VMEM_LIMIT = 56 * 1024 * 1024

ADAM_LR, ADAM_B1, ADAM_B2, ADAM_EPS, ADAM_WD, ADAM_STEP = 0.001, 0.9, 0.999, 1e-08, 0.01, 10


def _params(sem=None, vmem=None):
    return pltpu.CompilerParams(dimension_semantics=sem, vmem_limit_bytes=vmem)


def _resident(shape):
    nd = len(shape)
    return pl.BlockSpec(shape, lambda *_: (0,) * nd, pipeline_mode=pl.Buffered(1))


def _dot(a, b):
    return jnp.dot(a.astype(BF16), b.astype(BF16), preferred_element_type=F32)


def _dot_nt(a, b):
    return lax.dot_general(a.astype(BF16), b.astype(BF16), (((1,), (1,)), ((), ())), preferred_element_type=F32)


def _dot_tn(a, b):
    return lax.dot_general(a.astype(BF16), b.astype(BF16), (((0,), (0,)), ((), ())), preferred_element_type=F32)


def _dot_hi(a, b):
    return jnp.dot(a, b, preferred_element_type=F32, precision=HIGHEST)


def _sigmoid(x):
    return 1.0 / (1.0 + jnp.exp(-x))


def _rstd(xf):
    return lax.rsqrt(jnp.mean(xf * xf, axis=-1, keepdims=True) + EPS)


def _rms_bwd(xf, r, nw, dxn):
    xhat = xf * r
    dxh = dxn * nw
    dx = r * (dxh - xhat * jnp.mean(dxh * xhat, axis=-1, keepdims=True))
    return dx, jnp.sum(dxn * xhat, axis=0, keepdims=True)


def _ffn_fwd(x, nw, wg, wu, wd, tm, name):
    t, d = x.shape
    nj, _, fs = wg.shape

    def body(x_ref, nw_ref, wg_ref, wu_ref, wd_ref, y_ref, xn_ref, g_ref, u_ref):
        xf = x_ref[...]
        xn = (xf * _rstd(xf) * nw_ref[...]).astype(BF16)
        xn_ref[...] = xn
        acc = jnp.zeros((tm, d), F32)
        for j in range(nj):
            g = jnp.dot(xn, wg_ref[j], preferred_element_type=F32)
            u = jnp.dot(xn, wu_ref[j], preferred_element_type=F32)
            h = (g * _sigmoid(g) * u).astype(BF16)
            acc = acc + jnp.dot(h, wd_ref[j], preferred_element_type=F32)
            g_ref[j] = g.astype(BF16)
            u_ref[j] = u.astype(BF16)
        y_ref[...] = xf + 0.5 * acc

    row = pl.BlockSpec((tm, d), lambda i: (i, 0))
    act = pl.BlockSpec((nj, tm, fs), lambda i: (0, i, 0))
    return pl.pallas_call(
        body, name=name, grid=(t // tm,),
        in_specs=[row, _resident((1, d)), _resident(wg.shape), _resident(wu.shape), _resident(wd.shape)],
        out_specs=[row, row, act, act],
        out_shape=[jax.ShapeDtypeStruct((t, d), F32), jax.ShapeDtypeStruct((t, d), BF16),
                   jax.ShapeDtypeStruct((nj, t, fs), BF16), jax.ShapeDtypeStruct((nj, t, fs), BF16)],
        compiler_params=_params(("arbitrary",), VMEM_LIMIT),
    )(x, nw, wg, wu, wd)


def _ffn_bwd_dx(dy, x, nw, g, u, wg, wu, wd, tm, name):
    t, d = x.shape
    nj, _, fs = wg.shape

    def body(dy_ref, x_ref, nw_ref, g_ref, u_ref, wg_ref, wu_ref, wd_ref,
             dx_ref, dyh_ref, dg_ref, du_ref, h_ref, dnw_ref):
        @pl.when(pl.program_id(0) == 0)
        def _():
            dnw_ref[...] = jnp.zeros_like(dnw_ref)

        dyv = dy_ref[...]
        dyh = (0.5 * dyv).astype(BF16)
        dyh_ref[...] = dyh
        dxn = jnp.zeros((tm, d), F32)
        for j in range(nj):
            gv = g_ref[j].astype(F32)
            uv = u_ref[j].astype(F32)
            dh = _dot_nt(dyh, wd_ref[j])
            sg = _sigmoid(gv)
            si = gv * sg
            dg = (dh * uv * (sg * (1.0 + gv * (1.0 - sg)))).astype(BF16)
            du = (dh * si).astype(BF16)
            h_ref[j] = (si * uv).astype(BF16)
            dg_ref[j] = dg
            du_ref[j] = du
            dxn = dxn + _dot_nt(dg, wg_ref[j]) + _dot_nt(du, wu_ref[j])
        xf = x_ref[...]
        dxr, dnw = _rms_bwd(xf, _rstd(xf), nw_ref[...], dxn)
        dx_ref[...] = dyv + dxr
        dnw_ref[...] += dnw

    row = pl.BlockSpec((tm, d), lambda i: (i, 0))
    act = pl.BlockSpec((nj, tm, fs), lambda i: (0, i, 0))
    act_shape = jax.ShapeDtypeStruct((nj, t, fs), BF16)
    return pl.pallas_call(
        body, name=name, grid=(t // tm,),
        in_specs=[row, row, _resident((1, d)), act, act, _resident(wg.shape), _resident(wu.shape), _resident(wd.shape)],
        out_specs=[row, row, act, act, act, pl.BlockSpec((1, d), lambda i: (0, 0))],
        out_shape=[jax.ShapeDtypeStruct((t, d), F32), jax.ShapeDtypeStruct((t, d), BF16),
                   act_shape, act_shape, act_shape, jax.ShapeDtypeStruct((1, d), F32)],
        compiler_params=_params(("arbitrary",), VMEM_LIMIT),
    )(dy, x, nw, g, u, wg, wu, wd)


def _matmul_tn(a, b, tk, name):
    a3, b3 = a.ndim == 3, b.ndim == 3
    nj = a.shape[0] if a3 else (b.shape[0] if b3 else 1)
    t, m = a.shape[-2:]
    n = b.shape[-1]
    nt = t // tk

    def body(a_ref, b_ref, o_ref, acc_ref):
        k = pl.program_id(1)

        @pl.when(k == 0)
        def _():
            acc_ref[...] = jnp.zeros_like(acc_ref)

        acc_ref[...] += lax.dot_general(a_ref[...], b_ref[...], (((0,), (0,)), ((), ())),
                                        preferred_element_type=F32)

        @pl.when(k == nt - 1)
        def _():
            o_ref[...] = acc_ref[...].astype(o_ref.dtype)

    a_spec = (pl.BlockSpec((None, tk, m), lambda j, k: (j, k, 0)) if a3
              else pl.BlockSpec((tk, m), lambda j, k: (k, 0)))
    b_spec = (pl.BlockSpec((None, tk, n), lambda j, k: (j, k, 0)) if b3
              else pl.BlockSpec((tk, n), lambda j, k: (k, 0)))
    return pl.pallas_call(
        body, name=name, grid=(nj, nt),
        in_specs=[a_spec, b_spec],
        out_specs=pl.BlockSpec((None, m, n), lambda j, k: (j, 0, 0)),
        out_shape=jax.ShapeDtypeStruct((nj, m, n), BF16),
        scratch_shapes=[pltpu.VMEM((m, n), F32)],
        compiler_params=_params(("arbitrary", "arbitrary"), VMEM_LIMIT),
    )(a, b)


P_QKVA, P_Z, P_AB, P_QKVB = (0, 1536), (1536, 2048), (2048, 2176), (2176, 3712)
P_PIECES = (P_QKVA, P_Z, P_AB, P_QKVB)
P_COLS = 3712


def _mix_in_fwd(x1, nw, wp, tm):
    t, d = x1.shape

    def body(x_ref, nw_ref, w_ref, hn_ref, *outs):
        xf = x_ref[...]
        xn = (xf * _rstd(xf) * nw_ref[...]).astype(BF16)
        hn_ref[...] = xn
        for (a, b), o_ref in zip(P_PIECES, outs):
            o_ref[...] = jnp.dot(xn, w_ref[:, a:b], preferred_element_type=F32)

    row = pl.BlockSpec((tm, d), lambda i: (i, 0))
    return pl.pallas_call(
        body, name="mix_in_fwd", grid=(t // tm,),
        in_specs=[row, _resident((1, d)), _resident(wp.shape)],
        out_specs=[row] + [pl.BlockSpec((tm, b - a), lambda i: (i, 0)) for a, b in P_PIECES],
        out_shape=[jax.ShapeDtypeStruct((t, d), BF16)]
                  + [jax.ShapeDtypeStruct((t, b - a), F32) for a, b in P_PIECES],
        compiler_params=_params(("arbitrary",), VMEM_LIMIT),
    )(x1, nw, wp)


def _mix_in_bwd_dx(dx, x1, nw, dpieces, wp, tm):
    t, d = x1.shape

    def body(dx_ref, x_ref, nw_ref, p0, p1, p2, p3, w_ref, o_ref, dnw_ref):
        @pl.when(pl.program_id(0) == 0)
        def _():
            dnw_ref[...] = jnp.zeros_like(dnw_ref)

        dh = jnp.zeros((tm, d), F32)
        for (a, b), p_ref in zip(P_PIECES, (p0, p1, p2, p3)):
            dh = dh + _dot_nt(p_ref[...], w_ref[:, a:b])
        xf = x_ref[...]
        dxr, dnw = _rms_bwd(xf, _rstd(xf), nw_ref[...], dh)
        o_ref[...] = dx_ref[...] + dxr
        dnw_ref[...] += dnw

    row = pl.BlockSpec((tm, d), lambda i: (i, 0))
    return pl.pallas_call(
        body, name="mix_in_bwd_dx", grid=(t // tm,),
        in_specs=[row, row, _resident((1, d))]
                 + [pl.BlockSpec((tm, b - a), lambda i: (i, 0)) for a, b in P_PIECES] + [_resident(wp.shape)],
        out_specs=[row, pl.BlockSpec((1, d), lambda i: (0, 0))],
        out_shape=[jax.ShapeDtypeStruct((t, d), F32), jax.ShapeDtypeStruct((1, d), F32)],
        compiler_params=_params(("arbitrary",), VMEM_LIMIT),
    )(dx, x1, nw, *dpieces, wp)


def _mix_out_fwd(x1, oa, ob, w, tm):
    t, d = x1.shape
    half = oa.shape[1]

    def body(x_ref, oa_ref, ob_ref, w_ref, o_ref):
        o_ref[...] = (x_ref[...] + _dot(oa_ref[...], w_ref[0:half, :]) + _dot(ob_ref[...], w_ref[half:2 * half, :]))

    row = pl.BlockSpec((tm, d), lambda i: (i, 0))
    hrow = pl.BlockSpec((tm, half), lambda i: (i, 0))
    return pl.pallas_call(
        body, name="mix_out_fwd", grid=(t // tm,),
        in_specs=[row, hrow, hrow, _resident(w.shape)],
        out_specs=row, out_shape=jax.ShapeDtypeStruct((t, d), F32),
        compiler_params=_params(("arbitrary",), VMEM_LIMIT),
    )(x1, oa, ob, w)


def _mix_out_bwd(dx2, w, tm):
    t, d = dx2.shape
    half = w.shape[0] // 2

    def body(dx_ref, w_ref, doa_ref, dob_ref, dxb_ref):
        dxb = dx_ref[...].astype(BF16)
        dxb_ref[...] = dxb
        doa_ref[...] = _dot_nt(dxb, w_ref[0:half, :])
        dob_ref[...] = _dot_nt(dxb, w_ref[half:2 * half, :])

    row = pl.BlockSpec((tm, d), lambda i: (i, 0))
    hrow = pl.BlockSpec((tm, half), lambda i: (i, 0))
    return pl.pallas_call(
        body, name="mix_out_bwd", grid=(t // tm,),
        in_specs=[row, _resident(w.shape)],
        out_specs=[hrow, hrow, row],
        out_shape=[jax.ShapeDtypeStruct((t, half), F32), jax.ShapeDtypeStruct((t, half), F32),
                   jax.ShapeDtypeStruct((t, d), BF16)],
        compiler_params=_params(("arbitrary",), VMEM_LIMIT),
    )(dx2, w)


def _final_loss(x3, fw, target, tm):
    t, d = x3.shape

    def body(x_ref, w_ref, t_ref, dx_ref, loss_ref, dw_ref):
        @pl.when(pl.program_id(0) == 0)
        def _():
            loss_ref[...] = jnp.zeros_like(loss_ref)
            dw_ref[...] = jnp.zeros_like(dw_ref)

        xf = x_ref[...]
        r = _rstd(xf)
        err = xf * r * w_ref[...] - t_ref[...]
        loss_ref[...] += 0.5 * jnp.sum(jnp.mean(err * err, axis=-1, keepdims=True), axis=0, keepdims=True)
        dxr, dw = _rms_bwd(xf, r, w_ref[...], err * (1.0 / d))
        dx_ref[...] = dxr
        dw_ref[...] += dw

    row = pl.BlockSpec((tm, d), lambda i: (i, 0))
    return pl.pallas_call(
        body, name="final_loss", grid=(t // tm,),
        in_specs=[row, _resident((1, d)), row],
        out_specs=[row, pl.BlockSpec((1, LANES), lambda i: (0, 0)), pl.BlockSpec((1, d), lambda i: (0, 0))],
        out_shape=[jax.ShapeDtypeStruct((t, d), F32), jax.ShapeDtypeStruct((1, LANES), F32),
                   jax.ShapeDtypeStruct((1, d), F32)],
        compiler_params=_params(("arbitrary",), VMEM_LIMIT),
    )(x3, fw, target)


HALO = 8


def _halo_row_specs(tr, cols, nrow8):
    per = tr // HALO
    return [pl.BlockSpec((tr, cols), lambda i: (i, 0)),
            pl.BlockSpec((HALO, cols), lambda i: (jnp.maximum(i * per - 1, 0), 0)),
            pl.BlockSpec((HALO, cols), lambda i: (jnp.minimum((i + 1) * per, nrow8 - 1), 0))]


def _conv_window(xm, xp, xn, first, last, cols):
    prev = jnp.where(first, 0.0, xp[:, cols])
    nxt = jnp.where(last, 0.0, xn[:, cols])
    return jnp.concatenate([prev, xm[:, cols], nxt], axis=0)


def _shift_rows(xw, off):
    n = xw.shape[0]
    sh = (-off) % n
    return xw if sh == 0 else pltpu.roll(xw, sh, 0)


def _conv_pre(xw, cw_ref, cols):
    acc = None
    for j in range(CONV_TAPS):
        term = _shift_rows(xw, j - CONV_TAPS // 2) * cw_ref[j:j + 1, cols]
        acc = term if acc is None else acc + term
    return acc


def _softplus(x):
    u = jnp.exp(-jnp.abs(x))
    w = 1.0 + u
    log1p = jnp.where(w == 1.0, u, jnp.log(w) * u / jnp.where(w == 1.0, 1.0, w - 1.0))
    return jnp.maximum(x, 0.0) + log1p


def _gdn_prep_fwd(qkva, cw, ab, gp, tr):
    t, c = qkva.shape
    nt = t // tr
    ncb = c // LANES

    def body(xm, xp, xn, cw_ref, ab_ref, gp_ref, o_ref, gb_ref):
        i = pl.program_id(0)
        first, last = i == 0, i == nt - 1
        for cb in range(ncb):
            cols = slice(cb * LANES, (cb + 1) * LANES)
            xw = _conv_window(xm, xp, xn, first, last, cols)
            pre = _conv_pre(xw, cw_ref, cols)[HALO:HALO + tr]
            y = pre * _sigmoid(pre)
            if cb < 2 * GDN_HEADS:
                y = y * lax.rsqrt(jnp.sum(y * y, axis=-1, keepdims=True) + EPS)
            if cb < GDN_HEADS:
                y = y * (GDN_DIM ** -0.5)
            o_ref[:, cols] = y
        abv = ab_ref[...]
        lane = lax.broadcasted_iota(jnp.int32, abv.shape, 1)
        g = -jnp.exp(gp_ref[0:1, :]) * _softplus(abv + gp_ref[1:2, :])
        gb_ref[...] = jnp.where(lane < 8, g, jnp.where(lane < 16, _sigmoid(abv), 0.0))

    return pl.pallas_call(
        body, name="gdn_prep_fwd", grid=(nt,),
        in_specs=_halo_row_specs(tr, c, t // HALO)
                 + [_resident(cw.shape), pl.BlockSpec((tr, LANES), lambda i: (i, 0)), _resident(gp.shape)],
        out_specs=[pl.BlockSpec((tr, c), lambda i: (i, 0)), pl.BlockSpec((tr, LANES), lambda i: (i, 0))],
        out_shape=[jax.ShapeDtypeStruct((t, c), F32), jax.ShapeDtypeStruct((t, LANES), F32)],
        compiler_params=_params(("arbitrary",), VMEM_LIMIT),
    )(qkva, qkva, qkva, cw, ab, gp)


def _gdn_prep_bwd(qkva, cw, ab, gp, dy, dgates, tr):
    t, c = qkva.shape
    nt = t // tr
    ncb = c // LANES

    def body(xm, xp, xn, fm, fp, fn, cw_ref, ab_ref, gp_ref, gf_ref, dx_ref, dab_ref, dcw_ref, dgp_ref):
        i = pl.program_id(0)
        first, last = i == 0, i == nt - 1

        @pl.when(first)
        def _():
            dcw_ref[...] = jnp.zeros_like(dcw_ref)
            dgp_ref[...] = jnp.zeros_like(dgp_ref)

        sub8 = lax.broadcasted_iota(jnp.int32, (8, LANES), 0)
        for cb in range(ncb):
            cols = slice(cb * LANES, (cb + 1) * LANES)
            xw = _conv_window(xm, xp, xn, first, last, cols)
            dyw = _conv_window(fm, fp, fn, first, last, cols)
            pre = _conv_pre(xw, cw_ref, cols)
            sg = _sigmoid(pre)
            s = pre * sg
            if cb < 2 * GDN_HEADS:
                scale = (GDN_DIM ** -0.5) if cb < GDN_HEADS else 1.0
                r = lax.rsqrt(jnp.sum(s * s, axis=-1, keepdims=True) + EPS)
                dn = dyw * scale
                ds = r * dn - s * (r * r * r) * jnp.sum(dn * s, axis=-1, keepdims=True)
            else:
                ds = dyw
            dpre = ds * (sg * (1.0 + pre * (1.0 - sg)))
            dx = None
            dcw = jnp.zeros((8, LANES), F32)
            for j in range(CONV_TAPS):
                off = j - CONV_TAPS // 2
                term = _shift_rows(dpre, -off)[HALO:HALO + tr] * cw_ref[j:j + 1, cols]
                dx = term if dx is None else dx + term
                tap = jnp.sum(dpre[HALO:HALO + tr] * _shift_rows(xw, off)[HALO:HALO + tr], axis=0, keepdims=True)
                dcw = dcw + jnp.where(sub8 == j, tap, 0.0)
            dx_ref[:, cols] = dx.astype(BF16)
            dcw_ref[:, cols] += dcw

        abv = ab_ref[...]
        dgb = gf_ref[...]
        lane = lax.broadcasted_iota(jnp.int32, abv.shape, 1)
        nea = -jnp.exp(gp_ref[0:1, :])
        xs = abv + gp_ref[1:2, :]
        g = nea * _softplus(xs)
        beta = _sigmoid(abv)
        da = dgb * nea * _sigmoid(xs)
        dab = jnp.where(lane < 8, da, jnp.where(lane < 16, dgb * beta * (1.0 - beta), 0.0))
        dab_ref[...] = dab.astype(BF16)
        keep = lane[0:1, :] < 8
        dalog = jnp.where(keep, jnp.sum(dgb * g, axis=0, keepdims=True), 0.0)
        ddtb = jnp.where(keep, jnp.sum(da, axis=0, keepdims=True), 0.0)
        dgp_ref[...] += jnp.where(sub8 == 0, dalog, 0.0) + jnp.where(sub8 == 1, ddtb, 0.0)

    lrow = pl.BlockSpec((tr, LANES), lambda i: (i, 0))
    halo = _halo_row_specs(tr, c, t // HALO)
    return pl.pallas_call(
        body, name="gdn_prep_bwd", grid=(nt,),
        in_specs=halo + halo + [_resident(cw.shape), lrow, _resident(gp.shape), lrow],
        out_specs=[pl.BlockSpec((tr, c), lambda i: (i, 0)), lrow,
                   pl.BlockSpec(cw.shape, lambda i: (0, 0)), pl.BlockSpec(gp.shape, lambda i: (0, 0))],
        out_shape=[jax.ShapeDtypeStruct((t, c), BF16), jax.ShapeDtypeStruct((t, LANES), BF16),
                   jax.ShapeDtypeStruct(cw.shape, F32), jax.ShapeDtypeStruct(gp.shape, F32)],
        compiler_params=_params(("arbitrary",), VMEM_LIMIT),
    )(qkva, qkva, qkva, dy, dy, dy, cw, ab, gp, dgates)


def _chunk_masks(lower):
    ii = lax.broadcasted_iota(jnp.int32, (CHUNK, CHUNK), 0)
    jj = lax.broadcasted_iota(jnp.int32, (CHUNK, CHUNK), 1)
    incl = (ii >= jj) if lower else (ii <= jj)
    strict = (ii > jj) if lower else (ii < jj)
    return ii, jj, incl, strict


def _dot3(a, b):
    ah = a.astype(BF16)
    al = (a - ah.astype(F32)).astype(BF16)
    bh = b.astype(BF16)
    bl = (b - bh.astype(F32)).astype(BF16)
    d = lambda u, v: jnp.dot(u, v, preferred_element_type=F32)
    return d(ah, bh) + (d(ah, bl) + d(al, bh))


def _tri_inv_many(lmats, ii, jj):
    m16 = (ii // 16) == (jj // 16)
    m32 = (ii // 32) == (jj // 32)
    eye = jnp.where(ii == jj, 1.0, 0.0)
    l16 = [jnp.where(m16, l, 0.0) for l in lmats]
    p2 = [_dot3(a, a) for a in l16]
    p4 = [_dot3(a, a) for a in p2]
    p8 = [_dot3(a, a) for a in p4]
    xs = [eye - a for a in l16]
    for ps in (p2, p4, p8):
        xs = [x + _dot3(x, p) for x, p in zip(xs, ps)]
    for off in ([jnp.where(m32 & jnp.logical_not(m16), l, 0.0) for l in lmats],
                [jnp.where(m32, 0.0, l) for l in lmats]):
        ys = [_dot3(x, c) for x, c in zip(xs, off)]
        xs = [x - _dot3(y, x) for x, y in zip(xs, ys)]
    return xs


def _col_to_row(col, ii, jj):
    return jnp.sum(jnp.where(ii == jj, col, 0.0), axis=0, keepdims=True)


def _row_to_col(row, ii, jj):
    return jnp.sum(jnp.where(ii == jj, row, 0.0), axis=1, keepdims=True)


def _chain_common(q, k, v, graw_col, graw_row, bcol, masks):
    ii, jj, incl, strict = masks
    inclt = jnp.logical_not(strict)
    gcol = jnp.sum(jnp.where(incl, graw_row, 0.0), axis=1, keepdims=True)
    grow = jnp.sum(jnp.where(inclt, graw_col, 0.0), axis=0, keepdims=True)
    glast = jnp.sum(graw_row, axis=1, keepdims=True)
    decay = jnp.where(incl, jnp.exp(jnp.where(incl, gcol - grow, 0.0)), 0.0)
    kb = k * bcol
    vb = v * bcol
    eg = jnp.exp(gcol)
    ek = jnp.exp(glast - gcol)
    kbg = kb * eg
    amat = _dot_nt(kb, k)
    qk = _dot_nt(q, k)
    return dict(gcol=gcol, glast=glast, decay=decay, kb=kb, vb=vb, eg=eg, ek=ek, kbg=kbg, amat=amat, qk=qk,
                intra=qk * decay, qg=q * eg, kdec=k * ek)


def _gdn_fwd(qkvc, gb, gbt):
    tm, u, w, qg, kd, intra, egl = _gdn_local_fwd(qkvc, gb, gbt)
    o_f, o_b, s_f, s_b, vn_f, vn_b = _gdn_scan_fwd(u, w, qg, kd, intra, egl, qkvc.shape[0])
    return o_f, o_b, dict(tm=tm, w=w, qg=qg, kd=kd, intra=intra, egl=egl, s=(s_f, s_b), vn=(vn_f, vn_b))


N_CHAINS = 2 * GDN_HEADS


def _load_chains(x_ref, g_ref, gt_ref):
    hd = GDN_HEADS * GDN_DIM
    chains = []
    for d in range(2):
        masks = _chunk_masks(d == 0)
        for h in range(GDN_HEADS):
            ch = d * GDN_HEADS + h
            q = x_ref[:, h * GDN_DIM:(h + 1) * GDN_DIM]
            k = x_ref[:, hd + h * GDN_DIM:hd + (h + 1) * GDN_DIM]
            v = x_ref[:, 2 * hd + h * GDN_DIM:2 * hd + (h + 1) * GDN_DIM]
            bcol = g_ref[:, 8 + ch:9 + ch]
            cm = _chain_common(q, k, v, g_ref[:, ch:ch + 1], gt_ref[0, ch:ch + 1, :], bcol, masks)
            chains.append(dict(cm, q=q, k=k, v=v, bcol=bcol, masks=masks, ch=ch, h=h))
    return chains


def _chain_shape(rows, cols, dtype):
    return lambda nc: jax.ShapeDtypeStruct((nc, N_CHAINS, rows, cols), dtype)


def _gdn_local_fwd(qkvc, gb, gbt):
    t = qkvc.shape[0]
    nc = t // CHUNK
    hd = GDN_HEADS * GDN_DIM

    def body(x_ref, g_ref, gt_ref, t_ref, u_ref, w_ref, qg_ref, kd_ref, in_ref, eg_ref):
        chains = _load_chains(x_ref, g_ref, gt_ref)
        ii, jj = chains[0]["masks"][0:2]
        tms = _tri_inv_many([jnp.where(c["masks"][3], c["amat"] * c["decay"], 0.0) for c in chains], ii, jj)
        us = [_dot(tm, c["vb"]) for tm, c in zip(tms, chains)]
        ws = [_dot(tm, c["kbg"]) for tm, c in zip(tms, chains)]
        for c, tm, u, w in zip(chains, tms, us, ws):
            ch = c["ch"]
            t_ref[0, ch] = tm
            u_ref[0, ch] = u
            w_ref[0, ch] = w.astype(BF16)
            qg_ref[0, ch] = c["qg"].astype(BF16)
            kd_ref[0, ch] = c["kdec"].astype(BF16)
            in_ref[0, ch] = c["intra"].astype(BF16)
            eg_ref[0, ch:ch + 1, :] = jnp.broadcast_to(jnp.exp(c["glast"]), (1, LANES))

    blk = lambda rows, cols: pl.BlockSpec((1, N_CHAINS, rows, cols), lambda n: (n, 0, 0, 0))
    shapes = [_chain_shape(CHUNK, CHUNK, F32), _chain_shape(CHUNK, GDN_DIM, F32), _chain_shape(CHUNK, GDN_DIM, BF16),
              _chain_shape(CHUNK, GDN_DIM, BF16), _chain_shape(CHUNK, GDN_DIM, BF16), _chain_shape(CHUNK, CHUNK, BF16)]
    return tuple(pl.pallas_call(
        body, name="gdn_local_fwd", grid=(nc,),
        in_specs=[pl.BlockSpec((CHUNK, 3 * hd), lambda n: (n, 0)), pl.BlockSpec((CHUNK, LANES), lambda n: (n, 0)),
                  pl.BlockSpec((1, 16, CHUNK), lambda n: (n, 0, 0))],
        out_specs=[blk(CHUNK, CHUNK), blk(CHUNK, GDN_DIM), blk(CHUNK, GDN_DIM), blk(CHUNK, GDN_DIM),
                   blk(CHUNK, GDN_DIM), blk(CHUNK, CHUNK), pl.BlockSpec((1, N_CHAINS, LANES), lambda n: (n, 0, 0))],
        out_shape=[s(nc) for s in shapes] + [jax.ShapeDtypeStruct((nc, N_CHAINS, LANES), F32)],
        compiler_params=_params(("arbitrary",), VMEM_LIMIT),
    )(qkvc, gb, gbt))


def _dir_specs(nc, rev):
    def spec(d, rows, cols, own=False):
        chunk = (lambda n: n) if (d == 0) != rev else (lambda n: nc - 1 - n)
        blk = 0 if own else d
        if rows is None:
            return pl.BlockSpec((1, GDN_HEADS if own else N_CHAINS, cols), lambda n: (chunk(n), 0, 0))
        return pl.BlockSpec((1, GDN_HEADS, rows, cols), lambda n: (chunk(n), blk, 0, 0))

    def rows_spec(d, cols):
        chunk = (lambda n: n) if (d == 0) != rev else (lambda n: nc - 1 - n)
        return pl.BlockSpec((CHUNK, cols), lambda n: (chunk(n), 0))
    return spec, rows_spec


def _gdn_scan_fwd(u, w, qg, kd, intra, egl, t):
    nc = t // CHUNK
    hd = GDN_HEADS * GDN_DIM

    def body(*refs):
        ins, outs, state = refs[:12], refs[12:18], refs[18]
        @pl.when(pl.program_id(0) == 0)
        def _():
            state[...] = jnp.zeros_like(state)

        chains = [(d, h) for d in range(2) for h in range(GDN_HEADS)]
        pick = lambda k, d, h: ins[2 * k + d][0, h]
        states = [state[ch] for ch in range(N_CHAINS)]
        sbs = [s.astype(BF16) for s in states]
        ws = [_dot(pick(1, d, h), sb) for (d, h), sb in zip(chains, sbs)]
        o1 = [_dot(pick(2, d, h), sb) for (d, h), sb in zip(chains, sbs)]
        vns = [(pick(0, d, h) - wsb).astype(BF16) for (d, h), wsb in zip(chains, ws)]
        o2 = [_dot(pick(4, d, h), vn) for (d, h), vn in zip(chains, vns)]
        kv = [_dot_tn(pick(3, d, h), vn) for (d, h), vn in zip(chains, vns)]
        for ch, (d, h) in enumerate(chains):
            outs[d][:, h * GDN_DIM:(h + 1) * GDN_DIM] = o1[ch] + o2[ch]
            outs[2 + d][0, h] = states[ch]
            outs[4 + d][0, h] = vns[ch]
            state[ch] = states[ch] * ins[10 + d][0, ch:ch + 1, :] + kv[ch]

    spec, rows_spec = _dir_specs(nc, False)
    pair = lambda rows, cols, own=False: [spec(0, rows, cols, own), spec(1, rows, cols, own)]
    s_shape = jax.ShapeDtypeStruct((nc, GDN_HEADS, GDN_DIM, GDN_DIM), F32)
    vn_shape = jax.ShapeDtypeStruct((nc, GDN_HEADS, CHUNK, GDN_DIM), BF16)
    return pl.pallas_call(
        body, name="gdn_scan_fwd", grid=(nc,),
        in_specs=(pair(CHUNK, GDN_DIM) + pair(CHUNK, GDN_DIM) + pair(CHUNK, GDN_DIM) + pair(CHUNK, GDN_DIM)
                  + pair(CHUNK, CHUNK) + pair(None, LANES)),
        out_specs=([rows_spec(0, hd), rows_spec(1, hd)] + pair(GDN_DIM, GDN_DIM, True)
                   + pair(CHUNK, GDN_DIM, True)),
        out_shape=[jax.ShapeDtypeStruct((t, hd), F32), jax.ShapeDtypeStruct((t, hd), F32),
                   s_shape, s_shape, vn_shape, vn_shape],
        scratch_shapes=[pltpu.VMEM((N_CHAINS, GDN_DIM, GDN_DIM), F32)],
        compiler_params=_params(("arbitrary",), VMEM_LIMIT),
    )(u, u, w, w, qg, qg, kd, kd, intra, intra, egl, egl)


def _gdn_bwd(qkvc, gb, gbt, do, saved):
    scan = _gdn_scan_bwd(do, saved, qkvc.shape[0])
    return _gdn_local_bwd(qkvc, gb, gbt, do, saved, scan)


def _gdn_scan_bwd(do, saved, t):
    nc = t // CHUNK
    hd = GDN_HEADS * GDN_DIM

    def body(*refs):
        ins, outs, dstate = refs[:16], refs[16:26], refs[26]
        @pl.when(pl.program_id(0) == 0)
        def _():
            dstate[...] = jnp.zeros_like(dstate)

        chains = [(d, h) for d in range(2) for h in range(GDN_HEADS)]
        pick = lambda k, d, h: ins[2 * k + d][0, h]
        dss = [dstate[ch] for ch in range(N_CHAINS)]
        dsbs = [ds.astype(BF16) for ds in dss]
        ss = [pick(1, d, h) for d, h in chains]
        sbs = [s.astype(BF16) for s in ss]
        dos = [ins[d][:, h * GDN_DIM:(h + 1) * GDN_DIM].astype(BF16) for d, h in chains]
        dv1 = [_dot_tn(pick(5, d, h), dov) for (d, h), dov in zip(chains, dos)]
        dv2 = [_dot(pick(4, d, h), dsb) for (d, h), dsb in zip(chains, dsbs)]
        ds1 = [_dot_tn(pick(3, d, h), dov) for (d, h), dov in zip(chains, dos)]
        dkds = [_dot_nt(pick(6, d, h), dsb) for (d, h), dsb in zip(chains, dsbs)]
        dqgs = [_dot_nt(dov, sb) for dov, sb in zip(dos, sbs)]
        dvns = [(a + b).astype(BF16) for a, b in zip(dv1, dv2)]
        ds2 = [_dot_tn(pick(2, d, h), dvn) for (d, h), dvn in zip(chains, dvns)]
        dws = [_dot_nt(dvn, sb) for dvn, sb in zip(dvns, sbs)]
        for ch, (d, h) in enumerate(chains):
            egl = ins[14 + d][0, ch:ch + 1, :]
            outs[d][0, h] = dvns[ch]
            outs[2 + d][0, h] = (-dws[ch]).astype(BF16)
            outs[4 + d][0, h] = dqgs[ch]
            outs[6 + d][0, h] = dkds[ch]
            outs[8 + d][0, h:h + 1, :] = egl * jnp.sum(jnp.sum(ss[ch] * dss[ch], axis=1, keepdims=True),
                                                       axis=0, keepdims=True)
            dstate[ch] = ds1[ch] + egl * dss[ch] - ds2[ch]

    spec, rows_spec = _dir_specs(nc, True)
    pair = lambda rows, cols, own=False: [spec(0, rows, cols, own), spec(1, rows, cols, own)]
    s_f, s_b = saved["s"]
    vn_f, vn_b = saved["vn"]
    w, qg, kd, intra, egl = saved["w"], saved["qg"], saved["kd"], saved["intra"], saved["egl"]
    own = lambda rows, cols, dtype: jax.ShapeDtypeStruct((nc, GDN_HEADS, rows, cols), dtype)
    row_shape = jax.ShapeDtypeStruct((nc, GDN_HEADS, LANES), F32)
    return pl.pallas_call(
        body, name="gdn_scan_bwd", grid=(nc,),
        in_specs=([rows_spec(0, hd), rows_spec(1, hd)] + pair(GDN_DIM, GDN_DIM, True) + pair(CHUNK, GDN_DIM)
                  + pair(CHUNK, GDN_DIM) + pair(CHUNK, GDN_DIM) + pair(CHUNK, CHUNK) + pair(CHUNK, GDN_DIM, True)
                  + pair(None, LANES)),
        out_specs=(pair(CHUNK, GDN_DIM, True) + pair(CHUNK, GDN_DIM, True) + pair(CHUNK, GDN_DIM, True)
                   + pair(CHUNK, GDN_DIM, True) + pair(None, LANES, True)),
        out_shape=[own(CHUNK, GDN_DIM, BF16)] * 4 + [own(CHUNK, GDN_DIM, F32)] * 4 + [row_shape] * 2,
        scratch_shapes=[pltpu.VMEM((N_CHAINS, GDN_DIM, GDN_DIM), F32)],
        compiler_params=_params(("arbitrary",), VMEM_LIMIT),
    )(do, do, s_f, s_b, w, w, qg, qg, kd, kd, intra, intra, vn_f, vn_b, egl, egl)


def _dot3_nt(a, b):
    ah = a.astype(BF16)
    al = (a - ah.astype(F32)).astype(BF16)
    bh = b.astype(BF16)
    bl = (b - bh.astype(F32)).astype(BF16)
    return _dot_nt(ah, bh) + (_dot_nt(ah, bl) + _dot_nt(al, bh))


def _dot3_tn(a, b):
    ah = a.astype(BF16)
    al = (a - ah.astype(F32)).astype(BF16)
    bh = b.astype(BF16)
    bl = (b - bh.astype(F32)).astype(BF16)
    return _dot_tn(ah, bh) + (_dot_tn(ah, bl) + _dot_tn(al, bh))


def _gdn_local_bwd(qkvc, gb, gbt, do, saved, scan):
    t = qkvc.shape[0]
    nc = t // CHUNK
    hd = GDN_HEADS * GDN_DIM

    def body(*refs):
        x_ref, g_ref, gt_ref, do_ref, t_ref = refs[:5]
        per_dir = refs[5:17]
        dx_ref, dg_ref = refs[17:]
        chains = _load_chains(x_ref, g_ref, gt_ref)
        lane = lax.broadcasted_iota(jnp.int32, (CHUNK, LANES), 1)
        dgates = jnp.zeros((CHUNK, LANES), F32)
        for c in chains:
            d = c["ch"] // GDN_HEADS
            vn_ref, dvn_ref, dw_ref, dqg_ref, dkd_ref, dgl_ref = per_dir[d::2]
            h = c["h"]
            c.update(tm=t_ref[0, c["ch"]], dov=do_ref[:, h * GDN_DIM:(h + 1) * GDN_DIM], vnew=vn_ref[0, h],
                     dvnew=dvn_ref[0, h], dw=dw_ref[0, h], dqg=dqg_ref[0, h], dkdec=dkd_ref[0, h],
                     dglast=dgl_ref[0, h:h + 1, 0:1])
        dintras = [_dot_nt(c["dov"], c["vnew"]) for c in chains]
        dts = [_dot_nt(c["dvnew"], c["vb"]) + _dot_nt(c["dw"], c["kbg"]) for c in chains]
        dvbs = [_dot_tn(c["tm"], c["dvnew"]) for c in chains]
        dkbgs = [_dot_tn(c["tm"], c["dw"]) for c in chains]
        tdts = [_dot3_nt(dt, c["tm"]) for dt, c in zip(dts, chains)]
        dls = [jnp.where(c["masks"][3], -_dot3_tn(c["tm"], tdt), 0.0) for tdt, c in zip(tdts, chains)]
        das = [dl * c["decay"] for dl, c in zip(dls, chains)]
        dqks = [jnp.where(c["masks"][2], di, 0.0) * c["decay"] for di, c in zip(dintras, chains)]
        dkb1 = [_dot(da, c["k"]) for da, c in zip(das, chains)]
        dk1 = [_dot_tn(da, c["kb"]) for da, c in zip(das, chains)]
        dk2 = [_dot_tn(dqk, c["q"]) for dqk, c in zip(dqks, chains)]
        dq1 = [_dot(dqk, c["k"]) for dqk, c in zip(dqks, chains)]
        grads = []
        for n, c in enumerate(chains):
            ch = c["ch"]
            ii, jj, incl, strict = c["masks"]
            k, v, bcol = c["k"], c["v"], c["bcol"]
            decay, eg, ek, kbg = c["decay"], c["eg"], c["ek"], c["kbg"]
            dqg, dkdec, dglast = c["dqg"], c["dkdec"], c["dglast"]
            dvb, dkbg, dl = dvbs[n], dkbgs[n], dls[n]
            dintra = jnp.where(incl, dintras[n], 0.0)
            mm = (dl * c["amat"] + dintra * c["qk"]) * decay
            dkb = dkb1[n] + dkbg * eg
            dk = dk1[n] + dk2[n] + dkdec * ek + dkb * bcol
            dq = dq1[n] + dqg * eg
            dv = dvb * bcol
            dbeta = jnp.sum(dkb * k, axis=1, keepdims=True) + jnp.sum(dvb * v, axis=1, keepdims=True)
            kd2 = jnp.sum(dkdec * c["kdec"], axis=1, keepdims=True)
            dgc = (jnp.sum(mm, axis=1, keepdims=True) - _row_to_col(jnp.sum(mm, axis=0, keepdims=True), ii, jj)
                   + jnp.sum(dqg * c["qg"], axis=1, keepdims=True) - kd2
                   + jnp.sum(dkbg * kbg, axis=1, keepdims=True))
            dgl = dglast + jnp.sum(kd2, axis=0, keepdims=True)
            draw = jnp.sum(jnp.where(jnp.logical_not(strict), _col_to_row(dgc, ii, jj), 0.0),
                           axis=1, keepdims=True) + dgl
            dgates = dgates + jnp.where(lane == ch, draw, 0.0) + jnp.where(lane == 8 + ch, dbeta, 0.0)
            grads.append((dq, dk, dv))
        for h in range(GDN_HEADS):
            for part in range(3):
                cols = slice(part * hd + h * GDN_DIM, part * hd + (h + 1) * GDN_DIM)
                dx_ref[:, cols] = grads[h][part] + grads[GDN_HEADS + h][part]
        dg_ref[...] = dgates

    all8 = lambda rows, cols: pl.BlockSpec((1, N_CHAINS, rows, cols), lambda n: (n, 0, 0, 0))
    own4 = lambda rows, cols: pl.BlockSpec((1, GDN_HEADS, rows, cols), lambda n: (n, 0, 0, 0))
    row4 = pl.BlockSpec((1, GDN_HEADS, LANES), lambda n: (n, 0, 0))
    vn_f, vn_b = saved["vn"]
    dvn_f, dvn_b, dw_f, dw_b, dqg_f, dqg_b, dkd_f, dkd_b, dgl_f, dgl_b = scan
    return pl.pallas_call(
        body, name="gdn_local_bwd", grid=(nc,),
        in_specs=[pl.BlockSpec((CHUNK, 3 * hd), lambda n: (n, 0)), pl.BlockSpec((CHUNK, LANES), lambda n: (n, 0)),
                  pl.BlockSpec((1, 16, CHUNK), lambda n: (n, 0, 0)), pl.BlockSpec((CHUNK, hd), lambda n: (n, 0)),
                  all8(CHUNK, CHUNK)] + [own4(CHUNK, GDN_DIM)] * 10 + [row4, row4],
        out_specs=[pl.BlockSpec((CHUNK, 3 * hd), lambda n: (n, 0)), pl.BlockSpec((CHUNK, LANES), lambda n: (n, 0))],
        out_shape=[jax.ShapeDtypeStruct((t, 3 * hd), F32), jax.ShapeDtypeStruct((t, LANES), F32)],
        compiler_params=_params(("arbitrary",), VMEM_LIMIT),
    )(qkvc, gb, gbt, do, saved["tm"], vn_f, vn_b, dvn_f, dvn_b, dw_f, dw_b, dqg_f, dqg_b, dkd_f, dkd_b, dgl_f, dgl_b)


def _gdn_post_fwd(of, ob, z, gw, tm):
    t, hd = of.shape

    def body(of_ref, ob_ref, z_ref, w_ref, o_ref):
        for h in range(GDN_HEADS):
            cols = slice(h * GDN_DIM, (h + 1) * GDN_DIM)
            o = of_ref[:, cols] + ob_ref[:, cols]
            zv = z_ref[:, cols]
            o_ref[:, cols] = (o * _rstd(o) * w_ref[...] * (zv * _sigmoid(zv))).astype(BF16)

    row = pl.BlockSpec((tm, hd), lambda i: (i, 0))
    return pl.pallas_call(
        body, name="gdn_post_fwd", grid=(t // tm,),
        in_specs=[row, row, row, _resident((1, GDN_DIM))],
        out_specs=row, out_shape=jax.ShapeDtypeStruct((t, hd), BF16),
        compiler_params=_params(("arbitrary",), VMEM_LIMIT),
    )(of, ob, z, gw)


def _gdn_post_bwd(doa, of, ob, z, gw, tm):
    t, hd = of.shape

    def body(d_ref, of_ref, ob_ref, z_ref, w_ref, do_ref, dz_ref, dw_ref):
        @pl.when(pl.program_id(0) == 0)
        def _():
            dw_ref[...] = jnp.zeros_like(dw_ref)

        dw = jnp.zeros((1, GDN_DIM), F32)
        for h in range(GDN_HEADS):
            cols = slice(h * GDN_DIM, (h + 1) * GDN_DIM)
            o = of_ref[:, cols] + ob_ref[:, cols]
            zv = z_ref[:, cols]
            dv = d_ref[:, cols]
            r = _rstd(o)
            sg = _sigmoid(zv)
            on = o * r * w_ref[...]
            dz_ref[:, cols] = (dv * on * (sg * (1.0 + zv * (1.0 - sg)))).astype(BF16)
            dxr, dwh = _rms_bwd(o, r, w_ref[...], dv * (zv * sg))
            do_ref[:, cols] = dxr
            dw = dw + dwh
        dw_ref[...] += dw

    row = pl.BlockSpec((tm, hd), lambda i: (i, 0))
    return pl.pallas_call(
        body, name="gdn_post_bwd", grid=(t // tm,),
        in_specs=[row, row, row, row, _resident((1, GDN_DIM))],
        out_specs=[row, row, pl.BlockSpec((1, GDN_DIM), lambda i: (0, 0))],
        out_shape=[jax.ShapeDtypeStruct((t, hd), F32), jax.ShapeDtypeStruct((t, hd), BF16),
                   jax.ShapeDtypeStruct((1, GDN_DIM), F32)],
        compiler_params=_params(("arbitrary",), VMEM_LIMIT),
    )(doa, of, ob, z, gw)


SWA_W = SWA_HEADS * SWA_DIM
QBLK = 128
KWIN = QBLK + 2 * RADIUS
WIN_OFFSETS = (0, RADIUS, 2 * RADIUS)


def _t5_bucket(rel):
    nb = REL_BUCKETS // 2
    bucket = (rel > 0).astype(np.int32) * nb
    n = np.abs(rel)
    max_exact = nb // 2
    large = max_exact + (np.log(np.maximum(n, 1) / max_exact)
                         / math.log(REL_MAX_DISTANCE / max_exact) * (nb - max_exact)).astype(np.int32)
    large = np.minimum(large, nb - 1)
    return (bucket + np.where(n < max_exact, n, large)).astype(np.int32)


def _band_tables(dilation):
    a = np.arange(QBLK)
    b = np.arange(KWIN)
    rel_a = np.stack([b[None, :] - w0 - a[:, None] for w0 in WIN_OFFSETS])
    rel_b = np.stack([a[None, :] + w0 - b[:, None] for w0 in WIN_OFFSETS])
    return ((_t5_bucket(rel_a * dilation), np.abs(rel_a) <= RADIUS),
            (_t5_bucket(rel_b * dilation), np.abs(rel_b) <= RADIUS))


def _bias_table(rel_bias, idx, valid):
    onehot = (jnp.arange(REL_BUCKETS, dtype=jnp.int32)[:, None] == jnp.asarray(idx.reshape(1, -1))).astype(F32)
    tab = jnp.dot(rel_bias.T, onehot, precision=HIGHEST)
    tab = jnp.where(jnp.asarray(valid.reshape(1, -1)), tab, NEG_BIG)
    tab = tab.reshape((SWA_HEADS,) + idx.shape)
    return jnp.transpose(tab, (1, 0, 2, 3)), onehot


def _head_mean(x2, bd_ref):
    return _dot_hi(x2, bd_ref[...])


def _swa_prep_fwd(qkvb, qw, kw, bd, tm):
    t = qkvb.shape[0]

    def body(x_ref, qw_ref, kw_ref, bd_ref, q_ref, k_ref, v_ref):
        for gidx in range(SWA_W // LANES):
            cols = slice(gidx * LANES, (gidx + 1) * LANES)
            xq = x_ref[:, cols]
            q_ref[:, cols] = (xq * lax.rsqrt(_head_mean(xq * xq, bd_ref) + EPS) * qw_ref[:, cols]
                              * (SWA_DIM ** -0.5)).astype(BF16)
            xk = x_ref[:, SWA_W + gidx * LANES:SWA_W + (gidx + 1) * LANES]
            k_ref[:, cols] = (xk * lax.rsqrt(_head_mean(xk * xk, bd_ref) + EPS) * kw_ref[:, cols]).astype(BF16)
        v_ref[...] = x_ref[:, 2 * SWA_W:3 * SWA_W].astype(BF16)

    hrow = pl.BlockSpec((tm, SWA_W), lambda i: (i, 0))
    shp = jax.ShapeDtypeStruct((t, SWA_W), BF16)
    return pl.pallas_call(
        body, name="swa_prep_fwd", grid=(t // tm,),
        in_specs=[pl.BlockSpec((tm, 3 * SWA_W), lambda i: (i, 0)), _resident((1, SWA_W)), _resident((1, SWA_W)),
                  _resident((LANES, LANES))],
        out_specs=[hrow, hrow, hrow], out_shape=[shp, shp, shp],
        compiler_params=_params(("arbitrary",), VMEM_LIMIT),
    )(qkvb, qw, kw, bd)


def _swa_prep_bwd(qkvb, qw, kw, bd, dqs, dks, dvs, tm):
    t = qkvb.shape[0]

    def body(x_ref, qw_ref, kw_ref, bd_ref, q0, q1, q2, k0, k1, k2, v0, v1, v2, dx_ref, dqw_ref, dkw_ref):
        @pl.when(pl.program_id(0) == 0)
        def _():
            dqw_ref[...] = jnp.zeros_like(dqw_ref)
            dkw_ref[...] = jnp.zeros_like(dkw_ref)

        for gidx in range(SWA_W // LANES):
            cols = slice(gidx * LANES, (gidx + 1) * LANES)
            for base, w_ref, parts, dw_ref, scale in ((0, qw_ref, (q0, q1, q2), dqw_ref, SWA_DIM ** -0.5),
                                                      (SWA_W, kw_ref, (k0, k1, k2), dkw_ref, 1.0)):
                xv = x_ref[:, base + gidx * LANES:base + (gidx + 1) * LANES]
                dy = (parts[0][:, cols] + parts[1][:, cols] + parts[2][:, cols]) * scale
                r = lax.rsqrt(_head_mean(xv * xv, bd_ref) + EPS)
                xhat = xv * r
                dxh = dy * w_ref[:, cols]
                dx = r * (dxh - xhat * _head_mean(dxh * xhat, bd_ref))
                dx_ref[:, base + gidx * LANES:base + (gidx + 1) * LANES] = dx.astype(BF16)
                dw_ref[:, cols] += jnp.sum(dy * xhat, axis=0, keepdims=True)
        dx_ref[:, 2 * SWA_W:3 * SWA_W] = (v0[...] + v1[...] + v2[...]).astype(BF16)

    hrow = pl.BlockSpec((tm, SWA_W), lambda i: (i, 0))
    wrow = pl.BlockSpec((1, SWA_W), lambda i: (0, 0))
    return pl.pallas_call(
        body, name="swa_prep_bwd", grid=(t // tm,),
        in_specs=[pl.BlockSpec((tm, 3 * SWA_W), lambda i: (i, 0)), _resident((1, SWA_W)), _resident((1, SWA_W)),
                  _resident((LANES, LANES))] + [hrow] * 9,
        out_specs=[pl.BlockSpec((tm, 3 * SWA_W), lambda i: (i, 0)), wrow, wrow],
        out_shape=[jax.ShapeDtypeStruct((t, 3 * SWA_W), BF16), jax.ShapeDtypeStruct((1, SWA_W), F32),
                   jax.ShapeDtypeStruct((1, SWA_W), F32)],
        compiler_params=_params(("arbitrary",), VMEM_LIMIT),
    )(qkvb, qw, kw, bd, *dqs, *dks, *dvs)


def _aligned(v, m):
    return v if isinstance(v, int) else pl.multiple_of(v, m)


BAND_GROUP = 2


def _band_loop(nsub, length, step):
    step([(0, 0)], 0)
    if nsub > 2:
        assert (nsub - 2) % BAND_GROUP == 0

        def inner(i, carry):
            s0 = 1 + i * BAND_GROUP
            step([(s0 + e, pl.multiple_of((s0 + e) * QBLK - RADIUS, RADIUS)) for e in range(BAND_GROUP)], 1)
            return carry
        lax.fori_loop(0, (nsub - 2) // BAND_GROUP, inner, 0)
    step([(nsub - 1, length - KWIN)], 2)


def _head_select(lane, a0, a1):
    return jnp.where(lane < SWA_DIM, a0, a1)


def _swa_fwd(qv, kv, vv, bias, dilation, name):
    length = qv.shape[0]
    nsub = length // QBLK
    assert nsub >= 2 and length % QBLK == 0

    def body(q_ref, k_ref, v_ref, b_ref, o_ref, l_ref):
        lane = lax.broadcasted_iota(jnp.int32, (QBLK, LANES), 1)

        def step(blocks, var):
            items = []
            for s, ws in blocks:
                rows = pl.ds(_aligned(s * QBLK, QBLK), QBLK)
                q, kk, vw = q_ref[rows, :], k_ref[pl.ds(ws, KWIN), :], v_ref[pl.ds(ws, KWIN), :]
                for hh in range(2):
                    items.append((hh, jnp.where((lane < SWA_DIM) == (hh == 0), q, jnp.zeros_like(q)), kk, vw))
            lgs = [_dot_nt(qh, kk) + b_ref[var, hh] for hh, qh, kk, _ in items]
            ms = [jnp.max(lg, axis=-1, keepdims=True) for lg in lgs]
            ps = [jnp.exp(lg - m) for lg, m in zip(lgs, ms)]
            dens = [jnp.sum(p, axis=-1, keepdims=True) for p in ps]
            pvs = [_dot(p, it[3]) for p, it in zip(ps, items)]
            for n, (s, _) in enumerate(blocks):
                rows = pl.ds(_aligned(s * QBLK, QBLK), QBLK)
                o0, o1 = (pvs[2 * n + hh] / dens[2 * n + hh] for hh in range(2))
                l0, l1 = (ms[2 * n + hh] + jnp.log(dens[2 * n + hh]) for hh in range(2))
                o_ref[rows, :] = _head_select(lane, o0, o1)
                l_ref[rows, :] = _head_select(lane, l0, l1)

        _band_loop(nsub, length, step)

    blk = pl.BlockSpec((length, LANES), lambda hp, r: (0, r * (SWA_W // LANES) + hp))
    shp = jax.ShapeDtypeStruct(qv.shape, F32)
    return pl.pallas_call(
        body, name=name, grid=(SWA_W // LANES, dilation),
        in_specs=[blk, blk, blk, pl.BlockSpec((3, 2, QBLK, KWIN), lambda hp, r: (0, hp, 0, 0))],
        out_specs=[blk, blk], out_shape=[shp, shp],
        compiler_params=_params(("arbitrary", "arbitrary"), VMEM_LIMIT),
    )(qv, kv, vv, bias)


def _swa_combine(os_, ls_, tm):
    t = os_[0].shape[0]

    def body(o0, o1, o2, l0, l1, l2, o_ref, ob_ref, l_ref):
        la, lb, lc = l0[...], l1[...], l2[...]
        m = jnp.maximum(jnp.maximum(la, lb), lc)
        tot = m + jnp.log(jnp.exp(la - m) + jnp.exp(lb - m) + jnp.exp(lc - m))
        o = jnp.exp(la - tot) * o0[...] + jnp.exp(lb - tot) * o1[...] + jnp.exp(lc - tot) * o2[...]
        o_ref[...] = o
        ob_ref[...] = o.astype(BF16)
        l_ref[...] = tot

    hrow = pl.BlockSpec((tm, SWA_W), lambda i: (i, 0))
    return pl.pallas_call(
        body, name="swa_combine", grid=(t // tm,), in_specs=[hrow] * 6, out_specs=[hrow] * 3,
        out_shape=[jax.ShapeDtypeStruct((t, SWA_W), F32), jax.ShapeDtypeStruct((t, SWA_W), BF16),
                   jax.ShapeDtypeStruct((t, SWA_W), F32)],
        compiler_params=_params(("arbitrary",), VMEM_LIMIT),
    )(*os_, *ls_)


def _swa_bwd_prep(do, o, bd, tm):
    t = do.shape[0]

    def body(d_ref, o_ref, bd_ref, dd_ref, db_ref):
        for gidx in range(SWA_W // LANES):
            cols = slice(gidx * LANES, (gidx + 1) * LANES)
            dd_ref[:, cols] = _head_mean(d_ref[:, cols] * o_ref[:, cols], bd_ref) * float(SWA_DIM)
        db_ref[...] = d_ref[...].astype(BF16)

    hrow = pl.BlockSpec((tm, SWA_W), lambda i: (i, 0))
    return pl.pallas_call(
        body, name="swa_bwd_prep", grid=(t // tm,), in_specs=[hrow, hrow, _resident((LANES, LANES))],
        out_specs=[hrow, hrow],
        out_shape=[jax.ShapeDtypeStruct((t, SWA_W), F32), jax.ShapeDtypeStruct((t, SWA_W), BF16)],
        compiler_params=_params(("arbitrary",), VMEM_LIMIT),
    )(do, o, bd)


def _swa_bwd(qv, kv, vv, dov, lv, ddv, bias_a, bias_b, dilation, name):
    length = qv.shape[0]
    nsub = length // QBLK
    single = pl.Buffered(1) if dilation == 1 else None

    def body(q_ref, k_ref, v_ref, do_ref, l_ref, dd_ref, ba_ref, bb_ref, dq_ref, dk_ref, dv_ref, db_ref):
        @pl.when(pl.program_id(1) == 0)
        def _():
            db_ref[...] = jnp.zeros_like(db_ref)

        lane = lax.broadcasted_iota(jnp.int32, (QBLK, LANES), 1)
        lanew = lax.broadcasted_iota(jnp.int32, (KWIN, LANES), 1)

        def step_q(blocks, var):
            items = []
            for s, ws in blocks:
                rows = pl.ds(_aligned(s * QBLK, QBLK), QBLK)
                win = pl.ds(ws, KWIN)
                q, dov_ = q_ref[rows, :], do_ref[rows, :]
                kk, vw = k_ref[win, :], v_ref[win, :]
                lse, dd = l_ref[rows, :], dd_ref[rows, :]
                for hh in range(2):
                    mine = (lane < SWA_DIM) == (hh == 0)
                    col = slice(hh * SWA_DIM, hh * SWA_DIM + 1)
                    items.append((hh, jnp.where(mine, q, jnp.zeros_like(q)), jnp.where(mine, dov_, jnp.zeros_like(dov_)),
                                  kk, vw, lse[:, col], dd[:, col]))
            lgs = [_dot_nt(qh, kk) + ba_ref[var, hh] for hh, qh, _, kk, _, _, _ in items]
            dps = [_dot_nt(doh, vw) for _, _, doh, _, vw, _, _ in items]
            dss = [jnp.exp(lg - it[5]) * (dp - it[6]) for lg, dp, it in zip(lgs, dps, items)]
            dqs = [_dot(ds, it[3]) for ds, it in zip(dss, items)]
            for n, (s, _) in enumerate(blocks):
                rows = pl.ds(_aligned(s * QBLK, QBLK), QBLK)
                dq_ref[rows, :] = _head_select(lane, dqs[2 * n], dqs[2 * n + 1])
            for hh in range(2):
                tot = dss[hh]
                for n in range(1, len(blocks)):
                    tot = tot + dss[2 * n + hh]
                db_ref[var, hh] += tot

        def step_k(blocks, var):
            items = []
            for s, ws in blocks:
                rows = pl.ds(_aligned(s * QBLK, QBLK), QBLK)
                win = pl.ds(ws, KWIN)
                kk, vw = k_ref[rows, :], v_ref[rows, :]
                qw, dow = q_ref[win, :], do_ref[win, :]
                lse, dd = l_ref[win, :], dd_ref[win, :]
                for hh in range(2):
                    mine = (lanew < SWA_DIM) == (hh == 0)
                    col = slice(hh * SWA_DIM, hh * SWA_DIM + 1)
                    items.append((hh, jnp.where(mine, qw, jnp.zeros_like(qw)), jnp.where(mine, dow, jnp.zeros_like(dow)),
                                  kk, vw, lse[:, col], dd[:, col], qw, dow))
            lgs = [_dot_nt(it[1], it[3]) + bb_ref[var, it[0]] for it in items]
            dps = [_dot_nt(it[2], it[4]) for it in items]
            ps = [jnp.exp(lg - it[5]) for lg, it in zip(lgs, items)]
            dss = [p * (dp - it[6]) for p, dp, it in zip(ps, dps, items)]
            dks = [_dot_tn(ds, it[7]) for ds, it in zip(dss, items)]
            dvs = [_dot_tn(p, it[8]) for p, it in zip(ps, items)]
            for n, (s, _) in enumerate(blocks):
                rows = pl.ds(_aligned(s * QBLK, QBLK), QBLK)
                dk_ref[rows, :] = _head_select(lane, dks[2 * n], dks[2 * n + 1])
                dv_ref[rows, :] = _head_select(lane, dvs[2 * n], dvs[2 * n + 1])

        _band_loop(nsub, length, step_q)
        _band_loop(nsub, length, step_k)

    imap = lambda hp, r: (0, r * (SWA_W // LANES) + hp)
    blk_in = pl.BlockSpec((length, LANES), imap, pipeline_mode=single)
    blk_out = pl.BlockSpec((length, LANES), imap)
    shp = jax.ShapeDtypeStruct(qv.shape, F32)
    return pl.pallas_call(
        body, name=name, grid=(SWA_W // LANES, dilation),
        in_specs=[blk_in] * 6 + [pl.BlockSpec((3, 2, QBLK, KWIN), lambda hp, r: (0, hp, 0, 0)),
                                 pl.BlockSpec((3, 2, KWIN, QBLK), lambda hp, r: (0, hp, 0, 0))],
        out_specs=[blk_out, blk_out, blk_out, pl.BlockSpec((3, 2, QBLK, KWIN), lambda hp, r: (0, hp, 0, 0))],
        out_shape=[shp, shp, shp, jax.ShapeDtypeStruct((3, SWA_HEADS, QBLK, KWIN), F32)],
        compiler_params=_params(("arbitrary", "arbitrary"), VMEM_LIMIT),
    )(qv, kv, vv, dov, lv, ddv, bias_a, bias_b)


def _bias_grad(ds2, onehot, tk):
    n = ds2.shape[1]
    nk = n // tk

    def body(a_ref, b_ref, o_ref):
        @pl.when(pl.program_id(0) == 0)
        def _():
            o_ref[...] = jnp.zeros_like(o_ref)

        o_ref[...] += lax.dot_general(a_ref[...], b_ref[...], (((1,), (1,)), ((), ())), precision=HIGHEST,
                                      preferred_element_type=F32)

    return pl.pallas_call(
        body, name="bias_grad", grid=(nk,),
        in_specs=[pl.BlockSpec((SWA_HEADS, tk), lambda k: (0, k)), pl.BlockSpec((REL_BUCKETS, tk), lambda k: (0, k))],
        out_specs=pl.BlockSpec((SWA_HEADS, REL_BUCKETS), lambda k: (0, 0)),
        out_shape=jax.ShapeDtypeStruct((SWA_HEADS, REL_BUCKETS), F32),
        compiler_params=_params(("arbitrary",), VMEM_LIMIT),
    )(ds2, onehot)


def _swa_branch_fwd(qkvb, qw_t, kw_t, rel_bias, bd, tm):
    t = qkvb.shape[0]
    q, k, v = _swa_prep_fwd(qkvb, qw_t, kw_t, bd, tm)
    os_, ls_, tabs = [], [], []
    for _, d in PATTERNS:
        (idx_a, val_a), (idx_b, val_b) = _band_tables(d)
        bias_a, onehot_a = _bias_table(rel_bias, idx_a, val_a)
        bias_b, _ = _bias_table(rel_bias, idx_b, val_b)
        view = lambda a: a.reshape(t // d, d * SWA_W)
        o_p, l_p = _swa_fwd(view(q), view(k), view(v), bias_a, d, f"swa_fwd_d{d}")
        os_.append(o_p.reshape(t, SWA_W))
        ls_.append(l_p.reshape(t, SWA_W))
        tabs.append((bias_a, bias_b, onehot_a))
    o, o16, lse = _swa_combine(os_, ls_, tm)
    return o, o16, (q, k, v, lse, tabs)


def _swa_branch_bwd(do, o, saved, qkvb, qw_t, kw_t, bd, tm):
    t = qkvb.shape[0]
    q, k, v, lse, tabs = saved
    dd, do16 = _swa_bwd_prep(do, o, bd, tm)
    dqs, dks, dvs, dss, ohs = [], [], [], [], []
    for (_, d), (bias_a, bias_b, onehot_a) in zip(PATTERNS, tabs):
        view = lambda a: a.reshape(t // d, d * SWA_W)
        dq, dk, dv, ds = _swa_bwd(view(q), view(k), view(v), view(do16), view(lse), view(dd),
                                  bias_a, bias_b, d, f"swa_bwd_d{d}")
        dqs.append(dq.reshape(t, SWA_W))
        dks.append(dk.reshape(t, SWA_W))
        dvs.append(dv.reshape(t, SWA_W))
        dss.append(jnp.transpose(ds, (1, 0, 2, 3)).reshape(SWA_HEADS, -1))
        ohs.append(onehot_a)
    dqkvb, dqw, dkw = _swa_prep_bwd(qkvb, qw_t, kw_t, bd, dqs, dks, dvs, tm)
    dbias = _bias_grad(jnp.concatenate(dss, axis=1), jnp.concatenate(ohs, axis=1), 8192)
    fold = lambda w: jnp.sum(w.reshape(SWA_HEADS, SWA_DIM), axis=0)
    return dqkvb, fold(dqw), fold(dkw), dbias.T


ANY = pl.BlockSpec(memory_space=pl.ANY)


def _mesh_pos():
    return lax.axis_index("x"), lax.axis_index("y"), lax.axis_index("c")


def _other_chips(x, y):
    return [(1 - x, y), (x, 1 - y), (1 - x, 1 - y)]


def _remote(src, dst, send_sem, recv_sem, device):
    return pltpu.make_async_remote_copy(src_ref=src, dst_ref=dst, send_sem=send_sem, recv_sem=recv_sem,
                                        device_id=device, device_id_type=MESH)


def _all_gather(xs):
    n = len(xs)

    def body(*refs):
        ins, outs = refs[:n], refs[n:2 * n]
        send_sems, recv_sems = refs[2 * n:]
        x, y, c = _mesh_pos()
        me = 2 * x + y
        chips = _other_chips(x, y)
        halves = []
        sends = []
        for a in range(n):
            h = ins[a].shape[0] // 2
            mine, other = pl.ds(c * h, h), pl.ds((1 - c) * h, h)
            halves.append((mine, other))
            for j, chip in enumerate(chips):
                cp = _remote(ins[a].at[mine], outs[a].at[me, mine], send_sems.at[a, j], recv_sems.at[a, j], (*chip, c))
                cp.start()
                sends.append(cp)
        for a in range(n):
            mine, _ = halves[a]
            for j, chip in enumerate(chips):
                src = 2 * chip[0] + chip[1]
                landed = outs[a].at[src, mine]
                _remote(landed, landed, send_sems.at[a, j], recv_sems.at[a, j], (x, y, c)).wait_recv()
                fwd = _remote(landed, landed, send_sems.at[a, 3 + j], recv_sems.at[a, 3 + j], (x, y, 1 - c))
                fwd.start()
                sends.append(fwd)
        for a in range(n):
            _, other = halves[a]
            for j, chip in enumerate(chips):
                src = 2 * chip[0] + chip[1]
                landed = outs[a].at[src, other]
                _remote(landed, landed, send_sems.at[a, 3 + j], recv_sems.at[a, 3 + j], (x, y, c)).wait_recv()
        for cp in sends:
            cp.wait_send()

    outs = pl.pallas_call(
        body, name="all_gather_weights",
        in_specs=[ANY] * n, out_specs=[ANY] * n,
        out_shape=[jax.ShapeDtypeStruct((N_SHARDS,) + a.shape, a.dtype) for a in xs],
        scratch_shapes=[pltpu.SemaphoreType.DMA((n, 6)), pltpu.SemaphoreType.DMA((n, 6))],
    )(*xs)
    me = 2 * lax.axis_index("x") + lax.axis_index("y")
    return [lax.dynamic_update_slice_in_dim(o, a[None], me, 0) for o, a in zip(outs, xs)]


def _rs_pair(gs):
    n = len(gs)

    def body(*refs):
        ins, lands = refs[:n], refs[n:2 * n]
        send_sems, recv_sems = refs[2 * n:]
        x, y, c = _mesh_pos()
        cps = []
        for a in range(n):
            h = ins[a].shape[1] // 2
            cp = _remote(ins[a].at[:, pl.ds((1 - c) * h, h), :], lands[a], send_sems.at[a], recv_sems.at[a],
                         (x, y, 1 - c))
            cp.start()
            cps.append(cp)
        for cp in cps:
            cp.wait()

    half = [jax.ShapeDtypeStruct((N_SHARDS, g.shape[1] // 2, g.shape[2]), g.dtype) for g in gs]
    lands = pl.pallas_call(
        body, name="rs_pair", in_specs=[ANY] * n, out_specs=[ANY] * n, out_shape=half,
        scratch_shapes=[pltpu.SemaphoreType.DMA((n,)), pltpu.SemaphoreType.DMA((n,))],
    )(*gs)
    c = lax.axis_index("c")
    owns = [lax.dynamic_slice_in_dim(g, c * (g.shape[1] // 2), g.shape[1] // 2, 1) for g in gs]
    return owns + list(lands)


def _rs_chips(ss):
    n = len(ss)

    def body(*refs):
        ins, outs = refs[:n], refs[n:2 * n]
        send_sems, recv_sems = refs[2 * n:]
        x, y, c = _mesh_pos()
        me = 2 * x + y
        chips = _other_chips(x, y)
        cps = []
        for a in range(n):
            for j, chip in enumerate(chips):
                dst_chip = 2 * chip[0] + chip[1]
                cp = _remote(ins[a].at[dst_chip], outs[a].at[me], send_sems.at[a, j], recv_sems.at[a, j], (*chip, c))
                cp.start()
                cps.append(cp)
        for a in range(n):
            for j, chip in enumerate(chips):
                src = 2 * chip[0] + chip[1]
                _remote(outs[a].at[src], outs[a].at[src], send_sems.at[a, j], recv_sems.at[a, j], (x, y, c)).wait_recv()
        for cp in cps:
            cp.wait_send()

    outs = pl.pallas_call(
        body, name="rs_chips", in_specs=[ANY] * n, out_specs=[ANY] * n,
        out_shape=[jax.ShapeDtypeStruct(s.shape, s.dtype) for s in ss],
        scratch_shapes=[pltpu.SemaphoreType.DMA((n, 3)), pltpu.SemaphoreType.DMA((n, 3))],
    )(*ss)
    me = 2 * lax.axis_index("x") + lax.axis_index("y")
    return [lax.dynamic_update_slice_in_dim(o, lax.dynamic_slice_in_dim(s, me, 1, 0), me, 0) for o, s in zip(outs, ss)]


def _rs_join(fs):
    n = len(fs)

    def body(*refs):
        ins, outs = refs[:n], refs[n:2 * n]
        send_sems, recv_sems = refs[2 * n:]
        x, y, c = _mesh_pos()
        cps = []
        for a in range(n):
            h = ins[a].shape[0]
            cp = _remote(ins[a], outs[a].at[pl.ds(c * h, h)], send_sems.at[a], recv_sems.at[a], (x, y, 1 - c))
            cp.start()
            cps.append(cp)
        for cp in cps:
            cp.wait()

    outs = pl.pallas_call(
        body, name="rs_join", in_specs=[ANY] * n, out_specs=[ANY] * n,
        out_shape=[jax.ShapeDtypeStruct((2 * f.shape[0], f.shape[1]), f.dtype) for f in fs],
        scratch_shapes=[pltpu.SemaphoreType.DMA((n,)), pltpu.SemaphoreType.DMA((n,))],
    )(*fs)
    c = lax.axis_index("c")
    return [lax.dynamic_update_slice_in_dim(o, f, c * f.shape[0], 0) for o, f in zip(outs, fs)]


def _add_pair(a, b, name):
    nj, h, c = a.shape

    def body(a_ref, b_ref, o_ref):
        o_ref[...] = (a_ref[...].astype(F32) + b_ref[...].astype(F32)).astype(BF16)

    blk = pl.BlockSpec((1, h, c), lambda j: (j, 0, 0))
    return pl.pallas_call(body, name=name, grid=(nj,), in_specs=[blk, blk], out_specs=blk,
                          out_shape=jax.ShapeDtypeStruct(a.shape, BF16),
                          compiler_params=_params(("arbitrary",), VMEM_LIMIT))(a, b)


def _sum_slots(l2, name):
    nj, h, c = l2.shape
    th = h // 2 if h % 32 == 0 else h

    def body(i_ref, o_ref):
        acc = i_ref[0].astype(F32)
        for s in range(1, nj):
            acc = acc + i_ref[s].astype(F32)
        o_ref[...] = acc

    return pl.pallas_call(body, name=name, grid=(h // th,),
                          in_specs=[pl.BlockSpec((nj, th, c), lambda i: (0, i, 0))],
                          out_specs=pl.BlockSpec((th, c), lambda i: (i, 0)),
                          out_shape=jax.ShapeDtypeStruct((h, c), F32),
                          compiler_params=_params(("arbitrary",), VMEM_LIMIT))(l2)


def _all_reduce_small(p):
    r = p.shape[0]

    def body(p_ref, o_ref, buf, send_sems, recv_sems):
        x, y, c = _mesh_pos()
        me = 4 * x + 2 * y + c
        buf[me] = p_ref[...]
        cps = []
        k = 0
        for fx in range(2):
            for fy in range(2):
                for fc in range(2):
                    if fx + fy + fc == 0:
                        continue
                    peer = (1 - x if fx else x, 1 - y if fy else y, 1 - c if fc else c)
                    peer_id = 4 * peer[0] + 2 * peer[1] + peer[2]
                    cp = _remote(p_ref, buf.at[me], send_sems.at[k], recv_sems.at[k], peer)
                    cp.start()
                    cps.append((cp, peer_id, k))
                    k += 1
        for cp, peer_id, k in cps:
            _remote(p_ref, buf.at[peer_id], send_sems.at[k], recv_sems.at[k], (x, y, c)).wait_recv()
        for cp, _, _ in cps:
            cp.wait_send()
        acc = buf[0]
        for s in range(1, 8):
            acc = acc + buf[s]
        o_ref[...] = acc

    vm = pl.BlockSpec(memory_space=pltpu.VMEM)
    return pl.pallas_call(
        body, name="all_reduce_small", in_specs=[vm], out_specs=vm,
        out_shape=jax.ShapeDtypeStruct(p.shape, F32),
        scratch_shapes=[pltpu.VMEM((8, r, LANES), F32), pltpu.SemaphoreType.DMA((7,)), pltpu.SemaphoreType.DMA((7,))],
    )(p)


def _adamw(w, g, m, v, name):
    r, c = w.shape
    tr = max(d for d in range(8, min(r, 256) + 1, 8) if r % d == 0)
    c1 = 1.0 / (1.0 - ADAM_B1 ** ADAM_STEP)
    c2 = 1.0 / (1.0 - ADAM_B2 ** ADAM_STEP)

    def body(w_ref, g_ref, m_ref, v_ref, d_ref, nm_ref, nv_ref):
        gv = g_ref[...]
        nm = ADAM_B1 * m_ref[...] + (1.0 - ADAM_B1) * gv
        nv = ADAM_B2 * v_ref[...] + (1.0 - ADAM_B2) * (gv * gv)
        d_ref[...] = -ADAM_LR * ((nm * c1) / (jnp.sqrt(nv * c2) + ADAM_EPS) + ADAM_WD * w_ref[...])
        nm_ref[...] = nm
        nv_ref[...] = nv

    blk = pl.BlockSpec((tr, c), lambda i: (i, 0))
    shp = jax.ShapeDtypeStruct((r, c), F32)
    return pl.pallas_call(body, name=name, grid=(r // tr,), in_specs=[blk] * 4, out_specs=[blk] * 3,
                          out_shape=[shp, shp, shp], compiler_params=_params(("arbitrary",), VMEM_LIMIT))(w, g, m, v)


PACK_UNIT = 8 * LANES


def _pack(arrs):
    parts = []
    for a in arrs:
        f = a.reshape(-1).astype(F32)
        parts.append(jnp.pad(f, (0, (-f.shape[0]) % PACK_UNIT)).reshape(-1, LANES))
    return jnp.concatenate(parts, axis=0)


def _unpack(m, shapes):
    outs, row = [], 0
    for s in shapes:
        n = int(np.prod(s))
        rows = -(-n // PACK_UNIT) * 8
        outs.append(m[row:row + rows].reshape(-1)[:n].reshape(s))
        row += rows
    return outs


WEIGHTS = ["ffn1_norm", "ffn1_w_gate", "ffn1_w_up", "ffn1_w_down", "mix_norm", "w_in", "conv_w", "a_log", "dt_bias",
           "gdn_norm_w", "q_norm_w", "k_norm_w", "rel_bias", "w_out", "ffn2_norm", "ffn2_w_gate", "ffn2_w_up",
           "ffn2_w_down", "final_norm"]
BIG = ["ffn1_w_gate", "ffn1_w_up", "ffn1_w_down", "w_in", "w_out", "ffn2_w_gate", "ffn2_w_up", "ffn2_w_down"]
SMALL = [n for n in WEIGHTS if n not in BIG]
N_IN_COLS = 3600
TM = 256
TE = 512


def kernel(x, ffn1_norm, ffn1_w_gate, ffn1_w_up, ffn1_w_down, mix_norm, w_in, conv_w, a_log, dt_bias, gdn_norm_w, q_norm_w, k_norm_w, rel_bias, w_out, ffn2_norm, ffn2_w_gate, ffn2_w_up, ffn2_w_down, final_norm, loss_target, m_ffn1_norm, m_ffn1_w_gate, m_ffn1_w_up, m_ffn1_w_down, m_mix_norm, m_w_in, m_conv_w, m_a_log, m_dt_bias, m_gdn_norm_w, m_q_norm_w, m_k_norm_w, m_rel_bias, m_w_out, m_ffn2_norm, m_ffn2_w_gate, m_ffn2_w_up, m_ffn2_w_down, m_final_norm, v_ffn1_norm, v_ffn1_w_gate, v_ffn1_w_up, v_ffn1_w_down, v_mix_norm, v_w_in, v_conv_w, v_a_log, v_dt_bias, v_gdn_norm_w, v_q_norm_w, v_k_norm_w, v_rel_bias, v_w_out, v_ffn2_norm, v_ffn2_w_gate, v_ffn2_w_up, v_ffn2_w_down, v_final_norm):
    p = dict(locals())
    xs, target = x[0], loss_target[0]
    t, d = xs.shape
    nc = t // CHUNK
    me = 2 * lax.axis_index("x") + lax.axis_index("y")

    gathered = _all_gather([p[n][0].astype(BF16) for n in BIG] + [conv_w[0]])
    gw = dict(zip(BIG + ["conv_w"], gathered))
    w_in_full = jnp.transpose(gw["w_in"], (1, 0, 2)).reshape(d, N_IN_COLS)
    wp = jnp.concatenate([w_in_full[:, :2048], jnp.pad(w_in_full[:, 2048:2064], ((0, 0), (0, LANES - 16))),
                          w_in_full[:, 2064:]], axis=1)
    w_out_full = gw["w_out"].reshape(d, d)
    conv_rows = conv_w.shape[1]
    cw = jnp.pad(gw["conv_w"].reshape(N_SHARDS * conv_rows, CONV_TAPS).T, ((0, 8 - CONV_TAPS), (0, 0)))
    gp = jnp.pad(jnp.stack([a_log.reshape(8), dt_bias.reshape(8)]), ((0, 6), (0, LANES - 8)))
    gdn_w = gdn_norm_w.reshape(1, GDN_DIM)
    qw_t = jnp.tile(q_norm_w.reshape(1, SWA_DIM), (1, SWA_HEADS))
    kw_t = jnp.tile(k_norm_w.reshape(1, SWA_DIM), (1, SWA_HEADS))
    bd = jnp.asarray(np.kron(np.eye(2), np.full((SWA_DIM, SWA_DIM), 1.0 / SWA_DIM)), F32)
    f1 = (gw["ffn1_w_gate"], gw["ffn1_w_up"], gw["ffn1_w_down"])
    f2 = (gw["ffn2_w_gate"], gw["ffn2_w_up"], gw["ffn2_w_down"])

    x1, xn1, g1, u1 = _ffn_fwd(xs, ffn1_norm, *f1, TM, "ffn1_fwd")
    hn, qkva, z, ab, qkvb = _mix_in_fwd(x1, mix_norm, wp, TM)
    qkvc, gb = _gdn_prep_fwd(qkva, cw, ab, gp, TM)
    gbt = jnp.transpose(gb[:, :16].reshape(nc, CHUNK, 16), (0, 2, 1))
    o_f, o_b, gdn_saved = _gdn_fwd(qkvc, gb, gbt)
    oa = _gdn_post_fwd(o_f, o_b, z, gdn_w, TE)
    o_swa, o_swa16, swa_saved = _swa_branch_fwd(qkvb, qw_t, kw_t, rel_bias, bd, TE)
    x2 = _mix_out_fwd(x1, oa, o_swa, w_out_full, TM)
    x3, xn2, g2, u2 = _ffn_fwd(x2, ffn2_norm, *f2, TM, "ffn2_fwd")
    dx3, loss_part, d_final = _final_loss(x3, final_norm, target, TE)

    dx2, dyh2, dg2, du2, h2, d_nw2 = _ffn_bwd_dx(dx3, x2, ffn2_norm, g2, u2, *f2, TM, "ffn2_bwd_dx")
    dwg2 = _matmul_tn(xn2, dg2, TE, "ffn2_dwg")
    dwu2 = _matmul_tn(xn2, du2, TE, "ffn2_dwu")
    dwd2 = _matmul_tn(h2, dyh2, TE, "ffn2_dwd")
    doa, dob, dx2b = _mix_out_bwd(dx2, w_out_full, TM)
    dwo = jnp.concatenate([_matmul_tn(oa, dx2b, TE, "w_out_dw_a")[0], _matmul_tn(o_swa16, dx2b, TE, "w_out_dw_b")[0]],
                          axis=0).reshape(N_SHARDS, d // N_SHARDS, d)
    do_g, dz, d_gdnw = _gdn_post_bwd(doa, o_f, o_b, z, gdn_w, TE)
    dqkvc, dgates = _gdn_bwd(qkvc, gb, gbt, do_g, gdn_saved)
    dqkva, dab, dcw, dgp = _gdn_prep_bwd(qkva, cw, ab, gp, dqkvc, dgates, TM)
    dqkvb, d_qw, d_kw, d_rel = _swa_branch_bwd(dob, o_swa, swa_saved, qkvb, qw_t, kw_t, bd, TE)
    dpieces = (dqkva, dz, dab, dqkvb)
    dx1, d_mixnw = _mix_in_bwd_dx(dx2, x1, mix_norm, dpieces, wp, TM)
    dwp = [_matmul_tn(hn, dp, TE, f"w_in_dw_{i}")[0] for i, dp in enumerate(dpieces)]
    dw_in = jnp.concatenate([dwp[0], dwp[1], dwp[2][:, :16], dwp[3]], axis=1)
    dw_in = jnp.transpose(dw_in.reshape(d, N_SHARDS, N_IN_COLS // N_SHARDS), (1, 0, 2))
    gx, dyh1, dg1, du1, h1, d_nw1 = _ffn_bwd_dx(dx1, xs, ffn1_norm, g1, u1, *f1, TM, "ffn1_bwd_dx")
    dwg1 = _matmul_tn(xn1, dg1, TE, "ffn1_dwg")
    dwu1 = _matmul_tn(xn1, du1, TE, "ffn1_dwu")
    dwd1 = _matmul_tn(h1, dyh1, TE, "ffn1_dwd")

    partial = [dwg1, dwu1, dwd1, dw_in, dwo, dwg2, dwu2, dwd2]
    nb = len(partial)
    pair = _rs_pair(partial)
    chip_sums = [_add_pair(pair[i], pair[nb + i], f"rs_add_{i}") for i in range(nb)]
    slots = _rs_chips(chip_sums)
    halves = [_sum_slots(s, f"rs_sum_{i}") for i, s in enumerate(slots)]
    g_big = dict(zip(BIG, _rs_join(halves)))

    small_partial = {"ffn1_norm": d_nw1, "mix_norm": d_mixnw, "a_log": dgp[0, 0:8], "dt_bias": dgp[1, 0:8],
                     "gdn_norm_w": d_gdnw, "q_norm_w": d_qw, "k_norm_w": d_kw, "rel_bias": d_rel,
                     "ffn2_norm": d_nw2, "final_norm": d_final, "conv_w": dcw[0:CONV_TAPS].T}
    red = _all_reduce_small(_pack([small_partial[n] for n in SMALL] + [loss_part[0, 0:1]]))
    full_shapes = [p[n].shape if n != "conv_w" else (N_SHARDS * conv_rows, CONV_TAPS) for n in SMALL]
    red_parts = _unpack(red, full_shapes + [(1,)])
    loss = red_parts[-1].reshape(())
    g_small = dict(zip(SMALL, red_parts[:-1]))
    g_small["conv_w"] = lax.dynamic_slice_in_dim(g_small["conv_w"], me * conv_rows, conv_rows, 0).reshape(conv_w.shape)

    grads, deltas, new_m, new_v = {}, {}, {}, {}
    for n in BIG:
        grads[n] = g_big[n][None]
        dl, nm, nv = _adamw(p[n][0], g_big[n], p["m_" + n][0], p["v_" + n][0], "adamw_" + n)
        deltas[n], new_m[n], new_v[n] = dl[None], nm[None], nv[None]
    packed = [_pack([src[n] for n in SMALL]) for src in
              ({n: p[n] for n in SMALL}, g_small, {n: p["m_" + n] for n in SMALL}, {n: p["v_" + n] for n in SMALL})]
    small_shapes = [p[n].shape for n in SMALL]
    for dst, arr in zip((deltas, new_m, new_v), _adamw(*packed, "adamw_small")):
        dst.update(zip(SMALL, _unpack(arr, small_shapes)))
    grads.update(g_small)

    return (loss, gx[None], *[grads[n] for n in WEIGHTS], *[deltas[n] for n in WEIGHTS],
            *[new_m[n] for n in WEIGHTS], *[new_v[n] for n in WEIGHTS])
```

```python
import functools
import math

import numpy as np
import jax
import jax.numpy as jnp
from jax import lax
from jax.experimental import pallas as pl
from jax.experimental.pallas import tpu as pltpu

F32 = jnp.float32
BF16 = jnp.bfloat16
HIGHEST = lax.Precision.HIGHEST
MESH = pl.DeviceIdType.MESH

EPS = 1e-6
NEG_BIG = -1e30
GDN_HEADS = 4
GDN_DIM = 128
CHUNK = 64
SWA_HEADS = 8
SWA_DIM = 64
PATTERNS = ((128, 1), (512, 4), (2048, 16))
RADIUS = 64
REL_BUCKETS = 32
REL_MAX_DISTANCE = 1024
CONV_TAPS = 5
N_SHARDS = 4
LANES = 128
VMEM_LIMIT = 56 * 1024 * 1024

ADAM_LR, ADAM_B1, ADAM_B2, ADAM_EPS, ADAM_WD, ADAM_STEP = 0.001, 0.9, 0.999, 1e-08, 0.01, 10


def _params(sem=None, vmem=None):
    return pltpu.CompilerParams(dimension_semantics=sem, vmem_limit_bytes=vmem)


def _resident(shape):
    nd = len(shape)
    return pl.BlockSpec(shape, lambda *_: (0,) * nd, pipeline_mode=pl.Buffered(1))


def _dot(a, b):
    return jnp.dot(a.astype(BF16), b.astype(BF16), preferred_element_type=F32)


def _dot_nt(a, b):
    return lax.dot_general(a.astype(BF16), b.astype(BF16), (((1,), (1,)), ((), ())), preferred_element_type=F32)


def _dot_tn(a, b):
    return lax.dot_general(a.astype(BF16), b.astype(BF16), (((0,), (0,)), ((), ())), preferred_element_type=F32)


def _dot_hi(a, b):
    return jnp.dot(a, b, preferred_element_type=F32, precision=HIGHEST)


def _sigmoid(x):
    return 1.0 / (1.0 + jnp.exp(-x))


def _rstd(xf):
    return lax.rsqrt(jnp.mean(xf * xf, axis=-1, keepdims=True) + EPS)


def _rms_bwd(xf, r, nw, dxn):
    xhat = xf * r
    dxh = dxn * nw
    dx = r * (dxh - xhat * jnp.mean(dxh * xhat, axis=-1, keepdims=True))
    return dx, jnp.sum(dxn * xhat, axis=0, keepdims=True)


def _ffn_fwd(x, nw, wg, wu, wd, tm, name):
    t, d = x.shape
    nj, _, fs = wg.shape

    def body(x_ref, nw_ref, wg_ref, wu_ref, wd_ref, y_ref, xn_ref, g_ref, u_ref):
        xf = x_ref[...]
        xn = (xf * _rstd(xf) * nw_ref[...]).astype(BF16)
        xn_ref[...] = xn
        acc = jnp.zeros((tm, d), F32)
        for j in range(nj):
            g = jnp.dot(xn, wg_ref[j], preferred_element_type=F32)
            u = jnp.dot(xn, wu_ref[j], preferred_element_type=F32)
            h = (g * _sigmoid(g) * u).astype(BF16)
            acc = acc + jnp.dot(h, wd_ref[j], preferred_element_type=F32)
            g_ref[j] = g.astype(BF16)
            u_ref[j] = u.astype(BF16)
        y_ref[...] = xf + 0.5 * acc

    row = pl.BlockSpec((tm, d), lambda i: (i, 0))
    act = pl.BlockSpec((nj, tm, fs), lambda i: (0, i, 0))
    return pl.pallas_call(
        body, name=name, grid=(t // tm,),
        in_specs=[row, _resident((1, d)), _resident(wg.shape), _resident(wu.shape), _resident(wd.shape)],
        out_specs=[row, row, act, act],
        out_shape=[jax.ShapeDtypeStruct((t, d), F32), jax.ShapeDtypeStruct((t, d), BF16),
                   jax.ShapeDtypeStruct((nj, t, fs), BF16), jax.ShapeDtypeStruct((nj, t, fs), BF16)],
        compiler_params=_params(("arbitrary",), VMEM_LIMIT),
    )(x, nw, wg, wu, wd)


def _ffn_bwd_dx(dy, x, nw, g, u, wg, wu, wd, tm, name):
    t, d = x.shape
    nj, _, fs = wg.shape

    def body(dy_ref, x_ref, nw_ref, g_ref, u_ref, wg_ref, wu_ref, wd_ref,
             dx_ref, dyh_ref, dg_ref, du_ref, h_ref, dnw_ref):
        @pl.when(pl.program_id(0) == 0)
        def _():
            dnw_ref[...] = jnp.zeros_like(dnw_ref)

        dyv = dy_ref[...]
        dyh = (0.5 * dyv).astype(BF16)
        dyh_ref[...] = dyh
        dxn = jnp.zeros((tm, d), F32)
        for j in range(nj):
            gv = g_ref[j].astype(F32)
            uv = u_ref[j].astype(F32)
            dh = _dot_nt(dyh, wd_ref[j])
            sg = _sigmoid(gv)
            si = gv * sg
            dg = (dh * uv * (sg * (1.0 + gv * (1.0 - sg)))).astype(BF16)
            du = (dh * si).astype(BF16)
            h_ref[j] = (si * uv).astype(BF16)
            dg_ref[j] = dg
            du_ref[j] = du
            dxn = dxn + _dot_nt(dg, wg_ref[j]) + _dot_nt(du, wu_ref[j])
        xf = x_ref[...]
        dxr, dnw = _rms_bwd(xf, _rstd(xf), nw_ref[...], dxn)
        dx_ref[...] = dyv + dxr
        dnw_ref[...] += dnw

    row = pl.BlockSpec((tm, d), lambda i: (i, 0))
    act = pl.BlockSpec((nj, tm, fs), lambda i: (0, i, 0))
    act_shape = jax.ShapeDtypeStruct((nj, t, fs), BF16)
    return pl.pallas_call(
        body, name=name, grid=(t // tm,),
        in_specs=[row, row, _resident((1, d)), act, act, _resident(wg.shape), _resident(wu.shape), _resident(wd.shape)],
        out_specs=[row, row, act, act, act, pl.BlockSpec((1, d), lambda i: (0, 0))],
        out_shape=[jax.ShapeDtypeStruct((t, d), F32), jax.ShapeDtypeStruct((t, d), BF16),
                   act_shape, act_shape, act_shape, jax.ShapeDtypeStruct((1, d), F32)],
        compiler_params=_params(("arbitrary",), VMEM_LIMIT),
    )(dy, x, nw, g, u, wg, wu, wd)


def _matmul_tn(a, b, tk, name):
    a3, b3 = a.ndim == 3, b.ndim == 3
    nj = a.shape[0] if a3 else (b.shape[0] if b3 else 1)
    t, m = a.shape[-2:]
    n = b.shape[-1]
    nt = t // tk

    def body(a_ref, b_ref, o_ref, acc_ref):
        k = pl.program_id(1)

        @pl.when(k == 0)
        def _():
            acc_ref[...] = jnp.zeros_like(acc_ref)

        acc_ref[...] += lax.dot_general(a_ref[...], b_ref[...], (((0,), (0,)), ((), ())),
                                        preferred_element_type=F32)

        @pl.when(k == nt - 1)
        def _():
            o_ref[...] = acc_ref[...].astype(o_ref.dtype)

    a_spec = (pl.BlockSpec((None, tk, m), lambda j, k: (j, k, 0)) if a3
              else pl.BlockSpec((tk, m), lambda j, k: (k, 0)))
    b_spec = (pl.BlockSpec((None, tk, n), lambda j, k: (j, k, 0)) if b3
              else pl.BlockSpec((tk, n), lambda j, k: (k, 0)))
    return pl.pallas_call(
        body, name=name, grid=(nj, nt),
        in_specs=[a_spec, b_spec],
        out_specs=pl.BlockSpec((None, m, n), lambda j, k: (j, 0, 0)),
        out_shape=jax.ShapeDtypeStruct((nj, m, n), BF16),
        scratch_shapes=[pltpu.VMEM((m, n), F32)],
        compiler_params=_params(("arbitrary", "arbitrary"), VMEM_LIMIT),
    )(a, b)


P_QKVA, P_Z, P_AB, P_QKVB = (0, 1536), (1536, 2048), (2048, 2176), (2176, 3712)
P_PIECES = (P_QKVA, P_Z, P_AB, P_QKVB)
P_COLS = 3712


def _mix_in_fwd(x1, nw, wp, tm):
    t, d = x1.shape

    def body(x_ref, nw_ref, w_ref, hn_ref, *outs):
        xf = x_ref[...]
        xn = (xf * _rstd(xf) * nw_ref[...]).astype(BF16)
        hn_ref[...] = xn
        for (a, b), o_ref in zip(P_PIECES, outs):
            o_ref[...] = jnp.dot(xn, w_ref[:, a:b], preferred_element_type=F32)

    row = pl.BlockSpec((tm, d), lambda i: (i, 0))
    return pl.pallas_call(
        body, name="mix_in_fwd", grid=(t // tm,),
        in_specs=[row, _resident((1, d)), _resident(wp.shape)],
        out_specs=[row] + [pl.BlockSpec((tm, b - a), lambda i: (i, 0)) for a, b in P_PIECES],
        out_shape=[jax.ShapeDtypeStruct((t, d), BF16)]
                  + [jax.ShapeDtypeStruct((t, b - a), F32) for a, b in P_PIECES],
        compiler_params=_params(("arbitrary",), VMEM_LIMIT),
    )(x1, nw, wp)


def _mix_in_bwd_dx(dx, x1, nw, dpieces, wp, tm):
    t, d = x1.shape

    def body(dx_ref, x_ref, nw_ref, p0, p1, p2, p3, w_ref, o_ref, dnw_ref):
        @pl.when(pl.program_id(0) == 0)
        def _():
            dnw_ref[...] = jnp.zeros_like(dnw_ref)

        dh = jnp.zeros((tm, d), F32)
        for (a, b), p_ref in zip(P_PIECES, (p0, p1, p2, p3)):
            dh = dh + _dot_nt(p_ref[...], w_ref[:, a:b])
        xf = x_ref[...]
        dxr, dnw = _rms_bwd(xf, _rstd(xf), nw_ref[...], dh)
        o_ref[...] = dx_ref[...] + dxr
        dnw_ref[...] += dnw

    row = pl.BlockSpec((tm, d), lambda i: (i, 0))
    return pl.pallas_call(
        body, name="mix_in_bwd_dx", grid=(t // tm,),
        in_specs=[row, row, _resident((1, d))]
                 + [pl.BlockSpec((tm, b - a), lambda i: (i, 0)) for a, b in P_PIECES] + [_resident(wp.shape)],
        out_specs=[row, pl.BlockSpec((1, d), lambda i: (0, 0))],
        out_shape=[jax.ShapeDtypeStruct((t, d), F32), jax.ShapeDtypeStruct((1, d), F32)],
        compiler_params=_params(("arbitrary",), VMEM_LIMIT),
    )(dx, x1, nw, *dpieces, wp)


def _mix_out_fwd(x1, oa, ob, w, tm):
    t, d = x1.shape
    half = oa.shape[1]

    def body(x_ref, oa_ref, ob_ref, w_ref, o_ref):
        o_ref[...] = (x_ref[...] + _dot(oa_ref[...], w_ref[0:half, :]) + _dot(ob_ref[...], w_ref[half:2 * half, :]))

    row = pl.BlockSpec((tm, d), lambda i: (i, 0))
    hrow = pl.BlockSpec((tm, half), lambda i: (i, 0))
    return pl.pallas_call(
        body, name="mix_out_fwd", grid=(t // tm,),
        in_specs=[row, hrow, hrow, _resident(w.shape)],
        out_specs=row, out_shape=jax.ShapeDtypeStruct((t, d), F32),
        compiler_params=_params(("arbitrary",), VMEM_LIMIT),
    )(x1, oa, ob, w)


def _mix_out_bwd(dx2, w, tm):
    t, d = dx2.shape
    half = w.shape[0] // 2

    def body(dx_ref, w_ref, doa_ref, dob_ref, dxb_ref):
        dxb = dx_ref[...].astype(BF16)
        dxb_ref[...] = dxb
        doa_ref[...] = _dot_nt(dxb, w_ref[0:half, :])
        dob_ref[...] = _dot_nt(dxb, w_ref[half:2 * half, :])

    row = pl.BlockSpec((tm, d), lambda i: (i, 0))
    hrow = pl.BlockSpec((tm, half), lambda i: (i, 0))
    return pl.pallas_call(
        body, name="mix_out_bwd", grid=(t // tm,),
        in_specs=[row, _resident(w.shape)],
        out_specs=[hrow, hrow, row],
        out_shape=[jax.ShapeDtypeStruct((t, half), F32), jax.ShapeDtypeStruct((t, half), F32),
                   jax.ShapeDtypeStruct((t, d), BF16)],
        compiler_params=_params(("arbitrary",), VMEM_LIMIT),
    )(dx2, w)


def _final_loss(x3, fw, target, tm):
    t, d = x3.shape

    def body(x_ref, w_ref, t_ref, dx_ref, loss_ref, dw_ref):
        @pl.when(pl.program_id(0) == 0)
        def _():
            loss_ref[...] = jnp.zeros_like(loss_ref)
            dw_ref[...] = jnp.zeros_like(dw_ref)

        xf = x_ref[...]
        r = _rstd(xf)
        err = xf * r * w_ref[...] - t_ref[...]
        loss_ref[...] += 0.5 * jnp.sum(jnp.mean(err * err, axis=-1, keepdims=True), axis=0, keepdims=True)
        dxr, dw = _rms_bwd(xf, r, w_ref[...], err * (1.0 / d))
        dx_ref[...] = dxr
        dw_ref[...] += dw

    row = pl.BlockSpec((tm, d), lambda i: (i, 0))
    return pl.pallas_call(
        body, name="final_loss", grid=(t // tm,),
        in_specs=[row, _resident((1, d)), row],
        out_specs=[row, pl.BlockSpec((1, LANES), lambda i: (0, 0)), pl.BlockSpec((1, d), lambda i: (0, 0))],
        out_shape=[jax.ShapeDtypeStruct((t, d), F32), jax.ShapeDtypeStruct((1, LANES), F32),
                   jax.ShapeDtypeStruct((1, d), F32)],
        compiler_params=_params(("arbitrary",), VMEM_LIMIT),
    )(x3, fw, target)


HALO = 8


def _halo_row_specs(tr, cols, nrow8):
    per = tr // HALO
    return [pl.BlockSpec((tr, cols), lambda i: (i, 0)),
            pl.BlockSpec((HALO, cols), lambda i: (jnp.maximum(i * per - 1, 0), 0)),
            pl.BlockSpec((HALO, cols), lambda i: (jnp.minimum((i + 1) * per, nrow8 - 1), 0))]


def _conv_window(xm, xp, xn, first, last, cols):
    prev = jnp.where(first, 0.0, xp[:, cols])
    nxt = jnp.where(last, 0.0, xn[:, cols])
    return jnp.concatenate([prev, xm[:, cols], nxt], axis=0)


def _shift_rows(xw, off):
    n = xw.shape[0]
    sh = (-off) % n
    return xw if sh == 0 else pltpu.roll(xw, sh, 0)


def _conv_pre(xw, cw_ref, cols):
    acc = None
    for j in range(CONV_TAPS):
        term = _shift_rows(xw, j - CONV_TAPS // 2) * cw_ref[j:j + 1, cols]
        acc = term if acc is None else acc + term
    return acc


def _softplus(x):
    u = jnp.exp(-jnp.abs(x))
    w = 1.0 + u
    log1p = jnp.where(w == 1.0, u, jnp.log(w) * u / jnp.where(w == 1.0, 1.0, w - 1.0))
    return jnp.maximum(x, 0.0) + log1p


def _gdn_prep_fwd(qkva, cw, ab, gp, tr):
    t, c = qkva.shape
    nt = t // tr
    ncb = c // LANES

    def body(xm, xp, xn, cw_ref, ab_ref, gp_ref, o_ref, gb_ref):
        i = pl.program_id(0)
        first, last = i == 0, i == nt - 1
        for cb in range(ncb):
            cols = slice(cb * LANES, (cb + 1) * LANES)
            xw = _conv_window(xm, xp, xn, first, last, cols)
            pre = _conv_pre(xw, cw_ref, cols)[HALO:HALO + tr]
            y = pre * _sigmoid(pre)
            if cb < 2 * GDN_HEADS:
                y = y * lax.rsqrt(jnp.sum(y * y, axis=-1, keepdims=True) + EPS)
            if cb < GDN_HEADS:
                y = y * (GDN_DIM ** -0.5)
            o_ref[:, cols] = y
        abv = ab_ref[...]
        lane = lax.broadcasted_iota(jnp.int32, abv.shape, 1)
        g = -jnp.exp(gp_ref[0:1, :]) * _softplus(abv + gp_ref[1:2, :])
        gb_ref[...] = jnp.where(lane < 8, g, jnp.where(lane < 16, _sigmoid(abv), 0.0))

    return pl.pallas_call(
        body, name="gdn_prep_fwd", grid=(nt,),
        in_specs=_halo_row_specs(tr, c, t // HALO)
                 + [_resident(cw.shape), pl.BlockSpec((tr, LANES), lambda i: (i, 0)), _resident(gp.shape)],
        out_specs=[pl.BlockSpec((tr, c), lambda i: (i, 0)), pl.BlockSpec((tr, LANES), lambda i: (i, 0))],
        out_shape=[jax.ShapeDtypeStruct((t, c), F32), jax.ShapeDtypeStruct((t, LANES), F32)],
        compiler_params=_params(("arbitrary",), VMEM_LIMIT),
    )(qkva, qkva, qkva, cw, ab, gp)


def _gdn_prep_bwd(qkva, cw, ab, gp, dy, dgates, tr):
    t, c = qkva.shape
    nt = t // tr
    ncb = c // LANES

    def body(xm, xp, xn, fm, fp, fn, cw_ref, ab_ref, gp_ref, gf_ref, dx_ref, dab_ref, dcw_ref, dgp_ref):
        i = pl.program_id(0)
        first, last = i == 0, i == nt - 1

        @pl.when(first)
        def _():
            dcw_ref[...] = jnp.zeros_like(dcw_ref)
            dgp_ref[...] = jnp.zeros_like(dgp_ref)

        sub8 = lax.broadcasted_iota(jnp.int32, (8, LANES), 0)
        for cb in range(ncb):
            cols = slice(cb * LANES, (cb + 1) * LANES)
            xw = _conv_window(xm, xp, xn, first, last, cols)
            dyw = _conv_window(fm, fp, fn, first, last, cols)
            pre = _conv_pre(xw, cw_ref, cols)
            sg = _sigmoid(pre)
            s = pre * sg
            if cb < 2 * GDN_HEADS:
                scale = (GDN_DIM ** -0.5) if cb < GDN_HEADS else 1.0
                r = lax.rsqrt(jnp.sum(s * s, axis=-1, keepdims=True) + EPS)
                dn = dyw * scale
                ds = r * dn - s * (r * r * r) * jnp.sum(dn * s, axis=-1, keepdims=True)
            else:
                ds = dyw
            dpre = ds * (sg * (1.0 + pre * (1.0 - sg)))
            dx = None
            dcw = jnp.zeros((8, LANES), F32)
            for j in range(CONV_TAPS):
                off = j - CONV_TAPS // 2
                term = _shift_rows(dpre, -off)[HALO:HALO + tr] * cw_ref[j:j + 1, cols]
                dx = term if dx is None else dx + term
                tap = jnp.sum(dpre[HALO:HALO + tr] * _shift_rows(xw, off)[HALO:HALO + tr], axis=0, keepdims=True)
                dcw = dcw + jnp.where(sub8 == j, tap, 0.0)
            dx_ref[:, cols] = dx.astype(BF16)
            dcw_ref[:, cols] += dcw

        abv = ab_ref[...]
        dgb = gf_ref[...]
        lane = lax.broadcasted_iota(jnp.int32, abv.shape, 1)
        nea = -jnp.exp(gp_ref[0:1, :])
        xs = abv + gp_ref[1:2, :]
        g = nea * _softplus(xs)
        beta = _sigmoid(abv)
        da = dgb * nea * _sigmoid(xs)
        dab = jnp.where(lane < 8, da, jnp.where(lane < 16, dgb * beta * (1.0 - beta), 0.0))
        dab_ref[...] = dab.astype(BF16)
        keep = lane[0:1, :] < 8
        dalog = jnp.where(keep, jnp.sum(dgb * g, axis=0, keepdims=True), 0.0)
        ddtb = jnp.where(keep, jnp.sum(da, axis=0, keepdims=True), 0.0)
        dgp_ref[...] += jnp.where(sub8 == 0, dalog, 0.0) + jnp.where(sub8 == 1, ddtb, 0.0)

    lrow = pl.BlockSpec((tr, LANES), lambda i: (i, 0))
    halo = _halo_row_specs(tr, c, t // HALO)
    return pl.pallas_call(
        body, name="gdn_prep_bwd", grid=(nt,),
        in_specs=halo + halo + [_resident(cw.shape), lrow, _resident(gp.shape), lrow],
        out_specs=[pl.BlockSpec((tr, c), lambda i: (i, 0)), lrow,
                   pl.BlockSpec(cw.shape, lambda i: (0, 0)), pl.BlockSpec(gp.shape, lambda i: (0, 0))],
        out_shape=[jax.ShapeDtypeStruct((t, c), BF16), jax.ShapeDtypeStruct((t, LANES), BF16),
                   jax.ShapeDtypeStruct(cw.shape, F32), jax.ShapeDtypeStruct(gp.shape, F32)],
        compiler_params=_params(("arbitrary",), VMEM_LIMIT),
    )(qkva, qkva, qkva, dy, dy, dy, cw, ab, gp, dgates)


def _chunk_masks(lower):
    ii = lax.broadcasted_iota(jnp.int32, (CHUNK, CHUNK), 0)
    jj = lax.broadcasted_iota(jnp.int32, (CHUNK, CHUNK), 1)
    incl = (ii >= jj) if lower else (ii <= jj)
    strict = (ii > jj) if lower else (ii < jj)
    return ii, jj, incl, strict


def _dot3(a, b):
    ah = a.astype(BF16)
    al = (a - ah.astype(F32)).astype(BF16)
    bh = b.astype(BF16)
    bl = (b - bh.astype(F32)).astype(BF16)
    d = lambda u, v: jnp.dot(u, v, preferred_element_type=F32)
    return d(ah, bh) + (d(ah, bl) + d(al, bh))


def _tri_inv_many(lmats, ii, jj):
    m16 = (ii // 16) == (jj // 16)
    m32 = (ii // 32) == (jj // 32)
    eye = jnp.where(ii == jj, 1.0, 0.0)
    l16 = [jnp.where(m16, l, 0.0) for l in lmats]
    p2 = [_dot3(a, a) for a in l16]
    p4 = [_dot3(a, a) for a in p2]
    p8 = [_dot3(a, a) for a in p4]
    xs = [eye - a for a in l16]
    for ps in (p2, p4, p8):
        xs = [x + _dot3(x, p) for x, p in zip(xs, ps)]
    for off in ([jnp.where(m32 & jnp.logical_not(m16), l, 0.0) for l in lmats],
                [jnp.where(m32, 0.0, l) for l in lmats]):
        ys = [_dot3(x, c) for x, c in zip(xs, off)]
        xs = [x - _dot3(y, x) for x, y in zip(xs, ys)]
    return xs


def _col_to_row(col, ii, jj):
    return jnp.sum(jnp.where(ii == jj, col, 0.0), axis=0, keepdims=True)


def _row_to_col(row, ii, jj):
    return jnp.sum(jnp.where(ii == jj, row, 0.0), axis=1, keepdims=True)


def _chain_common(q, k, v, graw_col, graw_row, bcol, masks):
    ii, jj, incl, strict = masks
    inclt = jnp.logical_not(strict)
    gcol = jnp.sum(jnp.where(incl, graw_row, 0.0), axis=1, keepdims=True)
    grow = jnp.sum(jnp.where(inclt, graw_col, 0.0), axis=0, keepdims=True)
    glast = jnp.sum(graw_row, axis=1, keepdims=True)
    decay = jnp.where(incl, jnp.exp(jnp.where(incl, gcol - grow, 0.0)), 0.0)
    kb = k * bcol
    vb = v * bcol
    eg = jnp.exp(gcol)
    ek = jnp.exp(glast - gcol)
    kbg = kb * eg
    amat = _dot_nt(kb, k)
    qk = _dot_nt(q, k)
    return dict(gcol=gcol, glast=glast, decay=decay, kb=kb, vb=vb, eg=eg, ek=ek, kbg=kbg, amat=amat, qk=qk,
                intra=qk * decay, qg=q * eg, kdec=k * ek)


def _gdn_fwd(qkvc, gb, gbt):
    tm, u, w, qg, kd, intra, egl = _gdn_local_fwd(qkvc, gb, gbt)
    o_f, o_b, s_f, s_b, vn_f, vn_b = _gdn_scan_fwd(u, w, qg, kd, intra, egl, qkvc.shape[0])
    return o_f, o_b, dict(tm=tm, w=w, qg=qg, kd=kd, intra=intra, egl=egl, s=(s_f, s_b), vn=(vn_f, vn_b))


N_CHAINS = 2 * GDN_HEADS


def _load_chains(x_ref, g_ref, gt_ref):
    hd = GDN_HEADS * GDN_DIM
    chains = []
    for d in range(2):
        masks = _chunk_masks(d == 0)
        for h in range(GDN_HEADS):
            ch = d * GDN_HEADS + h
            q = x_ref[:, h * GDN_DIM:(h + 1) * GDN_DIM]
            k = x_ref[:, hd + h * GDN_DIM:hd + (h + 1) * GDN_DIM]
            v = x_ref[:, 2 * hd + h * GDN_DIM:2 * hd + (h + 1) * GDN_DIM]
            bcol = g_ref[:, 8 + ch:9 + ch]
            cm = _chain_common(q, k, v, g_ref[:, ch:ch + 1], gt_ref[0, ch:ch + 1, :], bcol, masks)
            chains.append(dict(cm, q=q, k=k, v=v, bcol=bcol, masks=masks, ch=ch, h=h))
    return chains


def _chain_shape(rows, cols, dtype):
    return lambda nc: jax.ShapeDtypeStruct((nc, N_CHAINS, rows, cols), dtype)


def _gdn_local_fwd(qkvc, gb, gbt):
    t = qkvc.shape[0]
    nc = t // CHUNK
    hd = GDN_HEADS * GDN_DIM

    def body(x_ref, g_ref, gt_ref, t_ref, u_ref, w_ref, qg_ref, kd_ref, in_ref, eg_ref):
        chains = _load_chains(x_ref, g_ref, gt_ref)
        ii, jj = chains[0]["masks"][0:2]
        tms = _tri_inv_many([jnp.where(c["masks"][3], c["amat"] * c["decay"], 0.0) for c in chains], ii, jj)
        us = [_dot(tm, c["vb"]) for tm, c in zip(tms, chains)]
        ws = [_dot(tm, c["kbg"]) for tm, c in zip(tms, chains)]
        for c, tm, u, w in zip(chains, tms, us, ws):
            ch = c["ch"]
            t_ref[0, ch] = tm
            u_ref[0, ch] = u
            w_ref[0, ch] = w.astype(BF16)
            qg_ref[0, ch] = c["qg"].astype(BF16)
            kd_ref[0, ch] = c["kdec"].astype(BF16)
            in_ref[0, ch] = c["intra"].astype(BF16)
            eg_ref[0, ch:ch + 1, :] = jnp.broadcast_to(jnp.exp(c["glast"]), (1, LANES))

    blk = lambda rows, cols: pl.BlockSpec((1, N_CHAINS, rows, cols), lambda n: (n, 0, 0, 0))
    shapes = [_chain_shape(CHUNK, CHUNK, F32), _chain_shape(CHUNK, GDN_DIM, F32), _chain_shape(CHUNK, GDN_DIM, BF16),
              _chain_shape(CHUNK, GDN_DIM, BF16), _chain_shape(CHUNK, GDN_DIM, BF16), _chain_shape(CHUNK, CHUNK, BF16)]
    return tuple(pl.pallas_call(
        body, name="gdn_local_fwd", grid=(nc,),
        in_specs=[pl.BlockSpec((CHUNK, 3 * hd), lambda n: (n, 0)), pl.BlockSpec((CHUNK, LANES), lambda n: (n, 0)),
                  pl.BlockSpec((1, 16, CHUNK), lambda n: (n, 0, 0))],
        out_specs=[blk(CHUNK, CHUNK), blk(CHUNK, GDN_DIM), blk(CHUNK, GDN_DIM), blk(CHUNK, GDN_DIM),
                   blk(CHUNK, GDN_DIM), blk(CHUNK, CHUNK), pl.BlockSpec((1, N_CHAINS, LANES), lambda n: (n, 0, 0))],
        out_shape=[s(nc) for s in shapes] + [jax.ShapeDtypeStruct((nc, N_CHAINS, LANES), F32)],
        compiler_params=_params(("arbitrary",), VMEM_LIMIT),
    )(qkvc, gb, gbt))


def _dir_specs(nc, rev):
    def spec(d, rows, cols, own=False):
        chunk = (lambda n: n) if (d == 0) != rev else (lambda n: nc - 1 - n)
        blk = 0 if own else d
        if rows is None:
            return pl.BlockSpec((1, GDN_HEADS if own else N_CHAINS, cols), lambda n: (chunk(n), 0, 0))
        return pl.BlockSpec((1, GDN_HEADS, rows, cols), lambda n: (chunk(n), blk, 0, 0))

    def rows_spec(d, cols):
        chunk = (lambda n: n) if (d == 0) != rev else (lambda n: nc - 1 - n)
        return pl.BlockSpec((CHUNK, cols), lambda n: (chunk(n), 0))
    return spec, rows_spec


def _gdn_scan_fwd(u, w, qg, kd, intra, egl, t):
    nc = t // CHUNK
    hd = GDN_HEADS * GDN_DIM

    def body(*refs):
        ins, outs, state = refs[:12], refs[12:18], refs[18]
        @pl.when(pl.program_id(0) == 0)
        def _():
            state[...] = jnp.zeros_like(state)

        chains = [(d, h) for d in range(2) for h in range(GDN_HEADS)]
        pick = lambda k, d, h: ins[2 * k + d][0, h]
        states = [state[ch] for ch in range(N_CHAINS)]
        sbs = [s.astype(BF16) for s in states]
        ws = [_dot(pick(1, d, h), sb) for (d, h), sb in zip(chains, sbs)]
        o1 = [_dot(pick(2, d, h), sb) for (d, h), sb in zip(chains, sbs)]
        vns = [(pick(0, d, h) - wsb).astype(BF16) for (d, h), wsb in zip(chains, ws)]
        o2 = [_dot(pick(4, d, h), vn) for (d, h), vn in zip(chains, vns)]
        kv = [_dot_tn(pick(3, d, h), vn) for (d, h), vn in zip(chains, vns)]
        for ch, (d, h) in enumerate(chains):
            outs[d][:, h * GDN_DIM:(h + 1) * GDN_DIM] = o1[ch] + o2[ch]
            outs[2 + d][0, h] = states[ch]
            outs[4 + d][0, h] = vns[ch]
            state[ch] = states[ch] * ins[10 + d][0, ch:ch + 1, :] + kv[ch]

    spec, rows_spec = _dir_specs(nc, False)
    pair = lambda rows, cols, own=False: [spec(0, rows, cols, own), spec(1, rows, cols, own)]
    s_shape = jax.ShapeDtypeStruct((nc, GDN_HEADS, GDN_DIM, GDN_DIM), F32)
    vn_shape = jax.ShapeDtypeStruct((nc, GDN_HEADS, CHUNK, GDN_DIM), BF16)
    return pl.pallas_call(
        body, name="gdn_scan_fwd", grid=(nc,),
        in_specs=(pair(CHUNK, GDN_DIM) + pair(CHUNK, GDN_DIM) + pair(CHUNK, GDN_DIM) + pair(CHUNK, GDN_DIM)
                  + pair(CHUNK, CHUNK) + pair(None, LANES)),
        out_specs=([rows_spec(0, hd), rows_spec(1, hd)] + pair(GDN_DIM, GDN_DIM, True)
                   + pair(CHUNK, GDN_DIM, True)),
        out_shape=[jax.ShapeDtypeStruct((t, hd), F32), jax.ShapeDtypeStruct((t, hd), F32),
                   s_shape, s_shape, vn_shape, vn_shape],
        scratch_shapes=[pltpu.VMEM((N_CHAINS, GDN_DIM, GDN_DIM), F32)],
        compiler_params=_params(("arbitrary",), VMEM_LIMIT),
    )(u, u, w, w, qg, qg, kd, kd, intra, intra, egl, egl)


def _gdn_bwd(qkvc, gb, gbt, do, saved):
    scan = _gdn_scan_bwd(do, saved, qkvc.shape[0])
    return _gdn_local_bwd(qkvc, gb, gbt, do, saved, scan)


def _gdn_scan_bwd(do, saved, t):
    nc = t // CHUNK
    hd = GDN_HEADS * GDN_DIM

    def body(*refs):
        ins, outs, dstate = refs[:16], refs[16:26], refs[26]
        @pl.when(pl.program_id(0) == 0)
        def _():
            dstate[...] = jnp.zeros_like(dstate)

        chains = [(d, h) for d in range(2) for h in range(GDN_HEADS)]
        pick = lambda k, d, h: ins[2 * k + d][0, h]
        dss = [dstate[ch] for ch in range(N_CHAINS)]
        dsbs = [ds.astype(BF16) for ds in dss]
        ss = [pick(1, d, h) for d, h in chains]
        sbs = [s.astype(BF16) for s in ss]
        dos = [ins[d][:, h * GDN_DIM:(h + 1) * GDN_DIM].astype(BF16) for d, h in chains]
        dv1 = [_dot_tn(pick(5, d, h), dov) for (d, h), dov in zip(chains, dos)]
        dv2 = [_dot(pick(4, d, h), dsb) for (d, h), dsb in zip(chains, dsbs)]
        ds1 = [_dot_tn(pick(3, d, h), dov) for (d, h), dov in zip(chains, dos)]
        dkds = [_dot_nt(pick(6, d, h), dsb) for (d, h), dsb in zip(chains, dsbs)]
        dqgs = [_dot_nt(dov, sb) for dov, sb in zip(dos, sbs)]
        dvns = [(a + b).astype(BF16) for a, b in zip(dv1, dv2)]
        ds2 = [_dot_tn(pick(2, d, h), dvn) for (d, h), dvn in zip(chains, dvns)]
        dws = [_dot_nt(dvn, sb) for dvn, sb in zip(dvns, sbs)]
        for ch, (d, h) in enumerate(chains):
            egl = ins[14 + d][0, ch:ch + 1, :]
            outs[d][0, h] = dvns[ch]
            outs[2 + d][0, h] = (-dws[ch]).astype(BF16)
            outs[4 + d][0, h] = dqgs[ch]
            outs[6 + d][0, h] = dkds[ch]
            outs[8 + d][0, h:h + 1, :] = egl * jnp.sum(jnp.sum(ss[ch] * dss[ch], axis=1, keepdims=True),
                                                       axis=0, keepdims=True)
            dstate[ch] = ds1[ch] + egl * dss[ch] - ds2[ch]

    spec, rows_spec = _dir_specs(nc, True)
    pair = lambda rows, cols, own=False: [spec(0, rows, cols, own), spec(1, rows, cols, own)]
    s_f, s_b = saved["s"]
    vn_f, vn_b = saved["vn"]
    w, qg, kd, intra, egl = saved["w"], saved["qg"], saved["kd"], saved["intra"], saved["egl"]
    own = lambda rows, cols, dtype: jax.ShapeDtypeStruct((nc, GDN_HEADS, rows, cols), dtype)
    row_shape = jax.ShapeDtypeStruct((nc, GDN_HEADS, LANES), F32)
    return pl.pallas_call(
        body, name="gdn_scan_bwd", grid=(nc,),
        in_specs=([rows_spec(0, hd), rows_spec(1, hd)] + pair(GDN_DIM, GDN_DIM, True) + pair(CHUNK, GDN_DIM)
                  + pair(CHUNK, GDN_DIM) + pair(CHUNK, GDN_DIM) + pair(CHUNK, CHUNK) + pair(CHUNK, GDN_DIM, True)
                  + pair(None, LANES)),
        out_specs=(pair(CHUNK, GDN_DIM, True) + pair(CHUNK, GDN_DIM, True) + pair(CHUNK, GDN_DIM, True)
                   + pair(CHUNK, GDN_DIM, True) + pair(None, LANES, True)),
        out_shape=[own(CHUNK, GDN_DIM, BF16)] * 4 + [own(CHUNK, GDN_DIM, F32)] * 4 + [row_shape] * 2,
        scratch_shapes=[pltpu.VMEM((N_CHAINS, GDN_DIM, GDN_DIM), F32)],
        compiler_params=_params(("arbitrary",), VMEM_LIMIT),
    )(do, do, s_f, s_b, w, w, qg, qg, kd, kd, intra, intra, vn_f, vn_b, egl, egl)


def _dot3_nt(a, b):
    ah = a.astype(BF16)
    al = (a - ah.astype(F32)).astype(BF16)
    bh = b.astype(BF16)
    bl = (b - bh.astype(F32)).astype(BF16)
    return _dot_nt(ah, bh) + (_dot_nt(ah, bl) + _dot_nt(al, bh))


def _dot3_tn(a, b):
    ah = a.astype(BF16)
    al = (a - ah.astype(F32)).astype(BF16)
    bh = b.astype(BF16)
    bl = (b - bh.astype(F32)).astype(BF16)
    return _dot_tn(ah, bh) + (_dot_tn(ah, bl) + _dot_tn(al, bh))


def _gdn_local_bwd(qkvc, gb, gbt, do, saved, scan):
    t = qkvc.shape[0]
    nc = t // CHUNK
    hd = GDN_HEADS * GDN_DIM

    def body(*refs):
        x_ref, g_ref, gt_ref, do_ref, t_ref = refs[:5]
        per_dir = refs[5:17]
        dx_ref, dg_ref = refs[17:]
        chains = _load_chains(x_ref, g_ref, gt_ref)
        lane = lax.broadcasted_iota(jnp.int32, (CHUNK, LANES), 1)
        dgates = jnp.zeros((CHUNK, LANES), F32)
        for c in chains:
            d = c["ch"] // GDN_HEADS
            vn_ref, dvn_ref, dw_ref, dqg_ref, dkd_ref, dgl_ref = per_dir[d::2]
            h = c["h"]
            c.update(tm=t_ref[0, c["ch"]], dov=do_ref[:, h * GDN_DIM:(h + 1) * GDN_DIM], vnew=vn_ref[0, h],
                     dvnew=dvn_ref[0, h], dw=dw_ref[0, h], dqg=dqg_ref[0, h], dkdec=dkd_ref[0, h],
                     dglast=dgl_ref[0, h:h + 1, 0:1])
        dintras = [_dot_nt(c["dov"], c["vnew"]) for c in chains]
        dts = [_dot_nt(c["dvnew"], c["vb"]) + _dot_nt(c["dw"], c["kbg"]) for c in chains]
        dvbs = [_dot_tn(c["tm"], c["dvnew"]) for c in chains]
        dkbgs = [_dot_tn(c["tm"], c["dw"]) for c in chains]
        tdts = [_dot3_nt(dt, c["tm"]) for dt, c in zip(dts, chains)]
        dls = [jnp.where(c["masks"][3], -_dot3_tn(c["tm"], tdt), 0.0) for tdt, c in zip(tdts, chains)]
        das = [dl * c["decay"] for dl, c in zip(dls, chains)]
        dqks = [jnp.where(c["masks"][2], di, 0.0) * c["decay"] for di, c in zip(dintras, chains)]
        dkb1 = [_dot(da, c["k"]) for da, c in zip(das, chains)]
        dk1 = [_dot_tn(da, c["kb"]) for da, c in zip(das, chains)]
        dk2 = [_dot_tn(dqk, c["q"]) for dqk, c in zip(dqks, chains)]
        dq1 = [_dot(dqk, c["k"]) for dqk, c in zip(dqks, chains)]
        grads = []
        for n, c in enumerate(chains):
            ch = c["ch"]
            ii, jj, incl, strict = c["masks"]
            k, v, bcol = c["k"], c["v"], c["bcol"]
            decay, eg, ek, kbg = c["decay"], c["eg"], c["ek"], c["kbg"]
            dqg, dkdec, dglast = c["dqg"], c["dkdec"], c["dglast"]
            dvb, dkbg, dl = dvbs[n], dkbgs[n], dls[n]
            dintra = jnp.where(incl, dintras[n], 0.0)
            mm = (dl * c["amat"] + dintra * c["qk"]) * decay
            dkb = dkb1[n] + dkbg * eg
            dk = dk1[n] + dk2[n] + dkdec * ek + dkb * bcol
            dq = dq1[n] + dqg * eg
            dv = dvb * bcol
            dbeta = jnp.sum(dkb * k, axis=1, keepdims=True) + jnp.sum(dvb * v, axis=1, keepdims=True)
            kd2 = jnp.sum(dkdec * c["kdec"], axis=1, keepdims=True)
            dgc = (jnp.sum(mm, axis=1, keepdims=True) - _row_to_col(jnp.sum(mm, axis=0, keepdims=True), ii, jj)
                   + jnp.sum(dqg * c["qg"], axis=1, keepdims=True) - kd2
                   + jnp.sum(dkbg * kbg, axis=1, keepdims=True))
            dgl = dglast + jnp.sum(kd2, axis=0, keepdims=True)
            draw = jnp.sum(jnp.where(jnp.logical_not(strict), _col_to_row(dgc, ii, jj), 0.0),
                           axis=1, keepdims=True) + dgl
            dgates = dgates + jnp.where(lane == ch, draw, 0.0) + jnp.where(lane == 8 + ch, dbeta, 0.0)
            grads.append((dq, dk, dv))
        for h in range(GDN_HEADS):
            for part in range(3):
                cols = slice(part * hd + h * GDN_DIM, part * hd + (h + 1) * GDN_DIM)
                dx_ref[:, cols] = grads[h][part] + grads[GDN_HEADS + h][part]
        dg_ref[...] = dgates

    all8 = lambda rows, cols: pl.BlockSpec((1, N_CHAINS, rows, cols), lambda n: (n, 0, 0, 0))
    own4 = lambda rows, cols: pl.BlockSpec((1, GDN_HEADS, rows, cols), lambda n: (n, 0, 0, 0))
    row4 = pl.BlockSpec((1, GDN_HEADS, LANES), lambda n: (n, 0, 0))
    vn_f, vn_b = saved["vn"]
    dvn_f, dvn_b, dw_f, dw_b, dqg_f, dqg_b, dkd_f, dkd_b, dgl_f, dgl_b = scan
    return pl.pallas_call(
        body, name="gdn_local_bwd", grid=(nc,),
        in_specs=[pl.BlockSpec((CHUNK, 3 * hd), lambda n: (n, 0)), pl.BlockSpec((CHUNK, LANES), lambda n: (n, 0)),
                  pl.BlockSpec((1, 16, CHUNK), lambda n: (n, 0, 0)), pl.BlockSpec((CHUNK, hd), lambda n: (n, 0)),
                  all8(CHUNK, CHUNK)] + [own4(CHUNK, GDN_DIM)] * 10 + [row4, row4],
        out_specs=[pl.BlockSpec((CHUNK, 3 * hd), lambda n: (n, 0)), pl.BlockSpec((CHUNK, LANES), lambda n: (n, 0))],
        out_shape=[jax.ShapeDtypeStruct((t, 3 * hd), F32), jax.ShapeDtypeStruct((t, LANES), F32)],
        compiler_params=_params(("arbitrary",), VMEM_LIMIT),
    )(qkvc, gb, gbt, do, saved["tm"], vn_f, vn_b, dvn_f, dvn_b, dw_f, dw_b, dqg_f, dqg_b, dkd_f, dkd_b, dgl_f, dgl_b)


def _gdn_post_fwd(of, ob, z, gw, tm):
    t, hd = of.shape

    def body(of_ref, ob_ref, z_ref, w_ref, o_ref):
        for h in range(GDN_HEADS):
            cols = slice(h * GDN_DIM, (h + 1) * GDN_DIM)
            o = of_ref[:, cols] + ob_ref[:, cols]
            zv = z_ref[:, cols]
            o_ref[:, cols] = (o * _rstd(o) * w_ref[...] * (zv * _sigmoid(zv))).astype(BF16)

    row = pl.BlockSpec((tm, hd), lambda i: (i, 0))
    return pl.pallas_call(
        body, name="gdn_post_fwd", grid=(t // tm,),
        in_specs=[row, row, row, _resident((1, GDN_DIM))],
        out_specs=row, out_shape=jax.ShapeDtypeStruct((t, hd), BF16),
        compiler_params=_params(("arbitrary",), VMEM_LIMIT),
    )(of, ob, z, gw)


def _gdn_post_bwd(doa, of, ob, z, gw, tm):
    t, hd = of.shape

    def body(d_ref, of_ref, ob_ref, z_ref, w_ref, do_ref, dz_ref, dw_ref):
        @pl.when(pl.program_id(0) == 0)
        def _():
            dw_ref[...] = jnp.zeros_like(dw_ref)

        dw = jnp.zeros((1, GDN_DIM), F32)
        for h in range(GDN_HEADS):
            cols = slice(h * GDN_DIM, (h + 1) * GDN_DIM)
            o = of_ref[:, cols] + ob_ref[:, cols]
            zv = z_ref[:, cols]
            dv = d_ref[:, cols]
            r = _rstd(o)
            sg = _sigmoid(zv)
            on = o * r * w_ref[...]
            dz_ref[:, cols] = (dv * on * (sg * (1.0 + zv * (1.0 - sg)))).astype(BF16)
            dxr, dwh = _rms_bwd(o, r, w_ref[...], dv * (zv * sg))
            do_ref[:, cols] = dxr
            dw = dw + dwh
        dw_ref[...] += dw

    row = pl.BlockSpec((tm, hd), lambda i: (i, 0))
    return pl.pallas_call(
        body, name="gdn_post_bwd", grid=(t // tm,),
        in_specs=[row, row, row, row, _resident((1, GDN_DIM))],
        out_specs=[row, row, pl.BlockSpec((1, GDN_DIM), lambda i: (0, 0))],
        out_shape=[jax.ShapeDtypeStruct((t, hd), F32), jax.ShapeDtypeStruct((t, hd), BF16),
                   jax.ShapeDtypeStruct((1, GDN_DIM), F32)],
        compiler_params=_params(("arbitrary",), VMEM_LIMIT),
    )(doa, of, ob, z, gw)


SWA_W = SWA_HEADS * SWA_DIM
QBLK = 128
KWIN = QBLK + 2 * RADIUS
WIN_OFFSETS = (0, RADIUS, 2 * RADIUS)


def _t5_bucket(rel):
    nb = REL_BUCKETS // 2
    bucket = (rel > 0).astype(np.int32) * nb
    n = np.abs(rel)
    max_exact = nb // 2
    large = max_exact + (np.log(np.maximum(n, 1) / max_exact)
                         / math.log(REL_MAX_DISTANCE / max_exact) * (nb - max_exact)).astype(np.int32)
    large = np.minimum(large, nb - 1)
    return (bucket + np.where(n < max_exact, n, large)).astype(np.int32)


def _band_tables(dilation):
    a = np.arange(QBLK)
    b = np.arange(KWIN)
    rel_a = np.stack([b[None, :] - w0 - a[:, None] for w0 in WIN_OFFSETS])
    rel_b = np.stack([a[None, :] + w0 - b[:, None] for w0 in WIN_OFFSETS])
    return ((_t5_bucket(rel_a * dilation), np.abs(rel_a) <= RADIUS),
            (_t5_bucket(rel_b * dilation), np.abs(rel_b) <= RADIUS))


def _bias_table(rel_bias, idx, valid):
    onehot = (jnp.arange(REL_BUCKETS, dtype=jnp.int32)[:, None] == jnp.asarray(idx.reshape(1, -1))).astype(F32)
    tab = jnp.dot(rel_bias.T, onehot, precision=HIGHEST)
    tab = jnp.where(jnp.asarray(valid.reshape(1, -1)), tab, NEG_BIG)
    tab = tab.reshape((SWA_HEADS,) + idx.shape)
    return jnp.transpose(tab, (1, 0, 2, 3)), onehot


def _head_mean(x2, bd_ref):
    return _dot_hi(x2, bd_ref[...])


VIEW_DILATIONS = tuple(d for _, d in PATTERNS if d > 1)


def _view_spec(tm, d):
    return pl.BlockSpec((tm // d, d * SWA_W), lambda i: (i, 0))


def _view_shape(t, d, dtype):
    return jax.ShapeDtypeStruct((t // d, d * SWA_W), dtype)


N_GROUPS = SWA_W // LANES


def _to_view(src_ref, idx, dst_ref, d, rows):
    for r in range(d):
        for g in range(N_GROUPS):
            cols = slice(r * SWA_W + g * LANES, r * SWA_W + (g + 1) * LANES)
            dst_ref[:, cols] = src_ref[idx, g, pl.ds(r, rows // d, stride=d), :].astype(dst_ref.dtype)


def _from_view(src_ref, dst_ref, idx, d, rows):
    for r in range(d):
        for g in range(N_GROUPS):
            cols = slice(r * SWA_W + g * LANES, r * SWA_W + (g + 1) * LANES)
            dst_ref[idx, g, pl.ds(r, rows // d, stride=d), :] = src_ref[:, cols]


def _swa_prep_fwd(qkvb, qw, kw, bd, tm):
    t = qkvb.shape[0]

    def body(x_ref, qw_ref, kw_ref, bd_ref, *rest):
        outs, sc = rest[:-1], rest[-1]
        for gidx in range(N_GROUPS):
            cols = slice(gidx * LANES, (gidx + 1) * LANES)
            xq = x_ref[:, cols]
            sc[0, gidx] = xq * lax.rsqrt(_head_mean(xq * xq, bd_ref) + EPS) * qw_ref[:, cols] * (SWA_DIM ** -0.5)
            xk = x_ref[:, SWA_W + gidx * LANES:SWA_W + (gidx + 1) * LANES]
            sc[1, gidx] = xk * lax.rsqrt(_head_mean(xk * xk, bd_ref) + EPS) * kw_ref[:, cols]
            sc[2, gidx] = x_ref[:, 2 * SWA_W + gidx * LANES:2 * SWA_W + (gidx + 1) * LANES]
            for i in range(3):
                outs[i][:, cols] = sc[i, gidx].astype(BF16)
        for i in range(3):
            for n, d in enumerate(VIEW_DILATIONS):
                _to_view(sc, i, outs[3 * (n + 1) + i], d, tm)

    return pl.pallas_call(
        body, name="swa_prep_fwd", grid=(t // tm,),
        in_specs=[pl.BlockSpec((tm, 3 * SWA_W), lambda i: (i, 0)), _resident((1, SWA_W)), _resident((1, SWA_W)),
                  _resident((LANES, LANES))],
        out_specs=[_view_spec(tm, d) for d in (1,) + VIEW_DILATIONS for _ in range(3)],
        out_shape=[_view_shape(t, d, BF16) for d in (1,) + VIEW_DILATIONS for _ in range(3)],
        scratch_shapes=[pltpu.VMEM((3, N_GROUPS, tm, LANES), F32)],
        compiler_params=_params(("arbitrary",), VMEM_LIMIT),
    )(qkvb, qw, kw, bd)


def _swa_prep_bwd(qkvb, qw, kw, bd, grads, tm):
    t = qkvb.shape[0]

    def body(x_ref, qw_ref, kw_ref, bd_ref, *rest):
        parts, (dx_ref, dqw_ref, dkw_ref, sc) = rest[:9], rest[9:]
        @pl.when(pl.program_id(0) == 0)
        def _():
            dqw_ref[...] = jnp.zeros_like(dqw_ref)
            dkw_ref[...] = jnp.zeros_like(dkw_ref)

        for i in range(3):
            for n, d in enumerate(VIEW_DILATIONS):
                _from_view(parts[3 * (n + 1) + i], sc, 2 * i + n, d, tm)
        for gidx in range(N_GROUPS):
            cols = slice(gidx * LANES, (gidx + 1) * LANES)
            for i, base, w_ref, dw_ref, scale in ((0, 0, qw_ref, dqw_ref, SWA_DIM ** -0.5),
                                                  (1, SWA_W, kw_ref, dkw_ref, 1.0)):
                xv = x_ref[:, base + gidx * LANES:base + (gidx + 1) * LANES]
                dy = (parts[i][:, cols] + sc[2 * i, gidx] + sc[2 * i + 1, gidx]) * scale
                r = lax.rsqrt(_head_mean(xv * xv, bd_ref) + EPS)
                xhat = xv * r
                dxh = dy * w_ref[:, cols]
                dx = r * (dxh - xhat * _head_mean(dxh * xhat, bd_ref))
                dx_ref[:, base + gidx * LANES:base + (gidx + 1) * LANES] = dx.astype(BF16)
                dw_ref[:, cols] += jnp.sum(dy * xhat, axis=0, keepdims=True)
            dx_ref[:, 2 * SWA_W + gidx * LANES:2 * SWA_W + (gidx + 1) * LANES] = (
                parts[2][:, cols] + sc[4, gidx] + sc[5, gidx]).astype(BF16)

    wrow = pl.BlockSpec((1, SWA_W), lambda i: (0, 0))
    return pl.pallas_call(
        body, name="swa_prep_bwd", grid=(t // tm,),
        in_specs=[pl.BlockSpec((tm, 3 * SWA_W), lambda i: (i, 0)), _resident((1, SWA_W)), _resident((1, SWA_W)),
                  _resident((LANES, LANES))] + [_view_spec(tm, d) for d in (1,) + VIEW_DILATIONS for _ in range(3)],
        out_specs=[pl.BlockSpec((tm, 3 * SWA_W), lambda i: (i, 0)), wrow, wrow],
        out_shape=[jax.ShapeDtypeStruct((t, 3 * SWA_W), BF16), jax.ShapeDtypeStruct((1, SWA_W), F32),
                   jax.ShapeDtypeStruct((1, SWA_W), F32)],
        scratch_shapes=[pltpu.VMEM((6, N_GROUPS, tm, LANES), F32)],
        compiler_params=_params(("arbitrary",), VMEM_LIMIT),
    )(qkvb, qw, kw, bd, *grads)


def _aligned(v, m):
    return v if isinstance(v, int) else pl.multiple_of(v, m)


BAND_GROUP = 2


def _band_loop(nsub, length, step):
    step([(0, 0)], 0)
    if nsub > 2:
        assert (nsub - 2) % BAND_GROUP == 0

        def inner(i, carry):
            s0 = 1 + i * BAND_GROUP
            step([(s0 + e, pl.multiple_of((s0 + e) * QBLK - RADIUS, RADIUS)) for e in range(BAND_GROUP)], 1)
            return carry
        lax.fori_loop(0, (nsub - 2) // BAND_GROUP, inner, 0)
    step([(nsub - 1, length - KWIN)], 2)


def _head_select(lane, a0, a1):
    return jnp.where(lane < SWA_DIM, a0, a1)


def _swa_fwd(qv, kv, vv, bias, dilation, name):
    length = qv.shape[0]
    nsub = length // QBLK
    assert nsub >= 2 and length % QBLK == 0

    def body(q_ref, k_ref, v_ref, b_ref, o_ref, l_ref):
        lane = lax.broadcasted_iota(jnp.int32, (QBLK, LANES), 1)

        def step(blocks, var):
            items = []
            for s, ws in blocks:
                rows = pl.ds(_aligned(s * QBLK, QBLK), QBLK)
                q, kk, vw = q_ref[rows, :], k_ref[pl.ds(ws, KWIN), :], v_ref[pl.ds(ws, KWIN), :]
                for hh in range(2):
                    items.append((hh, jnp.where((lane < SWA_DIM) == (hh == 0), q, jnp.zeros_like(q)), kk, vw))
            lgs = [_dot_nt(qh, kk) + b_ref[var, hh] for hh, qh, kk, _ in items]
            ms = [jnp.max(lg, axis=-1, keepdims=True) for lg in lgs]
            ps = [jnp.exp(lg - m) for lg, m in zip(lgs, ms)]
            dens = [jnp.sum(p, axis=-1, keepdims=True) for p in ps]
            pvs = [_dot(p, it[3]) for p, it in zip(ps, items)]
            for n, (s, _) in enumerate(blocks):
                rows = pl.ds(_aligned(s * QBLK, QBLK), QBLK)
                o0, o1 = (pvs[2 * n + hh] / dens[2 * n + hh] for hh in range(2))
                l0, l1 = (ms[2 * n + hh] + jnp.log(dens[2 * n + hh]) for hh in range(2))
                o_ref[rows, :] = _head_select(lane, o0, o1)
                l_ref[rows, :] = _head_select(lane, l0, l1)

        _band_loop(nsub, length, step)

    blk = pl.BlockSpec((length, LANES), lambda hp, r: (0, r * (SWA_W // LANES) + hp))
    shp = jax.ShapeDtypeStruct(qv.shape, F32)
    return pl.pallas_call(
        body, name=name, grid=(SWA_W // LANES, dilation),
        in_specs=[blk, blk, blk, pl.BlockSpec((3, 2, QBLK, KWIN), lambda hp, r: (0, hp, 0, 0))],
        out_specs=[blk, blk], out_shape=[shp, shp],
        compiler_params=_params(("arbitrary", "arbitrary"), VMEM_LIMIT),
    )(qv, kv, vv, bias)


def _swa_combine(os_, ls_, tm):
    t = os_[0].shape[0]

    def body(o0, o1, o2, l0, l1, l2, o_ref, ob_ref, la_ref, lb_ref, lc_ref, sc):
        for n, d in enumerate(VIEW_DILATIONS):
            _from_view((o1, o2)[n], sc, n, d, tm)
            _from_view((l1, l2)[n], sc, 2 + n, d, tm)
        for g in range(N_GROUPS):
            cols = slice(g * LANES, (g + 1) * LANES)
            la, lb, lc = l0[:, cols], sc[2, g], sc[3, g]
            m = jnp.maximum(jnp.maximum(la, lb), lc)
            tot = m + jnp.log(jnp.exp(la - m) + jnp.exp(lb - m) + jnp.exp(lc - m))
            o = jnp.exp(la - tot) * o0[:, cols] + jnp.exp(lb - tot) * sc[0, g] + jnp.exp(lc - tot) * sc[1, g]
            o_ref[:, cols] = o
            ob_ref[:, cols] = o.astype(BF16)
            la_ref[:, cols] = tot
            sc[4, g] = tot
        for n, d in enumerate(VIEW_DILATIONS):
            _to_view(sc, 4, (lb_ref, lc_ref)[n], d, tm)

    specs = [_view_spec(tm, d) for d in (1,) + VIEW_DILATIONS]
    return pl.pallas_call(
        body, name="swa_combine", grid=(t // tm,), in_specs=specs + specs, out_specs=[specs[0], specs[0]] + specs,
        out_shape=[jax.ShapeDtypeStruct((t, SWA_W), F32), jax.ShapeDtypeStruct((t, SWA_W), BF16)]
                  + [_view_shape(t, d, F32) for d in (1,) + VIEW_DILATIONS],
        scratch_shapes=[pltpu.VMEM((5, N_GROUPS, tm, LANES), F32)],
        compiler_params=_params(("arbitrary",), VMEM_LIMIT),
    )(*os_, *ls_)


def _swa_bwd_prep(do, o, bd, tm):
    t = do.shape[0]

    def body(d_ref, o_ref, bd_ref, dd1, dd4, dd16, db1, db4, db16, sc):
        for gidx in range(N_GROUPS):
            cols = slice(gidx * LANES, (gidx + 1) * LANES)
            dv = d_ref[:, cols]
            dd = _head_mean(dv * o_ref[:, cols], bd_ref) * float(SWA_DIM)
            sc[0, gidx] = dd
            sc[1, gidx] = dv
            dd1[:, cols] = dd
            db1[:, cols] = dv.astype(BF16)
        for n, d in enumerate(VIEW_DILATIONS):
            _to_view(sc, 0, (dd4, dd16)[n], d, tm)
            _to_view(sc, 1, (db4, db16)[n], d, tm)

    specs = [_view_spec(tm, d) for d in (1,) + VIEW_DILATIONS]
    return pl.pallas_call(
        body, name="swa_bwd_prep", grid=(t // tm,), in_specs=[specs[0], specs[0], _resident((LANES, LANES))],
        out_specs=specs + specs,
        out_shape=[_view_shape(t, d, F32) for d in (1,) + VIEW_DILATIONS]
                  + [_view_shape(t, d, BF16) for d in (1,) + VIEW_DILATIONS],
        scratch_shapes=[pltpu.VMEM((2, N_GROUPS, tm, LANES), F32)],
        compiler_params=_params(("arbitrary",), VMEM_LIMIT),
    )(do, o, bd)


def _swa_bwd(qv, kv, vv, dov, lv, ddv, bias_a, bias_b, dilation, name):
    length = qv.shape[0]
    nsub = length // QBLK
    single = pl.Buffered(1) if dilation == 1 else None

    def body(q_ref, k_ref, v_ref, do_ref, l_ref, dd_ref, ba_ref, bb_ref, dq_ref, dk_ref, dv_ref, db_ref):
        @pl.when(pl.program_id(1) == 0)
        def _():
            db_ref[...] = jnp.zeros_like(db_ref)

        lane = lax.broadcasted_iota(jnp.int32, (QBLK, LANES), 1)
        lanew = lax.broadcasted_iota(jnp.int32, (KWIN, LANES), 1)

        def step_q(blocks, var):
            items = []
            for s, ws in blocks:
                rows = pl.ds(_aligned(s * QBLK, QBLK), QBLK)
                win = pl.ds(ws, KWIN)
                q, dov_ = q_ref[rows, :], do_ref[rows, :]
                kk, vw = k_ref[win, :], v_ref[win, :]
                lse, dd = l_ref[rows, :], dd_ref[rows, :]
                for hh in range(2):
                    mine = (lane < SWA_DIM) == (hh == 0)
                    col = slice(hh * SWA_DIM, hh * SWA_DIM + 1)
                    items.append((hh, jnp.where(mine, q, jnp.zeros_like(q)), jnp.where(mine, dov_, jnp.zeros_like(dov_)),
                                  kk, vw, lse[:, col], dd[:, col]))
            lgs = [_dot_nt(qh, kk) + ba_ref[var, hh] for hh, qh, _, kk, _, _, _ in items]
            dps = [_dot_nt(doh, vw) for _, _, doh, _, vw, _, _ in items]
            dss = [jnp.exp(lg - it[5]) * (dp - it[6]) for lg, dp, it in zip(lgs, dps, items)]
            dqs = [_dot(ds, it[3]) for ds, it in zip(dss, items)]
            for n, (s, _) in enumerate(blocks):
                rows = pl.ds(_aligned(s * QBLK, QBLK), QBLK)
                dq_ref[rows, :] = _head_select(lane, dqs[2 * n], dqs[2 * n + 1])
            for hh in range(2):
                tot = dss[hh]
                for n in range(1, len(blocks)):
                    tot = tot + dss[2 * n + hh]
                db_ref[var, hh] += tot

        def step_k(blocks, var):
            items = []
            for s, ws in blocks:
                rows = pl.ds(_aligned(s * QBLK, QBLK), QBLK)
                win = pl.ds(ws, KWIN)
                kk, vw = k_ref[rows, :], v_ref[rows, :]
                qw, dow = q_ref[win, :], do_ref[win, :]
                lse, dd = l_ref[win, :], dd_ref[win, :]
                for hh in range(2):
                    mine = (lanew < SWA_DIM) == (hh == 0)
                    col = slice(hh * SWA_DIM, hh * SWA_DIM + 1)
                    items.append((hh, jnp.where(mine, qw, jnp.zeros_like(qw)), jnp.where(mine, dow, jnp.zeros_like(dow)),
                                  kk, vw, lse[:, col], dd[:, col], qw, dow))
            lgs = [_dot_nt(it[1], it[3]) + bb_ref[var, it[0]] for it in items]
            dps = [_dot_nt(it[2], it[4]) for it in items]
            ps = [jnp.exp(lg - it[5]) for lg, it in zip(lgs, items)]
            dss = [p * (dp - it[6]) for p, dp, it in zip(ps, dps, items)]
            dks = [_dot_tn(ds, it[7]) for ds, it in zip(dss, items)]
            dvs = [_dot_tn(p, it[8]) for p, it in zip(ps, items)]
            for n, (s, _) in enumerate(blocks):
                rows = pl.ds(_aligned(s * QBLK, QBLK), QBLK)
                dk_ref[rows, :] = _head_select(lane, dks[2 * n], dks[2 * n + 1])
                dv_ref[rows, :] = _head_select(lane, dvs[2 * n], dvs[2 * n + 1])

        _band_loop(nsub, length, step_q)
        _band_loop(nsub, length, step_k)

    imap = lambda hp, r: (0, r * (SWA_W // LANES) + hp)
    blk_in = pl.BlockSpec((length, LANES), imap, pipeline_mode=single)
    blk_out = pl.BlockSpec((length, LANES), imap)
    shp = jax.ShapeDtypeStruct(qv.shape, F32)
    return pl.pallas_call(
        body, name=name, grid=(SWA_W // LANES, dilation),
        in_specs=[blk_in] * 6 + [pl.BlockSpec((3, 2, QBLK, KWIN), lambda hp, r: (0, hp, 0, 0)),
                                 pl.BlockSpec((3, 2, KWIN, QBLK), lambda hp, r: (0, hp, 0, 0))],
        out_specs=[blk_out, blk_out, blk_out, pl.BlockSpec((3, 2, QBLK, KWIN), lambda hp, r: (0, hp, 0, 0))],
        out_shape=[shp, shp, shp, jax.ShapeDtypeStruct((3, SWA_HEADS, QBLK, KWIN), F32)],
        compiler_params=_params(("arbitrary", "arbitrary"), VMEM_LIMIT),
    )(qv, kv, vv, dov, lv, ddv, bias_a, bias_b)


def _bias_grad(ds2, onehot, tk):
    n = ds2.shape[1]
    nk = n // tk

    def body(a_ref, b_ref, o_ref):
        @pl.when(pl.program_id(0) == 0)
        def _():
            o_ref[...] = jnp.zeros_like(o_ref)

        o_ref[...] += lax.dot_general(a_ref[...], b_ref[...], (((1,), (1,)), ((), ())), precision=HIGHEST,
                                      preferred_element_type=F32)

    return pl.pallas_call(
        body, name="bias_grad", grid=(nk,),
        in_specs=[pl.BlockSpec((SWA_HEADS, tk), lambda k: (0, k)), pl.BlockSpec((REL_BUCKETS, tk), lambda k: (0, k))],
        out_specs=pl.BlockSpec((SWA_HEADS, REL_BUCKETS), lambda k: (0, 0)),
        out_shape=jax.ShapeDtypeStruct((SWA_HEADS, REL_BUCKETS), F32),
        compiler_params=_params(("arbitrary",), VMEM_LIMIT),
    )(ds2, onehot)


def _swa_branch_fwd(qkvb, qw_t, kw_t, rel_bias, bd, tm):
    qkv = _swa_prep_fwd(qkvb, qw_t, kw_t, bd, tm)
    os_, ls_, tabs = [], [], []
    for n, (_, d) in enumerate(PATTERNS):
        (idx_a, val_a), (idx_b, val_b) = _band_tables(d)
        bias_a, onehot_a = _bias_table(rel_bias, idx_a, val_a)
        bias_b, _ = _bias_table(rel_bias, idx_b, val_b)
        o_p, l_p = _swa_fwd(*qkv[3 * n:3 * n + 3], bias_a, d, f"swa_fwd_d{d}")
        os_.append(o_p)
        ls_.append(l_p)
        tabs.append((bias_a, bias_b, onehot_a))
    o, o16, *lses = _swa_combine(os_, ls_, tm)
    return o, o16, (qkv, lses, tabs)


def _swa_branch_bwd(do, o, saved, qkvb, qw_t, kw_t, bd, tm):
    qkv, lses, tabs = saved
    prep = _swa_bwd_prep(do, o, bd, tm)
    grads, dss, ohs = [], [], []
    for n, ((_, d), (bias_a, bias_b, onehot_a)) in enumerate(zip(PATTERNS, tabs)):
        dq, dk, dv, ds = _swa_bwd(*qkv[3 * n:3 * n + 3], prep[3 + n], lses[n], prep[n],
                                  bias_a, bias_b, d, f"swa_bwd_d{d}")
        grads += [dq, dk, dv]
        dss.append(jnp.transpose(ds, (1, 0, 2, 3)).reshape(SWA_HEADS, -1))
        ohs.append(onehot_a)
    dqkvb, dqw, dkw = _swa_prep_bwd(qkvb, qw_t, kw_t, bd, grads, tm)
    dbias = _bias_grad(jnp.concatenate(dss, axis=1), jnp.concatenate(ohs, axis=1), 8192)
    fold = lambda w: jnp.sum(w.reshape(SWA_HEADS, SWA_DIM), axis=0)
    return dqkvb, fold(dqw), fold(dkw), dbias.T


ANY = pl.BlockSpec(memory_space=pl.ANY)


def _mesh_pos():
    return lax.axis_index("x"), lax.axis_index("y"), lax.axis_index("c")


def _other_chips(x, y):
    return [(1 - x, y), (x, 1 - y), (1 - x, 1 - y)]


def _remote(src, dst, send_sem, recv_sem, device):
    return pltpu.make_async_remote_copy(src_ref=src, dst_ref=dst, send_sem=send_sem, recv_sem=recv_sem,
                                        device_id=device, device_id_type=MESH)


def _all_gather(xs):
    n = len(xs)

    def body(*refs):
        ins, outs = refs[:n], refs[n:2 * n]
        send_sems, recv_sems = refs[2 * n:]
        x, y, c = _mesh_pos()
        me = 2 * x + y
        chips = _other_chips(x, y)
        halves = []
        sends = []
        for a in range(n):
            h = ins[a].shape[0] // 2
            mine, other = pl.ds(c * h, h), pl.ds((1 - c) * h, h)
            halves.append((mine, other))
            for j, chip in enumerate(chips):
                cp = _remote(ins[a].at[mine], outs[a].at[me, mine], send_sems.at[a, j], recv_sems.at[a, j], (*chip, c))
                cp.start()
                sends.append(cp)
        for a in range(n):
            mine, _ = halves[a]
            for j, chip in enumerate(chips):
                src = 2 * chip[0] + chip[1]
                landed = outs[a].at[src, mine]
                _remote(landed, landed, send_sems.at[a, j], recv_sems.at[a, j], (x, y, c)).wait_recv()
                fwd = _remote(landed, landed, send_sems.at[a, 3 + j], recv_sems.at[a, 3 + j], (x, y, 1 - c))
                fwd.start()
                sends.append(fwd)
        for a in range(n):
            _, other = halves[a]
            for j, chip in enumerate(chips):
                src = 2 * chip[0] + chip[1]
                landed = outs[a].at[src, other]
                _remote(landed, landed, send_sems.at[a, 3 + j], recv_sems.at[a, 3 + j], (x, y, c)).wait_recv()
        for cp in sends:
            cp.wait_send()

    outs = pl.pallas_call(
        body, name="all_gather_weights",
        in_specs=[ANY] * n, out_specs=[ANY] * n,
        out_shape=[jax.ShapeDtypeStruct((N_SHARDS,) + a.shape, a.dtype) for a in xs],
        scratch_shapes=[pltpu.SemaphoreType.DMA((n, 6)), pltpu.SemaphoreType.DMA((n, 6))],
    )(*xs)
    me = 2 * lax.axis_index("x") + lax.axis_index("y")
    return [lax.dynamic_update_slice_in_dim(o, a[None], me, 0) for o, a in zip(outs, xs)]


def _rs_pair(gs):
    n = len(gs)

    def body(*refs):
        ins, lands = refs[:n], refs[n:2 * n]
        send_sems, recv_sems = refs[2 * n:]
        x, y, c = _mesh_pos()
        cps = []
        for a in range(n):
            h = ins[a].shape[1] // 2
            cp = _remote(ins[a].at[:, pl.ds((1 - c) * h, h), :], lands[a], send_sems.at[a], recv_sems.at[a],
                         (x, y, 1 - c))
            cp.start()
            cps.append(cp)
        for cp in cps:
            cp.wait()

    half = [jax.ShapeDtypeStruct((N_SHARDS, g.shape[1] // 2, g.shape[2]), g.dtype) for g in gs]
    lands = pl.pallas_call(
        body, name="rs_pair", in_specs=[ANY] * n, out_specs=[ANY] * n, out_shape=half,
        scratch_shapes=[pltpu.SemaphoreType.DMA((n,)), pltpu.SemaphoreType.DMA((n,))],
    )(*gs)
    c = lax.axis_index("c")
    owns = [lax.dynamic_slice_in_dim(g, c * (g.shape[1] // 2), g.shape[1] // 2, 1) for g in gs]
    return owns + list(lands)


def _rs_chips(ss):
    n = len(ss)

    def body(*refs):
        ins, outs = refs[:n], refs[n:2 * n]
        send_sems, recv_sems = refs[2 * n:]
        x, y, c = _mesh_pos()
        me = 2 * x + y
        chips = _other_chips(x, y)
        cps = []
        for a in range(n):
            for j, chip in enumerate(chips):
                dst_chip = 2 * chip[0] + chip[1]
                cp = _remote(ins[a].at[dst_chip], outs[a].at[me], send_sems.at[a, j], recv_sems.at[a, j], (*chip, c))
                cp.start()
                cps.append(cp)
        for a in range(n):
            for j, chip in enumerate(chips):
                src = 2 * chip[0] + chip[1]
                _remote(outs[a].at[src], outs[a].at[src], send_sems.at[a, j], recv_sems.at[a, j], (x, y, c)).wait_recv()
        for cp in cps:
            cp.wait_send()

    outs = pl.pallas_call(
        body, name="rs_chips", in_specs=[ANY] * n, out_specs=[ANY] * n,
        out_shape=[jax.ShapeDtypeStruct(s.shape, s.dtype) for s in ss],
        scratch_shapes=[pltpu.SemaphoreType.DMA((n, 3)), pltpu.SemaphoreType.DMA((n, 3))],
    )(*ss)
    me = 2 * lax.axis_index("x") + lax.axis_index("y")
    return [lax.dynamic_update_slice_in_dim(o, lax.dynamic_slice_in_dim(s, me, 1, 0), me, 0) for o, s in zip(outs, ss)]


def _rs_join(fs):
    n = len(fs)

    def body(*refs):
        ins, outs = refs[:n], refs[n:2 * n]
        send_sems, recv_sems = refs[2 * n:]
        x, y, c = _mesh_pos()
        cps = []
        for a in range(n):
            h = ins[a].shape[0]
            cp = _remote(ins[a], outs[a].at[pl.ds(c * h, h)], send_sems.at[a], recv_sems.at[a], (x, y, 1 - c))
            cp.start()
            cps.append(cp)
        for cp in cps:
            cp.wait()

    outs = pl.pallas_call(
        body, name="rs_join", in_specs=[ANY] * n, out_specs=[ANY] * n,
        out_shape=[jax.ShapeDtypeStruct((2 * f.shape[0], f.shape[1]), f.dtype) for f in fs],
        scratch_shapes=[pltpu.SemaphoreType.DMA((n,)), pltpu.SemaphoreType.DMA((n,))],
    )(*fs)
    c = lax.axis_index("c")
    return [lax.dynamic_update_slice_in_dim(o, f, c * f.shape[0], 0) for o, f in zip(outs, fs)]


def _add_pair(a, b, name):
    nj, h, c = a.shape

    def body(a_ref, b_ref, o_ref):
        o_ref[...] = (a_ref[...].astype(F32) + b_ref[...].astype(F32)).astype(BF16)

    blk = pl.BlockSpec((1, h, c), lambda j: (j, 0, 0))
    return pl.pallas_call(body, name=name, grid=(nj,), in_specs=[blk, blk], out_specs=blk,
                          out_shape=jax.ShapeDtypeStruct(a.shape, BF16),
                          compiler_params=_params(("arbitrary",), VMEM_LIMIT))(a, b)


def _sum_slots(l2, name):
    nj, h, c = l2.shape
    th = h // 2 if h % 32 == 0 else h

    def body(i_ref, o_ref):
        acc = i_ref[0].astype(F32)
        for s in range(1, nj):
            acc = acc + i_ref[s].astype(F32)
        o_ref[...] = acc

    return pl.pallas_call(body, name=name, grid=(h // th,),
                          in_specs=[pl.BlockSpec((nj, th, c), lambda i: (0, i, 0))],
                          out_specs=pl.BlockSpec((th, c), lambda i: (i, 0)),
                          out_shape=jax.ShapeDtypeStruct((h, c), F32),
                          compiler_params=_params(("arbitrary",), VMEM_LIMIT))(l2)


def _all_reduce_small(p):
    r = p.shape[0]

    def body(p_ref, o_ref, buf, send_sems, recv_sems):
        x, y, c = _mesh_pos()
        me = 4 * x + 2 * y + c
        buf[me] = p_ref[...]
        cps = []
        k = 0
        for fx in range(2):
            for fy in range(2):
                for fc in range(2):
                    if fx + fy + fc == 0:
                        continue
                    peer = (1 - x if fx else x, 1 - y if fy else y, 1 - c if fc else c)
                    peer_id = 4 * peer[0] + 2 * peer[1] + peer[2]
                    cp = _remote(p_ref, buf.at[me], send_sems.at[k], recv_sems.at[k], peer)
                    cp.start()
                    cps.append((cp, peer_id, k))
                    k += 1
        for cp, peer_id, k in cps:
            _remote(p_ref, buf.at[peer_id], send_sems.at[k], recv_sems.at[k], (x, y, c)).wait_recv()
        for cp, _, _ in cps:
            cp.wait_send()
        acc = buf[0]
        for s in range(1, 8):
            acc = acc + buf[s]
        o_ref[...] = acc

    vm = pl.BlockSpec(memory_space=pltpu.VMEM)
    return pl.pallas_call(
        body, name="all_reduce_small", in_specs=[vm], out_specs=vm,
        out_shape=jax.ShapeDtypeStruct(p.shape, F32),
        scratch_shapes=[pltpu.VMEM((8, r, LANES), F32), pltpu.SemaphoreType.DMA((7,)), pltpu.SemaphoreType.DMA((7,))],
    )(p)


def _adamw(w, g, m, v, name):
    r, c = w.shape
    tr = max(d for d in range(8, min(r, 256) + 1, 8) if r % d == 0)
    c1 = 1.0 / (1.0 - ADAM_B1 ** ADAM_STEP)
    c2 = 1.0 / (1.0 - ADAM_B2 ** ADAM_STEP)

    def body(w_ref, g_ref, m_ref, v_ref, d_ref, nm_ref, nv_ref):
        gv = g_ref[...]
        nm = ADAM_B1 * m_ref[...] + (1.0 - ADAM_B1) * gv
        nv = ADAM_B2 * v_ref[...] + (1.0 - ADAM_B2) * (gv * gv)
        d_ref[...] = -ADAM_LR * ((nm * c1) / (jnp.sqrt(nv * c2) + ADAM_EPS) + ADAM_WD * w_ref[...])
        nm_ref[...] = nm
        nv_ref[...] = nv

    blk = pl.BlockSpec((tr, c), lambda i: (i, 0))
    shp = jax.ShapeDtypeStruct((r, c), F32)
    return pl.pallas_call(body, name=name, grid=(r // tr,), in_specs=[blk] * 4, out_specs=[blk] * 3,
                          out_shape=[shp, shp, shp], compiler_params=_params(("arbitrary",), VMEM_LIMIT))(w, g, m, v)


PACK_UNIT = 8 * LANES


def _pack(arrs):
    parts = []
    for a in arrs:
        f = a.reshape(-1).astype(F32)
        parts.append(jnp.pad(f, (0, (-f.shape[0]) % PACK_UNIT)).reshape(-1, LANES))
    return jnp.concatenate(parts, axis=0)


def _unpack(m, shapes):
    outs, row = [], 0
    for s in shapes:
        n = int(np.prod(s))
        rows = -(-n // PACK_UNIT) * 8
        outs.append(m[row:row + rows].reshape(-1)[:n].reshape(s))
        row += rows
    return outs


WEIGHTS = ["ffn1_norm", "ffn1_w_gate", "ffn1_w_up", "ffn1_w_down", "mix_norm", "w_in", "conv_w", "a_log", "dt_bias",
           "gdn_norm_w", "q_norm_w", "k_norm_w", "rel_bias", "w_out", "ffn2_norm", "ffn2_w_gate", "ffn2_w_up",
           "ffn2_w_down", "final_norm"]
BIG = ["ffn1_w_gate", "ffn1_w_up", "ffn1_w_down", "w_in", "w_out", "ffn2_w_gate", "ffn2_w_up", "ffn2_w_down"]
SMALL = [n for n in WEIGHTS if n not in BIG]
N_IN_COLS = 3600
TM = 256
TE = 512
TK = 2048


def kernel(x, ffn1_norm, ffn1_w_gate, ffn1_w_up, ffn1_w_down, mix_norm, w_in, conv_w, a_log, dt_bias, gdn_norm_w, q_norm_w, k_norm_w, rel_bias, w_out, ffn2_norm, ffn2_w_gate, ffn2_w_up, ffn2_w_down, final_norm, loss_target, m_ffn1_norm, m_ffn1_w_gate, m_ffn1_w_up, m_ffn1_w_down, m_mix_norm, m_w_in, m_conv_w, m_a_log, m_dt_bias, m_gdn_norm_w, m_q_norm_w, m_k_norm_w, m_rel_bias, m_w_out, m_ffn2_norm, m_ffn2_w_gate, m_ffn2_w_up, m_ffn2_w_down, m_final_norm, v_ffn1_norm, v_ffn1_w_gate, v_ffn1_w_up, v_ffn1_w_down, v_mix_norm, v_w_in, v_conv_w, v_a_log, v_dt_bias, v_gdn_norm_w, v_q_norm_w, v_k_norm_w, v_rel_bias, v_w_out, v_ffn2_norm, v_ffn2_w_gate, v_ffn2_w_up, v_ffn2_w_down, v_final_norm):
    p = dict(locals())
    xs, target = x[0], loss_target[0]
    t, d = xs.shape
    nc = t // CHUNK
    tk = min(TK, t)
    me = 2 * lax.axis_index("x") + lax.axis_index("y")

    gathered = _all_gather([p[n][0].astype(BF16) for n in BIG] + [conv_w[0]])
    gw = dict(zip(BIG + ["conv_w"], gathered))
    w_in_full = jnp.transpose(gw["w_in"], (1, 0, 2)).reshape(d, N_IN_COLS)
    wp = jnp.concatenate([w_in_full[:, :2048], jnp.pad(w_in_full[:, 2048:2064], ((0, 0), (0, LANES - 16))),
                          w_in_full[:, 2064:]], axis=1)
    w_out_full = gw["w_out"].reshape(d, d)
    conv_rows = conv_w.shape[1]
    cw = jnp.pad(gw["conv_w"].reshape(N_SHARDS * conv_rows, CONV_TAPS).T, ((0, 8 - CONV_TAPS), (0, 0)))
    gp = jnp.pad(jnp.stack([a_log.reshape(8), dt_bias.reshape(8)]), ((0, 6), (0, LANES - 8)))
    gdn_w = gdn_norm_w.reshape(1, GDN_DIM)
    qw_t = jnp.tile(q_norm_w.reshape(1, SWA_DIM), (1, SWA_HEADS))
    kw_t = jnp.tile(k_norm_w.reshape(1, SWA_DIM), (1, SWA_HEADS))
    bd = jnp.asarray(np.kron(np.eye(2), np.full((SWA_DIM, SWA_DIM), 1.0 / SWA_DIM)), F32)
    f1 = (gw["ffn1_w_gate"], gw["ffn1_w_up"], gw["ffn1_w_down"])
    f2 = (gw["ffn2_w_gate"], gw["ffn2_w_up"], gw["ffn2_w_down"])

    x1, xn1, g1, u1 = _ffn_fwd(xs, ffn1_norm, *f1, TM, "ffn1_fwd")
    hn, qkva, z, ab, qkvb = _mix_in_fwd(x1, mix_norm, wp, TM)
    qkvc, gb = _gdn_prep_fwd(qkva, cw, ab, gp, TM)
    gbt = jnp.transpose(gb[:, :16].reshape(nc, CHUNK, 16), (0, 2, 1))
    o_f, o_b, gdn_saved = _gdn_fwd(qkvc, gb, gbt)
    oa = _gdn_post_fwd(o_f, o_b, z, gdn_w, TE)
    o_swa, o_swa16, swa_saved = _swa_branch_fwd(qkvb, qw_t, kw_t, rel_bias, bd, TE)
    x2 = _mix_out_fwd(x1, oa, o_swa, w_out_full, TM)
    x3, xn2, g2, u2 = _ffn_fwd(x2, ffn2_norm, *f2, TM, "ffn2_fwd")
    dx3, loss_part, d_final = _final_loss(x3, final_norm, target, TE)

    dx2, dyh2, dg2, du2, h2, d_nw2 = _ffn_bwd_dx(dx3, x2, ffn2_norm, g2, u2, *f2, TM, "ffn2_bwd_dx")
    dwg2 = _matmul_tn(xn2, dg2, tk, "ffn2_dwg")
    dwu2 = _matmul_tn(xn2, du2, tk, "ffn2_dwu")
    dwd2 = _matmul_tn(h2, dyh2, tk, "ffn2_dwd")
    doa, dob, dx2b = _mix_out_bwd(dx2, w_out_full, TM)
    dwo = jnp.concatenate([_matmul_tn(oa, dx2b, tk, "w_out_dw_a")[0], _matmul_tn(o_swa16, dx2b, tk, "w_out_dw_b")[0]],
                          axis=0).reshape(N_SHARDS, d // N_SHARDS, d)
    do_g, dz, d_gdnw = _gdn_post_bwd(doa, o_f, o_b, z, gdn_w, TE)
    dqkvc, dgates = _gdn_bwd(qkvc, gb, gbt, do_g, gdn_saved)
    dqkva, dab, dcw, dgp = _gdn_prep_bwd(qkva, cw, ab, gp, dqkvc, dgates, TM)
    dqkvb, d_qw, d_kw, d_rel = _swa_branch_bwd(dob, o_swa, swa_saved, qkvb, qw_t, kw_t, bd, TE)
    dpieces = (dqkva, dz, dab, dqkvb)
    dx1, d_mixnw = _mix_in_bwd_dx(dx2, x1, mix_norm, dpieces, wp, TM)
    dwp = [_matmul_tn(hn, dp, tk, f"w_in_dw_{i}")[0] for i, dp in enumerate(dpieces)]
    dw_in = jnp.concatenate([dwp[0], dwp[1], dwp[2][:, :16], dwp[3]], axis=1)
    dw_in = jnp.transpose(dw_in.reshape(d, N_SHARDS, N_IN_COLS // N_SHARDS), (1, 0, 2))
    gx, dyh1, dg1, du1, h1, d_nw1 = _ffn_bwd_dx(dx1, xs, ffn1_norm, g1, u1, *f1, TM, "ffn1_bwd_dx")
    dwg1 = _matmul_tn(xn1, dg1, tk, "ffn1_dwg")
    dwu1 = _matmul_tn(xn1, du1, tk, "ffn1_dwu")
    dwd1 = _matmul_tn(h1, dyh1, tk, "ffn1_dwd")

    partial = [dwg1, dwu1, dwd1, dw_in, dwo, dwg2, dwu2, dwd2]
    nb = len(partial)
    pair = _rs_pair(partial)
    chip_sums = [_add_pair(pair[i], pair[nb + i], f"rs_add_{i}") for i in range(nb)]
    slots = _rs_chips(chip_sums)
    halves = [_sum_slots(s, f"rs_sum_{i}") for i, s in enumerate(slots)]
    g_big = dict(zip(BIG, _rs_join(halves)))

    small_partial = {"ffn1_norm": d_nw1, "mix_norm": d_mixnw, "a_log": dgp[0, 0:8], "dt_bias": dgp[1, 0:8],
                     "gdn_norm_w": d_gdnw, "q_norm_w": d_qw, "k_norm_w": d_kw, "rel_bias": d_rel,
                     "ffn2_norm": d_nw2, "final_norm": d_final, "conv_w": dcw[0:CONV_TAPS].T}
    red = _all_reduce_small(_pack([small_partial[n] for n in SMALL] + [loss_part[0, 0:1]]))
    full_shapes = [p[n].shape if n != "conv_w" else (N_SHARDS * conv_rows, CONV_TAPS) for n in SMALL]
    red_parts = _unpack(red, full_shapes + [(1,)])
    loss = red_parts[-1].reshape(())
    g_small = dict(zip(SMALL, red_parts[:-1]))
    g_small["conv_w"] = lax.dynamic_slice_in_dim(g_small["conv_w"], me * conv_rows, conv_rows, 0).reshape(conv_w.shape)

    grads, deltas, new_m, new_v = {}, {}, {}, {}
    for n in BIG:
        grads[n] = g_big[n][None]
        dl, nm, nv = _adamw(p[n][0], g_big[n], p["m_" + n][0], p["v_" + n][0], "adamw_" + n)
        deltas[n], new_m[n], new_v[n] = dl[None], nm[None], nv[None]
    packed = [_pack([src[n] for n in SMALL]) for src in
              ({n: p[n] for n in SMALL}, g_small, {n: p["m_" + n] for n in SMALL}, {n: p["v_" + n] for n in SMALL})]
    small_shapes = [p[n].shape for n in SMALL]
    for dst, arr in zip((deltas, new_m, new_v), _adamw(*packed, "adamw_small")):
        dst.update(zip(SMALL, _unpack(arr, small_shapes)))
    grads.update(g_small)

    return (loss, gx[None], *[grads[n] for n in WEIGHTS], *[deltas[n] for n in WEIGHTS],
            *[new_m[n] for n in WEIGHTS], *[new_v[n] for n in WEIGHTS])
```

```python
import math
from typing import Callable, NamedTuple

import numpy as np
import jax
import jax.numpy as jnp
from jax import lax
from jax.experimental import pallas as pl
from jax.experimental.pallas import tpu as pltpu

F32 = jnp.float32
BF16 = jnp.bfloat16
HIGHEST = lax.Precision.HIGHEST
MESH = pl.DeviceIdType.MESH

EPS = 1e-6
NEG_BIG = -1e30
GDN_HEADS = 4
GDN_DIM = 128
CHUNK = 64
SWA_HEADS = 8
SWA_DIM = 64
PATTERNS = ((128, 1), (512, 4), (2048, 16))
RADIUS = 64
REL_BUCKETS = 32
REL_MAX_DISTANCE = 1024
CONV_TAPS = 5
N_SHARDS = 4
LANES = 128
VMEM_LIMIT = 56 * 1024 * 1024

ADAM_LR, ADAM_B1, ADAM_B2, ADAM_EPS, ADAM_WD, ADAM_STEP = 0.001, 0.9, 0.999, 1e-08, 0.01, 10


def _params(sem=None, vmem=None):
    return pltpu.CompilerParams(dimension_semantics=sem, vmem_limit_bytes=vmem)


def _resident(shape):
    nd = len(shape)
    return pl.BlockSpec(shape, lambda *_: (0,) * nd, pipeline_mode=pl.Buffered(1))


ANY = pl.BlockSpec(memory_space=pl.ANY)


class _Exchange(NamedTuple):
    arrays: tuple
    out_shape: tuple
    start: Callable
    finish: Callable


def _grid_call(body, name, nsteps, in_specs, out_specs, out_shape, operands, scratch=(), exchange=None):
    params = _params(("arbitrary",), VMEM_LIMIT)
    if exchange is None:
        res = pl.pallas_call(body, name=name, grid=(nsteps,), in_specs=list(in_specs), out_specs=list(out_specs),
                             out_shape=list(out_shape), scratch_shapes=list(scratch), compiler_params=params)(*operands)
        return list(res), []
    n_in, n_out, k, n_scr = len(in_specs), len(out_specs), len(exchange.arrays), len(scratch)

    def wrapped(*refs):
        ins, cin = refs[:n_in], refs[n_in:n_in + k]
        outs, cout = refs[n_in + k:n_in + k + n_out], refs[n_in + k + n_out:n_in + 2 * k + n_out]
        rest = refs[n_in + 2 * k + n_out:]
        scr, (send_sems, recv_sems) = rest[:n_scr], rest[n_scr:]

        @pl.when(pl.program_id(0) == 0)
        def _():
            exchange.start(cin, cout, send_sems, recv_sems)

        body(*ins, *outs, *scr)

        @pl.when(pl.program_id(0) == nsteps - 1)
        def _():
            exchange.finish(cin, cout, send_sems, recv_sems)

    res = pl.pallas_call(
        wrapped, name=name, grid=(nsteps,), in_specs=list(in_specs) + [ANY] * k, out_specs=list(out_specs) + [ANY] * k,
        out_shape=list(out_shape) + list(exchange.out_shape),
        scratch_shapes=list(scratch) + [pltpu.SemaphoreType.DMA((k, 3)), pltpu.SemaphoreType.DMA((k, 3))],
        compiler_params=params)(*operands, *exchange.arrays)
    return list(res[:n_out]), list(res[n_out:])


def _dot(a, b):
    return jnp.dot(a.astype(BF16), b.astype(BF16), preferred_element_type=F32)


def _dot_nt(a, b):
    return lax.dot_general(a.astype(BF16), b.astype(BF16), (((1,), (1,)), ((), ())), preferred_element_type=F32)


def _dot_tn(a, b):
    return lax.dot_general(a.astype(BF16), b.astype(BF16), (((0,), (0,)), ((), ())), preferred_element_type=F32)


def _dot_hi(a, b):
    return jnp.dot(a, b, preferred_element_type=F32, precision=HIGHEST)


def _sigmoid(x):
    return 1.0 / (1.0 + jnp.exp(-x))


def _rstd(xf):
    return lax.rsqrt(jnp.mean(xf * xf, axis=-1, keepdims=True) + EPS)


def _rms_bwd(xf, r, nw, dxn):
    xhat = xf * r
    dxh = dxn * nw
    dx = r * (dxh - xhat * jnp.mean(dxh * xhat, axis=-1, keepdims=True))
    return dx, jnp.sum(dxn * xhat, axis=0, keepdims=True)


def _ffn_fwd(x, nw, wg, wu, wd, tm, name, exchange=None):
    t, d = x.shape
    nj, _, fs = wg.shape

    def body(x_ref, nw_ref, wg_ref, wu_ref, wd_ref, y_ref, xn_ref, g_ref, u_ref):
        xf = x_ref[...]
        xn = (xf * _rstd(xf) * nw_ref[...]).astype(BF16)
        xn_ref[...] = xn
        acc = jnp.zeros((tm, d), F32)
        for j in range(nj):
            g = jnp.dot(xn, wg_ref[j], preferred_element_type=F32)
            u = jnp.dot(xn, wu_ref[j], preferred_element_type=F32)
            h = (g * _sigmoid(g) * u).astype(BF16)
            acc = acc + jnp.dot(h, wd_ref[j], preferred_element_type=F32)
            g_ref[j] = g.astype(BF16)
            u_ref[j] = u.astype(BF16)
        y_ref[...] = xf + 0.5 * acc

    row = pl.BlockSpec((tm, d), lambda i: (i, 0))
    act = pl.BlockSpec((nj, tm, fs), lambda i: (0, i, 0))
    return _grid_call(
        body, name, t // tm,
        [row, _resident((1, d)), _resident(wg.shape), _resident(wu.shape), _resident(wd.shape)],
        [row, row, act, act],
        [jax.ShapeDtypeStruct((t, d), F32), jax.ShapeDtypeStruct((t, d), BF16),
         jax.ShapeDtypeStruct((nj, t, fs), BF16), jax.ShapeDtypeStruct((nj, t, fs), BF16)],
        (x, nw, wg, wu, wd), exchange=exchange)


def _ffn_bwd_dx(dy, x, nw, g, u, wg, wu, wd, tm, name, exchange=None):
    t, d = x.shape
    nj, _, fs = wg.shape

    def body(dy_ref, x_ref, nw_ref, g_ref, u_ref, wg_ref, wu_ref, wd_ref,
             dx_ref, dyh_ref, dg_ref, du_ref, h_ref, dnw_ref):
        @pl.when(pl.program_id(0) == 0)
        def _():
            dnw_ref[...] = jnp.zeros_like(dnw_ref)

        dyv = dy_ref[...]
        dyh = (0.5 * dyv).astype(BF16)
        dyh_ref[...] = dyh
        dxn = jnp.zeros((tm, d), F32)
        for j in range(nj):
            gv = g_ref[j].astype(F32)
            uv = u_ref[j].astype(F32)
            dh = _dot_nt(dyh, wd_ref[j])
            sg = _sigmoid(gv)
            si = gv * sg
            dg = (dh * uv * (sg * (1.0 + gv * (1.0 - sg)))).astype(BF16)
            du = (dh * si).astype(BF16)
            h_ref[j] = (si * uv).astype(BF16)
            dg_ref[j] = dg
            du_ref[j] = du
            dxn = dxn + _dot_nt(dg, wg_ref[j]) + _dot_nt(du, wu_ref[j])
        xf = x_ref[...]
        dxr, dnw = _rms_bwd(xf, _rstd(xf), nw_ref[...], dxn)
        dx_ref[...] = dyv + dxr
        dnw_ref[...] += dnw

    row = pl.BlockSpec((tm, d), lambda i: (i, 0))
    act = pl.BlockSpec((nj, tm, fs), lambda i: (0, i, 0))
    act_shape = jax.ShapeDtypeStruct((nj, t, fs), BF16)
    return _grid_call(
        body, name, t // tm,
        [row, row, _resident((1, d)), act, act, _resident(wg.shape), _resident(wu.shape), _resident(wd.shape)],
        [row, row, act, act, act, pl.BlockSpec((1, d), lambda i: (0, 0))],
        [jax.ShapeDtypeStruct((t, d), F32), jax.ShapeDtypeStruct((t, d), BF16),
         act_shape, act_shape, act_shape, jax.ShapeDtypeStruct((1, d), F32)],
        (dy, x, nw, g, u, wg, wu, wd), exchange=exchange)


def _matmul_tn(a, b, tk, name):
    a3, b3 = a.ndim == 3, b.ndim == 3
    nj = a.shape[0] if a3 else (b.shape[0] if b3 else 1)
    t, m = a.shape[-2:]
    n = b.shape[-1]
    nt = t // tk

    def body(a_ref, b_ref, o_ref, acc_ref):
        k = pl.program_id(1)

        @pl.when(k == 0)
        def _():
            acc_ref[...] = jnp.zeros_like(acc_ref)

        acc_ref[...] += lax.dot_general(a_ref[...], b_ref[...], (((0,), (0,)), ((), ())),
                                        preferred_element_type=F32)

        @pl.when(k == nt - 1)
        def _():
            o_ref[...] = acc_ref[...].astype(o_ref.dtype)

    a_spec = (pl.BlockSpec((None, tk, m), lambda j, k: (j, k, 0)) if a3
              else pl.BlockSpec((tk, m), lambda j, k: (k, 0)))
    b_spec = (pl.BlockSpec((None, tk, n), lambda j, k: (j, k, 0)) if b3
              else pl.BlockSpec((tk, n), lambda j, k: (k, 0)))
    return pl.pallas_call(
        body, name=name, grid=(nj, nt),
        in_specs=[a_spec, b_spec],
        out_specs=pl.BlockSpec((None, m, n), lambda j, k: (j, 0, 0)),
        out_shape=jax.ShapeDtypeStruct((nj, m, n), BF16),
        scratch_shapes=[pltpu.VMEM((m, n), F32)],
        compiler_params=_params(("arbitrary", "arbitrary"), VMEM_LIMIT),
    )(a, b)


P_QKVA, P_Z, P_AB, P_QKVB = (0, 1536), (1536, 2048), (2048, 2176), (2176, 3712)
P_PIECES = (P_QKVA, P_Z, P_AB, P_QKVB)
P_COLS = 3712


def _mix_in_fwd(x1, nw, wp, tm):
    t, d = x1.shape

    def body(x_ref, nw_ref, w_ref, hn_ref, *outs):
        xf = x_ref[...]
        xn = (xf * _rstd(xf) * nw_ref[...]).astype(BF16)
        hn_ref[...] = xn
        for (a, b), o_ref in zip(P_PIECES, outs):
            o_ref[...] = jnp.dot(xn, w_ref[:, a:b], preferred_element_type=F32)

    row = pl.BlockSpec((tm, d), lambda i: (i, 0))
    return pl.pallas_call(
        body, name="mix_in_fwd", grid=(t // tm,),
        in_specs=[row, _resident((1, d)), _resident(wp.shape)],
        out_specs=[row] + [pl.BlockSpec((tm, b - a), lambda i: (i, 0)) for a, b in P_PIECES],
        out_shape=[jax.ShapeDtypeStruct((t, d), BF16)]
                  + [jax.ShapeDtypeStruct((t, b - a), F32) for a, b in P_PIECES],
        compiler_params=_params(("arbitrary",), VMEM_LIMIT),
    )(x1, nw, wp)


def _mix_in_bwd_dx(dx, x1, nw, dpieces, wp, tm):
    t, d = x1.shape

    def body(dx_ref, x_ref, nw_ref, p0, p1, p2, p3, w_ref, o_ref, dnw_ref):
        @pl.when(pl.program_id(0) == 0)
        def _():
            dnw_ref[...] = jnp.zeros_like(dnw_ref)

        dh = jnp.zeros((tm, d), F32)
        for (a, b), p_ref in zip(P_PIECES, (p0, p1, p2, p3)):
            dh = dh + _dot_nt(p_ref[...], w_ref[:, a:b])
        xf = x_ref[...]
        dxr, dnw = _rms_bwd(xf, _rstd(xf), nw_ref[...], dh)
        o_ref[...] = dx_ref[...] + dxr
        dnw_ref[...] += dnw

    row = pl.BlockSpec((tm, d), lambda i: (i, 0))
    return pl.pallas_call(
        body, name="mix_in_bwd_dx", grid=(t // tm,),
        in_specs=[row, row, _resident((1, d))]
                 + [pl.BlockSpec((tm, b - a), lambda i: (i, 0)) for a, b in P_PIECES] + [_resident(wp.shape)],
        out_specs=[row, pl.BlockSpec((1, d), lambda i: (0, 0))],
        out_shape=[jax.ShapeDtypeStruct((t, d), F32), jax.ShapeDtypeStruct((1, d), F32)],
        compiler_params=_params(("arbitrary",), VMEM_LIMIT),
    )(dx, x1, nw, *dpieces, wp)


def _mix_out_fwd(x1, oa, ob, w, tm):
    t, d = x1.shape
    half = oa.shape[1]

    def body(x_ref, oa_ref, ob_ref, w_ref, o_ref):
        o_ref[...] = (x_ref[...] + _dot(oa_ref[...], w_ref[0:half, :]) + _dot(ob_ref[...], w_ref[half:2 * half, :]))

    row = pl.BlockSpec((tm, d), lambda i: (i, 0))
    hrow = pl.BlockSpec((tm, half), lambda i: (i, 0))
    return pl.pallas_call(
        body, name="mix_out_fwd", grid=(t // tm,),
        in_specs=[row, hrow, hrow, _resident(w.shape)],
        out_specs=row, out_shape=jax.ShapeDtypeStruct((t, d), F32),
        compiler_params=_params(("arbitrary",), VMEM_LIMIT),
    )(x1, oa, ob, w)


def _mix_out_bwd(dx2, w, tm):
    t, d = dx2.shape
    half = w.shape[0] // 2

    def body(dx_ref, w_ref, doa_ref, dob_ref, dxb_ref):
        dxb = dx_ref[...].astype(BF16)
        dxb_ref[...] = dxb
        doa_ref[...] = _dot_nt(dxb, w_ref[0:half, :])
        dob_ref[...] = _dot_nt(dxb, w_ref[half:2 * half, :])

    row = pl.BlockSpec((tm, d), lambda i: (i, 0))
    hrow = pl.BlockSpec((tm, half), lambda i: (i, 0))
    return pl.pallas_call(
        body, name="mix_out_bwd", grid=(t // tm,),
        in_specs=[row, _resident(w.shape)],
        out_specs=[hrow, hrow, row],
        out_shape=[jax.ShapeDtypeStruct((t, half), F32), jax.ShapeDtypeStruct((t, half), F32),
                   jax.ShapeDtypeStruct((t, d), BF16)],
        compiler_params=_params(("arbitrary",), VMEM_LIMIT),
    )(dx2, w)


def _final_loss(x3, fw, target, tm):
    t, d = x3.shape

    def body(x_ref, w_ref, t_ref, dx_ref, loss_ref, dw_ref):
        @pl.when(pl.program_id(0) == 0)
        def _():
            loss_ref[...] = jnp.zeros_like(loss_ref)
            dw_ref[...] = jnp.zeros_like(dw_ref)

        xf = x_ref[...]
        r = _rstd(xf)
        err = xf * r * w_ref[...] - t_ref[...]
        loss_ref[...] += 0.5 * jnp.sum(jnp.mean(err * err, axis=-1, keepdims=True), axis=0, keepdims=True)
        dxr, dw = _rms_bwd(xf, r, w_ref[...], err * (1.0 / d))
        dx_ref[...] = dxr
        dw_ref[...] += dw

    row = pl.BlockSpec((tm, d), lambda i: (i, 0))
    return pl.pallas_call(
        body, name="final_loss", grid=(t // tm,),
        in_specs=[row, _resident((1, d)), row],
        out_specs=[row, pl.BlockSpec((1, LANES), lambda i: (0, 0)), pl.BlockSpec((1, d), lambda i: (0, 0))],
        out_shape=[jax.ShapeDtypeStruct((t, d), F32), jax.ShapeDtypeStruct((1, LANES), F32),
                   jax.ShapeDtypeStruct((1, d), F32)],
        compiler_params=_params(("arbitrary",), VMEM_LIMIT),
    )(x3, fw, target)


HALO = 8


def _halo_row_specs(tr, cols, nrow8):
    per = tr // HALO
    return [pl.BlockSpec((tr, cols), lambda i: (i, 0)),
            pl.BlockSpec((HALO, cols), lambda i: (jnp.maximum(i * per - 1, 0), 0)),
            pl.BlockSpec((HALO, cols), lambda i: (jnp.minimum((i + 1) * per, nrow8 - 1), 0))]


def _conv_window(xm, xp, xn, first, last, cols):
    prev = jnp.where(first, 0.0, xp[:, cols])
    nxt = jnp.where(last, 0.0, xn[:, cols])
    return jnp.concatenate([prev, xm[:, cols], nxt], axis=0)


def _shift_rows(xw, off):
    n = xw.shape[0]
    sh = (-off) % n
    return xw if sh == 0 else pltpu.roll(xw, sh, 0)


def _conv_pre(xw, cw_ref, cols):
    acc = None
    for j in range(CONV_TAPS):
        term = _shift_rows(xw, j - CONV_TAPS // 2) * cw_ref[j:j + 1, cols]
        acc = term if acc is None else acc + term
    return acc


def _softplus(x):
    u = jnp.exp(-jnp.abs(x))
    w = 1.0 + u
    log1p = jnp.where(w == 1.0, u, jnp.log(w) * u / jnp.where(w == 1.0, 1.0, w - 1.0))
    return jnp.maximum(x, 0.0) + log1p


def _gdn_prep_fwd(qkva, cw, ab, gp, tr):
    t, c = qkva.shape
    nt = t // tr
    ncb = c // LANES

    def body(xm, xp, xn, cw_ref, ab_ref, gp_ref, o_ref, gb_ref):
        i = pl.program_id(0)
        first, last = i == 0, i == nt - 1
        for cb in range(ncb):
            cols = slice(cb * LANES, (cb + 1) * LANES)
            xw = _conv_window(xm, xp, xn, first, last, cols)
            pre = _conv_pre(xw, cw_ref, cols)[HALO:HALO + tr]
            y = pre * _sigmoid(pre)
            if cb < 2 * GDN_HEADS:
                y = y * lax.rsqrt(jnp.sum(y * y, axis=-1, keepdims=True) + EPS)
            if cb < GDN_HEADS:
                y = y * (GDN_DIM ** -0.5)
            o_ref[:, cols] = y
        abv = ab_ref[...]
        lane = lax.broadcasted_iota(jnp.int32, abv.shape, 1)
        g = -jnp.exp(gp_ref[0:1, :]) * _softplus(abv + gp_ref[1:2, :])
        gb_ref[...] = jnp.where(lane < 8, g, jnp.where(lane < 16, _sigmoid(abv), 0.0))

    return pl.pallas_call(
        body, name="gdn_prep_fwd", grid=(nt,),
        in_specs=_halo_row_specs(tr, c, t // HALO)
                 + [_resident(cw.shape), pl.BlockSpec((tr, LANES), lambda i: (i, 0)), _resident(gp.shape)],
        out_specs=[pl.BlockSpec((tr, c), lambda i: (i, 0)), pl.BlockSpec((tr, LANES), lambda i: (i, 0))],
        out_shape=[jax.ShapeDtypeStruct((t, c), F32), jax.ShapeDtypeStruct((t, LANES), F32)],
        compiler_params=_params(("arbitrary",), VMEM_LIMIT),
    )(qkva, qkva, qkva, cw, ab, gp)


def _gdn_prep_bwd(qkva, cw, ab, gp, dy, dgates, tr):
    t, c = qkva.shape
    nt = t // tr
    ncb = c // LANES

    def body(xm, xp, xn, fm, fp, fn, cw_ref, ab_ref, gp_ref, gf_ref, dx_ref, dab_ref, dcw_ref, dgp_ref):
        i = pl.program_id(0)
        first, last = i == 0, i == nt - 1

        @pl.when(first)
        def _():
            dcw_ref[...] = jnp.zeros_like(dcw_ref)
            dgp_ref[...] = jnp.zeros_like(dgp_ref)

        sub8 = lax.broadcasted_iota(jnp.int32, (8, LANES), 0)
        for cb in range(ncb):
            cols = slice(cb * LANES, (cb + 1) * LANES)
            xw = _conv_window(xm, xp, xn, first, last, cols)
            dyw = _conv_window(fm, fp, fn, first, last, cols)
            pre = _conv_pre(xw, cw_ref, cols)
            sg = _sigmoid(pre)
            s = pre * sg
            if cb < 2 * GDN_HEADS:
                scale = (GDN_DIM ** -0.5) if cb < GDN_HEADS else 1.0
                r = lax.rsqrt(jnp.sum(s * s, axis=-1, keepdims=True) + EPS)
                dn = dyw * scale
                ds = r * dn - s * (r * r * r) * jnp.sum(dn * s, axis=-1, keepdims=True)
            else:
                ds = dyw
            dpre = ds * (sg * (1.0 + pre * (1.0 - sg)))
            dx = None
            dcw = jnp.zeros((8, LANES), F32)
            for j in range(CONV_TAPS):
                off = j - CONV_TAPS // 2
                term = _shift_rows(dpre, -off)[HALO:HALO + tr] * cw_ref[j:j + 1, cols]
                dx = term if dx is None else dx + term
                tap = jnp.sum(dpre[HALO:HALO + tr] * _shift_rows(xw, off)[HALO:HALO + tr], axis=0, keepdims=True)
                dcw = dcw + jnp.where(sub8 == j, tap, 0.0)
            dx_ref[:, cols] = dx.astype(BF16)
            dcw_ref[:, cols] += dcw

        abv = ab_ref[...]
        dgb = gf_ref[...]
        lane = lax.broadcasted_iota(jnp.int32, abv.shape, 1)
        nea = -jnp.exp(gp_ref[0:1, :])
        xs = abv + gp_ref[1:2, :]
        g = nea * _softplus(xs)
        beta = _sigmoid(abv)
        da = dgb * nea * _sigmoid(xs)
        dab = jnp.where(lane < 8, da, jnp.where(lane < 16, dgb * beta * (1.0 - beta), 0.0))
        dab_ref[...] = dab.astype(BF16)
        keep = lane[0:1, :] < 8
        dalog = jnp.where(keep, jnp.sum(dgb * g, axis=0, keepdims=True), 0.0)
        ddtb = jnp.where(keep, jnp.sum(da, axis=0, keepdims=True), 0.0)
        dgp_ref[...] += jnp.where(sub8 == 0, dalog, 0.0) + jnp.where(sub8 == 1, ddtb, 0.0)

    lrow = pl.BlockSpec((tr, LANES), lambda i: (i, 0))
    halo = _halo_row_specs(tr, c, t // HALO)
    return pl.pallas_call(
        body, name="gdn_prep_bwd", grid=(nt,),
        in_specs=halo + halo + [_resident(cw.shape), lrow, _resident(gp.shape), lrow],
        out_specs=[pl.BlockSpec((tr, c), lambda i: (i, 0)), lrow,
                   pl.BlockSpec(cw.shape, lambda i: (0, 0)), pl.BlockSpec(gp.shape, lambda i: (0, 0))],
        out_shape=[jax.ShapeDtypeStruct((t, c), BF16), jax.ShapeDtypeStruct((t, LANES), BF16),
                   jax.ShapeDtypeStruct(cw.shape, F32), jax.ShapeDtypeStruct(gp.shape, F32)],
        compiler_params=_params(("arbitrary",), VMEM_LIMIT),
    )(qkva, qkva, qkva, dy, dy, dy, cw, ab, gp, dgates)


def _chunk_masks(lower):
    ii = lax.broadcasted_iota(jnp.int32, (CHUNK, CHUNK), 0)
    jj = lax.broadcasted_iota(jnp.int32, (CHUNK, CHUNK), 1)
    incl = (ii >= jj) if lower else (ii <= jj)
    strict = (ii > jj) if lower else (ii < jj)
    return ii, jj, incl, strict


def _dot3(a, b):
    ah = a.astype(BF16)
    al = (a - ah.astype(F32)).astype(BF16)
    bh = b.astype(BF16)
    bl = (b - bh.astype(F32)).astype(BF16)
    d = lambda u, v: jnp.dot(u, v, preferred_element_type=F32)
    return d(ah, bh) + (d(ah, bl) + d(al, bh))


def _tri_inv_many(lmats, ii, jj):
    m16 = (ii // 16) == (jj // 16)
    m32 = (ii // 32) == (jj // 32)
    eye = jnp.where(ii == jj, 1.0, 0.0)
    l16 = [jnp.where(m16, l, 0.0) for l in lmats]
    p2 = [_dot3(a, a) for a in l16]
    p4 = [_dot3(a, a) for a in p2]
    p8 = [_dot3(a, a) for a in p4]
    xs = [eye - a for a in l16]
    for ps in (p2, p4, p8):
        xs = [x + _dot3(x, p) for x, p in zip(xs, ps)]
    for off in ([jnp.where(m32 & jnp.logical_not(m16), l, 0.0) for l in lmats],
                [jnp.where(m32, 0.0, l) for l in lmats]):
        ys = [_dot3(x, c) for x, c in zip(xs, off)]
        xs = [x - _dot3(y, x) for x, y in zip(xs, ys)]
    return xs


def _col_to_row(col, ii, jj):
    return jnp.sum(jnp.where(ii == jj, col, 0.0), axis=0, keepdims=True)


def _row_to_col(row, ii, jj):
    return jnp.sum(jnp.where(ii == jj, row, 0.0), axis=1, keepdims=True)


def _chain_common(q, k, v, graw_col, graw_row, bcol, masks):
    ii, jj, incl, strict = masks
    inclt = jnp.logical_not(strict)
    gcol = jnp.sum(jnp.where(incl, graw_row, 0.0), axis=1, keepdims=True)
    grow = jnp.sum(jnp.where(inclt, graw_col, 0.0), axis=0, keepdims=True)
    glast = jnp.sum(graw_row, axis=1, keepdims=True)
    decay = jnp.where(incl, jnp.exp(jnp.where(incl, gcol - grow, 0.0)), 0.0)
    kb = k * bcol
    vb = v * bcol
    eg = jnp.exp(gcol)
    ek = jnp.exp(glast - gcol)
    kbg = kb * eg
    amat = _dot_nt(kb, k)
    qk = _dot_nt(q, k)
    return dict(gcol=gcol, glast=glast, decay=decay, kb=kb, vb=vb, eg=eg, ek=ek, kbg=kbg, amat=amat, qk=qk,
                intra=qk * decay, qg=q * eg, kdec=k * ek)


def _gdn_fwd(qkvc, gb, gbt):
    tm, u, w, qg, kd, intra, egl = _gdn_local_fwd(qkvc, gb, gbt)
    o_f, o_b, s_f, s_b, vn_f, vn_b = _gdn_scan_fwd(u, w, qg, kd, intra, egl, qkvc.shape[0])
    return o_f, o_b, dict(tm=tm, w=w, qg=qg, kd=kd, intra=intra, egl=egl, s=(s_f, s_b), vn=(vn_f, vn_b))


N_CHAINS = 2 * GDN_HEADS


def _load_chains(x_ref, g_ref, gt_ref):
    hd = GDN_HEADS * GDN_DIM
    chains = []
    for d in range(2):
        masks = _chunk_masks(d == 0)
        for h in range(GDN_HEADS):
            ch = d * GDN_HEADS + h
            q = x_ref[:, h * GDN_DIM:(h + 1) * GDN_DIM]
            k = x_ref[:, hd + h * GDN_DIM:hd + (h + 1) * GDN_DIM]
            v = x_ref[:, 2 * hd + h * GDN_DIM:2 * hd + (h + 1) * GDN_DIM]
            bcol = g_ref[:, 8 + ch:9 + ch]
            cm = _chain_common(q, k, v, g_ref[:, ch:ch + 1], gt_ref[0, ch:ch + 1, :], bcol, masks)
            chains.append(dict(cm, q=q, k=k, v=v, bcol=bcol, masks=masks, ch=ch, h=h))
    return chains


def _chain_shape(rows, cols, dtype):
    return lambda nc: jax.ShapeDtypeStruct((nc, N_CHAINS, rows, cols), dtype)


def _gdn_local_fwd(qkvc, gb, gbt):
    t = qkvc.shape[0]
    nc = t // CHUNK
    hd = GDN_HEADS * GDN_DIM

    def body(x_ref, g_ref, gt_ref, t_ref, u_ref, w_ref, qg_ref, kd_ref, in_ref, eg_ref):
        chains = _load_chains(x_ref, g_ref, gt_ref)
        ii, jj = chains[0]["masks"][0:2]
        tms = _tri_inv_many([jnp.where(c["masks"][3], c["amat"] * c["decay"], 0.0) for c in chains], ii, jj)
        us = [_dot(tm, c["vb"]) for tm, c in zip(tms, chains)]
        ws = [_dot(tm, c["kbg"]) for tm, c in zip(tms, chains)]
        for c, tm, u, w in zip(chains, tms, us, ws):
            ch = c["ch"]
            t_ref[0, ch] = tm
            u_ref[0, ch] = u
            w_ref[0, ch] = w.astype(BF16)
            qg_ref[0, ch] = c["qg"].astype(BF16)
            kd_ref[0, ch] = c["kdec"].astype(BF16)
            in_ref[0, ch] = c["intra"].astype(BF16)
            eg_ref[0, ch:ch + 1, :] = jnp.broadcast_to(jnp.exp(c["glast"]), (1, LANES))

    blk = lambda rows, cols: pl.BlockSpec((1, N_CHAINS, rows, cols), lambda n: (n, 0, 0, 0))
    shapes = [_chain_shape(CHUNK, CHUNK, F32), _chain_shape(CHUNK, GDN_DIM, F32), _chain_shape(CHUNK, GDN_DIM, BF16),
              _chain_shape(CHUNK, GDN_DIM, BF16), _chain_shape(CHUNK, GDN_DIM, BF16), _chain_shape(CHUNK, CHUNK, BF16)]
    return tuple(pl.pallas_call(
        body, name="gdn_local_fwd", grid=(nc,),
        in_specs=[pl.BlockSpec((CHUNK, 3 * hd), lambda n: (n, 0)), pl.BlockSpec((CHUNK, LANES), lambda n: (n, 0)),
                  pl.BlockSpec((1, 16, CHUNK), lambda n: (n, 0, 0))],
        out_specs=[blk(CHUNK, CHUNK), blk(CHUNK, GDN_DIM), blk(CHUNK, GDN_DIM), blk(CHUNK, GDN_DIM),
                   blk(CHUNK, GDN_DIM), blk(CHUNK, CHUNK), pl.BlockSpec((1, N_CHAINS, LANES), lambda n: (n, 0, 0))],
        out_shape=[s(nc) for s in shapes] + [jax.ShapeDtypeStruct((nc, N_CHAINS, LANES), F32)],
        compiler_params=_params(("arbitrary",), VMEM_LIMIT),
    )(qkvc, gb, gbt))


def _dir_specs(nc, rev):
    def spec(d, rows, cols, own=False):
        chunk = (lambda n: n) if (d == 0) != rev else (lambda n: nc - 1 - n)
        blk = 0 if own else d
        if rows is None:
            return pl.BlockSpec((1, GDN_HEADS if own else N_CHAINS, cols), lambda n: (chunk(n), 0, 0))
        return pl.BlockSpec((1, GDN_HEADS, rows, cols), lambda n: (chunk(n), blk, 0, 0))

    def rows_spec(d, cols):
        chunk = (lambda n: n) if (d == 0) != rev else (lambda n: nc - 1 - n)
        return pl.BlockSpec((CHUNK, cols), lambda n: (chunk(n), 0))
    return spec, rows_spec


def _gdn_scan_fwd(u, w, qg, kd, intra, egl, t):
    nc = t // CHUNK
    hd = GDN_HEADS * GDN_DIM

    def body(*refs):
        ins, outs, state = refs[:12], refs[12:18], refs[18]
        @pl.when(pl.program_id(0) == 0)
        def _():
            state[...] = jnp.zeros_like(state)

        chains = [(d, h) for d in range(2) for h in range(GDN_HEADS)]
        pick = lambda k, d, h: ins[2 * k + d][0, h]
        states = [state[ch] for ch in range(N_CHAINS)]
        sbs = [s.astype(BF16) for s in states]
        ws = [_dot(pick(1, d, h), sb) for (d, h), sb in zip(chains, sbs)]
        o1 = [_dot(pick(2, d, h), sb) for (d, h), sb in zip(chains, sbs)]
        vns = [(pick(0, d, h) - wsb).astype(BF16) for (d, h), wsb in zip(chains, ws)]
        o2 = [_dot(pick(4, d, h), vn) for (d, h), vn in zip(chains, vns)]
        kv = [_dot_tn(pick(3, d, h), vn) for (d, h), vn in zip(chains, vns)]
        for ch, (d, h) in enumerate(chains):
            outs[d][:, h * GDN_DIM:(h + 1) * GDN_DIM] = o1[ch] + o2[ch]
            outs[2 + d][0, h] = states[ch]
            outs[4 + d][0, h] = vns[ch]
            state[ch] = states[ch] * ins[10 + d][0, ch:ch + 1, :] + kv[ch]

    spec, rows_spec = _dir_specs(nc, False)
    pair = lambda rows, cols, own=False: [spec(0, rows, cols, own), spec(1, rows, cols, own)]
    s_shape = jax.ShapeDtypeStruct((nc, GDN_HEADS, GDN_DIM, GDN_DIM), F32)
    vn_shape = jax.ShapeDtypeStruct((nc, GDN_HEADS, CHUNK, GDN_DIM), BF16)
    return pl.pallas_call(
        body, name="gdn_scan_fwd", grid=(nc,),
        in_specs=(pair(CHUNK, GDN_DIM) + pair(CHUNK, GDN_DIM) + pair(CHUNK, GDN_DIM) + pair(CHUNK, GDN_DIM)
                  + pair(CHUNK, CHUNK) + pair(None, LANES)),
        out_specs=([rows_spec(0, hd), rows_spec(1, hd)] + pair(GDN_DIM, GDN_DIM, True)
                   + pair(CHUNK, GDN_DIM, True)),
        out_shape=[jax.ShapeDtypeStruct((t, hd), F32), jax.ShapeDtypeStruct((t, hd), F32),
                   s_shape, s_shape, vn_shape, vn_shape],
        scratch_shapes=[pltpu.VMEM((N_CHAINS, GDN_DIM, GDN_DIM), F32)],
        compiler_params=_params(("arbitrary",), VMEM_LIMIT),
    )(u, u, w, w, qg, qg, kd, kd, intra, intra, egl, egl)


def _gdn_bwd(qkvc, gb, gbt, do, saved, exchange=None):
    scan = _gdn_scan_bwd(do, saved, qkvc.shape[0])
    return _gdn_local_bwd(qkvc, gb, gbt, do, saved, scan, exchange)


def _gdn_scan_bwd(do, saved, t):
    nc = t // CHUNK
    hd = GDN_HEADS * GDN_DIM

    def body(*refs):
        ins, outs, dstate = refs[:16], refs[16:26], refs[26]
        @pl.when(pl.program_id(0) == 0)
        def _():
            dstate[...] = jnp.zeros_like(dstate)

        chains = [(d, h) for d in range(2) for h in range(GDN_HEADS)]
        pick = lambda k, d, h: ins[2 * k + d][0, h]
        dss = [dstate[ch] for ch in range(N_CHAINS)]
        dsbs = [ds.astype(BF16) for ds in dss]
        ss = [pick(1, d, h) for d, h in chains]
        sbs = [s.astype(BF16) for s in ss]
        dos = [ins[d][:, h * GDN_DIM:(h + 1) * GDN_DIM].astype(BF16) for d, h in chains]
        dv1 = [_dot_tn(pick(5, d, h), dov) for (d, h), dov in zip(chains, dos)]
        dv2 = [_dot(pick(4, d, h), dsb) for (d, h), dsb in zip(chains, dsbs)]
        ds1 = [_dot_tn(pick(3, d, h), dov) for (d, h), dov in zip(chains, dos)]
        dkds = [_dot_nt(pick(6, d, h), dsb) for (d, h), dsb in zip(chains, dsbs)]
        dqgs = [_dot_nt(dov, sb) for dov, sb in zip(dos, sbs)]
        dvns = [(a + b).astype(BF16) for a, b in zip(dv1, dv2)]
        ds2 = [_dot_tn(pick(2, d, h), dvn) for (d, h), dvn in zip(chains, dvns)]
        dws = [_dot_nt(dvn, sb) for dvn, sb in zip(dvns, sbs)]
        for ch, (d, h) in enumerate(chains):
            egl = ins[14 + d][0, ch:ch + 1, :]
            outs[d][0, h] = dvns[ch]
            outs[2 + d][0, h] = (-dws[ch]).astype(BF16)
            outs[4 + d][0, h] = dqgs[ch]
            outs[6 + d][0, h] = dkds[ch]
            outs[8 + d][0, h:h + 1, :] = egl * jnp.sum(jnp.sum(ss[ch] * dss[ch], axis=1, keepdims=True),
                                                       axis=0, keepdims=True)
            dstate[ch] = ds1[ch] + egl * dss[ch] - ds2[ch]

    spec, rows_spec = _dir_specs(nc, True)
    pair = lambda rows, cols, own=False: [spec(0, rows, cols, own), spec(1, rows, cols, own)]
    s_f, s_b = saved["s"]
    vn_f, vn_b = saved["vn"]
    w, qg, kd, intra, egl = saved["w"], saved["qg"], saved["kd"], saved["intra"], saved["egl"]
    own = lambda rows, cols, dtype: jax.ShapeDtypeStruct((nc, GDN_HEADS, rows, cols), dtype)
    row_shape = jax.ShapeDtypeStruct((nc, GDN_HEADS, LANES), F32)
    return pl.pallas_call(
        body, name="gdn_scan_bwd", grid=(nc,),
        in_specs=([rows_spec(0, hd), rows_spec(1, hd)] + pair(GDN_DIM, GDN_DIM, True) + pair(CHUNK, GDN_DIM)
                  + pair(CHUNK, GDN_DIM) + pair(CHUNK, GDN_DIM) + pair(CHUNK, CHUNK) + pair(CHUNK, GDN_DIM, True)
                  + pair(None, LANES)),
        out_specs=(pair(CHUNK, GDN_DIM, True) + pair(CHUNK, GDN_DIM, True) + pair(CHUNK, GDN_DIM, True)
                   + pair(CHUNK, GDN_DIM, True) + pair(None, LANES, True)),
        out_shape=[own(CHUNK, GDN_DIM, BF16)] * 4 + [own(CHUNK, GDN_DIM, F32)] * 4 + [row_shape] * 2,
        scratch_shapes=[pltpu.VMEM((N_CHAINS, GDN_DIM, GDN_DIM), F32)],
        compiler_params=_params(("arbitrary",), VMEM_LIMIT),
    )(do, do, s_f, s_b, w, w, qg, qg, kd, kd, intra, intra, vn_f, vn_b, egl, egl)


def _dot3_nt(a, b):
    ah = a.astype(BF16)
    al = (a - ah.astype(F32)).astype(BF16)
    bh = b.astype(BF16)
    bl = (b - bh.astype(F32)).astype(BF16)
    return _dot_nt(ah, bh) + (_dot_nt(ah, bl) + _dot_nt(al, bh))


def _dot3_tn(a, b):
    ah = a.astype(BF16)
    al = (a - ah.astype(F32)).astype(BF16)
    bh = b.astype(BF16)
    bl = (b - bh.astype(F32)).astype(BF16)
    return _dot_tn(ah, bh) + (_dot_tn(ah, bl) + _dot_tn(al, bh))


def _gdn_local_bwd(qkvc, gb, gbt, do, saved, scan, exchange=None):
    t = qkvc.shape[0]
    nc = t // CHUNK
    hd = GDN_HEADS * GDN_DIM

    def body(*refs):
        x_ref, g_ref, gt_ref, do_ref, t_ref = refs[:5]
        per_dir = refs[5:17]
        dx_ref, dg_ref = refs[17:]
        chains = _load_chains(x_ref, g_ref, gt_ref)
        lane = lax.broadcasted_iota(jnp.int32, (CHUNK, LANES), 1)
        dgates = jnp.zeros((CHUNK, LANES), F32)
        for c in chains:
            d = c["ch"] // GDN_HEADS
            vn_ref, dvn_ref, dw_ref, dqg_ref, dkd_ref, dgl_ref = per_dir[d::2]
            h = c["h"]
            c.update(tm=t_ref[0, c["ch"]], dov=do_ref[:, h * GDN_DIM:(h + 1) * GDN_DIM], vnew=vn_ref[0, h],
                     dvnew=dvn_ref[0, h], dw=dw_ref[0, h], dqg=dqg_ref[0, h], dkdec=dkd_ref[0, h],
                     dglast=dgl_ref[0, h:h + 1, 0:1])
        dintras = [_dot_nt(c["dov"], c["vnew"]) for c in chains]
        dts = [_dot_nt(c["dvnew"], c["vb"]) + _dot_nt(c["dw"], c["kbg"]) for c in chains]
        dvbs = [_dot_tn(c["tm"], c["dvnew"]) for c in chains]
        dkbgs = [_dot_tn(c["tm"], c["dw"]) for c in chains]
        tdts = [_dot3_nt(dt, c["tm"]) for dt, c in zip(dts, chains)]
        dls = [jnp.where(c["masks"][3], -_dot3_tn(c["tm"], tdt), 0.0) for tdt, c in zip(tdts, chains)]
        das = [dl * c["decay"] for dl, c in zip(dls, chains)]
        dqks = [jnp.where(c["masks"][2], di, 0.0) * c["decay"] for di, c in zip(dintras, chains)]
        dkb1 = [_dot(da, c["k"]) for da, c in zip(das, chains)]
        dk1 = [_dot_tn(da, c["kb"]) for da, c in zip(das, chains)]
        dk2 = [_dot_tn(dqk, c["q"]) for dqk, c in zip(dqks, chains)]
        dq1 = [_dot(dqk, c["k"]) for dqk, c in zip(dqks, chains)]
        grads = []
        for n, c in enumerate(chains):
            ch = c["ch"]
            ii, jj, incl, strict = c["masks"]
            k, v, bcol = c["k"], c["v"], c["bcol"]
            decay, eg, ek, kbg = c["decay"], c["eg"], c["ek"], c["kbg"]
            dqg, dkdec, dglast = c["dqg"], c["dkdec"], c["dglast"]
            dvb, dkbg, dl = dvbs[n], dkbgs[n], dls[n]
            dintra = jnp.where(incl, dintras[n], 0.0)
            mm = (dl * c["amat"] + dintra * c["qk"]) * decay
            dkb = dkb1[n] + dkbg * eg
            dk = dk1[n] + dk2[n] + dkdec * ek + dkb * bcol
            dq = dq1[n] + dqg * eg
            dv = dvb * bcol
            dbeta = jnp.sum(dkb * k, axis=1, keepdims=True) + jnp.sum(dvb * v, axis=1, keepdims=True)
            kd2 = jnp.sum(dkdec * c["kdec"], axis=1, keepdims=True)
            dgc = (jnp.sum(mm, axis=1, keepdims=True) - _row_to_col(jnp.sum(mm, axis=0, keepdims=True), ii, jj)
                   + jnp.sum(dqg * c["qg"], axis=1, keepdims=True) - kd2
                   + jnp.sum(dkbg * kbg, axis=1, keepdims=True))
            dgl = dglast + jnp.sum(kd2, axis=0, keepdims=True)
            draw = jnp.sum(jnp.where(jnp.logical_not(strict), _col_to_row(dgc, ii, jj), 0.0),
                           axis=1, keepdims=True) + dgl
            dgates = dgates + jnp.where(lane == ch, draw, 0.0) + jnp.where(lane == 8 + ch, dbeta, 0.0)
            grads.append((dq, dk, dv))
        for h in range(GDN_HEADS):
            for part in range(3):
                cols = slice(part * hd + h * GDN_DIM, part * hd + (h + 1) * GDN_DIM)
                dx_ref[:, cols] = grads[h][part] + grads[GDN_HEADS + h][part]
        dg_ref[...] = dgates

    all8 = lambda rows, cols: pl.BlockSpec((1, N_CHAINS, rows, cols), lambda n: (n, 0, 0, 0))
    own4 = lambda rows, cols: pl.BlockSpec((1, GDN_HEADS, rows, cols), lambda n: (n, 0, 0, 0))
    row4 = pl.BlockSpec((1, GDN_HEADS, LANES), lambda n: (n, 0, 0))
    vn_f, vn_b = saved["vn"]
    dvn_f, dvn_b, dw_f, dw_b, dqg_f, dqg_b, dkd_f, dkd_b, dgl_f, dgl_b = scan
    return _grid_call(
        body, "gdn_local_bwd", nc,
        [pl.BlockSpec((CHUNK, 3 * hd), lambda n: (n, 0)), pl.BlockSpec((CHUNK, LANES), lambda n: (n, 0)),
         pl.BlockSpec((1, 16, CHUNK), lambda n: (n, 0, 0)), pl.BlockSpec((CHUNK, hd), lambda n: (n, 0)),
         all8(CHUNK, CHUNK)] + [own4(CHUNK, GDN_DIM)] * 10 + [row4, row4],
        [pl.BlockSpec((CHUNK, 3 * hd), lambda n: (n, 0)), pl.BlockSpec((CHUNK, LANES), lambda n: (n, 0))],
        [jax.ShapeDtypeStruct((t, 3 * hd), F32), jax.ShapeDtypeStruct((t, LANES), F32)],
        (qkvc, gb, gbt, do, saved["tm"], vn_f, vn_b, dvn_f, dvn_b, dw_f, dw_b, dqg_f, dqg_b, dkd_f, dkd_b, dgl_f, dgl_b),
        exchange=exchange)


def _gdn_post_fwd(of, ob, z, gw, tm):
    t, hd = of.shape

    def body(of_ref, ob_ref, z_ref, w_ref, o_ref):
        for h in range(GDN_HEADS):
            cols = slice(h * GDN_DIM, (h + 1) * GDN_DIM)
            o = of_ref[:, cols] + ob_ref[:, cols]
            zv = z_ref[:, cols]
            o_ref[:, cols] = (o * _rstd(o) * w_ref[...] * (zv * _sigmoid(zv))).astype(BF16)

    row = pl.BlockSpec((tm, hd), lambda i: (i, 0))
    return pl.pallas_call(
        body, name="gdn_post_fwd", grid=(t // tm,),
        in_specs=[row, row, row, _resident((1, GDN_DIM))],
        out_specs=row, out_shape=jax.ShapeDtypeStruct((t, hd), BF16),
        compiler_params=_params(("arbitrary",), VMEM_LIMIT),
    )(of, ob, z, gw)


def _gdn_post_bwd(doa, of, ob, z, gw, tm):
    t, hd = of.shape

    def body(d_ref, of_ref, ob_ref, z_ref, w_ref, do_ref, dz_ref, dw_ref):
        @pl.when(pl.program_id(0) == 0)
        def _():
            dw_ref[...] = jnp.zeros_like(dw_ref)

        dw = jnp.zeros((1, GDN_DIM), F32)
        for h in range(GDN_HEADS):
            cols = slice(h * GDN_DIM, (h + 1) * GDN_DIM)
            o = of_ref[:, cols] + ob_ref[:, cols]
            zv = z_ref[:, cols]
            dv = d_ref[:, cols]
            r = _rstd(o)
            sg = _sigmoid(zv)
            on = o * r * w_ref[...]
            dz_ref[:, cols] = (dv * on * (sg * (1.0 + zv * (1.0 - sg)))).astype(BF16)
            dxr, dwh = _rms_bwd(o, r, w_ref[...], dv * (zv * sg))
            do_ref[:, cols] = dxr
            dw = dw + dwh
        dw_ref[...] += dw

    row = pl.BlockSpec((tm, hd), lambda i: (i, 0))
    return pl.pallas_call(
        body, name="gdn_post_bwd", grid=(t // tm,),
        in_specs=[row, row, row, row, _resident((1, GDN_DIM))],
        out_specs=[row, row, pl.BlockSpec((1, GDN_DIM), lambda i: (0, 0))],
        out_shape=[jax.ShapeDtypeStruct((t, hd), F32), jax.ShapeDtypeStruct((t, hd), BF16),
                   jax.ShapeDtypeStruct((1, GDN_DIM), F32)],
        compiler_params=_params(("arbitrary",), VMEM_LIMIT),
    )(doa, of, ob, z, gw)


SWA_W = SWA_HEADS * SWA_DIM
QBLK = 128
KWIN = QBLK + 2 * RADIUS
WIN_OFFSETS = (0, RADIUS, 2 * RADIUS)


def _t5_bucket(rel):
    nb = REL_BUCKETS // 2
    bucket = (rel > 0).astype(np.int32) * nb
    n = np.abs(rel)
    max_exact = nb // 2
    large = max_exact + (np.log(np.maximum(n, 1) / max_exact)
                         / math.log(REL_MAX_DISTANCE / max_exact) * (nb - max_exact)).astype(np.int32)
    large = np.minimum(large, nb - 1)
    return (bucket + np.where(n < max_exact, n, large)).astype(np.int32)


def _band_tables(dilation):
    a = np.arange(QBLK)
    b = np.arange(KWIN)
    rel_a = np.stack([b[None, :] - w0 - a[:, None] for w0 in WIN_OFFSETS])
    rel_b = np.stack([a[None, :] + w0 - b[:, None] for w0 in WIN_OFFSETS])
    return ((_t5_bucket(rel_a * dilation), np.abs(rel_a) <= RADIUS),
            (_t5_bucket(rel_b * dilation), np.abs(rel_b) <= RADIUS))


def _bias_table(rel_bias, idx, valid):
    onehot = (jnp.arange(REL_BUCKETS, dtype=jnp.int32)[:, None] == jnp.asarray(idx.reshape(1, -1))).astype(F32)
    tab = jnp.dot(rel_bias.T, onehot, precision=HIGHEST)
    tab = jnp.where(jnp.asarray(valid.reshape(1, -1)), tab, NEG_BIG)
    tab = tab.reshape((SWA_HEADS,) + idx.shape)
    return jnp.transpose(tab, (1, 0, 2, 3)), onehot


def _head_mean(x2, bd_ref):
    return _dot_hi(x2, bd_ref[...])


VIEW_DILATIONS = tuple(d for _, d in PATTERNS if d > 1)


def _view_spec(tm, d):
    return pl.BlockSpec((tm // d, d * SWA_W), lambda i: (i, 0))


def _view_shape(t, d, dtype):
    return jax.ShapeDtypeStruct((t // d, d * SWA_W), dtype)


N_GROUPS = SWA_W // LANES


def _to_view(src_ref, idx, dst_ref, d, rows):
    for r in range(d):
        for g in range(N_GROUPS):
            cols = slice(r * SWA_W + g * LANES, r * SWA_W + (g + 1) * LANES)
            dst_ref[:, cols] = src_ref[idx, g, pl.ds(r, rows // d, stride=d), :].astype(dst_ref.dtype)


def _from_view(src_ref, dst_ref, idx, d, rows):
    for r in range(d):
        for g in range(N_GROUPS):
            cols = slice(r * SWA_W + g * LANES, r * SWA_W + (g + 1) * LANES)
            dst_ref[idx, g, pl.ds(r, rows // d, stride=d), :] = src_ref[:, cols]


def _swa_prep_fwd(qkvb, qw, kw, bd, tm):
    t = qkvb.shape[0]

    def body(x_ref, qw_ref, kw_ref, bd_ref, *rest):
        outs, sc = rest[:-1], rest[-1]
        for gidx in range(N_GROUPS):
            cols = slice(gidx * LANES, (gidx + 1) * LANES)
            xq = x_ref[:, cols]
            sc[0, gidx] = xq * lax.rsqrt(_head_mean(xq * xq, bd_ref) + EPS) * qw_ref[:, cols] * (SWA_DIM ** -0.5)
            xk = x_ref[:, SWA_W + gidx * LANES:SWA_W + (gidx + 1) * LANES]
            sc[1, gidx] = xk * lax.rsqrt(_head_mean(xk * xk, bd_ref) + EPS) * kw_ref[:, cols]
            sc[2, gidx] = x_ref[:, 2 * SWA_W + gidx * LANES:2 * SWA_W + (gidx + 1) * LANES]
            for i in range(3):
                outs[i][:, cols] = sc[i, gidx].astype(BF16)
        for i in range(3):
            for n, d in enumerate(VIEW_DILATIONS):
                _to_view(sc, i, outs[3 * (n + 1) + i], d, tm)

    return pl.pallas_call(
        body, name="swa_prep_fwd", grid=(t // tm,),
        in_specs=[pl.BlockSpec((tm, 3 * SWA_W), lambda i: (i, 0)), _resident((1, SWA_W)), _resident((1, SWA_W)),
                  _resident((LANES, LANES))],
        out_specs=[_view_spec(tm, d) for d in (1,) + VIEW_DILATIONS for _ in range(3)],
        out_shape=[_view_shape(t, d, BF16) for d in (1,) + VIEW_DILATIONS for _ in range(3)],
        scratch_shapes=[pltpu.VMEM((3, N_GROUPS, tm, LANES), F32)],
        compiler_params=_params(("arbitrary",), VMEM_LIMIT),
    )(qkvb, qw, kw, bd)


def _swa_prep_bwd(qkvb, qw, kw, bd, grads, tm):
    t = qkvb.shape[0]

    def body(x_ref, qw_ref, kw_ref, bd_ref, *rest):
        parts, (dx_ref, dqw_ref, dkw_ref, sc) = rest[:9], rest[9:]
        @pl.when(pl.program_id(0) == 0)
        def _():
            dqw_ref[...] = jnp.zeros_like(dqw_ref)
            dkw_ref[...] = jnp.zeros_like(dkw_ref)

        for i in range(3):
            for n, d in enumerate(VIEW_DILATIONS):
                _from_view(parts[3 * (n + 1) + i], sc, 2 * i + n, d, tm)
        for gidx in range(N_GROUPS):
            cols = slice(gidx * LANES, (gidx + 1) * LANES)
            for i, base, w_ref, dw_ref, scale in ((0, 0, qw_ref, dqw_ref, SWA_DIM ** -0.5),
                                                  (1, SWA_W, kw_ref, dkw_ref, 1.0)):
                xv = x_ref[:, base + gidx * LANES:base + (gidx + 1) * LANES]
                dy = (parts[i][:, cols] + sc[2 * i, gidx] + sc[2 * i + 1, gidx]) * scale
                r = lax.rsqrt(_head_mean(xv * xv, bd_ref) + EPS)
                xhat = xv * r
                dxh = dy * w_ref[:, cols]
                dx = r * (dxh - xhat * _head_mean(dxh * xhat, bd_ref))
                dx_ref[:, base + gidx * LANES:base + (gidx + 1) * LANES] = dx.astype(BF16)
                dw_ref[:, cols] += jnp.sum(dy * xhat, axis=0, keepdims=True)
            dx_ref[:, 2 * SWA_W + gidx * LANES:2 * SWA_W + (gidx + 1) * LANES] = (
                parts[2][:, cols] + sc[4, gidx] + sc[5, gidx]).astype(BF16)

    wrow = pl.BlockSpec((1, SWA_W), lambda i: (0, 0))
    return pl.pallas_call(
        body, name="swa_prep_bwd", grid=(t // tm,),
        in_specs=[pl.BlockSpec((tm, 3 * SWA_W), lambda i: (i, 0)), _resident((1, SWA_W)), _resident((1, SWA_W)),
                  _resident((LANES, LANES))] + [_view_spec(tm, d) for d in (1,) + VIEW_DILATIONS for _ in range(3)],
        out_specs=[pl.BlockSpec((tm, 3 * SWA_W), lambda i: (i, 0)), wrow, wrow],
        out_shape=[jax.ShapeDtypeStruct((t, 3 * SWA_W), BF16), jax.ShapeDtypeStruct((1, SWA_W), F32),
                   jax.ShapeDtypeStruct((1, SWA_W), F32)],
        scratch_shapes=[pltpu.VMEM((6, N_GROUPS, tm, LANES), F32)],
        compiler_params=_params(("arbitrary",), VMEM_LIMIT),
    )(qkvb, qw, kw, bd, *grads)


def _aligned(v, m):
    return v if isinstance(v, int) else pl.multiple_of(v, m)


BAND_GROUP = 2


def _band_loop(nsub, length, step):
    step([(0, 0)], 0)
    if nsub > 2:
        assert (nsub - 2) % BAND_GROUP == 0

        def inner(i, carry):
            s0 = 1 + i * BAND_GROUP
            step([(s0 + e, pl.multiple_of((s0 + e) * QBLK - RADIUS, RADIUS)) for e in range(BAND_GROUP)], 1)
            return carry
        lax.fori_loop(0, (nsub - 2) // BAND_GROUP, inner, 0)
    step([(nsub - 1, length - KWIN)], 2)


def _head_select(lane, a0, a1):
    return jnp.where(lane < SWA_DIM, a0, a1)


def _swa_fwd(qv, kv, vv, bias, dilation, name):
    length = qv.shape[0]
    nsub = length // QBLK
    assert nsub >= 2 and length % QBLK == 0

    def body(q_ref, k_ref, v_ref, b_ref, o_ref, l_ref):
        lane = lax.broadcasted_iota(jnp.int32, (QBLK, LANES), 1)

        def step(blocks, var):
            items = []
            for s, ws in blocks:
                rows = pl.ds(_aligned(s * QBLK, QBLK), QBLK)
                q, kk, vw = q_ref[rows, :], k_ref[pl.ds(ws, KWIN), :], v_ref[pl.ds(ws, KWIN), :]
                for hh in range(2):
                    items.append((hh, jnp.where((lane < SWA_DIM) == (hh == 0), q, jnp.zeros_like(q)), kk, vw))
            lgs = [_dot_nt(qh, kk) + b_ref[var, hh] for hh, qh, kk, _ in items]
            ms = [jnp.max(lg, axis=-1, keepdims=True) for lg in lgs]
            ps = [jnp.exp(lg - m) for lg, m in zip(lgs, ms)]
            dens = [jnp.sum(p, axis=-1, keepdims=True) for p in ps]
            pvs = [_dot(p, it[3]) for p, it in zip(ps, items)]
            for n, (s, _) in enumerate(blocks):
                rows = pl.ds(_aligned(s * QBLK, QBLK), QBLK)
                o0, o1 = (pvs[2 * n + hh] / dens[2 * n + hh] for hh in range(2))
                l0, l1 = (ms[2 * n + hh] + jnp.log(dens[2 * n + hh]) for hh in range(2))
                o_ref[rows, :] = _head_select(lane, o0, o1)
                l_ref[rows, :] = _head_select(lane, l0, l1)

        _band_loop(nsub, length, step)

    blk = pl.BlockSpec((length, LANES), lambda hp, r: (0, r * (SWA_W // LANES) + hp))
    shp = jax.ShapeDtypeStruct(qv.shape, F32)
    return pl.pallas_call(
        body, name=name, grid=(SWA_W // LANES, dilation),
        in_specs=[blk, blk, blk, pl.BlockSpec((3, 2, QBLK, KWIN), lambda hp, r: (0, hp, 0, 0))],
        out_specs=[blk, blk], out_shape=[shp, shp],
        compiler_params=_params(("arbitrary", "arbitrary"), VMEM_LIMIT),
    )(qv, kv, vv, bias)


def _swa_combine(os_, ls_, tm):
    t = os_[0].shape[0]

    def body(o0, o1, o2, l0, l1, l2, o_ref, ob_ref, la_ref, lb_ref, lc_ref, sc):
        for n, d in enumerate(VIEW_DILATIONS):
            _from_view((o1, o2)[n], sc, n, d, tm)
            _from_view((l1, l2)[n], sc, 2 + n, d, tm)
        for g in range(N_GROUPS):
            cols = slice(g * LANES, (g + 1) * LANES)
            la, lb, lc = l0[:, cols], sc[2, g], sc[3, g]
            m = jnp.maximum(jnp.maximum(la, lb), lc)
            tot = m + jnp.log(jnp.exp(la - m) + jnp.exp(lb - m) + jnp.exp(lc - m))
            o = jnp.exp(la - tot) * o0[:, cols] + jnp.exp(lb - tot) * sc[0, g] + jnp.exp(lc - tot) * sc[1, g]
            o_ref[:, cols] = o
            ob_ref[:, cols] = o.astype(BF16)
            la_ref[:, cols] = tot
            sc[4, g] = tot
        for n, d in enumerate(VIEW_DILATIONS):
            _to_view(sc, 4, (lb_ref, lc_ref)[n], d, tm)

    specs = [_view_spec(tm, d) for d in (1,) + VIEW_DILATIONS]
    return pl.pallas_call(
        body, name="swa_combine", grid=(t // tm,), in_specs=specs + specs, out_specs=[specs[0], specs[0]] + specs,
        out_shape=[jax.ShapeDtypeStruct((t, SWA_W), F32), jax.ShapeDtypeStruct((t, SWA_W), BF16)]
                  + [_view_shape(t, d, F32) for d in (1,) + VIEW_DILATIONS],
        scratch_shapes=[pltpu.VMEM((5, N_GROUPS, tm, LANES), F32)],
        compiler_params=_params(("arbitrary",), VMEM_LIMIT),
    )(*os_, *ls_)


def _swa_bwd_prep(do, o, bd, tm):
    t = do.shape[0]

    def body(d_ref, o_ref, bd_ref, dd1, dd4, dd16, db1, db4, db16, sc):
        for gidx in range(N_GROUPS):
            cols = slice(gidx * LANES, (gidx + 1) * LANES)
            dv = d_ref[:, cols]
            dd = _head_mean(dv * o_ref[:, cols], bd_ref) * float(SWA_DIM)
            sc[0, gidx] = dd
            sc[1, gidx] = dv
            dd1[:, cols] = dd
            db1[:, cols] = dv.astype(BF16)
        for n, d in enumerate(VIEW_DILATIONS):
            _to_view(sc, 0, (dd4, dd16)[n], d, tm)
            _to_view(sc, 1, (db4, db16)[n], d, tm)

    specs = [_view_spec(tm, d) for d in (1,) + VIEW_DILATIONS]
    return pl.pallas_call(
        body, name="swa_bwd_prep", grid=(t // tm,), in_specs=[specs[0], specs[0], _resident((LANES, LANES))],
        out_specs=specs + specs,
        out_shape=[_view_shape(t, d, F32) for d in (1,) + VIEW_DILATIONS]
                  + [_view_shape(t, d, BF16) for d in (1,) + VIEW_DILATIONS],
        scratch_shapes=[pltpu.VMEM((2, N_GROUPS, tm, LANES), F32)],
        compiler_params=_params(("arbitrary",), VMEM_LIMIT),
    )(do, o, bd)


def _swa_bwd(qv, kv, vv, dov, lv, ddv, bias_a, bias_b, dilation, name):
    length = qv.shape[0]
    nsub = length // QBLK
    single = pl.Buffered(1) if dilation == 1 else None

    def body(q_ref, k_ref, v_ref, do_ref, l_ref, dd_ref, ba_ref, bb_ref, dq_ref, dk_ref, dv_ref, db_ref):
        @pl.when(pl.program_id(1) == 0)
        def _():
            db_ref[...] = jnp.zeros_like(db_ref)

        lane = lax.broadcasted_iota(jnp.int32, (QBLK, LANES), 1)
        lanew = lax.broadcasted_iota(jnp.int32, (KWIN, LANES), 1)

        def step_q(blocks, var):
            items = []
            for s, ws in blocks:
                rows = pl.ds(_aligned(s * QBLK, QBLK), QBLK)
                win = pl.ds(ws, KWIN)
                q, dov_ = q_ref[rows, :], do_ref[rows, :]
                kk, vw = k_ref[win, :], v_ref[win, :]
                lse, dd = l_ref[rows, :], dd_ref[rows, :]
                for hh in range(2):
                    mine = (lane < SWA_DIM) == (hh == 0)
                    col = slice(hh * SWA_DIM, hh * SWA_DIM + 1)
                    items.append((hh, jnp.where(mine, q, jnp.zeros_like(q)), jnp.where(mine, dov_, jnp.zeros_like(dov_)),
                                  kk, vw, lse[:, col], dd[:, col]))
            lgs = [_dot_nt(qh, kk) + ba_ref[var, hh] for hh, qh, _, kk, _, _, _ in items]
            dps = [_dot_nt(doh, vw) for _, _, doh, _, vw, _, _ in items]
            dss = [jnp.exp(lg - it[5]) * (dp - it[6]) for lg, dp, it in zip(lgs, dps, items)]
            dqs = [_dot(ds, it[3]) for ds, it in zip(dss, items)]
            for n, (s, _) in enumerate(blocks):
                rows = pl.ds(_aligned(s * QBLK, QBLK), QBLK)
                dq_ref[rows, :] = _head_select(lane, dqs[2 * n], dqs[2 * n + 1])
            for hh in range(2):
                tot = dss[hh]
                for n in range(1, len(blocks)):
                    tot = tot + dss[2 * n + hh]
                db_ref[var, hh] += tot

        def step_k(blocks, var):
            items = []
            for s, ws in blocks:
                rows = pl.ds(_aligned(s * QBLK, QBLK), QBLK)
                win = pl.ds(ws, KWIN)
                kk, vw = k_ref[rows, :], v_ref[rows, :]
                qw, dow = q_ref[win, :], do_ref[win, :]
                lse, dd = l_ref[win, :], dd_ref[win, :]
                for hh in range(2):
                    mine = (lanew < SWA_DIM) == (hh == 0)
                    col = slice(hh * SWA_DIM, hh * SWA_DIM + 1)
                    items.append((hh, jnp.where(mine, qw, jnp.zeros_like(qw)), jnp.where(mine, dow, jnp.zeros_like(dow)),
                                  kk, vw, lse[:, col], dd[:, col], qw, dow))
            lgs = [_dot_nt(it[1], it[3]) + bb_ref[var, it[0]] for it in items]
            dps = [_dot_nt(it[2], it[4]) for it in items]
            ps = [jnp.exp(lg - it[5]) for lg, it in zip(lgs, items)]
            dss = [p * (dp - it[6]) for p, dp, it in zip(ps, dps, items)]
            dks = [_dot_tn(ds, it[7]) for ds, it in zip(dss, items)]
            dvs = [_dot_tn(p, it[8]) for p, it in zip(ps, items)]
            for n, (s, _) in enumerate(blocks):
                rows = pl.ds(_aligned(s * QBLK, QBLK), QBLK)
                dk_ref[rows, :] = _head_select(lane, dks[2 * n], dks[2 * n + 1])
                dv_ref[rows, :] = _head_select(lane, dvs[2 * n], dvs[2 * n + 1])

        _band_loop(nsub, length, step_q)
        _band_loop(nsub, length, step_k)

    imap = lambda hp, r: (0, r * (SWA_W // LANES) + hp)
    blk_in = pl.BlockSpec((length, LANES), imap, pipeline_mode=single)
    blk_out = pl.BlockSpec((length, LANES), imap)
    shp = jax.ShapeDtypeStruct(qv.shape, F32)
    return pl.pallas_call(
        body, name=name, grid=(SWA_W // LANES, dilation),
        in_specs=[blk_in] * 6 + [pl.BlockSpec((3, 2, QBLK, KWIN), lambda hp, r: (0, hp, 0, 0)),
                                 pl.BlockSpec((3, 2, KWIN, QBLK), lambda hp, r: (0, hp, 0, 0))],
        out_specs=[blk_out, blk_out, blk_out, pl.BlockSpec((3, 2, QBLK, KWIN), lambda hp, r: (0, hp, 0, 0))],
        out_shape=[shp, shp, shp, jax.ShapeDtypeStruct((3, SWA_HEADS, QBLK, KWIN), F32)],
        compiler_params=_params(("arbitrary", "arbitrary"), VMEM_LIMIT),
    )(qv, kv, vv, dov, lv, ddv, bias_a, bias_b)


def _bias_grad(ds2, onehot, tk):
    n = ds2.shape[1]
    nk = n // tk

    def body(a_ref, b_ref, o_ref):
        @pl.when(pl.program_id(0) == 0)
        def _():
            o_ref[...] = jnp.zeros_like(o_ref)

        o_ref[...] += lax.dot_general(a_ref[...], b_ref[...], (((1,), (1,)), ((), ())), precision=HIGHEST,
                                      preferred_element_type=F32)

    return pl.pallas_call(
        body, name="bias_grad", grid=(nk,),
        in_specs=[pl.BlockSpec((SWA_HEADS, tk), lambda k: (0, k)), pl.BlockSpec((REL_BUCKETS, tk), lambda k: (0, k))],
        out_specs=pl.BlockSpec((SWA_HEADS, REL_BUCKETS), lambda k: (0, 0)),
        out_shape=jax.ShapeDtypeStruct((SWA_HEADS, REL_BUCKETS), F32),
        compiler_params=_params(("arbitrary",), VMEM_LIMIT),
    )(ds2, onehot)


def _swa_branch_fwd(qkvb, qw_t, kw_t, rel_bias, bd, tm):
    qkv = _swa_prep_fwd(qkvb, qw_t, kw_t, bd, tm)
    os_, ls_, tabs = [], [], []
    for n, (_, d) in enumerate(PATTERNS):
        (idx_a, val_a), (idx_b, val_b) = _band_tables(d)
        bias_a, onehot_a = _bias_table(rel_bias, idx_a, val_a)
        bias_b, _ = _bias_table(rel_bias, idx_b, val_b)
        o_p, l_p = _swa_fwd(*qkv[3 * n:3 * n + 3], bias_a, d, f"swa_fwd_d{d}")
        os_.append(o_p)
        ls_.append(l_p)
        tabs.append((bias_a, bias_b, onehot_a))
    o, o16, *lses = _swa_combine(os_, ls_, tm)
    return o, o16, (qkv, lses, tabs)


def _swa_branch_bwd(do, o, saved, qkvb, qw_t, kw_t, bd, tm):
    qkv, lses, tabs = saved
    prep = _swa_bwd_prep(do, o, bd, tm)
    grads, dss, ohs = [], [], []
    for n, ((_, d), (bias_a, bias_b, onehot_a)) in enumerate(zip(PATTERNS, tabs)):
        dq, dk, dv, ds = _swa_bwd(*qkv[3 * n:3 * n + 3], prep[3 + n], lses[n], prep[n],
                                  bias_a, bias_b, d, f"swa_bwd_d{d}")
        grads += [dq, dk, dv]
        dss.append(jnp.transpose(ds, (1, 0, 2, 3)).reshape(SWA_HEADS, -1))
        ohs.append(onehot_a)
    dqkvb, dqw, dkw = _swa_prep_bwd(qkvb, qw_t, kw_t, bd, grads, tm)
    dbias = _bias_grad(jnp.concatenate(dss, axis=1), jnp.concatenate(ohs, axis=1), 8192)
    fold = lambda w: jnp.sum(w.reshape(SWA_HEADS, SWA_DIM), axis=0)
    return dqkvb, fold(dqw), fold(dkw), dbias.T


def _mesh_pos():
    return lax.axis_index("x"), lax.axis_index("y"), lax.axis_index("c")


def _other_chips(x, y):
    return [(1 - x, y), (x, 1 - y), (1 - x, 1 - y)]


def _remote(src, dst, send_sem, recv_sem, device):
    return pltpu.make_async_remote_copy(src_ref=src, dst_ref=dst, send_sem=send_sem, recv_sem=recv_sem,
                                        device_id=device, device_id_type=MESH)


def _all_gather(xs):
    n = len(xs)

    def body(*refs):
        ins, outs = refs[:n], refs[n:2 * n]
        send_sems, recv_sems = refs[2 * n:]
        x, y, c = _mesh_pos()
        me = 2 * x + y
        chips = _other_chips(x, y)
        halves = []
        sends = []
        for a in range(n):
            h = ins[a].shape[0] // 2
            mine, other = pl.ds(c * h, h), pl.ds((1 - c) * h, h)
            halves.append((mine, other))
            for j, chip in enumerate(chips):
                cp = _remote(ins[a].at[mine], outs[a].at[me, mine], send_sems.at[a, j], recv_sems.at[a, j], (*chip, c))
                cp.start()
                sends.append(cp)
        for a in range(n):
            mine, _ = halves[a]
            for j, chip in enumerate(chips):
                src = 2 * chip[0] + chip[1]
                landed = outs[a].at[src, mine]
                _remote(landed, landed, send_sems.at[a, j], recv_sems.at[a, j], (x, y, c)).wait_recv()
                fwd = _remote(landed, landed, send_sems.at[a, 3 + j], recv_sems.at[a, 3 + j], (x, y, 1 - c))
                fwd.start()
                sends.append(fwd)
        for a in range(n):
            _, other = halves[a]
            for j, chip in enumerate(chips):
                src = 2 * chip[0] + chip[1]
                landed = outs[a].at[src, other]
                _remote(landed, landed, send_sems.at[a, 3 + j], recv_sems.at[a, 3 + j], (x, y, c)).wait_recv()
        for cp in sends:
            cp.wait_send()

    outs = pl.pallas_call(
        body, name="all_gather_weights",
        in_specs=[ANY] * n, out_specs=[ANY] * n,
        out_shape=[jax.ShapeDtypeStruct((N_SHARDS,) + a.shape, a.dtype) for a in xs],
        scratch_shapes=[pltpu.SemaphoreType.DMA((n, 6)), pltpu.SemaphoreType.DMA((n, 6))],
    )(*xs)
    me = 2 * lax.axis_index("x") + lax.axis_index("y")
    return [lax.dynamic_update_slice_in_dim(o, a[None], me, 0) for o, a in zip(outs, xs)]


def _rs_pair(gs):
    n = len(gs)

    def body(*refs):
        ins, lands = refs[:n], refs[n:2 * n]
        send_sems, recv_sems = refs[2 * n:]
        x, y, c = _mesh_pos()
        cps = []
        for a in range(n):
            h = ins[a].shape[1] // 2
            cp = _remote(ins[a].at[:, pl.ds((1 - c) * h, h), :], lands[a], send_sems.at[a], recv_sems.at[a],
                         (x, y, 1 - c))
            cp.start()
            cps.append(cp)
        for cp in cps:
            cp.wait()

    half = [jax.ShapeDtypeStruct((N_SHARDS, g.shape[1] // 2, g.shape[2]), g.dtype) for g in gs]
    lands = pl.pallas_call(
        body, name="rs_pair", in_specs=[ANY] * n, out_specs=[ANY] * n, out_shape=half,
        scratch_shapes=[pltpu.SemaphoreType.DMA((n,)), pltpu.SemaphoreType.DMA((n,))],
    )(*gs)
    c = lax.axis_index("c")
    owns = [lax.dynamic_slice_in_dim(g, c * (g.shape[1] // 2), g.shape[1] // 2, 1) for g in gs]
    return owns + list(lands)


def _rs_chips(ss):
    n = len(ss)

    def body(*refs):
        ins, outs = refs[:n], refs[n:2 * n]
        send_sems, recv_sems = refs[2 * n:]
        x, y, c = _mesh_pos()
        me = 2 * x + y
        chips = _other_chips(x, y)
        cps = []
        for a in range(n):
            for j, chip in enumerate(chips):
                dst_chip = 2 * chip[0] + chip[1]
                cp = _remote(ins[a].at[dst_chip], outs[a].at[me], send_sems.at[a, j], recv_sems.at[a, j], (*chip, c))
                cp.start()
                cps.append(cp)
        for a in range(n):
            for j, chip in enumerate(chips):
                src = 2 * chip[0] + chip[1]
                _remote(outs[a].at[src], outs[a].at[src], send_sems.at[a, j], recv_sems.at[a, j], (x, y, c)).wait_recv()
        for cp in cps:
            cp.wait_send()

    outs = pl.pallas_call(
        body, name="rs_chips", in_specs=[ANY] * n, out_specs=[ANY] * n,
        out_shape=[jax.ShapeDtypeStruct(s.shape, s.dtype) for s in ss],
        scratch_shapes=[pltpu.SemaphoreType.DMA((n, 3)), pltpu.SemaphoreType.DMA((n, 3))],
    )(*ss)
    me = 2 * lax.axis_index("x") + lax.axis_index("y")
    return [lax.dynamic_update_slice_in_dim(o, lax.dynamic_slice_in_dim(s, me, 1, 0), me, 0) for o, s in zip(outs, ss)]


def _rs_join(fs):
    n = len(fs)

    def body(*refs):
        ins, outs = refs[:n], refs[n:2 * n]
        send_sems, recv_sems = refs[2 * n:]
        x, y, c = _mesh_pos()
        cps = []
        for a in range(n):
            h = ins[a].shape[0]
            cp = _remote(ins[a], outs[a].at[pl.ds(c * h, h)], send_sems.at[a], recv_sems.at[a], (x, y, 1 - c))
            cp.start()
            cps.append(cp)
        for cp in cps:
            cp.wait()

    outs = pl.pallas_call(
        body, name="rs_join", in_specs=[ANY] * n, out_specs=[ANY] * n,
        out_shape=[jax.ShapeDtypeStruct((2 * f.shape[0], f.shape[1]), f.dtype) for f in fs],
        scratch_shapes=[pltpu.SemaphoreType.DMA((n,)), pltpu.SemaphoreType.DMA((n,))],
    )(*fs)
    c = lax.axis_index("c")
    return [lax.dynamic_update_slice_in_dim(o, f, c * f.shape[0], 0) for o, f in zip(outs, fs)]


def _gather_exchange(xs):
    def start(cin, cout, send_sems, recv_sems):
        x, y, c = _mesh_pos()
        me = 2 * x + y
        for a, (src, dst) in enumerate(zip(cin, cout)):
            h = src.shape[0] // 2
            mine = pl.ds(c * h, h)
            for j, chip in enumerate(_other_chips(x, y)):
                _remote(src.at[mine], dst.at[me, mine], send_sems.at[a, j], recv_sems.at[a, j], (*chip, c)).start()

    def finish(cin, cout, send_sems, recv_sems):
        x, y, c = _mesh_pos()
        for a, dst in enumerate(cout):
            h = dst.shape[1] // 2
            for j, chip in enumerate(_other_chips(x, y)):
                landed = dst.at[2 * chip[0] + chip[1], pl.ds(c * h, h)]
                _remote(landed, landed, send_sems.at[a, j], recv_sems.at[a, j], (x, y, c)).wait()

    return _Exchange(tuple(xs), tuple(jax.ShapeDtypeStruct((N_SHARDS,) + a.shape, a.dtype) for a in xs), start, finish)


def _gather_forward(gs, xs):
    n = len(gs)

    def body(*refs):
        outs = refs[n:2 * n]
        send_sems, recv_sems = refs[2 * n:]
        x, y, c = _mesh_pos()
        chips = _other_chips(x, y)
        cps = []
        for a in range(n):
            h = outs[a].shape[1] // 2
            for j, chip in enumerate(chips):
                landed = outs[a].at[2 * chip[0] + chip[1], pl.ds(c * h, h)]
                cp = _remote(landed, landed, send_sems.at[a, j], recv_sems.at[a, j], (x, y, 1 - c))
                cp.start()
                cps.append(cp)
        for a in range(n):
            h = outs[a].shape[1] // 2
            for j, chip in enumerate(chips):
                other = outs[a].at[2 * chip[0] + chip[1], pl.ds((1 - c) * h, h)]
                _remote(other, other, send_sems.at[a, j], recv_sems.at[a, j], (x, y, c)).wait_recv()
        for cp in cps:
            cp.wait_send()

    outs = pl.pallas_call(
        body, name="gather_forward", in_specs=[ANY] * n, out_specs=[ANY] * n,
        out_shape=[jax.ShapeDtypeStruct(g.shape, g.dtype) for g in gs],
        input_output_aliases={i: i for i in range(n)},
        scratch_shapes=[pltpu.SemaphoreType.DMA((n, 3)), pltpu.SemaphoreType.DMA((n, 3))],
    )(*gs)
    me = 2 * lax.axis_index("x") + lax.axis_index("y")
    return [lax.dynamic_update_slice_in_dim(o, a[None], me, 0) for o, a in zip(outs, xs)]


def _scatter_exchange(ss):
    def start(cin, cout, send_sems, recv_sems):
        x, y, c = _mesh_pos()
        me = 2 * x + y
        for a, (src, dst) in enumerate(zip(cin, cout)):
            for j, chip in enumerate(_other_chips(x, y)):
                _remote(src.at[2 * chip[0] + chip[1]], dst.at[me], send_sems.at[a, j], recv_sems.at[a, j],
                        (*chip, c)).start()

    def finish(cin, cout, send_sems, recv_sems):
        x, y, c = _mesh_pos()
        for a, dst in enumerate(cout):
            for j, chip in enumerate(_other_chips(x, y)):
                slot = dst.at[2 * chip[0] + chip[1]]
                _remote(slot, slot, send_sems.at[a, j], recv_sems.at[a, j], (x, y, c)).wait()

    return _Exchange(tuple(ss), tuple(jax.ShapeDtypeStruct(s.shape, s.dtype) for s in ss), start, finish)


def _own_slots(slots, ss):
    me = 2 * lax.axis_index("x") + lax.axis_index("y")
    return [lax.dynamic_update_slice_in_dim(o, lax.dynamic_slice_in_dim(s, me, 1, 0), me, 0) for o, s in zip(slots, ss)]


def _add_pair(a, b, name):
    nj, h, c = a.shape

    def body(a_ref, b_ref, o_ref):
        o_ref[...] = (a_ref[...].astype(F32) + b_ref[...].astype(F32)).astype(BF16)

    blk = pl.BlockSpec((1, h, c), lambda j: (j, 0, 0))
    return pl.pallas_call(body, name=name, grid=(nj,), in_specs=[blk, blk], out_specs=blk,
                          out_shape=jax.ShapeDtypeStruct(a.shape, BF16),
                          compiler_params=_params(("arbitrary",), VMEM_LIMIT))(a, b)


def _sum_slots(l2, name):
    nj, h, c = l2.shape
    th = h // 2 if h % 32 == 0 else h

    def body(i_ref, o_ref):
        acc = i_ref[0].astype(F32)
        for s in range(1, nj):
            acc = acc + i_ref[s].astype(F32)
        o_ref[...] = acc

    return pl.pallas_call(body, name=name, grid=(h // th,),
                          in_specs=[pl.BlockSpec((nj, th, c), lambda i: (0, i, 0))],
                          out_specs=pl.BlockSpec((th, c), lambda i: (i, 0)),
                          out_shape=jax.ShapeDtypeStruct((h, c), F32),
                          compiler_params=_params(("arbitrary",), VMEM_LIMIT))(l2)


def _all_reduce_small(p):
    r = p.shape[0]

    def body(p_ref, o_ref, buf, send_sems, recv_sems):
        x, y, c = _mesh_pos()
        me = 4 * x + 2 * y + c
        buf[me] = p_ref[...]
        cps = []
        k = 0
        for fx in range(2):
            for fy in range(2):
                for fc in range(2):
                    if fx + fy + fc == 0:
                        continue
                    peer = (1 - x if fx else x, 1 - y if fy else y, 1 - c if fc else c)
                    peer_id = 4 * peer[0] + 2 * peer[1] + peer[2]
                    cp = _remote(p_ref, buf.at[me], send_sems.at[k], recv_sems.at[k], peer)
                    cp.start()
                    cps.append((cp, peer_id, k))
                    k += 1
        for cp, peer_id, k in cps:
            _remote(p_ref, buf.at[peer_id], send_sems.at[k], recv_sems.at[k], (x, y, c)).wait_recv()
        for cp, _, _ in cps:
            cp.wait_send()
        acc = buf[0]
        for s in range(1, 8):
            acc = acc + buf[s]
        o_ref[...] = acc

    vm = pl.BlockSpec(memory_space=pltpu.VMEM)
    return pl.pallas_call(
        body, name="all_reduce_small", in_specs=[vm], out_specs=vm,
        out_shape=jax.ShapeDtypeStruct(p.shape, F32),
        scratch_shapes=[pltpu.VMEM((8, r, LANES), F32), pltpu.SemaphoreType.DMA((7,)), pltpu.SemaphoreType.DMA((7,))],
    )(p)


def _adamw(w, g, m, v, name):
    r, c = w.shape
    tr = max(d for d in range(8, min(r, 256) + 1, 8) if r % d == 0)
    c1 = 1.0 / (1.0 - ADAM_B1 ** ADAM_STEP)
    c2 = 1.0 / (1.0 - ADAM_B2 ** ADAM_STEP)

    def body(w_ref, g_ref, m_ref, v_ref, d_ref, nm_ref, nv_ref):
        gv = g_ref[...]
        nm = ADAM_B1 * m_ref[...] + (1.0 - ADAM_B1) * gv
        nv = ADAM_B2 * v_ref[...] + (1.0 - ADAM_B2) * (gv * gv)
        d_ref[...] = -ADAM_LR * ((nm * c1) / (jnp.sqrt(nv * c2) + ADAM_EPS) + ADAM_WD * w_ref[...])
        nm_ref[...] = nm
        nv_ref[...] = nv

    blk = pl.BlockSpec((tr, c), lambda i: (i, 0))
    shp = jax.ShapeDtypeStruct((r, c), F32)
    return pl.pallas_call(body, name=name, grid=(r // tr,), in_specs=[blk] * 4, out_specs=[blk] * 3,
                          out_shape=[shp, shp, shp], compiler_params=_params(("arbitrary",), VMEM_LIMIT))(w, g, m, v)


PACK_UNIT = 8 * LANES


def _pack(arrs):
    parts = []
    for a in arrs:
        f = a.reshape(-1).astype(F32)
        parts.append(jnp.pad(f, (0, (-f.shape[0]) % PACK_UNIT)).reshape(-1, LANES))
    return jnp.concatenate(parts, axis=0)


def _unpack(m, shapes):
    outs, row = [], 0
    for s in shapes:
        n = int(np.prod(s))
        rows = -(-n // PACK_UNIT) * 8
        outs.append(m[row:row + rows].reshape(-1)[:n].reshape(s))
        row += rows
    return outs


WEIGHTS = ["ffn1_norm", "ffn1_w_gate", "ffn1_w_up", "ffn1_w_down", "mix_norm", "w_in", "conv_w", "a_log", "dt_bias",
           "gdn_norm_w", "q_norm_w", "k_norm_w", "rel_bias", "w_out", "ffn2_norm", "ffn2_w_gate", "ffn2_w_up",
           "ffn2_w_down", "final_norm"]
BIG = ["ffn1_w_gate", "ffn1_w_up", "ffn1_w_down", "w_in", "w_out", "ffn2_w_gate", "ffn2_w_up", "ffn2_w_down"]
SMALL = [n for n in WEIGHTS if n not in BIG]
N_IN_COLS = 3600
TM = 256
TE = 512
TK = 2048


def kernel(x, ffn1_norm, ffn1_w_gate, ffn1_w_up, ffn1_w_down, mix_norm, w_in, conv_w, a_log, dt_bias, gdn_norm_w, q_norm_w, k_norm_w, rel_bias, w_out, ffn2_norm, ffn2_w_gate, ffn2_w_up, ffn2_w_down, final_norm, loss_target, m_ffn1_norm, m_ffn1_w_gate, m_ffn1_w_up, m_ffn1_w_down, m_mix_norm, m_w_in, m_conv_w, m_a_log, m_dt_bias, m_gdn_norm_w, m_q_norm_w, m_k_norm_w, m_rel_bias, m_w_out, m_ffn2_norm, m_ffn2_w_gate, m_ffn2_w_up, m_ffn2_w_down, m_final_norm, v_ffn1_norm, v_ffn1_w_gate, v_ffn1_w_up, v_ffn1_w_down, v_mix_norm, v_w_in, v_conv_w, v_a_log, v_dt_bias, v_gdn_norm_w, v_q_norm_w, v_k_norm_w, v_rel_bias, v_w_out, v_ffn2_norm, v_ffn2_w_gate, v_ffn2_w_up, v_ffn2_w_down, v_final_norm):
    p = dict(locals())
    xs, target = x[0], loss_target[0]
    t, d = xs.shape
    nc = t // CHUNK
    tk = min(TK, t)
    me = 2 * lax.axis_index("x") + lax.axis_index("y")

    first = ["ffn1_w_gate", "ffn1_w_up", "ffn1_w_down"]
    later = [n for n in BIG if n not in first] + ["conv_w"]
    shards = {n: p[n][0].astype(BF16) for n in BIG}
    shards["conv_w"] = conv_w[0]
    gw = dict(zip(first, _all_gather([shards[n] for n in first])))
    f1 = (gw["ffn1_w_gate"], gw["ffn1_w_up"], gw["ffn1_w_down"])
    (x1, xn1, g1, u1), landed = _ffn_fwd(xs, ffn1_norm, *f1, TM, "ffn1_fwd",
                                         exchange=_gather_exchange([shards[n] for n in later]))
    gw.update(zip(later, _gather_forward(landed, [shards[n] for n in later])))
    w_in_full = jnp.transpose(gw["w_in"], (1, 0, 2)).reshape(d, N_IN_COLS)
    wp = jnp.concatenate([w_in_full[:, :2048], jnp.pad(w_in_full[:, 2048:2064], ((0, 0), (0, LANES - 16))),
                          w_in_full[:, 2064:]], axis=1)
    w_out_full = gw["w_out"].reshape(d, d)
    conv_rows = conv_w.shape[1]
    cw = jnp.pad(gw["conv_w"].reshape(N_SHARDS * conv_rows, CONV_TAPS).T, ((0, 8 - CONV_TAPS), (0, 0)))
    gp = jnp.pad(jnp.stack([a_log.reshape(8), dt_bias.reshape(8)]), ((0, 6), (0, LANES - 8)))
    gdn_w = gdn_norm_w.reshape(1, GDN_DIM)
    qw_t = jnp.tile(q_norm_w.reshape(1, SWA_DIM), (1, SWA_HEADS))
    kw_t = jnp.tile(k_norm_w.reshape(1, SWA_DIM), (1, SWA_HEADS))
    bd = jnp.asarray(np.kron(np.eye(2), np.full((SWA_DIM, SWA_DIM), 1.0 / SWA_DIM)), F32)
    f2 = (gw["ffn2_w_gate"], gw["ffn2_w_up"], gw["ffn2_w_down"])

    hn, qkva, z, ab, qkvb = _mix_in_fwd(x1, mix_norm, wp, TM)
    qkvc, gb = _gdn_prep_fwd(qkva, cw, ab, gp, TM)
    gbt = jnp.transpose(gb[:, :16].reshape(nc, CHUNK, 16), (0, 2, 1))
    o_f, o_b, gdn_saved = _gdn_fwd(qkvc, gb, gbt)
    oa = _gdn_post_fwd(o_f, o_b, z, gdn_w, TE)
    o_swa, o_swa16, swa_saved = _swa_branch_fwd(qkvb, qw_t, kw_t, rel_bias, bd, TE)
    x2 = _mix_out_fwd(x1, oa, o_swa, w_out_full, TM)
    (x3, xn2, g2, u2), _ = _ffn_fwd(x2, ffn2_norm, *f2, TM, "ffn2_fwd")
    dx3, loss_part, d_final = _final_loss(x3, final_norm, target, TE)

    def pair_sums(partials, tag):
        pair = _rs_pair(partials)
        k = len(partials)
        return [_add_pair(pair[i], pair[k + i], f"rs_add_{tag}{i}") for i in range(k)]

    (dx2, dyh2, dg2, du2, h2, d_nw2), _ = _ffn_bwd_dx(dx3, x2, ffn2_norm, g2, u2, *f2, TM, "ffn2_bwd_dx")
    dwg2 = _matmul_tn(xn2, dg2, tk, "ffn2_dwg")
    dwu2 = _matmul_tn(xn2, du2, tk, "ffn2_dwu")
    dwd2 = _matmul_tn(h2, dyh2, tk, "ffn2_dwd")
    sums_f2 = pair_sums([dwg2, dwu2, dwd2], "a")
    doa, dob, dx2b = _mix_out_bwd(dx2, w_out_full, TM)
    dwo = jnp.concatenate([_matmul_tn(oa, dx2b, tk, "w_out_dw_a")[0], _matmul_tn(o_swa16, dx2b, tk, "w_out_dw_b")[0]],
                          axis=0).reshape(N_SHARDS, d // N_SHARDS, d)
    do_g, dz, d_gdnw = _gdn_post_bwd(doa, o_f, o_b, z, gdn_w, TE)
    (dqkvc, dgates), slots_f2 = _gdn_bwd(qkvc, gb, gbt, do_g, gdn_saved, exchange=_scatter_exchange(sums_f2))
    dqkva, dab, dcw, dgp = _gdn_prep_bwd(qkva, cw, ab, gp, dqkvc, dgates, TM)
    dqkvb, d_qw, d_kw, d_rel = _swa_branch_bwd(dob, o_swa, swa_saved, qkvb, qw_t, kw_t, bd, TE)
    dpieces = (dqkva, dz, dab, dqkvb)
    dx1, d_mixnw = _mix_in_bwd_dx(dx2, x1, mix_norm, dpieces, wp, TM)
    dwp = [_matmul_tn(hn, dp, tk, f"w_in_dw_{i}")[0] for i, dp in enumerate(dpieces)]
    dw_in = jnp.concatenate([dwp[0], dwp[1], dwp[2][:, :16], dwp[3]], axis=1)
    dw_in = jnp.transpose(dw_in.reshape(d, N_SHARDS, N_IN_COLS // N_SHARDS), (1, 0, 2))
    sums_mix = pair_sums([dw_in, dwo], "b")
    (gx, dyh1, dg1, du1, h1, d_nw1), slots_mix = _ffn_bwd_dx(dx1, xs, ffn1_norm, g1, u1, *f1, TM, "ffn1_bwd_dx",
                                                              exchange=_scatter_exchange(sums_mix))
    dwg1 = _matmul_tn(xn1, dg1, tk, "ffn1_dwg")
    dwu1 = _matmul_tn(xn1, du1, tk, "ffn1_dwu")
    dwd1 = _matmul_tn(h1, dyh1, tk, "ffn1_dwd")
    slots_f1 = _rs_chips(pair_sums([dwg1, dwu1, dwd1], "c"))
    slots = slots_f1 + _own_slots(slots_mix, sums_mix) + _own_slots(slots_f2, sums_f2)
    halves = [_sum_slots(s, f"rs_sum_{i}") for i, s in enumerate(slots)]
    g_big = dict(zip(BIG, _rs_join(halves)))

    small_partial = {"ffn1_norm": d_nw1, "mix_norm": d_mixnw, "a_log": dgp[0, 0:8], "dt_bias": dgp[1, 0:8],
                     "gdn_norm_w": d_gdnw, "q_norm_w": d_qw, "k_norm_w": d_kw, "rel_bias": d_rel,
                     "ffn2_norm": d_nw2, "final_norm": d_final, "conv_w": dcw[0:CONV_TAPS].T}
    red = _all_reduce_small(_pack([small_partial[n] for n in SMALL] + [loss_part[0, 0:1]]))
    full_shapes = [p[n].shape if n != "conv_w" else (N_SHARDS * conv_rows, CONV_TAPS) for n in SMALL]
    red_parts = _unpack(red, full_shapes + [(1,)])
    loss = red_parts[-1].reshape(())
    g_small = dict(zip(SMALL, red_parts[:-1]))
    g_small["conv_w"] = lax.dynamic_slice_in_dim(g_small["conv_w"], me * conv_rows, conv_rows, 0).reshape(conv_w.shape)

    grads, deltas, new_m, new_v = {}, {}, {}, {}
    for n in BIG:
        grads[n] = g_big[n][None]
        dl, nm, nv = _adamw(p[n][0], g_big[n], p["m_" + n][0], p["v_" + n][0], "adamw_" + n)
        deltas[n], new_m[n], new_v[n] = dl[None], nm[None], nv[None]
    packed = [_pack([src[n] for n in SMALL]) for src in
              ({n: p[n] for n in SMALL}, g_small, {n: p["m_" + n] for n in SMALL}, {n: p["v_" + n] for n in SMALL})]
    small_shapes = [p[n].shape for n in SMALL]
    for dst, arr in zip((deltas, new_m, new_v), _adamw(*packed, "adamw_small")):
        dst.update(zip(SMALL, _unpack(arr, small_shapes)))
    grads.update(g_small)

    return (loss, gx[None], *[grads[n] for n in WEIGHTS], *[deltas[n] for n in WEIGHTS],
            *[new_m[n] for n in WEIGHTS], *[new_v[n] for n in WEIGHTS])
```

```python
import math
from typing import Callable, NamedTuple

import numpy as np
import jax
import jax.numpy as jnp
from jax import lax
from jax.experimental import pallas as pl
from jax.experimental.pallas import tpu as pltpu

F32 = jnp.float32
BF16 = jnp.bfloat16
HIGHEST = lax.Precision.HIGHEST
MESH = pl.DeviceIdType.MESH

EPS = 1e-6
NEG_BIG = -1e30
GDN_HEADS = 4
GDN_DIM = 128
CHUNK = 64
SWA_HEADS = 8
SWA_DIM = 64
PATTERNS = ((128, 1), (512, 4), (2048, 16))
RADIUS = 64
REL_BUCKETS = 32
REL_MAX_DISTANCE = 1024
CONV_TAPS = 5
N_SHARDS = 4
LANES = 128
VMEM_LIMIT = 56 * 1024 * 1024

ADAM_LR, ADAM_B1, ADAM_B2, ADAM_EPS, ADAM_WD, ADAM_STEP = 0.001, 0.9, 0.999, 1e-08, 0.01, 10


def _params(sem=None, vmem=None):
    return pltpu.CompilerParams(dimension_semantics=sem, vmem_limit_bytes=vmem)


def _resident(shape):
    nd = len(shape)
    return pl.BlockSpec(shape, lambda *_: (0,) * nd, pipeline_mode=pl.Buffered(1))


ANY = pl.BlockSpec(memory_space=pl.ANY)


class _Exchange(NamedTuple):
    arrays: tuple
    out_shape: tuple
    start: Callable
    finish: Callable


def _grid_call(body, name, nsteps, in_specs, out_specs, out_shape, operands, scratch=(), exchange=None):
    params = _params(("arbitrary",), VMEM_LIMIT)
    if exchange is None:
        res = pl.pallas_call(body, name=name, grid=(nsteps,), in_specs=list(in_specs), out_specs=list(out_specs),
                             out_shape=list(out_shape), scratch_shapes=list(scratch), compiler_params=params)(*operands)
        return list(res), []
    n_in, n_out, k, n_scr = len(in_specs), len(out_specs), len(exchange.arrays), len(scratch)

    def wrapped(*refs):
        ins, cin = refs[:n_in], refs[n_in:n_in + k]
        outs, cout = refs[n_in + k:n_in + k + n_out], refs[n_in + k + n_out:n_in + 2 * k + n_out]
        rest = refs[n_in + 2 * k + n_out:]
        scr, (send_sems, recv_sems) = rest[:n_scr], rest[n_scr:]

        @pl.when(pl.program_id(0) == 0)
        def _():
            exchange.start(cin, cout, send_sems, recv_sems)

        body(*ins, *outs, *scr)

        @pl.when(pl.program_id(0) == nsteps - 1)
        def _():
            exchange.finish(cin, cout, send_sems, recv_sems)

    res = pl.pallas_call(
        wrapped, name=name, grid=(nsteps,), in_specs=list(in_specs) + [ANY] * k, out_specs=list(out_specs) + [ANY] * k,
        out_shape=list(out_shape) + list(exchange.out_shape),
        scratch_shapes=list(scratch) + [pltpu.SemaphoreType.DMA((k, 3)), pltpu.SemaphoreType.DMA((k, 3))],
        compiler_params=params)(*operands, *exchange.arrays)
    return list(res[:n_out]), list(res[n_out:])


def _dot(a, b):
    return jnp.dot(a.astype(BF16), b.astype(BF16), preferred_element_type=F32)


def _dot_nt(a, b):
    return lax.dot_general(a.astype(BF16), b.astype(BF16), (((1,), (1,)), ((), ())), preferred_element_type=F32)


def _dot_tn(a, b):
    return lax.dot_general(a.astype(BF16), b.astype(BF16), (((0,), (0,)), ((), ())), preferred_element_type=F32)


def _dot_hi(a, b):
    return jnp.dot(a, b, preferred_element_type=F32, precision=HIGHEST)


def _sigmoid(x):
    return 1.0 / (1.0 + jnp.exp(-x))


def _rstd(xf):
    return lax.rsqrt(jnp.mean(xf * xf, axis=-1, keepdims=True) + EPS)


def _rms_bwd(xf, r, nw, dxn):
    xhat = xf * r
    dxh = dxn * nw
    dx = r * (dxh - xhat * jnp.mean(dxh * xhat, axis=-1, keepdims=True))
    return dx, jnp.sum(dxn * xhat, axis=0, keepdims=True)


def _ffn_fwd(x, nw, wg, wu, wd, tm, name, exchange=None):
    t, d = x.shape
    nj, _, fs = wg.shape

    def body(x_ref, nw_ref, wg_ref, wu_ref, wd_ref, y_ref, xn_ref, g_ref, u_ref):
        xf = x_ref[...]
        xn = (xf * _rstd(xf) * nw_ref[...]).astype(BF16)
        xn_ref[...] = xn
        acc = jnp.zeros((tm, d), F32)
        for j in range(nj):
            g = jnp.dot(xn, wg_ref[j], preferred_element_type=F32)
            u = jnp.dot(xn, wu_ref[j], preferred_element_type=F32)
            h = (g * _sigmoid(g) * u).astype(BF16)
            acc = acc + jnp.dot(h, wd_ref[j], preferred_element_type=F32)
            g_ref[j] = g.astype(BF16)
            u_ref[j] = u.astype(BF16)
        y_ref[...] = xf + 0.5 * acc

    row = pl.BlockSpec((tm, d), lambda i: (i, 0))
    act = pl.BlockSpec((nj, tm, fs), lambda i: (0, i, 0))
    return _grid_call(
        body, name, t // tm,
        [row, _resident((1, d)), _resident(wg.shape), _resident(wu.shape), _resident(wd.shape)],
        [row, row, act, act],
        [jax.ShapeDtypeStruct((t, d), F32), jax.ShapeDtypeStruct((t, d), BF16),
         jax.ShapeDtypeStruct((nj, t, fs), BF16), jax.ShapeDtypeStruct((nj, t, fs), BF16)],
        (x, nw, wg, wu, wd), exchange=exchange)


def _ffn_bwd_dx(dy, x, nw, g, u, wg, wu, wd, tm, name, exchange=None):
    t, d = x.shape
    nj, _, fs = wg.shape

    def body(dy_ref, x_ref, nw_ref, g_ref, u_ref, wg_ref, wu_ref, wd_ref,
             dx_ref, dyh_ref, dg_ref, du_ref, h_ref, dnw_ref):
        @pl.when(pl.program_id(0) == 0)
        def _():
            dnw_ref[...] = jnp.zeros_like(dnw_ref)

        dyv = dy_ref[...]
        dyh = (0.5 * dyv).astype(BF16)
        dyh_ref[...] = dyh
        dxn = jnp.zeros((tm, d), F32)
        dh_next = _dot_nt(dyh, wd_ref[0])
        for j in range(nj):
            dh = dh_next
            gv = g_ref[j].astype(F32)
            uv = u_ref[j].astype(F32)
            sg = _sigmoid(gv)
            si = gv * sg
            dg = (dh * uv * (sg * (1.0 + gv * (1.0 - sg)))).astype(BF16)
            du = (dh * si).astype(BF16)
            if j + 1 < nj:
                dh_next = _dot_nt(dyh, wd_ref[j + 1])
            h_ref[j] = (si * uv).astype(BF16)
            dg_ref[j] = dg
            du_ref[j] = du
            dxn = dxn + _dot_nt(dg, wg_ref[j]) + _dot_nt(du, wu_ref[j])
        xf = x_ref[...]
        dxr, dnw = _rms_bwd(xf, _rstd(xf), nw_ref[...], dxn)
        dx_ref[...] = dyv + dxr
        dnw_ref[...] += dnw

    row = pl.BlockSpec((tm, d), lambda i: (i, 0))
    act = pl.BlockSpec((nj, tm, fs), lambda i: (0, i, 0))
    act_shape = jax.ShapeDtypeStruct((nj, t, fs), BF16)
    return _grid_call(
        body, name, t // tm,
        [row, row, _resident((1, d)), act, act, _resident(wg.shape), _resident(wu.shape), _resident(wd.shape)],
        [row, row, act, act, act, pl.BlockSpec((1, d), lambda i: (0, 0))],
        [jax.ShapeDtypeStruct((t, d), F32), jax.ShapeDtypeStruct((t, d), BF16),
         act_shape, act_shape, act_shape, jax.ShapeDtypeStruct((1, d), F32)],
        (dy, x, nw, g, u, wg, wu, wd), exchange=exchange)


def _matmul_tn(a, b, tk, name):
    a3, b3 = a.ndim == 3, b.ndim == 3
    nj = a.shape[0] if a3 else (b.shape[0] if b3 else 1)
    t, m = a.shape[-2:]
    n = b.shape[-1]
    nt = t // tk

    def body(a_ref, b_ref, o_ref, acc_ref):
        k = pl.program_id(1)

        @pl.when(k == 0)
        def _():
            acc_ref[...] = jnp.zeros_like(acc_ref)

        acc_ref[...] += lax.dot_general(a_ref[...], b_ref[...], (((0,), (0,)), ((), ())),
                                        preferred_element_type=F32)

        @pl.when(k == nt - 1)
        def _():
            o_ref[...] = acc_ref[...].astype(o_ref.dtype)

    a_spec = (pl.BlockSpec((None, tk, m), lambda j, k: (j, k, 0)) if a3
              else pl.BlockSpec((tk, m), lambda j, k: (k, 0)))
    b_spec = (pl.BlockSpec((None, tk, n), lambda j, k: (j, k, 0)) if b3
              else pl.BlockSpec((tk, n), lambda j, k: (k, 0)))
    return pl.pallas_call(
        body, name=name, grid=(nj, nt),
        in_specs=[a_spec, b_spec],
        out_specs=pl.BlockSpec((None, m, n), lambda j, k: (j, 0, 0)),
        out_shape=jax.ShapeDtypeStruct((nj, m, n), BF16),
        scratch_shapes=[pltpu.VMEM((m, n), F32)],
        compiler_params=_params(("arbitrary", "arbitrary"), VMEM_LIMIT),
    )(a, b)


P_QKVA, P_Z, P_AB, P_QKVB = (0, 1536), (1536, 2048), (2048, 2176), (2176, 3712)
P_PIECES = (P_QKVA, P_Z, P_AB, P_QKVB)
P_COLS = 3712


def _mix_in_fwd(x1, nw, wp, tm):
    t, d = x1.shape

    def body(x_ref, nw_ref, w_ref, hn_ref, *outs):
        xf = x_ref[...]
        xn = (xf * _rstd(xf) * nw_ref[...]).astype(BF16)
        hn_ref[...] = xn
        for (a, b), o_ref in zip(P_PIECES, outs):
            o_ref[...] = jnp.dot(xn, w_ref[:, a:b], preferred_element_type=F32)

    row = pl.BlockSpec((tm, d), lambda i: (i, 0))
    return pl.pallas_call(
        body, name="mix_in_fwd", grid=(t // tm,),
        in_specs=[row, _resident((1, d)), _resident(wp.shape)],
        out_specs=[row] + [pl.BlockSpec((tm, b - a), lambda i: (i, 0)) for a, b in P_PIECES],
        out_shape=[jax.ShapeDtypeStruct((t, d), BF16)]
                  + [jax.ShapeDtypeStruct((t, b - a), F32) for a, b in P_PIECES],
        compiler_params=_params(("arbitrary",), VMEM_LIMIT),
    )(x1, nw, wp)


def _mix_in_bwd_dx(dx, x1, nw, dpieces, wp, tm):
    t, d = x1.shape

    def body(dx_ref, x_ref, nw_ref, p0, p1, p2, p3, w_ref, o_ref, dnw_ref):
        @pl.when(pl.program_id(0) == 0)
        def _():
            dnw_ref[...] = jnp.zeros_like(dnw_ref)

        dh = jnp.zeros((tm, d), F32)
        for (a, b), p_ref in zip(P_PIECES, (p0, p1, p2, p3)):
            dh = dh + _dot_nt(p_ref[...], w_ref[:, a:b])
        xf = x_ref[...]
        dxr, dnw = _rms_bwd(xf, _rstd(xf), nw_ref[...], dh)
        o_ref[...] = dx_ref[...] + dxr
        dnw_ref[...] += dnw

    row = pl.BlockSpec((tm, d), lambda i: (i, 0))
    return pl.pallas_call(
        body, name="mix_in_bwd_dx", grid=(t // tm,),
        in_specs=[row, row, _resident((1, d))]
                 + [pl.BlockSpec((tm, b - a), lambda i: (i, 0)) for a, b in P_PIECES] + [_resident(wp.shape)],
        out_specs=[row, pl.BlockSpec((1, d), lambda i: (0, 0))],
        out_shape=[jax.ShapeDtypeStruct((t, d), F32), jax.ShapeDtypeStruct((1, d), F32)],
        compiler_params=_params(("arbitrary",), VMEM_LIMIT),
    )(dx, x1, nw, *dpieces, wp)


def _mix_out_fwd(x1, oa, ob, w, tm):
    t, d = x1.shape
    half = oa.shape[1]

    def body(x_ref, oa_ref, ob_ref, w_ref, o_ref):
        o_ref[...] = (x_ref[...] + _dot(oa_ref[...], w_ref[0:half, :]) + _dot(ob_ref[...], w_ref[half:2 * half, :]))

    row = pl.BlockSpec((tm, d), lambda i: (i, 0))
    hrow = pl.BlockSpec((tm, half), lambda i: (i, 0))
    return pl.pallas_call(
        body, name="mix_out_fwd", grid=(t // tm,),
        in_specs=[row, hrow, hrow, _resident(w.shape)],
        out_specs=row, out_shape=jax.ShapeDtypeStruct((t, d), F32),
        compiler_params=_params(("arbitrary",), VMEM_LIMIT),
    )(x1, oa, ob, w)


def _mix_out_bwd(dx2, w, tm):
    t, d = dx2.shape
    half = w.shape[0] // 2

    def body(dx_ref, w_ref, doa_ref, dob_ref, dxb_ref):
        dxb = dx_ref[...].astype(BF16)
        dxb_ref[...] = dxb
        doa_ref[...] = _dot_nt(dxb, w_ref[0:half, :])
        dob_ref[...] = _dot_nt(dxb, w_ref[half:2 * half, :])

    row = pl.BlockSpec((tm, d), lambda i: (i, 0))
    hrow = pl.BlockSpec((tm, half), lambda i: (i, 0))
    return pl.pallas_call(
        body, name="mix_out_bwd", grid=(t // tm,),
        in_specs=[row, _resident(w.shape)],
        out_specs=[hrow, hrow, row],
        out_shape=[jax.ShapeDtypeStruct((t, half), F32), jax.ShapeDtypeStruct((t, half), F32),
                   jax.ShapeDtypeStruct((t, d), BF16)],
        compiler_params=_params(("arbitrary",), VMEM_LIMIT),
    )(dx2, w)


def _final_loss(x3, fw, target, tm):
    t, d = x3.shape

    def body(x_ref, w_ref, t_ref, dx_ref, loss_ref, dw_ref):
        @pl.when(pl.program_id(0) == 0)
        def _():
            loss_ref[...] = jnp.zeros_like(loss_ref)
            dw_ref[...] = jnp.zeros_like(dw_ref)

        xf = x_ref[...]
        r = _rstd(xf)
        err = xf * r * w_ref[...] - t_ref[...]
        loss_ref[...] += 0.5 * jnp.sum(jnp.mean(err * err, axis=-1, keepdims=True), axis=0, keepdims=True)
        dxr, dw = _rms_bwd(xf, r, w_ref[...], err * (1.0 / d))
        dx_ref[...] = dxr
        dw_ref[...] += dw

    row = pl.BlockSpec((tm, d), lambda i: (i, 0))
    return pl.pallas_call(
        body, name="final_loss", grid=(t // tm,),
        in_specs=[row, _resident((1, d)), row],
        out_specs=[row, pl.BlockSpec((1, LANES), lambda i: (0, 0)), pl.BlockSpec((1, d), lambda i: (0, 0))],
        out_shape=[jax.ShapeDtypeStruct((t, d), F32), jax.ShapeDtypeStruct((1, LANES), F32),
                   jax.ShapeDtypeStruct((1, d), F32)],
        compiler_params=_params(("arbitrary",), VMEM_LIMIT),
    )(x3, fw, target)


HALO = 8


def _halo_row_specs(tr, cols, nrow8):
    per = tr // HALO
    return [pl.BlockSpec((tr, cols), lambda i: (i, 0)),
            pl.BlockSpec((HALO, cols), lambda i: (jnp.maximum(i * per - 1, 0), 0)),
            pl.BlockSpec((HALO, cols), lambda i: (jnp.minimum((i + 1) * per, nrow8 - 1), 0))]


def _conv_window(xm, xp, xn, first, last, cols):
    prev = jnp.where(first, 0.0, xp[:, cols])
    nxt = jnp.where(last, 0.0, xn[:, cols])
    return jnp.concatenate([prev, xm[:, cols], nxt], axis=0)


def _shift_rows(xw, off):
    n = xw.shape[0]
    sh = (-off) % n
    return xw if sh == 0 else pltpu.roll(xw, sh, 0)


def _conv_pre(xw, cw_ref, cols):
    acc = None
    for j in range(CONV_TAPS):
        term = _shift_rows(xw, j - CONV_TAPS // 2) * cw_ref[j:j + 1, cols]
        acc = term if acc is None else acc + term
    return acc


def _softplus(x):
    u = jnp.exp(-jnp.abs(x))
    w = 1.0 + u
    log1p = jnp.where(w == 1.0, u, jnp.log(w) * u / jnp.where(w == 1.0, 1.0, w - 1.0))
    return jnp.maximum(x, 0.0) + log1p


def _gdn_prep_fwd(qkva, cw, ab, gp, tr):
    t, c = qkva.shape
    nt = t // tr
    ncb = c // LANES

    def body(xm, xp, xn, cw_ref, ab_ref, gp_ref, o_ref, gb_ref):
        i = pl.program_id(0)
        first, last = i == 0, i == nt - 1
        for cb in range(ncb):
            cols = slice(cb * LANES, (cb + 1) * LANES)
            xw = _conv_window(xm, xp, xn, first, last, cols)
            pre = _conv_pre(xw, cw_ref, cols)[HALO:HALO + tr]
            y = pre * _sigmoid(pre)
            if cb < 2 * GDN_HEADS:
                y = y * lax.rsqrt(jnp.sum(y * y, axis=-1, keepdims=True) + EPS)
            if cb < GDN_HEADS:
                y = y * (GDN_DIM ** -0.5)
            o_ref[:, cols] = y
        abv = ab_ref[...]
        lane = lax.broadcasted_iota(jnp.int32, abv.shape, 1)
        g = -jnp.exp(gp_ref[0:1, :]) * _softplus(abv + gp_ref[1:2, :])
        gb_ref[...] = jnp.where(lane < 8, g, jnp.where(lane < 16, _sigmoid(abv), 0.0))

    return pl.pallas_call(
        body, name="gdn_prep_fwd", grid=(nt,),
        in_specs=_halo_row_specs(tr, c, t // HALO)
                 + [_resident(cw.shape), pl.BlockSpec((tr, LANES), lambda i: (i, 0)), _resident(gp.shape)],
        out_specs=[pl.BlockSpec((tr, c), lambda i: (i, 0)), pl.BlockSpec((tr, LANES), lambda i: (i, 0))],
        out_shape=[jax.ShapeDtypeStruct((t, c), F32), jax.ShapeDtypeStruct((t, LANES), F32)],
        compiler_params=_params(("arbitrary",), VMEM_LIMIT),
    )(qkva, qkva, qkva, cw, ab, gp)


def _gdn_prep_bwd(qkva, cw, ab, gp, dy, dgates, tr):
    t, c = qkva.shape
    nt = t // tr
    ncb = c // LANES

    def body(xm, xp, xn, fm, fp, fn, cw_ref, ab_ref, gp_ref, gf_ref, dx_ref, dab_ref, dcw_ref, dgp_ref):
        i = pl.program_id(0)
        first, last = i == 0, i == nt - 1

        @pl.when(first)
        def _():
            dcw_ref[...] = jnp.zeros_like(dcw_ref)
            dgp_ref[...] = jnp.zeros_like(dgp_ref)

        sub8 = lax.broadcasted_iota(jnp.int32, (8, LANES), 0)
        for cb in range(ncb):
            cols = slice(cb * LANES, (cb + 1) * LANES)
            xw = _conv_window(xm, xp, xn, first, last, cols)
            dyw = _conv_window(fm, fp, fn, first, last, cols)
            pre = _conv_pre(xw, cw_ref, cols)
            sg = _sigmoid(pre)
            s = pre * sg
            if cb < 2 * GDN_HEADS:
                scale = (GDN_DIM ** -0.5) if cb < GDN_HEADS else 1.0
                r = lax.rsqrt(jnp.sum(s * s, axis=-1, keepdims=True) + EPS)
                dn = dyw * scale
                ds = r * dn - s * (r * r * r) * jnp.sum(dn * s, axis=-1, keepdims=True)
            else:
                ds = dyw
            dpre = ds * (sg * (1.0 + pre * (1.0 - sg)))
            dx = None
            dcw = jnp.zeros((8, LANES), F32)
            for j in range(CONV_TAPS):
                off = j - CONV_TAPS // 2
                term = _shift_rows(dpre, -off)[HALO:HALO + tr] * cw_ref[j:j + 1, cols]
                dx = term if dx is None else dx + term
                tap = jnp.sum(dpre[HALO:HALO + tr] * _shift_rows(xw, off)[HALO:HALO + tr], axis=0, keepdims=True)
                dcw = dcw + jnp.where(sub8 == j, tap, 0.0)
            dx_ref[:, cols] = dx.astype(BF16)
            dcw_ref[:, cols] += dcw

        abv = ab_ref[...]
        dgb = gf_ref[...]
        lane = lax.broadcasted_iota(jnp.int32, abv.shape, 1)
        nea = -jnp.exp(gp_ref[0:1, :])
        xs = abv + gp_ref[1:2, :]
        g = nea * _softplus(xs)
        beta = _sigmoid(abv)
        da = dgb * nea * _sigmoid(xs)
        dab = jnp.where(lane < 8, da, jnp.where(lane < 16, dgb * beta * (1.0 - beta), 0.0))
        dab_ref[...] = dab.astype(BF16)
        keep = lane[0:1, :] < 8
        dalog = jnp.where(keep, jnp.sum(dgb * g, axis=0, keepdims=True), 0.0)
        ddtb = jnp.where(keep, jnp.sum(da, axis=0, keepdims=True), 0.0)
        dgp_ref[...] += jnp.where(sub8 == 0, dalog, 0.0) + jnp.where(sub8 == 1, ddtb, 0.0)

    lrow = pl.BlockSpec((tr, LANES), lambda i: (i, 0))
    halo = _halo_row_specs(tr, c, t // HALO)
    return pl.pallas_call(
        body, name="gdn_prep_bwd", grid=(nt,),
        in_specs=halo + halo + [_resident(cw.shape), lrow, _resident(gp.shape), lrow],
        out_specs=[pl.BlockSpec((tr, c), lambda i: (i, 0)), lrow,
                   pl.BlockSpec(cw.shape, lambda i: (0, 0)), pl.BlockSpec(gp.shape, lambda i: (0, 0))],
        out_shape=[jax.ShapeDtypeStruct((t, c), BF16), jax.ShapeDtypeStruct((t, LANES), BF16),
                   jax.ShapeDtypeStruct(cw.shape, F32), jax.ShapeDtypeStruct(gp.shape, F32)],
        compiler_params=_params(("arbitrary",), VMEM_LIMIT),
    )(qkva, qkva, qkva, dy, dy, dy, cw, ab, gp, dgates)


def _chunk_masks(lower):
    ii = lax.broadcasted_iota(jnp.int32, (CHUNK, CHUNK), 0)
    jj = lax.broadcasted_iota(jnp.int32, (CHUNK, CHUNK), 1)
    incl = (ii >= jj) if lower else (ii <= jj)
    strict = (ii > jj) if lower else (ii < jj)
    return ii, jj, incl, strict


def _dot3(a, b):
    ah = a.astype(BF16)
    al = (a - ah.astype(F32)).astype(BF16)
    bh = b.astype(BF16)
    bl = (b - bh.astype(F32)).astype(BF16)
    d = lambda u, v: jnp.dot(u, v, preferred_element_type=F32)
    return d(ah, bh) + (d(ah, bl) + d(al, bh))


def _tri_inv_many(lmats, ii, jj):
    m16 = (ii // 16) == (jj // 16)
    m32 = (ii // 32) == (jj // 32)
    eye = jnp.where(ii == jj, 1.0, 0.0)
    l16 = [jnp.where(m16, l, 0.0) for l in lmats]
    p2 = [_dot3(a, a) for a in l16]
    p4 = [_dot3(a, a) for a in p2]
    p8 = [_dot3(a, a) for a in p4]
    xs = [eye - a for a in l16]
    for ps in (p2, p4, p8):
        xs = [x + _dot3(x, p) for x, p in zip(xs, ps)]
    for off in ([jnp.where(m32 & jnp.logical_not(m16), l, 0.0) for l in lmats],
                [jnp.where(m32, 0.0, l) for l in lmats]):
        ys = [_dot3(x, c) for x, c in zip(xs, off)]
        xs = [x - _dot3(y, x) for x, y in zip(xs, ys)]
    return xs


def _col_to_row(col, ii, jj):
    return jnp.sum(jnp.where(ii == jj, col, 0.0), axis=0, keepdims=True)


def _row_to_col(row, ii, jj):
    return jnp.sum(jnp.where(ii == jj, row, 0.0), axis=1, keepdims=True)


def _chain_common(q, k, v, graw_col, graw_row, bcol, masks):
    ii, jj, incl, strict = masks
    inclt = jnp.logical_not(strict)
    gcol = jnp.sum(jnp.where(incl, graw_row, 0.0), axis=1, keepdims=True)
    grow = jnp.sum(jnp.where(inclt, graw_col, 0.0), axis=0, keepdims=True)
    glast = jnp.sum(graw_row, axis=1, keepdims=True)
    decay = jnp.where(incl, jnp.exp(jnp.where(incl, gcol - grow, 0.0)), 0.0)
    kb = k * bcol
    vb = v * bcol
    eg = jnp.exp(gcol)
    ek = jnp.exp(glast - gcol)
    kbg = kb * eg
    amat = _dot_nt(kb, k)
    qk = _dot_nt(q, k)
    return dict(gcol=gcol, glast=glast, decay=decay, kb=kb, vb=vb, eg=eg, ek=ek, kbg=kbg, amat=amat, qk=qk,
                intra=qk * decay, qg=q * eg, kdec=k * ek)


def _gdn_fwd(qkvc, gb, gbt):
    tm, u, w, qg, kd, intra, egl = _gdn_local_fwd(qkvc, gb, gbt)
    o_f, o_b, s_f, s_b, vn_f, vn_b = _gdn_scan_fwd(u, w, qg, kd, intra, egl, qkvc.shape[0])
    return o_f, o_b, dict(tm=tm, w=w, qg=qg, kd=kd, intra=intra, egl=egl, s=(s_f, s_b), vn=(vn_f, vn_b))


N_CHAINS = 2 * GDN_HEADS


def _load_chains(x_ref, g_ref, gt_ref):
    hd = GDN_HEADS * GDN_DIM
    chains = []
    for d in range(2):
        masks = _chunk_masks(d == 0)
        for h in range(GDN_HEADS):
            ch = d * GDN_HEADS + h
            q = x_ref[:, h * GDN_DIM:(h + 1) * GDN_DIM]
            k = x_ref[:, hd + h * GDN_DIM:hd + (h + 1) * GDN_DIM]
            v = x_ref[:, 2 * hd + h * GDN_DIM:2 * hd + (h + 1) * GDN_DIM]
            bcol = g_ref[:, 8 + ch:9 + ch]
            cm = _chain_common(q, k, v, g_ref[:, ch:ch + 1], gt_ref[0, ch:ch + 1, :], bcol, masks)
            chains.append(dict(cm, q=q, k=k, v=v, bcol=bcol, masks=masks, ch=ch, h=h))
    return chains


def _chain_shape(rows, cols, dtype):
    return lambda nc: jax.ShapeDtypeStruct((nc, N_CHAINS, rows, cols), dtype)


def _gdn_local_fwd(qkvc, gb, gbt):
    t = qkvc.shape[0]
    nc = t // CHUNK
    hd = GDN_HEADS * GDN_DIM

    def body(x_ref, g_ref, gt_ref, t_ref, u_ref, w_ref, qg_ref, kd_ref, in_ref, eg_ref):
        chains = _load_chains(x_ref, g_ref, gt_ref)
        ii, jj = chains[0]["masks"][0:2]
        tms = _tri_inv_many([jnp.where(c["masks"][3], c["amat"] * c["decay"], 0.0) for c in chains], ii, jj)
        us = [_dot(tm, c["vb"]) for tm, c in zip(tms, chains)]
        ws = [_dot(tm, c["kbg"]) for tm, c in zip(tms, chains)]
        for c, tm, u, w in zip(chains, tms, us, ws):
            ch = c["ch"]
            t_ref[0, ch] = tm
            u_ref[0, ch] = u
            w_ref[0, ch] = w.astype(BF16)
            qg_ref[0, ch] = c["qg"].astype(BF16)
            kd_ref[0, ch] = c["kdec"].astype(BF16)
            in_ref[0, ch] = c["intra"].astype(BF16)
            eg_ref[0, ch:ch + 1, :] = jnp.broadcast_to(jnp.exp(c["glast"]), (1, LANES))

    blk = lambda rows, cols: pl.BlockSpec((1, N_CHAINS, rows, cols), lambda n: (n, 0, 0, 0))
    shapes = [_chain_shape(CHUNK, CHUNK, F32), _chain_shape(CHUNK, GDN_DIM, F32), _chain_shape(CHUNK, GDN_DIM, BF16),
              _chain_shape(CHUNK, GDN_DIM, BF16), _chain_shape(CHUNK, GDN_DIM, BF16), _chain_shape(CHUNK, CHUNK, BF16)]
    return tuple(pl.pallas_call(
        body, name="gdn_local_fwd", grid=(nc,),
        in_specs=[pl.BlockSpec((CHUNK, 3 * hd), lambda n: (n, 0)), pl.BlockSpec((CHUNK, LANES), lambda n: (n, 0)),
                  pl.BlockSpec((1, 16, CHUNK), lambda n: (n, 0, 0))],
        out_specs=[blk(CHUNK, CHUNK), blk(CHUNK, GDN_DIM), blk(CHUNK, GDN_DIM), blk(CHUNK, GDN_DIM),
                   blk(CHUNK, GDN_DIM), blk(CHUNK, CHUNK), pl.BlockSpec((1, N_CHAINS, LANES), lambda n: (n, 0, 0))],
        out_shape=[s(nc) for s in shapes] + [jax.ShapeDtypeStruct((nc, N_CHAINS, LANES), F32)],
        compiler_params=_params(("arbitrary",), VMEM_LIMIT),
    )(qkvc, gb, gbt))


def _dir_specs(nc, rev):
    def spec(d, rows, cols, own=False):
        chunk = (lambda n: n) if (d == 0) != rev else (lambda n: nc - 1 - n)
        blk = 0 if own else d
        if rows is None:
            return pl.BlockSpec((1, GDN_HEADS if own else N_CHAINS, cols), lambda n: (chunk(n), 0, 0))
        return pl.BlockSpec((1, GDN_HEADS, rows, cols), lambda n: (chunk(n), blk, 0, 0))

    def rows_spec(d, cols):
        chunk = (lambda n: n) if (d == 0) != rev else (lambda n: nc - 1 - n)
        return pl.BlockSpec((CHUNK, cols), lambda n: (chunk(n), 0))
    return spec, rows_spec


def _gdn_scan_fwd(u, w, qg, kd, intra, egl, t):
    nc = t // CHUNK
    hd = GDN_HEADS * GDN_DIM

    def body(*refs):
        ins, outs, state = refs[:12], refs[12:18], refs[18]
        @pl.when(pl.program_id(0) == 0)
        def _():
            state[...] = jnp.zeros_like(state)

        chains = [(d, h) for d in range(2) for h in range(GDN_HEADS)]
        pick = lambda k, d, h: ins[2 * k + d][0, h]
        states = [state[ch] for ch in range(N_CHAINS)]
        sbs = [s.astype(BF16) for s in states]
        ws = [_dot(pick(1, d, h), sb) for (d, h), sb in zip(chains, sbs)]
        o1 = [_dot(pick(2, d, h), sb) for (d, h), sb in zip(chains, sbs)]
        vns = [(pick(0, d, h) - wsb).astype(BF16) for (d, h), wsb in zip(chains, ws)]
        o2 = [_dot(pick(4, d, h), vn) for (d, h), vn in zip(chains, vns)]
        kv = [_dot_tn(pick(3, d, h), vn) for (d, h), vn in zip(chains, vns)]
        for ch, (d, h) in enumerate(chains):
            outs[d][:, h * GDN_DIM:(h + 1) * GDN_DIM] = o1[ch] + o2[ch]
            outs[2 + d][0, h] = states[ch]
            outs[4 + d][0, h] = vns[ch]
            state[ch] = states[ch] * ins[10 + d][0, ch:ch + 1, :] + kv[ch]

    spec, rows_spec = _dir_specs(nc, False)
    pair = lambda rows, cols, own=False: [spec(0, rows, cols, own), spec(1, rows, cols, own)]
    s_shape = jax.ShapeDtypeStruct((nc, GDN_HEADS, GDN_DIM, GDN_DIM), F32)
    vn_shape = jax.ShapeDtypeStruct((nc, GDN_HEADS, CHUNK, GDN_DIM), BF16)
    return pl.pallas_call(
        body, name="gdn_scan_fwd", grid=(nc,),
        in_specs=(pair(CHUNK, GDN_DIM) + pair(CHUNK, GDN_DIM) + pair(CHUNK, GDN_DIM) + pair(CHUNK, GDN_DIM)
                  + pair(CHUNK, CHUNK) + pair(None, LANES)),
        out_specs=([rows_spec(0, hd), rows_spec(1, hd)] + pair(GDN_DIM, GDN_DIM, True)
                   + pair(CHUNK, GDN_DIM, True)),
        out_shape=[jax.ShapeDtypeStruct((t, hd), F32), jax.ShapeDtypeStruct((t, hd), F32),
                   s_shape, s_shape, vn_shape, vn_shape],
        scratch_shapes=[pltpu.VMEM((N_CHAINS, GDN_DIM, GDN_DIM), F32)],
        compiler_params=_params(("arbitrary",), VMEM_LIMIT),
    )(u, u, w, w, qg, qg, kd, kd, intra, intra, egl, egl)


def _gdn_bwd(qkvc, gb, gbt, do, saved, exchange=None):
    scan = _gdn_scan_bwd(do, saved, qkvc.shape[0])
    return _gdn_local_bwd(qkvc, gb, gbt, do, saved, scan, exchange)


def _gdn_scan_bwd(do, saved, t):
    nc = t // CHUNK
    hd = GDN_HEADS * GDN_DIM

    def body(*refs):
        ins, outs, dstate = refs[:16], refs[16:26], refs[26]
        @pl.when(pl.program_id(0) == 0)
        def _():
            dstate[...] = jnp.zeros_like(dstate)

        chains = [(d, h) for d in range(2) for h in range(GDN_HEADS)]
        pick = lambda k, d, h: ins[2 * k + d][0, h]
        dss = [dstate[ch] for ch in range(N_CHAINS)]
        dsbs = [ds.astype(BF16) for ds in dss]
        ss = [pick(1, d, h) for d, h in chains]
        sbs = [s.astype(BF16) for s in ss]
        dos = [ins[d][:, h * GDN_DIM:(h + 1) * GDN_DIM].astype(BF16) for d, h in chains]
        dv1 = [_dot_tn(pick(5, d, h), dov) for (d, h), dov in zip(chains, dos)]
        dv2 = [_dot(pick(4, d, h), dsb) for (d, h), dsb in zip(chains, dsbs)]
        ds1 = [_dot_tn(pick(3, d, h), dov) for (d, h), dov in zip(chains, dos)]
        dkds = [_dot_nt(pick(6, d, h), dsb) for (d, h), dsb in zip(chains, dsbs)]
        dqgs = [_dot_nt(dov, sb) for dov, sb in zip(dos, sbs)]
        dvns = [(a + b).astype(BF16) for a, b in zip(dv1, dv2)]
        ds2 = [_dot_tn(pick(2, d, h), dvn) for (d, h), dvn in zip(chains, dvns)]
        dws = [_dot_nt(dvn, sb) for dvn, sb in zip(dvns, sbs)]
        for ch, (d, h) in enumerate(chains):
            egl = ins[14 + d][0, ch:ch + 1, :]
            outs[d][0, h] = dvns[ch]
            outs[2 + d][0, h] = (-dws[ch]).astype(BF16)
            outs[4 + d][0, h] = dqgs[ch]
            outs[6 + d][0, h] = dkds[ch]
            outs[8 + d][0, h:h + 1, :] = egl * jnp.sum(jnp.sum(ss[ch] * dss[ch], axis=1, keepdims=True),
                                                       axis=0, keepdims=True)
            dstate[ch] = ds1[ch] + egl * dss[ch] - ds2[ch]

    spec, rows_spec = _dir_specs(nc, True)
    pair = lambda rows, cols, own=False: [spec(0, rows, cols, own), spec(1, rows, cols, own)]
    s_f, s_b = saved["s"]
    vn_f, vn_b = saved["vn"]
    w, qg, kd, intra, egl = saved["w"], saved["qg"], saved["kd"], saved["intra"], saved["egl"]
    own = lambda rows, cols, dtype: jax.ShapeDtypeStruct((nc, GDN_HEADS, rows, cols), dtype)
    row_shape = jax.ShapeDtypeStruct((nc, GDN_HEADS, LANES), F32)
    return pl.pallas_call(
        body, name="gdn_scan_bwd", grid=(nc,),
        in_specs=([rows_spec(0, hd), rows_spec(1, hd)] + pair(GDN_DIM, GDN_DIM, True) + pair(CHUNK, GDN_DIM)
                  + pair(CHUNK, GDN_DIM) + pair(CHUNK, GDN_DIM) + pair(CHUNK, CHUNK) + pair(CHUNK, GDN_DIM, True)
                  + pair(None, LANES)),
        out_specs=(pair(CHUNK, GDN_DIM, True) + pair(CHUNK, GDN_DIM, True) + pair(CHUNK, GDN_DIM, True)
                   + pair(CHUNK, GDN_DIM, True) + pair(None, LANES, True)),
        out_shape=[own(CHUNK, GDN_DIM, BF16)] * 4 + [own(CHUNK, GDN_DIM, F32)] * 4 + [row_shape] * 2,
        scratch_shapes=[pltpu.VMEM((N_CHAINS, GDN_DIM, GDN_DIM), F32)],
        compiler_params=_params(("arbitrary",), VMEM_LIMIT),
    )(do, do, s_f, s_b, w, w, qg, qg, kd, kd, intra, intra, vn_f, vn_b, egl, egl)


def _dot3_nt(a, b):
    ah = a.astype(BF16)
    al = (a - ah.astype(F32)).astype(BF16)
    bh = b.astype(BF16)
    bl = (b - bh.astype(F32)).astype(BF16)
    return _dot_nt(ah, bh) + (_dot_nt(ah, bl) + _dot_nt(al, bh))


def _dot3_tn(a, b):
    ah = a.astype(BF16)
    al = (a - ah.astype(F32)).astype(BF16)
    bh = b.astype(BF16)
    bl = (b - bh.astype(F32)).astype(BF16)
    return _dot_tn(ah, bh) + (_dot_tn(ah, bl) + _dot_tn(al, bh))


def _gdn_local_bwd(qkvc, gb, gbt, do, saved, scan, exchange=None):
    t = qkvc.shape[0]
    nc = t // CHUNK
    hd = GDN_HEADS * GDN_DIM

    def body(*refs):
        x_ref, g_ref, gt_ref, do_ref, t_ref = refs[:5]
        per_dir = refs[5:17]
        dx_ref, dg_ref = refs[17:]
        chains = _load_chains(x_ref, g_ref, gt_ref)
        lane = lax.broadcasted_iota(jnp.int32, (CHUNK, LANES), 1)
        dgates = jnp.zeros((CHUNK, LANES), F32)
        for c in chains:
            d = c["ch"] // GDN_HEADS
            vn_ref, dvn_ref, dw_ref, dqg_ref, dkd_ref, dgl_ref = per_dir[d::2]
            h = c["h"]
            c.update(tm=t_ref[0, c["ch"]], dov=do_ref[:, h * GDN_DIM:(h + 1) * GDN_DIM], vnew=vn_ref[0, h],
                     dvnew=dvn_ref[0, h], dw=dw_ref[0, h], dqg=dqg_ref[0, h], dkdec=dkd_ref[0, h],
                     dglast=dgl_ref[0, h:h + 1, 0:1])
        dintras = [_dot_nt(c["dov"], c["vnew"]) for c in chains]
        dts = [_dot_nt(c["dvnew"], c["vb"]) + _dot_nt(c["dw"], c["kbg"]) for c in chains]
        dvbs = [_dot_tn(c["tm"], c["dvnew"]) for c in chains]
        dkbgs = [_dot_tn(c["tm"], c["dw"]) for c in chains]
        tdts = [_dot3_nt(dt, c["tm"]) for dt, c in zip(dts, chains)]
        dls = [jnp.where(c["masks"][3], -_dot3_tn(c["tm"], tdt), 0.0) for tdt, c in zip(tdts, chains)]
        das = [dl * c["decay"] for dl, c in zip(dls, chains)]
        dqks = [jnp.where(c["masks"][2], di, 0.0) * c["decay"] for di, c in zip(dintras, chains)]
        dkb1 = [_dot(da, c["k"]) for da, c in zip(das, chains)]
        dk1 = [_dot_tn(da, c["kb"]) for da, c in zip(das, chains)]
        dk2 = [_dot_tn(dqk, c["q"]) for dqk, c in zip(dqks, chains)]
        dq1 = [_dot(dqk, c["k"]) for dqk, c in zip(dqks, chains)]
        grads = []
        for n, c in enumerate(chains):
            ch = c["ch"]
            ii, jj, incl, strict = c["masks"]
            k, v, bcol = c["k"], c["v"], c["bcol"]
            decay, eg, ek, kbg = c["decay"], c["eg"], c["ek"], c["kbg"]
            dqg, dkdec, dglast = c["dqg"], c["dkdec"], c["dglast"]
            dvb, dkbg, dl = dvbs[n], dkbgs[n], dls[n]
            dintra = jnp.where(incl, dintras[n], 0.0)
            mm = (dl * c["amat"] + dintra * c["qk"]) * decay
            dkb = dkb1[n] + dkbg * eg
            dk = dk1[n] + dk2[n] + dkdec * ek + dkb * bcol
            dq = dq1[n] + dqg * eg
            dv = dvb * bcol
            dbeta = jnp.sum(dkb * k, axis=1, keepdims=True) + jnp.sum(dvb * v, axis=1, keepdims=True)
            kd2 = jnp.sum(dkdec * c["kdec"], axis=1, keepdims=True)
            dgc = (jnp.sum(mm, axis=1, keepdims=True) - _row_to_col(jnp.sum(mm, axis=0, keepdims=True), ii, jj)
                   + jnp.sum(dqg * c["qg"], axis=1, keepdims=True) - kd2
                   + jnp.sum(dkbg * kbg, axis=1, keepdims=True))
            dgl = dglast + jnp.sum(kd2, axis=0, keepdims=True)
            draw = jnp.sum(jnp.where(jnp.logical_not(strict), _col_to_row(dgc, ii, jj), 0.0),
                           axis=1, keepdims=True) + dgl
            dgates = dgates + jnp.where(lane == ch, draw, 0.0) + jnp.where(lane == 8 + ch, dbeta, 0.0)
            grads.append((dq, dk, dv))
        for h in range(GDN_HEADS):
            for part in range(3):
                cols = slice(part * hd + h * GDN_DIM, part * hd + (h + 1) * GDN_DIM)
                dx_ref[:, cols] = grads[h][part] + grads[GDN_HEADS + h][part]
        dg_ref[...] = dgates

    all8 = lambda rows, cols: pl.BlockSpec((1, N_CHAINS, rows, cols), lambda n: (n, 0, 0, 0))
    own4 = lambda rows, cols: pl.BlockSpec((1, GDN_HEADS, rows, cols), lambda n: (n, 0, 0, 0))
    row4 = pl.BlockSpec((1, GDN_HEADS, LANES), lambda n: (n, 0, 0))
    vn_f, vn_b = saved["vn"]
    dvn_f, dvn_b, dw_f, dw_b, dqg_f, dqg_b, dkd_f, dkd_b, dgl_f, dgl_b = scan
    return _grid_call(
        body, "gdn_local_bwd", nc,
        [pl.BlockSpec((CHUNK, 3 * hd), lambda n: (n, 0)), pl.BlockSpec((CHUNK, LANES), lambda n: (n, 0)),
         pl.BlockSpec((1, 16, CHUNK), lambda n: (n, 0, 0)), pl.BlockSpec((CHUNK, hd), lambda n: (n, 0)),
         all8(CHUNK, CHUNK)] + [own4(CHUNK, GDN_DIM)] * 10 + [row4, row4],
        [pl.BlockSpec((CHUNK, 3 * hd), lambda n: (n, 0)), pl.BlockSpec((CHUNK, LANES), lambda n: (n, 0))],
        [jax.ShapeDtypeStruct((t, 3 * hd), F32), jax.ShapeDtypeStruct((t, LANES), F32)],
        (qkvc, gb, gbt, do, saved["tm"], vn_f, vn_b, dvn_f, dvn_b, dw_f, dw_b, dqg_f, dqg_b, dkd_f, dkd_b, dgl_f, dgl_b),
        exchange=exchange)


def _gdn_post_fwd(of, ob, z, gw, tm):
    t, hd = of.shape

    def body(of_ref, ob_ref, z_ref, w_ref, o_ref):
        for h in range(GDN_HEADS):
            cols = slice(h * GDN_DIM, (h + 1) * GDN_DIM)
            o = of_ref[:, cols] + ob_ref[:, cols]
            zv = z_ref[:, cols]
            o_ref[:, cols] = (o * _rstd(o) * w_ref[...] * (zv * _sigmoid(zv))).astype(BF16)

    row = pl.BlockSpec((tm, hd), lambda i: (i, 0))
    return pl.pallas_call(
        body, name="gdn_post_fwd", grid=(t // tm,),
        in_specs=[row, row, row, _resident((1, GDN_DIM))],
        out_specs=row, out_shape=jax.ShapeDtypeStruct((t, hd), BF16),
        compiler_params=_params(("arbitrary",), VMEM_LIMIT),
    )(of, ob, z, gw)


def _gdn_post_bwd(doa, of, ob, z, gw, tm):
    t, hd = of.shape

    def body(d_ref, of_ref, ob_ref, z_ref, w_ref, do_ref, dz_ref, dw_ref):
        @pl.when(pl.program_id(0) == 0)
        def _():
            dw_ref[...] = jnp.zeros_like(dw_ref)

        dw = jnp.zeros((1, GDN_DIM), F32)
        for h in range(GDN_HEADS):
            cols = slice(h * GDN_DIM, (h + 1) * GDN_DIM)
            o = of_ref[:, cols] + ob_ref[:, cols]
            zv = z_ref[:, cols]
            dv = d_ref[:, cols]
            r = _rstd(o)
            sg = _sigmoid(zv)
            on = o * r * w_ref[...]
            dz_ref[:, cols] = (dv * on * (sg * (1.0 + zv * (1.0 - sg)))).astype(BF16)
            dxr, dwh = _rms_bwd(o, r, w_ref[...], dv * (zv * sg))
            do_ref[:, cols] = dxr
            dw = dw + dwh
        dw_ref[...] += dw

    row = pl.BlockSpec((tm, hd), lambda i: (i, 0))
    return pl.pallas_call(
        body, name="gdn_post_bwd", grid=(t // tm,),
        in_specs=[row, row, row, row, _resident((1, GDN_DIM))],
        out_specs=[row, row, pl.BlockSpec((1, GDN_DIM), lambda i: (0, 0))],
        out_shape=[jax.ShapeDtypeStruct((t, hd), F32), jax.ShapeDtypeStruct((t, hd), BF16),
                   jax.ShapeDtypeStruct((1, GDN_DIM), F32)],
        compiler_params=_params(("arbitrary",), VMEM_LIMIT),
    )(doa, of, ob, z, gw)


SWA_W = SWA_HEADS * SWA_DIM
QBLK = 128
KWIN = QBLK + 2 * RADIUS
WIN_OFFSETS = (0, RADIUS, 2 * RADIUS)


def _t5_bucket(rel):
    nb = REL_BUCKETS // 2
    bucket = (rel > 0).astype(np.int32) * nb
    n = np.abs(rel)
    max_exact = nb // 2
    large = max_exact + (np.log(np.maximum(n, 1) / max_exact)
                         / math.log(REL_MAX_DISTANCE / max_exact) * (nb - max_exact)).astype(np.int32)
    large = np.minimum(large, nb - 1)
    return (bucket + np.where(n < max_exact, n, large)).astype(np.int32)


def _band_tables(dilation):
    a = np.arange(QBLK)
    b = np.arange(KWIN)
    rel = np.stack([b[None, :] - w0 - a[:, None] for w0 in WIN_OFFSETS])
    return _t5_bucket(rel * dilation), np.abs(rel) <= RADIUS


def _bias_table(rel_bias, idx, valid):
    onehot = (jnp.arange(REL_BUCKETS, dtype=jnp.int32)[:, None] == jnp.asarray(idx.reshape(1, -1))).astype(F32)
    tab = jnp.dot(rel_bias.T, onehot, precision=HIGHEST)
    tab = jnp.where(jnp.asarray(valid.reshape(1, -1)), tab, NEG_BIG)
    return tab.reshape((SWA_HEADS,) + idx.shape), onehot


def _head_mean(x2, bd_ref):
    return _dot_hi(x2, bd_ref[...])


VIEW_DILATIONS = tuple(d for _, d in PATTERNS if d > 1)


def _view_spec(tm, d):
    return pl.BlockSpec((tm // d, d * SWA_W), lambda i: (i, 0))


def _view_shape(t, d, dtype):
    return jax.ShapeDtypeStruct((t // d, d * SWA_W), dtype)


N_GROUPS = SWA_W // LANES


def _to_view(src_ref, idx, dst_ref, d, rows):
    for r in range(d):
        for g in range(N_GROUPS):
            cols = slice(r * SWA_W + g * LANES, r * SWA_W + (g + 1) * LANES)
            dst_ref[:, cols] = src_ref[idx, g, pl.ds(r, rows // d, stride=d), :].astype(dst_ref.dtype)


def _from_view(src_ref, dst_ref, idx, d, rows):
    for r in range(d):
        for g in range(N_GROUPS):
            cols = slice(r * SWA_W + g * LANES, r * SWA_W + (g + 1) * LANES)
            dst_ref[idx, g, pl.ds(r, rows // d, stride=d), :] = src_ref[:, cols]


def _swa_prep_fwd(qkvb, qw, kw, bd, tm):
    t = qkvb.shape[0]

    def body(x_ref, qw_ref, kw_ref, bd_ref, *rest):
        outs, sc = rest[:-1], rest[-1]
        for gidx in range(N_GROUPS):
            cols = slice(gidx * LANES, (gidx + 1) * LANES)
            xq = x_ref[:, cols]
            sc[0, gidx] = xq * lax.rsqrt(_head_mean(xq * xq, bd_ref) + EPS) * qw_ref[:, cols] * (SWA_DIM ** -0.5)
            xk = x_ref[:, SWA_W + gidx * LANES:SWA_W + (gidx + 1) * LANES]
            sc[1, gidx] = xk * lax.rsqrt(_head_mean(xk * xk, bd_ref) + EPS) * kw_ref[:, cols]
            sc[2, gidx] = x_ref[:, 2 * SWA_W + gidx * LANES:2 * SWA_W + (gidx + 1) * LANES]
            for i in range(3):
                outs[i][:, cols] = sc[i, gidx].astype(BF16)
        for i in range(3):
            for n, d in enumerate(VIEW_DILATIONS):
                _to_view(sc, i, outs[3 * (n + 1) + i], d, tm)

    return pl.pallas_call(
        body, name="swa_prep_fwd", grid=(t // tm,),
        in_specs=[pl.BlockSpec((tm, 3 * SWA_W), lambda i: (i, 0)), _resident((1, SWA_W)), _resident((1, SWA_W)),
                  _resident((LANES, LANES))],
        out_specs=[_view_spec(tm, d) for d in (1,) + VIEW_DILATIONS for _ in range(3)],
        out_shape=[_view_shape(t, d, BF16) for d in (1,) + VIEW_DILATIONS for _ in range(3)],
        scratch_shapes=[pltpu.VMEM((3, N_GROUPS, tm, LANES), F32)],
        compiler_params=_params(("arbitrary",), VMEM_LIMIT),
    )(qkvb, qw, kw, bd)


def _swa_prep_bwd(qkvb, qw, kw, bd, grads, tm):
    t = qkvb.shape[0]

    def body(x_ref, qw_ref, kw_ref, bd_ref, *rest):
        parts, (dx_ref, dqw_ref, dkw_ref, sc) = rest[:9], rest[9:]
        @pl.when(pl.program_id(0) == 0)
        def _():
            dqw_ref[...] = jnp.zeros_like(dqw_ref)
            dkw_ref[...] = jnp.zeros_like(dkw_ref)

        for i in range(3):
            for n, d in enumerate(VIEW_DILATIONS):
                _from_view(parts[3 * (n + 1) + i], sc, 2 * i + n, d, tm)
        for gidx in range(N_GROUPS):
            cols = slice(gidx * LANES, (gidx + 1) * LANES)
            for i, base, w_ref, dw_ref, scale in ((0, 0, qw_ref, dqw_ref, SWA_DIM ** -0.5),
                                                  (1, SWA_W, kw_ref, dkw_ref, 1.0)):
                xv = x_ref[:, base + gidx * LANES:base + (gidx + 1) * LANES]
                dy = (parts[i][:, cols] + sc[2 * i, gidx] + sc[2 * i + 1, gidx]) * scale
                r = lax.rsqrt(_head_mean(xv * xv, bd_ref) + EPS)
                xhat = xv * r
                dxh = dy * w_ref[:, cols]
                dx = r * (dxh - xhat * _head_mean(dxh * xhat, bd_ref))
                dx_ref[:, base + gidx * LANES:base + (gidx + 1) * LANES] = dx.astype(BF16)
                dw_ref[:, cols] += jnp.sum(dy * xhat, axis=0, keepdims=True)
            dx_ref[:, 2 * SWA_W + gidx * LANES:2 * SWA_W + (gidx + 1) * LANES] = (
                parts[2][:, cols] + sc[4, gidx] + sc[5, gidx]).astype(BF16)

    wrow = pl.BlockSpec((1, SWA_W), lambda i: (0, 0))
    return pl.pallas_call(
        body, name="swa_prep_bwd", grid=(t // tm,),
        in_specs=[pl.BlockSpec((tm, 3 * SWA_W), lambda i: (i, 0)), _resident((1, SWA_W)), _resident((1, SWA_W)),
                  _resident((LANES, LANES))] + [_view_spec(tm, d) for d in (1,) + VIEW_DILATIONS for _ in range(3)],
        out_specs=[pl.BlockSpec((tm, 3 * SWA_W), lambda i: (i, 0)), wrow, wrow],
        out_shape=[jax.ShapeDtypeStruct((t, 3 * SWA_W), BF16), jax.ShapeDtypeStruct((1, SWA_W), F32),
                   jax.ShapeDtypeStruct((1, SWA_W), F32)],
        scratch_shapes=[pltpu.VMEM((6, N_GROUPS, tm, LANES), F32)],
        compiler_params=_params(("arbitrary",), VMEM_LIMIT),
    )(qkvb, qw, kw, bd, *grads)


def _aligned(v, m):
    return v if isinstance(v, int) else pl.multiple_of(v, m)


BAND_GROUP = 2


def _band_loop(nsub, length, step):
    step([(0, 0)], 0)
    if nsub > 2:
        assert (nsub - 2) % BAND_GROUP == 0

        def inner(i, carry):
            s0 = 1 + i * BAND_GROUP
            step([(s0 + e, pl.multiple_of((s0 + e) * QBLK - RADIUS, RADIUS)) for e in range(BAND_GROUP)], 1)
            return carry
        lax.fori_loop(0, (nsub - 2) // BAND_GROUP, inner, 0)
    step([(nsub - 1, length - KWIN)], 2)


def _head_select(lane, a0, a1):
    return jnp.where(lane < SWA_DIM, a0, a1)


def _swa_fwd(qv, kv, vv, bias, dilation, name):
    length = qv.shape[0]
    nsub = length // QBLK
    assert nsub >= 2 and length % QBLK == 0

    def body(q_ref, k_ref, v_ref, b_ref, o_ref, l_ref):
        lane = lax.broadcasted_iota(jnp.int32, (QBLK, LANES), 1)

        def step(blocks, var):
            items = []
            for s, ws in blocks:
                rows = pl.ds(_aligned(s * QBLK, QBLK), QBLK)
                q, kk, vw = q_ref[rows, :], k_ref[pl.ds(ws, KWIN), :], v_ref[pl.ds(ws, KWIN), :]
                for hh in range(2):
                    items.append((hh, jnp.where((lane < SWA_DIM) == (hh == 0), q, jnp.zeros_like(q)), kk, vw))
            lgs = [_dot_nt(qh, kk) + b_ref[hh, var] for hh, qh, kk, _ in items]
            ms = [jnp.max(lg, axis=-1, keepdims=True) for lg in lgs]
            ps = [jnp.exp(lg - m) for lg, m in zip(lgs, ms)]
            dens = [jnp.sum(p, axis=-1, keepdims=True) for p in ps]
            pvs = [_dot(p, it[3]) for p, it in zip(ps, items)]
            for n, (s, _) in enumerate(blocks):
                rows = pl.ds(_aligned(s * QBLK, QBLK), QBLK)
                o0, o1 = (pvs[2 * n + hh] / dens[2 * n + hh] for hh in range(2))
                l0, l1 = (ms[2 * n + hh] + jnp.log(dens[2 * n + hh]) for hh in range(2))
                o_ref[rows, :] = _head_select(lane, o0, o1)
                l_ref[rows, :] = _head_select(lane, l0, l1)

        _band_loop(nsub, length, step)

    blk = pl.BlockSpec((length, LANES), lambda hp, r: (0, r * (SWA_W // LANES) + hp))
    shp = jax.ShapeDtypeStruct(qv.shape, F32)
    return pl.pallas_call(
        body, name=name, grid=(SWA_W // LANES, dilation),
        in_specs=[blk, blk, blk, pl.BlockSpec((2, 3, QBLK, KWIN), lambda hp, r: (hp, 0, 0, 0))],
        out_specs=[blk, blk], out_shape=[shp, shp],
        compiler_params=_params(("arbitrary", "arbitrary"), VMEM_LIMIT),
    )(qv, kv, vv, bias)


def _swa_combine(os_, ls_, tm):
    t = os_[0].shape[0]

    def body(o0, o1, o2, l0, l1, l2, o_ref, ob_ref, la_ref, lb_ref, lc_ref, sc):
        for n, d in enumerate(VIEW_DILATIONS):
            _from_view((o1, o2)[n], sc, n, d, tm)
            _from_view((l1, l2)[n], sc, 2 + n, d, tm)
        for g in range(N_GROUPS):
            cols = slice(g * LANES, (g + 1) * LANES)
            la, lb, lc = l0[:, cols], sc[2, g], sc[3, g]
            m = jnp.maximum(jnp.maximum(la, lb), lc)
            tot = m + jnp.log(jnp.exp(la - m) + jnp.exp(lb - m) + jnp.exp(lc - m))
            o = jnp.exp(la - tot) * o0[:, cols] + jnp.exp(lb - tot) * sc[0, g] + jnp.exp(lc - tot) * sc[1, g]
            o_ref[:, cols] = o
            ob_ref[:, cols] = o.astype(BF16)
            la_ref[:, cols] = tot
            sc[4, g] = tot
        for n, d in enumerate(VIEW_DILATIONS):
            _to_view(sc, 4, (lb_ref, lc_ref)[n], d, tm)

    specs = [_view_spec(tm, d) for d in (1,) + VIEW_DILATIONS]
    return pl.pallas_call(
        body, name="swa_combine", grid=(t // tm,), in_specs=specs + specs, out_specs=[specs[0], specs[0]] + specs,
        out_shape=[jax.ShapeDtypeStruct((t, SWA_W), F32), jax.ShapeDtypeStruct((t, SWA_W), BF16)]
                  + [_view_shape(t, d, F32) for d in (1,) + VIEW_DILATIONS],
        scratch_shapes=[pltpu.VMEM((5, N_GROUPS, tm, LANES), F32)],
        compiler_params=_params(("arbitrary",), VMEM_LIMIT),
    )(*os_, *ls_)


def _swa_bwd_prep(do, o, bd, tm):
    t = do.shape[0]

    def body(d_ref, o_ref, bd_ref, dd1, dd4, dd16, db1, db4, db16, sc):
        for gidx in range(N_GROUPS):
            cols = slice(gidx * LANES, (gidx + 1) * LANES)
            dv = d_ref[:, cols]
            dd = _head_mean(dv * o_ref[:, cols], bd_ref) * float(SWA_DIM)
            sc[0, gidx] = dd
            sc[1, gidx] = dv
            dd1[:, cols] = dd
            db1[:, cols] = dv.astype(BF16)
        for n, d in enumerate(VIEW_DILATIONS):
            _to_view(sc, 0, (dd4, dd16)[n], d, tm)
            _to_view(sc, 1, (db4, db16)[n], d, tm)

    specs = [_view_spec(tm, d) for d in (1,) + VIEW_DILATIONS]
    return pl.pallas_call(
        body, name="swa_bwd_prep", grid=(t // tm,), in_specs=[specs[0], specs[0], _resident((LANES, LANES))],
        out_specs=specs + specs,
        out_shape=[_view_shape(t, d, F32) for d in (1,) + VIEW_DILATIONS]
                  + [_view_shape(t, d, BF16) for d in (1,) + VIEW_DILATIONS],
        scratch_shapes=[pltpu.VMEM((2, N_GROUPS, tm, LANES), F32)],
        compiler_params=_params(("arbitrary",), VMEM_LIMIT),
    )(do, o, bd)


def _swa_bwd(qv, kv, vv, dov, lv, ddv, bias_a, dilation, name):
    length = qv.shape[0]
    nsub = length // QBLK
    single = pl.Buffered(1) if dilation == 1 else None

    def body(q_ref, k_ref, v_ref, do_ref, l_ref, dd_ref, ba_ref, dq_ref, dk_ref, dv_ref, db_ref):
        @pl.when(pl.program_id(1) == 0)
        def _():
            db_ref[...] = jnp.zeros_like(db_ref)

        lane = lax.broadcasted_iota(jnp.int32, (QBLK, LANES), 1)
        lanew = lax.broadcasted_iota(jnp.int32, (KWIN, LANES), 1)

        def step(blocks, var):
            items = []
            for s, ws in blocks:
                rows = pl.ds(_aligned(s * QBLK, QBLK), QBLK)
                win = pl.ds(ws, KWIN)
                q, dov_ = q_ref[rows, :], do_ref[rows, :]
                kk, vw = k_ref[win, :], v_ref[win, :]
                lse, dd = l_ref[rows, :], dd_ref[rows, :]
                for hh in range(2):
                    mine = (lane < SWA_DIM) == (hh == 0)
                    col = slice(hh * SWA_DIM, hh * SWA_DIM + 1)
                    items.append((hh, jnp.where(mine, q, jnp.zeros_like(q)), jnp.where(mine, dov_, jnp.zeros_like(dov_)),
                                  kk, vw, lse[:, col], dd[:, col], q, dov_))
            lgs = [_dot_nt(it[1], it[3]) + ba_ref[it[0], var] for it in items]
            dps = [_dot_nt(it[2], it[4]) for it in items]
            ps = [jnp.exp(lg - it[5]) for lg, it in zip(lgs, items)]
            dss = [p * (dp - it[6]) for p, dp, it in zip(ps, dps, items)]
            dqs = [_dot(ds, it[3]) for ds, it in zip(dss, items)]
            dks = [_dot_tn(ds, it[7]) for ds, it in zip(dss, items)]
            dvs = [_dot_tn(p, it[8]) for p, it in zip(ps, items)]
            for n, (s, ws) in enumerate(blocks):
                rows = pl.ds(_aligned(s * QBLK, QBLK), QBLK)
                win = pl.ds(ws, KWIN)
                dq_ref[rows, :] = _head_select(lane, dqs[2 * n], dqs[2 * n + 1])
                dk_ref[win, :] += _head_select(lanew, dks[2 * n], dks[2 * n + 1])
                dv_ref[win, :] += _head_select(lanew, dvs[2 * n], dvs[2 * n + 1])
            for hh in range(2):
                tot = dss[hh]
                for n in range(1, len(blocks)):
                    tot = tot + dss[2 * n + hh]
                db_ref[hh, var] += tot

        dk_ref[...] = jnp.zeros_like(dk_ref)
        dv_ref[...] = jnp.zeros_like(dv_ref)
        _band_loop(nsub, length, step)

    imap = lambda hp, r: (0, r * (SWA_W // LANES) + hp)
    blk_in = pl.BlockSpec((length, LANES), imap, pipeline_mode=single)
    blk_out = pl.BlockSpec((length, LANES), imap)
    shp = jax.ShapeDtypeStruct(qv.shape, F32)
    return pl.pallas_call(
        body, name=name, grid=(SWA_W // LANES, dilation),
        in_specs=[blk_in] * 6 + [pl.BlockSpec((2, 3, QBLK, KWIN), lambda hp, r: (hp, 0, 0, 0))],
        out_specs=[blk_out, blk_out, blk_out, pl.BlockSpec((2, 3, QBLK, KWIN), lambda hp, r: (hp, 0, 0, 0))],
        out_shape=[shp, shp, shp, jax.ShapeDtypeStruct((SWA_HEADS, 3, QBLK, KWIN), F32)],
        compiler_params=_params(("arbitrary", "arbitrary"), VMEM_LIMIT),
    )(qv, kv, vv, dov, lv, ddv, bias_a)


def _bias_grad(ds2, onehot, tk):
    n = ds2.shape[1]
    nk = n // tk

    def body(a_ref, b_ref, o_ref):
        @pl.when(pl.program_id(0) == 0)
        def _():
            o_ref[...] = jnp.zeros_like(o_ref)

        o_ref[...] += lax.dot_general(a_ref[...], b_ref[...], (((1,), (1,)), ((), ())), precision=HIGHEST,
                                      preferred_element_type=F32)

    return pl.pallas_call(
        body, name="bias_grad", grid=(nk,),
        in_specs=[pl.BlockSpec((SWA_HEADS, tk), lambda k: (0, k)), pl.BlockSpec((REL_BUCKETS, tk), lambda k: (0, k))],
        out_specs=pl.BlockSpec((SWA_HEADS, REL_BUCKETS), lambda k: (0, 0)),
        out_shape=jax.ShapeDtypeStruct((SWA_HEADS, REL_BUCKETS), F32),
        compiler_params=_params(("arbitrary",), VMEM_LIMIT),
    )(ds2, onehot)


def _swa_branch_fwd(qkvb, qw_t, kw_t, rel_bias, bd, tm):
    qkv = _swa_prep_fwd(qkvb, qw_t, kw_t, bd, tm)
    os_, ls_, tabs = [], [], []
    for n, (_, d) in enumerate(PATTERNS):
        bias, onehot = _bias_table(rel_bias, *_band_tables(d))
        o_p, l_p = _swa_fwd(*qkv[3 * n:3 * n + 3], bias, d, f"swa_fwd_d{d}")
        os_.append(o_p)
        ls_.append(l_p)
        tabs.append((bias, onehot))
    o, o16, *lses = _swa_combine(os_, ls_, tm)
    return o, o16, (qkv, lses, tabs)


def _swa_branch_bwd(do, o, saved, qkvb, qw_t, kw_t, bd, tm):
    qkv, lses, tabs = saved
    prep = _swa_bwd_prep(do, o, bd, tm)
    grads, dss, ohs = [], [], []
    for n, ((_, d), (bias, onehot)) in enumerate(zip(PATTERNS, tabs)):
        dq, dk, dv, ds = _swa_bwd(*qkv[3 * n:3 * n + 3], prep[3 + n], lses[n], prep[n], bias, d, f"swa_bwd_d{d}")
        grads += [dq, dk, dv]
        dss.append(ds.reshape(SWA_HEADS, -1))
        ohs.append(onehot)
    dqkvb, dqw, dkw = _swa_prep_bwd(qkvb, qw_t, kw_t, bd, grads, tm)
    dbias = _bias_grad(jnp.concatenate(dss, axis=1), jnp.concatenate(ohs, axis=1), 8192)
    fold = lambda w: jnp.sum(w.reshape(SWA_HEADS, SWA_DIM), axis=0)
    return dqkvb, fold(dqw), fold(dkw), dbias.T


def _mesh_pos():
    return lax.axis_index("x"), lax.axis_index("y"), lax.axis_index("c")


def _other_chips(x, y):
    return [(1 - x, y), (x, 1 - y), (1 - x, 1 - y)]


def _remote(src, dst, send_sem, recv_sem, device):
    return pltpu.make_async_remote_copy(src_ref=src, dst_ref=dst, send_sem=send_sem, recv_sem=recv_sem,
                                        device_id=device, device_id_type=MESH)


def _all_gather(xs):
    n = len(xs)

    def body(*refs):
        ins, outs = refs[:n], refs[n:2 * n]
        send_sems, recv_sems = refs[2 * n:]
        x, y, c = _mesh_pos()
        me = 2 * x + y
        chips = _other_chips(x, y)
        halves = []
        sends = []
        for a in range(n):
            h = ins[a].shape[0] // 2
            mine, other = pl.ds(c * h, h), pl.ds((1 - c) * h, h)
            halves.append((mine, other))
            for j, chip in enumerate(chips):
                cp = _remote(ins[a].at[mine], outs[a].at[me, mine], send_sems.at[a, j], recv_sems.at[a, j], (*chip, c))
                cp.start()
                sends.append(cp)
        for a in range(n):
            mine, _ = halves[a]
            for j, chip in enumerate(chips):
                src = 2 * chip[0] + chip[1]
                landed = outs[a].at[src, mine]
                _remote(landed, landed, send_sems.at[a, j], recv_sems.at[a, j], (x, y, c)).wait_recv()
                fwd = _remote(landed, landed, send_sems.at[a, 3 + j], recv_sems.at[a, 3 + j], (x, y, 1 - c))
                fwd.start()
                sends.append(fwd)
        for a in range(n):
            _, other = halves[a]
            for j, chip in enumerate(chips):
                src = 2 * chip[0] + chip[1]
                landed = outs[a].at[src, other]
                _remote(landed, landed, send_sems.at[a, 3 + j], recv_sems.at[a, 3 + j], (x, y, c)).wait_recv()
        for cp in sends:
            cp.wait_send()

    outs = pl.pallas_call(
        body, name="all_gather_weights",
        in_specs=[ANY] * n, out_specs=[ANY] * n,
        out_shape=[jax.ShapeDtypeStruct((N_SHARDS,) + a.shape, a.dtype) for a in xs],
        scratch_shapes=[pltpu.SemaphoreType.DMA((n, 6)), pltpu.SemaphoreType.DMA((n, 6))],
    )(*xs)
    me = 2 * lax.axis_index("x") + lax.axis_index("y")
    return [lax.dynamic_update_slice_in_dim(o, a[None], me, 0) for o, a in zip(outs, xs)]


def _rs_pair(gs):
    n = len(gs)

    def body(*refs):
        ins, lands = refs[:n], refs[n:2 * n]
        send_sems, recv_sems = refs[2 * n:]
        x, y, c = _mesh_pos()
        cps = []
        for a in range(n):
            h = ins[a].shape[1] // 2
            cp = _remote(ins[a].at[:, pl.ds((1 - c) * h, h), :], lands[a], send_sems.at[a], recv_sems.at[a],
                         (x, y, 1 - c))
            cp.start()
            cps.append(cp)
        for cp in cps:
            cp.wait()

    half = [jax.ShapeDtypeStruct((N_SHARDS, g.shape[1] // 2, g.shape[2]), g.dtype) for g in gs]
    lands = pl.pallas_call(
        body, name="rs_pair", in_specs=[ANY] * n, out_specs=[ANY] * n, out_shape=half,
        scratch_shapes=[pltpu.SemaphoreType.DMA((n,)), pltpu.SemaphoreType.DMA((n,))],
    )(*gs)
    c = lax.axis_index("c")
    owns = [lax.dynamic_slice_in_dim(g, c * (g.shape[1] // 2), g.shape[1] // 2, 1) for g in gs]
    return owns + list(lands)


def _rs_chips(ss):
    n = len(ss)

    def body(*refs):
        ins, outs = refs[:n], refs[n:2 * n]
        send_sems, recv_sems = refs[2 * n:]
        x, y, c = _mesh_pos()
        me = 2 * x + y
        chips = _other_chips(x, y)
        cps = []
        for a in range(n):
            for j, chip in enumerate(chips):
                dst_chip = 2 * chip[0] + chip[1]
                cp = _remote(ins[a].at[dst_chip], outs[a].at[me], send_sems.at[a, j], recv_sems.at[a, j], (*chip, c))
                cp.start()
                cps.append(cp)
        for a in range(n):
            for j, chip in enumerate(chips):
                src = 2 * chip[0] + chip[1]
                _remote(outs[a].at[src], outs[a].at[src], send_sems.at[a, j], recv_sems.at[a, j], (x, y, c)).wait_recv()
        for cp in cps:
            cp.wait_send()

    outs = pl.pallas_call(
        body, name="rs_chips", in_specs=[ANY] * n, out_specs=[ANY] * n,
        out_shape=[jax.ShapeDtypeStruct(s.shape, s.dtype) for s in ss],
        scratch_shapes=[pltpu.SemaphoreType.DMA((n, 3)), pltpu.SemaphoreType.DMA((n, 3))],
    )(*ss)
    me = 2 * lax.axis_index("x") + lax.axis_index("y")
    return [lax.dynamic_update_slice_in_dim(o, lax.dynamic_slice_in_dim(s, me, 1, 0), me, 0) for o, s in zip(outs, ss)]


def _rs_join(fs):
    n = len(fs)

    def body(*refs):
        ins, outs = refs[:n], refs[n:2 * n]
        send_sems, recv_sems = refs[2 * n:]
        x, y, c = _mesh_pos()
        cps = []
        for a in range(n):
            h = ins[a].shape[0]
            cp = _remote(ins[a], outs[a].at[pl.ds(c * h, h)], send_sems.at[a], recv_sems.at[a], (x, y, 1 - c))
            cp.start()
            cps.append(cp)
        for cp in cps:
            cp.wait()

    outs = pl.pallas_call(
        body, name="rs_join", in_specs=[ANY] * n, out_specs=[ANY] * n,
        out_shape=[jax.ShapeDtypeStruct((2 * f.shape[0], f.shape[1]), f.dtype) for f in fs],
        scratch_shapes=[pltpu.SemaphoreType.DMA((n,)), pltpu.SemaphoreType.DMA((n,))],
    )(*fs)
    c = lax.axis_index("c")
    return [lax.dynamic_update_slice_in_dim(o, f, c * f.shape[0], 0) for o, f in zip(outs, fs)]


def _gather_exchange(xs):
    def start(cin, cout, send_sems, recv_sems):
        x, y, c = _mesh_pos()
        me = 2 * x + y
        for a, (src, dst) in enumerate(zip(cin, cout)):
            h = src.shape[0] // 2
            mine = pl.ds(c * h, h)
            for j, chip in enumerate(_other_chips(x, y)):
                _remote(src.at[mine], dst.at[me, mine], send_sems.at[a, j], recv_sems.at[a, j], (*chip, c)).start()

    def finish(cin, cout, send_sems, recv_sems):
        x, y, c = _mesh_pos()
        for a, dst in enumerate(cout):
            h = dst.shape[1] // 2
            for j, chip in enumerate(_other_chips(x, y)):
                landed = dst.at[2 * chip[0] + chip[1], pl.ds(c * h, h)]
                _remote(landed, landed, send_sems.at[a, j], recv_sems.at[a, j], (x, y, c)).wait()

    return _Exchange(tuple(xs), tuple(jax.ShapeDtypeStruct((N_SHARDS,) + a.shape, a.dtype) for a in xs), start, finish)


def _gather_forward(gs, xs):
    n = len(gs)

    def body(*refs):
        outs = refs[n:2 * n]
        send_sems, recv_sems = refs[2 * n:]
        x, y, c = _mesh_pos()
        chips = _other_chips(x, y)
        cps = []
        for a in range(n):
            h = outs[a].shape[1] // 2
            for j, chip in enumerate(chips):
                landed = outs[a].at[2 * chip[0] + chip[1], pl.ds(c * h, h)]
                cp = _remote(landed, landed, send_sems.at[a, j], recv_sems.at[a, j], (x, y, 1 - c))
                cp.start()
                cps.append(cp)
        for a in range(n):
            h = outs[a].shape[1] // 2
            for j, chip in enumerate(chips):
                other = outs[a].at[2 * chip[0] + chip[1], pl.ds((1 - c) * h, h)]
                _remote(other, other, send_sems.at[a, j], recv_sems.at[a, j], (x, y, c)).wait_recv()
        for cp in cps:
            cp.wait_send()

    outs = pl.pallas_call(
        body, name="gather_forward", in_specs=[ANY] * n, out_specs=[ANY] * n,
        out_shape=[jax.ShapeDtypeStruct(g.shape, g.dtype) for g in gs],
        input_output_aliases={i: i for i in range(n)},
        scratch_shapes=[pltpu.SemaphoreType.DMA((n, 3)), pltpu.SemaphoreType.DMA((n, 3))],
    )(*gs)
    me = 2 * lax.axis_index("x") + lax.axis_index("y")
    return [lax.dynamic_update_slice_in_dim(o, a[None], me, 0) for o, a in zip(outs, xs)]


def _scatter_exchange(ss):
    def start(cin, cout, send_sems, recv_sems):
        x, y, c = _mesh_pos()
        me = 2 * x + y
        for a, (src, dst) in enumerate(zip(cin, cout)):
            for j, chip in enumerate(_other_chips(x, y)):
                _remote(src.at[2 * chip[0] + chip[1]], dst.at[me], send_sems.at[a, j], recv_sems.at[a, j],
                        (*chip, c)).start()

    def finish(cin, cout, send_sems, recv_sems):
        x, y, c = _mesh_pos()
        for a, dst in enumerate(cout):
            for j, chip in enumerate(_other_chips(x, y)):
                slot = dst.at[2 * chip[0] + chip[1]]
                _remote(slot, slot, send_sems.at[a, j], recv_sems.at[a, j], (x, y, c)).wait()

    return _Exchange(tuple(ss), tuple(jax.ShapeDtypeStruct(s.shape, s.dtype) for s in ss), start, finish)


def _own_slots(slots, ss):
    me = 2 * lax.axis_index("x") + lax.axis_index("y")
    return [lax.dynamic_update_slice_in_dim(o, lax.dynamic_slice_in_dim(s, me, 1, 0), me, 0) for o, s in zip(slots, ss)]


def _add_pair(a, b, name):
    nj, h, c = a.shape

    def body(a_ref, b_ref, o_ref):
        o_ref[...] = (a_ref[...].astype(F32) + b_ref[...].astype(F32)).astype(BF16)

    blk = pl.BlockSpec((1, h, c), lambda j: (j, 0, 0))
    return pl.pallas_call(body, name=name, grid=(nj,), in_specs=[blk, blk], out_specs=blk,
                          out_shape=jax.ShapeDtypeStruct(a.shape, BF16),
                          compiler_params=_params(("arbitrary",), VMEM_LIMIT))(a, b)


def _sum_slots(l2, name):
    nj, h, c = l2.shape
    th = h // 2 if h % 32 == 0 else h

    def body(i_ref, o_ref):
        acc = i_ref[0].astype(F32)
        for s in range(1, nj):
            acc = acc + i_ref[s].astype(F32)
        o_ref[...] = acc

    return pl.pallas_call(body, name=name, grid=(h // th,),
                          in_specs=[pl.BlockSpec((nj, th, c), lambda i: (0, i, 0))],
                          out_specs=pl.BlockSpec((th, c), lambda i: (i, 0)),
                          out_shape=jax.ShapeDtypeStruct((h, c), F32),
                          compiler_params=_params(("arbitrary",), VMEM_LIMIT))(l2)


def _all_reduce_small(p):
    r = p.shape[0]

    def body(p_ref, o_ref, buf, send_sems, recv_sems):
        x, y, c = _mesh_pos()
        me = 4 * x + 2 * y + c
        buf[me] = p_ref[...]
        cps = []
        k = 0
        for fx in range(2):
            for fy in range(2):
                for fc in range(2):
                    if fx + fy + fc == 0:
                        continue
                    peer = (1 - x if fx else x, 1 - y if fy else y, 1 - c if fc else c)
                    peer_id = 4 * peer[0] + 2 * peer[1] + peer[2]
                    cp = _remote(p_ref, buf.at[me], send_sems.at[k], recv_sems.at[k], peer)
                    cp.start()
                    cps.append((cp, peer_id, k))
                    k += 1
        for cp, peer_id, k in cps:
            _remote(p_ref, buf.at[peer_id], send_sems.at[k], recv_sems.at[k], (x, y, c)).wait_recv()
        for cp, _, _ in cps:
            cp.wait_send()
        acc = buf[0]
        for s in range(1, 8):
            acc = acc + buf[s]
        o_ref[...] = acc

    vm = pl.BlockSpec(memory_space=pltpu.VMEM)
    return pl.pallas_call(
        body, name="all_reduce_small", in_specs=[vm], out_specs=vm,
        out_shape=jax.ShapeDtypeStruct(p.shape, F32),
        scratch_shapes=[pltpu.VMEM((8, r, LANES), F32), pltpu.SemaphoreType.DMA((7,)), pltpu.SemaphoreType.DMA((7,))],
    )(p)


def _adamw(w, g, m, v, name):
    r, c = w.shape
    tr = max(d for d in range(8, min(r, 256) + 1, 8) if r % d == 0)
    c1 = 1.0 / (1.0 - ADAM_B1 ** ADAM_STEP)
    c2 = 1.0 / (1.0 - ADAM_B2 ** ADAM_STEP)

    def body(w_ref, g_ref, m_ref, v_ref, d_ref, nm_ref, nv_ref):
        gv = g_ref[...]
        nm = ADAM_B1 * m_ref[...] + (1.0 - ADAM_B1) * gv
        nv = ADAM_B2 * v_ref[...] + (1.0 - ADAM_B2) * (gv * gv)
        d_ref[...] = -ADAM_LR * ((nm * c1) / (jnp.sqrt(nv * c2) + ADAM_EPS) + ADAM_WD * w_ref[...])
        nm_ref[...] = nm
        nv_ref[...] = nv

    blk = pl.BlockSpec((tr, c), lambda i: (i, 0))
    shp = jax.ShapeDtypeStruct((r, c), F32)
    return pl.pallas_call(body, name=name, grid=(r // tr,), in_specs=[blk] * 4, out_specs=[blk] * 3,
                          out_shape=[shp, shp, shp], compiler_params=_params(("arbitrary",), VMEM_LIMIT))(w, g, m, v)


PACK_UNIT = 8 * LANES


def _pack(arrs):
    parts = []
    for a in arrs:
        f = a.reshape(-1).astype(F32)
        parts.append(jnp.pad(f, (0, (-f.shape[0]) % PACK_UNIT)).reshape(-1, LANES))
    return jnp.concatenate(parts, axis=0)


def _unpack(m, shapes):
    outs, row = [], 0
    for s in shapes:
        n = int(np.prod(s))
        rows = -(-n // PACK_UNIT) * 8
        outs.append(m[row:row + rows].reshape(-1)[:n].reshape(s))
        row += rows
    return outs


WEIGHTS = ["ffn1_norm", "ffn1_w_gate", "ffn1_w_up", "ffn1_w_down", "mix_norm", "w_in", "conv_w", "a_log", "dt_bias",
           "gdn_norm_w", "q_norm_w", "k_norm_w", "rel_bias", "w_out", "ffn2_norm", "ffn2_w_gate", "ffn2_w_up",
           "ffn2_w_down", "final_norm"]
BIG = ["ffn1_w_gate", "ffn1_w_up", "ffn1_w_down", "w_in", "w_out", "ffn2_w_gate", "ffn2_w_up", "ffn2_w_down"]
SMALL = [n for n in WEIGHTS if n not in BIG]
N_IN_COLS = 3600
TM = 256
TE = 512
TK = 2048


def kernel(x, ffn1_norm, ffn1_w_gate, ffn1_w_up, ffn1_w_down, mix_norm, w_in, conv_w, a_log, dt_bias, gdn_norm_w, q_norm_w, k_norm_w, rel_bias, w_out, ffn2_norm, ffn2_w_gate, ffn2_w_up, ffn2_w_down, final_norm, loss_target, m_ffn1_norm, m_ffn1_w_gate, m_ffn1_w_up, m_ffn1_w_down, m_mix_norm, m_w_in, m_conv_w, m_a_log, m_dt_bias, m_gdn_norm_w, m_q_norm_w, m_k_norm_w, m_rel_bias, m_w_out, m_ffn2_norm, m_ffn2_w_gate, m_ffn2_w_up, m_ffn2_w_down, m_final_norm, v_ffn1_norm, v_ffn1_w_gate, v_ffn1_w_up, v_ffn1_w_down, v_mix_norm, v_w_in, v_conv_w, v_a_log, v_dt_bias, v_gdn_norm_w, v_q_norm_w, v_k_norm_w, v_rel_bias, v_w_out, v_ffn2_norm, v_ffn2_w_gate, v_ffn2_w_up, v_ffn2_w_down, v_final_norm):
    p = dict(locals())
    xs, target = x[0], loss_target[0]
    t, d = xs.shape
    nc = t // CHUNK
    tk = min(TK, t)
    me = 2 * lax.axis_index("x") + lax.axis_index("y")

    first = ["ffn1_w_gate", "ffn1_w_up", "ffn1_w_down"]
    later = [n for n in BIG if n not in first] + ["conv_w"]
    shards = {n: p[n][0].astype(BF16) for n in BIG}
    shards["conv_w"] = conv_w[0]
    gw = dict(zip(first, _all_gather([shards[n] for n in first])))
    f1 = (gw["ffn1_w_gate"], gw["ffn1_w_up"], gw["ffn1_w_down"])
    (x1, xn1, g1, u1), landed = _ffn_fwd(xs, ffn1_norm, *f1, TM, "ffn1_fwd",
                                         exchange=_gather_exchange([shards[n] for n in later]))
    gw.update(zip(later, _gather_forward(landed, [shards[n] for n in later])))
    w_in_full = jnp.transpose(gw["w_in"], (1, 0, 2)).reshape(d, N_IN_COLS)
    wp = jnp.concatenate([w_in_full[:, :2048], jnp.pad(w_in_full[:, 2048:2064], ((0, 0), (0, LANES - 16))),
                          w_in_full[:, 2064:]], axis=1)
    w_out_full = gw["w_out"].reshape(d, d)
    conv_rows = conv_w.shape[1]
    cw = jnp.pad(gw["conv_w"].reshape(N_SHARDS * conv_rows, CONV_TAPS).T, ((0, 8 - CONV_TAPS), (0, 0)))
    gp = jnp.pad(jnp.stack([a_log.reshape(8), dt_bias.reshape(8)]), ((0, 6), (0, LANES - 8)))
    gdn_w = gdn_norm_w.reshape(1, GDN_DIM)
    qw_t = jnp.tile(q_norm_w.reshape(1, SWA_DIM), (1, SWA_HEADS))
    kw_t = jnp.tile(k_norm_w.reshape(1, SWA_DIM), (1, SWA_HEADS))
    bd = jnp.asarray(np.kron(np.eye(2), np.full((SWA_DIM, SWA_DIM), 1.0 / SWA_DIM)), F32)
    f2 = (gw["ffn2_w_gate"], gw["ffn2_w_up"], gw["ffn2_w_down"])

    hn, qkva, z, ab, qkvb = _mix_in_fwd(x1, mix_norm, wp, TM)
    qkvc, gb = _gdn_prep_fwd(qkva, cw, ab, gp, TM)
    gbt = jnp.transpose(gb[:, :16].reshape(nc, CHUNK, 16), (0, 2, 1))
    o_f, o_b, gdn_saved = _gdn_fwd(qkvc, gb, gbt)
    oa = _gdn_post_fwd(o_f, o_b, z, gdn_w, TE)
    o_swa, o_swa16, swa_saved = _swa_branch_fwd(qkvb, qw_t, kw_t, rel_bias, bd, TE)
    x2 = _mix_out_fwd(x1, oa, o_swa, w_out_full, TM)
    (x3, xn2, g2, u2), _ = _ffn_fwd(x2, ffn2_norm, *f2, TM, "ffn2_fwd")
    dx3, loss_part, d_final = _final_loss(x3, final_norm, target, TE)

    def pair_sums(partials, tag):
        pair = _rs_pair(partials)
        k = len(partials)
        return [_add_pair(pair[i], pair[k + i], f"rs_add_{tag}{i}") for i in range(k)]

    (dx2, dyh2, dg2, du2, h2, d_nw2), _ = _ffn_bwd_dx(dx3, x2, ffn2_norm, g2, u2, *f2, TM, "ffn2_bwd_dx")
    dwg2 = _matmul_tn(xn2, dg2, tk, "ffn2_dwg")
    dwu2 = _matmul_tn(xn2, du2, tk, "ffn2_dwu")
    dwd2 = _matmul_tn(h2, dyh2, tk, "ffn2_dwd")
    sums_f2 = pair_sums([dwg2, dwu2, dwd2], "a")
    doa, dob, dx2b = _mix_out_bwd(dx2, w_out_full, TM)
    dwo = jnp.concatenate([_matmul_tn(oa, dx2b, tk, "w_out_dw_a")[0], _matmul_tn(o_swa16, dx2b, tk, "w_out_dw_b")[0]],
                          axis=0).reshape(N_SHARDS, d // N_SHARDS, d)
    do_g, dz, d_gdnw = _gdn_post_bwd(doa, o_f, o_b, z, gdn_w, TE)
    (dqkvc, dgates), slots_f2 = _gdn_bwd(qkvc, gb, gbt, do_g, gdn_saved, exchange=_scatter_exchange(sums_f2))
    dqkva, dab, dcw, dgp = _gdn_prep_bwd(qkva, cw, ab, gp, dqkvc, dgates, TM)
    dqkvb, d_qw, d_kw, d_rel = _swa_branch_bwd(dob, o_swa, swa_saved, qkvb, qw_t, kw_t, bd, TE)
    dpieces = (dqkva, dz, dab, dqkvb)
    dx1, d_mixnw = _mix_in_bwd_dx(dx2, x1, mix_norm, dpieces, wp, TM)
    dwp = [_matmul_tn(hn, dp, tk, f"w_in_dw_{i}")[0] for i, dp in enumerate(dpieces)]
    dw_in = jnp.concatenate([dwp[0], dwp[1], dwp[2][:, :16], dwp[3]], axis=1)
    dw_in = jnp.transpose(dw_in.reshape(d, N_SHARDS, N_IN_COLS // N_SHARDS), (1, 0, 2))
    sums_mix = pair_sums([dw_in, dwo], "b")
    (gx, dyh1, dg1, du1, h1, d_nw1), slots_mix = _ffn_bwd_dx(dx1, xs, ffn1_norm, g1, u1, *f1, TM, "ffn1_bwd_dx",
                                                              exchange=_scatter_exchange(sums_mix))
    dwg1 = _matmul_tn(xn1, dg1, tk, "ffn1_dwg")
    dwu1 = _matmul_tn(xn1, du1, tk, "ffn1_dwu")
    dwd1 = _matmul_tn(h1, dyh1, tk, "ffn1_dwd")
    slots_f1 = _rs_chips(pair_sums([dwg1, dwu1, dwd1], "c"))
    slots = slots_f1 + _own_slots(slots_mix, sums_mix) + _own_slots(slots_f2, sums_f2)
    halves = [_sum_slots(s, f"rs_sum_{i}") for i, s in enumerate(slots)]
    g_big = dict(zip(BIG, _rs_join(halves)))

    small_partial = {"ffn1_norm": d_nw1, "mix_norm": d_mixnw, "a_log": dgp[0, 0:8], "dt_bias": dgp[1, 0:8],
                     "gdn_norm_w": d_gdnw, "q_norm_w": d_qw, "k_norm_w": d_kw, "rel_bias": d_rel,
                     "ffn2_norm": d_nw2, "final_norm": d_final, "conv_w": dcw[0:CONV_TAPS].T}
    red = _all_reduce_small(_pack([small_partial[n] for n in SMALL] + [loss_part[0, 0:1]]))
    full_shapes = [p[n].shape if n != "conv_w" else (N_SHARDS * conv_rows, CONV_TAPS) for n in SMALL]
    red_parts = _unpack(red, full_shapes + [(1,)])
    loss = red_parts[-1].reshape(())
    g_small = dict(zip(SMALL, red_parts[:-1]))
    g_small["conv_w"] = lax.dynamic_slice_in_dim(g_small["conv_w"], me * conv_rows, conv_rows, 0).reshape(conv_w.shape)

    grads, deltas, new_m, new_v = {}, {}, {}, {}
    for n in BIG:
        grads[n] = g_big[n][None]
        dl, nm, nv = _adamw(p[n][0], g_big[n], p["m_" + n][0], p["v_" + n][0], "adamw_" + n)
        deltas[n], new_m[n], new_v[n] = dl[None], nm[None], nv[None]
    packed = [_pack([src[n] for n in SMALL]) for src in
              ({n: p[n] for n in SMALL}, g_small, {n: p["m_" + n] for n in SMALL}, {n: p["v_" + n] for n in SMALL})]
    small_shapes = [p[n].shape for n in SMALL]
    for dst, arr in zip((deltas, new_m, new_v), _adamw(*packed, "adamw_small")):
        dst.update(zip(SMALL, _unpack(arr, small_shapes)))
    grads.update(g_small)

    return (loss, gx[None], *[grads[n] for n in WEIGHTS], *[deltas[n] for n in WEIGHTS],
            *[new_m[n] for n in WEIGHTS], *[new_v[n] for n in WEIGHTS])
```

```python
import math
from typing import Callable, NamedTuple

import numpy as np
import jax
import jax.numpy as jnp
from jax import lax
from jax.experimental import pallas as pl
from jax.experimental.pallas import tpu as pltpu

F32 = jnp.float32
BF16 = jnp.bfloat16
HIGHEST = lax.Precision.HIGHEST
MESH = pl.DeviceIdType.MESH

EPS = 1e-6
NEG_BIG = -1e30
GDN_HEADS = 4
GDN_DIM = 128
CHUNK = 64
SWA_HEADS = 8
SWA_DIM = 64
PATTERNS = ((128, 1), (512, 4), (2048, 16))
RADIUS = 64
REL_BUCKETS = 32
REL_MAX_DISTANCE = 1024
CONV_TAPS = 5
N_SHARDS = 4
LANES = 128
VMEM_LIMIT = 56 * 1024 * 1024

ADAM_LR, ADAM_B1, ADAM_B2, ADAM_EPS, ADAM_WD, ADAM_STEP = 0.001, 0.9, 0.999, 1e-08, 0.01, 10


def _params(sem=None, vmem=None):
    return pltpu.CompilerParams(dimension_semantics=sem, vmem_limit_bytes=vmem)


def _resident(shape):
    nd = len(shape)
    return pl.BlockSpec(shape, lambda *_: (0,) * nd, pipeline_mode=pl.Buffered(1))


ANY = pl.BlockSpec(memory_space=pl.ANY)


class _Exchange(NamedTuple):
    arrays: tuple
    out_shape: tuple
    start: Callable
    finish: Callable


def _grid_call(body, name, nsteps, in_specs, out_specs, out_shape, operands, scratch=(), exchange=None):
    params = _params(("arbitrary",), VMEM_LIMIT)
    if exchange is None:
        res = pl.pallas_call(body, name=name, grid=(nsteps,), in_specs=list(in_specs), out_specs=list(out_specs),
                             out_shape=list(out_shape), scratch_shapes=list(scratch), compiler_params=params)(*operands)
        return list(res), []
    n_in, n_out, k, n_scr = len(in_specs), len(out_specs), len(exchange.arrays), len(scratch)

    def wrapped(*refs):
        ins, cin = refs[:n_in], refs[n_in:n_in + k]
        outs, cout = refs[n_in + k:n_in + k + n_out], refs[n_in + k + n_out:n_in + 2 * k + n_out]
        rest = refs[n_in + 2 * k + n_out:]
        scr, (send_sems, recv_sems) = rest[:n_scr], rest[n_scr:]

        @pl.when(pl.program_id(0) == 0)
        def _():
            exchange.start(cin, cout, send_sems, recv_sems)

        body(*ins, *outs, *scr)

        @pl.when(pl.program_id(0) == nsteps - 1)
        def _():
            exchange.finish(cin, cout, send_sems, recv_sems)

    res = pl.pallas_call(
        wrapped, name=name, grid=(nsteps,), in_specs=list(in_specs) + [ANY] * k, out_specs=list(out_specs) + [ANY] * k,
        out_shape=list(out_shape) + list(exchange.out_shape),
        scratch_shapes=list(scratch) + [pltpu.SemaphoreType.DMA((k, 3)), pltpu.SemaphoreType.DMA((k, 3))],
        compiler_params=params)(*operands, *exchange.arrays)
    return list(res[:n_out]), list(res[n_out:])


def _dot(a, b):
    return jnp.dot(a.astype(BF16), b.astype(BF16), preferred_element_type=F32)


def _dot_nt(a, b):
    return lax.dot_general(a.astype(BF16), b.astype(BF16), (((1,), (1,)), ((), ())), preferred_element_type=F32)


def _dot_tn(a, b):
    return lax.dot_general(a.astype(BF16), b.astype(BF16), (((0,), (0,)), ((), ())), preferred_element_type=F32)


def _dot_hi(a, b):
    return jnp.dot(a, b, preferred_element_type=F32, precision=HIGHEST)


def _sigmoid(x):
    return 1.0 / (1.0 + jnp.exp(-x))


def _rstd(xf):
    return lax.rsqrt(jnp.mean(xf * xf, axis=-1, keepdims=True) + EPS)


def _rms_bwd(xf, r, nw, dxn):
    xhat = xf * r
    dxh = dxn * nw
    dx = r * (dxh - xhat * jnp.mean(dxh * xhat, axis=-1, keepdims=True))
    return dx, jnp.sum(dxn * xhat, axis=0, keepdims=True)


def _ffn_fwd(x, nw, wg, wu, wd, tm, name, exchange=None):
    t, d = x.shape
    nj, fs, _ = wg.shape

    def body(x_ref, nw_ref, wg_ref, wu_ref, wd_ref, y_ref, xn_ref, g_ref, u_ref):
        xf = x_ref[...]
        xn = (xf * _rstd(xf) * nw_ref[...]).astype(BF16)
        xn_ref[...] = xn
        acc = jnp.zeros((tm, d), F32)
        for j in range(nj):
            g = _dot_nt(xn, wg_ref[j])
            u = _dot_nt(xn, wu_ref[j])
            h = (g * _sigmoid(g) * u).astype(BF16)
            acc = acc + jnp.dot(h, wd_ref[j], preferred_element_type=F32)
            g_ref[j] = g.astype(BF16)
            u_ref[j] = u.astype(BF16)
        y_ref[...] = xf + 0.5 * acc

    row = pl.BlockSpec((tm, d), lambda i: (i, 0))
    act = pl.BlockSpec((nj, tm, fs), lambda i: (0, i, 0))
    return _grid_call(
        body, name, t // tm,
        [row, _resident((1, d)), _resident(wg.shape), _resident(wu.shape), _resident(wd.shape)],
        [row, row, act, act],
        [jax.ShapeDtypeStruct((t, d), F32), jax.ShapeDtypeStruct((t, d), BF16),
         jax.ShapeDtypeStruct((nj, t, fs), BF16), jax.ShapeDtypeStruct((nj, t, fs), BF16)],
        (x, nw, wg, wu, wd), exchange=exchange)


def _ffn_bwd_dx(dy, x, nw, g, u, wg, wu, wd, tm, name, exchange=None):
    t, d = x.shape
    nj, fs, _ = wg.shape

    def body(dy_ref, x_ref, nw_ref, g_ref, u_ref, wg_ref, wu_ref, wd_ref,
             dx_ref, dyh_ref, dg_ref, du_ref, h_ref, dnw_ref):
        @pl.when(pl.program_id(0) == 0)
        def _():
            dnw_ref[...] = jnp.zeros_like(dnw_ref)

        dyv = dy_ref[...]
        dyh = (0.5 * dyv).astype(BF16)
        dyh_ref[...] = dyh
        dxn = jnp.zeros((tm, d), F32)
        dh_next = _dot_nt(dyh, wd_ref[0])
        for j in range(nj):
            dh = dh_next
            gv = g_ref[j].astype(F32)
            uv = u_ref[j].astype(F32)
            sg = _sigmoid(gv)
            si = gv * sg
            dg = (dh * uv * (sg * (1.0 + gv * (1.0 - sg)))).astype(BF16)
            du = (dh * si).astype(BF16)
            if j + 1 < nj:
                dh_next = _dot_nt(dyh, wd_ref[j + 1])
            h_ref[j] = (si * uv).astype(BF16)
            dg_ref[j] = dg
            du_ref[j] = du
            dxn = dxn + _dot(dg, wg_ref[j]) + _dot(du, wu_ref[j])
        xf = x_ref[...]
        dxr, dnw = _rms_bwd(xf, _rstd(xf), nw_ref[...], dxn)
        dx_ref[...] = dyv + dxr
        dnw_ref[...] += dnw

    row = pl.BlockSpec((tm, d), lambda i: (i, 0))
    act = pl.BlockSpec((nj, tm, fs), lambda i: (0, i, 0))
    act_shape = jax.ShapeDtypeStruct((nj, t, fs), BF16)
    return _grid_call(
        body, name, t // tm,
        [row, row, _resident((1, d)), act, act, _resident(wg.shape), _resident(wu.shape), _resident(wd.shape)],
        [row, row, act, act, act, pl.BlockSpec((1, d), lambda i: (0, 0))],
        [jax.ShapeDtypeStruct((t, d), F32), jax.ShapeDtypeStruct((t, d), BF16),
         act_shape, act_shape, act_shape, jax.ShapeDtypeStruct((1, d), F32)],
        (dy, x, nw, g, u, wg, wu, wd), exchange=exchange)


def _matmul_tn(a, b, tk, name):
    a3, b3 = a.ndim == 3, b.ndim == 3
    nj = a.shape[0] if a3 else (b.shape[0] if b3 else 1)
    t, m = a.shape[-2:]
    n = b.shape[-1]
    nt = t // tk

    def body(a_ref, b_ref, o_ref, acc_ref):
        k = pl.program_id(1)

        @pl.when(k == 0)
        def _():
            acc_ref[...] = jnp.zeros_like(acc_ref)

        acc_ref[...] += lax.dot_general(a_ref[...], b_ref[...], (((0,), (0,)), ((), ())),
                                        preferred_element_type=F32)

        @pl.when(k == nt - 1)
        def _():
            o_ref[...] = acc_ref[...].astype(o_ref.dtype)

    a_spec = (pl.BlockSpec((None, tk, m), lambda j, k: (j, k, 0)) if a3
              else pl.BlockSpec((tk, m), lambda j, k: (k, 0)))
    b_spec = (pl.BlockSpec((None, tk, n), lambda j, k: (j, k, 0)) if b3
              else pl.BlockSpec((tk, n), lambda j, k: (k, 0)))
    return pl.pallas_call(
        body, name=name, grid=(nj, nt),
        in_specs=[a_spec, b_spec],
        out_specs=pl.BlockSpec((None, m, n), lambda j, k: (j, 0, 0)),
        out_shape=jax.ShapeDtypeStruct((nj, m, n), BF16),
        scratch_shapes=[pltpu.VMEM((m, n), F32)],
        compiler_params=_params(("arbitrary", "arbitrary"), VMEM_LIMIT),
    )(a, b)


P_QKVA, P_Z, P_AB, P_QKVB = (0, 1536), (1536, 2048), (2048, 2176), (2176, 3712)
P_PIECES = (P_QKVA, P_Z, P_AB, P_QKVB)
P_COLS = 3712


def _mix_in_fwd(x1, nw, wp, tm):
    t, d = x1.shape

    def body(x_ref, nw_ref, w_ref, hn_ref, *outs):
        xf = x_ref[...]
        xn = (xf * _rstd(xf) * nw_ref[...]).astype(BF16)
        hn_ref[...] = xn
        for (a, b), o_ref in zip(P_PIECES, outs):
            o_ref[...] = _dot_nt(xn, w_ref[a:b, :])

    row = pl.BlockSpec((tm, d), lambda i: (i, 0))
    return pl.pallas_call(
        body, name="mix_in_fwd", grid=(t // tm,),
        in_specs=[row, _resident((1, d)), _resident(wp.shape)],
        out_specs=[row] + [pl.BlockSpec((tm, b - a), lambda i: (i, 0)) for a, b in P_PIECES],
        out_shape=[jax.ShapeDtypeStruct((t, d), BF16)]
                  + [jax.ShapeDtypeStruct((t, b - a), F32) for a, b in P_PIECES],
        compiler_params=_params(("arbitrary",), VMEM_LIMIT),
    )(x1, nw, wp)


def _mix_in_bwd_dx(dx, x1, nw, dpieces, wp, tm):
    t, d = x1.shape

    def body(dx_ref, x_ref, nw_ref, p0, p1, p2, p3, w_ref, o_ref, dnw_ref):
        @pl.when(pl.program_id(0) == 0)
        def _():
            dnw_ref[...] = jnp.zeros_like(dnw_ref)

        dh = jnp.zeros((tm, d), F32)
        for (a, b), p_ref in zip(P_PIECES, (p0, p1, p2, p3)):
            dh = dh + _dot(p_ref[...], w_ref[a:b, :])
        xf = x_ref[...]
        dxr, dnw = _rms_bwd(xf, _rstd(xf), nw_ref[...], dh)
        o_ref[...] = dx_ref[...] + dxr
        dnw_ref[...] += dnw

    row = pl.BlockSpec((tm, d), lambda i: (i, 0))
    return pl.pallas_call(
        body, name="mix_in_bwd_dx", grid=(t // tm,),
        in_specs=[row, row, _resident((1, d))]
                 + [pl.BlockSpec((tm, b - a), lambda i: (i, 0)) for a, b in P_PIECES] + [_resident(wp.shape)],
        out_specs=[row, pl.BlockSpec((1, d), lambda i: (0, 0))],
        out_shape=[jax.ShapeDtypeStruct((t, d), F32), jax.ShapeDtypeStruct((1, d), F32)],
        compiler_params=_params(("arbitrary",), VMEM_LIMIT),
    )(dx, x1, nw, *dpieces, wp)


def _mix_out_fwd(x1, oa, ob, w, tm):
    t, d = x1.shape
    half = oa.shape[1]

    def body(x_ref, oa_ref, ob_ref, w_ref, o_ref):
        o_ref[...] = (x_ref[...] + _dot(oa_ref[...], w_ref[0:half, :]) + _dot(ob_ref[...], w_ref[half:2 * half, :]))

    row = pl.BlockSpec((tm, d), lambda i: (i, 0))
    hrow = pl.BlockSpec((tm, half), lambda i: (i, 0))
    return pl.pallas_call(
        body, name="mix_out_fwd", grid=(t // tm,),
        in_specs=[row, hrow, hrow, _resident(w.shape)],
        out_specs=row, out_shape=jax.ShapeDtypeStruct((t, d), F32),
        compiler_params=_params(("arbitrary",), VMEM_LIMIT),
    )(x1, oa, ob, w)


def _mix_out_bwd(dx2, w, tm):
    t, d = dx2.shape
    half = w.shape[0] // 2

    def body(dx_ref, w_ref, doa_ref, dob_ref, dxb_ref):
        dxb = dx_ref[...].astype(BF16)
        dxb_ref[...] = dxb
        doa_ref[...] = _dot_nt(dxb, w_ref[0:half, :])
        dob_ref[...] = _dot_nt(dxb, w_ref[half:2 * half, :])

    row = pl.BlockSpec((tm, d), lambda i: (i, 0))
    hrow = pl.BlockSpec((tm, half), lambda i: (i, 0))
    return pl.pallas_call(
        body, name="mix_out_bwd", grid=(t // tm,),
        in_specs=[row, _resident(w.shape)],
        out_specs=[hrow, hrow, row],
        out_shape=[jax.ShapeDtypeStruct((t, half), F32), jax.ShapeDtypeStruct((t, half), F32),
                   jax.ShapeDtypeStruct((t, d), BF16)],
        compiler_params=_params(("arbitrary",), VMEM_LIMIT),
    )(dx2, w)


def _final_loss(x3, fw, target, tm):
    t, d = x3.shape

    def body(x_ref, w_ref, t_ref, dx_ref, loss_ref, dw_ref):
        @pl.when(pl.program_id(0) == 0)
        def _():
            loss_ref[...] = jnp.zeros_like(loss_ref)
            dw_ref[...] = jnp.zeros_like(dw_ref)

        xf = x_ref[...]
        r = _rstd(xf)
        err = xf * r * w_ref[...] - t_ref[...]
        loss_ref[...] += 0.5 * jnp.sum(jnp.mean(err * err, axis=-1, keepdims=True), axis=0, keepdims=True)
        dxr, dw = _rms_bwd(xf, r, w_ref[...], err * (1.0 / d))
        dx_ref[...] = dxr
        dw_ref[...] += dw

    row = pl.BlockSpec((tm, d), lambda i: (i, 0))
    return pl.pallas_call(
        body, name="final_loss", grid=(t // tm,),
        in_specs=[row, _resident((1, d)), row],
        out_specs=[row, pl.BlockSpec((1, LANES), lambda i: (0, 0)), pl.BlockSpec((1, d), lambda i: (0, 0))],
        out_shape=[jax.ShapeDtypeStruct((t, d), F32), jax.ShapeDtypeStruct((1, LANES), F32),
                   jax.ShapeDtypeStruct((1, d), F32)],
        compiler_params=_params(("arbitrary",), VMEM_LIMIT),
    )(x3, fw, target)


HALO = 8


def _halo_row_specs(tr, cols, nrow8):
    per = tr // HALO
    return [pl.BlockSpec((tr, cols), lambda i: (i, 0)),
            pl.BlockSpec((HALO, cols), lambda i: (jnp.maximum(i * per - 1, 0), 0)),
            pl.BlockSpec((HALO, cols), lambda i: (jnp.minimum((i + 1) * per, nrow8 - 1), 0))]


def _conv_window(xm, xp, xn, first, last, cols):
    prev = jnp.where(first, 0.0, xp[:, cols])
    nxt = jnp.where(last, 0.0, xn[:, cols])
    return jnp.concatenate([prev, xm[:, cols], nxt], axis=0)


def _shift_rows(xw, off):
    n = xw.shape[0]
    sh = (-off) % n
    return xw if sh == 0 else pltpu.roll(xw, sh, 0)


def _conv_pre(xw, cw_ref, cols):
    acc = None
    for j in range(CONV_TAPS):
        term = _shift_rows(xw, j - CONV_TAPS // 2) * cw_ref[j:j + 1, cols]
        acc = term if acc is None else acc + term
    return acc


def _softplus(x):
    u = jnp.exp(-jnp.abs(x))
    w = 1.0 + u
    log1p = jnp.where(w == 1.0, u, jnp.log(w) * u / jnp.where(w == 1.0, 1.0, w - 1.0))
    return jnp.maximum(x, 0.0) + log1p


def _gdn_prep_fwd(qkva, cw, ab, gp, tr):
    t, c = qkva.shape
    nt = t // tr
    ncb = c // LANES

    def body(xm, xp, xn, cw_ref, ab_ref, gp_ref, o_ref, gb_ref):
        i = pl.program_id(0)
        first, last = i == 0, i == nt - 1
        for cb in range(ncb):
            cols = slice(cb * LANES, (cb + 1) * LANES)
            xw = _conv_window(xm, xp, xn, first, last, cols)
            pre = _conv_pre(xw, cw_ref, cols)[HALO:HALO + tr]
            y = pre * _sigmoid(pre)
            if cb < 2 * GDN_HEADS:
                y = y * lax.rsqrt(jnp.sum(y * y, axis=-1, keepdims=True) + EPS)
            if cb < GDN_HEADS:
                y = y * (GDN_DIM ** -0.5)
            o_ref[:, cols] = y
        abv = ab_ref[...]
        lane = lax.broadcasted_iota(jnp.int32, abv.shape, 1)
        g = -jnp.exp(gp_ref[0:1, :]) * _softplus(abv + gp_ref[1:2, :])
        gb_ref[...] = jnp.where(lane < 8, g, jnp.where(lane < 16, _sigmoid(abv), 0.0))

    return pl.pallas_call(
        body, name="gdn_prep_fwd", grid=(nt,),
        in_specs=_halo_row_specs(tr, c, t // HALO)
                 + [_resident(cw.shape), pl.BlockSpec((tr, LANES), lambda i: (i, 0)), _resident(gp.shape)],
        out_specs=[pl.BlockSpec((tr, c), lambda i: (i, 0)), pl.BlockSpec((tr, LANES), lambda i: (i, 0))],
        out_shape=[jax.ShapeDtypeStruct((t, c), F32), jax.ShapeDtypeStruct((t, LANES), F32)],
        compiler_params=_params(("arbitrary",), VMEM_LIMIT),
    )(qkva, qkva, qkva, cw, ab, gp)


def _gdn_prep_bwd(qkva, cw, ab, gp, dy, dgates, tr):
    t, c = qkva.shape
    nt = t // tr
    ncb = c // LANES

    def body(xm, xp, xn, fm, fp, fn, cw_ref, ab_ref, gp_ref, gf_ref, dx_ref, dab_ref, dcw_ref, dgp_ref):
        i = pl.program_id(0)
        first, last = i == 0, i == nt - 1

        @pl.when(first)
        def _():
            dcw_ref[...] = jnp.zeros_like(dcw_ref)
            dgp_ref[...] = jnp.zeros_like(dgp_ref)

        sub8 = lax.broadcasted_iota(jnp.int32, (8, LANES), 0)
        for cb in range(ncb):
            cols = slice(cb * LANES, (cb + 1) * LANES)
            xw = _conv_window(xm, xp, xn, first, last, cols)
            dyw = _conv_window(fm, fp, fn, first, last, cols)
            pre = _conv_pre(xw, cw_ref, cols)
            sg = _sigmoid(pre)
            s = pre * sg
            if cb < 2 * GDN_HEADS:
                scale = (GDN_DIM ** -0.5) if cb < GDN_HEADS else 1.0
                r = lax.rsqrt(jnp.sum(s * s, axis=-1, keepdims=True) + EPS)
                dn = dyw * scale
                ds = r * dn - s * (r * r * r) * jnp.sum(dn * s, axis=-1, keepdims=True)
            else:
                ds = dyw
            dpre = ds * (sg * (1.0 + pre * (1.0 - sg)))
            dx = None
            dcw = jnp.zeros((8, LANES), F32)
            for j in range(CONV_TAPS):
                off = j - CONV_TAPS // 2
                term = _shift_rows(dpre, -off)[HALO:HALO + tr] * cw_ref[j:j + 1, cols]
                dx = term if dx is None else dx + term
                tap = jnp.sum(dpre[HALO:HALO + tr] * _shift_rows(xw, off)[HALO:HALO + tr], axis=0, keepdims=True)
                dcw = dcw + jnp.where(sub8 == j, tap, 0.0)
            dx_ref[:, cols] = dx.astype(BF16)
            dcw_ref[:, cols] += dcw

        abv = ab_ref[...]
        dgb = gf_ref[...]
        lane = lax.broadcasted_iota(jnp.int32, abv.shape, 1)
        nea = -jnp.exp(gp_ref[0:1, :])
        xs = abv + gp_ref[1:2, :]
        g = nea * _softplus(xs)
        beta = _sigmoid(abv)
        da = dgb * nea * _sigmoid(xs)
        dab = jnp.where(lane < 8, da, jnp.where(lane < 16, dgb * beta * (1.0 - beta), 0.0))
        dab_ref[...] = dab.astype(BF16)
        keep = lane[0:1, :] < 8
        dalog = jnp.where(keep, jnp.sum(dgb * g, axis=0, keepdims=True), 0.0)
        ddtb = jnp.where(keep, jnp.sum(da, axis=0, keepdims=True), 0.0)
        dgp_ref[...] += jnp.where(sub8 == 0, dalog, 0.0) + jnp.where(sub8 == 1, ddtb, 0.0)

    lrow = pl.BlockSpec((tr, LANES), lambda i: (i, 0))
    halo = _halo_row_specs(tr, c, t // HALO)
    return pl.pallas_call(
        body, name="gdn_prep_bwd", grid=(nt,),
        in_specs=halo + halo + [_resident(cw.shape), lrow, _resident(gp.shape), lrow],
        out_specs=[pl.BlockSpec((tr, c), lambda i: (i, 0)), lrow,
                   pl.BlockSpec(cw.shape, lambda i: (0, 0)), pl.BlockSpec(gp.shape, lambda i: (0, 0))],
        out_shape=[jax.ShapeDtypeStruct((t, c), BF16), jax.ShapeDtypeStruct((t, LANES), BF16),
                   jax.ShapeDtypeStruct(cw.shape, F32), jax.ShapeDtypeStruct(gp.shape, F32)],
        compiler_params=_params(("arbitrary",), VMEM_LIMIT),
    )(qkva, qkva, qkva, dy, dy, dy, cw, ab, gp, dgates)


def _chunk_masks(lower):
    ii = lax.broadcasted_iota(jnp.int32, (CHUNK, CHUNK), 0)
    jj = lax.broadcasted_iota(jnp.int32, (CHUNK, CHUNK), 1)
    incl = (ii >= jj) if lower else (ii <= jj)
    strict = (ii > jj) if lower else (ii < jj)
    return ii, jj, incl, strict


def _dot3(a, b):
    ah = a.astype(BF16)
    al = (a - ah.astype(F32)).astype(BF16)
    bh = b.astype(BF16)
    bl = (b - bh.astype(F32)).astype(BF16)
    d = lambda u, v: jnp.dot(u, v, preferred_element_type=F32)
    return d(ah, bh) + (d(ah, bl) + d(al, bh))


def _tri_inv_many(lmats, ii, jj):
    m16 = (ii // 16) == (jj // 16)
    m32 = (ii // 32) == (jj // 32)
    eye = jnp.where(ii == jj, 1.0, 0.0)
    l16 = [jnp.where(m16, l, 0.0) for l in lmats]
    p2 = [_dot3(a, a) for a in l16]
    p4 = [_dot3(a, a) for a in p2]
    p8 = [_dot3(a, a) for a in p4]
    xs = [eye - a for a in l16]
    for ps in (p2, p4, p8):
        xs = [x + _dot3(x, p) for x, p in zip(xs, ps)]
    for off in ([jnp.where(m32 & jnp.logical_not(m16), l, 0.0) for l in lmats],
                [jnp.where(m32, 0.0, l) for l in lmats]):
        ys = [_dot3(x, c) for x, c in zip(xs, off)]
        xs = [x - _dot3(y, x) for x, y in zip(xs, ys)]
    return xs


def _col_to_row(col, ii, jj):
    return jnp.sum(jnp.where(ii == jj, col, 0.0), axis=0, keepdims=True)


def _row_to_col(row, ii, jj):
    return jnp.sum(jnp.where(ii == jj, row, 0.0), axis=1, keepdims=True)


def _chain_common(q, k, v, graw_col, graw_row, bcol, masks):
    ii, jj, incl, strict = masks
    inclt = jnp.logical_not(strict)
    gcol = jnp.sum(jnp.where(incl, graw_row, 0.0), axis=1, keepdims=True)
    grow = jnp.sum(jnp.where(inclt, graw_col, 0.0), axis=0, keepdims=True)
    glast = jnp.sum(graw_row, axis=1, keepdims=True)
    decay = jnp.where(incl, jnp.exp(jnp.where(incl, gcol - grow, 0.0)), 0.0)
    kb = k * bcol
    vb = v * bcol
    eg = jnp.exp(gcol)
    ek = jnp.exp(glast - gcol)
    kbg = kb * eg
    amat = _dot_nt(kb, k)
    qk = _dot_nt(q, k)
    return dict(gcol=gcol, glast=glast, decay=decay, kb=kb, vb=vb, eg=eg, ek=ek, kbg=kbg, amat=amat, qk=qk,
                intra=qk * decay, qg=q * eg, kdec=k * ek)


def _gdn_fwd(qkvc, gb, gbt):
    tm, u, w, qg, kd, intra, egl = _gdn_local_fwd(qkvc, gb, gbt)
    o_f, o_b, s_f, s_b, vn_f, vn_b = _gdn_scan_fwd(u, w, qg, kd, intra, egl, qkvc.shape[0])
    return o_f, o_b, dict(tm=tm, w=w, qg=qg, kd=kd, intra=intra, egl=egl, s=(s_f, s_b), vn=(vn_f, vn_b))


N_CHAINS = 2 * GDN_HEADS


def _load_chains(x_ref, g_ref, gt_ref):
    hd = GDN_HEADS * GDN_DIM
    chains = []
    for d in range(2):
        masks = _chunk_masks(d == 0)
        for h in range(GDN_HEADS):
            ch = d * GDN_HEADS + h
            q = x_ref[:, h * GDN_DIM:(h + 1) * GDN_DIM]
            k = x_ref[:, hd + h * GDN_DIM:hd + (h + 1) * GDN_DIM]
            v = x_ref[:, 2 * hd + h * GDN_DIM:2 * hd + (h + 1) * GDN_DIM]
            bcol = g_ref[:, 8 + ch:9 + ch]
            cm = _chain_common(q, k, v, g_ref[:, ch:ch + 1], gt_ref[0, ch:ch + 1, :], bcol, masks)
            chains.append(dict(cm, q=q, k=k, v=v, bcol=bcol, masks=masks, ch=ch, h=h))
    return chains


def _chain_shape(rows, cols, dtype):
    return lambda nc: jax.ShapeDtypeStruct((nc, N_CHAINS, rows, cols), dtype)


def _gdn_local_fwd(qkvc, gb, gbt):
    t = qkvc.shape[0]
    nc = t // CHUNK
    hd = GDN_HEADS * GDN_DIM

    def body(x_ref, g_ref, gt_ref, t_ref, u_ref, w_ref, qg_ref, kd_ref, in_ref, eg_ref):
        chains = _load_chains(x_ref, g_ref, gt_ref)
        ii, jj = chains[0]["masks"][0:2]
        tms = _tri_inv_many([jnp.where(c["masks"][3], c["amat"] * c["decay"], 0.0) for c in chains], ii, jj)
        us = [_dot(tm, c["vb"]) for tm, c in zip(tms, chains)]
        ws = [_dot(tm, c["kbg"]) for tm, c in zip(tms, chains)]
        for c, tm, u, w in zip(chains, tms, us, ws):
            ch = c["ch"]
            t_ref[0, ch] = tm
            u_ref[0, ch] = u
            w_ref[0, ch] = w.astype(BF16)
            qg_ref[0, ch] = c["qg"].astype(BF16)
            kd_ref[0, ch] = c["kdec"].astype(BF16)
            in_ref[0, ch] = c["intra"].astype(BF16)
            eg_ref[0, ch:ch + 1, :] = jnp.broadcast_to(jnp.exp(c["glast"]), (1, LANES))

    blk = lambda rows, cols: pl.BlockSpec((1, N_CHAINS, rows, cols), lambda n: (n, 0, 0, 0))
    shapes = [_chain_shape(CHUNK, CHUNK, F32), _chain_shape(CHUNK, GDN_DIM, F32), _chain_shape(CHUNK, GDN_DIM, BF16),
              _chain_shape(CHUNK, GDN_DIM, BF16), _chain_shape(CHUNK, GDN_DIM, BF16), _chain_shape(CHUNK, CHUNK, BF16)]
    return tuple(pl.pallas_call(
        body, name="gdn_local_fwd", grid=(nc,),
        in_specs=[pl.BlockSpec((CHUNK, 3 * hd), lambda n: (n, 0)), pl.BlockSpec((CHUNK, LANES), lambda n: (n, 0)),
                  pl.BlockSpec((1, 16, CHUNK), lambda n: (n, 0, 0))],
        out_specs=[blk(CHUNK, CHUNK), blk(CHUNK, GDN_DIM), blk(CHUNK, GDN_DIM), blk(CHUNK, GDN_DIM),
                   blk(CHUNK, GDN_DIM), blk(CHUNK, CHUNK), pl.BlockSpec((1, N_CHAINS, LANES), lambda n: (n, 0, 0))],
        out_shape=[s(nc) for s in shapes] + [jax.ShapeDtypeStruct((nc, N_CHAINS, LANES), F32)],
        compiler_params=_params(("arbitrary",), VMEM_LIMIT),
    )(qkvc, gb, gbt))


def _dir_specs(nc, rev):
    def spec(d, rows, cols, own=False):
        chunk = (lambda n: n) if (d == 0) != rev else (lambda n: nc - 1 - n)
        blk = 0 if own else d
        if rows is None:
            return pl.BlockSpec((1, GDN_HEADS if own else N_CHAINS, cols), lambda n: (chunk(n), 0, 0))
        return pl.BlockSpec((1, GDN_HEADS, rows, cols), lambda n: (chunk(n), blk, 0, 0))

    def rows_spec(d, cols):
        chunk = (lambda n: n) if (d == 0) != rev else (lambda n: nc - 1 - n)
        return pl.BlockSpec((CHUNK, cols), lambda n: (chunk(n), 0))
    return spec, rows_spec


def _gdn_scan_fwd(u, w, qg, kd, intra, egl, t):
    nc = t // CHUNK
    hd = GDN_HEADS * GDN_DIM

    def body(*refs):
        ins, outs, state = refs[:12], refs[12:18], refs[18]
        @pl.when(pl.program_id(0) == 0)
        def _():
            state[...] = jnp.zeros_like(state)

        chains = [(d, h) for d in range(2) for h in range(GDN_HEADS)]
        pick = lambda k, d, h: ins[2 * k + d][0, h]
        states = [state[ch] for ch in range(N_CHAINS)]
        sbs = [s.astype(BF16) for s in states]
        ws = [_dot(pick(1, d, h), sb) for (d, h), sb in zip(chains, sbs)]
        o1 = [_dot(pick(2, d, h), sb) for (d, h), sb in zip(chains, sbs)]
        vns = [(pick(0, d, h) - wsb).astype(BF16) for (d, h), wsb in zip(chains, ws)]
        o2 = [_dot(pick(4, d, h), vn) for (d, h), vn in zip(chains, vns)]
        kv = [_dot_tn(pick(3, d, h), vn) for (d, h), vn in zip(chains, vns)]
        for ch, (d, h) in enumerate(chains):
            outs[d][:, h * GDN_DIM:(h + 1) * GDN_DIM] = o1[ch] + o2[ch]
            outs[2 + d][0, h] = states[ch]
            outs[4 + d][0, h] = vns[ch]
            state[ch] = states[ch] * ins[10 + d][0, ch:ch + 1, :] + kv[ch]

    spec, rows_spec = _dir_specs(nc, False)
    pair = lambda rows, cols, own=False: [spec(0, rows, cols, own), spec(1, rows, cols, own)]
    s_shape = jax.ShapeDtypeStruct((nc, GDN_HEADS, GDN_DIM, GDN_DIM), F32)
    vn_shape = jax.ShapeDtypeStruct((nc, GDN_HEADS, CHUNK, GDN_DIM), BF16)
    return pl.pallas_call(
        body, name="gdn_scan_fwd", grid=(nc,),
        in_specs=(pair(CHUNK, GDN_DIM) + pair(CHUNK, GDN_DIM) + pair(CHUNK, GDN_DIM) + pair(CHUNK, GDN_DIM)
                  + pair(CHUNK, CHUNK) + pair(None, LANES)),
        out_specs=([rows_spec(0, hd), rows_spec(1, hd)] + pair(GDN_DIM, GDN_DIM, True)
                   + pair(CHUNK, GDN_DIM, True)),
        out_shape=[jax.ShapeDtypeStruct((t, hd), F32), jax.ShapeDtypeStruct((t, hd), F32),
                   s_shape, s_shape, vn_shape, vn_shape],
        scratch_shapes=[pltpu.VMEM((N_CHAINS, GDN_DIM, GDN_DIM), F32)],
        compiler_params=_params(("arbitrary",), VMEM_LIMIT),
    )(u, u, w, w, qg, qg, kd, kd, intra, intra, egl, egl)


def _gdn_bwd(qkvc, gb, gbt, do, saved, exchange=None):
    scan = _gdn_scan_bwd(do, saved, qkvc.shape[0])
    return _gdn_local_bwd(qkvc, gb, gbt, do, saved, scan, exchange)


def _gdn_scan_bwd(do, saved, t):
    nc = t // CHUNK
    hd = GDN_HEADS * GDN_DIM

    def body(*refs):
        ins, outs, dstate = refs[:16], refs[16:26], refs[26]
        @pl.when(pl.program_id(0) == 0)
        def _():
            dstate[...] = jnp.zeros_like(dstate)

        chains = [(d, h) for d in range(2) for h in range(GDN_HEADS)]
        pick = lambda k, d, h: ins[2 * k + d][0, h]
        dss = [dstate[ch] for ch in range(N_CHAINS)]
        dsbs = [ds.astype(BF16) for ds in dss]
        ss = [pick(1, d, h) for d, h in chains]
        sbs = [s.astype(BF16) for s in ss]
        dos = [ins[d][:, h * GDN_DIM:(h + 1) * GDN_DIM].astype(BF16) for d, h in chains]
        dv1 = [_dot_tn(pick(5, d, h), dov) for (d, h), dov in zip(chains, dos)]
        dv2 = [_dot(pick(4, d, h), dsb) for (d, h), dsb in zip(chains, dsbs)]
        ds1 = [_dot_tn(pick(3, d, h), dov) for (d, h), dov in zip(chains, dos)]
        dkds = [_dot_nt(pick(6, d, h), dsb) for (d, h), dsb in zip(chains, dsbs)]
        dqgs = [_dot_nt(dov, sb) for dov, sb in zip(dos, sbs)]
        dvns = [(a + b).astype(BF16) for a, b in zip(dv1, dv2)]
        ds2 = [_dot_tn(pick(2, d, h), dvn) for (d, h), dvn in zip(chains, dvns)]
        dws = [_dot_nt(dvn, sb) for dvn, sb in zip(dvns, sbs)]
        for ch, (d, h) in enumerate(chains):
            egl = ins[14 + d][0, ch:ch + 1, :]
            outs[d][0, h] = dvns[ch]
            outs[2 + d][0, h] = (-dws[ch]).astype(BF16)
            outs[4 + d][0, h] = dqgs[ch]
            outs[6 + d][0, h] = dkds[ch]
            outs[8 + d][0, h:h + 1, :] = egl * jnp.sum(jnp.sum(ss[ch] * dss[ch], axis=1, keepdims=True),
                                                       axis=0, keepdims=True)
            dstate[ch] = ds1[ch] + egl * dss[ch] - ds2[ch]

    spec, rows_spec = _dir_specs(nc, True)
    pair = lambda rows, cols, own=False: [spec(0, rows, cols, own), spec(1, rows, cols, own)]
    s_f, s_b = saved["s"]
    vn_f, vn_b = saved["vn"]
    w, qg, kd, intra, egl = saved["w"], saved["qg"], saved["kd"], saved["intra"], saved["egl"]
    own = lambda rows, cols, dtype: jax.ShapeDtypeStruct((nc, GDN_HEADS, rows, cols), dtype)
    row_shape = jax.ShapeDtypeStruct((nc, GDN_HEADS, LANES), F32)
    return pl.pallas_call(
        body, name="gdn_scan_bwd", grid=(nc,),
        in_specs=([rows_spec(0, hd), rows_spec(1, hd)] + pair(GDN_DIM, GDN_DIM, True) + pair(CHUNK, GDN_DIM)
                  + pair(CHUNK, GDN_DIM) + pair(CHUNK, GDN_DIM) + pair(CHUNK, CHUNK) + pair(CHUNK, GDN_DIM, True)
                  + pair(None, LANES)),
        out_specs=(pair(CHUNK, GDN_DIM, True) + pair(CHUNK, GDN_DIM, True) + pair(CHUNK, GDN_DIM, True)
                   + pair(CHUNK, GDN_DIM, True) + pair(None, LANES, True)),
        out_shape=[own(CHUNK, GDN_DIM, BF16)] * 4 + [own(CHUNK, GDN_DIM, F32)] * 4 + [row_shape] * 2,
        scratch_shapes=[pltpu.VMEM((N_CHAINS, GDN_DIM, GDN_DIM), F32)],
        compiler_params=_params(("arbitrary",), VMEM_LIMIT),
    )(do, do, s_f, s_b, w, w, qg, qg, kd, kd, intra, intra, vn_f, vn_b, egl, egl)


def _dot3_nt(a, b):
    ah = a.astype(BF16)
    al = (a - ah.astype(F32)).astype(BF16)
    bh = b.astype(BF16)
    bl = (b - bh.astype(F32)).astype(BF16)
    return _dot_nt(ah, bh) + (_dot_nt(ah, bl) + _dot_nt(al, bh))


def _dot3_tn(a, b):
    ah = a.astype(BF16)
    al = (a - ah.astype(F32)).astype(BF16)
    bh = b.astype(BF16)
    bl = (b - bh.astype(F32)).astype(BF16)
    return _dot_tn(ah, bh) + (_dot_tn(ah, bl) + _dot_tn(al, bh))


def _gdn_local_bwd(qkvc, gb, gbt, do, saved, scan, exchange=None):
    t = qkvc.shape[0]
    nc = t // CHUNK
    hd = GDN_HEADS * GDN_DIM

    def body(*refs):
        x_ref, g_ref, gt_ref, do_ref, t_ref = refs[:5]
        per_dir = refs[5:17]
        dx_ref, dg_ref = refs[17:]
        chains = _load_chains(x_ref, g_ref, gt_ref)
        lane = lax.broadcasted_iota(jnp.int32, (CHUNK, LANES), 1)
        dgates = jnp.zeros((CHUNK, LANES), F32)
        for c in chains:
            d = c["ch"] // GDN_HEADS
            vn_ref, dvn_ref, dw_ref, dqg_ref, dkd_ref, dgl_ref = per_dir[d::2]
            h = c["h"]
            c.update(tm=t_ref[0, c["ch"]], dov=do_ref[:, h * GDN_DIM:(h + 1) * GDN_DIM], vnew=vn_ref[0, h],
                     dvnew=dvn_ref[0, h], dw=dw_ref[0, h], dqg=dqg_ref[0, h], dkdec=dkd_ref[0, h],
                     dglast=dgl_ref[0, h:h + 1, 0:1])
        dintras = [_dot_nt(c["dov"], c["vnew"]) for c in chains]
        dts = [_dot_nt(c["dvnew"], c["vb"]) + _dot_nt(c["dw"], c["kbg"]) for c in chains]
        dvbs = [_dot_tn(c["tm"], c["dvnew"]) for c in chains]
        dkbgs = [_dot_tn(c["tm"], c["dw"]) for c in chains]
        tdts = [_dot3_nt(dt, c["tm"]) for dt, c in zip(dts, chains)]
        dls = [jnp.where(c["masks"][3], -_dot3_tn(c["tm"], tdt), 0.0) for tdt, c in zip(tdts, chains)]
        das = [dl * c["decay"] for dl, c in zip(dls, chains)]
        dqks = [jnp.where(c["masks"][2], di, 0.0) * c["decay"] for di, c in zip(dintras, chains)]
        dkb1 = [_dot(da, c["k"]) for da, c in zip(das, chains)]
        dk1 = [_dot_tn(da, c["kb"]) for da, c in zip(das, chains)]
        dk2 = [_dot_tn(dqk, c["q"]) for dqk, c in zip(dqks, chains)]
        dq1 = [_dot(dqk, c["k"]) for dqk, c in zip(dqks, chains)]
        grads = []
        for n, c in enumerate(chains):
            ch = c["ch"]
            ii, jj, incl, strict = c["masks"]
            k, v, bcol = c["k"], c["v"], c["bcol"]
            decay, eg, ek, kbg = c["decay"], c["eg"], c["ek"], c["kbg"]
            dqg, dkdec, dglast = c["dqg"], c["dkdec"], c["dglast"]
            dvb, dkbg, dl = dvbs[n], dkbgs[n], dls[n]
            dintra = jnp.where(incl, dintras[n], 0.0)
            mm = (dl * c["amat"] + dintra * c["qk"]) * decay
            dkb = dkb1[n] + dkbg * eg
            dk = dk1[n] + dk2[n] + dkdec * ek + dkb * bcol
            dq = dq1[n] + dqg * eg
            dv = dvb * bcol
            dbeta = jnp.sum(dkb * k, axis=1, keepdims=True) + jnp.sum(dvb * v, axis=1, keepdims=True)
            kd2 = jnp.sum(dkdec * c["kdec"], axis=1, keepdims=True)
            dgc = (jnp.sum(mm, axis=1, keepdims=True) - _row_to_col(jnp.sum(mm, axis=0, keepdims=True), ii, jj)
                   + jnp.sum(dqg * c["qg"], axis=1, keepdims=True) - kd2
                   + jnp.sum(dkbg * kbg, axis=1, keepdims=True))
            dgl = dglast + jnp.sum(kd2, axis=0, keepdims=True)
            draw = jnp.sum(jnp.where(jnp.logical_not(strict), _col_to_row(dgc, ii, jj), 0.0),
                           axis=1, keepdims=True) + dgl
            dgates = dgates + jnp.where(lane == ch, draw, 0.0) + jnp.where(lane == 8 + ch, dbeta, 0.0)
            grads.append((dq, dk, dv))
        for h in range(GDN_HEADS):
            for part in range(3):
                cols = slice(part * hd + h * GDN_DIM, part * hd + (h + 1) * GDN_DIM)
                dx_ref[:, cols] = grads[h][part] + grads[GDN_HEADS + h][part]
        dg_ref[...] = dgates

    all8 = lambda rows, cols: pl.BlockSpec((1, N_CHAINS, rows, cols), lambda n: (n, 0, 0, 0))
    own4 = lambda rows, cols: pl.BlockSpec((1, GDN_HEADS, rows, cols), lambda n: (n, 0, 0, 0))
    row4 = pl.BlockSpec((1, GDN_HEADS, LANES), lambda n: (n, 0, 0))
    vn_f, vn_b = saved["vn"]
    dvn_f, dvn_b, dw_f, dw_b, dqg_f, dqg_b, dkd_f, dkd_b, dgl_f, dgl_b = scan
    return _grid_call(
        body, "gdn_local_bwd", nc,
        [pl.BlockSpec((CHUNK, 3 * hd), lambda n: (n, 0)), pl.BlockSpec((CHUNK, LANES), lambda n: (n, 0)),
         pl.BlockSpec((1, 16, CHUNK), lambda n: (n, 0, 0)), pl.BlockSpec((CHUNK, hd), lambda n: (n, 0)),
         all8(CHUNK, CHUNK)] + [own4(CHUNK, GDN_DIM)] * 10 + [row4, row4],
        [pl.BlockSpec((CHUNK, 3 * hd), lambda n: (n, 0)), pl.BlockSpec((CHUNK, LANES), lambda n: (n, 0))],
        [jax.ShapeDtypeStruct((t, 3 * hd), F32), jax.ShapeDtypeStruct((t, LANES), F32)],
        (qkvc, gb, gbt, do, saved["tm"], vn_f, vn_b, dvn_f, dvn_b, dw_f, dw_b, dqg_f, dqg_b, dkd_f, dkd_b, dgl_f, dgl_b),
        exchange=exchange)


def _gdn_post_fwd(of, ob, z, gw, tm):
    t, hd = of.shape

    def body(of_ref, ob_ref, z_ref, w_ref, o_ref):
        for h in range(GDN_HEADS):
            cols = slice(h * GDN_DIM, (h + 1) * GDN_DIM)
            o = of_ref[:, cols] + ob_ref[:, cols]
            zv = z_ref[:, cols]
            o_ref[:, cols] = (o * _rstd(o) * w_ref[...] * (zv * _sigmoid(zv))).astype(BF16)

    row = pl.BlockSpec((tm, hd), lambda i: (i, 0))
    return pl.pallas_call(
        body, name="gdn_post_fwd", grid=(t // tm,),
        in_specs=[row, row, row, _resident((1, GDN_DIM))],
        out_specs=row, out_shape=jax.ShapeDtypeStruct((t, hd), BF16),
        compiler_params=_params(("arbitrary",), VMEM_LIMIT),
    )(of, ob, z, gw)


def _gdn_post_bwd(doa, of, ob, z, gw, tm):
    t, hd = of.shape

    def body(d_ref, of_ref, ob_ref, z_ref, w_ref, do_ref, dz_ref, dw_ref):
        @pl.when(pl.program_id(0) == 0)
        def _():
            dw_ref[...] = jnp.zeros_like(dw_ref)

        dw = jnp.zeros((1, GDN_DIM), F32)
        for h in range(GDN_HEADS):
            cols = slice(h * GDN_DIM, (h + 1) * GDN_DIM)
            o = of_ref[:, cols] + ob_ref[:, cols]
            zv = z_ref[:, cols]
            dv = d_ref[:, cols]
            r = _rstd(o)
            sg = _sigmoid(zv)
            on = o * r * w_ref[...]
            dz_ref[:, cols] = (dv * on * (sg * (1.0 + zv * (1.0 - sg)))).astype(BF16)
            dxr, dwh = _rms_bwd(o, r, w_ref[...], dv * (zv * sg))
            do_ref[:, cols] = dxr
            dw = dw + dwh
        dw_ref[...] += dw

    row = pl.BlockSpec((tm, hd), lambda i: (i, 0))
    return pl.pallas_call(
        body, name="gdn_post_bwd", grid=(t // tm,),
        in_specs=[row, row, row, row, _resident((1, GDN_DIM))],
        out_specs=[row, row, pl.BlockSpec((1, GDN_DIM), lambda i: (0, 0))],
        out_shape=[jax.ShapeDtypeStruct((t, hd), F32), jax.ShapeDtypeStruct((t, hd), BF16),
                   jax.ShapeDtypeStruct((1, GDN_DIM), F32)],
        compiler_params=_params(("arbitrary",), VMEM_LIMIT),
    )(doa, of, ob, z, gw)


SWA_W = SWA_HEADS * SWA_DIM
QBLK = 128
KWIN = QBLK + 2 * RADIUS
WIN_OFFSETS = (0, RADIUS, 2 * RADIUS)


def _t5_bucket(rel):
    nb = REL_BUCKETS // 2
    bucket = (rel > 0).astype(np.int32) * nb
    n = np.abs(rel)
    max_exact = nb // 2
    large = max_exact + (np.log(np.maximum(n, 1) / max_exact)
                         / math.log(REL_MAX_DISTANCE / max_exact) * (nb - max_exact)).astype(np.int32)
    large = np.minimum(large, nb - 1)
    return (bucket + np.where(n < max_exact, n, large)).astype(np.int32)


def _band_tables(dilation):
    a = np.arange(QBLK)
    b = np.arange(KWIN)
    rel = np.stack([b[None, :] - w0 - a[:, None] for w0 in WIN_OFFSETS])
    return _t5_bucket(rel * dilation), np.abs(rel) <= RADIUS


def _bias_table(rel_bias, idx, valid):
    onehot = (jnp.arange(REL_BUCKETS, dtype=jnp.int32)[:, None] == jnp.asarray(idx.reshape(1, -1))).astype(F32)
    tab = jnp.dot(rel_bias.T, onehot, precision=HIGHEST)
    tab = jnp.where(jnp.asarray(valid.reshape(1, -1)), tab, NEG_BIG)
    return tab.reshape((SWA_HEADS,) + idx.shape), onehot


def _head_mean(x2, bd_ref):
    return _dot_hi(x2, bd_ref[...])


VIEW_DILATIONS = tuple(d for _, d in PATTERNS if d > 1)


def _view_spec(tm, d):
    return pl.BlockSpec((tm // d, d * SWA_W), lambda i: (i, 0))


def _view_shape(t, d, dtype):
    return jax.ShapeDtypeStruct((t // d, d * SWA_W), dtype)


N_GROUPS = SWA_W // LANES


def _to_view(src_ref, idx, dst_ref, d, rows):
    for r in range(d):
        for g in range(N_GROUPS):
            cols = slice(r * SWA_W + g * LANES, r * SWA_W + (g + 1) * LANES)
            dst_ref[:, cols] = src_ref[idx, g, pl.ds(r, rows // d, stride=d), :].astype(dst_ref.dtype)


def _from_view(src_ref, dst_ref, idx, d, rows):
    for r in range(d):
        for g in range(N_GROUPS):
            cols = slice(r * SWA_W + g * LANES, r * SWA_W + (g + 1) * LANES)
            dst_ref[idx, g, pl.ds(r, rows // d, stride=d), :] = src_ref[:, cols]


def _swa_prep_fwd(qkvb, qw, kw, bd, tm):
    t = qkvb.shape[0]

    def body(x_ref, qw_ref, kw_ref, bd_ref, *rest):
        outs, sc = rest[:-1], rest[-1]
        for gidx in range(N_GROUPS):
            cols = slice(gidx * LANES, (gidx + 1) * LANES)
            xq = x_ref[:, cols]
            sc[0, gidx] = xq * lax.rsqrt(_head_mean(xq * xq, bd_ref) + EPS) * qw_ref[:, cols] * (SWA_DIM ** -0.5)
            xk = x_ref[:, SWA_W + gidx * LANES:SWA_W + (gidx + 1) * LANES]
            sc[1, gidx] = xk * lax.rsqrt(_head_mean(xk * xk, bd_ref) + EPS) * kw_ref[:, cols]
            sc[2, gidx] = x_ref[:, 2 * SWA_W + gidx * LANES:2 * SWA_W + (gidx + 1) * LANES]
            for i in range(3):
                outs[i][:, cols] = sc[i, gidx].astype(BF16)
        for i in range(3):
            for n, d in enumerate(VIEW_DILATIONS):
                _to_view(sc, i, outs[3 * (n + 1) + i], d, tm)

    return pl.pallas_call(
        body, name="swa_prep_fwd", grid=(t // tm,),
        in_specs=[pl.BlockSpec((tm, 3 * SWA_W), lambda i: (i, 0)), _resident((1, SWA_W)), _resident((1, SWA_W)),
                  _resident((LANES, LANES))],
        out_specs=[_view_spec(tm, d) for d in (1,) + VIEW_DILATIONS for _ in range(3)],
        out_shape=[_view_shape(t, d, BF16) for d in (1,) + VIEW_DILATIONS for _ in range(3)],
        scratch_shapes=[pltpu.VMEM((3, N_GROUPS, tm, LANES), F32)],
        compiler_params=_params(("arbitrary",), VMEM_LIMIT),
    )(qkvb, qw, kw, bd)


def _swa_prep_bwd(qkvb, qw, kw, bd, grads, tm):
    t = qkvb.shape[0]

    def body(x_ref, qw_ref, kw_ref, bd_ref, *rest):
        parts, (dx_ref, dqw_ref, dkw_ref, sc) = rest[:9], rest[9:]
        @pl.when(pl.program_id(0) == 0)
        def _():
            dqw_ref[...] = jnp.zeros_like(dqw_ref)
            dkw_ref[...] = jnp.zeros_like(dkw_ref)

        for i in range(3):
            for n, d in enumerate(VIEW_DILATIONS):
                _from_view(parts[3 * (n + 1) + i], sc, 2 * i + n, d, tm)
        for gidx in range(N_GROUPS):
            cols = slice(gidx * LANES, (gidx + 1) * LANES)
            for i, base, w_ref, dw_ref, scale in ((0, 0, qw_ref, dqw_ref, SWA_DIM ** -0.5),
                                                  (1, SWA_W, kw_ref, dkw_ref, 1.0)):
                xv = x_ref[:, base + gidx * LANES:base + (gidx + 1) * LANES]
                dy = (parts[i][:, cols] + sc[2 * i, gidx] + sc[2 * i + 1, gidx]) * scale
                r = lax.rsqrt(_head_mean(xv * xv, bd_ref) + EPS)
                xhat = xv * r
                dxh = dy * w_ref[:, cols]
                dx = r * (dxh - xhat * _head_mean(dxh * xhat, bd_ref))
                dx_ref[:, base + gidx * LANES:base + (gidx + 1) * LANES] = dx.astype(BF16)
                dw_ref[:, cols] += jnp.sum(dy * xhat, axis=0, keepdims=True)
            dx_ref[:, 2 * SWA_W + gidx * LANES:2 * SWA_W + (gidx + 1) * LANES] = (
                parts[2][:, cols] + sc[4, gidx] + sc[5, gidx]).astype(BF16)

    wrow = pl.BlockSpec((1, SWA_W), lambda i: (0, 0))
    return pl.pallas_call(
        body, name="swa_prep_bwd", grid=(t // tm,),
        in_specs=[pl.BlockSpec((tm, 3 * SWA_W), lambda i: (i, 0)), _resident((1, SWA_W)), _resident((1, SWA_W)),
                  _resident((LANES, LANES))] + [_view_spec(tm, d) for d in (1,) + VIEW_DILATIONS for _ in range(3)],
        out_specs=[pl.BlockSpec((tm, 3 * SWA_W), lambda i: (i, 0)), wrow, wrow],
        out_shape=[jax.ShapeDtypeStruct((t, 3 * SWA_W), BF16), jax.ShapeDtypeStruct((1, SWA_W), F32),
                   jax.ShapeDtypeStruct((1, SWA_W), F32)],
        scratch_shapes=[pltpu.VMEM((6, N_GROUPS, tm, LANES), F32)],
        compiler_params=_params(("arbitrary",), VMEM_LIMIT),
    )(qkvb, qw, kw, bd, *grads)


def _aligned(v, m):
    return v if isinstance(v, int) else pl.multiple_of(v, m)


BAND_GROUP = 2


def _band_loop(nsub, length, step):
    step([(0, 0)], 0)
    if nsub > 2:
        assert (nsub - 2) % BAND_GROUP == 0

        def inner(i, carry):
            s0 = 1 + i * BAND_GROUP
            step([(s0 + e, pl.multiple_of((s0 + e) * QBLK - RADIUS, RADIUS)) for e in range(BAND_GROUP)], 1)
            return carry
        lax.fori_loop(0, (nsub - 2) // BAND_GROUP, inner, 0)
    step([(nsub - 1, length - KWIN)], 2)


def _head_select(lane, a0, a1):
    return jnp.where(lane < SWA_DIM, a0, a1)


def _swa_fwd(qv, kv, vv, bias, dilation, name):
    length = qv.shape[0]
    nsub = length // QBLK
    assert nsub >= 2 and length % QBLK == 0

    def body(q_ref, k_ref, v_ref, b_ref, o_ref, l_ref):
        lane = lax.broadcasted_iota(jnp.int32, (QBLK, LANES), 1)

        def step(blocks, var):
            items = []
            for s, ws in blocks:
                rows = pl.ds(_aligned(s * QBLK, QBLK), QBLK)
                q, kk, vw = q_ref[rows, :], k_ref[pl.ds(ws, KWIN), :], v_ref[pl.ds(ws, KWIN), :]
                for hh in range(2):
                    items.append((hh, jnp.where((lane < SWA_DIM) == (hh == 0), q, jnp.zeros_like(q)), kk, vw))
            lgs = [_dot_nt(qh, kk) + b_ref[hh, var] for hh, qh, kk, _ in items]
            ms = [jnp.max(lg, axis=-1, keepdims=True) for lg in lgs]
            ps = [jnp.exp(lg - m) for lg, m in zip(lgs, ms)]
            dens = [jnp.sum(p, axis=-1, keepdims=True) for p in ps]
            pvs = [_dot(p, it[3]) for p, it in zip(ps, items)]
            for n, (s, _) in enumerate(blocks):
                rows = pl.ds(_aligned(s * QBLK, QBLK), QBLK)
                o0, o1 = (pvs[2 * n + hh] / dens[2 * n + hh] for hh in range(2))
                l0, l1 = (ms[2 * n + hh] + jnp.log(dens[2 * n + hh]) for hh in range(2))
                o_ref[rows, :] = _head_select(lane, o0, o1)
                l_ref[rows, :] = _head_select(lane, l0, l1)

        _band_loop(nsub, length, step)

    blk = pl.BlockSpec((length, LANES), lambda hp, r: (0, r * (SWA_W // LANES) + hp))
    shp = jax.ShapeDtypeStruct(qv.shape, F32)
    return pl.pallas_call(
        body, name=name, grid=(SWA_W // LANES, dilation),
        in_specs=[blk, blk, blk, pl.BlockSpec((2, 3, QBLK, KWIN), lambda hp, r: (hp, 0, 0, 0))],
        out_specs=[blk, blk], out_shape=[shp, shp],
        compiler_params=_params(("arbitrary", "arbitrary"), VMEM_LIMIT),
    )(qv, kv, vv, bias)


def _swa_combine(os_, ls_, tm):
    t = os_[0].shape[0]

    def body(o0, o1, o2, l0, l1, l2, o_ref, ob_ref, la_ref, lb_ref, lc_ref, sc):
        for n, d in enumerate(VIEW_DILATIONS):
            _from_view((o1, o2)[n], sc, n, d, tm)
            _from_view((l1, l2)[n], sc, 2 + n, d, tm)
        for g in range(N_GROUPS):
            cols = slice(g * LANES, (g + 1) * LANES)
            la, lb, lc = l0[:, cols], sc[2, g], sc[3, g]
            m = jnp.maximum(jnp.maximum(la, lb), lc)
            tot = m + jnp.log(jnp.exp(la - m) + jnp.exp(lb - m) + jnp.exp(lc - m))
            o = jnp.exp(la - tot) * o0[:, cols] + jnp.exp(lb - tot) * sc[0, g] + jnp.exp(lc - tot) * sc[1, g]
            o_ref[:, cols] = o
            ob_ref[:, cols] = o.astype(BF16)
            la_ref[:, cols] = tot
            sc[4, g] = tot
        for n, d in enumerate(VIEW_DILATIONS):
            _to_view(sc, 4, (lb_ref, lc_ref)[n], d, tm)

    specs = [_view_spec(tm, d) for d in (1,) + VIEW_DILATIONS]
    return pl.pallas_call(
        body, name="swa_combine", grid=(t // tm,), in_specs=specs + specs, out_specs=[specs[0], specs[0]] + specs,
        out_shape=[jax.ShapeDtypeStruct((t, SWA_W), F32), jax.ShapeDtypeStruct((t, SWA_W), BF16)]
                  + [_view_shape(t, d, F32) for d in (1,) + VIEW_DILATIONS],
        scratch_shapes=[pltpu.VMEM((5, N_GROUPS, tm, LANES), F32)],
        compiler_params=_params(("arbitrary",), VMEM_LIMIT),
    )(*os_, *ls_)


def _swa_bwd_prep(do, o, bd, tm):
    t = do.shape[0]

    def body(d_ref, o_ref, bd_ref, dd1, dd4, dd16, db1, db4, db16, sc):
        for gidx in range(N_GROUPS):
            cols = slice(gidx * LANES, (gidx + 1) * LANES)
            dv = d_ref[:, cols]
            dd = _head_mean(dv * o_ref[:, cols], bd_ref) * float(SWA_DIM)
            sc[0, gidx] = dd
            sc[1, gidx] = dv
            dd1[:, cols] = dd
            db1[:, cols] = dv.astype(BF16)
        for n, d in enumerate(VIEW_DILATIONS):
            _to_view(sc, 0, (dd4, dd16)[n], d, tm)
            _to_view(sc, 1, (db4, db16)[n], d, tm)

    specs = [_view_spec(tm, d) for d in (1,) + VIEW_DILATIONS]
    return pl.pallas_call(
        body, name="swa_bwd_prep", grid=(t // tm,), in_specs=[specs[0], specs[0], _resident((LANES, LANES))],
        out_specs=specs + specs,
        out_shape=[_view_shape(t, d, F32) for d in (1,) + VIEW_DILATIONS]
                  + [_view_shape(t, d, BF16) for d in (1,) + VIEW_DILATIONS],
        scratch_shapes=[pltpu.VMEM((2, N_GROUPS, tm, LANES), F32)],
        compiler_params=_params(("arbitrary",), VMEM_LIMIT),
    )(do, o, bd)


def _swa_bwd(qv, kv, vv, dov, lv, ddv, bias_a, dilation, name):
    length = qv.shape[0]
    nsub = length // QBLK
    single = pl.Buffered(1) if dilation == 1 else None

    def body(q_ref, k_ref, v_ref, do_ref, l_ref, dd_ref, ba_ref, dq_ref, dk_ref, dv_ref, db_ref):
        @pl.when(pl.program_id(1) == 0)
        def _():
            db_ref[...] = jnp.zeros_like(db_ref)

        lane = lax.broadcasted_iota(jnp.int32, (QBLK, LANES), 1)
        lanew = lax.broadcasted_iota(jnp.int32, (KWIN, LANES), 1)

        def step(blocks, var):
            items = []
            for s, ws in blocks:
                rows = pl.ds(_aligned(s * QBLK, QBLK), QBLK)
                win = pl.ds(ws, KWIN)
                q, dov_ = q_ref[rows, :], do_ref[rows, :]
                kk, vw = k_ref[win, :], v_ref[win, :]
                lse, dd = l_ref[rows, :], dd_ref[rows, :]
                for hh in range(2):
                    mine = (lane < SWA_DIM) == (hh == 0)
                    col = slice(hh * SWA_DIM, hh * SWA_DIM + 1)
                    items.append((hh, jnp.where(mine, q, jnp.zeros_like(q)), jnp.where(mine, dov_, jnp.zeros_like(dov_)),
                                  kk, vw, lse[:, col], dd[:, col], q, dov_))
            lgs = [_dot_nt(it[1], it[3]) + ba_ref[it[0], var] for it in items]
            dps = [_dot_nt(it[2], it[4]) for it in items]
            ps = [jnp.exp(lg - it[5]) for lg, it in zip(lgs, items)]
            dss = [p * (dp - it[6]) for p, dp, it in zip(ps, dps, items)]
            dqs = [_dot(ds, it[3]) for ds, it in zip(dss, items)]
            dks = [_dot_tn(ds, it[7]) for ds, it in zip(dss, items)]
            dvs = [_dot_tn(p, it[8]) for p, it in zip(ps, items)]
            for n, (s, ws) in enumerate(blocks):
                rows = pl.ds(_aligned(s * QBLK, QBLK), QBLK)
                win = pl.ds(ws, KWIN)
                dq_ref[rows, :] = _head_select(lane, dqs[2 * n], dqs[2 * n + 1])
                dk_ref[win, :] += _head_select(lanew, dks[2 * n], dks[2 * n + 1])
                dv_ref[win, :] += _head_select(lanew, dvs[2 * n], dvs[2 * n + 1])
            for hh in range(2):
                tot = dss[hh]
                for n in range(1, len(blocks)):
                    tot = tot + dss[2 * n + hh]
                db_ref[hh, var] += tot

        dk_ref[...] = jnp.zeros_like(dk_ref)
        dv_ref[...] = jnp.zeros_like(dv_ref)
        _band_loop(nsub, length, step)

    imap = lambda hp, r: (0, r * (SWA_W // LANES) + hp)
    blk_in = pl.BlockSpec((length, LANES), imap, pipeline_mode=single)
    blk_out = pl.BlockSpec((length, LANES), imap)
    shp = jax.ShapeDtypeStruct(qv.shape, F32)
    return pl.pallas_call(
        body, name=name, grid=(SWA_W // LANES, dilation),
        in_specs=[blk_in] * 6 + [pl.BlockSpec((2, 3, QBLK, KWIN), lambda hp, r: (hp, 0, 0, 0))],
        out_specs=[blk_out, blk_out, blk_out, pl.BlockSpec((2, 3, QBLK, KWIN), lambda hp, r: (hp, 0, 0, 0))],
        out_shape=[shp, shp, shp, jax.ShapeDtypeStruct((SWA_HEADS, 3, QBLK, KWIN), F32)],
        compiler_params=_params(("arbitrary", "arbitrary"), VMEM_LIMIT),
    )(qv, kv, vv, dov, lv, ddv, bias_a)


def _bias_grad(ds2, onehot, tk):
    n = ds2.shape[1]
    nk = n // tk

    def body(a_ref, b_ref, o_ref):
        @pl.when(pl.program_id(0) == 0)
        def _():
            o_ref[...] = jnp.zeros_like(o_ref)

        o_ref[...] += lax.dot_general(a_ref[...], b_ref[...], (((1,), (1,)), ((), ())), precision=HIGHEST,
                                      preferred_element_type=F32)

    return pl.pallas_call(
        body, name="bias_grad", grid=(nk,),
        in_specs=[pl.BlockSpec((SWA_HEADS, tk), lambda k: (0, k)), pl.BlockSpec((REL_BUCKETS, tk), lambda k: (0, k))],
        out_specs=pl.BlockSpec((SWA_HEADS, REL_BUCKETS), lambda k: (0, 0)),
        out_shape=jax.ShapeDtypeStruct((SWA_HEADS, REL_BUCKETS), F32),
        compiler_params=_params(("arbitrary",), VMEM_LIMIT),
    )(ds2, onehot)


def _swa_branch_fwd(qkvb, qw_t, kw_t, rel_bias, bd, tm):
    qkv = _swa_prep_fwd(qkvb, qw_t, kw_t, bd, tm)
    os_, ls_, tabs = [], [], []
    for n, (_, d) in enumerate(PATTERNS):
        bias, onehot = _bias_table(rel_bias, *_band_tables(d))
        o_p, l_p = _swa_fwd(*qkv[3 * n:3 * n + 3], bias, d, f"swa_fwd_d{d}")
        os_.append(o_p)
        ls_.append(l_p)
        tabs.append((bias, onehot))
    o, o16, *lses = _swa_combine(os_, ls_, tm)
    return o, o16, (qkv, lses, tabs)


def _swa_branch_bwd(do, o, saved, qkvb, qw_t, kw_t, bd, tm):
    qkv, lses, tabs = saved
    prep = _swa_bwd_prep(do, o, bd, tm)
    grads, dss, ohs = [], [], []
    for n, ((_, d), (bias, onehot)) in enumerate(zip(PATTERNS, tabs)):
        dq, dk, dv, ds = _swa_bwd(*qkv[3 * n:3 * n + 3], prep[3 + n], lses[n], prep[n], bias, d, f"swa_bwd_d{d}")
        grads += [dq, dk, dv]
        dss.append(ds.reshape(SWA_HEADS, -1))
        ohs.append(onehot)
    dqkvb, dqw, dkw = _swa_prep_bwd(qkvb, qw_t, kw_t, bd, grads, tm)
    dbias = _bias_grad(jnp.concatenate(dss, axis=1), jnp.concatenate(ohs, axis=1), 8192)
    fold = lambda w: jnp.sum(w.reshape(SWA_HEADS, SWA_DIM), axis=0)
    return dqkvb, fold(dqw), fold(dkw), dbias.T


def _mesh_pos():
    return lax.axis_index("x"), lax.axis_index("y"), lax.axis_index("c")


def _other_chips(x, y):
    return [(1 - x, y), (x, 1 - y), (1 - x, 1 - y)]


def _remote(src, dst, send_sem, recv_sem, device):
    return pltpu.make_async_remote_copy(src_ref=src, dst_ref=dst, send_sem=send_sem, recv_sem=recv_sem,
                                        device_id=device, device_id_type=MESH)


def _all_gather(xs):
    n = len(xs)

    def body(*refs):
        ins, outs = refs[:n], refs[n:2 * n]
        send_sems, recv_sems = refs[2 * n:]
        x, y, c = _mesh_pos()
        me = 2 * x + y
        chips = _other_chips(x, y)
        halves = []
        sends = []
        for a in range(n):
            h = ins[a].shape[0] // 2
            mine, other = pl.ds(c * h, h), pl.ds((1 - c) * h, h)
            halves.append((mine, other))
            for j, chip in enumerate(chips):
                cp = _remote(ins[a].at[mine], outs[a].at[me, mine], send_sems.at[a, j], recv_sems.at[a, j], (*chip, c))
                cp.start()
                sends.append(cp)
        for a in range(n):
            mine, _ = halves[a]
            for j, chip in enumerate(chips):
                src = 2 * chip[0] + chip[1]
                landed = outs[a].at[src, mine]
                _remote(landed, landed, send_sems.at[a, j], recv_sems.at[a, j], (x, y, c)).wait_recv()
                fwd = _remote(landed, landed, send_sems.at[a, 3 + j], recv_sems.at[a, 3 + j], (x, y, 1 - c))
                fwd.start()
                sends.append(fwd)
        for a in range(n):
            _, other = halves[a]
            for j, chip in enumerate(chips):
                src = 2 * chip[0] + chip[1]
                landed = outs[a].at[src, other]
                _remote(landed, landed, send_sems.at[a, 3 + j], recv_sems.at[a, 3 + j], (x, y, c)).wait_recv()
        for cp in sends:
            cp.wait_send()

    outs = pl.pallas_call(
        body, name="all_gather_weights",
        in_specs=[ANY] * n, out_specs=[ANY] * n,
        out_shape=[jax.ShapeDtypeStruct((N_SHARDS,) + a.shape, a.dtype) for a in xs],
        scratch_shapes=[pltpu.SemaphoreType.DMA((n, 6)), pltpu.SemaphoreType.DMA((n, 6))],
    )(*xs)
    me = 2 * lax.axis_index("x") + lax.axis_index("y")
    return [lax.dynamic_update_slice_in_dim(o, a[None], me, 0) for o, a in zip(outs, xs)]


def _rs_pair(gs):
    n = len(gs)

    def body(*refs):
        ins, lands = refs[:n], refs[n:2 * n]
        send_sems, recv_sems = refs[2 * n:]
        x, y, c = _mesh_pos()
        cps = []
        for a in range(n):
            h = ins[a].shape[1] // 2
            cp = _remote(ins[a].at[:, pl.ds((1 - c) * h, h), :], lands[a], send_sems.at[a], recv_sems.at[a],
                         (x, y, 1 - c))
            cp.start()
            cps.append(cp)
        for cp in cps:
            cp.wait()

    half = [jax.ShapeDtypeStruct((N_SHARDS, g.shape[1] // 2, g.shape[2]), g.dtype) for g in gs]
    lands = pl.pallas_call(
        body, name="rs_pair", in_specs=[ANY] * n, out_specs=[ANY] * n, out_shape=half,
        scratch_shapes=[pltpu.SemaphoreType.DMA((n,)), pltpu.SemaphoreType.DMA((n,))],
    )(*gs)
    c = lax.axis_index("c")
    owns = [lax.dynamic_slice_in_dim(g, c * (g.shape[1] // 2), g.shape[1] // 2, 1) for g in gs]
    return owns + list(lands)


def _rs_chips(ss):
    n = len(ss)

    def body(*refs):
        ins, outs = refs[:n], refs[n:2 * n]
        send_sems, recv_sems = refs[2 * n:]
        x, y, c = _mesh_pos()
        me = 2 * x + y
        chips = _other_chips(x, y)
        cps = []
        for a in range(n):
            for j, chip in enumerate(chips):
                dst_chip = 2 * chip[0] + chip[1]
                cp = _remote(ins[a].at[dst_chip], outs[a].at[me], send_sems.at[a, j], recv_sems.at[a, j], (*chip, c))
                cp.start()
                cps.append(cp)
        for a in range(n):
            for j, chip in enumerate(chips):
                src = 2 * chip[0] + chip[1]
                _remote(outs[a].at[src], outs[a].at[src], send_sems.at[a, j], recv_sems.at[a, j], (x, y, c)).wait_recv()
        for cp in cps:
            cp.wait_send()

    outs = pl.pallas_call(
        body, name="rs_chips", in_specs=[ANY] * n, out_specs=[ANY] * n,
        out_shape=[jax.ShapeDtypeStruct(s.shape, s.dtype) for s in ss],
        scratch_shapes=[pltpu.SemaphoreType.DMA((n, 3)), pltpu.SemaphoreType.DMA((n, 3))],
    )(*ss)
    me = 2 * lax.axis_index("x") + lax.axis_index("y")
    return [lax.dynamic_update_slice_in_dim(o, lax.dynamic_slice_in_dim(s, me, 1, 0), me, 0) for o, s in zip(outs, ss)]


def _rs_join(fs):
    n = len(fs)

    def body(*refs):
        ins, outs = refs[:n], refs[n:2 * n]
        send_sems, recv_sems = refs[2 * n:]
        x, y, c = _mesh_pos()
        cps = []
        for a in range(n):
            h = ins[a].shape[0]
            cp = _remote(ins[a], outs[a].at[pl.ds(c * h, h)], send_sems.at[a], recv_sems.at[a], (x, y, 1 - c))
            cp.start()
            cps.append(cp)
        for cp in cps:
            cp.wait()

    outs = pl.pallas_call(
        body, name="rs_join", in_specs=[ANY] * n, out_specs=[ANY] * n,
        out_shape=[jax.ShapeDtypeStruct((2 * f.shape[0], f.shape[1]), f.dtype) for f in fs],
        scratch_shapes=[pltpu.SemaphoreType.DMA((n,)), pltpu.SemaphoreType.DMA((n,))],
    )(*fs)
    c = lax.axis_index("c")
    return [lax.dynamic_update_slice_in_dim(o, f, c * f.shape[0], 0) for o, f in zip(outs, fs)]


def _gather_exchange(xs):
    def start(cin, cout, send_sems, recv_sems):
        x, y, c = _mesh_pos()
        me = 2 * x + y
        for a, (src, dst) in enumerate(zip(cin, cout)):
            h = src.shape[0] // 2
            mine = pl.ds(c * h, h)
            for j, chip in enumerate(_other_chips(x, y)):
                _remote(src.at[mine], dst.at[me, mine], send_sems.at[a, j], recv_sems.at[a, j], (*chip, c)).start()

    def finish(cin, cout, send_sems, recv_sems):
        x, y, c = _mesh_pos()
        for a, dst in enumerate(cout):
            h = dst.shape[1] // 2
            for j, chip in enumerate(_other_chips(x, y)):
                landed = dst.at[2 * chip[0] + chip[1], pl.ds(c * h, h)]
                _remote(landed, landed, send_sems.at[a, j], recv_sems.at[a, j], (x, y, c)).wait()

    return _Exchange(tuple(xs), tuple(jax.ShapeDtypeStruct((N_SHARDS,) + a.shape, a.dtype) for a in xs), start, finish)


def _gather_forward(gs, xs):
    n = len(gs)

    def body(*refs):
        outs = refs[n:2 * n]
        send_sems, recv_sems = refs[2 * n:]
        x, y, c = _mesh_pos()
        chips = _other_chips(x, y)
        cps = []
        for a in range(n):
            h = outs[a].shape[1] // 2
            for j, chip in enumerate(chips):
                landed = outs[a].at[2 * chip[0] + chip[1], pl.ds(c * h, h)]
                cp = _remote(landed, landed, send_sems.at[a, j], recv_sems.at[a, j], (x, y, 1 - c))
                cp.start()
                cps.append(cp)
        for a in range(n):
            h = outs[a].shape[1] // 2
            for j, chip in enumerate(chips):
                other = outs[a].at[2 * chip[0] + chip[1], pl.ds((1 - c) * h, h)]
                _remote(other, other, send_sems.at[a, j], recv_sems.at[a, j], (x, y, c)).wait_recv()
        for cp in cps:
            cp.wait_send()

    outs = pl.pallas_call(
        body, name="gather_forward", in_specs=[ANY] * n, out_specs=[ANY] * n,
        out_shape=[jax.ShapeDtypeStruct(g.shape, g.dtype) for g in gs],
        input_output_aliases={i: i for i in range(n)},
        scratch_shapes=[pltpu.SemaphoreType.DMA((n, 3)), pltpu.SemaphoreType.DMA((n, 3))],
    )(*gs)
    me = 2 * lax.axis_index("x") + lax.axis_index("y")
    return [lax.dynamic_update_slice_in_dim(o, a[None], me, 0) for o, a in zip(outs, xs)]


def _scatter_exchange(ss):
    def start(cin, cout, send_sems, recv_sems):
        x, y, c = _mesh_pos()
        me = 2 * x + y
        for a, (src, dst) in enumerate(zip(cin, cout)):
            for j, chip in enumerate(_other_chips(x, y)):
                _remote(src.at[2 * chip[0] + chip[1]], dst.at[me], send_sems.at[a, j], recv_sems.at[a, j],
                        (*chip, c)).start()

    def finish(cin, cout, send_sems, recv_sems):
        x, y, c = _mesh_pos()
        for a, dst in enumerate(cout):
            for j, chip in enumerate(_other_chips(x, y)):
                slot = dst.at[2 * chip[0] + chip[1]]
                _remote(slot, slot, send_sems.at[a, j], recv_sems.at[a, j], (x, y, c)).wait()

    return _Exchange(tuple(ss), tuple(jax.ShapeDtypeStruct(s.shape, s.dtype) for s in ss), start, finish)


def _own_slots(slots, ss):
    me = 2 * lax.axis_index("x") + lax.axis_index("y")
    return [lax.dynamic_update_slice_in_dim(o, lax.dynamic_slice_in_dim(s, me, 1, 0), me, 0) for o, s in zip(slots, ss)]


def _add_pair(a, b, name):
    nj, h, c = a.shape

    def body(a_ref, b_ref, o_ref):
        o_ref[...] = (a_ref[...].astype(F32) + b_ref[...].astype(F32)).astype(BF16)

    blk = pl.BlockSpec((1, h, c), lambda j: (j, 0, 0))
    return pl.pallas_call(body, name=name, grid=(nj,), in_specs=[blk, blk], out_specs=blk,
                          out_shape=jax.ShapeDtypeStruct(a.shape, BF16),
                          compiler_params=_params(("arbitrary",), VMEM_LIMIT))(a, b)


def _sum_slots(l2, name):
    nj, h, c = l2.shape
    th = h // 2 if h % 32 == 0 else h

    def body(i_ref, o_ref):
        acc = i_ref[0].astype(F32)
        for s in range(1, nj):
            acc = acc + i_ref[s].astype(F32)
        o_ref[...] = acc

    return pl.pallas_call(body, name=name, grid=(h // th,),
                          in_specs=[pl.BlockSpec((nj, th, c), lambda i: (0, i, 0))],
                          out_specs=pl.BlockSpec((th, c), lambda i: (i, 0)),
                          out_shape=jax.ShapeDtypeStruct((h, c), F32),
                          compiler_params=_params(("arbitrary",), VMEM_LIMIT))(l2)


def _all_reduce_small(p):
    r = p.shape[0]

    def body(p_ref, o_ref, buf, send_sems, recv_sems):
        x, y, c = _mesh_pos()
        me = 4 * x + 2 * y + c
        buf[me] = p_ref[...]
        cps = []
        k = 0
        for fx in range(2):
            for fy in range(2):
                for fc in range(2):
                    if fx + fy + fc == 0:
                        continue
                    peer = (1 - x if fx else x, 1 - y if fy else y, 1 - c if fc else c)
                    peer_id = 4 * peer[0] + 2 * peer[1] + peer[2]
                    cp = _remote(p_ref, buf.at[me], send_sems.at[k], recv_sems.at[k], peer)
                    cp.start()
                    cps.append((cp, peer_id, k))
                    k += 1
        for cp, peer_id, k in cps:
            _remote(p_ref, buf.at[peer_id], send_sems.at[k], recv_sems.at[k], (x, y, c)).wait_recv()
        for cp, _, _ in cps:
            cp.wait_send()
        acc = buf[0]
        for s in range(1, 8):
            acc = acc + buf[s]
        o_ref[...] = acc

    vm = pl.BlockSpec(memory_space=pltpu.VMEM)
    return pl.pallas_call(
        body, name="all_reduce_small", in_specs=[vm], out_specs=vm,
        out_shape=jax.ShapeDtypeStruct(p.shape, F32),
        scratch_shapes=[pltpu.VMEM((8, r, LANES), F32), pltpu.SemaphoreType.DMA((7,)), pltpu.SemaphoreType.DMA((7,))],
    )(p)


def _adamw(w, g, m, v, name):
    r, c = w.shape
    row_tiles = [d for d in range(8, min(r, 256) + 1, 8) if r % d == 0]
    tr, tc = (max(row_tiles), c) if row_tiles else (r, 256 if c % 256 == 0 else c)
    c1 = 1.0 / (1.0 - ADAM_B1 ** ADAM_STEP)
    c2 = 1.0 / (1.0 - ADAM_B2 ** ADAM_STEP)

    def body(w_ref, g_ref, m_ref, v_ref, d_ref, nm_ref, nv_ref):
        gv = g_ref[...]
        nm = ADAM_B1 * m_ref[...] + (1.0 - ADAM_B1) * gv
        nv = ADAM_B2 * v_ref[...] + (1.0 - ADAM_B2) * (gv * gv)
        d_ref[...] = -ADAM_LR * ((nm * c1) / (jnp.sqrt(nv * c2) + ADAM_EPS) + ADAM_WD * w_ref[...])
        nm_ref[...] = nm
        nv_ref[...] = nv

    blk = pl.BlockSpec((tr, tc), lambda i, j: (i, j))
    shp = jax.ShapeDtypeStruct((r, c), F32)
    return pl.pallas_call(body, name=name, grid=(r // tr, c // tc), in_specs=[blk] * 4, out_specs=[blk] * 3,
                          out_shape=[shp, shp, shp],
                          compiler_params=_params(("arbitrary", "arbitrary"), VMEM_LIMIT))(w, g, m, v)


PACK_UNIT = 8 * LANES


def _pack(arrs):
    parts = []
    for a in arrs:
        f = a.reshape(-1).astype(F32)
        parts.append(jnp.pad(f, (0, (-f.shape[0]) % PACK_UNIT)).reshape(-1, LANES))
    return jnp.concatenate(parts, axis=0)


def _unpack(m, shapes):
    outs, row = [], 0
    for s in shapes:
        n = int(np.prod(s))
        rows = -(-n // PACK_UNIT) * 8
        outs.append(m[row:row + rows].reshape(-1)[:n].reshape(s))
        row += rows
    return outs


WEIGHTS = ["ffn1_norm", "ffn1_w_gate", "ffn1_w_up", "ffn1_w_down", "mix_norm", "w_in", "conv_w", "a_log", "dt_bias",
           "gdn_norm_w", "q_norm_w", "k_norm_w", "rel_bias", "w_out", "ffn2_norm", "ffn2_w_gate", "ffn2_w_up",
           "ffn2_w_down", "final_norm"]
BIG = ["ffn1_w_gate", "ffn1_w_up", "ffn1_w_down", "w_in", "w_out", "ffn2_w_gate", "ffn2_w_up", "ffn2_w_down"]
SMALL = [n for n in WEIGHTS if n not in BIG]
COL_SHARDED = ["ffn1_w_gate", "ffn1_w_up", "ffn2_w_gate", "ffn2_w_up"]
N_IN_COLS = 3600
TM = 256
TE = 512
TK = 2048


def kernel(x, ffn1_norm, ffn1_w_gate, ffn1_w_up, ffn1_w_down, mix_norm, w_in, conv_w, a_log, dt_bias, gdn_norm_w, q_norm_w, k_norm_w, rel_bias, w_out, ffn2_norm, ffn2_w_gate, ffn2_w_up, ffn2_w_down, final_norm, loss_target, m_ffn1_norm, m_ffn1_w_gate, m_ffn1_w_up, m_ffn1_w_down, m_mix_norm, m_w_in, m_conv_w, m_a_log, m_dt_bias, m_gdn_norm_w, m_q_norm_w, m_k_norm_w, m_rel_bias, m_w_out, m_ffn2_norm, m_ffn2_w_gate, m_ffn2_w_up, m_ffn2_w_down, m_final_norm, v_ffn1_norm, v_ffn1_w_gate, v_ffn1_w_up, v_ffn1_w_down, v_mix_norm, v_w_in, v_conv_w, v_a_log, v_dt_bias, v_gdn_norm_w, v_q_norm_w, v_k_norm_w, v_rel_bias, v_w_out, v_ffn2_norm, v_ffn2_w_gate, v_ffn2_w_up, v_ffn2_w_down, v_final_norm):
    p = dict(locals())
    xs, target = x[0], loss_target[0]
    t, d = xs.shape
    nc = t // CHUNK
    tk = min(TK, t)
    me = 2 * lax.axis_index("x") + lax.axis_index("y")

    first = ["ffn1_w_gate", "ffn1_w_up", "ffn1_w_down"]
    later = [n for n in BIG if n not in first] + ["conv_w"]
    local = lambda n, a: a[0].T if n in COL_SHARDED else a[0]
    shards = {n: local(n, p[n]).astype(BF16) for n in BIG}
    shards["conv_w"] = conv_w[0]
    gw = dict(zip(first, _all_gather([shards[n] for n in first])))
    f1 = (gw["ffn1_w_gate"], gw["ffn1_w_up"], gw["ffn1_w_down"])
    (x1, xn1, g1, u1), landed = _ffn_fwd(xs, ffn1_norm, *f1, TM, "ffn1_fwd",
                                         exchange=_gather_exchange([shards[n] for n in later]))
    gw.update(zip(later, _gather_forward(landed, [shards[n] for n in later])))
    w_in_t = jnp.transpose(gw["w_in"], (0, 2, 1)).reshape(N_IN_COLS, d)
    wp = jnp.concatenate([w_in_t[:2048], jnp.pad(w_in_t[2048:2064], ((0, LANES - 16), (0, 0))), w_in_t[2064:]], axis=0)
    w_out_full = gw["w_out"].reshape(d, d)
    conv_rows = conv_w.shape[1]
    cw = jnp.pad(gw["conv_w"].reshape(N_SHARDS * conv_rows, CONV_TAPS).T, ((0, 8 - CONV_TAPS), (0, 0)))
    gp = jnp.pad(jnp.stack([a_log.reshape(8), dt_bias.reshape(8)]), ((0, 6), (0, LANES - 8)))
    gdn_w = gdn_norm_w.reshape(1, GDN_DIM)
    qw_t = jnp.tile(q_norm_w.reshape(1, SWA_DIM), (1, SWA_HEADS))
    kw_t = jnp.tile(k_norm_w.reshape(1, SWA_DIM), (1, SWA_HEADS))
    bd = jnp.asarray(np.kron(np.eye(2), np.full((SWA_DIM, SWA_DIM), 1.0 / SWA_DIM)), F32)
    f2 = (gw["ffn2_w_gate"], gw["ffn2_w_up"], gw["ffn2_w_down"])

    hn, qkva, z, ab, qkvb = _mix_in_fwd(x1, mix_norm, wp, TM)
    qkvc, gb = _gdn_prep_fwd(qkva, cw, ab, gp, TM)
    gbt = jnp.transpose(gb[:, :16].reshape(nc, CHUNK, 16), (0, 2, 1))
    o_f, o_b, gdn_saved = _gdn_fwd(qkvc, gb, gbt)
    oa = _gdn_post_fwd(o_f, o_b, z, gdn_w, TE)
    o_swa, o_swa16, swa_saved = _swa_branch_fwd(qkvb, qw_t, kw_t, rel_bias, bd, TE)
    x2 = _mix_out_fwd(x1, oa, o_swa, w_out_full, TM)
    (x3, xn2, g2, u2), _ = _ffn_fwd(x2, ffn2_norm, *f2, TM, "ffn2_fwd")
    dx3, loss_part, d_final = _final_loss(x3, final_norm, target, TE)

    def pair_sums(partials, tag):
        pair = _rs_pair(partials)
        k = len(partials)
        return [_add_pair(pair[i], pair[k + i], f"rs_add_{tag}{i}") for i in range(k)]

    (dx2, dyh2, dg2, du2, h2, d_nw2), _ = _ffn_bwd_dx(dx3, x2, ffn2_norm, g2, u2, *f2, TM, "ffn2_bwd_dx")
    dwg2 = _matmul_tn(dg2, xn2, tk, "ffn2_dwg")
    dwu2 = _matmul_tn(du2, xn2, tk, "ffn2_dwu")
    dwd2 = _matmul_tn(h2, dyh2, tk, "ffn2_dwd")
    sums_f2 = pair_sums([dwg2, dwu2, dwd2], "a")
    doa, dob, dx2b = _mix_out_bwd(dx2, w_out_full, TM)
    dwo = jnp.concatenate([_matmul_tn(oa, dx2b, tk, "w_out_dw_a")[0], _matmul_tn(o_swa16, dx2b, tk, "w_out_dw_b")[0]],
                          axis=0).reshape(N_SHARDS, d // N_SHARDS, d)
    do_g, dz, d_gdnw = _gdn_post_bwd(doa, o_f, o_b, z, gdn_w, TE)
    (dqkvc, dgates), slots_f2 = _gdn_bwd(qkvc, gb, gbt, do_g, gdn_saved, exchange=_scatter_exchange(sums_f2))
    dqkva, dab, dcw, dgp = _gdn_prep_bwd(qkva, cw, ab, gp, dqkvc, dgates, TM)
    dqkvb, d_qw, d_kw, d_rel = _swa_branch_bwd(dob, o_swa, swa_saved, qkvb, qw_t, kw_t, bd, TE)
    dpieces = (dqkva, dz, dab, dqkvb)
    dx1, d_mixnw = _mix_in_bwd_dx(dx2, x1, mix_norm, dpieces, wp, TM)
    dwp = [_matmul_tn(dp, hn, tk, f"w_in_dw_{i}")[0] for i, dp in enumerate(dpieces)]
    dw_in = jnp.concatenate([dwp[0], dwp[1], dwp[2][:16], dwp[3]], axis=0).reshape(N_SHARDS, N_IN_COLS // N_SHARDS, d)
    dw_in = jnp.transpose(dw_in, (0, 2, 1))
    sums_mix = pair_sums([dw_in, dwo], "b")
    (gx, dyh1, dg1, du1, h1, d_nw1), slots_mix = _ffn_bwd_dx(dx1, xs, ffn1_norm, g1, u1, *f1, TM, "ffn1_bwd_dx",
                                                              exchange=_scatter_exchange(sums_mix))
    dwg1 = _matmul_tn(dg1, xn1, tk, "ffn1_dwg")
    dwu1 = _matmul_tn(du1, xn1, tk, "ffn1_dwu")
    dwd1 = _matmul_tn(h1, dyh1, tk, "ffn1_dwd")
    slots_f1 = _rs_chips(pair_sums([dwg1, dwu1, dwd1], "c"))
    slots = slots_f1 + _own_slots(slots_mix, sums_mix) + _own_slots(slots_f2, sums_f2)
    halves = [_sum_slots(s, f"rs_sum_{i}") for i, s in enumerate(slots)]
    g_big = dict(zip(BIG, _rs_join(halves)))

    small_partial = {"ffn1_norm": d_nw1, "mix_norm": d_mixnw, "a_log": dgp[0, 0:8], "dt_bias": dgp[1, 0:8],
                     "gdn_norm_w": d_gdnw, "q_norm_w": d_qw, "k_norm_w": d_kw, "rel_bias": d_rel,
                     "ffn2_norm": d_nw2, "final_norm": d_final, "conv_w": dcw[0:CONV_TAPS].T}
    red = _all_reduce_small(_pack([small_partial[n] for n in SMALL] + [loss_part[0, 0:1]]))
    full_shapes = [p[n].shape if n != "conv_w" else (N_SHARDS * conv_rows, CONV_TAPS) for n in SMALL]
    red_parts = _unpack(red, full_shapes + [(1,)])
    loss = red_parts[-1].reshape(())
    g_small = dict(zip(SMALL, red_parts[:-1]))
    g_small["conv_w"] = lax.dynamic_slice_in_dim(g_small["conv_w"], me * conv_rows, conv_rows, 0).reshape(conv_w.shape)

    grads, deltas, new_m, new_v = {}, {}, {}, {}
    for n in BIG:
        back = (lambda a: a.T[None]) if n in COL_SHARDED else (lambda a: a[None])
        grads[n] = back(g_big[n])
        dl, nm, nv = _adamw(local(n, p[n]), g_big[n], local(n, p["m_" + n]), local(n, p["v_" + n]), "adamw_" + n)
        deltas[n], new_m[n], new_v[n] = back(dl), back(nm), back(nv)
    packed = [_pack([src[n] for n in SMALL]) for src in
              ({n: p[n] for n in SMALL}, g_small, {n: p["m_" + n] for n in SMALL}, {n: p["v_" + n] for n in SMALL})]
    small_shapes = [p[n].shape for n in SMALL]
    for dst, arr in zip((deltas, new_m, new_v), _adamw(*packed, "adamw_small")):
        dst.update(zip(SMALL, _unpack(arr, small_shapes)))
    grads.update(g_small)

    return (loss, gx[None], *[grads[n] for n in WEIGHTS], *[deltas[n] for n in WEIGHTS],
            *[new_m[n] for n in WEIGHTS], *[new_v[n] for n in WEIGHTS])
```

```python
import math
from typing import Callable, NamedTuple

import numpy as np
import jax
import jax.numpy as jnp
from jax import lax
from jax.experimental import pallas as pl
from jax.experimental.pallas import tpu as pltpu

F32 = jnp.float32
BF16 = jnp.bfloat16
HIGHEST = lax.Precision.HIGHEST
MESH = pl.DeviceIdType.MESH

EPS = 1e-6
NEG_BIG = -1e30
GDN_HEADS = 4
GDN_DIM = 128
CHUNK = 64
SWA_HEADS = 8
SWA_DIM = 64
PATTERNS = ((128, 1), (512, 4), (2048, 16))
RADIUS = 64
REL_BUCKETS = 32
REL_MAX_DISTANCE = 1024
CONV_TAPS = 5
N_SHARDS = 4
LANES = 128
VMEM_LIMIT = 56 * 1024 * 1024

ADAM_LR, ADAM_B1, ADAM_B2, ADAM_EPS, ADAM_WD, ADAM_STEP = 0.001, 0.9, 0.999, 1e-08, 0.01, 10


def _params(sem=None, vmem=None):
    return pltpu.CompilerParams(dimension_semantics=sem, vmem_limit_bytes=vmem)


def _resident(shape):
    nd = len(shape)
    return pl.BlockSpec(shape, lambda *_: (0,) * nd, pipeline_mode=pl.Buffered(1))


ANY = pl.BlockSpec(memory_space=pl.ANY)


class _Exchange(NamedTuple):
    arrays: tuple
    out_shape: tuple
    start: Callable
    finish: Callable


def _grid_call(body, name, nsteps, in_specs, out_specs, out_shape, operands, scratch=(), exchange=None):
    params = _params(("arbitrary",), VMEM_LIMIT)
    if exchange is None:
        res = pl.pallas_call(body, name=name, grid=(nsteps,), in_specs=list(in_specs), out_specs=list(out_specs),
                             out_shape=list(out_shape), scratch_shapes=list(scratch), compiler_params=params)(*operands)
        return list(res), []
    n_in, n_out, k, n_scr = len(in_specs), len(out_specs), len(exchange.arrays), len(scratch)

    def wrapped(*refs):
        ins, cin = refs[:n_in], refs[n_in:n_in + k]
        outs, cout = refs[n_in + k:n_in + k + n_out], refs[n_in + k + n_out:n_in + 2 * k + n_out]
        rest = refs[n_in + 2 * k + n_out:]
        scr, (send_sems, recv_sems) = rest[:n_scr], rest[n_scr:]

        @pl.when(pl.program_id(0) == 0)
        def _():
            exchange.start(cin, cout, send_sems, recv_sems)

        body(*ins, *outs, *scr)

        @pl.when(pl.program_id(0) == nsteps - 1)
        def _():
            exchange.finish(cin, cout, send_sems, recv_sems)

    res = pl.pallas_call(
        wrapped, name=name, grid=(nsteps,), in_specs=list(in_specs) + [ANY] * k, out_specs=list(out_specs) + [ANY] * k,
        out_shape=list(out_shape) + list(exchange.out_shape),
        scratch_shapes=list(scratch) + [pltpu.SemaphoreType.DMA((k, 3)), pltpu.SemaphoreType.DMA((k, 3))],
        compiler_params=params)(*operands, *exchange.arrays)
    return list(res[:n_out]), list(res[n_out:])


def _dot(a, b):
    return jnp.dot(a.astype(BF16), b.astype(BF16), preferred_element_type=F32)


def _dot_nt(a, b):
    return lax.dot_general(a.astype(BF16), b.astype(BF16), (((1,), (1,)), ((), ())), preferred_element_type=F32)


def _dot_tn(a, b):
    return lax.dot_general(a.astype(BF16), b.astype(BF16), (((0,), (0,)), ((), ())), preferred_element_type=F32)


def _dot_hi(a, b):
    return jnp.dot(a, b, preferred_element_type=F32, precision=HIGHEST)


def _sigmoid(x):
    return 1.0 / (1.0 + jnp.exp(-x))


def _rstd(xf):
    return lax.rsqrt(jnp.mean(xf * xf, axis=-1, keepdims=True) + EPS)


def _rms_bwd(xf, r, nw, dxn):
    xhat = xf * r
    dxh = dxn * nw
    dx = r * (dxh - xhat * jnp.mean(dxh * xhat, axis=-1, keepdims=True))
    return dx, jnp.sum(dxn * xhat, axis=0, keepdims=True)


def _ffn_fwd(x, nw, wg, wu, wd, tm, name, exchange=None):
    t, d = x.shape
    nj, fs, _ = wg.shape

    def body(x_ref, nw_ref, wg_ref, wu_ref, wd_ref, y_ref, xn_ref, g_ref, u_ref):
        xf = x_ref[...]
        xn = (xf * _rstd(xf) * nw_ref[...]).astype(BF16)
        xn_ref[...] = xn
        acc = jnp.zeros((tm, d), F32)
        for j in range(nj):
            g = _dot_nt(xn, wg_ref[j])
            u = _dot_nt(xn, wu_ref[j])
            h = (g * _sigmoid(g) * u).astype(BF16)
            acc = acc + jnp.dot(h, wd_ref[j], preferred_element_type=F32)
            g_ref[j] = g.astype(BF16)
            u_ref[j] = u.astype(BF16)
        y_ref[...] = xf + 0.5 * acc

    row = pl.BlockSpec((tm, d), lambda i: (i, 0))
    act = pl.BlockSpec((nj, tm, fs), lambda i: (0, i, 0))
    return _grid_call(
        body, name, t // tm,
        [row, _resident((1, d)), _resident(wg.shape), _resident(wu.shape), _resident(wd.shape)],
        [row, row, act, act],
        [jax.ShapeDtypeStruct((t, d), F32), jax.ShapeDtypeStruct((t, d), BF16),
         jax.ShapeDtypeStruct((nj, t, fs), BF16), jax.ShapeDtypeStruct((nj, t, fs), BF16)],
        (x, nw, wg, wu, wd), exchange=exchange)


def _ffn_bwd_dx(dy, x, nw, g, u, wg, wu, wd, tm, name, exchange=None):
    t, d = x.shape
    nj, fs, _ = wg.shape

    def body(dy_ref, x_ref, nw_ref, g_ref, u_ref, wg_ref, wu_ref, wd_ref,
             dx_ref, dyh_ref, dg_ref, du_ref, h_ref, dnw_ref):
        @pl.when(pl.program_id(0) == 0)
        def _():
            dnw_ref[...] = jnp.zeros_like(dnw_ref)

        dyv = dy_ref[...]
        dyh = (0.5 * dyv).astype(BF16)
        dyh_ref[...] = dyh
        dxn = jnp.zeros((tm, d), F32)
        dh_next = _dot_nt(dyh, wd_ref[0])
        for j in range(nj):
            dh = dh_next
            gv = g_ref[j].astype(F32)
            uv = u_ref[j].astype(F32)
            sg = _sigmoid(gv)
            si = gv * sg
            dg = (dh * uv * (sg * (1.0 + gv * (1.0 - sg)))).astype(BF16)
            du = (dh * si).astype(BF16)
            if j + 1 < nj:
                dh_next = _dot_nt(dyh, wd_ref[j + 1])
            h_ref[j] = (si * uv).astype(BF16)
            dg_ref[j] = dg
            du_ref[j] = du
            dxn = dxn + _dot(dg, wg_ref[j]) + _dot(du, wu_ref[j])
        xf = x_ref[...]
        dxr, dnw = _rms_bwd(xf, _rstd(xf), nw_ref[...], dxn)
        dx_ref[...] = dyv + dxr
        dnw_ref[...] += dnw

    row = pl.BlockSpec((tm, d), lambda i: (i, 0))
    act = pl.BlockSpec((nj, tm, fs), lambda i: (0, i, 0))
    act_shape = jax.ShapeDtypeStruct((nj, t, fs), BF16)
    return _grid_call(
        body, name, t // tm,
        [row, row, _resident((1, d)), act, act, _resident(wg.shape), _resident(wu.shape), _resident(wd.shape)],
        [row, row, act, act, act, pl.BlockSpec((1, d), lambda i: (0, 0))],
        [jax.ShapeDtypeStruct((t, d), F32), jax.ShapeDtypeStruct((t, d), BF16),
         act_shape, act_shape, act_shape, jax.ShapeDtypeStruct((1, d), F32)],
        (dy, x, nw, g, u, wg, wu, wd), exchange=exchange)


def _matmul_tn(a, b, tk, name):
    a3, b3 = a.ndim == 3, b.ndim == 3
    nj = a.shape[0] if a3 else (b.shape[0] if b3 else 1)
    t, m = a.shape[-2:]
    n = b.shape[-1]
    nt = t // tk

    def body(a_ref, b_ref, o_ref, acc_ref):
        k = pl.program_id(1)

        @pl.when(k == 0)
        def _():
            acc_ref[...] = jnp.zeros_like(acc_ref)

        acc_ref[...] += lax.dot_general(a_ref[...], b_ref[...], (((0,), (0,)), ((), ())),
                                        preferred_element_type=F32)

        @pl.when(k == nt - 1)
        def _():
            o_ref[...] = acc_ref[...].astype(o_ref.dtype)

    a_spec = (pl.BlockSpec((None, tk, m), lambda j, k: (j, k, 0)) if a3
              else pl.BlockSpec((tk, m), lambda j, k: (k, 0)))
    b_spec = (pl.BlockSpec((None, tk, n), lambda j, k: (j, k, 0)) if b3
              else pl.BlockSpec((tk, n), lambda j, k: (k, 0)))
    return pl.pallas_call(
        body, name=name, grid=(nj, nt),
        in_specs=[a_spec, b_spec],
        out_specs=pl.BlockSpec((None, m, n), lambda j, k: (j, 0, 0)),
        out_shape=jax.ShapeDtypeStruct((nj, m, n), BF16),
        scratch_shapes=[pltpu.VMEM((m, n), F32)],
        compiler_params=_params(("arbitrary", "arbitrary"), VMEM_LIMIT),
    )(a, b)


P_QKVA, P_Z, P_AB, P_QKVB = (0, 1536), (1536, 2048), (2048, 2176), (2176, 3712)
P_PIECES = (P_QKVA, P_Z, P_AB, P_QKVB)
P_COLS = 3712


def _mix_in_fwd(x1, nw, wp, tm):
    t, d = x1.shape

    def body(x_ref, nw_ref, w_ref, hn_ref, *outs):
        xf = x_ref[...]
        xn = (xf * _rstd(xf) * nw_ref[...]).astype(BF16)
        hn_ref[...] = xn
        for (a, b), o_ref in zip(P_PIECES, outs):
            o_ref[...] = _dot_nt(xn, w_ref[a:b, :])

    row = pl.BlockSpec((tm, d), lambda i: (i, 0))
    return pl.pallas_call(
        body, name="mix_in_fwd", grid=(t // tm,),
        in_specs=[row, _resident((1, d)), _resident(wp.shape)],
        out_specs=[row] + [pl.BlockSpec((tm, b - a), lambda i: (i, 0)) for a, b in P_PIECES],
        out_shape=[jax.ShapeDtypeStruct((t, d), BF16)]
                  + [jax.ShapeDtypeStruct((t, b - a), F32) for a, b in P_PIECES],
        compiler_params=_params(("arbitrary",), VMEM_LIMIT),
    )(x1, nw, wp)


def _mix_in_bwd_dx(dx, x1, nw, dpieces, wp, tm):
    t, d = x1.shape

    def body(dx_ref, x_ref, nw_ref, p0, p1, p2, p3, w_ref, o_ref, dnw_ref):
        @pl.when(pl.program_id(0) == 0)
        def _():
            dnw_ref[...] = jnp.zeros_like(dnw_ref)

        dh = jnp.zeros((tm, d), F32)
        for (a, b), p_ref in zip(P_PIECES, (p0, p1, p2, p3)):
            dh = dh + _dot(p_ref[...], w_ref[a:b, :])
        xf = x_ref[...]
        dxr, dnw = _rms_bwd(xf, _rstd(xf), nw_ref[...], dh)
        o_ref[...] = dx_ref[...] + dxr
        dnw_ref[...] += dnw

    row = pl.BlockSpec((tm, d), lambda i: (i, 0))
    return pl.pallas_call(
        body, name="mix_in_bwd_dx", grid=(t // tm,),
        in_specs=[row, row, _resident((1, d))]
                 + [pl.BlockSpec((tm, b - a), lambda i: (i, 0)) for a, b in P_PIECES] + [_resident(wp.shape)],
        out_specs=[row, pl.BlockSpec((1, d), lambda i: (0, 0))],
        out_shape=[jax.ShapeDtypeStruct((t, d), F32), jax.ShapeDtypeStruct((1, d), F32)],
        compiler_params=_params(("arbitrary",), VMEM_LIMIT),
    )(dx, x1, nw, *dpieces, wp)


def _mix_out_fwd(x1, oa, ob, w, tm):
    t, d = x1.shape
    half = oa.shape[1]

    def body(x_ref, oa_ref, ob_ref, w_ref, o_ref):
        o_ref[...] = (x_ref[...] + _dot(oa_ref[...], w_ref[0:half, :]) + _dot(ob_ref[...], w_ref[half:2 * half, :]))

    row = pl.BlockSpec((tm, d), lambda i: (i, 0))
    hrow = pl.BlockSpec((tm, half), lambda i: (i, 0))
    return pl.pallas_call(
        body, name="mix_out_fwd", grid=(t // tm,),
        in_specs=[row, hrow, hrow, _resident(w.shape)],
        out_specs=row, out_shape=jax.ShapeDtypeStruct((t, d), F32),
        compiler_params=_params(("arbitrary",), VMEM_LIMIT),
    )(x1, oa, ob, w)


def _mix_out_bwd(dx2, w, tm):
    t, d = dx2.shape
    half = w.shape[0] // 2

    def body(dx_ref, w_ref, doa_ref, dob_ref, dxb_ref):
        dxb = dx_ref[...].astype(BF16)
        dxb_ref[...] = dxb
        doa_ref[...] = _dot_nt(dxb, w_ref[0:half, :])
        dob_ref[...] = _dot_nt(dxb, w_ref[half:2 * half, :])

    row = pl.BlockSpec((tm, d), lambda i: (i, 0))
    hrow = pl.BlockSpec((tm, half), lambda i: (i, 0))
    return pl.pallas_call(
        body, name="mix_out_bwd", grid=(t // tm,),
        in_specs=[row, _resident(w.shape)],
        out_specs=[hrow, hrow, row],
        out_shape=[jax.ShapeDtypeStruct((t, half), F32), jax.ShapeDtypeStruct((t, half), F32),
                   jax.ShapeDtypeStruct((t, d), BF16)],
        compiler_params=_params(("arbitrary",), VMEM_LIMIT),
    )(dx2, w)


def _final_loss(x3, fw, target, tm):
    t, d = x3.shape

    def body(x_ref, w_ref, t_ref, dx_ref, loss_ref, dw_ref):
        @pl.when(pl.program_id(0) == 0)
        def _():
            loss_ref[...] = jnp.zeros_like(loss_ref)
            dw_ref[...] = jnp.zeros_like(dw_ref)

        xf = x_ref[...]
        r = _rstd(xf)
        err = xf * r * w_ref[...] - t_ref[...]
        loss_ref[...] += 0.5 * jnp.sum(jnp.mean(err * err, axis=-1, keepdims=True), axis=0, keepdims=True)
        dxr, dw = _rms_bwd(xf, r, w_ref[...], err * (1.0 / d))
        dx_ref[...] = dxr
        dw_ref[...] += dw

    row = pl.BlockSpec((tm, d), lambda i: (i, 0))
    return pl.pallas_call(
        body, name="final_loss", grid=(t // tm,),
        in_specs=[row, _resident((1, d)), row],
        out_specs=[row, pl.BlockSpec((1, LANES), lambda i: (0, 0)), pl.BlockSpec((1, d), lambda i: (0, 0))],
        out_shape=[jax.ShapeDtypeStruct((t, d), F32), jax.ShapeDtypeStruct((1, LANES), F32),
                   jax.ShapeDtypeStruct((1, d), F32)],
        compiler_params=_params(("arbitrary",), VMEM_LIMIT),
    )(x3, fw, target)


HALO = 8


def _halo_row_specs(tr, cols, nrow8):
    per = tr // HALO
    return [pl.BlockSpec((tr, cols), lambda i: (i, 0)),
            pl.BlockSpec((HALO, cols), lambda i: (jnp.maximum(i * per - 1, 0), 0)),
            pl.BlockSpec((HALO, cols), lambda i: (jnp.minimum((i + 1) * per, nrow8 - 1), 0))]


def _conv_window(xm, xp, xn, first, last, cols):
    prev = jnp.where(first, 0.0, xp[:, cols])
    nxt = jnp.where(last, 0.0, xn[:, cols])
    return jnp.concatenate([prev, xm[:, cols], nxt], axis=0)


def _shift_rows(xw, off):
    n = xw.shape[0]
    sh = (-off) % n
    return xw if sh == 0 else pltpu.roll(xw, sh, 0)


def _conv_pre(xw, cw_ref, cols):
    acc = None
    for j in range(CONV_TAPS):
        term = _shift_rows(xw, j - CONV_TAPS // 2) * cw_ref[j:j + 1, cols]
        acc = term if acc is None else acc + term
    return acc


def _softplus(x):
    u = jnp.exp(-jnp.abs(x))
    w = 1.0 + u
    log1p = jnp.where(w == 1.0, u, jnp.log(w) * u / jnp.where(w == 1.0, 1.0, w - 1.0))
    return jnp.maximum(x, 0.0) + log1p


def _gdn_prep_fwd(qkva, cw, ab, gp, tr):
    t, c = qkva.shape
    nt = t // tr
    ncb = c // LANES

    def body(xm, xp, xn, cw_ref, ab_ref, gp_ref, o_ref, gb_ref):
        i = pl.program_id(0)
        first, last = i == 0, i == nt - 1
        for cb in range(ncb):
            cols = slice(cb * LANES, (cb + 1) * LANES)
            xw = _conv_window(xm, xp, xn, first, last, cols)
            pre = _conv_pre(xw, cw_ref, cols)[HALO:HALO + tr]
            y = pre * _sigmoid(pre)
            if cb < 2 * GDN_HEADS:
                y = y * lax.rsqrt(jnp.sum(y * y, axis=-1, keepdims=True) + EPS)
            if cb < GDN_HEADS:
                y = y * (GDN_DIM ** -0.5)
            o_ref[:, cols] = y
        abv = ab_ref[...]
        lane = lax.broadcasted_iota(jnp.int32, abv.shape, 1)
        g = -jnp.exp(gp_ref[0:1, :]) * _softplus(abv + gp_ref[1:2, :])
        gb_ref[...] = jnp.where(lane < 8, g, jnp.where(lane < 16, _sigmoid(abv), 0.0))

    return pl.pallas_call(
        body, name="gdn_prep_fwd", grid=(nt,),
        in_specs=_halo_row_specs(tr, c, t // HALO)
                 + [_resident(cw.shape), pl.BlockSpec((tr, LANES), lambda i: (i, 0)), _resident(gp.shape)],
        out_specs=[pl.BlockSpec((tr, c), lambda i: (i, 0)), pl.BlockSpec((tr, LANES), lambda i: (i, 0))],
        out_shape=[jax.ShapeDtypeStruct((t, c), F32), jax.ShapeDtypeStruct((t, LANES), F32)],
        compiler_params=_params(("arbitrary",), VMEM_LIMIT),
    )(qkva, qkva, qkva, cw, ab, gp)


def _gdn_prep_bwd(qkva, cw, ab, gp, dy, dgates, tr):
    t, c = qkva.shape
    nt = t // tr
    ncb = c // LANES

    def body(xm, xp, xn, fm, fp, fn, cw_ref, ab_ref, gp_ref, gf_ref, dx_ref, dab_ref, dcw_ref, dgp_ref):
        i = pl.program_id(0)
        first, last = i == 0, i == nt - 1

        @pl.when(first)
        def _():
            dcw_ref[...] = jnp.zeros_like(dcw_ref)
            dgp_ref[...] = jnp.zeros_like(dgp_ref)

        sub8 = lax.broadcasted_iota(jnp.int32, (8, LANES), 0)
        for cb in range(ncb):
            cols = slice(cb * LANES, (cb + 1) * LANES)
            xw = _conv_window(xm, xp, xn, first, last, cols)
            dyw = _conv_window(fm, fp, fn, first, last, cols)
            pre = _conv_pre(xw, cw_ref, cols)
            sg = _sigmoid(pre)
            s = pre * sg
            if cb < 2 * GDN_HEADS:
                scale = (GDN_DIM ** -0.5) if cb < GDN_HEADS else 1.0
                r = lax.rsqrt(jnp.sum(s * s, axis=-1, keepdims=True) + EPS)
                dn = dyw * scale
                ds = r * dn - s * (r * r * r) * jnp.sum(dn * s, axis=-1, keepdims=True)
            else:
                ds = dyw
            dpre = ds * (sg * (1.0 + pre * (1.0 - sg)))
            dx = None
            dcw = jnp.zeros((8, LANES), F32)
            for j in range(CONV_TAPS):
                off = j - CONV_TAPS // 2
                term = _shift_rows(dpre, -off)[HALO:HALO + tr] * cw_ref[j:j + 1, cols]
                dx = term if dx is None else dx + term
                tap = jnp.sum(dpre[HALO:HALO + tr] * _shift_rows(xw, off)[HALO:HALO + tr], axis=0, keepdims=True)
                dcw = dcw + jnp.where(sub8 == j, tap, 0.0)
            dx_ref[:, cols] = dx.astype(BF16)
            dcw_ref[:, cols] += dcw

        abv = ab_ref[...]
        dgb = gf_ref[...]
        lane = lax.broadcasted_iota(jnp.int32, abv.shape, 1)
        nea = -jnp.exp(gp_ref[0:1, :])
        xs = abv + gp_ref[1:2, :]
        g = nea * _softplus(xs)
        beta = _sigmoid(abv)
        da = dgb * nea * _sigmoid(xs)
        dab = jnp.where(lane < 8, da, jnp.where(lane < 16, dgb * beta * (1.0 - beta), 0.0))
        dab_ref[...] = dab.astype(BF16)
        keep = lane[0:1, :] < 8
        dalog = jnp.where(keep, jnp.sum(dgb * g, axis=0, keepdims=True), 0.0)
        ddtb = jnp.where(keep, jnp.sum(da, axis=0, keepdims=True), 0.0)
        dgp_ref[...] += jnp.where(sub8 == 0, dalog, 0.0) + jnp.where(sub8 == 1, ddtb, 0.0)

    lrow = pl.BlockSpec((tr, LANES), lambda i: (i, 0))
    halo = _halo_row_specs(tr, c, t // HALO)
    return pl.pallas_call(
        body, name="gdn_prep_bwd", grid=(nt,),
        in_specs=halo + halo + [_resident(cw.shape), lrow, _resident(gp.shape), lrow],
        out_specs=[pl.BlockSpec((tr, c), lambda i: (i, 0)), lrow,
                   pl.BlockSpec(cw.shape, lambda i: (0, 0)), pl.BlockSpec(gp.shape, lambda i: (0, 0))],
        out_shape=[jax.ShapeDtypeStruct((t, c), BF16), jax.ShapeDtypeStruct((t, LANES), BF16),
                   jax.ShapeDtypeStruct(cw.shape, F32), jax.ShapeDtypeStruct(gp.shape, F32)],
        compiler_params=_params(("arbitrary",), VMEM_LIMIT),
    )(qkva, qkva, qkva, dy, dy, dy, cw, ab, gp, dgates)


def _chunk_masks(lower):
    ii = lax.broadcasted_iota(jnp.int32, (CHUNK, CHUNK), 0)
    jj = lax.broadcasted_iota(jnp.int32, (CHUNK, CHUNK), 1)
    incl = (ii >= jj) if lower else (ii <= jj)
    strict = (ii > jj) if lower else (ii < jj)
    return ii, jj, incl, strict


def _dot3(a, b):
    ah = a.astype(BF16)
    al = (a - ah.astype(F32)).astype(BF16)
    bh = b.astype(BF16)
    bl = (b - bh.astype(F32)).astype(BF16)
    d = lambda u, v: jnp.dot(u, v, preferred_element_type=F32)
    return d(ah, bh) + (d(ah, bl) + d(al, bh))


def _tri_inv_many(lmats, ii, jj):
    m16 = (ii // 16) == (jj // 16)
    m32 = (ii // 32) == (jj // 32)
    eye = jnp.where(ii == jj, 1.0, 0.0)
    l16 = [jnp.where(m16, l, 0.0) for l in lmats]
    p2 = [_dot3(a, a) for a in l16]
    p4 = [_dot3(a, a) for a in p2]
    p8 = [_dot3(a, a) for a in p4]
    xs = [eye - a for a in l16]
    for ps in (p2, p4, p8):
        xs = [x + _dot3(x, p) for x, p in zip(xs, ps)]
    for off in ([jnp.where(m32 & jnp.logical_not(m16), l, 0.0) for l in lmats],
                [jnp.where(m32, 0.0, l) for l in lmats]):
        ys = [_dot3(x, c) for x, c in zip(xs, off)]
        xs = [x - _dot3(y, x) for x, y in zip(xs, ys)]
    return xs


def _col_to_row(col, ii, jj):
    return jnp.sum(jnp.where(ii == jj, col, 0.0), axis=0, keepdims=True)


def _row_to_col(row, ii, jj):
    return jnp.sum(jnp.where(ii == jj, row, 0.0), axis=1, keepdims=True)


def _chain_common(q, k, v, graw_col, graw_row, bcol, masks):
    ii, jj, incl, strict = masks
    inclt = jnp.logical_not(strict)
    gcol = jnp.sum(jnp.where(incl, graw_row, 0.0), axis=1, keepdims=True)
    grow = jnp.sum(jnp.where(inclt, graw_col, 0.0), axis=0, keepdims=True)
    glast = jnp.sum(graw_row, axis=1, keepdims=True)
    decay = jnp.where(incl, jnp.exp(jnp.where(incl, gcol - grow, 0.0)), 0.0)
    kb = k * bcol
    vb = v * bcol
    eg = jnp.exp(gcol)
    ek = jnp.exp(glast - gcol)
    kbg = kb * eg
    amat = _dot_nt(kb, k)
    qk = _dot_nt(q, k)
    return dict(gcol=gcol, glast=glast, decay=decay, kb=kb, vb=vb, eg=eg, ek=ek, kbg=kbg, amat=amat, qk=qk,
                intra=qk * decay, qg=q * eg, kdec=k * ek)


def _gdn_fwd(qkvc, gb, gbt):
    tm, u, w, qg, kd, intra, egl = _gdn_local_fwd(qkvc, gb, gbt)
    o_f, o_b, s_f, s_b, vn_f, vn_b = _gdn_scan_fwd(u, w, qg, kd, intra, egl, qkvc.shape[0])
    return o_f, o_b, dict(tm=tm, w=w, qg=qg, kd=kd, intra=intra, egl=egl, s=(s_f, s_b), vn=(vn_f, vn_b))


N_CHAINS = 2 * GDN_HEADS


LOCAL_CHUNKS = 2


def _load_chains(x_ref, g_ref, gt_ref, cc=0):
    hd = GDN_HEADS * GDN_DIM
    rows = slice(cc * CHUNK, (cc + 1) * CHUNK)
    chains = []
    for d in range(2):
        masks = _chunk_masks(d == 0)
        for h in range(GDN_HEADS):
            ch = d * GDN_HEADS + h
            q = x_ref[rows, h * GDN_DIM:(h + 1) * GDN_DIM]
            k = x_ref[rows, hd + h * GDN_DIM:hd + (h + 1) * GDN_DIM]
            v = x_ref[rows, 2 * hd + h * GDN_DIM:2 * hd + (h + 1) * GDN_DIM]
            bcol = g_ref[rows, 8 + ch:9 + ch]
            cm = _chain_common(q, k, v, g_ref[rows, ch:ch + 1], gt_ref[cc, ch:ch + 1, :], bcol, masks)
            chains.append(dict(cm, q=q, k=k, v=v, bcol=bcol, masks=masks, ch=ch, h=h, cc=cc))
    return chains


def _chain_shape(rows, cols, dtype):
    return lambda nc: jax.ShapeDtypeStruct((nc, N_CHAINS, rows, cols), dtype)


def _gdn_local_fwd(qkvc, gb, gbt):
    t = qkvc.shape[0]
    nc = t // CHUNK
    hd = GDN_HEADS * GDN_DIM

    def body(x_ref, g_ref, gt_ref, t_ref, u_ref, w_ref, qg_ref, kd_ref, in_ref, eg_ref):
        chains = [c for cc in range(LOCAL_CHUNKS) for c in _load_chains(x_ref, g_ref, gt_ref, cc)]
        ii, jj = chains[0]["masks"][0:2]
        tms = _tri_inv_many([jnp.where(c["masks"][3], c["amat"] * c["decay"], 0.0) for c in chains], ii, jj)
        uws = [_dot(tm, jnp.concatenate([c["vb"], c["kbg"]], axis=1)) for tm, c in zip(tms, chains)]
        for c, tm, uw in zip(chains, tms, uws):
            cc, ch = c["cc"], c["ch"]
            t_ref[cc, ch] = tm
            u_ref[cc, ch] = uw[:, :GDN_DIM]
            w_ref[cc, ch] = uw[:, GDN_DIM:].astype(BF16)
            qg_ref[cc, ch] = c["qg"].astype(BF16)
            kd_ref[cc, ch] = c["kdec"].astype(BF16)
            in_ref[cc, ch] = c["intra"].astype(BF16)
            eg_ref[cc, ch:ch + 1, :] = jnp.broadcast_to(jnp.exp(c["glast"]), (1, LANES))

    lc = LOCAL_CHUNKS
    blk = lambda rows, cols: pl.BlockSpec((lc, N_CHAINS, rows, cols), lambda n: (n, 0, 0, 0))
    shapes = [_chain_shape(CHUNK, CHUNK, F32), _chain_shape(CHUNK, GDN_DIM, F32), _chain_shape(CHUNK, GDN_DIM, BF16),
              _chain_shape(CHUNK, GDN_DIM, BF16), _chain_shape(CHUNK, GDN_DIM, BF16), _chain_shape(CHUNK, CHUNK, BF16)]
    return tuple(pl.pallas_call(
        body, name="gdn_local_fwd", grid=(nc // lc,),
        in_specs=[pl.BlockSpec((lc * CHUNK, 3 * hd), lambda n: (n, 0)), pl.BlockSpec((lc * CHUNK, LANES), lambda n: (n, 0)),
                  pl.BlockSpec((lc, 16, CHUNK), lambda n: (n, 0, 0))],
        out_specs=[blk(CHUNK, CHUNK), blk(CHUNK, GDN_DIM), blk(CHUNK, GDN_DIM), blk(CHUNK, GDN_DIM),
                   blk(CHUNK, GDN_DIM), blk(CHUNK, CHUNK), pl.BlockSpec((lc, N_CHAINS, LANES), lambda n: (n, 0, 0))],
        out_shape=[s(nc) for s in shapes] + [jax.ShapeDtypeStruct((nc, N_CHAINS, LANES), F32)],
        compiler_params=_params(("arbitrary",), VMEM_LIMIT),
    )(qkvc, gb, gbt))


def _dir_specs(nc, rev):
    def spec(d, rows, cols, own=False):
        chunk = (lambda n: n) if (d == 0) != rev else (lambda n: nc - 1 - n)
        blk = 0 if own else d
        if rows is None:
            return pl.BlockSpec((1, GDN_HEADS if own else N_CHAINS, cols), lambda n: (chunk(n), 0, 0))
        return pl.BlockSpec((1, GDN_HEADS, rows, cols), lambda n: (chunk(n), blk, 0, 0))

    def rows_spec(d, cols):
        chunk = (lambda n: n) if (d == 0) != rev else (lambda n: nc - 1 - n)
        return pl.BlockSpec((CHUNK, cols), lambda n: (chunk(n), 0))
    return spec, rows_spec


def _gdn_scan_fwd(u, w, qg, kd, intra, egl, t):
    nc = t // CHUNK
    hd = GDN_HEADS * GDN_DIM

    def body(*refs):
        ins, outs, state = refs[:12], refs[12:18], refs[18]
        @pl.when(pl.program_id(0) == 0)
        def _():
            state[...] = jnp.zeros_like(state)

        chains = [(d, h) for d in range(2) for h in range(GDN_HEADS)]
        pick = lambda k, d, h: ins[2 * k + d][0, h]
        states = [state[ch] for ch in range(N_CHAINS)]
        sbs = [s.astype(BF16) for s in states]
        ws = [_dot(pick(1, d, h), sb) for (d, h), sb in zip(chains, sbs)]
        o1 = [_dot(pick(2, d, h), sb) for (d, h), sb in zip(chains, sbs)]
        vns = [(pick(0, d, h) - wsb).astype(BF16) for (d, h), wsb in zip(chains, ws)]
        o2 = [_dot(pick(4, d, h), vn) for (d, h), vn in zip(chains, vns)]
        kv = [_dot_tn(pick(3, d, h), vn) for (d, h), vn in zip(chains, vns)]
        for ch, (d, h) in enumerate(chains):
            outs[d][:, h * GDN_DIM:(h + 1) * GDN_DIM] = o1[ch] + o2[ch]
            outs[2 + d][0, h] = states[ch]
            outs[4 + d][0, h] = vns[ch]
            state[ch] = states[ch] * ins[10 + d][0, ch:ch + 1, :] + kv[ch]

    spec, rows_spec = _dir_specs(nc, False)
    pair = lambda rows, cols, own=False: [spec(0, rows, cols, own), spec(1, rows, cols, own)]
    s_shape = jax.ShapeDtypeStruct((nc, GDN_HEADS, GDN_DIM, GDN_DIM), F32)
    vn_shape = jax.ShapeDtypeStruct((nc, GDN_HEADS, CHUNK, GDN_DIM), BF16)
    return pl.pallas_call(
        body, name="gdn_scan_fwd", grid=(nc,),
        in_specs=(pair(CHUNK, GDN_DIM) + pair(CHUNK, GDN_DIM) + pair(CHUNK, GDN_DIM) + pair(CHUNK, GDN_DIM)
                  + pair(CHUNK, CHUNK) + pair(None, LANES)),
        out_specs=([rows_spec(0, hd), rows_spec(1, hd)] + pair(GDN_DIM, GDN_DIM, True)
                   + pair(CHUNK, GDN_DIM, True)),
        out_shape=[jax.ShapeDtypeStruct((t, hd), F32), jax.ShapeDtypeStruct((t, hd), F32),
                   s_shape, s_shape, vn_shape, vn_shape],
        scratch_shapes=[pltpu.VMEM((N_CHAINS, GDN_DIM, GDN_DIM), F32)],
        compiler_params=_params(("arbitrary",), VMEM_LIMIT),
    )(u, u, w, w, qg, qg, kd, kd, intra, intra, egl, egl)


def _gdn_bwd(qkvc, gb, gbt, do, saved, exchange=None):
    scan = _gdn_scan_bwd(do, saved, qkvc.shape[0])
    return _gdn_local_bwd(qkvc, gb, gbt, do, saved, scan, exchange)


def _gdn_scan_bwd(do, saved, t):
    nc = t // CHUNK
    hd = GDN_HEADS * GDN_DIM

    def body(*refs):
        ins, outs, dstate = refs[:16], refs[16:26], refs[26]
        @pl.when(pl.program_id(0) == 0)
        def _():
            dstate[...] = jnp.zeros_like(dstate)

        chains = [(d, h) for d in range(2) for h in range(GDN_HEADS)]
        pick = lambda k, d, h: ins[2 * k + d][0, h]
        dss = [dstate[ch] for ch in range(N_CHAINS)]
        dsbs = [ds.astype(BF16) for ds in dss]
        ss = [pick(1, d, h) for d, h in chains]
        sbs = [s.astype(BF16) for s in ss]
        dos = [ins[d][:, h * GDN_DIM:(h + 1) * GDN_DIM].astype(BF16) for d, h in chains]
        dv1 = [_dot_tn(pick(5, d, h), dov) for (d, h), dov in zip(chains, dos)]
        dv2 = [_dot(pick(4, d, h), dsb) for (d, h), dsb in zip(chains, dsbs)]
        ds1 = [_dot_tn(pick(3, d, h), dov) for (d, h), dov in zip(chains, dos)]
        dkds = [_dot_nt(pick(6, d, h), dsb) for (d, h), dsb in zip(chains, dsbs)]
        dqgs = [_dot_nt(dov, sb) for dov, sb in zip(dos, sbs)]
        dvns = [(a + b).astype(BF16) for a, b in zip(dv1, dv2)]
        ds2 = [_dot_tn(pick(2, d, h), dvn) for (d, h), dvn in zip(chains, dvns)]
        dws = [_dot_nt(dvn, sb) for dvn, sb in zip(dvns, sbs)]
        for ch, (d, h) in enumerate(chains):
            egl = ins[14 + d][0, ch:ch + 1, :]
            outs[d][0, h] = dvns[ch]
            outs[2 + d][0, h] = (-dws[ch]).astype(BF16)
            outs[4 + d][0, h] = dqgs[ch]
            outs[6 + d][0, h] = dkds[ch]
            outs[8 + d][0, h:h + 1, :] = egl * jnp.sum(jnp.sum(ss[ch] * dss[ch], axis=1, keepdims=True),
                                                       axis=0, keepdims=True)
            dstate[ch] = ds1[ch] + egl * dss[ch] - ds2[ch]

    spec, rows_spec = _dir_specs(nc, True)
    pair = lambda rows, cols, own=False: [spec(0, rows, cols, own), spec(1, rows, cols, own)]
    s_f, s_b = saved["s"]
    vn_f, vn_b = saved["vn"]
    w, qg, kd, intra, egl = saved["w"], saved["qg"], saved["kd"], saved["intra"], saved["egl"]
    own = lambda rows, cols, dtype: jax.ShapeDtypeStruct((nc, GDN_HEADS, rows, cols), dtype)
    row_shape = jax.ShapeDtypeStruct((nc, GDN_HEADS, LANES), F32)
    return pl.pallas_call(
        body, name="gdn_scan_bwd", grid=(nc,),
        in_specs=([rows_spec(0, hd), rows_spec(1, hd)] + pair(GDN_DIM, GDN_DIM, True) + pair(CHUNK, GDN_DIM)
                  + pair(CHUNK, GDN_DIM) + pair(CHUNK, GDN_DIM) + pair(CHUNK, CHUNK) + pair(CHUNK, GDN_DIM, True)
                  + pair(None, LANES)),
        out_specs=(pair(CHUNK, GDN_DIM, True) + pair(CHUNK, GDN_DIM, True) + pair(CHUNK, GDN_DIM, True)
                   + pair(CHUNK, GDN_DIM, True) + pair(None, LANES, True)),
        out_shape=[own(CHUNK, GDN_DIM, BF16)] * 4 + [own(CHUNK, GDN_DIM, F32)] * 4 + [row_shape] * 2,
        scratch_shapes=[pltpu.VMEM((N_CHAINS, GDN_DIM, GDN_DIM), F32)],
        compiler_params=_params(("arbitrary",), VMEM_LIMIT),
    )(do, do, s_f, s_b, w, w, qg, qg, kd, kd, intra, intra, vn_f, vn_b, egl, egl)


def _dot3_nt(a, b):
    ah = a.astype(BF16)
    al = (a - ah.astype(F32)).astype(BF16)
    bh = b.astype(BF16)
    bl = (b - bh.astype(F32)).astype(BF16)
    return _dot_nt(ah, bh) + (_dot_nt(ah, bl) + _dot_nt(al, bh))


def _dot3_tn(a, b):
    ah = a.astype(BF16)
    al = (a - ah.astype(F32)).astype(BF16)
    bh = b.astype(BF16)
    bl = (b - bh.astype(F32)).astype(BF16)
    return _dot_tn(ah, bh) + (_dot_tn(ah, bl) + _dot_tn(al, bh))


def _gdn_local_bwd(qkvc, gb, gbt, do, saved, scan, exchange=None):
    t = qkvc.shape[0]
    nc = t // CHUNK
    hd = GDN_HEADS * GDN_DIM

    def body(*refs):
        x_ref, g_ref, gt_ref, do_ref, t_ref = refs[:5]
        per_dir = refs[5:17]
        dx_ref, dg_ref = refs[17:]
        chains = [c for cc in range(LOCAL_CHUNKS) for c in _load_chains(x_ref, g_ref, gt_ref, cc)]
        lane = lax.broadcasted_iota(jnp.int32, (CHUNK, LANES), 1)
        dgates = [jnp.zeros((CHUNK, LANES), F32) for _ in range(LOCAL_CHUNKS)]
        for c in chains:
            d = c["ch"] // GDN_HEADS
            vn_ref, dvn_ref, dw_ref, dqg_ref, dkd_ref, dgl_ref = per_dir[d::2]
            h, cc = c["h"], c["cc"]
            rows = slice(cc * CHUNK, (cc + 1) * CHUNK)
            c.update(tm=t_ref[cc, c["ch"]], dov=do_ref[rows, h * GDN_DIM:(h + 1) * GDN_DIM], vnew=vn_ref[cc, h],
                     dvnew=dvn_ref[cc, h], dw=dw_ref[cc, h], dqg=dqg_ref[cc, h], dkdec=dkd_ref[cc, h],
                     dglast=dgl_ref[cc, h:h + 1, 0:1])
        dintras = [_dot_nt(c["dov"], c["vnew"]) for c in chains]
        dts = [_dot_nt(c["dvnew"], c["vb"]) + _dot_nt(c["dw"], c["kbg"]) for c in chains]
        dvbs = [_dot_tn(c["tm"], c["dvnew"]) for c in chains]
        dkbgs = [_dot_tn(c["tm"], c["dw"]) for c in chains]
        tdts = [_dot3_nt(dt, c["tm"]) for dt, c in zip(dts, chains)]
        dls = [jnp.where(c["masks"][3], -_dot3_tn(c["tm"], tdt), 0.0) for tdt, c in zip(tdts, chains)]
        das = [dl * c["decay"] for dl, c in zip(dls, chains)]
        dqks = [jnp.where(c["masks"][2], di, 0.0) * c["decay"] for di, c in zip(dintras, chains)]
        dkb1 = [_dot(da, c["k"]) for da, c in zip(das, chains)]
        dk1 = [_dot_tn(da, c["kb"]) for da, c in zip(das, chains)]
        dk2 = [_dot_tn(dqk, c["q"]) for dqk, c in zip(dqks, chains)]
        dq1 = [_dot(dqk, c["k"]) for dqk, c in zip(dqks, chains)]
        grads, mms, p_gs, p_betas, p_kds = [], [], [], [], []
        for n, c in enumerate(chains):
            incl = c["masks"][2]
            dkb = dkb1[n] + dkbgs[n] * c["eg"]
            kd = c["dkdec"] * c["kdec"]
            mms.append((dls[n] * c["amat"] + jnp.where(incl, dintras[n], 0.0) * c["qk"]) * c["decay"])
            p_gs.append(c["dqg"] * c["qg"] - kd + dkbgs[n] * c["kbg"])
            p_betas.append(dkb * c["k"] + dvbs[n] * c["v"])
            p_kds.append(kd)
            grads.append((dq1[n] + c["dqg"] * c["eg"],
                          dk1[n] + dk2[n] + c["dkdec"] * c["ek"] + dkb * c["bcol"],
                          dvbs[n] * c["bcol"]))
        row_sums = [jnp.sum(mm, axis=1, keepdims=True) for mm in mms]
        col_sums = [jnp.sum(mm, axis=0, keepdims=True) for mm in mms]
        g_sums = [jnp.sum(pg, axis=1, keepdims=True) for pg in p_gs]
        dbetas = [jnp.sum(pb, axis=1, keepdims=True) for pb in p_betas]
        kd_tots = [jnp.sum(jnp.sum(pk, axis=1, keepdims=True), axis=0, keepdims=True) for pk in p_kds]
        dgcs = [rs - _row_to_col(cs, *c["masks"][0:2]) + gs for rs, cs, gs, c in zip(row_sums, col_sums, g_sums, chains)]
        dgrs = [_col_to_row(dgc, *c["masks"][0:2]) for dgc, c in zip(dgcs, chains)]
        draws = [jnp.sum(jnp.where(jnp.logical_not(c["masks"][3]), dgr, 0.0), axis=1, keepdims=True) + c["dglast"] + kt
                 for dgr, kt, c in zip(dgrs, kd_tots, chains)]
        for c, draw, dbeta in zip(chains, draws, dbetas):
            ch = c["ch"]
            dgates[c["cc"]] = dgates[c["cc"]] + jnp.where(lane == ch, draw, 0.0) + jnp.where(lane == 8 + ch, dbeta, 0.0)
        for cc in range(LOCAL_CHUNKS):
            rows = slice(cc * CHUNK, (cc + 1) * CHUNK)
            for h in range(GDN_HEADS):
                for part in range(3):
                    cols = slice(part * hd + h * GDN_DIM, part * hd + (h + 1) * GDN_DIM)
                    dx_ref[rows, cols] = grads[cc * N_CHAINS + h][part] + grads[cc * N_CHAINS + GDN_HEADS + h][part]
            dg_ref[rows, :] = dgates[cc]

    lc = LOCAL_CHUNKS
    all8 = lambda rows, cols: pl.BlockSpec((lc, N_CHAINS, rows, cols), lambda n: (n, 0, 0, 0))
    own4 = lambda rows, cols: pl.BlockSpec((lc, GDN_HEADS, rows, cols), lambda n: (n, 0, 0, 0))
    row4 = pl.BlockSpec((lc, GDN_HEADS, LANES), lambda n: (n, 0, 0))
    vn_f, vn_b = saved["vn"]
    dvn_f, dvn_b, dw_f, dw_b, dqg_f, dqg_b, dkd_f, dkd_b, dgl_f, dgl_b = scan
    return _grid_call(
        body, "gdn_local_bwd", nc // lc,
        [pl.BlockSpec((lc * CHUNK, 3 * hd), lambda n: (n, 0)), pl.BlockSpec((lc * CHUNK, LANES), lambda n: (n, 0)),
         pl.BlockSpec((lc, 16, CHUNK), lambda n: (n, 0, 0)), pl.BlockSpec((lc * CHUNK, hd), lambda n: (n, 0)),
         all8(CHUNK, CHUNK)] + [own4(CHUNK, GDN_DIM)] * 10 + [row4, row4],
        [pl.BlockSpec((lc * CHUNK, 3 * hd), lambda n: (n, 0)), pl.BlockSpec((lc * CHUNK, LANES), lambda n: (n, 0))],
        [jax.ShapeDtypeStruct((t, 3 * hd), F32), jax.ShapeDtypeStruct((t, LANES), F32)],
        (qkvc, gb, gbt, do, saved["tm"], vn_f, vn_b, dvn_f, dvn_b, dw_f, dw_b, dqg_f, dqg_b, dkd_f, dkd_b, dgl_f, dgl_b),
        exchange=exchange)


def _gdn_post_fwd(of, ob, z, gw, tm):
    t, hd = of.shape

    def body(of_ref, ob_ref, z_ref, w_ref, o_ref):
        for h in range(GDN_HEADS):
            cols = slice(h * GDN_DIM, (h + 1) * GDN_DIM)
            o = of_ref[:, cols] + ob_ref[:, cols]
            zv = z_ref[:, cols]
            o_ref[:, cols] = (o * _rstd(o) * w_ref[...] * (zv * _sigmoid(zv))).astype(BF16)

    row = pl.BlockSpec((tm, hd), lambda i: (i, 0))
    return pl.pallas_call(
        body, name="gdn_post_fwd", grid=(t // tm,),
        in_specs=[row, row, row, _resident((1, GDN_DIM))],
        out_specs=row, out_shape=jax.ShapeDtypeStruct((t, hd), BF16),
        compiler_params=_params(("arbitrary",), VMEM_LIMIT),
    )(of, ob, z, gw)


def _gdn_post_bwd(doa, of, ob, z, gw, tm):
    t, hd = of.shape

    def body(d_ref, of_ref, ob_ref, z_ref, w_ref, do_ref, dz_ref, dw_ref):
        @pl.when(pl.program_id(0) == 0)
        def _():
            dw_ref[...] = jnp.zeros_like(dw_ref)

        dw = jnp.zeros((1, GDN_DIM), F32)
        for h in range(GDN_HEADS):
            cols = slice(h * GDN_DIM, (h + 1) * GDN_DIM)
            o = of_ref[:, cols] + ob_ref[:, cols]
            zv = z_ref[:, cols]
            dv = d_ref[:, cols]
            r = _rstd(o)
            sg = _sigmoid(zv)
            on = o * r * w_ref[...]
            dz_ref[:, cols] = (dv * on * (sg * (1.0 + zv * (1.0 - sg)))).astype(BF16)
            dxr, dwh = _rms_bwd(o, r, w_ref[...], dv * (zv * sg))
            do_ref[:, cols] = dxr
            dw = dw + dwh
        dw_ref[...] += dw

    row = pl.BlockSpec((tm, hd), lambda i: (i, 0))
    return pl.pallas_call(
        body, name="gdn_post_bwd", grid=(t // tm,),
        in_specs=[row, row, row, row, _resident((1, GDN_DIM))],
        out_specs=[row, row, pl.BlockSpec((1, GDN_DIM), lambda i: (0, 0))],
        out_shape=[jax.ShapeDtypeStruct((t, hd), F32), jax.ShapeDtypeStruct((t, hd), BF16),
                   jax.ShapeDtypeStruct((1, GDN_DIM), F32)],
        compiler_params=_params(("arbitrary",), VMEM_LIMIT),
    )(doa, of, ob, z, gw)


SWA_W = SWA_HEADS * SWA_DIM
QBLK = 128
KWIN = QBLK + 2 * RADIUS
WIN_OFFSETS = (0, RADIUS, 2 * RADIUS)


def _t5_bucket(rel):
    nb = REL_BUCKETS // 2
    bucket = (rel > 0).astype(np.int32) * nb
    n = np.abs(rel)
    max_exact = nb // 2
    large = max_exact + (np.log(np.maximum(n, 1) / max_exact)
                         / math.log(REL_MAX_DISTANCE / max_exact) * (nb - max_exact)).astype(np.int32)
    large = np.minimum(large, nb - 1)
    return (bucket + np.where(n < max_exact, n, large)).astype(np.int32)


def _band_tables(dilation):
    a = np.arange(QBLK)
    b = np.arange(KWIN)
    rel = np.stack([b[None, :] - w0 - a[:, None] for w0 in WIN_OFFSETS])
    return _t5_bucket(rel * dilation), np.abs(rel) <= RADIUS


def _bias_table(rel_bias, idx, valid):
    onehot = (jnp.arange(REL_BUCKETS, dtype=jnp.int32)[:, None] == jnp.asarray(idx.reshape(1, -1))).astype(F32)
    tab = jnp.dot(rel_bias.T, onehot, precision=HIGHEST)
    tab = jnp.where(jnp.asarray(valid.reshape(1, -1)), tab, NEG_BIG)
    return tab.reshape((SWA_HEADS,) + idx.shape), onehot


def _head_mean(x2, bd_ref):
    return _dot_hi(x2, bd_ref[...])


VIEW_DILATIONS = tuple(d for _, d in PATTERNS if d > 1)


def _view_spec(tm, d):
    return pl.BlockSpec((tm // d, d * SWA_W), lambda i: (i, 0))


def _view_shape(t, d, dtype):
    return jax.ShapeDtypeStruct((t // d, d * SWA_W), dtype)


N_GROUPS = SWA_W // LANES


def _to_view(src_ref, idx, dst_ref, d, rows):
    for r in range(d):
        for g in range(N_GROUPS):
            cols = slice(r * SWA_W + g * LANES, r * SWA_W + (g + 1) * LANES)
            dst_ref[:, cols] = src_ref[idx, g, pl.ds(r, rows // d, stride=d), :].astype(dst_ref.dtype)


def _from_view(src_ref, dst_ref, idx, d, rows):
    for r in range(d):
        for g in range(N_GROUPS):
            cols = slice(r * SWA_W + g * LANES, r * SWA_W + (g + 1) * LANES)
            dst_ref[idx, g, pl.ds(r, rows // d, stride=d), :] = src_ref[:, cols]


def _swa_prep_fwd(qkvb, qw, kw, bd, tm):
    t = qkvb.shape[0]

    def body(x_ref, qw_ref, kw_ref, bd_ref, *rest):
        outs, sc = rest[:-1], rest[-1]
        for gidx in range(N_GROUPS):
            cols = slice(gidx * LANES, (gidx + 1) * LANES)
            xq = x_ref[:, cols]
            sc[0, gidx] = xq * lax.rsqrt(_head_mean(xq * xq, bd_ref) + EPS) * qw_ref[:, cols] * (SWA_DIM ** -0.5)
            xk = x_ref[:, SWA_W + gidx * LANES:SWA_W + (gidx + 1) * LANES]
            sc[1, gidx] = xk * lax.rsqrt(_head_mean(xk * xk, bd_ref) + EPS) * kw_ref[:, cols]
            sc[2, gidx] = x_ref[:, 2 * SWA_W + gidx * LANES:2 * SWA_W + (gidx + 1) * LANES]
            for i in range(3):
                outs[i][:, cols] = sc[i, gidx].astype(BF16)
        for i in range(3):
            for n, d in enumerate(VIEW_DILATIONS):
                _to_view(sc, i, outs[3 * (n + 1) + i], d, tm)

    return pl.pallas_call(
        body, name="swa_prep_fwd", grid=(t // tm,),
        in_specs=[pl.BlockSpec((tm, 3 * SWA_W), lambda i: (i, 0)), _resident((1, SWA_W)), _resident((1, SWA_W)),
                  _resident((LANES, LANES))],
        out_specs=[_view_spec(tm, d) for d in (1,) + VIEW_DILATIONS for _ in range(3)],
        out_shape=[_view_shape(t, d, BF16) for d in (1,) + VIEW_DILATIONS for _ in range(3)],
        scratch_shapes=[pltpu.VMEM((3, N_GROUPS, tm, LANES), F32)],
        compiler_params=_params(("arbitrary",), VMEM_LIMIT),
    )(qkvb, qw, kw, bd)


def _swa_prep_bwd(qkvb, qw, kw, bd, grads, tm):
    t = qkvb.shape[0]

    def body(x_ref, qw_ref, kw_ref, bd_ref, *rest):
        parts, (dx_ref, dqw_ref, dkw_ref, sc) = rest[:9], rest[9:]
        @pl.when(pl.program_id(0) == 0)
        def _():
            dqw_ref[...] = jnp.zeros_like(dqw_ref)
            dkw_ref[...] = jnp.zeros_like(dkw_ref)

        for i in range(3):
            for n, d in enumerate(VIEW_DILATIONS):
                _from_view(parts[3 * (n + 1) + i], sc, 2 * i + n, d, tm)
        for gidx in range(N_GROUPS):
            cols = slice(gidx * LANES, (gidx + 1) * LANES)
            for i, base, w_ref, dw_ref, scale in ((0, 0, qw_ref, dqw_ref, SWA_DIM ** -0.5),
                                                  (1, SWA_W, kw_ref, dkw_ref, 1.0)):
                xv = x_ref[:, base + gidx * LANES:base + (gidx + 1) * LANES]
                dy = (parts[i][:, cols] + sc[2 * i, gidx] + sc[2 * i + 1, gidx]) * scale
                r = lax.rsqrt(_head_mean(xv * xv, bd_ref) + EPS)
                xhat = xv * r
                dxh = dy * w_ref[:, cols]
                dx = r * (dxh - xhat * _head_mean(dxh * xhat, bd_ref))
                dx_ref[:, base + gidx * LANES:base + (gidx + 1) * LANES] = dx.astype(BF16)
                dw_ref[:, cols] += jnp.sum(dy * xhat, axis=0, keepdims=True)
            dx_ref[:, 2 * SWA_W + gidx * LANES:2 * SWA_W + (gidx + 1) * LANES] = (
                parts[2][:, cols] + sc[4, gidx] + sc[5, gidx]).astype(BF16)

    wrow = pl.BlockSpec((1, SWA_W), lambda i: (0, 0))
    return pl.pallas_call(
        body, name="swa_prep_bwd", grid=(t // tm,),
        in_specs=[pl.BlockSpec((tm, 3 * SWA_W), lambda i: (i, 0)), _resident((1, SWA_W)), _resident((1, SWA_W)),
                  _resident((LANES, LANES))] + [_view_spec(tm, d) for d in (1,) + VIEW_DILATIONS for _ in range(3)],
        out_specs=[pl.BlockSpec((tm, 3 * SWA_W), lambda i: (i, 0)), wrow, wrow],
        out_shape=[jax.ShapeDtypeStruct((t, 3 * SWA_W), BF16), jax.ShapeDtypeStruct((1, SWA_W), F32),
                   jax.ShapeDtypeStruct((1, SWA_W), F32)],
        scratch_shapes=[pltpu.VMEM((6, N_GROUPS, tm, LANES), F32)],
        compiler_params=_params(("arbitrary",), VMEM_LIMIT),
    )(qkvb, qw, kw, bd, *grads)


def _aligned(v, m):
    return v if isinstance(v, int) else pl.multiple_of(v, m)


BAND_GROUP = 2


def _band_loop(nsub, length, step):
    step([(0, 0)], 0)
    if nsub > 2:
        assert (nsub - 2) % BAND_GROUP == 0

        def inner(i, carry):
            s0 = 1 + i * BAND_GROUP
            step([(s0 + e, pl.multiple_of((s0 + e) * QBLK - RADIUS, RADIUS)) for e in range(BAND_GROUP)], 1)
            return carry
        lax.fori_loop(0, (nsub - 2) // BAND_GROUP, inner, 0)
    step([(nsub - 1, length - KWIN)], 2)


def _head_select(lane, a0, a1):
    return jnp.where(lane < SWA_DIM, a0, a1)


def _swa_fwd(qv, kv, vv, bias, dilation, name):
    length = qv.shape[0]
    nsub = length // QBLK
    assert nsub >= 2 and length % QBLK == 0

    def body(q_ref, k_ref, v_ref, b_ref, o_ref, l_ref):
        lane = lax.broadcasted_iota(jnp.int32, (QBLK, LANES), 1)

        def step(blocks, var):
            items = []
            for s, ws in blocks:
                rows = pl.ds(_aligned(s * QBLK, QBLK), QBLK)
                q, kk, vw = q_ref[rows, :], k_ref[pl.ds(ws, KWIN), :], v_ref[pl.ds(ws, KWIN), :]
                for hh in range(2):
                    items.append((hh, jnp.where((lane < SWA_DIM) == (hh == 0), q, jnp.zeros_like(q)), kk, vw))
            lgs = [_dot_nt(qh, kk) + b_ref[hh, var] for hh, qh, kk, _ in items]
            ms = [jnp.max(lg, axis=-1, keepdims=True) for lg in lgs]
            ps = [jnp.exp(lg - m) for lg, m in zip(lgs, ms)]
            dens = [jnp.sum(p, axis=-1, keepdims=True) for p in ps]
            pvs = [_dot(p, it[3]) for p, it in zip(ps, items)]
            for n, (s, _) in enumerate(blocks):
                rows = pl.ds(_aligned(s * QBLK, QBLK), QBLK)
                o0, o1 = (pvs[2 * n + hh] / dens[2 * n + hh] for hh in range(2))
                l0, l1 = (ms[2 * n + hh] + jnp.log(dens[2 * n + hh]) for hh in range(2))
                o_ref[rows, :] = _head_select(lane, o0, o1)
                l_ref[rows, :] = _head_select(lane, l0, l1)

        _band_loop(nsub, length, step)

    blk = pl.BlockSpec((length, LANES), lambda hp, r: (0, r * (SWA_W // LANES) + hp))
    shp = jax.ShapeDtypeStruct(qv.shape, F32)
    return pl.pallas_call(
        body, name=name, grid=(SWA_W // LANES, dilation),
        in_specs=[blk, blk, blk, pl.BlockSpec((2, 3, QBLK, KWIN), lambda hp, r: (hp, 0, 0, 0))],
        out_specs=[blk, blk], out_shape=[shp, shp],
        compiler_params=_params(("arbitrary", "arbitrary"), VMEM_LIMIT),
    )(qv, kv, vv, bias)


def _swa_combine(os_, ls_, tm):
    t = os_[0].shape[0]

    def body(o0, o1, o2, l0, l1, l2, o_ref, ob_ref, la_ref, lb_ref, lc_ref, sc):
        for n, d in enumerate(VIEW_DILATIONS):
            _from_view((o1, o2)[n], sc, n, d, tm)
            _from_view((l1, l2)[n], sc, 2 + n, d, tm)
        for g in range(N_GROUPS):
            cols = slice(g * LANES, (g + 1) * LANES)
            la, lb, lc = l0[:, cols], sc[2, g], sc[3, g]
            m = jnp.maximum(jnp.maximum(la, lb), lc)
            tot = m + jnp.log(jnp.exp(la - m) + jnp.exp(lb - m) + jnp.exp(lc - m))
            o = jnp.exp(la - tot) * o0[:, cols] + jnp.exp(lb - tot) * sc[0, g] + jnp.exp(lc - tot) * sc[1, g]
            o_ref[:, cols] = o
            ob_ref[:, cols] = o.astype(BF16)
            la_ref[:, cols] = tot
            sc[4, g] = tot
        for n, d in enumerate(VIEW_DILATIONS):
            _to_view(sc, 4, (lb_ref, lc_ref)[n], d, tm)

    specs = [_view_spec(tm, d) for d in (1,) + VIEW_DILATIONS]
    return pl.pallas_call(
        body, name="swa_combine", grid=(t // tm,), in_specs=specs + specs, out_specs=[specs[0], specs[0]] + specs,
        out_shape=[jax.ShapeDtypeStruct((t, SWA_W), F32), jax.ShapeDtypeStruct((t, SWA_W), BF16)]
                  + [_view_shape(t, d, F32) for d in (1,) + VIEW_DILATIONS],
        scratch_shapes=[pltpu.VMEM((5, N_GROUPS, tm, LANES), F32)],
        compiler_params=_params(("arbitrary",), VMEM_LIMIT),
    )(*os_, *ls_)


def _swa_bwd_prep(do, o, bd, tm):
    t = do.shape[0]

    def body(d_ref, o_ref, bd_ref, dd1, dd4, dd16, db1, db4, db16, sc):
        for gidx in range(N_GROUPS):
            cols = slice(gidx * LANES, (gidx + 1) * LANES)
            dv = d_ref[:, cols]
            dd = _head_mean(dv * o_ref[:, cols], bd_ref) * float(SWA_DIM)
            sc[0, gidx] = dd
            sc[1, gidx] = dv
            dd1[:, cols] = dd
            db1[:, cols] = dv.astype(BF16)
        for n, d in enumerate(VIEW_DILATIONS):
            _to_view(sc, 0, (dd4, dd16)[n], d, tm)
            _to_view(sc, 1, (db4, db16)[n], d, tm)

    specs = [_view_spec(tm, d) for d in (1,) + VIEW_DILATIONS]
    return pl.pallas_call(
        body, name="swa_bwd_prep", grid=(t // tm,), in_specs=[specs[0], specs[0], _resident((LANES, LANES))],
        out_specs=specs + specs,
        out_shape=[_view_shape(t, d, F32) for d in (1,) + VIEW_DILATIONS]
                  + [_view_shape(t, d, BF16) for d in (1,) + VIEW_DILATIONS],
        scratch_shapes=[pltpu.VMEM((2, N_GROUPS, tm, LANES), F32)],
        compiler_params=_params(("arbitrary",), VMEM_LIMIT),
    )(do, o, bd)


def _swa_bwd(qv, kv, vv, dov, lv, ddv, bias_a, dilation, name):
    length = qv.shape[0]
    nsub = length // QBLK
    single = pl.Buffered(1) if dilation == 1 else None

    def body(q_ref, k_ref, v_ref, do_ref, l_ref, dd_ref, ba_ref, dq_ref, dk_ref, dv_ref, db_ref):
        @pl.when(pl.program_id(1) == 0)
        def _():
            db_ref[...] = jnp.zeros_like(db_ref)

        lane = lax.broadcasted_iota(jnp.int32, (QBLK, LANES), 1)
        lanew = lax.broadcasted_iota(jnp.int32, (KWIN, LANES), 1)

        def step(blocks, var):
            items = []
            for s, ws in blocks:
                rows = pl.ds(_aligned(s * QBLK, QBLK), QBLK)
                win = pl.ds(ws, KWIN)
                q, dov_ = q_ref[rows, :], do_ref[rows, :]
                kk, vw = k_ref[win, :], v_ref[win, :]
                lse, dd = l_ref[rows, :], dd_ref[rows, :]
                for hh in range(2):
                    mine = (lane < SWA_DIM) == (hh == 0)
                    col = slice(hh * SWA_DIM, hh * SWA_DIM + 1)
                    items.append((hh, jnp.where(mine, q, jnp.zeros_like(q)), jnp.where(mine, dov_, jnp.zeros_like(dov_)),
                                  kk, vw, lse[:, col], dd[:, col], q, dov_))
            lgs = [_dot_nt(it[1], it[3]) + ba_ref[it[0], var] for it in items]
            dps = [_dot_nt(it[2], it[4]) for it in items]
            ps = [jnp.exp(lg - it[5]) for lg, it in zip(lgs, items)]
            dss = [p * (dp - it[6]) for p, dp, it in zip(ps, dps, items)]
            dqs = [_dot(ds, it[3]) for ds, it in zip(dss, items)]
            dks = [_dot_tn(ds, it[7]) for ds, it in zip(dss, items)]
            dvs = [_dot_tn(p, it[8]) for p, it in zip(ps, items)]
            for n, (s, ws) in enumerate(blocks):
                rows = pl.ds(_aligned(s * QBLK, QBLK), QBLK)
                win = pl.ds(ws, KWIN)
                dq_ref[rows, :] = _head_select(lane, dqs[2 * n], dqs[2 * n + 1])
                dk_ref[win, :] += _head_select(lanew, dks[2 * n], dks[2 * n + 1])
                dv_ref[win, :] += _head_select(lanew, dvs[2 * n], dvs[2 * n + 1])
            for hh in range(2):
                tot = dss[hh]
                for n in range(1, len(blocks)):
                    tot = tot + dss[2 * n + hh]
                db_ref[hh, var] += tot

        dk_ref[...] = jnp.zeros_like(dk_ref)
        dv_ref[...] = jnp.zeros_like(dv_ref)
        _band_loop(nsub, length, step)

    imap = lambda hp, r: (0, r * (SWA_W // LANES) + hp)
    blk_in = pl.BlockSpec((length, LANES), imap, pipeline_mode=single)
    blk_out = pl.BlockSpec((length, LANES), imap)
    shp = jax.ShapeDtypeStruct(qv.shape, F32)
    return pl.pallas_call(
        body, name=name, grid=(SWA_W // LANES, dilation),
        in_specs=[blk_in] * 6 + [pl.BlockSpec((2, 3, QBLK, KWIN), lambda hp, r: (hp, 0, 0, 0))],
        out_specs=[blk_out, blk_out, blk_out, pl.BlockSpec((2, 3, QBLK, KWIN), lambda hp, r: (hp, 0, 0, 0))],
        out_shape=[shp, shp, shp, jax.ShapeDtypeStruct((SWA_HEADS, 3, QBLK, KWIN), F32)],
        compiler_params=_params(("arbitrary", "arbitrary"), VMEM_LIMIT),
    )(qv, kv, vv, dov, lv, ddv, bias_a)


def _bias_grad(ds2, onehot, tk):
    n = ds2.shape[1]
    nk = n // tk

    def body(a_ref, b_ref, o_ref):
        @pl.when(pl.program_id(0) == 0)
        def _():
            o_ref[...] = jnp.zeros_like(o_ref)

        o_ref[...] += lax.dot_general(a_ref[...], b_ref[...], (((1,), (1,)), ((), ())), precision=HIGHEST,
                                      preferred_element_type=F32)

    return pl.pallas_call(
        body, name="bias_grad", grid=(nk,),
        in_specs=[pl.BlockSpec((SWA_HEADS, tk), lambda k: (0, k)), pl.BlockSpec((REL_BUCKETS, tk), lambda k: (0, k))],
        out_specs=pl.BlockSpec((SWA_HEADS, REL_BUCKETS), lambda k: (0, 0)),
        out_shape=jax.ShapeDtypeStruct((SWA_HEADS, REL_BUCKETS), F32),
        compiler_params=_params(("arbitrary",), VMEM_LIMIT),
    )(ds2, onehot)


def _swa_branch_fwd(qkvb, qw_t, kw_t, rel_bias, bd, tm):
    qkv = _swa_prep_fwd(qkvb, qw_t, kw_t, bd, tm)
    os_, ls_, tabs = [], [], []
    for n, (_, d) in enumerate(PATTERNS):
        bias, onehot = _bias_table(rel_bias, *_band_tables(d))
        o_p, l_p = _swa_fwd(*qkv[3 * n:3 * n + 3], bias, d, f"swa_fwd_d{d}")
        os_.append(o_p)
        ls_.append(l_p)
        tabs.append((bias, onehot))
    o, o16, *lses = _swa_combine(os_, ls_, tm)
    return o, o16, (qkv, lses, tabs)


def _swa_branch_bwd(do, o, saved, qkvb, qw_t, kw_t, bd, tm):
    qkv, lses, tabs = saved
    prep = _swa_bwd_prep(do, o, bd, tm)
    grads, dss, ohs = [], [], []
    for n, ((_, d), (bias, onehot)) in enumerate(zip(PATTERNS, tabs)):
        dq, dk, dv, ds = _swa_bwd(*qkv[3 * n:3 * n + 3], prep[3 + n], lses[n], prep[n], bias, d, f"swa_bwd_d{d}")
        grads += [dq, dk, dv]
        dss.append(ds.reshape(SWA_HEADS, -1))
        ohs.append(onehot)
    dqkvb, dqw, dkw = _swa_prep_bwd(qkvb, qw_t, kw_t, bd, grads, tm)
    dbias = _bias_grad(jnp.concatenate(dss, axis=1), jnp.concatenate(ohs, axis=1), 8192)
    fold = lambda w: jnp.sum(w.reshape(SWA_HEADS, SWA_DIM), axis=0)
    return dqkvb, fold(dqw), fold(dkw), dbias.T


def _mesh_pos():
    return lax.axis_index("x"), lax.axis_index("y"), lax.axis_index("c")


def _other_chips(x, y):
    return [(1 - x, y), (x, 1 - y), (1 - x, 1 - y)]


def _remote(src, dst, send_sem, recv_sem, device):
    return pltpu.make_async_remote_copy(src_ref=src, dst_ref=dst, send_sem=send_sem, recv_sem=recv_sem,
                                        device_id=device, device_id_type=MESH)


def _all_gather(xs):
    n = len(xs)

    def body(*refs):
        ins, outs = refs[:n], refs[n:2 * n]
        send_sems, recv_sems = refs[2 * n:]
        x, y, c = _mesh_pos()
        me = 2 * x + y
        chips = _other_chips(x, y)
        halves = []
        sends = []
        for a in range(n):
            h = ins[a].shape[0] // 2
            mine, other = pl.ds(c * h, h), pl.ds((1 - c) * h, h)
            halves.append((mine, other))
            for j, chip in enumerate(chips):
                cp = _remote(ins[a].at[mine], outs[a].at[me, mine], send_sems.at[a, j], recv_sems.at[a, j], (*chip, c))
                cp.start()
                sends.append(cp)
        for a in range(n):
            mine, _ = halves[a]
            for j, chip in enumerate(chips):
                src = 2 * chip[0] + chip[1]
                landed = outs[a].at[src, mine]
                _remote(landed, landed, send_sems.at[a, j], recv_sems.at[a, j], (x, y, c)).wait_recv()
                fwd = _remote(landed, landed, send_sems.at[a, 3 + j], recv_sems.at[a, 3 + j], (x, y, 1 - c))
                fwd.start()
                sends.append(fwd)
        for a in range(n):
            _, other = halves[a]
            for j, chip in enumerate(chips):
                src = 2 * chip[0] + chip[1]
                landed = outs[a].at[src, other]
                _remote(landed, landed, send_sems.at[a, 3 + j], recv_sems.at[a, 3 + j], (x, y, c)).wait_recv()
        for cp in sends:
            cp.wait_send()

    outs = pl.pallas_call(
        body, name="all_gather_weights",
        in_specs=[ANY] * n, out_specs=[ANY] * n,
        out_shape=[jax.ShapeDtypeStruct((N_SHARDS,) + a.shape, a.dtype) for a in xs],
        scratch_shapes=[pltpu.SemaphoreType.DMA((n, 6)), pltpu.SemaphoreType.DMA((n, 6))],
    )(*xs)
    me = 2 * lax.axis_index("x") + lax.axis_index("y")
    return [lax.dynamic_update_slice_in_dim(o, a[None], me, 0) for o, a in zip(outs, xs)]


def _rs_pair(gs):
    n = len(gs)

    def body(*refs):
        ins, lands = refs[:n], refs[n:2 * n]
        send_sems, recv_sems = refs[2 * n:]
        x, y, c = _mesh_pos()
        cps = []
        for a in range(n):
            h = ins[a].shape[1] // 2
            cp = _remote(ins[a].at[:, pl.ds((1 - c) * h, h), :], lands[a], send_sems.at[a], recv_sems.at[a],
                         (x, y, 1 - c))
            cp.start()
            cps.append(cp)
        for cp in cps:
            cp.wait()

    half = [jax.ShapeDtypeStruct((N_SHARDS, g.shape[1] // 2, g.shape[2]), g.dtype) for g in gs]
    lands = pl.pallas_call(
        body, name="rs_pair", in_specs=[ANY] * n, out_specs=[ANY] * n, out_shape=half,
        scratch_shapes=[pltpu.SemaphoreType.DMA((n,)), pltpu.SemaphoreType.DMA((n,))],
    )(*gs)
    c = lax.axis_index("c")
    owns = [lax.dynamic_slice_in_dim(g, c * (g.shape[1] // 2), g.shape[1] // 2, 1) for g in gs]
    return owns + list(lands)


def _rs_chips(ss):
    n = len(ss)

    def body(*refs):
        ins, outs = refs[:n], refs[n:2 * n]
        send_sems, recv_sems = refs[2 * n:]
        x, y, c = _mesh_pos()
        me = 2 * x + y
        chips = _other_chips(x, y)
        cps = []
        for a in range(n):
            for j, chip in enumerate(chips):
                dst_chip = 2 * chip[0] + chip[1]
                cp = _remote(ins[a].at[dst_chip], outs[a].at[me], send_sems.at[a, j], recv_sems.at[a, j], (*chip, c))
                cp.start()
                cps.append(cp)
        for a in range(n):
            for j, chip in enumerate(chips):
                src = 2 * chip[0] + chip[1]
                _remote(outs[a].at[src], outs[a].at[src], send_sems.at[a, j], recv_sems.at[a, j], (x, y, c)).wait_recv()
        for cp in cps:
            cp.wait_send()

    outs = pl.pallas_call(
        body, name="rs_chips", in_specs=[ANY] * n, out_specs=[ANY] * n,
        out_shape=[jax.ShapeDtypeStruct(s.shape, s.dtype) for s in ss],
        scratch_shapes=[pltpu.SemaphoreType.DMA((n, 3)), pltpu.SemaphoreType.DMA((n, 3))],
    )(*ss)
    me = 2 * lax.axis_index("x") + lax.axis_index("y")
    return [lax.dynamic_update_slice_in_dim(o, lax.dynamic_slice_in_dim(s, me, 1, 0), me, 0) for o, s in zip(outs, ss)]


def _rs_join(fs):
    n = len(fs)

    def body(*refs):
        ins, outs = refs[:n], refs[n:2 * n]
        send_sems, recv_sems = refs[2 * n:]
        x, y, c = _mesh_pos()
        cps = []
        for a in range(n):
            h = ins[a].shape[0]
            cp = _remote(ins[a], outs[a].at[pl.ds(c * h, h)], send_sems.at[a], recv_sems.at[a], (x, y, 1 - c))
            cp.start()
            cps.append(cp)
        for cp in cps:
            cp.wait()

    outs = pl.pallas_call(
        body, name="rs_join", in_specs=[ANY] * n, out_specs=[ANY] * n,
        out_shape=[jax.ShapeDtypeStruct((2 * f.shape[0], f.shape[1]), f.dtype) for f in fs],
        scratch_shapes=[pltpu.SemaphoreType.DMA((n,)), pltpu.SemaphoreType.DMA((n,))],
    )(*fs)
    c = lax.axis_index("c")
    return [lax.dynamic_update_slice_in_dim(o, f, c * f.shape[0], 0) for o, f in zip(outs, fs)]


def _gather_exchange(xs):
    def start(cin, cout, send_sems, recv_sems):
        x, y, c = _mesh_pos()
        me = 2 * x + y
        for a, (src, dst) in enumerate(zip(cin, cout)):
            h = src.shape[0] // 2
            mine = pl.ds(c * h, h)
            for j, chip in enumerate(_other_chips(x, y)):
                _remote(src.at[mine], dst.at[me, mine], send_sems.at[a, j], recv_sems.at[a, j], (*chip, c)).start()

    def finish(cin, cout, send_sems, recv_sems):
        x, y, c = _mesh_pos()
        for a, dst in enumerate(cout):
            h = dst.shape[1] // 2
            for j, chip in enumerate(_other_chips(x, y)):
                landed = dst.at[2 * chip[0] + chip[1], pl.ds(c * h, h)]
                _remote(landed, landed, send_sems.at[a, j], recv_sems.at[a, j], (x, y, c)).wait()

    return _Exchange(tuple(xs), tuple(jax.ShapeDtypeStruct((N_SHARDS,) + a.shape, a.dtype) for a in xs), start, finish)


def _gather_forward(gs, xs):
    n = len(gs)

    def body(*refs):
        outs = refs[n:2 * n]
        send_sems, recv_sems = refs[2 * n:]
        x, y, c = _mesh_pos()
        chips = _other_chips(x, y)
        cps = []
        for a in range(n):
            h = outs[a].shape[1] // 2
            for j, chip in enumerate(chips):
                landed = outs[a].at[2 * chip[0] + chip[1], pl.ds(c * h, h)]
                cp = _remote(landed, landed, send_sems.at[a, j], recv_sems.at[a, j], (x, y, 1 - c))
                cp.start()
                cps.append(cp)
        for a in range(n):
            h = outs[a].shape[1] // 2
            for j, chip in enumerate(chips):
                other = outs[a].at[2 * chip[0] + chip[1], pl.ds((1 - c) * h, h)]
                _remote(other, other, send_sems.at[a, j], recv_sems.at[a, j], (x, y, c)).wait_recv()
        for cp in cps:
            cp.wait_send()

    outs = pl.pallas_call(
        body, name="gather_forward", in_specs=[ANY] * n, out_specs=[ANY] * n,
        out_shape=[jax.ShapeDtypeStruct(g.shape, g.dtype) for g in gs],
        input_output_aliases={i: i for i in range(n)},
        scratch_shapes=[pltpu.SemaphoreType.DMA((n, 3)), pltpu.SemaphoreType.DMA((n, 3))],
    )(*gs)
    me = 2 * lax.axis_index("x") + lax.axis_index("y")
    return [lax.dynamic_update_slice_in_dim(o, a[None], me, 0) for o, a in zip(outs, xs)]


def _scatter_exchange(ss):
    def start(cin, cout, send_sems, recv_sems):
        x, y, c = _mesh_pos()
        me = 2 * x + y
        for a, (src, dst) in enumerate(zip(cin, cout)):
            for j, chip in enumerate(_other_chips(x, y)):
                _remote(src.at[2 * chip[0] + chip[1]], dst.at[me], send_sems.at[a, j], recv_sems.at[a, j],
                        (*chip, c)).start()

    def finish(cin, cout, send_sems, recv_sems):
        x, y, c = _mesh_pos()
        for a, dst in enumerate(cout):
            for j, chip in enumerate(_other_chips(x, y)):
                slot = dst.at[2 * chip[0] + chip[1]]
                _remote(slot, slot, send_sems.at[a, j], recv_sems.at[a, j], (x, y, c)).wait()

    return _Exchange(tuple(ss), tuple(jax.ShapeDtypeStruct(s.shape, s.dtype) for s in ss), start, finish)


def _own_slots(slots, ss):
    me = 2 * lax.axis_index("x") + lax.axis_index("y")
    return [lax.dynamic_update_slice_in_dim(o, lax.dynamic_slice_in_dim(s, me, 1, 0), me, 0) for o, s in zip(slots, ss)]


def _add_pair(a, b, name):
    nj, h, c = a.shape

    def body(a_ref, b_ref, o_ref):
        o_ref[...] = (a_ref[...].astype(F32) + b_ref[...].astype(F32)).astype(BF16)

    blk = pl.BlockSpec((1, h, c), lambda j: (j, 0, 0))
    return pl.pallas_call(body, name=name, grid=(nj,), in_specs=[blk, blk], out_specs=blk,
                          out_shape=jax.ShapeDtypeStruct(a.shape, BF16),
                          compiler_params=_params(("arbitrary",), VMEM_LIMIT))(a, b)


def _sum_slots(l2, name):
    nj, h, c = l2.shape
    th = h // 2 if h % 32 == 0 else h

    def body(i_ref, o_ref):
        acc = i_ref[0].astype(F32)
        for s in range(1, nj):
            acc = acc + i_ref[s].astype(F32)
        o_ref[...] = acc

    return pl.pallas_call(body, name=name, grid=(h // th,),
                          in_specs=[pl.BlockSpec((nj, th, c), lambda i: (0, i, 0))],
                          out_specs=pl.BlockSpec((th, c), lambda i: (i, 0)),
                          out_shape=jax.ShapeDtypeStruct((h, c), F32),
                          compiler_params=_params(("arbitrary",), VMEM_LIMIT))(l2)


def _all_reduce_small(p):
    r = p.shape[0]

    def body(p_ref, o_ref, buf, send_sems, recv_sems):
        x, y, c = _mesh_pos()
        me = 4 * x + 2 * y + c
        buf[me] = p_ref[...]
        cps = []
        k = 0
        for fx in range(2):
            for fy in range(2):
                for fc in range(2):
                    if fx + fy + fc == 0:
                        continue
                    peer = (1 - x if fx else x, 1 - y if fy else y, 1 - c if fc else c)
                    peer_id = 4 * peer[0] + 2 * peer[1] + peer[2]
                    cp = _remote(p_ref, buf.at[me], send_sems.at[k], recv_sems.at[k], peer)
                    cp.start()
                    cps.append((cp, peer_id, k))
                    k += 1
        for cp, peer_id, k in cps:
            _remote(p_ref, buf.at[peer_id], send_sems.at[k], recv_sems.at[k], (x, y, c)).wait_recv()
        for cp, _, _ in cps:
            cp.wait_send()
        acc = buf[0]
        for s in range(1, 8):
            acc = acc + buf[s]
        o_ref[...] = acc

    vm = pl.BlockSpec(memory_space=pltpu.VMEM)
    return pl.pallas_call(
        body, name="all_reduce_small", in_specs=[vm], out_specs=vm,
        out_shape=jax.ShapeDtypeStruct(p.shape, F32),
        scratch_shapes=[pltpu.VMEM((8, r, LANES), F32), pltpu.SemaphoreType.DMA((7,)), pltpu.SemaphoreType.DMA((7,))],
    )(p)


def _adamw(w, g, m, v, name):
    r, c = w.shape
    row_tiles = [d for d in range(8, min(r, 256) + 1, 8) if r % d == 0]
    tr, tc = (max(row_tiles), c) if row_tiles else (r, 256 if c % 256 == 0 else c)
    c1 = 1.0 / (1.0 - ADAM_B1 ** ADAM_STEP)
    c2 = 1.0 / (1.0 - ADAM_B2 ** ADAM_STEP)

    def body(w_ref, g_ref, m_ref, v_ref, d_ref, nm_ref, nv_ref):
        gv = g_ref[...]
        nm = ADAM_B1 * m_ref[...] + (1.0 - ADAM_B1) * gv
        nv = ADAM_B2 * v_ref[...] + (1.0 - ADAM_B2) * (gv * gv)
        d_ref[...] = -ADAM_LR * ((nm * c1) / (jnp.sqrt(nv * c2) + ADAM_EPS) + ADAM_WD * w_ref[...])
        nm_ref[...] = nm
        nv_ref[...] = nv

    blk = pl.BlockSpec((tr, tc), lambda i, j: (i, j))
    shp = jax.ShapeDtypeStruct((r, c), F32)
    return pl.pallas_call(body, name=name, grid=(r // tr, c // tc), in_specs=[blk] * 4, out_specs=[blk] * 3,
                          out_shape=[shp, shp, shp],
                          compiler_params=_params(("arbitrary", "arbitrary"), VMEM_LIMIT))(w, g, m, v)


PACK_UNIT = 8 * LANES


def _pack(arrs):
    parts = []
    for a in arrs:
        f = a.reshape(-1).astype(F32)
        parts.append(jnp.pad(f, (0, (-f.shape[0]) % PACK_UNIT)).reshape(-1, LANES))
    return jnp.concatenate(parts, axis=0)


def _unpack(m, shapes):
    outs, row = [], 0
    for s in shapes:
        n = int(np.prod(s))
        rows = -(-n // PACK_UNIT) * 8
        outs.append(m[row:row + rows].reshape(-1)[:n].reshape(s))
        row += rows
    return outs


WEIGHTS = ["ffn1_norm", "ffn1_w_gate", "ffn1_w_up", "ffn1_w_down", "mix_norm", "w_in", "conv_w", "a_log", "dt_bias",
           "gdn_norm_w", "q_norm_w", "k_norm_w", "rel_bias", "w_out", "ffn2_norm", "ffn2_w_gate", "ffn2_w_up",
           "ffn2_w_down", "final_norm"]
BIG = ["ffn1_w_gate", "ffn1_w_up", "ffn1_w_down", "w_in", "w_out", "ffn2_w_gate", "ffn2_w_up", "ffn2_w_down"]
SMALL = [n for n in WEIGHTS if n not in BIG]
COL_SHARDED = ["ffn1_w_gate", "ffn1_w_up", "ffn2_w_gate", "ffn2_w_up"]
N_IN_COLS = 3600
TM = 256
TE = 512
TK = 2048


def kernel(x, ffn1_norm, ffn1_w_gate, ffn1_w_up, ffn1_w_down, mix_norm, w_in, conv_w, a_log, dt_bias, gdn_norm_w, q_norm_w, k_norm_w, rel_bias, w_out, ffn2_norm, ffn2_w_gate, ffn2_w_up, ffn2_w_down, final_norm, loss_target, m_ffn1_norm, m_ffn1_w_gate, m_ffn1_w_up, m_ffn1_w_down, m_mix_norm, m_w_in, m_conv_w, m_a_log, m_dt_bias, m_gdn_norm_w, m_q_norm_w, m_k_norm_w, m_rel_bias, m_w_out, m_ffn2_norm, m_ffn2_w_gate, m_ffn2_w_up, m_ffn2_w_down, m_final_norm, v_ffn1_norm, v_ffn1_w_gate, v_ffn1_w_up, v_ffn1_w_down, v_mix_norm, v_w_in, v_conv_w, v_a_log, v_dt_bias, v_gdn_norm_w, v_q_norm_w, v_k_norm_w, v_rel_bias, v_w_out, v_ffn2_norm, v_ffn2_w_gate, v_ffn2_w_up, v_ffn2_w_down, v_final_norm):
    p = dict(locals())
    xs, target = x[0], loss_target[0]
    t, d = xs.shape
    nc = t // CHUNK
    tk = min(TK, t)
    me = 2 * lax.axis_index("x") + lax.axis_index("y")

    first = ["ffn1_w_gate", "ffn1_w_up", "ffn1_w_down"]
    later = [n for n in BIG if n not in first] + ["conv_w"]
    local = lambda n, a: a[0].T if n in COL_SHARDED else a[0]
    shards = {n: local(n, p[n]).astype(BF16) for n in BIG}
    shards["conv_w"] = conv_w[0]
    gw = dict(zip(first, _all_gather([shards[n] for n in first])))
    f1 = (gw["ffn1_w_gate"], gw["ffn1_w_up"], gw["ffn1_w_down"])
    (x1, xn1, g1, u1), landed = _ffn_fwd(xs, ffn1_norm, *f1, TM, "ffn1_fwd",
                                         exchange=_gather_exchange([shards[n] for n in later]))
    gw.update(zip(later, _gather_forward(landed, [shards[n] for n in later])))
    w_in_t = jnp.transpose(gw["w_in"], (0, 2, 1)).reshape(N_IN_COLS, d)
    wp = jnp.concatenate([w_in_t[:2048], jnp.pad(w_in_t[2048:2064], ((0, LANES - 16), (0, 0))), w_in_t[2064:]], axis=0)
    w_out_full = gw["w_out"].reshape(d, d)
    conv_rows = conv_w.shape[1]
    cw = jnp.pad(gw["conv_w"].reshape(N_SHARDS * conv_rows, CONV_TAPS).T, ((0, 8 - CONV_TAPS), (0, 0)))
    gp = jnp.pad(jnp.stack([a_log.reshape(8), dt_bias.reshape(8)]), ((0, 6), (0, LANES - 8)))
    gdn_w = gdn_norm_w.reshape(1, GDN_DIM)
    qw_t = jnp.tile(q_norm_w.reshape(1, SWA_DIM), (1, SWA_HEADS))
    kw_t = jnp.tile(k_norm_w.reshape(1, SWA_DIM), (1, SWA_HEADS))
    bd = jnp.asarray(np.kron(np.eye(2), np.full((SWA_DIM, SWA_DIM), 1.0 / SWA_DIM)), F32)
    f2 = (gw["ffn2_w_gate"], gw["ffn2_w_up"], gw["ffn2_w_down"])

    hn, qkva, z, ab, qkvb = _mix_in_fwd(x1, mix_norm, wp, TM)
    qkvc, gb = _gdn_prep_fwd(qkva, cw, ab, gp, TM)
    gbt = jnp.transpose(gb[:, :16].reshape(nc, CHUNK, 16), (0, 2, 1))
    o_f, o_b, gdn_saved = _gdn_fwd(qkvc, gb, gbt)
    oa = _gdn_post_fwd(o_f, o_b, z, gdn_w, TE)
    o_swa, o_swa16, swa_saved = _swa_branch_fwd(qkvb, qw_t, kw_t, rel_bias, bd, TE)
    x2 = _mix_out_fwd(x1, oa, o_swa, w_out_full, TM)
    (x3, xn2, g2, u2), _ = _ffn_fwd(x2, ffn2_norm, *f2, TM, "ffn2_fwd")
    dx3, loss_part, d_final = _final_loss(x3, final_norm, target, TE)

    def pair_sums(partials, tag):
        pair = _rs_pair(partials)
        k = len(partials)
        return [_add_pair(pair[i], pair[k + i], f"rs_add_{tag}{i}") for i in range(k)]

    (dx2, dyh2, dg2, du2, h2, d_nw2), _ = _ffn_bwd_dx(dx3, x2, ffn2_norm, g2, u2, *f2, TM, "ffn2_bwd_dx")
    dwg2 = _matmul_tn(dg2, xn2, tk, "ffn2_dwg")
    dwu2 = _matmul_tn(du2, xn2, tk, "ffn2_dwu")
    dwd2 = _matmul_tn(h2, dyh2, tk, "ffn2_dwd")
    sums_f2 = pair_sums([dwg2, dwu2, dwd2], "a")
    doa, dob, dx2b = _mix_out_bwd(dx2, w_out_full, TM)
    dwo = jnp.concatenate([_matmul_tn(oa, dx2b, tk, "w_out_dw_a")[0], _matmul_tn(o_swa16, dx2b, tk, "w_out_dw_b")[0]],
                          axis=0).reshape(N_SHARDS, d // N_SHARDS, d)
    do_g, dz, d_gdnw = _gdn_post_bwd(doa, o_f, o_b, z, gdn_w, TE)
    (dqkvc, dgates), slots_f2 = _gdn_bwd(qkvc, gb, gbt, do_g, gdn_saved, exchange=_scatter_exchange(sums_f2))
    dqkva, dab, dcw, dgp = _gdn_prep_bwd(qkva, cw, ab, gp, dqkvc, dgates, TM)
    dqkvb, d_qw, d_kw, d_rel = _swa_branch_bwd(dob, o_swa, swa_saved, qkvb, qw_t, kw_t, bd, TE)
    dpieces = (dqkva, dz, dab, dqkvb)
    dx1, d_mixnw = _mix_in_bwd_dx(dx2, x1, mix_norm, dpieces, wp, TM)
    dwp = [_matmul_tn(dp, hn, tk, f"w_in_dw_{i}")[0] for i, dp in enumerate(dpieces)]
    dw_in = jnp.concatenate([dwp[0], dwp[1], dwp[2][:16], dwp[3]], axis=0).reshape(N_SHARDS, N_IN_COLS // N_SHARDS, d)
    dw_in = jnp.transpose(dw_in, (0, 2, 1))
    sums_mix = pair_sums([dw_in, dwo], "b")
    (gx, dyh1, dg1, du1, h1, d_nw1), slots_mix = _ffn_bwd_dx(dx1, xs, ffn1_norm, g1, u1, *f1, TM, "ffn1_bwd_dx",
                                                              exchange=_scatter_exchange(sums_mix))
    dwg1 = _matmul_tn(dg1, xn1, tk, "ffn1_dwg")
    dwu1 = _matmul_tn(du1, xn1, tk, "ffn1_dwu")
    dwd1 = _matmul_tn(h1, dyh1, tk, "ffn1_dwd")
    slots_f1 = _rs_chips(pair_sums([dwg1, dwu1, dwd1], "c"))
    slots = slots_f1 + _own_slots(slots_mix, sums_mix) + _own_slots(slots_f2, sums_f2)
    halves = [_sum_slots(s, f"rs_sum_{i}") for i, s in enumerate(slots)]
    g_big = dict(zip(BIG, _rs_join(halves)))

    small_partial = {"ffn1_norm": d_nw1, "mix_norm": d_mixnw, "a_log": dgp[0, 0:8], "dt_bias": dgp[1, 0:8],
                     "gdn_norm_w": d_gdnw, "q_norm_w": d_qw, "k_norm_w": d_kw, "rel_bias": d_rel,
                     "ffn2_norm": d_nw2, "final_norm": d_final, "conv_w": dcw[0:CONV_TAPS].T}
    red = _all_reduce_small(_pack([small_partial[n] for n in SMALL] + [loss_part[0, 0:1]]))
    full_shapes = [p[n].shape if n != "conv_w" else (N_SHARDS * conv_rows, CONV_TAPS) for n in SMALL]
    red_parts = _unpack(red, full_shapes + [(1,)])
    loss = red_parts[-1].reshape(())
    g_small = dict(zip(SMALL, red_parts[:-1]))
    g_small["conv_w"] = lax.dynamic_slice_in_dim(g_small["conv_w"], me * conv_rows, conv_rows, 0).reshape(conv_w.shape)

    grads, deltas, new_m, new_v = {}, {}, {}, {}
    for n in BIG:
        back = (lambda a: a.T[None]) if n in COL_SHARDED else (lambda a: a[None])
        grads[n] = back(g_big[n])
        dl, nm, nv = _adamw(local(n, p[n]), g_big[n], local(n, p["m_" + n]), local(n, p["v_" + n]), "adamw_" + n)
        deltas[n], new_m[n], new_v[n] = back(dl), back(nm), back(nv)
    packed = [_pack([src[n] for n in SMALL]) for src in
              ({n: p[n] for n in SMALL}, g_small, {n: p["m_" + n] for n in SMALL}, {n: p["v_" + n] for n in SMALL})]
    small_shapes = [p[n].shape for n in SMALL]
    for dst, arr in zip((deltas, new_m, new_v), _adamw(*packed, "adamw_small")):
        dst.update(zip(SMALL, _unpack(arr, small_shapes)))
    grads.update(g_small)

    return (loss, gx[None], *[grads[n] for n in WEIGHTS], *[deltas[n] for n in WEIGHTS],
            *[new_m[n] for n in WEIGHTS], *[new_v[n] for n in WEIGHTS])
```

```python
import math
from typing import Callable, NamedTuple

import numpy as np
import jax
import jax.numpy as jnp
from jax import lax
from jax.experimental import pallas as pl
from jax.experimental.pallas import tpu as pltpu

F32 = jnp.float32
BF16 = jnp.bfloat16
HIGHEST = lax.Precision.HIGHEST
MESH = pl.DeviceIdType.MESH

EPS = 1e-6
NEG_BIG = -1e30
GDN_HEADS = 4
GDN_DIM = 128
CHUNK = 64
SWA_HEADS = 8
SWA_DIM = 64
PATTERNS = ((128, 1), (512, 4), (2048, 16))
RADIUS = 64
REL_BUCKETS = 32
REL_MAX_DISTANCE = 1024
CONV_TAPS = 5
N_SHARDS = 4
LANES = 128
VMEM_LIMIT = 56 * 1024 * 1024

ADAM_LR, ADAM_B1, ADAM_B2, ADAM_EPS, ADAM_WD, ADAM_STEP = 0.001, 0.9, 0.999, 1e-08, 0.01, 10


def _params(sem=None, vmem=None):
    return pltpu.CompilerParams(dimension_semantics=sem, vmem_limit_bytes=vmem)


def _resident(shape):
    nd = len(shape)
    return pl.BlockSpec(shape, lambda *_: (0,) * nd, pipeline_mode=pl.Buffered(1))


ANY = pl.BlockSpec(memory_space=pl.ANY)


class _Exchange(NamedTuple):
    arrays: tuple
    out_shape: tuple
    start: Callable
    finish: Callable


def _grid_call(body, name, nsteps, in_specs, out_specs, out_shape, operands, scratch=(), exchange=None):
    params = _params(("arbitrary",), VMEM_LIMIT)
    if exchange is None:
        res = pl.pallas_call(body, name=name, grid=(nsteps,), in_specs=list(in_specs), out_specs=list(out_specs),
                             out_shape=list(out_shape), scratch_shapes=list(scratch), compiler_params=params)(*operands)
        return list(res), []
    n_in, n_out, k, n_scr = len(in_specs), len(out_specs), len(exchange.arrays), len(scratch)

    def wrapped(*refs):
        ins, cin = refs[:n_in], refs[n_in:n_in + k]
        outs, cout = refs[n_in + k:n_in + k + n_out], refs[n_in + k + n_out:n_in + 2 * k + n_out]
        rest = refs[n_in + 2 * k + n_out:]
        scr, (send_sems, recv_sems) = rest[:n_scr], rest[n_scr:]

        @pl.when(pl.program_id(0) == 0)
        def _():
            exchange.start(cin, cout, send_sems, recv_sems)

        body(*ins, *outs, *scr)

        @pl.when(pl.program_id(0) == nsteps - 1)
        def _():
            exchange.finish(cin, cout, send_sems, recv_sems)

    res = pl.pallas_call(
        wrapped, name=name, grid=(nsteps,), in_specs=list(in_specs) + [ANY] * k, out_specs=list(out_specs) + [ANY] * k,
        out_shape=list(out_shape) + list(exchange.out_shape),
        scratch_shapes=list(scratch) + [pltpu.SemaphoreType.DMA((k, 3)), pltpu.SemaphoreType.DMA((k, 3))],
        compiler_params=params)(*operands, *exchange.arrays)
    return list(res[:n_out]), list(res[n_out:])


def _dot(a, b):
    return jnp.dot(a.astype(BF16), b.astype(BF16), preferred_element_type=F32)


def _dot_nt(a, b):
    return lax.dot_general(a.astype(BF16), b.astype(BF16), (((1,), (1,)), ((), ())), preferred_element_type=F32)


def _dot_tn(a, b):
    return lax.dot_general(a.astype(BF16), b.astype(BF16), (((0,), (0,)), ((), ())), preferred_element_type=F32)


def _dot_hi(a, b):
    return jnp.dot(a, b, preferred_element_type=F32, precision=HIGHEST)


def _sigmoid(x):
    return 1.0 / (1.0 + jnp.exp(-x))


def _rstd(xf):
    return lax.rsqrt(jnp.mean(xf * xf, axis=-1, keepdims=True) + EPS)


def _rms_bwd(xf, r, nw, dxn):
    xhat = xf * r
    dxh = dxn * nw
    dx = r * (dxh - xhat * jnp.mean(dxh * xhat, axis=-1, keepdims=True))
    return dx, jnp.sum(dxn * xhat, axis=0, keepdims=True)


def _ffn_fwd(x, nw, wg, wu, wd, tm, name, exchange=None):
    t, d = x.shape
    nj, fs, _ = wg.shape

    def body(x_ref, nw_ref, wg_ref, wu_ref, wd_ref, y_ref, xn_ref, g_ref, u_ref):
        xf = x_ref[...]
        xn = (xf * _rstd(xf) * nw_ref[...]).astype(BF16)
        xn_ref[...] = xn
        acc = jnp.zeros((tm, d), F32)
        for j in range(nj):
            g = _dot_nt(xn, wg_ref[j])
            u = _dot_nt(xn, wu_ref[j])
            h = (g * _sigmoid(g) * u).astype(BF16)
            acc = acc + jnp.dot(h, wd_ref[j], preferred_element_type=F32)
            g_ref[j] = g.astype(BF16)
            u_ref[j] = u.astype(BF16)
        y_ref[...] = xf + 0.5 * acc

    row = pl.BlockSpec((tm, d), lambda i: (i, 0))
    act = pl.BlockSpec((nj, tm, fs), lambda i: (0, i, 0))
    return _grid_call(
        body, name, t // tm,
        [row, _resident((1, d)), _resident(wg.shape), _resident(wu.shape), _resident(wd.shape)],
        [row, row, act, act],
        [jax.ShapeDtypeStruct((t, d), F32), jax.ShapeDtypeStruct((t, d), BF16),
         jax.ShapeDtypeStruct((nj, t, fs), BF16), jax.ShapeDtypeStruct((nj, t, fs), BF16)],
        (x, nw, wg, wu, wd), exchange=exchange)


def _ffn_bwd_dx(dy, x, nw, g, u, wg, wu, wd, tm, name, exchange=None):
    t, d = x.shape
    nj, fs, _ = wg.shape

    def body(dy_ref, x_ref, nw_ref, g_ref, u_ref, wg_ref, wu_ref, wd_ref,
             dx_ref, dyh_ref, dg_ref, du_ref, h_ref, dnw_ref):
        @pl.when(pl.program_id(0) == 0)
        def _():
            dnw_ref[...] = jnp.zeros_like(dnw_ref)

        dyv = dy_ref[...]
        dyh = (0.5 * dyv).astype(BF16)
        dyh_ref[...] = dyh
        dxn = jnp.zeros((tm, d), F32)
        dh_next = _dot_nt(dyh, wd_ref[0])
        for j in range(nj):
            dh = dh_next
            gv = g_ref[j].astype(F32)
            uv = u_ref[j].astype(F32)
            sg = _sigmoid(gv)
            si = gv * sg
            dg = (dh * uv * (sg * (1.0 + gv * (1.0 - sg)))).astype(BF16)
            du = (dh * si).astype(BF16)
            if j + 1 < nj:
                dh_next = _dot_nt(dyh, wd_ref[j + 1])
            h_ref[j] = (si * uv).astype(BF16)
            dg_ref[j] = dg
            du_ref[j] = du
            dxn = dxn + _dot(dg, wg_ref[j]) + _dot(du, wu_ref[j])
        xf = x_ref[...]
        dxr, dnw = _rms_bwd(xf, _rstd(xf), nw_ref[...], dxn)
        dx_ref[...] = dyv + dxr
        dnw_ref[...] += dnw

    row = pl.BlockSpec((tm, d), lambda i: (i, 0))
    act = pl.BlockSpec((nj, tm, fs), lambda i: (0, i, 0))
    act_shape = jax.ShapeDtypeStruct((nj, t, fs), BF16)
    return _grid_call(
        body, name, t // tm,
        [row, row, _resident((1, d)), act, act, _resident(wg.shape), _resident(wu.shape), _resident(wd.shape)],
        [row, row, act, act, act, pl.BlockSpec((1, d), lambda i: (0, 0))],
        [jax.ShapeDtypeStruct((t, d), F32), jax.ShapeDtypeStruct((t, d), BF16),
         act_shape, act_shape, act_shape, jax.ShapeDtypeStruct((1, d), F32)],
        (dy, x, nw, g, u, wg, wu, wd), exchange=exchange)


def _matmul_tn(a, b, tk, name):
    a3, b3 = a.ndim == 3, b.ndim == 3
    nj = a.shape[0] if a3 else (b.shape[0] if b3 else 1)
    t, m = a.shape[-2:]
    n = b.shape[-1]
    nt = t // tk

    def body(a_ref, b_ref, o_ref, acc_ref):
        k = pl.program_id(1)

        @pl.when(k == 0)
        def _():
            acc_ref[...] = jnp.zeros_like(acc_ref)

        acc_ref[...] += lax.dot_general(a_ref[...], b_ref[...], (((0,), (0,)), ((), ())),
                                        preferred_element_type=F32)

        @pl.when(k == nt - 1)
        def _():
            o_ref[...] = acc_ref[...].astype(o_ref.dtype)

    a_spec = (pl.BlockSpec((None, tk, m), lambda j, k: (j, k, 0)) if a3
              else pl.BlockSpec((tk, m), lambda j, k: (k, 0)))
    b_spec = (pl.BlockSpec((None, tk, n), lambda j, k: (j, k, 0)) if b3
              else pl.BlockSpec((tk, n), lambda j, k: (k, 0)))
    return pl.pallas_call(
        body, name=name, grid=(nj, nt),
        in_specs=[a_spec, b_spec],
        out_specs=pl.BlockSpec((None, m, n), lambda j, k: (j, 0, 0)),
        out_shape=jax.ShapeDtypeStruct((nj, m, n), BF16),
        scratch_shapes=[pltpu.VMEM((m, n), F32)],
        compiler_params=_params(("arbitrary", "arbitrary"), VMEM_LIMIT),
    )(a, b)


P_QKVA, P_Z, P_AB, P_QKVB = (0, 1536), (1536, 2048), (2048, 2176), (2176, 3712)
P_PIECES = (P_QKVA, P_Z, P_AB, P_QKVB)
P_COLS = 3712


def _mix_in_fwd(x1, nw, wp, tm):
    t, d = x1.shape

    def body(x_ref, nw_ref, w_ref, hn_ref, *outs):
        xf = x_ref[...]
        xn = (xf * _rstd(xf) * nw_ref[...]).astype(BF16)
        hn_ref[...] = xn
        for (a, b), o_ref in zip(P_PIECES, outs):
            o_ref[...] = _dot_nt(xn, w_ref[a:b, :])

    row = pl.BlockSpec((tm, d), lambda i: (i, 0))
    return pl.pallas_call(
        body, name="mix_in_fwd", grid=(t // tm,),
        in_specs=[row, _resident((1, d)), _resident(wp.shape)],
        out_specs=[row] + [pl.BlockSpec((tm, b - a), lambda i: (i, 0)) for a, b in P_PIECES],
        out_shape=[jax.ShapeDtypeStruct((t, d), BF16)]
                  + [jax.ShapeDtypeStruct((t, b - a), F32) for a, b in P_PIECES],
        compiler_params=_params(("arbitrary",), VMEM_LIMIT),
    )(x1, nw, wp)


def _mix_in_bwd_dx(dx, x1, nw, dpieces, wp, tm):
    t, d = x1.shape

    def body(dx_ref, x_ref, nw_ref, p0, p1, p2, p3, w_ref, o_ref, dnw_ref):
        @pl.when(pl.program_id(0) == 0)
        def _():
            dnw_ref[...] = jnp.zeros_like(dnw_ref)

        dh = jnp.zeros((tm, d), F32)
        for (a, b), p_ref in zip(P_PIECES, (p0, p1, p2, p3)):
            dh = dh + _dot(p_ref[...], w_ref[a:b, :])
        xf = x_ref[...]
        dxr, dnw = _rms_bwd(xf, _rstd(xf), nw_ref[...], dh)
        o_ref[...] = dx_ref[...] + dxr
        dnw_ref[...] += dnw

    row = pl.BlockSpec((tm, d), lambda i: (i, 0))
    return pl.pallas_call(
        body, name="mix_in_bwd_dx", grid=(t // tm,),
        in_specs=[row, row, _resident((1, d))]
                 + [pl.BlockSpec((tm, b - a), lambda i: (i, 0)) for a, b in P_PIECES] + [_resident(wp.shape)],
        out_specs=[row, pl.BlockSpec((1, d), lambda i: (0, 0))],
        out_shape=[jax.ShapeDtypeStruct((t, d), F32), jax.ShapeDtypeStruct((1, d), F32)],
        compiler_params=_params(("arbitrary",), VMEM_LIMIT),
    )(dx, x1, nw, *dpieces, wp)


def _mix_out_fwd(x1, oa, ob, w, tm):
    t, d = x1.shape
    half = oa.shape[1]

    def body(x_ref, oa_ref, ob_ref, w_ref, o_ref):
        o_ref[...] = (x_ref[...] + _dot(oa_ref[...], w_ref[0:half, :]) + _dot(ob_ref[...], w_ref[half:2 * half, :]))

    row = pl.BlockSpec((tm, d), lambda i: (i, 0))
    hrow = pl.BlockSpec((tm, half), lambda i: (i, 0))
    return pl.pallas_call(
        body, name="mix_out_fwd", grid=(t // tm,),
        in_specs=[row, hrow, hrow, _resident(w.shape)],
        out_specs=row, out_shape=jax.ShapeDtypeStruct((t, d), F32),
        compiler_params=_params(("arbitrary",), VMEM_LIMIT),
    )(x1, oa, ob, w)


def _mix_out_bwd(dx2, w, tm):
    t, d = dx2.shape
    half = w.shape[0] // 2

    def body(dx_ref, w_ref, doa_ref, dob_ref, dxb_ref):
        dxb = dx_ref[...].astype(BF16)
        dxb_ref[...] = dxb
        doa_ref[...] = _dot_nt(dxb, w_ref[0:half, :])
        dob_ref[...] = _dot_nt(dxb, w_ref[half:2 * half, :])

    row = pl.BlockSpec((tm, d), lambda i: (i, 0))
    hrow = pl.BlockSpec((tm, half), lambda i: (i, 0))
    return pl.pallas_call(
        body, name="mix_out_bwd", grid=(t // tm,),
        in_specs=[row, _resident(w.shape)],
        out_specs=[hrow, hrow, row],
        out_shape=[jax.ShapeDtypeStruct((t, half), F32), jax.ShapeDtypeStruct((t, half), F32),
                   jax.ShapeDtypeStruct((t, d), BF16)],
        compiler_params=_params(("arbitrary",), VMEM_LIMIT),
    )(dx2, w)


def _final_loss(x3, fw, target, tm):
    t, d = x3.shape

    def body(x_ref, w_ref, t_ref, dx_ref, loss_ref, dw_ref):
        @pl.when(pl.program_id(0) == 0)
        def _():
            loss_ref[...] = jnp.zeros_like(loss_ref)
            dw_ref[...] = jnp.zeros_like(dw_ref)

        xf = x_ref[...]
        r = _rstd(xf)
        err = xf * r * w_ref[...] - t_ref[...]
        loss_ref[...] += 0.5 * jnp.sum(jnp.mean(err * err, axis=-1, keepdims=True), axis=0, keepdims=True)
        dxr, dw = _rms_bwd(xf, r, w_ref[...], err * (1.0 / d))
        dx_ref[...] = dxr
        dw_ref[...] += dw

    row = pl.BlockSpec((tm, d), lambda i: (i, 0))
    return pl.pallas_call(
        body, name="final_loss", grid=(t // tm,),
        in_specs=[row, _resident((1, d)), row],
        out_specs=[row, pl.BlockSpec((1, LANES), lambda i: (0, 0)), pl.BlockSpec((1, d), lambda i: (0, 0))],
        out_shape=[jax.ShapeDtypeStruct((t, d), F32), jax.ShapeDtypeStruct((1, LANES), F32),
                   jax.ShapeDtypeStruct((1, d), F32)],
        compiler_params=_params(("arbitrary",), VMEM_LIMIT),
    )(x3, fw, target)


HALO = 8


def _halo_row_specs(tr, cols, nrow8):
    per = tr // HALO
    return [pl.BlockSpec((tr, cols), lambda i: (i, 0)),
            pl.BlockSpec((HALO, cols), lambda i: (jnp.maximum(i * per - 1, 0), 0)),
            pl.BlockSpec((HALO, cols), lambda i: (jnp.minimum((i + 1) * per, nrow8 - 1), 0))]


def _conv_window(xm, xp, xn, first, last, cols):
    prev = jnp.where(first, 0.0, xp[:, cols])
    nxt = jnp.where(last, 0.0, xn[:, cols])
    return jnp.concatenate([prev, xm[:, cols], nxt], axis=0)


def _shift_rows(xw, off):
    n = xw.shape[0]
    sh = (-off) % n
    return xw if sh == 0 else pltpu.roll(xw, sh, 0)


def _conv_pre(xw, cw_ref, cols):
    acc = None
    for j in range(CONV_TAPS):
        term = _shift_rows(xw, j - CONV_TAPS // 2) * cw_ref[j:j + 1, cols]
        acc = term if acc is None else acc + term
    return acc


def _softplus(x):
    u = jnp.exp(-jnp.abs(x))
    w = 1.0 + u
    log1p = jnp.where(w == 1.0, u, jnp.log(w) * u / jnp.where(w == 1.0, 1.0, w - 1.0))
    return jnp.maximum(x, 0.0) + log1p


def _gdn_prep_fwd(qkva, cw, ab, gp, tr):
    t, c = qkva.shape
    nt = t // tr
    ncb = c // LANES

    def body(xm, xp, xn, cw_ref, ab_ref, gp_ref, o_ref, gb_ref):
        i = pl.program_id(0)
        first, last = i == 0, i == nt - 1
        for cb in range(ncb):
            cols = slice(cb * LANES, (cb + 1) * LANES)
            xw = _conv_window(xm, xp, xn, first, last, cols)
            pre = _conv_pre(xw, cw_ref, cols)[HALO:HALO + tr]
            y = pre * _sigmoid(pre)
            if cb < 2 * GDN_HEADS:
                y = y * lax.rsqrt(jnp.sum(y * y, axis=-1, keepdims=True) + EPS)
            if cb < GDN_HEADS:
                y = y * (GDN_DIM ** -0.5)
            o_ref[:, cols] = y
        abv = ab_ref[...]
        lane = lax.broadcasted_iota(jnp.int32, abv.shape, 1)
        g = -jnp.exp(gp_ref[0:1, :]) * _softplus(abv + gp_ref[1:2, :])
        gb_ref[...] = jnp.where(lane < 8, g, jnp.where(lane < 16, _sigmoid(abv), 0.0))

    return pl.pallas_call(
        body, name="gdn_prep_fwd", grid=(nt,),
        in_specs=_halo_row_specs(tr, c, t // HALO)
                 + [_resident(cw.shape), pl.BlockSpec((tr, LANES), lambda i: (i, 0)), _resident(gp.shape)],
        out_specs=[pl.BlockSpec((tr, c), lambda i: (i, 0)), pl.BlockSpec((tr, LANES), lambda i: (i, 0))],
        out_shape=[jax.ShapeDtypeStruct((t, c), F32), jax.ShapeDtypeStruct((t, LANES), F32)],
        compiler_params=_params(("arbitrary",), VMEM_LIMIT),
    )(qkva, qkva, qkva, cw, ab, gp)


def _gdn_prep_bwd(qkva, cw, ab, gp, dy, dgates, tr):
    t, c = qkva.shape
    nt = t // tr
    ncb = c // LANES

    def body(xm, xp, xn, fm, fp, fn, cw_ref, ab_ref, gp_ref, gf_ref, dx_ref, dab_ref, dcw_ref, dgp_ref):
        i = pl.program_id(0)
        first, last = i == 0, i == nt - 1

        @pl.when(first)
        def _():
            dcw_ref[...] = jnp.zeros_like(dcw_ref)
            dgp_ref[...] = jnp.zeros_like(dgp_ref)

        sub8 = lax.broadcasted_iota(jnp.int32, (8, LANES), 0)
        for cb in range(ncb):
            cols = slice(cb * LANES, (cb + 1) * LANES)
            xw = _conv_window(xm, xp, xn, first, last, cols)
            dyw = _conv_window(fm, fp, fn, first, last, cols)
            pre = _conv_pre(xw, cw_ref, cols)
            sg = _sigmoid(pre)
            s = pre * sg
            if cb < 2 * GDN_HEADS:
                scale = (GDN_DIM ** -0.5) if cb < GDN_HEADS else 1.0
                r = lax.rsqrt(jnp.sum(s * s, axis=-1, keepdims=True) + EPS)
                dn = dyw * scale
                ds = r * dn - s * (r * r * r) * jnp.sum(dn * s, axis=-1, keepdims=True)
            else:
                ds = dyw
            dpre = ds * (sg * (1.0 + pre * (1.0 - sg)))
            dx = None
            dcw = jnp.zeros((8, LANES), F32)
            for j in range(CONV_TAPS):
                off = j - CONV_TAPS // 2
                term = _shift_rows(dpre, -off)[HALO:HALO + tr] * cw_ref[j:j + 1, cols]
                dx = term if dx is None else dx + term
                tap = jnp.sum(dpre[HALO:HALO + tr] * _shift_rows(xw, off)[HALO:HALO + tr], axis=0, keepdims=True)
                dcw = dcw + jnp.where(sub8 == j, tap, 0.0)
            dx_ref[:, cols] = dx.astype(BF16)
            dcw_ref[:, cols] += dcw

        abv = ab_ref[...]
        dgb = gf_ref[...]
        lane = lax.broadcasted_iota(jnp.int32, abv.shape, 1)
        nea = -jnp.exp(gp_ref[0:1, :])
        xs = abv + gp_ref[1:2, :]
        g = nea * _softplus(xs)
        beta = _sigmoid(abv)
        da = dgb * nea * _sigmoid(xs)
        dab = jnp.where(lane < 8, da, jnp.where(lane < 16, dgb * beta * (1.0 - beta), 0.0))
        dab_ref[...] = dab.astype(BF16)
        keep = lane[0:1, :] < 8
        dalog = jnp.where(keep, jnp.sum(dgb * g, axis=0, keepdims=True), 0.0)
        ddtb = jnp.where(keep, jnp.sum(da, axis=0, keepdims=True), 0.0)
        dgp_ref[...] += jnp.where(sub8 == 0, dalog, 0.0) + jnp.where(sub8 == 1, ddtb, 0.0)

    lrow = pl.BlockSpec((tr, LANES), lambda i: (i, 0))
    halo = _halo_row_specs(tr, c, t // HALO)
    return pl.pallas_call(
        body, name="gdn_prep_bwd", grid=(nt,),
        in_specs=halo + halo + [_resident(cw.shape), lrow, _resident(gp.shape), lrow],
        out_specs=[pl.BlockSpec((tr, c), lambda i: (i, 0)), lrow,
                   pl.BlockSpec(cw.shape, lambda i: (0, 0)), pl.BlockSpec(gp.shape, lambda i: (0, 0))],
        out_shape=[jax.ShapeDtypeStruct((t, c), BF16), jax.ShapeDtypeStruct((t, LANES), BF16),
                   jax.ShapeDtypeStruct(cw.shape, F32), jax.ShapeDtypeStruct(gp.shape, F32)],
        compiler_params=_params(("arbitrary",), VMEM_LIMIT),
    )(qkva, qkva, qkva, dy, dy, dy, cw, ab, gp, dgates)


def _chunk_masks(lower):
    ii = lax.broadcasted_iota(jnp.int32, (CHUNK, CHUNK), 0)
    jj = lax.broadcasted_iota(jnp.int32, (CHUNK, CHUNK), 1)
    incl = (ii >= jj) if lower else (ii <= jj)
    strict = (ii > jj) if lower else (ii < jj)
    return ii, jj, incl, strict


def _dot3(a, b):
    ah = a.astype(BF16)
    al = (a - ah.astype(F32)).astype(BF16)
    bh = b.astype(BF16)
    bl = (b - bh.astype(F32)).astype(BF16)
    d = lambda u, v: jnp.dot(u, v, preferred_element_type=F32)
    return d(ah, bh) + (d(ah, bl) + d(al, bh))


def _tri_inv_many(lmats, ii, jj):
    m16 = (ii // 16) == (jj // 16)
    m32 = (ii // 32) == (jj // 32)
    eye = jnp.where(ii == jj, 1.0, 0.0)
    l16 = [jnp.where(m16, l, 0.0) for l in lmats]
    p2 = [_dot3(a, a) for a in l16]
    p4 = [_dot3(a, a) for a in p2]
    p8 = [_dot3(a, a) for a in p4]
    xs = [eye - a for a in l16]
    for ps in (p2, p4, p8):
        xs = [x + _dot3(x, p) for x, p in zip(xs, ps)]
    for off in ([jnp.where(m32 & jnp.logical_not(m16), l, 0.0) for l in lmats],
                [jnp.where(m32, 0.0, l) for l in lmats]):
        ys = [_dot3(x, c) for x, c in zip(xs, off)]
        xs = [x - _dot3(y, x) for x, y in zip(xs, ys)]
    return xs


def _col_to_row(col, ii, jj):
    return jnp.sum(jnp.where(ii == jj, col, 0.0), axis=0, keepdims=True)


def _row_to_col(row, ii, jj):
    return jnp.sum(jnp.where(ii == jj, row, 0.0), axis=1, keepdims=True)


def _chain_common(q, k, v, graw_col, graw_row, bcol, masks):
    ii, jj, incl, strict = masks
    inclt = jnp.logical_not(strict)
    gcol = jnp.sum(jnp.where(incl, graw_row, 0.0), axis=1, keepdims=True)
    grow = jnp.sum(jnp.where(inclt, graw_col, 0.0), axis=0, keepdims=True)
    glast = jnp.sum(graw_row, axis=1, keepdims=True)
    decay = jnp.where(incl, jnp.exp(jnp.where(incl, gcol - grow, 0.0)), 0.0)
    kb = k * bcol
    vb = v * bcol
    eg = jnp.exp(gcol)
    ek = jnp.exp(glast - gcol)
    kbg = kb * eg
    amat = _dot_nt(kb, k)
    qk = _dot_nt(q, k)
    return dict(gcol=gcol, glast=glast, decay=decay, kb=kb, vb=vb, eg=eg, ek=ek, kbg=kbg, amat=amat, qk=qk,
                intra=qk * decay, qg=q * eg, kdec=k * ek)


def _gdn_fwd(qkvc, gb, gbt):
    tm, u, w, qg, kd, intra, egl = _gdn_local_fwd(qkvc, gb, gbt)
    o_f, o_b, s_f, s_b, vn_f, vn_b = _gdn_scan_fwd(u, w, qg, kd, intra, egl, qkvc.shape[0])
    return o_f, o_b, dict(tm=tm, w=w, qg=qg, kd=kd, intra=intra, egl=egl, s=(s_f, s_b), vn=(vn_f, vn_b))


N_CHAINS = 2 * GDN_HEADS


LOCAL_CHUNKS = 2


def _load_chains(x_ref, g_ref, gt_ref, cc=0):
    hd = GDN_HEADS * GDN_DIM
    rows = slice(cc * CHUNK, (cc + 1) * CHUNK)
    chains = []
    for d in range(2):
        masks = _chunk_masks(d == 0)
        for h in range(GDN_HEADS):
            ch = d * GDN_HEADS + h
            q = x_ref[rows, h * GDN_DIM:(h + 1) * GDN_DIM]
            k = x_ref[rows, hd + h * GDN_DIM:hd + (h + 1) * GDN_DIM]
            v = x_ref[rows, 2 * hd + h * GDN_DIM:2 * hd + (h + 1) * GDN_DIM]
            bcol = g_ref[rows, 8 + ch:9 + ch]
            cm = _chain_common(q, k, v, g_ref[rows, ch:ch + 1], gt_ref[cc, ch:ch + 1, :], bcol, masks)
            chains.append(dict(cm, q=q, k=k, v=v, bcol=bcol, masks=masks, ch=ch, h=h, cc=cc))
    return chains


def _chain_shape(rows, cols, dtype):
    return lambda nc: jax.ShapeDtypeStruct((nc, N_CHAINS, rows, cols), dtype)


def _gdn_local_fwd(qkvc, gb, gbt):
    t = qkvc.shape[0]
    nc = t // CHUNK
    hd = GDN_HEADS * GDN_DIM

    def body(x_ref, g_ref, gt_ref, t_ref, u_ref, w_ref, qg_ref, kd_ref, in_ref, eg_ref):
        chains = [c for cc in range(LOCAL_CHUNKS) for c in _load_chains(x_ref, g_ref, gt_ref, cc)]
        ii, jj = chains[0]["masks"][0:2]
        tms = _tri_inv_many([jnp.where(c["masks"][3], c["amat"] * c["decay"], 0.0) for c in chains], ii, jj)
        uws = [_dot(tm, jnp.concatenate([c["vb"], c["kbg"]], axis=1)) for tm, c in zip(tms, chains)]
        for c, tm, uw in zip(chains, tms, uws):
            cc, ch = c["cc"], c["ch"]
            t_ref[cc, ch] = tm
            u_ref[cc, ch] = uw[:, :GDN_DIM]
            w_ref[cc, ch] = uw[:, GDN_DIM:].astype(BF16)
            qg_ref[cc, ch] = c["qg"].astype(BF16)
            kd_ref[cc, ch] = c["kdec"].astype(BF16)
            in_ref[cc, ch] = c["intra"].astype(BF16)
            eg_ref[cc, ch:ch + 1, :] = jnp.broadcast_to(jnp.exp(c["glast"]), (1, LANES))

    lc = LOCAL_CHUNKS
    blk = lambda rows, cols: pl.BlockSpec((lc, N_CHAINS, rows, cols), lambda n: (n, 0, 0, 0))
    shapes = [_chain_shape(CHUNK, CHUNK, F32), _chain_shape(CHUNK, GDN_DIM, F32), _chain_shape(CHUNK, GDN_DIM, BF16),
              _chain_shape(CHUNK, GDN_DIM, BF16), _chain_shape(CHUNK, GDN_DIM, BF16), _chain_shape(CHUNK, CHUNK, BF16)]
    return tuple(pl.pallas_call(
        body, name="gdn_local_fwd", grid=(nc // lc,),
        in_specs=[pl.BlockSpec((lc * CHUNK, 3 * hd), lambda n: (n, 0)), pl.BlockSpec((lc * CHUNK, LANES), lambda n: (n, 0)),
                  pl.BlockSpec((lc, 16, CHUNK), lambda n: (n, 0, 0))],
        out_specs=[blk(CHUNK, CHUNK), blk(CHUNK, GDN_DIM), blk(CHUNK, GDN_DIM), blk(CHUNK, GDN_DIM),
                   blk(CHUNK, GDN_DIM), blk(CHUNK, CHUNK), pl.BlockSpec((lc, N_CHAINS, LANES), lambda n: (n, 0, 0))],
        out_shape=[s(nc) for s in shapes] + [jax.ShapeDtypeStruct((nc, N_CHAINS, LANES), F32)],
        compiler_params=_params(("arbitrary",), VMEM_LIMIT),
    )(qkvc, gb, gbt))


SCAN_CHUNKS = 4


def _dir_specs(nc, rev):
    nb = nc // SCAN_CHUNKS

    def spec(d, rows, cols, own=False):
        chunk = (lambda n: n) if (d == 0) != rev else (lambda n: nb - 1 - n)
        blk = 0 if own else d
        if rows is None:
            return pl.BlockSpec((SCAN_CHUNKS, GDN_HEADS if own else N_CHAINS, cols), lambda n: (chunk(n), 0, 0))
        return pl.BlockSpec((SCAN_CHUNKS, GDN_HEADS, rows, cols), lambda n: (chunk(n), blk, 0, 0))

    def rows_spec(d, cols):
        chunk = (lambda n: n) if (d == 0) != rev else (lambda n: nb - 1 - n)
        return pl.BlockSpec((SCAN_CHUNKS * CHUNK, cols), lambda n: (chunk(n), 0))

    def order(d):
        return list(range(SCAN_CHUNKS)) if (d == 0) != rev else list(range(SCAN_CHUNKS - 1, -1, -1))
    return spec, rows_spec, order


def _gdn_scan_fwd(u, w, qg, kd, intra, egl, t):
    nc = t // CHUNK
    hd = GDN_HEADS * GDN_DIM

    def body(*refs):
        ins, outs, state = refs[:12], refs[12:18], refs[18]
        @pl.when(pl.program_id(0) == 0)
        def _():
            state[...] = jnp.zeros_like(state)

        chains = [(d, h) for d in range(2) for h in range(GDN_HEADS)]
        states = [state[ch] for ch in range(N_CHAINS)]
        for step in range(SCAN_CHUNKS):
            at = [order(d)[step] for d in range(2)]
            pick = lambda k, d, h: ins[2 * k + d][at[d], h]
            sbs = [s.astype(BF16) for s in states]
            ws = [_dot(pick(1, d, h), sb) for (d, h), sb in zip(chains, sbs)]
            o1 = [_dot(pick(2, d, h), sb) for (d, h), sb in zip(chains, sbs)]
            vns = [(pick(0, d, h) - wsb).astype(BF16) for (d, h), wsb in zip(chains, ws)]
            o2 = [_dot(pick(4, d, h), vn) for (d, h), vn in zip(chains, vns)]
            kv = [_dot_tn(pick(3, d, h), vn) for (d, h), vn in zip(chains, vns)]
            new_states = []
            for ch, (d, h) in enumerate(chains):
                outs[d][at[d] * CHUNK:(at[d] + 1) * CHUNK, h * GDN_DIM:(h + 1) * GDN_DIM] = o1[ch] + o2[ch]
                outs[2 + d][at[d], h] = states[ch]
                outs[4 + d][at[d], h] = vns[ch]
                new_states.append(states[ch] * ins[10 + d][at[d], ch:ch + 1, :] + kv[ch])
            states = new_states
        for ch in range(N_CHAINS):
            state[ch] = states[ch]

    spec, rows_spec, order = _dir_specs(nc, False)
    pair = lambda rows, cols, own=False: [spec(0, rows, cols, own), spec(1, rows, cols, own)]
    s_shape = jax.ShapeDtypeStruct((nc, GDN_HEADS, GDN_DIM, GDN_DIM), F32)
    vn_shape = jax.ShapeDtypeStruct((nc, GDN_HEADS, CHUNK, GDN_DIM), BF16)
    return pl.pallas_call(
        body, name="gdn_scan_fwd", grid=(nc // SCAN_CHUNKS,),
        in_specs=(pair(CHUNK, GDN_DIM) + pair(CHUNK, GDN_DIM) + pair(CHUNK, GDN_DIM) + pair(CHUNK, GDN_DIM)
                  + pair(CHUNK, CHUNK) + pair(None, LANES)),
        out_specs=([rows_spec(0, hd), rows_spec(1, hd)] + pair(GDN_DIM, GDN_DIM, True)
                   + pair(CHUNK, GDN_DIM, True)),
        out_shape=[jax.ShapeDtypeStruct((t, hd), F32), jax.ShapeDtypeStruct((t, hd), F32),
                   s_shape, s_shape, vn_shape, vn_shape],
        scratch_shapes=[pltpu.VMEM((N_CHAINS, GDN_DIM, GDN_DIM), F32)],
        compiler_params=_params(("arbitrary",), VMEM_LIMIT),
    )(u, u, w, w, qg, qg, kd, kd, intra, intra, egl, egl)


def _gdn_bwd(qkvc, gb, gbt, do, saved, exchange=None):
    scan = _gdn_scan_bwd(do, saved, qkvc.shape[0])
    return _gdn_local_bwd(qkvc, gb, gbt, do, saved, scan, exchange)


def _gdn_scan_bwd(do, saved, t):
    nc = t // CHUNK
    hd = GDN_HEADS * GDN_DIM

    def body(*refs):
        ins, outs, dstate = refs[:16], refs[16:26], refs[26]
        @pl.when(pl.program_id(0) == 0)
        def _():
            dstate[...] = jnp.zeros_like(dstate)

        chains = [(d, h) for d in range(2) for h in range(GDN_HEADS)]
        dss = [dstate[ch] for ch in range(N_CHAINS)]
        for step in range(SCAN_CHUNKS):
            at = [order(d)[step] for d in range(2)]
            pick = lambda k, d, h: ins[2 * k + d][at[d], h]
            dsbs = [ds.astype(BF16) for ds in dss]
            ss = [pick(1, d, h) for d, h in chains]
            sbs = [s.astype(BF16) for s in ss]
            dos = [ins[d][at[d] * CHUNK:(at[d] + 1) * CHUNK, h * GDN_DIM:(h + 1) * GDN_DIM].astype(BF16)
                   for d, h in chains]
            dv1 = [_dot_tn(pick(5, d, h), dov) for (d, h), dov in zip(chains, dos)]
            dv2 = [_dot(pick(4, d, h), dsb) for (d, h), dsb in zip(chains, dsbs)]
            ds1 = [_dot_tn(pick(3, d, h), dov) for (d, h), dov in zip(chains, dos)]
            dkds = [_dot_nt(pick(6, d, h), dsb) for (d, h), dsb in zip(chains, dsbs)]
            dqgs = [_dot_nt(dov, sb) for dov, sb in zip(dos, sbs)]
            dvns = [(a + b).astype(BF16) for a, b in zip(dv1, dv2)]
            ds2 = [_dot_tn(pick(2, d, h), dvn) for (d, h), dvn in zip(chains, dvns)]
            dws = [_dot_nt(dvn, sb) for dvn, sb in zip(dvns, sbs)]
            new_dss = []
            for ch, (d, h) in enumerate(chains):
                egl = ins[14 + d][at[d], ch:ch + 1, :]
                outs[d][at[d], h] = dvns[ch]
                outs[2 + d][at[d], h] = (-dws[ch]).astype(BF16)
                outs[4 + d][at[d], h] = dqgs[ch]
                outs[6 + d][at[d], h] = dkds[ch]
                outs[8 + d][at[d], h:h + 1, :] = egl * jnp.sum(jnp.sum(ss[ch] * dss[ch], axis=1, keepdims=True),
                                                               axis=0, keepdims=True)
                new_dss.append(ds1[ch] + egl * dss[ch] - ds2[ch])
            dss = new_dss
        for ch in range(N_CHAINS):
            dstate[ch] = dss[ch]

    spec, rows_spec, order = _dir_specs(nc, True)
    pair = lambda rows, cols, own=False: [spec(0, rows, cols, own), spec(1, rows, cols, own)]
    s_f, s_b = saved["s"]
    vn_f, vn_b = saved["vn"]
    w, qg, kd, intra, egl = saved["w"], saved["qg"], saved["kd"], saved["intra"], saved["egl"]
    own = lambda rows, cols, dtype: jax.ShapeDtypeStruct((nc, GDN_HEADS, rows, cols), dtype)
    row_shape = jax.ShapeDtypeStruct((nc, GDN_HEADS, LANES), F32)
    return pl.pallas_call(
        body, name="gdn_scan_bwd", grid=(nc // SCAN_CHUNKS,),
        in_specs=([rows_spec(0, hd), rows_spec(1, hd)] + pair(GDN_DIM, GDN_DIM, True) + pair(CHUNK, GDN_DIM)
                  + pair(CHUNK, GDN_DIM) + pair(CHUNK, GDN_DIM) + pair(CHUNK, CHUNK) + pair(CHUNK, GDN_DIM, True)
                  + pair(None, LANES)),
        out_specs=(pair(CHUNK, GDN_DIM, True) + pair(CHUNK, GDN_DIM, True) + pair(CHUNK, GDN_DIM, True)
                   + pair(CHUNK, GDN_DIM, True) + pair(None, LANES, True)),
        out_shape=[own(CHUNK, GDN_DIM, BF16)] * 4 + [own(CHUNK, GDN_DIM, F32)] * 4 + [row_shape] * 2,
        scratch_shapes=[pltpu.VMEM((N_CHAINS, GDN_DIM, GDN_DIM), F32)],
        compiler_params=_params(("arbitrary",), VMEM_LIMIT),
    )(do, do, s_f, s_b, w, w, qg, qg, kd, kd, intra, intra, vn_f, vn_b, egl, egl)


def _dot3_nt(a, b):
    ah = a.astype(BF16)
    al = (a - ah.astype(F32)).astype(BF16)
    bh = b.astype(BF16)
    bl = (b - bh.astype(F32)).astype(BF16)
    return _dot_nt(ah, bh) + (_dot_nt(ah, bl) + _dot_nt(al, bh))


def _dot3_tn(a, b):
    ah = a.astype(BF16)
    al = (a - ah.astype(F32)).astype(BF16)
    bh = b.astype(BF16)
    bl = (b - bh.astype(F32)).astype(BF16)
    return _dot_tn(ah, bh) + (_dot_tn(ah, bl) + _dot_tn(al, bh))


def _gdn_local_bwd(qkvc, gb, gbt, do, saved, scan, exchange=None):
    t = qkvc.shape[0]
    nc = t // CHUNK
    hd = GDN_HEADS * GDN_DIM

    def body(*refs):
        x_ref, g_ref, gt_ref, do_ref, t_ref = refs[:5]
        per_dir = refs[5:17]
        dx_ref, dg_ref = refs[17:]
        chains = [c for cc in range(LOCAL_CHUNKS) for c in _load_chains(x_ref, g_ref, gt_ref, cc)]
        lane = lax.broadcasted_iota(jnp.int32, (CHUNK, LANES), 1)
        dgates = [jnp.zeros((CHUNK, LANES), F32) for _ in range(LOCAL_CHUNKS)]
        for c in chains:
            d = c["ch"] // GDN_HEADS
            vn_ref, dvn_ref, dw_ref, dqg_ref, dkd_ref, dgl_ref = per_dir[d::2]
            h, cc = c["h"], c["cc"]
            rows = slice(cc * CHUNK, (cc + 1) * CHUNK)
            c.update(tm=t_ref[cc, c["ch"]], dov=do_ref[rows, h * GDN_DIM:(h + 1) * GDN_DIM], vnew=vn_ref[cc, h],
                     dvnew=dvn_ref[cc, h], dw=dw_ref[cc, h], dqg=dqg_ref[cc, h], dkdec=dkd_ref[cc, h],
                     dglast=dgl_ref[cc, h:h + 1, 0:1])
        dintras = [_dot_nt(c["dov"], c["vnew"]) for c in chains]
        dts = [_dot_nt(c["dvnew"], c["vb"]) + _dot_nt(c["dw"], c["kbg"]) for c in chains]
        dvbs = [_dot_tn(c["tm"], c["dvnew"]) for c in chains]
        dkbgs = [_dot_tn(c["tm"], c["dw"]) for c in chains]
        tdts = [_dot3_nt(dt, c["tm"]) for dt, c in zip(dts, chains)]
        dls = [jnp.where(c["masks"][3], -_dot3_tn(c["tm"], tdt), 0.0) for tdt, c in zip(tdts, chains)]
        das = [dl * c["decay"] for dl, c in zip(dls, chains)]
        dqks = [jnp.where(c["masks"][2], di, 0.0) * c["decay"] for di, c in zip(dintras, chains)]
        dkb1 = [_dot(da, c["k"]) for da, c in zip(das, chains)]
        dk1 = [_dot_tn(da, c["kb"]) for da, c in zip(das, chains)]
        dk2 = [_dot_tn(dqk, c["q"]) for dqk, c in zip(dqks, chains)]
        dq1 = [_dot(dqk, c["k"]) for dqk, c in zip(dqks, chains)]
        grads, mms, p_gs, p_betas, p_kds = [], [], [], [], []
        for n, c in enumerate(chains):
            incl = c["masks"][2]
            dkb = dkb1[n] + dkbgs[n] * c["eg"]
            kd = c["dkdec"] * c["kdec"]
            mms.append((dls[n] * c["amat"] + jnp.where(incl, dintras[n], 0.0) * c["qk"]) * c["decay"])
            p_gs.append(c["dqg"] * c["qg"] - kd + dkbgs[n] * c["kbg"])
            p_betas.append(dkb * c["k"] + dvbs[n] * c["v"])
            p_kds.append(kd)
            grads.append((dq1[n] + c["dqg"] * c["eg"],
                          dk1[n] + dk2[n] + c["dkdec"] * c["ek"] + dkb * c["bcol"],
                          dvbs[n] * c["bcol"]))
        row_sums = [jnp.sum(mm, axis=1, keepdims=True) for mm in mms]
        col_sums = [jnp.sum(mm, axis=0, keepdims=True) for mm in mms]
        g_sums = [jnp.sum(pg, axis=1, keepdims=True) for pg in p_gs]
        dbetas = [jnp.sum(pb, axis=1, keepdims=True) for pb in p_betas]
        kd_tots = [jnp.sum(jnp.sum(pk, axis=1, keepdims=True), axis=0, keepdims=True) for pk in p_kds]
        dgcs = [rs - _row_to_col(cs, *c["masks"][0:2]) + gs for rs, cs, gs, c in zip(row_sums, col_sums, g_sums, chains)]
        dgrs = [_col_to_row(dgc, *c["masks"][0:2]) for dgc, c in zip(dgcs, chains)]
        draws = [jnp.sum(jnp.where(jnp.logical_not(c["masks"][3]), dgr, 0.0), axis=1, keepdims=True) + c["dglast"] + kt
                 for dgr, kt, c in zip(dgrs, kd_tots, chains)]
        for c, draw, dbeta in zip(chains, draws, dbetas):
            ch = c["ch"]
            dgates[c["cc"]] = dgates[c["cc"]] + jnp.where(lane == ch, draw, 0.0) + jnp.where(lane == 8 + ch, dbeta, 0.0)
        for cc in range(LOCAL_CHUNKS):
            rows = slice(cc * CHUNK, (cc + 1) * CHUNK)
            for h in range(GDN_HEADS):
                for part in range(3):
                    cols = slice(part * hd + h * GDN_DIM, part * hd + (h + 1) * GDN_DIM)
                    dx_ref[rows, cols] = grads[cc * N_CHAINS + h][part] + grads[cc * N_CHAINS + GDN_HEADS + h][part]
            dg_ref[rows, :] = dgates[cc]

    lc = LOCAL_CHUNKS
    all8 = lambda rows, cols: pl.BlockSpec((lc, N_CHAINS, rows, cols), lambda n: (n, 0, 0, 0))
    own4 = lambda rows, cols: pl.BlockSpec((lc, GDN_HEADS, rows, cols), lambda n: (n, 0, 0, 0))
    row4 = pl.BlockSpec((lc, GDN_HEADS, LANES), lambda n: (n, 0, 0))
    vn_f, vn_b = saved["vn"]
    dvn_f, dvn_b, dw_f, dw_b, dqg_f, dqg_b, dkd_f, dkd_b, dgl_f, dgl_b = scan
    return _grid_call(
        body, "gdn_local_bwd", nc // lc,
        [pl.BlockSpec((lc * CHUNK, 3 * hd), lambda n: (n, 0)), pl.BlockSpec((lc * CHUNK, LANES), lambda n: (n, 0)),
         pl.BlockSpec((lc, 16, CHUNK), lambda n: (n, 0, 0)), pl.BlockSpec((lc * CHUNK, hd), lambda n: (n, 0)),
         all8(CHUNK, CHUNK)] + [own4(CHUNK, GDN_DIM)] * 10 + [row4, row4],
        [pl.BlockSpec((lc * CHUNK, 3 * hd), lambda n: (n, 0)), pl.BlockSpec((lc * CHUNK, LANES), lambda n: (n, 0))],
        [jax.ShapeDtypeStruct((t, 3 * hd), F32), jax.ShapeDtypeStruct((t, LANES), F32)],
        (qkvc, gb, gbt, do, saved["tm"], vn_f, vn_b, dvn_f, dvn_b, dw_f, dw_b, dqg_f, dqg_b, dkd_f, dkd_b, dgl_f, dgl_b),
        exchange=exchange)


def _gdn_post_fwd(of, ob, z, gw, tm):
    t, hd = of.shape

    def body(of_ref, ob_ref, z_ref, w_ref, o_ref):
        for h in range(GDN_HEADS):
            cols = slice(h * GDN_DIM, (h + 1) * GDN_DIM)
            o = of_ref[:, cols] + ob_ref[:, cols]
            zv = z_ref[:, cols]
            o_ref[:, cols] = (o * _rstd(o) * w_ref[...] * (zv * _sigmoid(zv))).astype(BF16)

    row = pl.BlockSpec((tm, hd), lambda i: (i, 0))
    return pl.pallas_call(
        body, name="gdn_post_fwd", grid=(t // tm,),
        in_specs=[row, row, row, _resident((1, GDN_DIM))],
        out_specs=row, out_shape=jax.ShapeDtypeStruct((t, hd), BF16),
        compiler_params=_params(("arbitrary",), VMEM_LIMIT),
    )(of, ob, z, gw)


def _gdn_post_bwd(doa, of, ob, z, gw, tm):
    t, hd = of.shape

    def body(d_ref, of_ref, ob_ref, z_ref, w_ref, do_ref, dz_ref, dw_ref):
        @pl.when(pl.program_id(0) == 0)
        def _():
            dw_ref[...] = jnp.zeros_like(dw_ref)

        dw = jnp.zeros((1, GDN_DIM), F32)
        for h in range(GDN_HEADS):
            cols = slice(h * GDN_DIM, (h + 1) * GDN_DIM)
            o = of_ref[:, cols] + ob_ref[:, cols]
            zv = z_ref[:, cols]
            dv = d_ref[:, cols]
            r = _rstd(o)
            sg = _sigmoid(zv)
            on = o * r * w_ref[...]
            dz_ref[:, cols] = (dv * on * (sg * (1.0 + zv * (1.0 - sg)))).astype(BF16)
            dxr, dwh = _rms_bwd(o, r, w_ref[...], dv * (zv * sg))
            do_ref[:, cols] = dxr
            dw = dw + dwh
        dw_ref[...] += dw

    row = pl.BlockSpec((tm, hd), lambda i: (i, 0))
    return pl.pallas_call(
        body, name="gdn_post_bwd", grid=(t // tm,),
        in_specs=[row, row, row, row, _resident((1, GDN_DIM))],
        out_specs=[row, row, pl.BlockSpec((1, GDN_DIM), lambda i: (0, 0))],
        out_shape=[jax.ShapeDtypeStruct((t, hd), F32), jax.ShapeDtypeStruct((t, hd), BF16),
                   jax.ShapeDtypeStruct((1, GDN_DIM), F32)],
        compiler_params=_params(("arbitrary",), VMEM_LIMIT),
    )(doa, of, ob, z, gw)


SWA_W = SWA_HEADS * SWA_DIM
QBLK = 128
KWIN = QBLK + 2 * RADIUS
WIN_OFFSETS = (0, RADIUS, 2 * RADIUS)


def _t5_bucket(rel):
    nb = REL_BUCKETS // 2
    bucket = (rel > 0).astype(np.int32) * nb
    n = np.abs(rel)
    max_exact = nb // 2
    large = max_exact + (np.log(np.maximum(n, 1) / max_exact)
                         / math.log(REL_MAX_DISTANCE / max_exact) * (nb - max_exact)).astype(np.int32)
    large = np.minimum(large, nb - 1)
    return (bucket + np.where(n < max_exact, n, large)).astype(np.int32)


def _band_tables(dilation):
    a = np.arange(QBLK)
    b = np.arange(KWIN)
    rel = np.stack([b[None, :] - w0 - a[:, None] for w0 in WIN_OFFSETS])
    return _t5_bucket(rel * dilation), np.abs(rel) <= RADIUS


def _bias_table(rel_bias, idx, valid):
    onehot = (jnp.arange(REL_BUCKETS, dtype=jnp.int32)[:, None] == jnp.asarray(idx.reshape(1, -1))).astype(F32)
    tab = jnp.dot(rel_bias.T, onehot, precision=HIGHEST)
    tab = jnp.where(jnp.asarray(valid.reshape(1, -1)), tab, NEG_BIG)
    return tab.reshape((SWA_HEADS,) + idx.shape), onehot


def _head_mean(x2, bd_ref):
    return _dot_hi(x2, bd_ref[...])


VIEW_DILATIONS = tuple(d for _, d in PATTERNS if d > 1)


def _view_spec(tm, d):
    return pl.BlockSpec((tm // d, d * SWA_W), lambda i: (i, 0))


def _view_shape(t, d, dtype):
    return jax.ShapeDtypeStruct((t // d, d * SWA_W), dtype)


N_GROUPS = SWA_W // LANES


def _to_view(src_ref, idx, dst_ref, d, rows):
    for r in range(d):
        for g in range(N_GROUPS):
            cols = slice(r * SWA_W + g * LANES, r * SWA_W + (g + 1) * LANES)
            dst_ref[:, cols] = src_ref[idx, g, pl.ds(r, rows // d, stride=d), :].astype(dst_ref.dtype)


def _from_view(src_ref, dst_ref, idx, d, rows):
    for r in range(d):
        for g in range(N_GROUPS):
            cols = slice(r * SWA_W + g * LANES, r * SWA_W + (g + 1) * LANES)
            dst_ref[idx, g, pl.ds(r, rows // d, stride=d), :] = src_ref[:, cols]


def _swa_prep_fwd(qkvb, qw, kw, bd, tm):
    t = qkvb.shape[0]

    def body(x_ref, qw_ref, kw_ref, bd_ref, *rest):
        outs, sc = rest[:-1], rest[-1]
        for gidx in range(N_GROUPS):
            cols = slice(gidx * LANES, (gidx + 1) * LANES)
            xq = x_ref[:, cols]
            sc[0, gidx] = xq * lax.rsqrt(_head_mean(xq * xq, bd_ref) + EPS) * qw_ref[:, cols] * (SWA_DIM ** -0.5)
            xk = x_ref[:, SWA_W + gidx * LANES:SWA_W + (gidx + 1) * LANES]
            sc[1, gidx] = xk * lax.rsqrt(_head_mean(xk * xk, bd_ref) + EPS) * kw_ref[:, cols]
            sc[2, gidx] = x_ref[:, 2 * SWA_W + gidx * LANES:2 * SWA_W + (gidx + 1) * LANES]
            for i in range(3):
                outs[i][:, cols] = sc[i, gidx].astype(BF16)
        for i in range(3):
            for n, d in enumerate(VIEW_DILATIONS):
                _to_view(sc, i, outs[3 * (n + 1) + i], d, tm)

    return pl.pallas_call(
        body, name="swa_prep_fwd", grid=(t // tm,),
        in_specs=[pl.BlockSpec((tm, 3 * SWA_W), lambda i: (i, 0)), _resident((1, SWA_W)), _resident((1, SWA_W)),
                  _resident((LANES, LANES))],
        out_specs=[_view_spec(tm, d) for d in (1,) + VIEW_DILATIONS for _ in range(3)],
        out_shape=[_view_shape(t, d, BF16) for d in (1,) + VIEW_DILATIONS for _ in range(3)],
        scratch_shapes=[pltpu.VMEM((3, N_GROUPS, tm, LANES), F32)],
        compiler_params=_params(("arbitrary",), VMEM_LIMIT),
    )(qkvb, qw, kw, bd)


def _swa_prep_bwd(qkvb, qw, kw, bd, grads, tm):
    t = qkvb.shape[0]

    def body(x_ref, qw_ref, kw_ref, bd_ref, *rest):
        parts, (dx_ref, dqw_ref, dkw_ref, sc) = rest[:9], rest[9:]
        @pl.when(pl.program_id(0) == 0)
        def _():
            dqw_ref[...] = jnp.zeros_like(dqw_ref)
            dkw_ref[...] = jnp.zeros_like(dkw_ref)

        for i in range(3):
            for n, d in enumerate(VIEW_DILATIONS):
                _from_view(parts[3 * (n + 1) + i], sc, 2 * i + n, d, tm)
        for gidx in range(N_GROUPS):
            cols = slice(gidx * LANES, (gidx + 1) * LANES)
            for i, base, w_ref, dw_ref, scale in ((0, 0, qw_ref, dqw_ref, SWA_DIM ** -0.5),
                                                  (1, SWA_W, kw_ref, dkw_ref, 1.0)):
                xv = x_ref[:, base + gidx * LANES:base + (gidx + 1) * LANES]
                dy = (parts[i][:, cols] + sc[2 * i, gidx] + sc[2 * i + 1, gidx]) * scale
                r = lax.rsqrt(_head_mean(xv * xv, bd_ref) + EPS)
                xhat = xv * r
                dxh = dy * w_ref[:, cols]
                dx = r * (dxh - xhat * _head_mean(dxh * xhat, bd_ref))
                dx_ref[:, base + gidx * LANES:base + (gidx + 1) * LANES] = dx.astype(BF16)
                dw_ref[:, cols] += jnp.sum(dy * xhat, axis=0, keepdims=True)
            dx_ref[:, 2 * SWA_W + gidx * LANES:2 * SWA_W + (gidx + 1) * LANES] = (
                parts[2][:, cols] + sc[4, gidx] + sc[5, gidx]).astype(BF16)

    wrow = pl.BlockSpec((1, SWA_W), lambda i: (0, 0))
    return pl.pallas_call(
        body, name="swa_prep_bwd", grid=(t // tm,),
        in_specs=[pl.BlockSpec((tm, 3 * SWA_W), lambda i: (i, 0)), _resident((1, SWA_W)), _resident((1, SWA_W)),
                  _resident((LANES, LANES))] + [_view_spec(tm, d) for d in (1,) + VIEW_DILATIONS for _ in range(3)],
        out_specs=[pl.BlockSpec((tm, 3 * SWA_W), lambda i: (i, 0)), wrow, wrow],
        out_shape=[jax.ShapeDtypeStruct((t, 3 * SWA_W), BF16), jax.ShapeDtypeStruct((1, SWA_W), F32),
                   jax.ShapeDtypeStruct((1, SWA_W), F32)],
        scratch_shapes=[pltpu.VMEM((6, N_GROUPS, tm, LANES), F32)],
        compiler_params=_params(("arbitrary",), VMEM_LIMIT),
    )(qkvb, qw, kw, bd, *grads)


def _aligned(v, m):
    return v if isinstance(v, int) else pl.multiple_of(v, m)


BAND_GROUP = 2


def _band_loop(nsub, length, step):
    step([(0, 0)], 0)
    if nsub > 2:
        assert (nsub - 2) % BAND_GROUP == 0

        def inner(i, carry):
            s0 = 1 + i * BAND_GROUP
            step([(s0 + e, pl.multiple_of((s0 + e) * QBLK - RADIUS, RADIUS)) for e in range(BAND_GROUP)], 1)
            return carry
        lax.fori_loop(0, (nsub - 2) // BAND_GROUP, inner, 0)
    step([(nsub - 1, length - KWIN)], 2)


def _head_select(lane, a0, a1):
    return jnp.where(lane < SWA_DIM, a0, a1)


def _swa_fwd(qv, kv, vv, bias, dilation, name):
    length = qv.shape[0]
    nsub = length // QBLK
    assert nsub >= 2 and length % QBLK == 0

    def body(q_ref, k_ref, v_ref, b_ref, o_ref, l_ref):
        lane = lax.broadcasted_iota(jnp.int32, (QBLK, LANES), 1)

        def step(blocks, var):
            items = []
            for s, ws in blocks:
                rows = pl.ds(_aligned(s * QBLK, QBLK), QBLK)
                q, kk, vw = q_ref[rows, :], k_ref[pl.ds(ws, KWIN), :], v_ref[pl.ds(ws, KWIN), :]
                for hh in range(2):
                    items.append((hh, jnp.where((lane < SWA_DIM) == (hh == 0), q, jnp.zeros_like(q)), kk, vw))
            lgs = [_dot_nt(qh, kk) + b_ref[hh, var] for hh, qh, kk, _ in items]
            ms = [jnp.max(lg, axis=-1, keepdims=True) for lg in lgs]
            ps = [jnp.exp(lg - m) for lg, m in zip(lgs, ms)]
            dens = [jnp.sum(p, axis=-1, keepdims=True) for p in ps]
            pvs = [_dot(p, it[3]) for p, it in zip(ps, items)]
            for n, (s, _) in enumerate(blocks):
                rows = pl.ds(_aligned(s * QBLK, QBLK), QBLK)
                o0, o1 = (pvs[2 * n + hh] / dens[2 * n + hh] for hh in range(2))
                l0, l1 = (ms[2 * n + hh] + jnp.log(dens[2 * n + hh]) for hh in range(2))
                o_ref[rows, :] = _head_select(lane, o0, o1)
                l_ref[rows, :] = _head_select(lane, l0, l1)

        _band_loop(nsub, length, step)

    blk = pl.BlockSpec((length, LANES), lambda hp, r: (0, r * (SWA_W // LANES) + hp))
    shp = jax.ShapeDtypeStruct(qv.shape, F32)
    return pl.pallas_call(
        body, name=name, grid=(SWA_W // LANES, dilation),
        in_specs=[blk, blk, blk, pl.BlockSpec((2, 3, QBLK, KWIN), lambda hp, r: (hp, 0, 0, 0))],
        out_specs=[blk, blk], out_shape=[shp, shp],
        compiler_params=_params(("arbitrary", "arbitrary"), VMEM_LIMIT),
    )(qv, kv, vv, bias)


def _swa_combine(os_, ls_, tm):
    t = os_[0].shape[0]

    def body(o0, o1, o2, l0, l1, l2, o_ref, ob_ref, la_ref, lb_ref, lc_ref, sc):
        for n, d in enumerate(VIEW_DILATIONS):
            _from_view((o1, o2)[n], sc, n, d, tm)
            _from_view((l1, l2)[n], sc, 2 + n, d, tm)
        for g in range(N_GROUPS):
            cols = slice(g * LANES, (g + 1) * LANES)
            la, lb, lc = l0[:, cols], sc[2, g], sc[3, g]
            m = jnp.maximum(jnp.maximum(la, lb), lc)
            tot = m + jnp.log(jnp.exp(la - m) + jnp.exp(lb - m) + jnp.exp(lc - m))
            o = jnp.exp(la - tot) * o0[:, cols] + jnp.exp(lb - tot) * sc[0, g] + jnp.exp(lc - tot) * sc[1, g]
            o_ref[:, cols] = o
            ob_ref[:, cols] = o.astype(BF16)
            la_ref[:, cols] = tot
            sc[4, g] = tot
        for n, d in enumerate(VIEW_DILATIONS):
            _to_view(sc, 4, (lb_ref, lc_ref)[n], d, tm)

    specs = [_view_spec(tm, d) for d in (1,) + VIEW_DILATIONS]
    return pl.pallas_call(
        body, name="swa_combine", grid=(t // tm,), in_specs=specs + specs, out_specs=[specs[0], specs[0]] + specs,
        out_shape=[jax.ShapeDtypeStruct((t, SWA_W), F32), jax.ShapeDtypeStruct((t, SWA_W), BF16)]
                  + [_view_shape(t, d, F32) for d in (1,) + VIEW_DILATIONS],
        scratch_shapes=[pltpu.VMEM((5, N_GROUPS, tm, LANES), F32)],
        compiler_params=_params(("arbitrary",), VMEM_LIMIT),
    )(*os_, *ls_)


def _swa_bwd_prep(do, o, bd, tm):
    t = do.shape[0]

    def body(d_ref, o_ref, bd_ref, dd1, dd4, dd16, db1, db4, db16, sc):
        for gidx in range(N_GROUPS):
            cols = slice(gidx * LANES, (gidx + 1) * LANES)
            dv = d_ref[:, cols]
            dd = _head_mean(dv * o_ref[:, cols], bd_ref) * float(SWA_DIM)
            sc[0, gidx] = dd
            sc[1, gidx] = dv
            dd1[:, cols] = dd
            db1[:, cols] = dv.astype(BF16)
        for n, d in enumerate(VIEW_DILATIONS):
            _to_view(sc, 0, (dd4, dd16)[n], d, tm)
            _to_view(sc, 1, (db4, db16)[n], d, tm)

    specs = [_view_spec(tm, d) for d in (1,) + VIEW_DILATIONS]
    return pl.pallas_call(
        body, name="swa_bwd_prep", grid=(t // tm,), in_specs=[specs[0], specs[0], _resident((LANES, LANES))],
        out_specs=specs + specs,
        out_shape=[_view_shape(t, d, F32) for d in (1,) + VIEW_DILATIONS]
                  + [_view_shape(t, d, BF16) for d in (1,) + VIEW_DILATIONS],
        scratch_shapes=[pltpu.VMEM((2, N_GROUPS, tm, LANES), F32)],
        compiler_params=_params(("arbitrary",), VMEM_LIMIT),
    )(do, o, bd)


def _swa_bwd(qv, kv, vv, dov, lv, ddv, bias_a, dilation, name):
    length = qv.shape[0]
    nsub = length // QBLK
    single = pl.Buffered(1) if dilation == 1 else None

    def body(q_ref, k_ref, v_ref, do_ref, l_ref, dd_ref, ba_ref, dq_ref, dk_ref, dv_ref, db_ref):
        @pl.when(pl.program_id(1) == 0)
        def _():
            db_ref[...] = jnp.zeros_like(db_ref)

        lane = lax.broadcasted_iota(jnp.int32, (QBLK, LANES), 1)
        lanew = lax.broadcasted_iota(jnp.int32, (KWIN, LANES), 1)

        def step(blocks, var):
            items = []
            for s, ws in blocks:
                rows = pl.ds(_aligned(s * QBLK, QBLK), QBLK)
                win = pl.ds(ws, KWIN)
                q, dov_ = q_ref[rows, :], do_ref[rows, :]
                kk, vw = k_ref[win, :], v_ref[win, :]
                lse, dd = l_ref[rows, :], dd_ref[rows, :]
                for hh in range(2):
                    mine = (lane < SWA_DIM) == (hh == 0)
                    col = slice(hh * SWA_DIM, hh * SWA_DIM + 1)
                    items.append((hh, jnp.where(mine, q, jnp.zeros_like(q)), jnp.where(mine, dov_, jnp.zeros_like(dov_)),
                                  kk, vw, lse[:, col], dd[:, col], q, dov_))
            lgs = [_dot_nt(it[1], it[3]) + ba_ref[it[0], var] for it in items]
            dps = [_dot_nt(it[2], it[4]) for it in items]
            ps = [jnp.exp(lg - it[5]) for lg, it in zip(lgs, items)]
            dss = [p * (dp - it[6]) for p, dp, it in zip(ps, dps, items)]
            dqs = [_dot(ds, it[3]) for ds, it in zip(dss, items)]
            dks = [_dot_tn(ds, it[7]) for ds, it in zip(dss, items)]
            dvs = [_dot_tn(p, it[8]) for p, it in zip(ps, items)]
            for n, (s, ws) in enumerate(blocks):
                rows = pl.ds(_aligned(s * QBLK, QBLK), QBLK)
                win = pl.ds(ws, KWIN)
                dq_ref[rows, :] = _head_select(lane, dqs[2 * n], dqs[2 * n + 1])
                dk_ref[win, :] += _head_select(lanew, dks[2 * n], dks[2 * n + 1])
                dv_ref[win, :] += _head_select(lanew, dvs[2 * n], dvs[2 * n + 1])
            for hh in range(2):
                tot = dss[hh]
                for n in range(1, len(blocks)):
                    tot = tot + dss[2 * n + hh]
                db_ref[hh, var] += tot

        dk_ref[...] = jnp.zeros_like(dk_ref)
        dv_ref[...] = jnp.zeros_like(dv_ref)
        _band_loop(nsub, length, step)

    imap = lambda hp, r: (0, r * (SWA_W // LANES) + hp)
    blk_in = pl.BlockSpec((length, LANES), imap, pipeline_mode=single)
    blk_out = pl.BlockSpec((length, LANES), imap)
    shp = jax.ShapeDtypeStruct(qv.shape, F32)
    return pl.pallas_call(
        body, name=name, grid=(SWA_W // LANES, dilation),
        in_specs=[blk_in] * 6 + [pl.BlockSpec((2, 3, QBLK, KWIN), lambda hp, r: (hp, 0, 0, 0))],
        out_specs=[blk_out, blk_out, blk_out, pl.BlockSpec((2, 3, QBLK, KWIN), lambda hp, r: (hp, 0, 0, 0))],
        out_shape=[shp, shp, shp, jax.ShapeDtypeStruct((SWA_HEADS, 3, QBLK, KWIN), F32)],
        compiler_params=_params(("arbitrary", "arbitrary"), VMEM_LIMIT),
    )(qv, kv, vv, dov, lv, ddv, bias_a)


def _bias_grad(ds2, onehot, tk):
    n = ds2.shape[1]
    nk = n // tk

    def body(a_ref, b_ref, o_ref):
        @pl.when(pl.program_id(0) == 0)
        def _():
            o_ref[...] = jnp.zeros_like(o_ref)

        o_ref[...] += lax.dot_general(a_ref[...], b_ref[...], (((1,), (1,)), ((), ())), precision=HIGHEST,
                                      preferred_element_type=F32)

    return pl.pallas_call(
        body, name="bias_grad", grid=(nk,),
        in_specs=[pl.BlockSpec((SWA_HEADS, tk), lambda k: (0, k)), pl.BlockSpec((REL_BUCKETS, tk), lambda k: (0, k))],
        out_specs=pl.BlockSpec((SWA_HEADS, REL_BUCKETS), lambda k: (0, 0)),
        out_shape=jax.ShapeDtypeStruct((SWA_HEADS, REL_BUCKETS), F32),
        compiler_params=_params(("arbitrary",), VMEM_LIMIT),
    )(ds2, onehot)


def _swa_branch_fwd(qkvb, qw_t, kw_t, rel_bias, bd, tm):
    qkv = _swa_prep_fwd(qkvb, qw_t, kw_t, bd, tm)
    os_, ls_, tabs = [], [], []
    for n, (_, d) in enumerate(PATTERNS):
        bias, onehot = _bias_table(rel_bias, *_band_tables(d))
        o_p, l_p = _swa_fwd(*qkv[3 * n:3 * n + 3], bias, d, f"swa_fwd_d{d}")
        os_.append(o_p)
        ls_.append(l_p)
        tabs.append((bias, onehot))
    o, o16, *lses = _swa_combine(os_, ls_, tm)
    return o, o16, (qkv, lses, tabs)


def _swa_branch_bwd(do, o, saved, qkvb, qw_t, kw_t, bd, tm):
    qkv, lses, tabs = saved
    prep = _swa_bwd_prep(do, o, bd, tm)
    grads, dss, ohs = [], [], []
    for n, ((_, d), (bias, onehot)) in enumerate(zip(PATTERNS, tabs)):
        dq, dk, dv, ds = _swa_bwd(*qkv[3 * n:3 * n + 3], prep[3 + n], lses[n], prep[n], bias, d, f"swa_bwd_d{d}")
        grads += [dq, dk, dv]
        dss.append(ds.reshape(SWA_HEADS, -1))
        ohs.append(onehot)
    dqkvb, dqw, dkw = _swa_prep_bwd(qkvb, qw_t, kw_t, bd, grads, tm)
    dbias = _bias_grad(jnp.concatenate(dss, axis=1), jnp.concatenate(ohs, axis=1), 8192)
    fold = lambda w: jnp.sum(w.reshape(SWA_HEADS, SWA_DIM), axis=0)
    return dqkvb, fold(dqw), fold(dkw), dbias.T


def _mesh_pos():
    return lax.axis_index("x"), lax.axis_index("y"), lax.axis_index("c")


def _other_chips(x, y):
    return [(1 - x, y), (x, 1 - y), (1 - x, 1 - y)]


def _remote(src, dst, send_sem, recv_sem, device):
    return pltpu.make_async_remote_copy(src_ref=src, dst_ref=dst, send_sem=send_sem, recv_sem=recv_sem,
                                        device_id=device, device_id_type=MESH)


def _all_gather(xs):
    n = len(xs)

    def body(*refs):
        ins, outs = refs[:n], refs[n:2 * n]
        send_sems, recv_sems = refs[2 * n:]
        x, y, c = _mesh_pos()
        me = 2 * x + y
        chips = _other_chips(x, y)
        halves = []
        sends = []
        for a in range(n):
            h = ins[a].shape[0] // 2
            mine, other = pl.ds(c * h, h), pl.ds((1 - c) * h, h)
            halves.append((mine, other))
            for j, chip in enumerate(chips):
                cp = _remote(ins[a].at[mine], outs[a].at[me, mine], send_sems.at[a, j], recv_sems.at[a, j], (*chip, c))
                cp.start()
                sends.append(cp)
        for a in range(n):
            mine, _ = halves[a]
            for j, chip in enumerate(chips):
                src = 2 * chip[0] + chip[1]
                landed = outs[a].at[src, mine]
                _remote(landed, landed, send_sems.at[a, j], recv_sems.at[a, j], (x, y, c)).wait_recv()
                fwd = _remote(landed, landed, send_sems.at[a, 3 + j], recv_sems.at[a, 3 + j], (x, y, 1 - c))
                fwd.start()
                sends.append(fwd)
        for a in range(n):
            _, other = halves[a]
            for j, chip in enumerate(chips):
                src = 2 * chip[0] + chip[1]
                landed = outs[a].at[src, other]
                _remote(landed, landed, send_sems.at[a, 3 + j], recv_sems.at[a, 3 + j], (x, y, c)).wait_recv()
        for cp in sends:
            cp.wait_send()

    outs = pl.pallas_call(
        body, name="all_gather_weights",
        in_specs=[ANY] * n, out_specs=[ANY] * n,
        out_shape=[jax.ShapeDtypeStruct((N_SHARDS,) + a.shape, a.dtype) for a in xs],
        scratch_shapes=[pltpu.SemaphoreType.DMA((n, 6)), pltpu.SemaphoreType.DMA((n, 6))],
    )(*xs)
    me = 2 * lax.axis_index("x") + lax.axis_index("y")
    return [lax.dynamic_update_slice_in_dim(o, a[None], me, 0) for o, a in zip(outs, xs)]


def _rs_pair(gs):
    n = len(gs)

    def body(*refs):
        ins, lands = refs[:n], refs[n:2 * n]
        send_sems, recv_sems = refs[2 * n:]
        x, y, c = _mesh_pos()
        cps = []
        for a in range(n):
            h = ins[a].shape[1] // 2
            cp = _remote(ins[a].at[:, pl.ds((1 - c) * h, h), :], lands[a], send_sems.at[a], recv_sems.at[a],
                         (x, y, 1 - c))
            cp.start()
            cps.append(cp)
        for cp in cps:
            cp.wait()

    half = [jax.ShapeDtypeStruct((N_SHARDS, g.shape[1] // 2, g.shape[2]), g.dtype) for g in gs]
    lands = pl.pallas_call(
        body, name="rs_pair", in_specs=[ANY] * n, out_specs=[ANY] * n, out_shape=half,
        scratch_shapes=[pltpu.SemaphoreType.DMA((n,)), pltpu.SemaphoreType.DMA((n,))],
    )(*gs)
    c = lax.axis_index("c")
    owns = [lax.dynamic_slice_in_dim(g, c * (g.shape[1] // 2), g.shape[1] // 2, 1) for g in gs]
    return owns + list(lands)


def _rs_chips(ss):
    n = len(ss)

    def body(*refs):
        ins, outs = refs[:n], refs[n:2 * n]
        send_sems, recv_sems = refs[2 * n:]
        x, y, c = _mesh_pos()
        me = 2 * x + y
        chips = _other_chips(x, y)
        cps = []
        for a in range(n):
            for j, chip in enumerate(chips):
                dst_chip = 2 * chip[0] + chip[1]
                cp = _remote(ins[a].at[dst_chip], outs[a].at[me], send_sems.at[a, j], recv_sems.at[a, j], (*chip, c))
                cp.start()
                cps.append(cp)
        for a in range(n):
            for j, chip in enumerate(chips):
                src = 2 * chip[0] + chip[1]
                _remote(outs[a].at[src], outs[a].at[src], send_sems.at[a, j], recv_sems.at[a, j], (x, y, c)).wait_recv()
        for cp in cps:
            cp.wait_send()

    outs = pl.pallas_call(
        body, name="rs_chips", in_specs=[ANY] * n, out_specs=[ANY] * n,
        out_shape=[jax.ShapeDtypeStruct(s.shape, s.dtype) for s in ss],
        scratch_shapes=[pltpu.SemaphoreType.DMA((n, 3)), pltpu.SemaphoreType.DMA((n, 3))],
    )(*ss)
    me = 2 * lax.axis_index("x") + lax.axis_index("y")
    return [lax.dynamic_update_slice_in_dim(o, lax.dynamic_slice_in_dim(s, me, 1, 0), me, 0) for o, s in zip(outs, ss)]


def _rs_join(fs):
    n = len(fs)

    def body(*refs):
        ins, outs = refs[:n], refs[n:2 * n]
        send_sems, recv_sems = refs[2 * n:]
        x, y, c = _mesh_pos()
        cps = []
        for a in range(n):
            h = ins[a].shape[0]
            cp = _remote(ins[a], outs[a].at[pl.ds(c * h, h)], send_sems.at[a], recv_sems.at[a], (x, y, 1 - c))
            cp.start()
            cps.append(cp)
        for cp in cps:
            cp.wait()

    outs = pl.pallas_call(
        body, name="rs_join", in_specs=[ANY] * n, out_specs=[ANY] * n,
        out_shape=[jax.ShapeDtypeStruct((2 * f.shape[0], f.shape[1]), f.dtype) for f in fs],
        scratch_shapes=[pltpu.SemaphoreType.DMA((n,)), pltpu.SemaphoreType.DMA((n,))],
    )(*fs)
    c = lax.axis_index("c")
    return [lax.dynamic_update_slice_in_dim(o, f, c * f.shape[0], 0) for o, f in zip(outs, fs)]


def _gather_exchange(xs):
    def start(cin, cout, send_sems, recv_sems):
        x, y, c = _mesh_pos()
        me = 2 * x + y
        for a, (src, dst) in enumerate(zip(cin, cout)):
            h = src.shape[0] // 2
            mine = pl.ds(c * h, h)
            for j, chip in enumerate(_other_chips(x, y)):
                _remote(src.at[mine], dst.at[me, mine], send_sems.at[a, j], recv_sems.at[a, j], (*chip, c)).start()

    def finish(cin, cout, send_sems, recv_sems):
        x, y, c = _mesh_pos()
        for a, dst in enumerate(cout):
            h = dst.shape[1] // 2
            for j, chip in enumerate(_other_chips(x, y)):
                landed = dst.at[2 * chip[0] + chip[1], pl.ds(c * h, h)]
                _remote(landed, landed, send_sems.at[a, j], recv_sems.at[a, j], (x, y, c)).wait()

    return _Exchange(tuple(xs), tuple(jax.ShapeDtypeStruct((N_SHARDS,) + a.shape, a.dtype) for a in xs), start, finish)


def _gather_forward(gs, xs):
    n = len(gs)

    def body(*refs):
        outs = refs[n:2 * n]
        send_sems, recv_sems = refs[2 * n:]
        x, y, c = _mesh_pos()
        chips = _other_chips(x, y)
        cps = []
        for a in range(n):
            h = outs[a].shape[1] // 2
            for j, chip in enumerate(chips):
                landed = outs[a].at[2 * chip[0] + chip[1], pl.ds(c * h, h)]
                cp = _remote(landed, landed, send_sems.at[a, j], recv_sems.at[a, j], (x, y, 1 - c))
                cp.start()
                cps.append(cp)
        for a in range(n):
            h = outs[a].shape[1] // 2
            for j, chip in enumerate(chips):
                other = outs[a].at[2 * chip[0] + chip[1], pl.ds((1 - c) * h, h)]
                _remote(other, other, send_sems.at[a, j], recv_sems.at[a, j], (x, y, c)).wait_recv()
        for cp in cps:
            cp.wait_send()

    outs = pl.pallas_call(
        body, name="gather_forward", in_specs=[ANY] * n, out_specs=[ANY] * n,
        out_shape=[jax.ShapeDtypeStruct(g.shape, g.dtype) for g in gs],
        input_output_aliases={i: i for i in range(n)},
        scratch_shapes=[pltpu.SemaphoreType.DMA((n, 3)), pltpu.SemaphoreType.DMA((n, 3))],
    )(*gs)
    me = 2 * lax.axis_index("x") + lax.axis_index("y")
    return [lax.dynamic_update_slice_in_dim(o, a[None], me, 0) for o, a in zip(outs, xs)]


def _scatter_exchange(ss):
    def start(cin, cout, send_sems, recv_sems):
        x, y, c = _mesh_pos()
        me = 2 * x + y
        for a, (src, dst) in enumerate(zip(cin, cout)):
            for j, chip in enumerate(_other_chips(x, y)):
                _remote(src.at[2 * chip[0] + chip[1]], dst.at[me], send_sems.at[a, j], recv_sems.at[a, j],
                        (*chip, c)).start()

    def finish(cin, cout, send_sems, recv_sems):
        x, y, c = _mesh_pos()
        for a, dst in enumerate(cout):
            for j, chip in enumerate(_other_chips(x, y)):
                slot = dst.at[2 * chip[0] + chip[1]]
                _remote(slot, slot, send_sems.at[a, j], recv_sems.at[a, j], (x, y, c)).wait()

    return _Exchange(tuple(ss), tuple(jax.ShapeDtypeStruct(s.shape, s.dtype) for s in ss), start, finish)


def _own_slots(slots, ss):
    me = 2 * lax.axis_index("x") + lax.axis_index("y")
    return [lax.dynamic_update_slice_in_dim(o, lax.dynamic_slice_in_dim(s, me, 1, 0), me, 0) for o, s in zip(slots, ss)]


def _add_pair(a, b, name):
    nj, h, c = a.shape

    def body(a_ref, b_ref, o_ref):
        o_ref[...] = (a_ref[...].astype(F32) + b_ref[...].astype(F32)).astype(BF16)

    blk = pl.BlockSpec((1, h, c), lambda j: (j, 0, 0))
    return pl.pallas_call(body, name=name, grid=(nj,), in_specs=[blk, blk], out_specs=blk,
                          out_shape=jax.ShapeDtypeStruct(a.shape, BF16),
                          compiler_params=_params(("arbitrary",), VMEM_LIMIT))(a, b)


def _sum_slots(l2, name):
    nj, h, c = l2.shape
    th = h // 2 if h % 32 == 0 else h

    def body(i_ref, o_ref):
        acc = i_ref[0].astype(F32)
        for s in range(1, nj):
            acc = acc + i_ref[s].astype(F32)
        o_ref[...] = acc

    return pl.pallas_call(body, name=name, grid=(h // th,),
                          in_specs=[pl.BlockSpec((nj, th, c), lambda i: (0, i, 0))],
                          out_specs=pl.BlockSpec((th, c), lambda i: (i, 0)),
                          out_shape=jax.ShapeDtypeStruct((h, c), F32),
                          compiler_params=_params(("arbitrary",), VMEM_LIMIT))(l2)


def _all_reduce_small(p):
    r = p.shape[0]

    def body(p_ref, o_ref, buf, send_sems, recv_sems):
        x, y, c = _mesh_pos()
        me = 4 * x + 2 * y + c
        buf[me] = p_ref[...]
        cps = []
        k = 0
        for fx in range(2):
            for fy in range(2):
                for fc in range(2):
                    if fx + fy + fc == 0:
                        continue
                    peer = (1 - x if fx else x, 1 - y if fy else y, 1 - c if fc else c)
                    peer_id = 4 * peer[0] + 2 * peer[1] + peer[2]
                    cp = _remote(p_ref, buf.at[me], send_sems.at[k], recv_sems.at[k], peer)
                    cp.start()
                    cps.append((cp, peer_id, k))
                    k += 1
        for cp, peer_id, k in cps:
            _remote(p_ref, buf.at[peer_id], send_sems.at[k], recv_sems.at[k], (x, y, c)).wait_recv()
        for cp, _, _ in cps:
            cp.wait_send()
        acc = buf[0]
        for s in range(1, 8):
            acc = acc + buf[s]
        o_ref[...] = acc

    vm = pl.BlockSpec(memory_space=pltpu.VMEM)
    return pl.pallas_call(
        body, name="all_reduce_small", in_specs=[vm], out_specs=vm,
        out_shape=jax.ShapeDtypeStruct(p.shape, F32),
        scratch_shapes=[pltpu.VMEM((8, r, LANES), F32), pltpu.SemaphoreType.DMA((7,)), pltpu.SemaphoreType.DMA((7,))],
    )(p)


def _adamw(w, g, m, v, name):
    r, c = w.shape
    row_tiles = [d for d in range(8, min(r, 256) + 1, 8) if r % d == 0]
    tr, tc = (max(row_tiles), c) if row_tiles else (r, 256 if c % 256 == 0 else c)
    c1 = 1.0 / (1.0 - ADAM_B1 ** ADAM_STEP)
    c2 = 1.0 / (1.0 - ADAM_B2 ** ADAM_STEP)

    def body(w_ref, g_ref, m_ref, v_ref, d_ref, nm_ref, nv_ref):
        gv = g_ref[...]
        nm = ADAM_B1 * m_ref[...] + (1.0 - ADAM_B1) * gv
        nv = ADAM_B2 * v_ref[...] + (1.0 - ADAM_B2) * (gv * gv)
        d_ref[...] = -ADAM_LR * ((nm * c1) / (jnp.sqrt(nv * c2) + ADAM_EPS) + ADAM_WD * w_ref[...])
        nm_ref[...] = nm
        nv_ref[...] = nv

    blk = pl.BlockSpec((tr, tc), lambda i, j: (i, j))
    shp = jax.ShapeDtypeStruct((r, c), F32)
    return pl.pallas_call(body, name=name, grid=(r // tr, c // tc), in_specs=[blk] * 4, out_specs=[blk] * 3,
                          out_shape=[shp, shp, shp],
                          compiler_params=_params(("arbitrary", "arbitrary"), VMEM_LIMIT))(w, g, m, v)


PACK_UNIT = 8 * LANES


def _pack(arrs):
    parts = []
    for a in arrs:
        f = a.reshape(-1).astype(F32)
        parts.append(jnp.pad(f, (0, (-f.shape[0]) % PACK_UNIT)).reshape(-1, LANES))
    return jnp.concatenate(parts, axis=0)


def _unpack(m, shapes):
    outs, row = [], 0
    for s in shapes:
        n = int(np.prod(s))
        rows = -(-n // PACK_UNIT) * 8
        outs.append(m[row:row + rows].reshape(-1)[:n].reshape(s))
        row += rows
    return outs


WEIGHTS = ["ffn1_norm", "ffn1_w_gate", "ffn1_w_up", "ffn1_w_down", "mix_norm", "w_in", "conv_w", "a_log", "dt_bias",
           "gdn_norm_w", "q_norm_w", "k_norm_w", "rel_bias", "w_out", "ffn2_norm", "ffn2_w_gate", "ffn2_w_up",
           "ffn2_w_down", "final_norm"]
BIG = ["ffn1_w_gate", "ffn1_w_up", "ffn1_w_down", "w_in", "w_out", "ffn2_w_gate", "ffn2_w_up", "ffn2_w_down"]
SMALL = [n for n in WEIGHTS if n not in BIG]
COL_SHARDED = ["ffn1_w_gate", "ffn1_w_up", "ffn2_w_gate", "ffn2_w_up"]
N_IN_COLS = 3600
TM = 256
TE = 512
TK = 2048


def kernel(x, ffn1_norm, ffn1_w_gate, ffn1_w_up, ffn1_w_down, mix_norm, w_in, conv_w, a_log, dt_bias, gdn_norm_w, q_norm_w, k_norm_w, rel_bias, w_out, ffn2_norm, ffn2_w_gate, ffn2_w_up, ffn2_w_down, final_norm, loss_target, m_ffn1_norm, m_ffn1_w_gate, m_ffn1_w_up, m_ffn1_w_down, m_mix_norm, m_w_in, m_conv_w, m_a_log, m_dt_bias, m_gdn_norm_w, m_q_norm_w, m_k_norm_w, m_rel_bias, m_w_out, m_ffn2_norm, m_ffn2_w_gate, m_ffn2_w_up, m_ffn2_w_down, m_final_norm, v_ffn1_norm, v_ffn1_w_gate, v_ffn1_w_up, v_ffn1_w_down, v_mix_norm, v_w_in, v_conv_w, v_a_log, v_dt_bias, v_gdn_norm_w, v_q_norm_w, v_k_norm_w, v_rel_bias, v_w_out, v_ffn2_norm, v_ffn2_w_gate, v_ffn2_w_up, v_ffn2_w_down, v_final_norm):
    p = dict(locals())
    xs, target = x[0], loss_target[0]
    t, d = xs.shape
    nc = t // CHUNK
    tk = min(TK, t)
    me = 2 * lax.axis_index("x") + lax.axis_index("y")

    first = ["ffn1_w_gate", "ffn1_w_up", "ffn1_w_down"]
    later = [n for n in BIG if n not in first] + ["conv_w"]
    local = lambda n, a: a[0].T if n in COL_SHARDED else a[0]
    shards = {n: local(n, p[n]).astype(BF16) for n in BIG}
    shards["conv_w"] = conv_w[0]
    gw = dict(zip(first, _all_gather([shards[n] for n in first])))
    f1 = (gw["ffn1_w_gate"], gw["ffn1_w_up"], gw["ffn1_w_down"])
    (x1, xn1, g1, u1), landed = _ffn_fwd(xs, ffn1_norm, *f1, TM, "ffn1_fwd",
                                         exchange=_gather_exchange([shards[n] for n in later]))
    gw.update(zip(later, _gather_forward(landed, [shards[n] for n in later])))
    w_in_t = jnp.transpose(gw["w_in"], (0, 2, 1)).reshape(N_IN_COLS, d)
    wp = jnp.concatenate([w_in_t[:2048], jnp.pad(w_in_t[2048:2064], ((0, LANES - 16), (0, 0))), w_in_t[2064:]], axis=0)
    w_out_full = gw["w_out"].reshape(d, d)
    conv_rows = conv_w.shape[1]
    cw = jnp.pad(gw["conv_w"].reshape(N_SHARDS * conv_rows, CONV_TAPS).T, ((0, 8 - CONV_TAPS), (0, 0)))
    gp = jnp.pad(jnp.stack([a_log.reshape(8), dt_bias.reshape(8)]), ((0, 6), (0, LANES - 8)))
    gdn_w = gdn_norm_w.reshape(1, GDN_DIM)
    qw_t = jnp.tile(q_norm_w.reshape(1, SWA_DIM), (1, SWA_HEADS))
    kw_t = jnp.tile(k_norm_w.reshape(1, SWA_DIM), (1, SWA_HEADS))
    bd = jnp.asarray(np.kron(np.eye(2), np.full((SWA_DIM, SWA_DIM), 1.0 / SWA_DIM)), F32)
    f2 = (gw["ffn2_w_gate"], gw["ffn2_w_up"], gw["ffn2_w_down"])

    hn, qkva, z, ab, qkvb = _mix_in_fwd(x1, mix_norm, wp, TM)
    qkvc, gb = _gdn_prep_fwd(qkva, cw, ab, gp, TM)
    gbt = jnp.transpose(gb[:, :16].reshape(nc, CHUNK, 16), (0, 2, 1))
    o_f, o_b, gdn_saved = _gdn_fwd(qkvc, gb, gbt)
    oa = _gdn_post_fwd(o_f, o_b, z, gdn_w, TE)
    o_swa, o_swa16, swa_saved = _swa_branch_fwd(qkvb, qw_t, kw_t, rel_bias, bd, TE)
    x2 = _mix_out_fwd(x1, oa, o_swa, w_out_full, TM)
    (x3, xn2, g2, u2), _ = _ffn_fwd(x2, ffn2_norm, *f2, TM, "ffn2_fwd")
    dx3, loss_part, d_final = _final_loss(x3, final_norm, target, TE)

    def pair_sums(partials, tag):
        pair = _rs_pair(partials)
        k = len(partials)
        return [_add_pair(pair[i], pair[k + i], f"rs_add_{tag}{i}") for i in range(k)]

    (dx2, dyh2, dg2, du2, h2, d_nw2), _ = _ffn_bwd_dx(dx3, x2, ffn2_norm, g2, u2, *f2, TM, "ffn2_bwd_dx")
    dwg2 = _matmul_tn(dg2, xn2, tk, "ffn2_dwg")
    dwu2 = _matmul_tn(du2, xn2, tk, "ffn2_dwu")
    dwd2 = _matmul_tn(h2, dyh2, tk, "ffn2_dwd")
    sums_f2 = pair_sums([dwg2, dwu2, dwd2], "a")
    doa, dob, dx2b = _mix_out_bwd(dx2, w_out_full, TM)
    dwo = jnp.concatenate([_matmul_tn(oa, dx2b, tk, "w_out_dw_a")[0], _matmul_tn(o_swa16, dx2b, tk, "w_out_dw_b")[0]],
                          axis=0).reshape(N_SHARDS, d // N_SHARDS, d)
    do_g, dz, d_gdnw = _gdn_post_bwd(doa, o_f, o_b, z, gdn_w, TE)
    (dqkvc, dgates), slots_f2 = _gdn_bwd(qkvc, gb, gbt, do_g, gdn_saved, exchange=_scatter_exchange(sums_f2))
    dqkva, dab, dcw, dgp = _gdn_prep_bwd(qkva, cw, ab, gp, dqkvc, dgates, TM)
    dqkvb, d_qw, d_kw, d_rel = _swa_branch_bwd(dob, o_swa, swa_saved, qkvb, qw_t, kw_t, bd, TE)
    dpieces = (dqkva, dz, dab, dqkvb)
    dx1, d_mixnw = _mix_in_bwd_dx(dx2, x1, mix_norm, dpieces, wp, TM)
    dwp = [_matmul_tn(dp, hn, tk, f"w_in_dw_{i}")[0] for i, dp in enumerate(dpieces)]
    dw_in = jnp.concatenate([dwp[0], dwp[1], dwp[2][:16], dwp[3]], axis=0).reshape(N_SHARDS, N_IN_COLS // N_SHARDS, d)
    dw_in = jnp.transpose(dw_in, (0, 2, 1))
    sums_mix = pair_sums([dw_in, dwo], "b")
    (gx, dyh1, dg1, du1, h1, d_nw1), slots_mix = _ffn_bwd_dx(dx1, xs, ffn1_norm, g1, u1, *f1, TM, "ffn1_bwd_dx",
                                                              exchange=_scatter_exchange(sums_mix))
    dwg1 = _matmul_tn(dg1, xn1, tk, "ffn1_dwg")
    dwu1 = _matmul_tn(du1, xn1, tk, "ffn1_dwu")
    dwd1 = _matmul_tn(h1, dyh1, tk, "ffn1_dwd")
    slots_f1 = _rs_chips(pair_sums([dwg1, dwu1, dwd1], "c"))
    slots = slots_f1 + _own_slots(slots_mix, sums_mix) + _own_slots(slots_f2, sums_f2)
    halves = [_sum_slots(s, f"rs_sum_{i}") for i, s in enumerate(slots)]
    g_big = dict(zip(BIG, _rs_join(halves)))

    small_partial = {"ffn1_norm": d_nw1, "mix_norm": d_mixnw, "a_log": dgp[0, 0:8], "dt_bias": dgp[1, 0:8],
                     "gdn_norm_w": d_gdnw, "q_norm_w": d_qw, "k_norm_w": d_kw, "rel_bias": d_rel,
                     "ffn2_norm": d_nw2, "final_norm": d_final, "conv_w": dcw[0:CONV_TAPS].T}
    red = _all_reduce_small(_pack([small_partial[n] for n in SMALL] + [loss_part[0, 0:1]]))
    full_shapes = [p[n].shape if n != "conv_w" else (N_SHARDS * conv_rows, CONV_TAPS) for n in SMALL]
    red_parts = _unpack(red, full_shapes + [(1,)])
    loss = red_parts[-1].reshape(())
    g_small = dict(zip(SMALL, red_parts[:-1]))
    g_small["conv_w"] = lax.dynamic_slice_in_dim(g_small["conv_w"], me * conv_rows, conv_rows, 0).reshape(conv_w.shape)

    grads, deltas, new_m, new_v = {}, {}, {}, {}
    for n in BIG:
        back = (lambda a: a.T[None]) if n in COL_SHARDED else (lambda a: a[None])
        grads[n] = back(g_big[n])
        dl, nm, nv = _adamw(local(n, p[n]), g_big[n], local(n, p["m_" + n]), local(n, p["v_" + n]), "adamw_" + n)
        deltas[n], new_m[n], new_v[n] = back(dl), back(nm), back(nv)
    packed = [_pack([src[n] for n in SMALL]) for src in
              ({n: p[n] for n in SMALL}, g_small, {n: p["m_" + n] for n in SMALL}, {n: p["v_" + n] for n in SMALL})]
    small_shapes = [p[n].shape for n in SMALL]
    for dst, arr in zip((deltas, new_m, new_v), _adamw(*packed, "adamw_small")):
        dst.update(zip(SMALL, _unpack(arr, small_shapes)))
    grads.update(g_small)

    return (loss, gx[None], *[grads[n] for n in WEIGHTS], *[deltas[n] for n in WEIGHTS],
            *[new_m[n] for n in WEIGHTS], *[new_v[n] for n in WEIGHTS])
```

```python
import math
from typing import Callable, NamedTuple

import numpy as np
import jax
import jax.numpy as jnp
from jax import lax
from jax.experimental import pallas as pl
from jax.experimental.pallas import tpu as pltpu

F32 = jnp.float32
BF16 = jnp.bfloat16
HIGHEST = lax.Precision.HIGHEST
MESH = pl.DeviceIdType.MESH

EPS = 1e-6
NEG_BIG = -1e30
GDN_HEADS = 4
GDN_DIM = 128
CHUNK = 64
SWA_HEADS = 8
SWA_DIM = 64
PATTERNS = ((128, 1), (512, 4), (2048, 16))
RADIUS = 64
REL_BUCKETS = 32
REL_MAX_DISTANCE = 1024
CONV_TAPS = 5
N_SHARDS = 4
LANES = 128
VMEM_LIMIT = 56 * 1024 * 1024

ADAM_LR, ADAM_B1, ADAM_B2, ADAM_EPS, ADAM_WD, ADAM_STEP = 0.001, 0.9, 0.999, 1e-08, 0.01, 10


def _params(sem=None, vmem=None):
    return pltpu.CompilerParams(dimension_semantics=sem, vmem_limit_bytes=vmem)


def _resident(shape):
    nd = len(shape)
    return pl.BlockSpec(shape, lambda *_: (0,) * nd, pipeline_mode=pl.Buffered(1))


ANY = pl.BlockSpec(memory_space=pl.ANY)


class _Exchange(NamedTuple):
    arrays: tuple
    out_shape: tuple
    start: Callable
    finish: Callable


def _grid_call(body, name, nsteps, in_specs, out_specs, out_shape, operands, scratch=(), exchange=None):
    params = _params(("arbitrary",), VMEM_LIMIT)
    if exchange is None:
        res = pl.pallas_call(body, name=name, grid=(nsteps,), in_specs=list(in_specs), out_specs=list(out_specs),
                             out_shape=list(out_shape), scratch_shapes=list(scratch), compiler_params=params)(*operands)
        return list(res), []
    n_in, n_out, k, n_scr = len(in_specs), len(out_specs), len(exchange.arrays), len(scratch)

    def wrapped(*refs):
        ins, cin = refs[:n_in], refs[n_in:n_in + k]
        outs, cout = refs[n_in + k:n_in + k + n_out], refs[n_in + k + n_out:n_in + 2 * k + n_out]
        rest = refs[n_in + 2 * k + n_out:]
        scr, (send_sems, recv_sems) = rest[:n_scr], rest[n_scr:]

        @pl.when(pl.program_id(0) == 0)
        def _():
            exchange.start(cin, cout, send_sems, recv_sems)

        body(*ins, *outs, *scr)

        @pl.when(pl.program_id(0) == nsteps - 1)
        def _():
            exchange.finish(cin, cout, send_sems, recv_sems)

    res = pl.pallas_call(
        wrapped, name=name, grid=(nsteps,), in_specs=list(in_specs) + [ANY] * k, out_specs=list(out_specs) + [ANY] * k,
        out_shape=list(out_shape) + list(exchange.out_shape),
        scratch_shapes=list(scratch) + [pltpu.SemaphoreType.DMA((k, 3)), pltpu.SemaphoreType.DMA((k, 3))],
        compiler_params=params)(*operands, *exchange.arrays)
    return list(res[:n_out]), list(res[n_out:])


def _dot(a, b):
    return jnp.dot(a.astype(BF16), b.astype(BF16), preferred_element_type=F32)


def _dot_nt(a, b):
    return lax.dot_general(a.astype(BF16), b.astype(BF16), (((1,), (1,)), ((), ())), preferred_element_type=F32)


def _dot_tn(a, b):
    return lax.dot_general(a.astype(BF16), b.astype(BF16), (((0,), (0,)), ((), ())), preferred_element_type=F32)


def _dot_hi(a, b):
    return jnp.dot(a, b, preferred_element_type=F32, precision=HIGHEST)


def _sigmoid(x):
    return 1.0 / (1.0 + jnp.exp(-x))


def _rstd(xf):
    return lax.rsqrt(jnp.mean(xf * xf, axis=-1, keepdims=True) + EPS)


def _rms_bwd(xf, r, nw, dxn):
    xhat = xf * r
    dxh = dxn * nw
    dx = r * (dxh - xhat * jnp.mean(dxh * xhat, axis=-1, keepdims=True))
    return dx, jnp.sum(dxn * xhat, axis=0, keepdims=True)


def _ffn_fwd(x, nw, wg, wu, wd, tm, name, exchange=None, head=None):
    t, d = x.shape
    nj, fs, _ = wg.shape

    def body(x_ref, nw_ref, wg_ref, wu_ref, wd_ref, *rest):
        if head is None:
            y_ref, xn_ref, g_ref, u_ref = rest
        else:
            fw_ref, t_ref, y_ref, xn_ref, g_ref, u_ref, loss_ref, dfw_ref = rest

            @pl.when(pl.program_id(0) == 0)
            def _():
                loss_ref[...] = jnp.zeros_like(loss_ref)
                dfw_ref[...] = jnp.zeros_like(dfw_ref)

        xf = x_ref[...]
        xn = (xf * _rstd(xf) * nw_ref[...]).astype(BF16)
        xn_ref[...] = xn
        acc = jnp.zeros((tm, d), F32)
        for j in range(nj):
            g = _dot_nt(xn, wg_ref[j])
            u = _dot_nt(xn, wu_ref[j])
            h = (g * _sigmoid(g) * u).astype(BF16)
            acc = acc + jnp.dot(h, wd_ref[j], preferred_element_type=F32)
            g_ref[j] = g.astype(BF16)
            u_ref[j] = u.astype(BF16)
        y = xf + 0.5 * acc
        if head is None:
            y_ref[...] = y
        else:
            r = _rstd(y)
            err = y * r * fw_ref[...] - t_ref[...]
            loss_ref[...] += 0.5 * jnp.sum(jnp.mean(err * err, axis=-1, keepdims=True), axis=0, keepdims=True)
            dy, dfw = _rms_bwd(y, r, fw_ref[...], err * (1.0 / d))
            y_ref[...] = dy
            dfw_ref[...] += dfw

    row = pl.BlockSpec((tm, d), lambda i: (i, 0))
    act = pl.BlockSpec((nj, tm, fs), lambda i: (0, i, 0))
    in_specs = [row, _resident((1, d)), _resident(wg.shape), _resident(wu.shape), _resident(wd.shape)]
    out_specs = [row, row, act, act]
    out_shape = [jax.ShapeDtypeStruct((t, d), F32), jax.ShapeDtypeStruct((t, d), BF16),
                 jax.ShapeDtypeStruct((nj, t, fs), BF16), jax.ShapeDtypeStruct((nj, t, fs), BF16)]
    operands = (x, nw, wg, wu, wd)
    if head is not None:
        in_specs += [_resident((1, d)), row]
        out_specs += [pl.BlockSpec((1, LANES), lambda i: (0, 0)), pl.BlockSpec((1, d), lambda i: (0, 0))]
        out_shape += [jax.ShapeDtypeStruct((1, LANES), F32), jax.ShapeDtypeStruct((1, d), F32)]
        operands += tuple(head)
    return _grid_call(body, name, t // tm, in_specs, out_specs, out_shape, operands, exchange=exchange)


def _ffn_bwd_dx(dy, x, nw, g, u, wg, wu, wd, tm, name, exchange=None):
    t, d = x.shape
    nj, fs, _ = wg.shape

    def body(dy_ref, x_ref, nw_ref, g_ref, u_ref, wg_ref, wu_ref, wd_ref,
             dx_ref, dyh_ref, dg_ref, du_ref, h_ref, dnw_ref):
        @pl.when(pl.program_id(0) == 0)
        def _():
            dnw_ref[...] = jnp.zeros_like(dnw_ref)

        dyv = dy_ref[...]
        dyh = (0.5 * dyv).astype(BF16)
        dyh_ref[...] = dyh
        dxn = jnp.zeros((tm, d), F32)
        dh_next = _dot_nt(dyh, wd_ref[0])
        for j in range(nj):
            dh = dh_next
            gv = g_ref[j].astype(F32)
            uv = u_ref[j].astype(F32)
            sg = _sigmoid(gv)
            si = gv * sg
            dg = (dh * uv * (sg * (1.0 + gv * (1.0 - sg)))).astype(BF16)
            du = (dh * si).astype(BF16)
            if j + 1 < nj:
                dh_next = _dot_nt(dyh, wd_ref[j + 1])
            h_ref[j] = (si * uv).astype(BF16)
            dg_ref[j] = dg
            du_ref[j] = du
            dxn = dxn + _dot(dg, wg_ref[j]) + _dot(du, wu_ref[j])
        xf = x_ref[...]
        dxr, dnw = _rms_bwd(xf, _rstd(xf), nw_ref[...], dxn)
        dx_ref[...] = dyv + dxr
        dnw_ref[...] += dnw

    row = pl.BlockSpec((tm, d), lambda i: (i, 0))
    act = pl.BlockSpec((nj, tm, fs), lambda i: (0, i, 0))
    act_shape = jax.ShapeDtypeStruct((nj, t, fs), BF16)
    return _grid_call(
        body, name, t // tm,
        [row, row, _resident((1, d)), act, act, _resident(wg.shape), _resident(wu.shape), _resident(wd.shape)],
        [row, row, act, act, act, pl.BlockSpec((1, d), lambda i: (0, 0))],
        [jax.ShapeDtypeStruct((t, d), F32), jax.ShapeDtypeStruct((t, d), BF16),
         act_shape, act_shape, act_shape, jax.ShapeDtypeStruct((1, d), F32)],
        (dy, x, nw, g, u, wg, wu, wd), exchange=exchange)


def _matmul_tn(a, b, tk, name):
    a3, b3 = a.ndim == 3, b.ndim == 3
    nj = a.shape[0] if a3 else (b.shape[0] if b3 else 1)
    t, m = a.shape[-2:]
    n = b.shape[-1]
    nt = t // tk

    def body(a_ref, b_ref, o_ref, acc_ref):
        k = pl.program_id(1)

        @pl.when(k == 0)
        def _():
            acc_ref[...] = jnp.zeros_like(acc_ref)

        acc_ref[...] += lax.dot_general(a_ref[...], b_ref[...], (((0,), (0,)), ((), ())),
                                        preferred_element_type=F32)

        @pl.when(k == nt - 1)
        def _():
            o_ref[...] = acc_ref[...].astype(o_ref.dtype)

    a_spec = (pl.BlockSpec((None, tk, m), lambda j, k: (j, k, 0)) if a3
              else pl.BlockSpec((tk, m), lambda j, k: (k, 0)))
    b_spec = (pl.BlockSpec((None, tk, n), lambda j, k: (j, k, 0)) if b3
              else pl.BlockSpec((tk, n), lambda j, k: (k, 0)))
    return pl.pallas_call(
        body, name=name, grid=(nj, nt),
        in_specs=[a_spec, b_spec],
        out_specs=pl.BlockSpec((None, m, n), lambda j, k: (j, 0, 0)),
        out_shape=jax.ShapeDtypeStruct((nj, m, n), BF16),
        scratch_shapes=[pltpu.VMEM((m, n), F32)],
        compiler_params=_params(("arbitrary", "arbitrary"), VMEM_LIMIT),
    )(a, b)


P_QKVA, P_Z, P_AB, P_QKVB = (0, 1536), (1536, 2048), (2048, 2176), (2176, 3712)
P_PIECES = (P_QKVA, P_Z, P_AB, P_QKVB)
P_COLS = 3712


def _mix_in_fwd(x1, nw, wp, tm):
    t, d = x1.shape

    def body(x_ref, nw_ref, w_ref, hn_ref, *outs):
        xf = x_ref[...]
        xn = (xf * _rstd(xf) * nw_ref[...]).astype(BF16)
        hn_ref[...] = xn
        for (a, b), o_ref in zip(P_PIECES, outs):
            o_ref[...] = _dot_nt(xn, w_ref[a:b, :])

    row = pl.BlockSpec((tm, d), lambda i: (i, 0))
    return pl.pallas_call(
        body, name="mix_in_fwd", grid=(t // tm,),
        in_specs=[row, _resident((1, d)), _resident(wp.shape)],
        out_specs=[row] + [pl.BlockSpec((tm, b - a), lambda i: (i, 0)) for a, b in P_PIECES],
        out_shape=[jax.ShapeDtypeStruct((t, d), BF16)]
                  + [jax.ShapeDtypeStruct((t, b - a), F32) for a, b in P_PIECES],
        compiler_params=_params(("arbitrary",), VMEM_LIMIT),
    )(x1, nw, wp)


def _mix_in_bwd_dx(dx, x1, nw, dpieces, wp, tm, exchange=None):
    t, d = x1.shape

    def body(dx_ref, x_ref, nw_ref, p0, p1, p2, p3, w_ref, o_ref, dnw_ref):
        @pl.when(pl.program_id(0) == 0)
        def _():
            dnw_ref[...] = jnp.zeros_like(dnw_ref)

        dh = jnp.zeros((tm, d), F32)
        for (a, b), p_ref in zip(P_PIECES, (p0, p1, p2, p3)):
            dh = dh + _dot(p_ref[...], w_ref[a:b, :])
        xf = x_ref[...]
        dxr, dnw = _rms_bwd(xf, _rstd(xf), nw_ref[...], dh)
        o_ref[...] = dx_ref[...] + dxr
        dnw_ref[...] += dnw

    row = pl.BlockSpec((tm, d), lambda i: (i, 0))
    return _grid_call(
        body, "mix_in_bwd_dx", t // tm,
        [row, row, _resident((1, d))]
        + [pl.BlockSpec((tm, b - a), lambda i: (i, 0)) for a, b in P_PIECES] + [_resident(wp.shape)],
        [row, pl.BlockSpec((1, d), lambda i: (0, 0))],
        [jax.ShapeDtypeStruct((t, d), F32), jax.ShapeDtypeStruct((1, d), F32)],
        (dx, x1, nw, *dpieces, wp), exchange=exchange)


def _mix_out_fwd(x1, oa, ob, w, tm):
    t, d = x1.shape
    half = oa.shape[1]

    def body(x_ref, oa_ref, ob_ref, w_ref, o_ref):
        o_ref[...] = (x_ref[...] + _dot(oa_ref[...], w_ref[0:half, :]) + _dot(ob_ref[...], w_ref[half:2 * half, :]))

    row = pl.BlockSpec((tm, d), lambda i: (i, 0))
    hrow = pl.BlockSpec((tm, half), lambda i: (i, 0))
    return pl.pallas_call(
        body, name="mix_out_fwd", grid=(t // tm,),
        in_specs=[row, hrow, hrow, _resident(w.shape)],
        out_specs=row, out_shape=jax.ShapeDtypeStruct((t, d), F32),
        compiler_params=_params(("arbitrary",), VMEM_LIMIT),
    )(x1, oa, ob, w)


def _mix_out_bwd(dx2, w, tm):
    t, d = dx2.shape
    half = w.shape[0] // 2

    def body(dx_ref, w_ref, doa_ref, dob_ref, dxb_ref):
        dxb = dx_ref[...].astype(BF16)
        dxb_ref[...] = dxb
        doa_ref[...] = _dot_nt(dxb, w_ref[0:half, :])
        dob_ref[...] = _dot_nt(dxb, w_ref[half:2 * half, :])

    row = pl.BlockSpec((tm, d), lambda i: (i, 0))
    hrow = pl.BlockSpec((tm, half), lambda i: (i, 0))
    return pl.pallas_call(
        body, name="mix_out_bwd", grid=(t // tm,),
        in_specs=[row, _resident(w.shape)],
        out_specs=[hrow, hrow, row],
        out_shape=[jax.ShapeDtypeStruct((t, half), F32), jax.ShapeDtypeStruct((t, half), F32),
                   jax.ShapeDtypeStruct((t, d), BF16)],
        compiler_params=_params(("arbitrary",), VMEM_LIMIT),
    )(dx2, w)


HALO = 8


def _halo_row_specs(tr, cols, nrow8):
    per = tr // HALO
    return [pl.BlockSpec((tr, cols), lambda i: (i, 0)),
            pl.BlockSpec((HALO, cols), lambda i: (jnp.maximum(i * per - 1, 0), 0)),
            pl.BlockSpec((HALO, cols), lambda i: (jnp.minimum((i + 1) * per, nrow8 - 1), 0))]


def _conv_window(xm, xp, xn, first, last, cols):
    prev = jnp.where(first, 0.0, xp[:, cols])
    nxt = jnp.where(last, 0.0, xn[:, cols])
    return jnp.concatenate([prev, xm[:, cols], nxt], axis=0)


def _shift_rows(xw, off):
    n = xw.shape[0]
    sh = (-off) % n
    return xw if sh == 0 else pltpu.roll(xw, sh, 0)


def _conv_pre(xw, cw_ref, cols):
    acc = None
    for j in range(CONV_TAPS):
        term = _shift_rows(xw, j - CONV_TAPS // 2) * cw_ref[j:j + 1, cols]
        acc = term if acc is None else acc + term
    return acc


def _softplus(x):
    u = jnp.exp(-jnp.abs(x))
    w = 1.0 + u
    log1p = jnp.where(w == 1.0, u, jnp.log(w) * u / jnp.where(w == 1.0, 1.0, w - 1.0))
    return jnp.maximum(x, 0.0) + log1p


def _gdn_prep_fwd(qkva, cw, ab, gp, tr):
    t, c = qkva.shape
    nt = t // tr
    ncb = c // LANES

    def body(xm, xp, xn, cw_ref, ab_ref, gp_ref, o_ref, gb_ref):
        i = pl.program_id(0)
        first, last = i == 0, i == nt - 1
        for cb in range(ncb):
            cols = slice(cb * LANES, (cb + 1) * LANES)
            xw = _conv_window(xm, xp, xn, first, last, cols)
            pre = _conv_pre(xw, cw_ref, cols)[HALO:HALO + tr]
            y = pre * _sigmoid(pre)
            if cb < 2 * GDN_HEADS:
                y = y * lax.rsqrt(jnp.sum(y * y, axis=-1, keepdims=True) + EPS)
            if cb < GDN_HEADS:
                y = y * (GDN_DIM ** -0.5)
            o_ref[:, cols] = y
        abv = ab_ref[...]
        lane = lax.broadcasted_iota(jnp.int32, abv.shape, 1)
        g = -jnp.exp(gp_ref[0:1, :]) * _softplus(abv + gp_ref[1:2, :])
        gb_ref[...] = jnp.where(lane < 8, g, jnp.where(lane < 16, _sigmoid(abv), 0.0))

    return pl.pallas_call(
        body, name="gdn_prep_fwd", grid=(nt,),
        in_specs=_halo_row_specs(tr, c, t // HALO)
                 + [_resident(cw.shape), pl.BlockSpec((tr, LANES), lambda i: (i, 0)), _resident(gp.shape)],
        out_specs=[pl.BlockSpec((tr, c), lambda i: (i, 0)), pl.BlockSpec((tr, LANES), lambda i: (i, 0))],
        out_shape=[jax.ShapeDtypeStruct((t, c), F32), jax.ShapeDtypeStruct((t, LANES), F32)],
        compiler_params=_params(("arbitrary",), VMEM_LIMIT),
    )(qkva, qkva, qkva, cw, ab, gp)


def _gdn_prep_bwd(qkva, cw, ab, gp, dy, dgates, tr):
    t, c = qkva.shape
    nt = t // tr
    ncb = c // LANES

    def body(xm, xp, xn, fm, fp, fn, cw_ref, ab_ref, gp_ref, gf_ref, dx_ref, dab_ref, dcw_ref, dgp_ref):
        i = pl.program_id(0)
        first, last = i == 0, i == nt - 1

        @pl.when(first)
        def _():
            dcw_ref[...] = jnp.zeros_like(dcw_ref)
            dgp_ref[...] = jnp.zeros_like(dgp_ref)

        sub8 = lax.broadcasted_iota(jnp.int32, (8, LANES), 0)
        for cb in range(ncb):
            cols = slice(cb * LANES, (cb + 1) * LANES)
            xw = _conv_window(xm, xp, xn, first, last, cols)
            dyw = _conv_window(fm, fp, fn, first, last, cols)
            pre = _conv_pre(xw, cw_ref, cols)
            sg = _sigmoid(pre)
            s = pre * sg
            if cb < 2 * GDN_HEADS:
                scale = (GDN_DIM ** -0.5) if cb < GDN_HEADS else 1.0
                r = lax.rsqrt(jnp.sum(s * s, axis=-1, keepdims=True) + EPS)
                dn = dyw * scale
                ds = r * dn - s * (r * r * r) * jnp.sum(dn * s, axis=-1, keepdims=True)
            else:
                ds = dyw
            dpre = ds * (sg * (1.0 + pre * (1.0 - sg)))
            dx = None
            dcw = jnp.zeros((8, LANES), F32)
            for j in range(CONV_TAPS):
                off = j - CONV_TAPS // 2
                term = _shift_rows(dpre, -off)[HALO:HALO + tr] * cw_ref[j:j + 1, cols]
                dx = term if dx is None else dx + term
                tap = jnp.sum(dpre[HALO:HALO + tr] * _shift_rows(xw, off)[HALO:HALO + tr], axis=0, keepdims=True)
                dcw = dcw + jnp.where(sub8 == j, tap, 0.0)
            dx_ref[:, cols] = dx.astype(BF16)
            dcw_ref[:, cols] += dcw

        abv = ab_ref[...]
        dgb = gf_ref[...]
        lane = lax.broadcasted_iota(jnp.int32, abv.shape, 1)
        nea = -jnp.exp(gp_ref[0:1, :])
        xs = abv + gp_ref[1:2, :]
        g = nea * _softplus(xs)
        beta = _sigmoid(abv)
        da = dgb * nea * _sigmoid(xs)
        dab = jnp.where(lane < 8, da, jnp.where(lane < 16, dgb * beta * (1.0 - beta), 0.0))
        dab_ref[...] = dab.astype(BF16)
        keep = lane[0:1, :] < 8
        dalog = jnp.where(keep, jnp.sum(dgb * g, axis=0, keepdims=True), 0.0)
        ddtb = jnp.where(keep, jnp.sum(da, axis=0, keepdims=True), 0.0)
        dgp_ref[...] += jnp.where(sub8 == 0, dalog, 0.0) + jnp.where(sub8 == 1, ddtb, 0.0)

    lrow = pl.BlockSpec((tr, LANES), lambda i: (i, 0))
    halo = _halo_row_specs(tr, c, t // HALO)
    return pl.pallas_call(
        body, name="gdn_prep_bwd", grid=(nt,),
        in_specs=halo + halo + [_resident(cw.shape), lrow, _resident(gp.shape), lrow],
        out_specs=[pl.BlockSpec((tr, c), lambda i: (i, 0)), lrow,
                   pl.BlockSpec(cw.shape, lambda i: (0, 0)), pl.BlockSpec(gp.shape, lambda i: (0, 0))],
        out_shape=[jax.ShapeDtypeStruct((t, c), BF16), jax.ShapeDtypeStruct((t, LANES), BF16),
                   jax.ShapeDtypeStruct(cw.shape, F32), jax.ShapeDtypeStruct(gp.shape, F32)],
        compiler_params=_params(("arbitrary",), VMEM_LIMIT),
    )(qkva, qkva, qkva, dy, dy, dy, cw, ab, gp, dgates)


def _chunk_masks(lower):
    ii = lax.broadcasted_iota(jnp.int32, (CHUNK, CHUNK), 0)
    jj = lax.broadcasted_iota(jnp.int32, (CHUNK, CHUNK), 1)
    incl = (ii >= jj) if lower else (ii <= jj)
    strict = (ii > jj) if lower else (ii < jj)
    return ii, jj, incl, strict


def _dot3(a, b):
    ah = a.astype(BF16)
    al = (a - ah.astype(F32)).astype(BF16)
    bh = b.astype(BF16)
    bl = (b - bh.astype(F32)).astype(BF16)
    d = lambda u, v: jnp.dot(u, v, preferred_element_type=F32)
    return d(ah, bh) + (d(ah, bl) + d(al, bh))


def _tri_inv_many(lmats, ii, jj):
    m16 = (ii // 16) == (jj // 16)
    m32 = (ii // 32) == (jj // 32)
    eye = jnp.where(ii == jj, 1.0, 0.0)
    l16 = [jnp.where(m16, l, 0.0) for l in lmats]
    p2 = [_dot3(a, a) for a in l16]
    p4 = [_dot3(a, a) for a in p2]
    p8 = [_dot3(a, a) for a in p4]
    xs = [eye - a for a in l16]
    for ps in (p2, p4, p8):
        xs = [x + _dot3(x, p) for x, p in zip(xs, ps)]
    for off in ([jnp.where(m32 & jnp.logical_not(m16), l, 0.0) for l in lmats],
                [jnp.where(m32, 0.0, l) for l in lmats]):
        ys = [_dot3(x, c) for x, c in zip(xs, off)]
        xs = [x - _dot3(y, x) for x, y in zip(xs, ys)]
    return xs


def _col_to_row(col, ii, jj):
    return jnp.sum(jnp.where(ii == jj, col, 0.0), axis=0, keepdims=True)


def _row_to_col(row, ii, jj):
    return jnp.sum(jnp.where(ii == jj, row, 0.0), axis=1, keepdims=True)


def _chain_common(q, k, v, graw_col, graw_row, bcol, masks):
    ii, jj, incl, strict = masks
    inclt = jnp.logical_not(strict)
    gcol = jnp.sum(jnp.where(incl, graw_row, 0.0), axis=1, keepdims=True)
    grow = jnp.sum(jnp.where(inclt, graw_col, 0.0), axis=0, keepdims=True)
    glast = jnp.sum(graw_row, axis=1, keepdims=True)
    decay = jnp.where(incl, jnp.exp(jnp.where(incl, gcol - grow, 0.0)), 0.0)
    kb = k * bcol
    vb = v * bcol
    eg = jnp.exp(gcol)
    ek = jnp.exp(glast - gcol)
    kbg = kb * eg
    amat = _dot_nt(kb, k)
    qk = _dot_nt(q, k)
    return dict(gcol=gcol, glast=glast, decay=decay, kb=kb, vb=vb, eg=eg, ek=ek, kbg=kbg, amat=amat, qk=qk,
                intra=qk * decay, qg=q * eg, kdec=k * ek)


def _gdn_fwd(qkvc, gb, gbt):
    tm, u, w, qg, kd, intra, egl = _gdn_local_fwd(qkvc, gb, gbt)
    o_f, o_b, s_f, s_b, vn_f, vn_b = _gdn_scan_fwd(u, w, qg, kd, intra, egl, qkvc.shape[0])
    return o_f, o_b, dict(tm=tm, w=w, qg=qg, kd=kd, intra=intra, egl=egl, s=(s_f, s_b), vn=(vn_f, vn_b))


N_CHAINS = 2 * GDN_HEADS


LOCAL_CHUNKS = 2


def _load_chains(x_ref, g_ref, gt_ref, cc=0):
    hd = GDN_HEADS * GDN_DIM
    rows = slice(cc * CHUNK, (cc + 1) * CHUNK)
    chains = []
    for d in range(2):
        masks = _chunk_masks(d == 0)
        for h in range(GDN_HEADS):
            ch = d * GDN_HEADS + h
            q = x_ref[rows, h * GDN_DIM:(h + 1) * GDN_DIM]
            k = x_ref[rows, hd + h * GDN_DIM:hd + (h + 1) * GDN_DIM]
            v = x_ref[rows, 2 * hd + h * GDN_DIM:2 * hd + (h + 1) * GDN_DIM]
            bcol = g_ref[rows, 8 + ch:9 + ch]
            cm = _chain_common(q, k, v, g_ref[rows, ch:ch + 1], gt_ref[cc, ch:ch + 1, :], bcol, masks)
            chains.append(dict(cm, q=q, k=k, v=v, bcol=bcol, masks=masks, ch=ch, h=h, cc=cc))
    return chains


def _chain_shape(rows, cols, dtype):
    return lambda nc: jax.ShapeDtypeStruct((nc, N_CHAINS, rows, cols), dtype)


def _gdn_local_fwd(qkvc, gb, gbt):
    t = qkvc.shape[0]
    nc = t // CHUNK
    hd = GDN_HEADS * GDN_DIM

    def body(x_ref, g_ref, gt_ref, t_ref, u_ref, w_ref, qg_ref, kd_ref, in_ref, eg_ref):
        chains = [c for cc in range(LOCAL_CHUNKS) for c in _load_chains(x_ref, g_ref, gt_ref, cc)]
        ii, jj = chains[0]["masks"][0:2]
        tms = _tri_inv_many([jnp.where(c["masks"][3], c["amat"] * c["decay"], 0.0) for c in chains], ii, jj)
        uws = [_dot(tm, jnp.concatenate([c["vb"], c["kbg"]], axis=1)) for tm, c in zip(tms, chains)]
        for c, tm, uw in zip(chains, tms, uws):
            cc, ch = c["cc"], c["ch"]
            t_ref[cc, ch] = tm
            u_ref[cc, ch] = uw[:, :GDN_DIM]
            w_ref[cc, ch] = uw[:, GDN_DIM:].astype(BF16)
            qg_ref[cc, ch] = c["qg"].astype(BF16)
            kd_ref[cc, ch] = c["kdec"].astype(BF16)
            in_ref[cc, ch] = c["intra"].astype(BF16)
            eg_ref[cc, ch:ch + 1, :] = jnp.broadcast_to(jnp.exp(c["glast"]), (1, LANES))

    lc = LOCAL_CHUNKS
    blk = lambda rows, cols: pl.BlockSpec((lc, N_CHAINS, rows, cols), lambda n: (n, 0, 0, 0))
    shapes = [_chain_shape(CHUNK, CHUNK, F32), _chain_shape(CHUNK, GDN_DIM, F32), _chain_shape(CHUNK, GDN_DIM, BF16),
              _chain_shape(CHUNK, GDN_DIM, BF16), _chain_shape(CHUNK, GDN_DIM, BF16), _chain_shape(CHUNK, CHUNK, BF16)]
    return tuple(pl.pallas_call(
        body, name="gdn_local_fwd", grid=(nc // lc,),
        in_specs=[pl.BlockSpec((lc * CHUNK, 3 * hd), lambda n: (n, 0)), pl.BlockSpec((lc * CHUNK, LANES), lambda n: (n, 0)),
                  pl.BlockSpec((lc, 16, CHUNK), lambda n: (n, 0, 0))],
        out_specs=[blk(CHUNK, CHUNK), blk(CHUNK, GDN_DIM), blk(CHUNK, GDN_DIM), blk(CHUNK, GDN_DIM),
                   blk(CHUNK, GDN_DIM), blk(CHUNK, CHUNK), pl.BlockSpec((lc, N_CHAINS, LANES), lambda n: (n, 0, 0))],
        out_shape=[s(nc) for s in shapes] + [jax.ShapeDtypeStruct((nc, N_CHAINS, LANES), F32)],
        compiler_params=_params(("arbitrary",), VMEM_LIMIT),
    )(qkvc, gb, gbt))


SCAN_CHUNKS = 4


def _dir_specs(nc, rev):
    nb = nc // SCAN_CHUNKS

    def spec(d, rows, cols, own=False):
        chunk = (lambda n: n) if (d == 0) != rev else (lambda n: nb - 1 - n)
        blk = 0 if own else d
        if rows is None:
            return pl.BlockSpec((SCAN_CHUNKS, GDN_HEADS if own else N_CHAINS, cols), lambda n: (chunk(n), 0, 0))
        return pl.BlockSpec((SCAN_CHUNKS, GDN_HEADS, rows, cols), lambda n: (chunk(n), blk, 0, 0))

    def rows_spec(d, cols):
        chunk = (lambda n: n) if (d == 0) != rev else (lambda n: nb - 1 - n)
        return pl.BlockSpec((SCAN_CHUNKS * CHUNK, cols), lambda n: (chunk(n), 0))

    def order(d):
        return list(range(SCAN_CHUNKS)) if (d == 0) != rev else list(range(SCAN_CHUNKS - 1, -1, -1))
    return spec, rows_spec, order


def _gdn_scan_fwd(u, w, qg, kd, intra, egl, t):
    nc = t // CHUNK
    hd = GDN_HEADS * GDN_DIM

    def body(*refs):
        ins, outs, state = refs[:12], refs[12:18], refs[18]
        @pl.when(pl.program_id(0) == 0)
        def _():
            state[...] = jnp.zeros_like(state)

        chains = [(d, h) for d in range(2) for h in range(GDN_HEADS)]
        states = [state[ch] for ch in range(N_CHAINS)]
        for step in range(SCAN_CHUNKS):
            at = [order(d)[step] for d in range(2)]
            pick = lambda k, d, h: ins[2 * k + d][at[d], h]
            sbs = [s.astype(BF16) for s in states]
            ws = [_dot(pick(1, d, h), sb) for (d, h), sb in zip(chains, sbs)]
            o1 = [_dot(pick(2, d, h), sb) for (d, h), sb in zip(chains, sbs)]
            vns = [(pick(0, d, h) - wsb).astype(BF16) for (d, h), wsb in zip(chains, ws)]
            o2 = [_dot(pick(4, d, h), vn) for (d, h), vn in zip(chains, vns)]
            kv = [_dot_tn(pick(3, d, h), vn) for (d, h), vn in zip(chains, vns)]
            new_states = []
            for ch, (d, h) in enumerate(chains):
                outs[d][at[d] * CHUNK:(at[d] + 1) * CHUNK, h * GDN_DIM:(h + 1) * GDN_DIM] = o1[ch] + o2[ch]
                outs[2 + d][at[d], h] = states[ch]
                outs[4 + d][at[d], h] = vns[ch]
                new_states.append(states[ch] * ins[10 + d][at[d], ch:ch + 1, :] + kv[ch])
            states = new_states
        for ch in range(N_CHAINS):
            state[ch] = states[ch]

    spec, rows_spec, order = _dir_specs(nc, False)
    pair = lambda rows, cols, own=False: [spec(0, rows, cols, own), spec(1, rows, cols, own)]
    s_shape = jax.ShapeDtypeStruct((nc, GDN_HEADS, GDN_DIM, GDN_DIM), F32)
    vn_shape = jax.ShapeDtypeStruct((nc, GDN_HEADS, CHUNK, GDN_DIM), BF16)
    return pl.pallas_call(
        body, name="gdn_scan_fwd", grid=(nc // SCAN_CHUNKS,),
        in_specs=(pair(CHUNK, GDN_DIM) + pair(CHUNK, GDN_DIM) + pair(CHUNK, GDN_DIM) + pair(CHUNK, GDN_DIM)
                  + pair(CHUNK, CHUNK) + pair(None, LANES)),
        out_specs=([rows_spec(0, hd), rows_spec(1, hd)] + pair(GDN_DIM, GDN_DIM, True)
                   + pair(CHUNK, GDN_DIM, True)),
        out_shape=[jax.ShapeDtypeStruct((t, hd), F32), jax.ShapeDtypeStruct((t, hd), F32),
                   s_shape, s_shape, vn_shape, vn_shape],
        scratch_shapes=[pltpu.VMEM((N_CHAINS, GDN_DIM, GDN_DIM), F32)],
        compiler_params=_params(("arbitrary",), VMEM_LIMIT),
    )(u, u, w, w, qg, qg, kd, kd, intra, intra, egl, egl)


def _gdn_bwd(qkvc, gb, gbt, do, saved, exchange=None):
    scan = _gdn_scan_bwd(do, saved, qkvc.shape[0])
    return _gdn_local_bwd(qkvc, gb, gbt, do, saved, scan, exchange)


def _gdn_scan_bwd(do, saved, t):
    nc = t // CHUNK
    hd = GDN_HEADS * GDN_DIM

    def body(*refs):
        ins, outs, dstate = refs[:16], refs[16:26], refs[26]
        @pl.when(pl.program_id(0) == 0)
        def _():
            dstate[...] = jnp.zeros_like(dstate)

        chains = [(d, h) for d in range(2) for h in range(GDN_HEADS)]
        dss = [dstate[ch] for ch in range(N_CHAINS)]
        for step in range(SCAN_CHUNKS):
            at = [order(d)[step] for d in range(2)]
            pick = lambda k, d, h: ins[2 * k + d][at[d], h]
            dsbs = [ds.astype(BF16) for ds in dss]
            ss = [pick(1, d, h) for d, h in chains]
            sbs = [s.astype(BF16) for s in ss]
            dos = [ins[d][at[d] * CHUNK:(at[d] + 1) * CHUNK, h * GDN_DIM:(h + 1) * GDN_DIM].astype(BF16)
                   for d, h in chains]
            dv1 = [_dot_tn(pick(5, d, h), dov) for (d, h), dov in zip(chains, dos)]
            dv2 = [_dot(pick(4, d, h), dsb) for (d, h), dsb in zip(chains, dsbs)]
            ds1 = [_dot_tn(pick(3, d, h), dov) for (d, h), dov in zip(chains, dos)]
            dkds = [_dot_nt(pick(6, d, h), dsb) for (d, h), dsb in zip(chains, dsbs)]
            dqgs = [_dot_nt(dov, sb) for dov, sb in zip(dos, sbs)]
            dvns = [(a + b).astype(BF16) for a, b in zip(dv1, dv2)]
            ds2 = [_dot_tn(pick(2, d, h), dvn) for (d, h), dvn in zip(chains, dvns)]
            dws = [_dot_nt(dvn, sb) for dvn, sb in zip(dvns, sbs)]
            new_dss = []
            for ch, (d, h) in enumerate(chains):
                egl = ins[14 + d][at[d], ch:ch + 1, :]
                outs[d][at[d], h] = dvns[ch]
                outs[2 + d][at[d], h] = (-dws[ch]).astype(BF16)
                outs[4 + d][at[d], h] = dqgs[ch]
                outs[6 + d][at[d], h] = dkds[ch]
                outs[8 + d][at[d], h:h + 1, :] = egl * jnp.sum(jnp.sum(ss[ch] * dss[ch], axis=1, keepdims=True),
                                                               axis=0, keepdims=True)
                new_dss.append(ds1[ch] + egl * dss[ch] - ds2[ch])
            dss = new_dss
        for ch in range(N_CHAINS):
            dstate[ch] = dss[ch]

    spec, rows_spec, order = _dir_specs(nc, True)
    pair = lambda rows, cols, own=False: [spec(0, rows, cols, own), spec(1, rows, cols, own)]
    s_f, s_b = saved["s"]
    vn_f, vn_b = saved["vn"]
    w, qg, kd, intra, egl = saved["w"], saved["qg"], saved["kd"], saved["intra"], saved["egl"]
    own = lambda rows, cols, dtype: jax.ShapeDtypeStruct((nc, GDN_HEADS, rows, cols), dtype)
    row_shape = jax.ShapeDtypeStruct((nc, GDN_HEADS, LANES), F32)
    return pl.pallas_call(
        body, name="gdn_scan_bwd", grid=(nc // SCAN_CHUNKS,),
        in_specs=([rows_spec(0, hd), rows_spec(1, hd)] + pair(GDN_DIM, GDN_DIM, True) + pair(CHUNK, GDN_DIM)
                  + pair(CHUNK, GDN_DIM) + pair(CHUNK, GDN_DIM) + pair(CHUNK, CHUNK) + pair(CHUNK, GDN_DIM, True)
                  + pair(None, LANES)),
        out_specs=(pair(CHUNK, GDN_DIM, True) + pair(CHUNK, GDN_DIM, True) + pair(CHUNK, GDN_DIM, True)
                   + pair(CHUNK, GDN_DIM, True) + pair(None, LANES, True)),
        out_shape=[own(CHUNK, GDN_DIM, BF16)] * 4 + [own(CHUNK, GDN_DIM, F32)] * 4 + [row_shape] * 2,
        scratch_shapes=[pltpu.VMEM((N_CHAINS, GDN_DIM, GDN_DIM), F32)],
        compiler_params=_params(("arbitrary",), VMEM_LIMIT),
    )(do, do, s_f, s_b, w, w, qg, qg, kd, kd, intra, intra, vn_f, vn_b, egl, egl)


def _dot3_nt(a, b):
    ah = a.astype(BF16)
    al = (a - ah.astype(F32)).astype(BF16)
    bh = b.astype(BF16)
    bl = (b - bh.astype(F32)).astype(BF16)
    return _dot_nt(ah, bh) + (_dot_nt(ah, bl) + _dot_nt(al, bh))


def _dot3_tn(a, b):
    ah = a.astype(BF16)
    al = (a - ah.astype(F32)).astype(BF16)
    bh = b.astype(BF16)
    bl = (b - bh.astype(F32)).astype(BF16)
    return _dot_tn(ah, bh) + (_dot_tn(ah, bl) + _dot_tn(al, bh))


def _gdn_local_bwd(qkvc, gb, gbt, do, saved, scan, exchange=None):
    t = qkvc.shape[0]
    nc = t // CHUNK
    hd = GDN_HEADS * GDN_DIM

    def body(*refs):
        x_ref, g_ref, gt_ref, do_ref, t_ref = refs[:5]
        per_dir = refs[5:17]
        dx_ref, dg_ref = refs[17:]
        chains = [c for cc in range(LOCAL_CHUNKS) for c in _load_chains(x_ref, g_ref, gt_ref, cc)]
        lane = lax.broadcasted_iota(jnp.int32, (CHUNK, LANES), 1)
        dgates = [jnp.zeros((CHUNK, LANES), F32) for _ in range(LOCAL_CHUNKS)]
        for c in chains:
            d = c["ch"] // GDN_HEADS
            vn_ref, dvn_ref, dw_ref, dqg_ref, dkd_ref, dgl_ref = per_dir[d::2]
            h, cc = c["h"], c["cc"]
            rows = slice(cc * CHUNK, (cc + 1) * CHUNK)
            c.update(tm=t_ref[cc, c["ch"]], dov=do_ref[rows, h * GDN_DIM:(h + 1) * GDN_DIM], vnew=vn_ref[cc, h],
                     dvnew=dvn_ref[cc, h], dw=dw_ref[cc, h], dqg=dqg_ref[cc, h], dkdec=dkd_ref[cc, h],
                     dglast=dgl_ref[cc, h:h + 1, 0:1])
        dintras = [_dot_nt(c["dov"], c["vnew"]) for c in chains]
        dts = [_dot_nt(c["dvnew"], c["vb"]) + _dot_nt(c["dw"], c["kbg"]) for c in chains]
        dvbs = [_dot_tn(c["tm"], c["dvnew"]) for c in chains]
        dkbgs = [_dot_tn(c["tm"], c["dw"]) for c in chains]
        tdts = [_dot3_nt(dt, c["tm"]) for dt, c in zip(dts, chains)]
        dls = [jnp.where(c["masks"][3], -_dot3_tn(c["tm"], tdt), 0.0) for tdt, c in zip(tdts, chains)]
        das = [dl * c["decay"] for dl, c in zip(dls, chains)]
        dqks = [jnp.where(c["masks"][2], di, 0.0) * c["decay"] for di, c in zip(dintras, chains)]
        dkb1 = [_dot(da, c["k"]) for da, c in zip(das, chains)]
        dk1 = [_dot_tn(da, c["kb"]) for da, c in zip(das, chains)]
        dk2 = [_dot_tn(dqk, c["q"]) for dqk, c in zip(dqks, chains)]
        dq1 = [_dot(dqk, c["k"]) for dqk, c in zip(dqks, chains)]
        grads, mms, p_gs, p_betas, p_kds = [], [], [], [], []
        for n, c in enumerate(chains):
            incl = c["masks"][2]
            dkb = dkb1[n] + dkbgs[n] * c["eg"]
            kd = c["dkdec"] * c["kdec"]
            mms.append((dls[n] * c["amat"] + jnp.where(incl, dintras[n], 0.0) * c["qk"]) * c["decay"])
            p_gs.append(c["dqg"] * c["qg"] - kd + dkbgs[n] * c["kbg"])
            p_betas.append(dkb * c["k"] + dvbs[n] * c["v"])
            p_kds.append(kd)
            grads.append((dq1[n] + c["dqg"] * c["eg"],
                          dk1[n] + dk2[n] + c["dkdec"] * c["ek"] + dkb * c["bcol"],
                          dvbs[n] * c["bcol"]))
        row_sums = [jnp.sum(mm, axis=1, keepdims=True) for mm in mms]
        col_sums = [jnp.sum(mm, axis=0, keepdims=True) for mm in mms]
        g_sums = [jnp.sum(pg, axis=1, keepdims=True) for pg in p_gs]
        dbetas = [jnp.sum(pb, axis=1, keepdims=True) for pb in p_betas]
        kd_tots = [jnp.sum(jnp.sum(pk, axis=1, keepdims=True), axis=0, keepdims=True) for pk in p_kds]
        dgcs = [rs - _row_to_col(cs, *c["masks"][0:2]) + gs for rs, cs, gs, c in zip(row_sums, col_sums, g_sums, chains)]
        dgrs = [_col_to_row(dgc, *c["masks"][0:2]) for dgc, c in zip(dgcs, chains)]
        draws = [jnp.sum(jnp.where(jnp.logical_not(c["masks"][3]), dgr, 0.0), axis=1, keepdims=True) + c["dglast"] + kt
                 for dgr, kt, c in zip(dgrs, kd_tots, chains)]
        for c, draw, dbeta in zip(chains, draws, dbetas):
            ch = c["ch"]
            dgates[c["cc"]] = dgates[c["cc"]] + jnp.where(lane == ch, draw, 0.0) + jnp.where(lane == 8 + ch, dbeta, 0.0)
        for cc in range(LOCAL_CHUNKS):
            rows = slice(cc * CHUNK, (cc + 1) * CHUNK)
            for h in range(GDN_HEADS):
                for part in range(3):
                    cols = slice(part * hd + h * GDN_DIM, part * hd + (h + 1) * GDN_DIM)
                    dx_ref[rows, cols] = grads[cc * N_CHAINS + h][part] + grads[cc * N_CHAINS + GDN_HEADS + h][part]
            dg_ref[rows, :] = dgates[cc]

    lc = LOCAL_CHUNKS
    all8 = lambda rows, cols: pl.BlockSpec((lc, N_CHAINS, rows, cols), lambda n: (n, 0, 0, 0))
    own4 = lambda rows, cols: pl.BlockSpec((lc, GDN_HEADS, rows, cols), lambda n: (n, 0, 0, 0))
    row4 = pl.BlockSpec((lc, GDN_HEADS, LANES), lambda n: (n, 0, 0))
    vn_f, vn_b = saved["vn"]
    dvn_f, dvn_b, dw_f, dw_b, dqg_f, dqg_b, dkd_f, dkd_b, dgl_f, dgl_b = scan
    return _grid_call(
        body, "gdn_local_bwd", nc // lc,
        [pl.BlockSpec((lc * CHUNK, 3 * hd), lambda n: (n, 0)), pl.BlockSpec((lc * CHUNK, LANES), lambda n: (n, 0)),
         pl.BlockSpec((lc, 16, CHUNK), lambda n: (n, 0, 0)), pl.BlockSpec((lc * CHUNK, hd), lambda n: (n, 0)),
         all8(CHUNK, CHUNK)] + [own4(CHUNK, GDN_DIM)] * 10 + [row4, row4],
        [pl.BlockSpec((lc * CHUNK, 3 * hd), lambda n: (n, 0)), pl.BlockSpec((lc * CHUNK, LANES), lambda n: (n, 0))],
        [jax.ShapeDtypeStruct((t, 3 * hd), F32), jax.ShapeDtypeStruct((t, LANES), F32)],
        (qkvc, gb, gbt, do, saved["tm"], vn_f, vn_b, dvn_f, dvn_b, dw_f, dw_b, dqg_f, dqg_b, dkd_f, dkd_b, dgl_f, dgl_b),
        exchange=exchange)


def _gdn_post_fwd(of, ob, z, gw, tm):
    t, hd = of.shape

    def body(of_ref, ob_ref, z_ref, w_ref, o_ref):
        for h in range(GDN_HEADS):
            cols = slice(h * GDN_DIM, (h + 1) * GDN_DIM)
            o = of_ref[:, cols] + ob_ref[:, cols]
            zv = z_ref[:, cols]
            o_ref[:, cols] = (o * _rstd(o) * w_ref[...] * (zv * _sigmoid(zv))).astype(BF16)

    row = pl.BlockSpec((tm, hd), lambda i: (i, 0))
    return pl.pallas_call(
        body, name="gdn_post_fwd", grid=(t // tm,),
        in_specs=[row, row, row, _resident((1, GDN_DIM))],
        out_specs=row, out_shape=jax.ShapeDtypeStruct((t, hd), BF16),
        compiler_params=_params(("arbitrary",), VMEM_LIMIT),
    )(of, ob, z, gw)


def _gdn_post_bwd(doa, of, ob, z, gw, tm):
    t, hd = of.shape

    def body(d_ref, of_ref, ob_ref, z_ref, w_ref, do_ref, dz_ref, dw_ref):
        @pl.when(pl.program_id(0) == 0)
        def _():
            dw_ref[...] = jnp.zeros_like(dw_ref)

        dw = jnp.zeros((1, GDN_DIM), F32)
        for h in range(GDN_HEADS):
            cols = slice(h * GDN_DIM, (h + 1) * GDN_DIM)
            o = of_ref[:, cols] + ob_ref[:, cols]
            zv = z_ref[:, cols]
            dv = d_ref[:, cols]
            r = _rstd(o)
            sg = _sigmoid(zv)
            on = o * r * w_ref[...]
            dz_ref[:, cols] = (dv * on * (sg * (1.0 + zv * (1.0 - sg)))).astype(BF16)
            dxr, dwh = _rms_bwd(o, r, w_ref[...], dv * (zv * sg))
            do_ref[:, cols] = dxr
            dw = dw + dwh
        dw_ref[...] += dw

    row = pl.BlockSpec((tm, hd), lambda i: (i, 0))
    return pl.pallas_call(
        body, name="gdn_post_bwd", grid=(t // tm,),
        in_specs=[row, row, row, row, _resident((1, GDN_DIM))],
        out_specs=[row, row, pl.BlockSpec((1, GDN_DIM), lambda i: (0, 0))],
        out_shape=[jax.ShapeDtypeStruct((t, hd), F32), jax.ShapeDtypeStruct((t, hd), BF16),
                   jax.ShapeDtypeStruct((1, GDN_DIM), F32)],
        compiler_params=_params(("arbitrary",), VMEM_LIMIT),
    )(doa, of, ob, z, gw)


SWA_W = SWA_HEADS * SWA_DIM
QBLK = 128
KWIN = QBLK + 2 * RADIUS
WIN_OFFSETS = (0, RADIUS, 2 * RADIUS)


def _t5_bucket(rel):
    nb = REL_BUCKETS // 2
    bucket = (rel > 0).astype(np.int32) * nb
    n = np.abs(rel)
    max_exact = nb // 2
    large = max_exact + (np.log(np.maximum(n, 1) / max_exact)
                         / math.log(REL_MAX_DISTANCE / max_exact) * (nb - max_exact)).astype(np.int32)
    large = np.minimum(large, nb - 1)
    return (bucket + np.where(n < max_exact, n, large)).astype(np.int32)


def _band_tables(dilation):
    a = np.arange(QBLK)
    b = np.arange(KWIN)
    rel = np.stack([b[None, :] - w0 - a[:, None] for w0 in WIN_OFFSETS])
    return _t5_bucket(rel * dilation), np.abs(rel) <= RADIUS


def _bias_table(rel_bias, idx, valid):
    onehot = (jnp.arange(REL_BUCKETS, dtype=jnp.int32)[:, None] == jnp.asarray(idx.reshape(1, -1))).astype(F32)
    tab = jnp.dot(rel_bias.T, onehot, precision=HIGHEST)
    tab = jnp.where(jnp.asarray(valid.reshape(1, -1)), tab, NEG_BIG)
    return tab.reshape((SWA_HEADS,) + idx.shape), onehot


def _head_mean(x2, bd_ref):
    return _dot_hi(x2, bd_ref[...])


VIEW_DILATIONS = tuple(d for _, d in PATTERNS if d > 1)


def _view_spec(tm, d):
    return pl.BlockSpec((tm // d, d * SWA_W), lambda i: (i, 0))


def _view_shape(t, d, dtype):
    return jax.ShapeDtypeStruct((t // d, d * SWA_W), dtype)


N_GROUPS = SWA_W // LANES


def _to_view(src_ref, idx, dst_ref, d, rows):
    for r in range(d):
        for g in range(N_GROUPS):
            cols = slice(r * SWA_W + g * LANES, r * SWA_W + (g + 1) * LANES)
            dst_ref[:, cols] = src_ref[idx, g, pl.ds(r, rows // d, stride=d), :].astype(dst_ref.dtype)


def _from_view(src_ref, dst_ref, idx, d, rows):
    for r in range(d):
        for g in range(N_GROUPS):
            cols = slice(r * SWA_W + g * LANES, r * SWA_W + (g + 1) * LANES)
            dst_ref[idx, g, pl.ds(r, rows // d, stride=d), :] = src_ref[:, cols]


def _swa_prep_fwd(qkvb, qw, kw, bd, tm):
    t = qkvb.shape[0]

    def body(x_ref, qw_ref, kw_ref, bd_ref, *rest):
        outs, sc = rest[:-1], rest[-1]
        for gidx in range(N_GROUPS):
            cols = slice(gidx * LANES, (gidx + 1) * LANES)
            xq = x_ref[:, cols]
            sc[0, gidx] = xq * lax.rsqrt(_head_mean(xq * xq, bd_ref) + EPS) * qw_ref[:, cols] * (SWA_DIM ** -0.5)
            xk = x_ref[:, SWA_W + gidx * LANES:SWA_W + (gidx + 1) * LANES]
            sc[1, gidx] = xk * lax.rsqrt(_head_mean(xk * xk, bd_ref) + EPS) * kw_ref[:, cols]
            sc[2, gidx] = x_ref[:, 2 * SWA_W + gidx * LANES:2 * SWA_W + (gidx + 1) * LANES]
            for i in range(3):
                outs[i][:, cols] = sc[i, gidx].astype(BF16)
        for i in range(3):
            for n, d in enumerate(VIEW_DILATIONS):
                _to_view(sc, i, outs[3 * (n + 1) + i], d, tm)

    return pl.pallas_call(
        body, name="swa_prep_fwd", grid=(t // tm,),
        in_specs=[pl.BlockSpec((tm, 3 * SWA_W), lambda i: (i, 0)), _resident((1, SWA_W)), _resident((1, SWA_W)),
                  _resident((LANES, LANES))],
        out_specs=[_view_spec(tm, d) for d in (1,) + VIEW_DILATIONS for _ in range(3)],
        out_shape=[_view_shape(t, d, BF16) for d in (1,) + VIEW_DILATIONS for _ in range(3)],
        scratch_shapes=[pltpu.VMEM((3, N_GROUPS, tm, LANES), F32)],
        compiler_params=_params(("arbitrary",), VMEM_LIMIT),
    )(qkvb, qw, kw, bd)


def _swa_prep_bwd(qkvb, qw, kw, bd, grads, tm):
    t = qkvb.shape[0]

    def body(x_ref, qw_ref, kw_ref, bd_ref, *rest):
        parts, (dx_ref, dqw_ref, dkw_ref, sc) = rest[:9], rest[9:]
        @pl.when(pl.program_id(0) == 0)
        def _():
            dqw_ref[...] = jnp.zeros_like(dqw_ref)
            dkw_ref[...] = jnp.zeros_like(dkw_ref)

        for i in range(3):
            for n, d in enumerate(VIEW_DILATIONS):
                _from_view(parts[3 * (n + 1) + i], sc, 2 * i + n, d, tm)
        for gidx in range(N_GROUPS):
            cols = slice(gidx * LANES, (gidx + 1) * LANES)
            for i, base, w_ref, dw_ref, scale in ((0, 0, qw_ref, dqw_ref, SWA_DIM ** -0.5),
                                                  (1, SWA_W, kw_ref, dkw_ref, 1.0)):
                xv = x_ref[:, base + gidx * LANES:base + (gidx + 1) * LANES]
                dy = (parts[i][:, cols] + sc[2 * i, gidx] + sc[2 * i + 1, gidx]) * scale
                r = lax.rsqrt(_head_mean(xv * xv, bd_ref) + EPS)
                xhat = xv * r
                dxh = dy * w_ref[:, cols]
                dx = r * (dxh - xhat * _head_mean(dxh * xhat, bd_ref))
                dx_ref[:, base + gidx * LANES:base + (gidx + 1) * LANES] = dx.astype(BF16)
                dw_ref[:, cols] += jnp.sum(dy * xhat, axis=0, keepdims=True)
            dx_ref[:, 2 * SWA_W + gidx * LANES:2 * SWA_W + (gidx + 1) * LANES] = (
                parts[2][:, cols] + sc[4, gidx] + sc[5, gidx]).astype(BF16)

    wrow = pl.BlockSpec((1, SWA_W), lambda i: (0, 0))
    return pl.pallas_call(
        body, name="swa_prep_bwd", grid=(t // tm,),
        in_specs=[pl.BlockSpec((tm, 3 * SWA_W), lambda i: (i, 0)), _resident((1, SWA_W)), _resident((1, SWA_W)),
                  _resident((LANES, LANES))] + [_view_spec(tm, d) for d in (1,) + VIEW_DILATIONS for _ in range(3)],
        out_specs=[pl.BlockSpec((tm, 3 * SWA_W), lambda i: (i, 0)), wrow, wrow],
        out_shape=[jax.ShapeDtypeStruct((t, 3 * SWA_W), BF16), jax.ShapeDtypeStruct((1, SWA_W), F32),
                   jax.ShapeDtypeStruct((1, SWA_W), F32)],
        scratch_shapes=[pltpu.VMEM((6, N_GROUPS, tm, LANES), F32)],
        compiler_params=_params(("arbitrary",), VMEM_LIMIT),
    )(qkvb, qw, kw, bd, *grads)


def _aligned(v, m):
    return v if isinstance(v, int) else pl.multiple_of(v, m)


BAND_GROUP = 2


def _band_loop(nsub, length, step):
    step([(0, 0)], 0)
    if nsub > 2:
        assert (nsub - 2) % BAND_GROUP == 0

        def inner(i, carry):
            s0 = 1 + i * BAND_GROUP
            step([(s0 + e, pl.multiple_of((s0 + e) * QBLK - RADIUS, RADIUS)) for e in range(BAND_GROUP)], 1)
            return carry
        lax.fori_loop(0, (nsub - 2) // BAND_GROUP, inner, 0)
    step([(nsub - 1, length - KWIN)], 2)


def _head_select(lane, a0, a1):
    return jnp.where(lane < SWA_DIM, a0, a1)


def _swa_fwd(qv, kv, vv, bias, dilation, name):
    length = qv.shape[0]
    nsub = length // QBLK
    assert nsub >= 2 and length % QBLK == 0

    def body(q_ref, k_ref, v_ref, b_ref, o_ref, l_ref):
        lane = lax.broadcasted_iota(jnp.int32, (QBLK, LANES), 1)

        def step(blocks, var):
            items = []
            for s, ws in blocks:
                rows = pl.ds(_aligned(s * QBLK, QBLK), QBLK)
                q, kk, vw = q_ref[rows, :], k_ref[pl.ds(ws, KWIN), :], v_ref[pl.ds(ws, KWIN), :]
                for hh in range(2):
                    items.append((hh, jnp.where((lane < SWA_DIM) == (hh == 0), q, jnp.zeros_like(q)), kk, vw))
            lgs = [_dot_nt(qh, kk) + b_ref[hh, var] for hh, qh, kk, _ in items]
            ms = [jnp.max(lg, axis=-1, keepdims=True) for lg in lgs]
            ps = [jnp.exp(lg - m) for lg, m in zip(lgs, ms)]
            dens = [jnp.sum(p, axis=-1, keepdims=True) for p in ps]
            pvs = [_dot(p, it[3]) for p, it in zip(ps, items)]
            for n, (s, _) in enumerate(blocks):
                rows = pl.ds(_aligned(s * QBLK, QBLK), QBLK)
                o0, o1 = (pvs[2 * n + hh] / dens[2 * n + hh] for hh in range(2))
                l0, l1 = (ms[2 * n + hh] + jnp.log(dens[2 * n + hh]) for hh in range(2))
                o_ref[rows, :] = _head_select(lane, o0, o1)
                l_ref[rows, :] = _head_select(lane, l0, l1)

        _band_loop(nsub, length, step)

    blk = pl.BlockSpec((length, LANES), lambda hp, r: (0, r * (SWA_W // LANES) + hp))
    shp = jax.ShapeDtypeStruct(qv.shape, F32)
    return pl.pallas_call(
        body, name=name, grid=(SWA_W // LANES, dilation),
        in_specs=[blk, blk, blk, pl.BlockSpec((2, 3, QBLK, KWIN), lambda hp, r: (hp, 0, 0, 0))],
        out_specs=[blk, blk], out_shape=[shp, shp],
        compiler_params=_params(("arbitrary", "arbitrary"), VMEM_LIMIT),
    )(qv, kv, vv, bias)


def _swa_combine(os_, ls_, tm):
    t = os_[0].shape[0]

    def body(o0, o1, o2, l0, l1, l2, o_ref, ob_ref, la_ref, lb_ref, lc_ref, sc):
        for n, d in enumerate(VIEW_DILATIONS):
            _from_view((o1, o2)[n], sc, n, d, tm)
            _from_view((l1, l2)[n], sc, 2 + n, d, tm)
        for g in range(N_GROUPS):
            cols = slice(g * LANES, (g + 1) * LANES)
            la, lb, lc = l0[:, cols], sc[2, g], sc[3, g]
            m = jnp.maximum(jnp.maximum(la, lb), lc)
            tot = m + jnp.log(jnp.exp(la - m) + jnp.exp(lb - m) + jnp.exp(lc - m))
            o = jnp.exp(la - tot) * o0[:, cols] + jnp.exp(lb - tot) * sc[0, g] + jnp.exp(lc - tot) * sc[1, g]
            o_ref[:, cols] = o
            ob_ref[:, cols] = o.astype(BF16)
            la_ref[:, cols] = tot
            sc[4, g] = tot
        for n, d in enumerate(VIEW_DILATIONS):
            _to_view(sc, 4, (lb_ref, lc_ref)[n], d, tm)

    specs = [_view_spec(tm, d) for d in (1,) + VIEW_DILATIONS]
    return pl.pallas_call(
        body, name="swa_combine", grid=(t // tm,), in_specs=specs + specs, out_specs=[specs[0], specs[0]] + specs,
        out_shape=[jax.ShapeDtypeStruct((t, SWA_W), F32), jax.ShapeDtypeStruct((t, SWA_W), BF16)]
                  + [_view_shape(t, d, F32) for d in (1,) + VIEW_DILATIONS],
        scratch_shapes=[pltpu.VMEM((5, N_GROUPS, tm, LANES), F32)],
        compiler_params=_params(("arbitrary",), VMEM_LIMIT),
    )(*os_, *ls_)


def _swa_bwd_prep(do, o, bd, tm):
    t = do.shape[0]

    def body(d_ref, o_ref, bd_ref, dd1, dd4, dd16, db1, db4, db16, sc):
        for gidx in range(N_GROUPS):
            cols = slice(gidx * LANES, (gidx + 1) * LANES)
            dv = d_ref[:, cols]
            dd = _head_mean(dv * o_ref[:, cols], bd_ref) * float(SWA_DIM)
            sc[0, gidx] = dd
            sc[1, gidx] = dv
            dd1[:, cols] = dd
            db1[:, cols] = dv.astype(BF16)
        for n, d in enumerate(VIEW_DILATIONS):
            _to_view(sc, 0, (dd4, dd16)[n], d, tm)
            _to_view(sc, 1, (db4, db16)[n], d, tm)

    specs = [_view_spec(tm, d) for d in (1,) + VIEW_DILATIONS]
    return pl.pallas_call(
        body, name="swa_bwd_prep", grid=(t // tm,), in_specs=[specs[0], specs[0], _resident((LANES, LANES))],
        out_specs=specs + specs,
        out_shape=[_view_shape(t, d, F32) for d in (1,) + VIEW_DILATIONS]
                  + [_view_shape(t, d, BF16) for d in (1,) + VIEW_DILATIONS],
        scratch_shapes=[pltpu.VMEM((2, N_GROUPS, tm, LANES), F32)],
        compiler_params=_params(("arbitrary",), VMEM_LIMIT),
    )(do, o, bd)


def _swa_bwd(qv, kv, vv, dov, lv, ddv, bias_a, dilation, name):
    length = qv.shape[0]
    nsub = length // QBLK
    single = pl.Buffered(1) if dilation == 1 else None

    def body(q_ref, k_ref, v_ref, do_ref, l_ref, dd_ref, ba_ref, dq_ref, dk_ref, dv_ref, db_ref):
        @pl.when(pl.program_id(1) == 0)
        def _():
            db_ref[...] = jnp.zeros_like(db_ref)

        lane = lax.broadcasted_iota(jnp.int32, (QBLK, LANES), 1)
        lanew = lax.broadcasted_iota(jnp.int32, (KWIN, LANES), 1)

        def step(blocks, var):
            items = []
            for s, ws in blocks:
                rows = pl.ds(_aligned(s * QBLK, QBLK), QBLK)
                win = pl.ds(ws, KWIN)
                q, dov_ = q_ref[rows, :], do_ref[rows, :]
                kk, vw = k_ref[win, :], v_ref[win, :]
                lse, dd = l_ref[rows, :], dd_ref[rows, :]
                for hh in range(2):
                    mine = (lane < SWA_DIM) == (hh == 0)
                    col = slice(hh * SWA_DIM, hh * SWA_DIM + 1)
                    items.append((hh, jnp.where(mine, q, jnp.zeros_like(q)), jnp.where(mine, dov_, jnp.zeros_like(dov_)),
                                  kk, vw, lse[:, col], dd[:, col], q, dov_))
            lgs = [_dot_nt(it[1], it[3]) + ba_ref[it[0], var] for it in items]
            dps = [_dot_nt(it[2], it[4]) for it in items]
            ps = [jnp.exp(lg - it[5]) for lg, it in zip(lgs, items)]
            dss = [p * (dp - it[6]) for p, dp, it in zip(ps, dps, items)]
            dqs = [_dot(ds, it[3]) for ds, it in zip(dss, items)]
            dks = [_dot_tn(ds, it[7]) for ds, it in zip(dss, items)]
            dvs = [_dot_tn(p, it[8]) for p, it in zip(ps, items)]
            for n, (s, ws) in enumerate(blocks):
                rows = pl.ds(_aligned(s * QBLK, QBLK), QBLK)
                win = pl.ds(ws, KWIN)
                dq_ref[rows, :] = _head_select(lane, dqs[2 * n], dqs[2 * n + 1])
                dk_ref[win, :] += _head_select(lanew, dks[2 * n], dks[2 * n + 1])
                dv_ref[win, :] += _head_select(lanew, dvs[2 * n], dvs[2 * n + 1])
            for hh in range(2):
                tot = dss[hh]
                for n in range(1, len(blocks)):
                    tot = tot + dss[2 * n + hh]
                db_ref[hh, var] += tot

        dk_ref[...] = jnp.zeros_like(dk_ref)
        dv_ref[...] = jnp.zeros_like(dv_ref)
        _band_loop(nsub, length, step)

    imap = lambda hp, r: (0, r * (SWA_W // LANES) + hp)
    blk_in = pl.BlockSpec((length, LANES), imap, pipeline_mode=single)
    blk_out = pl.BlockSpec((length, LANES), imap)
    shp = jax.ShapeDtypeStruct(qv.shape, F32)
    return pl.pallas_call(
        body, name=name, grid=(SWA_W // LANES, dilation),
        in_specs=[blk_in] * 6 + [pl.BlockSpec((2, 3, QBLK, KWIN), lambda hp, r: (hp, 0, 0, 0))],
        out_specs=[blk_out, blk_out, blk_out, pl.BlockSpec((2, 3, QBLK, KWIN), lambda hp, r: (hp, 0, 0, 0))],
        out_shape=[shp, shp, shp, jax.ShapeDtypeStruct((SWA_HEADS, 3, QBLK, KWIN), F32)],
        compiler_params=_params(("arbitrary", "arbitrary"), VMEM_LIMIT),
    )(qv, kv, vv, dov, lv, ddv, bias_a)


def _bias_grad(ds2, onehot, tk):
    n = ds2.shape[1]
    nk = n // tk

    def body(a_ref, b_ref, o_ref):
        @pl.when(pl.program_id(0) == 0)
        def _():
            o_ref[...] = jnp.zeros_like(o_ref)

        o_ref[...] += lax.dot_general(a_ref[...], b_ref[...], (((1,), (1,)), ((), ())), precision=HIGHEST,
                                      preferred_element_type=F32)

    return pl.pallas_call(
        body, name="bias_grad", grid=(nk,),
        in_specs=[pl.BlockSpec((SWA_HEADS, tk), lambda k: (0, k)), pl.BlockSpec((REL_BUCKETS, tk), lambda k: (0, k))],
        out_specs=pl.BlockSpec((SWA_HEADS, REL_BUCKETS), lambda k: (0, 0)),
        out_shape=jax.ShapeDtypeStruct((SWA_HEADS, REL_BUCKETS), F32),
        compiler_params=_params(("arbitrary",), VMEM_LIMIT),
    )(ds2, onehot)


def _swa_branch_fwd(qkvb, qw_t, kw_t, rel_bias, bd, tm):
    qkv = _swa_prep_fwd(qkvb, qw_t, kw_t, bd, tm)
    os_, ls_, tabs = [], [], []
    for n, (_, d) in enumerate(PATTERNS):
        bias, onehot = _bias_table(rel_bias, *_band_tables(d))
        o_p, l_p = _swa_fwd(*qkv[3 * n:3 * n + 3], bias, d, f"swa_fwd_d{d}")
        os_.append(o_p)
        ls_.append(l_p)
        tabs.append((bias, onehot))
    o, o16, *lses = _swa_combine(os_, ls_, tm)
    return o, o16, (qkv, lses, tabs)


def _swa_branch_bwd(do, o, saved, qkvb, qw_t, kw_t, bd, tm):
    qkv, lses, tabs = saved
    prep = _swa_bwd_prep(do, o, bd, tm)
    grads, dss, ohs = [], [], []
    for n, ((_, d), (bias, onehot)) in enumerate(zip(PATTERNS, tabs)):
        dq, dk, dv, ds = _swa_bwd(*qkv[3 * n:3 * n + 3], prep[3 + n], lses[n], prep[n], bias, d, f"swa_bwd_d{d}")
        grads += [dq, dk, dv]
        dss.append(ds.reshape(SWA_HEADS, -1))
        ohs.append(onehot)
    dqkvb, dqw, dkw = _swa_prep_bwd(qkvb, qw_t, kw_t, bd, grads, tm)
    dbias = _bias_grad(jnp.concatenate(dss, axis=1), jnp.concatenate(ohs, axis=1), 8192)
    fold = lambda w: jnp.sum(w.reshape(SWA_HEADS, SWA_DIM), axis=0)
    return dqkvb, fold(dqw), fold(dkw), dbias.T


def _mesh_pos():
    return lax.axis_index("x"), lax.axis_index("y"), lax.axis_index("c")


def _other_chips(x, y):
    return [(1 - x, y), (x, 1 - y), (1 - x, 1 - y)]


def _remote(src, dst, send_sem, recv_sem, device):
    return pltpu.make_async_remote_copy(src_ref=src, dst_ref=dst, send_sem=send_sem, recv_sem=recv_sem,
                                        device_id=device, device_id_type=MESH)


def _all_gather(xs):
    n = len(xs)

    def body(*refs):
        ins, outs = refs[:n], refs[n:2 * n]
        send_sems, recv_sems = refs[2 * n:]
        x, y, c = _mesh_pos()
        me = 2 * x + y
        chips = _other_chips(x, y)
        halves = []
        sends = []
        for a in range(n):
            h = ins[a].shape[0] // 2
            mine, other = pl.ds(c * h, h), pl.ds((1 - c) * h, h)
            halves.append((mine, other))
            for j, chip in enumerate(chips):
                cp = _remote(ins[a].at[mine], outs[a].at[me, mine], send_sems.at[a, j], recv_sems.at[a, j], (*chip, c))
                cp.start()
                sends.append(cp)
        for a in range(n):
            mine, _ = halves[a]
            for j, chip in enumerate(chips):
                src = 2 * chip[0] + chip[1]
                landed = outs[a].at[src, mine]
                _remote(landed, landed, send_sems.at[a, j], recv_sems.at[a, j], (x, y, c)).wait_recv()
                fwd = _remote(landed, landed, send_sems.at[a, 3 + j], recv_sems.at[a, 3 + j], (x, y, 1 - c))
                fwd.start()
                sends.append(fwd)
        for a in range(n):
            _, other = halves[a]
            for j, chip in enumerate(chips):
                src = 2 * chip[0] + chip[1]
                landed = outs[a].at[src, other]
                _remote(landed, landed, send_sems.at[a, 3 + j], recv_sems.at[a, 3 + j], (x, y, c)).wait_recv()
        for cp in sends:
            cp.wait_send()

    outs = pl.pallas_call(
        body, name="all_gather_weights",
        in_specs=[ANY] * n, out_specs=[ANY] * n,
        out_shape=[jax.ShapeDtypeStruct((N_SHARDS,) + a.shape, a.dtype) for a in xs],
        scratch_shapes=[pltpu.SemaphoreType.DMA((n, 6)), pltpu.SemaphoreType.DMA((n, 6))],
    )(*xs)
    me = 2 * lax.axis_index("x") + lax.axis_index("y")
    return [lax.dynamic_update_slice_in_dim(o, a[None], me, 0) for o, a in zip(outs, xs)]


def _rs_pair(gs):
    n = len(gs)

    def body(*refs):
        ins, lands = refs[:n], refs[n:2 * n]
        send_sems, recv_sems = refs[2 * n:]
        x, y, c = _mesh_pos()
        cps = []
        for a in range(n):
            h = ins[a].shape[1] // 2
            cp = _remote(ins[a].at[:, pl.ds((1 - c) * h, h), :], lands[a], send_sems.at[a], recv_sems.at[a],
                         (x, y, 1 - c))
            cp.start()
            cps.append(cp)
        for cp in cps:
            cp.wait()

    half = [jax.ShapeDtypeStruct((N_SHARDS, g.shape[1] // 2, g.shape[2]), g.dtype) for g in gs]
    lands = pl.pallas_call(
        body, name="rs_pair", in_specs=[ANY] * n, out_specs=[ANY] * n, out_shape=half,
        scratch_shapes=[pltpu.SemaphoreType.DMA((n,)), pltpu.SemaphoreType.DMA((n,))],
    )(*gs)
    c = lax.axis_index("c")
    owns = [lax.dynamic_slice_in_dim(g, c * (g.shape[1] // 2), g.shape[1] // 2, 1) for g in gs]
    return owns + list(lands)


def _rs_chips(ss):
    n = len(ss)

    def body(*refs):
        ins, outs = refs[:n], refs[n:2 * n]
        send_sems, recv_sems = refs[2 * n:]
        x, y, c = _mesh_pos()
        me = 2 * x + y
        chips = _other_chips(x, y)
        cps = []
        for a in range(n):
            for j, chip in enumerate(chips):
                dst_chip = 2 * chip[0] + chip[1]
                cp = _remote(ins[a].at[dst_chip], outs[a].at[me], send_sems.at[a, j], recv_sems.at[a, j], (*chip, c))
                cp.start()
                cps.append(cp)
        for a in range(n):
            for j, chip in enumerate(chips):
                src = 2 * chip[0] + chip[1]
                _remote(outs[a].at[src], outs[a].at[src], send_sems.at[a, j], recv_sems.at[a, j], (x, y, c)).wait_recv()
        for cp in cps:
            cp.wait_send()

    outs = pl.pallas_call(
        body, name="rs_chips", in_specs=[ANY] * n, out_specs=[ANY] * n,
        out_shape=[jax.ShapeDtypeStruct(s.shape, s.dtype) for s in ss],
        scratch_shapes=[pltpu.SemaphoreType.DMA((n, 3)), pltpu.SemaphoreType.DMA((n, 3))],
    )(*ss)
    me = 2 * lax.axis_index("x") + lax.axis_index("y")
    return [lax.dynamic_update_slice_in_dim(o, lax.dynamic_slice_in_dim(s, me, 1, 0), me, 0) for o, s in zip(outs, ss)]


def _rs_join(fs):
    n = len(fs)

    def body(*refs):
        ins, outs = refs[:n], refs[n:2 * n]
        send_sems, recv_sems = refs[2 * n:]
        x, y, c = _mesh_pos()
        cps = []
        for a in range(n):
            h = ins[a].shape[0]
            cp = _remote(ins[a], outs[a].at[pl.ds(c * h, h)], send_sems.at[a], recv_sems.at[a], (x, y, 1 - c))
            cp.start()
            cps.append(cp)
        for cp in cps:
            cp.wait()

    outs = pl.pallas_call(
        body, name="rs_join", in_specs=[ANY] * n, out_specs=[ANY] * n,
        out_shape=[jax.ShapeDtypeStruct((2 * f.shape[0], f.shape[1]), f.dtype) for f in fs],
        scratch_shapes=[pltpu.SemaphoreType.DMA((n,)), pltpu.SemaphoreType.DMA((n,))],
    )(*fs)
    c = lax.axis_index("c")
    return [lax.dynamic_update_slice_in_dim(o, f, c * f.shape[0], 0) for o, f in zip(outs, fs)]


def _gather_exchange(xs):
    def start(cin, cout, send_sems, recv_sems):
        x, y, c = _mesh_pos()
        me = 2 * x + y
        for a, (src, dst) in enumerate(zip(cin, cout)):
            h = src.shape[0] // 2
            mine = pl.ds(c * h, h)
            for j, chip in enumerate(_other_chips(x, y)):
                _remote(src.at[mine], dst.at[me, mine], send_sems.at[a, j], recv_sems.at[a, j], (*chip, c)).start()

    def finish(cin, cout, send_sems, recv_sems):
        x, y, c = _mesh_pos()
        for a, dst in enumerate(cout):
            h = dst.shape[1] // 2
            for j, chip in enumerate(_other_chips(x, y)):
                landed = dst.at[2 * chip[0] + chip[1], pl.ds(c * h, h)]
                _remote(landed, landed, send_sems.at[a, j], recv_sems.at[a, j], (x, y, c)).wait()

    return _Exchange(tuple(xs), tuple(jax.ShapeDtypeStruct((N_SHARDS,) + a.shape, a.dtype) for a in xs), start, finish)


def _gather_forward(gs, xs):
    n = len(gs)

    def body(*refs):
        outs = refs[n:2 * n]
        send_sems, recv_sems = refs[2 * n:]
        x, y, c = _mesh_pos()
        chips = _other_chips(x, y)
        cps = []
        for a in range(n):
            h = outs[a].shape[1] // 2
            for j, chip in enumerate(chips):
                landed = outs[a].at[2 * chip[0] + chip[1], pl.ds(c * h, h)]
                cp = _remote(landed, landed, send_sems.at[a, j], recv_sems.at[a, j], (x, y, 1 - c))
                cp.start()
                cps.append(cp)
        for a in range(n):
            h = outs[a].shape[1] // 2
            for j, chip in enumerate(chips):
                other = outs[a].at[2 * chip[0] + chip[1], pl.ds((1 - c) * h, h)]
                _remote(other, other, send_sems.at[a, j], recv_sems.at[a, j], (x, y, c)).wait_recv()
        for cp in cps:
            cp.wait_send()

    outs = pl.pallas_call(
        body, name="gather_forward", in_specs=[ANY] * n, out_specs=[ANY] * n,
        out_shape=[jax.ShapeDtypeStruct(g.shape, g.dtype) for g in gs],
        input_output_aliases={i: i for i in range(n)},
        scratch_shapes=[pltpu.SemaphoreType.DMA((n, 3)), pltpu.SemaphoreType.DMA((n, 3))],
    )(*gs)
    me = 2 * lax.axis_index("x") + lax.axis_index("y")
    return [lax.dynamic_update_slice_in_dim(o, a[None], me, 0) for o, a in zip(outs, xs)]


def _scatter_exchange(ss):
    def start(cin, cout, send_sems, recv_sems):
        x, y, c = _mesh_pos()
        me = 2 * x + y
        for a, (src, dst) in enumerate(zip(cin, cout)):
            for j, chip in enumerate(_other_chips(x, y)):
                _remote(src.at[2 * chip[0] + chip[1]], dst.at[me], send_sems.at[a, j], recv_sems.at[a, j],
                        (*chip, c)).start()

    def finish(cin, cout, send_sems, recv_sems):
        x, y, c = _mesh_pos()
        for a, dst in enumerate(cout):
            for j, chip in enumerate(_other_chips(x, y)):
                slot = dst.at[2 * chip[0] + chip[1]]
                _remote(slot, slot, send_sems.at[a, j], recv_sems.at[a, j], (x, y, c)).wait()

    return _Exchange(tuple(ss), tuple(jax.ShapeDtypeStruct(s.shape, s.dtype) for s in ss), start, finish)


def _own_slots(slots, ss):
    me = 2 * lax.axis_index("x") + lax.axis_index("y")
    return [lax.dynamic_update_slice_in_dim(o, lax.dynamic_slice_in_dim(s, me, 1, 0), me, 0) for o, s in zip(slots, ss)]


def _add_pair(a, b, name):
    nj, h, c = a.shape

    def body(a_ref, b_ref, o_ref):
        o_ref[...] = (a_ref[...].astype(F32) + b_ref[...].astype(F32)).astype(BF16)

    blk = pl.BlockSpec((1, h, c), lambda j: (j, 0, 0))
    return pl.pallas_call(body, name=name, grid=(nj,), in_specs=[blk, blk], out_specs=blk,
                          out_shape=jax.ShapeDtypeStruct(a.shape, BF16),
                          compiler_params=_params(("arbitrary",), VMEM_LIMIT))(a, b)


def _sum_slots(l2, name):
    nj, h, c = l2.shape
    th = h // 2 if h % 32 == 0 else h

    def body(i_ref, o_ref):
        acc = i_ref[0].astype(F32)
        for s in range(1, nj):
            acc = acc + i_ref[s].astype(F32)
        o_ref[...] = acc

    return pl.pallas_call(body, name=name, grid=(h // th,),
                          in_specs=[pl.BlockSpec((nj, th, c), lambda i: (0, i, 0))],
                          out_specs=pl.BlockSpec((th, c), lambda i: (i, 0)),
                          out_shape=jax.ShapeDtypeStruct((h, c), F32),
                          compiler_params=_params(("arbitrary",), VMEM_LIMIT))(l2)


def _all_reduce_small(p):
    r = p.shape[0]

    def body(p_ref, o_ref, buf, send_sems, recv_sems):
        x, y, c = _mesh_pos()
        me = 4 * x + 2 * y + c
        buf[me] = p_ref[...]
        cps = []
        k = 0
        for fx in range(2):
            for fy in range(2):
                for fc in range(2):
                    if fx + fy + fc == 0:
                        continue
                    peer = (1 - x if fx else x, 1 - y if fy else y, 1 - c if fc else c)
                    peer_id = 4 * peer[0] + 2 * peer[1] + peer[2]
                    cp = _remote(p_ref, buf.at[me], send_sems.at[k], recv_sems.at[k], peer)
                    cp.start()
                    cps.append((cp, peer_id, k))
                    k += 1
        for cp, peer_id, k in cps:
            _remote(p_ref, buf.at[peer_id], send_sems.at[k], recv_sems.at[k], (x, y, c)).wait_recv()
        for cp, _, _ in cps:
            cp.wait_send()
        acc = buf[0]
        for s in range(1, 8):
            acc = acc + buf[s]
        o_ref[...] = acc

    vm = pl.BlockSpec(memory_space=pltpu.VMEM)
    return pl.pallas_call(
        body, name="all_reduce_small", in_specs=[vm], out_specs=vm,
        out_shape=jax.ShapeDtypeStruct(p.shape, F32),
        scratch_shapes=[pltpu.VMEM((8, r, LANES), F32), pltpu.SemaphoreType.DMA((7,)), pltpu.SemaphoreType.DMA((7,))],
    )(p)


def _adamw(w, g, m, v, name):
    r, c = w.shape
    row_tiles = [d for d in range(8, min(r, 256) + 1, 8) if r % d == 0]
    tr, tc = (max(row_tiles), c) if row_tiles else (r, 256 if c % 256 == 0 else c)
    c1 = 1.0 / (1.0 - ADAM_B1 ** ADAM_STEP)
    c2 = 1.0 / (1.0 - ADAM_B2 ** ADAM_STEP)

    def body(w_ref, g_ref, m_ref, v_ref, d_ref, nm_ref, nv_ref):
        gv = g_ref[...]
        nm = ADAM_B1 * m_ref[...] + (1.0 - ADAM_B1) * gv
        nv = ADAM_B2 * v_ref[...] + (1.0 - ADAM_B2) * (gv * gv)
        d_ref[...] = -ADAM_LR * ((nm * c1) / (jnp.sqrt(nv * c2) + ADAM_EPS) + ADAM_WD * w_ref[...])
        nm_ref[...] = nm
        nv_ref[...] = nv

    blk = pl.BlockSpec((tr, tc), lambda i, j: (i, j))
    shp = jax.ShapeDtypeStruct((r, c), F32)
    return pl.pallas_call(body, name=name, grid=(r // tr, c // tc), in_specs=[blk] * 4, out_specs=[blk] * 3,
                          out_shape=[shp, shp, shp],
                          compiler_params=_params(("arbitrary", "arbitrary"), VMEM_LIMIT))(w, g, m, v)


PACK_UNIT = 8 * LANES


def _pack(arrs):
    parts = []
    for a in arrs:
        f = a.reshape(-1).astype(F32)
        parts.append(jnp.pad(f, (0, (-f.shape[0]) % PACK_UNIT)).reshape(-1, LANES))
    return jnp.concatenate(parts, axis=0)


def _unpack(m, shapes):
    outs, row = [], 0
    for s in shapes:
        n = int(np.prod(s))
        rows = -(-n // PACK_UNIT) * 8
        outs.append(m[row:row + rows].reshape(-1)[:n].reshape(s))
        row += rows
    return outs


WEIGHTS = ["ffn1_norm", "ffn1_w_gate", "ffn1_w_up", "ffn1_w_down", "mix_norm", "w_in", "conv_w", "a_log", "dt_bias",
           "gdn_norm_w", "q_norm_w", "k_norm_w", "rel_bias", "w_out", "ffn2_norm", "ffn2_w_gate", "ffn2_w_up",
           "ffn2_w_down", "final_norm"]
BIG = ["ffn1_w_gate", "ffn1_w_up", "ffn1_w_down", "w_in", "w_out", "ffn2_w_gate", "ffn2_w_up", "ffn2_w_down"]
SMALL = [n for n in WEIGHTS if n not in BIG]
COL_SHARDED = ["ffn1_w_gate", "ffn1_w_up", "ffn2_w_gate", "ffn2_w_up"]
N_IN_COLS = 3600
TM = 256
TE = 512
TK = 2048


def kernel(x, ffn1_norm, ffn1_w_gate, ffn1_w_up, ffn1_w_down, mix_norm, w_in, conv_w, a_log, dt_bias, gdn_norm_w, q_norm_w, k_norm_w, rel_bias, w_out, ffn2_norm, ffn2_w_gate, ffn2_w_up, ffn2_w_down, final_norm, loss_target, m_ffn1_norm, m_ffn1_w_gate, m_ffn1_w_up, m_ffn1_w_down, m_mix_norm, m_w_in, m_conv_w, m_a_log, m_dt_bias, m_gdn_norm_w, m_q_norm_w, m_k_norm_w, m_rel_bias, m_w_out, m_ffn2_norm, m_ffn2_w_gate, m_ffn2_w_up, m_ffn2_w_down, m_final_norm, v_ffn1_norm, v_ffn1_w_gate, v_ffn1_w_up, v_ffn1_w_down, v_mix_norm, v_w_in, v_conv_w, v_a_log, v_dt_bias, v_gdn_norm_w, v_q_norm_w, v_k_norm_w, v_rel_bias, v_w_out, v_ffn2_norm, v_ffn2_w_gate, v_ffn2_w_up, v_ffn2_w_down, v_final_norm):
    p = dict(locals())
    xs, target = x[0], loss_target[0]
    t, d = xs.shape
    nc = t // CHUNK
    tk = min(TK, t)
    me = 2 * lax.axis_index("x") + lax.axis_index("y")

    first = ["ffn1_w_gate", "ffn1_w_up", "ffn1_w_down"]
    later = [n for n in BIG if n not in first] + ["conv_w"]
    local = lambda n, a: a[0].T if n in COL_SHARDED else a[0]
    shards = {n: local(n, p[n]).astype(BF16) for n in BIG}
    shards["conv_w"] = conv_w[0]
    gw = dict(zip(first, _all_gather([shards[n] for n in first])))
    f1 = (gw["ffn1_w_gate"], gw["ffn1_w_up"], gw["ffn1_w_down"])
    (x1, xn1, g1, u1), landed = _ffn_fwd(xs, ffn1_norm, *f1, TM, "ffn1_fwd",
                                         exchange=_gather_exchange([shards[n] for n in later]))
    gw.update(zip(later, _gather_forward(landed, [shards[n] for n in later])))
    w_in_t = jnp.transpose(gw["w_in"], (0, 2, 1)).reshape(N_IN_COLS, d)
    wp = jnp.concatenate([w_in_t[:2048], jnp.pad(w_in_t[2048:2064], ((0, LANES - 16), (0, 0))), w_in_t[2064:]], axis=0)
    w_out_full = gw["w_out"].reshape(d, d)
    conv_rows = conv_w.shape[1]
    cw = jnp.pad(gw["conv_w"].reshape(N_SHARDS * conv_rows, CONV_TAPS).T, ((0, 8 - CONV_TAPS), (0, 0)))
    gp = jnp.pad(jnp.stack([a_log.reshape(8), dt_bias.reshape(8)]), ((0, 6), (0, LANES - 8)))
    gdn_w = gdn_norm_w.reshape(1, GDN_DIM)
    qw_t = jnp.tile(q_norm_w.reshape(1, SWA_DIM), (1, SWA_HEADS))
    kw_t = jnp.tile(k_norm_w.reshape(1, SWA_DIM), (1, SWA_HEADS))
    bd = jnp.asarray(np.kron(np.eye(2), np.full((SWA_DIM, SWA_DIM), 1.0 / SWA_DIM)), F32)
    f2 = (gw["ffn2_w_gate"], gw["ffn2_w_up"], gw["ffn2_w_down"])

    hn, qkva, z, ab, qkvb = _mix_in_fwd(x1, mix_norm, wp, TM)
    qkvc, gb = _gdn_prep_fwd(qkva, cw, ab, gp, TM)
    gbt = jnp.transpose(gb[:, :16].reshape(nc, CHUNK, 16), (0, 2, 1))
    o_f, o_b, gdn_saved = _gdn_fwd(qkvc, gb, gbt)
    oa = _gdn_post_fwd(o_f, o_b, z, gdn_w, TE)
    o_swa, o_swa16, swa_saved = _swa_branch_fwd(qkvb, qw_t, kw_t, rel_bias, bd, TE)
    x2 = _mix_out_fwd(x1, oa, o_swa, w_out_full, TM)
    (dx3, xn2, g2, u2, loss_part, d_final), _ = _ffn_fwd(x2, ffn2_norm, *f2, TM, "ffn2_fwd", head=(final_norm, target))

    def pair_sums(partials, tag):
        pair = _rs_pair(partials)
        k = len(partials)
        return [_add_pair(pair[i], pair[k + i], f"rs_add_{tag}{i}") for i in range(k)]

    (dx2, dyh2, dg2, du2, h2, d_nw2), _ = _ffn_bwd_dx(dx3, x2, ffn2_norm, g2, u2, *f2, TM, "ffn2_bwd_dx")
    dwg2 = _matmul_tn(dg2, xn2, tk, "ffn2_dwg")
    dwu2 = _matmul_tn(du2, xn2, tk, "ffn2_dwu")
    dwd2 = _matmul_tn(h2, dyh2, tk, "ffn2_dwd")
    sums_f2 = pair_sums([dwg2, dwu2, dwd2], "a")
    doa, dob, dx2b = _mix_out_bwd(dx2, w_out_full, TM)
    dwo = jnp.concatenate([_matmul_tn(oa, dx2b, tk, "w_out_dw_a")[0], _matmul_tn(o_swa16, dx2b, tk, "w_out_dw_b")[0]],
                          axis=0).reshape(N_SHARDS, d // N_SHARDS, d)
    do_g, dz, d_gdnw = _gdn_post_bwd(doa, o_f, o_b, z, gdn_w, TE)
    (dqkvc, dgates), slots_f2 = _gdn_bwd(qkvc, gb, gbt, do_g, gdn_saved, exchange=_scatter_exchange(sums_f2))
    dqkva, dab, dcw, dgp = _gdn_prep_bwd(qkva, cw, ab, gp, dqkvc, dgates, TM)
    dqkvb, d_qw, d_kw, d_rel = _swa_branch_bwd(dob, o_swa, swa_saved, qkvb, qw_t, kw_t, bd, TE)
    dpieces = (dqkva, dz, dab, dqkvb)
    dwp = [_matmul_tn(dp, hn, tk, f"w_in_dw_{i}")[0] for i, dp in enumerate(dpieces)]
    dw_in = jnp.concatenate([dwp[0], dwp[1], dwp[2][:16], dwp[3]], axis=0).reshape(N_SHARDS, N_IN_COLS // N_SHARDS, d)
    dw_in = jnp.transpose(dw_in, (0, 2, 1))
    sums_mix = pair_sums([dw_in, dwo], "b")
    (dx1, d_mixnw), slots_mix = _mix_in_bwd_dx(dx2, x1, mix_norm, dpieces, wp, TM, exchange=_scatter_exchange(sums_mix))
    (gx, dyh1, dg1, du1, h1, d_nw1), _ = _ffn_bwd_dx(dx1, xs, ffn1_norm, g1, u1, *f1, TM, "ffn1_bwd_dx")
    dwg1 = _matmul_tn(dg1, xn1, tk, "ffn1_dwg")
    dwu1 = _matmul_tn(du1, xn1, tk, "ffn1_dwu")
    dwd1 = _matmul_tn(h1, dyh1, tk, "ffn1_dwd")
    slots_f1 = _rs_chips(pair_sums([dwg1, dwu1, dwd1], "c"))
    slots = slots_f1 + _own_slots(slots_mix, sums_mix) + _own_slots(slots_f2, sums_f2)
    halves = [_sum_slots(s, f"rs_sum_{i}") for i, s in enumerate(slots)]
    g_big = dict(zip(BIG, _rs_join(halves)))

    small_partial = {"ffn1_norm": d_nw1, "mix_norm": d_mixnw, "a_log": dgp[0, 0:8], "dt_bias": dgp[1, 0:8],
                     "gdn_norm_w": d_gdnw, "q_norm_w": d_qw, "k_norm_w": d_kw, "rel_bias": d_rel,
                     "ffn2_norm": d_nw2, "final_norm": d_final, "conv_w": dcw[0:CONV_TAPS].T}
    red = _all_reduce_small(_pack([small_partial[n] for n in SMALL] + [loss_part[0, 0:1]]))
    full_shapes = [p[n].shape if n != "conv_w" else (N_SHARDS * conv_rows, CONV_TAPS) for n in SMALL]
    red_parts = _unpack(red, full_shapes + [(1,)])
    loss = red_parts[-1].reshape(())
    g_small = dict(zip(SMALL, red_parts[:-1]))
    g_small["conv_w"] = lax.dynamic_slice_in_dim(g_small["conv_w"], me * conv_rows, conv_rows, 0).reshape(conv_w.shape)

    grads, deltas, new_m, new_v = {}, {}, {}, {}
    for n in BIG:
        back = (lambda a: a.T[None]) if n in COL_SHARDED else (lambda a: a[None])
        grads[n] = back(g_big[n])
        dl, nm, nv = _adamw(local(n, p[n]), g_big[n], local(n, p["m_" + n]), local(n, p["v_" + n]), "adamw_" + n)
        deltas[n], new_m[n], new_v[n] = back(dl), back(nm), back(nv)
    packed = [_pack([src[n] for n in SMALL]) for src in
              ({n: p[n] for n in SMALL}, g_small, {n: p["m_" + n] for n in SMALL}, {n: p["v_" + n] for n in SMALL})]
    small_shapes = [p[n].shape for n in SMALL]
    for dst, arr in zip((deltas, new_m, new_v), _adamw(*packed, "adamw_small")):
        dst.update(zip(SMALL, _unpack(arr, small_shapes)))
    grads.update(g_small)

    return (loss, gx[None], *[grads[n] for n in WEIGHTS], *[deltas[n] for n in WEIGHTS],
            *[new_m[n] for n in WEIGHTS], *[new_v[n] for n in WEIGHTS])
```

```python
import math
from typing import Callable, NamedTuple

import numpy as np
import jax
import jax.numpy as jnp
from jax import lax
from jax.experimental import pallas as pl
from jax.experimental.pallas import tpu as pltpu

F32 = jnp.float32
BF16 = jnp.bfloat16
HIGHEST = lax.Precision.HIGHEST
MESH = pl.DeviceIdType.MESH

EPS = 1e-6
NEG_BIG = -1e30
GDN_HEADS = 4
GDN_DIM = 128
CHUNK = 64
SWA_HEADS = 8
SWA_DIM = 64
PATTERNS = ((128, 1), (512, 4), (2048, 16))
RADIUS = 64
REL_BUCKETS = 32
REL_MAX_DISTANCE = 1024
CONV_TAPS = 5
N_SHARDS = 4
LANES = 128
VMEM_LIMIT = 56 * 1024 * 1024

ADAM_LR, ADAM_B1, ADAM_B2, ADAM_EPS, ADAM_WD, ADAM_STEP = 0.001, 0.9, 0.999, 1e-08, 0.01, 10


def _params(sem=None, vmem=None):
    return pltpu.CompilerParams(dimension_semantics=sem, vmem_limit_bytes=vmem)


def _resident(shape):
    nd = len(shape)
    return pl.BlockSpec(shape, lambda *_: (0,) * nd, pipeline_mode=pl.Buffered(1))


ANY = pl.BlockSpec(memory_space=pl.ANY)


class _Exchange(NamedTuple):
    arrays: tuple
    out_shape: tuple
    start: Callable
    finish: Callable


def _grid_call(body, name, nsteps, in_specs, out_specs, out_shape, operands, scratch=(), exchange=None):
    params = _params(("arbitrary",), VMEM_LIMIT)
    if exchange is None:
        res = pl.pallas_call(body, name=name, grid=(nsteps,), in_specs=list(in_specs), out_specs=list(out_specs),
                             out_shape=list(out_shape), scratch_shapes=list(scratch), compiler_params=params)(*operands)
        return list(res), []
    n_in, n_out, k, n_scr = len(in_specs), len(out_specs), len(exchange.arrays), len(scratch)

    def wrapped(*refs):
        ins, cin = refs[:n_in], refs[n_in:n_in + k]
        outs, cout = refs[n_in + k:n_in + k + n_out], refs[n_in + k + n_out:n_in + 2 * k + n_out]
        rest = refs[n_in + 2 * k + n_out:]
        scr, (send_sems, recv_sems) = rest[:n_scr], rest[n_scr:]

        @pl.when(pl.program_id(0) == 0)
        def _():
            exchange.start(cin, cout, send_sems, recv_sems)

        body(*ins, *outs, *scr)

        @pl.when(pl.program_id(0) == nsteps - 1)
        def _():
            exchange.finish(cin, cout, send_sems, recv_sems)

    res = pl.pallas_call(
        wrapped, name=name, grid=(nsteps,), in_specs=list(in_specs) + [ANY] * k, out_specs=list(out_specs) + [ANY] * k,
        out_shape=list(out_shape) + list(exchange.out_shape),
        scratch_shapes=list(scratch) + [pltpu.SemaphoreType.DMA((k, 3)), pltpu.SemaphoreType.DMA((k, 3))],
        compiler_params=params)(*operands, *exchange.arrays)
    return list(res[:n_out]), list(res[n_out:])


def _dot(a, b):
    return jnp.dot(a.astype(BF16), b.astype(BF16), preferred_element_type=F32)


def _dot_nt(a, b):
    return lax.dot_general(a.astype(BF16), b.astype(BF16), (((1,), (1,)), ((), ())), preferred_element_type=F32)


def _dot_tn(a, b):
    return lax.dot_general(a.astype(BF16), b.astype(BF16), (((0,), (0,)), ((), ())), preferred_element_type=F32)


def _dot_hi(a, b):
    return jnp.dot(a, b, preferred_element_type=F32, precision=HIGHEST)


def _sigmoid(x):
    return 1.0 / (1.0 + jnp.exp(-x))


def _rstd(xf):
    return lax.rsqrt(jnp.mean(xf * xf, axis=-1, keepdims=True) + EPS)


def _rms_bwd(xf, r, nw, dxn):
    xhat = xf * r
    dxh = dxn * nw
    dx = r * (dxh - xhat * jnp.mean(dxh * xhat, axis=-1, keepdims=True))
    return dx, jnp.sum(dxn * xhat, axis=0, keepdims=True)


def _ffn_fwd(x, nw, wg, wu, wd, tm, name, exchange=None, head=None):
    t, d = x.shape
    nj, fs, _ = wg.shape

    def body(x_ref, nw_ref, wg_ref, wu_ref, wd_ref, *rest):
        if head is None:
            y_ref, xn_ref, g_ref, u_ref = rest
        else:
            fw_ref, t_ref, y_ref, xn_ref, g_ref, u_ref, loss_ref, dfw_ref = rest

            @pl.when(pl.program_id(0) == 0)
            def _():
                loss_ref[...] = jnp.zeros_like(loss_ref)
                dfw_ref[...] = jnp.zeros_like(dfw_ref)

        xf = x_ref[...]
        xn = (xf * _rstd(xf) * nw_ref[...]).astype(BF16)
        xn_ref[...] = xn
        acc = jnp.zeros((tm, d), F32)
        for j in range(nj):
            g = _dot_nt(xn, wg_ref[j])
            u = _dot_nt(xn, wu_ref[j])
            h = (g * _sigmoid(g) * u).astype(BF16)
            acc = acc + jnp.dot(h, wd_ref[j], preferred_element_type=F32)
            g_ref[j] = g.astype(BF16)
            u_ref[j] = u.astype(BF16)
        y = xf + 0.5 * acc
        if head is None:
            y_ref[...] = y
        else:
            r = _rstd(y)
            err = y * r * fw_ref[...] - t_ref[...]
            loss_ref[...] += 0.5 * jnp.sum(jnp.mean(err * err, axis=-1, keepdims=True), axis=0, keepdims=True)
            dy, dfw = _rms_bwd(y, r, fw_ref[...], err * (1.0 / d))
            y_ref[...] = dy
            dfw_ref[...] += dfw

    row = pl.BlockSpec((tm, d), lambda i: (i, 0))
    act = pl.BlockSpec((nj, tm, fs), lambda i: (0, i, 0))
    in_specs = [row, _resident((1, d)), _resident(wg.shape), _resident(wu.shape), _resident(wd.shape)]
    out_specs = [row, row, act, act]
    out_shape = [jax.ShapeDtypeStruct((t, d), F32), jax.ShapeDtypeStruct((t, d), BF16),
                 jax.ShapeDtypeStruct((nj, t, fs), BF16), jax.ShapeDtypeStruct((nj, t, fs), BF16)]
    operands = (x, nw, wg, wu, wd)
    if head is not None:
        in_specs += [_resident((1, d)), row]
        out_specs += [pl.BlockSpec((1, LANES), lambda i: (0, 0)), pl.BlockSpec((1, d), lambda i: (0, 0))]
        out_shape += [jax.ShapeDtypeStruct((1, LANES), F32), jax.ShapeDtypeStruct((1, d), F32)]
        operands += tuple(head)
    return _grid_call(body, name, t // tm, in_specs, out_specs, out_shape, operands, exchange=exchange)


def _ffn_bwd_dx(dy, x, nw, g, u, wg, wu, wd, tm, name, exchange=None):
    t, d = x.shape
    nj, fs, _ = wg.shape

    def body(dy_ref, x_ref, nw_ref, g_ref, u_ref, wg_ref, wu_ref, wd_ref,
             dx_ref, dyh_ref, dg_ref, du_ref, h_ref, dnw_ref):
        @pl.when(pl.program_id(0) == 0)
        def _():
            dnw_ref[...] = jnp.zeros_like(dnw_ref)

        dyv = dy_ref[...]
        dyh = (0.5 * dyv).astype(BF16)
        dyh_ref[...] = dyh
        dxn = jnp.zeros((tm, d), F32)
        dh_next = _dot_nt(dyh, wd_ref[0])
        for j in range(nj):
            dh = dh_next
            gv = g_ref[j].astype(F32)
            uv = u_ref[j].astype(F32)
            sg = _sigmoid(gv)
            si = gv * sg
            dg = (dh * uv * (sg * (1.0 + gv * (1.0 - sg)))).astype(BF16)
            du = (dh * si).astype(BF16)
            if j + 1 < nj:
                dh_next = _dot_nt(dyh, wd_ref[j + 1])
            h_ref[j] = (si * uv).astype(BF16)
            dg_ref[j] = dg
            du_ref[j] = du
            dxn = dxn + _dot(dg, wg_ref[j]) + _dot(du, wu_ref[j])
        xf = x_ref[...]
        dxr, dnw = _rms_bwd(xf, _rstd(xf), nw_ref[...], dxn)
        dx_ref[...] = dyv + dxr
        dnw_ref[...] += dnw

    row = pl.BlockSpec((tm, d), lambda i: (i, 0))
    act = pl.BlockSpec((nj, tm, fs), lambda i: (0, i, 0))
    act_shape = jax.ShapeDtypeStruct((nj, t, fs), BF16)
    return _grid_call(
        body, name, t // tm,
        [row, row, _resident((1, d)), act, act, _resident(wg.shape), _resident(wu.shape), _resident(wd.shape)],
        [row, row, act, act, act, pl.BlockSpec((1, d), lambda i: (0, 0))],
        [jax.ShapeDtypeStruct((t, d), F32), jax.ShapeDtypeStruct((t, d), BF16),
         act_shape, act_shape, act_shape, jax.ShapeDtypeStruct((1, d), F32)],
        (dy, x, nw, g, u, wg, wu, wd), exchange=exchange)


def _matmul_tn(a, b, tk, name):
    a3, b3 = a.ndim == 3, b.ndim == 3
    nj = a.shape[0] if a3 else (b.shape[0] if b3 else 1)
    t, m = a.shape[-2:]
    n = b.shape[-1]
    nt = t // tk

    def body(a_ref, b_ref, o_ref, acc_ref):
        k = pl.program_id(1)

        @pl.when(k == 0)
        def _():
            acc_ref[...] = jnp.zeros_like(acc_ref)

        acc_ref[...] += lax.dot_general(a_ref[...], b_ref[...], (((0,), (0,)), ((), ())),
                                        preferred_element_type=F32)

        @pl.when(k == nt - 1)
        def _():
            o_ref[...] = acc_ref[...].astype(o_ref.dtype)

    a_spec = (pl.BlockSpec((None, tk, m), lambda j, k: (j, k, 0)) if a3
              else pl.BlockSpec((tk, m), lambda j, k: (k, 0)))
    b_spec = (pl.BlockSpec((None, tk, n), lambda j, k: (j, k, 0)) if b3
              else pl.BlockSpec((tk, n), lambda j, k: (k, 0)))
    return pl.pallas_call(
        body, name=name, grid=(nj, nt),
        in_specs=[a_spec, b_spec],
        out_specs=pl.BlockSpec((None, m, n), lambda j, k: (j, 0, 0)),
        out_shape=jax.ShapeDtypeStruct((nj, m, n), BF16),
        scratch_shapes=[pltpu.VMEM((m, n), F32)],
        compiler_params=_params(("arbitrary", "arbitrary"), VMEM_LIMIT),
    )(a, b)


P_QKVA, P_Z, P_AB, P_QKVB = (0, 1536), (1536, 2048), (2048, 2176), (2176, 3712)
P_PIECES = (P_QKVA, P_Z, P_AB, P_QKVB)
P_COLS = 3712


def _mix_in_fwd(x1, nw, wp, tm):
    t, d = x1.shape

    def body(x_ref, nw_ref, w_ref, hn_ref, *outs):
        xf = x_ref[...]
        xn = (xf * _rstd(xf) * nw_ref[...]).astype(BF16)
        hn_ref[...] = xn
        for (a, b), o_ref in zip(P_PIECES, outs):
            o_ref[...] = _dot_nt(xn, w_ref[a:b, :])

    row = pl.BlockSpec((tm, d), lambda i: (i, 0))
    return pl.pallas_call(
        body, name="mix_in_fwd", grid=(t // tm,),
        in_specs=[row, _resident((1, d)), _resident(wp.shape)],
        out_specs=[row] + [pl.BlockSpec((tm, b - a), lambda i: (i, 0)) for a, b in P_PIECES],
        out_shape=[jax.ShapeDtypeStruct((t, d), BF16)]
                  + [jax.ShapeDtypeStruct((t, b - a), F32) for a, b in P_PIECES],
        compiler_params=_params(("arbitrary",), VMEM_LIMIT),
    )(x1, nw, wp)


def _mix_in_bwd_dx(dx, x1, nw, dpieces, wp, tm, exchange=None):
    t, d = x1.shape

    def body(dx_ref, x_ref, nw_ref, p0, p1, p2, p3, w_ref, o_ref, dnw_ref):
        @pl.when(pl.program_id(0) == 0)
        def _():
            dnw_ref[...] = jnp.zeros_like(dnw_ref)

        dh = jnp.zeros((tm, d), F32)
        for (a, b), p_ref in zip(P_PIECES, (p0, p1, p2, p3)):
            dh = dh + _dot(p_ref[...], w_ref[a:b, :])
        xf = x_ref[...]
        dxr, dnw = _rms_bwd(xf, _rstd(xf), nw_ref[...], dh)
        o_ref[...] = dx_ref[...] + dxr
        dnw_ref[...] += dnw

    row = pl.BlockSpec((tm, d), lambda i: (i, 0))
    return _grid_call(
        body, "mix_in_bwd_dx", t // tm,
        [row, row, _resident((1, d))]
        + [pl.BlockSpec((tm, b - a), lambda i: (i, 0)) for a, b in P_PIECES] + [_resident(wp.shape)],
        [row, pl.BlockSpec((1, d), lambda i: (0, 0))],
        [jax.ShapeDtypeStruct((t, d), F32), jax.ShapeDtypeStruct((1, d), F32)],
        (dx, x1, nw, *dpieces, wp), exchange=exchange)


def _mix_out_fwd(x1, oa, ob, w, tm):
    t, d = x1.shape
    half = oa.shape[1]

    def body(x_ref, oa_ref, ob_ref, w_ref, o_ref):
        o_ref[...] = (x_ref[...] + _dot(oa_ref[...], w_ref[0:half, :]) + _dot(ob_ref[...], w_ref[half:2 * half, :]))

    row = pl.BlockSpec((tm, d), lambda i: (i, 0))
    hrow = pl.BlockSpec((tm, half), lambda i: (i, 0))
    return pl.pallas_call(
        body, name="mix_out_fwd", grid=(t // tm,),
        in_specs=[row, hrow, hrow, _resident(w.shape)],
        out_specs=row, out_shape=jax.ShapeDtypeStruct((t, d), F32),
        compiler_params=_params(("arbitrary",), VMEM_LIMIT),
    )(x1, oa, ob, w)


def _mix_out_bwd(dx2, w, tm):
    t, d = dx2.shape
    half = w.shape[0] // 2

    def body(dx_ref, w_ref, doa_ref, dob_ref, dxb_ref):
        dxb = dx_ref[...].astype(BF16)
        dxb_ref[...] = dxb
        doa_ref[...] = _dot_nt(dxb, w_ref[0:half, :])
        dob_ref[...] = _dot_nt(dxb, w_ref[half:2 * half, :])

    row = pl.BlockSpec((tm, d), lambda i: (i, 0))
    hrow = pl.BlockSpec((tm, half), lambda i: (i, 0))
    return pl.pallas_call(
        body, name="mix_out_bwd", grid=(t // tm,),
        in_specs=[row, _resident(w.shape)],
        out_specs=[hrow, hrow, row],
        out_shape=[jax.ShapeDtypeStruct((t, half), F32), jax.ShapeDtypeStruct((t, half), F32),
                   jax.ShapeDtypeStruct((t, d), BF16)],
        compiler_params=_params(("arbitrary",), VMEM_LIMIT),
    )(dx2, w)


HALO = 8


def _halo_row_specs(tr, cols, nrow8):
    per = tr // HALO
    return [pl.BlockSpec((tr, cols), lambda i: (i, 0)),
            pl.BlockSpec((HALO, cols), lambda i: (jnp.maximum(i * per - 1, 0), 0)),
            pl.BlockSpec((HALO, cols), lambda i: (jnp.minimum((i + 1) * per, nrow8 - 1), 0))]


def _conv_window(xm, xp, xn, first, last, cols):
    prev = jnp.where(first, 0.0, xp[:, cols])
    nxt = jnp.where(last, 0.0, xn[:, cols])
    return jnp.concatenate([prev, xm[:, cols], nxt], axis=0)


def _shift_rows(xw, off):
    n = xw.shape[0]
    sh = (-off) % n
    return xw if sh == 0 else pltpu.roll(xw, sh, 0)


def _conv_pre(xw, cw_ref, cols):
    acc = None
    for j in range(CONV_TAPS):
        term = _shift_rows(xw, j - CONV_TAPS // 2) * cw_ref[j:j + 1, cols]
        acc = term if acc is None else acc + term
    return acc


def _softplus(x):
    u = jnp.exp(-jnp.abs(x))
    w = 1.0 + u
    log1p = jnp.where(w == 1.0, u, jnp.log(w) * u / jnp.where(w == 1.0, 1.0, w - 1.0))
    return jnp.maximum(x, 0.0) + log1p


def _gdn_prep_fwd(qkva, cw, ab, gp, tr):
    t, c = qkva.shape
    nt = t // tr
    ncb = c // LANES

    def body(xm, xp, xn, cw_ref, ab_ref, gp_ref, o_ref, gb_ref):
        i = pl.program_id(0)
        first, last = i == 0, i == nt - 1
        for cb in range(ncb):
            cols = slice(cb * LANES, (cb + 1) * LANES)
            xw = _conv_window(xm, xp, xn, first, last, cols)
            pre = _conv_pre(xw, cw_ref, cols)[HALO:HALO + tr]
            y = pre * _sigmoid(pre)
            if cb < 2 * GDN_HEADS:
                y = y * lax.rsqrt(jnp.sum(y * y, axis=-1, keepdims=True) + EPS)
            if cb < GDN_HEADS:
                y = y * (GDN_DIM ** -0.5)
            o_ref[:, cols] = y
        abv = ab_ref[...]
        lane = lax.broadcasted_iota(jnp.int32, abv.shape, 1)
        g = -jnp.exp(gp_ref[0:1, :]) * _softplus(abv + gp_ref[1:2, :])
        gb_ref[...] = jnp.where(lane < 8, g, jnp.where(lane < 16, _sigmoid(abv), 0.0))

    return pl.pallas_call(
        body, name="gdn_prep_fwd", grid=(nt,),
        in_specs=_halo_row_specs(tr, c, t // HALO)
                 + [_resident(cw.shape), pl.BlockSpec((tr, LANES), lambda i: (i, 0)), _resident(gp.shape)],
        out_specs=[pl.BlockSpec((tr, c), lambda i: (i, 0)), pl.BlockSpec((tr, LANES), lambda i: (i, 0))],
        out_shape=[jax.ShapeDtypeStruct((t, c), F32), jax.ShapeDtypeStruct((t, LANES), F32)],
        compiler_params=_params(("arbitrary",), VMEM_LIMIT),
    )(qkva, qkva, qkva, cw, ab, gp)


def _gdn_prep_bwd(qkva, cw, ab, gp, dy, dgates, tr):
    t, c = qkva.shape
    nt = t // tr
    ncb = c // LANES

    def body(xm, xp, xn, fm, fp, fn, cw_ref, ab_ref, gp_ref, gf_ref, dx_ref, dab_ref, dcw_ref, dgp_ref):
        i = pl.program_id(0)
        first, last = i == 0, i == nt - 1

        @pl.when(first)
        def _():
            dcw_ref[...] = jnp.zeros_like(dcw_ref)
            dgp_ref[...] = jnp.zeros_like(dgp_ref)

        sub8 = lax.broadcasted_iota(jnp.int32, (8, LANES), 0)
        for cb in range(ncb):
            cols = slice(cb * LANES, (cb + 1) * LANES)
            xw = _conv_window(xm, xp, xn, first, last, cols)
            dyw = _conv_window(fm, fp, fn, first, last, cols)
            pre = _conv_pre(xw, cw_ref, cols)
            sg = _sigmoid(pre)
            s = pre * sg
            if cb < 2 * GDN_HEADS:
                scale = (GDN_DIM ** -0.5) if cb < GDN_HEADS else 1.0
                r = lax.rsqrt(jnp.sum(s * s, axis=-1, keepdims=True) + EPS)
                dn = dyw * scale
                ds = r * dn - s * (r * r * r) * jnp.sum(dn * s, axis=-1, keepdims=True)
            else:
                ds = dyw
            dpre = ds * (sg * (1.0 + pre * (1.0 - sg)))
            dx = None
            dcw = jnp.zeros((8, LANES), F32)
            for j in range(CONV_TAPS):
                off = j - CONV_TAPS // 2
                term = _shift_rows(dpre, -off)[HALO:HALO + tr] * cw_ref[j:j + 1, cols]
                dx = term if dx is None else dx + term
                tap = jnp.sum(dpre[HALO:HALO + tr] * _shift_rows(xw, off)[HALO:HALO + tr], axis=0, keepdims=True)
                dcw = dcw + jnp.where(sub8 == j, tap, 0.0)
            dx_ref[:, cols] = dx.astype(BF16)
            dcw_ref[:, cols] += dcw

        abv = ab_ref[...]
        dgb = gf_ref[...]
        lane = lax.broadcasted_iota(jnp.int32, abv.shape, 1)
        nea = -jnp.exp(gp_ref[0:1, :])
        xs = abv + gp_ref[1:2, :]
        g = nea * _softplus(xs)
        beta = _sigmoid(abv)
        da = dgb * nea * _sigmoid(xs)
        dab = jnp.where(lane < 8, da, jnp.where(lane < 16, dgb * beta * (1.0 - beta), 0.0))
        dab_ref[...] = dab.astype(BF16)
        keep = lane[0:1, :] < 8
        dalog = jnp.where(keep, jnp.sum(dgb * g, axis=0, keepdims=True), 0.0)
        ddtb = jnp.where(keep, jnp.sum(da, axis=0, keepdims=True), 0.0)
        dgp_ref[...] += jnp.where(sub8 == 0, dalog, 0.0) + jnp.where(sub8 == 1, ddtb, 0.0)

    lrow = pl.BlockSpec((tr, LANES), lambda i: (i, 0))
    halo = _halo_row_specs(tr, c, t // HALO)
    return pl.pallas_call(
        body, name="gdn_prep_bwd", grid=(nt,),
        in_specs=halo + halo + [_resident(cw.shape), lrow, _resident(gp.shape), lrow],
        out_specs=[pl.BlockSpec((tr, c), lambda i: (i, 0)), lrow,
                   pl.BlockSpec(cw.shape, lambda i: (0, 0)), pl.BlockSpec(gp.shape, lambda i: (0, 0))],
        out_shape=[jax.ShapeDtypeStruct((t, c), BF16), jax.ShapeDtypeStruct((t, LANES), BF16),
                   jax.ShapeDtypeStruct(cw.shape, F32), jax.ShapeDtypeStruct(gp.shape, F32)],
        compiler_params=_params(("arbitrary",), VMEM_LIMIT),
    )(qkva, qkva, qkva, dy, dy, dy, cw, ab, gp, dgates)


def _chunk_masks(lower):
    ii = lax.broadcasted_iota(jnp.int32, (CHUNK, CHUNK), 0)
    jj = lax.broadcasted_iota(jnp.int32, (CHUNK, CHUNK), 1)
    incl = (ii >= jj) if lower else (ii <= jj)
    strict = (ii > jj) if lower else (ii < jj)
    return ii, jj, incl, strict


def _dot3(a, b):
    ah = a.astype(BF16)
    al = (a - ah.astype(F32)).astype(BF16)
    bh = b.astype(BF16)
    bl = (b - bh.astype(F32)).astype(BF16)
    d = lambda u, v: jnp.dot(u, v, preferred_element_type=F32)
    return d(ah, bh) + (d(ah, bl) + d(al, bh))


def _tri_inv_many(lmats, ii, jj):
    m16 = (ii // 16) == (jj // 16)
    m32 = (ii // 32) == (jj // 32)
    eye = jnp.where(ii == jj, 1.0, 0.0)
    l16 = [jnp.where(m16, l, 0.0) for l in lmats]
    p2 = [_dot3(a, a) for a in l16]
    p4 = [_dot3(a, a) for a in p2]
    p8 = [_dot3(a, a) for a in p4]
    xs = [eye - a for a in l16]
    for ps in (p2, p4, p8):
        xs = [x + _dot3(x, p) for x, p in zip(xs, ps)]
    for off in ([jnp.where(m32 & jnp.logical_not(m16), l, 0.0) for l in lmats],
                [jnp.where(m32, 0.0, l) for l in lmats]):
        ys = [_dot3(x, c) for x, c in zip(xs, off)]
        xs = [x - _dot3(y, x) for x, y in zip(xs, ys)]
    return xs


def _col_to_row(col, ii, jj):
    return jnp.sum(jnp.where(ii == jj, col, 0.0), axis=0, keepdims=True)


def _row_to_col(row, ii, jj):
    return jnp.sum(jnp.where(ii == jj, row, 0.0), axis=1, keepdims=True)


def _chain_common(q, k, v, graw_col, graw_row, bcol, masks):
    ii, jj, incl, strict = masks
    inclt = jnp.logical_not(strict)
    gcol = jnp.sum(jnp.where(incl, graw_row, 0.0), axis=1, keepdims=True)
    grow = jnp.sum(jnp.where(inclt, graw_col, 0.0), axis=0, keepdims=True)
    glast = jnp.sum(graw_row, axis=1, keepdims=True)
    decay = jnp.where(incl, jnp.exp(jnp.where(incl, gcol - grow, 0.0)), 0.0)
    kb = k * bcol
    vb = v * bcol
    eg = jnp.exp(gcol)
    ek = jnp.exp(glast - gcol)
    kbg = kb * eg
    amat = _dot_nt(kb, k)
    qk = _dot_nt(q, k)
    return dict(gcol=gcol, glast=glast, decay=decay, kb=kb, vb=vb, eg=eg, ek=ek, kbg=kbg, amat=amat, qk=qk,
                intra=qk * decay, qg=q * eg, kdec=k * ek)


def _gdn_fwd(qkvc, gb, gbt):
    tm, u, w, qg, kd, intra, egl = _gdn_local_fwd(qkvc, gb, gbt)
    o_f, o_b, s_f, s_b, vn_f, vn_b = _gdn_scan_fwd(u, w, qg, kd, intra, egl, qkvc.shape[0])
    return o_f, o_b, dict(tm=tm, w=w, qg=qg, kd=kd, intra=intra, egl=egl, s=(s_f, s_b), vn=(vn_f, vn_b))


N_CHAINS = 2 * GDN_HEADS


LOCAL_CHUNKS = 2


def _load_chains(x_ref, g_ref, gt_ref, cc=0):
    hd = GDN_HEADS * GDN_DIM
    rows = slice(cc * CHUNK, (cc + 1) * CHUNK)
    chains = []
    for d in range(2):
        masks = _chunk_masks(d == 0)
        for h in range(GDN_HEADS):
            ch = d * GDN_HEADS + h
            q = x_ref[rows, h * GDN_DIM:(h + 1) * GDN_DIM]
            k = x_ref[rows, hd + h * GDN_DIM:hd + (h + 1) * GDN_DIM]
            v = x_ref[rows, 2 * hd + h * GDN_DIM:2 * hd + (h + 1) * GDN_DIM]
            bcol = g_ref[rows, 8 + ch:9 + ch]
            cm = _chain_common(q, k, v, g_ref[rows, ch:ch + 1], gt_ref[cc, ch:ch + 1, :], bcol, masks)
            chains.append(dict(cm, q=q, k=k, v=v, bcol=bcol, masks=masks, ch=ch, h=h, cc=cc))
    return chains


def _chain_shape(rows, cols, dtype):
    return lambda nc: jax.ShapeDtypeStruct((nc, N_CHAINS, rows, cols), dtype)


def _gdn_local_fwd(qkvc, gb, gbt):
    t = qkvc.shape[0]
    nc = t // CHUNK
    hd = GDN_HEADS * GDN_DIM

    def body(x_ref, g_ref, gt_ref, t_ref, u_ref, w_ref, qg_ref, kd_ref, in_ref, eg_ref):
        chains = [c for cc in range(LOCAL_CHUNKS) for c in _load_chains(x_ref, g_ref, gt_ref, cc)]
        ii, jj = chains[0]["masks"][0:2]
        tms = _tri_inv_many([jnp.where(c["masks"][3], c["amat"] * c["decay"], 0.0) for c in chains], ii, jj)
        uws = [_dot(tm, jnp.concatenate([c["vb"], c["kbg"]], axis=1)) for tm, c in zip(tms, chains)]
        for c, tm, uw in zip(chains, tms, uws):
            cc, ch = c["cc"], c["ch"]
            t_ref[cc, ch] = tm
            u_ref[cc, ch] = uw[:, :GDN_DIM]
            w_ref[cc, ch] = uw[:, GDN_DIM:].astype(BF16)
            qg_ref[cc, ch] = c["qg"].astype(BF16)
            kd_ref[cc, ch] = c["kdec"].astype(BF16)
            in_ref[cc, ch] = c["intra"].astype(BF16)
            eg_ref[cc, ch:ch + 1, :] = jnp.broadcast_to(jnp.exp(c["glast"]), (1, LANES))

    lc = LOCAL_CHUNKS
    blk = lambda rows, cols: pl.BlockSpec((lc, N_CHAINS, rows, cols), lambda n: (n, 0, 0, 0))
    shapes = [_chain_shape(CHUNK, CHUNK, F32), _chain_shape(CHUNK, GDN_DIM, F32), _chain_shape(CHUNK, GDN_DIM, BF16),
              _chain_shape(CHUNK, GDN_DIM, BF16), _chain_shape(CHUNK, GDN_DIM, BF16), _chain_shape(CHUNK, CHUNK, BF16)]
    return tuple(pl.pallas_call(
        body, name="gdn_local_fwd", grid=(nc // lc,),
        in_specs=[pl.BlockSpec((lc * CHUNK, 3 * hd), lambda n: (n, 0)), pl.BlockSpec((lc * CHUNK, LANES), lambda n: (n, 0)),
                  pl.BlockSpec((lc, 16, CHUNK), lambda n: (n, 0, 0))],
        out_specs=[blk(CHUNK, CHUNK), blk(CHUNK, GDN_DIM), blk(CHUNK, GDN_DIM), blk(CHUNK, GDN_DIM),
                   blk(CHUNK, GDN_DIM), blk(CHUNK, CHUNK), pl.BlockSpec((lc, N_CHAINS, LANES), lambda n: (n, 0, 0))],
        out_shape=[s(nc) for s in shapes] + [jax.ShapeDtypeStruct((nc, N_CHAINS, LANES), F32)],
        compiler_params=_params(("arbitrary",), VMEM_LIMIT),
    )(qkvc, gb, gbt))


SCAN_CHUNKS = 4


def _dir_specs(nc, rev):
    nb = nc // SCAN_CHUNKS

    def spec(d, rows, cols, own=False):
        chunk = (lambda n: n) if (d == 0) != rev else (lambda n: nb - 1 - n)
        blk = 0 if own else d
        if rows is None:
            return pl.BlockSpec((SCAN_CHUNKS, GDN_HEADS if own else N_CHAINS, cols), lambda n: (chunk(n), 0, 0))
        return pl.BlockSpec((SCAN_CHUNKS, GDN_HEADS, rows, cols), lambda n: (chunk(n), blk, 0, 0))

    def rows_spec(d, cols):
        chunk = (lambda n: n) if (d == 0) != rev else (lambda n: nb - 1 - n)
        return pl.BlockSpec((SCAN_CHUNKS * CHUNK, cols), lambda n: (chunk(n), 0))

    def order(d):
        return list(range(SCAN_CHUNKS)) if (d == 0) != rev else list(range(SCAN_CHUNKS - 1, -1, -1))
    return spec, rows_spec, order


def _gdn_scan_fwd(u, w, qg, kd, intra, egl, t):
    nc = t // CHUNK
    hd = GDN_HEADS * GDN_DIM

    def body(*refs):
        ins, outs, state = refs[:12], refs[12:18], refs[18]
        @pl.when(pl.program_id(0) == 0)
        def _():
            state[...] = jnp.zeros_like(state)

        chains = [(d, h) for d in range(2) for h in range(GDN_HEADS)]
        states = [state[ch] for ch in range(N_CHAINS)]
        for step in range(SCAN_CHUNKS):
            at = [order(d)[step] for d in range(2)]
            pick = lambda k, d, h: ins[2 * k + d][at[d], h]
            sbs = [s.astype(BF16) for s in states]
            ws = [_dot(pick(1, d, h), sb) for (d, h), sb in zip(chains, sbs)]
            o1 = [_dot(pick(2, d, h), sb) for (d, h), sb in zip(chains, sbs)]
            vns = [(pick(0, d, h) - wsb).astype(BF16) for (d, h), wsb in zip(chains, ws)]
            o2 = [_dot(pick(4, d, h), vn) for (d, h), vn in zip(chains, vns)]
            kv = [_dot_tn(pick(3, d, h), vn) for (d, h), vn in zip(chains, vns)]
            new_states = []
            for ch, (d, h) in enumerate(chains):
                outs[d][at[d] * CHUNK:(at[d] + 1) * CHUNK, h * GDN_DIM:(h + 1) * GDN_DIM] = o1[ch] + o2[ch]
                outs[2 + d][at[d], h] = states[ch]
                outs[4 + d][at[d], h] = vns[ch]
                new_states.append(states[ch] * ins[10 + d][at[d], ch:ch + 1, :] + kv[ch])
            states = new_states
        for ch in range(N_CHAINS):
            state[ch] = states[ch]

    spec, rows_spec, order = _dir_specs(nc, False)
    pair = lambda rows, cols, own=False: [spec(0, rows, cols, own), spec(1, rows, cols, own)]
    s_shape = jax.ShapeDtypeStruct((nc, GDN_HEADS, GDN_DIM, GDN_DIM), F32)
    vn_shape = jax.ShapeDtypeStruct((nc, GDN_HEADS, CHUNK, GDN_DIM), BF16)
    return pl.pallas_call(
        body, name="gdn_scan_fwd", grid=(nc // SCAN_CHUNKS,),
        in_specs=(pair(CHUNK, GDN_DIM) + pair(CHUNK, GDN_DIM) + pair(CHUNK, GDN_DIM) + pair(CHUNK, GDN_DIM)
                  + pair(CHUNK, CHUNK) + pair(None, LANES)),
        out_specs=([rows_spec(0, hd), rows_spec(1, hd)] + pair(GDN_DIM, GDN_DIM, True)
                   + pair(CHUNK, GDN_DIM, True)),
        out_shape=[jax.ShapeDtypeStruct((t, hd), F32), jax.ShapeDtypeStruct((t, hd), F32),
                   s_shape, s_shape, vn_shape, vn_shape],
        scratch_shapes=[pltpu.VMEM((N_CHAINS, GDN_DIM, GDN_DIM), F32)],
        compiler_params=_params(("arbitrary",), VMEM_LIMIT),
    )(u, u, w, w, qg, qg, kd, kd, intra, intra, egl, egl)


def _gdn_bwd(qkvc, gb, gbt, do, saved, exchange=None):
    scan = _gdn_scan_bwd(do, saved, qkvc.shape[0])
    return _gdn_local_bwd(qkvc, gb, gbt, do, saved, scan, exchange)


def _gdn_scan_bwd(do, saved, t):
    nc = t // CHUNK
    hd = GDN_HEADS * GDN_DIM

    def body(*refs):
        ins, outs, dstate = refs[:16], refs[16:26], refs[26]
        @pl.when(pl.program_id(0) == 0)
        def _():
            dstate[...] = jnp.zeros_like(dstate)

        chains = [(d, h) for d in range(2) for h in range(GDN_HEADS)]
        dss = [dstate[ch] for ch in range(N_CHAINS)]
        for step in range(SCAN_CHUNKS):
            at = [order(d)[step] for d in range(2)]
            pick = lambda k, d, h: ins[2 * k + d][at[d], h]
            dsbs = [ds.astype(BF16) for ds in dss]
            ss = [pick(1, d, h) for d, h in chains]
            sbs = [s.astype(BF16) for s in ss]
            dos = [ins[d][at[d] * CHUNK:(at[d] + 1) * CHUNK, h * GDN_DIM:(h + 1) * GDN_DIM].astype(BF16)
                   for d, h in chains]
            dv1 = [_dot_tn(pick(5, d, h), dov) for (d, h), dov in zip(chains, dos)]
            dv2 = [_dot(pick(4, d, h), dsb) for (d, h), dsb in zip(chains, dsbs)]
            ds1 = [_dot_tn(pick(3, d, h), dov) for (d, h), dov in zip(chains, dos)]
            dkds = [_dot_nt(pick(6, d, h), dsb) for (d, h), dsb in zip(chains, dsbs)]
            dqgs = [_dot_nt(dov, sb) for dov, sb in zip(dos, sbs)]
            dvns = [(a + b).astype(BF16) for a, b in zip(dv1, dv2)]
            ds2 = [_dot_tn(pick(2, d, h), dvn) for (d, h), dvn in zip(chains, dvns)]
            dws = [_dot_nt(dvn, sb) for dvn, sb in zip(dvns, sbs)]
            new_dss = []
            for ch, (d, h) in enumerate(chains):
                egl = ins[14 + d][at[d], ch:ch + 1, :]
                outs[d][at[d], h] = dvns[ch]
                outs[2 + d][at[d], h] = (-dws[ch]).astype(BF16)
                outs[4 + d][at[d], h] = dqgs[ch]
                outs[6 + d][at[d], h] = dkds[ch]
                outs[8 + d][at[d], h:h + 1, :] = egl * jnp.sum(jnp.sum(ss[ch] * dss[ch], axis=1, keepdims=True),
                                                               axis=0, keepdims=True)
                new_dss.append(ds1[ch] + egl * dss[ch] - ds2[ch])
            dss = new_dss
        for ch in range(N_CHAINS):
            dstate[ch] = dss[ch]

    spec, rows_spec, order = _dir_specs(nc, True)
    pair = lambda rows, cols, own=False: [spec(0, rows, cols, own), spec(1, rows, cols, own)]
    s_f, s_b = saved["s"]
    vn_f, vn_b = saved["vn"]
    w, qg, kd, intra, egl = saved["w"], saved["qg"], saved["kd"], saved["intra"], saved["egl"]
    own = lambda rows, cols, dtype: jax.ShapeDtypeStruct((nc, GDN_HEADS, rows, cols), dtype)
    row_shape = jax.ShapeDtypeStruct((nc, GDN_HEADS, LANES), F32)
    return pl.pallas_call(
        body, name="gdn_scan_bwd", grid=(nc // SCAN_CHUNKS,),
        in_specs=([rows_spec(0, hd), rows_spec(1, hd)] + pair(GDN_DIM, GDN_DIM, True) + pair(CHUNK, GDN_DIM)
                  + pair(CHUNK, GDN_DIM) + pair(CHUNK, GDN_DIM) + pair(CHUNK, CHUNK) + pair(CHUNK, GDN_DIM, True)
                  + pair(None, LANES)),
        out_specs=(pair(CHUNK, GDN_DIM, True) + pair(CHUNK, GDN_DIM, True) + pair(CHUNK, GDN_DIM, True)
                   + pair(CHUNK, GDN_DIM, True) + pair(None, LANES, True)),
        out_shape=[own(CHUNK, GDN_DIM, BF16)] * 4 + [own(CHUNK, GDN_DIM, F32)] * 4 + [row_shape] * 2,
        scratch_shapes=[pltpu.VMEM((N_CHAINS, GDN_DIM, GDN_DIM), F32)],
        compiler_params=_params(("arbitrary",), VMEM_LIMIT),
    )(do, do, s_f, s_b, w, w, qg, qg, kd, kd, intra, intra, vn_f, vn_b, egl, egl)


def _dot3_nt(a, b):
    ah = a.astype(BF16)
    al = (a - ah.astype(F32)).astype(BF16)
    bh = b.astype(BF16)
    bl = (b - bh.astype(F32)).astype(BF16)
    return _dot_nt(ah, bh) + (_dot_nt(ah, bl) + _dot_nt(al, bh))


def _dot3_tn(a, b):
    ah = a.astype(BF16)
    al = (a - ah.astype(F32)).astype(BF16)
    bh = b.astype(BF16)
    bl = (b - bh.astype(F32)).astype(BF16)
    return _dot_tn(ah, bh) + (_dot_tn(ah, bl) + _dot_tn(al, bh))


def _gdn_local_bwd(qkvc, gb, gbt, do, saved, scan, exchange=None):
    t = qkvc.shape[0]
    nc = t // CHUNK
    hd = GDN_HEADS * GDN_DIM

    def body(*refs):
        x_ref, g_ref, gt_ref, do_ref, t_ref = refs[:5]
        per_dir = refs[5:17]
        dx_ref, dg_ref = refs[17:]
        chains = [c for cc in range(LOCAL_CHUNKS) for c in _load_chains(x_ref, g_ref, gt_ref, cc)]
        lane = lax.broadcasted_iota(jnp.int32, (CHUNK, LANES), 1)
        dgates = [jnp.zeros((CHUNK, LANES), F32) for _ in range(LOCAL_CHUNKS)]
        for c in chains:
            d = c["ch"] // GDN_HEADS
            vn_ref, dvn_ref, dw_ref, dqg_ref, dkd_ref, dgl_ref = per_dir[d::2]
            h, cc = c["h"], c["cc"]
            rows = slice(cc * CHUNK, (cc + 1) * CHUNK)
            c.update(tm=t_ref[cc, c["ch"]], dov=do_ref[rows, h * GDN_DIM:(h + 1) * GDN_DIM], vnew=vn_ref[cc, h],
                     dvnew=dvn_ref[cc, h], dw=dw_ref[cc, h], dqg=dqg_ref[cc, h], dkdec=dkd_ref[cc, h],
                     dglast=dgl_ref[cc, h:h + 1, 0:1])
        dintras = [_dot_nt(c["dov"], c["vnew"]) for c in chains]
        dts = [_dot_nt(c["dvnew"], c["vb"]) + _dot_nt(c["dw"], c["kbg"]) for c in chains]
        dvbs = [_dot_tn(c["tm"], c["dvnew"]) for c in chains]
        dkbgs = [_dot_tn(c["tm"], c["dw"]) for c in chains]
        tdts = [_dot3_nt(dt, c["tm"]) for dt, c in zip(dts, chains)]
        dls = [jnp.where(c["masks"][3], -_dot3_tn(c["tm"], tdt), 0.0) for tdt, c in zip(tdts, chains)]
        das = [dl * c["decay"] for dl, c in zip(dls, chains)]
        dqks = [jnp.where(c["masks"][2], di, 0.0) * c["decay"] for di, c in zip(dintras, chains)]
        dkb1 = [_dot(da, c["k"]) for da, c in zip(das, chains)]
        dk1 = [_dot_tn(da, c["kb"]) for da, c in zip(das, chains)]
        dk2 = [_dot_tn(dqk, c["q"]) for dqk, c in zip(dqks, chains)]
        dq1 = [_dot(dqk, c["k"]) for dqk, c in zip(dqks, chains)]
        grads, mms, p_gs, p_betas, p_kds = [], [], [], [], []
        for n, c in enumerate(chains):
            incl = c["masks"][2]
            dkb = dkb1[n] + dkbgs[n] * c["eg"]
            kd = c["dkdec"] * c["kdec"]
            mms.append((dls[n] * c["amat"] + jnp.where(incl, dintras[n], 0.0) * c["qk"]) * c["decay"])
            p_gs.append(c["dqg"] * c["qg"] - kd + dkbgs[n] * c["kbg"])
            p_betas.append(dkb * c["k"] + dvbs[n] * c["v"])
            p_kds.append(kd)
            grads.append((dq1[n] + c["dqg"] * c["eg"],
                          dk1[n] + dk2[n] + c["dkdec"] * c["ek"] + dkb * c["bcol"],
                          dvbs[n] * c["bcol"]))
        row_sums = [jnp.sum(mm, axis=1, keepdims=True) for mm in mms]
        col_sums = [jnp.sum(mm, axis=0, keepdims=True) for mm in mms]
        g_sums = [jnp.sum(pg, axis=1, keepdims=True) for pg in p_gs]
        dbetas = [jnp.sum(pb, axis=1, keepdims=True) for pb in p_betas]
        kd_tots = [jnp.sum(jnp.sum(pk, axis=1, keepdims=True), axis=0, keepdims=True) for pk in p_kds]
        dgcs = [rs - _row_to_col(cs, *c["masks"][0:2]) + gs for rs, cs, gs, c in zip(row_sums, col_sums, g_sums, chains)]
        dgrs = [_col_to_row(dgc, *c["masks"][0:2]) for dgc, c in zip(dgcs, chains)]
        draws = [jnp.sum(jnp.where(jnp.logical_not(c["masks"][3]), dgr, 0.0), axis=1, keepdims=True) + c["dglast"] + kt
                 for dgr, kt, c in zip(dgrs, kd_tots, chains)]
        for c, draw, dbeta in zip(chains, draws, dbetas):
            ch = c["ch"]
            dgates[c["cc"]] = dgates[c["cc"]] + jnp.where(lane == ch, draw, 0.0) + jnp.where(lane == 8 + ch, dbeta, 0.0)
        for cc in range(LOCAL_CHUNKS):
            rows = slice(cc * CHUNK, (cc + 1) * CHUNK)
            for h in range(GDN_HEADS):
                for part in range(3):
                    cols = slice(part * hd + h * GDN_DIM, part * hd + (h + 1) * GDN_DIM)
                    dx_ref[rows, cols] = grads[cc * N_CHAINS + h][part] + grads[cc * N_CHAINS + GDN_HEADS + h][part]
            dg_ref[rows, :] = dgates[cc]

    lc = LOCAL_CHUNKS
    all8 = lambda rows, cols: pl.BlockSpec((lc, N_CHAINS, rows, cols), lambda n: (n, 0, 0, 0))
    own4 = lambda rows, cols: pl.BlockSpec((lc, GDN_HEADS, rows, cols), lambda n: (n, 0, 0, 0))
    row4 = pl.BlockSpec((lc, GDN_HEADS, LANES), lambda n: (n, 0, 0))
    vn_f, vn_b = saved["vn"]
    dvn_f, dvn_b, dw_f, dw_b, dqg_f, dqg_b, dkd_f, dkd_b, dgl_f, dgl_b = scan
    return _grid_call(
        body, "gdn_local_bwd", nc // lc,
        [pl.BlockSpec((lc * CHUNK, 3 * hd), lambda n: (n, 0)), pl.BlockSpec((lc * CHUNK, LANES), lambda n: (n, 0)),
         pl.BlockSpec((lc, 16, CHUNK), lambda n: (n, 0, 0)), pl.BlockSpec((lc * CHUNK, hd), lambda n: (n, 0)),
         all8(CHUNK, CHUNK)] + [own4(CHUNK, GDN_DIM)] * 10 + [row4, row4],
        [pl.BlockSpec((lc * CHUNK, 3 * hd), lambda n: (n, 0)), pl.BlockSpec((lc * CHUNK, LANES), lambda n: (n, 0))],
        [jax.ShapeDtypeStruct((t, 3 * hd), F32), jax.ShapeDtypeStruct((t, LANES), F32)],
        (qkvc, gb, gbt, do, saved["tm"], vn_f, vn_b, dvn_f, dvn_b, dw_f, dw_b, dqg_f, dqg_b, dkd_f, dkd_b, dgl_f, dgl_b),
        exchange=exchange)


def _gdn_post_fwd(of, ob, z, gw, tm):
    t, hd = of.shape

    def body(of_ref, ob_ref, z_ref, w_ref, o_ref):
        for h in range(GDN_HEADS):
            cols = slice(h * GDN_DIM, (h + 1) * GDN_DIM)
            o = of_ref[:, cols] + ob_ref[:, cols]
            zv = z_ref[:, cols]
            o_ref[:, cols] = (o * _rstd(o) * w_ref[...] * (zv * _sigmoid(zv))).astype(BF16)

    row = pl.BlockSpec((tm, hd), lambda i: (i, 0))
    return pl.pallas_call(
        body, name="gdn_post_fwd", grid=(t // tm,),
        in_specs=[row, row, row, _resident((1, GDN_DIM))],
        out_specs=row, out_shape=jax.ShapeDtypeStruct((t, hd), BF16),
        compiler_params=_params(("arbitrary",), VMEM_LIMIT),
    )(of, ob, z, gw)


def _gdn_post_bwd(doa, of, ob, z, gw, tm):
    t, hd = of.shape

    def body(d_ref, of_ref, ob_ref, z_ref, w_ref, do_ref, dz_ref, dw_ref):
        @pl.when(pl.program_id(0) == 0)
        def _():
            dw_ref[...] = jnp.zeros_like(dw_ref)

        dw = jnp.zeros((1, GDN_DIM), F32)
        for h in range(GDN_HEADS):
            cols = slice(h * GDN_DIM, (h + 1) * GDN_DIM)
            o = of_ref[:, cols] + ob_ref[:, cols]
            zv = z_ref[:, cols]
            dv = d_ref[:, cols]
            r = _rstd(o)
            sg = _sigmoid(zv)
            on = o * r * w_ref[...]
            dz_ref[:, cols] = (dv * on * (sg * (1.0 + zv * (1.0 - sg)))).astype(BF16)
            dxr, dwh = _rms_bwd(o, r, w_ref[...], dv * (zv * sg))
            do_ref[:, cols] = dxr
            dw = dw + dwh
        dw_ref[...] += dw

    row = pl.BlockSpec((tm, hd), lambda i: (i, 0))
    return pl.pallas_call(
        body, name="gdn_post_bwd", grid=(t // tm,),
        in_specs=[row, row, row, row, _resident((1, GDN_DIM))],
        out_specs=[row, row, pl.BlockSpec((1, GDN_DIM), lambda i: (0, 0))],
        out_shape=[jax.ShapeDtypeStruct((t, hd), F32), jax.ShapeDtypeStruct((t, hd), BF16),
                   jax.ShapeDtypeStruct((1, GDN_DIM), F32)],
        compiler_params=_params(("arbitrary",), VMEM_LIMIT),
    )(doa, of, ob, z, gw)


SWA_W = SWA_HEADS * SWA_DIM
QBLK = 128
KWIN = QBLK + 2 * RADIUS
WIN_OFFSETS = (0, RADIUS, 2 * RADIUS)


def _t5_bucket(rel):
    nb = REL_BUCKETS // 2
    bucket = (rel > 0).astype(np.int32) * nb
    n = np.abs(rel)
    max_exact = nb // 2
    large = max_exact + (np.log(np.maximum(n, 1) / max_exact)
                         / math.log(REL_MAX_DISTANCE / max_exact) * (nb - max_exact)).astype(np.int32)
    large = np.minimum(large, nb - 1)
    return (bucket + np.where(n < max_exact, n, large)).astype(np.int32)


def _band_tables(dilation):
    a = np.arange(QBLK)
    b = np.arange(KWIN)
    rel = np.stack([b[None, :] - w0 - a[:, None] for w0 in WIN_OFFSETS])
    return np.where(np.abs(rel) <= RADIUS, _t5_bucket(rel * dilation), -1).astype(np.int32)


BAND_CELLS = len(WIN_OFFSETS) * QBLK * KWIN


def _band_index():
    return jnp.asarray(np.concatenate([_band_tables(d).reshape(-1) for _, d in PATTERNS])[None, :])


def _onehot(idx, dtype):
    return (lax.broadcasted_iota(jnp.int32, (REL_BUCKETS, idx.shape[1]), 0) == idx).astype(dtype)


def _bias_tables(rel_bias, idx, tk):
    n = idx.shape[1]

    def body(rb_ref, i_ref, o_ref):
        iv = i_ref[...]
        o_ref[...] = jnp.where(iv < 0, NEG_BIG, _dot_hi(rb_ref[...], _onehot(iv, F32)))

    return pl.pallas_call(
        body, name="bias_tables", grid=(n // tk,),
        in_specs=[_resident((SWA_HEADS, REL_BUCKETS)), pl.BlockSpec((1, tk), lambda k: (0, k))],
        out_specs=pl.BlockSpec((SWA_HEADS, tk), lambda k: (0, k)),
        out_shape=jax.ShapeDtypeStruct((SWA_HEADS, n), F32),
        compiler_params=_params(("arbitrary",), VMEM_LIMIT),
    )(rel_bias.T, idx)


def _head_mean(x2, bd_ref):
    return _dot_hi(x2, bd_ref[...])


VIEW_DILATIONS = tuple(d for _, d in PATTERNS if d > 1)


def _view_spec(tm, d):
    return pl.BlockSpec((tm // d, d * SWA_W), lambda i: (i, 0))


def _view_shape(t, d, dtype):
    return jax.ShapeDtypeStruct((t // d, d * SWA_W), dtype)


N_GROUPS = SWA_W // LANES


def _to_view(src_ref, idx, dst_ref, d, rows):
    for r in range(d):
        for g in range(N_GROUPS):
            cols = slice(r * SWA_W + g * LANES, r * SWA_W + (g + 1) * LANES)
            dst_ref[:, cols] = src_ref[idx, g, pl.ds(r, rows // d, stride=d), :].astype(dst_ref.dtype)


def _from_view(src_ref, dst_ref, idx, d, rows):
    for r in range(d):
        for g in range(N_GROUPS):
            cols = slice(r * SWA_W + g * LANES, r * SWA_W + (g + 1) * LANES)
            dst_ref[idx, g, pl.ds(r, rows // d, stride=d), :] = src_ref[:, cols]


def _swa_prep_fwd(qkvb, qw, kw, bd, tm):
    t = qkvb.shape[0]

    def body(x_ref, qw_ref, kw_ref, bd_ref, *rest):
        outs, sc = rest[:-1], rest[-1]
        for gidx in range(N_GROUPS):
            cols = slice(gidx * LANES, (gidx + 1) * LANES)
            xq = x_ref[:, cols]
            sc[0, gidx] = xq * lax.rsqrt(_head_mean(xq * xq, bd_ref) + EPS) * qw_ref[:, cols] * (SWA_DIM ** -0.5)
            xk = x_ref[:, SWA_W + gidx * LANES:SWA_W + (gidx + 1) * LANES]
            sc[1, gidx] = xk * lax.rsqrt(_head_mean(xk * xk, bd_ref) + EPS) * kw_ref[:, cols]
            sc[2, gidx] = x_ref[:, 2 * SWA_W + gidx * LANES:2 * SWA_W + (gidx + 1) * LANES]
            for i in range(3):
                outs[i][:, cols] = sc[i, gidx].astype(BF16)
        for i in range(3):
            for n, d in enumerate(VIEW_DILATIONS):
                _to_view(sc, i, outs[3 * (n + 1) + i], d, tm)

    return pl.pallas_call(
        body, name="swa_prep_fwd", grid=(t // tm,),
        in_specs=[pl.BlockSpec((tm, 3 * SWA_W), lambda i: (i, 0)), _resident((1, SWA_W)), _resident((1, SWA_W)),
                  _resident((LANES, LANES))],
        out_specs=[_view_spec(tm, d) for d in (1,) + VIEW_DILATIONS for _ in range(3)],
        out_shape=[_view_shape(t, d, BF16) for d in (1,) + VIEW_DILATIONS for _ in range(3)],
        scratch_shapes=[pltpu.VMEM((3, N_GROUPS, tm, LANES), F32)],
        compiler_params=_params(("arbitrary",), VMEM_LIMIT),
    )(qkvb, qw, kw, bd)


def _swa_prep_bwd(qkvb, qw, kw, bd, grads, tm):
    t = qkvb.shape[0]

    def body(x_ref, qw_ref, kw_ref, bd_ref, *rest):
        parts, (dx_ref, dqw_ref, dkw_ref, sc) = rest[:9], rest[9:]
        @pl.when(pl.program_id(0) == 0)
        def _():
            dqw_ref[...] = jnp.zeros_like(dqw_ref)
            dkw_ref[...] = jnp.zeros_like(dkw_ref)

        for i in range(3):
            for n, d in enumerate(VIEW_DILATIONS):
                _from_view(parts[3 * (n + 1) + i], sc, 2 * i + n, d, tm)
        for gidx in range(N_GROUPS):
            cols = slice(gidx * LANES, (gidx + 1) * LANES)
            for i, base, w_ref, dw_ref, scale in ((0, 0, qw_ref, dqw_ref, SWA_DIM ** -0.5),
                                                  (1, SWA_W, kw_ref, dkw_ref, 1.0)):
                xv = x_ref[:, base + gidx * LANES:base + (gidx + 1) * LANES]
                dy = (parts[i][:, cols] + sc[2 * i, gidx] + sc[2 * i + 1, gidx]) * scale
                r = lax.rsqrt(_head_mean(xv * xv, bd_ref) + EPS)
                xhat = xv * r
                dxh = dy * w_ref[:, cols]
                dx = r * (dxh - xhat * _head_mean(dxh * xhat, bd_ref))
                dx_ref[:, base + gidx * LANES:base + (gidx + 1) * LANES] = dx.astype(BF16)
                dw_ref[:, cols] += jnp.sum(dy * xhat, axis=0, keepdims=True)
            dx_ref[:, 2 * SWA_W + gidx * LANES:2 * SWA_W + (gidx + 1) * LANES] = (
                parts[2][:, cols] + sc[4, gidx] + sc[5, gidx]).astype(BF16)

    wrow = pl.BlockSpec((1, SWA_W), lambda i: (0, 0))
    return pl.pallas_call(
        body, name="swa_prep_bwd", grid=(t // tm,),
        in_specs=[pl.BlockSpec((tm, 3 * SWA_W), lambda i: (i, 0)), _resident((1, SWA_W)), _resident((1, SWA_W)),
                  _resident((LANES, LANES))] + [_view_spec(tm, d) for d in (1,) + VIEW_DILATIONS for _ in range(3)],
        out_specs=[pl.BlockSpec((tm, 3 * SWA_W), lambda i: (i, 0)), wrow, wrow],
        out_shape=[jax.ShapeDtypeStruct((t, 3 * SWA_W), BF16), jax.ShapeDtypeStruct((1, SWA_W), F32),
                   jax.ShapeDtypeStruct((1, SWA_W), F32)],
        scratch_shapes=[pltpu.VMEM((6, N_GROUPS, tm, LANES), F32)],
        compiler_params=_params(("arbitrary",), VMEM_LIMIT),
    )(qkvb, qw, kw, bd, *grads)


def _aligned(v, m):
    return v if isinstance(v, int) else pl.multiple_of(v, m)


BAND_GROUP = 2


def _band_loop(nsub, length, step):
    step([(0, 0)], 0)
    if nsub > 2:
        assert (nsub - 2) % BAND_GROUP == 0

        def inner(i, carry):
            s0 = 1 + i * BAND_GROUP
            step([(s0 + e, pl.multiple_of((s0 + e) * QBLK - RADIUS, RADIUS)) for e in range(BAND_GROUP)], 1)
            return carry
        lax.fori_loop(0, (nsub - 2) // BAND_GROUP, inner, 0)
    step([(nsub - 1, length - KWIN)], 2)


def _head_select(lane, a0, a1):
    return jnp.where(lane < SWA_DIM, a0, a1)


def _swa_fwd(qv, kv, vv, bias, dilation, name):
    length = qv.shape[0]
    nsub = length // QBLK
    assert nsub >= 2 and length % QBLK == 0

    def body(q_ref, k_ref, v_ref, b_ref, o_ref, l_ref):
        lane = lax.broadcasted_iota(jnp.int32, (QBLK, LANES), 1)

        def step(blocks, var):
            items = []
            for s, ws in blocks:
                rows = pl.ds(_aligned(s * QBLK, QBLK), QBLK)
                q, kk, vw = q_ref[rows, :], k_ref[pl.ds(ws, KWIN), :], v_ref[pl.ds(ws, KWIN), :]
                for hh in range(2):
                    items.append((hh, jnp.where((lane < SWA_DIM) == (hh == 0), q, jnp.zeros_like(q)), kk, vw))
            lgs = [_dot_nt(qh, kk) + b_ref[hh, var] for hh, qh, kk, _ in items]
            ms = [jnp.max(lg, axis=-1, keepdims=True) for lg in lgs]
            ps = [jnp.exp(lg - m) for lg, m in zip(lgs, ms)]
            dens = [jnp.sum(p, axis=-1, keepdims=True) for p in ps]
            pvs = [_dot(p, it[3]) for p, it in zip(ps, items)]
            for n, (s, _) in enumerate(blocks):
                rows = pl.ds(_aligned(s * QBLK, QBLK), QBLK)
                o0, o1 = (pvs[2 * n + hh] / dens[2 * n + hh] for hh in range(2))
                l0, l1 = (ms[2 * n + hh] + jnp.log(dens[2 * n + hh]) for hh in range(2))
                o_ref[rows, :] = _head_select(lane, o0, o1)
                l_ref[rows, :] = _head_select(lane, l0, l1)

        _band_loop(nsub, length, step)

    blk = pl.BlockSpec((length, LANES), lambda hp, r: (0, r * (SWA_W // LANES) + hp))
    shp = jax.ShapeDtypeStruct(qv.shape, F32)
    return pl.pallas_call(
        body, name=name, grid=(SWA_W // LANES, dilation),
        in_specs=[blk, blk, blk, pl.BlockSpec((2, 3, QBLK, KWIN), lambda hp, r: (hp, 0, 0, 0))],
        out_specs=[blk, blk], out_shape=[shp, shp],
        compiler_params=_params(("arbitrary", "arbitrary"), VMEM_LIMIT),
    )(qv, kv, vv, bias)


def _swa_combine(os_, ls_, tm):
    t = os_[0].shape[0]

    def body(o0, o1, o2, l0, l1, l2, o_ref, ob_ref, la_ref, lb_ref, lc_ref, sc):
        for n, d in enumerate(VIEW_DILATIONS):
            _from_view((o1, o2)[n], sc, n, d, tm)
            _from_view((l1, l2)[n], sc, 2 + n, d, tm)
        for g in range(N_GROUPS):
            cols = slice(g * LANES, (g + 1) * LANES)
            la, lb, lc = l0[:, cols], sc[2, g], sc[3, g]
            m = jnp.maximum(jnp.maximum(la, lb), lc)
            tot = m + jnp.log(jnp.exp(la - m) + jnp.exp(lb - m) + jnp.exp(lc - m))
            o = jnp.exp(la - tot) * o0[:, cols] + jnp.exp(lb - tot) * sc[0, g] + jnp.exp(lc - tot) * sc[1, g]
            o_ref[:, cols] = o
            ob_ref[:, cols] = o.astype(BF16)
            la_ref[:, cols] = tot
            sc[4, g] = tot
        for n, d in enumerate(VIEW_DILATIONS):
            _to_view(sc, 4, (lb_ref, lc_ref)[n], d, tm)

    specs = [_view_spec(tm, d) for d in (1,) + VIEW_DILATIONS]
    return pl.pallas_call(
        body, name="swa_combine", grid=(t // tm,), in_specs=specs + specs, out_specs=[specs[0], specs[0]] + specs,
        out_shape=[jax.ShapeDtypeStruct((t, SWA_W), F32), jax.ShapeDtypeStruct((t, SWA_W), BF16)]
                  + [_view_shape(t, d, F32) for d in (1,) + VIEW_DILATIONS],
        scratch_shapes=[pltpu.VMEM((5, N_GROUPS, tm, LANES), F32)],
        compiler_params=_params(("arbitrary",), VMEM_LIMIT),
    )(*os_, *ls_)


def _swa_bwd_prep(do, o, bd, tm):
    t = do.shape[0]

    def body(d_ref, o_ref, bd_ref, dd1, dd4, dd16, db1, db4, db16, sc):
        for gidx in range(N_GROUPS):
            cols = slice(gidx * LANES, (gidx + 1) * LANES)
            dv = d_ref[:, cols]
            dd = _head_mean(dv * o_ref[:, cols], bd_ref) * float(SWA_DIM)
            sc[0, gidx] = dd
            sc[1, gidx] = dv
            dd1[:, cols] = dd
            db1[:, cols] = dv.astype(BF16)
        for n, d in enumerate(VIEW_DILATIONS):
            _to_view(sc, 0, (dd4, dd16)[n], d, tm)
            _to_view(sc, 1, (db4, db16)[n], d, tm)

    specs = [_view_spec(tm, d) for d in (1,) + VIEW_DILATIONS]
    return pl.pallas_call(
        body, name="swa_bwd_prep", grid=(t // tm,), in_specs=[specs[0], specs[0], _resident((LANES, LANES))],
        out_specs=specs + specs,
        out_shape=[_view_shape(t, d, F32) for d in (1,) + VIEW_DILATIONS]
                  + [_view_shape(t, d, BF16) for d in (1,) + VIEW_DILATIONS],
        scratch_shapes=[pltpu.VMEM((2, N_GROUPS, tm, LANES), F32)],
        compiler_params=_params(("arbitrary",), VMEM_LIMIT),
    )(do, o, bd)


def _swa_bwd(qv, kv, vv, dov, lv, ddv, bias_a, dilation, name):
    length = qv.shape[0]
    nsub = length // QBLK
    single = pl.Buffered(1) if dilation == 1 else None

    def body(q_ref, k_ref, v_ref, do_ref, l_ref, dd_ref, ba_ref, dq_ref, dk_ref, dv_ref, db_ref):
        @pl.when(pl.program_id(1) == 0)
        def _():
            db_ref[...] = jnp.zeros_like(db_ref)

        lane = lax.broadcasted_iota(jnp.int32, (QBLK, LANES), 1)
        lanew = lax.broadcasted_iota(jnp.int32, (KWIN, LANES), 1)

        def step(blocks, var):
            items = []
            for s, ws in blocks:
                rows = pl.ds(_aligned(s * QBLK, QBLK), QBLK)
                win = pl.ds(ws, KWIN)
                q, dov_ = q_ref[rows, :], do_ref[rows, :]
                kk, vw = k_ref[win, :], v_ref[win, :]
                lse, dd = l_ref[rows, :], dd_ref[rows, :]
                for hh in range(2):
                    mine = (lane < SWA_DIM) == (hh == 0)
                    col = slice(hh * SWA_DIM, hh * SWA_DIM + 1)
                    items.append((hh, jnp.where(mine, q, jnp.zeros_like(q)), jnp.where(mine, dov_, jnp.zeros_like(dov_)),
                                  kk, vw, lse[:, col], dd[:, col], q, dov_))
            lgs = [_dot_nt(it[1], it[3]) + ba_ref[it[0], var] for it in items]
            dps = [_dot_nt(it[2], it[4]) for it in items]
            ps = [jnp.exp(lg - it[5]) for lg, it in zip(lgs, items)]
            dss = [p * (dp - it[6]) for p, dp, it in zip(ps, dps, items)]
            dqs = [_dot(ds, it[3]) for ds, it in zip(dss, items)]
            dks = [_dot_tn(ds, it[7]) for ds, it in zip(dss, items)]
            dvs = [_dot_tn(p, it[8]) for p, it in zip(ps, items)]
            for n, (s, ws) in enumerate(blocks):
                rows = pl.ds(_aligned(s * QBLK, QBLK), QBLK)
                win = pl.ds(ws, KWIN)
                dq_ref[rows, :] = _head_select(lane, dqs[2 * n], dqs[2 * n + 1])
                dk_ref[win, :] += _head_select(lanew, dks[2 * n], dks[2 * n + 1])
                dv_ref[win, :] += _head_select(lanew, dvs[2 * n], dvs[2 * n + 1])
            for hh in range(2):
                tot = dss[hh]
                for n in range(1, len(blocks)):
                    tot = tot + dss[2 * n + hh]
                db_ref[hh, var] += tot

        dk_ref[...] = jnp.zeros_like(dk_ref)
        dv_ref[...] = jnp.zeros_like(dv_ref)
        _band_loop(nsub, length, step)

    imap = lambda hp, r: (0, r * (SWA_W // LANES) + hp)
    blk_in = pl.BlockSpec((length, LANES), imap, pipeline_mode=single)
    blk_out = pl.BlockSpec((length, LANES), imap)
    shp = jax.ShapeDtypeStruct(qv.shape, F32)
    return pl.pallas_call(
        body, name=name, grid=(SWA_W // LANES, dilation),
        in_specs=[blk_in] * 6 + [pl.BlockSpec((2, 3, QBLK, KWIN), lambda hp, r: (hp, 0, 0, 0))],
        out_specs=[blk_out, blk_out, blk_out, pl.BlockSpec((2, 3, QBLK, KWIN), lambda hp, r: (hp, 0, 0, 0))],
        out_shape=[shp, shp, shp, jax.ShapeDtypeStruct((SWA_HEADS, 3, QBLK, KWIN), F32)],
        compiler_params=_params(("arbitrary", "arbitrary"), VMEM_LIMIT),
    )(qv, kv, vv, dov, lv, ddv, bias_a)


def _bias_grad(ds2, idx, tk):
    n = ds2.shape[1]
    nk = n // tk

    def body(a_ref, i_ref, o_ref):
        @pl.when(pl.program_id(0) == 0)
        def _():
            o_ref[...] = jnp.zeros_like(o_ref)

        oh = _onehot(i_ref[...], BF16)
        rest = a_ref[...]
        acc = jnp.zeros((SWA_HEADS, REL_BUCKETS), F32)
        for _ in range(3):
            piece = rest.astype(BF16)
            acc = acc + _dot_nt(piece, oh)
            rest = rest - piece.astype(F32)
        o_ref[...] += acc

    return pl.pallas_call(
        body, name="bias_grad", grid=(nk,),
        in_specs=[pl.BlockSpec((SWA_HEADS, tk), lambda k: (0, k)), pl.BlockSpec((1, tk), lambda k: (0, k))],
        out_specs=pl.BlockSpec((SWA_HEADS, REL_BUCKETS), lambda k: (0, 0)),
        out_shape=jax.ShapeDtypeStruct((SWA_HEADS, REL_BUCKETS), F32),
        compiler_params=_params(("arbitrary",), VMEM_LIMIT),
    )(ds2, idx)


def _swa_branch_fwd(qkvb, qw_t, kw_t, rel_bias, bd, tm):
    qkv = _swa_prep_fwd(qkvb, qw_t, kw_t, bd, tm)
    tables = _bias_tables(rel_bias, _band_index(), 8192)
    os_, ls_, tabs = [], [], []
    for n, (_, d) in enumerate(PATTERNS):
        bias = tables[:, n * BAND_CELLS:(n + 1) * BAND_CELLS].reshape(SWA_HEADS, len(WIN_OFFSETS), QBLK, KWIN)
        o_p, l_p = _swa_fwd(*qkv[3 * n:3 * n + 3], bias, d, f"swa_fwd_d{d}")
        os_.append(o_p)
        ls_.append(l_p)
        tabs.append(bias)
    o, o16, *lses = _swa_combine(os_, ls_, tm)
    return o, o16, (qkv, lses, tabs)


def _swa_branch_bwd(do, o, saved, qkvb, qw_t, kw_t, bd, tm):
    qkv, lses, tabs = saved
    prep = _swa_bwd_prep(do, o, bd, tm)
    grads, dss = [], []
    for n, ((_, d), bias) in enumerate(zip(PATTERNS, tabs)):
        dq, dk, dv, ds = _swa_bwd(*qkv[3 * n:3 * n + 3], prep[3 + n], lses[n], prep[n], bias, d, f"swa_bwd_d{d}")
        grads += [dq, dk, dv]
        dss.append(ds.reshape(SWA_HEADS, -1))
    dqkvb, dqw, dkw = _swa_prep_bwd(qkvb, qw_t, kw_t, bd, grads, tm)
    dbias = _bias_grad(jnp.concatenate(dss, axis=1), _band_index(), 8192)
    fold = lambda w: jnp.sum(w.reshape(SWA_HEADS, SWA_DIM), axis=0)
    return dqkvb, fold(dqw), fold(dkw), dbias.T


def _mesh_pos():
    return lax.axis_index("x"), lax.axis_index("y"), lax.axis_index("c")


def _other_chips(x, y):
    return [(1 - x, y), (x, 1 - y), (1 - x, 1 - y)]


def _remote(src, dst, send_sem, recv_sem, device):
    return pltpu.make_async_remote_copy(src_ref=src, dst_ref=dst, send_sem=send_sem, recv_sem=recv_sem,
                                        device_id=device, device_id_type=MESH)


def _split_axis(shape2):
    return 0 if (shape2[0] // 2) % 16 == 0 else 1


def _half_index(shape2, c):
    axis = _split_axis(shape2)
    h = shape2[axis] // 2
    return (pl.ds(c * h, h), slice(None)) if axis == 0 else (slice(None), pl.ds(c * h, h))


def _all_gather(xs):
    n = len(xs)

    def body(*refs):
        ins, outs = refs[:n], refs[n:2 * n]
        send_sems, recv_sems = refs[2 * n:]
        x, y, c = _mesh_pos()
        me = 2 * x + y
        chips = _other_chips(x, y)
        halves = []
        sends = []
        for a in range(n):
            h = ins[a].shape[0] // 2
            mine, other = pl.ds(c * h, h), pl.ds((1 - c) * h, h)
            halves.append((mine, other))
            for j, chip in enumerate(chips):
                cp = _remote(ins[a].at[mine], outs[a].at[me, mine], send_sems.at[a, j], recv_sems.at[a, j], (*chip, c))
                cp.start()
                sends.append(cp)
        for a in range(n):
            mine, _ = halves[a]
            for j, chip in enumerate(chips):
                src = 2 * chip[0] + chip[1]
                landed = outs[a].at[src, mine]
                _remote(landed, landed, send_sems.at[a, j], recv_sems.at[a, j], (x, y, c)).wait_recv()
                fwd = _remote(landed, landed, send_sems.at[a, 3 + j], recv_sems.at[a, 3 + j], (x, y, 1 - c))
                fwd.start()
                sends.append(fwd)
        for a in range(n):
            _, other = halves[a]
            for j, chip in enumerate(chips):
                src = 2 * chip[0] + chip[1]
                landed = outs[a].at[src, other]
                _remote(landed, landed, send_sems.at[a, 3 + j], recv_sems.at[a, 3 + j], (x, y, c)).wait_recv()
        for cp in sends:
            cp.wait_send()

    outs = pl.pallas_call(
        body, name="all_gather_weights",
        in_specs=[ANY] * n, out_specs=[ANY] * n,
        out_shape=[jax.ShapeDtypeStruct((N_SHARDS,) + a.shape, a.dtype) for a in xs],
        scratch_shapes=[pltpu.SemaphoreType.DMA((n, 6)), pltpu.SemaphoreType.DMA((n, 6))],
    )(*xs)
    me = 2 * lax.axis_index("x") + lax.axis_index("y")
    return [lax.dynamic_update_slice_in_dim(o, a[None], me, 0) for o, a in zip(outs, xs)]


def _rs_pair(gs):
    n = len(gs)

    def body(*refs):
        ins, lands = refs[:n], refs[n:2 * n]
        send_sems, recv_sems = refs[2 * n:]
        x, y, c = _mesh_pos()
        cps = []
        for a in range(n):
            theirs = (slice(None),) + _half_index(ins[a].shape[1:], 1 - c)
            cp = _remote(ins[a].at[theirs], lands[a], send_sems.at[a], recv_sems.at[a], (x, y, 1 - c))
            cp.start()
            cps.append(cp)
        for cp in cps:
            cp.wait()

    def half_shape(g):
        dims = list(g.shape)
        dims[1 + _split_axis(g.shape[1:])] //= 2
        return tuple(dims)

    lands = pl.pallas_call(
        body, name="rs_pair", in_specs=[ANY] * n, out_specs=[ANY] * n,
        out_shape=[jax.ShapeDtypeStruct(half_shape(g), g.dtype) for g in gs],
        scratch_shapes=[pltpu.SemaphoreType.DMA((n,)), pltpu.SemaphoreType.DMA((n,))],
    )(*gs)
    c = lax.axis_index("c")
    owns = []
    for g in gs:
        axis = 1 + _split_axis(g.shape[1:])
        owns.append(lax.dynamic_slice_in_dim(g, c * (g.shape[axis] // 2), g.shape[axis] // 2, axis))
    return owns + list(lands)


def _rs_chips(ss):
    n = len(ss)

    def body(*refs):
        ins, outs = refs[:n], refs[n:2 * n]
        send_sems, recv_sems = refs[2 * n:]
        x, y, c = _mesh_pos()
        me = 2 * x + y
        chips = _other_chips(x, y)
        cps = []
        for a in range(n):
            for j, chip in enumerate(chips):
                dst_chip = 2 * chip[0] + chip[1]
                cp = _remote(ins[a].at[dst_chip], outs[a].at[me], send_sems.at[a, j], recv_sems.at[a, j], (*chip, c))
                cp.start()
                cps.append(cp)
        for a in range(n):
            for j, chip in enumerate(chips):
                src = 2 * chip[0] + chip[1]
                _remote(outs[a].at[src], outs[a].at[src], send_sems.at[a, j], recv_sems.at[a, j], (x, y, c)).wait_recv()
        for cp in cps:
            cp.wait_send()

    outs = pl.pallas_call(
        body, name="rs_chips", in_specs=[ANY] * n, out_specs=[ANY] * n,
        out_shape=[jax.ShapeDtypeStruct(s.shape, s.dtype) for s in ss],
        scratch_shapes=[pltpu.SemaphoreType.DMA((n, 3)), pltpu.SemaphoreType.DMA((n, 3))],
    )(*ss)
    me = 2 * lax.axis_index("x") + lax.axis_index("y")
    return [lax.dynamic_update_slice_in_dim(o, lax.dynamic_slice_in_dim(s, me, 1, 0), me, 0) for o, s in zip(outs, ss)]


def _rs_join(fs, axes):
    n = len(fs)

    def whole(f, axis):
        dims = list(f.shape)
        dims[axis] *= 2
        return tuple(dims)

    def body(*refs):
        ins, outs = refs[:n], refs[n:2 * n]
        send_sems, recv_sems = refs[2 * n:]
        x, y, c = _mesh_pos()
        cps = []
        for a in range(n):
            h = ins[a].shape[axes[a]]
            mine = (pl.ds(c * h, h), slice(None)) if axes[a] == 0 else (slice(None), pl.ds(c * h, h))
            cp = _remote(ins[a], outs[a].at[mine], send_sems.at[a], recv_sems.at[a], (x, y, 1 - c))
            cp.start()
            cps.append(cp)
        for cp in cps:
            cp.wait()

    outs = pl.pallas_call(
        body, name="rs_join", in_specs=[ANY] * n, out_specs=[ANY] * n,
        out_shape=[jax.ShapeDtypeStruct(whole(f, ax), f.dtype) for f, ax in zip(fs, axes)],
        scratch_shapes=[pltpu.SemaphoreType.DMA((n,)), pltpu.SemaphoreType.DMA((n,))],
    )(*fs)
    c = lax.axis_index("c")
    return [lax.dynamic_update_slice_in_dim(o, f, c * f.shape[ax], ax) for o, f, ax in zip(outs, fs, axes)]


def _gather_exchange(xs):
    def start(cin, cout, send_sems, recv_sems):
        x, y, c = _mesh_pos()
        me = 2 * x + y
        for a, (src, dst) in enumerate(zip(cin, cout)):
            mine = _half_index(src.shape, c)
            for j, chip in enumerate(_other_chips(x, y)):
                _remote(src.at[mine], dst.at[(me,) + mine], send_sems.at[a, j], recv_sems.at[a, j], (*chip, c)).start()

    def finish(cin, cout, send_sems, recv_sems):
        x, y, c = _mesh_pos()
        for a, dst in enumerate(cout):
            for j, chip in enumerate(_other_chips(x, y)):
                landed = dst.at[(2 * chip[0] + chip[1],) + _half_index(dst.shape[1:], c)]
                _remote(landed, landed, send_sems.at[a, j], recv_sems.at[a, j], (x, y, c)).wait()

    return _Exchange(tuple(xs), tuple(jax.ShapeDtypeStruct((N_SHARDS,) + a.shape, a.dtype) for a in xs), start, finish)


def _gather_forward(gs, xs):
    n = len(gs)

    def body(*refs):
        outs = refs[n:2 * n]
        send_sems, recv_sems = refs[2 * n:]
        x, y, c = _mesh_pos()
        chips = _other_chips(x, y)
        cps = []
        for a in range(n):
            for j, chip in enumerate(chips):
                landed = outs[a].at[(2 * chip[0] + chip[1],) + _half_index(outs[a].shape[1:], c)]
                cp = _remote(landed, landed, send_sems.at[a, j], recv_sems.at[a, j], (x, y, 1 - c))
                cp.start()
                cps.append(cp)
        for a in range(n):
            for j, chip in enumerate(chips):
                other = outs[a].at[(2 * chip[0] + chip[1],) + _half_index(outs[a].shape[1:], 1 - c)]
                _remote(other, other, send_sems.at[a, j], recv_sems.at[a, j], (x, y, c)).wait_recv()
        for cp in cps:
            cp.wait_send()

    outs = pl.pallas_call(
        body, name="gather_forward", in_specs=[ANY] * n, out_specs=[ANY] * n,
        out_shape=[jax.ShapeDtypeStruct(g.shape, g.dtype) for g in gs],
        input_output_aliases={i: i for i in range(n)},
        scratch_shapes=[pltpu.SemaphoreType.DMA((n, 3)), pltpu.SemaphoreType.DMA((n, 3))],
    )(*gs)
    me = 2 * lax.axis_index("x") + lax.axis_index("y")
    return [lax.dynamic_update_slice_in_dim(o, a[None], me, 0) for o, a in zip(outs, xs)]


def _scatter_exchange(ss):
    def start(cin, cout, send_sems, recv_sems):
        x, y, c = _mesh_pos()
        me = 2 * x + y
        for a, (src, dst) in enumerate(zip(cin, cout)):
            for j, chip in enumerate(_other_chips(x, y)):
                _remote(src.at[2 * chip[0] + chip[1]], dst.at[me], send_sems.at[a, j], recv_sems.at[a, j],
                        (*chip, c)).start()

    def finish(cin, cout, send_sems, recv_sems):
        x, y, c = _mesh_pos()
        for a, dst in enumerate(cout):
            for j, chip in enumerate(_other_chips(x, y)):
                slot = dst.at[2 * chip[0] + chip[1]]
                _remote(slot, slot, send_sems.at[a, j], recv_sems.at[a, j], (x, y, c)).wait()

    return _Exchange(tuple(ss), tuple(jax.ShapeDtypeStruct(s.shape, s.dtype) for s in ss), start, finish)


def _own_slots(slots, ss):
    me = 2 * lax.axis_index("x") + lax.axis_index("y")
    return [lax.dynamic_update_slice_in_dim(o, lax.dynamic_slice_in_dim(s, me, 1, 0), me, 0) for o, s in zip(slots, ss)]


def _add_pair(a, b, name):
    nj, h, c = a.shape

    def body(a_ref, b_ref, o_ref):
        o_ref[...] = (a_ref[...].astype(F32) + b_ref[...].astype(F32)).astype(BF16)

    blk = pl.BlockSpec((1, h, c), lambda j: (j, 0, 0))
    return pl.pallas_call(body, name=name, grid=(nj,), in_specs=[blk, blk], out_specs=blk,
                          out_shape=jax.ShapeDtypeStruct(a.shape, BF16),
                          compiler_params=_params(("arbitrary",), VMEM_LIMIT))(a, b)


def _sum_slots(l2, name):
    nj, h, c = l2.shape
    th = h // 2 if h % 32 == 0 else h

    def body(i_ref, o_ref):
        acc = i_ref[0].astype(F32)
        for s in range(1, nj):
            acc = acc + i_ref[s].astype(F32)
        o_ref[...] = acc

    return pl.pallas_call(body, name=name, grid=(h // th,),
                          in_specs=[pl.BlockSpec((nj, th, c), lambda i: (0, i, 0))],
                          out_specs=pl.BlockSpec((th, c), lambda i: (i, 0)),
                          out_shape=jax.ShapeDtypeStruct((h, c), F32),
                          compiler_params=_params(("arbitrary",), VMEM_LIMIT))(l2)


def _all_reduce_small(p):
    r = p.shape[0]

    def body(p_ref, o_ref, buf, send_sems, recv_sems):
        x, y, c = _mesh_pos()
        me = 4 * x + 2 * y + c
        buf[me] = p_ref[...]
        cps = []
        k = 0
        for fx in range(2):
            for fy in range(2):
                for fc in range(2):
                    if fx + fy + fc == 0:
                        continue
                    peer = (1 - x if fx else x, 1 - y if fy else y, 1 - c if fc else c)
                    peer_id = 4 * peer[0] + 2 * peer[1] + peer[2]
                    cp = _remote(p_ref, buf.at[me], send_sems.at[k], recv_sems.at[k], peer)
                    cp.start()
                    cps.append((cp, peer_id, k))
                    k += 1
        for cp, peer_id, k in cps:
            _remote(p_ref, buf.at[peer_id], send_sems.at[k], recv_sems.at[k], (x, y, c)).wait_recv()
        for cp, _, _ in cps:
            cp.wait_send()
        acc = buf[0]
        for s in range(1, 8):
            acc = acc + buf[s]
        o_ref[...] = acc

    vm = pl.BlockSpec(memory_space=pltpu.VMEM)
    return pl.pallas_call(
        body, name="all_reduce_small", in_specs=[vm], out_specs=vm,
        out_shape=jax.ShapeDtypeStruct(p.shape, F32),
        scratch_shapes=[pltpu.VMEM((8, r, LANES), F32), pltpu.SemaphoreType.DMA((7,)), pltpu.SemaphoreType.DMA((7,))],
    )(p)


def _adamw(w, g, m, v, name):
    r, c = w.shape
    row_tiles = [d for d in range(8, min(r, 256) + 1, 8) if r % d == 0]
    tr, tc = (max(row_tiles), c) if row_tiles else (r, 256 if c % 256 == 0 else c)
    c1 = 1.0 / (1.0 - ADAM_B1 ** ADAM_STEP)
    c2 = 1.0 / (1.0 - ADAM_B2 ** ADAM_STEP)

    def body(w_ref, g_ref, m_ref, v_ref, d_ref, nm_ref, nv_ref):
        gv = g_ref[...]
        nm = ADAM_B1 * m_ref[...] + (1.0 - ADAM_B1) * gv
        nv = ADAM_B2 * v_ref[...] + (1.0 - ADAM_B2) * (gv * gv)
        d_ref[...] = -ADAM_LR * ((nm * c1) / (jnp.sqrt(nv * c2) + ADAM_EPS) + ADAM_WD * w_ref[...])
        nm_ref[...] = nm
        nv_ref[...] = nv

    blk = pl.BlockSpec((tr, tc), lambda i, j: (i, j))
    shp = jax.ShapeDtypeStruct((r, c), F32)
    return pl.pallas_call(body, name=name, grid=(r // tr, c // tc), in_specs=[blk] * 4, out_specs=[blk] * 3,
                          out_shape=[shp, shp, shp],
                          compiler_params=_params(("arbitrary", "arbitrary"), VMEM_LIMIT))(w, g, m, v)


PACK_UNIT = 8 * LANES


def _pack(arrs):
    parts = []
    for a in arrs:
        f = a.reshape(-1).astype(F32)
        parts.append(jnp.pad(f, (0, (-f.shape[0]) % PACK_UNIT)).reshape(-1, LANES))
    return jnp.concatenate(parts, axis=0)


def _unpack(m, shapes):
    outs, row = [], 0
    for s in shapes:
        n = int(np.prod(s))
        rows = -(-n // PACK_UNIT) * 8
        outs.append(m[row:row + rows].reshape(-1)[:n].reshape(s))
        row += rows
    return outs


WEIGHTS = ["ffn1_norm", "ffn1_w_gate", "ffn1_w_up", "ffn1_w_down", "mix_norm", "w_in", "conv_w", "a_log", "dt_bias",
           "gdn_norm_w", "q_norm_w", "k_norm_w", "rel_bias", "w_out", "ffn2_norm", "ffn2_w_gate", "ffn2_w_up",
           "ffn2_w_down", "final_norm"]
BIG = ["ffn1_w_gate", "ffn1_w_up", "ffn1_w_down", "w_in", "w_out", "ffn2_w_gate", "ffn2_w_up", "ffn2_w_down"]
SMALL = [n for n in WEIGHTS if n not in BIG]
COL_SHARDED = ["ffn1_w_gate", "ffn1_w_up", "w_in", "ffn2_w_gate", "ffn2_w_up"]
N_IN_COLS = 3600
TM = 256
TE = 512
TK = 2048


def kernel(x, ffn1_norm, ffn1_w_gate, ffn1_w_up, ffn1_w_down, mix_norm, w_in, conv_w, a_log, dt_bias, gdn_norm_w, q_norm_w, k_norm_w, rel_bias, w_out, ffn2_norm, ffn2_w_gate, ffn2_w_up, ffn2_w_down, final_norm, loss_target, m_ffn1_norm, m_ffn1_w_gate, m_ffn1_w_up, m_ffn1_w_down, m_mix_norm, m_w_in, m_conv_w, m_a_log, m_dt_bias, m_gdn_norm_w, m_q_norm_w, m_k_norm_w, m_rel_bias, m_w_out, m_ffn2_norm, m_ffn2_w_gate, m_ffn2_w_up, m_ffn2_w_down, m_final_norm, v_ffn1_norm, v_ffn1_w_gate, v_ffn1_w_up, v_ffn1_w_down, v_mix_norm, v_w_in, v_conv_w, v_a_log, v_dt_bias, v_gdn_norm_w, v_q_norm_w, v_k_norm_w, v_rel_bias, v_w_out, v_ffn2_norm, v_ffn2_w_gate, v_ffn2_w_up, v_ffn2_w_down, v_final_norm):
    p = dict(locals())
    xs, target = x[0], loss_target[0]
    t, d = xs.shape
    nc = t // CHUNK
    tk = min(TK, t)
    me = 2 * lax.axis_index("x") + lax.axis_index("y")

    first = ["ffn1_w_gate", "ffn1_w_up", "ffn1_w_down"]
    later = [n for n in BIG if n not in first] + ["conv_w"]
    local = lambda n, a: a[0].T if n in COL_SHARDED else a[0]
    shards = {n: local(n, p[n]).astype(BF16) for n in BIG}
    shards["conv_w"] = conv_w[0]
    gw = dict(zip(first, _all_gather([shards[n] for n in first])))
    f1 = (gw["ffn1_w_gate"], gw["ffn1_w_up"], gw["ffn1_w_down"])
    (x1, xn1, g1, u1), landed = _ffn_fwd(xs, ffn1_norm, *f1, TM, "ffn1_fwd",
                                         exchange=_gather_exchange([shards[n] for n in later]))
    gw.update(zip(later, _gather_forward(landed, [shards[n] for n in later])))
    w_in_t = gw["w_in"].reshape(N_IN_COLS, d)
    wp = jnp.concatenate([w_in_t[:2048], jnp.pad(w_in_t[2048:2064], ((0, LANES - 16), (0, 0))), w_in_t[2064:]], axis=0)
    w_out_full = gw["w_out"].reshape(d, d)
    conv_rows = conv_w.shape[1]
    cw = jnp.pad(gw["conv_w"].reshape(N_SHARDS * conv_rows, CONV_TAPS).T, ((0, 8 - CONV_TAPS), (0, 0)))
    gp = jnp.pad(jnp.stack([a_log.reshape(8), dt_bias.reshape(8)]), ((0, 6), (0, LANES - 8)))
    gdn_w = gdn_norm_w.reshape(1, GDN_DIM)
    qw_t = jnp.tile(q_norm_w.reshape(1, SWA_DIM), (1, SWA_HEADS))
    kw_t = jnp.tile(k_norm_w.reshape(1, SWA_DIM), (1, SWA_HEADS))
    bd = jnp.asarray(np.kron(np.eye(2), np.full((SWA_DIM, SWA_DIM), 1.0 / SWA_DIM)), F32)
    f2 = (gw["ffn2_w_gate"], gw["ffn2_w_up"], gw["ffn2_w_down"])

    hn, qkva, z, ab, qkvb = _mix_in_fwd(x1, mix_norm, wp, TM)
    qkvc, gb = _gdn_prep_fwd(qkva, cw, ab, gp, TM)
    gbt = jnp.transpose(gb[:, :16].reshape(nc, CHUNK, 16), (0, 2, 1))
    o_f, o_b, gdn_saved = _gdn_fwd(qkvc, gb, gbt)
    oa = _gdn_post_fwd(o_f, o_b, z, gdn_w, TE)
    o_swa, o_swa16, swa_saved = _swa_branch_fwd(qkvb, qw_t, kw_t, rel_bias, bd, TE)
    x2 = _mix_out_fwd(x1, oa, o_swa, w_out_full, TM)
    (dx3, xn2, g2, u2, loss_part, d_final), _ = _ffn_fwd(x2, ffn2_norm, *f2, TM, "ffn2_fwd", head=(final_norm, target))

    def pair_sums(partials, tag):
        pair = _rs_pair(partials)
        k = len(partials)
        return [_add_pair(pair[i], pair[k + i], f"rs_add_{tag}{i}") for i in range(k)]

    (dx2, dyh2, dg2, du2, h2, d_nw2), _ = _ffn_bwd_dx(dx3, x2, ffn2_norm, g2, u2, *f2, TM, "ffn2_bwd_dx")
    dwg2 = _matmul_tn(dg2, xn2, tk, "ffn2_dwg")
    dwu2 = _matmul_tn(du2, xn2, tk, "ffn2_dwu")
    dwd2 = _matmul_tn(h2, dyh2, tk, "ffn2_dwd")
    sums_f2 = pair_sums([dwg2, dwu2, dwd2], "a")
    doa, dob, dx2b = _mix_out_bwd(dx2, w_out_full, TM)
    dwo = jnp.concatenate([_matmul_tn(oa, dx2b, tk, "w_out_dw_a")[0], _matmul_tn(o_swa16, dx2b, tk, "w_out_dw_b")[0]],
                          axis=0).reshape(N_SHARDS, d // N_SHARDS, d)
    do_g, dz, d_gdnw = _gdn_post_bwd(doa, o_f, o_b, z, gdn_w, TE)
    (dqkvc, dgates), slots_f2 = _gdn_bwd(qkvc, gb, gbt, do_g, gdn_saved, exchange=_scatter_exchange(sums_f2))
    dqkva, dab, dcw, dgp = _gdn_prep_bwd(qkva, cw, ab, gp, dqkvc, dgates, TM)
    dqkvb, d_qw, d_kw, d_rel = _swa_branch_bwd(dob, o_swa, swa_saved, qkvb, qw_t, kw_t, bd, TE)
    dpieces = (dqkva, dz, dab, dqkvb)
    dwp = [_matmul_tn(dp, hn, tk, f"w_in_dw_{i}")[0] for i, dp in enumerate(dpieces)]
    dw_in = jnp.concatenate([dwp[0], dwp[1], dwp[2][:16], dwp[3]], axis=0).reshape(N_SHARDS, N_IN_COLS // N_SHARDS, d)
    sums_mix = pair_sums([dw_in, dwo], "b")
    (dx1, d_mixnw), slots_mix = _mix_in_bwd_dx(dx2, x1, mix_norm, dpieces, wp, TM, exchange=_scatter_exchange(sums_mix))
    (gx, dyh1, dg1, du1, h1, d_nw1), _ = _ffn_bwd_dx(dx1, xs, ffn1_norm, g1, u1, *f1, TM, "ffn1_bwd_dx")
    dwg1 = _matmul_tn(dg1, xn1, tk, "ffn1_dwg")
    dwu1 = _matmul_tn(du1, xn1, tk, "ffn1_dwu")
    dwd1 = _matmul_tn(h1, dyh1, tk, "ffn1_dwd")
    slots_f1 = _rs_chips(pair_sums([dwg1, dwu1, dwd1], "c"))
    slots = slots_f1 + _own_slots(slots_mix, sums_mix) + _own_slots(slots_f2, sums_f2)
    halves = [_sum_slots(s, f"rs_sum_{i}") for i, s in enumerate(slots)]
    g_big = dict(zip(BIG, _rs_join(halves, [_split_axis(shards[n].shape) for n in BIG])))

    small_partial = {"ffn1_norm": d_nw1, "mix_norm": d_mixnw, "a_log": dgp[0, 0:8], "dt_bias": dgp[1, 0:8],
                     "gdn_norm_w": d_gdnw, "q_norm_w": d_qw, "k_norm_w": d_kw, "rel_bias": d_rel,
                     "ffn2_norm": d_nw2, "final_norm": d_final, "conv_w": dcw[0:CONV_TAPS].T}
    red = _all_reduce_small(_pack([small_partial[n] for n in SMALL] + [loss_part[0, 0:1]]))
    full_shapes = [p[n].shape if n != "conv_w" else (N_SHARDS * conv_rows, CONV_TAPS) for n in SMALL]
    red_parts = _unpack(red, full_shapes + [(1,)])
    loss = red_parts[-1].reshape(())
    g_small = dict(zip(SMALL, red_parts[:-1]))
    g_small["conv_w"] = lax.dynamic_slice_in_dim(g_small["conv_w"], me * conv_rows, conv_rows, 0).reshape(conv_w.shape)

    grads, deltas, new_m, new_v = {}, {}, {}, {}
    for n in BIG:
        back = (lambda a: a.T[None]) if n in COL_SHARDED else (lambda a: a[None])
        grads[n] = back(g_big[n])
        dl, nm, nv = _adamw(local(n, p[n]), g_big[n], local(n, p["m_" + n]), local(n, p["v_" + n]), "adamw_" + n)
        deltas[n], new_m[n], new_v[n] = back(dl), back(nm), back(nv)
    packed = [_pack([src[n] for n in SMALL]) for src in
              ({n: p[n] for n in SMALL}, g_small, {n: p["m_" + n] for n in SMALL}, {n: p["v_" + n] for n in SMALL})]
    small_shapes = [p[n].shape for n in SMALL]
    for dst, arr in zip((deltas, new_m, new_v), _adamw(*packed, "adamw_small")):
        dst.update(zip(SMALL, _unpack(arr, small_shapes)))
    grads.update(g_small)

    return (loss, gx[None], *[grads[n] for n in WEIGHTS], *[deltas[n] for n in WEIGHTS],
            *[new_m[n] for n in WEIGHTS], *[new_v[n] for n in WEIGHTS])
```

```python
import math
from typing import Callable, NamedTuple

import numpy as np
import jax
import jax.numpy as jnp
from jax import lax
from jax.experimental import pallas as pl
from jax.experimental.pallas import tpu as pltpu

F32 = jnp.float32
BF16 = jnp.bfloat16
HIGHEST = lax.Precision.HIGHEST
MESH = pl.DeviceIdType.MESH

EPS = 1e-6
NEG_BIG = -1e30
GDN_HEADS = 4
GDN_DIM = 128
CHUNK = 64
SWA_HEADS = 8
SWA_DIM = 64
PATTERNS = ((128, 1), (512, 4), (2048, 16))
RADIUS = 64
REL_BUCKETS = 32
REL_MAX_DISTANCE = 1024
CONV_TAPS = 5
N_SHARDS = 4
LANES = 128
VMEM_LIMIT = 56 * 1024 * 1024

ADAM_LR, ADAM_B1, ADAM_B2, ADAM_EPS, ADAM_WD, ADAM_STEP = 0.001, 0.9, 0.999, 1e-08, 0.01, 10


def _params(sem=None, vmem=None):
    return pltpu.CompilerParams(dimension_semantics=sem, vmem_limit_bytes=vmem)


def _resident(shape):
    nd = len(shape)
    return pl.BlockSpec(shape, lambda *_: (0,) * nd, pipeline_mode=pl.Buffered(1))


ANY = pl.BlockSpec(memory_space=pl.ANY)


class _Exchange(NamedTuple):
    arrays: tuple
    out_shape: tuple
    start: Callable
    finish: Callable


def _grid_call(body, name, nsteps, in_specs, out_specs, out_shape, operands, scratch=(), exchange=None):
    params = _params(("arbitrary",), VMEM_LIMIT)
    if exchange is None:
        res = pl.pallas_call(body, name=name, grid=(nsteps,), in_specs=list(in_specs), out_specs=list(out_specs),
                             out_shape=list(out_shape), scratch_shapes=list(scratch), compiler_params=params)(*operands)
        return list(res), []
    n_in, n_out, k, n_scr = len(in_specs), len(out_specs), len(exchange.arrays), len(scratch)

    def wrapped(*refs):
        ins, cin = refs[:n_in], refs[n_in:n_in + k]
        outs, cout = refs[n_in + k:n_in + k + n_out], refs[n_in + k + n_out:n_in + 2 * k + n_out]
        rest = refs[n_in + 2 * k + n_out:]
        scr, (send_sems, recv_sems) = rest[:n_scr], rest[n_scr:]

        @pl.when(pl.program_id(0) == 0)
        def _():
            exchange.start(cin, cout, send_sems, recv_sems)

        body(*ins, *outs, *scr)

        @pl.when(pl.program_id(0) == nsteps - 1)
        def _():
            exchange.finish(cin, cout, send_sems, recv_sems)

    res = pl.pallas_call(
        wrapped, name=name, grid=(nsteps,), in_specs=list(in_specs) + [ANY] * k, out_specs=list(out_specs) + [ANY] * k,
        out_shape=list(out_shape) + list(exchange.out_shape),
        scratch_shapes=list(scratch) + [pltpu.SemaphoreType.DMA((k, 3)), pltpu.SemaphoreType.DMA((k, 3))],
        compiler_params=params)(*operands, *exchange.arrays)
    return list(res[:n_out]), list(res[n_out:])


def _dot(a, b):
    return jnp.dot(a.astype(BF16), b.astype(BF16), preferred_element_type=F32)


def _dot_nt(a, b):
    return lax.dot_general(a.astype(BF16), b.astype(BF16), (((1,), (1,)), ((), ())), preferred_element_type=F32)


def _dot_tn(a, b):
    return lax.dot_general(a.astype(BF16), b.astype(BF16), (((0,), (0,)), ((), ())), preferred_element_type=F32)


def _dot_hi(a, b):
    return jnp.dot(a, b, preferred_element_type=F32, precision=HIGHEST)


def _sigmoid(x):
    return 1.0 / (1.0 + jnp.exp(-x))


def _rstd(xf):
    return lax.rsqrt(jnp.mean(xf * xf, axis=-1, keepdims=True) + EPS)


def _rms_bwd(xf, r, nw, dxn):
    xhat = xf * r
    dxh = dxn * nw
    dx = r * (dxh - xhat * jnp.mean(dxh * xhat, axis=-1, keepdims=True))
    return dx, jnp.sum(dxn * xhat, axis=0, keepdims=True)


def _ffn_fwd(x, nw, wg, wu, wd, tm, name, exchange=None, head=None):
    t, d = x.shape
    nj, fs, _ = wg.shape

    def body(x_ref, nw_ref, wg_ref, wu_ref, wd_ref, *rest):
        if head is None:
            y_ref, xn_ref, g_ref, u_ref = rest
        else:
            fw_ref, t_ref, y_ref, xn_ref, g_ref, u_ref, loss_ref, dfw_ref = rest

            @pl.when(pl.program_id(0) == 0)
            def _():
                loss_ref[...] = jnp.zeros_like(loss_ref)
                dfw_ref[...] = jnp.zeros_like(dfw_ref)

        xf = x_ref[...]
        xn = (xf * _rstd(xf) * nw_ref[...]).astype(BF16)
        xn_ref[...] = xn
        acc = jnp.zeros((tm, d), F32)
        for j in range(nj):
            g = _dot_nt(xn, wg_ref[j])
            u = _dot_nt(xn, wu_ref[j])
            h = (g * _sigmoid(g) * u).astype(BF16)
            acc = acc + jnp.dot(h, wd_ref[j], preferred_element_type=F32)
            g_ref[j] = g.astype(BF16)
            u_ref[j] = u.astype(BF16)
        y = xf + 0.5 * acc
        if head is None:
            y_ref[...] = y
        else:
            r = _rstd(y)
            err = y * r * fw_ref[...] - t_ref[...]
            loss_ref[...] += 0.5 * jnp.sum(jnp.mean(err * err, axis=-1, keepdims=True), axis=0, keepdims=True)
            dy, dfw = _rms_bwd(y, r, fw_ref[...], err * (1.0 / d))
            y_ref[...] = dy
            dfw_ref[...] += dfw

    row = pl.BlockSpec((tm, d), lambda i: (i, 0))
    act = pl.BlockSpec((nj, tm, fs), lambda i: (0, i, 0))
    in_specs = [row, _resident((1, d)), _resident(wg.shape), _resident(wu.shape), _resident(wd.shape)]
    out_specs = [row, row, act, act]
    out_shape = [jax.ShapeDtypeStruct((t, d), F32), jax.ShapeDtypeStruct((t, d), BF16),
                 jax.ShapeDtypeStruct((nj, t, fs), BF16), jax.ShapeDtypeStruct((nj, t, fs), BF16)]
    operands = (x, nw, wg, wu, wd)
    if head is not None:
        in_specs += [_resident((1, d)), row]
        out_specs += [pl.BlockSpec((1, LANES), lambda i: (0, 0)), pl.BlockSpec((1, d), lambda i: (0, 0))]
        out_shape += [jax.ShapeDtypeStruct((1, LANES), F32), jax.ShapeDtypeStruct((1, d), F32)]
        operands += tuple(head)
    return _grid_call(body, name, t // tm, in_specs, out_specs, out_shape, operands, exchange=exchange)


def _ffn_bwd_dx(dy, x, nw, g, u, wg, wu, wd, tm, name, exchange=None):
    t, d = x.shape
    nj, fs, _ = wg.shape

    def body(dy_ref, x_ref, nw_ref, g_ref, u_ref, wg_ref, wu_ref, wd_ref,
             dx_ref, dyh_ref, dg_ref, du_ref, h_ref, dnw_ref):
        @pl.when(pl.program_id(0) == 0)
        def _():
            dnw_ref[...] = jnp.zeros_like(dnw_ref)

        dyv = dy_ref[...]
        dyh = (0.5 * dyv).astype(BF16)
        dyh_ref[...] = dyh
        dxn = jnp.zeros((tm, d), F32)
        dh_next = _dot_nt(dyh, wd_ref[0])
        for j in range(nj):
            dh = dh_next
            gv = g_ref[j].astype(F32)
            uv = u_ref[j].astype(F32)
            sg = _sigmoid(gv)
            si = gv * sg
            dg = (dh * uv * (sg * (1.0 + gv * (1.0 - sg)))).astype(BF16)
            du = (dh * si).astype(BF16)
            if j + 1 < nj:
                dh_next = _dot_nt(dyh, wd_ref[j + 1])
            h_ref[j] = (si * uv).astype(BF16)
            dg_ref[j] = dg
            du_ref[j] = du
            dxn = dxn + _dot(dg, wg_ref[j]) + _dot(du, wu_ref[j])
        xf = x_ref[...]
        dxr, dnw = _rms_bwd(xf, _rstd(xf), nw_ref[...], dxn)
        dx_ref[...] = dyv + dxr
        dnw_ref[...] += dnw

    row = pl.BlockSpec((tm, d), lambda i: (i, 0))
    act = pl.BlockSpec((nj, tm, fs), lambda i: (0, i, 0))
    act_shape = jax.ShapeDtypeStruct((nj, t, fs), BF16)
    return _grid_call(
        body, name, t // tm,
        [row, row, _resident((1, d)), act, act, _resident(wg.shape), _resident(wu.shape), _resident(wd.shape)],
        [row, row, act, act, act, pl.BlockSpec((1, d), lambda i: (0, 0))],
        [jax.ShapeDtypeStruct((t, d), F32), jax.ShapeDtypeStruct((t, d), BF16),
         act_shape, act_shape, act_shape, jax.ShapeDtypeStruct((1, d), F32)],
        (dy, x, nw, g, u, wg, wu, wd), exchange=exchange)


def _matmul_tn(a, b, tk, name, exchange=None):
    a3, b3 = a.ndim == 3, b.ndim == 3
    nj = a.shape[0] if a3 else (b.shape[0] if b3 else 1)
    t, m = a.shape[-2:]
    n = b.shape[-1]
    nt = t // tk

    def body(a_ref, b_ref, o_ref, acc_ref):
        k = pl.program_id(0) % nt

        @pl.when(k == 0)
        def _():
            acc_ref[...] = jnp.zeros_like(acc_ref)

        acc_ref[...] += lax.dot_general(a_ref[...], b_ref[...], (((0,), (0,)), ((), ())),
                                        preferred_element_type=F32)

        @pl.when(k == nt - 1)
        def _():
            o_ref[...] = acc_ref[...].astype(o_ref.dtype)

    a_spec = (pl.BlockSpec((None, tk, m), lambda i: (i // nt, i % nt, 0)) if a3
              else pl.BlockSpec((tk, m), lambda i: (i % nt, 0)))
    b_spec = (pl.BlockSpec((None, tk, n), lambda i: (i // nt, i % nt, 0)) if b3
              else pl.BlockSpec((tk, n), lambda i: (i % nt, 0)))
    (out,), landed = _grid_call(
        body, name, nj * nt, [a_spec, b_spec], [pl.BlockSpec((None, m, n), lambda i: (i // nt, 0, 0))],
        [jax.ShapeDtypeStruct((nj, m, n), BF16)], (a, b), scratch=[pltpu.VMEM((m, n), F32)], exchange=exchange)
    return out if exchange is None else (out, landed)


P_QKVA, P_Z, P_AB, P_QKVB = (0, 1536), (1536, 2048), (2048, 2176), (2176, 3712)
P_PIECES = (P_QKVA, P_Z, P_AB, P_QKVB)
P_COLS = 3712


def _mix_in_fwd(x1, nw, wp, tm):
    t, d = x1.shape

    def body(x_ref, nw_ref, w_ref, hn_ref, *outs):
        xf = x_ref[...]
        xn = (xf * _rstd(xf) * nw_ref[...]).astype(BF16)
        hn_ref[...] = xn
        for (a, b), o_ref in zip(P_PIECES, outs):
            o_ref[...] = _dot_nt(xn, w_ref[a:b, :])

    row = pl.BlockSpec((tm, d), lambda i: (i, 0))
    return pl.pallas_call(
        body, name="mix_in_fwd", grid=(t // tm,),
        in_specs=[row, _resident((1, d)), _resident(wp.shape)],
        out_specs=[row] + [pl.BlockSpec((tm, b - a), lambda i: (i, 0)) for a, b in P_PIECES],
        out_shape=[jax.ShapeDtypeStruct((t, d), BF16)]
                  + [jax.ShapeDtypeStruct((t, b - a), F32) for a, b in P_PIECES],
        compiler_params=_params(("arbitrary",), VMEM_LIMIT),
    )(x1, nw, wp)


def _mix_in_bwd_dx(dx, x1, nw, dpieces, wp, tm, exchange=None):
    t, d = x1.shape

    def body(dx_ref, x_ref, nw_ref, p0, p1, p2, p3, w_ref, o_ref, dnw_ref):
        @pl.when(pl.program_id(0) == 0)
        def _():
            dnw_ref[...] = jnp.zeros_like(dnw_ref)

        dh = jnp.zeros((tm, d), F32)
        for (a, b), p_ref in zip(P_PIECES, (p0, p1, p2, p3)):
            dh = dh + _dot(p_ref[...], w_ref[a:b, :])
        xf = x_ref[...]
        dxr, dnw = _rms_bwd(xf, _rstd(xf), nw_ref[...], dh)
        o_ref[...] = dx_ref[...] + dxr
        dnw_ref[...] += dnw

    row = pl.BlockSpec((tm, d), lambda i: (i, 0))
    return _grid_call(
        body, "mix_in_bwd_dx", t // tm,
        [row, row, _resident((1, d))]
        + [pl.BlockSpec((tm, b - a), lambda i: (i, 0)) for a, b in P_PIECES] + [_resident(wp.shape)],
        [row, pl.BlockSpec((1, d), lambda i: (0, 0))],
        [jax.ShapeDtypeStruct((t, d), F32), jax.ShapeDtypeStruct((1, d), F32)],
        (dx, x1, nw, *dpieces, wp), exchange=exchange)


def _mix_out_fwd(x1, oa, ob, w, tm):
    t, d = x1.shape
    half = oa.shape[1]

    def body(x_ref, oa_ref, ob_ref, w_ref, o_ref):
        o_ref[...] = (x_ref[...] + _dot(oa_ref[...], w_ref[0:half, :]) + _dot(ob_ref[...], w_ref[half:2 * half, :]))

    row = pl.BlockSpec((tm, d), lambda i: (i, 0))
    hrow = pl.BlockSpec((tm, half), lambda i: (i, 0))
    return pl.pallas_call(
        body, name="mix_out_fwd", grid=(t // tm,),
        in_specs=[row, hrow, hrow, _resident(w.shape)],
        out_specs=row, out_shape=jax.ShapeDtypeStruct((t, d), F32),
        compiler_params=_params(("arbitrary",), VMEM_LIMIT),
    )(x1, oa, ob, w)


def _mix_out_bwd(dx2, w, tm):
    t, d = dx2.shape
    half = w.shape[0] // 2

    def body(dx_ref, w_ref, doa_ref, dob_ref, dxb_ref):
        dxb = dx_ref[...].astype(BF16)
        dxb_ref[...] = dxb
        doa_ref[...] = _dot_nt(dxb, w_ref[0:half, :])
        dob_ref[...] = _dot_nt(dxb, w_ref[half:2 * half, :])

    row = pl.BlockSpec((tm, d), lambda i: (i, 0))
    hrow = pl.BlockSpec((tm, half), lambda i: (i, 0))
    return pl.pallas_call(
        body, name="mix_out_bwd", grid=(t // tm,),
        in_specs=[row, _resident(w.shape)],
        out_specs=[hrow, hrow, row],
        out_shape=[jax.ShapeDtypeStruct((t, half), F32), jax.ShapeDtypeStruct((t, half), F32),
                   jax.ShapeDtypeStruct((t, d), BF16)],
        compiler_params=_params(("arbitrary",), VMEM_LIMIT),
    )(dx2, w)


HALO = 8


def _halo_row_specs(tr, cols, nrow8):
    per = tr // HALO
    return [pl.BlockSpec((tr, cols), lambda i: (i, 0)),
            pl.BlockSpec((HALO, cols), lambda i: (jnp.maximum(i * per - 1, 0), 0)),
            pl.BlockSpec((HALO, cols), lambda i: (jnp.minimum((i + 1) * per, nrow8 - 1), 0))]


def _conv_window(xm, xp, xn, first, last, cols):
    prev = jnp.where(first, 0.0, xp[:, cols])
    nxt = jnp.where(last, 0.0, xn[:, cols])
    return jnp.concatenate([prev, xm[:, cols], nxt], axis=0)


def _shift_rows(xw, off):
    n = xw.shape[0]
    sh = (-off) % n
    return xw if sh == 0 else pltpu.roll(xw, sh, 0)


def _conv_pre(xw, cw_ref, cols):
    acc = None
    for j in range(CONV_TAPS):
        term = _shift_rows(xw, j - CONV_TAPS // 2) * cw_ref[j:j + 1, cols]
        acc = term if acc is None else acc + term
    return acc


def _softplus(x):
    u = jnp.exp(-jnp.abs(x))
    w = 1.0 + u
    log1p = jnp.where(w == 1.0, u, jnp.log(w) * u / jnp.where(w == 1.0, 1.0, w - 1.0))
    return jnp.maximum(x, 0.0) + log1p


def _gdn_prep_fwd(qkva, cw, ab, gp, tr):
    t, c = qkva.shape
    nt = t // tr
    ncb = c // LANES

    def body(xm, xp, xn, cw_ref, ab_ref, gp_ref, o_ref, gb_ref):
        i = pl.program_id(0)
        first, last = i == 0, i == nt - 1
        for cb in range(ncb):
            cols = slice(cb * LANES, (cb + 1) * LANES)
            xw = _conv_window(xm, xp, xn, first, last, cols)
            pre = _conv_pre(xw, cw_ref, cols)[HALO:HALO + tr]
            y = pre * _sigmoid(pre)
            if cb < 2 * GDN_HEADS:
                y = y * lax.rsqrt(jnp.sum(y * y, axis=-1, keepdims=True) + EPS)
            if cb < GDN_HEADS:
                y = y * (GDN_DIM ** -0.5)
            o_ref[:, cols] = y
        abv = ab_ref[...]
        lane = lax.broadcasted_iota(jnp.int32, abv.shape, 1)
        g = -jnp.exp(gp_ref[0:1, :]) * _softplus(abv + gp_ref[1:2, :])
        gb_ref[...] = jnp.where(lane < 8, g, jnp.where(lane < 16, _sigmoid(abv), 0.0))

    return pl.pallas_call(
        body, name="gdn_prep_fwd", grid=(nt,),
        in_specs=_halo_row_specs(tr, c, t // HALO)
                 + [_resident(cw.shape), pl.BlockSpec((tr, LANES), lambda i: (i, 0)), _resident(gp.shape)],
        out_specs=[pl.BlockSpec((tr, c), lambda i: (i, 0)), pl.BlockSpec((tr, LANES), lambda i: (i, 0))],
        out_shape=[jax.ShapeDtypeStruct((t, c), F32), jax.ShapeDtypeStruct((t, LANES), F32)],
        compiler_params=_params(("arbitrary",), VMEM_LIMIT),
    )(qkva, qkva, qkva, cw, ab, gp)


def _gdn_prep_bwd(qkva, cw, ab, gp, dy, dgates, tr):
    t, c = qkva.shape
    nt = t // tr
    ncb = c // LANES

    def body(xm, xp, xn, fm, fp, fn, cw_ref, ab_ref, gp_ref, gf_ref, dx_ref, dab_ref, dcw_ref, dgp_ref):
        i = pl.program_id(0)
        first, last = i == 0, i == nt - 1

        @pl.when(first)
        def _():
            dcw_ref[...] = jnp.zeros_like(dcw_ref)
            dgp_ref[...] = jnp.zeros_like(dgp_ref)

        sub8 = lax.broadcasted_iota(jnp.int32, (8, LANES), 0)
        for cb in range(ncb):
            cols = slice(cb * LANES, (cb + 1) * LANES)
            xw = _conv_window(xm, xp, xn, first, last, cols)
            dyw = _conv_window(fm, fp, fn, first, last, cols)
            pre = _conv_pre(xw, cw_ref, cols)
            sg = _sigmoid(pre)
            s = pre * sg
            if cb < 2 * GDN_HEADS:
                scale = (GDN_DIM ** -0.5) if cb < GDN_HEADS else 1.0
                r = lax.rsqrt(jnp.sum(s * s, axis=-1, keepdims=True) + EPS)
                dn = dyw * scale
                ds = r * dn - s * (r * r * r) * jnp.sum(dn * s, axis=-1, keepdims=True)
            else:
                ds = dyw
            dpre = ds * (sg * (1.0 + pre * (1.0 - sg)))
            dx = None
            dcw = jnp.zeros((8, LANES), F32)
            for j in range(CONV_TAPS):
                off = j - CONV_TAPS // 2
                term = _shift_rows(dpre, -off)[HALO:HALO + tr] * cw_ref[j:j + 1, cols]
                dx = term if dx is None else dx + term
                tap = jnp.sum(dpre[HALO:HALO + tr] * _shift_rows(xw, off)[HALO:HALO + tr], axis=0, keepdims=True)
                dcw = dcw + jnp.where(sub8 == j, tap, 0.0)
            dx_ref[:, cols] = dx.astype(BF16)
            dcw_ref[:, cols] += dcw

        abv = ab_ref[...]
        dgb = gf_ref[...]
        lane = lax.broadcasted_iota(jnp.int32, abv.shape, 1)
        nea = -jnp.exp(gp_ref[0:1, :])
        xs = abv + gp_ref[1:2, :]
        g = nea * _softplus(xs)
        beta = _sigmoid(abv)
        da = dgb * nea * _sigmoid(xs)
        dab = jnp.where(lane < 8, da, jnp.where(lane < 16, dgb * beta * (1.0 - beta), 0.0))
        dab_ref[...] = dab.astype(BF16)
        keep = lane[0:1, :] < 8
        dalog = jnp.where(keep, jnp.sum(dgb * g, axis=0, keepdims=True), 0.0)
        ddtb = jnp.where(keep, jnp.sum(da, axis=0, keepdims=True), 0.0)
        dgp_ref[...] += jnp.where(sub8 == 0, dalog, 0.0) + jnp.where(sub8 == 1, ddtb, 0.0)

    lrow = pl.BlockSpec((tr, LANES), lambda i: (i, 0))
    halo = _halo_row_specs(tr, c, t // HALO)
    return pl.pallas_call(
        body, name="gdn_prep_bwd", grid=(nt,),
        in_specs=halo + halo + [_resident(cw.shape), lrow, _resident(gp.shape), lrow],
        out_specs=[pl.BlockSpec((tr, c), lambda i: (i, 0)), lrow,
                   pl.BlockSpec(cw.shape, lambda i: (0, 0)), pl.BlockSpec(gp.shape, lambda i: (0, 0))],
        out_shape=[jax.ShapeDtypeStruct((t, c), BF16), jax.ShapeDtypeStruct((t, LANES), BF16),
                   jax.ShapeDtypeStruct(cw.shape, F32), jax.ShapeDtypeStruct(gp.shape, F32)],
        compiler_params=_params(("arbitrary",), VMEM_LIMIT),
    )(qkva, qkva, qkva, dy, dy, dy, cw, ab, gp, dgates)


def _chunk_masks(lower):
    ii = lax.broadcasted_iota(jnp.int32, (CHUNK, CHUNK), 0)
    jj = lax.broadcasted_iota(jnp.int32, (CHUNK, CHUNK), 1)
    incl = (ii >= jj) if lower else (ii <= jj)
    strict = (ii > jj) if lower else (ii < jj)
    return ii, jj, incl, strict


def _dot3(a, b):
    ah = a.astype(BF16)
    al = (a - ah.astype(F32)).astype(BF16)
    bh = b.astype(BF16)
    bl = (b - bh.astype(F32)).astype(BF16)
    d = lambda u, v: jnp.dot(u, v, preferred_element_type=F32)
    return d(ah, bh) + (d(ah, bl) + d(al, bh))


def _tri_inv_many(lmats, ii, jj):
    m16 = (ii // 16) == (jj // 16)
    m32 = (ii // 32) == (jj // 32)
    eye = jnp.where(ii == jj, 1.0, 0.0)
    l16 = [jnp.where(m16, l, 0.0) for l in lmats]
    p2 = [_dot3(a, a) for a in l16]
    p4 = [_dot3(a, a) for a in p2]
    p8 = [_dot3(a, a) for a in p4]
    xs = [eye - a for a in l16]
    for ps in (p2, p4, p8):
        xs = [x + _dot3(x, p) for x, p in zip(xs, ps)]
    for off in ([jnp.where(m32 & jnp.logical_not(m16), l, 0.0) for l in lmats],
                [jnp.where(m32, 0.0, l) for l in lmats]):
        ys = [_dot3(x, c) for x, c in zip(xs, off)]
        xs = [x - _dot3(y, x) for x, y in zip(xs, ys)]
    return xs


def _col_to_row(col, ii, jj):
    return jnp.sum(jnp.where(ii == jj, col, 0.0), axis=0, keepdims=True)


def _row_to_col(row, ii, jj):
    return jnp.sum(jnp.where(ii == jj, row, 0.0), axis=1, keepdims=True)


def _chain_common(q, k, v, graw_col, graw_row, bcol, masks):
    ii, jj, incl, strict = masks
    inclt = jnp.logical_not(strict)
    gcol = jnp.sum(jnp.where(incl, graw_row, 0.0), axis=1, keepdims=True)
    grow = jnp.sum(jnp.where(inclt, graw_col, 0.0), axis=0, keepdims=True)
    glast = jnp.sum(graw_row, axis=1, keepdims=True)
    decay = jnp.where(incl, jnp.exp(jnp.where(incl, gcol - grow, 0.0)), 0.0)
    kb = k * bcol
    vb = v * bcol
    eg = jnp.exp(gcol)
    ek = jnp.exp(glast - gcol)
    kbg = kb * eg
    amat = _dot_nt(kb, k)
    qk = _dot_nt(q, k)
    return dict(gcol=gcol, glast=glast, decay=decay, kb=kb, vb=vb, eg=eg, ek=ek, kbg=kbg, amat=amat, qk=qk,
                intra=qk * decay, qg=q * eg, kdec=k * ek)


def _gdn_fwd(qkvc, gb, gbt):
    tm, u, w, qg, kd, intra, egl = _gdn_local_fwd(qkvc, gb, gbt)
    o_f, o_b, s_f, s_b, vn_f, vn_b = _gdn_scan_fwd(u, w, qg, kd, intra, egl, qkvc.shape[0])
    return o_f, o_b, dict(tm=tm, w=w, qg=qg, kd=kd, intra=intra, egl=egl, s=(s_f, s_b), vn=(vn_f, vn_b))


N_CHAINS = 2 * GDN_HEADS


LOCAL_CHUNKS = 2


def _load_chains(x_ref, g_ref, gt_ref, cc=0):
    hd = GDN_HEADS * GDN_DIM
    rows = slice(cc * CHUNK, (cc + 1) * CHUNK)
    chains = []
    for d in range(2):
        masks = _chunk_masks(d == 0)
        for h in range(GDN_HEADS):
            ch = d * GDN_HEADS + h
            q = x_ref[rows, h * GDN_DIM:(h + 1) * GDN_DIM]
            k = x_ref[rows, hd + h * GDN_DIM:hd + (h + 1) * GDN_DIM]
            v = x_ref[rows, 2 * hd + h * GDN_DIM:2 * hd + (h + 1) * GDN_DIM]
            bcol = g_ref[rows, 8 + ch:9 + ch]
            cm = _chain_common(q, k, v, g_ref[rows, ch:ch + 1], gt_ref[cc, ch:ch + 1, :], bcol, masks)
            chains.append(dict(cm, q=q, k=k, v=v, bcol=bcol, masks=masks, ch=ch, h=h, cc=cc))
    return chains


def _chain_shape(rows, cols, dtype):
    return lambda nc: jax.ShapeDtypeStruct((nc, N_CHAINS, rows, cols), dtype)


def _gdn_local_fwd(qkvc, gb, gbt):
    t = qkvc.shape[0]
    nc = t // CHUNK
    hd = GDN_HEADS * GDN_DIM

    def body(x_ref, g_ref, gt_ref, t_ref, u_ref, w_ref, qg_ref, kd_ref, in_ref, eg_ref):
        chains = [c for cc in range(LOCAL_CHUNKS) for c in _load_chains(x_ref, g_ref, gt_ref, cc)]
        ii, jj = chains[0]["masks"][0:2]
        tms = _tri_inv_many([jnp.where(c["masks"][3], c["amat"] * c["decay"], 0.0) for c in chains], ii, jj)
        uws = [_dot(tm, jnp.concatenate([c["vb"], c["kbg"]], axis=1)) for tm, c in zip(tms, chains)]
        for c, tm, uw in zip(chains, tms, uws):
            cc, ch = c["cc"], c["ch"]
            t_ref[cc, ch] = tm
            u_ref[cc, ch] = uw[:, :GDN_DIM]
            w_ref[cc, ch] = uw[:, GDN_DIM:].astype(BF16)
            qg_ref[cc, ch] = c["qg"].astype(BF16)
            kd_ref[cc, ch] = c["kdec"].astype(BF16)
            in_ref[cc, ch] = c["intra"].astype(BF16)
            eg_ref[cc, ch:ch + 1, :] = jnp.broadcast_to(jnp.exp(c["glast"]), (1, LANES))

    lc = LOCAL_CHUNKS
    blk = lambda rows, cols: pl.BlockSpec((lc, N_CHAINS, rows, cols), lambda n: (n, 0, 0, 0))
    shapes = [_chain_shape(CHUNK, CHUNK, F32), _chain_shape(CHUNK, GDN_DIM, F32), _chain_shape(CHUNK, GDN_DIM, BF16),
              _chain_shape(CHUNK, GDN_DIM, BF16), _chain_shape(CHUNK, GDN_DIM, BF16), _chain_shape(CHUNK, CHUNK, BF16)]
    return tuple(pl.pallas_call(
        body, name="gdn_local_fwd", grid=(nc // lc,),
        in_specs=[pl.BlockSpec((lc * CHUNK, 3 * hd), lambda n: (n, 0)), pl.BlockSpec((lc * CHUNK, LANES), lambda n: (n, 0)),
                  pl.BlockSpec((lc, 16, CHUNK), lambda n: (n, 0, 0))],
        out_specs=[blk(CHUNK, CHUNK), blk(CHUNK, GDN_DIM), blk(CHUNK, GDN_DIM), blk(CHUNK, GDN_DIM),
                   blk(CHUNK, GDN_DIM), blk(CHUNK, CHUNK), pl.BlockSpec((lc, N_CHAINS, LANES), lambda n: (n, 0, 0))],
        out_shape=[s(nc) for s in shapes] + [jax.ShapeDtypeStruct((nc, N_CHAINS, LANES), F32)],
        compiler_params=_params(("arbitrary",), VMEM_LIMIT),
    )(qkvc, gb, gbt))


SCAN_CHUNKS = 4


def _dir_specs(nc, rev):
    nb = nc // SCAN_CHUNKS

    def spec(d, rows, cols, own=False):
        chunk = (lambda n: n) if (d == 0) != rev else (lambda n: nb - 1 - n)
        blk = 0 if own else d
        if rows is None:
            return pl.BlockSpec((SCAN_CHUNKS, GDN_HEADS if own else N_CHAINS, cols), lambda n: (chunk(n), 0, 0))
        return pl.BlockSpec((SCAN_CHUNKS, GDN_HEADS, rows, cols), lambda n: (chunk(n), blk, 0, 0))

    def rows_spec(d, cols):
        chunk = (lambda n: n) if (d == 0) != rev else (lambda n: nb - 1 - n)
        return pl.BlockSpec((SCAN_CHUNKS * CHUNK, cols), lambda n: (chunk(n), 0))

    def order(d):
        return list(range(SCAN_CHUNKS)) if (d == 0) != rev else list(range(SCAN_CHUNKS - 1, -1, -1))
    return spec, rows_spec, order


def _gdn_scan_fwd(u, w, qg, kd, intra, egl, t):
    nc = t // CHUNK
    hd = GDN_HEADS * GDN_DIM

    def body(*refs):
        ins, outs, state = refs[:12], refs[12:18], refs[18]
        @pl.when(pl.program_id(0) == 0)
        def _():
            state[...] = jnp.zeros_like(state)

        chains = [(d, h) for d in range(2) for h in range(GDN_HEADS)]
        states = [state[ch] for ch in range(N_CHAINS)]
        for step in range(SCAN_CHUNKS):
            at = [order(d)[step] for d in range(2)]
            pick = lambda k, d, h: ins[2 * k + d][at[d], h]
            sbs = [s.astype(BF16) for s in states]
            ws = [_dot(pick(1, d, h), sb) for (d, h), sb in zip(chains, sbs)]
            o1 = [_dot(pick(2, d, h), sb) for (d, h), sb in zip(chains, sbs)]
            vns = [(pick(0, d, h) - wsb).astype(BF16) for (d, h), wsb in zip(chains, ws)]
            o2 = [_dot(pick(4, d, h), vn) for (d, h), vn in zip(chains, vns)]
            kv = [_dot_tn(pick(3, d, h), vn) for (d, h), vn in zip(chains, vns)]
            new_states = []
            for ch, (d, h) in enumerate(chains):
                outs[d][at[d] * CHUNK:(at[d] + 1) * CHUNK, h * GDN_DIM:(h + 1) * GDN_DIM] = o1[ch] + o2[ch]
                outs[2 + d][at[d], h] = states[ch]
                outs[4 + d][at[d], h] = vns[ch]
                new_states.append(states[ch] * ins[10 + d][at[d], ch:ch + 1, :] + kv[ch])
            states = new_states
        for ch in range(N_CHAINS):
            state[ch] = states[ch]

    spec, rows_spec, order = _dir_specs(nc, False)
    pair = lambda rows, cols, own=False: [spec(0, rows, cols, own), spec(1, rows, cols, own)]
    s_shape = jax.ShapeDtypeStruct((nc, GDN_HEADS, GDN_DIM, GDN_DIM), F32)
    vn_shape = jax.ShapeDtypeStruct((nc, GDN_HEADS, CHUNK, GDN_DIM), BF16)
    return pl.pallas_call(
        body, name="gdn_scan_fwd", grid=(nc // SCAN_CHUNKS,),
        in_specs=(pair(CHUNK, GDN_DIM) + pair(CHUNK, GDN_DIM) + pair(CHUNK, GDN_DIM) + pair(CHUNK, GDN_DIM)
                  + pair(CHUNK, CHUNK) + pair(None, LANES)),
        out_specs=([rows_spec(0, hd), rows_spec(1, hd)] + pair(GDN_DIM, GDN_DIM, True)
                   + pair(CHUNK, GDN_DIM, True)),
        out_shape=[jax.ShapeDtypeStruct((t, hd), F32), jax.ShapeDtypeStruct((t, hd), F32),
                   s_shape, s_shape, vn_shape, vn_shape],
        scratch_shapes=[pltpu.VMEM((N_CHAINS, GDN_DIM, GDN_DIM), F32)],
        compiler_params=_params(("arbitrary",), VMEM_LIMIT),
    )(u, u, w, w, qg, qg, kd, kd, intra, intra, egl, egl)


def _gdn_bwd(qkvc, gb, gbt, do, saved, exchange=None):
    scan = _gdn_scan_bwd(do, saved, qkvc.shape[0])
    return _gdn_local_bwd(qkvc, gb, gbt, do, saved, scan, exchange)


def _gdn_scan_bwd(do, saved, t):
    nc = t // CHUNK
    hd = GDN_HEADS * GDN_DIM

    def body(*refs):
        ins, outs, dstate = refs[:16], refs[16:26], refs[26]
        @pl.when(pl.program_id(0) == 0)
        def _():
            dstate[...] = jnp.zeros_like(dstate)

        chains = [(d, h) for d in range(2) for h in range(GDN_HEADS)]
        dss = [dstate[ch] for ch in range(N_CHAINS)]
        for step in range(SCAN_CHUNKS):
            at = [order(d)[step] for d in range(2)]
            pick = lambda k, d, h: ins[2 * k + d][at[d], h]
            dsbs = [ds.astype(BF16) for ds in dss]
            ss = [pick(1, d, h) for d, h in chains]
            sbs = [s.astype(BF16) for s in ss]
            dos = [ins[d][at[d] * CHUNK:(at[d] + 1) * CHUNK, h * GDN_DIM:(h + 1) * GDN_DIM].astype(BF16)
                   for d, h in chains]
            dv1 = [_dot_tn(pick(5, d, h), dov) for (d, h), dov in zip(chains, dos)]
            dv2 = [_dot(pick(4, d, h), dsb) for (d, h), dsb in zip(chains, dsbs)]
            ds1 = [_dot_tn(pick(3, d, h), dov) for (d, h), dov in zip(chains, dos)]
            dkds = [_dot_nt(pick(6, d, h), dsb) for (d, h), dsb in zip(chains, dsbs)]
            dqgs = [_dot_nt(dov, sb) for dov, sb in zip(dos, sbs)]
            dvns = [(a + b).astype(BF16) for a, b in zip(dv1, dv2)]
            ds2 = [_dot_tn(pick(2, d, h), dvn) for (d, h), dvn in zip(chains, dvns)]
            dws = [_dot_nt(dvn, sb) for dvn, sb in zip(dvns, sbs)]
            new_dss = []
            for ch, (d, h) in enumerate(chains):
                egl = ins[14 + d][at[d], ch:ch + 1, :]
                outs[d][at[d], h] = dvns[ch]
                outs[2 + d][at[d], h] = (-dws[ch]).astype(BF16)
                outs[4 + d][at[d], h] = dqgs[ch]
                outs[6 + d][at[d], h] = dkds[ch]
                outs[8 + d][at[d], h:h + 1, :] = egl * jnp.sum(jnp.sum(ss[ch] * dss[ch], axis=1, keepdims=True),
                                                               axis=0, keepdims=True)
                new_dss.append(ds1[ch] + egl * dss[ch] - ds2[ch])
            dss = new_dss
        for ch in range(N_CHAINS):
            dstate[ch] = dss[ch]

    spec, rows_spec, order = _dir_specs(nc, True)
    pair = lambda rows, cols, own=False: [spec(0, rows, cols, own), spec(1, rows, cols, own)]
    s_f, s_b = saved["s"]
    vn_f, vn_b = saved["vn"]
    w, qg, kd, intra, egl = saved["w"], saved["qg"], saved["kd"], saved["intra"], saved["egl"]
    own = lambda rows, cols, dtype: jax.ShapeDtypeStruct((nc, GDN_HEADS, rows, cols), dtype)
    row_shape = jax.ShapeDtypeStruct((nc, GDN_HEADS, LANES), F32)
    return pl.pallas_call(
        body, name="gdn_scan_bwd", grid=(nc // SCAN_CHUNKS,),
        in_specs=([rows_spec(0, hd), rows_spec(1, hd)] + pair(GDN_DIM, GDN_DIM, True) + pair(CHUNK, GDN_DIM)
                  + pair(CHUNK, GDN_DIM) + pair(CHUNK, GDN_DIM) + pair(CHUNK, CHUNK) + pair(CHUNK, GDN_DIM, True)
                  + pair(None, LANES)),
        out_specs=(pair(CHUNK, GDN_DIM, True) + pair(CHUNK, GDN_DIM, True) + pair(CHUNK, GDN_DIM, True)
                   + pair(CHUNK, GDN_DIM, True) + pair(None, LANES, True)),
        out_shape=[own(CHUNK, GDN_DIM, BF16)] * 4 + [own(CHUNK, GDN_DIM, F32)] * 4 + [row_shape] * 2,
        scratch_shapes=[pltpu.VMEM((N_CHAINS, GDN_DIM, GDN_DIM), F32)],
        compiler_params=_params(("arbitrary",), VMEM_LIMIT),
    )(do, do, s_f, s_b, w, w, qg, qg, kd, kd, intra, intra, vn_f, vn_b, egl, egl)


def _dot3_nt(a, b):
    ah = a.astype(BF16)
    al = (a - ah.astype(F32)).astype(BF16)
    bh = b.astype(BF16)
    bl = (b - bh.astype(F32)).astype(BF16)
    return _dot_nt(ah, bh) + (_dot_nt(ah, bl) + _dot_nt(al, bh))


def _dot3_tn(a, b):
    ah = a.astype(BF16)
    al = (a - ah.astype(F32)).astype(BF16)
    bh = b.astype(BF16)
    bl = (b - bh.astype(F32)).astype(BF16)
    return _dot_tn(ah, bh) + (_dot_tn(ah, bl) + _dot_tn(al, bh))


def _gdn_local_bwd(qkvc, gb, gbt, do, saved, scan, exchange=None):
    t = qkvc.shape[0]
    nc = t // CHUNK
    hd = GDN_HEADS * GDN_DIM

    def body(*refs):
        x_ref, g_ref, gt_ref, do_ref, t_ref = refs[:5]
        per_dir = refs[5:17]
        dx_ref, dg_ref = refs[17:]
        chains = [c for cc in range(LOCAL_CHUNKS) for c in _load_chains(x_ref, g_ref, gt_ref, cc)]
        lane = lax.broadcasted_iota(jnp.int32, (CHUNK, LANES), 1)
        dgates = [jnp.zeros((CHUNK, LANES), F32) for _ in range(LOCAL_CHUNKS)]
        for c in chains:
            d = c["ch"] // GDN_HEADS
            vn_ref, dvn_ref, dw_ref, dqg_ref, dkd_ref, dgl_ref = per_dir[d::2]
            h, cc = c["h"], c["cc"]
            rows = slice(cc * CHUNK, (cc + 1) * CHUNK)
            c.update(tm=t_ref[cc, c["ch"]], dov=do_ref[rows, h * GDN_DIM:(h + 1) * GDN_DIM], vnew=vn_ref[cc, h],
                     dvnew=dvn_ref[cc, h], dw=dw_ref[cc, h], dqg=dqg_ref[cc, h], dkdec=dkd_ref[cc, h],
                     dglast=dgl_ref[cc, h:h + 1, 0:1])
        dintras = [_dot_nt(c["dov"], c["vnew"]) for c in chains]
        dts = [_dot_nt(c["dvnew"], c["vb"]) + _dot_nt(c["dw"], c["kbg"]) for c in chains]
        dvbs = [_dot_tn(c["tm"], c["dvnew"]) for c in chains]
        dkbgs = [_dot_tn(c["tm"], c["dw"]) for c in chains]
        tdts = [_dot3_nt(dt, c["tm"]) for dt, c in zip(dts, chains)]
        dls = [jnp.where(c["masks"][3], -_dot3_tn(c["tm"], tdt), 0.0) for tdt, c in zip(tdts, chains)]
        das = [dl * c["decay"] for dl, c in zip(dls, chains)]
        dqks = [jnp.where(c["masks"][2], di, 0.0) * c["decay"] for di, c in zip(dintras, chains)]
        dkb1 = [_dot(da, c["k"]) for da, c in zip(das, chains)]
        dk1 = [_dot_tn(da, c["kb"]) for da, c in zip(das, chains)]
        dk2 = [_dot_tn(dqk, c["q"]) for dqk, c in zip(dqks, chains)]
        dq1 = [_dot(dqk, c["k"]) for dqk, c in zip(dqks, chains)]
        grads, mms, p_gs, p_betas, p_kds = [], [], [], [], []
        for n, c in enumerate(chains):
            incl = c["masks"][2]
            dkb = dkb1[n] + dkbgs[n] * c["eg"]
            kd = c["dkdec"] * c["kdec"]
            mms.append((dls[n] * c["amat"] + jnp.where(incl, dintras[n], 0.0) * c["qk"]) * c["decay"])
            p_gs.append(c["dqg"] * c["qg"] - kd + dkbgs[n] * c["kbg"])
            p_betas.append(dkb * c["k"] + dvbs[n] * c["v"])
            p_kds.append(kd)
            grads.append((dq1[n] + c["dqg"] * c["eg"],
                          dk1[n] + dk2[n] + c["dkdec"] * c["ek"] + dkb * c["bcol"],
                          dvbs[n] * c["bcol"]))
        row_sums = [jnp.sum(mm, axis=1, keepdims=True) for mm in mms]
        col_sums = [jnp.sum(mm, axis=0, keepdims=True) for mm in mms]
        g_sums = [jnp.sum(pg, axis=1, keepdims=True) for pg in p_gs]
        dbetas = [jnp.sum(pb, axis=1, keepdims=True) for pb in p_betas]
        kd_tots = [jnp.sum(jnp.sum(pk, axis=1, keepdims=True), axis=0, keepdims=True) for pk in p_kds]
        dgcs = [rs - _row_to_col(cs, *c["masks"][0:2]) + gs for rs, cs, gs, c in zip(row_sums, col_sums, g_sums, chains)]
        dgrs = [_col_to_row(dgc, *c["masks"][0:2]) for dgc, c in zip(dgcs, chains)]
        draws = [jnp.sum(jnp.where(jnp.logical_not(c["masks"][3]), dgr, 0.0), axis=1, keepdims=True) + c["dglast"] + kt
                 for dgr, kt, c in zip(dgrs, kd_tots, chains)]
        for c, draw, dbeta in zip(chains, draws, dbetas):
            ch = c["ch"]
            dgates[c["cc"]] = dgates[c["cc"]] + jnp.where(lane == ch, draw, 0.0) + jnp.where(lane == 8 + ch, dbeta, 0.0)
        for cc in range(LOCAL_CHUNKS):
            rows = slice(cc * CHUNK, (cc + 1) * CHUNK)
            for h in range(GDN_HEADS):
                for part in range(3):
                    cols = slice(part * hd + h * GDN_DIM, part * hd + (h + 1) * GDN_DIM)
                    dx_ref[rows, cols] = grads[cc * N_CHAINS + h][part] + grads[cc * N_CHAINS + GDN_HEADS + h][part]
            dg_ref[rows, :] = dgates[cc]

    lc = LOCAL_CHUNKS
    all8 = lambda rows, cols: pl.BlockSpec((lc, N_CHAINS, rows, cols), lambda n: (n, 0, 0, 0))
    own4 = lambda rows, cols: pl.BlockSpec((lc, GDN_HEADS, rows, cols), lambda n: (n, 0, 0, 0))
    row4 = pl.BlockSpec((lc, GDN_HEADS, LANES), lambda n: (n, 0, 0))
    vn_f, vn_b = saved["vn"]
    dvn_f, dvn_b, dw_f, dw_b, dqg_f, dqg_b, dkd_f, dkd_b, dgl_f, dgl_b = scan
    return _grid_call(
        body, "gdn_local_bwd", nc // lc,
        [pl.BlockSpec((lc * CHUNK, 3 * hd), lambda n: (n, 0)), pl.BlockSpec((lc * CHUNK, LANES), lambda n: (n, 0)),
         pl.BlockSpec((lc, 16, CHUNK), lambda n: (n, 0, 0)), pl.BlockSpec((lc * CHUNK, hd), lambda n: (n, 0)),
         all8(CHUNK, CHUNK)] + [own4(CHUNK, GDN_DIM)] * 10 + [row4, row4],
        [pl.BlockSpec((lc * CHUNK, 3 * hd), lambda n: (n, 0)), pl.BlockSpec((lc * CHUNK, LANES), lambda n: (n, 0))],
        [jax.ShapeDtypeStruct((t, 3 * hd), F32), jax.ShapeDtypeStruct((t, LANES), F32)],
        (qkvc, gb, gbt, do, saved["tm"], vn_f, vn_b, dvn_f, dvn_b, dw_f, dw_b, dqg_f, dqg_b, dkd_f, dkd_b, dgl_f, dgl_b),
        exchange=exchange)


def _gdn_post_fwd(of, ob, z, gw, tm):
    t, hd = of.shape

    def body(of_ref, ob_ref, z_ref, w_ref, o_ref):
        for h in range(GDN_HEADS):
            cols = slice(h * GDN_DIM, (h + 1) * GDN_DIM)
            o = of_ref[:, cols] + ob_ref[:, cols]
            zv = z_ref[:, cols]
            o_ref[:, cols] = (o * _rstd(o) * w_ref[...] * (zv * _sigmoid(zv))).astype(BF16)

    row = pl.BlockSpec((tm, hd), lambda i: (i, 0))
    return pl.pallas_call(
        body, name="gdn_post_fwd", grid=(t // tm,),
        in_specs=[row, row, row, _resident((1, GDN_DIM))],
        out_specs=row, out_shape=jax.ShapeDtypeStruct((t, hd), BF16),
        compiler_params=_params(("arbitrary",), VMEM_LIMIT),
    )(of, ob, z, gw)


def _gdn_post_bwd(doa, of, ob, z, gw, tm):
    t, hd = of.shape

    def body(d_ref, of_ref, ob_ref, z_ref, w_ref, do_ref, dz_ref, dw_ref):
        @pl.when(pl.program_id(0) == 0)
        def _():
            dw_ref[...] = jnp.zeros_like(dw_ref)

        dw = jnp.zeros((1, GDN_DIM), F32)
        for h in range(GDN_HEADS):
            cols = slice(h * GDN_DIM, (h + 1) * GDN_DIM)
            o = of_ref[:, cols] + ob_ref[:, cols]
            zv = z_ref[:, cols]
            dv = d_ref[:, cols]
            r = _rstd(o)
            sg = _sigmoid(zv)
            on = o * r * w_ref[...]
            dz_ref[:, cols] = (dv * on * (sg * (1.0 + zv * (1.0 - sg)))).astype(BF16)
            dxr, dwh = _rms_bwd(o, r, w_ref[...], dv * (zv * sg))
            do_ref[:, cols] = dxr
            dw = dw + dwh
        dw_ref[...] += dw

    row = pl.BlockSpec((tm, hd), lambda i: (i, 0))
    return pl.pallas_call(
        body, name="gdn_post_bwd", grid=(t // tm,),
        in_specs=[row, row, row, row, _resident((1, GDN_DIM))],
        out_specs=[row, row, pl.BlockSpec((1, GDN_DIM), lambda i: (0, 0))],
        out_shape=[jax.ShapeDtypeStruct((t, hd), F32), jax.ShapeDtypeStruct((t, hd), BF16),
                   jax.ShapeDtypeStruct((1, GDN_DIM), F32)],
        compiler_params=_params(("arbitrary",), VMEM_LIMIT),
    )(doa, of, ob, z, gw)


SWA_W = SWA_HEADS * SWA_DIM
QBLK = 128
KWIN = QBLK + 2 * RADIUS
WIN_OFFSETS = (0, RADIUS, 2 * RADIUS)


def _t5_bucket(rel):
    nb = REL_BUCKETS // 2
    bucket = (rel > 0).astype(np.int32) * nb
    n = np.abs(rel)
    max_exact = nb // 2
    large = max_exact + (np.log(np.maximum(n, 1) / max_exact)
                         / math.log(REL_MAX_DISTANCE / max_exact) * (nb - max_exact)).astype(np.int32)
    large = np.minimum(large, nb - 1)
    return (bucket + np.where(n < max_exact, n, large)).astype(np.int32)


def _band_tables(dilation):
    a = np.arange(QBLK)
    b = np.arange(KWIN)
    rel = np.stack([b[None, :] - w0 - a[:, None] for w0 in WIN_OFFSETS])
    return np.where(np.abs(rel) <= RADIUS, _t5_bucket(rel * dilation), -1).astype(np.int32)


BAND_CELLS = len(WIN_OFFSETS) * QBLK * KWIN


def _band_index():
    return jnp.asarray(np.concatenate([_band_tables(d).reshape(-1) for _, d in PATTERNS])[None, :])


def _onehot(idx, dtype):
    return (lax.broadcasted_iota(jnp.int32, (REL_BUCKETS, idx.shape[1]), 0) == idx).astype(dtype)


def _bias_tables(rel_bias, idx, tk):
    n = idx.shape[1]

    def body(rb_ref, i_ref, o_ref):
        iv = i_ref[...]
        o_ref[...] = jnp.where(iv < 0, NEG_BIG, _dot_hi(rb_ref[...], _onehot(iv, F32)))

    return pl.pallas_call(
        body, name="bias_tables", grid=(n // tk,),
        in_specs=[_resident((SWA_HEADS, REL_BUCKETS)), pl.BlockSpec((1, tk), lambda k: (0, k))],
        out_specs=pl.BlockSpec((SWA_HEADS, tk), lambda k: (0, k)),
        out_shape=jax.ShapeDtypeStruct((SWA_HEADS, n), F32),
        compiler_params=_params(("arbitrary",), VMEM_LIMIT),
    )(rel_bias.T, idx)


def _head_mean(x2, bd_ref):
    return _dot_hi(x2, bd_ref[...])


VIEW_DILATIONS = tuple(d for _, d in PATTERNS if d > 1)


def _view_spec(tm, d):
    return pl.BlockSpec((tm // d, d * SWA_W), lambda i: (i, 0))


def _view_shape(t, d, dtype):
    return jax.ShapeDtypeStruct((t // d, d * SWA_W), dtype)


N_GROUPS = SWA_W // LANES


def _to_view(src_ref, idx, dst_ref, d, rows):
    for r in range(d):
        for g in range(N_GROUPS):
            cols = slice(r * SWA_W + g * LANES, r * SWA_W + (g + 1) * LANES)
            dst_ref[:, cols] = src_ref[idx, g, pl.ds(r, rows // d, stride=d), :].astype(dst_ref.dtype)


def _from_view(src_ref, dst_ref, idx, d, rows):
    for r in range(d):
        for g in range(N_GROUPS):
            cols = slice(r * SWA_W + g * LANES, r * SWA_W + (g + 1) * LANES)
            dst_ref[idx, g, pl.ds(r, rows // d, stride=d), :] = src_ref[:, cols]


def _swa_prep_fwd(qkvb, qw, kw, bd, tm):
    t = qkvb.shape[0]

    def body(x_ref, qw_ref, kw_ref, bd_ref, *rest):
        outs, sc = rest[:-1], rest[-1]
        for gidx in range(N_GROUPS):
            cols = slice(gidx * LANES, (gidx + 1) * LANES)
            xq = x_ref[:, cols]
            sc[0, gidx] = xq * lax.rsqrt(_head_mean(xq * xq, bd_ref) + EPS) * qw_ref[:, cols] * (SWA_DIM ** -0.5)
            xk = x_ref[:, SWA_W + gidx * LANES:SWA_W + (gidx + 1) * LANES]
            sc[1, gidx] = xk * lax.rsqrt(_head_mean(xk * xk, bd_ref) + EPS) * kw_ref[:, cols]
            sc[2, gidx] = x_ref[:, 2 * SWA_W + gidx * LANES:2 * SWA_W + (gidx + 1) * LANES]
            for i in range(3):
                outs[i][:, cols] = sc[i, gidx].astype(BF16)
        for i in range(3):
            for n, d in enumerate(VIEW_DILATIONS):
                _to_view(sc, i, outs[3 * (n + 1) + i], d, tm)

    return pl.pallas_call(
        body, name="swa_prep_fwd", grid=(t // tm,),
        in_specs=[pl.BlockSpec((tm, 3 * SWA_W), lambda i: (i, 0)), _resident((1, SWA_W)), _resident((1, SWA_W)),
                  _resident((LANES, LANES))],
        out_specs=[_view_spec(tm, d) for d in (1,) + VIEW_DILATIONS for _ in range(3)],
        out_shape=[_view_shape(t, d, BF16) for d in (1,) + VIEW_DILATIONS for _ in range(3)],
        scratch_shapes=[pltpu.VMEM((3, N_GROUPS, tm, LANES), F32)],
        compiler_params=_params(("arbitrary",), VMEM_LIMIT),
    )(qkvb, qw, kw, bd)


def _swa_prep_bwd(qkvb, qw, kw, bd, grads, tm):
    t = qkvb.shape[0]

    def body(x_ref, qw_ref, kw_ref, bd_ref, *rest):
        parts, (dx_ref, dqw_ref, dkw_ref, sc) = rest[:9], rest[9:]
        @pl.when(pl.program_id(0) == 0)
        def _():
            dqw_ref[...] = jnp.zeros_like(dqw_ref)
            dkw_ref[...] = jnp.zeros_like(dkw_ref)

        for i in range(3):
            for n, d in enumerate(VIEW_DILATIONS):
                _from_view(parts[3 * (n + 1) + i], sc, 2 * i + n, d, tm)
        for gidx in range(N_GROUPS):
            cols = slice(gidx * LANES, (gidx + 1) * LANES)
            for i, base, w_ref, dw_ref, scale in ((0, 0, qw_ref, dqw_ref, SWA_DIM ** -0.5),
                                                  (1, SWA_W, kw_ref, dkw_ref, 1.0)):
                xv = x_ref[:, base + gidx * LANES:base + (gidx + 1) * LANES]
                dy = (parts[i][:, cols] + sc[2 * i, gidx] + sc[2 * i + 1, gidx]) * scale
                r = lax.rsqrt(_head_mean(xv * xv, bd_ref) + EPS)
                xhat = xv * r
                dxh = dy * w_ref[:, cols]
                dx = r * (dxh - xhat * _head_mean(dxh * xhat, bd_ref))
                dx_ref[:, base + gidx * LANES:base + (gidx + 1) * LANES] = dx.astype(BF16)
                dw_ref[:, cols] += jnp.sum(dy * xhat, axis=0, keepdims=True)
            dx_ref[:, 2 * SWA_W + gidx * LANES:2 * SWA_W + (gidx + 1) * LANES] = (
                parts[2][:, cols] + sc[4, gidx] + sc[5, gidx]).astype(BF16)

    wrow = pl.BlockSpec((1, SWA_W), lambda i: (0, 0))
    return pl.pallas_call(
        body, name="swa_prep_bwd", grid=(t // tm,),
        in_specs=[pl.BlockSpec((tm, 3 * SWA_W), lambda i: (i, 0)), _resident((1, SWA_W)), _resident((1, SWA_W)),
                  _resident((LANES, LANES))] + [_view_spec(tm, d) for d in (1,) + VIEW_DILATIONS for _ in range(3)],
        out_specs=[pl.BlockSpec((tm, 3 * SWA_W), lambda i: (i, 0)), wrow, wrow],
        out_shape=[jax.ShapeDtypeStruct((t, 3 * SWA_W), BF16), jax.ShapeDtypeStruct((1, SWA_W), F32),
                   jax.ShapeDtypeStruct((1, SWA_W), F32)],
        scratch_shapes=[pltpu.VMEM((6, N_GROUPS, tm, LANES), F32)],
        compiler_params=_params(("arbitrary",), VMEM_LIMIT),
    )(qkvb, qw, kw, bd, *grads)


def _aligned(v, m):
    return v if isinstance(v, int) else pl.multiple_of(v, m)


BAND_GROUP = 2


def _band_loop(nsub, length, step):
    step([(0, 0)], 0)
    if nsub > 2:
        assert (nsub - 2) % BAND_GROUP == 0

        def inner(i, carry):
            s0 = 1 + i * BAND_GROUP
            step([(s0 + e, pl.multiple_of((s0 + e) * QBLK - RADIUS, RADIUS)) for e in range(BAND_GROUP)], 1)
            return carry
        lax.fori_loop(0, (nsub - 2) // BAND_GROUP, inner, 0)
    step([(nsub - 1, length - KWIN)], 2)


def _head_select(lane, a0, a1):
    return jnp.where(lane < SWA_DIM, a0, a1)


def _swa_fwd(qv, kv, vv, bias, dilation, name):
    length = qv.shape[0]
    nsub = length // QBLK
    assert nsub >= 2 and length % QBLK == 0

    def body(q_ref, k_ref, v_ref, b_ref, o_ref, l_ref):
        lane = lax.broadcasted_iota(jnp.int32, (QBLK, LANES), 1)

        def step(blocks, var):
            items = []
            for s, ws in blocks:
                rows = pl.ds(_aligned(s * QBLK, QBLK), QBLK)
                q, kk, vw = q_ref[rows, :], k_ref[pl.ds(ws, KWIN), :], v_ref[pl.ds(ws, KWIN), :]
                for hh in range(2):
                    items.append((hh, jnp.where((lane < SWA_DIM) == (hh == 0), q, jnp.zeros_like(q)), kk, vw))
            lgs = [_dot_nt(qh, kk) + b_ref[hh, var] for hh, qh, kk, _ in items]
            ms = [jnp.max(lg, axis=-1, keepdims=True) for lg in lgs]
            ps = [jnp.exp(lg - m) for lg, m in zip(lgs, ms)]
            dens = [jnp.sum(p, axis=-1, keepdims=True) for p in ps]
            pvs = [_dot(p, it[3]) for p, it in zip(ps, items)]
            for n, (s, _) in enumerate(blocks):
                rows = pl.ds(_aligned(s * QBLK, QBLK), QBLK)
                o0, o1 = (pvs[2 * n + hh] / dens[2 * n + hh] for hh in range(2))
                l0, l1 = (ms[2 * n + hh] + jnp.log(dens[2 * n + hh]) for hh in range(2))
                o_ref[rows, :] = _head_select(lane, o0, o1)
                l_ref[rows, :] = _head_select(lane, l0, l1)

        _band_loop(nsub, length, step)

    blk = pl.BlockSpec((length, LANES), lambda hp, r: (0, r * (SWA_W // LANES) + hp))
    shp = jax.ShapeDtypeStruct(qv.shape, F32)
    return pl.pallas_call(
        body, name=name, grid=(SWA_W // LANES, dilation),
        in_specs=[blk, blk, blk, pl.BlockSpec((2, 3, QBLK, KWIN), lambda hp, r: (hp, 0, 0, 0))],
        out_specs=[blk, blk], out_shape=[shp, shp],
        compiler_params=_params(("arbitrary", "arbitrary"), VMEM_LIMIT),
    )(qv, kv, vv, bias)


def _swa_combine(os_, ls_, tm):
    t = os_[0].shape[0]

    def body(o0, o1, o2, l0, l1, l2, o_ref, ob_ref, la_ref, lb_ref, lc_ref, sc):
        for n, d in enumerate(VIEW_DILATIONS):
            _from_view((o1, o2)[n], sc, n, d, tm)
            _from_view((l1, l2)[n], sc, 2 + n, d, tm)
        for g in range(N_GROUPS):
            cols = slice(g * LANES, (g + 1) * LANES)
            la, lb, lc = l0[:, cols], sc[2, g], sc[3, g]
            m = jnp.maximum(jnp.maximum(la, lb), lc)
            tot = m + jnp.log(jnp.exp(la - m) + jnp.exp(lb - m) + jnp.exp(lc - m))
            o = jnp.exp(la - tot) * o0[:, cols] + jnp.exp(lb - tot) * sc[0, g] + jnp.exp(lc - tot) * sc[1, g]
            o_ref[:, cols] = o
            ob_ref[:, cols] = o.astype(BF16)
            la_ref[:, cols] = tot
            sc[4, g] = tot
        for n, d in enumerate(VIEW_DILATIONS):
            _to_view(sc, 4, (lb_ref, lc_ref)[n], d, tm)

    specs = [_view_spec(tm, d) for d in (1,) + VIEW_DILATIONS]
    return pl.pallas_call(
        body, name="swa_combine", grid=(t // tm,), in_specs=specs + specs, out_specs=[specs[0], specs[0]] + specs,
        out_shape=[jax.ShapeDtypeStruct((t, SWA_W), F32), jax.ShapeDtypeStruct((t, SWA_W), BF16)]
                  + [_view_shape(t, d, F32) for d in (1,) + VIEW_DILATIONS],
        scratch_shapes=[pltpu.VMEM((5, N_GROUPS, tm, LANES), F32)],
        compiler_params=_params(("arbitrary",), VMEM_LIMIT),
    )(*os_, *ls_)


def _swa_bwd_prep(do, o, bd, tm):
    t = do.shape[0]

    def body(d_ref, o_ref, bd_ref, dd1, dd4, dd16, db1, db4, db16, sc):
        for gidx in range(N_GROUPS):
            cols = slice(gidx * LANES, (gidx + 1) * LANES)
            dv = d_ref[:, cols]
            dd = _head_mean(dv * o_ref[:, cols], bd_ref) * float(SWA_DIM)
            sc[0, gidx] = dd
            sc[1, gidx] = dv
            dd1[:, cols] = dd
            db1[:, cols] = dv.astype(BF16)
        for n, d in enumerate(VIEW_DILATIONS):
            _to_view(sc, 0, (dd4, dd16)[n], d, tm)
            _to_view(sc, 1, (db4, db16)[n], d, tm)

    specs = [_view_spec(tm, d) for d in (1,) + VIEW_DILATIONS]
    return pl.pallas_call(
        body, name="swa_bwd_prep", grid=(t // tm,), in_specs=[specs[0], specs[0], _resident((LANES, LANES))],
        out_specs=specs + specs,
        out_shape=[_view_shape(t, d, F32) for d in (1,) + VIEW_DILATIONS]
                  + [_view_shape(t, d, BF16) for d in (1,) + VIEW_DILATIONS],
        scratch_shapes=[pltpu.VMEM((2, N_GROUPS, tm, LANES), F32)],
        compiler_params=_params(("arbitrary",), VMEM_LIMIT),
    )(do, o, bd)


def _swa_bwd(qv, kv, vv, dov, lv, ddv, bias_a, dilation, name):
    length = qv.shape[0]
    nsub = length // QBLK
    single = pl.Buffered(1) if dilation == 1 else None

    def body(q_ref, k_ref, v_ref, do_ref, l_ref, dd_ref, ba_ref, dq_ref, dk_ref, dv_ref, db_ref):
        @pl.when(pl.program_id(1) == 0)
        def _():
            db_ref[...] = jnp.zeros_like(db_ref)

        lane = lax.broadcasted_iota(jnp.int32, (QBLK, LANES), 1)
        lanew = lax.broadcasted_iota(jnp.int32, (KWIN, LANES), 1)

        def step(blocks, var):
            items = []
            for s, ws in blocks:
                rows = pl.ds(_aligned(s * QBLK, QBLK), QBLK)
                win = pl.ds(ws, KWIN)
                q, dov_ = q_ref[rows, :], do_ref[rows, :]
                kk, vw = k_ref[win, :], v_ref[win, :]
                lse, dd = l_ref[rows, :], dd_ref[rows, :]
                for hh in range(2):
                    mine = (lane < SWA_DIM) == (hh == 0)
                    col = slice(hh * SWA_DIM, hh * SWA_DIM + 1)
                    items.append((hh, jnp.where(mine, q, jnp.zeros_like(q)), jnp.where(mine, dov_, jnp.zeros_like(dov_)),
                                  kk, vw, lse[:, col], dd[:, col], q, dov_))
            lgs = [_dot_nt(it[1], it[3]) + ba_ref[it[0], var] for it in items]
            dps = [_dot_nt(it[2], it[4]) for it in items]
            ps = [jnp.exp(lg - it[5]) for lg, it in zip(lgs, items)]
            dss = [p * (dp - it[6]) for p, dp, it in zip(ps, dps, items)]
            dqs = [_dot(ds, it[3]) for ds, it in zip(dss, items)]
            dks = [_dot_tn(ds, it[7]) for ds, it in zip(dss, items)]
            dvs = [_dot_tn(p, it[8]) for p, it in zip(ps, items)]
            for n, (s, ws) in enumerate(blocks):
                rows = pl.ds(_aligned(s * QBLK, QBLK), QBLK)
                win = pl.ds(ws, KWIN)
                dq_ref[rows, :] = _head_select(lane, dqs[2 * n], dqs[2 * n + 1])
                dk_ref[win, :] += _head_select(lanew, dks[2 * n], dks[2 * n + 1])
                dv_ref[win, :] += _head_select(lanew, dvs[2 * n], dvs[2 * n + 1])
            for hh in range(2):
                tot = dss[hh]
                for n in range(1, len(blocks)):
                    tot = tot + dss[2 * n + hh]
                db_ref[hh, var] += tot

        dk_ref[...] = jnp.zeros_like(dk_ref)
        dv_ref[...] = jnp.zeros_like(dv_ref)
        _band_loop(nsub, length, step)

    imap = lambda hp, r: (0, r * (SWA_W // LANES) + hp)
    blk_in = pl.BlockSpec((length, LANES), imap, pipeline_mode=single)
    blk_out = pl.BlockSpec((length, LANES), imap)
    shp = jax.ShapeDtypeStruct(qv.shape, F32)
    return pl.pallas_call(
        body, name=name, grid=(SWA_W // LANES, dilation),
        in_specs=[blk_in] * 6 + [pl.BlockSpec((2, 3, QBLK, KWIN), lambda hp, r: (hp, 0, 0, 0))],
        out_specs=[blk_out, blk_out, blk_out, pl.BlockSpec((2, 3, QBLK, KWIN), lambda hp, r: (hp, 0, 0, 0))],
        out_shape=[shp, shp, shp, jax.ShapeDtypeStruct((SWA_HEADS, 3, QBLK, KWIN), F32)],
        compiler_params=_params(("arbitrary", "arbitrary"), VMEM_LIMIT),
    )(qv, kv, vv, dov, lv, ddv, bias_a)


def _bias_grad(ds2, idx, tk):
    n = ds2.shape[1]
    nk = n // tk

    def body(a_ref, i_ref, o_ref):
        @pl.when(pl.program_id(0) == 0)
        def _():
            o_ref[...] = jnp.zeros_like(o_ref)

        oh = _onehot(i_ref[...], BF16)
        rest = a_ref[...]
        acc = jnp.zeros((SWA_HEADS, REL_BUCKETS), F32)
        for _ in range(3):
            piece = rest.astype(BF16)
            acc = acc + _dot_nt(piece, oh)
            rest = rest - piece.astype(F32)
        o_ref[...] += acc

    return pl.pallas_call(
        body, name="bias_grad", grid=(nk,),
        in_specs=[pl.BlockSpec((SWA_HEADS, tk), lambda k: (0, k)), pl.BlockSpec((1, tk), lambda k: (0, k))],
        out_specs=pl.BlockSpec((SWA_HEADS, REL_BUCKETS), lambda k: (0, 0)),
        out_shape=jax.ShapeDtypeStruct((SWA_HEADS, REL_BUCKETS), F32),
        compiler_params=_params(("arbitrary",), VMEM_LIMIT),
    )(ds2, idx)


def _swa_branch_fwd(qkvb, qw_t, kw_t, rel_bias, bd, tm):
    qkv = _swa_prep_fwd(qkvb, qw_t, kw_t, bd, tm)
    tables = _bias_tables(rel_bias, _band_index(), 8192)
    os_, ls_, tabs = [], [], []
    for n, (_, d) in enumerate(PATTERNS):
        bias = tables[:, n * BAND_CELLS:(n + 1) * BAND_CELLS].reshape(SWA_HEADS, len(WIN_OFFSETS), QBLK, KWIN)
        o_p, l_p = _swa_fwd(*qkv[3 * n:3 * n + 3], bias, d, f"swa_fwd_d{d}")
        os_.append(o_p)
        ls_.append(l_p)
        tabs.append(bias)
    o, o16, *lses = _swa_combine(os_, ls_, tm)
    return o, o16, (qkv, lses, tabs)


def _swa_branch_bwd(do, o, saved, qkvb, qw_t, kw_t, bd, tm):
    qkv, lses, tabs = saved
    prep = _swa_bwd_prep(do, o, bd, tm)
    grads, dss = [], []
    for n, ((_, d), bias) in enumerate(zip(PATTERNS, tabs)):
        dq, dk, dv, ds = _swa_bwd(*qkv[3 * n:3 * n + 3], prep[3 + n], lses[n], prep[n], bias, d, f"swa_bwd_d{d}")
        grads += [dq, dk, dv]
        dss.append(ds.reshape(SWA_HEADS, -1))
    dqkvb, dqw, dkw = _swa_prep_bwd(qkvb, qw_t, kw_t, bd, grads, tm)
    dbias = _bias_grad(jnp.concatenate(dss, axis=1), _band_index(), 8192)
    fold = lambda w: jnp.sum(w.reshape(SWA_HEADS, SWA_DIM), axis=0)
    return dqkvb, fold(dqw), fold(dkw), dbias.T


def _mesh_pos():
    return lax.axis_index("x"), lax.axis_index("y"), lax.axis_index("c")


def _other_chips(x, y):
    return [(1 - x, y), (x, 1 - y), (1 - x, 1 - y)]


def _remote(src, dst, send_sem, recv_sem, device):
    return pltpu.make_async_remote_copy(src_ref=src, dst_ref=dst, send_sem=send_sem, recv_sem=recv_sem,
                                        device_id=device, device_id_type=MESH)


def _split_axis(shape2):
    return 0 if (shape2[0] // 2) % 16 == 0 else 1


def _half_index(shape2, c):
    axis = _split_axis(shape2)
    h = shape2[axis] // 2
    return (pl.ds(c * h, h), slice(None)) if axis == 0 else (slice(None), pl.ds(c * h, h))


def _all_gather(xs):
    n = len(xs)

    def body(*refs):
        ins, outs = refs[:n], refs[n:2 * n]
        send_sems, recv_sems = refs[2 * n:]
        x, y, c = _mesh_pos()
        me = 2 * x + y
        chips = _other_chips(x, y)
        halves = []
        sends = []
        for a in range(n):
            h = ins[a].shape[0] // 2
            mine, other = pl.ds(c * h, h), pl.ds((1 - c) * h, h)
            halves.append((mine, other))
            for j, chip in enumerate(chips):
                cp = _remote(ins[a].at[mine], outs[a].at[me, mine], send_sems.at[a, j], recv_sems.at[a, j], (*chip, c))
                cp.start()
                sends.append(cp)
        for a in range(n):
            mine, _ = halves[a]
            for j, chip in enumerate(chips):
                src = 2 * chip[0] + chip[1]
                landed = outs[a].at[src, mine]
                _remote(landed, landed, send_sems.at[a, j], recv_sems.at[a, j], (x, y, c)).wait_recv()
                fwd = _remote(landed, landed, send_sems.at[a, 3 + j], recv_sems.at[a, 3 + j], (x, y, 1 - c))
                fwd.start()
                sends.append(fwd)
        for a in range(n):
            _, other = halves[a]
            for j, chip in enumerate(chips):
                src = 2 * chip[0] + chip[1]
                landed = outs[a].at[src, other]
                _remote(landed, landed, send_sems.at[a, 3 + j], recv_sems.at[a, 3 + j], (x, y, c)).wait_recv()
        for cp in sends:
            cp.wait_send()

    outs = pl.pallas_call(
        body, name="all_gather_weights",
        in_specs=[ANY] * n, out_specs=[ANY] * n,
        out_shape=[jax.ShapeDtypeStruct((N_SHARDS,) + a.shape, a.dtype) for a in xs],
        scratch_shapes=[pltpu.SemaphoreType.DMA((n, 6)), pltpu.SemaphoreType.DMA((n, 6))],
    )(*xs)
    me = 2 * lax.axis_index("x") + lax.axis_index("y")
    return [lax.dynamic_update_slice_in_dim(o, a[None], me, 0) for o, a in zip(outs, xs)]


def _rs_pair(gs):
    n = len(gs)

    def body(*refs):
        ins, lands = refs[:n], refs[n:2 * n]
        send_sems, recv_sems = refs[2 * n:]
        x, y, c = _mesh_pos()
        cps = []
        for a in range(n):
            theirs = (slice(None),) + _half_index(ins[a].shape[1:], 1 - c)
            cp = _remote(ins[a].at[theirs], lands[a], send_sems.at[a], recv_sems.at[a], (x, y, 1 - c))
            cp.start()
            cps.append(cp)
        for cp in cps:
            cp.wait()

    def half_shape(g):
        dims = list(g.shape)
        dims[1 + _split_axis(g.shape[1:])] //= 2
        return tuple(dims)

    lands = pl.pallas_call(
        body, name="rs_pair", in_specs=[ANY] * n, out_specs=[ANY] * n,
        out_shape=[jax.ShapeDtypeStruct(half_shape(g), g.dtype) for g in gs],
        scratch_shapes=[pltpu.SemaphoreType.DMA((n,)), pltpu.SemaphoreType.DMA((n,))],
    )(*gs)
    c = lax.axis_index("c")
    owns = []
    for g in gs:
        axis = 1 + _split_axis(g.shape[1:])
        owns.append(lax.dynamic_slice_in_dim(g, c * (g.shape[axis] // 2), g.shape[axis] // 2, axis))
    return owns + list(lands)


def _rs_chips(ss):
    n = len(ss)

    def body(*refs):
        ins, outs = refs[:n], refs[n:2 * n]
        send_sems, recv_sems = refs[2 * n:]
        x, y, c = _mesh_pos()
        me = 2 * x + y
        chips = _other_chips(x, y)
        cps = []
        for a in range(n):
            for j, chip in enumerate(chips):
                dst_chip = 2 * chip[0] + chip[1]
                cp = _remote(ins[a].at[dst_chip], outs[a].at[me], send_sems.at[a, j], recv_sems.at[a, j], (*chip, c))
                cp.start()
                cps.append(cp)
        for a in range(n):
            for j, chip in enumerate(chips):
                src = 2 * chip[0] + chip[1]
                _remote(outs[a].at[src], outs[a].at[src], send_sems.at[a, j], recv_sems.at[a, j], (x, y, c)).wait_recv()
        for cp in cps:
            cp.wait_send()

    outs = pl.pallas_call(
        body, name="rs_chips", in_specs=[ANY] * n, out_specs=[ANY] * n,
        out_shape=[jax.ShapeDtypeStruct(s.shape, s.dtype) for s in ss],
        scratch_shapes=[pltpu.SemaphoreType.DMA((n, 3)), pltpu.SemaphoreType.DMA((n, 3))],
    )(*ss)
    me = 2 * lax.axis_index("x") + lax.axis_index("y")
    return [lax.dynamic_update_slice_in_dim(o, lax.dynamic_slice_in_dim(s, me, 1, 0), me, 0) for o, s in zip(outs, ss)]


def _rs_join(fs, axes):
    n = len(fs)

    def whole(f, axis):
        dims = list(f.shape)
        dims[axis] *= 2
        return tuple(dims)

    def body(*refs):
        ins, outs = refs[:n], refs[n:2 * n]
        send_sems, recv_sems = refs[2 * n:]
        x, y, c = _mesh_pos()
        cps = []
        for a in range(n):
            h = ins[a].shape[axes[a]]
            mine = (pl.ds(c * h, h), slice(None)) if axes[a] == 0 else (slice(None), pl.ds(c * h, h))
            cp = _remote(ins[a], outs[a].at[mine], send_sems.at[a], recv_sems.at[a], (x, y, 1 - c))
            cp.start()
            cps.append(cp)
        for cp in cps:
            cp.wait()

    outs = pl.pallas_call(
        body, name="rs_join", in_specs=[ANY] * n, out_specs=[ANY] * n,
        out_shape=[jax.ShapeDtypeStruct(whole(f, ax), f.dtype) for f, ax in zip(fs, axes)],
        scratch_shapes=[pltpu.SemaphoreType.DMA((n,)), pltpu.SemaphoreType.DMA((n,))],
    )(*fs)
    c = lax.axis_index("c")
    return [lax.dynamic_update_slice_in_dim(o, f, c * f.shape[ax], ax) for o, f, ax in zip(outs, fs, axes)]


def _gather_exchange(xs):
    def start(cin, cout, send_sems, recv_sems):
        x, y, c = _mesh_pos()
        me = 2 * x + y
        for a, (src, dst) in enumerate(zip(cin, cout)):
            mine = _half_index(src.shape, c)
            for j, chip in enumerate(_other_chips(x, y)):
                _remote(src.at[mine], dst.at[(me,) + mine], send_sems.at[a, j], recv_sems.at[a, j], (*chip, c)).start()

    def finish(cin, cout, send_sems, recv_sems):
        x, y, c = _mesh_pos()
        for a, dst in enumerate(cout):
            for j, chip in enumerate(_other_chips(x, y)):
                landed = dst.at[(2 * chip[0] + chip[1],) + _half_index(dst.shape[1:], c)]
                _remote(landed, landed, send_sems.at[a, j], recv_sems.at[a, j], (x, y, c)).wait()

    return _Exchange(tuple(xs), tuple(jax.ShapeDtypeStruct((N_SHARDS,) + a.shape, a.dtype) for a in xs), start, finish)


def _gather_forward(gs, xs):
    n = len(gs)

    def body(*refs):
        outs = refs[n:2 * n]
        send_sems, recv_sems = refs[2 * n:]
        x, y, c = _mesh_pos()
        chips = _other_chips(x, y)
        cps = []
        for a in range(n):
            for j, chip in enumerate(chips):
                landed = outs[a].at[(2 * chip[0] + chip[1],) + _half_index(outs[a].shape[1:], c)]
                cp = _remote(landed, landed, send_sems.at[a, j], recv_sems.at[a, j], (x, y, 1 - c))
                cp.start()
                cps.append(cp)
        for a in range(n):
            for j, chip in enumerate(chips):
                other = outs[a].at[(2 * chip[0] + chip[1],) + _half_index(outs[a].shape[1:], 1 - c)]
                _remote(other, other, send_sems.at[a, j], recv_sems.at[a, j], (x, y, c)).wait_recv()
        for cp in cps:
            cp.wait_send()

    outs = pl.pallas_call(
        body, name="gather_forward", in_specs=[ANY] * n, out_specs=[ANY] * n,
        out_shape=[jax.ShapeDtypeStruct(g.shape, g.dtype) for g in gs],
        input_output_aliases={i: i for i in range(n)},
        scratch_shapes=[pltpu.SemaphoreType.DMA((n, 3)), pltpu.SemaphoreType.DMA((n, 3))],
    )(*gs)
    me = 2 * lax.axis_index("x") + lax.axis_index("y")
    return [lax.dynamic_update_slice_in_dim(o, a[None], me, 0) for o, a in zip(outs, xs)]


def _scatter_exchange(ss):
    def start(cin, cout, send_sems, recv_sems):
        x, y, c = _mesh_pos()
        me = 2 * x + y
        for a, (src, dst) in enumerate(zip(cin, cout)):
            for j, chip in enumerate(_other_chips(x, y)):
                _remote(src.at[2 * chip[0] + chip[1]], dst.at[me], send_sems.at[a, j], recv_sems.at[a, j],
                        (*chip, c)).start()

    def finish(cin, cout, send_sems, recv_sems):
        x, y, c = _mesh_pos()
        for a, dst in enumerate(cout):
            for j, chip in enumerate(_other_chips(x, y)):
                slot = dst.at[2 * chip[0] + chip[1]]
                _remote(slot, slot, send_sems.at[a, j], recv_sems.at[a, j], (x, y, c)).wait()

    return _Exchange(tuple(ss), tuple(jax.ShapeDtypeStruct(s.shape, s.dtype) for s in ss), start, finish)


def _own_slots(slots, ss):
    me = 2 * lax.axis_index("x") + lax.axis_index("y")
    return [lax.dynamic_update_slice_in_dim(o, lax.dynamic_slice_in_dim(s, me, 1, 0), me, 0) for o, s in zip(slots, ss)]


def _add_pair(a, b, name):
    nj, h, c = a.shape

    def body(a_ref, b_ref, o_ref):
        o_ref[...] = (a_ref[...].astype(F32) + b_ref[...].astype(F32)).astype(BF16)

    blk = pl.BlockSpec((1, h, c), lambda j: (j, 0, 0))
    return pl.pallas_call(body, name=name, grid=(nj,), in_specs=[blk, blk], out_specs=blk,
                          out_shape=jax.ShapeDtypeStruct(a.shape, BF16),
                          compiler_params=_params(("arbitrary",), VMEM_LIMIT))(a, b)


def _sum_slots(l2, name):
    nj, h, c = l2.shape
    th = h // 2 if h % 32 == 0 else h

    def body(i_ref, o_ref):
        acc = i_ref[0].astype(F32)
        for s in range(1, nj):
            acc = acc + i_ref[s].astype(F32)
        o_ref[...] = acc

    return pl.pallas_call(body, name=name, grid=(h // th,),
                          in_specs=[pl.BlockSpec((nj, th, c), lambda i: (0, i, 0))],
                          out_specs=pl.BlockSpec((th, c), lambda i: (i, 0)),
                          out_shape=jax.ShapeDtypeStruct((h, c), F32),
                          compiler_params=_params(("arbitrary",), VMEM_LIMIT))(l2)


def _all_reduce_small(p):
    r = p.shape[0]

    def body(p_ref, o_ref, buf, send_sems, recv_sems):
        x, y, c = _mesh_pos()
        me = 4 * x + 2 * y + c
        buf[me] = p_ref[...]
        cps = []
        k = 0
        for fx in range(2):
            for fy in range(2):
                for fc in range(2):
                    if fx + fy + fc == 0:
                        continue
                    peer = (1 - x if fx else x, 1 - y if fy else y, 1 - c if fc else c)
                    peer_id = 4 * peer[0] + 2 * peer[1] + peer[2]
                    cp = _remote(p_ref, buf.at[me], send_sems.at[k], recv_sems.at[k], peer)
                    cp.start()
                    cps.append((cp, peer_id, k))
                    k += 1
        for cp, peer_id, k in cps:
            _remote(p_ref, buf.at[peer_id], send_sems.at[k], recv_sems.at[k], (x, y, c)).wait_recv()
        for cp, _, _ in cps:
            cp.wait_send()
        acc = buf[0]
        for s in range(1, 8):
            acc = acc + buf[s]
        o_ref[...] = acc

    vm = pl.BlockSpec(memory_space=pltpu.VMEM)
    return pl.pallas_call(
        body, name="all_reduce_small", in_specs=[vm], out_specs=vm,
        out_shape=jax.ShapeDtypeStruct(p.shape, F32),
        scratch_shapes=[pltpu.VMEM((8, r, LANES), F32), pltpu.SemaphoreType.DMA((7,)), pltpu.SemaphoreType.DMA((7,))],
    )(p)


def _adamw(w, g, m, v, name):
    r, c = w.shape
    row_tiles = [d for d in range(8, min(r, 256) + 1, 8) if r % d == 0]
    tr, tc = (max(row_tiles), c) if row_tiles else (r, 256 if c % 256 == 0 else c)
    c1 = 1.0 / (1.0 - ADAM_B1 ** ADAM_STEP)
    c2 = 1.0 / (1.0 - ADAM_B2 ** ADAM_STEP)

    def body(w_ref, g_ref, m_ref, v_ref, d_ref, nm_ref, nv_ref):
        gv = g_ref[...]
        nm = ADAM_B1 * m_ref[...] + (1.0 - ADAM_B1) * gv
        nv = ADAM_B2 * v_ref[...] + (1.0 - ADAM_B2) * (gv * gv)
        d_ref[...] = -ADAM_LR * ((nm * c1) / (jnp.sqrt(nv * c2) + ADAM_EPS) + ADAM_WD * w_ref[...])
        nm_ref[...] = nm
        nv_ref[...] = nv

    blk = pl.BlockSpec((tr, tc), lambda i, j: (i, j))
    shp = jax.ShapeDtypeStruct((r, c), F32)
    return pl.pallas_call(body, name=name, grid=(r // tr, c // tc), in_specs=[blk] * 4, out_specs=[blk] * 3,
                          out_shape=[shp, shp, shp],
                          compiler_params=_params(("arbitrary", "arbitrary"), VMEM_LIMIT))(w, g, m, v)


PACK_UNIT = 8 * LANES


def _pack(arrs):
    parts = []
    for a in arrs:
        f = a.reshape(-1).astype(F32)
        parts.append(jnp.pad(f, (0, (-f.shape[0]) % PACK_UNIT)).reshape(-1, LANES))
    return jnp.concatenate(parts, axis=0)


def _unpack(m, shapes):
    outs, row = [], 0
    for s in shapes:
        n = int(np.prod(s))
        rows = -(-n // PACK_UNIT) * 8
        outs.append(m[row:row + rows].reshape(-1)[:n].reshape(s))
        row += rows
    return outs


WEIGHTS = ["ffn1_norm", "ffn1_w_gate", "ffn1_w_up", "ffn1_w_down", "mix_norm", "w_in", "conv_w", "a_log", "dt_bias",
           "gdn_norm_w", "q_norm_w", "k_norm_w", "rel_bias", "w_out", "ffn2_norm", "ffn2_w_gate", "ffn2_w_up",
           "ffn2_w_down", "final_norm"]
BIG = ["ffn1_w_gate", "ffn1_w_up", "ffn1_w_down", "w_in", "w_out", "ffn2_w_gate", "ffn2_w_up", "ffn2_w_down"]
SMALL = [n for n in WEIGHTS if n not in BIG]
COL_SHARDED = ["ffn1_w_gate", "ffn1_w_up", "w_in", "ffn2_w_gate", "ffn2_w_up"]
N_IN_COLS = 3600
TM = 256
TE = 512
TK = 2048


def kernel(x, ffn1_norm, ffn1_w_gate, ffn1_w_up, ffn1_w_down, mix_norm, w_in, conv_w, a_log, dt_bias, gdn_norm_w, q_norm_w, k_norm_w, rel_bias, w_out, ffn2_norm, ffn2_w_gate, ffn2_w_up, ffn2_w_down, final_norm, loss_target, m_ffn1_norm, m_ffn1_w_gate, m_ffn1_w_up, m_ffn1_w_down, m_mix_norm, m_w_in, m_conv_w, m_a_log, m_dt_bias, m_gdn_norm_w, m_q_norm_w, m_k_norm_w, m_rel_bias, m_w_out, m_ffn2_norm, m_ffn2_w_gate, m_ffn2_w_up, m_ffn2_w_down, m_final_norm, v_ffn1_norm, v_ffn1_w_gate, v_ffn1_w_up, v_ffn1_w_down, v_mix_norm, v_w_in, v_conv_w, v_a_log, v_dt_bias, v_gdn_norm_w, v_q_norm_w, v_k_norm_w, v_rel_bias, v_w_out, v_ffn2_norm, v_ffn2_w_gate, v_ffn2_w_up, v_ffn2_w_down, v_final_norm):
    p = dict(locals())
    xs, target = x[0], loss_target[0]
    t, d = xs.shape
    nc = t // CHUNK
    tk = min(TK, t)
    me = 2 * lax.axis_index("x") + lax.axis_index("y")

    first = ["ffn1_w_gate", "ffn1_w_up", "ffn1_w_down"]
    later = [n for n in BIG if n not in first] + ["conv_w"]
    local = lambda n, a: a[0].T if n in COL_SHARDED else a[0]
    shards = {n: local(n, p[n]).astype(BF16) for n in BIG}
    shards["conv_w"] = conv_w[0]
    gw = dict(zip(first, _all_gather([shards[n] for n in first])))
    f1 = (gw["ffn1_w_gate"], gw["ffn1_w_up"], gw["ffn1_w_down"])
    (x1, xn1, g1, u1), landed = _ffn_fwd(xs, ffn1_norm, *f1, TM, "ffn1_fwd",
                                         exchange=_gather_exchange([shards[n] for n in later]))
    gw.update(zip(later, _gather_forward(landed, [shards[n] for n in later])))
    w_in_t = gw["w_in"].reshape(N_IN_COLS, d)
    wp = jnp.concatenate([w_in_t[:2048], jnp.pad(w_in_t[2048:2064], ((0, LANES - 16), (0, 0))), w_in_t[2064:]], axis=0)
    w_out_full = gw["w_out"].reshape(d, d)
    conv_rows = conv_w.shape[1]
    cw = jnp.pad(gw["conv_w"].reshape(N_SHARDS * conv_rows, CONV_TAPS).T, ((0, 8 - CONV_TAPS), (0, 0)))
    gp = jnp.pad(jnp.stack([a_log.reshape(8), dt_bias.reshape(8)]), ((0, 6), (0, LANES - 8)))
    gdn_w = gdn_norm_w.reshape(1, GDN_DIM)
    qw_t = jnp.tile(q_norm_w.reshape(1, SWA_DIM), (1, SWA_HEADS))
    kw_t = jnp.tile(k_norm_w.reshape(1, SWA_DIM), (1, SWA_HEADS))
    bd = jnp.asarray(np.kron(np.eye(2), np.full((SWA_DIM, SWA_DIM), 1.0 / SWA_DIM)), F32)
    f2 = (gw["ffn2_w_gate"], gw["ffn2_w_up"], gw["ffn2_w_down"])

    hn, qkva, z, ab, qkvb = _mix_in_fwd(x1, mix_norm, wp, TM)
    qkvc, gb = _gdn_prep_fwd(qkva, cw, ab, gp, TM)
    gbt = jnp.transpose(gb[:, :16].reshape(nc, CHUNK, 16), (0, 2, 1))
    o_f, o_b, gdn_saved = _gdn_fwd(qkvc, gb, gbt)
    oa = _gdn_post_fwd(o_f, o_b, z, gdn_w, TE)
    o_swa, o_swa16, swa_saved = _swa_branch_fwd(qkvb, qw_t, kw_t, rel_bias, bd, TE)
    x2 = _mix_out_fwd(x1, oa, o_swa, w_out_full, TM)
    (dx3, xn2, g2, u2, loss_part, d_final), _ = _ffn_fwd(x2, ffn2_norm, *f2, TM, "ffn2_fwd", head=(final_norm, target))

    def pair_sums(partials, tag):
        pair = _rs_pair(partials)
        k = len(partials)
        return [_add_pair(pair[i], pair[k + i], f"rs_add_{tag}{i}") for i in range(k)]

    (dx2, dyh2, dg2, du2, h2, d_nw2), _ = _ffn_bwd_dx(dx3, x2, ffn2_norm, g2, u2, *f2, TM, "ffn2_bwd_dx")
    dwg2 = _matmul_tn(dg2, xn2, tk, "ffn2_dwg")
    dwu2 = _matmul_tn(du2, xn2, tk, "ffn2_dwu")
    dwd2 = _matmul_tn(h2, dyh2, tk, "ffn2_dwd")
    sums_f2 = pair_sums([dwg2, dwu2, dwd2], "a")
    doa, dob, dx2b = _mix_out_bwd(dx2, w_out_full, TM)
    dwo = jnp.concatenate([_matmul_tn(oa, dx2b, tk, "w_out_dw_a")[0], _matmul_tn(o_swa16, dx2b, tk, "w_out_dw_b")[0]],
                          axis=0).reshape(N_SHARDS, d // N_SHARDS, d)
    do_g, dz, d_gdnw = _gdn_post_bwd(doa, o_f, o_b, z, gdn_w, TE)
    (dqkvc, dgates), slots_f2 = _gdn_bwd(qkvc, gb, gbt, do_g, gdn_saved, exchange=_scatter_exchange(sums_f2))
    dqkva, dab, dcw, dgp = _gdn_prep_bwd(qkva, cw, ab, gp, dqkvc, dgates, TM)
    dqkvb, d_qw, d_kw, d_rel = _swa_branch_bwd(dob, o_swa, swa_saved, qkvb, qw_t, kw_t, bd, TE)
    dpieces = (dqkva, dz, dab, dqkvb)
    dwp = [_matmul_tn(dp, hn, tk, f"w_in_dw_{i}")[0] for i, dp in enumerate(dpieces)]
    dw_in = jnp.concatenate([dwp[0], dwp[1], dwp[2][:16], dwp[3]], axis=0).reshape(N_SHARDS, N_IN_COLS // N_SHARDS, d)
    sums_mix = pair_sums([dw_in, dwo], "b")
    (dx1, d_mixnw), slots_mix = _mix_in_bwd_dx(dx2, x1, mix_norm, dpieces, wp, TM, exchange=_scatter_exchange(sums_mix))
    (gx, dyh1, dg1, du1, h1, d_nw1), _ = _ffn_bwd_dx(dx1, xs, ffn1_norm, g1, u1, *f1, TM, "ffn1_bwd_dx")
    dwg1 = _matmul_tn(dg1, xn1, tk, "ffn1_dwg")
    dwu1 = _matmul_tn(du1, xn1, tk, "ffn1_dwu")
    sums_gu = pair_sums([dwg1, dwu1], "c")
    dwd1, slots_gu = _matmul_tn(h1, dyh1, tk, "ffn1_dwd", exchange=_scatter_exchange(sums_gu))
    slots_f1 = _own_slots(slots_gu, sums_gu) + _rs_chips(pair_sums([dwd1], "d"))
    slots = slots_f1 + _own_slots(slots_mix, sums_mix) + _own_slots(slots_f2, sums_f2)
    halves = [_sum_slots(s, f"rs_sum_{i}") for i, s in enumerate(slots)]
    g_big = dict(zip(BIG, _rs_join(halves, [_split_axis(shards[n].shape) for n in BIG])))

    small_partial = {"ffn1_norm": d_nw1, "mix_norm": d_mixnw, "a_log": dgp[0, 0:8], "dt_bias": dgp[1, 0:8],
                     "gdn_norm_w": d_gdnw, "q_norm_w": d_qw, "k_norm_w": d_kw, "rel_bias": d_rel,
                     "ffn2_norm": d_nw2, "final_norm": d_final, "conv_w": dcw[0:CONV_TAPS].T}
    red = _all_reduce_small(_pack([small_partial[n] for n in SMALL] + [loss_part[0, 0:1]]))
    full_shapes = [p[n].shape if n != "conv_w" else (N_SHARDS * conv_rows, CONV_TAPS) for n in SMALL]
    red_parts = _unpack(red, full_shapes + [(1,)])
    loss = red_parts[-1].reshape(())
    g_small = dict(zip(SMALL, red_parts[:-1]))
    g_small["conv_w"] = lax.dynamic_slice_in_dim(g_small["conv_w"], me * conv_rows, conv_rows, 0).reshape(conv_w.shape)

    grads, deltas, new_m, new_v = {}, {}, {}, {}
    for n in BIG:
        back = (lambda a: a.T[None]) if n in COL_SHARDED else (lambda a: a[None])
        grads[n] = back(g_big[n])
        dl, nm, nv = _adamw(local(n, p[n]), g_big[n], local(n, p["m_" + n]), local(n, p["v_" + n]), "adamw_" + n)
        deltas[n], new_m[n], new_v[n] = back(dl), back(nm), back(nv)
    packed = [_pack([src[n] for n in SMALL]) for src in
              ({n: p[n] for n in SMALL}, g_small, {n: p["m_" + n] for n in SMALL}, {n: p["v_" + n] for n in SMALL})]
    small_shapes = [p[n].shape for n in SMALL]
    for dst, arr in zip((deltas, new_m, new_v), _adamw(*packed, "adamw_small")):
        dst.update(zip(SMALL, _unpack(arr, small_shapes)))
    grads.update(g_small)

    return (loss, gx[None], *[grads[n] for n in WEIGHTS], *[deltas[n] for n in WEIGHTS],
            *[new_m[n] for n in WEIGHTS], *[new_v[n] for n in WEIGHTS])
```

```python
import math
from typing import Callable, NamedTuple

import numpy as np
import jax
import jax.numpy as jnp
from jax import lax
from jax.experimental import pallas as pl
from jax.experimental.pallas import tpu as pltpu

F32 = jnp.float32
BF16 = jnp.bfloat16
HIGHEST = lax.Precision.HIGHEST
MESH = pl.DeviceIdType.MESH

EPS = 1e-6
NEG_BIG = -1e30
GDN_HEADS = 4
GDN_DIM = 128
CHUNK = 64
SWA_HEADS = 8
SWA_DIM = 64
PATTERNS = ((128, 1), (512, 4), (2048, 16))
RADIUS = 64
REL_BUCKETS = 32
REL_MAX_DISTANCE = 1024
CONV_TAPS = 5
N_SHARDS = 4
LANES = 128
VMEM_LIMIT = 56 * 1024 * 1024

ADAM_LR, ADAM_B1, ADAM_B2, ADAM_EPS, ADAM_WD, ADAM_STEP = 0.001, 0.9, 0.999, 1e-08, 0.01, 10


def _params(sem=None, vmem=None):
    return pltpu.CompilerParams(dimension_semantics=sem, vmem_limit_bytes=vmem)


def _resident(shape):
    nd = len(shape)
    return pl.BlockSpec(shape, lambda *_: (0,) * nd, pipeline_mode=pl.Buffered(1))


ANY = pl.BlockSpec(memory_space=pl.ANY)


class _Exchange(NamedTuple):
    arrays: tuple
    out_shape: tuple
    start: Callable
    finish: Callable


def _grid_call(body, name, nsteps, in_specs, out_specs, out_shape, operands, scratch=(), exchange=None):
    params = _params(("arbitrary",), VMEM_LIMIT)
    if exchange is None:
        res = pl.pallas_call(body, name=name, grid=(nsteps,), in_specs=list(in_specs), out_specs=list(out_specs),
                             out_shape=list(out_shape), scratch_shapes=list(scratch), compiler_params=params)(*operands)
        return list(res), []
    n_in, n_out, k, n_scr = len(in_specs), len(out_specs), len(exchange.arrays), len(scratch)

    def wrapped(*refs):
        ins, cin = refs[:n_in], refs[n_in:n_in + k]
        outs, cout = refs[n_in + k:n_in + k + n_out], refs[n_in + k + n_out:n_in + 2 * k + n_out]
        rest = refs[n_in + 2 * k + n_out:]
        scr, (send_sems, recv_sems) = rest[:n_scr], rest[n_scr:]

        @pl.when(pl.program_id(0) == 0)
        def _():
            exchange.start(cin, cout, send_sems, recv_sems)

        body(*ins, *outs, *scr)

        @pl.when(pl.program_id(0) == nsteps - 1)
        def _():
            exchange.finish(cin, cout, send_sems, recv_sems)

    res = pl.pallas_call(
        wrapped, name=name, grid=(nsteps,), in_specs=list(in_specs) + [ANY] * k, out_specs=list(out_specs) + [ANY] * k,
        out_shape=list(out_shape) + list(exchange.out_shape),
        scratch_shapes=list(scratch) + [pltpu.SemaphoreType.DMA((k, 3)), pltpu.SemaphoreType.DMA((k, 3))],
        compiler_params=params)(*operands, *exchange.arrays)
    return list(res[:n_out]), list(res[n_out:])


def _dot(a, b):
    return jnp.dot(a.astype(BF16), b.astype(BF16), preferred_element_type=F32)


def _dot_nt(a, b):
    return lax.dot_general(a.astype(BF16), b.astype(BF16), (((1,), (1,)), ((), ())), preferred_element_type=F32)


def _dot_tn(a, b):
    return lax.dot_general(a.astype(BF16), b.astype(BF16), (((0,), (0,)), ((), ())), preferred_element_type=F32)


def _dot_hi(a, b):
    return jnp.dot(a, b, preferred_element_type=F32, precision=HIGHEST)


def _sigmoid(x):
    return 1.0 / (1.0 + jnp.exp(-x))


def _rstd(xf):
    return lax.rsqrt(jnp.mean(xf * xf, axis=-1, keepdims=True) + EPS)


def _rms_bwd(xf, r, nw, dxn):
    xhat = xf * r
    dxh = dxn * nw
    dx = r * (dxh - xhat * jnp.mean(dxh * xhat, axis=-1, keepdims=True))
    return dx, jnp.sum(dxn * xhat, axis=0, keepdims=True)


def _ffn_fwd(x, nw, wg, wu, wd, tm, name, exchange=None, head=None):
    t, d = x.shape
    nj, fs, _ = wg.shape

    def body(x_ref, nw_ref, wg_ref, wu_ref, wd_ref, *rest):
        if head is None:
            y_ref, xn_ref, g_ref, u_ref = rest
        else:
            fw_ref, t_ref, y_ref, xn_ref, g_ref, u_ref, loss_ref, dfw_ref = rest

            @pl.when(pl.program_id(0) == 0)
            def _():
                loss_ref[...] = jnp.zeros_like(loss_ref)
                dfw_ref[...] = jnp.zeros_like(dfw_ref)

        xf = x_ref[...]
        xn = (xf * _rstd(xf) * nw_ref[...]).astype(BF16)
        xn_ref[...] = xn
        acc = jnp.zeros((tm, d), F32)
        for j in range(nj):
            g = _dot_nt(xn, wg_ref[j])
            u = _dot_nt(xn, wu_ref[j])
            h = (g * _sigmoid(g) * u).astype(BF16)
            acc = acc + jnp.dot(h, wd_ref[j], preferred_element_type=F32)
            g_ref[j] = g.astype(BF16)
            u_ref[j] = u.astype(BF16)
        y = xf + 0.5 * acc
        if head is None:
            y_ref[...] = y
        else:
            r = _rstd(y)
            err = y * r * fw_ref[...] - t_ref[...]
            loss_ref[...] += 0.5 * jnp.sum(jnp.mean(err * err, axis=-1, keepdims=True), axis=0, keepdims=True)
            dy, dfw = _rms_bwd(y, r, fw_ref[...], err * (1.0 / d))
            y_ref[...] = dy
            dfw_ref[...] += dfw

    row = pl.BlockSpec((tm, d), lambda i: (i, 0))
    act = pl.BlockSpec((nj, tm, fs), lambda i: (0, i, 0))
    in_specs = [row, _resident((1, d)), _resident(wg.shape), _resident(wu.shape), _resident(wd.shape)]
    out_specs = [row, row, act, act]
    out_shape = [jax.ShapeDtypeStruct((t, d), F32), jax.ShapeDtypeStruct((t, d), BF16),
                 jax.ShapeDtypeStruct((nj, t, fs), BF16), jax.ShapeDtypeStruct((nj, t, fs), BF16)]
    operands = (x, nw, wg, wu, wd)
    if head is not None:
        in_specs += [_resident((1, d)), row]
        out_specs += [pl.BlockSpec((1, LANES), lambda i: (0, 0)), pl.BlockSpec((1, d), lambda i: (0, 0))]
        out_shape += [jax.ShapeDtypeStruct((1, LANES), F32), jax.ShapeDtypeStruct((1, d), F32)]
        operands += tuple(head)
    return _grid_call(body, name, t // tm, in_specs, out_specs, out_shape, operands, exchange=exchange)


def _ffn_bwd_dx(dy, x, nw, g, u, wg, wu, wd, tm, name, exchange=None):
    t, d = x.shape
    nj, fs, _ = wg.shape

    def body(dy_ref, x_ref, nw_ref, g_ref, u_ref, wg_ref, wu_ref, wd_ref,
             dx_ref, dyh_ref, dg_ref, du_ref, h_ref, dnw_ref):
        @pl.when(pl.program_id(0) == 0)
        def _():
            dnw_ref[...] = jnp.zeros_like(dnw_ref)

        dyv = dy_ref[...]
        dyh = (0.5 * dyv).astype(BF16)
        dyh_ref[...] = dyh
        dxn = jnp.zeros((tm, d), F32)
        dh_next = _dot_nt(dyh, wd_ref[0])
        for j in range(nj):
            dh = dh_next
            gv = g_ref[j].astype(F32)
            uv = u_ref[j].astype(F32)
            sg = _sigmoid(gv)
            si = gv * sg
            dg = (dh * uv * (sg * (1.0 + gv * (1.0 - sg)))).astype(BF16)
            du = (dh * si).astype(BF16)
            if j + 1 < nj:
                dh_next = _dot_nt(dyh, wd_ref[j + 1])
            h_ref[j] = (si * uv).astype(BF16)
            dg_ref[j] = dg
            du_ref[j] = du
            dxn = dxn + _dot(dg, wg_ref[j]) + _dot(du, wu_ref[j])
        xf = x_ref[...]
        dxr, dnw = _rms_bwd(xf, _rstd(xf), nw_ref[...], dxn)
        dx_ref[...] = dyv + dxr
        dnw_ref[...] += dnw

    row = pl.BlockSpec((tm, d), lambda i: (i, 0))
    act = pl.BlockSpec((nj, tm, fs), lambda i: (0, i, 0))
    act_shape = jax.ShapeDtypeStruct((nj, t, fs), BF16)
    return _grid_call(
        body, name, t // tm,
        [row, row, _resident((1, d)), act, act, _resident(wg.shape), _resident(wu.shape), _resident(wd.shape)],
        [row, row, act, act, act, pl.BlockSpec((1, d), lambda i: (0, 0))],
        [jax.ShapeDtypeStruct((t, d), F32), jax.ShapeDtypeStruct((t, d), BF16),
         act_shape, act_shape, act_shape, jax.ShapeDtypeStruct((1, d), F32)],
        (dy, x, nw, g, u, wg, wu, wd), exchange=exchange)


def _matmul_tn(a, b, tk, name, exchange=None):
    a3, b3 = a.ndim == 3, b.ndim == 3
    nj = a.shape[0] if a3 else (b.shape[0] if b3 else 1)
    t, m = a.shape[-2:]
    n = b.shape[-1]
    nt = t // tk

    def body(a_ref, b_ref, o_ref, acc_ref):
        k = pl.program_id(0) % nt

        @pl.when(k == 0)
        def _():
            acc_ref[...] = jnp.zeros_like(acc_ref)

        acc_ref[...] += lax.dot_general(a_ref[...], b_ref[...], (((0,), (0,)), ((), ())),
                                        preferred_element_type=F32)

        @pl.when(k == nt - 1)
        def _():
            o_ref[...] = acc_ref[...].astype(o_ref.dtype)

    a_spec = (pl.BlockSpec((None, tk, m), lambda i: (i // nt, i % nt, 0)) if a3
              else pl.BlockSpec((tk, m), lambda i: (i % nt, 0)))
    b_spec = (pl.BlockSpec((None, tk, n), lambda i: (i // nt, i % nt, 0)) if b3
              else pl.BlockSpec((tk, n), lambda i: (i % nt, 0)))
    (out,), landed = _grid_call(
        body, name, nj * nt, [a_spec, b_spec], [pl.BlockSpec((None, m, n), lambda i: (i // nt, 0, 0))],
        [jax.ShapeDtypeStruct((nj, m, n), BF16)], (a, b), scratch=[pltpu.VMEM((m, n), F32)], exchange=exchange)
    return out if exchange is None else (out, landed)


P_QKVA, P_Z, P_AB, P_QKVB = (0, 1536), (1536, 2048), (2048, 2176), (2176, 3712)
P_PIECES = (P_QKVA, P_Z, P_AB, P_QKVB)
N_GATE_COLS = 4 * GDN_HEADS
P_COLS = 3712


def _mix_in_fwd(x1, nw, wp, tm):
    t, d = x1.shape

    def body(x_ref, nw_ref, w_ref, hn_ref, *outs):
        xf = x_ref[...]
        xn = (xf * _rstd(xf) * nw_ref[...]).astype(BF16)
        hn_ref[...] = xn
        for (a, b), o_ref in zip(P_PIECES, outs):
            o_ref[...] = _dot_nt(xn, w_ref[a:b, :])

    row = pl.BlockSpec((tm, d), lambda i: (i, 0))
    return pl.pallas_call(
        body, name="mix_in_fwd", grid=(t // tm,),
        in_specs=[row, _resident((1, d)), _resident(wp.shape)],
        out_specs=[row] + [pl.BlockSpec((tm, b - a), lambda i: (i, 0)) for a, b in P_PIECES],
        out_shape=[jax.ShapeDtypeStruct((t, d), BF16)]
                  + [jax.ShapeDtypeStruct((t, b - a), F32) for a, b in P_PIECES],
        compiler_params=_params(("arbitrary",), VMEM_LIMIT),
    )(x1, nw, wp)


def _mix_in_bwd_dx(dx, x1, nw, dpieces, wp, tm, exchange=None):
    t, d = x1.shape

    def body(dx_ref, x_ref, nw_ref, p0, p1, p2, p3, w_ref, o_ref, dnw_ref):
        @pl.when(pl.program_id(0) == 0)
        def _():
            dnw_ref[...] = jnp.zeros_like(dnw_ref)

        dh = jnp.zeros((tm, d), F32)
        for (a, b), p_ref in zip(P_PIECES, (p0, p1, p2, p3)):
            dh = dh + _dot(p_ref[...], w_ref[a:b, :])
        xf = x_ref[...]
        dxr, dnw = _rms_bwd(xf, _rstd(xf), nw_ref[...], dh)
        o_ref[...] = dx_ref[...] + dxr
        dnw_ref[...] += dnw

    row = pl.BlockSpec((tm, d), lambda i: (i, 0))
    return _grid_call(
        body, "mix_in_bwd_dx", t // tm,
        [row, row, _resident((1, d))]
        + [pl.BlockSpec((tm, b - a), lambda i: (i, 0)) for a, b in P_PIECES] + [_resident(wp.shape)],
        [row, pl.BlockSpec((1, d), lambda i: (0, 0))],
        [jax.ShapeDtypeStruct((t, d), F32), jax.ShapeDtypeStruct((1, d), F32)],
        (dx, x1, nw, *dpieces, wp), exchange=exchange)


def _mix_out_fwd(x1, oa, ob, w, tm):
    t, d = x1.shape
    half = oa.shape[1]

    def body(x_ref, oa_ref, ob_ref, w_ref, o_ref):
        o_ref[...] = (x_ref[...] + _dot(oa_ref[...], w_ref[0:half, :]) + _dot(ob_ref[...], w_ref[half:2 * half, :]))

    row = pl.BlockSpec((tm, d), lambda i: (i, 0))
    hrow = pl.BlockSpec((tm, half), lambda i: (i, 0))
    return pl.pallas_call(
        body, name="mix_out_fwd", grid=(t // tm,),
        in_specs=[row, hrow, hrow, _resident(w.shape)],
        out_specs=row, out_shape=jax.ShapeDtypeStruct((t, d), F32),
        compiler_params=_params(("arbitrary",), VMEM_LIMIT),
    )(x1, oa, ob, w)


def _mix_out_bwd(dx2, w, tm):
    t, d = dx2.shape
    half = w.shape[0] // 2

    def body(dx_ref, w_ref, doa_ref, dob_ref, dxb_ref):
        dxb = dx_ref[...].astype(BF16)
        dxb_ref[...] = dxb
        doa_ref[...] = _dot_nt(dxb, w_ref[0:half, :])
        dob_ref[...] = _dot_nt(dxb, w_ref[half:2 * half, :])

    row = pl.BlockSpec((tm, d), lambda i: (i, 0))
    hrow = pl.BlockSpec((tm, half), lambda i: (i, 0))
    return pl.pallas_call(
        body, name="mix_out_bwd", grid=(t // tm,),
        in_specs=[row, _resident(w.shape)],
        out_specs=[hrow, hrow, row],
        out_shape=[jax.ShapeDtypeStruct((t, half), F32), jax.ShapeDtypeStruct((t, half), F32),
                   jax.ShapeDtypeStruct((t, d), BF16)],
        compiler_params=_params(("arbitrary",), VMEM_LIMIT),
    )(dx2, w)


HALO = 8


def _halo_row_specs(tr, cols, nrow8):
    per = tr // HALO
    return [pl.BlockSpec((tr, cols), lambda i: (i, 0)),
            pl.BlockSpec((HALO, cols), lambda i: (jnp.maximum(i * per - 1, 0), 0)),
            pl.BlockSpec((HALO, cols), lambda i: (jnp.minimum((i + 1) * per, nrow8 - 1), 0))]


def _conv_window(xm, xp, xn, first, last, cols):
    prev = jnp.where(first, 0.0, xp[:, cols])
    nxt = jnp.where(last, 0.0, xn[:, cols])
    return jnp.concatenate([prev, xm[:, cols], nxt], axis=0)


def _shift_rows(xw, off):
    n = xw.shape[0]
    sh = (-off) % n
    return xw if sh == 0 else pltpu.roll(xw, sh, 0)


def _conv_pre(xw, cw_ref, cols):
    acc = None
    for j in range(CONV_TAPS):
        term = _shift_rows(xw, j - CONV_TAPS // 2) * cw_ref[j:j + 1, cols]
        acc = term if acc is None else acc + term
    return acc


def _softplus(x):
    u = jnp.exp(-jnp.abs(x))
    w = 1.0 + u
    log1p = jnp.where(w == 1.0, u, jnp.log(w) * u / jnp.where(w == 1.0, 1.0, w - 1.0))
    return jnp.maximum(x, 0.0) + log1p


def _gdn_prep_fwd(qkva, cw, ab, gp, tr):
    t, c = qkva.shape
    nt = t // tr
    ncb = c // LANES

    def body(xm, xp, xn, cw_ref, ab_ref, gp_ref, o_ref, gb_ref):
        i = pl.program_id(0)
        first, last = i == 0, i == nt - 1
        for cb in range(ncb):
            cols = slice(cb * LANES, (cb + 1) * LANES)
            xw = _conv_window(xm, xp, xn, first, last, cols)
            pre = _conv_pre(xw, cw_ref, cols)[HALO:HALO + tr]
            y = pre * _sigmoid(pre)
            if cb < 2 * GDN_HEADS:
                y = y * lax.rsqrt(jnp.sum(y * y, axis=-1, keepdims=True) + EPS)
            if cb < GDN_HEADS:
                y = y * (GDN_DIM ** -0.5)
            o_ref[:, cols] = y
        abv = ab_ref[...]
        lane = lax.broadcasted_iota(jnp.int32, abv.shape, 1)
        g = -jnp.exp(gp_ref[0:1, :]) * _softplus(abv + gp_ref[1:2, :])
        gb_ref[...] = jnp.where(lane < 8, g, jnp.where(lane < 16, _sigmoid(abv), 0.0))

    return pl.pallas_call(
        body, name="gdn_prep_fwd", grid=(nt,),
        in_specs=_halo_row_specs(tr, c, t // HALO)
                 + [_resident(cw.shape), pl.BlockSpec((tr, LANES), lambda i: (i, 0)), _resident(gp.shape)],
        out_specs=[pl.BlockSpec((tr, c), lambda i: (i, 0)), pl.BlockSpec((tr, LANES), lambda i: (i, 0))],
        out_shape=[jax.ShapeDtypeStruct((t, c), F32), jax.ShapeDtypeStruct((t, LANES), F32)],
        compiler_params=_params(("arbitrary",), VMEM_LIMIT),
    )(qkva, qkva, qkva, cw, ab, gp)


def _gdn_prep_bwd(qkva, cw, ab, gp, dy, dgates, tr):
    t, c = qkva.shape
    nt = t // tr
    ncb = c // LANES

    def body(xm, xp, xn, fm, fp, fn, cw_ref, ab_ref, gp_ref, gf_ref, dx_ref, dab_ref, dcw_ref, dgp_ref):
        i = pl.program_id(0)
        first, last = i == 0, i == nt - 1

        @pl.when(first)
        def _():
            dcw_ref[...] = jnp.zeros_like(dcw_ref)
            dgp_ref[...] = jnp.zeros_like(dgp_ref)

        sub8 = lax.broadcasted_iota(jnp.int32, (8, LANES), 0)
        for cb in range(ncb):
            cols = slice(cb * LANES, (cb + 1) * LANES)
            xw = _conv_window(xm, xp, xn, first, last, cols)
            dyw = _conv_window(fm, fp, fn, first, last, cols)
            pre = _conv_pre(xw, cw_ref, cols)
            sg = _sigmoid(pre)
            s = pre * sg
            if cb < 2 * GDN_HEADS:
                scale = (GDN_DIM ** -0.5) if cb < GDN_HEADS else 1.0
                r = lax.rsqrt(jnp.sum(s * s, axis=-1, keepdims=True) + EPS)
                dn = dyw * scale
                ds = r * dn - s * (r * r * r) * jnp.sum(dn * s, axis=-1, keepdims=True)
            else:
                ds = dyw
            dpre = ds * (sg * (1.0 + pre * (1.0 - sg)))
            dx = None
            dcw = jnp.zeros((8, LANES), F32)
            for j in range(CONV_TAPS):
                off = j - CONV_TAPS // 2
                term = _shift_rows(dpre, -off)[HALO:HALO + tr] * cw_ref[j:j + 1, cols]
                dx = term if dx is None else dx + term
                tap = jnp.sum(dpre[HALO:HALO + tr] * _shift_rows(xw, off)[HALO:HALO + tr], axis=0, keepdims=True)
                dcw = dcw + jnp.where(sub8 == j, tap, 0.0)
            dx_ref[:, cols] = dx.astype(BF16)
            dcw_ref[:, cols] += dcw

        abv = ab_ref[...]
        dgb = gf_ref[...]
        lane = lax.broadcasted_iota(jnp.int32, abv.shape, 1)
        nea = -jnp.exp(gp_ref[0:1, :])
        xs = abv + gp_ref[1:2, :]
        g = nea * _softplus(xs)
        beta = _sigmoid(abv)
        da = dgb * nea * _sigmoid(xs)
        dab = jnp.where(lane < 8, da, jnp.where(lane < 16, dgb * beta * (1.0 - beta), 0.0))
        dab_ref[...] = dab.astype(BF16)
        keep = lane[0:1, :] < 8
        dalog = jnp.where(keep, jnp.sum(dgb * g, axis=0, keepdims=True), 0.0)
        ddtb = jnp.where(keep, jnp.sum(da, axis=0, keepdims=True), 0.0)
        dgp_ref[...] += jnp.where(sub8 == 0, dalog, 0.0) + jnp.where(sub8 == 1, ddtb, 0.0)

    lrow = pl.BlockSpec((tr, LANES), lambda i: (i, 0))
    halo = _halo_row_specs(tr, c, t // HALO)
    return pl.pallas_call(
        body, name="gdn_prep_bwd", grid=(nt,),
        in_specs=halo + halo + [_resident(cw.shape), lrow, _resident(gp.shape), lrow],
        out_specs=[pl.BlockSpec((tr, c), lambda i: (i, 0)), lrow,
                   pl.BlockSpec(cw.shape, lambda i: (0, 0)), pl.BlockSpec(gp.shape, lambda i: (0, 0))],
        out_shape=[jax.ShapeDtypeStruct((t, c), BF16), jax.ShapeDtypeStruct((t, LANES), BF16),
                   jax.ShapeDtypeStruct(cw.shape, F32), jax.ShapeDtypeStruct(gp.shape, F32)],
        compiler_params=_params(("arbitrary",), VMEM_LIMIT),
    )(qkva, qkva, qkva, dy, dy, dy, cw, ab, gp, dgates)


def _chunk_masks(lower):
    ii = lax.broadcasted_iota(jnp.int32, (CHUNK, CHUNK), 0)
    jj = lax.broadcasted_iota(jnp.int32, (CHUNK, CHUNK), 1)
    incl = (ii >= jj) if lower else (ii <= jj)
    strict = (ii > jj) if lower else (ii < jj)
    return ii, jj, incl, strict


def _dot3(a, b):
    ah = a.astype(BF16)
    al = (a - ah.astype(F32)).astype(BF16)
    bh = b.astype(BF16)
    bl = (b - bh.astype(F32)).astype(BF16)
    d = lambda u, v: jnp.dot(u, v, preferred_element_type=F32)
    return d(ah, bh) + (d(ah, bl) + d(al, bh))


def _tri_inv_many(lmats, ii, jj):
    m16 = (ii // 16) == (jj // 16)
    m32 = (ii // 32) == (jj // 32)
    eye = jnp.where(ii == jj, 1.0, 0.0)
    l16 = [jnp.where(m16, l, 0.0) for l in lmats]
    p2 = [_dot3(a, a) for a in l16]
    p4 = [_dot3(a, a) for a in p2]
    p8 = [_dot3(a, a) for a in p4]
    xs = [eye - a for a in l16]
    for ps in (p2, p4, p8):
        xs = [x + _dot3(x, p) for x, p in zip(xs, ps)]
    for off in ([jnp.where(m32 & jnp.logical_not(m16), l, 0.0) for l in lmats],
                [jnp.where(m32, 0.0, l) for l in lmats]):
        ys = [_dot3(x, c) for x, c in zip(xs, off)]
        xs = [x - _dot3(y, x) for x, y in zip(xs, ys)]
    return xs


def _col_to_row(col, ii, jj):
    return jnp.sum(jnp.where(ii == jj, col, 0.0), axis=0, keepdims=True)


def _row_to_col(row, ii, jj):
    return jnp.sum(jnp.where(ii == jj, row, 0.0), axis=1, keepdims=True)


def _chain_common(q, k, v, graw_col, graw_row, bcol, masks):
    ii, jj, incl, strict = masks
    inclt = jnp.logical_not(strict)
    gcol = jnp.sum(jnp.where(incl, graw_row, 0.0), axis=1, keepdims=True)
    grow = jnp.sum(jnp.where(inclt, graw_col, 0.0), axis=0, keepdims=True)
    glast = jnp.sum(graw_row, axis=1, keepdims=True)
    decay = jnp.where(incl, jnp.exp(jnp.where(incl, gcol - grow, 0.0)), 0.0)
    kb = k * bcol
    vb = v * bcol
    eg = jnp.exp(gcol)
    ek = jnp.exp(glast - gcol)
    kbg = kb * eg
    amat = _dot_nt(kb, k)
    qk = _dot_nt(q, k)
    return dict(gcol=gcol, glast=glast, decay=decay, kb=kb, vb=vb, eg=eg, ek=ek, kbg=kbg, amat=amat, qk=qk,
                intra=qk * decay, qg=q * eg, kdec=k * ek)


def _gdn_fwd(qkvc, gb, gbt):
    tm, u, w, qg, kd, intra, egl = _gdn_local_fwd(qkvc, gb, gbt)
    o_f, o_b, s_f, s_b, vn_f, vn_b = _gdn_scan_fwd(u, w, qg, kd, intra, egl, qkvc.shape[0])
    return o_f, o_b, dict(tm=tm, w=w, qg=qg, kd=kd, intra=intra, egl=egl, s=(s_f, s_b), vn=(vn_f, vn_b))


N_CHAINS = 2 * GDN_HEADS


LOCAL_CHUNKS = 2


def _load_chains(x_ref, g_ref, gt_ref, cc=0):
    hd = GDN_HEADS * GDN_DIM
    rows = slice(cc * CHUNK, (cc + 1) * CHUNK)
    chains = []
    for d in range(2):
        masks = _chunk_masks(d == 0)
        for h in range(GDN_HEADS):
            ch = d * GDN_HEADS + h
            q = x_ref[rows, h * GDN_DIM:(h + 1) * GDN_DIM]
            k = x_ref[rows, hd + h * GDN_DIM:hd + (h + 1) * GDN_DIM]
            v = x_ref[rows, 2 * hd + h * GDN_DIM:2 * hd + (h + 1) * GDN_DIM]
            bcol = g_ref[rows, 8 + ch:9 + ch]
            cm = _chain_common(q, k, v, g_ref[rows, ch:ch + 1], gt_ref[cc, ch:ch + 1, :], bcol, masks)
            chains.append(dict(cm, q=q, k=k, v=v, bcol=bcol, masks=masks, ch=ch, h=h, cc=cc))
    return chains


def _chain_shape(rows, cols, dtype):
    return lambda nc: jax.ShapeDtypeStruct((nc, N_CHAINS, rows, cols), dtype)


def _gdn_local_fwd(qkvc, gb, gbt):
    t = qkvc.shape[0]
    nc = t // CHUNK
    hd = GDN_HEADS * GDN_DIM

    def body(x_ref, g_ref, gt_ref, t_ref, u_ref, w_ref, qg_ref, kd_ref, in_ref, eg_ref):
        chains = [c for cc in range(LOCAL_CHUNKS) for c in _load_chains(x_ref, g_ref, gt_ref, cc)]
        ii, jj = chains[0]["masks"][0:2]
        tms = _tri_inv_many([jnp.where(c["masks"][3], c["amat"] * c["decay"], 0.0) for c in chains], ii, jj)
        uws = [_dot(tm, jnp.concatenate([c["vb"], c["kbg"]], axis=1)) for tm, c in zip(tms, chains)]
        for c, tm, uw in zip(chains, tms, uws):
            cc, ch = c["cc"], c["ch"]
            t_ref[cc, ch] = tm
            u_ref[cc, ch] = uw[:, :GDN_DIM]
            w_ref[cc, ch] = uw[:, GDN_DIM:].astype(BF16)
            qg_ref[cc, ch] = c["qg"].astype(BF16)
            kd_ref[cc, ch] = c["kdec"].astype(BF16)
            in_ref[cc, ch] = c["intra"].astype(BF16)
            eg_ref[cc, ch:ch + 1, :] = jnp.broadcast_to(jnp.exp(c["glast"]), (1, LANES))

    lc = LOCAL_CHUNKS
    blk = lambda rows, cols: pl.BlockSpec((lc, N_CHAINS, rows, cols), lambda n: (n, 0, 0, 0))
    shapes = [_chain_shape(CHUNK, CHUNK, F32), _chain_shape(CHUNK, GDN_DIM, F32), _chain_shape(CHUNK, GDN_DIM, BF16),
              _chain_shape(CHUNK, GDN_DIM, BF16), _chain_shape(CHUNK, GDN_DIM, BF16), _chain_shape(CHUNK, CHUNK, BF16)]
    return tuple(pl.pallas_call(
        body, name="gdn_local_fwd", grid=(nc // lc,),
        in_specs=[pl.BlockSpec((lc * CHUNK, 3 * hd), lambda n: (n, 0)), pl.BlockSpec((lc * CHUNK, LANES), lambda n: (n, 0)),
                  pl.BlockSpec((lc, 16, CHUNK), lambda n: (n, 0, 0))],
        out_specs=[blk(CHUNK, CHUNK), blk(CHUNK, GDN_DIM), blk(CHUNK, GDN_DIM), blk(CHUNK, GDN_DIM),
                   blk(CHUNK, GDN_DIM), blk(CHUNK, CHUNK), pl.BlockSpec((lc, N_CHAINS, LANES), lambda n: (n, 0, 0))],
        out_shape=[s(nc) for s in shapes] + [jax.ShapeDtypeStruct((nc, N_CHAINS, LANES), F32)],
        compiler_params=_params(("arbitrary",), VMEM_LIMIT),
    )(qkvc, gb, gbt))


SCAN_CHUNKS = 4


def _dir_specs(nc, rev):
    nb = nc // SCAN_CHUNKS

    def spec(d, rows, cols, own=False):
        chunk = (lambda n: n) if (d == 0) != rev else (lambda n: nb - 1 - n)
        blk = 0 if own else d
        if rows is None:
            return pl.BlockSpec((SCAN_CHUNKS, GDN_HEADS if own else N_CHAINS, cols), lambda n: (chunk(n), 0, 0))
        return pl.BlockSpec((SCAN_CHUNKS, GDN_HEADS, rows, cols), lambda n: (chunk(n), blk, 0, 0))

    def rows_spec(d, cols):
        chunk = (lambda n: n) if (d == 0) != rev else (lambda n: nb - 1 - n)
        return pl.BlockSpec((SCAN_CHUNKS * CHUNK, cols), lambda n: (chunk(n), 0))

    def order(d):
        return list(range(SCAN_CHUNKS)) if (d == 0) != rev else list(range(SCAN_CHUNKS - 1, -1, -1))
    return spec, rows_spec, order


def _gdn_scan_fwd(u, w, qg, kd, intra, egl, t):
    nc = t // CHUNK
    hd = GDN_HEADS * GDN_DIM

    def body(*refs):
        ins, outs, state = refs[:12], refs[12:18], refs[18]
        @pl.when(pl.program_id(0) == 0)
        def _():
            state[...] = jnp.zeros_like(state)

        chains = [(d, h) for d in range(2) for h in range(GDN_HEADS)]
        states = [state[ch] for ch in range(N_CHAINS)]
        for step in range(SCAN_CHUNKS):
            at = [order(d)[step] for d in range(2)]
            pick = lambda k, d, h: ins[2 * k + d][at[d], h]
            sbs = [s.astype(BF16) for s in states]
            ws = [_dot(pick(1, d, h), sb) for (d, h), sb in zip(chains, sbs)]
            o1 = [_dot(pick(2, d, h), sb) for (d, h), sb in zip(chains, sbs)]
            vns = [(pick(0, d, h) - wsb).astype(BF16) for (d, h), wsb in zip(chains, ws)]
            o2 = [_dot(pick(4, d, h), vn) for (d, h), vn in zip(chains, vns)]
            kv = [_dot_tn(pick(3, d, h), vn) for (d, h), vn in zip(chains, vns)]
            new_states = []
            for ch, (d, h) in enumerate(chains):
                outs[d][at[d] * CHUNK:(at[d] + 1) * CHUNK, h * GDN_DIM:(h + 1) * GDN_DIM] = o1[ch] + o2[ch]
                outs[2 + d][at[d], h] = states[ch]
                outs[4 + d][at[d], h] = vns[ch]
                new_states.append(states[ch] * ins[10 + d][at[d], ch:ch + 1, :] + kv[ch])
            states = new_states
        for ch in range(N_CHAINS):
            state[ch] = states[ch]

    spec, rows_spec, order = _dir_specs(nc, False)
    pair = lambda rows, cols, own=False: [spec(0, rows, cols, own), spec(1, rows, cols, own)]
    s_shape = jax.ShapeDtypeStruct((nc, GDN_HEADS, GDN_DIM, GDN_DIM), F32)
    vn_shape = jax.ShapeDtypeStruct((nc, GDN_HEADS, CHUNK, GDN_DIM), BF16)
    return pl.pallas_call(
        body, name="gdn_scan_fwd", grid=(nc // SCAN_CHUNKS,),
        in_specs=(pair(CHUNK, GDN_DIM) + pair(CHUNK, GDN_DIM) + pair(CHUNK, GDN_DIM) + pair(CHUNK, GDN_DIM)
                  + pair(CHUNK, CHUNK) + pair(None, LANES)),
        out_specs=([rows_spec(0, hd), rows_spec(1, hd)] + pair(GDN_DIM, GDN_DIM, True)
                   + pair(CHUNK, GDN_DIM, True)),
        out_shape=[jax.ShapeDtypeStruct((t, hd), F32), jax.ShapeDtypeStruct((t, hd), F32),
                   s_shape, s_shape, vn_shape, vn_shape],
        scratch_shapes=[pltpu.VMEM((N_CHAINS, GDN_DIM, GDN_DIM), F32)],
        compiler_params=_params(("arbitrary",), VMEM_LIMIT),
    )(u, u, w, w, qg, qg, kd, kd, intra, intra, egl, egl)


def _gdn_bwd(qkvc, gb, gbt, do, saved, exchange=None):
    scan = _gdn_scan_bwd(do, saved, qkvc.shape[0])
    return _gdn_local_bwd(qkvc, gb, gbt, do, saved, scan, exchange)


def _gdn_scan_bwd(do, saved, t):
    nc = t // CHUNK
    hd = GDN_HEADS * GDN_DIM

    def body(*refs):
        ins, outs, dstate = refs[:16], refs[16:26], refs[26]
        @pl.when(pl.program_id(0) == 0)
        def _():
            dstate[...] = jnp.zeros_like(dstate)

        chains = [(d, h) for d in range(2) for h in range(GDN_HEADS)]
        dss = [dstate[ch] for ch in range(N_CHAINS)]
        for step in range(SCAN_CHUNKS):
            at = [order(d)[step] for d in range(2)]
            pick = lambda k, d, h: ins[2 * k + d][at[d], h]
            dsbs = [ds.astype(BF16) for ds in dss]
            ss = [pick(1, d, h) for d, h in chains]
            sbs = [s.astype(BF16) for s in ss]
            dos = [ins[d][at[d] * CHUNK:(at[d] + 1) * CHUNK, h * GDN_DIM:(h + 1) * GDN_DIM].astype(BF16)
                   for d, h in chains]
            dv1 = [_dot_tn(pick(5, d, h), dov) for (d, h), dov in zip(chains, dos)]
            dv2 = [_dot(pick(4, d, h), dsb) for (d, h), dsb in zip(chains, dsbs)]
            ds1 = [_dot_tn(pick(3, d, h), dov) for (d, h), dov in zip(chains, dos)]
            dkds = [_dot_nt(pick(6, d, h), dsb) for (d, h), dsb in zip(chains, dsbs)]
            dqgs = [_dot_nt(dov, sb) for dov, sb in zip(dos, sbs)]
            dvns = [(a + b).astype(BF16) for a, b in zip(dv1, dv2)]
            ds2 = [_dot_tn(pick(2, d, h), dvn) for (d, h), dvn in zip(chains, dvns)]
            dws = [_dot_nt(dvn, sb) for dvn, sb in zip(dvns, sbs)]
            new_dss = []
            for ch, (d, h) in enumerate(chains):
                egl = ins[14 + d][at[d], ch:ch + 1, :]
                outs[d][at[d], h] = dvns[ch]
                outs[2 + d][at[d], h] = (-dws[ch]).astype(BF16)
                outs[4 + d][at[d], h] = dqgs[ch]
                outs[6 + d][at[d], h] = dkds[ch]
                outs[8 + d][at[d], h:h + 1, :] = egl * jnp.sum(jnp.sum(ss[ch] * dss[ch], axis=1, keepdims=True),
                                                               axis=0, keepdims=True)
                new_dss.append(ds1[ch] + egl * dss[ch] - ds2[ch])
            dss = new_dss
        for ch in range(N_CHAINS):
            dstate[ch] = dss[ch]

    spec, rows_spec, order = _dir_specs(nc, True)
    pair = lambda rows, cols, own=False: [spec(0, rows, cols, own), spec(1, rows, cols, own)]
    s_f, s_b = saved["s"]
    vn_f, vn_b = saved["vn"]
    w, qg, kd, intra, egl = saved["w"], saved["qg"], saved["kd"], saved["intra"], saved["egl"]
    own = lambda rows, cols, dtype: jax.ShapeDtypeStruct((nc, GDN_HEADS, rows, cols), dtype)
    row_shape = jax.ShapeDtypeStruct((nc, GDN_HEADS, LANES), F32)
    return pl.pallas_call(
        body, name="gdn_scan_bwd", grid=(nc // SCAN_CHUNKS,),
        in_specs=([rows_spec(0, hd), rows_spec(1, hd)] + pair(GDN_DIM, GDN_DIM, True) + pair(CHUNK, GDN_DIM)
                  + pair(CHUNK, GDN_DIM) + pair(CHUNK, GDN_DIM) + pair(CHUNK, CHUNK) + pair(CHUNK, GDN_DIM, True)
                  + pair(None, LANES)),
        out_specs=(pair(CHUNK, GDN_DIM, True) + pair(CHUNK, GDN_DIM, True) + pair(CHUNK, GDN_DIM, True)
                   + pair(CHUNK, GDN_DIM, True) + pair(None, LANES, True)),
        out_shape=[own(CHUNK, GDN_DIM, BF16)] * 4 + [own(CHUNK, GDN_DIM, F32)] * 4 + [row_shape] * 2,
        scratch_shapes=[pltpu.VMEM((N_CHAINS, GDN_DIM, GDN_DIM), F32)],
        compiler_params=_params(("arbitrary",), VMEM_LIMIT),
    )(do, do, s_f, s_b, w, w, qg, qg, kd, kd, intra, intra, vn_f, vn_b, egl, egl)


def _dot3_nt(a, b):
    ah = a.astype(BF16)
    al = (a - ah.astype(F32)).astype(BF16)
    bh = b.astype(BF16)
    bl = (b - bh.astype(F32)).astype(BF16)
    return _dot_nt(ah, bh) + (_dot_nt(ah, bl) + _dot_nt(al, bh))


def _dot3_tn(a, b):
    ah = a.astype(BF16)
    al = (a - ah.astype(F32)).astype(BF16)
    bh = b.astype(BF16)
    bl = (b - bh.astype(F32)).astype(BF16)
    return _dot_tn(ah, bh) + (_dot_tn(ah, bl) + _dot_tn(al, bh))


def _gdn_local_bwd(qkvc, gb, gbt, do, saved, scan, exchange=None):
    t = qkvc.shape[0]
    nc = t // CHUNK
    hd = GDN_HEADS * GDN_DIM

    def body(*refs):
        x_ref, g_ref, gt_ref, do_ref, t_ref = refs[:5]
        per_dir = refs[5:17]
        dx_ref, dg_ref = refs[17:]
        chains = [c for cc in range(LOCAL_CHUNKS) for c in _load_chains(x_ref, g_ref, gt_ref, cc)]
        lane = lax.broadcasted_iota(jnp.int32, (CHUNK, LANES), 1)
        dgates = [jnp.zeros((CHUNK, LANES), F32) for _ in range(LOCAL_CHUNKS)]
        for c in chains:
            d = c["ch"] // GDN_HEADS
            vn_ref, dvn_ref, dw_ref, dqg_ref, dkd_ref, dgl_ref = per_dir[d::2]
            h, cc = c["h"], c["cc"]
            rows = slice(cc * CHUNK, (cc + 1) * CHUNK)
            c.update(tm=t_ref[cc, c["ch"]], dov=do_ref[rows, h * GDN_DIM:(h + 1) * GDN_DIM], vnew=vn_ref[cc, h],
                     dvnew=dvn_ref[cc, h], dw=dw_ref[cc, h], dqg=dqg_ref[cc, h], dkdec=dkd_ref[cc, h],
                     dglast=dgl_ref[cc, h:h + 1, 0:1])
        dintras = [_dot_nt(c["dov"], c["vnew"]) for c in chains]
        dts = [_dot_nt(c["dvnew"], c["vb"]) + _dot_nt(c["dw"], c["kbg"]) for c in chains]
        dvbs = [_dot_tn(c["tm"], c["dvnew"]) for c in chains]
        dkbgs = [_dot_tn(c["tm"], c["dw"]) for c in chains]
        tdts = [_dot3_nt(dt, c["tm"]) for dt, c in zip(dts, chains)]
        dls = [jnp.where(c["masks"][3], -_dot3_tn(c["tm"], tdt), 0.0) for tdt, c in zip(tdts, chains)]
        das = [dl * c["decay"] for dl, c in zip(dls, chains)]
        dqks = [jnp.where(c["masks"][2], di, 0.0) * c["decay"] for di, c in zip(dintras, chains)]
        dkb1 = [_dot(da, c["k"]) for da, c in zip(das, chains)]
        dk1 = [_dot_tn(da, c["kb"]) for da, c in zip(das, chains)]
        dk2 = [_dot_tn(dqk, c["q"]) for dqk, c in zip(dqks, chains)]
        dq1 = [_dot(dqk, c["k"]) for dqk, c in zip(dqks, chains)]
        grads, mms, p_gs, p_betas, p_kds = [], [], [], [], []
        for n, c in enumerate(chains):
            incl = c["masks"][2]
            dkb = dkb1[n] + dkbgs[n] * c["eg"]
            kd = c["dkdec"] * c["kdec"]
            mms.append((dls[n] * c["amat"] + jnp.where(incl, dintras[n], 0.0) * c["qk"]) * c["decay"])
            p_gs.append(c["dqg"] * c["qg"] - kd + dkbgs[n] * c["kbg"])
            p_betas.append(dkb * c["k"] + dvbs[n] * c["v"])
            p_kds.append(kd)
            grads.append((dq1[n] + c["dqg"] * c["eg"],
                          dk1[n] + dk2[n] + c["dkdec"] * c["ek"] + dkb * c["bcol"],
                          dvbs[n] * c["bcol"]))
        row_sums = [jnp.sum(mm, axis=1, keepdims=True) for mm in mms]
        col_sums = [jnp.sum(mm, axis=0, keepdims=True) for mm in mms]
        g_sums = [jnp.sum(pg, axis=1, keepdims=True) for pg in p_gs]
        dbetas = [jnp.sum(pb, axis=1, keepdims=True) for pb in p_betas]
        kd_tots = [jnp.sum(jnp.sum(pk, axis=1, keepdims=True), axis=0, keepdims=True) for pk in p_kds]
        dgcs = [rs - _row_to_col(cs, *c["masks"][0:2]) + gs for rs, cs, gs, c in zip(row_sums, col_sums, g_sums, chains)]
        dgrs = [_col_to_row(dgc, *c["masks"][0:2]) for dgc, c in zip(dgcs, chains)]
        draws = [jnp.sum(jnp.where(jnp.logical_not(c["masks"][3]), dgr, 0.0), axis=1, keepdims=True) + c["dglast"] + kt
                 for dgr, kt, c in zip(dgrs, kd_tots, chains)]
        for c, draw, dbeta in zip(chains, draws, dbetas):
            ch = c["ch"]
            dgates[c["cc"]] = dgates[c["cc"]] + jnp.where(lane == ch, draw, 0.0) + jnp.where(lane == 8 + ch, dbeta, 0.0)
        for cc in range(LOCAL_CHUNKS):
            rows = slice(cc * CHUNK, (cc + 1) * CHUNK)
            for h in range(GDN_HEADS):
                for part in range(3):
                    cols = slice(part * hd + h * GDN_DIM, part * hd + (h + 1) * GDN_DIM)
                    dx_ref[rows, cols] = grads[cc * N_CHAINS + h][part] + grads[cc * N_CHAINS + GDN_HEADS + h][part]
            dg_ref[rows, :] = dgates[cc]

    lc = LOCAL_CHUNKS
    all8 = lambda rows, cols: pl.BlockSpec((lc, N_CHAINS, rows, cols), lambda n: (n, 0, 0, 0))
    own4 = lambda rows, cols: pl.BlockSpec((lc, GDN_HEADS, rows, cols), lambda n: (n, 0, 0, 0))
    row4 = pl.BlockSpec((lc, GDN_HEADS, LANES), lambda n: (n, 0, 0))
    vn_f, vn_b = saved["vn"]
    dvn_f, dvn_b, dw_f, dw_b, dqg_f, dqg_b, dkd_f, dkd_b, dgl_f, dgl_b = scan
    return _grid_call(
        body, "gdn_local_bwd", nc // lc,
        [pl.BlockSpec((lc * CHUNK, 3 * hd), lambda n: (n, 0)), pl.BlockSpec((lc * CHUNK, LANES), lambda n: (n, 0)),
         pl.BlockSpec((lc, 16, CHUNK), lambda n: (n, 0, 0)), pl.BlockSpec((lc * CHUNK, hd), lambda n: (n, 0)),
         all8(CHUNK, CHUNK)] + [own4(CHUNK, GDN_DIM)] * 10 + [row4, row4],
        [pl.BlockSpec((lc * CHUNK, 3 * hd), lambda n: (n, 0)), pl.BlockSpec((lc * CHUNK, LANES), lambda n: (n, 0))],
        [jax.ShapeDtypeStruct((t, 3 * hd), F32), jax.ShapeDtypeStruct((t, LANES), F32)],
        (qkvc, gb, gbt, do, saved["tm"], vn_f, vn_b, dvn_f, dvn_b, dw_f, dw_b, dqg_f, dqg_b, dkd_f, dkd_b, dgl_f, dgl_b),
        exchange=exchange)


def _gdn_post_fwd(of, ob, z, gw, tm):
    t, hd = of.shape

    def body(of_ref, ob_ref, z_ref, w_ref, o_ref):
        for h in range(GDN_HEADS):
            cols = slice(h * GDN_DIM, (h + 1) * GDN_DIM)
            o = of_ref[:, cols] + ob_ref[:, cols]
            zv = z_ref[:, cols]
            o_ref[:, cols] = (o * _rstd(o) * w_ref[...] * (zv * _sigmoid(zv))).astype(BF16)

    row = pl.BlockSpec((tm, hd), lambda i: (i, 0))
    return pl.pallas_call(
        body, name="gdn_post_fwd", grid=(t // tm,),
        in_specs=[row, row, row, _resident((1, GDN_DIM))],
        out_specs=row, out_shape=jax.ShapeDtypeStruct((t, hd), BF16),
        compiler_params=_params(("arbitrary",), VMEM_LIMIT),
    )(of, ob, z, gw)


def _gdn_post_bwd(doa, of, ob, z, gw, tm):
    t, hd = of.shape

    def body(d_ref, of_ref, ob_ref, z_ref, w_ref, do_ref, dz_ref, dw_ref):
        @pl.when(pl.program_id(0) == 0)
        def _():
            dw_ref[...] = jnp.zeros_like(dw_ref)

        dw = jnp.zeros((1, GDN_DIM), F32)
        for h in range(GDN_HEADS):
            cols = slice(h * GDN_DIM, (h + 1) * GDN_DIM)
            o = of_ref[:, cols] + ob_ref[:, cols]
            zv = z_ref[:, cols]
            dv = d_ref[:, cols]
            r = _rstd(o)
            sg = _sigmoid(zv)
            on = o * r * w_ref[...]
            dz_ref[:, cols] = (dv * on * (sg * (1.0 + zv * (1.0 - sg)))).astype(BF16)
            dxr, dwh = _rms_bwd(o, r, w_ref[...], dv * (zv * sg))
            do_ref[:, cols] = dxr
            dw = dw + dwh
        dw_ref[...] += dw

    row = pl.BlockSpec((tm, hd), lambda i: (i, 0))
    return pl.pallas_call(
        body, name="gdn_post_bwd", grid=(t // tm,),
        in_specs=[row, row, row, row, _resident((1, GDN_DIM))],
        out_specs=[row, row, pl.BlockSpec((1, GDN_DIM), lambda i: (0, 0))],
        out_shape=[jax.ShapeDtypeStruct((t, hd), F32), jax.ShapeDtypeStruct((t, hd), BF16),
                   jax.ShapeDtypeStruct((1, GDN_DIM), F32)],
        compiler_params=_params(("arbitrary",), VMEM_LIMIT),
    )(doa, of, ob, z, gw)


SWA_W = SWA_HEADS * SWA_DIM
QBLK = 128
KWIN = QBLK + 2 * RADIUS
WIN_OFFSETS = (0, RADIUS, 2 * RADIUS)


def _t5_bucket(rel):
    nb = REL_BUCKETS // 2
    bucket = (rel > 0).astype(np.int32) * nb
    n = np.abs(rel)
    max_exact = nb // 2
    large = max_exact + (np.log(np.maximum(n, 1) / max_exact)
                         / math.log(REL_MAX_DISTANCE / max_exact) * (nb - max_exact)).astype(np.int32)
    large = np.minimum(large, nb - 1)
    return (bucket + np.where(n < max_exact, n, large)).astype(np.int32)


def _band_tables(dilation):
    a = np.arange(QBLK)
    b = np.arange(KWIN)
    rel = np.stack([b[None, :] - w0 - a[:, None] for w0 in WIN_OFFSETS])
    return np.where(np.abs(rel) <= RADIUS, _t5_bucket(rel * dilation), -1).astype(np.int32)


BAND_CELLS = len(WIN_OFFSETS) * QBLK * KWIN


def _band_index():
    return jnp.asarray(np.concatenate([_band_tables(d).reshape(-1) for _, d in PATTERNS])[None, :])


def _onehot(idx, dtype):
    return (lax.broadcasted_iota(jnp.int32, (REL_BUCKETS, idx.shape[1]), 0) == idx).astype(dtype)


def _bias_tables(rel_bias, idx, tk):
    n = idx.shape[1]

    def body(rb_ref, i_ref, o_ref):
        iv = i_ref[...]
        o_ref[...] = jnp.where(iv < 0, NEG_BIG, _dot_hi(rb_ref[...], _onehot(iv, F32)))

    return pl.pallas_call(
        body, name="bias_tables", grid=(n // tk,),
        in_specs=[_resident((SWA_HEADS, REL_BUCKETS)), pl.BlockSpec((1, tk), lambda k: (0, k))],
        out_specs=pl.BlockSpec((SWA_HEADS, tk), lambda k: (0, k)),
        out_shape=jax.ShapeDtypeStruct((SWA_HEADS, n), F32),
        compiler_params=_params(("arbitrary",), VMEM_LIMIT),
    )(rel_bias.T, idx)


def _head_mean(x2, bd_ref):
    return _dot_hi(x2, bd_ref[...])


VIEW_DILATIONS = tuple(d for _, d in PATTERNS if d > 1)


def _view_spec(tm, d):
    return pl.BlockSpec((tm // d, d * SWA_W), lambda i: (i, 0))


def _view_shape(t, d, dtype):
    return jax.ShapeDtypeStruct((t // d, d * SWA_W), dtype)


N_GROUPS = SWA_W // LANES


def _to_view(src_ref, idx, dst_ref, d, rows):
    for r in range(d):
        for g in range(N_GROUPS):
            cols = slice(r * SWA_W + g * LANES, r * SWA_W + (g + 1) * LANES)
            dst_ref[:, cols] = src_ref[idx, g, pl.ds(r, rows // d, stride=d), :].astype(dst_ref.dtype)


def _from_view(src_ref, dst_ref, idx, d, rows):
    for r in range(d):
        for g in range(N_GROUPS):
            cols = slice(r * SWA_W + g * LANES, r * SWA_W + (g + 1) * LANES)
            dst_ref[idx, g, pl.ds(r, rows // d, stride=d), :] = src_ref[:, cols]


def _swa_prep_fwd(qkvb, qw, kw, bd, tm):
    t = qkvb.shape[0]

    def body(x_ref, qw_ref, kw_ref, bd_ref, *rest):
        outs, sc = rest[:-1], rest[-1]
        for gidx in range(N_GROUPS):
            cols = slice(gidx * LANES, (gidx + 1) * LANES)
            xq = x_ref[:, cols]
            sc[0, gidx] = xq * lax.rsqrt(_head_mean(xq * xq, bd_ref) + EPS) * qw_ref[:, cols] * (SWA_DIM ** -0.5)
            xk = x_ref[:, SWA_W + gidx * LANES:SWA_W + (gidx + 1) * LANES]
            sc[1, gidx] = xk * lax.rsqrt(_head_mean(xk * xk, bd_ref) + EPS) * kw_ref[:, cols]
            sc[2, gidx] = x_ref[:, 2 * SWA_W + gidx * LANES:2 * SWA_W + (gidx + 1) * LANES]
            for i in range(3):
                outs[i][:, cols] = sc[i, gidx].astype(BF16)
        for i in range(3):
            for n, d in enumerate(VIEW_DILATIONS):
                _to_view(sc, i, outs[3 * (n + 1) + i], d, tm)

    return pl.pallas_call(
        body, name="swa_prep_fwd", grid=(t // tm,),
        in_specs=[pl.BlockSpec((tm, 3 * SWA_W), lambda i: (i, 0)), _resident((1, SWA_W)), _resident((1, SWA_W)),
                  _resident((LANES, LANES))],
        out_specs=[_view_spec(tm, d) for d in (1,) + VIEW_DILATIONS for _ in range(3)],
        out_shape=[_view_shape(t, d, BF16) for d in (1,) + VIEW_DILATIONS for _ in range(3)],
        scratch_shapes=[pltpu.VMEM((3, N_GROUPS, tm, LANES), F32)],
        compiler_params=_params(("arbitrary",), VMEM_LIMIT),
    )(qkvb, qw, kw, bd)


def _swa_prep_bwd(qkvb, qw, kw, bd, grads, tm):
    t = qkvb.shape[0]

    def body(x_ref, qw_ref, kw_ref, bd_ref, *rest):
        parts, (dx_ref, dqw_ref, dkw_ref, sc) = rest[:9], rest[9:]
        @pl.when(pl.program_id(0) == 0)
        def _():
            dqw_ref[...] = jnp.zeros_like(dqw_ref)
            dkw_ref[...] = jnp.zeros_like(dkw_ref)

        for i in range(3):
            for n, d in enumerate(VIEW_DILATIONS):
                _from_view(parts[3 * (n + 1) + i], sc, 2 * i + n, d, tm)
        for gidx in range(N_GROUPS):
            cols = slice(gidx * LANES, (gidx + 1) * LANES)
            for i, base, w_ref, dw_ref, scale in ((0, 0, qw_ref, dqw_ref, SWA_DIM ** -0.5),
                                                  (1, SWA_W, kw_ref, dkw_ref, 1.0)):
                xv = x_ref[:, base + gidx * LANES:base + (gidx + 1) * LANES]
                dy = (parts[i][:, cols] + sc[2 * i, gidx] + sc[2 * i + 1, gidx]) * scale
                r = lax.rsqrt(_head_mean(xv * xv, bd_ref) + EPS)
                xhat = xv * r
                dxh = dy * w_ref[:, cols]
                dx = r * (dxh - xhat * _head_mean(dxh * xhat, bd_ref))
                dx_ref[:, base + gidx * LANES:base + (gidx + 1) * LANES] = dx.astype(BF16)
                dw_ref[:, cols] += jnp.sum(dy * xhat, axis=0, keepdims=True)
            dx_ref[:, 2 * SWA_W + gidx * LANES:2 * SWA_W + (gidx + 1) * LANES] = (
                parts[2][:, cols] + sc[4, gidx] + sc[5, gidx]).astype(BF16)

    wrow = pl.BlockSpec((1, SWA_W), lambda i: (0, 0))
    return pl.pallas_call(
        body, name="swa_prep_bwd", grid=(t // tm,),
        in_specs=[pl.BlockSpec((tm, 3 * SWA_W), lambda i: (i, 0)), _resident((1, SWA_W)), _resident((1, SWA_W)),
                  _resident((LANES, LANES))] + [_view_spec(tm, d) for d in (1,) + VIEW_DILATIONS for _ in range(3)],
        out_specs=[pl.BlockSpec((tm, 3 * SWA_W), lambda i: (i, 0)), wrow, wrow],
        out_shape=[jax.ShapeDtypeStruct((t, 3 * SWA_W), BF16), jax.ShapeDtypeStruct((1, SWA_W), F32),
                   jax.ShapeDtypeStruct((1, SWA_W), F32)],
        scratch_shapes=[pltpu.VMEM((6, N_GROUPS, tm, LANES), F32)],
        compiler_params=_params(("arbitrary",), VMEM_LIMIT),
    )(qkvb, qw, kw, bd, *grads)


def _aligned(v, m):
    return v if isinstance(v, int) else pl.multiple_of(v, m)


BAND_GROUP = 2


def _band_loop(nsub, length, step):
    step([(0, 0)], 0)
    if nsub > 2:
        assert (nsub - 2) % BAND_GROUP == 0

        def inner(i, carry):
            s0 = 1 + i * BAND_GROUP
            step([(s0 + e, pl.multiple_of((s0 + e) * QBLK - RADIUS, RADIUS)) for e in range(BAND_GROUP)], 1)
            return carry
        lax.fori_loop(0, (nsub - 2) // BAND_GROUP, inner, 0)
    step([(nsub - 1, length - KWIN)], 2)


def _head_select(lane, a0, a1):
    return jnp.where(lane < SWA_DIM, a0, a1)


def _swa_fwd(qv, kv, vv, bias, dilation, name):
    length = qv.shape[0]
    nsub = length // QBLK
    assert nsub >= 2 and length % QBLK == 0

    def body(q_ref, k_ref, v_ref, b_ref, o_ref, l_ref):
        lane = lax.broadcasted_iota(jnp.int32, (QBLK, LANES), 1)

        def step(blocks, var):
            items = []
            for s, ws in blocks:
                rows = pl.ds(_aligned(s * QBLK, QBLK), QBLK)
                q, kk, vw = q_ref[rows, :], k_ref[pl.ds(ws, KWIN), :], v_ref[pl.ds(ws, KWIN), :]
                for hh in range(2):
                    items.append((hh, jnp.where((lane < SWA_DIM) == (hh == 0), q, jnp.zeros_like(q)), kk, vw))
            lgs = [_dot_nt(qh, kk) + b_ref[hh, var] for hh, qh, kk, _ in items]
            ms = [jnp.max(lg, axis=-1, keepdims=True) for lg in lgs]
            ps = [jnp.exp(lg - m) for lg, m in zip(lgs, ms)]
            dens = [jnp.sum(p, axis=-1, keepdims=True) for p in ps]
            pvs = [_dot(p, it[3]) for p, it in zip(ps, items)]
            for n, (s, _) in enumerate(blocks):
                rows = pl.ds(_aligned(s * QBLK, QBLK), QBLK)
                o0, o1 = (pvs[2 * n + hh] / dens[2 * n + hh] for hh in range(2))
                l0, l1 = (ms[2 * n + hh] + jnp.log(dens[2 * n + hh]) for hh in range(2))
                o_ref[rows, :] = _head_select(lane, o0, o1)
                l_ref[rows, :] = _head_select(lane, l0, l1)

        _band_loop(nsub, length, step)

    blk = pl.BlockSpec((length, LANES), lambda hp, r: (0, r * (SWA_W // LANES) + hp))
    shp = jax.ShapeDtypeStruct(qv.shape, F32)
    return pl.pallas_call(
        body, name=name, grid=(SWA_W // LANES, dilation),
        in_specs=[blk, blk, blk, pl.BlockSpec((2, 3, QBLK, KWIN), lambda hp, r: (hp, 0, 0, 0))],
        out_specs=[blk, blk], out_shape=[shp, shp],
        compiler_params=_params(("arbitrary", "arbitrary"), VMEM_LIMIT),
    )(qv, kv, vv, bias)


def _swa_combine(os_, ls_, tm):
    t = os_[0].shape[0]

    def body(o0, o1, o2, l0, l1, l2, o_ref, ob_ref, la_ref, lb_ref, lc_ref, sc):
        for n, d in enumerate(VIEW_DILATIONS):
            _from_view((o1, o2)[n], sc, n, d, tm)
            _from_view((l1, l2)[n], sc, 2 + n, d, tm)
        for g in range(N_GROUPS):
            cols = slice(g * LANES, (g + 1) * LANES)
            la, lb, lc = l0[:, cols], sc[2, g], sc[3, g]
            m = jnp.maximum(jnp.maximum(la, lb), lc)
            tot = m + jnp.log(jnp.exp(la - m) + jnp.exp(lb - m) + jnp.exp(lc - m))
            o = jnp.exp(la - tot) * o0[:, cols] + jnp.exp(lb - tot) * sc[0, g] + jnp.exp(lc - tot) * sc[1, g]
            o_ref[:, cols] = o
            ob_ref[:, cols] = o.astype(BF16)
            la_ref[:, cols] = tot
            sc[4, g] = tot
        for n, d in enumerate(VIEW_DILATIONS):
            _to_view(sc, 4, (lb_ref, lc_ref)[n], d, tm)

    specs = [_view_spec(tm, d) for d in (1,) + VIEW_DILATIONS]
    return pl.pallas_call(
        body, name="swa_combine", grid=(t // tm,), in_specs=specs + specs, out_specs=[specs[0], specs[0]] + specs,
        out_shape=[jax.ShapeDtypeStruct((t, SWA_W), F32), jax.ShapeDtypeStruct((t, SWA_W), BF16)]
                  + [_view_shape(t, d, F32) for d in (1,) + VIEW_DILATIONS],
        scratch_shapes=[pltpu.VMEM((5, N_GROUPS, tm, LANES), F32)],
        compiler_params=_params(("arbitrary",), VMEM_LIMIT),
    )(*os_, *ls_)


def _swa_bwd_prep(do, o, bd, tm):
    t = do.shape[0]

    def body(d_ref, o_ref, bd_ref, dd1, dd4, dd16, db1, db4, db16, sc):
        for gidx in range(N_GROUPS):
            cols = slice(gidx * LANES, (gidx + 1) * LANES)
            dv = d_ref[:, cols]
            dd = _head_mean(dv * o_ref[:, cols], bd_ref) * float(SWA_DIM)
            sc[0, gidx] = dd
            sc[1, gidx] = dv
            dd1[:, cols] = dd
            db1[:, cols] = dv.astype(BF16)
        for n, d in enumerate(VIEW_DILATIONS):
            _to_view(sc, 0, (dd4, dd16)[n], d, tm)
            _to_view(sc, 1, (db4, db16)[n], d, tm)

    specs = [_view_spec(tm, d) for d in (1,) + VIEW_DILATIONS]
    return pl.pallas_call(
        body, name="swa_bwd_prep", grid=(t // tm,), in_specs=[specs[0], specs[0], _resident((LANES, LANES))],
        out_specs=specs + specs,
        out_shape=[_view_shape(t, d, F32) for d in (1,) + VIEW_DILATIONS]
                  + [_view_shape(t, d, BF16) for d in (1,) + VIEW_DILATIONS],
        scratch_shapes=[pltpu.VMEM((2, N_GROUPS, tm, LANES), F32)],
        compiler_params=_params(("arbitrary",), VMEM_LIMIT),
    )(do, o, bd)


def _swa_bwd(qv, kv, vv, dov, lv, ddv, bias_a, dilation, name):
    length = qv.shape[0]
    nsub = length // QBLK
    single = pl.Buffered(1) if dilation == 1 else None

    def body(q_ref, k_ref, v_ref, do_ref, l_ref, dd_ref, ba_ref, dq_ref, dk_ref, dv_ref, db_ref):
        @pl.when(pl.program_id(1) == 0)
        def _():
            db_ref[...] = jnp.zeros_like(db_ref)

        lane = lax.broadcasted_iota(jnp.int32, (QBLK, LANES), 1)
        lanew = lax.broadcasted_iota(jnp.int32, (KWIN, LANES), 1)

        def step(blocks, var):
            items = []
            for s, ws in blocks:
                rows = pl.ds(_aligned(s * QBLK, QBLK), QBLK)
                win = pl.ds(ws, KWIN)
                q, dov_ = q_ref[rows, :], do_ref[rows, :]
                kk, vw = k_ref[win, :], v_ref[win, :]
                lse, dd = l_ref[rows, :], dd_ref[rows, :]
                for hh in range(2):
                    mine = (lane < SWA_DIM) == (hh == 0)
                    col = slice(hh * SWA_DIM, hh * SWA_DIM + 1)
                    items.append((hh, jnp.where(mine, q, jnp.zeros_like(q)), jnp.where(mine, dov_, jnp.zeros_like(dov_)),
                                  kk, vw, lse[:, col], dd[:, col], q, dov_))
            lgs = [_dot_nt(it[1], it[3]) + ba_ref[it[0], var] for it in items]
            dps = [_dot_nt(it[2], it[4]) for it in items]
            ps = [jnp.exp(lg - it[5]) for lg, it in zip(lgs, items)]
            dss = [p * (dp - it[6]) for p, dp, it in zip(ps, dps, items)]
            dqs = [_dot(ds, it[3]) for ds, it in zip(dss, items)]
            dks = [_dot_tn(ds, it[7]) for ds, it in zip(dss, items)]
            dvs = [_dot_tn(p, it[8]) for p, it in zip(ps, items)]
            for n, (s, ws) in enumerate(blocks):
                rows = pl.ds(_aligned(s * QBLK, QBLK), QBLK)
                win = pl.ds(ws, KWIN)
                dq_ref[rows, :] = _head_select(lane, dqs[2 * n], dqs[2 * n + 1])
                dk_ref[win, :] += _head_select(lanew, dks[2 * n], dks[2 * n + 1])
                dv_ref[win, :] += _head_select(lanew, dvs[2 * n], dvs[2 * n + 1])
            for hh in range(2):
                tot = dss[hh]
                for n in range(1, len(blocks)):
                    tot = tot + dss[2 * n + hh]
                db_ref[hh, var] += tot

        dk_ref[...] = jnp.zeros_like(dk_ref)
        dv_ref[...] = jnp.zeros_like(dv_ref)
        _band_loop(nsub, length, step)

    imap = lambda hp, r: (0, r * (SWA_W // LANES) + hp)
    blk_in = pl.BlockSpec((length, LANES), imap, pipeline_mode=single)
    blk_out = pl.BlockSpec((length, LANES), imap)
    shp = jax.ShapeDtypeStruct(qv.shape, F32)
    return pl.pallas_call(
        body, name=name, grid=(SWA_W // LANES, dilation),
        in_specs=[blk_in] * 6 + [pl.BlockSpec((2, 3, QBLK, KWIN), lambda hp, r: (hp, 0, 0, 0))],
        out_specs=[blk_out, blk_out, blk_out, pl.BlockSpec((2, 3, QBLK, KWIN), lambda hp, r: (hp, 0, 0, 0))],
        out_shape=[shp, shp, shp, jax.ShapeDtypeStruct((SWA_HEADS, 3, QBLK, KWIN), F32)],
        compiler_params=_params(("arbitrary", "arbitrary"), VMEM_LIMIT),
    )(qv, kv, vv, dov, lv, ddv, bias_a)


def _bias_grad(ds2, idx, tk):
    n = ds2.shape[1]
    nk = n // tk

    def body(a_ref, i_ref, o_ref):
        @pl.when(pl.program_id(0) == 0)
        def _():
            o_ref[...] = jnp.zeros_like(o_ref)

        oh = _onehot(i_ref[...], BF16)
        rest = a_ref[...]
        acc = jnp.zeros((SWA_HEADS, REL_BUCKETS), F32)
        for _ in range(3):
            piece = rest.astype(BF16)
            acc = acc + _dot_nt(piece, oh)
            rest = rest - piece.astype(F32)
        o_ref[...] += acc

    return pl.pallas_call(
        body, name="bias_grad", grid=(nk,),
        in_specs=[pl.BlockSpec((SWA_HEADS, tk), lambda k: (0, k)), pl.BlockSpec((1, tk), lambda k: (0, k))],
        out_specs=pl.BlockSpec((SWA_HEADS, REL_BUCKETS), lambda k: (0, 0)),
        out_shape=jax.ShapeDtypeStruct((SWA_HEADS, REL_BUCKETS), F32),
        compiler_params=_params(("arbitrary",), VMEM_LIMIT),
    )(ds2, idx)


def _swa_branch_fwd(qkvb, qw_t, kw_t, rel_bias, bd, tm):
    qkv = _swa_prep_fwd(qkvb, qw_t, kw_t, bd, tm)
    tables = _bias_tables(rel_bias, _band_index(), 8192)
    os_, ls_, tabs = [], [], []
    for n, (_, d) in enumerate(PATTERNS):
        bias = tables[:, n * BAND_CELLS:(n + 1) * BAND_CELLS].reshape(SWA_HEADS, len(WIN_OFFSETS), QBLK, KWIN)
        o_p, l_p = _swa_fwd(*qkv[3 * n:3 * n + 3], bias, d, f"swa_fwd_d{d}")
        os_.append(o_p)
        ls_.append(l_p)
        tabs.append(bias)
    o, o16, *lses = _swa_combine(os_, ls_, tm)
    return o, o16, (qkv, lses, tabs)


def _swa_branch_bwd(do, o, saved, qkvb, qw_t, kw_t, bd, tm):
    qkv, lses, tabs = saved
    prep = _swa_bwd_prep(do, o, bd, tm)
    grads, dss = [], []
    for n, ((_, d), bias) in enumerate(zip(PATTERNS, tabs)):
        dq, dk, dv, ds = _swa_bwd(*qkv[3 * n:3 * n + 3], prep[3 + n], lses[n], prep[n], bias, d, f"swa_bwd_d{d}")
        grads += [dq, dk, dv]
        dss.append(ds.reshape(SWA_HEADS, -1))
    dqkvb, dqw, dkw = _swa_prep_bwd(qkvb, qw_t, kw_t, bd, grads, tm)
    dbias = _bias_grad(jnp.concatenate(dss, axis=1), _band_index(), 8192)
    fold = lambda w: jnp.sum(w.reshape(SWA_HEADS, SWA_DIM), axis=0)
    return dqkvb, fold(dqw), fold(dkw), dbias.T


def _mesh_pos():
    return lax.axis_index("x"), lax.axis_index("y"), lax.axis_index("c")


def _other_chips(x, y):
    return [(1 - x, y), (x, 1 - y), (1 - x, 1 - y)]


def _remote(src, dst, send_sem, recv_sem, device):
    return pltpu.make_async_remote_copy(src_ref=src, dst_ref=dst, send_sem=send_sem, recv_sem=recv_sem,
                                        device_id=device, device_id_type=MESH)


def _split_axis(shape2):
    return 0 if (shape2[0] // 2) % 16 == 0 else 1


def _half_index(shape2, c):
    axis = _split_axis(shape2)
    h = shape2[axis] // 2
    return (pl.ds(c * h, h), slice(None)) if axis == 0 else (slice(None), pl.ds(c * h, h))


def _all_gather(xs):
    n = len(xs)

    def body(*refs):
        ins, outs = refs[:n], refs[n:2 * n]
        send_sems, recv_sems = refs[2 * n:]
        x, y, c = _mesh_pos()
        me = 2 * x + y
        chips = _other_chips(x, y)
        halves = []
        sends = []
        for a in range(n):
            h = ins[a].shape[0] // 2
            mine, other = pl.ds(c * h, h), pl.ds((1 - c) * h, h)
            halves.append((mine, other))
            own = _remote(ins[a], outs[a].at[me], send_sems.at[a, 6], recv_sems.at[a, 6], (x, y, 1 - c))
            own.start()
            sends.append(own)
            for j, chip in enumerate(chips):
                cp = _remote(ins[a].at[mine], outs[a].at[me, mine], send_sems.at[a, j], recv_sems.at[a, j], (*chip, c))
                cp.start()
                sends.append(cp)
        for a in range(n):
            mine, _ = halves[a]
            for j, chip in enumerate(chips):
                src = 2 * chip[0] + chip[1]
                landed = outs[a].at[src, mine]
                _remote(landed, landed, send_sems.at[a, j], recv_sems.at[a, j], (x, y, c)).wait_recv()
                fwd = _remote(landed, landed, send_sems.at[a, 3 + j], recv_sems.at[a, 3 + j], (x, y, 1 - c))
                fwd.start()
                sends.append(fwd)
        for a in range(n):
            _, other = halves[a]
            for j, chip in enumerate(chips):
                src = 2 * chip[0] + chip[1]
                landed = outs[a].at[src, other]
                _remote(landed, landed, send_sems.at[a, 3 + j], recv_sems.at[a, 3 + j], (x, y, c)).wait_recv()
            mine_slot = outs[a].at[me]
            _remote(mine_slot, mine_slot, send_sems.at[a, 6], recv_sems.at[a, 6], (x, y, c)).wait_recv()
        for cp in sends:
            cp.wait_send()

    return list(pl.pallas_call(
        body, name="all_gather_weights",
        in_specs=[ANY] * n, out_specs=[ANY] * n,
        out_shape=[jax.ShapeDtypeStruct((N_SHARDS,) + a.shape, a.dtype) for a in xs],
        scratch_shapes=[pltpu.SemaphoreType.DMA((n, 7)), pltpu.SemaphoreType.DMA((n, 7))],
    )(*xs))


def _rs_pair(gs):
    n = len(gs)

    def body(*refs):
        ins, lands = refs[:n], refs[n:2 * n]
        send_sems, recv_sems = refs[2 * n:]
        x, y, c = _mesh_pos()
        cps = []
        for a in range(n):
            theirs = (slice(None),) + _half_index(ins[a].shape[1:], 1 - c)
            cp = _remote(ins[a].at[theirs], lands[a], send_sems.at[a], recv_sems.at[a], (x, y, 1 - c))
            cp.start()
            cps.append(cp)
        for cp in cps:
            cp.wait()

    def half_shape(g):
        dims = list(g.shape)
        dims[1 + _split_axis(g.shape[1:])] //= 2
        return tuple(dims)

    return list(pl.pallas_call(
        body, name="rs_pair", in_specs=[ANY] * n, out_specs=[ANY] * n,
        out_shape=[jax.ShapeDtypeStruct(half_shape(g), g.dtype) for g in gs],
        scratch_shapes=[pltpu.SemaphoreType.DMA((n,)), pltpu.SemaphoreType.DMA((n,))],
    )(*gs))


def _rs_chips(ss):
    n = len(ss)

    def body(*refs):
        ins, outs = refs[:n], refs[n:2 * n]
        send_sems, recv_sems = refs[2 * n:]
        x, y, c = _mesh_pos()
        me = 2 * x + y
        chips = _other_chips(x, y)
        cps = []
        for a in range(n):
            for j, chip in enumerate(chips):
                dst_chip = 2 * chip[0] + chip[1]
                cp = _remote(ins[a].at[dst_chip], outs[a].at[me], send_sems.at[a, j], recv_sems.at[a, j], (*chip, c))
                cp.start()
                cps.append(cp)
        for a in range(n):
            for j, chip in enumerate(chips):
                src = 2 * chip[0] + chip[1]
                _remote(outs[a].at[src], outs[a].at[src], send_sems.at[a, j], recv_sems.at[a, j], (x, y, c)).wait_recv()
        for cp in cps:
            cp.wait_send()

    return list(pl.pallas_call(
        body, name="rs_chips", in_specs=[ANY] * n, out_specs=[ANY] * n,
        out_shape=[jax.ShapeDtypeStruct(s.shape, s.dtype) for s in ss],
        scratch_shapes=[pltpu.SemaphoreType.DMA((n, 3)), pltpu.SemaphoreType.DMA((n, 3))],
    )(*ss))


def _rs_join(fs, axes):
    n = len(fs)

    def whole(f, axis):
        dims = list(f.shape)
        dims[axis] *= 2
        return tuple(dims)

    def body(*refs):
        ins, outs = refs[:n], refs[n:2 * n]
        send_sems, recv_sems = refs[2 * n:]
        x, y, c = _mesh_pos()
        cps = []
        for a in range(n):
            h = ins[a].shape[axes[a]]
            mine = (pl.ds(c * h, h), slice(None)) if axes[a] == 0 else (slice(None), pl.ds(c * h, h))
            cp = _remote(ins[a], outs[a].at[mine], send_sems.at[a], recv_sems.at[a], (x, y, 1 - c))
            cp.start()
            cps.append(cp)
        for cp in cps:
            cp.wait()

    outs = pl.pallas_call(
        body, name="rs_join", in_specs=[ANY] * n, out_specs=[ANY] * n,
        out_shape=[jax.ShapeDtypeStruct(whole(f, ax), f.dtype) for f, ax in zip(fs, axes)],
        scratch_shapes=[pltpu.SemaphoreType.DMA((n,)), pltpu.SemaphoreType.DMA((n,))],
    )(*fs)
    c = lax.axis_index("c")
    return [lax.dynamic_update_slice_in_dim(o, f, c * f.shape[ax], ax) for o, f, ax in zip(outs, fs, axes)]


def _gather_exchange(xs):
    def start(cin, cout, send_sems, recv_sems):
        x, y, c = _mesh_pos()
        me = 2 * x + y
        for a, (src, dst) in enumerate(zip(cin, cout)):
            mine = _half_index(src.shape, c)
            for j, chip in enumerate(_other_chips(x, y)):
                _remote(src.at[mine], dst.at[(me,) + mine], send_sems.at[a, j], recv_sems.at[a, j], (*chip, c)).start()

    def finish(cin, cout, send_sems, recv_sems):
        x, y, c = _mesh_pos()
        for a, dst in enumerate(cout):
            for j, chip in enumerate(_other_chips(x, y)):
                landed = dst.at[(2 * chip[0] + chip[1],) + _half_index(dst.shape[1:], c)]
                _remote(landed, landed, send_sems.at[a, j], recv_sems.at[a, j], (x, y, c)).wait()

    return _Exchange(tuple(xs), tuple(jax.ShapeDtypeStruct((N_SHARDS,) + a.shape, a.dtype) for a in xs), start, finish)


def _gather_forward(gs, xs):
    n = len(gs)

    def body(*refs):
        shards, outs = refs[n:2 * n], refs[2 * n:3 * n]
        send_sems, recv_sems = refs[3 * n:]
        x, y, c = _mesh_pos()
        me = 2 * x + y
        chips = _other_chips(x, y)
        cps = []
        for a in range(n):
            own = _remote(shards[a], outs[a].at[me], send_sems.at[a, 3], recv_sems.at[a, 3], (x, y, 1 - c))
            own.start()
            cps.append(own)
            for j, chip in enumerate(chips):
                landed = outs[a].at[(2 * chip[0] + chip[1],) + _half_index(outs[a].shape[1:], c)]
                cp = _remote(landed, landed, send_sems.at[a, j], recv_sems.at[a, j], (x, y, 1 - c))
                cp.start()
                cps.append(cp)
        for a in range(n):
            for j, chip in enumerate(chips):
                other = outs[a].at[(2 * chip[0] + chip[1],) + _half_index(outs[a].shape[1:], 1 - c)]
                _remote(other, other, send_sems.at[a, j], recv_sems.at[a, j], (x, y, c)).wait_recv()
            mine_slot = outs[a].at[me]
            _remote(mine_slot, mine_slot, send_sems.at[a, 3], recv_sems.at[a, 3], (x, y, c)).wait_recv()
        for cp in cps:
            cp.wait_send()

    return list(pl.pallas_call(
        body, name="gather_forward", in_specs=[ANY] * (2 * n), out_specs=[ANY] * n,
        out_shape=[jax.ShapeDtypeStruct(g.shape, g.dtype) for g in gs],
        input_output_aliases={i: i for i in range(n)},
        scratch_shapes=[pltpu.SemaphoreType.DMA((n, 4)), pltpu.SemaphoreType.DMA((n, 4))],
    )(*gs, *xs))


def _scatter_exchange(ss):
    def start(cin, cout, send_sems, recv_sems):
        x, y, c = _mesh_pos()
        me = 2 * x + y
        for a, (src, dst) in enumerate(zip(cin, cout)):
            for j, chip in enumerate(_other_chips(x, y)):
                _remote(src.at[2 * chip[0] + chip[1]], dst.at[me], send_sems.at[a, j], recv_sems.at[a, j],
                        (*chip, c)).start()

    def finish(cin, cout, send_sems, recv_sems):
        x, y, c = _mesh_pos()
        for a, dst in enumerate(cout):
            for j, chip in enumerate(_other_chips(x, y)):
                slot = dst.at[2 * chip[0] + chip[1]]
                _remote(slot, slot, send_sems.at[a, j], recv_sems.at[a, j], (x, y, c)).wait()

    return _Exchange(tuple(ss), tuple(jax.ShapeDtypeStruct(s.shape, s.dtype) for s in ss), start, finish)


def _add_pair(g, land, name):
    nj = g.shape[0]
    shape2 = g.shape[1:]

    def body(g_ref, l_ref, o_ref):
        mine = g_ref[(0,) + _half_index(shape2, lax.axis_index("c"))]
        o_ref[0] = (mine.astype(F32) + l_ref[0].astype(F32)).astype(BF16)

    half = pl.BlockSpec((1,) + land.shape[1:], lambda j: (j, 0, 0))
    return pl.pallas_call(body, name=name, grid=(nj,),
                          in_specs=[pl.BlockSpec((1,) + shape2, lambda j: (j, 0, 0)), half], out_specs=half,
                          out_shape=jax.ShapeDtypeStruct(land.shape, BF16),
                          compiler_params=_params(("arbitrary",), VMEM_LIMIT))(g, land)


def _sum_slots(slots, own, name):
    nj, h, c = slots.shape
    th = h // 2 if h % 32 == 0 else h

    def body(s_ref, o_ref, out_ref):
        me = 2 * lax.axis_index("x") + lax.axis_index("y")
        acc = jnp.zeros((th, c), F32)
        for s in range(nj):
            acc = acc + jnp.where(me == s, o_ref[s], s_ref[s]).astype(F32)
        out_ref[...] = acc

    blk = pl.BlockSpec((nj, th, c), lambda i: (0, i, 0))
    return pl.pallas_call(body, name=name, grid=(h // th,), in_specs=[blk, blk],
                          out_specs=pl.BlockSpec((th, c), lambda i: (i, 0)),
                          out_shape=jax.ShapeDtypeStruct((h, c), F32),
                          compiler_params=_params(("arbitrary",), VMEM_LIMIT))(slots, own)


def _all_reduce_small(p):
    r = p.shape[0]

    def body(p_ref, o_ref, buf, send_sems, recv_sems):
        x, y, c = _mesh_pos()
        me = 4 * x + 2 * y + c
        buf[me] = p_ref[...]
        cps = []
        k = 0
        for fx in range(2):
            for fy in range(2):
                for fc in range(2):
                    if fx + fy + fc == 0:
                        continue
                    peer = (1 - x if fx else x, 1 - y if fy else y, 1 - c if fc else c)
                    peer_id = 4 * peer[0] + 2 * peer[1] + peer[2]
                    cp = _remote(p_ref, buf.at[me], send_sems.at[k], recv_sems.at[k], peer)
                    cp.start()
                    cps.append((cp, peer_id, k))
                    k += 1
        for cp, peer_id, k in cps:
            _remote(p_ref, buf.at[peer_id], send_sems.at[k], recv_sems.at[k], (x, y, c)).wait_recv()
        for cp, _, _ in cps:
            cp.wait_send()
        acc = buf[0]
        for s in range(1, 8):
            acc = acc + buf[s]
        o_ref[...] = acc

    vm = pl.BlockSpec(memory_space=pltpu.VMEM)
    return pl.pallas_call(
        body, name="all_reduce_small", in_specs=[vm], out_specs=vm,
        out_shape=jax.ShapeDtypeStruct(p.shape, F32),
        scratch_shapes=[pltpu.VMEM((8, r, LANES), F32), pltpu.SemaphoreType.DMA((7,)), pltpu.SemaphoreType.DMA((7,))],
    )(p)


def _adamw(w, g, m, v, name):
    r, c = w.shape
    row_tiles = [d for d in range(8, min(r, 256) + 1, 8) if r % d == 0]
    tr, tc = (max(row_tiles), c) if row_tiles else (r, 256 if c % 256 == 0 else c)
    c1 = 1.0 / (1.0 - ADAM_B1 ** ADAM_STEP)
    c2 = 1.0 / (1.0 - ADAM_B2 ** ADAM_STEP)

    def body(w_ref, g_ref, m_ref, v_ref, d_ref, nm_ref, nv_ref):
        gv = g_ref[...]
        nm = ADAM_B1 * m_ref[...] + (1.0 - ADAM_B1) * gv
        nv = ADAM_B2 * v_ref[...] + (1.0 - ADAM_B2) * (gv * gv)
        d_ref[...] = -ADAM_LR * ((nm * c1) / (jnp.sqrt(nv * c2) + ADAM_EPS) + ADAM_WD * w_ref[...])
        nm_ref[...] = nm
        nv_ref[...] = nv

    blk = pl.BlockSpec((tr, tc), lambda i, j: (i, j))
    shp = jax.ShapeDtypeStruct((r, c), F32)
    return pl.pallas_call(body, name=name, grid=(r // tr, c // tc), in_specs=[blk] * 4, out_specs=[blk] * 3,
                          out_shape=[shp, shp, shp],
                          compiler_params=_params(("arbitrary", "arbitrary"), VMEM_LIMIT))(w, g, m, v)


PACK_UNIT = 8 * LANES


def _pack(arrs):
    parts = []
    for a in arrs:
        f = a.reshape(-1).astype(F32)
        parts.append(jnp.pad(f, (0, (-f.shape[0]) % PACK_UNIT)).reshape(-1, LANES))
    return jnp.concatenate(parts, axis=0)


def _unpack(m, shapes):
    outs, row = [], 0
    for s in shapes:
        n = int(np.prod(s))
        rows = -(-n // PACK_UNIT) * 8
        outs.append(m[row:row + rows].reshape(-1)[:n].reshape(s))
        row += rows
    return outs


WEIGHTS = ["ffn1_norm", "ffn1_w_gate", "ffn1_w_up", "ffn1_w_down", "mix_norm", "w_in", "conv_w", "a_log", "dt_bias",
           "gdn_norm_w", "q_norm_w", "k_norm_w", "rel_bias", "w_out", "ffn2_norm", "ffn2_w_gate", "ffn2_w_up",
           "ffn2_w_down", "final_norm"]
BIG = ["ffn1_w_gate", "ffn1_w_up", "ffn1_w_down", "w_in", "w_out", "ffn2_w_gate", "ffn2_w_up", "ffn2_w_down"]
SMALL = [n for n in WEIGHTS if n not in BIG]
COL_SHARDED = ["ffn1_w_gate", "ffn1_w_up", "w_in", "ffn2_w_gate", "ffn2_w_up"]
N_IN_COLS = 3600
TM = 256
TE = 512
TK = 2048


def kernel(x, ffn1_norm, ffn1_w_gate, ffn1_w_up, ffn1_w_down, mix_norm, w_in, conv_w, a_log, dt_bias, gdn_norm_w, q_norm_w, k_norm_w, rel_bias, w_out, ffn2_norm, ffn2_w_gate, ffn2_w_up, ffn2_w_down, final_norm, loss_target, m_ffn1_norm, m_ffn1_w_gate, m_ffn1_w_up, m_ffn1_w_down, m_mix_norm, m_w_in, m_conv_w, m_a_log, m_dt_bias, m_gdn_norm_w, m_q_norm_w, m_k_norm_w, m_rel_bias, m_w_out, m_ffn2_norm, m_ffn2_w_gate, m_ffn2_w_up, m_ffn2_w_down, m_final_norm, v_ffn1_norm, v_ffn1_w_gate, v_ffn1_w_up, v_ffn1_w_down, v_mix_norm, v_w_in, v_conv_w, v_a_log, v_dt_bias, v_gdn_norm_w, v_q_norm_w, v_k_norm_w, v_rel_bias, v_w_out, v_ffn2_norm, v_ffn2_w_gate, v_ffn2_w_up, v_ffn2_w_down, v_final_norm):
    p = dict(locals())
    xs, target = x[0], loss_target[0]
    t, d = xs.shape
    nc = t // CHUNK
    tk = min(TK, t)
    me = 2 * lax.axis_index("x") + lax.axis_index("y")

    first = ["ffn1_w_gate", "ffn1_w_up", "ffn1_w_down"]
    later = [n for n in BIG if n not in first] + ["conv_w"]
    local = lambda n, a: a[0].T if n in COL_SHARDED else a[0]
    shards = {n: local(n, p[n]).astype(BF16) for n in BIG}
    shards["conv_w"] = conv_w[0]
    gw = dict(zip(first, _all_gather([shards[n] for n in first])))
    f1 = (gw["ffn1_w_gate"], gw["ffn1_w_up"], gw["ffn1_w_down"])
    (x1, xn1, g1, u1), landed = _ffn_fwd(xs, ffn1_norm, *f1, TM, "ffn1_fwd",
                                         exchange=_gather_exchange([shards[n] for n in later]))
    gw.update(zip(later, _gather_forward(landed, [shards[n] for n in later])))
    w_in_t = gw["w_in"].reshape(N_IN_COLS, d)
    gates = slice(P_AB[0], P_AB[0] + N_GATE_COLS)
    wp = jnp.concatenate([w_in_t[:gates.start], jnp.pad(w_in_t[gates], ((0, LANES - N_GATE_COLS), (0, 0))),
                          w_in_t[gates.stop:]], axis=0)
    w_out_full = gw["w_out"].reshape(d, d)
    conv_rows = conv_w.shape[1]
    cw = jnp.pad(gw["conv_w"].reshape(N_SHARDS * conv_rows, CONV_TAPS).T, ((0, 8 - CONV_TAPS), (0, 0)))
    gp = jnp.pad(jnp.stack([a_log.reshape(8), dt_bias.reshape(8)]), ((0, 6), (0, LANES - 8)))
    gdn_w = gdn_norm_w.reshape(1, GDN_DIM)
    qw_t = jnp.tile(q_norm_w.reshape(1, SWA_DIM), (1, SWA_HEADS))
    kw_t = jnp.tile(k_norm_w.reshape(1, SWA_DIM), (1, SWA_HEADS))
    bd = jnp.asarray(np.kron(np.eye(2), np.full((SWA_DIM, SWA_DIM), 1.0 / SWA_DIM)), F32)
    f2 = (gw["ffn2_w_gate"], gw["ffn2_w_up"], gw["ffn2_w_down"])

    hn, qkva, z, ab, qkvb = _mix_in_fwd(x1, mix_norm, wp, TM)
    qkvc, gb = _gdn_prep_fwd(qkva, cw, ab, gp, TM)
    gbt = jnp.transpose(gb[:, :16].reshape(nc, CHUNK, 16), (0, 2, 1))
    o_f, o_b, gdn_saved = _gdn_fwd(qkvc, gb, gbt)
    oa = _gdn_post_fwd(o_f, o_b, z, gdn_w, TE)
    o_swa, o_swa16, swa_saved = _swa_branch_fwd(qkvb, qw_t, kw_t, rel_bias, bd, TE)
    x2 = _mix_out_fwd(x1, oa, o_swa, w_out_full, TM)
    (dx3, xn2, g2, u2, loss_part, d_final), _ = _ffn_fwd(x2, ffn2_norm, *f2, TM, "ffn2_fwd", head=(final_norm, target))

    def pair_sums(partials, tag):
        return [_add_pair(g, land, f"rs_add_{tag}{i}") for i, (g, land) in enumerate(zip(partials, _rs_pair(partials)))]

    (dx2, dyh2, dg2, du2, h2, d_nw2), _ = _ffn_bwd_dx(dx3, x2, ffn2_norm, g2, u2, *f2, TM, "ffn2_bwd_dx")
    dwg2 = _matmul_tn(dg2, xn2, tk, "ffn2_dwg")
    dwu2 = _matmul_tn(du2, xn2, tk, "ffn2_dwu")
    dwd2 = _matmul_tn(h2, dyh2, tk, "ffn2_dwd")
    sums_f2 = pair_sums([dwg2, dwu2, dwd2], "a")
    doa, dob, dx2b = _mix_out_bwd(dx2, w_out_full, TM)
    dwo = jnp.concatenate([_matmul_tn(oa, dx2b, tk, "w_out_dw_a")[0], _matmul_tn(o_swa16, dx2b, tk, "w_out_dw_b")[0]],
                          axis=0).reshape(N_SHARDS, d // N_SHARDS, d)
    do_g, dz, d_gdnw = _gdn_post_bwd(doa, o_f, o_b, z, gdn_w, TE)
    (dqkvc, dgates), slots_f2 = _gdn_bwd(qkvc, gb, gbt, do_g, gdn_saved, exchange=_scatter_exchange(sums_f2))
    dqkva, dab, dcw, dgp = _gdn_prep_bwd(qkva, cw, ab, gp, dqkvc, dgates, TM)
    dqkvb, d_qw, d_kw, d_rel = _swa_branch_bwd(dob, o_swa, swa_saved, qkvb, qw_t, kw_t, bd, TE)
    dpieces = (dqkva, dz, dab, dqkvb)
    dwp = [_matmul_tn(dp, hn, tk, f"w_in_dw_{i}")[0] for i, dp in enumerate(dpieces)]
    dw_in = jnp.concatenate([dwp[0], dwp[1], dwp[2][:N_GATE_COLS], dwp[3]], axis=0)
    dw_in = dw_in.reshape(N_SHARDS, N_IN_COLS // N_SHARDS, d)
    sums_mix = pair_sums([dw_in, dwo], "b")
    (dx1, d_mixnw), slots_mix = _mix_in_bwd_dx(dx2, x1, mix_norm, dpieces, wp, TM, exchange=_scatter_exchange(sums_mix))
    (gx, dyh1, dg1, du1, h1, d_nw1), _ = _ffn_bwd_dx(dx1, xs, ffn1_norm, g1, u1, *f1, TM, "ffn1_bwd_dx")
    dwg1 = _matmul_tn(dg1, xn1, tk, "ffn1_dwg")
    dwu1 = _matmul_tn(du1, xn1, tk, "ffn1_dwu")
    sums_gu = pair_sums([dwg1, dwu1], "c")
    dwd1, slots_gu = _matmul_tn(h1, dyh1, tk, "ffn1_dwd", exchange=_scatter_exchange(sums_gu))
    sums_d = pair_sums([dwd1], "d")
    slots = slots_gu + _rs_chips(sums_d) + slots_mix + slots_f2
    sums = sums_gu + sums_d + sums_mix + sums_f2
    halves = [_sum_slots(s, own, f"rs_sum_{i}") for i, (s, own) in enumerate(zip(slots, sums))]
    g_big = dict(zip(BIG, _rs_join(halves, [_split_axis(shards[n].shape) for n in BIG])))

    small_partial = {"ffn1_norm": d_nw1, "mix_norm": d_mixnw, "a_log": dgp[0, 0:8], "dt_bias": dgp[1, 0:8],
                     "gdn_norm_w": d_gdnw, "q_norm_w": d_qw, "k_norm_w": d_kw, "rel_bias": d_rel,
                     "ffn2_norm": d_nw2, "final_norm": d_final, "conv_w": dcw[0:CONV_TAPS].T}
    red = _all_reduce_small(_pack([small_partial[n] for n in SMALL] + [loss_part[0, 0:1]]))
    full_shapes = [p[n].shape if n != "conv_w" else (N_SHARDS * conv_rows, CONV_TAPS) for n in SMALL]
    red_parts = _unpack(red, full_shapes + [(1,)])
    loss = red_parts[-1].reshape(())
    g_small = dict(zip(SMALL, red_parts[:-1]))
    g_small["conv_w"] = lax.dynamic_slice_in_dim(g_small["conv_w"], me * conv_rows, conv_rows, 0).reshape(conv_w.shape)

    grads, deltas, new_m, new_v = {}, {}, {}, {}
    for n in BIG:
        back = (lambda a: a.T[None]) if n in COL_SHARDED else (lambda a: a[None])
        grads[n] = back(g_big[n])
        dl, nm, nv = _adamw(local(n, p[n]), g_big[n], local(n, p["m_" + n]), local(n, p["v_" + n]), "adamw_" + n)
        deltas[n], new_m[n], new_v[n] = back(dl), back(nm), back(nv)
    packed = [_pack([src[n] for n in SMALL]) for src in
              ({n: p[n] for n in SMALL}, g_small, {n: p["m_" + n] for n in SMALL}, {n: p["v_" + n] for n in SMALL})]
    small_shapes = [p[n].shape for n in SMALL]
    for dst, arr in zip((deltas, new_m, new_v), _adamw(*packed, "adamw_small")):
        dst.update(zip(SMALL, _unpack(arr, small_shapes)))
    grads.update(g_small)

    return (loss, gx[None], *[grads[n] for n in WEIGHTS], *[deltas[n] for n in WEIGHTS],
            *[new_m[n] for n in WEIGHTS], *[new_v[n] for n in WEIGHTS])
```

```python
import math
from typing import Callable, NamedTuple

import numpy as np
import jax
import jax.numpy as jnp
from jax import lax
from jax.experimental import pallas as pl
from jax.experimental.pallas import tpu as pltpu

F32 = jnp.float32
BF16 = jnp.bfloat16
HIGHEST = lax.Precision.HIGHEST
MESH = pl.DeviceIdType.MESH

EPS = 1e-6
NEG_BIG = -1e30
GDN_HEADS = 4
GDN_DIM = 128
CHUNK = 64
SWA_HEADS = 8
SWA_DIM = 64
PATTERNS = ((128, 1), (512, 4), (2048, 16))
RADIUS = 64
REL_BUCKETS = 32
REL_MAX_DISTANCE = 1024
CONV_TAPS = 5
N_SHARDS = 4
LANES = 128
VMEM_LIMIT = 56 * 1024 * 1024

ADAM_LR, ADAM_B1, ADAM_B2, ADAM_EPS, ADAM_WD, ADAM_STEP = 0.001, 0.9, 0.999, 1e-08, 0.01, 10


def _params(sem=None, vmem=None):
    return pltpu.CompilerParams(dimension_semantics=sem, vmem_limit_bytes=vmem)


def _resident(shape):
    nd = len(shape)
    return pl.BlockSpec(shape, lambda *_: (0,) * nd, pipeline_mode=pl.Buffered(1))


ANY = pl.BlockSpec(memory_space=pl.ANY)


class _Exchange(NamedTuple):
    arrays: tuple
    out_shape: tuple
    start: Callable
    finish: Callable


def _grid_call(body, name, nsteps, in_specs, out_specs, out_shape, operands, scratch=(), exchange=None):
    params = _params(("arbitrary",), VMEM_LIMIT)
    if exchange is None:
        res = pl.pallas_call(body, name=name, grid=(nsteps,), in_specs=list(in_specs), out_specs=list(out_specs),
                             out_shape=list(out_shape), scratch_shapes=list(scratch), compiler_params=params)(*operands)
        return list(res), []
    n_in, n_out, k, n_scr = len(in_specs), len(out_specs), len(exchange.arrays), len(scratch)

    def wrapped(*refs):
        ins, cin = refs[:n_in], refs[n_in:n_in + k]
        outs, cout = refs[n_in + k:n_in + k + n_out], refs[n_in + k + n_out:n_in + 2 * k + n_out]
        rest = refs[n_in + 2 * k + n_out:]
        scr, (send_sems, recv_sems) = rest[:n_scr], rest[n_scr:]

        @pl.when(pl.program_id(0) == 0)
        def _():
            exchange.start(cin, cout, send_sems, recv_sems)

        body(*ins, *outs, *scr)

        @pl.when(pl.program_id(0) == nsteps - 1)
        def _():
            exchange.finish(cin, cout, send_sems, recv_sems)

    res = pl.pallas_call(
        wrapped, name=name, grid=(nsteps,), in_specs=list(in_specs) + [ANY] * k, out_specs=list(out_specs) + [ANY] * k,
        out_shape=list(out_shape) + list(exchange.out_shape),
        scratch_shapes=list(scratch) + [pltpu.SemaphoreType.DMA((k, 3)), pltpu.SemaphoreType.DMA((k, 3))],
        compiler_params=params)(*operands, *exchange.arrays)
    return list(res[:n_out]), list(res[n_out:])


def _dot(a, b):
    return jnp.dot(a.astype(BF16), b.astype(BF16), preferred_element_type=F32)


def _dot_nt(a, b):
    return lax.dot_general(a.astype(BF16), b.astype(BF16), (((1,), (1,)), ((), ())), preferred_element_type=F32)


def _dot_tn(a, b):
    return lax.dot_general(a.astype(BF16), b.astype(BF16), (((0,), (0,)), ((), ())), preferred_element_type=F32)


def _dot_hi(a, b):
    return jnp.dot(a, b, preferred_element_type=F32, precision=HIGHEST)


def _sigmoid(x):
    return 1.0 / (1.0 + jnp.exp(-x))


def _rstd(xf):
    return lax.rsqrt(jnp.mean(xf * xf, axis=-1, keepdims=True) + EPS)


def _rms_bwd(xf, r, nw, dxn):
    xhat = xf * r
    dxh = dxn * nw
    dx = r * (dxh - xhat * jnp.mean(dxh * xhat, axis=-1, keepdims=True))
    return dx, jnp.sum(dxn * xhat, axis=0, keepdims=True)


def _ffn_fwd(x, nw, wg, wu, wd, tm, name, exchange=None, head=None):
    t, d = x.shape
    nj, fs, _ = wg.shape

    def body(x_ref, nw_ref, wg_ref, wu_ref, wd_ref, *rest):
        if head is None:
            y_ref, xn_ref, g_ref, u_ref = rest
        else:
            fw_ref, t_ref, y_ref, xn_ref, g_ref, u_ref, loss_ref, dfw_ref = rest

            @pl.when(pl.program_id(0) == 0)
            def _():
                loss_ref[...] = jnp.zeros_like(loss_ref)
                dfw_ref[...] = jnp.zeros_like(dfw_ref)

        xf = x_ref[...]
        xn = (xf * _rstd(xf) * nw_ref[...]).astype(BF16)
        xn_ref[...] = xn
        acc = jnp.zeros((tm, d), F32)
        for j in range(nj):
            g = _dot_nt(xn, wg_ref[j])
            u = _dot_nt(xn, wu_ref[j])
            h = (g * _sigmoid(g) * u).astype(BF16)
            acc = acc + jnp.dot(h, wd_ref[j], preferred_element_type=F32)
            g_ref[j] = g.astype(BF16)
            u_ref[j] = u.astype(BF16)
        y = xf + 0.5 * acc
        if head is None:
            y_ref[...] = y
        else:
            r = _rstd(y)
            err = y * r * fw_ref[...] - t_ref[...]
            loss_ref[...] += 0.5 * jnp.sum(jnp.mean(err * err, axis=-1, keepdims=True), axis=0, keepdims=True)
            dy, dfw = _rms_bwd(y, r, fw_ref[...], err * (1.0 / d))
            y_ref[...] = dy
            dfw_ref[...] += dfw

    row = pl.BlockSpec((tm, d), lambda i: (i, 0))
    act = pl.BlockSpec((nj, tm, fs), lambda i: (0, i, 0))
    in_specs = [row, _resident((1, d)), _resident(wg.shape), _resident(wu.shape), _resident(wd.shape)]
    out_specs = [row, row, act, act]
    out_shape = [jax.ShapeDtypeStruct((t, d), F32), jax.ShapeDtypeStruct((t, d), BF16),
                 jax.ShapeDtypeStruct((nj, t, fs), BF16), jax.ShapeDtypeStruct((nj, t, fs), BF16)]
    operands = (x, nw, wg, wu, wd)
    if head is not None:
        in_specs += [_resident((1, d)), row]
        out_specs += [pl.BlockSpec((1, LANES), lambda i: (0, 0)), pl.BlockSpec((1, d), lambda i: (0, 0))]
        out_shape += [jax.ShapeDtypeStruct((1, LANES), F32), jax.ShapeDtypeStruct((1, d), F32)]
        operands += tuple(head)
    return _grid_call(body, name, t // tm, in_specs, out_specs, out_shape, operands, exchange=exchange)


def _ffn_bwd_dx(dy, x, nw, g, u, wg, wu, wd, tm, name, exchange=None):
    t, d = x.shape
    nj, fs, _ = wg.shape

    def body(dy_ref, x_ref, nw_ref, g_ref, u_ref, wg_ref, wu_ref, wd_ref,
             dx_ref, dyh_ref, dg_ref, du_ref, h_ref, dnw_ref):
        @pl.when(pl.program_id(0) == 0)
        def _():
            dnw_ref[...] = jnp.zeros_like(dnw_ref)

        dyv = dy_ref[...]
        dyh = (0.5 * dyv).astype(BF16)
        dyh_ref[...] = dyh
        dxn = jnp.zeros((tm, d), F32)
        dh_next = _dot_nt(dyh, wd_ref[0])
        for j in range(nj):
            dh = dh_next
            gv = g_ref[j].astype(F32)
            uv = u_ref[j].astype(F32)
            sg = _sigmoid(gv)
            si = gv * sg
            dg = (dh * uv * (sg * (1.0 + gv * (1.0 - sg)))).astype(BF16)
            du = (dh * si).astype(BF16)
            if j + 1 < nj:
                dh_next = _dot_nt(dyh, wd_ref[j + 1])
            h_ref[j] = (si * uv).astype(BF16)
            dg_ref[j] = dg
            du_ref[j] = du
            dxn = dxn + _dot(dg, wg_ref[j]) + _dot(du, wu_ref[j])
        xf = x_ref[...]
        dxr, dnw = _rms_bwd(xf, _rstd(xf), nw_ref[...], dxn)
        dx_ref[...] = dyv + dxr
        dnw_ref[...] += dnw

    row = pl.BlockSpec((tm, d), lambda i: (i, 0))
    act = pl.BlockSpec((nj, tm, fs), lambda i: (0, i, 0))
    act_shape = jax.ShapeDtypeStruct((nj, t, fs), BF16)
    return _grid_call(
        body, name, t // tm,
        [row, row, _resident((1, d)), act, act, _resident(wg.shape), _resident(wu.shape), _resident(wd.shape)],
        [row, row, act, act, act, pl.BlockSpec((1, d), lambda i: (0, 0))],
        [jax.ShapeDtypeStruct((t, d), F32), jax.ShapeDtypeStruct((t, d), BF16),
         act_shape, act_shape, act_shape, jax.ShapeDtypeStruct((1, d), F32)],
        (dy, x, nw, g, u, wg, wu, wd), exchange=exchange)


def _matmul_tn(a, b, tk, name, exchange=None):
    a3, b3 = a.ndim == 3, b.ndim == 3
    nj = a.shape[0] if a3 else (b.shape[0] if b3 else 1)
    t, m = a.shape[-2:]
    n = b.shape[-1]
    nt = t // tk

    def body(a_ref, b_ref, o_ref, acc_ref):
        k = pl.program_id(0) % nt

        @pl.when(k == 0)
        def _():
            acc_ref[...] = jnp.zeros_like(acc_ref)

        acc_ref[...] += lax.dot_general(a_ref[...], b_ref[...], (((0,), (0,)), ((), ())),
                                        preferred_element_type=F32)

        @pl.when(k == nt - 1)
        def _():
            o_ref[...] = acc_ref[...].astype(o_ref.dtype)

    a_spec = (pl.BlockSpec((None, tk, m), lambda i: (i // nt, i % nt, 0)) if a3
              else pl.BlockSpec((tk, m), lambda i: (i % nt, 0)))
    b_spec = (pl.BlockSpec((None, tk, n), lambda i: (i // nt, i % nt, 0)) if b3
              else pl.BlockSpec((tk, n), lambda i: (i % nt, 0)))
    (out,), landed = _grid_call(
        body, name, nj * nt, [a_spec, b_spec], [pl.BlockSpec((None, m, n), lambda i: (i // nt, 0, 0))],
        [jax.ShapeDtypeStruct((nj, m, n), BF16)], (a, b), scratch=[pltpu.VMEM((m, n), F32)], exchange=exchange)
    return out if exchange is None else (out, landed)


P_QKVA, P_Z, P_AB, P_QKVB = (0, 1536), (1536, 2048), (2048, 2176), (2176, 3712)
P_PIECES = (P_QKVA, P_Z, P_AB, P_QKVB)
N_GATE_COLS = 4 * GDN_HEADS
P_COLS = 3712


def _mix_in_fwd(x1, nw, wp, tm):
    t, d = x1.shape

    def body(x_ref, nw_ref, w_ref, hn_ref, *outs):
        xf = x_ref[...]
        xn = (xf * _rstd(xf) * nw_ref[...]).astype(BF16)
        hn_ref[...] = xn
        for (a, b), o_ref in zip(P_PIECES, outs):
            o_ref[...] = _dot_nt(xn, w_ref[a:b, :])

    row = pl.BlockSpec((tm, d), lambda i: (i, 0))
    return pl.pallas_call(
        body, name="mix_in_fwd", grid=(t // tm,),
        in_specs=[row, _resident((1, d)), _resident(wp.shape)],
        out_specs=[row] + [pl.BlockSpec((tm, b - a), lambda i: (i, 0)) for a, b in P_PIECES],
        out_shape=[jax.ShapeDtypeStruct((t, d), BF16)]
                  + [jax.ShapeDtypeStruct((t, b - a), F32) for a, b in P_PIECES],
        compiler_params=_params(("arbitrary",), VMEM_LIMIT),
    )(x1, nw, wp)


def _mix_in_bwd_dx(dx, x1, nw, dpieces, wp, tm, exchange=None):
    t, d = x1.shape

    def body(dx_ref, x_ref, nw_ref, p0, p1, p2, p3, w_ref, o_ref, dnw_ref):
        @pl.when(pl.program_id(0) == 0)
        def _():
            dnw_ref[...] = jnp.zeros_like(dnw_ref)

        dh = jnp.zeros((tm, d), F32)
        for (a, b), p_ref in zip(P_PIECES, (p0, p1, p2, p3)):
            dh = dh + _dot(p_ref[...], w_ref[a:b, :])
        xf = x_ref[...]
        dxr, dnw = _rms_bwd(xf, _rstd(xf), nw_ref[...], dh)
        o_ref[...] = dx_ref[...] + dxr
        dnw_ref[...] += dnw

    row = pl.BlockSpec((tm, d), lambda i: (i, 0))
    return _grid_call(
        body, "mix_in_bwd_dx", t // tm,
        [row, row, _resident((1, d))]
        + [pl.BlockSpec((tm, b - a), lambda i: (i, 0)) for a, b in P_PIECES] + [_resident(wp.shape)],
        [row, pl.BlockSpec((1, d), lambda i: (0, 0))],
        [jax.ShapeDtypeStruct((t, d), F32), jax.ShapeDtypeStruct((1, d), F32)],
        (dx, x1, nw, *dpieces, wp), exchange=exchange)


def _mix_out_fwd(x1, oa, ob, w, tm):
    t, d = x1.shape
    half = oa.shape[1]

    def body(x_ref, oa_ref, ob_ref, w_ref, o_ref):
        o_ref[...] = (x_ref[...] + _dot(oa_ref[...], w_ref[0:half, :]) + _dot(ob_ref[...], w_ref[half:2 * half, :]))

    row = pl.BlockSpec((tm, d), lambda i: (i, 0))
    hrow = pl.BlockSpec((tm, half), lambda i: (i, 0))
    return pl.pallas_call(
        body, name="mix_out_fwd", grid=(t // tm,),
        in_specs=[row, hrow, hrow, _resident(w.shape)],
        out_specs=row, out_shape=jax.ShapeDtypeStruct((t, d), F32),
        compiler_params=_params(("arbitrary",), VMEM_LIMIT),
    )(x1, oa, ob, w)


def _mix_out_bwd(dx2, w, tm):
    t, d = dx2.shape
    half = w.shape[0] // 2

    def body(dx_ref, w_ref, doa_ref, dob_ref, dxb_ref):
        dxb = dx_ref[...].astype(BF16)
        dxb_ref[...] = dxb
        doa_ref[...] = _dot_nt(dxb, w_ref[0:half, :])
        dob_ref[...] = _dot_nt(dxb, w_ref[half:2 * half, :])

    row = pl.BlockSpec((tm, d), lambda i: (i, 0))
    hrow = pl.BlockSpec((tm, half), lambda i: (i, 0))
    return pl.pallas_call(
        body, name="mix_out_bwd", grid=(t // tm,),
        in_specs=[row, _resident(w.shape)],
        out_specs=[hrow, hrow, row],
        out_shape=[jax.ShapeDtypeStruct((t, half), F32), jax.ShapeDtypeStruct((t, half), F32),
                   jax.ShapeDtypeStruct((t, d), BF16)],
        compiler_params=_params(("arbitrary",), VMEM_LIMIT),
    )(dx2, w)


HALO = 8


def _halo_row_specs(tr, cols, nrow8):
    per = tr // HALO
    return [pl.BlockSpec((tr, cols), lambda i: (i, 0)),
            pl.BlockSpec((HALO, cols), lambda i: (jnp.maximum(i * per - 1, 0), 0)),
            pl.BlockSpec((HALO, cols), lambda i: (jnp.minimum((i + 1) * per, nrow8 - 1), 0))]


def _conv_window(xm, xp, xn, first, last, cols):
    prev = jnp.where(first, 0.0, xp[:, cols])
    nxt = jnp.where(last, 0.0, xn[:, cols])
    return jnp.concatenate([prev, xm[:, cols], nxt], axis=0)


def _shift_rows(xw, off):
    n = xw.shape[0]
    sh = (-off) % n
    return xw if sh == 0 else pltpu.roll(xw, sh, 0)


def _conv_pre(xw, cw_ref, cols):
    acc, shifted = None, []
    for j in range(CONV_TAPS):
        shifted.append(_shift_rows(xw, j - CONV_TAPS // 2))
        term = shifted[-1] * cw_ref[j:j + 1, cols]
        acc = term if acc is None else acc + term
    return acc, shifted


def _softplus(x):
    u = jnp.exp(-jnp.abs(x))
    w = 1.0 + u
    log1p = jnp.where(w == 1.0, u, jnp.log(w) * u / jnp.where(w == 1.0, 1.0, w - 1.0))
    return jnp.maximum(x, 0.0) + log1p


def _gdn_prep_fwd(qkva, cw, ab, gp, tr):
    t, c = qkva.shape
    nt = t // tr
    ncb = c // LANES

    def body(xm, xp, xn, cw_ref, ab_ref, gp_ref, o_ref, gb_ref):
        i = pl.program_id(0)
        first, last = i == 0, i == nt - 1
        for cb in range(ncb):
            cols = slice(cb * LANES, (cb + 1) * LANES)
            xw = _conv_window(xm, xp, xn, first, last, cols)
            pre = _conv_pre(xw, cw_ref, cols)[0][HALO:HALO + tr]
            y = pre * _sigmoid(pre)
            if cb < 2 * GDN_HEADS:
                y = y * lax.rsqrt(jnp.sum(y * y, axis=-1, keepdims=True) + EPS)
            if cb < GDN_HEADS:
                y = y * (GDN_DIM ** -0.5)
            o_ref[:, cols] = y
        abv = ab_ref[...]
        lane = lax.broadcasted_iota(jnp.int32, abv.shape, 1)
        g = -jnp.exp(gp_ref[0:1, :]) * _softplus(abv + gp_ref[1:2, :])
        gb_ref[...] = jnp.where(lane < 8, g, jnp.where(lane < 16, _sigmoid(abv), 0.0))

    return pl.pallas_call(
        body, name="gdn_prep_fwd", grid=(nt,),
        in_specs=_halo_row_specs(tr, c, t // HALO)
                 + [_resident(cw.shape), pl.BlockSpec((tr, LANES), lambda i: (i, 0)), _resident(gp.shape)],
        out_specs=[pl.BlockSpec((tr, c), lambda i: (i, 0)), pl.BlockSpec((tr, LANES), lambda i: (i, 0))],
        out_shape=[jax.ShapeDtypeStruct((t, c), F32), jax.ShapeDtypeStruct((t, LANES), F32)],
        compiler_params=_params(("arbitrary",), VMEM_LIMIT),
    )(qkva, qkva, qkva, cw, ab, gp)


def _gdn_prep_bwd(qkva, cw, ab, gp, dy, dgates, tr):
    t, c = qkva.shape
    nt = t // tr
    ncb = c // LANES

    def body(xm, xp, xn, fm, fp, fn, cw_ref, ab_ref, gp_ref, gf_ref, dx_ref, dab_ref, dcw_ref, dgp_ref):
        i = pl.program_id(0)
        first, last = i == 0, i == nt - 1

        @pl.when(first)
        def _():
            dcw_ref[...] = jnp.zeros_like(dcw_ref)
            dgp_ref[...] = jnp.zeros_like(dgp_ref)

        sub8 = lax.broadcasted_iota(jnp.int32, (8, LANES), 0)
        for cb in range(ncb):
            cols = slice(cb * LANES, (cb + 1) * LANES)
            xw = _conv_window(xm, xp, xn, first, last, cols)
            dyw = _conv_window(fm, fp, fn, first, last, cols)
            pre, x_shifted = _conv_pre(xw, cw_ref, cols)
            sg = _sigmoid(pre)
            s = pre * sg
            if cb < 2 * GDN_HEADS:
                scale = (GDN_DIM ** -0.5) if cb < GDN_HEADS else 1.0
                r = lax.rsqrt(jnp.sum(s * s, axis=-1, keepdims=True) + EPS)
                dn = dyw * scale
                ds = r * dn - s * (r * r * r) * jnp.sum(dn * s, axis=-1, keepdims=True)
            else:
                ds = dyw
            dpre = ds * (sg * (1.0 + pre * (1.0 - sg)))
            dx = None
            dcw = jnp.zeros((8, LANES), F32)
            for j in range(CONV_TAPS):
                off = j - CONV_TAPS // 2
                term = _shift_rows(dpre, -off)[HALO:HALO + tr] * cw_ref[j:j + 1, cols]
                dx = term if dx is None else dx + term
                tap = jnp.sum(dpre[HALO:HALO + tr] * x_shifted[j][HALO:HALO + tr], axis=0, keepdims=True)
                dcw = dcw + jnp.where(sub8 == j, tap, 0.0)
            dx_ref[:, cols] = dx.astype(BF16)
            dcw_ref[:, cols] += dcw

        abv = ab_ref[...]
        dgb = gf_ref[...]
        lane = lax.broadcasted_iota(jnp.int32, abv.shape, 1)
        nea = -jnp.exp(gp_ref[0:1, :])
        xs = abv + gp_ref[1:2, :]
        g = nea * _softplus(xs)
        beta = _sigmoid(abv)
        da = dgb * nea * _sigmoid(xs)
        dab = jnp.where(lane < 8, da, jnp.where(lane < 16, dgb * beta * (1.0 - beta), 0.0))
        dab_ref[...] = dab.astype(BF16)
        keep = lane[0:1, :] < 8
        dalog = jnp.where(keep, jnp.sum(dgb * g, axis=0, keepdims=True), 0.0)
        ddtb = jnp.where(keep, jnp.sum(da, axis=0, keepdims=True), 0.0)
        dgp_ref[...] += jnp.where(sub8 == 0, dalog, 0.0) + jnp.where(sub8 == 1, ddtb, 0.0)

    lrow = pl.BlockSpec((tr, LANES), lambda i: (i, 0))
    halo = _halo_row_specs(tr, c, t // HALO)
    return pl.pallas_call(
        body, name="gdn_prep_bwd", grid=(nt,),
        in_specs=halo + halo + [_resident(cw.shape), lrow, _resident(gp.shape), lrow],
        out_specs=[pl.BlockSpec((tr, c), lambda i: (i, 0)), lrow,
                   pl.BlockSpec(cw.shape, lambda i: (0, 0)), pl.BlockSpec(gp.shape, lambda i: (0, 0))],
        out_shape=[jax.ShapeDtypeStruct((t, c), BF16), jax.ShapeDtypeStruct((t, LANES), BF16),
                   jax.ShapeDtypeStruct(cw.shape, F32), jax.ShapeDtypeStruct(gp.shape, F32)],
        compiler_params=_params(("arbitrary",), VMEM_LIMIT),
    )(qkva, qkva, qkva, dy, dy, dy, cw, ab, gp, dgates)


def _chunk_masks(lower):
    ii = lax.broadcasted_iota(jnp.int32, (CHUNK, CHUNK), 0)
    jj = lax.broadcasted_iota(jnp.int32, (CHUNK, CHUNK), 1)
    incl = (ii >= jj) if lower else (ii <= jj)
    strict = (ii > jj) if lower else (ii < jj)
    return ii, jj, incl, strict


def _dot3(a, b):
    ah = a.astype(BF16)
    al = (a - ah.astype(F32)).astype(BF16)
    bh = b.astype(BF16)
    bl = (b - bh.astype(F32)).astype(BF16)
    d = lambda u, v: jnp.dot(u, v, preferred_element_type=F32)
    return d(ah, bh) + (d(ah, bl) + d(al, bh))


def _tri_inv_many(lmats, ii, jj):
    m16 = (ii // 16) == (jj // 16)
    m32 = (ii // 32) == (jj // 32)
    eye = jnp.where(ii == jj, 1.0, 0.0)
    l16 = [jnp.where(m16, l, 0.0) for l in lmats]
    p2 = [_dot3(a, a) for a in l16]
    p4 = [_dot3(a, a) for a in p2]
    p8 = [_dot3(a, a) for a in p4]
    xs = [eye - a for a in l16]
    for ps in (p2, p4, p8):
        xs = [x + _dot3(x, p) for x, p in zip(xs, ps)]
    for off in ([jnp.where(m32 & jnp.logical_not(m16), l, 0.0) for l in lmats],
                [jnp.where(m32, 0.0, l) for l in lmats]):
        ys = [_dot3(x, c) for x, c in zip(xs, off)]
        xs = [x - _dot3(y, x) for x, y in zip(xs, ys)]
    return xs


def _col_to_row(col, ii, jj):
    return jnp.sum(jnp.where(ii == jj, col, 0.0), axis=0, keepdims=True)


def _row_to_col(row, ii, jj):
    return jnp.sum(jnp.where(ii == jj, row, 0.0), axis=1, keepdims=True)


def _chain_common(q, k, v, graw_col, graw_row, bcol, masks):
    ii, jj, incl, strict = masks
    inclt = jnp.logical_not(strict)
    gcol = jnp.sum(jnp.where(incl, graw_row, 0.0), axis=1, keepdims=True)
    grow = jnp.sum(jnp.where(inclt, graw_col, 0.0), axis=0, keepdims=True)
    glast = jnp.sum(graw_row, axis=1, keepdims=True)
    decay = jnp.where(incl, jnp.exp(jnp.where(incl, gcol - grow, 0.0)), 0.0)
    kb = k * bcol
    vb = v * bcol
    eg = jnp.exp(gcol)
    ek = jnp.exp(glast - gcol)
    kbg = kb * eg
    amat = _dot_nt(kb, k)
    qk = _dot_nt(q, k)
    return dict(gcol=gcol, glast=glast, decay=decay, kb=kb, vb=vb, eg=eg, ek=ek, kbg=kbg, amat=amat, qk=qk,
                intra=qk * decay, qg=q * eg, kdec=k * ek)


def _gdn_fwd(qkvc, gb, gbt):
    tm, u, w, qg, kd, intra, egl = _gdn_local_fwd(qkvc, gb, gbt)
    o_f, o_b, s_f, s_b, vn_f, vn_b = _gdn_scan_fwd(u, w, qg, kd, intra, egl, qkvc.shape[0])
    return o_f, o_b, dict(tm=tm, w=w, qg=qg, kd=kd, intra=intra, egl=egl, s=(s_f, s_b), vn=(vn_f, vn_b))


N_CHAINS = 2 * GDN_HEADS


LOCAL_CHUNKS = 2


def _load_chains(x_ref, g_ref, gt_ref, cc=0):
    hd = GDN_HEADS * GDN_DIM
    rows = slice(cc * CHUNK, (cc + 1) * CHUNK)
    chains = []
    for d in range(2):
        masks = _chunk_masks(d == 0)
        for h in range(GDN_HEADS):
            ch = d * GDN_HEADS + h
            q = x_ref[rows, h * GDN_DIM:(h + 1) * GDN_DIM]
            k = x_ref[rows, hd + h * GDN_DIM:hd + (h + 1) * GDN_DIM]
            v = x_ref[rows, 2 * hd + h * GDN_DIM:2 * hd + (h + 1) * GDN_DIM]
            bcol = g_ref[rows, 8 + ch:9 + ch]
            cm = _chain_common(q, k, v, g_ref[rows, ch:ch + 1], gt_ref[cc, ch:ch + 1, :], bcol, masks)
            chains.append(dict(cm, q=q, k=k, v=v, bcol=bcol, masks=masks, ch=ch, h=h, cc=cc))
    return chains


def _chain_shape(rows, cols, dtype):
    return lambda nc: jax.ShapeDtypeStruct((nc, N_CHAINS, rows, cols), dtype)


def _gdn_local_fwd(qkvc, gb, gbt):
    t = qkvc.shape[0]
    nc = t // CHUNK
    hd = GDN_HEADS * GDN_DIM

    def body(x_ref, g_ref, gt_ref, t_ref, u_ref, w_ref, qg_ref, kd_ref, in_ref, eg_ref):
        chains = [c for cc in range(LOCAL_CHUNKS) for c in _load_chains(x_ref, g_ref, gt_ref, cc)]
        ii, jj = chains[0]["masks"][0:2]
        tms = _tri_inv_many([jnp.where(c["masks"][3], c["amat"] * c["decay"], 0.0) for c in chains], ii, jj)
        uws = [_dot(tm, jnp.concatenate([c["vb"], c["kbg"]], axis=1)) for tm, c in zip(tms, chains)]
        for c, tm, uw in zip(chains, tms, uws):
            cc, ch = c["cc"], c["ch"]
            t_ref[cc, ch] = tm
            u_ref[cc, ch] = uw[:, :GDN_DIM]
            w_ref[cc, ch] = uw[:, GDN_DIM:].astype(BF16)
            qg_ref[cc, ch] = c["qg"].astype(BF16)
            kd_ref[cc, ch] = c["kdec"].astype(BF16)
            in_ref[cc, ch] = c["intra"].astype(BF16)
            eg_ref[cc, ch:ch + 1, :] = jnp.broadcast_to(jnp.exp(c["glast"]), (1, LANES))

    lc = LOCAL_CHUNKS
    blk = lambda rows, cols: pl.BlockSpec((lc, N_CHAINS, rows, cols), lambda n: (n, 0, 0, 0))
    shapes = [_chain_shape(CHUNK, CHUNK, F32), _chain_shape(CHUNK, GDN_DIM, F32), _chain_shape(CHUNK, GDN_DIM, BF16),
              _chain_shape(CHUNK, GDN_DIM, BF16), _chain_shape(CHUNK, GDN_DIM, BF16), _chain_shape(CHUNK, CHUNK, BF16)]
    return tuple(pl.pallas_call(
        body, name="gdn_local_fwd", grid=(nc // lc,),
        in_specs=[pl.BlockSpec((lc * CHUNK, 3 * hd), lambda n: (n, 0)), pl.BlockSpec((lc * CHUNK, LANES), lambda n: (n, 0)),
                  pl.BlockSpec((lc, 16, CHUNK), lambda n: (n, 0, 0))],
        out_specs=[blk(CHUNK, CHUNK), blk(CHUNK, GDN_DIM), blk(CHUNK, GDN_DIM), blk(CHUNK, GDN_DIM),
                   blk(CHUNK, GDN_DIM), blk(CHUNK, CHUNK), pl.BlockSpec((lc, N_CHAINS, LANES), lambda n: (n, 0, 0))],
        out_shape=[s(nc) for s in shapes] + [jax.ShapeDtypeStruct((nc, N_CHAINS, LANES), F32)],
        compiler_params=_params(("arbitrary",), VMEM_LIMIT),
    )(qkvc, gb, gbt))


SCAN_CHUNKS = 4


def _dir_specs(nc, rev):
    nb = nc // SCAN_CHUNKS

    def spec(d, rows, cols, own=False):
        chunk = (lambda n: n) if (d == 0) != rev else (lambda n: nb - 1 - n)
        blk = 0 if own else d
        if rows is None:
            return pl.BlockSpec((SCAN_CHUNKS, GDN_HEADS if own else N_CHAINS, cols), lambda n: (chunk(n), 0, 0))
        return pl.BlockSpec((SCAN_CHUNKS, GDN_HEADS, rows, cols), lambda n: (chunk(n), blk, 0, 0))

    def rows_spec(d, cols):
        chunk = (lambda n: n) if (d == 0) != rev else (lambda n: nb - 1 - n)
        return pl.BlockSpec((SCAN_CHUNKS * CHUNK, cols), lambda n: (chunk(n), 0))

    def order(d):
        return list(range(SCAN_CHUNKS)) if (d == 0) != rev else list(range(SCAN_CHUNKS - 1, -1, -1))
    return spec, rows_spec, order


def _gdn_scan_fwd(u, w, qg, kd, intra, egl, t):
    nc = t // CHUNK
    hd = GDN_HEADS * GDN_DIM

    def body(*refs):
        ins, outs, state = refs[:12], refs[12:18], refs[18]
        @pl.when(pl.program_id(0) == 0)
        def _():
            state[...] = jnp.zeros_like(state)

        chains = [(d, h) for d in range(2) for h in range(GDN_HEADS)]
        states = [state[ch] for ch in range(N_CHAINS)]
        for step in range(SCAN_CHUNKS):
            at = [order(d)[step] for d in range(2)]
            pick = lambda k, d, h: ins[2 * k + d][at[d], h]
            sbs = [s.astype(BF16) for s in states]
            ws = [_dot(pick(1, d, h), sb) for (d, h), sb in zip(chains, sbs)]
            o1 = [_dot(pick(2, d, h), sb) for (d, h), sb in zip(chains, sbs)]
            vns = [(pick(0, d, h) - wsb).astype(BF16) for (d, h), wsb in zip(chains, ws)]
            o2 = [_dot(pick(4, d, h), vn) for (d, h), vn in zip(chains, vns)]
            kv = [_dot_tn(pick(3, d, h), vn) for (d, h), vn in zip(chains, vns)]
            new_states = []
            for ch, (d, h) in enumerate(chains):
                outs[d][at[d] * CHUNK:(at[d] + 1) * CHUNK, h * GDN_DIM:(h + 1) * GDN_DIM] = o1[ch] + o2[ch]
                outs[2 + d][at[d], h] = states[ch]
                outs[4 + d][at[d], h] = vns[ch]
                new_states.append(states[ch] * ins[10 + d][at[d], ch:ch + 1, :] + kv[ch])
            states = new_states
        for ch in range(N_CHAINS):
            state[ch] = states[ch]

    spec, rows_spec, order = _dir_specs(nc, False)
    pair = lambda rows, cols, own=False: [spec(0, rows, cols, own), spec(1, rows, cols, own)]
    s_shape = jax.ShapeDtypeStruct((nc, GDN_HEADS, GDN_DIM, GDN_DIM), F32)
    vn_shape = jax.ShapeDtypeStruct((nc, GDN_HEADS, CHUNK, GDN_DIM), BF16)
    return pl.pallas_call(
        body, name="gdn_scan_fwd", grid=(nc // SCAN_CHUNKS,),
        in_specs=(pair(CHUNK, GDN_DIM) + pair(CHUNK, GDN_DIM) + pair(CHUNK, GDN_DIM) + pair(CHUNK, GDN_DIM)
                  + pair(CHUNK, CHUNK) + pair(None, LANES)),
        out_specs=([rows_spec(0, hd), rows_spec(1, hd)] + pair(GDN_DIM, GDN_DIM, True)
                   + pair(CHUNK, GDN_DIM, True)),
        out_shape=[jax.ShapeDtypeStruct((t, hd), F32), jax.ShapeDtypeStruct((t, hd), F32),
                   s_shape, s_shape, vn_shape, vn_shape],
        scratch_shapes=[pltpu.VMEM((N_CHAINS, GDN_DIM, GDN_DIM), F32)],
        compiler_params=_params(("arbitrary",), VMEM_LIMIT),
    )(u, u, w, w, qg, qg, kd, kd, intra, intra, egl, egl)


def _gdn_bwd(qkvc, gb, gbt, do, saved, exchange=None):
    scan = _gdn_scan_bwd(do, saved, qkvc.shape[0])
    return _gdn_local_bwd(qkvc, gb, gbt, do, saved, scan, exchange)


def _gdn_scan_bwd(do, saved, t):
    nc = t // CHUNK
    hd = GDN_HEADS * GDN_DIM

    def body(*refs):
        ins, outs, dstate = refs[:16], refs[16:26], refs[26]
        @pl.when(pl.program_id(0) == 0)
        def _():
            dstate[...] = jnp.zeros_like(dstate)

        chains = [(d, h) for d in range(2) for h in range(GDN_HEADS)]
        dss = [dstate[ch] for ch in range(N_CHAINS)]
        for step in range(SCAN_CHUNKS):
            at = [order(d)[step] for d in range(2)]
            pick = lambda k, d, h: ins[2 * k + d][at[d], h]
            dsbs = [ds.astype(BF16) for ds in dss]
            ss = [pick(1, d, h) for d, h in chains]
            sbs = [s.astype(BF16) for s in ss]
            dos = [ins[d][at[d] * CHUNK:(at[d] + 1) * CHUNK, h * GDN_DIM:(h + 1) * GDN_DIM].astype(BF16)
                   for d, h in chains]
            dv1 = [_dot_tn(pick(5, d, h), dov) for (d, h), dov in zip(chains, dos)]
            dv2 = [_dot(pick(4, d, h), dsb) for (d, h), dsb in zip(chains, dsbs)]
            ds1 = [_dot_tn(pick(3, d, h), dov) for (d, h), dov in zip(chains, dos)]
            dkds = [_dot_nt(pick(6, d, h), dsb) for (d, h), dsb in zip(chains, dsbs)]
            dqgs = [_dot_nt(dov, sb) for dov, sb in zip(dos, sbs)]
            dvns = [(a + b).astype(BF16) for a, b in zip(dv1, dv2)]
            ds2 = [_dot_tn(pick(2, d, h), dvn) for (d, h), dvn in zip(chains, dvns)]
            dws = [_dot_nt(dvn, sb) for dvn, sb in zip(dvns, sbs)]
            new_dss = []
            for ch, (d, h) in enumerate(chains):
                egl = ins[14 + d][at[d], ch:ch + 1, :]
                outs[d][at[d], h] = dvns[ch]
                outs[2 + d][at[d], h] = (-dws[ch]).astype(BF16)
                outs[4 + d][at[d], h] = dqgs[ch]
                outs[6 + d][at[d], h] = dkds[ch]
                outs[8 + d][at[d], h:h + 1, :] = egl * jnp.sum(jnp.sum(ss[ch] * dss[ch], axis=1, keepdims=True),
                                                               axis=0, keepdims=True)
                new_dss.append(ds1[ch] + egl * dss[ch] - ds2[ch])
            dss = new_dss
        for ch in range(N_CHAINS):
            dstate[ch] = dss[ch]

    spec, rows_spec, order = _dir_specs(nc, True)
    pair = lambda rows, cols, own=False: [spec(0, rows, cols, own), spec(1, rows, cols, own)]
    s_f, s_b = saved["s"]
    vn_f, vn_b = saved["vn"]
    w, qg, kd, intra, egl = saved["w"], saved["qg"], saved["kd"], saved["intra"], saved["egl"]
    own = lambda rows, cols, dtype: jax.ShapeDtypeStruct((nc, GDN_HEADS, rows, cols), dtype)
    row_shape = jax.ShapeDtypeStruct((nc, GDN_HEADS, LANES), F32)
    return pl.pallas_call(
        body, name="gdn_scan_bwd", grid=(nc // SCAN_CHUNKS,),
        in_specs=([rows_spec(0, hd), rows_spec(1, hd)] + pair(GDN_DIM, GDN_DIM, True) + pair(CHUNK, GDN_DIM)
                  + pair(CHUNK, GDN_DIM) + pair(CHUNK, GDN_DIM) + pair(CHUNK, CHUNK) + pair(CHUNK, GDN_DIM, True)
                  + pair(None, LANES)),
        out_specs=(pair(CHUNK, GDN_DIM, True) + pair(CHUNK, GDN_DIM, True) + pair(CHUNK, GDN_DIM, True)
                   + pair(CHUNK, GDN_DIM, True) + pair(None, LANES, True)),
        out_shape=[own(CHUNK, GDN_DIM, BF16)] * 4 + [own(CHUNK, GDN_DIM, F32)] * 4 + [row_shape] * 2,
        scratch_shapes=[pltpu.VMEM((N_CHAINS, GDN_DIM, GDN_DIM), F32)],
        compiler_params=_params(("arbitrary",), VMEM_LIMIT),
    )(do, do, s_f, s_b, w, w, qg, qg, kd, kd, intra, intra, vn_f, vn_b, egl, egl)


def _dot3_nt(a, b):
    ah = a.astype(BF16)
    al = (a - ah.astype(F32)).astype(BF16)
    bh = b.astype(BF16)
    bl = (b - bh.astype(F32)).astype(BF16)
    return _dot_nt(ah, bh) + (_dot_nt(ah, bl) + _dot_nt(al, bh))


def _dot3_tn(a, b):
    ah = a.astype(BF16)
    al = (a - ah.astype(F32)).astype(BF16)
    bh = b.astype(BF16)
    bl = (b - bh.astype(F32)).astype(BF16)
    return _dot_tn(ah, bh) + (_dot_tn(ah, bl) + _dot_tn(al, bh))


def _gdn_local_bwd(qkvc, gb, gbt, do, saved, scan, exchange=None):
    t = qkvc.shape[0]
    nc = t // CHUNK
    hd = GDN_HEADS * GDN_DIM

    def body(*refs):
        x_ref, g_ref, gt_ref, do_ref, t_ref = refs[:5]
        per_dir = refs[5:17]
        dx_ref, dg_ref = refs[17:]
        chains = [c for cc in range(LOCAL_CHUNKS) for c in _load_chains(x_ref, g_ref, gt_ref, cc)]
        lane = lax.broadcasted_iota(jnp.int32, (CHUNK, LANES), 1)
        dgates = [jnp.zeros((CHUNK, LANES), F32) for _ in range(LOCAL_CHUNKS)]
        for c in chains:
            d = c["ch"] // GDN_HEADS
            vn_ref, dvn_ref, dw_ref, dqg_ref, dkd_ref, dgl_ref = per_dir[d::2]
            h, cc = c["h"], c["cc"]
            rows = slice(cc * CHUNK, (cc + 1) * CHUNK)
            c.update(tm=t_ref[cc, c["ch"]], dov=do_ref[rows, h * GDN_DIM:(h + 1) * GDN_DIM], vnew=vn_ref[cc, h],
                     dvnew=dvn_ref[cc, h], dw=dw_ref[cc, h], dqg=dqg_ref[cc, h], dkdec=dkd_ref[cc, h],
                     dglast=dgl_ref[cc, h:h + 1, 0:1])
        dintras = [_dot_nt(c["dov"], c["vnew"]) for c in chains]
        dts = [_dot_nt(c["dvnew"], c["vb"]) + _dot_nt(c["dw"], c["kbg"]) for c in chains]
        dvbs = [_dot_tn(c["tm"], c["dvnew"]) for c in chains]
        dkbgs = [_dot_tn(c["tm"], c["dw"]) for c in chains]
        tdts = [_dot3_nt(dt, c["tm"]) for dt, c in zip(dts, chains)]
        dls = [jnp.where(c["masks"][3], -_dot3_tn(c["tm"], tdt), 0.0) for tdt, c in zip(tdts, chains)]
        das = [dl * c["decay"] for dl, c in zip(dls, chains)]
        dqks = [jnp.where(c["masks"][2], di, 0.0) * c["decay"] for di, c in zip(dintras, chains)]
        dkb1 = [_dot(da, c["k"]) for da, c in zip(das, chains)]
        dk1 = [_dot_tn(da, c["kb"]) for da, c in zip(das, chains)]
        dk2 = [_dot_tn(dqk, c["q"]) for dqk, c in zip(dqks, chains)]
        dq1 = [_dot(dqk, c["k"]) for dqk, c in zip(dqks, chains)]
        grads, mms, p_gs, p_betas, p_kds = [], [], [], [], []
        for n, c in enumerate(chains):
            incl = c["masks"][2]
            dkb = dkb1[n] + dkbgs[n] * c["eg"]
            kd = c["dkdec"] * c["kdec"]
            mms.append((dls[n] * c["amat"] + jnp.where(incl, dintras[n], 0.0) * c["qk"]) * c["decay"])
            p_gs.append(c["dqg"] * c["qg"] - kd + dkbgs[n] * c["kbg"])
            p_betas.append(dkb * c["k"] + dvbs[n] * c["v"])
            p_kds.append(kd)
            grads.append((dq1[n] + c["dqg"] * c["eg"],
                          dk1[n] + dk2[n] + c["dkdec"] * c["ek"] + dkb * c["bcol"],
                          dvbs[n] * c["bcol"]))
        row_sums = [jnp.sum(mm, axis=1, keepdims=True) for mm in mms]
        col_sums = [jnp.sum(mm, axis=0, keepdims=True) for mm in mms]
        g_sums = [jnp.sum(pg, axis=1, keepdims=True) for pg in p_gs]
        dbetas = [jnp.sum(pb, axis=1, keepdims=True) for pb in p_betas]
        kd_tots = [jnp.sum(jnp.sum(pk, axis=1, keepdims=True), axis=0, keepdims=True) for pk in p_kds]
        dgcs = [rs - _row_to_col(cs, *c["masks"][0:2]) + gs for rs, cs, gs, c in zip(row_sums, col_sums, g_sums, chains)]
        dgrs = [_col_to_row(dgc, *c["masks"][0:2]) for dgc, c in zip(dgcs, chains)]
        draws = [jnp.sum(jnp.where(jnp.logical_not(c["masks"][3]), dgr, 0.0), axis=1, keepdims=True) + c["dglast"] + kt
                 for dgr, kt, c in zip(dgrs, kd_tots, chains)]
        for c, draw, dbeta in zip(chains, draws, dbetas):
            ch = c["ch"]
            dgates[c["cc"]] = dgates[c["cc"]] + jnp.where(lane == ch, draw, 0.0) + jnp.where(lane == 8 + ch, dbeta, 0.0)
        for cc in range(LOCAL_CHUNKS):
            rows = slice(cc * CHUNK, (cc + 1) * CHUNK)
            for h in range(GDN_HEADS):
                for part in range(3):
                    cols = slice(part * hd + h * GDN_DIM, part * hd + (h + 1) * GDN_DIM)
                    dx_ref[rows, cols] = grads[cc * N_CHAINS + h][part] + grads[cc * N_CHAINS + GDN_HEADS + h][part]
            dg_ref[rows, :] = dgates[cc]

    lc = LOCAL_CHUNKS
    all8 = lambda rows, cols: pl.BlockSpec((lc, N_CHAINS, rows, cols), lambda n: (n, 0, 0, 0))
    own4 = lambda rows, cols: pl.BlockSpec((lc, GDN_HEADS, rows, cols), lambda n: (n, 0, 0, 0))
    row4 = pl.BlockSpec((lc, GDN_HEADS, LANES), lambda n: (n, 0, 0))
    vn_f, vn_b = saved["vn"]
    dvn_f, dvn_b, dw_f, dw_b, dqg_f, dqg_b, dkd_f, dkd_b, dgl_f, dgl_b = scan
    return _grid_call(
        body, "gdn_local_bwd", nc // lc,
        [pl.BlockSpec((lc * CHUNK, 3 * hd), lambda n: (n, 0)), pl.BlockSpec((lc * CHUNK, LANES), lambda n: (n, 0)),
         pl.BlockSpec((lc, 16, CHUNK), lambda n: (n, 0, 0)), pl.BlockSpec((lc * CHUNK, hd), lambda n: (n, 0)),
         all8(CHUNK, CHUNK)] + [own4(CHUNK, GDN_DIM)] * 10 + [row4, row4],
        [pl.BlockSpec((lc * CHUNK, 3 * hd), lambda n: (n, 0)), pl.BlockSpec((lc * CHUNK, LANES), lambda n: (n, 0))],
        [jax.ShapeDtypeStruct((t, 3 * hd), F32), jax.ShapeDtypeStruct((t, LANES), F32)],
        (qkvc, gb, gbt, do, saved["tm"], vn_f, vn_b, dvn_f, dvn_b, dw_f, dw_b, dqg_f, dqg_b, dkd_f, dkd_b, dgl_f, dgl_b),
        exchange=exchange)


def _gdn_post_fwd(of, ob, z, gw, tm):
    t, hd = of.shape

    def body(of_ref, ob_ref, z_ref, w_ref, o_ref):
        for h in range(GDN_HEADS):
            cols = slice(h * GDN_DIM, (h + 1) * GDN_DIM)
            o = of_ref[:, cols] + ob_ref[:, cols]
            zv = z_ref[:, cols]
            o_ref[:, cols] = (o * _rstd(o) * w_ref[...] * (zv * _sigmoid(zv))).astype(BF16)

    row = pl.BlockSpec((tm, hd), lambda i: (i, 0))
    return pl.pallas_call(
        body, name="gdn_post_fwd", grid=(t // tm,),
        in_specs=[row, row, row, _resident((1, GDN_DIM))],
        out_specs=row, out_shape=jax.ShapeDtypeStruct((t, hd), BF16),
        compiler_params=_params(("arbitrary",), VMEM_LIMIT),
    )(of, ob, z, gw)


def _gdn_post_bwd(doa, of, ob, z, gw, tm):
    t, hd = of.shape

    def body(d_ref, of_ref, ob_ref, z_ref, w_ref, do_ref, dz_ref, dw_ref):
        @pl.when(pl.program_id(0) == 0)
        def _():
            dw_ref[...] = jnp.zeros_like(dw_ref)

        dw = jnp.zeros((1, GDN_DIM), F32)
        for h in range(GDN_HEADS):
            cols = slice(h * GDN_DIM, (h + 1) * GDN_DIM)
            o = of_ref[:, cols] + ob_ref[:, cols]
            zv = z_ref[:, cols]
            dv = d_ref[:, cols]
            r = _rstd(o)
            sg = _sigmoid(zv)
            on = o * r * w_ref[...]
            dz_ref[:, cols] = (dv * on * (sg * (1.0 + zv * (1.0 - sg)))).astype(BF16)
            dxr, dwh = _rms_bwd(o, r, w_ref[...], dv * (zv * sg))
            do_ref[:, cols] = dxr
            dw = dw + dwh
        dw_ref[...] += dw

    row = pl.BlockSpec((tm, hd), lambda i: (i, 0))
    return pl.pallas_call(
        body, name="gdn_post_bwd", grid=(t // tm,),
        in_specs=[row, row, row, row, _resident((1, GDN_DIM))],
        out_specs=[row, row, pl.BlockSpec((1, GDN_DIM), lambda i: (0, 0))],
        out_shape=[jax.ShapeDtypeStruct((t, hd), F32), jax.ShapeDtypeStruct((t, hd), BF16),
                   jax.ShapeDtypeStruct((1, GDN_DIM), F32)],
        compiler_params=_params(("arbitrary",), VMEM_LIMIT),
    )(doa, of, ob, z, gw)


SWA_W = SWA_HEADS * SWA_DIM
QBLK = 128
KWIN = QBLK + 2 * RADIUS
WIN_OFFSETS = (0, RADIUS, 2 * RADIUS)


def _t5_bucket(rel):
    nb = REL_BUCKETS // 2
    bucket = (rel > 0).astype(np.int32) * nb
    n = np.abs(rel)
    max_exact = nb // 2
    large = max_exact + (np.log(np.maximum(n, 1) / max_exact)
                         / math.log(REL_MAX_DISTANCE / max_exact) * (nb - max_exact)).astype(np.int32)
    large = np.minimum(large, nb - 1)
    return (bucket + np.where(n < max_exact, n, large)).astype(np.int32)


def _band_tables(dilation):
    a = np.arange(QBLK)
    b = np.arange(KWIN)
    rel = np.stack([b[None, :] - w0 - a[:, None] for w0 in WIN_OFFSETS])
    return np.where(np.abs(rel) <= RADIUS, _t5_bucket(rel * dilation), -1).astype(np.int32)


BAND_CELLS = len(WIN_OFFSETS) * QBLK * KWIN


def _band_index():
    return jnp.asarray(np.concatenate([_band_tables(d).reshape(-1) for _, d in PATTERNS])[None, :])


def _onehot(idx, dtype):
    return (lax.broadcasted_iota(jnp.int32, (REL_BUCKETS, idx.shape[1]), 0) == idx).astype(dtype)


def _bias_tables(rel_bias, idx, tk):
    n = idx.shape[1]

    def body(rb_ref, i_ref, o_ref):
        iv = i_ref[...]
        o_ref[...] = jnp.where(iv < 0, NEG_BIG, _dot_hi(rb_ref[...], _onehot(iv, F32)))

    return pl.pallas_call(
        body, name="bias_tables", grid=(n // tk,),
        in_specs=[_resident((SWA_HEADS, REL_BUCKETS)), pl.BlockSpec((1, tk), lambda k: (0, k))],
        out_specs=pl.BlockSpec((SWA_HEADS, tk), lambda k: (0, k)),
        out_shape=jax.ShapeDtypeStruct((SWA_HEADS, n), F32),
        compiler_params=_params(("arbitrary",), VMEM_LIMIT),
    )(rel_bias.T, idx)


def _head_mean(x2, bd_ref):
    bd = bd_ref[...]
    rest, acc = x2, None
    for _ in range(3):
        piece = rest.astype(BF16)
        part = jnp.dot(piece, bd, preferred_element_type=F32)
        acc = part if acc is None else acc + part
        rest = rest - piece.astype(F32)
    return acc


VIEW_DILATIONS = tuple(d for _, d in PATTERNS if d > 1)


def _view_spec(tm, d):
    return pl.BlockSpec((tm // d, d * SWA_W), lambda i: (i, 0))


def _view_shape(t, d, dtype):
    return jax.ShapeDtypeStruct((t // d, d * SWA_W), dtype)


N_GROUPS = SWA_W // LANES


def _to_view(src_ref, idx, dst_ref, d, rows):
    for r in range(d):
        for g in range(N_GROUPS):
            cols = slice(r * SWA_W + g * LANES, r * SWA_W + (g + 1) * LANES)
            dst_ref[:, cols] = src_ref[idx, g, pl.ds(r, rows // d, stride=d), :].astype(dst_ref.dtype)


def _from_view(src_ref, dst_ref, idx, d, rows):
    for r in range(d):
        for g in range(N_GROUPS):
            cols = slice(r * SWA_W + g * LANES, r * SWA_W + (g + 1) * LANES)
            dst_ref[idx, g, pl.ds(r, rows // d, stride=d), :] = src_ref[:, cols]


def _swa_prep_fwd(qkvb, qw, kw, bd, tm):
    t = qkvb.shape[0]

    def body(x_ref, qw_ref, kw_ref, bd_ref, *rest):
        outs, sc = rest[:-1], rest[-1]
        for gidx in range(N_GROUPS):
            cols = slice(gidx * LANES, (gidx + 1) * LANES)
            xq = x_ref[:, cols]
            sc[0, gidx] = xq * lax.rsqrt(_head_mean(xq * xq, bd_ref) + EPS) * qw_ref[:, cols] * (SWA_DIM ** -0.5)
            xk = x_ref[:, SWA_W + gidx * LANES:SWA_W + (gidx + 1) * LANES]
            sc[1, gidx] = xk * lax.rsqrt(_head_mean(xk * xk, bd_ref) + EPS) * kw_ref[:, cols]
            sc[2, gidx] = x_ref[:, 2 * SWA_W + gidx * LANES:2 * SWA_W + (gidx + 1) * LANES]
            for i in range(3):
                outs[i][:, cols] = sc[i, gidx].astype(BF16)
        for i in range(3):
            for n, d in enumerate(VIEW_DILATIONS):
                _to_view(sc, i, outs[3 * (n + 1) + i], d, tm)

    return pl.pallas_call(
        body, name="swa_prep_fwd", grid=(t // tm,),
        in_specs=[pl.BlockSpec((tm, 3 * SWA_W), lambda i: (i, 0)), _resident((1, SWA_W)), _resident((1, SWA_W)),
                  _resident((LANES, LANES))],
        out_specs=[_view_spec(tm, d) for d in (1,) + VIEW_DILATIONS for _ in range(3)],
        out_shape=[_view_shape(t, d, BF16) for d in (1,) + VIEW_DILATIONS for _ in range(3)],
        scratch_shapes=[pltpu.VMEM((3, N_GROUPS, tm, LANES), F32)],
        compiler_params=_params(("arbitrary",), VMEM_LIMIT),
    )(qkvb, qw, kw, bd)


def _swa_prep_bwd(qkvb, qw, kw, bd, grads, tm):
    t = qkvb.shape[0]

    def body(x_ref, qw_ref, kw_ref, bd_ref, *rest):
        parts, (dx_ref, dqw_ref, dkw_ref, sc) = rest[:9], rest[9:]
        @pl.when(pl.program_id(0) == 0)
        def _():
            dqw_ref[...] = jnp.zeros_like(dqw_ref)
            dkw_ref[...] = jnp.zeros_like(dkw_ref)

        for i in range(3):
            for n, d in enumerate(VIEW_DILATIONS):
                _from_view(parts[3 * (n + 1) + i], sc, 2 * i + n, d, tm)
        for gidx in range(N_GROUPS):
            cols = slice(gidx * LANES, (gidx + 1) * LANES)
            for i, base, w_ref, dw_ref, scale in ((0, 0, qw_ref, dqw_ref, SWA_DIM ** -0.5),
                                                  (1, SWA_W, kw_ref, dkw_ref, 1.0)):
                xv = x_ref[:, base + gidx * LANES:base + (gidx + 1) * LANES]
                dy = (parts[i][:, cols] + sc[2 * i, gidx] + sc[2 * i + 1, gidx]) * scale
                r = lax.rsqrt(_head_mean(xv * xv, bd_ref) + EPS)
                xhat = xv * r
                dxh = dy * w_ref[:, cols]
                dx = r * (dxh - xhat * _head_mean(dxh * xhat, bd_ref))
                dx_ref[:, base + gidx * LANES:base + (gidx + 1) * LANES] = dx.astype(BF16)
                dw_ref[:, cols] += jnp.sum(dy * xhat, axis=0, keepdims=True)
            dx_ref[:, 2 * SWA_W + gidx * LANES:2 * SWA_W + (gidx + 1) * LANES] = (
                parts[2][:, cols] + sc[4, gidx] + sc[5, gidx]).astype(BF16)

    wrow = pl.BlockSpec((1, SWA_W), lambda i: (0, 0))
    return pl.pallas_call(
        body, name="swa_prep_bwd", grid=(t // tm,),
        in_specs=[pl.BlockSpec((tm, 3 * SWA_W), lambda i: (i, 0)), _resident((1, SWA_W)), _resident((1, SWA_W)),
                  _resident((LANES, LANES))] + [_view_spec(tm, d) for d in (1,) + VIEW_DILATIONS for _ in range(3)],
        out_specs=[pl.BlockSpec((tm, 3 * SWA_W), lambda i: (i, 0)), wrow, wrow],
        out_shape=[jax.ShapeDtypeStruct((t, 3 * SWA_W), BF16), jax.ShapeDtypeStruct((1, SWA_W), F32),
                   jax.ShapeDtypeStruct((1, SWA_W), F32)],
        scratch_shapes=[pltpu.VMEM((6, N_GROUPS, tm, LANES), F32)],
        compiler_params=_params(("arbitrary",), VMEM_LIMIT),
    )(qkvb, qw, kw, bd, *grads)


def _aligned(v, m):
    return v if isinstance(v, int) else pl.multiple_of(v, m)


BAND_GROUP = 2


def _band_loop(nsub, length, step, group=BAND_GROUP):
    step([(0, 0)], 0)
    if nsub > 2:
        assert (nsub - 2) % group == 0

        def inner(i, carry):
            s0 = 1 + i * group
            step([(s0 + e, pl.multiple_of((s0 + e) * QBLK - RADIUS, RADIUS)) for e in range(group)], 1)
            return carry
        lax.fori_loop(0, (nsub - 2) // group, inner, 0)
    step([(nsub - 1, length - KWIN)], 2)


def _head_select(lane, a0, a1):
    return jnp.where(lane < SWA_DIM, a0, a1)


def _swa_fwd(qv, kv, vv, bias, dilation, name):
    length = qv.shape[0]
    nsub = length // QBLK
    assert nsub >= 2 and length % QBLK == 0

    def body(q_ref, k_ref, v_ref, b_ref, o_ref, l_ref):
        lane = lax.broadcasted_iota(jnp.int32, (QBLK, LANES), 1)

        def step(blocks, var):
            items = []
            for s, ws in blocks:
                rows = pl.ds(_aligned(s * QBLK, QBLK), QBLK)
                q, kk, vw = q_ref[rows, :], k_ref[pl.ds(ws, KWIN), :], v_ref[pl.ds(ws, KWIN), :]
                for hh in range(2):
                    items.append((hh, jnp.where((lane < SWA_DIM) == (hh == 0), q, jnp.zeros_like(q)), kk, vw))
            lgs = [_dot_nt(qh, kk) + b_ref[hh, var] for hh, qh, kk, _ in items]
            ms = [jnp.max(lg, axis=-1, keepdims=True) for lg in lgs]
            ps = [jnp.exp(lg - m) for lg, m in zip(lgs, ms)]
            dens = [jnp.sum(p, axis=-1, keepdims=True) for p in ps]
            pvs = [_dot(p, it[3]) for p, it in zip(ps, items)]
            for n, (s, _) in enumerate(blocks):
                rows = pl.ds(_aligned(s * QBLK, QBLK), QBLK)
                o0, o1 = (pvs[2 * n + hh] / dens[2 * n + hh] for hh in range(2))
                l0, l1 = (ms[2 * n + hh] + jnp.log(dens[2 * n + hh]) for hh in range(2))
                o_ref[rows, :] = _head_select(lane, o0, o1)
                l_ref[rows, :] = _head_select(lane, l0, l1)

        _band_loop(nsub, length, step)

    blk = pl.BlockSpec((length, LANES), lambda hp, r: (0, r * (SWA_W // LANES) + hp))
    shp = jax.ShapeDtypeStruct(qv.shape, F32)
    return pl.pallas_call(
        body, name=name, grid=(SWA_W // LANES, dilation),
        in_specs=[blk, blk, blk, pl.BlockSpec((2, 3, QBLK, KWIN), lambda hp, r: (hp, 0, 0, 0))],
        out_specs=[blk, blk], out_shape=[shp, shp],
        compiler_params=_params(("arbitrary", "arbitrary"), VMEM_LIMIT),
    )(qv, kv, vv, bias)


def _swa_combine(os_, ls_, tm):
    t = os_[0].shape[0]

    def body(o0, o1, o2, l0, l1, l2, o_ref, ob_ref, la_ref, lb_ref, lc_ref, sc):
        for n, d in enumerate(VIEW_DILATIONS):
            _from_view((o1, o2)[n], sc, n, d, tm)
            _from_view((l1, l2)[n], sc, 2 + n, d, tm)
        for g in range(N_GROUPS):
            cols = slice(g * LANES, (g + 1) * LANES)
            la, lb, lc = l0[:, cols], sc[2, g], sc[3, g]
            m = jnp.maximum(jnp.maximum(la, lb), lc)
            tot = m + jnp.log(jnp.exp(la - m) + jnp.exp(lb - m) + jnp.exp(lc - m))
            o = jnp.exp(la - tot) * o0[:, cols] + jnp.exp(lb - tot) * sc[0, g] + jnp.exp(lc - tot) * sc[1, g]
            o_ref[:, cols] = o
            ob_ref[:, cols] = o.astype(BF16)
            la_ref[:, cols] = tot
            sc[4, g] = tot
        for n, d in enumerate(VIEW_DILATIONS):
            _to_view(sc, 4, (lb_ref, lc_ref)[n], d, tm)

    specs = [_view_spec(tm, d) for d in (1,) + VIEW_DILATIONS]
    return pl.pallas_call(
        body, name="swa_combine", grid=(t // tm,), in_specs=specs + specs, out_specs=[specs[0], specs[0]] + specs,
        out_shape=[jax.ShapeDtypeStruct((t, SWA_W), F32), jax.ShapeDtypeStruct((t, SWA_W), BF16)]
                  + [_view_shape(t, d, F32) for d in (1,) + VIEW_DILATIONS],
        scratch_shapes=[pltpu.VMEM((5, N_GROUPS, tm, LANES), F32)],
        compiler_params=_params(("arbitrary",), VMEM_LIMIT),
    )(*os_, *ls_)


def _swa_bwd_prep(do, o, bd, tm):
    t = do.shape[0]

    def body(d_ref, o_ref, bd_ref, dd1, dd4, dd16, db1, db4, db16, sc):
        for gidx in range(N_GROUPS):
            cols = slice(gidx * LANES, (gidx + 1) * LANES)
            dv = d_ref[:, cols]
            dd = _head_mean(dv * o_ref[:, cols], bd_ref) * float(SWA_DIM)
            sc[0, gidx] = dd
            sc[1, gidx] = dv
            dd1[:, cols] = dd
            db1[:, cols] = dv.astype(BF16)
        for n, d in enumerate(VIEW_DILATIONS):
            _to_view(sc, 0, (dd4, dd16)[n], d, tm)
            _to_view(sc, 1, (db4, db16)[n], d, tm)

    specs = [_view_spec(tm, d) for d in (1,) + VIEW_DILATIONS]
    return pl.pallas_call(
        body, name="swa_bwd_prep", grid=(t // tm,), in_specs=[specs[0], specs[0], _resident((LANES, LANES))],
        out_specs=specs + specs,
        out_shape=[_view_shape(t, d, F32) for d in (1,) + VIEW_DILATIONS]
                  + [_view_shape(t, d, BF16) for d in (1,) + VIEW_DILATIONS],
        scratch_shapes=[pltpu.VMEM((2, N_GROUPS, tm, LANES), F32)],
        compiler_params=_params(("arbitrary",), VMEM_LIMIT),
    )(do, o, bd)


def _swa_bwd(qv, kv, vv, dov, lv, ddv, bias_a, dilation, name):
    length = qv.shape[0]
    nsub = length // QBLK
    single = pl.Buffered(1) if dilation == 1 else None

    def body(q_ref, k_ref, v_ref, do_ref, l_ref, dd_ref, ba_ref, dq_ref, dk_ref, dv_ref, db_ref):
        @pl.when(pl.program_id(1) == 0)
        def _():
            db_ref[...] = jnp.zeros_like(db_ref)

        lane = lax.broadcasted_iota(jnp.int32, (QBLK, LANES), 1)
        lanew = lax.broadcasted_iota(jnp.int32, (KWIN, LANES), 1)

        def step(blocks, var):
            items = []
            for s, ws in blocks:
                rows = pl.ds(_aligned(s * QBLK, QBLK), QBLK)
                win = pl.ds(ws, KWIN)
                q, dov_ = q_ref[rows, :], do_ref[rows, :]
                kk, vw = k_ref[win, :], v_ref[win, :]
                lse, dd = l_ref[rows, :], dd_ref[rows, :]
                for hh in range(2):
                    mine = (lane < SWA_DIM) == (hh == 0)
                    col = slice(hh * SWA_DIM, hh * SWA_DIM + 1)
                    items.append((hh, jnp.where(mine, q, jnp.zeros_like(q)), jnp.where(mine, dov_, jnp.zeros_like(dov_)),
                                  kk, vw, lse[:, col], dd[:, col], q, dov_))
            lgs = [_dot_nt(it[1], it[3]) + ba_ref[it[0], var] for it in items]
            dps = [_dot_nt(it[2], it[4]) for it in items]
            ps = [jnp.exp(lg - it[5]) for lg, it in zip(lgs, items)]
            dss = [p * (dp - it[6]) for p, dp, it in zip(ps, dps, items)]
            dqs = [_dot(ds, it[3]) for ds, it in zip(dss, items)]
            dks = [_dot_tn(ds, it[7]) for ds, it in zip(dss, items)]
            dvs = [_dot_tn(p, it[8]) for p, it in zip(ps, items)]
            for n, (s, ws) in enumerate(blocks):
                rows = pl.ds(_aligned(s * QBLK, QBLK), QBLK)
                win = pl.ds(ws, KWIN)
                dq_ref[rows, :] = _head_select(lane, dqs[2 * n], dqs[2 * n + 1])
                dk_ref[win, :] += _head_select(lanew, dks[2 * n], dks[2 * n + 1])
                dv_ref[win, :] += _head_select(lanew, dvs[2 * n], dvs[2 * n + 1])
            for hh in range(2):
                tot = dss[hh]
                for n in range(1, len(blocks)):
                    tot = tot + dss[2 * n + hh]
                db_ref[hh, var] += tot

        dk_ref[...] = jnp.zeros_like(dk_ref)
        dv_ref[...] = jnp.zeros_like(dv_ref)
        _band_loop(nsub, length, step)

    imap = lambda hp, r: (0, r * (SWA_W // LANES) + hp)
    blk_in = pl.BlockSpec((length, LANES), imap, pipeline_mode=single)
    blk_out = pl.BlockSpec((length, LANES), imap)
    shp = jax.ShapeDtypeStruct(qv.shape, F32)
    return pl.pallas_call(
        body, name=name, grid=(SWA_W // LANES, dilation),
        in_specs=[blk_in] * 6 + [pl.BlockSpec((2, 3, QBLK, KWIN), lambda hp, r: (hp, 0, 0, 0))],
        out_specs=[blk_out, blk_out, blk_out, pl.BlockSpec((2, 3, QBLK, KWIN), lambda hp, r: (hp, 0, 0, 0))],
        out_shape=[shp, shp, shp, jax.ShapeDtypeStruct((SWA_HEADS, 3, QBLK, KWIN), F32)],
        compiler_params=_params(("arbitrary", "arbitrary"), VMEM_LIMIT),
    )(qv, kv, vv, dov, lv, ddv, bias_a)


def _bias_grad(ds2, idx, tk):
    n = ds2.shape[1]
    nk = n // tk

    def body(a_ref, i_ref, o_ref):
        @pl.when(pl.program_id(0) == 0)
        def _():
            o_ref[...] = jnp.zeros_like(o_ref)

        oh = _onehot(i_ref[...], BF16)
        rest = a_ref[...]
        acc = jnp.zeros((SWA_HEADS, REL_BUCKETS), F32)
        for _ in range(3):
            piece = rest.astype(BF16)
            acc = acc + _dot_nt(piece, oh)
            rest = rest - piece.astype(F32)
        o_ref[...] += acc

    return pl.pallas_call(
        body, name="bias_grad", grid=(nk,),
        in_specs=[pl.BlockSpec((SWA_HEADS, tk), lambda k: (0, k)), pl.BlockSpec((1, tk), lambda k: (0, k))],
        out_specs=pl.BlockSpec((SWA_HEADS, REL_BUCKETS), lambda k: (0, 0)),
        out_shape=jax.ShapeDtypeStruct((SWA_HEADS, REL_BUCKETS), F32),
        compiler_params=_params(("arbitrary",), VMEM_LIMIT),
    )(ds2, idx)


def _swa_branch_fwd(qkvb, qw_t, kw_t, rel_bias, bd, tm):
    qkv = _swa_prep_fwd(qkvb, qw_t, kw_t, bd, tm)
    tables = _bias_tables(rel_bias, _band_index(), 8192)
    os_, ls_, tabs = [], [], []
    for n, (_, d) in enumerate(PATTERNS):
        bias = tables[:, n * BAND_CELLS:(n + 1) * BAND_CELLS].reshape(SWA_HEADS, len(WIN_OFFSETS), QBLK, KWIN)
        o_p, l_p = _swa_fwd(*qkv[3 * n:3 * n + 3], bias, d, f"swa_fwd_d{d}")
        os_.append(o_p)
        ls_.append(l_p)
        tabs.append(bias)
    o, o16, *lses = _swa_combine(os_, ls_, tm)
    return o, o16, (qkv, lses, tabs)


def _swa_branch_bwd(do, o, saved, qkvb, qw_t, kw_t, bd, tm):
    qkv, lses, tabs = saved
    prep = _swa_bwd_prep(do, o, bd, tm)
    grads, dss = [], []
    for n, ((_, d), bias) in enumerate(zip(PATTERNS, tabs)):
        dq, dk, dv, ds = _swa_bwd(*qkv[3 * n:3 * n + 3], prep[3 + n], lses[n], prep[n], bias, d, f"swa_bwd_d{d}")
        grads += [dq, dk, dv]
        dss.append(ds.reshape(SWA_HEADS, -1))
    dqkvb, dqw, dkw = _swa_prep_bwd(qkvb, qw_t, kw_t, bd, grads, tm)
    dbias = _bias_grad(jnp.concatenate(dss, axis=1), _band_index(), 8192)
    fold = lambda w: jnp.sum(w.reshape(SWA_HEADS, SWA_DIM), axis=0)
    return dqkvb, fold(dqw), fold(dkw), dbias.T


def _mesh_pos():
    return lax.axis_index("x"), lax.axis_index("y"), lax.axis_index("c")


def _other_chips(x, y):
    return [(1 - x, y), (x, 1 - y), (1 - x, 1 - y)]


def _remote(src, dst, send_sem, recv_sem, device):
    return pltpu.make_async_remote_copy(src_ref=src, dst_ref=dst, send_sem=send_sem, recv_sem=recv_sem,
                                        device_id=device, device_id_type=MESH)


def _split_axis(shape2):
    return 0 if (shape2[0] // 2) % 16 == 0 else 1


def _half_index(shape2, c):
    axis = _split_axis(shape2)
    h = shape2[axis] // 2
    return (pl.ds(c * h, h), slice(None)) if axis == 0 else (slice(None), pl.ds(c * h, h))


def _all_gather(xs):
    n = len(xs)

    def body(*refs):
        ins, outs = refs[:n], refs[n:2 * n]
        send_sems, recv_sems = refs[2 * n:]
        x, y, c = _mesh_pos()
        me = 2 * x + y
        chips = _other_chips(x, y)
        halves = []
        sends = []
        for a in range(n):
            h = ins[a].shape[0] // 2
            mine, other = pl.ds(c * h, h), pl.ds((1 - c) * h, h)
            halves.append((mine, other))
            own = _remote(ins[a], outs[a].at[me], send_sems.at[a, 6], recv_sems.at[a, 6], (x, y, 1 - c))
            own.start()
            sends.append(own)
            for j, chip in enumerate(chips):
                cp = _remote(ins[a].at[mine], outs[a].at[me, mine], send_sems.at[a, j], recv_sems.at[a, j], (*chip, c))
                cp.start()
                sends.append(cp)
        for a in range(n):
            mine, _ = halves[a]
            for j, chip in enumerate(chips):
                src = 2 * chip[0] + chip[1]
                landed = outs[a].at[src, mine]
                _remote(landed, landed, send_sems.at[a, j], recv_sems.at[a, j], (x, y, c)).wait_recv()
                fwd = _remote(landed, landed, send_sems.at[a, 3 + j], recv_sems.at[a, 3 + j], (x, y, 1 - c))
                fwd.start()
                sends.append(fwd)
        for a in range(n):
            _, other = halves[a]
            for j, chip in enumerate(chips):
                src = 2 * chip[0] + chip[1]
                landed = outs[a].at[src, other]
                _remote(landed, landed, send_sems.at[a, 3 + j], recv_sems.at[a, 3 + j], (x, y, c)).wait_recv()
            mine_slot = outs[a].at[me]
            _remote(mine_slot, mine_slot, send_sems.at[a, 6], recv_sems.at[a, 6], (x, y, c)).wait_recv()
        for cp in sends:
            cp.wait_send()

    return list(pl.pallas_call(
        body, name="all_gather_weights",
        in_specs=[ANY] * n, out_specs=[ANY] * n,
        out_shape=[jax.ShapeDtypeStruct((N_SHARDS,) + a.shape, a.dtype) for a in xs],
        scratch_shapes=[pltpu.SemaphoreType.DMA((n, 7)), pltpu.SemaphoreType.DMA((n, 7))],
    )(*xs))


def _rs_pair(gs):
    n = len(gs)

    def body(*refs):
        ins, lands = refs[:n], refs[n:2 * n]
        send_sems, recv_sems = refs[2 * n:]
        x, y, c = _mesh_pos()
        cps = []
        for a in range(n):
            theirs = (slice(None),) + _half_index(ins[a].shape[1:], 1 - c)
            cp = _remote(ins[a].at[theirs], lands[a], send_sems.at[a], recv_sems.at[a], (x, y, 1 - c))
            cp.start()
            cps.append(cp)
        for cp in cps:
            cp.wait()

    def half_shape(g):
        dims = list(g.shape)
        dims[1 + _split_axis(g.shape[1:])] //= 2
        return tuple(dims)

    return list(pl.pallas_call(
        body, name="rs_pair", in_specs=[ANY] * n, out_specs=[ANY] * n,
        out_shape=[jax.ShapeDtypeStruct(half_shape(g), g.dtype) for g in gs],
        scratch_shapes=[pltpu.SemaphoreType.DMA((n,)), pltpu.SemaphoreType.DMA((n,))],
    )(*gs))


def _rs_chips(ss):
    n = len(ss)

    def body(*refs):
        ins, outs = refs[:n], refs[n:2 * n]
        send_sems, recv_sems = refs[2 * n:]
        x, y, c = _mesh_pos()
        me = 2 * x + y
        chips = _other_chips(x, y)
        cps = []
        for a in range(n):
            for j, chip in enumerate(chips):
                dst_chip = 2 * chip[0] + chip[1]
                cp = _remote(ins[a].at[dst_chip], outs[a].at[me], send_sems.at[a, j], recv_sems.at[a, j], (*chip, c))
                cp.start()
                cps.append(cp)
        for a in range(n):
            for j, chip in enumerate(chips):
                src = 2 * chip[0] + chip[1]
                _remote(outs[a].at[src], outs[a].at[src], send_sems.at[a, j], recv_sems.at[a, j], (x, y, c)).wait_recv()
        for cp in cps:
            cp.wait_send()

    return list(pl.pallas_call(
        body, name="rs_chips", in_specs=[ANY] * n, out_specs=[ANY] * n,
        out_shape=[jax.ShapeDtypeStruct(s.shape, s.dtype) for s in ss],
        scratch_shapes=[pltpu.SemaphoreType.DMA((n, 3)), pltpu.SemaphoreType.DMA((n, 3))],
    )(*ss))


def _rs_join(fs, axes):
    n = len(fs)

    def whole(f, axis):
        dims = list(f.shape)
        dims[axis] *= 2
        return tuple(dims)

    def body(*refs):
        ins, outs = refs[:n], refs[n:2 * n]
        send_sems, recv_sems = refs[2 * n:]
        x, y, c = _mesh_pos()
        cps = []
        for a in range(n):
            h = ins[a].shape[axes[a]]
            mine = (pl.ds(c * h, h), slice(None)) if axes[a] == 0 else (slice(None), pl.ds(c * h, h))
            cp = _remote(ins[a], outs[a].at[mine], send_sems.at[a], recv_sems.at[a], (x, y, 1 - c))
            cp.start()
            cps.append(cp)
        for cp in cps:
            cp.wait()

    outs = pl.pallas_call(
        body, name="rs_join", in_specs=[ANY] * n, out_specs=[ANY] * n,
        out_shape=[jax.ShapeDtypeStruct(whole(f, ax), f.dtype) for f, ax in zip(fs, axes)],
        scratch_shapes=[pltpu.SemaphoreType.DMA((n,)), pltpu.SemaphoreType.DMA((n,))],
    )(*fs)
    c = lax.axis_index("c")
    return [lax.dynamic_update_slice_in_dim(o, f, c * f.shape[ax], ax) for o, f, ax in zip(outs, fs, axes)]


def _gather_exchange(xs):
    def start(cin, cout, send_sems, recv_sems):
        x, y, c = _mesh_pos()
        me = 2 * x + y
        for a, (src, dst) in enumerate(zip(cin, cout)):
            mine = _half_index(src.shape, c)
            for j, chip in enumerate(_other_chips(x, y)):
                _remote(src.at[mine], dst.at[(me,) + mine], send_sems.at[a, j], recv_sems.at[a, j], (*chip, c)).start()

    def finish(cin, cout, send_sems, recv_sems):
        x, y, c = _mesh_pos()
        for a, dst in enumerate(cout):
            for j, chip in enumerate(_other_chips(x, y)):
                landed = dst.at[(2 * chip[0] + chip[1],) + _half_index(dst.shape[1:], c)]
                _remote(landed, landed, send_sems.at[a, j], recv_sems.at[a, j], (x, y, c)).wait()

    return _Exchange(tuple(xs), tuple(jax.ShapeDtypeStruct((N_SHARDS,) + a.shape, a.dtype) for a in xs), start, finish)


def _gather_forward(gs, xs):
    n = len(gs)

    def body(*refs):
        shards, outs = refs[n:2 * n], refs[2 * n:3 * n]
        send_sems, recv_sems = refs[3 * n:]
        x, y, c = _mesh_pos()
        me = 2 * x + y
        chips = _other_chips(x, y)
        cps = []
        for a in range(n):
            own = _remote(shards[a], outs[a].at[me], send_sems.at[a, 3], recv_sems.at[a, 3], (x, y, 1 - c))
            own.start()
            cps.append(own)
            for j, chip in enumerate(chips):
                landed = outs[a].at[(2 * chip[0] + chip[1],) + _half_index(outs[a].shape[1:], c)]
                cp = _remote(landed, landed, send_sems.at[a, j], recv_sems.at[a, j], (x, y, 1 - c))
                cp.start()
                cps.append(cp)
        for a in range(n):
            for j, chip in enumerate(chips):
                other = outs[a].at[(2 * chip[0] + chip[1],) + _half_index(outs[a].shape[1:], 1 - c)]
                _remote(other, other, send_sems.at[a, j], recv_sems.at[a, j], (x, y, c)).wait_recv()
            mine_slot = outs[a].at[me]
            _remote(mine_slot, mine_slot, send_sems.at[a, 3], recv_sems.at[a, 3], (x, y, c)).wait_recv()
        for cp in cps:
            cp.wait_send()

    return list(pl.pallas_call(
        body, name="gather_forward", in_specs=[ANY] * (2 * n), out_specs=[ANY] * n,
        out_shape=[jax.ShapeDtypeStruct(g.shape, g.dtype) for g in gs],
        input_output_aliases={i: i for i in range(n)},
        scratch_shapes=[pltpu.SemaphoreType.DMA((n, 4)), pltpu.SemaphoreType.DMA((n, 4))],
    )(*gs, *xs))


def _scatter_exchange(ss):
    def start(cin, cout, send_sems, recv_sems):
        x, y, c = _mesh_pos()
        me = 2 * x + y
        for a, (src, dst) in enumerate(zip(cin, cout)):
            for j, chip in enumerate(_other_chips(x, y)):
                _remote(src.at[2 * chip[0] + chip[1]], dst.at[me], send_sems.at[a, j], recv_sems.at[a, j],
                        (*chip, c)).start()

    def finish(cin, cout, send_sems, recv_sems):
        x, y, c = _mesh_pos()
        for a, dst in enumerate(cout):
            for j, chip in enumerate(_other_chips(x, y)):
                slot = dst.at[2 * chip[0] + chip[1]]
                _remote(slot, slot, send_sems.at[a, j], recv_sems.at[a, j], (x, y, c)).wait()

    return _Exchange(tuple(ss), tuple(jax.ShapeDtypeStruct(s.shape, s.dtype) for s in ss), start, finish)


def _add_pair(g, land, name):
    nj = g.shape[0]
    shape2 = g.shape[1:]

    def body(g_ref, l_ref, o_ref):
        mine = g_ref[(0,) + _half_index(shape2, lax.axis_index("c"))]
        o_ref[0] = (mine.astype(F32) + l_ref[0].astype(F32)).astype(BF16)

    half = pl.BlockSpec((1,) + land.shape[1:], lambda j: (j, 0, 0))
    return pl.pallas_call(body, name=name, grid=(nj,),
                          in_specs=[pl.BlockSpec((1,) + shape2, lambda j: (j, 0, 0)), half], out_specs=half,
                          out_shape=jax.ShapeDtypeStruct(land.shape, BF16),
                          compiler_params=_params(("arbitrary",), VMEM_LIMIT))(g, land)


def _sum_slots(slots, own, name):
    nj, h, c = slots.shape
    th = h // 2 if h % 32 == 0 else h

    def body(s_ref, o_ref, out_ref):
        me = 2 * lax.axis_index("x") + lax.axis_index("y")
        acc = jnp.zeros((th, c), F32)
        for s in range(nj):
            acc = acc + jnp.where(me == s, o_ref[s], s_ref[s]).astype(F32)
        out_ref[...] = acc

    blk = pl.BlockSpec((nj, th, c), lambda i: (0, i, 0))
    return pl.pallas_call(body, name=name, grid=(h // th,), in_specs=[blk, blk],
                          out_specs=pl.BlockSpec((th, c), lambda i: (i, 0)),
                          out_shape=jax.ShapeDtypeStruct((h, c), F32),
                          compiler_params=_params(("arbitrary",), VMEM_LIMIT))(slots, own)


def _all_reduce_small(p):
    r = p.shape[0]

    def body(p_ref, o_ref, buf, send_sems, recv_sems):
        x, y, c = _mesh_pos()
        me = 4 * x + 2 * y + c
        buf[me] = p_ref[...]
        cps = []
        k = 0
        for fx in range(2):
            for fy in range(2):
                for fc in range(2):
                    if fx + fy + fc == 0:
                        continue
                    peer = (1 - x if fx else x, 1 - y if fy else y, 1 - c if fc else c)
                    peer_id = 4 * peer[0] + 2 * peer[1] + peer[2]
                    cp = _remote(p_ref, buf.at[me], send_sems.at[k], recv_sems.at[k], peer)
                    cp.start()
                    cps.append((cp, peer_id, k))
                    k += 1
        for cp, peer_id, k in cps:
            _remote(p_ref, buf.at[peer_id], send_sems.at[k], recv_sems.at[k], (x, y, c)).wait_recv()
        for cp, _, _ in cps:
            cp.wait_send()
        acc = buf[0]
        for s in range(1, 8):
            acc = acc + buf[s]
        o_ref[...] = acc

    vm = pl.BlockSpec(memory_space=pltpu.VMEM)
    return pl.pallas_call(
        body, name="all_reduce_small", in_specs=[vm], out_specs=vm,
        out_shape=jax.ShapeDtypeStruct(p.shape, F32),
        scratch_shapes=[pltpu.VMEM((8, r, LANES), F32), pltpu.SemaphoreType.DMA((7,)), pltpu.SemaphoreType.DMA((7,))],
    )(p)


def _adamw(w, g, m, v, name):
    r, c = w.shape
    row_tiles = [d for d in range(8, min(r, 256) + 1, 8) if r % d == 0]
    tr, tc = (max(row_tiles), c) if row_tiles else (r, 256 if c % 256 == 0 else c)
    c1 = 1.0 / (1.0 - ADAM_B1 ** ADAM_STEP)
    c2 = 1.0 / (1.0 - ADAM_B2 ** ADAM_STEP)

    def body(w_ref, g_ref, m_ref, v_ref, d_ref, nm_ref, nv_ref):
        gv = g_ref[...]
        nm = ADAM_B1 * m_ref[...] + (1.0 - ADAM_B1) * gv
        nv = ADAM_B2 * v_ref[...] + (1.0 - ADAM_B2) * (gv * gv)
        d_ref[...] = -ADAM_LR * ((nm * c1) / (jnp.sqrt(nv * c2) + ADAM_EPS) + ADAM_WD * w_ref[...])
        nm_ref[...] = nm
        nv_ref[...] = nv

    blk = pl.BlockSpec((tr, tc), lambda i, j: (i, j))
    shp = jax.ShapeDtypeStruct((r, c), F32)
    return pl.pallas_call(body, name=name, grid=(r // tr, c // tc), in_specs=[blk] * 4, out_specs=[blk] * 3,
                          out_shape=[shp, shp, shp],
                          compiler_params=_params(("arbitrary", "arbitrary"), VMEM_LIMIT))(w, g, m, v)


PACK_UNIT = 8 * LANES


def _pack(arrs):
    parts = []
    for a in arrs:
        f = a.reshape(-1).astype(F32)
        parts.append(jnp.pad(f, (0, (-f.shape[0]) % PACK_UNIT)).reshape(-1, LANES))
    return jnp.concatenate(parts, axis=0)


def _unpack(m, shapes):
    outs, row = [], 0
    for s in shapes:
        n = int(np.prod(s))
        rows = -(-n // PACK_UNIT) * 8
        outs.append(m[row:row + rows].reshape(-1)[:n].reshape(s))
        row += rows
    return outs


WEIGHTS = ["ffn1_norm", "ffn1_w_gate", "ffn1_w_up", "ffn1_w_down", "mix_norm", "w_in", "conv_w", "a_log", "dt_bias",
           "gdn_norm_w", "q_norm_w", "k_norm_w", "rel_bias", "w_out", "ffn2_norm", "ffn2_w_gate", "ffn2_w_up",
           "ffn2_w_down", "final_norm"]
BIG = ["ffn1_w_gate", "ffn1_w_up", "ffn1_w_down", "w_in", "w_out", "ffn2_w_gate", "ffn2_w_up", "ffn2_w_down"]
SMALL = [n for n in WEIGHTS if n not in BIG]
COL_SHARDED = ["ffn1_w_gate", "ffn1_w_up", "w_in", "ffn2_w_gate", "ffn2_w_up"]
N_IN_COLS = 3600
TM = 256
TE = 512
TK = 2048


def kernel(x, ffn1_norm, ffn1_w_gate, ffn1_w_up, ffn1_w_down, mix_norm, w_in, conv_w, a_log, dt_bias, gdn_norm_w, q_norm_w, k_norm_w, rel_bias, w_out, ffn2_norm, ffn2_w_gate, ffn2_w_up, ffn2_w_down, final_norm, loss_target, m_ffn1_norm, m_ffn1_w_gate, m_ffn1_w_up, m_ffn1_w_down, m_mix_norm, m_w_in, m_conv_w, m_a_log, m_dt_bias, m_gdn_norm_w, m_q_norm_w, m_k_norm_w, m_rel_bias, m_w_out, m_ffn2_norm, m_ffn2_w_gate, m_ffn2_w_up, m_ffn2_w_down, m_final_norm, v_ffn1_norm, v_ffn1_w_gate, v_ffn1_w_up, v_ffn1_w_down, v_mix_norm, v_w_in, v_conv_w, v_a_log, v_dt_bias, v_gdn_norm_w, v_q_norm_w, v_k_norm_w, v_rel_bias, v_w_out, v_ffn2_norm, v_ffn2_w_gate, v_ffn2_w_up, v_ffn2_w_down, v_final_norm):
    p = dict(locals())
    xs, target = x[0], loss_target[0]
    t, d = xs.shape
    nc = t // CHUNK
    tk = min(TK, t)
    me = 2 * lax.axis_index("x") + lax.axis_index("y")

    first = ["ffn1_w_gate", "ffn1_w_up", "ffn1_w_down"]
    later = [n for n in BIG if n not in first] + ["conv_w"]
    local = lambda n, a: a[0].T if n in COL_SHARDED else a[0]
    shards = {n: local(n, p[n]).astype(BF16) for n in BIG}
    shards["conv_w"] = conv_w[0]
    gw = dict(zip(first, _all_gather([shards[n] for n in first])))
    f1 = (gw["ffn1_w_gate"], gw["ffn1_w_up"], gw["ffn1_w_down"])
    (x1, xn1, g1, u1), landed = _ffn_fwd(xs, ffn1_norm, *f1, TM, "ffn1_fwd",
                                         exchange=_gather_exchange([shards[n] for n in later]))
    gw.update(zip(later, _gather_forward(landed, [shards[n] for n in later])))
    w_in_t = gw["w_in"].reshape(N_IN_COLS, d)
    gates = slice(P_AB[0], P_AB[0] + N_GATE_COLS)
    wp = jnp.concatenate([w_in_t[:gates.start], jnp.pad(w_in_t[gates], ((0, LANES - N_GATE_COLS), (0, 0))),
                          w_in_t[gates.stop:]], axis=0)
    w_out_full = gw["w_out"].reshape(d, d)
    conv_rows = conv_w.shape[1]
    cw = jnp.pad(gw["conv_w"].reshape(N_SHARDS * conv_rows, CONV_TAPS).T, ((0, 8 - CONV_TAPS), (0, 0)))
    gp = jnp.pad(jnp.stack([a_log.reshape(8), dt_bias.reshape(8)]), ((0, 6), (0, LANES - 8)))
    gdn_w = gdn_norm_w.reshape(1, GDN_DIM)
    qw_t = jnp.tile(q_norm_w.reshape(1, SWA_DIM), (1, SWA_HEADS))
    kw_t = jnp.tile(k_norm_w.reshape(1, SWA_DIM), (1, SWA_HEADS))
    bd = jnp.asarray(np.kron(np.eye(2), np.full((SWA_DIM, SWA_DIM), 1.0 / SWA_DIM)), BF16)
    f2 = (gw["ffn2_w_gate"], gw["ffn2_w_up"], gw["ffn2_w_down"])

    hn, qkva, z, ab, qkvb = _mix_in_fwd(x1, mix_norm, wp, TM)
    qkvc, gb = _gdn_prep_fwd(qkva, cw, ab, gp, TM)
    gbt = jnp.transpose(gb[:, :16].reshape(nc, CHUNK, 16), (0, 2, 1))
    o_f, o_b, gdn_saved = _gdn_fwd(qkvc, gb, gbt)
    oa = _gdn_post_fwd(o_f, o_b, z, gdn_w, TE)
    o_swa, o_swa16, swa_saved = _swa_branch_fwd(qkvb, qw_t, kw_t, rel_bias, bd, TE)
    x2 = _mix_out_fwd(x1, oa, o_swa, w_out_full, TE)
    (dx3, xn2, g2, u2, loss_part, d_final), _ = _ffn_fwd(x2, ffn2_norm, *f2, TM, "ffn2_fwd", head=(final_norm, target))

    def pair_sums(partials, tag):
        return [_add_pair(g, land, f"rs_add_{tag}{i}") for i, (g, land) in enumerate(zip(partials, _rs_pair(partials)))]

    (dx2, dyh2, dg2, du2, h2, d_nw2), _ = _ffn_bwd_dx(dx3, x2, ffn2_norm, g2, u2, *f2, TM, "ffn2_bwd_dx")
    dwg2 = _matmul_tn(dg2, xn2, tk, "ffn2_dwg")
    dwu2 = _matmul_tn(du2, xn2, tk, "ffn2_dwu")
    dwd2 = _matmul_tn(h2, dyh2, tk, "ffn2_dwd")
    sums_f2 = pair_sums([dwg2, dwu2, dwd2], "a")
    doa, dob, dx2b = _mix_out_bwd(dx2, w_out_full, TE)
    dwo = jnp.concatenate([_matmul_tn(oa, dx2b, tk, "w_out_dw_a")[0], _matmul_tn(o_swa16, dx2b, tk, "w_out_dw_b")[0]],
                          axis=0).reshape(N_SHARDS, d // N_SHARDS, d)
    do_g, dz, d_gdnw = _gdn_post_bwd(doa, o_f, o_b, z, gdn_w, TE)
    (dqkvc, dgates), slots_f2 = _gdn_bwd(qkvc, gb, gbt, do_g, gdn_saved, exchange=_scatter_exchange(sums_f2))
    dqkva, dab, dcw, dgp = _gdn_prep_bwd(qkva, cw, ab, gp, dqkvc, dgates, TM)
    dqkvb, d_qw, d_kw, d_rel = _swa_branch_bwd(dob, o_swa, swa_saved, qkvb, qw_t, kw_t, bd, TE)
    dpieces = (dqkva, dz, dab, dqkvb)
    dwp = [_matmul_tn(dp, hn, tk, f"w_in_dw_{i}")[0] for i, dp in enumerate(dpieces)]
    dw_in = jnp.concatenate([dwp[0], dwp[1], dwp[2][:N_GATE_COLS], dwp[3]], axis=0)
    dw_in = dw_in.reshape(N_SHARDS, N_IN_COLS // N_SHARDS, d)
    sums_mix = pair_sums([dw_in, dwo], "b")
    (dx1, d_mixnw), slots_mix = _mix_in_bwd_dx(dx2, x1, mix_norm, dpieces, wp, TM, exchange=_scatter_exchange(sums_mix))
    (gx, dyh1, dg1, du1, h1, d_nw1), _ = _ffn_bwd_dx(dx1, xs, ffn1_norm, g1, u1, *f1, TM, "ffn1_bwd_dx")
    dwg1 = _matmul_tn(dg1, xn1, tk, "ffn1_dwg")
    dwu1 = _matmul_tn(du1, xn1, tk, "ffn1_dwu")
    sums_gu = pair_sums([dwg1, dwu1], "c")
    dwd1, slots_gu = _matmul_tn(h1, dyh1, tk, "ffn1_dwd", exchange=_scatter_exchange(sums_gu))
    sums_d = pair_sums([dwd1], "d")
    slots = slots_gu + _rs_chips(sums_d) + slots_mix + slots_f2
    sums = sums_gu + sums_d + sums_mix + sums_f2
    halves = [_sum_slots(s, own, f"rs_sum_{i}") for i, (s, own) in enumerate(zip(slots, sums))]
    g_big = dict(zip(BIG, _rs_join(halves, [_split_axis(shards[n].shape) for n in BIG])))

    small_partial = {"ffn1_norm": d_nw1, "mix_norm": d_mixnw, "a_log": dgp[0, 0:8], "dt_bias": dgp[1, 0:8],
                     "gdn_norm_w": d_gdnw, "q_norm_w": d_qw, "k_norm_w": d_kw, "rel_bias": d_rel,
                     "ffn2_norm": d_nw2, "final_norm": d_final, "conv_w": dcw[0:CONV_TAPS].T}
    red = _all_reduce_small(_pack([small_partial[n] for n in SMALL] + [loss_part[0, 0:1]]))
    full_shapes = [p[n].shape if n != "conv_w" else (N_SHARDS * conv_rows, CONV_TAPS) for n in SMALL]
    red_parts = _unpack(red, full_shapes + [(1,)])
    loss = red_parts[-1].reshape(())
    g_small = dict(zip(SMALL, red_parts[:-1]))
    g_small["conv_w"] = lax.dynamic_slice_in_dim(g_small["conv_w"], me * conv_rows, conv_rows, 0).reshape(conv_w.shape)

    grads, deltas, new_m, new_v = {}, {}, {}, {}
    for n in BIG:
        back = (lambda a: a.T[None]) if n in COL_SHARDED else (lambda a: a[None])
        grads[n] = back(g_big[n])
        dl, nm, nv = _adamw(local(n, p[n]), g_big[n], local(n, p["m_" + n]), local(n, p["v_" + n]), "adamw_" + n)
        deltas[n], new_m[n], new_v[n] = back(dl), back(nm), back(nv)
    packed = [_pack([src[n] for n in SMALL]) for src in
              ({n: p[n] for n in SMALL}, g_small, {n: p["m_" + n] for n in SMALL}, {n: p["v_" + n] for n in SMALL})]
    small_shapes = [p[n].shape for n in SMALL]
    for dst, arr in zip((deltas, new_m, new_v), _adamw(*packed, "adamw_small")):
        dst.update(zip(SMALL, _unpack(arr, small_shapes)))
    grads.update(g_small)

    return (loss, gx[None], *[grads[n] for n in WEIGHTS], *[deltas[n] for n in WEIGHTS],
            *[new_m[n] for n in WEIGHTS], *[new_v[n] for n in WEIGHTS])
```

```python
import math
from typing import Callable, NamedTuple

import numpy as np
import jax
import jax.numpy as jnp
from jax import lax
from jax.experimental import pallas as pl
from jax.experimental.pallas import tpu as pltpu

F32 = jnp.float32
BF16 = jnp.bfloat16
HIGHEST = lax.Precision.HIGHEST
MESH = pl.DeviceIdType.MESH

EPS = 1e-6
NEG_BIG = -1e30
GDN_HEADS = 4
GDN_DIM = 128
CHUNK = 64
SWA_HEADS = 8
SWA_DIM = 64
PATTERNS = ((128, 1), (512, 4), (2048, 16))
RADIUS = 64
REL_BUCKETS = 32
REL_MAX_DISTANCE = 1024
CONV_TAPS = 5
N_SHARDS = 4
LANES = 128
VMEM_LIMIT = 56 * 1024 * 1024

ADAM_LR, ADAM_B1, ADAM_B2, ADAM_EPS, ADAM_WD, ADAM_STEP = 0.001, 0.9, 0.999, 1e-08, 0.01, 10


def _params(sem=None, vmem=None):
    return pltpu.CompilerParams(dimension_semantics=sem, vmem_limit_bytes=vmem)


def _resident(shape):
    nd = len(shape)
    return pl.BlockSpec(shape, lambda *_: (0,) * nd, pipeline_mode=pl.Buffered(1))


ANY = pl.BlockSpec(memory_space=pl.ANY)


class _Exchange(NamedTuple):
    arrays: tuple
    out_shape: tuple
    start: Callable
    finish: Callable


def _grid_call(body, name, nsteps, in_specs, out_specs, out_shape, operands, scratch=(), exchange=None):
    params = _params(("arbitrary",), VMEM_LIMIT)
    if exchange is None:
        res = pl.pallas_call(body, name=name, grid=(nsteps,), in_specs=list(in_specs), out_specs=list(out_specs),
                             out_shape=list(out_shape), scratch_shapes=list(scratch), compiler_params=params)(*operands)
        return list(res), []
    n_in, n_out, k, n_scr = len(in_specs), len(out_specs), len(exchange.arrays), len(scratch)

    def wrapped(*refs):
        ins, cin = refs[:n_in], refs[n_in:n_in + k]
        outs, cout = refs[n_in + k:n_in + k + n_out], refs[n_in + k + n_out:n_in + 2 * k + n_out]
        rest = refs[n_in + 2 * k + n_out:]
        scr, (send_sems, recv_sems) = rest[:n_scr], rest[n_scr:]

        @pl.when(pl.program_id(0) == 0)
        def _():
            exchange.start(cin, cout, send_sems, recv_sems)

        body(*ins, *outs, *scr)

        @pl.when(pl.program_id(0) == nsteps - 1)
        def _():
            exchange.finish(cin, cout, send_sems, recv_sems)

    res = pl.pallas_call(
        wrapped, name=name, grid=(nsteps,), in_specs=list(in_specs) + [ANY] * k, out_specs=list(out_specs) + [ANY] * k,
        out_shape=list(out_shape) + list(exchange.out_shape),
        scratch_shapes=list(scratch) + [pltpu.SemaphoreType.DMA((k, 3)), pltpu.SemaphoreType.DMA((k, 3))],
        compiler_params=params)(*operands, *exchange.arrays)
    return list(res[:n_out]), list(res[n_out:])


def _dot(a, b):
    return jnp.dot(a.astype(BF16), b.astype(BF16), preferred_element_type=F32)


def _dot_nt(a, b):
    return lax.dot_general(a.astype(BF16), b.astype(BF16), (((1,), (1,)), ((), ())), preferred_element_type=F32)


def _dot_tn(a, b):
    return lax.dot_general(a.astype(BF16), b.astype(BF16), (((0,), (0,)), ((), ())), preferred_element_type=F32)


def _dot_hi(a, b):
    return jnp.dot(a, b, preferred_element_type=F32, precision=HIGHEST)


def _sigmoid(x):
    return 1.0 / (1.0 + jnp.exp(-x))


def _rstd(xf):
    return lax.rsqrt(jnp.mean(xf * xf, axis=-1, keepdims=True) + EPS)


def _rms_bwd(xf, r, nw, dxn):
    xhat = xf * r
    dxh = dxn * nw
    dx = r * (dxh - xhat * jnp.mean(dxh * xhat, axis=-1, keepdims=True))
    return dx, jnp.sum(dxn * xhat, axis=0, keepdims=True)


def _ffn_fwd(x, nw, wg, wu, wd, tm, name, exchange=None, head=None):
    t, d = x.shape
    nj, fs, _ = wg.shape

    def body(x_ref, nw_ref, wg_ref, wu_ref, wd_ref, *rest):
        if head is None:
            y_ref, xn_ref, g_ref, u_ref = rest
        else:
            fw_ref, t_ref, y_ref, xn_ref, g_ref, u_ref, loss_ref, dfw_ref = rest

            @pl.when(pl.program_id(0) == 0)
            def _():
                loss_ref[...] = jnp.zeros_like(loss_ref)
                dfw_ref[...] = jnp.zeros_like(dfw_ref)

        xf = x_ref[...]
        xn = (xf * _rstd(xf) * nw_ref[...]).astype(BF16)
        xn_ref[...] = xn
        acc = jnp.zeros((tm, d), F32)
        for j in range(nj):
            g = _dot_nt(xn, wg_ref[j])
            u = _dot_nt(xn, wu_ref[j])
            h = (g * _sigmoid(g) * u).astype(BF16)
            acc = acc + jnp.dot(h, wd_ref[j], preferred_element_type=F32)
            g_ref[j] = g.astype(BF16)
            u_ref[j] = u.astype(BF16)
        y = xf + 0.5 * acc
        if head is None:
            y_ref[...] = y
        else:
            r = _rstd(y)
            err = y * r * fw_ref[...] - t_ref[...]
            loss_ref[...] += 0.5 * jnp.sum(jnp.mean(err * err, axis=-1, keepdims=True), axis=0, keepdims=True)
            dy, dfw = _rms_bwd(y, r, fw_ref[...], err * (1.0 / d))
            y_ref[...] = dy
            dfw_ref[...] += dfw

    row = pl.BlockSpec((tm, d), lambda i: (i, 0))
    act = pl.BlockSpec((nj, tm, fs), lambda i: (0, i, 0))
    in_specs = [row, _resident((1, d)), _resident(wg.shape), _resident(wu.shape), _resident(wd.shape)]
    out_specs = [row, row, act, act]
    out_shape = [jax.ShapeDtypeStruct((t, d), F32), jax.ShapeDtypeStruct((t, d), BF16),
                 jax.ShapeDtypeStruct((nj, t, fs), BF16), jax.ShapeDtypeStruct((nj, t, fs), BF16)]
    operands = (x, nw, wg, wu, wd)
    if head is not None:
        in_specs += [_resident((1, d)), row]
        out_specs += [pl.BlockSpec((1, LANES), lambda i: (0, 0)), pl.BlockSpec((1, d), lambda i: (0, 0))]
        out_shape += [jax.ShapeDtypeStruct((1, LANES), F32), jax.ShapeDtypeStruct((1, d), F32)]
        operands += tuple(head)
    return _grid_call(body, name, t // tm, in_specs, out_specs, out_shape, operands, exchange=exchange)


def _ffn_bwd_dx(dy, x, nw, g, u, wg, wu, wd, tm, name, exchange=None):
    t, d = x.shape
    nj, fs, _ = wg.shape

    def body(dy_ref, x_ref, nw_ref, g_ref, u_ref, wg_ref, wu_ref, wd_ref,
             dx_ref, dyh_ref, dg_ref, du_ref, h_ref, dnw_ref):
        @pl.when(pl.program_id(0) == 0)
        def _():
            dnw_ref[...] = jnp.zeros_like(dnw_ref)

        dyv = dy_ref[...]
        dyh = (0.5 * dyv).astype(BF16)
        dyh_ref[...] = dyh
        dxn = jnp.zeros((tm, d), F32)
        dh_next = _dot_nt(dyh, wd_ref[0])
        for j in range(nj):
            dh = dh_next
            gv = g_ref[j].astype(F32)
            uv = u_ref[j].astype(F32)
            sg = _sigmoid(gv)
            si = gv * sg
            dg = (dh * uv * (sg * (1.0 + gv * (1.0 - sg)))).astype(BF16)
            du = (dh * si).astype(BF16)
            if j + 1 < nj:
                dh_next = _dot_nt(dyh, wd_ref[j + 1])
            h_ref[j] = (si * uv).astype(BF16)
            dg_ref[j] = dg
            du_ref[j] = du
            dxn = dxn + _dot(dg, wg_ref[j]) + _dot(du, wu_ref[j])
        xf = x_ref[...]
        dxr, dnw = _rms_bwd(xf, _rstd(xf), nw_ref[...], dxn)
        dx_ref[...] = dyv + dxr
        dnw_ref[...] += dnw

    row = pl.BlockSpec((tm, d), lambda i: (i, 0))
    act = pl.BlockSpec((nj, tm, fs), lambda i: (0, i, 0))
    act_shape = jax.ShapeDtypeStruct((nj, t, fs), BF16)
    return _grid_call(
        body, name, t // tm,
        [row, row, _resident((1, d)), act, act, _resident(wg.shape), _resident(wu.shape), _resident(wd.shape)],
        [row, row, act, act, act, pl.BlockSpec((1, d), lambda i: (0, 0))],
        [jax.ShapeDtypeStruct((t, d), F32), jax.ShapeDtypeStruct((t, d), BF16),
         act_shape, act_shape, act_shape, jax.ShapeDtypeStruct((1, d), F32)],
        (dy, x, nw, g, u, wg, wu, wd), exchange=exchange)


def _matmul_tn(a, b, tk, name, exchange=None):
    a3, b3 = a.ndim == 3, b.ndim == 3
    nj = a.shape[0] if a3 else (b.shape[0] if b3 else 1)
    t, m = a.shape[-2:]
    n = b.shape[-1]
    nt = t // tk

    def body(a_ref, b_ref, o_ref, acc_ref):
        k = pl.program_id(0) % nt

        @pl.when(k == 0)
        def _():
            acc_ref[...] = jnp.zeros_like(acc_ref)

        acc_ref[...] += lax.dot_general(a_ref[...], b_ref[...], (((0,), (0,)), ((), ())),
                                        preferred_element_type=F32)

        @pl.when(k == nt - 1)
        def _():
            o_ref[...] = acc_ref[...].astype(o_ref.dtype)

    a_spec = (pl.BlockSpec((None, tk, m), lambda i: (i // nt, i % nt, 0)) if a3
              else pl.BlockSpec((tk, m), lambda i: (i % nt, 0)))
    b_spec = (pl.BlockSpec((None, tk, n), lambda i: (i // nt, i % nt, 0)) if b3
              else pl.BlockSpec((tk, n), lambda i: (i % nt, 0)))
    (out,), landed = _grid_call(
        body, name, nj * nt, [a_spec, b_spec], [pl.BlockSpec((None, m, n), lambda i: (i // nt, 0, 0))],
        [jax.ShapeDtypeStruct((nj, m, n), BF16)], (a, b), scratch=[pltpu.VMEM((m, n), F32)], exchange=exchange)
    return out if exchange is None else (out, landed)


P_QKVA, P_Z, P_AB, P_QKVB = (0, 1536), (1536, 2048), (2048, 2176), (2176, 3712)
P_PIECES = (P_QKVA, P_Z, P_AB, P_QKVB)
N_GATE_COLS = 4 * GDN_HEADS
P_COLS = 3712


def _mix_in_fwd(x1, nw, wp, tm):
    t, d = x1.shape

    def body(x_ref, nw_ref, w_ref, hn_ref, *outs):
        xf = x_ref[...]
        xn = (xf * _rstd(xf) * nw_ref[...]).astype(BF16)
        hn_ref[...] = xn
        for (a, b), o_ref in zip(P_PIECES, outs):
            o_ref[...] = _dot_nt(xn, w_ref[a:b, :])

    row = pl.BlockSpec((tm, d), lambda i: (i, 0))
    return pl.pallas_call(
        body, name="mix_in_fwd", grid=(t // tm,),
        in_specs=[row, _resident((1, d)), _resident(wp.shape)],
        out_specs=[row] + [pl.BlockSpec((tm, b - a), lambda i: (i, 0)) for a, b in P_PIECES],
        out_shape=[jax.ShapeDtypeStruct((t, d), BF16)]
                  + [jax.ShapeDtypeStruct((t, b - a), F32) for a, b in P_PIECES],
        compiler_params=_params(("arbitrary",), VMEM_LIMIT),
    )(x1, nw, wp)


def _mix_in_bwd_dx(dx, x1, nw, dpieces, wp, tm, exchange=None):
    t, d = x1.shape

    def body(dx_ref, x_ref, nw_ref, p0, p1, p2, p3, w_ref, o_ref, dnw_ref):
        @pl.when(pl.program_id(0) == 0)
        def _():
            dnw_ref[...] = jnp.zeros_like(dnw_ref)

        dh = jnp.zeros((tm, d), F32)
        for (a, b), p_ref in zip(P_PIECES, (p0, p1, p2, p3)):
            dh = dh + _dot(p_ref[...], w_ref[a:b, :])
        xf = x_ref[...]
        dxr, dnw = _rms_bwd(xf, _rstd(xf), nw_ref[...], dh)
        o_ref[...] = dx_ref[...] + dxr
        dnw_ref[...] += dnw

    row = pl.BlockSpec((tm, d), lambda i: (i, 0))
    return _grid_call(
        body, "mix_in_bwd_dx", t // tm,
        [row, row, _resident((1, d))]
        + [pl.BlockSpec((tm, b - a), lambda i: (i, 0)) for a, b in P_PIECES] + [_resident(wp.shape)],
        [row, pl.BlockSpec((1, d), lambda i: (0, 0))],
        [jax.ShapeDtypeStruct((t, d), F32), jax.ShapeDtypeStruct((1, d), F32)],
        (dx, x1, nw, *dpieces, wp), exchange=exchange)


def _mix_out_fwd(x1, oa, ob, w, tm):
    t, d = x1.shape
    half = oa.shape[1]

    def body(x_ref, oa_ref, ob_ref, w_ref, o_ref):
        o_ref[...] = (x_ref[...] + _dot(oa_ref[...], w_ref[0:half, :]) + _dot(ob_ref[...], w_ref[half:2 * half, :]))

    row = pl.BlockSpec((tm, d), lambda i: (i, 0))
    hrow = pl.BlockSpec((tm, half), lambda i: (i, 0))
    return pl.pallas_call(
        body, name="mix_out_fwd", grid=(t // tm,),
        in_specs=[row, hrow, hrow, _resident(w.shape)],
        out_specs=row, out_shape=jax.ShapeDtypeStruct((t, d), F32),
        compiler_params=_params(("arbitrary",), VMEM_LIMIT),
    )(x1, oa, ob, w)


def _mix_out_bwd(dx2, w, tm):
    t, d = dx2.shape
    half = w.shape[0] // 2

    def body(dx_ref, w_ref, doa_ref, dob_ref, dxb_ref):
        dxb = dx_ref[...].astype(BF16)
        dxb_ref[...] = dxb
        doa_ref[...] = _dot_nt(dxb, w_ref[0:half, :])
        dob_ref[...] = _dot_nt(dxb, w_ref[half:2 * half, :])

    row = pl.BlockSpec((tm, d), lambda i: (i, 0))
    hrow = pl.BlockSpec((tm, half), lambda i: (i, 0))
    return pl.pallas_call(
        body, name="mix_out_bwd", grid=(t // tm,),
        in_specs=[row, _resident(w.shape)],
        out_specs=[hrow, hrow, row],
        out_shape=[jax.ShapeDtypeStruct((t, half), F32), jax.ShapeDtypeStruct((t, half), F32),
                   jax.ShapeDtypeStruct((t, d), BF16)],
        compiler_params=_params(("arbitrary",), VMEM_LIMIT),
    )(dx2, w)


HALO = 8


def _halo_row_specs(tr, cols, nrow8):
    per = tr // HALO
    return [pl.BlockSpec((tr, cols), lambda i: (i, 0)),
            pl.BlockSpec((HALO, cols), lambda i: (jnp.maximum(i * per - 1, 0), 0)),
            pl.BlockSpec((HALO, cols), lambda i: (jnp.minimum((i + 1) * per, nrow8 - 1), 0))]


def _conv_window(xm, xp, xn, first, last, cols):
    prev = jnp.where(first, 0.0, xp[:, cols])
    nxt = jnp.where(last, 0.0, xn[:, cols])
    return jnp.concatenate([prev, xm[:, cols], nxt], axis=0)


def _shift_rows(xw, off):
    n = xw.shape[0]
    sh = (-off) % n
    return xw if sh == 0 else pltpu.roll(xw, sh, 0)


def _conv_pre(xw, cw_ref, cols):
    acc, shifted = None, []
    for j in range(CONV_TAPS):
        shifted.append(_shift_rows(xw, j - CONV_TAPS // 2))
        term = shifted[-1] * cw_ref[j:j + 1, cols]
        acc = term if acc is None else acc + term
    return acc, shifted


def _softplus(x):
    u = jnp.exp(-jnp.abs(x))
    w = 1.0 + u
    log1p = jnp.where(w == 1.0, u, jnp.log(w) * u / jnp.where(w == 1.0, 1.0, w - 1.0))
    return jnp.maximum(x, 0.0) + log1p


def _gdn_prep_fwd(qkva, cw, ab, gp, tr):
    t, c = qkva.shape
    nt = t // tr
    ncb = c // LANES

    def body(xm, xp, xn, cw_ref, ab_ref, gp_ref, o_ref, gb_ref):
        i = pl.program_id(0)
        first, last = i == 0, i == nt - 1
        for cb in range(ncb):
            cols = slice(cb * LANES, (cb + 1) * LANES)
            xw = _conv_window(xm, xp, xn, first, last, cols)
            pre = _conv_pre(xw, cw_ref, cols)[0][HALO:HALO + tr]
            y = pre * _sigmoid(pre)
            if cb < 2 * GDN_HEADS:
                y = y * lax.rsqrt(jnp.sum(y * y, axis=-1, keepdims=True) + EPS)
            if cb < GDN_HEADS:
                y = y * (GDN_DIM ** -0.5)
            o_ref[:, cols] = y
        abv = ab_ref[...]
        lane = lax.broadcasted_iota(jnp.int32, abv.shape, 1)
        g = -jnp.exp(gp_ref[0:1, :]) * _softplus(abv + gp_ref[1:2, :])
        gb_ref[...] = jnp.where(lane < 8, g, jnp.where(lane < 16, _sigmoid(abv), 0.0))

    return pl.pallas_call(
        body, name="gdn_prep_fwd", grid=(nt,),
        in_specs=_halo_row_specs(tr, c, t // HALO)
                 + [_resident(cw.shape), pl.BlockSpec((tr, LANES), lambda i: (i, 0)), _resident(gp.shape)],
        out_specs=[pl.BlockSpec((tr, c), lambda i: (i, 0)), pl.BlockSpec((tr, LANES), lambda i: (i, 0))],
        out_shape=[jax.ShapeDtypeStruct((t, c), F32), jax.ShapeDtypeStruct((t, LANES), F32)],
        compiler_params=_params(("arbitrary",), VMEM_LIMIT),
    )(qkva, qkva, qkva, cw, ab, gp)


def _gdn_prep_bwd(qkva, cw, ab, gp, dy, dgates, tr):
    t, c = qkva.shape
    nt = t // tr
    ncb = c // LANES

    def body(xm, xp, xn, fm, fp, fn, cw_ref, ab_ref, gp_ref, gf_ref, dx_ref, dab_ref, dcw_ref, dgp_ref):
        i = pl.program_id(0)
        first, last = i == 0, i == nt - 1

        @pl.when(first)
        def _():
            dcw_ref[...] = jnp.zeros_like(dcw_ref)
            dgp_ref[...] = jnp.zeros_like(dgp_ref)

        sub8 = lax.broadcasted_iota(jnp.int32, (8, LANES), 0)
        for cb in range(ncb):
            cols = slice(cb * LANES, (cb + 1) * LANES)
            xw = _conv_window(xm, xp, xn, first, last, cols)
            dyw = _conv_window(fm, fp, fn, first, last, cols)
            pre, x_shifted = _conv_pre(xw, cw_ref, cols)
            sg = _sigmoid(pre)
            s = pre * sg
            if cb < 2 * GDN_HEADS:
                scale = (GDN_DIM ** -0.5) if cb < GDN_HEADS else 1.0
                r = lax.rsqrt(jnp.sum(s * s, axis=-1, keepdims=True) + EPS)
                dn = dyw * scale
                ds = r * dn - s * (r * r * r) * jnp.sum(dn * s, axis=-1, keepdims=True)
            else:
                ds = dyw
            dpre = ds * (sg * (1.0 + pre * (1.0 - sg)))
            dx = None
            dcw = jnp.zeros((8, LANES), F32)
            for j in range(CONV_TAPS):
                off = j - CONV_TAPS // 2
                term = _shift_rows(dpre, -off)[HALO:HALO + tr] * cw_ref[j:j + 1, cols]
                dx = term if dx is None else dx + term
                tap = jnp.sum(dpre[HALO:HALO + tr] * x_shifted[j][HALO:HALO + tr], axis=0, keepdims=True)
                dcw = dcw + jnp.where(sub8 == j, tap, 0.0)
            dx_ref[:, cols] = dx.astype(BF16)
            dcw_ref[:, cols] += dcw

        abv = ab_ref[...]
        dgb = gf_ref[...]
        lane = lax.broadcasted_iota(jnp.int32, abv.shape, 1)
        nea = -jnp.exp(gp_ref[0:1, :])
        xs = abv + gp_ref[1:2, :]
        g = nea * _softplus(xs)
        beta = _sigmoid(abv)
        da = dgb * nea * _sigmoid(xs)
        dab = jnp.where(lane < 8, da, jnp.where(lane < 16, dgb * beta * (1.0 - beta), 0.0))
        dab_ref[...] = dab.astype(BF16)
        keep = lane[0:1, :] < 8
        dalog = jnp.where(keep, jnp.sum(dgb * g, axis=0, keepdims=True), 0.0)
        ddtb = jnp.where(keep, jnp.sum(da, axis=0, keepdims=True), 0.0)
        dgp_ref[...] += jnp.where(sub8 == 0, dalog, 0.0) + jnp.where(sub8 == 1, ddtb, 0.0)

    lrow = pl.BlockSpec((tr, LANES), lambda i: (i, 0))
    halo = _halo_row_specs(tr, c, t // HALO)
    return pl.pallas_call(
        body, name="gdn_prep_bwd", grid=(nt,),
        in_specs=halo + halo + [_resident(cw.shape), lrow, _resident(gp.shape), lrow],
        out_specs=[pl.BlockSpec((tr, c), lambda i: (i, 0)), lrow,
                   pl.BlockSpec(cw.shape, lambda i: (0, 0)), pl.BlockSpec(gp.shape, lambda i: (0, 0))],
        out_shape=[jax.ShapeDtypeStruct((t, c), BF16), jax.ShapeDtypeStruct((t, LANES), BF16),
                   jax.ShapeDtypeStruct(cw.shape, F32), jax.ShapeDtypeStruct(gp.shape, F32)],
        compiler_params=_params(("arbitrary",), VMEM_LIMIT),
    )(qkva, qkva, qkva, dy, dy, dy, cw, ab, gp, dgates)


def _chunk_masks(lower):
    ii = lax.broadcasted_iota(jnp.int32, (CHUNK, CHUNK), 0)
    jj = lax.broadcasted_iota(jnp.int32, (CHUNK, CHUNK), 1)
    incl = (ii >= jj) if lower else (ii <= jj)
    strict = (ii > jj) if lower else (ii < jj)
    return ii, jj, incl, strict


def _dot3(a, b):
    ah = a.astype(BF16)
    al = (a - ah.astype(F32)).astype(BF16)
    bh = b.astype(BF16)
    bl = (b - bh.astype(F32)).astype(BF16)
    d = lambda u, v: jnp.dot(u, v, preferred_element_type=F32)
    return d(ah, bh) + (d(ah, bl) + d(al, bh))


def _tri_inv_many(lmats, ii, jj):
    m16 = (ii // 16) == (jj // 16)
    m32 = (ii // 32) == (jj // 32)
    eye = jnp.where(ii == jj, 1.0, 0.0)
    l16 = [jnp.where(m16, l, 0.0) for l in lmats]
    p2 = [_dot3(a, a) for a in l16]
    p4 = [_dot3(a, a) for a in p2]
    p8 = [_dot3(a, a) for a in p4]
    xs = [eye - a for a in l16]
    for ps in (p2, p4, p8):
        xs = [x + _dot3(x, p) for x, p in zip(xs, ps)]
    for off in ([jnp.where(m32 & jnp.logical_not(m16), l, 0.0) for l in lmats],
                [jnp.where(m32, 0.0, l) for l in lmats]):
        ys = [_dot3(x, c) for x, c in zip(xs, off)]
        xs = [x - _dot3(y, x) for x, y in zip(xs, ys)]
    return xs


def _col_to_row(col, ii, jj):
    return jnp.sum(jnp.where(ii == jj, col, 0.0), axis=0, keepdims=True)


def _row_to_col(row, ii, jj):
    return jnp.sum(jnp.where(ii == jj, row, 0.0), axis=1, keepdims=True)


def _chain_common(q, k, v, graw_col, graw_row, bcol, masks):
    ii, jj, incl, strict = masks
    inclt = jnp.logical_not(strict)
    gcol = jnp.sum(jnp.where(incl, graw_row, 0.0), axis=1, keepdims=True)
    grow = jnp.sum(jnp.where(inclt, graw_col, 0.0), axis=0, keepdims=True)
    glast = jnp.sum(graw_row, axis=1, keepdims=True)
    decay = jnp.where(incl, jnp.exp(jnp.where(incl, gcol - grow, 0.0)), 0.0)
    kb = k * bcol
    vb = v * bcol
    eg = jnp.exp(gcol)
    ek = jnp.exp(glast - gcol)
    kbg = kb * eg
    amat = _dot_nt(kb, k)
    qk = _dot_nt(q, k)
    return dict(gcol=gcol, glast=glast, decay=decay, kb=kb, vb=vb, eg=eg, ek=ek, kbg=kbg, amat=amat, qk=qk,
                intra=qk * decay, qg=q * eg, kdec=k * ek)


def _gdn_fwd(qkvc, gb, gbt):
    tm, u, w, qg, kd, intra, egl = _gdn_local_fwd(qkvc, gb, gbt)
    o_f, o_b, s_f, s_b, vn_f, vn_b = _gdn_scan_fwd(u, w, qg, kd, intra, egl, qkvc.shape[0])
    return o_f, o_b, dict(tm=tm, w=w, qg=qg, kd=kd, intra=intra, egl=egl, s=(s_f, s_b), vn=(vn_f, vn_b))


N_CHAINS = 2 * GDN_HEADS


LOCAL_CHUNKS = 4


def _load_chains(x_ref, g_ref, gt_ref, cc=0):
    hd = GDN_HEADS * GDN_DIM
    rows = slice(cc * CHUNK, (cc + 1) * CHUNK)
    chains = []
    for d in range(2):
        masks = _chunk_masks(d == 0)
        for h in range(GDN_HEADS):
            ch = d * GDN_HEADS + h
            q = x_ref[rows, h * GDN_DIM:(h + 1) * GDN_DIM]
            k = x_ref[rows, hd + h * GDN_DIM:hd + (h + 1) * GDN_DIM]
            v = x_ref[rows, 2 * hd + h * GDN_DIM:2 * hd + (h + 1) * GDN_DIM]
            bcol = g_ref[rows, 8 + ch:9 + ch]
            cm = _chain_common(q, k, v, g_ref[rows, ch:ch + 1], gt_ref[cc, ch:ch + 1, :], bcol, masks)
            chains.append(dict(cm, q=q, k=k, v=v, bcol=bcol, masks=masks, ch=ch, h=h, cc=cc))
    return chains


def _chain_shape(rows, cols, dtype):
    return lambda nc: jax.ShapeDtypeStruct((nc, N_CHAINS, rows, cols), dtype)


def _gdn_local_fwd(qkvc, gb, gbt):
    t = qkvc.shape[0]
    nc = t // CHUNK
    hd = GDN_HEADS * GDN_DIM

    def body(x_ref, g_ref, gt_ref, t_ref, u_ref, w_ref, qg_ref, kd_ref, in_ref, eg_ref):
        chains = [c for cc in range(LOCAL_CHUNKS) for c in _load_chains(x_ref, g_ref, gt_ref, cc)]
        ii, jj = chains[0]["masks"][0:2]
        tms = _tri_inv_many([jnp.where(c["masks"][3], c["amat"] * c["decay"], 0.0) for c in chains], ii, jj)
        uws = [_dot(tm, jnp.concatenate([c["vb"], c["kbg"]], axis=1)) for tm, c in zip(tms, chains)]
        for c, tm, uw in zip(chains, tms, uws):
            cc, ch = c["cc"], c["ch"]
            t_ref[cc, ch] = tm
            u_ref[cc, ch] = uw[:, :GDN_DIM]
            w_ref[cc, ch] = uw[:, GDN_DIM:].astype(BF16)
            qg_ref[cc, ch] = c["qg"].astype(BF16)
            kd_ref[cc, ch] = c["kdec"].astype(BF16)
            in_ref[cc, ch] = c["intra"].astype(BF16)
            eg_ref[cc, ch:ch + 1, :] = jnp.broadcast_to(jnp.exp(c["glast"]), (1, LANES))

    lc = LOCAL_CHUNKS
    blk = lambda rows, cols: pl.BlockSpec((lc, N_CHAINS, rows, cols), lambda n: (n, 0, 0, 0))
    shapes = [_chain_shape(CHUNK, CHUNK, F32), _chain_shape(CHUNK, GDN_DIM, F32), _chain_shape(CHUNK, GDN_DIM, BF16),
              _chain_shape(CHUNK, GDN_DIM, BF16), _chain_shape(CHUNK, GDN_DIM, BF16), _chain_shape(CHUNK, CHUNK, BF16)]
    return tuple(pl.pallas_call(
        body, name="gdn_local_fwd", grid=(nc // lc,),
        in_specs=[pl.BlockSpec((lc * CHUNK, 3 * hd), lambda n: (n, 0)), pl.BlockSpec((lc * CHUNK, LANES), lambda n: (n, 0)),
                  pl.BlockSpec((lc, 16, CHUNK), lambda n: (n, 0, 0))],
        out_specs=[blk(CHUNK, CHUNK), blk(CHUNK, GDN_DIM), blk(CHUNK, GDN_DIM), blk(CHUNK, GDN_DIM),
                   blk(CHUNK, GDN_DIM), blk(CHUNK, CHUNK), pl.BlockSpec((lc, N_CHAINS, LANES), lambda n: (n, 0, 0))],
        out_shape=[s(nc) for s in shapes] + [jax.ShapeDtypeStruct((nc, N_CHAINS, LANES), F32)],
        compiler_params=_params(("arbitrary",), VMEM_LIMIT),
    )(qkvc, gb, gbt))


SCAN_CHUNKS = 8


def _dir_specs(nc, rev):
    nb = nc // SCAN_CHUNKS

    def spec(d, rows, cols, own=False):
        chunk = (lambda n: n) if (d == 0) != rev else (lambda n: nb - 1 - n)
        blk = 0 if own else d
        if rows is None:
            return pl.BlockSpec((SCAN_CHUNKS, GDN_HEADS if own else N_CHAINS, cols), lambda n: (chunk(n), 0, 0))
        return pl.BlockSpec((SCAN_CHUNKS, GDN_HEADS, rows, cols), lambda n: (chunk(n), blk, 0, 0))

    def rows_spec(d, cols):
        chunk = (lambda n: n) if (d == 0) != rev else (lambda n: nb - 1 - n)
        return pl.BlockSpec((SCAN_CHUNKS * CHUNK, cols), lambda n: (chunk(n), 0))

    def order(d):
        return list(range(SCAN_CHUNKS)) if (d == 0) != rev else list(range(SCAN_CHUNKS - 1, -1, -1))
    return spec, rows_spec, order


def _gdn_scan_fwd(u, w, qg, kd, intra, egl, t):
    nc = t // CHUNK
    hd = GDN_HEADS * GDN_DIM

    def body(*refs):
        ins, outs, state = refs[:12], refs[12:18], refs[18]
        @pl.when(pl.program_id(0) == 0)
        def _():
            state[...] = jnp.zeros_like(state)

        chains = [(d, h) for d in range(2) for h in range(GDN_HEADS)]
        states = [state[ch] for ch in range(N_CHAINS)]
        for step in range(SCAN_CHUNKS):
            at = [order(d)[step] for d in range(2)]
            pick = lambda k, d, h: ins[2 * k + d][at[d], h]
            sbs = [s.astype(BF16) for s in states]
            ws = [_dot(pick(1, d, h), sb) for (d, h), sb in zip(chains, sbs)]
            o1 = [_dot(pick(2, d, h), sb) for (d, h), sb in zip(chains, sbs)]
            vns = [(pick(0, d, h) - wsb).astype(BF16) for (d, h), wsb in zip(chains, ws)]
            o2 = [_dot(pick(4, d, h), vn) for (d, h), vn in zip(chains, vns)]
            kv = [_dot_tn(pick(3, d, h), vn) for (d, h), vn in zip(chains, vns)]
            new_states = []
            for ch, (d, h) in enumerate(chains):
                outs[d][at[d] * CHUNK:(at[d] + 1) * CHUNK, h * GDN_DIM:(h + 1) * GDN_DIM] = o1[ch] + o2[ch]
                outs[2 + d][at[d], h] = states[ch]
                outs[4 + d][at[d], h] = vns[ch]
                new_states.append(states[ch] * ins[10 + d][at[d], ch:ch + 1, :] + kv[ch])
            states = new_states
        for ch in range(N_CHAINS):
            state[ch] = states[ch]

    spec, rows_spec, order = _dir_specs(nc, False)
    pair = lambda rows, cols, own=False: [spec(0, rows, cols, own), spec(1, rows, cols, own)]
    s_shape = jax.ShapeDtypeStruct((nc, GDN_HEADS, GDN_DIM, GDN_DIM), F32)
    vn_shape = jax.ShapeDtypeStruct((nc, GDN_HEADS, CHUNK, GDN_DIM), BF16)
    return pl.pallas_call(
        body, name="gdn_scan_fwd", grid=(nc // SCAN_CHUNKS,),
        in_specs=(pair(CHUNK, GDN_DIM) + pair(CHUNK, GDN_DIM) + pair(CHUNK, GDN_DIM) + pair(CHUNK, GDN_DIM)
                  + pair(CHUNK, CHUNK) + pair(None, LANES)),
        out_specs=([rows_spec(0, hd), rows_spec(1, hd)] + pair(GDN_DIM, GDN_DIM, True)
                   + pair(CHUNK, GDN_DIM, True)),
        out_shape=[jax.ShapeDtypeStruct((t, hd), F32), jax.ShapeDtypeStruct((t, hd), F32),
                   s_shape, s_shape, vn_shape, vn_shape],
        scratch_shapes=[pltpu.VMEM((N_CHAINS, GDN_DIM, GDN_DIM), F32)],
        compiler_params=_params(("arbitrary",), VMEM_LIMIT),
    )(u, u, w, w, qg, qg, kd, kd, intra, intra, egl, egl)


def _gdn_bwd(qkvc, gb, gbt, do, saved, exchange=None):
    scan = _gdn_scan_bwd(do, saved, qkvc.shape[0])
    return _gdn_local_bwd(qkvc, gb, gbt, do, saved, scan, exchange)


def _gdn_scan_bwd(do, saved, t):
    nc = t // CHUNK
    hd = GDN_HEADS * GDN_DIM

    def body(*refs):
        ins, outs, dstate = refs[:16], refs[16:26], refs[26]
        @pl.when(pl.program_id(0) == 0)
        def _():
            dstate[...] = jnp.zeros_like(dstate)

        chains = [(d, h) for d in range(2) for h in range(GDN_HEADS)]
        dss = [dstate[ch] for ch in range(N_CHAINS)]
        for step in range(SCAN_CHUNKS):
            at = [order(d)[step] for d in range(2)]
            pick = lambda k, d, h: ins[2 * k + d][at[d], h]
            dsbs = [ds.astype(BF16) for ds in dss]
            ss = [pick(1, d, h) for d, h in chains]
            sbs = [s.astype(BF16) for s in ss]
            dos = [ins[d][at[d] * CHUNK:(at[d] + 1) * CHUNK, h * GDN_DIM:(h + 1) * GDN_DIM].astype(BF16)
                   for d, h in chains]
            dv1 = [_dot_tn(pick(5, d, h), dov) for (d, h), dov in zip(chains, dos)]
            dv2 = [_dot(pick(4, d, h), dsb) for (d, h), dsb in zip(chains, dsbs)]
            ds1 = [_dot_tn(pick(3, d, h), dov) for (d, h), dov in zip(chains, dos)]
            dkds = [_dot_nt(pick(6, d, h), dsb) for (d, h), dsb in zip(chains, dsbs)]
            dqgs = [_dot_nt(dov, sb) for dov, sb in zip(dos, sbs)]
            dvns = [(a + b).astype(BF16) for a, b in zip(dv1, dv2)]
            ds2 = [_dot_tn(pick(2, d, h), dvn) for (d, h), dvn in zip(chains, dvns)]
            dws = [_dot_nt(dvn, sb) for dvn, sb in zip(dvns, sbs)]
            new_dss = []
            for ch, (d, h) in enumerate(chains):
                egl = ins[14 + d][at[d], ch:ch + 1, :]
                outs[d][at[d], h] = dvns[ch]
                outs[2 + d][at[d], h] = (-dws[ch]).astype(BF16)
                outs[4 + d][at[d], h] = dqgs[ch]
                outs[6 + d][at[d], h] = dkds[ch]
                outs[8 + d][at[d], h:h + 1, :] = egl * jnp.sum(jnp.sum(ss[ch] * dss[ch], axis=1, keepdims=True),
                                                               axis=0, keepdims=True)
                new_dss.append(ds1[ch] + egl * dss[ch] - ds2[ch])
            dss = new_dss
        for ch in range(N_CHAINS):
            dstate[ch] = dss[ch]

    spec, rows_spec, order = _dir_specs(nc, True)
    pair = lambda rows, cols, own=False: [spec(0, rows, cols, own), spec(1, rows, cols, own)]
    s_f, s_b = saved["s"]
    vn_f, vn_b = saved["vn"]
    w, qg, kd, intra, egl = saved["w"], saved["qg"], saved["kd"], saved["intra"], saved["egl"]
    own = lambda rows, cols, dtype: jax.ShapeDtypeStruct((nc, GDN_HEADS, rows, cols), dtype)
    row_shape = jax.ShapeDtypeStruct((nc, GDN_HEADS, LANES), F32)
    return pl.pallas_call(
        body, name="gdn_scan_bwd", grid=(nc // SCAN_CHUNKS,),
        in_specs=([rows_spec(0, hd), rows_spec(1, hd)] + pair(GDN_DIM, GDN_DIM, True) + pair(CHUNK, GDN_DIM)
                  + pair(CHUNK, GDN_DIM) + pair(CHUNK, GDN_DIM) + pair(CHUNK, CHUNK) + pair(CHUNK, GDN_DIM, True)
                  + pair(None, LANES)),
        out_specs=(pair(CHUNK, GDN_DIM, True) + pair(CHUNK, GDN_DIM, True) + pair(CHUNK, GDN_DIM, True)
                   + pair(CHUNK, GDN_DIM, True) + pair(None, LANES, True)),
        out_shape=[own(CHUNK, GDN_DIM, BF16)] * 4 + [own(CHUNK, GDN_DIM, F32)] * 4 + [row_shape] * 2,
        scratch_shapes=[pltpu.VMEM((N_CHAINS, GDN_DIM, GDN_DIM), F32)],
        compiler_params=_params(("arbitrary",), VMEM_LIMIT),
    )(do, do, s_f, s_b, w, w, qg, qg, kd, kd, intra, intra, vn_f, vn_b, egl, egl)


def _dot3_nt(a, b):
    ah = a.astype(BF16)
    al = (a - ah.astype(F32)).astype(BF16)
    bh = b.astype(BF16)
    bl = (b - bh.astype(F32)).astype(BF16)
    return _dot_nt(ah, bh) + (_dot_nt(ah, bl) + _dot_nt(al, bh))


def _dot3_tn(a, b):
    ah = a.astype(BF16)
    al = (a - ah.astype(F32)).astype(BF16)
    bh = b.astype(BF16)
    bl = (b - bh.astype(F32)).astype(BF16)
    return _dot_tn(ah, bh) + (_dot_tn(ah, bl) + _dot_tn(al, bh))


def _gdn_local_bwd(qkvc, gb, gbt, do, saved, scan, exchange=None):
    t = qkvc.shape[0]
    nc = t // CHUNK
    hd = GDN_HEADS * GDN_DIM

    def body(*refs):
        x_ref, g_ref, gt_ref, do_ref, t_ref = refs[:5]
        per_dir = refs[5:17]
        dx_ref, dg_ref = refs[17:]
        chains = [c for cc in range(LOCAL_CHUNKS) for c in _load_chains(x_ref, g_ref, gt_ref, cc)]
        lane = lax.broadcasted_iota(jnp.int32, (CHUNK, LANES), 1)
        dgates = [jnp.zeros((CHUNK, LANES), F32) for _ in range(LOCAL_CHUNKS)]
        for c in chains:
            d = c["ch"] // GDN_HEADS
            vn_ref, dvn_ref, dw_ref, dqg_ref, dkd_ref, dgl_ref = per_dir[d::2]
            h, cc = c["h"], c["cc"]
            rows = slice(cc * CHUNK, (cc + 1) * CHUNK)
            c.update(tm=t_ref[cc, c["ch"]], dov=do_ref[rows, h * GDN_DIM:(h + 1) * GDN_DIM], vnew=vn_ref[cc, h],
                     dvnew=dvn_ref[cc, h], dw=dw_ref[cc, h], dqg=dqg_ref[cc, h], dkdec=dkd_ref[cc, h],
                     dglast=dgl_ref[cc, h:h + 1, 0:1])
        dintras = [_dot_nt(c["dov"], c["vnew"]) for c in chains]
        dts = [_dot_nt(c["dvnew"], c["vb"]) + _dot_nt(c["dw"], c["kbg"]) for c in chains]
        dvbs = [_dot_tn(c["tm"], c["dvnew"]) for c in chains]
        dkbgs = [_dot_tn(c["tm"], c["dw"]) for c in chains]
        tdts = [_dot3_nt(dt, c["tm"]) for dt, c in zip(dts, chains)]
        dls = [jnp.where(c["masks"][3], -_dot3_tn(c["tm"], tdt), 0.0) for tdt, c in zip(tdts, chains)]
        das = [dl * c["decay"] for dl, c in zip(dls, chains)]
        dqks = [jnp.where(c["masks"][2], di, 0.0) * c["decay"] for di, c in zip(dintras, chains)]
        dkb1 = [_dot(da, c["k"]) for da, c in zip(das, chains)]
        dk1 = [_dot_tn(da, c["kb"]) for da, c in zip(das, chains)]
        dk2 = [_dot_tn(dqk, c["q"]) for dqk, c in zip(dqks, chains)]
        dq1 = [_dot(dqk, c["k"]) for dqk, c in zip(dqks, chains)]
        grads, mms, p_gs, p_betas, p_kds = [], [], [], [], []
        for n, c in enumerate(chains):
            incl = c["masks"][2]
            dkb = dkb1[n] + dkbgs[n] * c["eg"]
            kd = c["dkdec"] * c["kdec"]
            mms.append((dls[n] * c["amat"] + jnp.where(incl, dintras[n], 0.0) * c["qk"]) * c["decay"])
            p_gs.append(c["dqg"] * c["qg"] - kd + dkbgs[n] * c["kbg"])
            p_betas.append(dkb * c["k"] + dvbs[n] * c["v"])
            p_kds.append(kd)
            grads.append((dq1[n] + c["dqg"] * c["eg"],
                          dk1[n] + dk2[n] + c["dkdec"] * c["ek"] + dkb * c["bcol"],
                          dvbs[n] * c["bcol"]))
        row_sums = [jnp.sum(mm, axis=1, keepdims=True) for mm in mms]
        col_sums = [jnp.sum(mm, axis=0, keepdims=True) for mm in mms]
        g_sums = [jnp.sum(pg, axis=1, keepdims=True) for pg in p_gs]
        dbetas = [jnp.sum(pb, axis=1, keepdims=True) for pb in p_betas]
        kd_tots = [jnp.sum(jnp.sum(pk, axis=1, keepdims=True), axis=0, keepdims=True) for pk in p_kds]
        dgcs = [rs - _row_to_col(cs, *c["masks"][0:2]) + gs for rs, cs, gs, c in zip(row_sums, col_sums, g_sums, chains)]
        dgrs = [_col_to_row(dgc, *c["masks"][0:2]) for dgc, c in zip(dgcs, chains)]
        draws = [jnp.sum(jnp.where(jnp.logical_not(c["masks"][3]), dgr, 0.0), axis=1, keepdims=True) + c["dglast"] + kt
                 for dgr, kt, c in zip(dgrs, kd_tots, chains)]
        for c, draw, dbeta in zip(chains, draws, dbetas):
            ch = c["ch"]
            dgates[c["cc"]] = dgates[c["cc"]] + jnp.where(lane == ch, draw, 0.0) + jnp.where(lane == 8 + ch, dbeta, 0.0)
        for cc in range(LOCAL_CHUNKS):
            rows = slice(cc * CHUNK, (cc + 1) * CHUNK)
            for h in range(GDN_HEADS):
                for part in range(3):
                    cols = slice(part * hd + h * GDN_DIM, part * hd + (h + 1) * GDN_DIM)
                    dx_ref[rows, cols] = grads[cc * N_CHAINS + h][part] + grads[cc * N_CHAINS + GDN_HEADS + h][part]
            dg_ref[rows, :] = dgates[cc]

    lc = LOCAL_CHUNKS
    all8 = lambda rows, cols: pl.BlockSpec((lc, N_CHAINS, rows, cols), lambda n: (n, 0, 0, 0))
    own4 = lambda rows, cols: pl.BlockSpec((lc, GDN_HEADS, rows, cols), lambda n: (n, 0, 0, 0))
    row4 = pl.BlockSpec((lc, GDN_HEADS, LANES), lambda n: (n, 0, 0))
    vn_f, vn_b = saved["vn"]
    dvn_f, dvn_b, dw_f, dw_b, dqg_f, dqg_b, dkd_f, dkd_b, dgl_f, dgl_b = scan
    return _grid_call(
        body, "gdn_local_bwd", nc // lc,
        [pl.BlockSpec((lc * CHUNK, 3 * hd), lambda n: (n, 0)), pl.BlockSpec((lc * CHUNK, LANES), lambda n: (n, 0)),
         pl.BlockSpec((lc, 16, CHUNK), lambda n: (n, 0, 0)), pl.BlockSpec((lc * CHUNK, hd), lambda n: (n, 0)),
         all8(CHUNK, CHUNK)] + [own4(CHUNK, GDN_DIM)] * 10 + [row4, row4],
        [pl.BlockSpec((lc * CHUNK, 3 * hd), lambda n: (n, 0)), pl.BlockSpec((lc * CHUNK, LANES), lambda n: (n, 0))],
        [jax.ShapeDtypeStruct((t, 3 * hd), F32), jax.ShapeDtypeStruct((t, LANES), F32)],
        (qkvc, gb, gbt, do, saved["tm"], vn_f, vn_b, dvn_f, dvn_b, dw_f, dw_b, dqg_f, dqg_b, dkd_f, dkd_b, dgl_f, dgl_b),
        exchange=exchange)


def _gdn_post_fwd(of, ob, z, gw, tm):
    t, hd = of.shape

    def body(of_ref, ob_ref, z_ref, w_ref, o_ref):
        for h in range(GDN_HEADS):
            cols = slice(h * GDN_DIM, (h + 1) * GDN_DIM)
            o = of_ref[:, cols] + ob_ref[:, cols]
            zv = z_ref[:, cols]
            o_ref[:, cols] = (o * _rstd(o) * w_ref[...] * (zv * _sigmoid(zv))).astype(BF16)

    row = pl.BlockSpec((tm, hd), lambda i: (i, 0))
    return pl.pallas_call(
        body, name="gdn_post_fwd", grid=(t // tm,),
        in_specs=[row, row, row, _resident((1, GDN_DIM))],
        out_specs=row, out_shape=jax.ShapeDtypeStruct((t, hd), BF16),
        compiler_params=_params(("arbitrary",), VMEM_LIMIT),
    )(of, ob, z, gw)


def _gdn_post_bwd(doa, of, ob, z, gw, tm):
    t, hd = of.shape

    def body(d_ref, of_ref, ob_ref, z_ref, w_ref, do_ref, dz_ref, dw_ref):
        @pl.when(pl.program_id(0) == 0)
        def _():
            dw_ref[...] = jnp.zeros_like(dw_ref)

        dw = jnp.zeros((1, GDN_DIM), F32)
        for h in range(GDN_HEADS):
            cols = slice(h * GDN_DIM, (h + 1) * GDN_DIM)
            o = of_ref[:, cols] + ob_ref[:, cols]
            zv = z_ref[:, cols]
            dv = d_ref[:, cols]
            r = _rstd(o)
            sg = _sigmoid(zv)
            on = o * r * w_ref[...]
            dz_ref[:, cols] = (dv * on * (sg * (1.0 + zv * (1.0 - sg)))).astype(BF16)
            dxr, dwh = _rms_bwd(o, r, w_ref[...], dv * (zv * sg))
            do_ref[:, cols] = dxr
            dw = dw + dwh
        dw_ref[...] += dw

    row = pl.BlockSpec((tm, hd), lambda i: (i, 0))
    return pl.pallas_call(
        body, name="gdn_post_bwd", grid=(t // tm,),
        in_specs=[row, row, row, row, _resident((1, GDN_DIM))],
        out_specs=[row, row, pl.BlockSpec((1, GDN_DIM), lambda i: (0, 0))],
        out_shape=[jax.ShapeDtypeStruct((t, hd), F32), jax.ShapeDtypeStruct((t, hd), BF16),
                   jax.ShapeDtypeStruct((1, GDN_DIM), F32)],
        compiler_params=_params(("arbitrary",), VMEM_LIMIT),
    )(doa, of, ob, z, gw)


SWA_W = SWA_HEADS * SWA_DIM
QBLK = 128
KWIN = QBLK + 2 * RADIUS
WIN_OFFSETS = (0, RADIUS, 2 * RADIUS)


def _t5_bucket(rel):
    nb = REL_BUCKETS // 2
    bucket = (rel > 0).astype(np.int32) * nb
    n = np.abs(rel)
    max_exact = nb // 2
    large = max_exact + (np.log(np.maximum(n, 1) / max_exact)
                         / math.log(REL_MAX_DISTANCE / max_exact) * (nb - max_exact)).astype(np.int32)
    large = np.minimum(large, nb - 1)
    return (bucket + np.where(n < max_exact, n, large)).astype(np.int32)


def _band_tables(dilation):
    a = np.arange(QBLK)
    b = np.arange(KWIN)
    rel = np.stack([b[None, :] - w0 - a[:, None] for w0 in WIN_OFFSETS])
    return np.where(np.abs(rel) <= RADIUS, _t5_bucket(rel * dilation), -1).astype(np.int32)


BAND_CELLS = len(WIN_OFFSETS) * QBLK * KWIN


def _band_index():
    return jnp.asarray(np.concatenate([_band_tables(d).reshape(-1) for _, d in PATTERNS])[None, :])


def _onehot(idx, dtype):
    return (lax.broadcasted_iota(jnp.int32, (REL_BUCKETS, idx.shape[1]), 0) == idx).astype(dtype)


def _bias_tables(rel_bias, idx, tk):
    n = idx.shape[1]

    def body(rb_ref, i_ref, o_ref):
        iv = i_ref[...]
        o_ref[...] = jnp.where(iv < 0, NEG_BIG, _dot_hi(rb_ref[...], _onehot(iv, F32)))

    return pl.pallas_call(
        body, name="bias_tables", grid=(n // tk,),
        in_specs=[_resident((SWA_HEADS, REL_BUCKETS)), pl.BlockSpec((1, tk), lambda k: (0, k))],
        out_specs=pl.BlockSpec((SWA_HEADS, tk), lambda k: (0, k)),
        out_shape=jax.ShapeDtypeStruct((SWA_HEADS, n), F32),
        compiler_params=_params(("arbitrary",), VMEM_LIMIT),
    )(rel_bias.T, idx)


def _head_mean(x2, bd_ref):
    bd = bd_ref[...]
    rest, acc = x2, None
    for _ in range(3):
        piece = rest.astype(BF16)
        part = jnp.dot(piece, bd, preferred_element_type=F32)
        acc = part if acc is None else acc + part
        rest = rest - piece.astype(F32)
    return acc


VIEW_DILATIONS = tuple(d for _, d in PATTERNS if d > 1)


def _view_spec(tm, d):
    return pl.BlockSpec((tm // d, d * SWA_W), lambda i: (i, 0))


def _view_shape(t, d, dtype):
    return jax.ShapeDtypeStruct((t // d, d * SWA_W), dtype)


N_GROUPS = SWA_W // LANES


def _to_view(src_ref, idx, dst_ref, d, rows):
    for r in range(d):
        for g in range(N_GROUPS):
            cols = slice(r * SWA_W + g * LANES, r * SWA_W + (g + 1) * LANES)
            dst_ref[:, cols] = src_ref[idx, g, pl.ds(r, rows // d, stride=d), :].astype(dst_ref.dtype)


def _from_view(src_ref, dst_ref, idx, d, rows):
    for r in range(d):
        for g in range(N_GROUPS):
            cols = slice(r * SWA_W + g * LANES, r * SWA_W + (g + 1) * LANES)
            dst_ref[idx, g, pl.ds(r, rows // d, stride=d), :] = src_ref[:, cols]


def _swa_prep_fwd(qkvb, qw, kw, bd, tm):
    t = qkvb.shape[0]

    def body(x_ref, qw_ref, kw_ref, bd_ref, *rest):
        outs, sc = rest[:-1], rest[-1]
        for gidx in range(N_GROUPS):
            cols = slice(gidx * LANES, (gidx + 1) * LANES)
            xq = x_ref[:, cols]
            sc[0, gidx] = xq * lax.rsqrt(_head_mean(xq * xq, bd_ref) + EPS) * qw_ref[:, cols] * (SWA_DIM ** -0.5)
            xk = x_ref[:, SWA_W + gidx * LANES:SWA_W + (gidx + 1) * LANES]
            sc[1, gidx] = xk * lax.rsqrt(_head_mean(xk * xk, bd_ref) + EPS) * kw_ref[:, cols]
            sc[2, gidx] = x_ref[:, 2 * SWA_W + gidx * LANES:2 * SWA_W + (gidx + 1) * LANES]
            for i in range(3):
                outs[i][:, cols] = sc[i, gidx].astype(BF16)
        for i in range(3):
            for n, d in enumerate(VIEW_DILATIONS):
                _to_view(sc, i, outs[3 * (n + 1) + i], d, tm)

    return pl.pallas_call(
        body, name="swa_prep_fwd", grid=(t // tm,),
        in_specs=[pl.BlockSpec((tm, 3 * SWA_W), lambda i: (i, 0)), _resident((1, SWA_W)), _resident((1, SWA_W)),
                  _resident((LANES, LANES))],
        out_specs=[_view_spec(tm, d) for d in (1,) + VIEW_DILATIONS for _ in range(3)],
        out_shape=[_view_shape(t, d, BF16) for d in (1,) + VIEW_DILATIONS for _ in range(3)],
        scratch_shapes=[pltpu.VMEM((3, N_GROUPS, tm, LANES), F32)],
        compiler_params=_params(("arbitrary",), VMEM_LIMIT),
    )(qkvb, qw, kw, bd)


def _swa_prep_bwd(qkvb, qw, kw, bd, grads, tm):
    t = qkvb.shape[0]

    def body(x_ref, qw_ref, kw_ref, bd_ref, *rest):
        parts, (dx_ref, dqw_ref, dkw_ref, sc) = rest[:9], rest[9:]
        @pl.when(pl.program_id(0) == 0)
        def _():
            dqw_ref[...] = jnp.zeros_like(dqw_ref)
            dkw_ref[...] = jnp.zeros_like(dkw_ref)

        for i in range(3):
            for n, d in enumerate(VIEW_DILATIONS):
                _from_view(parts[3 * (n + 1) + i], sc, 2 * i + n, d, tm)
        for gidx in range(N_GROUPS):
            cols = slice(gidx * LANES, (gidx + 1) * LANES)
            for i, base, w_ref, dw_ref, scale in ((0, 0, qw_ref, dqw_ref, SWA_DIM ** -0.5),
                                                  (1, SWA_W, kw_ref, dkw_ref, 1.0)):
                xv = x_ref[:, base + gidx * LANES:base + (gidx + 1) * LANES]
                dy = (parts[i][:, cols] + sc[2 * i, gidx] + sc[2 * i + 1, gidx]) * scale
                r = lax.rsqrt(_head_mean(xv * xv, bd_ref) + EPS)
                xhat = xv * r
                dxh = dy * w_ref[:, cols]
                dx = r * (dxh - xhat * _head_mean(dxh * xhat, bd_ref))
                dx_ref[:, base + gidx * LANES:base + (gidx + 1) * LANES] = dx.astype(BF16)
                dw_ref[:, cols] += jnp.sum(dy * xhat, axis=0, keepdims=True)
            dx_ref[:, 2 * SWA_W + gidx * LANES:2 * SWA_W + (gidx + 1) * LANES] = (
                parts[2][:, cols] + sc[4, gidx] + sc[5, gidx]).astype(BF16)

    wrow = pl.BlockSpec((1, SWA_W), lambda i: (0, 0))
    return pl.pallas_call(
        body, name="swa_prep_bwd", grid=(t // tm,),
        in_specs=[pl.BlockSpec((tm, 3 * SWA_W), lambda i: (i, 0)), _resident((1, SWA_W)), _resident((1, SWA_W)),
                  _resident((LANES, LANES))] + [_view_spec(tm, d) for d in (1,) + VIEW_DILATIONS for _ in range(3)],
        out_specs=[pl.BlockSpec((tm, 3 * SWA_W), lambda i: (i, 0)), wrow, wrow],
        out_shape=[jax.ShapeDtypeStruct((t, 3 * SWA_W), BF16), jax.ShapeDtypeStruct((1, SWA_W), F32),
                   jax.ShapeDtypeStruct((1, SWA_W), F32)],
        scratch_shapes=[pltpu.VMEM((6, N_GROUPS, tm, LANES), F32)],
        compiler_params=_params(("arbitrary",), VMEM_LIMIT),
    )(qkvb, qw, kw, bd, *grads)


def _aligned(v, m):
    return v if isinstance(v, int) else pl.multiple_of(v, m)


BAND_GROUP = 2


def _band_loop(nsub, length, step, group=BAND_GROUP):
    step([(0, 0)], 0)
    if nsub > 2:
        assert (nsub - 2) % group == 0

        def inner(i, carry):
            s0 = 1 + i * group
            step([(s0 + e, pl.multiple_of((s0 + e) * QBLK - RADIUS, RADIUS)) for e in range(group)], 1)
            return carry
        lax.fori_loop(0, (nsub - 2) // group, inner, 0)
    step([(nsub - 1, length - KWIN)], 2)


def _head_select(lane, a0, a1):
    return jnp.where(lane < SWA_DIM, a0, a1)


def _swa_fwd(qv, kv, vv, bias, dilation, name):
    length = qv.shape[0]
    nsub = length // QBLK
    assert nsub >= 2 and length % QBLK == 0

    def body(q_ref, k_ref, v_ref, b_ref, o_ref, l_ref):
        lane = lax.broadcasted_iota(jnp.int32, (QBLK, LANES), 1)

        def step(blocks, var):
            items = []
            for s, ws in blocks:
                rows = pl.ds(_aligned(s * QBLK, QBLK), QBLK)
                q, kk, vw = q_ref[rows, :], k_ref[pl.ds(ws, KWIN), :], v_ref[pl.ds(ws, KWIN), :]
                for hh in range(2):
                    items.append((hh, jnp.where((lane < SWA_DIM) == (hh == 0), q, jnp.zeros_like(q)), kk, vw))
            lgs = [_dot_nt(qh, kk) + b_ref[hh, var] for hh, qh, kk, _ in items]
            ms = [jnp.max(lg, axis=-1, keepdims=True) for lg in lgs]
            ps = [jnp.exp(lg - m) for lg, m in zip(lgs, ms)]
            dens = [jnp.sum(p, axis=-1, keepdims=True) for p in ps]
            pvs = [_dot(p, it[3]) for p, it in zip(ps, items)]
            for n, (s, _) in enumerate(blocks):
                rows = pl.ds(_aligned(s * QBLK, QBLK), QBLK)
                o0, o1 = (pvs[2 * n + hh] / dens[2 * n + hh] for hh in range(2))
                l0, l1 = (ms[2 * n + hh] + jnp.log(dens[2 * n + hh]) for hh in range(2))
                o_ref[rows, :] = _head_select(lane, o0, o1)
                l_ref[rows, :] = _head_select(lane, l0, l1)

        _band_loop(nsub, length, step)

    blk = pl.BlockSpec((length, LANES), lambda hp, r: (0, r * (SWA_W // LANES) + hp))
    shp = jax.ShapeDtypeStruct(qv.shape, F32)
    return pl.pallas_call(
        body, name=name, grid=(SWA_W // LANES, dilation),
        in_specs=[blk, blk, blk, pl.BlockSpec((2, 3, QBLK, KWIN), lambda hp, r: (hp, 0, 0, 0))],
        out_specs=[blk, blk], out_shape=[shp, shp],
        compiler_params=_params(("arbitrary", "arbitrary"), VMEM_LIMIT),
    )(qv, kv, vv, bias)


def _swa_combine(os_, ls_, tm):
    t = os_[0].shape[0]

    def body(o0, o1, o2, l0, l1, l2, o_ref, ob_ref, la_ref, lb_ref, lc_ref, sc):
        for n, d in enumerate(VIEW_DILATIONS):
            _from_view((o1, o2)[n], sc, n, d, tm)
            _from_view((l1, l2)[n], sc, 2 + n, d, tm)
        for g in range(N_GROUPS):
            cols = slice(g * LANES, (g + 1) * LANES)
            la, lb, lc = l0[:, cols], sc[2, g], sc[3, g]
            m = jnp.maximum(jnp.maximum(la, lb), lc)
            tot = m + jnp.log(jnp.exp(la - m) + jnp.exp(lb - m) + jnp.exp(lc - m))
            o = jnp.exp(la - tot) * o0[:, cols] + jnp.exp(lb - tot) * sc[0, g] + jnp.exp(lc - tot) * sc[1, g]
            o_ref[:, cols] = o
            ob_ref[:, cols] = o.astype(BF16)
            la_ref[:, cols] = tot
            sc[4, g] = tot
        for n, d in enumerate(VIEW_DILATIONS):
            _to_view(sc, 4, (lb_ref, lc_ref)[n], d, tm)

    specs = [_view_spec(tm, d) for d in (1,) + VIEW_DILATIONS]
    return pl.pallas_call(
        body, name="swa_combine", grid=(t // tm,), in_specs=specs + specs, out_specs=[specs[0], specs[0]] + specs,
        out_shape=[jax.ShapeDtypeStruct((t, SWA_W), F32), jax.ShapeDtypeStruct((t, SWA_W), BF16)]
                  + [_view_shape(t, d, F32) for d in (1,) + VIEW_DILATIONS],
        scratch_shapes=[pltpu.VMEM((5, N_GROUPS, tm, LANES), F32)],
        compiler_params=_params(("arbitrary",), VMEM_LIMIT),
    )(*os_, *ls_)


def _swa_bwd_prep(do, o, bd, tm):
    t = do.shape[0]

    def body(d_ref, o_ref, bd_ref, dd1, dd4, dd16, db1, db4, db16, sc):
        for gidx in range(N_GROUPS):
            cols = slice(gidx * LANES, (gidx + 1) * LANES)
            dv = d_ref[:, cols]
            dd = _head_mean(dv * o_ref[:, cols], bd_ref) * float(SWA_DIM)
            sc[0, gidx] = dd
            sc[1, gidx] = dv
            dd1[:, cols] = dd
            db1[:, cols] = dv.astype(BF16)
        for n, d in enumerate(VIEW_DILATIONS):
            _to_view(sc, 0, (dd4, dd16)[n], d, tm)
            _to_view(sc, 1, (db4, db16)[n], d, tm)

    specs = [_view_spec(tm, d) for d in (1,) + VIEW_DILATIONS]
    return pl.pallas_call(
        body, name="swa_bwd_prep", grid=(t // tm,), in_specs=[specs[0], specs[0], _resident((LANES, LANES))],
        out_specs=specs + specs,
        out_shape=[_view_shape(t, d, F32) for d in (1,) + VIEW_DILATIONS]
                  + [_view_shape(t, d, BF16) for d in (1,) + VIEW_DILATIONS],
        scratch_shapes=[pltpu.VMEM((2, N_GROUPS, tm, LANES), F32)],
        compiler_params=_params(("arbitrary",), VMEM_LIMIT),
    )(do, o, bd)


def _swa_bwd(qv, kv, vv, dov, lv, ddv, bias_a, dilation, name):
    length = qv.shape[0]
    nsub = length // QBLK
    single = pl.Buffered(1) if dilation == 1 else None

    def body(q_ref, k_ref, v_ref, do_ref, l_ref, dd_ref, ba_ref, dq_ref, dk_ref, dv_ref, db_ref):
        @pl.when(pl.program_id(1) == 0)
        def _():
            db_ref[...] = jnp.zeros_like(db_ref)

        lane = lax.broadcasted_iota(jnp.int32, (QBLK, LANES), 1)
        lanew = lax.broadcasted_iota(jnp.int32, (KWIN, LANES), 1)

        def step(blocks, var):
            items = []
            for s, ws in blocks:
                rows = pl.ds(_aligned(s * QBLK, QBLK), QBLK)
                win = pl.ds(ws, KWIN)
                q, dov_ = q_ref[rows, :], do_ref[rows, :]
                kk, vw = k_ref[win, :], v_ref[win, :]
                lse, dd = l_ref[rows, :], dd_ref[rows, :]
                for hh in range(2):
                    mine = (lane < SWA_DIM) == (hh == 0)
                    col = slice(hh * SWA_DIM, hh * SWA_DIM + 1)
                    items.append((hh, jnp.where(mine, q, jnp.zeros_like(q)), jnp.where(mine, dov_, jnp.zeros_like(dov_)),
                                  kk, vw, lse[:, col], dd[:, col], q, dov_))
            lgs = [_dot_nt(it[1], it[3]) + ba_ref[it[0], var] for it in items]
            dps = [_dot_nt(it[2], it[4]) for it in items]
            ps = [jnp.exp(lg - it[5]) for lg, it in zip(lgs, items)]
            dss = [p * (dp - it[6]) for p, dp, it in zip(ps, dps, items)]
            dqs = [_dot(ds, it[3]) for ds, it in zip(dss, items)]
            dks = [_dot_tn(ds, it[7]) for ds, it in zip(dss, items)]
            dvs = [_dot_tn(p, it[8]) for p, it in zip(ps, items)]
            for n, (s, ws) in enumerate(blocks):
                rows = pl.ds(_aligned(s * QBLK, QBLK), QBLK)
                win = pl.ds(ws, KWIN)
                dq_ref[rows, :] = _head_select(lane, dqs[2 * n], dqs[2 * n + 1])
                dk_ref[win, :] += _head_select(lanew, dks[2 * n], dks[2 * n + 1])
                dv_ref[win, :] += _head_select(lanew, dvs[2 * n], dvs[2 * n + 1])
            for hh in range(2):
                tot = dss[hh]
                for n in range(1, len(blocks)):
                    tot = tot + dss[2 * n + hh]
                db_ref[hh, var] += tot

        dk_ref[...] = jnp.zeros_like(dk_ref)
        dv_ref[...] = jnp.zeros_like(dv_ref)
        _band_loop(nsub, length, step)

    imap = lambda hp, r: (0, r * (SWA_W // LANES) + hp)
    blk_in = pl.BlockSpec((length, LANES), imap, pipeline_mode=single)
    blk_out = pl.BlockSpec((length, LANES), imap)
    shp = jax.ShapeDtypeStruct(qv.shape, F32)
    return pl.pallas_call(
        body, name=name, grid=(SWA_W // LANES, dilation),
        in_specs=[blk_in] * 6 + [pl.BlockSpec((2, 3, QBLK, KWIN), lambda hp, r: (hp, 0, 0, 0))],
        out_specs=[blk_out, blk_out, blk_out, pl.BlockSpec((2, 3, QBLK, KWIN), lambda hp, r: (hp, 0, 0, 0))],
        out_shape=[shp, shp, shp, jax.ShapeDtypeStruct((SWA_HEADS, 3, QBLK, KWIN), F32)],
        compiler_params=_params(("arbitrary", "arbitrary"), VMEM_LIMIT),
    )(qv, kv, vv, dov, lv, ddv, bias_a)


def _bias_grad(ds2, idx, tk):
    n = ds2.shape[1]
    nk = n // tk

    def body(a_ref, i_ref, o_ref):
        @pl.when(pl.program_id(0) == 0)
        def _():
            o_ref[...] = jnp.zeros_like(o_ref)

        oh = _onehot(i_ref[...], BF16)
        rest = a_ref[...]
        acc = jnp.zeros((SWA_HEADS, REL_BUCKETS), F32)
        for _ in range(3):
            piece = rest.astype(BF16)
            acc = acc + _dot_nt(piece, oh)
            rest = rest - piece.astype(F32)
        o_ref[...] += acc

    return pl.pallas_call(
        body, name="bias_grad", grid=(nk,),
        in_specs=[pl.BlockSpec((SWA_HEADS, tk), lambda k: (0, k)), pl.BlockSpec((1, tk), lambda k: (0, k))],
        out_specs=pl.BlockSpec((SWA_HEADS, REL_BUCKETS), lambda k: (0, 0)),
        out_shape=jax.ShapeDtypeStruct((SWA_HEADS, REL_BUCKETS), F32),
        compiler_params=_params(("arbitrary",), VMEM_LIMIT),
    )(ds2, idx)


def _swa_branch_fwd(qkvb, qw_t, kw_t, rel_bias, bd, tm):
    qkv = _swa_prep_fwd(qkvb, qw_t, kw_t, bd, tm)
    tables = _bias_tables(rel_bias, _band_index(), 8192)
    os_, ls_, tabs = [], [], []
    for n, (_, d) in enumerate(PATTERNS):
        bias = tables[:, n * BAND_CELLS:(n + 1) * BAND_CELLS].reshape(SWA_HEADS, len(WIN_OFFSETS), QBLK, KWIN)
        o_p, l_p = _swa_fwd(*qkv[3 * n:3 * n + 3], bias, d, f"swa_fwd_d{d}")
        os_.append(o_p)
        ls_.append(l_p)
        tabs.append(bias)
    o, o16, *lses = _swa_combine(os_, ls_, tm)
    return o, o16, (qkv, lses, tabs)


def _swa_branch_bwd(do, o, saved, qkvb, qw_t, kw_t, bd, tm):
    qkv, lses, tabs = saved
    prep = _swa_bwd_prep(do, o, bd, tm)
    grads, dss = [], []
    for n, ((_, d), bias) in enumerate(zip(PATTERNS, tabs)):
        dq, dk, dv, ds = _swa_bwd(*qkv[3 * n:3 * n + 3], prep[3 + n], lses[n], prep[n], bias, d, f"swa_bwd_d{d}")
        grads += [dq, dk, dv]
        dss.append(ds.reshape(SWA_HEADS, -1))
    dqkvb, dqw, dkw = _swa_prep_bwd(qkvb, qw_t, kw_t, bd, grads, tm)
    dbias = _bias_grad(jnp.concatenate(dss, axis=1), _band_index(), 8192)
    fold = lambda w: jnp.sum(w.reshape(SWA_HEADS, SWA_DIM), axis=0)
    return dqkvb, fold(dqw), fold(dkw), dbias.T


def _mesh_pos():
    return lax.axis_index("x"), lax.axis_index("y"), lax.axis_index("c")


def _other_chips(x, y):
    return [(1 - x, y), (x, 1 - y), (1 - x, 1 - y)]


def _remote(src, dst, send_sem, recv_sem, device):
    return pltpu.make_async_remote_copy(src_ref=src, dst_ref=dst, send_sem=send_sem, recv_sem=recv_sem,
                                        device_id=device, device_id_type=MESH)


def _split_axis(shape2):
    return 0 if (shape2[0] // 2) % 16 == 0 else 1


def _half_index(shape2, c):
    axis = _split_axis(shape2)
    h = shape2[axis] // 2
    return (pl.ds(c * h, h), slice(None)) if axis == 0 else (slice(None), pl.ds(c * h, h))


def _all_gather(xs):
    n = len(xs)

    def body(*refs):
        ins, outs = refs[:n], refs[n:2 * n]
        send_sems, recv_sems = refs[2 * n:]
        x, y, c = _mesh_pos()
        me = 2 * x + y
        chips = _other_chips(x, y)
        halves = []
        sends = []
        for a in range(n):
            h = ins[a].shape[0] // 2
            mine, other = pl.ds(c * h, h), pl.ds((1 - c) * h, h)
            halves.append((mine, other))
            own = _remote(ins[a], outs[a].at[me], send_sems.at[a, 6], recv_sems.at[a, 6], (x, y, 1 - c))
            own.start()
            sends.append(own)
            for j, chip in enumerate(chips):
                cp = _remote(ins[a].at[mine], outs[a].at[me, mine], send_sems.at[a, j], recv_sems.at[a, j], (*chip, c))
                cp.start()
                sends.append(cp)
        for a in range(n):
            mine, _ = halves[a]
            for j, chip in enumerate(chips):
                src = 2 * chip[0] + chip[1]
                landed = outs[a].at[src, mine]
                _remote(landed, landed, send_sems.at[a, j], recv_sems.at[a, j], (x, y, c)).wait_recv()
                fwd = _remote(landed, landed, send_sems.at[a, 3 + j], recv_sems.at[a, 3 + j], (x, y, 1 - c))
                fwd.start()
                sends.append(fwd)
        for a in range(n):
            _, other = halves[a]
            for j, chip in enumerate(chips):
                src = 2 * chip[0] + chip[1]
                landed = outs[a].at[src, other]
                _remote(landed, landed, send_sems.at[a, 3 + j], recv_sems.at[a, 3 + j], (x, y, c)).wait_recv()
            mine_slot = outs[a].at[me]
            _remote(mine_slot, mine_slot, send_sems.at[a, 6], recv_sems.at[a, 6], (x, y, c)).wait_recv()
        for cp in sends:
            cp.wait_send()

    return list(pl.pallas_call(
        body, name="all_gather_weights",
        in_specs=[ANY] * n, out_specs=[ANY] * n,
        out_shape=[jax.ShapeDtypeStruct((N_SHARDS,) + a.shape, a.dtype) for a in xs],
        scratch_shapes=[pltpu.SemaphoreType.DMA((n, 7)), pltpu.SemaphoreType.DMA((n, 7))],
    )(*xs))


def _rs_pair(gs):
    n = len(gs)

    def body(*refs):
        ins, lands = refs[:n], refs[n:2 * n]
        send_sems, recv_sems = refs[2 * n:]
        x, y, c = _mesh_pos()
        cps = []
        for a in range(n):
            theirs = (slice(None),) + _half_index(ins[a].shape[1:], 1 - c)
            cp = _remote(ins[a].at[theirs], lands[a], send_sems.at[a], recv_sems.at[a], (x, y, 1 - c))
            cp.start()
            cps.append(cp)
        for cp in cps:
            cp.wait()

    def half_shape(g):
        dims = list(g.shape)
        dims[1 + _split_axis(g.shape[1:])] //= 2
        return tuple(dims)

    return list(pl.pallas_call(
        body, name="rs_pair", in_specs=[ANY] * n, out_specs=[ANY] * n,
        out_shape=[jax.ShapeDtypeStruct(half_shape(g), g.dtype) for g in gs],
        scratch_shapes=[pltpu.SemaphoreType.DMA((n,)), pltpu.SemaphoreType.DMA((n,))],
    )(*gs))


def _rs_chips(ss):
    n = len(ss)

    def body(*refs):
        ins, outs = refs[:n], refs[n:2 * n]
        send_sems, recv_sems = refs[2 * n:]
        x, y, c = _mesh_pos()
        me = 2 * x + y
        chips = _other_chips(x, y)
        cps = []
        for a in range(n):
            for j, chip in enumerate(chips):
                dst_chip = 2 * chip[0] + chip[1]
                cp = _remote(ins[a].at[dst_chip], outs[a].at[me], send_sems.at[a, j], recv_sems.at[a, j], (*chip, c))
                cp.start()
                cps.append(cp)
        for a in range(n):
            for j, chip in enumerate(chips):
                src = 2 * chip[0] + chip[1]
                _remote(outs[a].at[src], outs[a].at[src], send_sems.at[a, j], recv_sems.at[a, j], (x, y, c)).wait_recv()
        for cp in cps:
            cp.wait_send()

    return list(pl.pallas_call(
        body, name="rs_chips", in_specs=[ANY] * n, out_specs=[ANY] * n,
        out_shape=[jax.ShapeDtypeStruct(s.shape, s.dtype) for s in ss],
        scratch_shapes=[pltpu.SemaphoreType.DMA((n, 3)), pltpu.SemaphoreType.DMA((n, 3))],
    )(*ss))


def _rs_join(fs, axes):
    n = len(fs)

    def whole(f, axis):
        dims = list(f.shape)
        dims[axis] *= 2
        return tuple(dims)

    def body(*refs):
        ins, outs = refs[:n], refs[n:2 * n]
        send_sems, recv_sems = refs[2 * n:]
        x, y, c = _mesh_pos()
        cps = []
        for a in range(n):
            h = ins[a].shape[axes[a]]
            mine = (pl.ds(c * h, h), slice(None)) if axes[a] == 0 else (slice(None), pl.ds(c * h, h))
            cp = _remote(ins[a], outs[a].at[mine], send_sems.at[a], recv_sems.at[a], (x, y, 1 - c))
            cp.start()
            cps.append(cp)
        for cp in cps:
            cp.wait()

    outs = pl.pallas_call(
        body, name="rs_join", in_specs=[ANY] * n, out_specs=[ANY] * n,
        out_shape=[jax.ShapeDtypeStruct(whole(f, ax), f.dtype) for f, ax in zip(fs, axes)],
        scratch_shapes=[pltpu.SemaphoreType.DMA((n,)), pltpu.SemaphoreType.DMA((n,))],
    )(*fs)
    c = lax.axis_index("c")
    return [lax.dynamic_update_slice_in_dim(o, f, c * f.shape[ax], ax) for o, f, ax in zip(outs, fs, axes)]


def _gather_exchange(xs):
    def start(cin, cout, send_sems, recv_sems):
        x, y, c = _mesh_pos()
        me = 2 * x + y
        for a, (src, dst) in enumerate(zip(cin, cout)):
            mine = _half_index(src.shape, c)
            for j, chip in enumerate(_other_chips(x, y)):
                _remote(src.at[mine], dst.at[(me,) + mine], send_sems.at[a, j], recv_sems.at[a, j], (*chip, c)).start()

    def finish(cin, cout, send_sems, recv_sems):
        x, y, c = _mesh_pos()
        for a, dst in enumerate(cout):
            for j, chip in enumerate(_other_chips(x, y)):
                landed = dst.at[(2 * chip[0] + chip[1],) + _half_index(dst.shape[1:], c)]
                _remote(landed, landed, send_sems.at[a, j], recv_sems.at[a, j], (x, y, c)).wait()

    return _Exchange(tuple(xs), tuple(jax.ShapeDtypeStruct((N_SHARDS,) + a.shape, a.dtype) for a in xs), start, finish)


def _gather_forward(gs, xs):
    n = len(gs)

    def body(*refs):
        shards, outs = refs[n:2 * n], refs[2 * n:3 * n]
        send_sems, recv_sems = refs[3 * n:]
        x, y, c = _mesh_pos()
        me = 2 * x + y
        chips = _other_chips(x, y)
        cps = []
        for a in range(n):
            own = _remote(shards[a], outs[a].at[me], send_sems.at[a, 3], recv_sems.at[a, 3], (x, y, 1 - c))
            own.start()
            cps.append(own)
            for j, chip in enumerate(chips):
                landed = outs[a].at[(2 * chip[0] + chip[1],) + _half_index(outs[a].shape[1:], c)]
                cp = _remote(landed, landed, send_sems.at[a, j], recv_sems.at[a, j], (x, y, 1 - c))
                cp.start()
                cps.append(cp)
        for a in range(n):
            for j, chip in enumerate(chips):
                other = outs[a].at[(2 * chip[0] + chip[1],) + _half_index(outs[a].shape[1:], 1 - c)]
                _remote(other, other, send_sems.at[a, j], recv_sems.at[a, j], (x, y, c)).wait_recv()
            mine_slot = outs[a].at[me]
            _remote(mine_slot, mine_slot, send_sems.at[a, 3], recv_sems.at[a, 3], (x, y, c)).wait_recv()
        for cp in cps:
            cp.wait_send()

    return list(pl.pallas_call(
        body, name="gather_forward", in_specs=[ANY] * (2 * n), out_specs=[ANY] * n,
        out_shape=[jax.ShapeDtypeStruct(g.shape, g.dtype) for g in gs],
        input_output_aliases={i: i for i in range(n)},
        scratch_shapes=[pltpu.SemaphoreType.DMA((n, 4)), pltpu.SemaphoreType.DMA((n, 4))],
    )(*gs, *xs))


def _scatter_exchange(ss):
    def start(cin, cout, send_sems, recv_sems):
        x, y, c = _mesh_pos()
        me = 2 * x + y
        for a, (src, dst) in enumerate(zip(cin, cout)):
            for j, chip in enumerate(_other_chips(x, y)):
                _remote(src.at[2 * chip[0] + chip[1]], dst.at[me], send_sems.at[a, j], recv_sems.at[a, j],
                        (*chip, c)).start()

    def finish(cin, cout, send_sems, recv_sems):
        x, y, c = _mesh_pos()
        for a, dst in enumerate(cout):
            for j, chip in enumerate(_other_chips(x, y)):
                slot = dst.at[2 * chip[0] + chip[1]]
                _remote(slot, slot, send_sems.at[a, j], recv_sems.at[a, j], (x, y, c)).wait()

    return _Exchange(tuple(ss), tuple(jax.ShapeDtypeStruct(s.shape, s.dtype) for s in ss), start, finish)


def _add_pair(g, land, name):
    nj = g.shape[0]
    shape2 = g.shape[1:]

    def body(g_ref, l_ref, o_ref):
        mine = g_ref[(0,) + _half_index(shape2, lax.axis_index("c"))]
        o_ref[0] = (mine.astype(F32) + l_ref[0].astype(F32)).astype(BF16)

    half = pl.BlockSpec((1,) + land.shape[1:], lambda j: (j, 0, 0))
    return pl.pallas_call(body, name=name, grid=(nj,),
                          in_specs=[pl.BlockSpec((1,) + shape2, lambda j: (j, 0, 0)), half], out_specs=half,
                          out_shape=jax.ShapeDtypeStruct(land.shape, BF16),
                          compiler_params=_params(("arbitrary",), VMEM_LIMIT))(g, land)


def _sum_slots(slots, own, name):
    nj, h, c = slots.shape
    th = h // 2 if h % 32 == 0 else h

    def body(s_ref, o_ref, out_ref):
        me = 2 * lax.axis_index("x") + lax.axis_index("y")
        acc = jnp.zeros((th, c), F32)
        for s in range(nj):
            acc = acc + jnp.where(me == s, o_ref[s], s_ref[s]).astype(F32)
        out_ref[...] = acc

    blk = pl.BlockSpec((nj, th, c), lambda i: (0, i, 0))
    return pl.pallas_call(body, name=name, grid=(h // th,), in_specs=[blk, blk],
                          out_specs=pl.BlockSpec((th, c), lambda i: (i, 0)),
                          out_shape=jax.ShapeDtypeStruct((h, c), F32),
                          compiler_params=_params(("arbitrary",), VMEM_LIMIT))(slots, own)


def _all_reduce_small(p):
    r = p.shape[0]

    def body(p_ref, o_ref, buf, send_sems, recv_sems):
        x, y, c = _mesh_pos()
        me = 4 * x + 2 * y + c
        buf[me] = p_ref[...]
        cps = []
        k = 0
        for fx in range(2):
            for fy in range(2):
                for fc in range(2):
                    if fx + fy + fc == 0:
                        continue
                    peer = (1 - x if fx else x, 1 - y if fy else y, 1 - c if fc else c)
                    peer_id = 4 * peer[0] + 2 * peer[1] + peer[2]
                    cp = _remote(p_ref, buf.at[me], send_sems.at[k], recv_sems.at[k], peer)
                    cp.start()
                    cps.append((cp, peer_id, k))
                    k += 1
        for cp, peer_id, k in cps:
            _remote(p_ref, buf.at[peer_id], send_sems.at[k], recv_sems.at[k], (x, y, c)).wait_recv()
        for cp, _, _ in cps:
            cp.wait_send()
        acc = buf[0]
        for s in range(1, 8):
            acc = acc + buf[s]
        o_ref[...] = acc

    vm = pl.BlockSpec(memory_space=pltpu.VMEM)
    return pl.pallas_call(
        body, name="all_reduce_small", in_specs=[vm], out_specs=vm,
        out_shape=jax.ShapeDtypeStruct(p.shape, F32),
        scratch_shapes=[pltpu.VMEM((8, r, LANES), F32), pltpu.SemaphoreType.DMA((7,)), pltpu.SemaphoreType.DMA((7,))],
    )(p)


def _adamw(w, g, m, v, name):
    r, c = w.shape
    row_tiles = [d for d in range(8, min(r, 256) + 1, 8) if r % d == 0]
    tr, tc = (max(row_tiles), c) if row_tiles else (r, 256 if c % 256 == 0 else c)
    c1 = 1.0 / (1.0 - ADAM_B1 ** ADAM_STEP)
    c2 = 1.0 / (1.0 - ADAM_B2 ** ADAM_STEP)

    def body(w_ref, g_ref, m_ref, v_ref, d_ref, nm_ref, nv_ref):
        gv = g_ref[...]
        nm = ADAM_B1 * m_ref[...] + (1.0 - ADAM_B1) * gv
        nv = ADAM_B2 * v_ref[...] + (1.0 - ADAM_B2) * (gv * gv)
        d_ref[...] = -ADAM_LR * ((nm * c1) / (jnp.sqrt(nv * c2) + ADAM_EPS) + ADAM_WD * w_ref[...])
        nm_ref[...] = nm
        nv_ref[...] = nv

    blk = pl.BlockSpec((tr, tc), lambda i, j: (i, j))
    shp = jax.ShapeDtypeStruct((r, c), F32)
    return pl.pallas_call(body, name=name, grid=(r // tr, c // tc), in_specs=[blk] * 4, out_specs=[blk] * 3,
                          out_shape=[shp, shp, shp],
                          compiler_params=_params(("arbitrary", "arbitrary"), VMEM_LIMIT))(w, g, m, v)


PACK_UNIT = 8 * LANES


def _pack(arrs):
    parts = []
    for a in arrs:
        f = a.reshape(-1).astype(F32)
        parts.append(jnp.pad(f, (0, (-f.shape[0]) % PACK_UNIT)).reshape(-1, LANES))
    return jnp.concatenate(parts, axis=0)


def _unpack(m, shapes):
    outs, row = [], 0
    for s in shapes:
        n = int(np.prod(s))
        rows = -(-n // PACK_UNIT) * 8
        outs.append(m[row:row + rows].reshape(-1)[:n].reshape(s))
        row += rows
    return outs


WEIGHTS = ["ffn1_norm", "ffn1_w_gate", "ffn1_w_up", "ffn1_w_down", "mix_norm", "w_in", "conv_w", "a_log", "dt_bias",
           "gdn_norm_w", "q_norm_w", "k_norm_w", "rel_bias", "w_out", "ffn2_norm", "ffn2_w_gate", "ffn2_w_up",
           "ffn2_w_down", "final_norm"]
BIG = ["ffn1_w_gate", "ffn1_w_up", "ffn1_w_down", "w_in", "w_out", "ffn2_w_gate", "ffn2_w_up", "ffn2_w_down"]
SMALL = [n for n in WEIGHTS if n not in BIG]
COL_SHARDED = ["ffn1_w_gate", "ffn1_w_up", "w_in", "ffn2_w_gate", "ffn2_w_up"]
N_IN_COLS = 3600
TM = 256
TE = 512
TK = 2048


def kernel(x, ffn1_norm, ffn1_w_gate, ffn1_w_up, ffn1_w_down, mix_norm, w_in, conv_w, a_log, dt_bias, gdn_norm_w, q_norm_w, k_norm_w, rel_bias, w_out, ffn2_norm, ffn2_w_gate, ffn2_w_up, ffn2_w_down, final_norm, loss_target, m_ffn1_norm, m_ffn1_w_gate, m_ffn1_w_up, m_ffn1_w_down, m_mix_norm, m_w_in, m_conv_w, m_a_log, m_dt_bias, m_gdn_norm_w, m_q_norm_w, m_k_norm_w, m_rel_bias, m_w_out, m_ffn2_norm, m_ffn2_w_gate, m_ffn2_w_up, m_ffn2_w_down, m_final_norm, v_ffn1_norm, v_ffn1_w_gate, v_ffn1_w_up, v_ffn1_w_down, v_mix_norm, v_w_in, v_conv_w, v_a_log, v_dt_bias, v_gdn_norm_w, v_q_norm_w, v_k_norm_w, v_rel_bias, v_w_out, v_ffn2_norm, v_ffn2_w_gate, v_ffn2_w_up, v_ffn2_w_down, v_final_norm):
    p = dict(locals())
    xs, target = x[0], loss_target[0]
    t, d = xs.shape
    nc = t // CHUNK
    tk = min(TK, t)
    me = 2 * lax.axis_index("x") + lax.axis_index("y")

    first = ["ffn1_w_gate", "ffn1_w_up", "ffn1_w_down"]
    later = [n for n in BIG if n not in first] + ["conv_w"]
    local = lambda n, a: a[0].T if n in COL_SHARDED else a[0]
    shards = {n: local(n, p[n]).astype(BF16) for n in BIG}
    shards["conv_w"] = conv_w[0]
    gw = dict(zip(first, _all_gather([shards[n] for n in first])))
    f1 = (gw["ffn1_w_gate"], gw["ffn1_w_up"], gw["ffn1_w_down"])
    (x1, xn1, g1, u1), landed = _ffn_fwd(xs, ffn1_norm, *f1, TM, "ffn1_fwd",
                                         exchange=_gather_exchange([shards[n] for n in later]))
    gw.update(zip(later, _gather_forward(landed, [shards[n] for n in later])))
    w_in_t = gw["w_in"].reshape(N_IN_COLS, d)
    gates = slice(P_AB[0], P_AB[0] + N_GATE_COLS)
    wp = jnp.concatenate([w_in_t[:gates.start], jnp.pad(w_in_t[gates], ((0, LANES - N_GATE_COLS), (0, 0))),
                          w_in_t[gates.stop:]], axis=0)
    w_out_full = gw["w_out"].reshape(d, d)
    conv_rows = conv_w.shape[1]
    cw = jnp.pad(gw["conv_w"].reshape(N_SHARDS * conv_rows, CONV_TAPS).T, ((0, 8 - CONV_TAPS), (0, 0)))
    gp = jnp.pad(jnp.stack([a_log.reshape(8), dt_bias.reshape(8)]), ((0, 6), (0, LANES - 8)))
    gdn_w = gdn_norm_w.reshape(1, GDN_DIM)
    qw_t = jnp.tile(q_norm_w.reshape(1, SWA_DIM), (1, SWA_HEADS))
    kw_t = jnp.tile(k_norm_w.reshape(1, SWA_DIM), (1, SWA_HEADS))
    bd = jnp.asarray(np.kron(np.eye(2), np.full((SWA_DIM, SWA_DIM), 1.0 / SWA_DIM)), BF16)
    f2 = (gw["ffn2_w_gate"], gw["ffn2_w_up"], gw["ffn2_w_down"])

    hn, qkva, z, ab, qkvb = _mix_in_fwd(x1, mix_norm, wp, TM)
    qkvc, gb = _gdn_prep_fwd(qkva, cw, ab, gp, TM)
    gbt = jnp.transpose(gb[:, :16].reshape(nc, CHUNK, 16), (0, 2, 1))
    o_f, o_b, gdn_saved = _gdn_fwd(qkvc, gb, gbt)
    oa = _gdn_post_fwd(o_f, o_b, z, gdn_w, TE)
    o_swa, o_swa16, swa_saved = _swa_branch_fwd(qkvb, qw_t, kw_t, rel_bias, bd, TE)
    x2 = _mix_out_fwd(x1, oa, o_swa, w_out_full, TE)
    (dx3, xn2, g2, u2, loss_part, d_final), _ = _ffn_fwd(x2, ffn2_norm, *f2, TM, "ffn2_fwd", head=(final_norm, target))

    def pair_sums(partials, tag):
        return [_add_pair(g, land, f"rs_add_{tag}{i}") for i, (g, land) in enumerate(zip(partials, _rs_pair(partials)))]

    (dx2, dyh2, dg2, du2, h2, d_nw2), _ = _ffn_bwd_dx(dx3, x2, ffn2_norm, g2, u2, *f2, TM, "ffn2_bwd_dx")
    dwg2 = _matmul_tn(dg2, xn2, tk, "ffn2_dwg")
    dwu2 = _matmul_tn(du2, xn2, tk, "ffn2_dwu")
    dwd2 = _matmul_tn(h2, dyh2, tk, "ffn2_dwd")
    sums_f2 = pair_sums([dwg2, dwu2, dwd2], "a")
    doa, dob, dx2b = _mix_out_bwd(dx2, w_out_full, TE)
    dwo = jnp.concatenate([_matmul_tn(oa, dx2b, tk, "w_out_dw_a")[0], _matmul_tn(o_swa16, dx2b, tk, "w_out_dw_b")[0]],
                          axis=0).reshape(N_SHARDS, d // N_SHARDS, d)
    do_g, dz, d_gdnw = _gdn_post_bwd(doa, o_f, o_b, z, gdn_w, TE)
    (dqkvc, dgates), slots_f2 = _gdn_bwd(qkvc, gb, gbt, do_g, gdn_saved, exchange=_scatter_exchange(sums_f2))
    dqkva, dab, dcw, dgp = _gdn_prep_bwd(qkva, cw, ab, gp, dqkvc, dgates, TM)
    dqkvb, d_qw, d_kw, d_rel = _swa_branch_bwd(dob, o_swa, swa_saved, qkvb, qw_t, kw_t, bd, TE)
    dpieces = (dqkva, dz, dab, dqkvb)
    dwp = [_matmul_tn(dp, hn, tk, f"w_in_dw_{i}")[0] for i, dp in enumerate(dpieces)]
    dw_in = jnp.concatenate([dwp[0], dwp[1], dwp[2][:N_GATE_COLS], dwp[3]], axis=0)
    dw_in = dw_in.reshape(N_SHARDS, N_IN_COLS // N_SHARDS, d)
    sums_mix = pair_sums([dw_in, dwo], "b")
    (dx1, d_mixnw), slots_mix = _mix_in_bwd_dx(dx2, x1, mix_norm, dpieces, wp, TM, exchange=_scatter_exchange(sums_mix))
    (gx, dyh1, dg1, du1, h1, d_nw1), _ = _ffn_bwd_dx(dx1, xs, ffn1_norm, g1, u1, *f1, TM, "ffn1_bwd_dx")
    dwg1 = _matmul_tn(dg1, xn1, tk, "ffn1_dwg")
    dwu1 = _matmul_tn(du1, xn1, tk, "ffn1_dwu")
    sums_gu = pair_sums([dwg1, dwu1], "c")
    dwd1, slots_gu = _matmul_tn(h1, dyh1, tk, "ffn1_dwd", exchange=_scatter_exchange(sums_gu))
    sums_d = pair_sums([dwd1], "d")
    slots = slots_gu + _rs_chips(sums_d) + slots_mix + slots_f2
    sums = sums_gu + sums_d + sums_mix + sums_f2
    halves = [_sum_slots(s, own, f"rs_sum_{i}") for i, (s, own) in enumerate(zip(slots, sums))]
    g_big = dict(zip(BIG, _rs_join(halves, [_split_axis(shards[n].shape) for n in BIG])))

    small_partial = {"ffn1_norm": d_nw1, "mix_norm": d_mixnw, "a_log": dgp[0, 0:8], "dt_bias": dgp[1, 0:8],
                     "gdn_norm_w": d_gdnw, "q_norm_w": d_qw, "k_norm_w": d_kw, "rel_bias": d_rel,
                     "ffn2_norm": d_nw2, "final_norm": d_final, "conv_w": dcw[0:CONV_TAPS].T}
    red = _all_reduce_small(_pack([small_partial[n] for n in SMALL] + [loss_part[0, 0:1]]))
    full_shapes = [p[n].shape if n != "conv_w" else (N_SHARDS * conv_rows, CONV_TAPS) for n in SMALL]
    red_parts = _unpack(red, full_shapes + [(1,)])
    loss = red_parts[-1].reshape(())
    g_small = dict(zip(SMALL, red_parts[:-1]))
    g_small["conv_w"] = lax.dynamic_slice_in_dim(g_small["conv_w"], me * conv_rows, conv_rows, 0).reshape(conv_w.shape)

    grads, deltas, new_m, new_v = {}, {}, {}, {}
    for n in BIG:
        back = (lambda a: a.T[None]) if n in COL_SHARDED else (lambda a: a[None])
        grads[n] = back(g_big[n])
        dl, nm, nv = _adamw(local(n, p[n]), g_big[n], local(n, p["m_" + n]), local(n, p["v_" + n]), "adamw_" + n)
        deltas[n], new_m[n], new_v[n] = back(dl), back(nm), back(nv)
    packed = [_pack([src[n] for n in SMALL]) for src in
              ({n: p[n] for n in SMALL}, g_small, {n: p["m_" + n] for n in SMALL}, {n: p["v_" + n] for n in SMALL})]
    small_shapes = [p[n].shape for n in SMALL]
    for dst, arr in zip((deltas, new_m, new_v), _adamw(*packed, "adamw_small")):
        dst.update(zip(SMALL, _unpack(arr, small_shapes)))
    grads.update(g_small)

    return (loss, gx[None], *[grads[n] for n in WEIGHTS], *[deltas[n] for n in WEIGHTS],
            *[new_m[n] for n in WEIGHTS], *[new_v[n] for n in WEIGHTS])
```

```python
import math
from typing import Callable, NamedTuple

import numpy as np
import jax
import jax.numpy as jnp
from jax import lax
from jax.experimental import pallas as pl
from jax.experimental.pallas import tpu as pltpu

F32 = jnp.float32
BF16 = jnp.bfloat16
HIGHEST = lax.Precision.HIGHEST
MESH = pl.DeviceIdType.MESH

EPS = 1e-6
NEG_BIG = -1e30
GDN_HEADS = 4
GDN_DIM = 128
CHUNK = 64
SWA_HEADS = 8
SWA_DIM = 64
PATTERNS = ((128, 1), (512, 4), (2048, 16))
RADIUS = 64
REL_BUCKETS = 32
REL_MAX_DISTANCE = 1024
CONV_TAPS = 5
N_SHARDS = 4
LANES = 128
VMEM_LIMIT = 56 * 1024 * 1024

ADAM_LR, ADAM_B1, ADAM_B2, ADAM_EPS, ADAM_WD, ADAM_STEP = 0.001, 0.9, 0.999, 1e-08, 0.01, 10


def _params(sem=None, vmem=None):
    return pltpu.CompilerParams(dimension_semantics=sem, vmem_limit_bytes=vmem)


def _resident(shape):
    nd = len(shape)
    return pl.BlockSpec(shape, lambda *_: (0,) * nd, pipeline_mode=pl.Buffered(1))


ANY = pl.BlockSpec(memory_space=pl.ANY)


class _Exchange(NamedTuple):
    arrays: tuple
    out_shape: tuple
    start: Callable
    finish: Callable


def _grid_call(body, name, nsteps, in_specs, out_specs, out_shape, operands, scratch=(), exchange=None):
    params = _params(("arbitrary",), VMEM_LIMIT)
    if exchange is None:
        res = pl.pallas_call(body, name=name, grid=(nsteps,), in_specs=list(in_specs), out_specs=list(out_specs),
                             out_shape=list(out_shape), scratch_shapes=list(scratch), compiler_params=params)(*operands)
        return list(res), []
    n_in, n_out, k, n_scr = len(in_specs), len(out_specs), len(exchange.arrays), len(scratch)

    def wrapped(*refs):
        ins, cin = refs[:n_in], refs[n_in:n_in + k]
        outs, cout = refs[n_in + k:n_in + k + n_out], refs[n_in + k + n_out:n_in + 2 * k + n_out]
        rest = refs[n_in + 2 * k + n_out:]
        scr, (send_sems, recv_sems) = rest[:n_scr], rest[n_scr:]

        @pl.when(pl.program_id(0) == 0)
        def _():
            exchange.start(cin, cout, send_sems, recv_sems)

        body(*ins, *outs, *scr)

        @pl.when(pl.program_id(0) == nsteps - 1)
        def _():
            exchange.finish(cin, cout, send_sems, recv_sems)

    res = pl.pallas_call(
        wrapped, name=name, grid=(nsteps,), in_specs=list(in_specs) + [ANY] * k, out_specs=list(out_specs) + [ANY] * k,
        out_shape=list(out_shape) + list(exchange.out_shape),
        scratch_shapes=list(scratch) + [pltpu.SemaphoreType.DMA((k, 4)), pltpu.SemaphoreType.DMA((k, 4))],
        compiler_params=params)(*operands, *exchange.arrays)
    return list(res[:n_out]), list(res[n_out:])


def _dot(a, b):
    return jnp.dot(a.astype(BF16), b.astype(BF16), preferred_element_type=F32)


def _dot_nt(a, b):
    return lax.dot_general(a.astype(BF16), b.astype(BF16), (((1,), (1,)), ((), ())), preferred_element_type=F32)


def _dot_tn(a, b):
    return lax.dot_general(a.astype(BF16), b.astype(BF16), (((0,), (0,)), ((), ())), preferred_element_type=F32)


def _dot_hi(a, b):
    return jnp.dot(a, b, preferred_element_type=F32, precision=HIGHEST)


def _sigmoid(x):
    return 1.0 / (1.0 + jnp.exp(-x))


def _rstd(xf):
    return lax.rsqrt(jnp.mean(xf * xf, axis=-1, keepdims=True) + EPS)


def _rms_bwd(xf, r, nw, dxn):
    xhat = xf * r
    dxh = dxn * nw
    dx = r * (dxh - xhat * jnp.mean(dxh * xhat, axis=-1, keepdims=True))
    return dx, jnp.sum(dxn * xhat, axis=0, keepdims=True)


def _ffn_fwd(x, nw, wg, wu, wd, tm, name, exchange=None, head=None):
    t, d = x.shape
    nj, fs, _ = wg.shape

    def body(x_ref, nw_ref, wg_ref, wu_ref, wd_ref, *rest):
        if head is None:
            y_ref, xn_ref, g_ref, u_ref = rest
        else:
            fw_ref, t_ref, y_ref, xn_ref, g_ref, u_ref, loss_ref, dfw_ref = rest

            @pl.when(pl.program_id(0) == 0)
            def _():
                loss_ref[...] = jnp.zeros_like(loss_ref)
                dfw_ref[...] = jnp.zeros_like(dfw_ref)

        xf = x_ref[...]
        xn = (xf * _rstd(xf) * nw_ref[...]).astype(BF16)
        xn_ref[...] = xn
        acc = jnp.zeros((tm, d), F32)
        for j in range(nj):
            g = _dot_nt(xn, wg_ref[j])
            u = _dot_nt(xn, wu_ref[j])
            h = (g * _sigmoid(g) * u).astype(BF16)
            acc = acc + jnp.dot(h, wd_ref[j], preferred_element_type=F32)
            g_ref[j] = g.astype(BF16)
            u_ref[j] = u.astype(BF16)
        y = xf + 0.5 * acc
        if head is None:
            y_ref[...] = y
        else:
            r = _rstd(y)
            err = y * r * fw_ref[...] - t_ref[...]
            loss_ref[...] += 0.5 * jnp.sum(jnp.mean(err * err, axis=-1, keepdims=True), axis=0, keepdims=True)
            dy, dfw = _rms_bwd(y, r, fw_ref[...], err * (1.0 / d))
            y_ref[...] = dy
            dfw_ref[...] += dfw

    row = pl.BlockSpec((tm, d), lambda i: (i, 0))
    act = pl.BlockSpec((nj, tm, fs), lambda i: (0, i, 0))
    in_specs = [row, _resident((1, d)), _resident(wg.shape), _resident(wu.shape), _resident(wd.shape)]
    out_specs = [row, row, act, act]
    out_shape = [jax.ShapeDtypeStruct((t, d), F32), jax.ShapeDtypeStruct((t, d), BF16),
                 jax.ShapeDtypeStruct((nj, t, fs), BF16), jax.ShapeDtypeStruct((nj, t, fs), BF16)]
    operands = (x, nw, wg, wu, wd)
    if head is not None:
        in_specs += [_resident((1, d)), row]
        out_specs += [pl.BlockSpec((1, LANES), lambda i: (0, 0)), pl.BlockSpec((1, d), lambda i: (0, 0))]
        out_shape += [jax.ShapeDtypeStruct((1, LANES), F32), jax.ShapeDtypeStruct((1, d), F32)]
        operands += tuple(head)
    return _grid_call(body, name, t // tm, in_specs, out_specs, out_shape, operands, exchange=exchange)


def _ffn_bwd_dx(dy, x, nw, g, u, wg, wu, wd, tm, name, exchange=None):
    t, d = x.shape
    nj, fs, _ = wg.shape

    def body(dy_ref, x_ref, nw_ref, g_ref, u_ref, wg_ref, wu_ref, wd_ref,
             dx_ref, dyh_ref, dg_ref, du_ref, h_ref, dnw_ref):
        @pl.when(pl.program_id(0) == 0)
        def _():
            dnw_ref[...] = jnp.zeros_like(dnw_ref)

        dyv = dy_ref[...]
        dyh = (0.5 * dyv).astype(BF16)
        dyh_ref[...] = dyh
        dxn = jnp.zeros((tm, d), F32)
        dh_next = _dot_nt(dyh, wd_ref[0])
        for j in range(nj):
            dh = dh_next
            gv = g_ref[j].astype(F32)
            uv = u_ref[j].astype(F32)
            sg = _sigmoid(gv)
            si = gv * sg
            dg = (dh * uv * (sg * (1.0 + gv * (1.0 - sg)))).astype(BF16)
            du = (dh * si).astype(BF16)
            if j + 1 < nj:
                dh_next = _dot_nt(dyh, wd_ref[j + 1])
            h_ref[j] = (si * uv).astype(BF16)
            dg_ref[j] = dg
            du_ref[j] = du
            dxn = dxn + _dot(dg, wg_ref[j]) + _dot(du, wu_ref[j])
        xf = x_ref[...]
        dxr, dnw = _rms_bwd(xf, _rstd(xf), nw_ref[...], dxn)
        dx_ref[...] = dyv + dxr
        dnw_ref[...] += dnw

    row = pl.BlockSpec((tm, d), lambda i: (i, 0))
    act = pl.BlockSpec((nj, tm, fs), lambda i: (0, i, 0))
    act_shape = jax.ShapeDtypeStruct((nj, t, fs), BF16)
    return _grid_call(
        body, name, t // tm,
        [row, row, _resident((1, d)), act, act, _resident(wg.shape), _resident(wu.shape), _resident(wd.shape)],
        [row, row, act, act, act, pl.BlockSpec((1, d), lambda i: (0, 0))],
        [jax.ShapeDtypeStruct((t, d), F32), jax.ShapeDtypeStruct((t, d), BF16),
         act_shape, act_shape, act_shape, jax.ShapeDtypeStruct((1, d), F32)],
        (dy, x, nw, g, u, wg, wu, wd), exchange=exchange)


def _matmul_tn(a, b, tk, name, exchange=None):
    a3, b3 = a.ndim == 3, b.ndim == 3
    nj = a.shape[0] if a3 else (b.shape[0] if b3 else 1)
    t, m = a.shape[-2:]
    n = b.shape[-1]
    nt = t // tk

    def body(a_ref, b_ref, o_ref, acc_ref):
        k = pl.program_id(0) % nt

        @pl.when(k == 0)
        def _():
            acc_ref[...] = jnp.zeros_like(acc_ref)

        acc_ref[...] += lax.dot_general(a_ref[...], b_ref[...], (((0,), (0,)), ((), ())),
                                        preferred_element_type=F32)

        @pl.when(k == nt - 1)
        def _():
            o_ref[...] = acc_ref[...].astype(o_ref.dtype)

    a_spec = (pl.BlockSpec((None, tk, m), lambda i: (i // nt, i % nt, 0)) if a3
              else pl.BlockSpec((tk, m), lambda i: (i % nt, 0)))
    b_spec = (pl.BlockSpec((None, tk, n), lambda i: (i // nt, i % nt, 0)) if b3
              else pl.BlockSpec((tk, n), lambda i: (i % nt, 0)))
    (out,), landed = _grid_call(
        body, name, nj * nt, [a_spec, b_spec], [pl.BlockSpec((None, m, n), lambda i: (i // nt, 0, 0))],
        [jax.ShapeDtypeStruct((nj, m, n), BF16)], (a, b), scratch=[pltpu.VMEM((m, n), F32)], exchange=exchange)
    return out if exchange is None else (out, landed)


P_QKVA, P_Z, P_AB, P_QKVB = (0, 1536), (1536, 2048), (2048, 2176), (2176, 3712)
P_PIECES = (P_QKVA, P_Z, P_AB, P_QKVB)
N_GATE_COLS = 4 * GDN_HEADS


def _mix_in_fwd(x1, nw, wp, tm):
    t, d = x1.shape

    def body(x_ref, nw_ref, w_ref, hn_ref, *outs):
        xf = x_ref[...]
        xn = (xf * _rstd(xf) * nw_ref[...]).astype(BF16)
        hn_ref[...] = xn
        for (a, b), o_ref in zip(P_PIECES, outs):
            o_ref[...] = _dot_nt(xn, w_ref[a:b, :])

    row = pl.BlockSpec((tm, d), lambda i: (i, 0))
    return pl.pallas_call(
        body, name="mix_in_fwd", grid=(t // tm,),
        in_specs=[row, _resident((1, d)), _resident(wp.shape)],
        out_specs=[row] + [pl.BlockSpec((tm, b - a), lambda i: (i, 0)) for a, b in P_PIECES],
        out_shape=[jax.ShapeDtypeStruct((t, d), BF16)]
                  + [jax.ShapeDtypeStruct((t, b - a), F32) for a, b in P_PIECES],
        compiler_params=_params(("arbitrary",), VMEM_LIMIT),
    )(x1, nw, wp)


def _mix_in_bwd_dx(dx, x1, nw, dpieces, wp, tm, exchange=None):
    t, d = x1.shape

    def body(dx_ref, x_ref, nw_ref, p0, p1, p2, p3, w_ref, o_ref, dnw_ref):
        @pl.when(pl.program_id(0) == 0)
        def _():
            dnw_ref[...] = jnp.zeros_like(dnw_ref)

        dh = jnp.zeros((tm, d), F32)
        for (a, b), p_ref in zip(P_PIECES, (p0, p1, p2, p3)):
            dh = dh + _dot(p_ref[...], w_ref[a:b, :])
        xf = x_ref[...]
        dxr, dnw = _rms_bwd(xf, _rstd(xf), nw_ref[...], dh)
        o_ref[...] = dx_ref[...] + dxr
        dnw_ref[...] += dnw

    row = pl.BlockSpec((tm, d), lambda i: (i, 0))
    return _grid_call(
        body, "mix_in_bwd_dx", t // tm,
        [row, row, _resident((1, d))]
        + [pl.BlockSpec((tm, b - a), lambda i: (i, 0)) for a, b in P_PIECES] + [_resident(wp.shape)],
        [row, pl.BlockSpec((1, d), lambda i: (0, 0))],
        [jax.ShapeDtypeStruct((t, d), F32), jax.ShapeDtypeStruct((1, d), F32)],
        (dx, x1, nw, *dpieces, wp), exchange=exchange)


def _mix_out_fwd(x1, oa, ob, w, tm):
    t, d = x1.shape
    half = oa.shape[1]

    def body(x_ref, oa_ref, ob_ref, w_ref, o_ref):
        o_ref[...] = (x_ref[...] + _dot(oa_ref[...], w_ref[0:half, :]) + _dot(ob_ref[...], w_ref[half:2 * half, :]))

    row = pl.BlockSpec((tm, d), lambda i: (i, 0))
    hrow = pl.BlockSpec((tm, half), lambda i: (i, 0))
    return pl.pallas_call(
        body, name="mix_out_fwd", grid=(t // tm,),
        in_specs=[row, hrow, hrow, _resident(w.shape)],
        out_specs=row, out_shape=jax.ShapeDtypeStruct((t, d), F32),
        compiler_params=_params(("arbitrary",), VMEM_LIMIT),
    )(x1, oa, ob, w)


def _mix_out_bwd(dx2, w, tm):
    t, d = dx2.shape
    half = w.shape[0] // 2

    def body(dx_ref, w_ref, doa_ref, dob_ref, dxb_ref):
        dxb = dx_ref[...].astype(BF16)
        dxb_ref[...] = dxb
        doa_ref[...] = _dot_nt(dxb, w_ref[0:half, :])
        dob_ref[...] = _dot_nt(dxb, w_ref[half:2 * half, :])

    row = pl.BlockSpec((tm, d), lambda i: (i, 0))
    hrow = pl.BlockSpec((tm, half), lambda i: (i, 0))
    return pl.pallas_call(
        body, name="mix_out_bwd", grid=(t // tm,),
        in_specs=[row, _resident(w.shape)],
        out_specs=[hrow, hrow, row],
        out_shape=[jax.ShapeDtypeStruct((t, half), F32), jax.ShapeDtypeStruct((t, half), F32),
                   jax.ShapeDtypeStruct((t, d), BF16)],
        compiler_params=_params(("arbitrary",), VMEM_LIMIT),
    )(dx2, w)


HALO = 8


def _halo_row_specs(tr, cols, nrow8):
    per = tr // HALO
    return [pl.BlockSpec((tr, cols), lambda i: (i, 0)),
            pl.BlockSpec((HALO, cols), lambda i: (jnp.maximum(i * per - 1, 0), 0)),
            pl.BlockSpec((HALO, cols), lambda i: (jnp.minimum((i + 1) * per, nrow8 - 1), 0))]


def _conv_window(xm, xp, xn, first, last, cols):
    prev = jnp.where(first, 0.0, xp[:, cols])
    nxt = jnp.where(last, 0.0, xn[:, cols])
    return jnp.concatenate([prev, xm[:, cols], nxt], axis=0)


def _shift_rows(xw, off):
    n = xw.shape[0]
    sh = (-off) % n
    return xw if sh == 0 else pltpu.roll(xw, sh, 0)


def _conv_pre(xw, cw_ref, cols):
    acc, shifted = None, []
    for j in range(CONV_TAPS):
        shifted.append(_shift_rows(xw, j - CONV_TAPS // 2))
        term = shifted[-1] * cw_ref[j:j + 1, cols]
        acc = term if acc is None else acc + term
    return acc, shifted


def _softplus(x):
    u = jnp.exp(-jnp.abs(x))
    w = 1.0 + u
    log1p = jnp.where(w == 1.0, u, jnp.log(w) * u / jnp.where(w == 1.0, 1.0, w - 1.0))
    return jnp.maximum(x, 0.0) + log1p


def _gdn_prep_fwd(qkva, cw, ab, gp, tr):
    t, c = qkva.shape
    nt = t // tr
    ncb = c // LANES

    def body(xm, xp, xn, cw_ref, ab_ref, gp_ref, o_ref, gb_ref):
        i = pl.program_id(0)
        first, last = i == 0, i == nt - 1
        for cb in range(ncb):
            cols = slice(cb * LANES, (cb + 1) * LANES)
            xw = _conv_window(xm, xp, xn, first, last, cols)
            pre = _conv_pre(xw, cw_ref, cols)[0][HALO:HALO + tr]
            y = pre * _sigmoid(pre)
            if cb < 2 * GDN_HEADS:
                y = y * lax.rsqrt(jnp.sum(y * y, axis=-1, keepdims=True) + EPS)
            if cb < GDN_HEADS:
                y = y * (GDN_DIM ** -0.5)
            o_ref[:, cols] = y
        abv = ab_ref[...]
        lane = lax.broadcasted_iota(jnp.int32, abv.shape, 1)
        g = -jnp.exp(gp_ref[0:1, :]) * _softplus(abv + gp_ref[1:2, :])
        gb_ref[...] = jnp.where(lane < 8, g, jnp.where(lane < 16, _sigmoid(abv), 0.0))

    return pl.pallas_call(
        body, name="gdn_prep_fwd", grid=(nt,),
        in_specs=_halo_row_specs(tr, c, t // HALO)
                 + [_resident(cw.shape), pl.BlockSpec((tr, LANES), lambda i: (i, 0)), _resident(gp.shape)],
        out_specs=[pl.BlockSpec((tr, c), lambda i: (i, 0)), pl.BlockSpec((tr, LANES), lambda i: (i, 0))],
        out_shape=[jax.ShapeDtypeStruct((t, c), F32), jax.ShapeDtypeStruct((t, LANES), F32)],
        compiler_params=_params(("arbitrary",), VMEM_LIMIT),
    )(qkva, qkva, qkva, cw, ab, gp)


def _gdn_prep_bwd(qkva, cw, ab, gp, dy, dgates, tr):
    t, c = qkva.shape
    nt = t // tr
    ncb = c // LANES

    def body(xm, xp, xn, fm, fp, fn, cw_ref, ab_ref, gp_ref, gf_ref, dx_ref, dab_ref, dcw_ref, dgp_ref):
        i = pl.program_id(0)
        first, last = i == 0, i == nt - 1

        @pl.when(first)
        def _():
            dcw_ref[...] = jnp.zeros_like(dcw_ref)
            dgp_ref[...] = jnp.zeros_like(dgp_ref)

        sub8 = lax.broadcasted_iota(jnp.int32, (8, LANES), 0)
        for cb in range(ncb):
            cols = slice(cb * LANES, (cb + 1) * LANES)
            xw = _conv_window(xm, xp, xn, first, last, cols)
            dyw = _conv_window(fm, fp, fn, first, last, cols)
            pre, x_shifted = _conv_pre(xw, cw_ref, cols)
            sg = _sigmoid(pre)
            s = pre * sg
            if cb < 2 * GDN_HEADS:
                scale = (GDN_DIM ** -0.5) if cb < GDN_HEADS else 1.0
                r = lax.rsqrt(jnp.sum(s * s, axis=-1, keepdims=True) + EPS)
                dn = dyw * scale
                ds = r * dn - s * (r * r * r) * jnp.sum(dn * s, axis=-1, keepdims=True)
            else:
                ds = dyw
            dpre = ds * (sg * (1.0 + pre * (1.0 - sg)))
            dx = None
            dcw = jnp.zeros((8, LANES), F32)
            for j in range(CONV_TAPS):
                off = j - CONV_TAPS // 2
                term = _shift_rows(dpre, -off)[HALO:HALO + tr] * cw_ref[j:j + 1, cols]
                dx = term if dx is None else dx + term
                tap = jnp.sum(dpre[HALO:HALO + tr] * x_shifted[j][HALO:HALO + tr], axis=0, keepdims=True)
                dcw = dcw + jnp.where(sub8 == j, tap, 0.0)
            dx_ref[:, cols] = dx.astype(BF16)
            dcw_ref[:, cols] += dcw

        abv = ab_ref[...]
        dgb = gf_ref[...]
        lane = lax.broadcasted_iota(jnp.int32, abv.shape, 1)
        nea = -jnp.exp(gp_ref[0:1, :])
        xs = abv + gp_ref[1:2, :]
        g = nea * _softplus(xs)
        beta = _sigmoid(abv)
        da = dgb * nea * _sigmoid(xs)
        dab = jnp.where(lane < 8, da, jnp.where(lane < 16, dgb * beta * (1.0 - beta), 0.0))
        dab_ref[...] = dab.astype(BF16)
        keep = lane[0:1, :] < 8
        dalog = jnp.where(keep, jnp.sum(dgb * g, axis=0, keepdims=True), 0.0)
        ddtb = jnp.where(keep, jnp.sum(da, axis=0, keepdims=True), 0.0)
        dgp_ref[...] += jnp.where(sub8 == 0, dalog, 0.0) + jnp.where(sub8 == 1, ddtb, 0.0)

    lrow = pl.BlockSpec((tr, LANES), lambda i: (i, 0))
    halo = _halo_row_specs(tr, c, t // HALO)
    return pl.pallas_call(
        body, name="gdn_prep_bwd", grid=(nt,),
        in_specs=halo + halo + [_resident(cw.shape), lrow, _resident(gp.shape), lrow],
        out_specs=[pl.BlockSpec((tr, c), lambda i: (i, 0)), lrow,
                   pl.BlockSpec(cw.shape, lambda i: (0, 0)), pl.BlockSpec(gp.shape, lambda i: (0, 0))],
        out_shape=[jax.ShapeDtypeStruct((t, c), BF16), jax.ShapeDtypeStruct((t, LANES), BF16),
                   jax.ShapeDtypeStruct(cw.shape, F32), jax.ShapeDtypeStruct(gp.shape, F32)],
        compiler_params=_params(("arbitrary",), VMEM_LIMIT),
    )(qkva, qkva, qkva, dy, dy, dy, cw, ab, gp, dgates)


def _chunk_masks(lower):
    ii = lax.broadcasted_iota(jnp.int32, (CHUNK, CHUNK), 0)
    jj = lax.broadcasted_iota(jnp.int32, (CHUNK, CHUNK), 1)
    incl = (ii >= jj) if lower else (ii <= jj)
    strict = (ii > jj) if lower else (ii < jj)
    return ii, jj, incl, strict


def _dot3(a, b):
    ah = a.astype(BF16)
    al = (a - ah.astype(F32)).astype(BF16)
    bh = b.astype(BF16)
    bl = (b - bh.astype(F32)).astype(BF16)
    d = lambda u, v: jnp.dot(u, v, preferred_element_type=F32)
    return d(ah, bh) + (d(ah, bl) + d(al, bh))


def _tri_inv_many(lmats, ii, jj):
    m16 = (ii // 16) == (jj // 16)
    m32 = (ii // 32) == (jj // 32)
    eye = jnp.where(ii == jj, 1.0, 0.0)
    l16 = [jnp.where(m16, l, 0.0) for l in lmats]
    p2 = [_dot3(a, a) for a in l16]
    p4 = [_dot3(a, a) for a in p2]
    p8 = [_dot3(a, a) for a in p4]
    xs = [eye - a for a in l16]
    for ps in (p2, p4, p8):
        xs = [x + _dot3(x, p) for x, p in zip(xs, ps)]
    for off in ([jnp.where(m32 & jnp.logical_not(m16), l, 0.0) for l in lmats],
                [jnp.where(m32, 0.0, l) for l in lmats]):
        ys = [_dot3(x, c) for x, c in zip(xs, off)]
        xs = [x - _dot3(y, x) for x, y in zip(xs, ys)]
    return xs


def _col_to_row(col, ii, jj):
    return jnp.sum(jnp.where(ii == jj, col, 0.0), axis=0, keepdims=True)


def _row_to_col(row, ii, jj):
    return jnp.sum(jnp.where(ii == jj, row, 0.0), axis=1, keepdims=True)


def _chain_common(q, k, v, graw_col, graw_row, bcol, masks):
    ii, jj, incl, strict = masks
    inclt = jnp.logical_not(strict)
    gcol = jnp.sum(jnp.where(incl, graw_row, 0.0), axis=1, keepdims=True)
    grow = jnp.sum(jnp.where(inclt, graw_col, 0.0), axis=0, keepdims=True)
    glast = jnp.sum(graw_row, axis=1, keepdims=True)
    decay = jnp.where(incl, jnp.exp(jnp.where(incl, gcol - grow, 0.0)), 0.0)
    kb = k * bcol
    vb = v * bcol
    eg = jnp.exp(gcol)
    ek = jnp.exp(glast - gcol)
    kbg = kb * eg
    amat = _dot_nt(kb, k)
    qk = _dot_nt(q, k)
    return dict(gcol=gcol, glast=glast, decay=decay, kb=kb, vb=vb, eg=eg, ek=ek, kbg=kbg, amat=amat, qk=qk,
                intra=qk * decay, qg=q * eg, kdec=k * ek)


def _gdn_fwd(qkvc, gb, gbt):
    tm, u, w, qg, kd, intra, egl = _gdn_local_fwd(qkvc, gb, gbt)
    o_f, o_b, s_f, s_b, vn_f, vn_b = _gdn_scan_fwd(u, w, qg, kd, intra, egl, qkvc.shape[0])
    return o_f, o_b, dict(tm=tm, w=w, qg=qg, kd=kd, intra=intra, egl=egl, s=(s_f, s_b), vn=(vn_f, vn_b))


N_CHAINS = 2 * GDN_HEADS


LOCAL_CHUNKS = 4


def _load_chains(x_ref, g_ref, gt_ref, cc=0):
    hd = GDN_HEADS * GDN_DIM
    rows = slice(cc * CHUNK, (cc + 1) * CHUNK)
    chains = []
    for d in range(2):
        masks = _chunk_masks(d == 0)
        for h in range(GDN_HEADS):
            ch = d * GDN_HEADS + h
            q = x_ref[rows, h * GDN_DIM:(h + 1) * GDN_DIM]
            k = x_ref[rows, hd + h * GDN_DIM:hd + (h + 1) * GDN_DIM]
            v = x_ref[rows, 2 * hd + h * GDN_DIM:2 * hd + (h + 1) * GDN_DIM]
            bcol = g_ref[rows, 8 + ch:9 + ch]
            cm = _chain_common(q, k, v, g_ref[rows, ch:ch + 1], gt_ref[cc, ch:ch + 1, :], bcol, masks)
            chains.append(dict(cm, q=q, k=k, v=v, bcol=bcol, masks=masks, ch=ch, h=h, cc=cc))
    return chains


def _chain_shape(rows, cols, dtype):
    return lambda nc: jax.ShapeDtypeStruct((nc, N_CHAINS, rows, cols), dtype)


def _gdn_local_fwd(qkvc, gb, gbt):
    t = qkvc.shape[0]
    nc = t // CHUNK
    hd = GDN_HEADS * GDN_DIM

    def body(x_ref, g_ref, gt_ref, t_ref, u_ref, w_ref, qg_ref, kd_ref, in_ref, eg_ref):
        chains = [c for cc in range(LOCAL_CHUNKS) for c in _load_chains(x_ref, g_ref, gt_ref, cc)]
        ii, jj = chains[0]["masks"][0:2]
        tms = _tri_inv_many([jnp.where(c["masks"][3], c["amat"] * c["decay"], 0.0) for c in chains], ii, jj)
        uws = [_dot(tm, jnp.concatenate([c["vb"], c["kbg"]], axis=1)) for tm, c in zip(tms, chains)]
        for c, tm, uw in zip(chains, tms, uws):
            cc, ch = c["cc"], c["ch"]
            t_ref[cc, ch] = tm
            u_ref[cc, ch] = uw[:, :GDN_DIM]
            w_ref[cc, ch] = uw[:, GDN_DIM:].astype(BF16)
            qg_ref[cc, ch] = c["qg"].astype(BF16)
            kd_ref[cc, ch] = c["kdec"].astype(BF16)
            in_ref[cc, ch] = c["intra"].astype(BF16)
            eg_ref[cc, ch:ch + 1, :] = jnp.broadcast_to(jnp.exp(c["glast"]), (1, LANES))

    lc = LOCAL_CHUNKS
    blk = lambda rows, cols: pl.BlockSpec((lc, N_CHAINS, rows, cols), lambda n: (n, 0, 0, 0))
    shapes = [_chain_shape(CHUNK, CHUNK, F32), _chain_shape(CHUNK, GDN_DIM, F32), _chain_shape(CHUNK, GDN_DIM, BF16),
              _chain_shape(CHUNK, GDN_DIM, BF16), _chain_shape(CHUNK, GDN_DIM, BF16), _chain_shape(CHUNK, CHUNK, BF16)]
    return tuple(pl.pallas_call(
        body, name="gdn_local_fwd", grid=(nc // lc,),
        in_specs=[pl.BlockSpec((lc * CHUNK, 3 * hd), lambda n: (n, 0)), pl.BlockSpec((lc * CHUNK, LANES), lambda n: (n, 0)),
                  pl.BlockSpec((lc, 16, CHUNK), lambda n: (n, 0, 0))],
        out_specs=[blk(CHUNK, CHUNK), blk(CHUNK, GDN_DIM), blk(CHUNK, GDN_DIM), blk(CHUNK, GDN_DIM),
                   blk(CHUNK, GDN_DIM), blk(CHUNK, CHUNK), pl.BlockSpec((lc, N_CHAINS, LANES), lambda n: (n, 0, 0))],
        out_shape=[s(nc) for s in shapes] + [jax.ShapeDtypeStruct((nc, N_CHAINS, LANES), F32)],
        compiler_params=_params(("arbitrary",), VMEM_LIMIT),
    )(qkvc, gb, gbt))


SCAN_CHUNKS = 8


def _dir_specs(nc, rev):
    nb = nc // SCAN_CHUNKS

    def spec(d, rows, cols, own=False):
        chunk = (lambda n: n) if (d == 0) != rev else (lambda n: nb - 1 - n)
        blk = 0 if own else d
        if rows is None:
            return pl.BlockSpec((SCAN_CHUNKS, GDN_HEADS if own else N_CHAINS, cols), lambda n: (chunk(n), 0, 0))
        return pl.BlockSpec((SCAN_CHUNKS, GDN_HEADS, rows, cols), lambda n: (chunk(n), blk, 0, 0))

    def rows_spec(d, cols):
        chunk = (lambda n: n) if (d == 0) != rev else (lambda n: nb - 1 - n)
        return pl.BlockSpec((SCAN_CHUNKS * CHUNK, cols), lambda n: (chunk(n), 0))

    def order(d):
        return list(range(SCAN_CHUNKS)) if (d == 0) != rev else list(range(SCAN_CHUNKS - 1, -1, -1))
    return spec, rows_spec, order


def _gdn_scan_fwd(u, w, qg, kd, intra, egl, t):
    nc = t // CHUNK
    hd = GDN_HEADS * GDN_DIM

    def body(*refs):
        ins, outs, state = refs[:12], refs[12:18], refs[18]
        @pl.when(pl.program_id(0) == 0)
        def _():
            state[...] = jnp.zeros_like(state)

        chains = [(d, h) for d in range(2) for h in range(GDN_HEADS)]
        states = [state[ch] for ch in range(N_CHAINS)]
        for step in range(SCAN_CHUNKS):
            at = [order(d)[step] for d in range(2)]
            pick = lambda k, d, h: ins[2 * k + d][at[d], h]
            sbs = [s.astype(BF16) for s in states]
            ws = [_dot(pick(1, d, h), sb) for (d, h), sb in zip(chains, sbs)]
            o1 = [_dot(pick(2, d, h), sb) for (d, h), sb in zip(chains, sbs)]
            vns = [(pick(0, d, h) - wsb).astype(BF16) for (d, h), wsb in zip(chains, ws)]
            o2 = [_dot(pick(4, d, h), vn) for (d, h), vn in zip(chains, vns)]
            kv = [_dot_tn(pick(3, d, h), vn) for (d, h), vn in zip(chains, vns)]
            new_states = []
            for ch, (d, h) in enumerate(chains):
                outs[d][at[d] * CHUNK:(at[d] + 1) * CHUNK, h * GDN_DIM:(h + 1) * GDN_DIM] = o1[ch] + o2[ch]
                outs[2 + d][at[d], h] = states[ch]
                outs[4 + d][at[d], h] = vns[ch]
                new_states.append(states[ch] * ins[10 + d][at[d], ch:ch + 1, :] + kv[ch])
            states = new_states
        for ch in range(N_CHAINS):
            state[ch] = states[ch]

    spec, rows_spec, order = _dir_specs(nc, False)
    pair = lambda rows, cols, own=False: [spec(0, rows, cols, own), spec(1, rows, cols, own)]
    s_shape = jax.ShapeDtypeStruct((nc, GDN_HEADS, GDN_DIM, GDN_DIM), F32)
    vn_shape = jax.ShapeDtypeStruct((nc, GDN_HEADS, CHUNK, GDN_DIM), BF16)
    return pl.pallas_call(
        body, name="gdn_scan_fwd", grid=(nc // SCAN_CHUNKS,),
        in_specs=(pair(CHUNK, GDN_DIM) + pair(CHUNK, GDN_DIM) + pair(CHUNK, GDN_DIM) + pair(CHUNK, GDN_DIM)
                  + pair(CHUNK, CHUNK) + pair(None, LANES)),
        out_specs=([rows_spec(0, hd), rows_spec(1, hd)] + pair(GDN_DIM, GDN_DIM, True)
                   + pair(CHUNK, GDN_DIM, True)),
        out_shape=[jax.ShapeDtypeStruct((t, hd), F32), jax.ShapeDtypeStruct((t, hd), F32),
                   s_shape, s_shape, vn_shape, vn_shape],
        scratch_shapes=[pltpu.VMEM((N_CHAINS, GDN_DIM, GDN_DIM), F32)],
        compiler_params=_params(("arbitrary",), VMEM_LIMIT),
    )(u, u, w, w, qg, qg, kd, kd, intra, intra, egl, egl)


def _gdn_bwd(qkvc, gb, gbt, do, saved, exchange=None):
    scan = _gdn_scan_bwd(do, saved, qkvc.shape[0])
    return _gdn_local_bwd(qkvc, gb, gbt, do, saved, scan, exchange)


def _gdn_scan_bwd(do, saved, t):
    nc = t // CHUNK
    hd = GDN_HEADS * GDN_DIM

    def body(*refs):
        ins, outs, dstate = refs[:16], refs[16:26], refs[26]
        @pl.when(pl.program_id(0) == 0)
        def _():
            dstate[...] = jnp.zeros_like(dstate)

        chains = [(d, h) for d in range(2) for h in range(GDN_HEADS)]
        dss = [dstate[ch] for ch in range(N_CHAINS)]
        for step in range(SCAN_CHUNKS):
            at = [order(d)[step] for d in range(2)]
            pick = lambda k, d, h: ins[2 * k + d][at[d], h]
            dsbs = [ds.astype(BF16) for ds in dss]
            ss = [pick(1, d, h) for d, h in chains]
            sbs = [s.astype(BF16) for s in ss]
            dos = [ins[d][at[d] * CHUNK:(at[d] + 1) * CHUNK, h * GDN_DIM:(h + 1) * GDN_DIM].astype(BF16)
                   for d, h in chains]
            dv1 = [_dot_tn(pick(5, d, h), dov) for (d, h), dov in zip(chains, dos)]
            dv2 = [_dot(pick(4, d, h), dsb) for (d, h), dsb in zip(chains, dsbs)]
            ds1 = [_dot_tn(pick(3, d, h), dov) for (d, h), dov in zip(chains, dos)]
            dkds = [_dot_nt(pick(6, d, h), dsb) for (d, h), dsb in zip(chains, dsbs)]
            dqgs = [_dot_nt(dov, sb) for dov, sb in zip(dos, sbs)]
            dvns = [(a + b).astype(BF16) for a, b in zip(dv1, dv2)]
            ds2 = [_dot_tn(pick(2, d, h), dvn) for (d, h), dvn in zip(chains, dvns)]
            dws = [_dot_nt(dvn, sb) for dvn, sb in zip(dvns, sbs)]
            new_dss = []
            for ch, (d, h) in enumerate(chains):
                egl = ins[14 + d][at[d], ch:ch + 1, :]
                outs[d][at[d], h] = dvns[ch]
                outs[2 + d][at[d], h] = (-dws[ch]).astype(BF16)
                outs[4 + d][at[d], h] = dqgs[ch]
                outs[6 + d][at[d], h] = dkds[ch]
                outs[8 + d][at[d], h:h + 1, :] = egl * jnp.sum(jnp.sum(ss[ch] * dss[ch], axis=1, keepdims=True),
                                                               axis=0, keepdims=True)
                new_dss.append(ds1[ch] + egl * dss[ch] - ds2[ch])
            dss = new_dss
        for ch in range(N_CHAINS):
            dstate[ch] = dss[ch]

    spec, rows_spec, order = _dir_specs(nc, True)
    pair = lambda rows, cols, own=False: [spec(0, rows, cols, own), spec(1, rows, cols, own)]
    s_f, s_b = saved["s"]
    vn_f, vn_b = saved["vn"]
    w, qg, kd, intra, egl = saved["w"], saved["qg"], saved["kd"], saved["intra"], saved["egl"]
    own = lambda rows, cols, dtype: jax.ShapeDtypeStruct((nc, GDN_HEADS, rows, cols), dtype)
    row_shape = jax.ShapeDtypeStruct((nc, GDN_HEADS, LANES), F32)
    return pl.pallas_call(
        body, name="gdn_scan_bwd", grid=(nc // SCAN_CHUNKS,),
        in_specs=([rows_spec(0, hd), rows_spec(1, hd)] + pair(GDN_DIM, GDN_DIM, True) + pair(CHUNK, GDN_DIM)
                  + pair(CHUNK, GDN_DIM) + pair(CHUNK, GDN_DIM) + pair(CHUNK, CHUNK) + pair(CHUNK, GDN_DIM, True)
                  + pair(None, LANES)),
        out_specs=(pair(CHUNK, GDN_DIM, True) + pair(CHUNK, GDN_DIM, True) + pair(CHUNK, GDN_DIM, True)
                   + pair(CHUNK, GDN_DIM, True) + pair(None, LANES, True)),
        out_shape=[own(CHUNK, GDN_DIM, BF16)] * 4 + [own(CHUNK, GDN_DIM, F32)] * 4 + [row_shape] * 2,
        scratch_shapes=[pltpu.VMEM((N_CHAINS, GDN_DIM, GDN_DIM), F32)],
        compiler_params=_params(("arbitrary",), VMEM_LIMIT),
    )(do, do, s_f, s_b, w, w, qg, qg, kd, kd, intra, intra, vn_f, vn_b, egl, egl)


def _dot3_nt(a, b):
    ah = a.astype(BF16)
    al = (a - ah.astype(F32)).astype(BF16)
    bh = b.astype(BF16)
    bl = (b - bh.astype(F32)).astype(BF16)
    return _dot_nt(ah, bh) + (_dot_nt(ah, bl) + _dot_nt(al, bh))


def _dot3_tn(a, b):
    ah = a.astype(BF16)
    al = (a - ah.astype(F32)).astype(BF16)
    bh = b.astype(BF16)
    bl = (b - bh.astype(F32)).astype(BF16)
    return _dot_tn(ah, bh) + (_dot_tn(ah, bl) + _dot_tn(al, bh))


def _gdn_local_bwd(qkvc, gb, gbt, do, saved, scan, exchange=None):
    t = qkvc.shape[0]
    nc = t // CHUNK
    hd = GDN_HEADS * GDN_DIM

    def body(*refs):
        x_ref, g_ref, gt_ref, do_ref, t_ref = refs[:5]
        per_dir = refs[5:17]
        dx_ref, dg_ref = refs[17:]
        chains = [c for cc in range(LOCAL_CHUNKS) for c in _load_chains(x_ref, g_ref, gt_ref, cc)]
        lane = lax.broadcasted_iota(jnp.int32, (CHUNK, LANES), 1)
        dgates = [jnp.zeros((CHUNK, LANES), F32) for _ in range(LOCAL_CHUNKS)]
        for c in chains:
            d = c["ch"] // GDN_HEADS
            vn_ref, dvn_ref, dw_ref, dqg_ref, dkd_ref, dgl_ref = per_dir[d::2]
            h, cc = c["h"], c["cc"]
            rows = slice(cc * CHUNK, (cc + 1) * CHUNK)
            c.update(tm=t_ref[cc, c["ch"]], dov=do_ref[rows, h * GDN_DIM:(h + 1) * GDN_DIM], vnew=vn_ref[cc, h],
                     dvnew=dvn_ref[cc, h], dw=dw_ref[cc, h], dqg=dqg_ref[cc, h], dkdec=dkd_ref[cc, h],
                     dglast=dgl_ref[cc, h:h + 1, 0:1])
        dintras = [_dot_nt(c["dov"], c["vnew"]) for c in chains]
        dts = [_dot_nt(c["dvnew"], c["vb"]) + _dot_nt(c["dw"], c["kbg"]) for c in chains]
        dvbs = [_dot_tn(c["tm"], c["dvnew"]) for c in chains]
        dkbgs = [_dot_tn(c["tm"], c["dw"]) for c in chains]
        tdts = [_dot3_nt(dt, c["tm"]) for dt, c in zip(dts, chains)]
        dls = [jnp.where(c["masks"][3], -_dot3_tn(c["tm"], tdt), 0.0) for tdt, c in zip(tdts, chains)]
        das = [dl * c["decay"] for dl, c in zip(dls, chains)]
        dqks = [jnp.where(c["masks"][2], di, 0.0) * c["decay"] for di, c in zip(dintras, chains)]
        dkb1 = [_dot(da, c["k"]) for da, c in zip(das, chains)]
        dk1 = [_dot_tn(da, c["kb"]) for da, c in zip(das, chains)]
        dk2 = [_dot_tn(dqk, c["q"]) for dqk, c in zip(dqks, chains)]
        dq1 = [_dot(dqk, c["k"]) for dqk, c in zip(dqks, chains)]
        grads, mms, p_gs, p_betas, p_kds = [], [], [], [], []
        for n, c in enumerate(chains):
            incl = c["masks"][2]
            dkb = dkb1[n] + dkbgs[n] * c["eg"]
            kd = c["dkdec"] * c["kdec"]
            mms.append((dls[n] * c["amat"] + jnp.where(incl, dintras[n], 0.0) * c["qk"]) * c["decay"])
            p_gs.append(c["dqg"] * c["qg"] - kd + dkbgs[n] * c["kbg"])
            p_betas.append(dkb * c["k"] + dvbs[n] * c["v"])
            p_kds.append(kd)
            grads.append((dq1[n] + c["dqg"] * c["eg"],
                          dk1[n] + dk2[n] + c["dkdec"] * c["ek"] + dkb * c["bcol"],
                          dvbs[n] * c["bcol"]))
        row_sums = [jnp.sum(mm, axis=1, keepdims=True) for mm in mms]
        col_sums = [jnp.sum(mm, axis=0, keepdims=True) for mm in mms]
        g_sums = [jnp.sum(pg, axis=1, keepdims=True) for pg in p_gs]
        dbetas = [jnp.sum(pb, axis=1, keepdims=True) for pb in p_betas]
        kd_tots = [jnp.sum(jnp.sum(pk, axis=1, keepdims=True), axis=0, keepdims=True) for pk in p_kds]
        dgcs = [rs - _row_to_col(cs, *c["masks"][0:2]) + gs for rs, cs, gs, c in zip(row_sums, col_sums, g_sums, chains)]
        dgrs = [_col_to_row(dgc, *c["masks"][0:2]) for dgc, c in zip(dgcs, chains)]
        draws = [jnp.sum(jnp.where(jnp.logical_not(c["masks"][3]), dgr, 0.0), axis=1, keepdims=True) + c["dglast"] + kt
                 for dgr, kt, c in zip(dgrs, kd_tots, chains)]
        for c, draw, dbeta in zip(chains, draws, dbetas):
            ch = c["ch"]
            dgates[c["cc"]] = dgates[c["cc"]] + jnp.where(lane == ch, draw, 0.0) + jnp.where(lane == 8 + ch, dbeta, 0.0)
        for cc in range(LOCAL_CHUNKS):
            rows = slice(cc * CHUNK, (cc + 1) * CHUNK)
            for h in range(GDN_HEADS):
                for part in range(3):
                    cols = slice(part * hd + h * GDN_DIM, part * hd + (h + 1) * GDN_DIM)
                    dx_ref[rows, cols] = grads[cc * N_CHAINS + h][part] + grads[cc * N_CHAINS + GDN_HEADS + h][part]
            dg_ref[rows, :] = dgates[cc]

    lc = LOCAL_CHUNKS
    all8 = lambda rows, cols: pl.BlockSpec((lc, N_CHAINS, rows, cols), lambda n: (n, 0, 0, 0))
    own4 = lambda rows, cols: pl.BlockSpec((lc, GDN_HEADS, rows, cols), lambda n: (n, 0, 0, 0))
    row4 = pl.BlockSpec((lc, GDN_HEADS, LANES), lambda n: (n, 0, 0))
    vn_f, vn_b = saved["vn"]
    dvn_f, dvn_b, dw_f, dw_b, dqg_f, dqg_b, dkd_f, dkd_b, dgl_f, dgl_b = scan
    return _grid_call(
        body, "gdn_local_bwd", nc // lc,
        [pl.BlockSpec((lc * CHUNK, 3 * hd), lambda n: (n, 0)), pl.BlockSpec((lc * CHUNK, LANES), lambda n: (n, 0)),
         pl.BlockSpec((lc, 16, CHUNK), lambda n: (n, 0, 0)), pl.BlockSpec((lc * CHUNK, hd), lambda n: (n, 0)),
         all8(CHUNK, CHUNK)] + [own4(CHUNK, GDN_DIM)] * 10 + [row4, row4],
        [pl.BlockSpec((lc * CHUNK, 3 * hd), lambda n: (n, 0)), pl.BlockSpec((lc * CHUNK, LANES), lambda n: (n, 0))],
        [jax.ShapeDtypeStruct((t, 3 * hd), F32), jax.ShapeDtypeStruct((t, LANES), F32)],
        (qkvc, gb, gbt, do, saved["tm"], vn_f, vn_b, dvn_f, dvn_b, dw_f, dw_b, dqg_f, dqg_b, dkd_f, dkd_b, dgl_f, dgl_b),
        exchange=exchange)


def _gdn_post_fwd(of, ob, z, gw, tm):
    t, hd = of.shape

    def body(of_ref, ob_ref, z_ref, w_ref, o_ref):
        for h in range(GDN_HEADS):
            cols = slice(h * GDN_DIM, (h + 1) * GDN_DIM)
            o = of_ref[:, cols] + ob_ref[:, cols]
            zv = z_ref[:, cols]
            o_ref[:, cols] = (o * _rstd(o) * w_ref[...] * (zv * _sigmoid(zv))).astype(BF16)

    row = pl.BlockSpec((tm, hd), lambda i: (i, 0))
    return pl.pallas_call(
        body, name="gdn_post_fwd", grid=(t // tm,),
        in_specs=[row, row, row, _resident((1, GDN_DIM))],
        out_specs=row, out_shape=jax.ShapeDtypeStruct((t, hd), BF16),
        compiler_params=_params(("arbitrary",), VMEM_LIMIT),
    )(of, ob, z, gw)


def _gdn_post_bwd(doa, of, ob, z, gw, tm):
    t, hd = of.shape

    def body(d_ref, of_ref, ob_ref, z_ref, w_ref, do_ref, dz_ref, dw_ref):
        @pl.when(pl.program_id(0) == 0)
        def _():
            dw_ref[...] = jnp.zeros_like(dw_ref)

        dw = jnp.zeros((1, GDN_DIM), F32)
        for h in range(GDN_HEADS):
            cols = slice(h * GDN_DIM, (h + 1) * GDN_DIM)
            o = of_ref[:, cols] + ob_ref[:, cols]
            zv = z_ref[:, cols]
            dv = d_ref[:, cols]
            r = _rstd(o)
            sg = _sigmoid(zv)
            on = o * r * w_ref[...]
            dz_ref[:, cols] = (dv * on * (sg * (1.0 + zv * (1.0 - sg)))).astype(BF16)
            dxr, dwh = _rms_bwd(o, r, w_ref[...], dv * (zv * sg))
            do_ref[:, cols] = dxr
            dw = dw + dwh
        dw_ref[...] += dw

    row = pl.BlockSpec((tm, hd), lambda i: (i, 0))
    return pl.pallas_call(
        body, name="gdn_post_bwd", grid=(t // tm,),
        in_specs=[row, row, row, row, _resident((1, GDN_DIM))],
        out_specs=[row, row, pl.BlockSpec((1, GDN_DIM), lambda i: (0, 0))],
        out_shape=[jax.ShapeDtypeStruct((t, hd), F32), jax.ShapeDtypeStruct((t, hd), BF16),
                   jax.ShapeDtypeStruct((1, GDN_DIM), F32)],
        compiler_params=_params(("arbitrary",), VMEM_LIMIT),
    )(doa, of, ob, z, gw)


SWA_W = SWA_HEADS * SWA_DIM
QBLK = 128
KWIN = QBLK + 2 * RADIUS
WIN_OFFSETS = (0, RADIUS, 2 * RADIUS)


def _t5_bucket(rel):
    nb = REL_BUCKETS // 2
    bucket = (rel > 0).astype(np.int32) * nb
    n = np.abs(rel)
    max_exact = nb // 2
    large = max_exact + (np.log(np.maximum(n, 1) / max_exact)
                         / math.log(REL_MAX_DISTANCE / max_exact) * (nb - max_exact)).astype(np.int32)
    large = np.minimum(large, nb - 1)
    return (bucket + np.where(n < max_exact, n, large)).astype(np.int32)


def _band_tables(dilation):
    a = np.arange(QBLK)
    b = np.arange(KWIN)
    rel = np.stack([b[None, :] - w0 - a[:, None] for w0 in WIN_OFFSETS])
    return np.where(np.abs(rel) <= RADIUS, _t5_bucket(rel * dilation), -1).astype(np.int32)


BAND_CELLS = len(WIN_OFFSETS) * QBLK * KWIN


def _band_index():
    return jnp.asarray(np.concatenate([_band_tables(d).reshape(-1) for _, d in PATTERNS])[None, :])


def _onehot(idx, dtype):
    return (lax.broadcasted_iota(jnp.int32, (REL_BUCKETS, idx.shape[1]), 0) == idx).astype(dtype)


def _bias_tables(rel_bias, idx, tk):
    n = idx.shape[1]

    def body(rb_ref, i_ref, o_ref):
        iv = i_ref[...]
        o_ref[...] = jnp.where(iv < 0, NEG_BIG, _dot_hi(rb_ref[...], _onehot(iv, F32)))

    return pl.pallas_call(
        body, name="bias_tables", grid=(n // tk,),
        in_specs=[_resident((SWA_HEADS, REL_BUCKETS)), pl.BlockSpec((1, tk), lambda k: (0, k))],
        out_specs=pl.BlockSpec((SWA_HEADS, tk), lambda k: (0, k)),
        out_shape=jax.ShapeDtypeStruct((SWA_HEADS, n), F32),
        compiler_params=_params(("arbitrary",), VMEM_LIMIT),
    )(rel_bias.T, idx)


def _head_mean(x2, bd_ref):
    bd = bd_ref[...]
    rest, acc = x2, None
    for _ in range(3):
        piece = rest.astype(BF16)
        part = jnp.dot(piece, bd, preferred_element_type=F32)
        acc = part if acc is None else acc + part
        rest = rest - piece.astype(F32)
    return acc


VIEW_DILATIONS = tuple(d for _, d in PATTERNS if d > 1)


def _view_spec(tm, d):
    return pl.BlockSpec((tm // d, d * SWA_W), lambda i: (i, 0))


def _view_shape(t, d, dtype):
    return jax.ShapeDtypeStruct((t // d, d * SWA_W), dtype)


N_GROUPS = SWA_W // LANES


def _to_view(src_ref, idx, dst_ref, d, rows):
    for r in range(d):
        for g in range(N_GROUPS):
            cols = slice(r * SWA_W + g * LANES, r * SWA_W + (g + 1) * LANES)
            dst_ref[:, cols] = src_ref[idx, g, pl.ds(r, rows // d, stride=d), :].astype(dst_ref.dtype)


def _from_view(src_ref, dst_ref, idx, d, rows):
    for r in range(d):
        for g in range(N_GROUPS):
            cols = slice(r * SWA_W + g * LANES, r * SWA_W + (g + 1) * LANES)
            dst_ref[idx, g, pl.ds(r, rows // d, stride=d), :] = src_ref[:, cols]


def _swa_prep_fwd(qkvb, qw, kw, bd, tm):
    t = qkvb.shape[0]

    def body(x_ref, qw_ref, kw_ref, bd_ref, *rest):
        outs, sc = rest[:-1], rest[-1]
        for gidx in range(N_GROUPS):
            cols = slice(gidx * LANES, (gidx + 1) * LANES)
            xq = x_ref[:, cols]
            sc[0, gidx] = xq * lax.rsqrt(_head_mean(xq * xq, bd_ref) + EPS) * qw_ref[:, cols] * (SWA_DIM ** -0.5)
            xk = x_ref[:, SWA_W + gidx * LANES:SWA_W + (gidx + 1) * LANES]
            sc[1, gidx] = xk * lax.rsqrt(_head_mean(xk * xk, bd_ref) + EPS) * kw_ref[:, cols]
            sc[2, gidx] = x_ref[:, 2 * SWA_W + gidx * LANES:2 * SWA_W + (gidx + 1) * LANES]
            for i in range(3):
                outs[i][:, cols] = sc[i, gidx].astype(BF16)
        for i in range(3):
            for n, d in enumerate(VIEW_DILATIONS):
                _to_view(sc, i, outs[3 * (n + 1) + i], d, tm)

    return pl.pallas_call(
        body, name="swa_prep_fwd", grid=(t // tm,),
        in_specs=[pl.BlockSpec((tm, 3 * SWA_W), lambda i: (i, 0)), _resident((1, SWA_W)), _resident((1, SWA_W)),
                  _resident((LANES, LANES))],
        out_specs=[_view_spec(tm, d) for d in (1,) + VIEW_DILATIONS for _ in range(3)],
        out_shape=[_view_shape(t, d, BF16) for d in (1,) + VIEW_DILATIONS for _ in range(3)],
        scratch_shapes=[pltpu.VMEM((3, N_GROUPS, tm, LANES), F32)],
        compiler_params=_params(("arbitrary",), VMEM_LIMIT),
    )(qkvb, qw, kw, bd)


def _swa_prep_bwd(qkvb, qw, kw, bd, grads, tm):
    t = qkvb.shape[0]

    def body(x_ref, qw_ref, kw_ref, bd_ref, *rest):
        parts, (dx_ref, dqw_ref, dkw_ref, sc) = rest[:9], rest[9:]
        @pl.when(pl.program_id(0) == 0)
        def _():
            dqw_ref[...] = jnp.zeros_like(dqw_ref)
            dkw_ref[...] = jnp.zeros_like(dkw_ref)

        for i in range(3):
            for n, d in enumerate(VIEW_DILATIONS):
                _from_view(parts[3 * (n + 1) + i], sc, 2 * i + n, d, tm)
        for gidx in range(N_GROUPS):
            cols = slice(gidx * LANES, (gidx + 1) * LANES)
            for i, base, w_ref, dw_ref, scale in ((0, 0, qw_ref, dqw_ref, SWA_DIM ** -0.5),
                                                  (1, SWA_W, kw_ref, dkw_ref, 1.0)):
                xv = x_ref[:, base + gidx * LANES:base + (gidx + 1) * LANES]
                dy = (parts[i][:, cols] + sc[2 * i, gidx] + sc[2 * i + 1, gidx]) * scale
                r = lax.rsqrt(_head_mean(xv * xv, bd_ref) + EPS)
                xhat = xv * r
                dxh = dy * w_ref[:, cols]
                dx = r * (dxh - xhat * _head_mean(dxh * xhat, bd_ref))
                dx_ref[:, base + gidx * LANES:base + (gidx + 1) * LANES] = dx.astype(BF16)
                dw_ref[:, cols] += jnp.sum(dy * xhat, axis=0, keepdims=True)
            dx_ref[:, 2 * SWA_W + gidx * LANES:2 * SWA_W + (gidx + 1) * LANES] = (
                parts[2][:, cols] + sc[4, gidx] + sc[5, gidx]).astype(BF16)

    wrow = pl.BlockSpec((1, SWA_W), lambda i: (0, 0))
    return pl.pallas_call(
        body, name="swa_prep_bwd", grid=(t // tm,),
        in_specs=[pl.BlockSpec((tm, 3 * SWA_W), lambda i: (i, 0)), _resident((1, SWA_W)), _resident((1, SWA_W)),
                  _resident((LANES, LANES))] + [_view_spec(tm, d) for d in (1,) + VIEW_DILATIONS for _ in range(3)],
        out_specs=[pl.BlockSpec((tm, 3 * SWA_W), lambda i: (i, 0)), wrow, wrow],
        out_shape=[jax.ShapeDtypeStruct((t, 3 * SWA_W), BF16), jax.ShapeDtypeStruct((1, SWA_W), F32),
                   jax.ShapeDtypeStruct((1, SWA_W), F32)],
        scratch_shapes=[pltpu.VMEM((6, N_GROUPS, tm, LANES), F32)],
        compiler_params=_params(("arbitrary",), VMEM_LIMIT),
    )(qkvb, qw, kw, bd, *grads)


def _aligned(v, m):
    return v if isinstance(v, int) else pl.multiple_of(v, m)


BAND_GROUP = 2


def _band_loop(nsub, length, step, group=BAND_GROUP):
    step([(0, 0)], 0)
    if nsub > 2:
        assert (nsub - 2) % group == 0

        def inner(i, carry):
            s0 = 1 + i * group
            step([(s0 + e, pl.multiple_of((s0 + e) * QBLK - RADIUS, RADIUS)) for e in range(group)], 1)
            return carry
        lax.fori_loop(0, (nsub - 2) // group, inner, 0)
    step([(nsub - 1, length - KWIN)], 2)


def _head_select(lane, a0, a1):
    return jnp.where(lane < SWA_DIM, a0, a1)


def _swa_fwd(qv, kv, vv, bias, dilation, name):
    length = qv.shape[0]
    nsub = length // QBLK
    assert nsub >= 2 and length % QBLK == 0

    def body(q_ref, k_ref, v_ref, b_ref, o_ref, l_ref):
        lane = lax.broadcasted_iota(jnp.int32, (QBLK, LANES), 1)

        def step(blocks, var):
            items = []
            for s, ws in blocks:
                rows = pl.ds(_aligned(s * QBLK, QBLK), QBLK)
                q, kk, vw = q_ref[rows, :], k_ref[pl.ds(ws, KWIN), :], v_ref[pl.ds(ws, KWIN), :]
                for hh in range(2):
                    items.append((hh, jnp.where((lane < SWA_DIM) == (hh == 0), q, jnp.zeros_like(q)), kk, vw))
            lgs = [_dot_nt(qh, kk) + b_ref[hh, var] for hh, qh, kk, _ in items]
            ms = [jnp.max(lg, axis=-1, keepdims=True) for lg in lgs]
            ps = [jnp.exp(lg - m) for lg, m in zip(lgs, ms)]
            dens = [jnp.sum(p, axis=-1, keepdims=True) for p in ps]
            pvs = [_dot(p, it[3]) for p, it in zip(ps, items)]
            for n, (s, _) in enumerate(blocks):
                rows = pl.ds(_aligned(s * QBLK, QBLK), QBLK)
                o0, o1 = (pvs[2 * n + hh] / dens[2 * n + hh] for hh in range(2))
                l0, l1 = (ms[2 * n + hh] + jnp.log(dens[2 * n + hh]) for hh in range(2))
                o_ref[rows, :] = _head_select(lane, o0, o1)
                l_ref[rows, :] = _head_select(lane, l0, l1)

        _band_loop(nsub, length, step)

    blk = pl.BlockSpec((length, LANES), lambda hp, r: (0, r * (SWA_W // LANES) + hp))
    shp = jax.ShapeDtypeStruct(qv.shape, F32)
    return pl.pallas_call(
        body, name=name, grid=(SWA_W // LANES, dilation),
        in_specs=[blk, blk, blk, pl.BlockSpec((2, 3, QBLK, KWIN), lambda hp, r: (hp, 0, 0, 0))],
        out_specs=[blk, blk], out_shape=[shp, shp],
        compiler_params=_params(("arbitrary", "arbitrary"), VMEM_LIMIT),
    )(qv, kv, vv, bias)


def _swa_combine(os_, ls_, tm):
    t = os_[0].shape[0]

    def body(o0, o1, o2, l0, l1, l2, o_ref, ob_ref, la_ref, lb_ref, lc_ref, sc):
        for n, d in enumerate(VIEW_DILATIONS):
            _from_view((o1, o2)[n], sc, n, d, tm)
            _from_view((l1, l2)[n], sc, 2 + n, d, tm)
        for g in range(N_GROUPS):
            cols = slice(g * LANES, (g + 1) * LANES)
            la, lb, lc = l0[:, cols], sc[2, g], sc[3, g]
            m = jnp.maximum(jnp.maximum(la, lb), lc)
            tot = m + jnp.log(jnp.exp(la - m) + jnp.exp(lb - m) + jnp.exp(lc - m))
            o = jnp.exp(la - tot) * o0[:, cols] + jnp.exp(lb - tot) * sc[0, g] + jnp.exp(lc - tot) * sc[1, g]
            o_ref[:, cols] = o
            ob_ref[:, cols] = o.astype(BF16)
            la_ref[:, cols] = tot
            sc[4, g] = tot
        for n, d in enumerate(VIEW_DILATIONS):
            _to_view(sc, 4, (lb_ref, lc_ref)[n], d, tm)

    specs = [_view_spec(tm, d) for d in (1,) + VIEW_DILATIONS]
    return pl.pallas_call(
        body, name="swa_combine", grid=(t // tm,), in_specs=specs + specs, out_specs=[specs[0], specs[0]] + specs,
        out_shape=[jax.ShapeDtypeStruct((t, SWA_W), F32), jax.ShapeDtypeStruct((t, SWA_W), BF16)]
                  + [_view_shape(t, d, F32) for d in (1,) + VIEW_DILATIONS],
        scratch_shapes=[pltpu.VMEM((5, N_GROUPS, tm, LANES), F32)],
        compiler_params=_params(("arbitrary",), VMEM_LIMIT),
    )(*os_, *ls_)


def _swa_bwd_prep(do, o, bd, tm):
    t = do.shape[0]

    def body(d_ref, o_ref, bd_ref, dd1, dd4, dd16, db1, db4, db16, sc):
        for gidx in range(N_GROUPS):
            cols = slice(gidx * LANES, (gidx + 1) * LANES)
            dv = d_ref[:, cols]
            dd = _head_mean(dv * o_ref[:, cols], bd_ref) * float(SWA_DIM)
            sc[0, gidx] = dd
            sc[1, gidx] = dv
            dd1[:, cols] = dd
            db1[:, cols] = dv.astype(BF16)
        for n, d in enumerate(VIEW_DILATIONS):
            _to_view(sc, 0, (dd4, dd16)[n], d, tm)
            _to_view(sc, 1, (db4, db16)[n], d, tm)

    specs = [_view_spec(tm, d) for d in (1,) + VIEW_DILATIONS]
    return pl.pallas_call(
        body, name="swa_bwd_prep", grid=(t // tm,), in_specs=[specs[0], specs[0], _resident((LANES, LANES))],
        out_specs=specs + specs,
        out_shape=[_view_shape(t, d, F32) for d in (1,) + VIEW_DILATIONS]
                  + [_view_shape(t, d, BF16) for d in (1,) + VIEW_DILATIONS],
        scratch_shapes=[pltpu.VMEM((2, N_GROUPS, tm, LANES), F32)],
        compiler_params=_params(("arbitrary",), VMEM_LIMIT),
    )(do, o, bd)


def _swa_bwd(qv, kv, vv, dov, lv, ddv, bias_a, dilation, name):
    length = qv.shape[0]
    nsub = length // QBLK
    single = pl.Buffered(1) if dilation == 1 else None

    def body(q_ref, k_ref, v_ref, do_ref, l_ref, dd_ref, ba_ref, dq_ref, dk_ref, dv_ref, db_ref):
        @pl.when(pl.program_id(1) == 0)
        def _():
            db_ref[...] = jnp.zeros_like(db_ref)

        lane = lax.broadcasted_iota(jnp.int32, (QBLK, LANES), 1)
        lanew = lax.broadcasted_iota(jnp.int32, (KWIN, LANES), 1)

        def step(blocks, var):
            items = []
            for s, ws in blocks:
                rows = pl.ds(_aligned(s * QBLK, QBLK), QBLK)
                win = pl.ds(ws, KWIN)
                q, dov_ = q_ref[rows, :], do_ref[rows, :]
                kk, vw = k_ref[win, :], v_ref[win, :]
                lse, dd = l_ref[rows, :], dd_ref[rows, :]
                for hh in range(2):
                    mine = (lane < SWA_DIM) == (hh == 0)
                    col = slice(hh * SWA_DIM, hh * SWA_DIM + 1)
                    items.append((hh, jnp.where(mine, q, jnp.zeros_like(q)), jnp.where(mine, dov_, jnp.zeros_like(dov_)),
                                  kk, vw, lse[:, col], dd[:, col], q, dov_))
            lgs = [_dot_nt(it[1], it[3]) + ba_ref[it[0], var] for it in items]
            dps = [_dot_nt(it[2], it[4]) for it in items]
            ps = [jnp.exp(lg - it[5]) for lg, it in zip(lgs, items)]
            dss = [p * (dp - it[6]) for p, dp, it in zip(ps, dps, items)]
            dqs = [_dot(ds, it[3]) for ds, it in zip(dss, items)]
            dks = [_dot_tn(ds, it[7]) for ds, it in zip(dss, items)]
            dvs = [_dot_tn(p, it[8]) for p, it in zip(ps, items)]
            for n, (s, ws) in enumerate(blocks):
                rows = pl.ds(_aligned(s * QBLK, QBLK), QBLK)
                win = pl.ds(ws, KWIN)
                dq_ref[rows, :] = _head_select(lane, dqs[2 * n], dqs[2 * n + 1])
                dk_ref[win, :] += _head_select(lanew, dks[2 * n], dks[2 * n + 1])
                dv_ref[win, :] += _head_select(lanew, dvs[2 * n], dvs[2 * n + 1])
            for hh in range(2):
                tot = dss[hh]
                for n in range(1, len(blocks)):
                    tot = tot + dss[2 * n + hh]
                db_ref[hh, var] += tot

        dk_ref[...] = jnp.zeros_like(dk_ref)
        dv_ref[...] = jnp.zeros_like(dv_ref)
        _band_loop(nsub, length, step)

    imap = lambda hp, r: (0, r * (SWA_W // LANES) + hp)
    blk_in = pl.BlockSpec((length, LANES), imap, pipeline_mode=single)
    blk_out = pl.BlockSpec((length, LANES), imap)
    shp = jax.ShapeDtypeStruct(qv.shape, F32)
    return pl.pallas_call(
        body, name=name, grid=(SWA_W // LANES, dilation),
        in_specs=[blk_in] * 6 + [pl.BlockSpec((2, 3, QBLK, KWIN), lambda hp, r: (hp, 0, 0, 0))],
        out_specs=[blk_out, blk_out, blk_out, pl.BlockSpec((2, 3, QBLK, KWIN), lambda hp, r: (hp, 0, 0, 0))],
        out_shape=[shp, shp, shp, jax.ShapeDtypeStruct((SWA_HEADS, 3, QBLK, KWIN), F32)],
        compiler_params=_params(("arbitrary", "arbitrary"), VMEM_LIMIT),
    )(qv, kv, vv, dov, lv, ddv, bias_a)


def _bias_grad(ds2, idx, tk):
    n = ds2.shape[1]
    nk = n // tk

    def body(a_ref, i_ref, o_ref):
        @pl.when(pl.program_id(0) == 0)
        def _():
            o_ref[...] = jnp.zeros_like(o_ref)

        oh = _onehot(i_ref[...], BF16)
        rest = a_ref[...]
        acc = jnp.zeros((SWA_HEADS, REL_BUCKETS), F32)
        for _ in range(3):
            piece = rest.astype(BF16)
            acc = acc + _dot_nt(piece, oh)
            rest = rest - piece.astype(F32)
        o_ref[...] += acc

    return pl.pallas_call(
        body, name="bias_grad", grid=(nk,),
        in_specs=[pl.BlockSpec((SWA_HEADS, tk), lambda k: (0, k)), pl.BlockSpec((1, tk), lambda k: (0, k))],
        out_specs=pl.BlockSpec((SWA_HEADS, REL_BUCKETS), lambda k: (0, 0)),
        out_shape=jax.ShapeDtypeStruct((SWA_HEADS, REL_BUCKETS), F32),
        compiler_params=_params(("arbitrary",), VMEM_LIMIT),
    )(ds2, idx)


def _swa_branch_fwd(qkvb, qw_t, kw_t, rel_bias, bd, tm):
    qkv = _swa_prep_fwd(qkvb, qw_t, kw_t, bd, tm)
    tables = _bias_tables(rel_bias, _band_index(), 8192)
    os_, ls_, tabs = [], [], []
    for n, (_, d) in enumerate(PATTERNS):
        bias = tables[:, n * BAND_CELLS:(n + 1) * BAND_CELLS].reshape(SWA_HEADS, len(WIN_OFFSETS), QBLK, KWIN)
        o_p, l_p = _swa_fwd(*qkv[3 * n:3 * n + 3], bias, d, f"swa_fwd_d{d}")
        os_.append(o_p)
        ls_.append(l_p)
        tabs.append(bias)
    o, o16, *lses = _swa_combine(os_, ls_, tm)
    return o, o16, (qkv, lses, tabs)


def _swa_branch_bwd(do, o, saved, qkvb, qw_t, kw_t, bd, tm):
    qkv, lses, tabs = saved
    prep = _swa_bwd_prep(do, o, bd, tm)
    grads, dss = [], []
    for n, ((_, d), bias) in enumerate(zip(PATTERNS, tabs)):
        dq, dk, dv, ds = _swa_bwd(*qkv[3 * n:3 * n + 3], prep[3 + n], lses[n], prep[n], bias, d, f"swa_bwd_d{d}")
        grads += [dq, dk, dv]
        dss.append(ds.reshape(SWA_HEADS, -1))
    dqkvb, dqw, dkw = _swa_prep_bwd(qkvb, qw_t, kw_t, bd, grads, tm)
    dbias = _bias_grad(jnp.concatenate(dss, axis=1), _band_index(), 8192)
    fold = lambda w: jnp.sum(w.reshape(SWA_HEADS, SWA_DIM), axis=0)
    return dqkvb, fold(dqw), fold(dkw), dbias.T


def _mesh_pos():
    return lax.axis_index("x"), lax.axis_index("y"), lax.axis_index("c")


def _other_chips(x, y):
    return [(1 - x, y), (x, 1 - y), (1 - x, 1 - y)]


def _remote(src, dst, send_sem, recv_sem, device):
    return pltpu.make_async_remote_copy(src_ref=src, dst_ref=dst, send_sem=send_sem, recv_sem=recv_sem,
                                        device_id=device, device_id_type=MESH)


def _split_axis(shape2):
    return 0 if (shape2[0] // 2) % 16 == 0 else 1


def _half_index(shape2, c):
    axis = _split_axis(shape2)
    h = shape2[axis] // 2
    return (pl.ds(c * h, h), slice(None)) if axis == 0 else (slice(None), pl.ds(c * h, h))


def _all_gather(xs):
    n = len(xs)

    def body(*refs):
        ins, outs = refs[:n], refs[n:2 * n]
        send_sems, recv_sems = refs[2 * n:]
        x, y, c = _mesh_pos()
        me = 2 * x + y
        chips = _other_chips(x, y)
        halves = []
        sends = []
        for a in range(n):
            h = ins[a].shape[0] // 2
            mine, other = pl.ds(c * h, h), pl.ds((1 - c) * h, h)
            halves.append((mine, other))
            own = _remote(ins[a], outs[a].at[me], send_sems.at[a, 6], recv_sems.at[a, 6], (x, y, 1 - c))
            own.start()
            sends.append(own)
            for j, chip in enumerate(chips):
                cp = _remote(ins[a].at[mine], outs[a].at[me, mine], send_sems.at[a, j], recv_sems.at[a, j], (*chip, c))
                cp.start()
                sends.append(cp)
        for a in range(n):
            mine, _ = halves[a]
            for j, chip in enumerate(chips):
                src = 2 * chip[0] + chip[1]
                landed = outs[a].at[src, mine]
                _remote(landed, landed, send_sems.at[a, j], recv_sems.at[a, j], (x, y, c)).wait_recv()
                fwd = _remote(landed, landed, send_sems.at[a, 3 + j], recv_sems.at[a, 3 + j], (x, y, 1 - c))
                fwd.start()
                sends.append(fwd)
        for a in range(n):
            _, other = halves[a]
            for j, chip in enumerate(chips):
                src = 2 * chip[0] + chip[1]
                landed = outs[a].at[src, other]
                _remote(landed, landed, send_sems.at[a, 3 + j], recv_sems.at[a, 3 + j], (x, y, c)).wait_recv()
            mine_slot = outs[a].at[me]
            _remote(mine_slot, mine_slot, send_sems.at[a, 6], recv_sems.at[a, 6], (x, y, c)).wait_recv()
        for cp in sends:
            cp.wait_send()

    return list(pl.pallas_call(
        body, name="all_gather_weights",
        in_specs=[ANY] * n, out_specs=[ANY] * n,
        out_shape=[jax.ShapeDtypeStruct((N_SHARDS,) + a.shape, a.dtype) for a in xs],
        scratch_shapes=[pltpu.SemaphoreType.DMA((n, 7)), pltpu.SemaphoreType.DMA((n, 7))],
    )(*xs))


def _rs_pair(gs):
    n = len(gs)

    def body(*refs):
        ins, lands = refs[:n], refs[n:2 * n]
        send_sems, recv_sems = refs[2 * n:]
        x, y, c = _mesh_pos()
        cps = []
        for a in range(n):
            theirs = (slice(None),) + _half_index(ins[a].shape[1:], 1 - c)
            cp = _remote(ins[a].at[theirs], lands[a], send_sems.at[a], recv_sems.at[a], (x, y, 1 - c))
            cp.start()
            cps.append(cp)
        for cp in cps:
            cp.wait()

    def half_shape(g):
        dims = list(g.shape)
        dims[1 + _split_axis(g.shape[1:])] //= 2
        return tuple(dims)

    return list(pl.pallas_call(
        body, name="rs_pair", in_specs=[ANY] * n, out_specs=[ANY] * n,
        out_shape=[jax.ShapeDtypeStruct(half_shape(g), g.dtype) for g in gs],
        scratch_shapes=[pltpu.SemaphoreType.DMA((n,)), pltpu.SemaphoreType.DMA((n,))],
    )(*gs))


def _rs_chips(ss):
    n = len(ss)

    def body(*refs):
        ins, outs = refs[:n], refs[n:2 * n]
        send_sems, recv_sems = refs[2 * n:]
        x, y, c = _mesh_pos()
        me = 2 * x + y
        chips = _other_chips(x, y)
        cps = []
        for a in range(n):
            for j, chip in enumerate(chips):
                dst_chip = 2 * chip[0] + chip[1]
                cp = _remote(ins[a].at[dst_chip], outs[a].at[me], send_sems.at[a, j], recv_sems.at[a, j], (*chip, c))
                cp.start()
                cps.append(cp)
        for a in range(n):
            for j, chip in enumerate(chips):
                src = 2 * chip[0] + chip[1]
                _remote(outs[a].at[src], outs[a].at[src], send_sems.at[a, j], recv_sems.at[a, j], (x, y, c)).wait_recv()
        for cp in cps:
            cp.wait_send()

    return list(pl.pallas_call(
        body, name="rs_chips", in_specs=[ANY] * n, out_specs=[ANY] * n,
        out_shape=[jax.ShapeDtypeStruct(s.shape, s.dtype) for s in ss],
        scratch_shapes=[pltpu.SemaphoreType.DMA((n, 3)), pltpu.SemaphoreType.DMA((n, 3))],
    )(*ss))


def _rs_join(fs, axes):
    n = len(fs)

    def whole(f, axis):
        dims = list(f.shape)
        dims[axis] *= 2
        return tuple(dims)

    def body(*refs):
        ins, outs = refs[:n], refs[n:2 * n]
        send_sems, recv_sems = refs[2 * n:]
        x, y, c = _mesh_pos()
        cps = []
        for a in range(n):
            h = ins[a].shape[axes[a]]
            mine = (pl.ds(c * h, h), slice(None)) if axes[a] == 0 else (slice(None), pl.ds(c * h, h))
            cp = _remote(ins[a], outs[a].at[mine], send_sems.at[a], recv_sems.at[a], (x, y, 1 - c))
            cp.start()
            cps.append(cp)
        for cp in cps:
            cp.wait()

    outs = pl.pallas_call(
        body, name="rs_join", in_specs=[ANY] * n, out_specs=[ANY] * n,
        out_shape=[jax.ShapeDtypeStruct(whole(f, ax), f.dtype) for f, ax in zip(fs, axes)],
        scratch_shapes=[pltpu.SemaphoreType.DMA((n,)), pltpu.SemaphoreType.DMA((n,))],
    )(*fs)
    c = lax.axis_index("c")
    return [lax.dynamic_update_slice_in_dim(o, f, c * f.shape[ax], ax) for o, f, ax in zip(outs, fs, axes)]


def _gather_exchange(xs):
    def start(cin, cout, send_sems, recv_sems):
        x, y, c = _mesh_pos()
        me = 2 * x + y
        for a, (src, dst) in enumerate(zip(cin, cout)):
            mine = _half_index(src.shape, c)
            for j, chip in enumerate(_other_chips(x, y)):
                _remote(src.at[mine], dst.at[(me,) + mine], send_sems.at[a, j], recv_sems.at[a, j], (*chip, c)).start()
            _remote(src, dst.at[me], send_sems.at[a, 3], recv_sems.at[a, 3], (x, y, 1 - c)).start()

    def finish(cin, cout, send_sems, recv_sems):
        x, y, c = _mesh_pos()
        for a, dst in enumerate(cout):
            for j, chip in enumerate(_other_chips(x, y)):
                landed = dst.at[(2 * chip[0] + chip[1],) + _half_index(dst.shape[1:], c)]
                _remote(landed, landed, send_sems.at[a, j], recv_sems.at[a, j], (x, y, c)).wait()
            own = dst.at[2 * x + y]
            _remote(own, own, send_sems.at[a, 3], recv_sems.at[a, 3], (x, y, c)).wait()

    return _Exchange(tuple(xs), tuple(jax.ShapeDtypeStruct((N_SHARDS,) + a.shape, a.dtype) for a in xs), start, finish)


def _gather_forward(gs):
    n = len(gs)

    def body(*refs):
        outs = refs[n:2 * n]
        send_sems, recv_sems = refs[2 * n:]
        x, y, c = _mesh_pos()
        chips = _other_chips(x, y)
        cps = []
        for a in range(n):
            for j, chip in enumerate(chips):
                landed = outs[a].at[(2 * chip[0] + chip[1],) + _half_index(outs[a].shape[1:], c)]
                cp = _remote(landed, landed, send_sems.at[a, j], recv_sems.at[a, j], (x, y, 1 - c))
                cp.start()
                cps.append(cp)
        for a in range(n):
            for j, chip in enumerate(chips):
                other = outs[a].at[(2 * chip[0] + chip[1],) + _half_index(outs[a].shape[1:], 1 - c)]
                _remote(other, other, send_sems.at[a, j], recv_sems.at[a, j], (x, y, c)).wait_recv()
        for cp in cps:
            cp.wait_send()

    return list(pl.pallas_call(
        body, name="gather_forward", in_specs=[ANY] * n, out_specs=[ANY] * n,
        out_shape=[jax.ShapeDtypeStruct(g.shape, g.dtype) for g in gs],
        input_output_aliases={i: i for i in range(n)},
        scratch_shapes=[pltpu.SemaphoreType.DMA((n, 3)), pltpu.SemaphoreType.DMA((n, 3))],
    )(*gs))


def _scatter_exchange(ss):
    def start(cin, cout, send_sems, recv_sems):
        x, y, c = _mesh_pos()
        me = 2 * x + y
        for a, (src, dst) in enumerate(zip(cin, cout)):
            for j, chip in enumerate(_other_chips(x, y)):
                _remote(src.at[2 * chip[0] + chip[1]], dst.at[me], send_sems.at[a, j], recv_sems.at[a, j],
                        (*chip, c)).start()

    def finish(cin, cout, send_sems, recv_sems):
        x, y, c = _mesh_pos()
        for a, dst in enumerate(cout):
            for j, chip in enumerate(_other_chips(x, y)):
                slot = dst.at[2 * chip[0] + chip[1]]
                _remote(slot, slot, send_sems.at[a, j], recv_sems.at[a, j], (x, y, c)).wait()

    return _Exchange(tuple(ss), tuple(jax.ShapeDtypeStruct(s.shape, s.dtype) for s in ss), start, finish)


def _add_pairs(gs, lands, name):
    n = len(gs)

    def body(*refs):
        c = lax.axis_index("c")
        for g_ref, l_ref, o_ref in zip(refs[:n], refs[n:2 * n], refs[2 * n:]):
            mine = g_ref[(0,) + _half_index(g_ref.shape[1:], c)]
            o_ref[0] = (mine.astype(F32) + l_ref[0].astype(F32)).astype(BF16)

    whole = [pl.BlockSpec((1,) + g.shape[1:], lambda j: (j, 0, 0)) for g in gs]
    half = [pl.BlockSpec((1,) + l.shape[1:], lambda j: (j, 0, 0)) for l in lands]
    return list(pl.pallas_call(body, name=name, grid=(gs[0].shape[0],), in_specs=whole + half, out_specs=half,
                               out_shape=[jax.ShapeDtypeStruct(l.shape, BF16) for l in lands],
                               compiler_params=_params(("arbitrary",), VMEM_LIMIT))(*gs, *lands))


def _sum_slots(slots, owns, name):
    n = len(slots)

    def body(*refs):
        me = 2 * lax.axis_index("x") + lax.axis_index("y")
        for s_ref, o_ref, out_ref in zip(refs[:n], refs[n:2 * n], refs[2 * n:]):
            acc = jnp.zeros(out_ref.shape, F32)
            for s in range(N_SHARDS):
                acc = acc + jnp.where(me == s, o_ref[s], s_ref[s]).astype(F32)
            out_ref[...] = acc

    def specs(a):
        _, h, c = a.shape
        if h % 32 == 0:
            return (pl.BlockSpec((N_SHARDS, h // 2, c), lambda i: (0, i, 0)), pl.BlockSpec((h // 2, c), lambda i: (i, 0)))
        return (pl.BlockSpec((N_SHARDS, h, c // 2), lambda i: (0, 0, i)), pl.BlockSpec((h, c // 2), lambda i: (0, i)))

    in_specs = [specs(a)[0] for a in slots]
    return list(pl.pallas_call(body, name=name, grid=(2,), in_specs=in_specs + in_specs,
                               out_specs=[specs(a)[1] for a in slots],
                               out_shape=[jax.ShapeDtypeStruct(a.shape[1:], F32) for a in slots],
                               compiler_params=_params(("arbitrary",), VMEM_LIMIT))(*slots, *owns))


def _all_reduce_small(p):
    r = p.shape[0]

    def body(p_ref, o_ref, buf, send_sems, recv_sems):
        x, y, c = _mesh_pos()
        me = 4 * x + 2 * y + c
        buf[me] = p_ref[...]
        cps = []
        k = 0
        for fx in range(2):
            for fy in range(2):
                for fc in range(2):
                    if fx + fy + fc == 0:
                        continue
                    peer = (1 - x if fx else x, 1 - y if fy else y, 1 - c if fc else c)
                    peer_id = 4 * peer[0] + 2 * peer[1] + peer[2]
                    cp = _remote(p_ref, buf.at[me], send_sems.at[k], recv_sems.at[k], peer)
                    cp.start()
                    cps.append((cp, peer_id, k))
                    k += 1
        for cp, peer_id, k in cps:
            _remote(p_ref, buf.at[peer_id], send_sems.at[k], recv_sems.at[k], (x, y, c)).wait_recv()
        for cp, _, _ in cps:
            cp.wait_send()
        acc = buf[0]
        for s in range(1, 8):
            acc = acc + buf[s]
        o_ref[...] = acc

    vm = pl.BlockSpec(memory_space=pltpu.VMEM)
    return pl.pallas_call(
        body, name="all_reduce_small", in_specs=[vm], out_specs=vm,
        out_shape=jax.ShapeDtypeStruct(p.shape, F32),
        scratch_shapes=[pltpu.VMEM((8, r, LANES), F32), pltpu.SemaphoreType.DMA((7,)), pltpu.SemaphoreType.DMA((7,))],
    )(p)


def _adamw(w, g, m, v, name):
    r, c = w.shape
    row_tiles = [d for d in range(8, min(r, 256) + 1, 8) if r % d == 0]
    tr, tc = (max(row_tiles), c) if row_tiles else (r, 256 if c % 256 == 0 else c)
    c1 = 1.0 / (1.0 - ADAM_B1 ** ADAM_STEP)
    c2 = 1.0 / (1.0 - ADAM_B2 ** ADAM_STEP)

    def body(w_ref, g_ref, m_ref, v_ref, d_ref, nm_ref, nv_ref):
        gv = g_ref[...]
        nm = ADAM_B1 * m_ref[...] + (1.0 - ADAM_B1) * gv
        nv = ADAM_B2 * v_ref[...] + (1.0 - ADAM_B2) * (gv * gv)
        d_ref[...] = -ADAM_LR * ((nm * c1) / (jnp.sqrt(nv * c2) + ADAM_EPS) + ADAM_WD * w_ref[...])
        nm_ref[...] = nm
        nv_ref[...] = nv

    blk = pl.BlockSpec((tr, tc), lambda i, j: (i, j))
    shp = jax.ShapeDtypeStruct((r, c), F32)
    return pl.pallas_call(body, name=name, grid=(r // tr, c // tc), in_specs=[blk] * 4, out_specs=[blk] * 3,
                          out_shape=[shp, shp, shp],
                          compiler_params=_params(("arbitrary", "arbitrary"), VMEM_LIMIT))(w, g, m, v)


PACK_UNIT = 8 * LANES


def _pack(arrs):
    parts = []
    for a in arrs:
        f = a.reshape(-1).astype(F32)
        parts.append(jnp.pad(f, (0, (-f.shape[0]) % PACK_UNIT)).reshape(-1, LANES))
    return jnp.concatenate(parts, axis=0)


def _unpack(m, shapes):
    outs, row = [], 0
    for s in shapes:
        n = int(np.prod(s))
        rows = -(-n // PACK_UNIT) * 8
        outs.append(m[row:row + rows].reshape(-1)[:n].reshape(s))
        row += rows
    return outs


WEIGHTS = ["ffn1_norm", "ffn1_w_gate", "ffn1_w_up", "ffn1_w_down", "mix_norm", "w_in", "conv_w", "a_log", "dt_bias",
           "gdn_norm_w", "q_norm_w", "k_norm_w", "rel_bias", "w_out", "ffn2_norm", "ffn2_w_gate", "ffn2_w_up",
           "ffn2_w_down", "final_norm"]
BIG = ["ffn1_w_gate", "ffn1_w_up", "ffn1_w_down", "w_in", "w_out", "ffn2_w_gate", "ffn2_w_up", "ffn2_w_down"]
SMALL = [n for n in WEIGHTS if n not in BIG]
COL_SHARDED = ["ffn1_w_gate", "ffn1_w_up", "w_in", "ffn2_w_gate", "ffn2_w_up"]
N_IN_COLS = 3600
TM = 256
TE = 512
TK = 2048


def kernel(x, ffn1_norm, ffn1_w_gate, ffn1_w_up, ffn1_w_down, mix_norm, w_in, conv_w, a_log, dt_bias, gdn_norm_w, q_norm_w, k_norm_w, rel_bias, w_out, ffn2_norm, ffn2_w_gate, ffn2_w_up, ffn2_w_down, final_norm, loss_target, m_ffn1_norm, m_ffn1_w_gate, m_ffn1_w_up, m_ffn1_w_down, m_mix_norm, m_w_in, m_conv_w, m_a_log, m_dt_bias, m_gdn_norm_w, m_q_norm_w, m_k_norm_w, m_rel_bias, m_w_out, m_ffn2_norm, m_ffn2_w_gate, m_ffn2_w_up, m_ffn2_w_down, m_final_norm, v_ffn1_norm, v_ffn1_w_gate, v_ffn1_w_up, v_ffn1_w_down, v_mix_norm, v_w_in, v_conv_w, v_a_log, v_dt_bias, v_gdn_norm_w, v_q_norm_w, v_k_norm_w, v_rel_bias, v_w_out, v_ffn2_norm, v_ffn2_w_gate, v_ffn2_w_up, v_ffn2_w_down, v_final_norm):
    p = dict(locals())
    xs, target = x[0], loss_target[0]
    t, d = xs.shape
    nc = t // CHUNK
    tk = min(TK, t)
    me = 2 * lax.axis_index("x") + lax.axis_index("y")

    first = ["ffn1_w_gate", "ffn1_w_up", "ffn1_w_down"]
    later = [n for n in BIG if n not in first] + ["conv_w"]
    local = lambda n, a: a[0].T if n in COL_SHARDED else a[0]
    shards = {n: local(n, p[n]).astype(BF16) for n in BIG}
    shards["conv_w"] = conv_w[0]
    gw = dict(zip(first, _all_gather([shards[n] for n in first])))
    f1 = (gw["ffn1_w_gate"], gw["ffn1_w_up"], gw["ffn1_w_down"])
    (x1, xn1, g1, u1), landed = _ffn_fwd(xs, ffn1_norm, *f1, TM, "ffn1_fwd",
                                         exchange=_gather_exchange([shards[n] for n in later]))
    gw.update(zip(later, _gather_forward(landed)))
    w_in_t = gw["w_in"].reshape(N_IN_COLS, d)
    gates = slice(P_AB[0], P_AB[0] + N_GATE_COLS)
    wp = jnp.concatenate([w_in_t[:gates.start], jnp.pad(w_in_t[gates], ((0, LANES - N_GATE_COLS), (0, 0))),
                          w_in_t[gates.stop:]], axis=0)
    w_out_full = gw["w_out"].reshape(d, d)
    conv_rows = conv_w.shape[1]
    cw = jnp.pad(gw["conv_w"].reshape(N_SHARDS * conv_rows, CONV_TAPS).T, ((0, 8 - CONV_TAPS), (0, 0)))
    gp = jnp.pad(jnp.stack([a_log.reshape(8), dt_bias.reshape(8)]), ((0, 6), (0, LANES - 8)))
    gdn_w = gdn_norm_w.reshape(1, GDN_DIM)
    qw_t = jnp.tile(q_norm_w.reshape(1, SWA_DIM), (1, SWA_HEADS))
    kw_t = jnp.tile(k_norm_w.reshape(1, SWA_DIM), (1, SWA_HEADS))
    bd = jnp.asarray(np.kron(np.eye(2), np.full((SWA_DIM, SWA_DIM), 1.0 / SWA_DIM)), BF16)
    f2 = (gw["ffn2_w_gate"], gw["ffn2_w_up"], gw["ffn2_w_down"])

    hn, qkva, z, ab, qkvb = _mix_in_fwd(x1, mix_norm, wp, TE)
    qkvc, gb = _gdn_prep_fwd(qkva, cw, ab, gp, TM)
    gbt = jnp.transpose(gb[:, :16].reshape(nc, CHUNK, 16), (0, 2, 1))
    o_f, o_b, gdn_saved = _gdn_fwd(qkvc, gb, gbt)
    oa = _gdn_post_fwd(o_f, o_b, z, gdn_w, TE)
    o_swa, o_swa16, swa_saved = _swa_branch_fwd(qkvb, qw_t, kw_t, rel_bias, bd, TE)
    x2 = _mix_out_fwd(x1, oa, o_swa, w_out_full, TE)
    (dx3, xn2, g2, u2, loss_part, d_final), _ = _ffn_fwd(x2, ffn2_norm, *f2, TM, "ffn2_fwd", head=(final_norm, target))

    def pair_sums(partials, tag):
        return _add_pairs(partials, _rs_pair(partials), f"rs_add_{tag}")

    (dx2, dyh2, dg2, du2, h2, d_nw2), _ = _ffn_bwd_dx(dx3, x2, ffn2_norm, g2, u2, *f2, TM, "ffn2_bwd_dx")
    dwg2 = _matmul_tn(dg2, xn2, tk, "ffn2_dwg")
    dwu2 = _matmul_tn(du2, xn2, tk, "ffn2_dwu")
    dwd2 = _matmul_tn(h2, dyh2, tk, "ffn2_dwd")
    sums_f2 = pair_sums([dwg2, dwu2, dwd2], "a")
    doa, dob, dx2b = _mix_out_bwd(dx2, w_out_full, TE)
    dwo = jnp.concatenate([_matmul_tn(oa, dx2b, tk, "w_out_dw_a")[0], _matmul_tn(o_swa16, dx2b, tk, "w_out_dw_b")[0]],
                          axis=0).reshape(N_SHARDS, d // N_SHARDS, d)
    do_g, dz, d_gdnw = _gdn_post_bwd(doa, o_f, o_b, z, gdn_w, TE)
    (dqkvc, dgates), slots_f2 = _gdn_bwd(qkvc, gb, gbt, do_g, gdn_saved, exchange=_scatter_exchange(sums_f2))
    dqkva, dab, dcw, dgp = _gdn_prep_bwd(qkva, cw, ab, gp, dqkvc, dgates, TM)
    dqkvb, d_qw, d_kw, d_rel = _swa_branch_bwd(dob, o_swa, swa_saved, qkvb, qw_t, kw_t, bd, TE)
    dpieces = (dqkva, dz, dab, dqkvb)
    dwp = [_matmul_tn(dp, hn, tk, f"w_in_dw_{i}")[0] for i, dp in enumerate(dpieces)]
    dw_in = jnp.concatenate([dwp[0], dwp[1], dwp[2][:N_GATE_COLS], dwp[3]], axis=0)
    dw_in = dw_in.reshape(N_SHARDS, N_IN_COLS // N_SHARDS, d)
    sums_mix = pair_sums([dw_in, dwo], "b")
    (dx1, d_mixnw), slots_mix = _mix_in_bwd_dx(dx2, x1, mix_norm, dpieces, wp, TE, exchange=_scatter_exchange(sums_mix))
    (gx, dyh1, dg1, du1, h1, d_nw1), _ = _ffn_bwd_dx(dx1, xs, ffn1_norm, g1, u1, *f1, TM, "ffn1_bwd_dx")
    dwg1 = _matmul_tn(dg1, xn1, tk, "ffn1_dwg")
    dwu1 = _matmul_tn(du1, xn1, tk, "ffn1_dwu")
    sums_gu = pair_sums([dwg1, dwu1], "c")
    dwd1, slots_gu = _matmul_tn(h1, dyh1, tk, "ffn1_dwd", exchange=_scatter_exchange(sums_gu))
    sums_d = pair_sums([dwd1], "d")
    slots = slots_gu + _rs_chips(sums_d) + slots_mix + slots_f2
    sums = sums_gu + sums_d + sums_mix + sums_f2
    halves = _sum_slots(slots[:4], sums[:4], "rs_sum_a") + _sum_slots(slots[4:], sums[4:], "rs_sum_b")
    g_big = dict(zip(BIG, _rs_join(halves, [_split_axis(shards[n].shape) for n in BIG])))

    small_partial = {"ffn1_norm": d_nw1, "mix_norm": d_mixnw, "a_log": dgp[0, 0:8], "dt_bias": dgp[1, 0:8],
                     "gdn_norm_w": d_gdnw, "q_norm_w": d_qw, "k_norm_w": d_kw, "rel_bias": d_rel,
                     "ffn2_norm": d_nw2, "final_norm": d_final, "conv_w": dcw[0:CONV_TAPS].T}
    red = _all_reduce_small(_pack([small_partial[n] for n in SMALL] + [loss_part[0, 0:1]]))
    full_shapes = [p[n].shape if n != "conv_w" else (N_SHARDS * conv_rows, CONV_TAPS) for n in SMALL]
    red_parts = _unpack(red, full_shapes + [(1,)])
    loss = red_parts[-1].reshape(())
    g_small = dict(zip(SMALL, red_parts[:-1]))
    g_small["conv_w"] = lax.dynamic_slice_in_dim(g_small["conv_w"], me * conv_rows, conv_rows, 0).reshape(conv_w.shape)

    grads, deltas, new_m, new_v = {}, {}, {}, {}
    for n in BIG:
        back = (lambda a: a.T[None]) if n in COL_SHARDED else (lambda a: a[None])
        grads[n] = back(g_big[n])
        dl, nm, nv = _adamw(local(n, p[n]), g_big[n], local(n, p["m_" + n]), local(n, p["v_" + n]), "adamw_" + n)
        deltas[n], new_m[n], new_v[n] = back(dl), back(nm), back(nv)
    packed = [_pack([src[n] for n in SMALL]) for src in
              ({n: p[n] for n in SMALL}, g_small, {n: p["m_" + n] for n in SMALL}, {n: p["v_" + n] for n in SMALL})]
    small_shapes = [p[n].shape for n in SMALL]
    for dst, arr in zip((deltas, new_m, new_v), _adamw(*packed, "adamw_small")):
        dst.update(zip(SMALL, _unpack(arr, small_shapes)))
    grads.update(g_small)

    return (loss, gx[None], *[grads[n] for n in WEIGHTS], *[deltas[n] for n in WEIGHTS],
            *[new_m[n] for n in WEIGHTS], *[new_v[n] for n in WEIGHTS])
```

```python
import math
from typing import Callable, NamedTuple

import numpy as np
import jax
import jax.numpy as jnp
from jax import lax
from jax.experimental import pallas as pl
from jax.experimental.pallas import tpu as pltpu

F32 = jnp.float32
BF16 = jnp.bfloat16
HIGHEST = lax.Precision.HIGHEST
MESH = pl.DeviceIdType.MESH

EPS = 1e-6
NEG_BIG = -1e30
GDN_HEADS = 4
GDN_DIM = 128
CHUNK = 64
SWA_HEADS = 8
SWA_DIM = 64
PATTERNS = ((128, 1), (512, 4), (2048, 16))
RADIUS = 64
REL_BUCKETS = 32
REL_MAX_DISTANCE = 1024
CONV_TAPS = 5
N_SHARDS = 4
LANES = 128
VMEM_LIMIT = 56 * 1024 * 1024

ADAM_LR, ADAM_B1, ADAM_B2, ADAM_EPS, ADAM_WD, ADAM_STEP = 0.001, 0.9, 0.999, 1e-08, 0.01, 10


def _params(sem=None, vmem=None):
    return pltpu.CompilerParams(dimension_semantics=sem, vmem_limit_bytes=vmem)


def _resident(shape):
    nd = len(shape)
    return pl.BlockSpec(shape, lambda *_: (0,) * nd, pipeline_mode=pl.Buffered(1))


ANY = pl.BlockSpec(memory_space=pl.ANY)


class _Exchange(NamedTuple):
    arrays: tuple
    out_shape: tuple
    start: Callable
    finish: Callable


def _grid_call(body, name, nsteps, in_specs, out_specs, out_shape, operands, scratch=(), exchange=None):
    params = _params(("arbitrary",), VMEM_LIMIT)
    if exchange is None:
        res = pl.pallas_call(body, name=name, grid=(nsteps,), in_specs=list(in_specs), out_specs=list(out_specs),
                             out_shape=list(out_shape), scratch_shapes=list(scratch), compiler_params=params)(*operands)
        return list(res), []
    n_in, n_out, k, n_scr = len(in_specs), len(out_specs), len(exchange.arrays), len(scratch)

    def wrapped(*refs):
        ins, cin = refs[:n_in], refs[n_in:n_in + k]
        outs, cout = refs[n_in + k:n_in + k + n_out], refs[n_in + k + n_out:n_in + 2 * k + n_out]
        rest = refs[n_in + 2 * k + n_out:]
        scr, (send_sems, recv_sems) = rest[:n_scr], rest[n_scr:]

        @pl.when(pl.program_id(0) == 0)
        def _():
            exchange.start(cin, cout, send_sems, recv_sems)

        body(*ins, *outs, *scr)

        @pl.when(pl.program_id(0) == nsteps - 1)
        def _():
            exchange.finish(cin, cout, send_sems, recv_sems)

    res = pl.pallas_call(
        wrapped, name=name, grid=(nsteps,), in_specs=list(in_specs) + [ANY] * k, out_specs=list(out_specs) + [ANY] * k,
        out_shape=list(out_shape) + list(exchange.out_shape),
        scratch_shapes=list(scratch) + [pltpu.SemaphoreType.DMA((k, 4)), pltpu.SemaphoreType.DMA((k, 4))],
        compiler_params=params)(*operands, *exchange.arrays)
    return list(res[:n_out]), list(res[n_out:])


def _dot(a, b):
    return jnp.dot(a.astype(BF16), b.astype(BF16), preferred_element_type=F32)


def _dot_nt(a, b):
    return lax.dot_general(a.astype(BF16), b.astype(BF16), (((1,), (1,)), ((), ())), preferred_element_type=F32)


def _dot_tn(a, b):
    return lax.dot_general(a.astype(BF16), b.astype(BF16), (((0,), (0,)), ((), ())), preferred_element_type=F32)


def _dot_hi(a, b):
    return jnp.dot(a, b, preferred_element_type=F32, precision=HIGHEST)


def _sigmoid(x):
    return 1.0 / (1.0 + jnp.exp(-x))


def _rstd(xf):
    return lax.rsqrt(jnp.mean(xf * xf, axis=-1, keepdims=True) + EPS)


def _rms_bwd(xf, r, nw, dxn):
    xhat = xf * r
    dxh = dxn * nw
    dx = r * (dxh - xhat * jnp.mean(dxh * xhat, axis=-1, keepdims=True))
    return dx, jnp.sum(dxn * xhat, axis=0, keepdims=True)


def _ffn_fwd(x, nw, wg, wu, wd, tm, name, exchange=None, head=None):
    t, d = x.shape
    nj, fs, _ = wg.shape

    def body(x_ref, nw_ref, wg_ref, wu_ref, wd_ref, *rest):
        if head is None:
            y_ref, xn_ref, g_ref, u_ref = rest
        else:
            fw_ref, t_ref, y_ref, xn_ref, g_ref, u_ref, loss_ref, dfw_ref = rest

            @pl.when(pl.program_id(0) == 0)
            def _():
                loss_ref[...] = jnp.zeros_like(loss_ref)
                dfw_ref[...] = jnp.zeros_like(dfw_ref)

        xf = x_ref[...]
        xn = (xf * _rstd(xf) * nw_ref[...]).astype(BF16)
        xn_ref[...] = xn
        acc = jnp.zeros((tm, d), F32)
        for j in range(nj):
            g = _dot_nt(xn, wg_ref[j])
            u = _dot_nt(xn, wu_ref[j])
            h = (g * _sigmoid(g) * u).astype(BF16)
            acc = acc + jnp.dot(h, wd_ref[j], preferred_element_type=F32)
            g_ref[j] = g.astype(BF16)
            u_ref[j] = u.astype(BF16)
        y = xf + 0.5 * acc
        if head is None:
            y_ref[...] = y
        else:
            r = _rstd(y)
            err = y * r * fw_ref[...] - t_ref[...]
            loss_ref[...] += 0.5 * jnp.sum(jnp.mean(err * err, axis=-1, keepdims=True), axis=0, keepdims=True)
            dy, dfw = _rms_bwd(y, r, fw_ref[...], err * (1.0 / d))
            y_ref[...] = dy
            dfw_ref[...] += dfw

    row = pl.BlockSpec((tm, d), lambda i: (i, 0))
    act = pl.BlockSpec((nj, tm, fs), lambda i: (0, i, 0))
    in_specs = [row, _resident((1, d)), _resident(wg.shape), _resident(wu.shape), _resident(wd.shape)]
    out_specs = [row, row, act, act]
    out_shape = [jax.ShapeDtypeStruct((t, d), F32), jax.ShapeDtypeStruct((t, d), BF16),
                 jax.ShapeDtypeStruct((nj, t, fs), BF16), jax.ShapeDtypeStruct((nj, t, fs), BF16)]
    operands = (x, nw, wg, wu, wd)
    if head is not None:
        in_specs += [_resident((1, d)), row]
        out_specs += [pl.BlockSpec((1, LANES), lambda i: (0, 0)), pl.BlockSpec((1, d), lambda i: (0, 0))]
        out_shape += [jax.ShapeDtypeStruct((1, LANES), F32), jax.ShapeDtypeStruct((1, d), F32)]
        operands += tuple(head)
    return _grid_call(body, name, t // tm, in_specs, out_specs, out_shape, operands, exchange=exchange)


def _ffn_bwd_dx(dy, x, nw, g, u, wg, wu, wd, tm, name, exchange=None):
    t, d = x.shape
    nj, fs, _ = wg.shape

    def body(dy_ref, x_ref, nw_ref, g_ref, u_ref, wg_ref, wu_ref, wd_ref,
             dx_ref, dyh_ref, dg_ref, du_ref, h_ref, dnw_ref):
        @pl.when(pl.program_id(0) == 0)
        def _():
            dnw_ref[...] = jnp.zeros_like(dnw_ref)

        dyv = dy_ref[...]
        dyh = (0.5 * dyv).astype(BF16)
        dyh_ref[...] = dyh
        dxn = jnp.zeros((tm, d), F32)
        dh_next = _dot_nt(dyh, wd_ref[0])
        for j in range(nj):
            dh = dh_next
            gv = g_ref[j].astype(F32)
            uv = u_ref[j].astype(F32)
            sg = _sigmoid(gv)
            si = gv * sg
            dg = (dh * uv * (sg * (1.0 + gv * (1.0 - sg)))).astype(BF16)
            du = (dh * si).astype(BF16)
            if j + 1 < nj:
                dh_next = _dot_nt(dyh, wd_ref[j + 1])
            h_ref[j] = (si * uv).astype(BF16)
            dg_ref[j] = dg
            du_ref[j] = du
            dxn = dxn + _dot(dg, wg_ref[j]) + _dot(du, wu_ref[j])
        xf = x_ref[...]
        dxr, dnw = _rms_bwd(xf, _rstd(xf), nw_ref[...], dxn)
        dx_ref[...] = dyv + dxr
        dnw_ref[...] += dnw

    row = pl.BlockSpec((tm, d), lambda i: (i, 0))
    act = pl.BlockSpec((nj, tm, fs), lambda i: (0, i, 0))
    act_shape = jax.ShapeDtypeStruct((nj, t, fs), BF16)
    return _grid_call(
        body, name, t // tm,
        [row, row, _resident((1, d)), act, act, _resident(wg.shape), _resident(wu.shape), _resident(wd.shape)],
        [row, row, act, act, act, pl.BlockSpec((1, d), lambda i: (0, 0))],
        [jax.ShapeDtypeStruct((t, d), F32), jax.ShapeDtypeStruct((t, d), BF16),
         act_shape, act_shape, act_shape, jax.ShapeDtypeStruct((1, d), F32)],
        (dy, x, nw, g, u, wg, wu, wd), exchange=exchange)


def _matmul_tn(a, b, tk, name, exchange=None):
    a3, b3 = a.ndim == 3, b.ndim == 3
    nj = a.shape[0] if a3 else (b.shape[0] if b3 else 1)
    t, m = a.shape[-2:]
    n = b.shape[-1]
    nt = t // tk

    def body(a_ref, b_ref, o_ref, acc_ref):
        k = pl.program_id(0) % nt

        @pl.when(k == 0)
        def _():
            acc_ref[...] = jnp.zeros_like(acc_ref)

        acc_ref[...] += lax.dot_general(a_ref[...], b_ref[...], (((0,), (0,)), ((), ())),
                                        preferred_element_type=F32)

        @pl.when(k == nt - 1)
        def _():
            o_ref[...] = acc_ref[...].astype(o_ref.dtype)

    a_spec = (pl.BlockSpec((None, tk, m), lambda i: (i // nt, i % nt, 0)) if a3
              else pl.BlockSpec((tk, m), lambda i: (i % nt, 0)))
    b_spec = (pl.BlockSpec((None, tk, n), lambda i: (i // nt, i % nt, 0)) if b3
              else pl.BlockSpec((tk, n), lambda i: (i % nt, 0)))
    (out,), landed = _grid_call(
        body, name, nj * nt, [a_spec, b_spec], [pl.BlockSpec((None, m, n), lambda i: (i // nt, 0, 0))],
        [jax.ShapeDtypeStruct((nj, m, n), BF16)], (a, b), scratch=[pltpu.VMEM((m, n), F32)], exchange=exchange)
    return out if exchange is None else (out, landed)


P_QKVA, P_Z, P_AB, P_QKVB = (0, 1536), (1536, 2048), (2048, 2176), (2176, 3712)
P_PIECES = (P_QKVA, P_Z, P_AB, P_QKVB)
N_GATE_COLS = 4 * GDN_HEADS


def _mix_in_fwd(x1, nw, wp, tm):
    t, d = x1.shape

    def body(x_ref, nw_ref, w_ref, hn_ref, *outs):
        xf = x_ref[...]
        xn = (xf * _rstd(xf) * nw_ref[...]).astype(BF16)
        hn_ref[...] = xn
        for (a, b), o_ref in zip(P_PIECES, outs):
            o_ref[...] = _dot_nt(xn, w_ref[a:b, :])

    row = pl.BlockSpec((tm, d), lambda i: (i, 0))
    return pl.pallas_call(
        body, name="mix_in_fwd", grid=(t // tm,),
        in_specs=[row, _resident((1, d)), _resident(wp.shape)],
        out_specs=[row] + [pl.BlockSpec((tm, b - a), lambda i: (i, 0)) for a, b in P_PIECES],
        out_shape=[jax.ShapeDtypeStruct((t, d), BF16)]
                  + [jax.ShapeDtypeStruct((t, b - a), F32) for a, b in P_PIECES],
        compiler_params=_params(("arbitrary",), VMEM_LIMIT),
    )(x1, nw, wp)


def _mix_in_bwd_dx(dx, x1, nw, dpieces, wp, tm, exchange=None):
    t, d = x1.shape

    def body(dx_ref, x_ref, nw_ref, p0, p1, p2, p3, w_ref, o_ref, dnw_ref):
        @pl.when(pl.program_id(0) == 0)
        def _():
            dnw_ref[...] = jnp.zeros_like(dnw_ref)

        dh = jnp.zeros((tm, d), F32)
        for (a, b), p_ref in zip(P_PIECES, (p0, p1, p2, p3)):
            dh = dh + _dot(p_ref[...], w_ref[a:b, :])
        xf = x_ref[...]
        dxr, dnw = _rms_bwd(xf, _rstd(xf), nw_ref[...], dh)
        o_ref[...] = dx_ref[...] + dxr
        dnw_ref[...] += dnw

    row = pl.BlockSpec((tm, d), lambda i: (i, 0))
    return _grid_call(
        body, "mix_in_bwd_dx", t // tm,
        [row, row, _resident((1, d))]
        + [pl.BlockSpec((tm, b - a), lambda i: (i, 0)) for a, b in P_PIECES] + [_resident(wp.shape)],
        [row, pl.BlockSpec((1, d), lambda i: (0, 0))],
        [jax.ShapeDtypeStruct((t, d), F32), jax.ShapeDtypeStruct((1, d), F32)],
        (dx, x1, nw, *dpieces, wp), exchange=exchange)


def _mix_out_fwd(x1, oa, ob, w, tm):
    t, d = x1.shape
    half = oa.shape[1]

    def body(x_ref, oa_ref, ob_ref, w_ref, o_ref):
        o_ref[...] = (x_ref[...] + _dot(oa_ref[...], w_ref[0:half, :]) + _dot(ob_ref[...], w_ref[half:2 * half, :]))

    row = pl.BlockSpec((tm, d), lambda i: (i, 0))
    hrow = pl.BlockSpec((tm, half), lambda i: (i, 0))
    return pl.pallas_call(
        body, name="mix_out_fwd", grid=(t // tm,),
        in_specs=[row, hrow, hrow, _resident(w.shape)],
        out_specs=row, out_shape=jax.ShapeDtypeStruct((t, d), F32),
        compiler_params=_params(("arbitrary",), VMEM_LIMIT),
    )(x1, oa, ob, w)


def _mix_out_bwd(dx2, w, tm, exchange=None):
    t, d = dx2.shape
    half = w.shape[0] // 2

    def body(dx_ref, w_ref, doa_ref, dob_ref, dxb_ref):
        dxb = dx_ref[...].astype(BF16)
        dxb_ref[...] = dxb
        doa_ref[...] = _dot_nt(dxb, w_ref[0:half, :])
        dob_ref[...] = _dot_nt(dxb, w_ref[half:2 * half, :])

    row = pl.BlockSpec((tm, d), lambda i: (i, 0))
    hrow = pl.BlockSpec((tm, half), lambda i: (i, 0))
    return _grid_call(
        body, "mix_out_bwd", t // tm, [row, _resident(w.shape)], [hrow, hrow, row],
        [jax.ShapeDtypeStruct((t, half), F32), jax.ShapeDtypeStruct((t, half), F32), jax.ShapeDtypeStruct((t, d), BF16)],
        (dx2, w), exchange=exchange)


HALO = 8


def _halo_row_specs(tr, cols, nrow8):
    per = tr // HALO
    return [pl.BlockSpec((tr, cols), lambda i: (i, 0)),
            pl.BlockSpec((HALO, cols), lambda i: (jnp.maximum(i * per - 1, 0), 0)),
            pl.BlockSpec((HALO, cols), lambda i: (jnp.minimum((i + 1) * per, nrow8 - 1), 0))]


def _conv_window(xm, xp, xn, first, last, cols):
    prev = jnp.where(first, 0.0, xp[:, cols])
    nxt = jnp.where(last, 0.0, xn[:, cols])
    return jnp.concatenate([prev, xm[:, cols], nxt], axis=0)


def _shift_rows(xw, off):
    n = xw.shape[0]
    sh = (-off) % n
    return xw if sh == 0 else pltpu.roll(xw, sh, 0)


def _conv_pre(xw, cw_ref, cols):
    acc, shifted = None, []
    for j in range(CONV_TAPS):
        shifted.append(_shift_rows(xw, j - CONV_TAPS // 2))
        term = shifted[-1] * cw_ref[j:j + 1, cols]
        acc = term if acc is None else acc + term
    return acc, shifted


def _softplus(x):
    u = jnp.exp(-jnp.abs(x))
    w = 1.0 + u
    log1p = jnp.where(w == 1.0, u, jnp.log(w) * u / jnp.where(w == 1.0, 1.0, w - 1.0))
    return jnp.maximum(x, 0.0) + log1p


def _gdn_prep_fwd(qkva, cw, ab, gp, tr):
    t, c = qkva.shape
    nt = t // tr
    ncb = c // LANES

    def body(xm, xp, xn, cw_ref, ab_ref, gp_ref, o_ref, gb_ref):
        i = pl.program_id(0)
        first, last = i == 0, i == nt - 1
        for cb in range(ncb):
            cols = slice(cb * LANES, (cb + 1) * LANES)
            xw = _conv_window(xm, xp, xn, first, last, cols)
            pre = _conv_pre(xw, cw_ref, cols)[0][HALO:HALO + tr]
            y = pre * _sigmoid(pre)
            if cb < 2 * GDN_HEADS:
                y = y * lax.rsqrt(jnp.sum(y * y, axis=-1, keepdims=True) + EPS)
            if cb < GDN_HEADS:
                y = y * (GDN_DIM ** -0.5)
            o_ref[:, cols] = y
        abv = ab_ref[...]
        lane = lax.broadcasted_iota(jnp.int32, abv.shape, 1)
        g = -jnp.exp(gp_ref[0:1, :]) * _softplus(abv + gp_ref[1:2, :])
        gb_ref[...] = jnp.where(lane < 8, g, jnp.where(lane < 16, _sigmoid(abv), 0.0))

    return pl.pallas_call(
        body, name="gdn_prep_fwd", grid=(nt,),
        in_specs=_halo_row_specs(tr, c, t // HALO)
                 + [_resident(cw.shape), pl.BlockSpec((tr, LANES), lambda i: (i, 0)), _resident(gp.shape)],
        out_specs=[pl.BlockSpec((tr, c), lambda i: (i, 0)), pl.BlockSpec((tr, LANES), lambda i: (i, 0))],
        out_shape=[jax.ShapeDtypeStruct((t, c), F32), jax.ShapeDtypeStruct((t, LANES), F32)],
        compiler_params=_params(("arbitrary",), VMEM_LIMIT),
    )(qkva, qkva, qkva, cw, ab, gp)


def _gdn_prep_bwd(qkva, cw, ab, gp, dy, dgates, tr):
    t, c = qkva.shape
    nt = t // tr
    ncb = c // LANES

    def body(xm, xp, xn, fm, fp, fn, cw_ref, ab_ref, gp_ref, gf_ref, dx_ref, dab_ref, dcw_ref, dgp_ref):
        i = pl.program_id(0)
        first, last = i == 0, i == nt - 1

        @pl.when(first)
        def _():
            dcw_ref[...] = jnp.zeros_like(dcw_ref)
            dgp_ref[...] = jnp.zeros_like(dgp_ref)

        sub8 = lax.broadcasted_iota(jnp.int32, (8, LANES), 0)
        for cb in range(ncb):
            cols = slice(cb * LANES, (cb + 1) * LANES)
            xw = _conv_window(xm, xp, xn, first, last, cols)
            dyw = _conv_window(fm, fp, fn, first, last, cols)
            pre, x_shifted = _conv_pre(xw, cw_ref, cols)
            sg = _sigmoid(pre)
            s = pre * sg
            if cb < 2 * GDN_HEADS:
                scale = (GDN_DIM ** -0.5) if cb < GDN_HEADS else 1.0
                r = lax.rsqrt(jnp.sum(s * s, axis=-1, keepdims=True) + EPS)
                dn = dyw * scale
                ds = r * dn - s * (r * r * r) * jnp.sum(dn * s, axis=-1, keepdims=True)
            else:
                ds = dyw
            dpre = ds * (sg * (1.0 + pre * (1.0 - sg)))
            dx = None
            dcw = jnp.zeros((8, LANES), F32)
            for j in range(CONV_TAPS):
                off = j - CONV_TAPS // 2
                term = _shift_rows(dpre, -off)[HALO:HALO + tr] * cw_ref[j:j + 1, cols]
                dx = term if dx is None else dx + term
                tap = jnp.sum(dpre[HALO:HALO + tr] * x_shifted[j][HALO:HALO + tr], axis=0, keepdims=True)
                dcw = dcw + jnp.where(sub8 == j, tap, 0.0)
            dx_ref[:, cols] = dx.astype(BF16)
            dcw_ref[:, cols] += dcw

        abv = ab_ref[...]
        dgb = gf_ref[...]
        lane = lax.broadcasted_iota(jnp.int32, abv.shape, 1)
        nea = -jnp.exp(gp_ref[0:1, :])
        xs = abv + gp_ref[1:2, :]
        g = nea * _softplus(xs)
        beta = _sigmoid(abv)
        da = dgb * nea * _sigmoid(xs)
        dab = jnp.where(lane < 8, da, jnp.where(lane < 16, dgb * beta * (1.0 - beta), 0.0))
        dab_ref[...] = dab.astype(BF16)
        keep = lane[0:1, :] < 8
        dalog = jnp.where(keep, jnp.sum(dgb * g, axis=0, keepdims=True), 0.0)
        ddtb = jnp.where(keep, jnp.sum(da, axis=0, keepdims=True), 0.0)
        dgp_ref[...] += jnp.where(sub8 == 0, dalog, 0.0) + jnp.where(sub8 == 1, ddtb, 0.0)

    lrow = pl.BlockSpec((tr, LANES), lambda i: (i, 0))
    halo = _halo_row_specs(tr, c, t // HALO)
    return pl.pallas_call(
        body, name="gdn_prep_bwd", grid=(nt,),
        in_specs=halo + halo + [_resident(cw.shape), lrow, _resident(gp.shape), lrow],
        out_specs=[pl.BlockSpec((tr, c), lambda i: (i, 0)), lrow,
                   pl.BlockSpec(cw.shape, lambda i: (0, 0)), pl.BlockSpec(gp.shape, lambda i: (0, 0))],
        out_shape=[jax.ShapeDtypeStruct((t, c), BF16), jax.ShapeDtypeStruct((t, LANES), BF16),
                   jax.ShapeDtypeStruct(cw.shape, F32), jax.ShapeDtypeStruct(gp.shape, F32)],
        compiler_params=_params(("arbitrary",), VMEM_LIMIT),
    )(qkva, qkva, qkva, dy, dy, dy, cw, ab, gp, dgates)


def _chunk_masks(lower):
    ii = lax.broadcasted_iota(jnp.int32, (CHUNK, CHUNK), 0)
    jj = lax.broadcasted_iota(jnp.int32, (CHUNK, CHUNK), 1)
    incl = (ii >= jj) if lower else (ii <= jj)
    strict = (ii > jj) if lower else (ii < jj)
    return ii, jj, incl, strict


def _dot3(a, b):
    ah = a.astype(BF16)
    al = (a - ah.astype(F32)).astype(BF16)
    bh = b.astype(BF16)
    bl = (b - bh.astype(F32)).astype(BF16)
    d = lambda u, v: jnp.dot(u, v, preferred_element_type=F32)
    return d(ah, bh) + (d(ah, bl) + d(al, bh))


def _tri_inv_many(lmats, ii, jj):
    m16 = (ii // 16) == (jj // 16)
    m32 = (ii // 32) == (jj // 32)
    eye = jnp.where(ii == jj, 1.0, 0.0)
    l16 = [jnp.where(m16, l, 0.0) for l in lmats]
    p2 = [_dot3(a, a) for a in l16]
    p4 = [_dot3(a, a) for a in p2]
    p8 = [_dot3(a, a) for a in p4]
    xs = [eye - a for a in l16]
    for ps in (p2, p4, p8):
        xs = [x + _dot3(x, p) for x, p in zip(xs, ps)]
    for off in ([jnp.where(m32 & jnp.logical_not(m16), l, 0.0) for l in lmats],
                [jnp.where(m32, 0.0, l) for l in lmats]):
        ys = [_dot3(x, c) for x, c in zip(xs, off)]
        xs = [x - _dot3(y, x) for x, y in zip(xs, ys)]
    return xs


def _col_to_row(col, ii, jj):
    return jnp.sum(jnp.where(ii == jj, col, 0.0), axis=0, keepdims=True)


def _row_to_col(row, ii, jj):
    return jnp.sum(jnp.where(ii == jj, row, 0.0), axis=1, keepdims=True)


def _chain_common(q, k, v, graw_col, graw_row, bcol, masks):
    ii, jj, incl, strict = masks
    inclt = jnp.logical_not(strict)
    gcol = jnp.sum(jnp.where(incl, graw_row, 0.0), axis=1, keepdims=True)
    grow = jnp.sum(jnp.where(inclt, graw_col, 0.0), axis=0, keepdims=True)
    glast = jnp.sum(graw_row, axis=1, keepdims=True)
    decay = jnp.where(incl, jnp.exp(jnp.where(incl, gcol - grow, 0.0)), 0.0)
    kb = k * bcol
    vb = v * bcol
    eg = jnp.exp(gcol)
    ek = jnp.exp(glast - gcol)
    kbg = kb * eg
    amat = _dot_nt(kb, k)
    qk = _dot_nt(q, k)
    return dict(gcol=gcol, glast=glast, decay=decay, kb=kb, vb=vb, eg=eg, ek=ek, kbg=kbg, amat=amat, qk=qk,
                intra=qk * decay, qg=q * eg, kdec=k * ek)


def _gdn_fwd(qkvc, gb, gbt):
    tm, u, w, qg, kd, intra, egl = _gdn_local_fwd(qkvc, gb, gbt)
    o_f, o_b, s_f, s_b, vn_f, vn_b = _gdn_scan_fwd(u, w, qg, kd, intra, egl, qkvc.shape[0])
    return o_f, o_b, dict(tm=tm, w=w, qg=qg, kd=kd, intra=intra, egl=egl, s=(s_f, s_b), vn=(vn_f, vn_b))


N_CHAINS = 2 * GDN_HEADS


LOCAL_CHUNKS = 4


def _load_chains(x_ref, g_ref, gt_ref, cc=0):
    hd = GDN_HEADS * GDN_DIM
    rows = slice(cc * CHUNK, (cc + 1) * CHUNK)
    chains = []
    for d in range(2):
        masks = _chunk_masks(d == 0)
        for h in range(GDN_HEADS):
            ch = d * GDN_HEADS + h
            q = x_ref[rows, h * GDN_DIM:(h + 1) * GDN_DIM]
            k = x_ref[rows, hd + h * GDN_DIM:hd + (h + 1) * GDN_DIM]
            v = x_ref[rows, 2 * hd + h * GDN_DIM:2 * hd + (h + 1) * GDN_DIM]
            bcol = g_ref[rows, 8 + ch:9 + ch]
            cm = _chain_common(q, k, v, g_ref[rows, ch:ch + 1], gt_ref[cc, ch:ch + 1, :], bcol, masks)
            chains.append(dict(cm, q=q, k=k, v=v, bcol=bcol, masks=masks, ch=ch, h=h, cc=cc))
    return chains


def _chain_shape(rows, cols, dtype):
    return lambda nc: jax.ShapeDtypeStruct((nc, N_CHAINS, rows, cols), dtype)


def _gdn_local_fwd(qkvc, gb, gbt):
    t = qkvc.shape[0]
    nc = t // CHUNK
    hd = GDN_HEADS * GDN_DIM

    def body(x_ref, g_ref, gt_ref, t_ref, u_ref, w_ref, qg_ref, kd_ref, in_ref, eg_ref):
        chains = [c for cc in range(LOCAL_CHUNKS) for c in _load_chains(x_ref, g_ref, gt_ref, cc)]
        ii, jj = chains[0]["masks"][0:2]
        tms = _tri_inv_many([jnp.where(c["masks"][3], c["amat"] * c["decay"], 0.0) for c in chains], ii, jj)
        uws = [_dot(tm, jnp.concatenate([c["vb"], c["kbg"]], axis=1)) for tm, c in zip(tms, chains)]
        for c, tm, uw in zip(chains, tms, uws):
            cc, ch = c["cc"], c["ch"]
            t_ref[cc, ch] = tm
            u_ref[cc, ch] = uw[:, :GDN_DIM]
            w_ref[cc, ch] = uw[:, GDN_DIM:].astype(BF16)
            qg_ref[cc, ch] = c["qg"].astype(BF16)
            kd_ref[cc, ch] = c["kdec"].astype(BF16)
            in_ref[cc, ch] = c["intra"].astype(BF16)
            eg_ref[cc, ch:ch + 1, :] = jnp.broadcast_to(jnp.exp(c["glast"]), (1, LANES))

    lc = LOCAL_CHUNKS
    blk = lambda rows, cols: pl.BlockSpec((lc, N_CHAINS, rows, cols), lambda n: (n, 0, 0, 0))
    shapes = [_chain_shape(CHUNK, CHUNK, F32), _chain_shape(CHUNK, GDN_DIM, F32), _chain_shape(CHUNK, GDN_DIM, BF16),
              _chain_shape(CHUNK, GDN_DIM, BF16), _chain_shape(CHUNK, GDN_DIM, BF16), _chain_shape(CHUNK, CHUNK, BF16)]
    return tuple(pl.pallas_call(
        body, name="gdn_local_fwd", grid=(nc // lc,),
        in_specs=[pl.BlockSpec((lc * CHUNK, 3 * hd), lambda n: (n, 0)), pl.BlockSpec((lc * CHUNK, LANES), lambda n: (n, 0)),
                  pl.BlockSpec((lc, 16, CHUNK), lambda n: (n, 0, 0))],
        out_specs=[blk(CHUNK, CHUNK), blk(CHUNK, GDN_DIM), blk(CHUNK, GDN_DIM), blk(CHUNK, GDN_DIM),
                   blk(CHUNK, GDN_DIM), blk(CHUNK, CHUNK), pl.BlockSpec((lc, N_CHAINS, LANES), lambda n: (n, 0, 0))],
        out_shape=[s(nc) for s in shapes] + [jax.ShapeDtypeStruct((nc, N_CHAINS, LANES), F32)],
        compiler_params=_params(("arbitrary",), VMEM_LIMIT),
    )(qkvc, gb, gbt))


SCAN_CHUNKS = 8


def _dir_specs(nc, rev):
    nb = nc // SCAN_CHUNKS

    def spec(d, rows, cols, own=False):
        chunk = (lambda n: n) if (d == 0) != rev else (lambda n: nb - 1 - n)
        blk = 0 if own else d
        if rows is None:
            return pl.BlockSpec((SCAN_CHUNKS, GDN_HEADS if own else N_CHAINS, cols), lambda n: (chunk(n), 0, 0))
        return pl.BlockSpec((SCAN_CHUNKS, GDN_HEADS, rows, cols), lambda n: (chunk(n), blk, 0, 0))

    def rows_spec(d, cols):
        chunk = (lambda n: n) if (d == 0) != rev else (lambda n: nb - 1 - n)
        return pl.BlockSpec((SCAN_CHUNKS * CHUNK, cols), lambda n: (chunk(n), 0))

    def order(d):
        return list(range(SCAN_CHUNKS)) if (d == 0) != rev else list(range(SCAN_CHUNKS - 1, -1, -1))
    return spec, rows_spec, order


def _gdn_scan_fwd(u, w, qg, kd, intra, egl, t):
    nc = t // CHUNK
    hd = GDN_HEADS * GDN_DIM

    def body(*refs):
        ins, outs, state = refs[:12], refs[12:18], refs[18]
        @pl.when(pl.program_id(0) == 0)
        def _():
            state[...] = jnp.zeros_like(state)

        chains = [(d, h) for d in range(2) for h in range(GDN_HEADS)]
        states = [state[ch] for ch in range(N_CHAINS)]
        for step in range(SCAN_CHUNKS):
            at = [order(d)[step] for d in range(2)]
            pick = lambda k, d, h: ins[2 * k + d][at[d], h]
            sbs = [s.astype(BF16) for s in states]
            ws = [_dot(pick(1, d, h), sb) for (d, h), sb in zip(chains, sbs)]
            o1 = [_dot(pick(2, d, h), sb) for (d, h), sb in zip(chains, sbs)]
            vns = [(pick(0, d, h) - wsb).astype(BF16) for (d, h), wsb in zip(chains, ws)]
            o2 = [_dot(pick(4, d, h), vn) for (d, h), vn in zip(chains, vns)]
            kv = [_dot_tn(pick(3, d, h), vn) for (d, h), vn in zip(chains, vns)]
            new_states = []
            for ch, (d, h) in enumerate(chains):
                outs[d][at[d] * CHUNK:(at[d] + 1) * CHUNK, h * GDN_DIM:(h + 1) * GDN_DIM] = o1[ch] + o2[ch]
                outs[2 + d][at[d], h] = states[ch]
                outs[4 + d][at[d], h] = vns[ch]
                new_states.append(states[ch] * ins[10 + d][at[d], ch:ch + 1, :] + kv[ch])
            states = new_states
        for ch in range(N_CHAINS):
            state[ch] = states[ch]

    spec, rows_spec, order = _dir_specs(nc, False)
    pair = lambda rows, cols, own=False: [spec(0, rows, cols, own), spec(1, rows, cols, own)]
    s_shape = jax.ShapeDtypeStruct((nc, GDN_HEADS, GDN_DIM, GDN_DIM), F32)
    vn_shape = jax.ShapeDtypeStruct((nc, GDN_HEADS, CHUNK, GDN_DIM), BF16)
    return pl.pallas_call(
        body, name="gdn_scan_fwd", grid=(nc // SCAN_CHUNKS,),
        in_specs=(pair(CHUNK, GDN_DIM) + pair(CHUNK, GDN_DIM) + pair(CHUNK, GDN_DIM) + pair(CHUNK, GDN_DIM)
                  + pair(CHUNK, CHUNK) + pair(None, LANES)),
        out_specs=([rows_spec(0, hd), rows_spec(1, hd)] + pair(GDN_DIM, GDN_DIM, True)
                   + pair(CHUNK, GDN_DIM, True)),
        out_shape=[jax.ShapeDtypeStruct((t, hd), F32), jax.ShapeDtypeStruct((t, hd), F32),
                   s_shape, s_shape, vn_shape, vn_shape],
        scratch_shapes=[pltpu.VMEM((N_CHAINS, GDN_DIM, GDN_DIM), F32)],
        compiler_params=_params(("arbitrary",), VMEM_LIMIT),
    )(u, u, w, w, qg, qg, kd, kd, intra, intra, egl, egl)


def _gdn_bwd(qkvc, gb, gbt, do, saved, exchange=None):
    scan = _gdn_scan_bwd(do, saved, qkvc.shape[0])
    return _gdn_local_bwd(qkvc, gb, gbt, do, saved, scan, exchange)


def _gdn_scan_bwd(do, saved, t):
    nc = t // CHUNK
    hd = GDN_HEADS * GDN_DIM

    def body(*refs):
        ins, outs, dstate = refs[:16], refs[16:26], refs[26]
        @pl.when(pl.program_id(0) == 0)
        def _():
            dstate[...] = jnp.zeros_like(dstate)

        chains = [(d, h) for d in range(2) for h in range(GDN_HEADS)]
        dss = [dstate[ch] for ch in range(N_CHAINS)]
        for step in range(SCAN_CHUNKS):
            at = [order(d)[step] for d in range(2)]
            pick = lambda k, d, h: ins[2 * k + d][at[d], h]
            dsbs = [ds.astype(BF16) for ds in dss]
            ss = [pick(1, d, h) for d, h in chains]
            sbs = [s.astype(BF16) for s in ss]
            dos = [ins[d][at[d] * CHUNK:(at[d] + 1) * CHUNK, h * GDN_DIM:(h + 1) * GDN_DIM].astype(BF16)
                   for d, h in chains]
            dv1 = [_dot_tn(pick(5, d, h), dov) for (d, h), dov in zip(chains, dos)]
            dv2 = [_dot(pick(4, d, h), dsb) for (d, h), dsb in zip(chains, dsbs)]
            ds1 = [_dot_tn(pick(3, d, h), dov) for (d, h), dov in zip(chains, dos)]
            dkds = [_dot_nt(pick(6, d, h), dsb) for (d, h), dsb in zip(chains, dsbs)]
            dqgs = [_dot_nt(dov, sb) for dov, sb in zip(dos, sbs)]
            dvns = [(a + b).astype(BF16) for a, b in zip(dv1, dv2)]
            ds2 = [_dot_tn(pick(2, d, h), dvn) for (d, h), dvn in zip(chains, dvns)]
            dws = [_dot_nt(dvn, sb) for dvn, sb in zip(dvns, sbs)]
            new_dss = []
            for ch, (d, h) in enumerate(chains):
                egl = ins[14 + d][at[d], ch:ch + 1, :]
                outs[d][at[d], h] = dvns[ch]
                outs[2 + d][at[d], h] = (-dws[ch]).astype(BF16)
                outs[4 + d][at[d], h] = dqgs[ch]
                outs[6 + d][at[d], h] = dkds[ch]
                outs[8 + d][at[d], h:h + 1, :] = egl * jnp.sum(jnp.sum(ss[ch] * dss[ch], axis=1, keepdims=True),
                                                               axis=0, keepdims=True)
                new_dss.append(ds1[ch] + egl * dss[ch] - ds2[ch])
            dss = new_dss
        for ch in range(N_CHAINS):
            dstate[ch] = dss[ch]

    spec, rows_spec, order = _dir_specs(nc, True)
    pair = lambda rows, cols, own=False: [spec(0, rows, cols, own), spec(1, rows, cols, own)]
    s_f, s_b = saved["s"]
    vn_f, vn_b = saved["vn"]
    w, qg, kd, intra, egl = saved["w"], saved["qg"], saved["kd"], saved["intra"], saved["egl"]
    own = lambda rows, cols, dtype: jax.ShapeDtypeStruct((nc, GDN_HEADS, rows, cols), dtype)
    row_shape = jax.ShapeDtypeStruct((nc, GDN_HEADS, LANES), F32)
    return pl.pallas_call(
        body, name="gdn_scan_bwd", grid=(nc // SCAN_CHUNKS,),
        in_specs=([rows_spec(0, hd), rows_spec(1, hd)] + pair(GDN_DIM, GDN_DIM, True) + pair(CHUNK, GDN_DIM)
                  + pair(CHUNK, GDN_DIM) + pair(CHUNK, GDN_DIM) + pair(CHUNK, CHUNK) + pair(CHUNK, GDN_DIM, True)
                  + pair(None, LANES)),
        out_specs=(pair(CHUNK, GDN_DIM, True) + pair(CHUNK, GDN_DIM, True) + pair(CHUNK, GDN_DIM, True)
                   + pair(CHUNK, GDN_DIM, True) + pair(None, LANES, True)),
        out_shape=[own(CHUNK, GDN_DIM, BF16)] * 4 + [own(CHUNK, GDN_DIM, F32)] * 4 + [row_shape] * 2,
        scratch_shapes=[pltpu.VMEM((N_CHAINS, GDN_DIM, GDN_DIM), F32)],
        compiler_params=_params(("arbitrary",), VMEM_LIMIT),
    )(do, do, s_f, s_b, w, w, qg, qg, kd, kd, intra, intra, vn_f, vn_b, egl, egl)


def _dot3_nt(a, b):
    ah = a.astype(BF16)
    al = (a - ah.astype(F32)).astype(BF16)
    bh = b.astype(BF16)
    bl = (b - bh.astype(F32)).astype(BF16)
    return _dot_nt(ah, bh) + (_dot_nt(ah, bl) + _dot_nt(al, bh))


def _dot3_tn(a, b):
    ah = a.astype(BF16)
    al = (a - ah.astype(F32)).astype(BF16)
    bh = b.astype(BF16)
    bl = (b - bh.astype(F32)).astype(BF16)
    return _dot_tn(ah, bh) + (_dot_tn(ah, bl) + _dot_tn(al, bh))


def _gdn_local_bwd(qkvc, gb, gbt, do, saved, scan, exchange=None):
    t = qkvc.shape[0]
    nc = t // CHUNK
    hd = GDN_HEADS * GDN_DIM

    def body(*refs):
        x_ref, g_ref, gt_ref, do_ref, t_ref = refs[:5]
        per_dir = refs[5:17]
        dx_ref, dg_ref = refs[17:]
        chains = [c for cc in range(LOCAL_CHUNKS) for c in _load_chains(x_ref, g_ref, gt_ref, cc)]
        lane = lax.broadcasted_iota(jnp.int32, (CHUNK, LANES), 1)
        dgates = [jnp.zeros((CHUNK, LANES), F32) for _ in range(LOCAL_CHUNKS)]
        for c in chains:
            d = c["ch"] // GDN_HEADS
            vn_ref, dvn_ref, dw_ref, dqg_ref, dkd_ref, dgl_ref = per_dir[d::2]
            h, cc = c["h"], c["cc"]
            rows = slice(cc * CHUNK, (cc + 1) * CHUNK)
            c.update(tm=t_ref[cc, c["ch"]], dov=do_ref[rows, h * GDN_DIM:(h + 1) * GDN_DIM], vnew=vn_ref[cc, h],
                     dvnew=dvn_ref[cc, h], dw=dw_ref[cc, h], dqg=dqg_ref[cc, h], dkdec=dkd_ref[cc, h],
                     dglast=dgl_ref[cc, h:h + 1, 0:1])
        dintras = [_dot_nt(c["dov"], c["vnew"]) for c in chains]
        dts = [_dot_nt(c["dvnew"], c["vb"]) + _dot_nt(c["dw"], c["kbg"]) for c in chains]
        dvbs = [_dot_tn(c["tm"], c["dvnew"]) for c in chains]
        dkbgs = [_dot_tn(c["tm"], c["dw"]) for c in chains]
        tdts = [_dot3_nt(dt, c["tm"]) for dt, c in zip(dts, chains)]
        dls = [jnp.where(c["masks"][3], -_dot3_tn(c["tm"], tdt), 0.0) for tdt, c in zip(tdts, chains)]
        das = [dl * c["decay"] for dl, c in zip(dls, chains)]
        dqks = [jnp.where(c["masks"][2], di, 0.0) * c["decay"] for di, c in zip(dintras, chains)]
        dkb1 = [_dot(da, c["k"]) for da, c in zip(das, chains)]
        dk1 = [_dot_tn(da, c["kb"]) for da, c in zip(das, chains)]
        dk2 = [_dot_tn(dqk, c["q"]) for dqk, c in zip(dqks, chains)]
        dq1 = [_dot(dqk, c["k"]) for dqk, c in zip(dqks, chains)]
        grads, mms, p_gs, p_betas, p_kds = [], [], [], [], []
        for n, c in enumerate(chains):
            incl = c["masks"][2]
            dkb = dkb1[n] + dkbgs[n] * c["eg"]
            kd = c["dkdec"] * c["kdec"]
            mms.append((dls[n] * c["amat"] + jnp.where(incl, dintras[n], 0.0) * c["qk"]) * c["decay"])
            p_gs.append(c["dqg"] * c["qg"] - kd + dkbgs[n] * c["kbg"])
            p_betas.append(dkb * c["k"] + dvbs[n] * c["v"])
            p_kds.append(kd)
            grads.append((dq1[n] + c["dqg"] * c["eg"],
                          dk1[n] + dk2[n] + c["dkdec"] * c["ek"] + dkb * c["bcol"],
                          dvbs[n] * c["bcol"]))
        row_sums = [jnp.sum(mm, axis=1, keepdims=True) for mm in mms]
        col_sums = [jnp.sum(mm, axis=0, keepdims=True) for mm in mms]
        g_sums = [jnp.sum(pg, axis=1, keepdims=True) for pg in p_gs]
        dbetas = [jnp.sum(pb, axis=1, keepdims=True) for pb in p_betas]
        kd_tots = [jnp.sum(jnp.sum(pk, axis=1, keepdims=True), axis=0, keepdims=True) for pk in p_kds]
        dgcs = [rs - _row_to_col(cs, *c["masks"][0:2]) + gs for rs, cs, gs, c in zip(row_sums, col_sums, g_sums, chains)]
        dgrs = [_col_to_row(dgc, *c["masks"][0:2]) for dgc, c in zip(dgcs, chains)]
        draws = [jnp.sum(jnp.where(jnp.logical_not(c["masks"][3]), dgr, 0.0), axis=1, keepdims=True) + c["dglast"] + kt
                 for dgr, kt, c in zip(dgrs, kd_tots, chains)]
        for c, draw, dbeta in zip(chains, draws, dbetas):
            ch = c["ch"]
            dgates[c["cc"]] = dgates[c["cc"]] + jnp.where(lane == ch, draw, 0.0) + jnp.where(lane == 8 + ch, dbeta, 0.0)
        for cc in range(LOCAL_CHUNKS):
            rows = slice(cc * CHUNK, (cc + 1) * CHUNK)
            for h in range(GDN_HEADS):
                for part in range(3):
                    cols = slice(part * hd + h * GDN_DIM, part * hd + (h + 1) * GDN_DIM)
                    dx_ref[rows, cols] = grads[cc * N_CHAINS + h][part] + grads[cc * N_CHAINS + GDN_HEADS + h][part]
            dg_ref[rows, :] = dgates[cc]

    lc = LOCAL_CHUNKS
    all8 = lambda rows, cols: pl.BlockSpec((lc, N_CHAINS, rows, cols), lambda n: (n, 0, 0, 0))
    own4 = lambda rows, cols: pl.BlockSpec((lc, GDN_HEADS, rows, cols), lambda n: (n, 0, 0, 0))
    row4 = pl.BlockSpec((lc, GDN_HEADS, LANES), lambda n: (n, 0, 0))
    vn_f, vn_b = saved["vn"]
    dvn_f, dvn_b, dw_f, dw_b, dqg_f, dqg_b, dkd_f, dkd_b, dgl_f, dgl_b = scan
    return _grid_call(
        body, "gdn_local_bwd", nc // lc,
        [pl.BlockSpec((lc * CHUNK, 3 * hd), lambda n: (n, 0)), pl.BlockSpec((lc * CHUNK, LANES), lambda n: (n, 0)),
         pl.BlockSpec((lc, 16, CHUNK), lambda n: (n, 0, 0)), pl.BlockSpec((lc * CHUNK, hd), lambda n: (n, 0)),
         all8(CHUNK, CHUNK)] + [own4(CHUNK, GDN_DIM)] * 10 + [row4, row4],
        [pl.BlockSpec((lc * CHUNK, 3 * hd), lambda n: (n, 0)), pl.BlockSpec((lc * CHUNK, LANES), lambda n: (n, 0))],
        [jax.ShapeDtypeStruct((t, 3 * hd), F32), jax.ShapeDtypeStruct((t, LANES), F32)],
        (qkvc, gb, gbt, do, saved["tm"], vn_f, vn_b, dvn_f, dvn_b, dw_f, dw_b, dqg_f, dqg_b, dkd_f, dkd_b, dgl_f, dgl_b),
        exchange=exchange)


def _gdn_post_fwd(of, ob, z, gw, tm):
    t, hd = of.shape

    def body(of_ref, ob_ref, z_ref, w_ref, o_ref):
        for h in range(GDN_HEADS):
            cols = slice(h * GDN_DIM, (h + 1) * GDN_DIM)
            o = of_ref[:, cols] + ob_ref[:, cols]
            zv = z_ref[:, cols]
            o_ref[:, cols] = (o * _rstd(o) * w_ref[...] * (zv * _sigmoid(zv))).astype(BF16)

    row = pl.BlockSpec((tm, hd), lambda i: (i, 0))
    return pl.pallas_call(
        body, name="gdn_post_fwd", grid=(t // tm,),
        in_specs=[row, row, row, _resident((1, GDN_DIM))],
        out_specs=row, out_shape=jax.ShapeDtypeStruct((t, hd), BF16),
        compiler_params=_params(("arbitrary",), VMEM_LIMIT),
    )(of, ob, z, gw)


def _gdn_post_bwd(doa, of, ob, z, gw, tm):
    t, hd = of.shape

    def body(d_ref, of_ref, ob_ref, z_ref, w_ref, do_ref, dz_ref, dw_ref):
        @pl.when(pl.program_id(0) == 0)
        def _():
            dw_ref[...] = jnp.zeros_like(dw_ref)

        dw = jnp.zeros((1, GDN_DIM), F32)
        for h in range(GDN_HEADS):
            cols = slice(h * GDN_DIM, (h + 1) * GDN_DIM)
            o = of_ref[:, cols] + ob_ref[:, cols]
            zv = z_ref[:, cols]
            dv = d_ref[:, cols]
            r = _rstd(o)
            sg = _sigmoid(zv)
            on = o * r * w_ref[...]
            dz_ref[:, cols] = (dv * on * (sg * (1.0 + zv * (1.0 - sg)))).astype(BF16)
            dxr, dwh = _rms_bwd(o, r, w_ref[...], dv * (zv * sg))
            do_ref[:, cols] = dxr
            dw = dw + dwh
        dw_ref[...] += dw

    row = pl.BlockSpec((tm, hd), lambda i: (i, 0))
    return pl.pallas_call(
        body, name="gdn_post_bwd", grid=(t // tm,),
        in_specs=[row, row, row, row, _resident((1, GDN_DIM))],
        out_specs=[row, row, pl.BlockSpec((1, GDN_DIM), lambda i: (0, 0))],
        out_shape=[jax.ShapeDtypeStruct((t, hd), F32), jax.ShapeDtypeStruct((t, hd), BF16),
                   jax.ShapeDtypeStruct((1, GDN_DIM), F32)],
        compiler_params=_params(("arbitrary",), VMEM_LIMIT),
    )(doa, of, ob, z, gw)


SWA_W = SWA_HEADS * SWA_DIM
QBLK = 128
KWIN = QBLK + 2 * RADIUS
WIN_OFFSETS = (0, RADIUS, 2 * RADIUS)


def _t5_bucket(rel):
    nb = REL_BUCKETS // 2
    bucket = (rel > 0).astype(np.int32) * nb
    n = np.abs(rel)
    max_exact = nb // 2
    large = max_exact + (np.log(np.maximum(n, 1) / max_exact)
                         / math.log(REL_MAX_DISTANCE / max_exact) * (nb - max_exact)).astype(np.int32)
    large = np.minimum(large, nb - 1)
    return (bucket + np.where(n < max_exact, n, large)).astype(np.int32)


def _band_tables(dilation):
    a = np.arange(QBLK)
    b = np.arange(KWIN)
    rel = np.stack([b[None, :] - w0 - a[:, None] for w0 in WIN_OFFSETS])
    return np.where(np.abs(rel) <= RADIUS, _t5_bucket(rel * dilation), -1).astype(np.int32)


BAND_CELLS = len(WIN_OFFSETS) * QBLK * KWIN
BIAS_TILE = BAND_CELLS // 3


def _band_index():
    return jnp.asarray(np.concatenate([_band_tables(d).reshape(-1) for _, d in PATTERNS])[None, :])


def _onehot(idx, dtype):
    return (lax.broadcasted_iota(jnp.int32, (REL_BUCKETS, idx.shape[1]), 0) == idx).astype(dtype)


def _bias_tables(rel_bias, idx, tk):
    n = idx.shape[1]

    def body(rb_ref, i_ref, o_ref):
        iv = i_ref[...]
        o_ref[...] = jnp.where(iv < 0, NEG_BIG, _dot_hi(rb_ref[...], _onehot(iv, F32)))

    return pl.pallas_call(
        body, name="bias_tables", grid=(n // tk,),
        in_specs=[_resident((SWA_HEADS, REL_BUCKETS)), pl.BlockSpec((1, tk), lambda k: (0, k))],
        out_specs=pl.BlockSpec((SWA_HEADS, tk), lambda k: (0, k)),
        out_shape=jax.ShapeDtypeStruct((SWA_HEADS, n), F32),
        compiler_params=_params(("arbitrary",), VMEM_LIMIT),
    )(rel_bias.T, idx)


def _head_mean(x2, bd_ref):
    bd = bd_ref[...]
    rest, acc = x2, None
    for _ in range(3):
        piece = rest.astype(BF16)
        part = jnp.dot(piece, bd, preferred_element_type=F32)
        acc = part if acc is None else acc + part
        rest = rest - piece.astype(F32)
    return acc


VIEW_DILATIONS = tuple(d for _, d in PATTERNS if d > 1)


def _view_spec(tm, d):
    return pl.BlockSpec((tm // d, d * SWA_W), lambda i: (i, 0))


def _view_shape(t, d, dtype):
    return jax.ShapeDtypeStruct((t // d, d * SWA_W), dtype)


N_GROUPS = SWA_W // LANES


def _to_view(src_ref, idx, dst_ref, d, rows):
    for r in range(d):
        for g in range(N_GROUPS):
            cols = slice(r * SWA_W + g * LANES, r * SWA_W + (g + 1) * LANES)
            dst_ref[:, cols] = src_ref[idx, g, pl.ds(r, rows // d, stride=d), :].astype(dst_ref.dtype)


def _from_view(src_ref, dst_ref, idx, d, rows):
    for r in range(d):
        for g in range(N_GROUPS):
            cols = slice(r * SWA_W + g * LANES, r * SWA_W + (g + 1) * LANES)
            dst_ref[idx, g, pl.ds(r, rows // d, stride=d), :] = src_ref[:, cols]


def _swa_prep_fwd(qkvb, qw, kw, bd, tm):
    t = qkvb.shape[0]

    def body(x_ref, qw_ref, kw_ref, bd_ref, *rest):
        outs, sc = rest[:-1], rest[-1]
        for gidx in range(N_GROUPS):
            cols = slice(gidx * LANES, (gidx + 1) * LANES)
            xq = x_ref[:, cols]
            sc[0, gidx] = xq * lax.rsqrt(_head_mean(xq * xq, bd_ref) + EPS) * qw_ref[:, cols] * (SWA_DIM ** -0.5)
            xk = x_ref[:, SWA_W + gidx * LANES:SWA_W + (gidx + 1) * LANES]
            sc[1, gidx] = xk * lax.rsqrt(_head_mean(xk * xk, bd_ref) + EPS) * kw_ref[:, cols]
            sc[2, gidx] = x_ref[:, 2 * SWA_W + gidx * LANES:2 * SWA_W + (gidx + 1) * LANES]
            for i in range(3):
                outs[i][:, cols] = sc[i, gidx].astype(BF16)
        for i in range(3):
            for n, d in enumerate(VIEW_DILATIONS):
                _to_view(sc, i, outs[3 * (n + 1) + i], d, tm)

    return pl.pallas_call(
        body, name="swa_prep_fwd", grid=(t // tm,),
        in_specs=[pl.BlockSpec((tm, 3 * SWA_W), lambda i: (i, 0)), _resident((1, SWA_W)), _resident((1, SWA_W)),
                  _resident((LANES, LANES))],
        out_specs=[_view_spec(tm, d) for d in (1,) + VIEW_DILATIONS for _ in range(3)],
        out_shape=[_view_shape(t, d, BF16) for d in (1,) + VIEW_DILATIONS for _ in range(3)],
        scratch_shapes=[pltpu.VMEM((3, N_GROUPS, tm, LANES), F32)],
        compiler_params=_params(("arbitrary",), VMEM_LIMIT),
    )(qkvb, qw, kw, bd)


def _swa_prep_bwd(qkvb, qw, kw, bd, grads, tm):
    t = qkvb.shape[0]

    def body(x_ref, qw_ref, kw_ref, bd_ref, *rest):
        parts, (dx_ref, dqw_ref, dkw_ref, sc) = rest[:9], rest[9:]
        @pl.when(pl.program_id(0) == 0)
        def _():
            dqw_ref[...] = jnp.zeros_like(dqw_ref)
            dkw_ref[...] = jnp.zeros_like(dkw_ref)

        for i in range(3):
            for n, d in enumerate(VIEW_DILATIONS):
                _from_view(parts[3 * (n + 1) + i], sc, 2 * i + n, d, tm)
        for gidx in range(N_GROUPS):
            cols = slice(gidx * LANES, (gidx + 1) * LANES)
            for i, base, w_ref, dw_ref, scale in ((0, 0, qw_ref, dqw_ref, SWA_DIM ** -0.5),
                                                  (1, SWA_W, kw_ref, dkw_ref, 1.0)):
                xv = x_ref[:, base + gidx * LANES:base + (gidx + 1) * LANES]
                dy = (parts[i][:, cols] + sc[2 * i, gidx] + sc[2 * i + 1, gidx]) * scale
                r = lax.rsqrt(_head_mean(xv * xv, bd_ref) + EPS)
                xhat = xv * r
                dxh = dy * w_ref[:, cols]
                dx = r * (dxh - xhat * _head_mean(dxh * xhat, bd_ref))
                dx_ref[:, base + gidx * LANES:base + (gidx + 1) * LANES] = dx.astype(BF16)
                dw_ref[:, cols] += jnp.sum(dy * xhat, axis=0, keepdims=True)
            dx_ref[:, 2 * SWA_W + gidx * LANES:2 * SWA_W + (gidx + 1) * LANES] = (
                parts[2][:, cols] + sc[4, gidx] + sc[5, gidx]).astype(BF16)

    wrow = pl.BlockSpec((1, SWA_W), lambda i: (0, 0))
    return pl.pallas_call(
        body, name="swa_prep_bwd", grid=(t // tm,),
        in_specs=[pl.BlockSpec((tm, 3 * SWA_W), lambda i: (i, 0)), _resident((1, SWA_W)), _resident((1, SWA_W)),
                  _resident((LANES, LANES))] + [_view_spec(tm, d) for d in (1,) + VIEW_DILATIONS for _ in range(3)],
        out_specs=[pl.BlockSpec((tm, 3 * SWA_W), lambda i: (i, 0)), wrow, wrow],
        out_shape=[jax.ShapeDtypeStruct((t, 3 * SWA_W), BF16), jax.ShapeDtypeStruct((1, SWA_W), F32),
                   jax.ShapeDtypeStruct((1, SWA_W), F32)],
        scratch_shapes=[pltpu.VMEM((6, N_GROUPS, tm, LANES), F32)],
        compiler_params=_params(("arbitrary",), VMEM_LIMIT),
    )(qkvb, qw, kw, bd, *grads)


def _aligned(v, m):
    return v if isinstance(v, int) else pl.multiple_of(v, m)


BAND_GROUP = 2


def _band_loop(nsub, length, step, group=BAND_GROUP):
    step([(0, 0)], 0)
    if nsub > 2:
        assert (nsub - 2) % group == 0

        def inner(i, carry):
            s0 = 1 + i * group
            step([(s0 + e, pl.multiple_of((s0 + e) * QBLK - RADIUS, RADIUS)) for e in range(group)], 1)
            return carry
        lax.fori_loop(0, (nsub - 2) // group, inner, 0)
    step([(nsub - 1, length - KWIN)], 2)


def _head_select(lane, a0, a1):
    return jnp.where(lane < SWA_DIM, a0, a1)


def _swa_fwd(qv, kv, vv, bias, dilation, name):
    length = qv.shape[0]
    nsub = length // QBLK
    assert nsub >= 2 and length % QBLK == 0

    def body(q_ref, k_ref, v_ref, b_ref, o_ref, l_ref):
        lane = lax.broadcasted_iota(jnp.int32, (QBLK, LANES), 1)

        def step(blocks, var):
            items = []
            for s, ws in blocks:
                rows = pl.ds(_aligned(s * QBLK, QBLK), QBLK)
                q, kk, vw = q_ref[rows, :], k_ref[pl.ds(ws, KWIN), :], v_ref[pl.ds(ws, KWIN), :]
                for hh in range(2):
                    items.append((hh, jnp.where((lane < SWA_DIM) == (hh == 0), q, jnp.zeros_like(q)), kk, vw))
            lgs = [_dot_nt(qh, kk) + b_ref[hh, var] for hh, qh, kk, _ in items]
            ms = [jnp.max(lg, axis=-1, keepdims=True) for lg in lgs]
            ps = [jnp.exp(lg - m) for lg, m in zip(lgs, ms)]
            dens = [jnp.sum(p, axis=-1, keepdims=True) for p in ps]
            pvs = [_dot(p, it[3]) for p, it in zip(ps, items)]
            for n, (s, _) in enumerate(blocks):
                rows = pl.ds(_aligned(s * QBLK, QBLK), QBLK)
                o0, o1 = (pvs[2 * n + hh] / dens[2 * n + hh] for hh in range(2))
                l0, l1 = (ms[2 * n + hh] + jnp.log(dens[2 * n + hh]) for hh in range(2))
                o_ref[rows, :] = _head_select(lane, o0, o1)
                l_ref[rows, :] = _head_select(lane, l0, l1)

        _band_loop(nsub, length, step)

    blk = pl.BlockSpec((length, LANES), lambda hp, r: (0, r * (SWA_W // LANES) + hp))
    shp = jax.ShapeDtypeStruct(qv.shape, F32)
    return pl.pallas_call(
        body, name=name, grid=(SWA_W // LANES, dilation),
        in_specs=[blk, blk, blk, pl.BlockSpec((2, 3, QBLK, KWIN), lambda hp, r: (hp, 0, 0, 0))],
        out_specs=[blk, blk], out_shape=[shp, shp],
        compiler_params=_params(("arbitrary", "arbitrary"), VMEM_LIMIT),
    )(qv, kv, vv, bias)


def _swa_combine(os_, ls_, tm):
    t = os_[0].shape[0]

    def body(o0, o1, o2, l0, l1, l2, o_ref, ob_ref, la_ref, lb_ref, lc_ref, sc):
        for n, d in enumerate(VIEW_DILATIONS):
            _from_view((o1, o2)[n], sc, n, d, tm)
            _from_view((l1, l2)[n], sc, 2 + n, d, tm)
        for g in range(N_GROUPS):
            cols = slice(g * LANES, (g + 1) * LANES)
            la, lb, lc = l0[:, cols], sc[2, g], sc[3, g]
            m = jnp.maximum(jnp.maximum(la, lb), lc)
            tot = m + jnp.log(jnp.exp(la - m) + jnp.exp(lb - m) + jnp.exp(lc - m))
            o = jnp.exp(la - tot) * o0[:, cols] + jnp.exp(lb - tot) * sc[0, g] + jnp.exp(lc - tot) * sc[1, g]
            o_ref[:, cols] = o
            ob_ref[:, cols] = o.astype(BF16)
            la_ref[:, cols] = tot
            sc[4, g] = tot
        for n, d in enumerate(VIEW_DILATIONS):
            _to_view(sc, 4, (lb_ref, lc_ref)[n], d, tm)

    specs = [_view_spec(tm, d) for d in (1,) + VIEW_DILATIONS]
    return pl.pallas_call(
        body, name="swa_combine", grid=(t // tm,), in_specs=specs + specs, out_specs=[specs[0], specs[0]] + specs,
        out_shape=[jax.ShapeDtypeStruct((t, SWA_W), F32), jax.ShapeDtypeStruct((t, SWA_W), BF16)]
                  + [_view_shape(t, d, F32) for d in (1,) + VIEW_DILATIONS],
        scratch_shapes=[pltpu.VMEM((5, N_GROUPS, tm, LANES), F32)],
        compiler_params=_params(("arbitrary",), VMEM_LIMIT),
    )(*os_, *ls_)


def _swa_bwd_prep(do, o, bd, tm):
    t = do.shape[0]

    def body(d_ref, o_ref, bd_ref, dd1, dd4, dd16, db1, db4, db16, sc):
        for gidx in range(N_GROUPS):
            cols = slice(gidx * LANES, (gidx + 1) * LANES)
            dv = d_ref[:, cols]
            dd = _head_mean(dv * o_ref[:, cols], bd_ref) * float(SWA_DIM)
            sc[0, gidx] = dd
            sc[1, gidx] = dv
            dd1[:, cols] = dd
            db1[:, cols] = dv.astype(BF16)
        for n, d in enumerate(VIEW_DILATIONS):
            _to_view(sc, 0, (dd4, dd16)[n], d, tm)
            _to_view(sc, 1, (db4, db16)[n], d, tm)

    specs = [_view_spec(tm, d) for d in (1,) + VIEW_DILATIONS]
    return pl.pallas_call(
        body, name="swa_bwd_prep", grid=(t // tm,), in_specs=[specs[0], specs[0], _resident((LANES, LANES))],
        out_specs=specs + specs,
        out_shape=[_view_shape(t, d, F32) for d in (1,) + VIEW_DILATIONS]
                  + [_view_shape(t, d, BF16) for d in (1,) + VIEW_DILATIONS],
        scratch_shapes=[pltpu.VMEM((2, N_GROUPS, tm, LANES), F32)],
        compiler_params=_params(("arbitrary",), VMEM_LIMIT),
    )(do, o, bd)


def _swa_bwd(qv, kv, vv, dov, lv, ddv, bias_a, dilation, name):
    length = qv.shape[0]
    nsub = length // QBLK
    single = pl.Buffered(1) if dilation == 1 else None

    def body(q_ref, k_ref, v_ref, do_ref, l_ref, dd_ref, ba_ref, dq_ref, dk_ref, dv_ref, db_ref):
        @pl.when(pl.program_id(1) == 0)
        def _():
            db_ref[...] = jnp.zeros_like(db_ref)

        lane = lax.broadcasted_iota(jnp.int32, (QBLK, LANES), 1)
        lanew = lax.broadcasted_iota(jnp.int32, (KWIN, LANES), 1)

        def step(blocks, var):
            items = []
            for s, ws in blocks:
                rows = pl.ds(_aligned(s * QBLK, QBLK), QBLK)
                win = pl.ds(ws, KWIN)
                q, dov_ = q_ref[rows, :], do_ref[rows, :]
                kk, vw = k_ref[win, :], v_ref[win, :]
                lse, dd = l_ref[rows, :], dd_ref[rows, :]
                for hh in range(2):
                    mine = (lane < SWA_DIM) == (hh == 0)
                    col = slice(hh * SWA_DIM, hh * SWA_DIM + 1)
                    items.append((hh, jnp.where(mine, q, jnp.zeros_like(q)), jnp.where(mine, dov_, jnp.zeros_like(dov_)),
                                  kk, vw, lse[:, col], dd[:, col], q, dov_))
            lgs = [_dot_nt(it[1], it[3]) + ba_ref[it[0], var] for it in items]
            dps = [_dot_nt(it[2], it[4]) for it in items]
            ps = [jnp.exp(lg - it[5]) for lg, it in zip(lgs, items)]
            dss = [p * (dp - it[6]) for p, dp, it in zip(ps, dps, items)]
            dqs = [_dot(ds, it[3]) for ds, it in zip(dss, items)]
            dks = [_dot_tn(ds, it[7]) for ds, it in zip(dss, items)]
            dvs = [_dot_tn(p, it[8]) for p, it in zip(ps, items)]
            for n, (s, ws) in enumerate(blocks):
                rows = pl.ds(_aligned(s * QBLK, QBLK), QBLK)
                win = pl.ds(ws, KWIN)
                dq_ref[rows, :] = _head_select(lane, dqs[2 * n], dqs[2 * n + 1])
                dk_ref[win, :] += _head_select(lanew, dks[2 * n], dks[2 * n + 1])
                dv_ref[win, :] += _head_select(lanew, dvs[2 * n], dvs[2 * n + 1])
            for hh in range(2):
                tot = dss[hh]
                for n in range(1, len(blocks)):
                    tot = tot + dss[2 * n + hh]
                db_ref[hh, var] += tot

        dk_ref[...] = jnp.zeros_like(dk_ref)
        dv_ref[...] = jnp.zeros_like(dv_ref)
        _band_loop(nsub, length, step)

    imap = lambda hp, r: (0, r * (SWA_W // LANES) + hp)
    blk_in = pl.BlockSpec((length, LANES), imap, pipeline_mode=single)
    blk_out = pl.BlockSpec((length, LANES), imap)
    shp = jax.ShapeDtypeStruct(qv.shape, F32)
    return pl.pallas_call(
        body, name=name, grid=(SWA_W // LANES, dilation),
        in_specs=[blk_in] * 6 + [pl.BlockSpec((2, 3, QBLK, KWIN), lambda hp, r: (hp, 0, 0, 0))],
        out_specs=[blk_out, blk_out, blk_out, pl.BlockSpec((2, 3, QBLK, KWIN), lambda hp, r: (hp, 0, 0, 0))],
        out_shape=[shp, shp, shp, jax.ShapeDtypeStruct((SWA_HEADS, 3, QBLK, KWIN), F32)],
        compiler_params=_params(("arbitrary", "arbitrary"), VMEM_LIMIT),
    )(qv, kv, vv, dov, lv, ddv, bias_a)


def _bias_grad(ds2, idx, tk):
    n = ds2.shape[1]
    nk = n // tk

    def body(a_ref, i_ref, o_ref):
        @pl.when(pl.program_id(0) == 0)
        def _():
            o_ref[...] = jnp.zeros_like(o_ref)

        oh = _onehot(i_ref[...], BF16)
        rest = a_ref[...]
        acc = jnp.zeros((SWA_HEADS, REL_BUCKETS), F32)
        for _ in range(3):
            piece = rest.astype(BF16)
            acc = acc + _dot_nt(piece, oh)
            rest = rest - piece.astype(F32)
        o_ref[...] += acc

    return pl.pallas_call(
        body, name="bias_grad", grid=(nk,),
        in_specs=[pl.BlockSpec((SWA_HEADS, tk), lambda k: (0, k)), pl.BlockSpec((1, tk), lambda k: (0, k))],
        out_specs=pl.BlockSpec((SWA_HEADS, REL_BUCKETS), lambda k: (0, 0)),
        out_shape=jax.ShapeDtypeStruct((SWA_HEADS, REL_BUCKETS), F32),
        compiler_params=_params(("arbitrary",), VMEM_LIMIT),
    )(ds2, idx)


def _swa_branch_fwd(qkvb, qw_t, kw_t, rel_bias, bd, tm):
    qkv = _swa_prep_fwd(qkvb, qw_t, kw_t, bd, tm)
    tables = _bias_tables(rel_bias, _band_index(), BIAS_TILE)
    os_, ls_, tabs = [], [], []
    for n, (_, d) in enumerate(PATTERNS):
        bias = tables[:, n * BAND_CELLS:(n + 1) * BAND_CELLS].reshape(SWA_HEADS, len(WIN_OFFSETS), QBLK, KWIN)
        o_p, l_p = _swa_fwd(*qkv[3 * n:3 * n + 3], bias, d, f"swa_fwd_d{d}")
        os_.append(o_p)
        ls_.append(l_p)
        tabs.append(bias)
    o, o16, *lses = _swa_combine(os_, ls_, tm)
    return o, o16, (qkv, lses, tabs)


def _swa_branch_bwd(do, o, saved, qkvb, qw_t, kw_t, bd, tm):
    qkv, lses, tabs = saved
    prep = _swa_bwd_prep(do, o, bd, tm)
    grads, dss = [], []
    for n, ((_, d), bias) in enumerate(zip(PATTERNS, tabs)):
        dq, dk, dv, ds = _swa_bwd(*qkv[3 * n:3 * n + 3], prep[3 + n], lses[n], prep[n], bias, d, f"swa_bwd_d{d}")
        grads += [dq, dk, dv]
        dss.append(ds.reshape(SWA_HEADS, -1))
    dqkvb, dqw, dkw = _swa_prep_bwd(qkvb, qw_t, kw_t, bd, grads, tm)
    dbias = _bias_grad(jnp.concatenate(dss, axis=1), _band_index(), BIAS_TILE)
    fold = lambda w: jnp.sum(w.reshape(SWA_HEADS, SWA_DIM), axis=0)
    return dqkvb, fold(dqw), fold(dkw), dbias.T


def _mesh_pos():
    return lax.axis_index("x"), lax.axis_index("y"), lax.axis_index("c")


def _other_chips(x, y):
    return [(1 - x, y), (x, 1 - y), (1 - x, 1 - y)]


def _remote(src, dst, send_sem, recv_sem, device):
    return pltpu.make_async_remote_copy(src_ref=src, dst_ref=dst, send_sem=send_sem, recv_sem=recv_sem,
                                        device_id=device, device_id_type=MESH)


def _split_axis(shape2):
    return 0 if (shape2[0] // 2) % 16 == 0 else 1


def _half_index(shape2, c):
    axis = _split_axis(shape2)
    h = shape2[axis] // 2
    return (pl.ds(c * h, h), slice(None)) if axis == 0 else (slice(None), pl.ds(c * h, h))


def _all_gather(xs):
    n = len(xs)

    def body(*refs):
        ins, outs = refs[:n], refs[n:2 * n]
        send_sems, recv_sems = refs[2 * n:]
        x, y, c = _mesh_pos()
        me = 2 * x + y
        chips = _other_chips(x, y)
        halves = []
        sends = []
        for a in range(n):
            h = ins[a].shape[0] // 2
            mine, other = pl.ds(c * h, h), pl.ds((1 - c) * h, h)
            halves.append((mine, other))
            own = _remote(ins[a], outs[a].at[me], send_sems.at[a, 6], recv_sems.at[a, 6], (x, y, 1 - c))
            own.start()
            sends.append(own)
            for j, chip in enumerate(chips):
                cp = _remote(ins[a].at[mine], outs[a].at[me, mine], send_sems.at[a, j], recv_sems.at[a, j], (*chip, c))
                cp.start()
                sends.append(cp)
        for a in range(n):
            mine, _ = halves[a]
            for j, chip in enumerate(chips):
                src = 2 * chip[0] + chip[1]
                landed = outs[a].at[src, mine]
                _remote(landed, landed, send_sems.at[a, j], recv_sems.at[a, j], (x, y, c)).wait_recv()
                fwd = _remote(landed, landed, send_sems.at[a, 3 + j], recv_sems.at[a, 3 + j], (x, y, 1 - c))
                fwd.start()
                sends.append(fwd)
        for a in range(n):
            _, other = halves[a]
            for j, chip in enumerate(chips):
                src = 2 * chip[0] + chip[1]
                landed = outs[a].at[src, other]
                _remote(landed, landed, send_sems.at[a, 3 + j], recv_sems.at[a, 3 + j], (x, y, c)).wait_recv()
            mine_slot = outs[a].at[me]
            _remote(mine_slot, mine_slot, send_sems.at[a, 6], recv_sems.at[a, 6], (x, y, c)).wait_recv()
        for cp in sends:
            cp.wait_send()

    return list(pl.pallas_call(
        body, name="all_gather_weights",
        in_specs=[ANY] * n, out_specs=[ANY] * n,
        out_shape=[jax.ShapeDtypeStruct((N_SHARDS,) + a.shape, a.dtype) for a in xs],
        scratch_shapes=[pltpu.SemaphoreType.DMA((n, 7)), pltpu.SemaphoreType.DMA((n, 7))],
    )(*xs))


def _rs_pair(gs):
    n = len(gs)

    def body(*refs):
        ins, lands = refs[:n], refs[n:2 * n]
        send_sems, recv_sems = refs[2 * n:]
        x, y, c = _mesh_pos()
        cps = []
        for a in range(n):
            theirs = (slice(None),) + _half_index(ins[a].shape[1:], 1 - c)
            cp = _remote(ins[a].at[theirs], lands[a], send_sems.at[a], recv_sems.at[a], (x, y, 1 - c))
            cp.start()
            cps.append(cp)
        for cp in cps:
            cp.wait()

    def half_shape(g):
        dims = list(g.shape)
        dims[1 + _split_axis(g.shape[1:])] //= 2
        return tuple(dims)

    return list(pl.pallas_call(
        body, name="rs_pair", in_specs=[ANY] * n, out_specs=[ANY] * n,
        out_shape=[jax.ShapeDtypeStruct(half_shape(g), g.dtype) for g in gs],
        scratch_shapes=[pltpu.SemaphoreType.DMA((n,)), pltpu.SemaphoreType.DMA((n,))],
    )(*gs))


def _pair_exchange(gs):
    def copies(cin, cout, send_sems, recv_sems):
        x, y, c = _mesh_pos()
        return [_remote(g.at[(slice(None),) + _half_index(g.shape[1:], 1 - c)], land, send_sems.at[a, 0],
                        recv_sems.at[a, 0], (x, y, 1 - c)) for a, (g, land) in enumerate(zip(cin, cout))]

    def start(*refs):
        for cp in copies(*refs):
            cp.start()

    def finish(*refs):
        for cp in copies(*refs):
            cp.wait()

    def half_shape(g):
        dims = list(g.shape)
        dims[1 + _split_axis(g.shape[1:])] //= 2
        return tuple(dims)

    return _Exchange(tuple(gs), tuple(jax.ShapeDtypeStruct(half_shape(g), g.dtype) for g in gs), start, finish)


def _rs_chips(ss):
    n = len(ss)

    def body(*refs):
        ins, outs = refs[:n], refs[n:2 * n]
        send_sems, recv_sems = refs[2 * n:]
        x, y, c = _mesh_pos()
        me = 2 * x + y
        chips = _other_chips(x, y)
        cps = []
        for a in range(n):
            for j, chip in enumerate(chips):
                dst_chip = 2 * chip[0] + chip[1]
                cp = _remote(ins[a].at[dst_chip], outs[a].at[me], send_sems.at[a, j], recv_sems.at[a, j], (*chip, c))
                cp.start()
                cps.append(cp)
        for a in range(n):
            for j, chip in enumerate(chips):
                src = 2 * chip[0] + chip[1]
                _remote(outs[a].at[src], outs[a].at[src], send_sems.at[a, j], recv_sems.at[a, j], (x, y, c)).wait_recv()
        for cp in cps:
            cp.wait_send()

    return list(pl.pallas_call(
        body, name="rs_chips", in_specs=[ANY] * n, out_specs=[ANY] * n,
        out_shape=[jax.ShapeDtypeStruct(s.shape, s.dtype) for s in ss],
        scratch_shapes=[pltpu.SemaphoreType.DMA((n, 3)), pltpu.SemaphoreType.DMA((n, 3))],
    )(*ss))


def _rs_join(fs, axes):
    n = len(fs)

    def whole(f, axis):
        dims = list(f.shape)
        dims[axis] *= 2
        return tuple(dims)

    def body(*refs):
        ins, outs = refs[:n], refs[n:2 * n]
        send_sems, recv_sems = refs[2 * n:]
        x, y, c = _mesh_pos()
        cps = []
        for a in range(n):
            h = ins[a].shape[axes[a]]
            mine = (pl.ds(c * h, h), slice(None)) if axes[a] == 0 else (slice(None), pl.ds(c * h, h))
            cp = _remote(ins[a], outs[a].at[mine], send_sems.at[a], recv_sems.at[a], (x, y, 1 - c))
            cp.start()
            cps.append(cp)
        for cp in cps:
            cp.wait()

    outs = pl.pallas_call(
        body, name="rs_join", in_specs=[ANY] * n, out_specs=[ANY] * n,
        out_shape=[jax.ShapeDtypeStruct(whole(f, ax), f.dtype) for f, ax in zip(fs, axes)],
        scratch_shapes=[pltpu.SemaphoreType.DMA((n,)), pltpu.SemaphoreType.DMA((n,))],
    )(*fs)
    c = lax.axis_index("c")
    return [lax.dynamic_update_slice_in_dim(o, f, c * f.shape[ax], ax) for o, f, ax in zip(outs, fs, axes)]


def _gather_exchange(xs):
    def start(cin, cout, send_sems, recv_sems):
        x, y, c = _mesh_pos()
        me = 2 * x + y
        for a, (src, dst) in enumerate(zip(cin, cout)):
            mine = _half_index(src.shape, c)
            for j, chip in enumerate(_other_chips(x, y)):
                _remote(src.at[mine], dst.at[(me,) + mine], send_sems.at[a, j], recv_sems.at[a, j], (*chip, c)).start()
            _remote(src, dst.at[me], send_sems.at[a, 3], recv_sems.at[a, 3], (x, y, 1 - c)).start()

    def finish(cin, cout, send_sems, recv_sems):
        x, y, c = _mesh_pos()
        for a, dst in enumerate(cout):
            for j, chip in enumerate(_other_chips(x, y)):
                landed = dst.at[(2 * chip[0] + chip[1],) + _half_index(dst.shape[1:], c)]
                _remote(landed, landed, send_sems.at[a, j], recv_sems.at[a, j], (x, y, c)).wait()
            own = dst.at[2 * x + y]
            _remote(own, own, send_sems.at[a, 3], recv_sems.at[a, 3], (x, y, c)).wait()

    return _Exchange(tuple(xs), tuple(jax.ShapeDtypeStruct((N_SHARDS,) + a.shape, a.dtype) for a in xs), start, finish)


def _gather_forward(gs):
    n = len(gs)

    def body(*refs):
        outs = refs[n:2 * n]
        send_sems, recv_sems = refs[2 * n:]
        x, y, c = _mesh_pos()
        chips = _other_chips(x, y)
        cps = []
        for a in range(n):
            for j, chip in enumerate(chips):
                landed = outs[a].at[(2 * chip[0] + chip[1],) + _half_index(outs[a].shape[1:], c)]
                cp = _remote(landed, landed, send_sems.at[a, j], recv_sems.at[a, j], (x, y, 1 - c))
                cp.start()
                cps.append(cp)
        for a in range(n):
            for j, chip in enumerate(chips):
                other = outs[a].at[(2 * chip[0] + chip[1],) + _half_index(outs[a].shape[1:], 1 - c)]
                _remote(other, other, send_sems.at[a, j], recv_sems.at[a, j], (x, y, c)).wait_recv()
        for cp in cps:
            cp.wait_send()

    return list(pl.pallas_call(
        body, name="gather_forward", in_specs=[ANY] * n, out_specs=[ANY] * n,
        out_shape=[jax.ShapeDtypeStruct(g.shape, g.dtype) for g in gs],
        input_output_aliases={i: i for i in range(n)},
        scratch_shapes=[pltpu.SemaphoreType.DMA((n, 3)), pltpu.SemaphoreType.DMA((n, 3))],
    )(*gs))


def _scatter_exchange(ss):
    def start(cin, cout, send_sems, recv_sems):
        x, y, c = _mesh_pos()
        me = 2 * x + y
        for a, (src, dst) in enumerate(zip(cin, cout)):
            for j, chip in enumerate(_other_chips(x, y)):
                _remote(src.at[2 * chip[0] + chip[1]], dst.at[me], send_sems.at[a, j], recv_sems.at[a, j],
                        (*chip, c)).start()

    def finish(cin, cout, send_sems, recv_sems):
        x, y, c = _mesh_pos()
        for a, dst in enumerate(cout):
            for j, chip in enumerate(_other_chips(x, y)):
                slot = dst.at[2 * chip[0] + chip[1]]
                _remote(slot, slot, send_sems.at[a, j], recv_sems.at[a, j], (x, y, c)).wait()

    return _Exchange(tuple(ss), tuple(jax.ShapeDtypeStruct(s.shape, s.dtype) for s in ss), start, finish)


def _add_pairs(gs, lands, name):
    n = len(gs)

    def body(*refs):
        c = lax.axis_index("c")
        for g_ref, l_ref, o_ref in zip(refs[:n], refs[n:2 * n], refs[2 * n:]):
            mine = g_ref[(0,) + _half_index(g_ref.shape[1:], c)]
            o_ref[0] = (mine.astype(F32) + l_ref[0].astype(F32)).astype(BF16)

    whole = [pl.BlockSpec((1,) + g.shape[1:], lambda j: (j, 0, 0)) for g in gs]
    half = [pl.BlockSpec((1,) + l.shape[1:], lambda j: (j, 0, 0)) for l in lands]
    return list(pl.pallas_call(body, name=name, grid=(gs[0].shape[0],), in_specs=whole + half, out_specs=half,
                               out_shape=[jax.ShapeDtypeStruct(l.shape, BF16) for l in lands],
                               compiler_params=_params(("arbitrary",), VMEM_LIMIT))(*gs, *lands))


def _sum_slots(slots, owns, name):
    n = len(slots)

    def body(*refs):
        me = 2 * lax.axis_index("x") + lax.axis_index("y")
        for s_ref, o_ref, out_ref in zip(refs[:n], refs[n:2 * n], refs[2 * n:]):
            acc = jnp.zeros(out_ref.shape, F32)
            for s in range(N_SHARDS):
                acc = acc + jnp.where(me == s, o_ref[s], s_ref[s]).astype(F32)
            out_ref[...] = acc

    def specs(a):
        _, h, c = a.shape
        if h % 32 == 0:
            return (pl.BlockSpec((N_SHARDS, h // 2, c), lambda i: (0, i, 0)), pl.BlockSpec((h // 2, c), lambda i: (i, 0)))
        return (pl.BlockSpec((N_SHARDS, h, c // 2), lambda i: (0, 0, i)), pl.BlockSpec((h, c // 2), lambda i: (0, i)))

    in_specs = [specs(a)[0] for a in slots]
    return list(pl.pallas_call(body, name=name, grid=(2,), in_specs=in_specs + in_specs,
                               out_specs=[specs(a)[1] for a in slots],
                               out_shape=[jax.ShapeDtypeStruct(a.shape[1:], F32) for a in slots],
                               compiler_params=_params(("arbitrary",), VMEM_LIMIT))(*slots, *owns))


def _all_reduce_small(p):
    r = p.shape[0]

    def body(p_ref, o_ref, buf, send_sems, recv_sems):
        x, y, c = _mesh_pos()
        me = 4 * x + 2 * y + c
        buf[me] = p_ref[...]
        cps = []
        k = 0
        for fx in range(2):
            for fy in range(2):
                for fc in range(2):
                    if fx + fy + fc == 0:
                        continue
                    peer = (1 - x if fx else x, 1 - y if fy else y, 1 - c if fc else c)
                    peer_id = 4 * peer[0] + 2 * peer[1] + peer[2]
                    cp = _remote(p_ref, buf.at[me], send_sems.at[k], recv_sems.at[k], peer)
                    cp.start()
                    cps.append((cp, peer_id, k))
                    k += 1
        for cp, peer_id, k in cps:
            _remote(p_ref, buf.at[peer_id], send_sems.at[k], recv_sems.at[k], (x, y, c)).wait_recv()
        for cp, _, _ in cps:
            cp.wait_send()
        acc = buf[0]
        for s in range(1, 8):
            acc = acc + buf[s]
        o_ref[...] = acc

    vm = pl.BlockSpec(memory_space=pltpu.VMEM)
    return pl.pallas_call(
        body, name="all_reduce_small", in_specs=[vm], out_specs=vm,
        out_shape=jax.ShapeDtypeStruct(p.shape, F32),
        scratch_shapes=[pltpu.VMEM((8, r, LANES), F32), pltpu.SemaphoreType.DMA((7,)), pltpu.SemaphoreType.DMA((7,))],
    )(p)


def _adamw(w, g, m, v, name):
    r, c = w.shape
    row_tiles = [d for d in range(8, min(r, 256) + 1, 8) if r % d == 0]
    tr, tc = (max(row_tiles), c) if row_tiles else (r, 256 if c % 256 == 0 else c)
    c1 = 1.0 / (1.0 - ADAM_B1 ** ADAM_STEP)
    c2 = 1.0 / (1.0 - ADAM_B2 ** ADAM_STEP)

    def body(w_ref, g_ref, m_ref, v_ref, d_ref, nm_ref, nv_ref):
        gv = g_ref[...]
        nm = ADAM_B1 * m_ref[...] + (1.0 - ADAM_B1) * gv
        nv = ADAM_B2 * v_ref[...] + (1.0 - ADAM_B2) * (gv * gv)
        d_ref[...] = -ADAM_LR * ((nm * c1) / (jnp.sqrt(nv * c2) + ADAM_EPS) + ADAM_WD * w_ref[...])
        nm_ref[...] = nm
        nv_ref[...] = nv

    blk = pl.BlockSpec((tr, tc), lambda i, j: (i, j))
    shp = jax.ShapeDtypeStruct((r, c), F32)
    return pl.pallas_call(body, name=name, grid=(r // tr, c // tc), in_specs=[blk] * 4, out_specs=[blk] * 3,
                          out_shape=[shp, shp, shp],
                          compiler_params=_params(("arbitrary", "arbitrary"), VMEM_LIMIT))(w, g, m, v)


PACK_UNIT = 8 * LANES


def _pack(arrs):
    parts = []
    for a in arrs:
        f = a.reshape(-1).astype(F32)
        parts.append(jnp.pad(f, (0, (-f.shape[0]) % PACK_UNIT)).reshape(-1, LANES))
    return jnp.concatenate(parts, axis=0)


def _unpack(m, shapes):
    outs, row = [], 0
    for s in shapes:
        n = int(np.prod(s))
        rows = -(-n // PACK_UNIT) * 8
        outs.append(m[row:row + rows].reshape(-1)[:n].reshape(s))
        row += rows
    return outs


WEIGHTS = ["ffn1_norm", "ffn1_w_gate", "ffn1_w_up", "ffn1_w_down", "mix_norm", "w_in", "conv_w", "a_log", "dt_bias",
           "gdn_norm_w", "q_norm_w", "k_norm_w", "rel_bias", "w_out", "ffn2_norm", "ffn2_w_gate", "ffn2_w_up",
           "ffn2_w_down", "final_norm"]
BIG = ["ffn1_w_gate", "ffn1_w_up", "ffn1_w_down", "w_in", "w_out", "ffn2_w_gate", "ffn2_w_up", "ffn2_w_down"]
SMALL = [n for n in WEIGHTS if n not in BIG]
COL_SHARDED = ["ffn1_w_gate", "ffn1_w_up", "w_in", "ffn2_w_gate", "ffn2_w_up"]
N_IN_COLS = 3600
TM = 256
TE = 512
TK = 2048


def kernel(x, ffn1_norm, ffn1_w_gate, ffn1_w_up, ffn1_w_down, mix_norm, w_in, conv_w, a_log, dt_bias, gdn_norm_w, q_norm_w, k_norm_w, rel_bias, w_out, ffn2_norm, ffn2_w_gate, ffn2_w_up, ffn2_w_down, final_norm, loss_target, m_ffn1_norm, m_ffn1_w_gate, m_ffn1_w_up, m_ffn1_w_down, m_mix_norm, m_w_in, m_conv_w, m_a_log, m_dt_bias, m_gdn_norm_w, m_q_norm_w, m_k_norm_w, m_rel_bias, m_w_out, m_ffn2_norm, m_ffn2_w_gate, m_ffn2_w_up, m_ffn2_w_down, m_final_norm, v_ffn1_norm, v_ffn1_w_gate, v_ffn1_w_up, v_ffn1_w_down, v_mix_norm, v_w_in, v_conv_w, v_a_log, v_dt_bias, v_gdn_norm_w, v_q_norm_w, v_k_norm_w, v_rel_bias, v_w_out, v_ffn2_norm, v_ffn2_w_gate, v_ffn2_w_up, v_ffn2_w_down, v_final_norm):
    p = dict(locals())
    xs, target = x[0], loss_target[0]
    t, d = xs.shape
    nc = t // CHUNK
    tk = min(TK, t)
    me = 2 * lax.axis_index("x") + lax.axis_index("y")

    first = ["ffn1_w_gate", "ffn1_w_up", "ffn1_w_down"]
    later = [n for n in BIG if n not in first] + ["conv_w"]
    local = lambda n, a: a[0].T if n in COL_SHARDED else a[0]
    shards = {n: local(n, p[n]).astype(BF16) for n in BIG}
    shards["conv_w"] = conv_w[0]
    gw = dict(zip(first, _all_gather([shards[n] for n in first])))
    f1 = (gw["ffn1_w_gate"], gw["ffn1_w_up"], gw["ffn1_w_down"])
    (x1, xn1, g1, u1), landed = _ffn_fwd(xs, ffn1_norm, *f1, TM, "ffn1_fwd",
                                         exchange=_gather_exchange([shards[n] for n in later]))
    gw.update(zip(later, _gather_forward(landed)))
    w_in_t = gw["w_in"].reshape(N_IN_COLS, d)
    gates = slice(P_AB[0], P_AB[0] + N_GATE_COLS)
    wp = jnp.concatenate([w_in_t[:gates.start], jnp.pad(w_in_t[gates], ((0, LANES - N_GATE_COLS), (0, 0))),
                          w_in_t[gates.stop:]], axis=0)
    w_out_full = gw["w_out"].reshape(d, d)
    conv_rows = conv_w.shape[1]
    cw = jnp.pad(gw["conv_w"].reshape(N_SHARDS * conv_rows, CONV_TAPS).T, ((0, 8 - CONV_TAPS), (0, 0)))
    gp = jnp.pad(jnp.stack([a_log.reshape(8), dt_bias.reshape(8)]), ((0, 6), (0, LANES - 8)))
    gdn_w = gdn_norm_w.reshape(1, GDN_DIM)
    qw_t = jnp.tile(q_norm_w.reshape(1, SWA_DIM), (1, SWA_HEADS))
    kw_t = jnp.tile(k_norm_w.reshape(1, SWA_DIM), (1, SWA_HEADS))
    bd = jnp.asarray(np.kron(np.eye(2), np.full((SWA_DIM, SWA_DIM), 1.0 / SWA_DIM)), BF16)
    f2 = (gw["ffn2_w_gate"], gw["ffn2_w_up"], gw["ffn2_w_down"])

    hn, qkva, z, ab, qkvb = _mix_in_fwd(x1, mix_norm, wp, TE)
    qkvc, gb = _gdn_prep_fwd(qkva, cw, ab, gp, TM)
    gbt = jnp.transpose(gb[:, :16].reshape(nc, CHUNK, 16), (0, 2, 1))
    o_f, o_b, gdn_saved = _gdn_fwd(qkvc, gb, gbt)
    oa = _gdn_post_fwd(o_f, o_b, z, gdn_w, TE)
    o_swa, o_swa16, swa_saved = _swa_branch_fwd(qkvb, qw_t, kw_t, rel_bias, bd, TE)
    x2 = _mix_out_fwd(x1, oa, o_swa, w_out_full, TE)
    (dx3, xn2, g2, u2, loss_part, d_final), _ = _ffn_fwd(x2, ffn2_norm, *f2, TM, "ffn2_fwd", head=(final_norm, target))

    def pair_sums(partials, tag):
        return _add_pairs(partials, _rs_pair(partials), f"rs_add_{tag}")

    (dx2, dyh2, dg2, du2, h2, d_nw2), _ = _ffn_bwd_dx(dx3, x2, ffn2_norm, g2, u2, *f2, TM, "ffn2_bwd_dx")
    dwg2 = _matmul_tn(dg2, xn2, tk, "ffn2_dwg")
    dwu2 = _matmul_tn(du2, xn2, tk, "ffn2_dwu")
    dwd2 = _matmul_tn(h2, dyh2, tk, "ffn2_dwd")
    (doa, dob, dx2b), lands_f2 = _mix_out_bwd(dx2, w_out_full, TE, exchange=_pair_exchange([dwg2, dwu2, dwd2]))
    sums_f2 = _add_pairs([dwg2, dwu2, dwd2], lands_f2, "rs_add_a")
    dwo = jnp.concatenate([_matmul_tn(oa, dx2b, tk, "w_out_dw_a")[0], _matmul_tn(o_swa16, dx2b, tk, "w_out_dw_b")[0]],
                          axis=0).reshape(N_SHARDS, d // N_SHARDS, d)
    do_g, dz, d_gdnw = _gdn_post_bwd(doa, o_f, o_b, z, gdn_w, TE)
    (dqkvc, dgates), slots_f2 = _gdn_bwd(qkvc, gb, gbt, do_g, gdn_saved, exchange=_scatter_exchange(sums_f2))
    dqkva, dab, dcw, dgp = _gdn_prep_bwd(qkva, cw, ab, gp, dqkvc, dgates, TM)
    dqkvb, d_qw, d_kw, d_rel = _swa_branch_bwd(dob, o_swa, swa_saved, qkvb, qw_t, kw_t, bd, TE)
    dpieces = (dqkva, dz, dab, dqkvb)
    dwp = [_matmul_tn(dp, hn, tk, f"w_in_dw_{i}")[0] for i, dp in enumerate(dpieces)]
    dw_in = jnp.concatenate([dwp[0], dwp[1], dwp[2][:N_GATE_COLS], dwp[3]], axis=0)
    dw_in = dw_in.reshape(N_SHARDS, N_IN_COLS // N_SHARDS, d)
    sums_mix = pair_sums([dw_in, dwo], "b")
    (dx1, d_mixnw), slots_mix = _mix_in_bwd_dx(dx2, x1, mix_norm, dpieces, wp, TE, exchange=_scatter_exchange(sums_mix))
    (gx, dyh1, dg1, du1, h1, d_nw1), _ = _ffn_bwd_dx(dx1, xs, ffn1_norm, g1, u1, *f1, TM, "ffn1_bwd_dx")
    dwg1 = _matmul_tn(dg1, xn1, tk, "ffn1_dwg")
    dwu1 = _matmul_tn(du1, xn1, tk, "ffn1_dwu")
    sums_gu = pair_sums([dwg1, dwu1], "c")
    dwd1, slots_gu = _matmul_tn(h1, dyh1, tk, "ffn1_dwd", exchange=_scatter_exchange(sums_gu))
    sums_d = pair_sums([dwd1], "d")
    slots = slots_gu + _rs_chips(sums_d) + slots_mix + slots_f2
    sums = sums_gu + sums_d + sums_mix + sums_f2
    halves = _sum_slots(slots[:4], sums[:4], "rs_sum_a") + _sum_slots(slots[4:], sums[4:], "rs_sum_b")
    g_big = dict(zip(BIG, _rs_join(halves, [_split_axis(shards[n].shape) for n in BIG])))

    small_partial = {"ffn1_norm": d_nw1, "mix_norm": d_mixnw, "a_log": dgp[0, 0:8], "dt_bias": dgp[1, 0:8],
                     "gdn_norm_w": d_gdnw, "q_norm_w": d_qw, "k_norm_w": d_kw, "rel_bias": d_rel,
                     "ffn2_norm": d_nw2, "final_norm": d_final, "conv_w": dcw[0:CONV_TAPS].T}
    red = _all_reduce_small(_pack([small_partial[n] for n in SMALL] + [loss_part[0, 0:1]]))
    full_shapes = [p[n].shape if n != "conv_w" else (N_SHARDS * conv_rows, CONV_TAPS) for n in SMALL]
    red_parts = _unpack(red, full_shapes + [(1,)])
    loss = red_parts[-1].reshape(())
    g_small = dict(zip(SMALL, red_parts[:-1]))
    g_small["conv_w"] = lax.dynamic_slice_in_dim(g_small["conv_w"], me * conv_rows, conv_rows, 0).reshape(conv_w.shape)

    grads, deltas, new_m, new_v = {}, {}, {}, {}
    for n in BIG:
        back = (lambda a: a.T[None]) if n in COL_SHARDED else (lambda a: a[None])
        grads[n] = back(g_big[n])
        dl, nm, nv = _adamw(local(n, p[n]), g_big[n], local(n, p["m_" + n]), local(n, p["v_" + n]), "adamw_" + n)
        deltas[n], new_m[n], new_v[n] = back(dl), back(nm), back(nv)
    packed = [_pack([src[n] for n in SMALL]) for src in
              ({n: p[n] for n in SMALL}, g_small, {n: p["m_" + n] for n in SMALL}, {n: p["v_" + n] for n in SMALL})]
    small_shapes = [p[n].shape for n in SMALL]
    for dst, arr in zip((deltas, new_m, new_v), _adamw(*packed, "adamw_small")):
        dst.update(zip(SMALL, _unpack(arr, small_shapes)))
    grads.update(g_small)

    return (loss, gx[None], *[grads[n] for n in WEIGHTS], *[deltas[n] for n in WEIGHTS],
            *[new_m[n] for n in WEIGHTS], *[new_v[n] for n in WEIGHTS])
```

```python
import math
from typing import Callable, NamedTuple

import numpy as np
import jax
import jax.numpy as jnp
from jax import lax
from jax.experimental import pallas as pl
from jax.experimental.pallas import tpu as pltpu

F32 = jnp.float32
BF16 = jnp.bfloat16
MESH = pl.DeviceIdType.MESH

EPS = 1e-6
NEG_BIG = -1e30
GDN_HEADS = 4
GDN_DIM = 128
CHUNK = 64
SWA_HEADS = 8
SWA_DIM = 64
PATTERNS = ((128, 1), (512, 4), (2048, 16))
RADIUS = 64
REL_BUCKETS = 32
REL_MAX_DISTANCE = 1024
CONV_TAPS = 5
N_SHARDS = 4
LANES = 128
VMEM_LIMIT = 56 * 1024 * 1024

ADAM_LR, ADAM_B1, ADAM_B2, ADAM_EPS, ADAM_WD, ADAM_STEP = 0.001, 0.9, 0.999, 1e-08, 0.01, 10


def _params(sem=None, vmem=None):
    return pltpu.CompilerParams(dimension_semantics=sem, vmem_limit_bytes=vmem)


def _resident(shape):
    nd = len(shape)
    return pl.BlockSpec(shape, lambda *_: (0,) * nd, pipeline_mode=pl.Buffered(1))


ANY = pl.BlockSpec(memory_space=pl.ANY)


class _Exchange(NamedTuple):
    arrays: tuple
    out_shape: tuple
    start: Callable
    finish: Callable


def _grid_call(body, name, nsteps, in_specs, out_specs, out_shape, operands, scratch=(), exchange=None):
    params = _params(("arbitrary",), VMEM_LIMIT)
    if exchange is None:
        res = pl.pallas_call(body, name=name, grid=(nsteps,), in_specs=list(in_specs), out_specs=list(out_specs),
                             out_shape=list(out_shape), scratch_shapes=list(scratch), compiler_params=params)(*operands)
        return list(res), []
    n_in, n_out, k, n_scr = len(in_specs), len(out_specs), len(exchange.arrays), len(scratch)

    def wrapped(*refs):
        ins, cin = refs[:n_in], refs[n_in:n_in + k]
        outs, cout = refs[n_in + k:n_in + k + n_out], refs[n_in + k + n_out:n_in + 2 * k + n_out]
        rest = refs[n_in + 2 * k + n_out:]
        scr, (send_sems, recv_sems) = rest[:n_scr], rest[n_scr:]

        @pl.when(pl.program_id(0) == 0)
        def _():
            exchange.start(cin, cout, send_sems, recv_sems)

        body(*ins, *outs, *scr)

        @pl.when(pl.program_id(0) == nsteps - 1)
        def _():
            exchange.finish(cin, cout, send_sems, recv_sems)

    res = pl.pallas_call(
        wrapped, name=name, grid=(nsteps,), in_specs=list(in_specs) + [ANY] * k, out_specs=list(out_specs) + [ANY] * k,
        out_shape=list(out_shape) + list(exchange.out_shape),
        scratch_shapes=list(scratch) + [pltpu.SemaphoreType.DMA((k, 4)), pltpu.SemaphoreType.DMA((k, 4))],
        compiler_params=params)(*operands, *exchange.arrays)
    return list(res[:n_out]), list(res[n_out:])


def _dot(a, b):
    return jnp.dot(a.astype(BF16), b.astype(BF16), preferred_element_type=F32)


def _dot_nt(a, b):
    return lax.dot_general(a.astype(BF16), b.astype(BF16), (((1,), (1,)), ((), ())), preferred_element_type=F32)


def _dot_tn(a, b):
    return lax.dot_general(a.astype(BF16), b.astype(BF16), (((0,), (0,)), ((), ())), preferred_element_type=F32)


def _sigmoid(x):
    return 1.0 / (1.0 + jnp.exp(-x))


def _rstd(xf):
    return lax.rsqrt(jnp.mean(xf * xf, axis=-1, keepdims=True) + EPS)


def _rms_bwd(xf, r, nw, dxn):
    xhat = xf * r
    dxh = dxn * nw
    dx = r * (dxh - xhat * jnp.mean(dxh * xhat, axis=-1, keepdims=True))
    return dx, jnp.sum(dxn * xhat, axis=0, keepdims=True)


def _ffn_fwd(x, nw, wg, wu, wd, tm, name, exchange=None, head=None):
    t, d = x.shape
    nj, fs, _ = wg.shape

    def body(x_ref, nw_ref, wg_ref, wu_ref, wd_ref, *rest):
        if head is None:
            y_ref, xn_ref, g_ref, u_ref = rest
        else:
            fw_ref, t_ref, y_ref, xn_ref, g_ref, u_ref, loss_ref, dfw_ref = rest

            @pl.when(pl.program_id(0) == 0)
            def _():
                loss_ref[...] = jnp.zeros_like(loss_ref)
                dfw_ref[...] = jnp.zeros_like(dfw_ref)

        xf = x_ref[...]
        xn = (xf * _rstd(xf) * nw_ref[...]).astype(BF16)
        xn_ref[...] = xn
        acc = jnp.zeros((tm, d), F32)
        for j in range(nj):
            g = _dot_nt(xn, wg_ref[j])
            u = _dot_nt(xn, wu_ref[j])
            h = (g * _sigmoid(g) * u).astype(BF16)
            acc = acc + jnp.dot(h, wd_ref[j], preferred_element_type=F32)
            g_ref[j] = g.astype(BF16)
            u_ref[j] = u.astype(BF16)
        y = xf + 0.5 * acc
        if head is None:
            y_ref[...] = y
        else:
            r = _rstd(y)
            err = y * r * fw_ref[...] - t_ref[...]
            loss_ref[...] += 0.5 * jnp.sum(jnp.mean(err * err, axis=-1, keepdims=True), axis=0, keepdims=True)
            dy, dfw = _rms_bwd(y, r, fw_ref[...], err * (1.0 / d))
            y_ref[...] = dy
            dfw_ref[...] += dfw

    row = pl.BlockSpec((tm, d), lambda i: (i, 0))
    act = pl.BlockSpec((nj, tm, fs), lambda i: (0, i, 0))
    in_specs = [row, _resident((1, d)), _resident(wg.shape), _resident(wu.shape), _resident(wd.shape)]
    out_specs = [row, row, act, act]
    out_shape = [jax.ShapeDtypeStruct((t, d), F32), jax.ShapeDtypeStruct((t, d), BF16),
                 jax.ShapeDtypeStruct((nj, t, fs), BF16), jax.ShapeDtypeStruct((nj, t, fs), BF16)]
    operands = (x, nw, wg, wu, wd)
    if head is not None:
        in_specs += [_resident((1, d)), row]
        out_specs += [pl.BlockSpec((1, LANES), lambda i: (0, 0)), pl.BlockSpec((1, d), lambda i: (0, 0))]
        out_shape += [jax.ShapeDtypeStruct((1, LANES), F32), jax.ShapeDtypeStruct((1, d), F32)]
        operands += tuple(head)
    return _grid_call(body, name, t // tm, in_specs, out_specs, out_shape, operands, exchange=exchange)


def _ffn_bwd_dx(dy, x, nw, g, u, wg, wu, wd, tm, name, exchange=None):
    t, d = x.shape
    nj, fs, _ = wg.shape

    def body(dy_ref, x_ref, nw_ref, g_ref, u_ref, wg_ref, wu_ref, wd_ref,
             dx_ref, dyh_ref, dg_ref, du_ref, h_ref, dnw_ref):
        @pl.when(pl.program_id(0) == 0)
        def _():
            dnw_ref[...] = jnp.zeros_like(dnw_ref)

        dyv = dy_ref[...]
        dyh = (0.5 * dyv).astype(BF16)
        dyh_ref[...] = dyh
        dxn = jnp.zeros((tm, d), F32)
        dh_next = _dot_nt(dyh, wd_ref[0])
        for j in range(nj):
            dh = dh_next
            gv = g_ref[j].astype(F32)
            uv = u_ref[j].astype(F32)
            sg = _sigmoid(gv)
            si = gv * sg
            dg = (dh * uv * (sg * (1.0 + gv * (1.0 - sg)))).astype(BF16)
            du = (dh * si).astype(BF16)
            if j + 1 < nj:
                dh_next = _dot_nt(dyh, wd_ref[j + 1])
            h_ref[j] = (si * uv).astype(BF16)
            dg_ref[j] = dg
            du_ref[j] = du
            dxn = dxn + _dot(dg, wg_ref[j]) + _dot(du, wu_ref[j])
        xf = x_ref[...]
        dxr, dnw = _rms_bwd(xf, _rstd(xf), nw_ref[...], dxn)
        dx_ref[...] = dyv + dxr
        dnw_ref[...] += dnw

    row = pl.BlockSpec((tm, d), lambda i: (i, 0))
    act = pl.BlockSpec((nj, tm, fs), lambda i: (0, i, 0))
    act_shape = jax.ShapeDtypeStruct((nj, t, fs), BF16)
    return _grid_call(
        body, name, t // tm,
        [row, row, _resident((1, d)), act, act, _resident(wg.shape), _resident(wu.shape), _resident(wd.shape)],
        [row, row, act, act, act, pl.BlockSpec((1, d), lambda i: (0, 0))],
        [jax.ShapeDtypeStruct((t, d), F32), jax.ShapeDtypeStruct((t, d), BF16),
         act_shape, act_shape, act_shape, jax.ShapeDtypeStruct((1, d), F32)],
        (dy, x, nw, g, u, wg, wu, wd), exchange=exchange)


def _matmul_tn(a, b, tk, name, exchange=None):
    a3, b3 = a.ndim == 3, b.ndim == 3
    nj = a.shape[0] if a3 else (b.shape[0] if b3 else 1)
    t, m = a.shape[-2:]
    n = b.shape[-1]
    nt = t // tk

    def body(a_ref, b_ref, o_ref, acc_ref):
        k = pl.program_id(0) % nt

        @pl.when(k == 0)
        def _():
            acc_ref[...] = jnp.zeros_like(acc_ref)

        acc_ref[...] += lax.dot_general(a_ref[...], b_ref[...], (((0,), (0,)), ((), ())),
                                        preferred_element_type=F32)

        @pl.when(k == nt - 1)
        def _():
            o_ref[...] = acc_ref[...].astype(o_ref.dtype)

    a_spec = (pl.BlockSpec((None, tk, m), lambda i: (i // nt, i % nt, 0)) if a3
              else pl.BlockSpec((tk, m), lambda i: (i % nt, 0)))
    b_spec = (pl.BlockSpec((None, tk, n), lambda i: (i // nt, i % nt, 0)) if b3
              else pl.BlockSpec((tk, n), lambda i: (i % nt, 0)))
    (out,), landed = _grid_call(
        body, name, nj * nt, [a_spec, b_spec], [pl.BlockSpec((None, m, n), lambda i: (i // nt, 0, 0))],
        [jax.ShapeDtypeStruct((nj, m, n), BF16)], (a, b), scratch=[pltpu.VMEM((m, n), F32)], exchange=exchange)
    return out if exchange is None else (out, landed)


N_GATE_COLS = 4 * GDN_HEADS
P_QKVA, P_Z, P_AB, P_QKVB = (0, 1536), (1536, 2048), (2048, 2048 + LANES), (2048 + N_GATE_COLS, 3600)
P_PIECES = (P_QKVA, P_Z, P_AB, P_QKVB)


def _mix_in_fwd(x1, nw, wp, tm):
    t, d = x1.shape

    def body(x_ref, nw_ref, w_ref, hn_ref, *outs):
        xf = x_ref[...]
        xn = (xf * _rstd(xf) * nw_ref[...]).astype(BF16)
        hn_ref[...] = xn
        for (a, b), o_ref in zip(P_PIECES, outs):
            o_ref[...] = _dot_nt(xn, w_ref[a:b, :])

    row = pl.BlockSpec((tm, d), lambda i: (i, 0))
    return pl.pallas_call(
        body, name="mix_in_fwd", grid=(t // tm,),
        in_specs=[row, _resident((1, d)), _resident(wp.shape)],
        out_specs=[row] + [pl.BlockSpec((tm, b - a), lambda i: (i, 0)) for a, b in P_PIECES],
        out_shape=[jax.ShapeDtypeStruct((t, d), BF16)]
                  + [jax.ShapeDtypeStruct((t, b - a), F32) for a, b in P_PIECES],
        compiler_params=_params(("arbitrary",), VMEM_LIMIT),
    )(x1, nw, wp)


def _mix_in_bwd_dx(dx, x1, nw, dpieces, wp, tm, exchange=None):
    t, d = x1.shape

    def body(dx_ref, x_ref, nw_ref, p0, p1, p2, p3, w_ref, o_ref, dnw_ref):
        @pl.when(pl.program_id(0) == 0)
        def _():
            dnw_ref[...] = jnp.zeros_like(dnw_ref)

        dh = jnp.zeros((tm, d), F32)
        for (a, b), p_ref in zip(P_PIECES, (p0, p1, p2, p3)):
            dh = dh + _dot(p_ref[...], w_ref[a:b, :])
        xf = x_ref[...]
        dxr, dnw = _rms_bwd(xf, _rstd(xf), nw_ref[...], dh)
        o_ref[...] = dx_ref[...] + dxr
        dnw_ref[...] += dnw

    row = pl.BlockSpec((tm, d), lambda i: (i, 0))
    return _grid_call(
        body, "mix_in_bwd_dx", t // tm,
        [row, row, _resident((1, d))]
        + [pl.BlockSpec((tm, b - a), lambda i: (i, 0)) for a, b in P_PIECES] + [_resident(wp.shape)],
        [row, pl.BlockSpec((1, d), lambda i: (0, 0))],
        [jax.ShapeDtypeStruct((t, d), F32), jax.ShapeDtypeStruct((1, d), F32)],
        (dx, x1, nw, *dpieces, wp), exchange=exchange)


def _mix_out_fwd(x1, oa, ob, w, tm):
    t, d = x1.shape
    half = oa.shape[1]

    def body(x_ref, oa_ref, ob_ref, w_ref, o_ref):
        o_ref[...] = (x_ref[...] + _dot(oa_ref[...], w_ref[0:half, :]) + _dot(ob_ref[...], w_ref[half:2 * half, :]))

    row = pl.BlockSpec((tm, d), lambda i: (i, 0))
    hrow = pl.BlockSpec((tm, half), lambda i: (i, 0))
    return pl.pallas_call(
        body, name="mix_out_fwd", grid=(t // tm,),
        in_specs=[row, hrow, hrow, _resident(w.shape)],
        out_specs=row, out_shape=jax.ShapeDtypeStruct((t, d), F32),
        compiler_params=_params(("arbitrary",), VMEM_LIMIT),
    )(x1, oa, ob, w)


def _mix_out_bwd(dx2, w, tm, exchange=None):
    t, d = dx2.shape
    half = w.shape[0] // 2

    def body(dx_ref, w_ref, doa_ref, dob_ref, dxb_ref):
        dxb = dx_ref[...].astype(BF16)
        dxb_ref[...] = dxb
        doa_ref[...] = _dot_nt(dxb, w_ref[0:half, :])
        dob_ref[...] = _dot_nt(dxb, w_ref[half:2 * half, :])

    row = pl.BlockSpec((tm, d), lambda i: (i, 0))
    hrow = pl.BlockSpec((tm, half), lambda i: (i, 0))
    return _grid_call(
        body, "mix_out_bwd", t // tm, [row, _resident(w.shape)], [hrow, hrow, row],
        [jax.ShapeDtypeStruct((t, half), F32), jax.ShapeDtypeStruct((t, half), F32), jax.ShapeDtypeStruct((t, d), BF16)],
        (dx2, w), exchange=exchange)


HALO = 8


def _halo_row_specs(tr, cols, nrow8):
    per = tr // HALO
    return [pl.BlockSpec((tr, cols), lambda i: (i, 0)),
            pl.BlockSpec((HALO, cols), lambda i: (jnp.maximum(i * per - 1, 0), 0)),
            pl.BlockSpec((HALO, cols), lambda i: (jnp.minimum((i + 1) * per, nrow8 - 1), 0))]


def _conv_window(xm, xp, xn, first, last, cols):
    prev = jnp.where(first, 0.0, xp[:, cols])
    nxt = jnp.where(last, 0.0, xn[:, cols])
    return jnp.concatenate([prev, xm[:, cols], nxt], axis=0)


def _shift_rows(xw, off):
    n = xw.shape[0]
    sh = (-off) % n
    return xw if sh == 0 else pltpu.roll(xw, sh, 0)


def _conv_pre(xw, cw_ref, cols):
    acc, shifted = None, []
    for j in range(CONV_TAPS):
        shifted.append(_shift_rows(xw, j - CONV_TAPS // 2))
        term = shifted[-1] * cw_ref[j:j + 1, cols]
        acc = term if acc is None else acc + term
    return acc, shifted


def _softplus(x):
    u = jnp.exp(-jnp.abs(x))
    w = 1.0 + u
    log1p = jnp.where(w == 1.0, u, jnp.log(w) * u / jnp.where(w == 1.0, 1.0, w - 1.0))
    return jnp.maximum(x, 0.0) + log1p


def _gdn_prep_fwd(qkva, cw, ab, gp, tr):
    t, c = qkva.shape
    nt = t // tr
    ncb = c // LANES

    def body(xm, xp, xn, cw_ref, ab_ref, gp_ref, o_ref, gb_ref):
        i = pl.program_id(0)
        first, last = i == 0, i == nt - 1
        for cb in range(ncb):
            cols = slice(cb * LANES, (cb + 1) * LANES)
            xw = _conv_window(xm, xp, xn, first, last, cols)
            pre = _conv_pre(xw, cw_ref, cols)[0][HALO:HALO + tr]
            y = pre * _sigmoid(pre)
            if cb < 2 * GDN_HEADS:
                y = y * lax.rsqrt(jnp.sum(y * y, axis=-1, keepdims=True) + EPS)
            if cb < GDN_HEADS:
                y = y * (GDN_DIM ** -0.5)
            o_ref[:, cols] = y
        abv = ab_ref[...]
        lane = lax.broadcasted_iota(jnp.int32, abv.shape, 1)
        g = -jnp.exp(gp_ref[0:1, :]) * _softplus(abv + gp_ref[1:2, :])
        gb_ref[...] = jnp.where(lane < 8, g, jnp.where(lane < 16, _sigmoid(abv), 0.0))

    return pl.pallas_call(
        body, name="gdn_prep_fwd", grid=(nt,),
        in_specs=_halo_row_specs(tr, c, t // HALO)
                 + [_resident(cw.shape), pl.BlockSpec((tr, LANES), lambda i: (i, 0)), _resident(gp.shape)],
        out_specs=[pl.BlockSpec((tr, c), lambda i: (i, 0)), pl.BlockSpec((tr, LANES), lambda i: (i, 0))],
        out_shape=[jax.ShapeDtypeStruct((t, c), F32), jax.ShapeDtypeStruct((t, LANES), F32)],
        compiler_params=_params(("arbitrary",), VMEM_LIMIT),
    )(qkva, qkva, qkva, cw, ab, gp)


def _gdn_prep_bwd(qkva, cw, ab, gp, dy, dgates, tr):
    t, c = qkva.shape
    nt = t // tr
    ncb = c // LANES

    def body(xm, xp, xn, fm, fp, fn, cw_ref, ab_ref, gp_ref, gf_ref, dx_ref, dab_ref, dcw_ref, dgp_ref):
        i = pl.program_id(0)
        first, last = i == 0, i == nt - 1

        @pl.when(first)
        def _():
            dcw_ref[...] = jnp.zeros_like(dcw_ref)
            dgp_ref[...] = jnp.zeros_like(dgp_ref)

        sub8 = lax.broadcasted_iota(jnp.int32, (8, LANES), 0)
        for cb in range(ncb):
            cols = slice(cb * LANES, (cb + 1) * LANES)
            xw = _conv_window(xm, xp, xn, first, last, cols)
            dyw = _conv_window(fm, fp, fn, first, last, cols)
            pre, x_shifted = _conv_pre(xw, cw_ref, cols)
            sg = _sigmoid(pre)
            s = pre * sg
            if cb < 2 * GDN_HEADS:
                scale = (GDN_DIM ** -0.5) if cb < GDN_HEADS else 1.0
                r = lax.rsqrt(jnp.sum(s * s, axis=-1, keepdims=True) + EPS)
                dn = dyw * scale
                ds = r * dn - s * (r * r * r) * jnp.sum(dn * s, axis=-1, keepdims=True)
            else:
                ds = dyw
            dpre = ds * (sg * (1.0 + pre * (1.0 - sg)))
            dx = None
            dcw = jnp.zeros((8, LANES), F32)
            for j in range(CONV_TAPS):
                off = j - CONV_TAPS // 2
                term = _shift_rows(dpre, -off)[HALO:HALO + tr] * cw_ref[j:j + 1, cols]
                dx = term if dx is None else dx + term
                tap = jnp.sum(dpre[HALO:HALO + tr] * x_shifted[j][HALO:HALO + tr], axis=0, keepdims=True)
                dcw = dcw + jnp.where(sub8 == j, tap, 0.0)
            dx_ref[:, cols] = dx.astype(BF16)
            dcw_ref[:, cols] += dcw

        abv = ab_ref[...]
        dgb = gf_ref[...]
        lane = lax.broadcasted_iota(jnp.int32, abv.shape, 1)
        nea = -jnp.exp(gp_ref[0:1, :])
        xs = abv + gp_ref[1:2, :]
        g = nea * _softplus(xs)
        beta = _sigmoid(abv)
        da = dgb * nea * _sigmoid(xs)
        dab = jnp.where(lane < 8, da, jnp.where(lane < 16, dgb * beta * (1.0 - beta), 0.0))
        dab_ref[...] = dab.astype(BF16)
        keep = lane[0:1, :] < 8
        dalog = jnp.where(keep, jnp.sum(dgb * g, axis=0, keepdims=True), 0.0)
        ddtb = jnp.where(keep, jnp.sum(da, axis=0, keepdims=True), 0.0)
        dgp_ref[...] += jnp.where(sub8 == 0, dalog, 0.0) + jnp.where(sub8 == 1, ddtb, 0.0)

    lrow = pl.BlockSpec((tr, LANES), lambda i: (i, 0))
    halo = _halo_row_specs(tr, c, t // HALO)
    return pl.pallas_call(
        body, name="gdn_prep_bwd", grid=(nt,),
        in_specs=halo + halo + [_resident(cw.shape), lrow, _resident(gp.shape), lrow],
        out_specs=[pl.BlockSpec((tr, c), lambda i: (i, 0)), lrow,
                   pl.BlockSpec(cw.shape, lambda i: (0, 0)), pl.BlockSpec(gp.shape, lambda i: (0, 0))],
        out_shape=[jax.ShapeDtypeStruct((t, c), BF16), jax.ShapeDtypeStruct((t, LANES), BF16),
                   jax.ShapeDtypeStruct(cw.shape, F32), jax.ShapeDtypeStruct(gp.shape, F32)],
        compiler_params=_params(("arbitrary",), VMEM_LIMIT),
    )(qkva, qkva, qkva, dy, dy, dy, cw, ab, gp, dgates)


def _chunk_masks(lower):
    ii = lax.broadcasted_iota(jnp.int32, (CHUNK, CHUNK), 0)
    jj = lax.broadcasted_iota(jnp.int32, (CHUNK, CHUNK), 1)
    incl = (ii >= jj) if lower else (ii <= jj)
    strict = (ii > jj) if lower else (ii < jj)
    return ii, jj, incl, strict


def _dot3(a, b):
    ah = a.astype(BF16)
    al = (a - ah.astype(F32)).astype(BF16)
    bh = b.astype(BF16)
    bl = (b - bh.astype(F32)).astype(BF16)
    d = lambda u, v: jnp.dot(u, v, preferred_element_type=F32)
    return d(ah, bh) + (d(ah, bl) + d(al, bh))


def _tri_inv_many(lmats, ii, jj):
    m16 = (ii // 16) == (jj // 16)
    m32 = (ii // 32) == (jj // 32)
    eye = jnp.where(ii == jj, 1.0, 0.0)
    l16 = [jnp.where(m16, l, 0.0) for l in lmats]
    p2 = [_dot3(a, a) for a in l16]
    p4 = [_dot3(a, a) for a in p2]
    p8 = [_dot3(a, a) for a in p4]
    xs = [eye - a for a in l16]
    for ps in (p2, p4, p8):
        xs = [x + _dot3(x, p) for x, p in zip(xs, ps)]
    for off in ([jnp.where(m32 & jnp.logical_not(m16), l, 0.0) for l in lmats],
                [jnp.where(m32, 0.0, l) for l in lmats]):
        ys = [_dot3(x, c) for x, c in zip(xs, off)]
        xs = [x - _dot3(y, x) for x, y in zip(xs, ys)]
    return xs


def _col_to_row(col, ii, jj):
    return jnp.sum(jnp.where(ii == jj, col, 0.0), axis=0, keepdims=True)


def _row_to_col(row, ii, jj):
    return jnp.sum(jnp.where(ii == jj, row, 0.0), axis=1, keepdims=True)


def _chain_common(q, k, v, graw_col, graw_row, bcol, masks):
    ii, jj, incl, strict = masks
    inclt = jnp.logical_not(strict)
    gcol = jnp.sum(jnp.where(incl, graw_row, 0.0), axis=1, keepdims=True)
    grow = jnp.sum(jnp.where(inclt, graw_col, 0.0), axis=0, keepdims=True)
    glast = jnp.sum(graw_row, axis=1, keepdims=True)
    decay = jnp.where(incl, jnp.exp(jnp.where(incl, gcol - grow, 0.0)), 0.0)
    kb = k * bcol
    vb = v * bcol
    eg = jnp.exp(gcol)
    ek = jnp.exp(glast - gcol)
    kbg = kb * eg
    amat = _dot_nt(kb, k)
    qk = _dot_nt(q, k)
    return dict(gcol=gcol, glast=glast, decay=decay, kb=kb, vb=vb, eg=eg, ek=ek, kbg=kbg, amat=amat, qk=qk,
                intra=qk * decay, qg=q * eg, kdec=k * ek)


def _gdn_fwd(qkvc, gb, gbt):
    tm, u, w, qg, kd, intra, egl = _gdn_local_fwd(qkvc, gb, gbt)
    o_f, o_b, s_f, s_b, vn_f, vn_b = _gdn_scan_fwd(u, w, qg, kd, intra, egl, qkvc.shape[0])
    return o_f, o_b, dict(tm=tm, w=w, qg=qg, kd=kd, intra=intra, egl=egl, s=(s_f, s_b), vn=(vn_f, vn_b))


N_CHAINS = 2 * GDN_HEADS


LOCAL_CHUNKS = 4


def _load_chains(x_ref, g_ref, gt_ref, cc=0):
    hd = GDN_HEADS * GDN_DIM
    rows = slice(cc * CHUNK, (cc + 1) * CHUNK)
    chains = []
    for d in range(2):
        masks = _chunk_masks(d == 0)
        for h in range(GDN_HEADS):
            ch = d * GDN_HEADS + h
            q = x_ref[rows, h * GDN_DIM:(h + 1) * GDN_DIM]
            k = x_ref[rows, hd + h * GDN_DIM:hd + (h + 1) * GDN_DIM]
            v = x_ref[rows, 2 * hd + h * GDN_DIM:2 * hd + (h + 1) * GDN_DIM]
            bcol = g_ref[rows, 8 + ch:9 + ch]
            cm = _chain_common(q, k, v, g_ref[rows, ch:ch + 1], gt_ref[cc, ch:ch + 1, :], bcol, masks)
            chains.append(dict(cm, q=q, k=k, v=v, bcol=bcol, masks=masks, ch=ch, h=h, cc=cc))
    return chains


def _chain_shape(rows, cols, dtype):
    return lambda nc: jax.ShapeDtypeStruct((nc, N_CHAINS, rows, cols), dtype)


def _gdn_local_fwd(qkvc, gb, gbt):
    t = qkvc.shape[0]
    nc = t // CHUNK
    hd = GDN_HEADS * GDN_DIM

    def body(x_ref, g_ref, gt_ref, t_ref, u_ref, w_ref, qg_ref, kd_ref, in_ref, eg_ref):
        chains = [c for cc in range(LOCAL_CHUNKS) for c in _load_chains(x_ref, g_ref, gt_ref, cc)]
        ii, jj = chains[0]["masks"][0:2]
        tms = _tri_inv_many([jnp.where(c["masks"][3], c["amat"] * c["decay"], 0.0) for c in chains], ii, jj)
        uws = [_dot(tm, jnp.concatenate([c["vb"], c["kbg"]], axis=1)) for tm, c in zip(tms, chains)]
        for c, tm, uw in zip(chains, tms, uws):
            cc, ch = c["cc"], c["ch"]
            t_ref[cc, ch] = tm
            u_ref[cc, ch] = uw[:, :GDN_DIM]
            w_ref[cc, ch] = uw[:, GDN_DIM:].astype(BF16)
            qg_ref[cc, ch] = c["qg"].astype(BF16)
            kd_ref[cc, ch] = c["kdec"].astype(BF16)
            in_ref[cc, ch] = c["intra"].astype(BF16)
            eg_ref[cc, ch:ch + 1, :] = jnp.broadcast_to(jnp.exp(c["glast"]), (1, LANES))

    lc = LOCAL_CHUNKS
    blk = lambda rows, cols: pl.BlockSpec((lc, N_CHAINS, rows, cols), lambda n: (n, 0, 0, 0))
    shapes = [_chain_shape(CHUNK, CHUNK, F32), _chain_shape(CHUNK, GDN_DIM, F32), _chain_shape(CHUNK, GDN_DIM, BF16),
              _chain_shape(CHUNK, GDN_DIM, BF16), _chain_shape(CHUNK, GDN_DIM, BF16), _chain_shape(CHUNK, CHUNK, BF16)]
    return tuple(pl.pallas_call(
        body, name="gdn_local_fwd", grid=(nc // lc,),
        in_specs=[pl.BlockSpec((lc * CHUNK, 3 * hd), lambda n: (n, 0)), pl.BlockSpec((lc * CHUNK, LANES), lambda n: (n, 0)),
                  pl.BlockSpec((lc, 16, CHUNK), lambda n: (n, 0, 0))],
        out_specs=[blk(CHUNK, CHUNK), blk(CHUNK, GDN_DIM), blk(CHUNK, GDN_DIM), blk(CHUNK, GDN_DIM),
                   blk(CHUNK, GDN_DIM), blk(CHUNK, CHUNK), pl.BlockSpec((lc, N_CHAINS, LANES), lambda n: (n, 0, 0))],
        out_shape=[s(nc) for s in shapes] + [jax.ShapeDtypeStruct((nc, N_CHAINS, LANES), F32)],
        compiler_params=_params(("arbitrary",), VMEM_LIMIT),
    )(qkvc, gb, gbt))


SCAN_CHUNKS = 8


def _dir_specs(nc, rev):
    nb = nc // SCAN_CHUNKS

    def spec(d, rows, cols, own=False):
        chunk = (lambda n: n) if (d == 0) != rev else (lambda n: nb - 1 - n)
        blk = 0 if own else d
        if rows is None:
            return pl.BlockSpec((SCAN_CHUNKS, GDN_HEADS if own else N_CHAINS, cols), lambda n: (chunk(n), 0, 0))
        return pl.BlockSpec((SCAN_CHUNKS, GDN_HEADS, rows, cols), lambda n: (chunk(n), blk, 0, 0))

    def rows_spec(d, cols):
        chunk = (lambda n: n) if (d == 0) != rev else (lambda n: nb - 1 - n)
        return pl.BlockSpec((SCAN_CHUNKS * CHUNK, cols), lambda n: (chunk(n), 0))

    def order(d):
        return list(range(SCAN_CHUNKS)) if (d == 0) != rev else list(range(SCAN_CHUNKS - 1, -1, -1))
    return spec, rows_spec, order


def _gdn_scan_fwd(u, w, qg, kd, intra, egl, t):
    nc = t // CHUNK
    hd = GDN_HEADS * GDN_DIM

    def body(*refs):
        ins, outs, state = refs[:12], refs[12:18], refs[18]
        @pl.when(pl.program_id(0) == 0)
        def _():
            state[...] = jnp.zeros_like(state)

        chains = [(d, h) for d in range(2) for h in range(GDN_HEADS)]
        states = [state[ch] for ch in range(N_CHAINS)]
        for step in range(SCAN_CHUNKS):
            at = [order(d)[step] for d in range(2)]
            pick = lambda k, d, h: ins[2 * k + d][at[d], h]
            sbs = [s.astype(BF16) for s in states]
            ws = [_dot(pick(1, d, h), sb) for (d, h), sb in zip(chains, sbs)]
            o1 = [_dot(pick(2, d, h), sb) for (d, h), sb in zip(chains, sbs)]
            vns = [(pick(0, d, h) - wsb).astype(BF16) for (d, h), wsb in zip(chains, ws)]
            o2 = [_dot(pick(4, d, h), vn) for (d, h), vn in zip(chains, vns)]
            kv = [_dot_tn(pick(3, d, h), vn) for (d, h), vn in zip(chains, vns)]
            new_states = []
            for ch, (d, h) in enumerate(chains):
                outs[d][at[d] * CHUNK:(at[d] + 1) * CHUNK, h * GDN_DIM:(h + 1) * GDN_DIM] = o1[ch] + o2[ch]
                outs[2 + d][at[d], h] = states[ch]
                outs[4 + d][at[d], h] = vns[ch]
                new_states.append(states[ch] * ins[10 + d][at[d], ch:ch + 1, :] + kv[ch])
            states = new_states
        for ch in range(N_CHAINS):
            state[ch] = states[ch]

    spec, rows_spec, order = _dir_specs(nc, False)
    pair = lambda rows, cols, own=False: [spec(0, rows, cols, own), spec(1, rows, cols, own)]
    s_shape = jax.ShapeDtypeStruct((nc, GDN_HEADS, GDN_DIM, GDN_DIM), F32)
    vn_shape = jax.ShapeDtypeStruct((nc, GDN_HEADS, CHUNK, GDN_DIM), BF16)
    return pl.pallas_call(
        body, name="gdn_scan_fwd", grid=(nc // SCAN_CHUNKS,),
        in_specs=(pair(CHUNK, GDN_DIM) + pair(CHUNK, GDN_DIM) + pair(CHUNK, GDN_DIM) + pair(CHUNK, GDN_DIM)
                  + pair(CHUNK, CHUNK) + pair(None, LANES)),
        out_specs=([rows_spec(0, hd), rows_spec(1, hd)] + pair(GDN_DIM, GDN_DIM, True)
                   + pair(CHUNK, GDN_DIM, True)),
        out_shape=[jax.ShapeDtypeStruct((t, hd), F32), jax.ShapeDtypeStruct((t, hd), F32),
                   s_shape, s_shape, vn_shape, vn_shape],
        scratch_shapes=[pltpu.VMEM((N_CHAINS, GDN_DIM, GDN_DIM), F32)],
        compiler_params=_params(("arbitrary",), VMEM_LIMIT),
    )(u, u, w, w, qg, qg, kd, kd, intra, intra, egl, egl)


def _gdn_bwd(qkvc, gb, gbt, do, saved, exchange=None):
    scan = _gdn_scan_bwd(do, saved, qkvc.shape[0])
    return _gdn_local_bwd(qkvc, gb, gbt, do, saved, scan, exchange)


def _gdn_scan_bwd(do, saved, t):
    nc = t // CHUNK
    hd = GDN_HEADS * GDN_DIM

    def body(*refs):
        ins, outs, dstate = refs[:16], refs[16:26], refs[26]
        @pl.when(pl.program_id(0) == 0)
        def _():
            dstate[...] = jnp.zeros_like(dstate)

        chains = [(d, h) for d in range(2) for h in range(GDN_HEADS)]
        dss = [dstate[ch] for ch in range(N_CHAINS)]
        for step in range(SCAN_CHUNKS):
            at = [order(d)[step] for d in range(2)]
            pick = lambda k, d, h: ins[2 * k + d][at[d], h]
            dsbs = [ds.astype(BF16) for ds in dss]
            ss = [pick(1, d, h) for d, h in chains]
            sbs = [s.astype(BF16) for s in ss]
            dos = [ins[d][at[d] * CHUNK:(at[d] + 1) * CHUNK, h * GDN_DIM:(h + 1) * GDN_DIM].astype(BF16)
                   for d, h in chains]
            dv1 = [_dot_tn(pick(5, d, h), dov) for (d, h), dov in zip(chains, dos)]
            dv2 = [_dot(pick(4, d, h), dsb) for (d, h), dsb in zip(chains, dsbs)]
            ds1 = [_dot_tn(pick(3, d, h), dov) for (d, h), dov in zip(chains, dos)]
            dkds = [_dot_nt(pick(6, d, h), dsb) for (d, h), dsb in zip(chains, dsbs)]
            dqgs = [_dot_nt(dov, sb) for dov, sb in zip(dos, sbs)]
            dvns = [(a + b).astype(BF16) for a, b in zip(dv1, dv2)]
            ds2 = [_dot_tn(pick(2, d, h), dvn) for (d, h), dvn in zip(chains, dvns)]
            dws = [_dot_nt(dvn, sb) for dvn, sb in zip(dvns, sbs)]
            new_dss = []
            for ch, (d, h) in enumerate(chains):
                egl = ins[14 + d][at[d], ch:ch + 1, :]
                outs[d][at[d], h] = dvns[ch]
                outs[2 + d][at[d], h] = (-dws[ch]).astype(BF16)
                outs[4 + d][at[d], h] = dqgs[ch]
                outs[6 + d][at[d], h] = dkds[ch]
                outs[8 + d][at[d], h:h + 1, :] = egl * jnp.sum(jnp.sum(ss[ch] * dss[ch], axis=1, keepdims=True),
                                                               axis=0, keepdims=True)
                new_dss.append(ds1[ch] + egl * dss[ch] - ds2[ch])
            dss = new_dss
        for ch in range(N_CHAINS):
            dstate[ch] = dss[ch]

    spec, rows_spec, order = _dir_specs(nc, True)
    pair = lambda rows, cols, own=False: [spec(0, rows, cols, own), spec(1, rows, cols, own)]
    s_f, s_b = saved["s"]
    vn_f, vn_b = saved["vn"]
    w, qg, kd, intra, egl = saved["w"], saved["qg"], saved["kd"], saved["intra"], saved["egl"]
    own = lambda rows, cols, dtype: jax.ShapeDtypeStruct((nc, GDN_HEADS, rows, cols), dtype)
    row_shape = jax.ShapeDtypeStruct((nc, GDN_HEADS, LANES), F32)
    return pl.pallas_call(
        body, name="gdn_scan_bwd", grid=(nc // SCAN_CHUNKS,),
        in_specs=([rows_spec(0, hd), rows_spec(1, hd)] + pair(GDN_DIM, GDN_DIM, True) + pair(CHUNK, GDN_DIM)
                  + pair(CHUNK, GDN_DIM) + pair(CHUNK, GDN_DIM) + pair(CHUNK, CHUNK) + pair(CHUNK, GDN_DIM, True)
                  + pair(None, LANES)),
        out_specs=(pair(CHUNK, GDN_DIM, True) + pair(CHUNK, GDN_DIM, True) + pair(CHUNK, GDN_DIM, True)
                   + pair(CHUNK, GDN_DIM, True) + pair(None, LANES, True)),
        out_shape=[own(CHUNK, GDN_DIM, BF16)] * 4 + [own(CHUNK, GDN_DIM, F32)] * 4 + [row_shape] * 2,
        scratch_shapes=[pltpu.VMEM((N_CHAINS, GDN_DIM, GDN_DIM), F32)],
        compiler_params=_params(("arbitrary",), VMEM_LIMIT),
    )(do, do, s_f, s_b, w, w, qg, qg, kd, kd, intra, intra, vn_f, vn_b, egl, egl)


def _dot3_nt(a, b):
    ah = a.astype(BF16)
    al = (a - ah.astype(F32)).astype(BF16)
    bh = b.astype(BF16)
    bl = (b - bh.astype(F32)).astype(BF16)
    return _dot_nt(ah, bh) + (_dot_nt(ah, bl) + _dot_nt(al, bh))


def _dot3_tn(a, b):
    ah = a.astype(BF16)
    al = (a - ah.astype(F32)).astype(BF16)
    bh = b.astype(BF16)
    bl = (b - bh.astype(F32)).astype(BF16)
    return _dot_tn(ah, bh) + (_dot_tn(ah, bl) + _dot_tn(al, bh))


def _gdn_local_bwd(qkvc, gb, gbt, do, saved, scan, exchange=None):
    t = qkvc.shape[0]
    nc = t // CHUNK
    hd = GDN_HEADS * GDN_DIM

    def body(*refs):
        x_ref, g_ref, gt_ref, do_ref, t_ref = refs[:5]
        per_dir = refs[5:17]
        dx_ref, dg_ref = refs[17:]
        chains = [c for cc in range(LOCAL_CHUNKS) for c in _load_chains(x_ref, g_ref, gt_ref, cc)]
        lane = lax.broadcasted_iota(jnp.int32, (CHUNK, LANES), 1)
        dgates = [jnp.zeros((CHUNK, LANES), F32) for _ in range(LOCAL_CHUNKS)]
        for c in chains:
            d = c["ch"] // GDN_HEADS
            vn_ref, dvn_ref, dw_ref, dqg_ref, dkd_ref, dgl_ref = per_dir[d::2]
            h, cc = c["h"], c["cc"]
            rows = slice(cc * CHUNK, (cc + 1) * CHUNK)
            c.update(tm=t_ref[cc, c["ch"]], dov=do_ref[rows, h * GDN_DIM:(h + 1) * GDN_DIM], vnew=vn_ref[cc, h],
                     dvnew=dvn_ref[cc, h], dw=dw_ref[cc, h], dqg=dqg_ref[cc, h], dkdec=dkd_ref[cc, h],
                     dglast=dgl_ref[cc, h:h + 1, 0:1])
        dintras = [_dot_nt(c["dov"], c["vnew"]) for c in chains]
        dts = [_dot_nt(c["dvnew"], c["vb"]) + _dot_nt(c["dw"], c["kbg"]) for c in chains]
        dvbs = [_dot_tn(c["tm"], c["dvnew"]) for c in chains]
        dkbgs = [_dot_tn(c["tm"], c["dw"]) for c in chains]
        tdts = [_dot3_nt(dt, c["tm"]) for dt, c in zip(dts, chains)]
        dls = [jnp.where(c["masks"][3], -_dot3_tn(c["tm"], tdt), 0.0) for tdt, c in zip(tdts, chains)]
        das = [dl * c["decay"] for dl, c in zip(dls, chains)]
        dqks = [jnp.where(c["masks"][2], di, 0.0) * c["decay"] for di, c in zip(dintras, chains)]
        dkb1 = [_dot(da, c["k"]) for da, c in zip(das, chains)]
        dk1 = [_dot_tn(da, c["kb"]) for da, c in zip(das, chains)]
        dk2 = [_dot_tn(dqk, c["q"]) for dqk, c in zip(dqks, chains)]
        dq1 = [_dot(dqk, c["k"]) for dqk, c in zip(dqks, chains)]
        grads, mms, p_gs, p_betas, p_kds = [], [], [], [], []
        for n, c in enumerate(chains):
            incl = c["masks"][2]
            dkb = dkb1[n] + dkbgs[n] * c["eg"]
            kd = c["dkdec"] * c["kdec"]
            mms.append((dls[n] * c["amat"] + jnp.where(incl, dintras[n], 0.0) * c["qk"]) * c["decay"])
            p_gs.append(c["dqg"] * c["qg"] - kd + dkbgs[n] * c["kbg"])
            p_betas.append(dkb * c["k"] + dvbs[n] * c["v"])
            p_kds.append(kd)
            grads.append((dq1[n] + c["dqg"] * c["eg"],
                          dk1[n] + dk2[n] + c["dkdec"] * c["ek"] + dkb * c["bcol"],
                          dvbs[n] * c["bcol"]))
        row_sums = [jnp.sum(mm, axis=1, keepdims=True) for mm in mms]
        col_sums = [jnp.sum(mm, axis=0, keepdims=True) for mm in mms]
        g_sums = [jnp.sum(pg, axis=1, keepdims=True) for pg in p_gs]
        dbetas = [jnp.sum(pb, axis=1, keepdims=True) for pb in p_betas]
        kd_tots = [jnp.sum(jnp.sum(pk, axis=1, keepdims=True), axis=0, keepdims=True) for pk in p_kds]
        dgcs = [rs - _row_to_col(cs, *c["masks"][0:2]) + gs for rs, cs, gs, c in zip(row_sums, col_sums, g_sums, chains)]
        dgrs = [_col_to_row(dgc, *c["masks"][0:2]) for dgc, c in zip(dgcs, chains)]
        draws = [jnp.sum(jnp.where(jnp.logical_not(c["masks"][3]), dgr, 0.0), axis=1, keepdims=True) + c["dglast"] + kt
                 for dgr, kt, c in zip(dgrs, kd_tots, chains)]
        for c, draw, dbeta in zip(chains, draws, dbetas):
            ch = c["ch"]
            dgates[c["cc"]] = dgates[c["cc"]] + jnp.where(lane == ch, draw, 0.0) + jnp.where(lane == 8 + ch, dbeta, 0.0)
        for cc in range(LOCAL_CHUNKS):
            rows = slice(cc * CHUNK, (cc + 1) * CHUNK)
            for h in range(GDN_HEADS):
                for part in range(3):
                    cols = slice(part * hd + h * GDN_DIM, part * hd + (h + 1) * GDN_DIM)
                    dx_ref[rows, cols] = grads[cc * N_CHAINS + h][part] + grads[cc * N_CHAINS + GDN_HEADS + h][part]
            dg_ref[rows, :] = dgates[cc]

    lc = LOCAL_CHUNKS
    all8 = lambda rows, cols: pl.BlockSpec((lc, N_CHAINS, rows, cols), lambda n: (n, 0, 0, 0))
    own4 = lambda rows, cols: pl.BlockSpec((lc, GDN_HEADS, rows, cols), lambda n: (n, 0, 0, 0))
    row4 = pl.BlockSpec((lc, GDN_HEADS, LANES), lambda n: (n, 0, 0))
    vn_f, vn_b = saved["vn"]
    dvn_f, dvn_b, dw_f, dw_b, dqg_f, dqg_b, dkd_f, dkd_b, dgl_f, dgl_b = scan
    return _grid_call(
        body, "gdn_local_bwd", nc // lc,
        [pl.BlockSpec((lc * CHUNK, 3 * hd), lambda n: (n, 0)), pl.BlockSpec((lc * CHUNK, LANES), lambda n: (n, 0)),
         pl.BlockSpec((lc, 16, CHUNK), lambda n: (n, 0, 0)), pl.BlockSpec((lc * CHUNK, hd), lambda n: (n, 0)),
         all8(CHUNK, CHUNK)] + [own4(CHUNK, GDN_DIM)] * 10 + [row4, row4],
        [pl.BlockSpec((lc * CHUNK, 3 * hd), lambda n: (n, 0)), pl.BlockSpec((lc * CHUNK, LANES), lambda n: (n, 0))],
        [jax.ShapeDtypeStruct((t, 3 * hd), F32), jax.ShapeDtypeStruct((t, LANES), F32)],
        (qkvc, gb, gbt, do, saved["tm"], vn_f, vn_b, dvn_f, dvn_b, dw_f, dw_b, dqg_f, dqg_b, dkd_f, dkd_b, dgl_f, dgl_b),
        exchange=exchange)


def _gdn_post_fwd(of, ob, z, gw, tm):
    t, hd = of.shape

    def body(of_ref, ob_ref, z_ref, w_ref, o_ref):
        for h in range(GDN_HEADS):
            cols = slice(h * GDN_DIM, (h + 1) * GDN_DIM)
            o = of_ref[:, cols] + ob_ref[:, cols]
            zv = z_ref[:, cols]
            o_ref[:, cols] = (o * _rstd(o) * w_ref[...] * (zv * _sigmoid(zv))).astype(BF16)

    row = pl.BlockSpec((tm, hd), lambda i: (i, 0))
    return pl.pallas_call(
        body, name="gdn_post_fwd", grid=(t // tm,),
        in_specs=[row, row, row, _resident((1, GDN_DIM))],
        out_specs=row, out_shape=jax.ShapeDtypeStruct((t, hd), BF16),
        compiler_params=_params(("arbitrary",), VMEM_LIMIT),
    )(of, ob, z, gw)


def _gdn_post_bwd(doa, of, ob, z, gw, tm):
    t, hd = of.shape

    def body(d_ref, of_ref, ob_ref, z_ref, w_ref, do_ref, dz_ref, dw_ref):
        @pl.when(pl.program_id(0) == 0)
        def _():
            dw_ref[...] = jnp.zeros_like(dw_ref)

        dw = jnp.zeros((1, GDN_DIM), F32)
        for h in range(GDN_HEADS):
            cols = slice(h * GDN_DIM, (h + 1) * GDN_DIM)
            o = of_ref[:, cols] + ob_ref[:, cols]
            zv = z_ref[:, cols]
            dv = d_ref[:, cols]
            r = _rstd(o)
            sg = _sigmoid(zv)
            on = o * r * w_ref[...]
            dz_ref[:, cols] = (dv * on * (sg * (1.0 + zv * (1.0 - sg)))).astype(BF16)
            dxr, dwh = _rms_bwd(o, r, w_ref[...], dv * (zv * sg))
            do_ref[:, cols] = dxr
            dw = dw + dwh
        dw_ref[...] += dw

    row = pl.BlockSpec((tm, hd), lambda i: (i, 0))
    return pl.pallas_call(
        body, name="gdn_post_bwd", grid=(t // tm,),
        in_specs=[row, row, row, row, _resident((1, GDN_DIM))],
        out_specs=[row, row, pl.BlockSpec((1, GDN_DIM), lambda i: (0, 0))],
        out_shape=[jax.ShapeDtypeStruct((t, hd), F32), jax.ShapeDtypeStruct((t, hd), BF16),
                   jax.ShapeDtypeStruct((1, GDN_DIM), F32)],
        compiler_params=_params(("arbitrary",), VMEM_LIMIT),
    )(doa, of, ob, z, gw)


SWA_W = SWA_HEADS * SWA_DIM
QBLK = 128
KWIN = QBLK + 2 * RADIUS
WIN_OFFSETS = (0, RADIUS, 2 * RADIUS)


def _t5_bucket(rel):
    nb = REL_BUCKETS // 2
    bucket = (rel > 0).astype(np.int32) * nb
    n = np.abs(rel)
    max_exact = nb // 2
    large = max_exact + (np.log(np.maximum(n, 1) / max_exact)
                         / math.log(REL_MAX_DISTANCE / max_exact) * (nb - max_exact)).astype(np.int32)
    large = np.minimum(large, nb - 1)
    return (bucket + np.where(n < max_exact, n, large)).astype(np.int32)


def _band_tables(dilation):
    a = np.arange(QBLK)
    b = np.arange(KWIN)
    rel = np.stack([b[None, :] - w0 - a[:, None] for w0 in WIN_OFFSETS])
    return np.where(np.abs(rel) <= RADIUS, _t5_bucket(rel * dilation), -1).astype(np.int32)


BAND_CELLS = len(WIN_OFFSETS) * QBLK * KWIN
BIAS_TILE = BAND_CELLS // 3


def _band_index():
    return jnp.asarray(np.concatenate([_band_tables(d).reshape(-1) for _, d in PATTERNS])[None, :])


def _onehot(idx, dtype):
    return (lax.broadcasted_iota(jnp.int32, (REL_BUCKETS, idx.shape[1]), 0) == idx).astype(dtype)


def _bias_tables(rel_bias, idx, tk):
    n = idx.shape[1]

    def body(rb_ref, i_ref, o_ref):
        iv = i_ref[...]
        oh = _onehot(iv, BF16)
        rest, acc = rb_ref[...], None
        for _ in range(3):
            piece = rest.astype(BF16)
            part = jnp.dot(piece, oh, preferred_element_type=F32)
            acc = part if acc is None else acc + part
            rest = rest - piece.astype(F32)
        o_ref[...] = jnp.where(iv < 0, NEG_BIG, acc)

    return pl.pallas_call(
        body, name="bias_tables", grid=(n // tk,),
        in_specs=[_resident((SWA_HEADS, REL_BUCKETS)), pl.BlockSpec((1, tk), lambda k: (0, k))],
        out_specs=pl.BlockSpec((SWA_HEADS, tk), lambda k: (0, k)),
        out_shape=jax.ShapeDtypeStruct((SWA_HEADS, n), F32),
        compiler_params=_params(("arbitrary",), VMEM_LIMIT),
    )(rel_bias.T, idx)


def _head_mean(x2, bd_ref):
    bd = bd_ref[...]
    rest, acc = x2, None
    for _ in range(3):
        piece = rest.astype(BF16)
        part = jnp.dot(piece, bd, preferred_element_type=F32)
        acc = part if acc is None else acc + part
        rest = rest - piece.astype(F32)
    return acc


VIEW_DILATIONS = tuple(d for _, d in PATTERNS if d > 1)


def _view_spec(tm, d):
    return pl.BlockSpec((tm // d, d * SWA_W), lambda i: (i, 0))


def _view_shape(t, d, dtype):
    return jax.ShapeDtypeStruct((t // d, d * SWA_W), dtype)


N_GROUPS = SWA_W // LANES


def _to_view(src_ref, idx, dst_ref, d, rows):
    for r in range(d):
        for g in range(N_GROUPS):
            cols = slice(r * SWA_W + g * LANES, r * SWA_W + (g + 1) * LANES)
            dst_ref[:, cols] = src_ref[idx, g, pl.ds(r, rows // d, stride=d), :].astype(dst_ref.dtype)


def _from_view(src_ref, dst_ref, idx, d, rows):
    for r in range(d):
        for g in range(N_GROUPS):
            cols = slice(r * SWA_W + g * LANES, r * SWA_W + (g + 1) * LANES)
            dst_ref[idx, g, pl.ds(r, rows // d, stride=d), :] = src_ref[:, cols]


def _swa_prep_fwd(qkvb, qw, kw, bd, tm):
    t = qkvb.shape[0]

    def body(x_ref, qw_ref, kw_ref, bd_ref, *rest):
        outs, sc = rest[:-1], rest[-1]
        for gidx in range(N_GROUPS):
            cols = slice(gidx * LANES, (gidx + 1) * LANES)
            xq = x_ref[:, cols]
            sc[0, gidx] = xq * lax.rsqrt(_head_mean(xq * xq, bd_ref) + EPS) * qw_ref[:, cols] * (SWA_DIM ** -0.5)
            xk = x_ref[:, SWA_W + gidx * LANES:SWA_W + (gidx + 1) * LANES]
            sc[1, gidx] = xk * lax.rsqrt(_head_mean(xk * xk, bd_ref) + EPS) * kw_ref[:, cols]
            sc[2, gidx] = x_ref[:, 2 * SWA_W + gidx * LANES:2 * SWA_W + (gidx + 1) * LANES]
            for i in range(3):
                outs[i][:, cols] = sc[i, gidx].astype(BF16)
        for i in range(3):
            for n, d in enumerate(VIEW_DILATIONS):
                _to_view(sc, i, outs[3 * (n + 1) + i], d, tm)

    return pl.pallas_call(
        body, name="swa_prep_fwd", grid=(t // tm,),
        in_specs=[pl.BlockSpec((tm, 3 * SWA_W), lambda i: (i, 0)), _resident((1, SWA_W)), _resident((1, SWA_W)),
                  _resident((LANES, LANES))],
        out_specs=[_view_spec(tm, d) for d in (1,) + VIEW_DILATIONS for _ in range(3)],
        out_shape=[_view_shape(t, d, BF16) for d in (1,) + VIEW_DILATIONS for _ in range(3)],
        scratch_shapes=[pltpu.VMEM((3, N_GROUPS, tm, LANES), F32)],
        compiler_params=_params(("arbitrary",), VMEM_LIMIT),
    )(qkvb, qw, kw, bd)


def _swa_prep_bwd(qkvb, qw, kw, bd, grads, tm):
    t = qkvb.shape[0]

    def body(x_ref, qw_ref, kw_ref, bd_ref, *rest):
        parts, (dx_ref, dqw_ref, dkw_ref, sc) = rest[:9], rest[9:]
        @pl.when(pl.program_id(0) == 0)
        def _():
            dqw_ref[...] = jnp.zeros_like(dqw_ref)
            dkw_ref[...] = jnp.zeros_like(dkw_ref)

        for i in range(3):
            for n, d in enumerate(VIEW_DILATIONS):
                _from_view(parts[3 * (n + 1) + i], sc, 2 * i + n, d, tm)
        for gidx in range(N_GROUPS):
            cols = slice(gidx * LANES, (gidx + 1) * LANES)
            for i, base, w_ref, dw_ref, scale in ((0, 0, qw_ref, dqw_ref, SWA_DIM ** -0.5),
                                                  (1, SWA_W, kw_ref, dkw_ref, 1.0)):
                xv = x_ref[:, base + gidx * LANES:base + (gidx + 1) * LANES]
                dy = (parts[i][:, cols] + sc[2 * i, gidx] + sc[2 * i + 1, gidx]) * scale
                r = lax.rsqrt(_head_mean(xv * xv, bd_ref) + EPS)
                xhat = xv * r
                dxh = dy * w_ref[:, cols]
                dx = r * (dxh - xhat * _head_mean(dxh * xhat, bd_ref))
                dx_ref[:, base + gidx * LANES:base + (gidx + 1) * LANES] = dx.astype(BF16)
                dw_ref[:, cols] += jnp.sum(dy * xhat, axis=0, keepdims=True)
            dx_ref[:, 2 * SWA_W + gidx * LANES:2 * SWA_W + (gidx + 1) * LANES] = (
                parts[2][:, cols] + sc[4, gidx] + sc[5, gidx]).astype(BF16)

    wrow = pl.BlockSpec((1, SWA_W), lambda i: (0, 0))
    return pl.pallas_call(
        body, name="swa_prep_bwd", grid=(t // tm,),
        in_specs=[pl.BlockSpec((tm, 3 * SWA_W), lambda i: (i, 0)), _resident((1, SWA_W)), _resident((1, SWA_W)),
                  _resident((LANES, LANES))] + [_view_spec(tm, d) for d in (1,) + VIEW_DILATIONS for _ in range(3)],
        out_specs=[pl.BlockSpec((tm, 3 * SWA_W), lambda i: (i, 0)), wrow, wrow],
        out_shape=[jax.ShapeDtypeStruct((t, 3 * SWA_W), BF16), jax.ShapeDtypeStruct((1, SWA_W), F32),
                   jax.ShapeDtypeStruct((1, SWA_W), F32)],
        scratch_shapes=[pltpu.VMEM((6, N_GROUPS, tm, LANES), F32)],
        compiler_params=_params(("arbitrary",), VMEM_LIMIT),
    )(qkvb, qw, kw, bd, *grads)


def _aligned(v, m):
    return v if isinstance(v, int) else pl.multiple_of(v, m)


BAND_GROUP = 2


def _band_loop(nsub, length, step, group=BAND_GROUP):
    step([(0, 0)], 0)
    if nsub > 2:
        assert (nsub - 2) % group == 0

        def inner(i, carry):
            s0 = 1 + i * group
            step([(s0 + e, pl.multiple_of((s0 + e) * QBLK - RADIUS, RADIUS)) for e in range(group)], 1)
            return carry
        lax.fori_loop(0, (nsub - 2) // group, inner, 0)
    step([(nsub - 1, length - KWIN)], 2)


def _head_select(lane, a0, a1):
    return jnp.where(lane < SWA_DIM, a0, a1)


def _swa_fwd(qv, kv, vv, bias, dilation, name):
    length = qv.shape[0]
    nsub = length // QBLK
    assert nsub >= 2 and length % QBLK == 0

    def body(q_ref, k_ref, v_ref, b_ref, o_ref, l_ref):
        lane = lax.broadcasted_iota(jnp.int32, (QBLK, LANES), 1)

        def step(blocks, var):
            items = []
            for s, ws in blocks:
                rows = pl.ds(_aligned(s * QBLK, QBLK), QBLK)
                q, kk, vw = q_ref[rows, :], k_ref[pl.ds(ws, KWIN), :], v_ref[pl.ds(ws, KWIN), :]
                for hh in range(2):
                    items.append((hh, jnp.where((lane < SWA_DIM) == (hh == 0), q, jnp.zeros_like(q)), kk, vw))
            lgs = [_dot_nt(qh, kk) + b_ref[hh, var] for hh, qh, kk, _ in items]
            ms = [jnp.max(lg, axis=-1, keepdims=True) for lg in lgs]
            ps = [jnp.exp(lg - m) for lg, m in zip(lgs, ms)]
            dens = [jnp.sum(p, axis=-1, keepdims=True) for p in ps]
            pvs = [_dot(p, it[3]) for p, it in zip(ps, items)]
            for n, (s, _) in enumerate(blocks):
                rows = pl.ds(_aligned(s * QBLK, QBLK), QBLK)
                o0, o1 = (pvs[2 * n + hh] / dens[2 * n + hh] for hh in range(2))
                l0, l1 = (ms[2 * n + hh] + jnp.log(dens[2 * n + hh]) for hh in range(2))
                o_ref[rows, :] = _head_select(lane, o0, o1)
                l_ref[rows, :] = _head_select(lane, l0, l1)

        _band_loop(nsub, length, step)

    blk = pl.BlockSpec((length, LANES), lambda hp, r: (0, r * (SWA_W // LANES) + hp))
    shp = jax.ShapeDtypeStruct(qv.shape, F32)
    return pl.pallas_call(
        body, name=name, grid=(SWA_W // LANES, dilation),
        in_specs=[blk, blk, blk, pl.BlockSpec((2, 3, QBLK, KWIN), lambda hp, r: (hp, 0, 0, 0))],
        out_specs=[blk, blk], out_shape=[shp, shp],
        compiler_params=_params(("arbitrary", "arbitrary"), VMEM_LIMIT),
    )(qv, kv, vv, bias)


def _swa_combine(os_, ls_, tm):
    t = os_[0].shape[0]

    def body(o0, o1, o2, l0, l1, l2, o_ref, ob_ref, la_ref, lb_ref, lc_ref, sc):
        for n, d in enumerate(VIEW_DILATIONS):
            _from_view((o1, o2)[n], sc, n, d, tm)
            _from_view((l1, l2)[n], sc, 2 + n, d, tm)
        for g in range(N_GROUPS):
            cols = slice(g * LANES, (g + 1) * LANES)
            la, lb, lc = l0[:, cols], sc[2, g], sc[3, g]
            m = jnp.maximum(jnp.maximum(la, lb), lc)
            tot = m + jnp.log(jnp.exp(la - m) + jnp.exp(lb - m) + jnp.exp(lc - m))
            o = jnp.exp(la - tot) * o0[:, cols] + jnp.exp(lb - tot) * sc[0, g] + jnp.exp(lc - tot) * sc[1, g]
            o_ref[:, cols] = o
            ob_ref[:, cols] = o.astype(BF16)
            la_ref[:, cols] = tot
            sc[4, g] = tot
        for n, d in enumerate(VIEW_DILATIONS):
            _to_view(sc, 4, (lb_ref, lc_ref)[n], d, tm)

    specs = [_view_spec(tm, d) for d in (1,) + VIEW_DILATIONS]
    return pl.pallas_call(
        body, name="swa_combine", grid=(t // tm,), in_specs=specs + specs, out_specs=[specs[0], specs[0]] + specs,
        out_shape=[jax.ShapeDtypeStruct((t, SWA_W), F32), jax.ShapeDtypeStruct((t, SWA_W), BF16)]
                  + [_view_shape(t, d, F32) for d in (1,) + VIEW_DILATIONS],
        scratch_shapes=[pltpu.VMEM((5, N_GROUPS, tm, LANES), F32)],
        compiler_params=_params(("arbitrary",), VMEM_LIMIT),
    )(*os_, *ls_)


def _swa_bwd_prep(do, o, bd, tm):
    t = do.shape[0]

    def body(d_ref, o_ref, bd_ref, dd1, dd4, dd16, db1, db4, db16, sc):
        for gidx in range(N_GROUPS):
            cols = slice(gidx * LANES, (gidx + 1) * LANES)
            dv = d_ref[:, cols]
            dd = _head_mean(dv * o_ref[:, cols], bd_ref) * float(SWA_DIM)
            sc[0, gidx] = dd
            sc[1, gidx] = dv
            dd1[:, cols] = dd
            db1[:, cols] = dv.astype(BF16)
        for n, d in enumerate(VIEW_DILATIONS):
            _to_view(sc, 0, (dd4, dd16)[n], d, tm)
            _to_view(sc, 1, (db4, db16)[n], d, tm)

    specs = [_view_spec(tm, d) for d in (1,) + VIEW_DILATIONS]
    return pl.pallas_call(
        body, name="swa_bwd_prep", grid=(t // tm,), in_specs=[specs[0], specs[0], _resident((LANES, LANES))],
        out_specs=specs + specs,
        out_shape=[_view_shape(t, d, F32) for d in (1,) + VIEW_DILATIONS]
                  + [_view_shape(t, d, BF16) for d in (1,) + VIEW_DILATIONS],
        scratch_shapes=[pltpu.VMEM((2, N_GROUPS, tm, LANES), F32)],
        compiler_params=_params(("arbitrary",), VMEM_LIMIT),
    )(do, o, bd)


def _swa_bwd(qv, kv, vv, dov, lv, ddv, bias_a, dilation, name):
    length = qv.shape[0]
    nsub = length // QBLK
    single = pl.Buffered(1) if dilation == 1 else None

    def body(q_ref, k_ref, v_ref, do_ref, l_ref, dd_ref, ba_ref, dq_ref, dk_ref, dv_ref, db_ref):
        @pl.when(pl.program_id(1) == 0)
        def _():
            db_ref[...] = jnp.zeros_like(db_ref)

        lane = lax.broadcasted_iota(jnp.int32, (QBLK, LANES), 1)
        lanew = lax.broadcasted_iota(jnp.int32, (KWIN, LANES), 1)

        def step(blocks, var):
            items = []
            for s, ws in blocks:
                rows = pl.ds(_aligned(s * QBLK, QBLK), QBLK)
                win = pl.ds(ws, KWIN)
                q, dov_ = q_ref[rows, :], do_ref[rows, :]
                kk, vw = k_ref[win, :], v_ref[win, :]
                lse, dd = l_ref[rows, :], dd_ref[rows, :]
                for hh in range(2):
                    mine = (lane < SWA_DIM) == (hh == 0)
                    col = slice(hh * SWA_DIM, hh * SWA_DIM + 1)
                    items.append((hh, jnp.where(mine, q, jnp.zeros_like(q)), jnp.where(mine, dov_, jnp.zeros_like(dov_)),
                                  kk, vw, lse[:, col], dd[:, col], q, dov_))
            lgs = [_dot_nt(it[1], it[3]) + ba_ref[it[0], var] for it in items]
            dps = [_dot_nt(it[2], it[4]) for it in items]
            ps = [jnp.exp(lg - it[5]) for lg, it in zip(lgs, items)]
            dss = [p * (dp - it[6]) for p, dp, it in zip(ps, dps, items)]
            dqs = [_dot(ds, it[3]) for ds, it in zip(dss, items)]
            dks = [_dot_tn(ds, it[7]) for ds, it in zip(dss, items)]
            dvs = [_dot_tn(p, it[8]) for p, it in zip(ps, items)]
            for n, (s, ws) in enumerate(blocks):
                rows = pl.ds(_aligned(s * QBLK, QBLK), QBLK)
                win = pl.ds(ws, KWIN)
                dq_ref[rows, :] = _head_select(lane, dqs[2 * n], dqs[2 * n + 1])
                dk_ref[win, :] += _head_select(lanew, dks[2 * n], dks[2 * n + 1])
                dv_ref[win, :] += _head_select(lanew, dvs[2 * n], dvs[2 * n + 1])
            for hh in range(2):
                tot = dss[hh]
                for n in range(1, len(blocks)):
                    tot = tot + dss[2 * n + hh]
                db_ref[hh, var] += tot

        dk_ref[...] = jnp.zeros_like(dk_ref)
        dv_ref[...] = jnp.zeros_like(dv_ref)
        _band_loop(nsub, length, step)

    imap = lambda hp, r: (0, r * (SWA_W // LANES) + hp)
    blk_in = pl.BlockSpec((length, LANES), imap, pipeline_mode=single)
    blk_out = pl.BlockSpec((length, LANES), imap)
    shp = jax.ShapeDtypeStruct(qv.shape, F32)
    return pl.pallas_call(
        body, name=name, grid=(SWA_W // LANES, dilation),
        in_specs=[blk_in] * 6 + [pl.BlockSpec((2, 3, QBLK, KWIN), lambda hp, r: (hp, 0, 0, 0))],
        out_specs=[blk_out, blk_out, blk_out, pl.BlockSpec((2, 3, QBLK, KWIN), lambda hp, r: (hp, 0, 0, 0))],
        out_shape=[shp, shp, shp, jax.ShapeDtypeStruct((SWA_HEADS, 3, QBLK, KWIN), F32)],
        compiler_params=_params(("arbitrary", "arbitrary"), VMEM_LIMIT),
    )(qv, kv, vv, dov, lv, ddv, bias_a)


def _bias_grad(ds2, idx, tk):
    n = ds2.shape[1]
    nk = n // tk

    def body(a_ref, i_ref, o_ref):
        @pl.when(pl.program_id(0) == 0)
        def _():
            o_ref[...] = jnp.zeros_like(o_ref)

        oh = _onehot(i_ref[...], BF16)
        rest = a_ref[...]
        acc = jnp.zeros((SWA_HEADS, REL_BUCKETS), F32)
        for _ in range(3):
            piece = rest.astype(BF16)
            acc = acc + _dot_nt(piece, oh)
            rest = rest - piece.astype(F32)
        o_ref[...] += acc

    return pl.pallas_call(
        body, name="bias_grad", grid=(nk,),
        in_specs=[pl.BlockSpec((SWA_HEADS, tk), lambda k: (0, k)), pl.BlockSpec((1, tk), lambda k: (0, k))],
        out_specs=pl.BlockSpec((SWA_HEADS, REL_BUCKETS), lambda k: (0, 0)),
        out_shape=jax.ShapeDtypeStruct((SWA_HEADS, REL_BUCKETS), F32),
        compiler_params=_params(("arbitrary",), VMEM_LIMIT),
    )(ds2, idx)


def _swa_branch_fwd(qkvb, qw_t, kw_t, rel_bias, bd, tm):
    qkv = _swa_prep_fwd(qkvb, qw_t, kw_t, bd, tm)
    tables = _bias_tables(rel_bias, _band_index(), BIAS_TILE)
    os_, ls_, tabs = [], [], []
    for n, (_, d) in enumerate(PATTERNS):
        bias = tables[:, n * BAND_CELLS:(n + 1) * BAND_CELLS].reshape(SWA_HEADS, len(WIN_OFFSETS), QBLK, KWIN)
        o_p, l_p = _swa_fwd(*qkv[3 * n:3 * n + 3], bias, d, f"swa_fwd_d{d}")
        os_.append(o_p)
        ls_.append(l_p)
        tabs.append(bias)
    o, o16, *lses = _swa_combine(os_, ls_, tm)
    return o, o16, (qkv, lses, tabs)


def _swa_branch_bwd(do, o, saved, qkvb, qw_t, kw_t, bd, tm):
    qkv, lses, tabs = saved
    prep = _swa_bwd_prep(do, o, bd, tm)
    grads, dss = [], []
    for n, ((_, d), bias) in enumerate(zip(PATTERNS, tabs)):
        dq, dk, dv, ds = _swa_bwd(*qkv[3 * n:3 * n + 3], prep[3 + n], lses[n], prep[n], bias, d, f"swa_bwd_d{d}")
        grads += [dq, dk, dv]
        dss.append(ds.reshape(SWA_HEADS, -1))
    dqkvb, dqw, dkw = _swa_prep_bwd(qkvb, qw_t, kw_t, bd, grads, tm)
    dbias = _bias_grad(jnp.concatenate(dss, axis=1), _band_index(), BIAS_TILE)
    fold = lambda w: jnp.sum(w.reshape(SWA_HEADS, SWA_DIM), axis=0)
    return dqkvb, fold(dqw), fold(dkw), dbias.T


def _mesh_pos():
    return lax.axis_index("x"), lax.axis_index("y"), lax.axis_index("c")


def _other_chips(x, y):
    return [(1 - x, y), (x, 1 - y), (1 - x, 1 - y)]


def _remote(src, dst, send_sem, recv_sem, device):
    return pltpu.make_async_remote_copy(src_ref=src, dst_ref=dst, send_sem=send_sem, recv_sem=recv_sem,
                                        device_id=device, device_id_type=MESH)


def _split_axis(shape2):
    return 0 if (shape2[0] // 2) % 16 == 0 else 1


def _half_index(shape2, c):
    axis = _split_axis(shape2)
    h = shape2[axis] // 2
    return (pl.ds(c * h, h), slice(None)) if axis == 0 else (slice(None), pl.ds(c * h, h))


def _all_gather(xs):
    n = len(xs)

    def body(*refs):
        ins, outs = refs[:n], refs[n:2 * n]
        send_sems, recv_sems = refs[2 * n:]
        x, y, c = _mesh_pos()
        me = 2 * x + y
        chips = _other_chips(x, y)
        halves = []
        sends = []
        for a in range(n):
            h = ins[a].shape[0] // 2
            mine, other = pl.ds(c * h, h), pl.ds((1 - c) * h, h)
            halves.append((mine, other))
            own = _remote(ins[a], outs[a].at[me], send_sems.at[a, 6], recv_sems.at[a, 6], (x, y, 1 - c))
            own.start()
            sends.append(own)
            for j, chip in enumerate(chips):
                cp = _remote(ins[a].at[mine], outs[a].at[me, mine], send_sems.at[a, j], recv_sems.at[a, j], (*chip, c))
                cp.start()
                sends.append(cp)
        for a in range(n):
            mine, _ = halves[a]
            for j, chip in enumerate(chips):
                src = 2 * chip[0] + chip[1]
                landed = outs[a].at[src, mine]
                _remote(landed, landed, send_sems.at[a, j], recv_sems.at[a, j], (x, y, c)).wait_recv()
                fwd = _remote(landed, landed, send_sems.at[a, 3 + j], recv_sems.at[a, 3 + j], (x, y, 1 - c))
                fwd.start()
                sends.append(fwd)
        for a in range(n):
            _, other = halves[a]
            for j, chip in enumerate(chips):
                src = 2 * chip[0] + chip[1]
                landed = outs[a].at[src, other]
                _remote(landed, landed, send_sems.at[a, 3 + j], recv_sems.at[a, 3 + j], (x, y, c)).wait_recv()
            mine_slot = outs[a].at[me]
            _remote(mine_slot, mine_slot, send_sems.at[a, 6], recv_sems.at[a, 6], (x, y, c)).wait_recv()
        for cp in sends:
            cp.wait_send()

    return list(pl.pallas_call(
        body, name="all_gather_weights",
        in_specs=[ANY] * n, out_specs=[ANY] * n,
        out_shape=[jax.ShapeDtypeStruct((N_SHARDS,) + a.shape, a.dtype) for a in xs],
        scratch_shapes=[pltpu.SemaphoreType.DMA((n, 7)), pltpu.SemaphoreType.DMA((n, 7))],
    )(*xs))


def _rs_pair(gs):
    n = len(gs)

    def body(*refs):
        ins, lands = refs[:n], refs[n:2 * n]
        send_sems, recv_sems = refs[2 * n:]
        x, y, c = _mesh_pos()
        cps = []
        for a in range(n):
            theirs = (slice(None),) + _half_index(ins[a].shape[1:], 1 - c)
            cp = _remote(ins[a].at[theirs], lands[a], send_sems.at[a], recv_sems.at[a], (x, y, 1 - c))
            cp.start()
            cps.append(cp)
        for cp in cps:
            cp.wait()

    def half_shape(g):
        dims = list(g.shape)
        dims[1 + _split_axis(g.shape[1:])] //= 2
        return tuple(dims)

    return list(pl.pallas_call(
        body, name="rs_pair", in_specs=[ANY] * n, out_specs=[ANY] * n,
        out_shape=[jax.ShapeDtypeStruct(half_shape(g), g.dtype) for g in gs],
        scratch_shapes=[pltpu.SemaphoreType.DMA((n,)), pltpu.SemaphoreType.DMA((n,))],
    )(*gs))


def _pair_exchange(gs):
    def copies(cin, cout, send_sems, recv_sems):
        x, y, c = _mesh_pos()
        return [_remote(g.at[(slice(None),) + _half_index(g.shape[1:], 1 - c)], land, send_sems.at[a, 0],
                        recv_sems.at[a, 0], (x, y, 1 - c)) for a, (g, land) in enumerate(zip(cin, cout))]

    def start(*refs):
        for cp in copies(*refs):
            cp.start()

    def finish(*refs):
        for cp in copies(*refs):
            cp.wait()

    def half_shape(g):
        dims = list(g.shape)
        dims[1 + _split_axis(g.shape[1:])] //= 2
        return tuple(dims)

    return _Exchange(tuple(gs), tuple(jax.ShapeDtypeStruct(half_shape(g), g.dtype) for g in gs), start, finish)


def _rs_chips(ss):
    n = len(ss)

    def body(*refs):
        ins, outs = refs[:n], refs[n:2 * n]
        send_sems, recv_sems = refs[2 * n:]
        x, y, c = _mesh_pos()
        me = 2 * x + y
        chips = _other_chips(x, y)
        cps = []
        for a in range(n):
            for j, chip in enumerate(chips):
                dst_chip = 2 * chip[0] + chip[1]
                cp = _remote(ins[a].at[dst_chip], outs[a].at[me], send_sems.at[a, j], recv_sems.at[a, j], (*chip, c))
                cp.start()
                cps.append(cp)
        for a in range(n):
            for j, chip in enumerate(chips):
                src = 2 * chip[0] + chip[1]
                _remote(outs[a].at[src], outs[a].at[src], send_sems.at[a, j], recv_sems.at[a, j], (x, y, c)).wait_recv()
        for cp in cps:
            cp.wait_send()

    return list(pl.pallas_call(
        body, name="rs_chips", in_specs=[ANY] * n, out_specs=[ANY] * n,
        out_shape=[jax.ShapeDtypeStruct(s.shape, s.dtype) for s in ss],
        scratch_shapes=[pltpu.SemaphoreType.DMA((n, 3)), pltpu.SemaphoreType.DMA((n, 3))],
    )(*ss))


def _rs_join(fs, axes):
    n = len(fs)

    def whole(f, axis):
        dims = list(f.shape)
        dims[axis] *= 2
        return tuple(dims)

    def body(*refs):
        ins, outs = refs[:n], refs[n:2 * n]
        send_sems, recv_sems = refs[2 * n:]
        x, y, c = _mesh_pos()
        cps = []
        for a in range(n):
            h = ins[a].shape[axes[a]]
            mine = (pl.ds(c * h, h), slice(None)) if axes[a] == 0 else (slice(None), pl.ds(c * h, h))
            cp = _remote(ins[a], outs[a].at[mine], send_sems.at[a], recv_sems.at[a], (x, y, 1 - c))
            cp.start()
            cps.append(cp)
        for cp in cps:
            cp.wait()

    outs = pl.pallas_call(
        body, name="rs_join", in_specs=[ANY] * n, out_specs=[ANY] * n,
        out_shape=[jax.ShapeDtypeStruct(whole(f, ax), f.dtype) for f, ax in zip(fs, axes)],
        scratch_shapes=[pltpu.SemaphoreType.DMA((n,)), pltpu.SemaphoreType.DMA((n,))],
    )(*fs)
    c = lax.axis_index("c")
    return [lax.dynamic_update_slice_in_dim(o, f, c * f.shape[ax], ax) for o, f, ax in zip(outs, fs, axes)]


def _gather_exchange(xs):
    def start(cin, cout, send_sems, recv_sems):
        x, y, c = _mesh_pos()
        me = 2 * x + y
        for a, (src, dst) in enumerate(zip(cin, cout)):
            mine = _half_index(src.shape, c)
            for j, chip in enumerate(_other_chips(x, y)):
                _remote(src.at[mine], dst.at[(me,) + mine], send_sems.at[a, j], recv_sems.at[a, j], (*chip, c)).start()
            _remote(src, dst.at[me], send_sems.at[a, 3], recv_sems.at[a, 3], (x, y, 1 - c)).start()

    def finish(cin, cout, send_sems, recv_sems):
        x, y, c = _mesh_pos()
        for a, dst in enumerate(cout):
            for j, chip in enumerate(_other_chips(x, y)):
                landed = dst.at[(2 * chip[0] + chip[1],) + _half_index(dst.shape[1:], c)]
                _remote(landed, landed, send_sems.at[a, j], recv_sems.at[a, j], (x, y, c)).wait()
            own = dst.at[2 * x + y]
            _remote(own, own, send_sems.at[a, 3], recv_sems.at[a, 3], (x, y, c)).wait()

    return _Exchange(tuple(xs), tuple(jax.ShapeDtypeStruct((N_SHARDS,) + a.shape, a.dtype) for a in xs), start, finish)


def _gather_forward(gs):
    n = len(gs)

    def body(*refs):
        outs = refs[n:2 * n]
        send_sems, recv_sems = refs[2 * n:]
        x, y, c = _mesh_pos()
        chips = _other_chips(x, y)
        cps = []
        for a in range(n):
            for j, chip in enumerate(chips):
                landed = outs[a].at[(2 * chip[0] + chip[1],) + _half_index(outs[a].shape[1:], c)]
                cp = _remote(landed, landed, send_sems.at[a, j], recv_sems.at[a, j], (x, y, 1 - c))
                cp.start()
                cps.append(cp)
        for a in range(n):
            for j, chip in enumerate(chips):
                other = outs[a].at[(2 * chip[0] + chip[1],) + _half_index(outs[a].shape[1:], 1 - c)]
                _remote(other, other, send_sems.at[a, j], recv_sems.at[a, j], (x, y, c)).wait_recv()
        for cp in cps:
            cp.wait_send()

    return list(pl.pallas_call(
        body, name="gather_forward", in_specs=[ANY] * n, out_specs=[ANY] * n,
        out_shape=[jax.ShapeDtypeStruct(g.shape, g.dtype) for g in gs],
        input_output_aliases={i: i for i in range(n)},
        scratch_shapes=[pltpu.SemaphoreType.DMA((n, 3)), pltpu.SemaphoreType.DMA((n, 3))],
    )(*gs))


def _scatter_exchange(ss):
    def start(cin, cout, send_sems, recv_sems):
        x, y, c = _mesh_pos()
        me = 2 * x + y
        for a, (src, dst) in enumerate(zip(cin, cout)):
            for j, chip in enumerate(_other_chips(x, y)):
                _remote(src.at[2 * chip[0] + chip[1]], dst.at[me], send_sems.at[a, j], recv_sems.at[a, j],
                        (*chip, c)).start()

    def finish(cin, cout, send_sems, recv_sems):
        x, y, c = _mesh_pos()
        for a, dst in enumerate(cout):
            for j, chip in enumerate(_other_chips(x, y)):
                slot = dst.at[2 * chip[0] + chip[1]]
                _remote(slot, slot, send_sems.at[a, j], recv_sems.at[a, j], (x, y, c)).wait()

    return _Exchange(tuple(ss), tuple(jax.ShapeDtypeStruct(s.shape, s.dtype) for s in ss), start, finish)


def _add_pairs(gs, lands, name):
    n = len(gs)

    def body(*refs):
        c = lax.axis_index("c")
        for g_ref, l_ref, o_ref in zip(refs[:n], refs[n:2 * n], refs[2 * n:]):
            mine = g_ref[(0,) + _half_index(g_ref.shape[1:], c)]
            o_ref[0] = (mine.astype(F32) + l_ref[0].astype(F32)).astype(BF16)

    whole = [pl.BlockSpec((1,) + g.shape[1:], lambda j: (j, 0, 0)) for g in gs]
    half = [pl.BlockSpec((1,) + l.shape[1:], lambda j: (j, 0, 0)) for l in lands]
    return list(pl.pallas_call(body, name=name, grid=(gs[0].shape[0],), in_specs=whole + half, out_specs=half,
                               out_shape=[jax.ShapeDtypeStruct(l.shape, BF16) for l in lands],
                               compiler_params=_params(("arbitrary",), VMEM_LIMIT))(*gs, *lands))


def _sum_slots(slots, owns, name):
    n = len(slots)

    def body(*refs):
        me = 2 * lax.axis_index("x") + lax.axis_index("y")
        for s_ref, o_ref, out_ref in zip(refs[:n], refs[n:2 * n], refs[2 * n:]):
            acc = jnp.zeros(out_ref.shape, F32)
            for s in range(N_SHARDS):
                acc = acc + jnp.where(me == s, o_ref[s], s_ref[s]).astype(F32)
            out_ref[...] = acc

    def specs(a):
        _, h, c = a.shape
        if h % 32 == 0:
            return (pl.BlockSpec((N_SHARDS, h // 2, c), lambda i: (0, i, 0)), pl.BlockSpec((h // 2, c), lambda i: (i, 0)))
        return (pl.BlockSpec((N_SHARDS, h, c // 2), lambda i: (0, 0, i)), pl.BlockSpec((h, c // 2), lambda i: (0, i)))

    in_specs = [specs(a)[0] for a in slots]
    return list(pl.pallas_call(body, name=name, grid=(2,), in_specs=in_specs + in_specs,
                               out_specs=[specs(a)[1] for a in slots],
                               out_shape=[jax.ShapeDtypeStruct(a.shape[1:], F32) for a in slots],
                               compiler_params=_params(("arbitrary",), VMEM_LIMIT))(*slots, *owns))


def _all_reduce_small(p):
    r = p.shape[0]

    def body(p_ref, o_ref, buf, send_sems, recv_sems):
        x, y, c = _mesh_pos()
        me = 4 * x + 2 * y + c
        buf[me] = p_ref[...]
        cps = []
        k = 0
        for fx in range(2):
            for fy in range(2):
                for fc in range(2):
                    if fx + fy + fc == 0:
                        continue
                    peer = (1 - x if fx else x, 1 - y if fy else y, 1 - c if fc else c)
                    peer_id = 4 * peer[0] + 2 * peer[1] + peer[2]
                    cp = _remote(p_ref, buf.at[me], send_sems.at[k], recv_sems.at[k], peer)
                    cp.start()
                    cps.append((cp, peer_id, k))
                    k += 1
        for cp, peer_id, k in cps:
            _remote(p_ref, buf.at[peer_id], send_sems.at[k], recv_sems.at[k], (x, y, c)).wait_recv()
        for cp, _, _ in cps:
            cp.wait_send()
        acc = buf[0]
        for s in range(1, 8):
            acc = acc + buf[s]
        o_ref[...] = acc

    vm = pl.BlockSpec(memory_space=pltpu.VMEM)
    return pl.pallas_call(
        body, name="all_reduce_small", in_specs=[vm], out_specs=vm,
        out_shape=jax.ShapeDtypeStruct(p.shape, F32),
        scratch_shapes=[pltpu.VMEM((8, r, LANES), F32), pltpu.SemaphoreType.DMA((7,)), pltpu.SemaphoreType.DMA((7,))],
    )(p)


def _adamw(w, g, m, v, name):
    r, c = w.shape
    row_tiles = [d for d in range(8, min(r, 256) + 1, 8) if r % d == 0]
    tr, tc = (max(row_tiles), c) if row_tiles else (r, 256 if c % 256 == 0 else c)
    c1 = 1.0 / (1.0 - ADAM_B1 ** ADAM_STEP)
    c2 = 1.0 / (1.0 - ADAM_B2 ** ADAM_STEP)

    def body(w_ref, g_ref, m_ref, v_ref, d_ref, nm_ref, nv_ref):
        gv = g_ref[...]
        nm = ADAM_B1 * m_ref[...] + (1.0 - ADAM_B1) * gv
        nv = ADAM_B2 * v_ref[...] + (1.0 - ADAM_B2) * (gv * gv)
        d_ref[...] = -ADAM_LR * ((nm * c1) / (jnp.sqrt(nv * c2) + ADAM_EPS) + ADAM_WD * w_ref[...])
        nm_ref[...] = nm
        nv_ref[...] = nv

    blk = pl.BlockSpec((tr, tc), lambda i, j: (i, j))
    shp = jax.ShapeDtypeStruct((r, c), F32)
    return pl.pallas_call(body, name=name, grid=(r // tr, c // tc), in_specs=[blk] * 4, out_specs=[blk] * 3,
                          out_shape=[shp, shp, shp],
                          compiler_params=_params(("arbitrary", "arbitrary"), VMEM_LIMIT))(w, g, m, v)


PACK_UNIT = 8 * LANES


def _pack(arrs):
    parts = []
    for a in arrs:
        f = a.reshape(-1).astype(F32)
        parts.append(jnp.pad(f, (0, (-f.shape[0]) % PACK_UNIT)).reshape(-1, LANES))
    return jnp.concatenate(parts, axis=0)


def _unpack(m, shapes):
    outs, row = [], 0
    for s in shapes:
        n = int(np.prod(s))
        rows = -(-n // PACK_UNIT) * 8
        outs.append(m[row:row + rows].reshape(-1)[:n].reshape(s))
        row += rows
    return outs


WEIGHTS = ["ffn1_norm", "ffn1_w_gate", "ffn1_w_up", "ffn1_w_down", "mix_norm", "w_in", "conv_w", "a_log", "dt_bias",
           "gdn_norm_w", "q_norm_w", "k_norm_w", "rel_bias", "w_out", "ffn2_norm", "ffn2_w_gate", "ffn2_w_up",
           "ffn2_w_down", "final_norm"]
BIG = ["ffn1_w_gate", "ffn1_w_up", "ffn1_w_down", "w_in", "w_out", "ffn2_w_gate", "ffn2_w_up", "ffn2_w_down"]
SMALL = [n for n in WEIGHTS if n not in BIG]
COL_SHARDED = ["ffn1_w_gate", "ffn1_w_up", "w_in", "ffn2_w_gate", "ffn2_w_up"]
N_IN_COLS = 3600
TM = 256
TE = 512
TK = 2048


def kernel(x, ffn1_norm, ffn1_w_gate, ffn1_w_up, ffn1_w_down, mix_norm, w_in, conv_w, a_log, dt_bias, gdn_norm_w, q_norm_w, k_norm_w, rel_bias, w_out, ffn2_norm, ffn2_w_gate, ffn2_w_up, ffn2_w_down, final_norm, loss_target, m_ffn1_norm, m_ffn1_w_gate, m_ffn1_w_up, m_ffn1_w_down, m_mix_norm, m_w_in, m_conv_w, m_a_log, m_dt_bias, m_gdn_norm_w, m_q_norm_w, m_k_norm_w, m_rel_bias, m_w_out, m_ffn2_norm, m_ffn2_w_gate, m_ffn2_w_up, m_ffn2_w_down, m_final_norm, v_ffn1_norm, v_ffn1_w_gate, v_ffn1_w_up, v_ffn1_w_down, v_mix_norm, v_w_in, v_conv_w, v_a_log, v_dt_bias, v_gdn_norm_w, v_q_norm_w, v_k_norm_w, v_rel_bias, v_w_out, v_ffn2_norm, v_ffn2_w_gate, v_ffn2_w_up, v_ffn2_w_down, v_final_norm):
    p = dict(locals())
    xs, target = x[0], loss_target[0]
    t, d = xs.shape
    nc = t // CHUNK
    tk = min(TK, t)
    me = 2 * lax.axis_index("x") + lax.axis_index("y")

    first = ["ffn1_w_gate", "ffn1_w_up", "ffn1_w_down"]
    later = [n for n in BIG if n not in first] + ["conv_w"]
    local = lambda n, a: a[0].T if n in COL_SHARDED else a[0]
    shards = {n: local(n, p[n]).astype(BF16) for n in BIG}
    shards["conv_w"] = conv_w[0]
    gw = dict(zip(first, _all_gather([shards[n] for n in first])))
    f1 = (gw["ffn1_w_gate"], gw["ffn1_w_up"], gw["ffn1_w_down"])
    (x1, xn1, g1, u1), landed = _ffn_fwd(xs, ffn1_norm, *f1, TM, "ffn1_fwd",
                                         exchange=_gather_exchange([shards[n] for n in later]))
    gw.update(zip(later, _gather_forward(landed)))
    wp = gw["w_in"].reshape(N_IN_COLS, d)
    w_out_full = gw["w_out"].reshape(d, d)
    conv_rows = conv_w.shape[1]
    cw = jnp.pad(gw["conv_w"].reshape(N_SHARDS * conv_rows, CONV_TAPS).T, ((0, 8 - CONV_TAPS), (0, 0)))
    gp = jnp.pad(jnp.stack([a_log.reshape(8), dt_bias.reshape(8)]), ((0, 6), (0, LANES - 8)))
    gdn_w = gdn_norm_w.reshape(1, GDN_DIM)
    qw_t = jnp.tile(q_norm_w.reshape(1, SWA_DIM), (1, SWA_HEADS))
    kw_t = jnp.tile(k_norm_w.reshape(1, SWA_DIM), (1, SWA_HEADS))
    bd = jnp.asarray(np.kron(np.eye(2), np.full((SWA_DIM, SWA_DIM), 1.0 / SWA_DIM)), BF16)
    f2 = (gw["ffn2_w_gate"], gw["ffn2_w_up"], gw["ffn2_w_down"])

    hn, qkva, z, ab, qkvb = _mix_in_fwd(x1, mix_norm, wp, TE)
    qkvc, gb = _gdn_prep_fwd(qkva, cw, ab, gp, TM)
    gbt = jnp.transpose(gb[:, :16].reshape(nc, CHUNK, 16), (0, 2, 1))
    o_f, o_b, gdn_saved = _gdn_fwd(qkvc, gb, gbt)
    oa = _gdn_post_fwd(o_f, o_b, z, gdn_w, TE)
    o_swa, o_swa16, swa_saved = _swa_branch_fwd(qkvb, qw_t, kw_t, rel_bias, bd, TE)
    x2 = _mix_out_fwd(x1, oa, o_swa, w_out_full, TE)
    (dx3, xn2, g2, u2, loss_part, d_final), _ = _ffn_fwd(x2, ffn2_norm, *f2, TM, "ffn2_fwd", head=(final_norm, target))

    def pair_sums(partials, tag):
        return _add_pairs(partials, _rs_pair(partials), f"rs_add_{tag}")

    (dx2, dyh2, dg2, du2, h2, d_nw2), _ = _ffn_bwd_dx(dx3, x2, ffn2_norm, g2, u2, *f2, TM, "ffn2_bwd_dx")
    dwg2 = _matmul_tn(dg2, xn2, tk, "ffn2_dwg")
    dwu2 = _matmul_tn(du2, xn2, tk, "ffn2_dwu")
    dwd2 = _matmul_tn(h2, dyh2, tk, "ffn2_dwd")
    (doa, dob, dx2b), lands_f2 = _mix_out_bwd(dx2, w_out_full, TE, exchange=_pair_exchange([dwg2, dwu2, dwd2]))
    sums_f2 = _add_pairs([dwg2, dwu2, dwd2], lands_f2, "rs_add_a")
    dwo = jnp.concatenate([_matmul_tn(oa, dx2b, tk, "w_out_dw_a")[0], _matmul_tn(o_swa16, dx2b, tk, "w_out_dw_b")[0]],
                          axis=0).reshape(N_SHARDS, d // N_SHARDS, d)
    do_g, dz, d_gdnw = _gdn_post_bwd(doa, o_f, o_b, z, gdn_w, TE)
    (dqkvc, dgates), slots_f2 = _gdn_bwd(qkvc, gb, gbt, do_g, gdn_saved, exchange=_scatter_exchange(sums_f2))
    dqkva, dab, dcw, dgp = _gdn_prep_bwd(qkva, cw, ab, gp, dqkvc, dgates, TM)
    dqkvb, d_qw, d_kw, d_rel = _swa_branch_bwd(dob, o_swa, swa_saved, qkvb, qw_t, kw_t, bd, TE)
    dpieces = (dqkva, dz, dab, dqkvb)
    dwp = [_matmul_tn(dp, hn, tk, f"w_in_dw_{i}")[0] for i, dp in enumerate(dpieces)]
    dw_in = jnp.concatenate([dwp[0], dwp[1], dwp[2][:N_GATE_COLS], dwp[3]], axis=0)
    dw_in = dw_in.reshape(N_SHARDS, N_IN_COLS // N_SHARDS, d)
    sums_mix = pair_sums([dw_in, dwo], "b")
    (dx1, d_mixnw), slots_mix = _mix_in_bwd_dx(dx2, x1, mix_norm, dpieces, wp, TE, exchange=_scatter_exchange(sums_mix))
    (gx, dyh1, dg1, du1, h1, d_nw1), _ = _ffn_bwd_dx(dx1, xs, ffn1_norm, g1, u1, *f1, TM, "ffn1_bwd_dx")
    dwg1 = _matmul_tn(dg1, xn1, tk, "ffn1_dwg")
    dwu1 = _matmul_tn(du1, xn1, tk, "ffn1_dwu")
    sums_gu = pair_sums([dwg1, dwu1], "c")
    dwd1, slots_gu = _matmul_tn(h1, dyh1, tk, "ffn1_dwd", exchange=_scatter_exchange(sums_gu))
    sums_d = pair_sums([dwd1], "d")
    slots = slots_gu + _rs_chips(sums_d) + slots_mix + slots_f2
    sums = sums_gu + sums_d + sums_mix + sums_f2
    halves = _sum_slots(slots[:4], sums[:4], "rs_sum_a") + _sum_slots(slots[4:], sums[4:], "rs_sum_b")
    g_big = dict(zip(BIG, _rs_join(halves, [_split_axis(shards[n].shape) for n in BIG])))

    small_partial = {"ffn1_norm": d_nw1, "mix_norm": d_mixnw, "a_log": dgp[0, 0:8], "dt_bias": dgp[1, 0:8],
                     "gdn_norm_w": d_gdnw, "q_norm_w": d_qw, "k_norm_w": d_kw, "rel_bias": d_rel,
                     "ffn2_norm": d_nw2, "final_norm": d_final, "conv_w": dcw[0:CONV_TAPS].T}
    red = _all_reduce_small(_pack([small_partial[n] for n in SMALL] + [loss_part[0, 0:1]]))
    full_shapes = [p[n].shape if n != "conv_w" else (N_SHARDS * conv_rows, CONV_TAPS) for n in SMALL]
    red_parts = _unpack(red, full_shapes + [(1,)])
    loss = red_parts[-1].reshape(())
    g_small = dict(zip(SMALL, red_parts[:-1]))
    g_small["conv_w"] = lax.dynamic_slice_in_dim(g_small["conv_w"], me * conv_rows, conv_rows, 0).reshape(conv_w.shape)

    grads, deltas, new_m, new_v = {}, {}, {}, {}
    for n in BIG:
        back = (lambda a: a.T[None]) if n in COL_SHARDED else (lambda a: a[None])
        grads[n] = back(g_big[n])
        dl, nm, nv = _adamw(local(n, p[n]), g_big[n], local(n, p["m_" + n]), local(n, p["v_" + n]), "adamw_" + n)
        deltas[n], new_m[n], new_v[n] = back(dl), back(nm), back(nv)
    packed = [_pack([src[n] for n in SMALL]) for src in
              ({n: p[n] for n in SMALL}, g_small, {n: p["m_" + n] for n in SMALL}, {n: p["v_" + n] for n in SMALL})]
    small_shapes = [p[n].shape for n in SMALL]
    for dst, arr in zip((deltas, new_m, new_v), _adamw(*packed, "adamw_small")):
        dst.update(zip(SMALL, _unpack(arr, small_shapes)))
    grads.update(g_small)

    return (loss, gx[None], *[grads[n] for n in WEIGHTS], *[deltas[n] for n in WEIGHTS],
            *[new_m[n] for n in WEIGHTS], *[new_v[n] for n in WEIGHTS])
```

```python
import math
from typing import Callable, NamedTuple

import numpy as np
import jax
import jax.numpy as jnp
from jax import lax
from jax.experimental import pallas as pl
from jax.experimental.pallas import tpu as pltpu

F32 = jnp.float32
BF16 = jnp.bfloat16
MESH = pl.DeviceIdType.MESH

EPS = 1e-6
NEG_BIG = -1e30
GDN_HEADS = 4
GDN_DIM = 128
CHUNK = 64
SWA_HEADS = 8
SWA_DIM = 64
PATTERNS = ((128, 1), (512, 4), (2048, 16))
RADIUS = 64
REL_BUCKETS = 32
REL_MAX_DISTANCE = 1024
CONV_TAPS = 5
N_SHARDS = 4
LANES = 128
VMEM_LIMIT = 56 * 1024 * 1024

ADAM_LR, ADAM_B1, ADAM_B2, ADAM_EPS, ADAM_WD, ADAM_STEP = 0.001, 0.9, 0.999, 1e-08, 0.01, 10


def _params(sem=None, vmem=None):
    return pltpu.CompilerParams(dimension_semantics=sem, vmem_limit_bytes=vmem)


def _resident(shape):
    nd = len(shape)
    return pl.BlockSpec(shape, lambda *_: (0,) * nd, pipeline_mode=pl.Buffered(1))


ANY = pl.BlockSpec(memory_space=pl.ANY)


class _Exchange(NamedTuple):
    arrays: tuple
    out_shape: tuple
    start: Callable
    finish: Callable


def _grid_call(body, name, nsteps, in_specs, out_specs, out_shape, operands, scratch=(), exchange=None):
    params = _params(("arbitrary",), VMEM_LIMIT)
    if exchange is None:
        res = pl.pallas_call(body, name=name, grid=(nsteps,), in_specs=list(in_specs), out_specs=list(out_specs),
                             out_shape=list(out_shape), scratch_shapes=list(scratch), compiler_params=params)(*operands)
        return list(res), []
    n_in, n_out, k, n_scr = len(in_specs), len(out_specs), len(exchange.arrays), len(scratch)

    def wrapped(*refs):
        ins, cin = refs[:n_in], refs[n_in:n_in + k]
        outs, cout = refs[n_in + k:n_in + k + n_out], refs[n_in + k + n_out:n_in + 2 * k + n_out]
        rest = refs[n_in + 2 * k + n_out:]
        scr, (send_sems, recv_sems) = rest[:n_scr], rest[n_scr:]

        @pl.when(pl.program_id(0) == 0)
        def _():
            exchange.start(cin, cout, send_sems, recv_sems)

        body(*ins, *outs, *scr)

        @pl.when(pl.program_id(0) == nsteps - 1)
        def _():
            exchange.finish(cin, cout, send_sems, recv_sems)

    res = pl.pallas_call(
        wrapped, name=name, grid=(nsteps,), in_specs=list(in_specs) + [ANY] * k, out_specs=list(out_specs) + [ANY] * k,
        out_shape=list(out_shape) + list(exchange.out_shape),
        scratch_shapes=list(scratch) + [pltpu.SemaphoreType.DMA((k, 4)), pltpu.SemaphoreType.DMA((k, 4))],
        compiler_params=params)(*operands, *exchange.arrays)
    return list(res[:n_out]), list(res[n_out:])


def _dot(a, b):
    return jnp.dot(a.astype(BF16), b.astype(BF16), preferred_element_type=F32)


def _dot_nt(a, b):
    return lax.dot_general(a.astype(BF16), b.astype(BF16), (((1,), (1,)), ((), ())), preferred_element_type=F32)


def _dot_tn(a, b):
    return lax.dot_general(a.astype(BF16), b.astype(BF16), (((0,), (0,)), ((), ())), preferred_element_type=F32)


def _sigmoid(x):
    return 1.0 / (1.0 + jnp.exp(-x))


def _rstd(xf):
    return lax.rsqrt(jnp.mean(xf * xf, axis=-1, keepdims=True) + EPS)


def _rms_bwd(xf, r, nw, dxn):
    xhat = xf * r
    dxh = dxn * nw
    dx = r * (dxh - xhat * jnp.mean(dxh * xhat, axis=-1, keepdims=True))
    return dx, jnp.sum(dxn * xhat, axis=0, keepdims=True)


def _ffn_fwd(x, nw, wg, wu, wd, tm, name, exchange=None, head=None):
    t, d = x.shape
    nj, fs, _ = wg.shape

    def body(x_ref, nw_ref, wg_ref, wu_ref, wd_ref, *rest):
        if head is None:
            y_ref, xn_ref, g_ref, u_ref = rest
        else:
            fw_ref, t_ref, y_ref, xn_ref, g_ref, u_ref, loss_ref, dfw_ref = rest

            @pl.when(pl.program_id(0) == 0)
            def _():
                loss_ref[...] = jnp.zeros_like(loss_ref)
                dfw_ref[...] = jnp.zeros_like(dfw_ref)

        xf = x_ref[...]
        xn = (xf * _rstd(xf) * nw_ref[...]).astype(BF16)
        xn_ref[...] = xn
        acc = jnp.zeros((tm, d), F32)
        for j in range(nj):
            g = _dot_nt(xn, wg_ref[j])
            u = _dot_nt(xn, wu_ref[j])
            h = (g * _sigmoid(g) * u).astype(BF16)
            acc = acc + jnp.dot(h, wd_ref[j], preferred_element_type=F32)
            g_ref[j] = g.astype(BF16)
            u_ref[j] = u.astype(BF16)
        y = xf + 0.5 * acc
        if head is None:
            y_ref[...] = y
        else:
            r = _rstd(y)
            err = y * r * fw_ref[...] - t_ref[...]
            loss_ref[...] += 0.5 * jnp.sum(jnp.mean(err * err, axis=-1, keepdims=True), axis=0, keepdims=True)
            dy, dfw = _rms_bwd(y, r, fw_ref[...], err * (1.0 / d))
            y_ref[...] = dy
            dfw_ref[...] += dfw

    row = pl.BlockSpec((tm, d), lambda i: (i, 0))
    act = pl.BlockSpec((nj, tm, fs), lambda i: (0, i, 0))
    in_specs = [row, _resident((1, d)), _resident(wg.shape), _resident(wu.shape), _resident(wd.shape)]
    out_specs = [row, row, act, act]
    out_shape = [jax.ShapeDtypeStruct((t, d), F32), jax.ShapeDtypeStruct((t, d), BF16),
                 jax.ShapeDtypeStruct((nj, t, fs), BF16), jax.ShapeDtypeStruct((nj, t, fs), BF16)]
    operands = (x, nw, wg, wu, wd)
    if head is not None:
        in_specs += [_resident((1, d)), row]
        out_specs += [pl.BlockSpec((1, LANES), lambda i: (0, 0)), pl.BlockSpec((1, d), lambda i: (0, 0))]
        out_shape += [jax.ShapeDtypeStruct((1, LANES), F32), jax.ShapeDtypeStruct((1, d), F32)]
        operands += tuple(head)
    return _grid_call(body, name, t // tm, in_specs, out_specs, out_shape, operands, exchange=exchange)


def _ffn_bwd_dx(dy, x, nw, g, u, wg, wu, wd, tm, name, exchange=None):
    t, d = x.shape
    nj, fs, _ = wg.shape

    def body(dy_ref, x_ref, nw_ref, g_ref, u_ref, wg_ref, wu_ref, wd_ref,
             dx_ref, dyh_ref, dg_ref, du_ref, h_ref, dnw_ref):
        @pl.when(pl.program_id(0) == 0)
        def _():
            dnw_ref[...] = jnp.zeros_like(dnw_ref)

        dyv = dy_ref[...]
        dyh = (0.5 * dyv).astype(BF16)
        dyh_ref[...] = dyh
        dxn = jnp.zeros((tm, d), F32)
        dh_next = _dot_nt(dyh, wd_ref[0])
        for j in range(nj):
            dh = dh_next
            gv = g_ref[j].astype(F32)
            uv = u_ref[j].astype(F32)
            sg = _sigmoid(gv)
            si = gv * sg
            dg = (dh * uv * (sg * (1.0 + gv * (1.0 - sg)))).astype(BF16)
            du = (dh * si).astype(BF16)
            if j + 1 < nj:
                dh_next = _dot_nt(dyh, wd_ref[j + 1])
            h_ref[j] = (si * uv).astype(BF16)
            dg_ref[j] = dg
            du_ref[j] = du
            dxn = dxn + _dot(dg, wg_ref[j]) + _dot(du, wu_ref[j])
        xf = x_ref[...]
        dxr, dnw = _rms_bwd(xf, _rstd(xf), nw_ref[...], dxn)
        dx_ref[...] = dyv + dxr
        dnw_ref[...] += dnw

    row = pl.BlockSpec((tm, d), lambda i: (i, 0))
    act = pl.BlockSpec((nj, tm, fs), lambda i: (0, i, 0))
    act_shape = jax.ShapeDtypeStruct((nj, t, fs), BF16)
    return _grid_call(
        body, name, t // tm,
        [row, row, _resident((1, d)), act, act, _resident(wg.shape), _resident(wu.shape), _resident(wd.shape)],
        [row, row, act, act, act, pl.BlockSpec((1, d), lambda i: (0, 0))],
        [jax.ShapeDtypeStruct((t, d), F32), jax.ShapeDtypeStruct((t, d), BF16),
         act_shape, act_shape, act_shape, jax.ShapeDtypeStruct((1, d), F32)],
        (dy, x, nw, g, u, wg, wu, wd), exchange=exchange)


def _matmul_tn(a, b, tk, name, exchange=None):
    a3, b3 = a.ndim == 3, b.ndim == 3
    nj = a.shape[0] if a3 else (b.shape[0] if b3 else 1)
    t, m = a.shape[-2:]
    n = b.shape[-1]
    nt = t // tk

    def body(a_ref, b_ref, o_ref, acc_ref):
        k = pl.program_id(0) % nt

        @pl.when(k == 0)
        def _():
            acc_ref[...] = jnp.zeros_like(acc_ref)

        acc_ref[...] += lax.dot_general(a_ref[...], b_ref[...], (((0,), (0,)), ((), ())),
                                        preferred_element_type=F32)

        @pl.when(k == nt - 1)
        def _():
            o_ref[...] = acc_ref[...].astype(o_ref.dtype)

    a_spec = (pl.BlockSpec((None, tk, m), lambda i: (i // nt, i % nt, 0)) if a3
              else pl.BlockSpec((tk, m), lambda i: (i % nt, 0)))
    b_spec = (pl.BlockSpec((None, tk, n), lambda i: (i // nt, i % nt, 0)) if b3
              else pl.BlockSpec((tk, n), lambda i: (i % nt, 0)))
    (out,), landed = _grid_call(
        body, name, nj * nt, [a_spec, b_spec], [pl.BlockSpec((None, m, n), lambda i: (i // nt, 0, 0))],
        [jax.ShapeDtypeStruct((nj, m, n), BF16)], (a, b), scratch=[pltpu.VMEM((m, n), F32)], exchange=exchange)
    return out if exchange is None else (out, landed)


N_GATE_COLS = 4 * GDN_HEADS
P_QKVA, P_Z, P_AB, P_QKVB = (0, 1536), (1536, 2048), (2048, 2048 + LANES), (2048 + N_GATE_COLS, 3600)
P_PIECES = (P_QKVA, P_Z, P_AB, P_QKVB)


def _mix_in_fwd(x1, nw, wp, tm):
    t, d = x1.shape

    def body(x_ref, nw_ref, w_ref, hn_ref, *outs):
        xf = x_ref[...]
        xn = (xf * _rstd(xf) * nw_ref[...]).astype(BF16)
        hn_ref[...] = xn
        for (a, b), o_ref in zip(P_PIECES, outs):
            o_ref[...] = _dot_nt(xn, w_ref[a:b, :])

    row = pl.BlockSpec((tm, d), lambda i: (i, 0))
    return pl.pallas_call(
        body, name="mix_in_fwd", grid=(t // tm,),
        in_specs=[row, _resident((1, d)), _resident(wp.shape)],
        out_specs=[row] + [pl.BlockSpec((tm, b - a), lambda i: (i, 0)) for a, b in P_PIECES],
        out_shape=[jax.ShapeDtypeStruct((t, d), BF16)]
                  + [jax.ShapeDtypeStruct((t, b - a), F32) for a, b in P_PIECES],
        compiler_params=_params(("arbitrary",), VMEM_LIMIT),
    )(x1, nw, wp)


def _mix_in_bwd_dx(dx, x1, nw, dpieces, wp, tm, exchange=None):
    t, d = x1.shape

    def body(dx_ref, x_ref, nw_ref, p0, p1, p2, p3, w_ref, o_ref, dnw_ref):
        @pl.when(pl.program_id(0) == 0)
        def _():
            dnw_ref[...] = jnp.zeros_like(dnw_ref)

        dh = jnp.zeros((tm, d), F32)
        for (a, b), p_ref in zip(P_PIECES, (p0, p1, p2, p3)):
            dh = dh + _dot(p_ref[...], w_ref[a:b, :])
        xf = x_ref[...]
        dxr, dnw = _rms_bwd(xf, _rstd(xf), nw_ref[...], dh)
        o_ref[...] = dx_ref[...] + dxr
        dnw_ref[...] += dnw

    row = pl.BlockSpec((tm, d), lambda i: (i, 0))
    return _grid_call(
        body, "mix_in_bwd_dx", t // tm,
        [row, row, _resident((1, d))]
        + [pl.BlockSpec((tm, b - a), lambda i: (i, 0)) for a, b in P_PIECES] + [_resident(wp.shape)],
        [row, pl.BlockSpec((1, d), lambda i: (0, 0))],
        [jax.ShapeDtypeStruct((t, d), F32), jax.ShapeDtypeStruct((1, d), F32)],
        (dx, x1, nw, *dpieces, wp), exchange=exchange)


def _mix_out_fwd(x1, oa, ob, w, tm):
    t, d = x1.shape
    half = oa.shape[1]

    def body(x_ref, oa_ref, ob_ref, w_ref, o_ref):
        o_ref[...] = (x_ref[...] + _dot(oa_ref[...], w_ref[0:half, :]) + _dot(ob_ref[...], w_ref[half:2 * half, :]))

    row = pl.BlockSpec((tm, d), lambda i: (i, 0))
    hrow = pl.BlockSpec((tm, half), lambda i: (i, 0))
    return pl.pallas_call(
        body, name="mix_out_fwd", grid=(t // tm,),
        in_specs=[row, hrow, hrow, _resident(w.shape)],
        out_specs=row, out_shape=jax.ShapeDtypeStruct((t, d), F32),
        compiler_params=_params(("arbitrary",), VMEM_LIMIT),
    )(x1, oa, ob, w)


def _mix_out_bwd(dx2, w, tm, exchange=None):
    t, d = dx2.shape
    half = w.shape[0] // 2

    def body(dx_ref, w_ref, doa_ref, dob_ref, dxb_ref):
        dxb = dx_ref[...].astype(BF16)
        dxb_ref[...] = dxb
        doa_ref[...] = _dot_nt(dxb, w_ref[0:half, :])
        dob_ref[...] = _dot_nt(dxb, w_ref[half:2 * half, :])

    row = pl.BlockSpec((tm, d), lambda i: (i, 0))
    hrow = pl.BlockSpec((tm, half), lambda i: (i, 0))
    return _grid_call(
        body, "mix_out_bwd", t // tm, [row, _resident(w.shape)], [hrow, hrow, row],
        [jax.ShapeDtypeStruct((t, half), F32), jax.ShapeDtypeStruct((t, half), F32), jax.ShapeDtypeStruct((t, d), BF16)],
        (dx2, w), exchange=exchange)


HALO = 8


def _halo_row_specs(tr, cols, nrow8):
    per = tr // HALO
    return [pl.BlockSpec((tr, cols), lambda i: (i, 0)),
            pl.BlockSpec((HALO, cols), lambda i: (jnp.maximum(i * per - 1, 0), 0)),
            pl.BlockSpec((HALO, cols), lambda i: (jnp.minimum((i + 1) * per, nrow8 - 1), 0))]


def _fill_window(win_ref, cb, xm, xp, xn, first, last):
    tr = xm.shape[0]
    cols = slice(cb * LANES, (cb + 1) * LANES)
    win_ref[cb, 0:HALO, :] = jnp.where(first, 0.0, xp[:, cols])
    win_ref[cb, HALO:HALO + tr, :] = xm[:, cols]
    win_ref[cb, HALO + tr:HALO + tr + HALO, :] = jnp.where(last, 0.0, xn[:, cols])


def _conv_taps(win_ref, cb, cw_ref, start, rows):
    cols = slice(cb * LANES, (cb + 1) * LANES)
    acc = None
    for j in range(CONV_TAPS):
        term = win_ref[cb, pl.ds(start + j - CONV_TAPS // 2, rows), :] * cw_ref[j:j + 1, cols]
        acc = term if acc is None else acc + term
    return acc


def _softplus(x):
    u = jnp.exp(-jnp.abs(x))
    w = 1.0 + u
    log1p = jnp.where(w == 1.0, u, jnp.log(w) * u / jnp.where(w == 1.0, 1.0, w - 1.0))
    return jnp.maximum(x, 0.0) + log1p


def _gdn_prep_fwd(qkva, cw, ab, gp, tr):
    t, c = qkva.shape
    nt = t // tr
    ncb = c // LANES

    def body(xm, xp, xn, cw_ref, ab_ref, gp_ref, o_ref, gb_ref, xw_ref):
        i = pl.program_id(0)
        first, last = i == 0, i == nt - 1
        for cb in range(ncb):
            cols = slice(cb * LANES, (cb + 1) * LANES)
            _fill_window(xw_ref, cb, xm, xp, xn, first, last)
            pre = _conv_taps(xw_ref, cb, cw_ref, HALO, tr)
            y = pre * _sigmoid(pre)
            if cb < 2 * GDN_HEADS:
                y = y * lax.rsqrt(jnp.sum(y * y, axis=-1, keepdims=True) + EPS)
            if cb < GDN_HEADS:
                y = y * (GDN_DIM ** -0.5)
            o_ref[:, cols] = y
        abv = ab_ref[...]
        lane = lax.broadcasted_iota(jnp.int32, abv.shape, 1)
        g = -jnp.exp(gp_ref[0:1, :]) * _softplus(abv + gp_ref[1:2, :])
        gb_ref[...] = jnp.where(lane < 8, g, jnp.where(lane < 16, _sigmoid(abv), 0.0))

    return pl.pallas_call(
        body, name="gdn_prep_fwd", grid=(nt,),
        in_specs=_halo_row_specs(tr, c, t // HALO)
                 + [_resident(cw.shape), pl.BlockSpec((tr, LANES), lambda i: (i, 0)), _resident(gp.shape)],
        out_specs=[pl.BlockSpec((tr, c), lambda i: (i, 0)), pl.BlockSpec((tr, LANES), lambda i: (i, 0))],
        out_shape=[jax.ShapeDtypeStruct((t, c), F32), jax.ShapeDtypeStruct((t, LANES), F32)],
        scratch_shapes=[pltpu.VMEM((ncb, tr + 2 * HALO, LANES), F32)],
        compiler_params=_params(("arbitrary",), VMEM_LIMIT),
    )(qkva, qkva, qkva, cw, ab, gp)


def _gdn_prep_bwd(qkva, cw, ab, gp, dy, dgates, tr):
    t, c = qkva.shape
    nt = t // tr
    ncb = c // LANES

    ext = HALO // 2
    rows_ext = tr + 2 * ext

    def body(xm, xp, xn, fm, fp, fn, cw_ref, ab_ref, gp_ref, gf_ref, dx_ref, dab_ref, dcw_ref, dgp_ref,
             xw_ref, dyw_ref, dp_ref):
        i = pl.program_id(0)
        first, last = i == 0, i == nt - 1

        @pl.when(first)
        def _():
            dcw_ref[...] = jnp.zeros_like(dcw_ref)
            dgp_ref[...] = jnp.zeros_like(dgp_ref)

        sub8 = lax.broadcasted_iota(jnp.int32, (8, LANES), 0)
        for cb in range(ncb):
            cols = slice(cb * LANES, (cb + 1) * LANES)
            _fill_window(xw_ref, cb, xm, xp, xn, first, last)
            _fill_window(dyw_ref, cb, fm, fp, fn, first, last)
            pre = _conv_taps(xw_ref, cb, cw_ref, HALO - ext, rows_ext)
            dyw = dyw_ref[cb, pl.ds(HALO - ext, rows_ext), :]
            sg = _sigmoid(pre)
            s = pre * sg
            if cb < 2 * GDN_HEADS:
                scale = (GDN_DIM ** -0.5) if cb < GDN_HEADS else 1.0
                r = lax.rsqrt(jnp.sum(s * s, axis=-1, keepdims=True) + EPS)
                dn = dyw * scale
                ds = r * dn - s * (r * r * r) * jnp.sum(dn * s, axis=-1, keepdims=True)
            else:
                ds = dyw
            dp_ref[cb] = ds * (sg * (1.0 + pre * (1.0 - sg)))
            dpre = dp_ref[cb, pl.ds(ext, tr), :]
            dx = None
            dcw = jnp.zeros((8, LANES), F32)
            for j in range(CONV_TAPS):
                off = j - CONV_TAPS // 2
                term = dp_ref[cb, pl.ds(ext - off, tr), :] * cw_ref[j:j + 1, cols]
                dx = term if dx is None else dx + term
                tap = jnp.sum(dpre * xw_ref[cb, pl.ds(HALO + off, tr), :], axis=0, keepdims=True)
                dcw = dcw + jnp.where(sub8 == j, tap, 0.0)
            dx_ref[:, cols] = dx.astype(BF16)
            dcw_ref[:, cols] += dcw

        abv = ab_ref[...]
        dgb = gf_ref[...]
        lane = lax.broadcasted_iota(jnp.int32, abv.shape, 1)
        nea = -jnp.exp(gp_ref[0:1, :])
        xs = abv + gp_ref[1:2, :]
        g = nea * _softplus(xs)
        beta = _sigmoid(abv)
        da = dgb * nea * _sigmoid(xs)
        dab = jnp.where(lane < 8, da, jnp.where(lane < 16, dgb * beta * (1.0 - beta), 0.0))
        dab_ref[...] = dab.astype(BF16)
        keep = lane[0:1, :] < 8
        dalog = jnp.where(keep, jnp.sum(dgb * g, axis=0, keepdims=True), 0.0)
        ddtb = jnp.where(keep, jnp.sum(da, axis=0, keepdims=True), 0.0)
        dgp_ref[...] += jnp.where(sub8 == 0, dalog, 0.0) + jnp.where(sub8 == 1, ddtb, 0.0)

    lrow = pl.BlockSpec((tr, LANES), lambda i: (i, 0))
    halo = _halo_row_specs(tr, c, t // HALO)
    return pl.pallas_call(
        body, name="gdn_prep_bwd", grid=(nt,),
        in_specs=halo + halo + [_resident(cw.shape), lrow, _resident(gp.shape), lrow],
        out_specs=[pl.BlockSpec((tr, c), lambda i: (i, 0)), lrow,
                   pl.BlockSpec(cw.shape, lambda i: (0, 0)), pl.BlockSpec(gp.shape, lambda i: (0, 0))],
        out_shape=[jax.ShapeDtypeStruct((t, c), BF16), jax.ShapeDtypeStruct((t, LANES), BF16),
                   jax.ShapeDtypeStruct(cw.shape, F32), jax.ShapeDtypeStruct(gp.shape, F32)],
        scratch_shapes=[pltpu.VMEM((ncb, tr + 2 * HALO, LANES), F32), pltpu.VMEM((ncb, tr + 2 * HALO, LANES), F32),
                        pltpu.VMEM((ncb, rows_ext, LANES), F32)],
        compiler_params=_params(("arbitrary",), VMEM_LIMIT),
    )(qkva, qkva, qkva, dy, dy, dy, cw, ab, gp, dgates)


def _chunk_masks(lower):
    ii = lax.broadcasted_iota(jnp.int32, (CHUNK, CHUNK), 0)
    jj = lax.broadcasted_iota(jnp.int32, (CHUNK, CHUNK), 1)
    incl = (ii >= jj) if lower else (ii <= jj)
    strict = (ii > jj) if lower else (ii < jj)
    return ii, jj, incl, strict


def _dot3(a, b):
    ah = a.astype(BF16)
    al = (a - ah.astype(F32)).astype(BF16)
    bh = b.astype(BF16)
    bl = (b - bh.astype(F32)).astype(BF16)
    d = lambda u, v: jnp.dot(u, v, preferred_element_type=F32)
    return d(ah, bh) + (d(ah, bl) + d(al, bh))


def _tri_inv_many(lmats, ii, jj):
    m16 = (ii // 16) == (jj // 16)
    m32 = (ii // 32) == (jj // 32)
    eye = jnp.where(ii == jj, 1.0, 0.0)
    l16 = [jnp.where(m16, l, 0.0) for l in lmats]
    p2 = [_dot3(a, a) for a in l16]
    p4 = [_dot3(a, a) for a in p2]
    p8 = [_dot3(a, a) for a in p4]
    xs = [eye - a for a in l16]
    for ps in (p2, p4, p8):
        xs = [x + _dot3(x, p) for x, p in zip(xs, ps)]
    for off in ([jnp.where(m32 & jnp.logical_not(m16), l, 0.0) for l in lmats],
                [jnp.where(m32, 0.0, l) for l in lmats]):
        ys = [_dot3(x, c) for x, c in zip(xs, off)]
        xs = [x - _dot3(y, x) for x, y in zip(xs, ys)]
    return xs


def _col_to_row(col, ii, jj):
    return jnp.sum(jnp.where(ii == jj, col, 0.0), axis=0, keepdims=True)


def _row_to_col(row, ii, jj):
    return jnp.sum(jnp.where(ii == jj, row, 0.0), axis=1, keepdims=True)


def _chain_common(q, k, v, graw_col, graw_row, bcol, masks):
    ii, jj, incl, strict = masks
    inclt = jnp.logical_not(strict)
    gcol = jnp.sum(jnp.where(incl, graw_row, 0.0), axis=1, keepdims=True)
    grow = jnp.sum(jnp.where(inclt, graw_col, 0.0), axis=0, keepdims=True)
    glast = jnp.sum(graw_row, axis=1, keepdims=True)
    decay = jnp.where(incl, jnp.exp(jnp.where(incl, gcol - grow, 0.0)), 0.0)
    kb = k * bcol
    vb = v * bcol
    eg = jnp.exp(gcol)
    ek = jnp.exp(glast - gcol)
    kbg = kb * eg
    amat = _dot_nt(kb, k)
    qk = _dot_nt(q, k)
    return dict(gcol=gcol, glast=glast, decay=decay, kb=kb, vb=vb, eg=eg, ek=ek, kbg=kbg, amat=amat, qk=qk,
                intra=qk * decay, qg=q * eg, kdec=k * ek)


def _gdn_fwd(qkvc, gb, gbt):
    tm, u, w, qg, kd, intra, egl = _gdn_local_fwd(qkvc, gb, gbt)
    o_f, o_b, s_f, s_b, vn_f, vn_b = _gdn_scan_fwd(u, w, qg, kd, intra, egl, qkvc.shape[0])
    return o_f, o_b, dict(tm=tm, w=w, qg=qg, kd=kd, intra=intra, egl=egl, s=(s_f, s_b), vn=(vn_f, vn_b))


N_CHAINS = 2 * GDN_HEADS


LOCAL_CHUNKS = 4


def _load_chains(x_ref, g_ref, gt_ref, cc=0):
    hd = GDN_HEADS * GDN_DIM
    rows = slice(cc * CHUNK, (cc + 1) * CHUNK)
    chains = []
    for d in range(2):
        masks = _chunk_masks(d == 0)
        for h in range(GDN_HEADS):
            ch = d * GDN_HEADS + h
            q = x_ref[rows, h * GDN_DIM:(h + 1) * GDN_DIM]
            k = x_ref[rows, hd + h * GDN_DIM:hd + (h + 1) * GDN_DIM]
            v = x_ref[rows, 2 * hd + h * GDN_DIM:2 * hd + (h + 1) * GDN_DIM]
            bcol = g_ref[rows, 8 + ch:9 + ch]
            cm = _chain_common(q, k, v, g_ref[rows, ch:ch + 1], gt_ref[cc, ch:ch + 1, :], bcol, masks)
            chains.append(dict(cm, q=q, k=k, v=v, bcol=bcol, masks=masks, ch=ch, h=h, cc=cc))
    return chains


def _chain_shape(rows, cols, dtype):
    return lambda nc: jax.ShapeDtypeStruct((nc, N_CHAINS, rows, cols), dtype)


def _gdn_local_fwd(qkvc, gb, gbt):
    t = qkvc.shape[0]
    nc = t // CHUNK
    hd = GDN_HEADS * GDN_DIM

    def body(x_ref, g_ref, gt_ref, t_ref, u_ref, w_ref, qg_ref, kd_ref, in_ref, eg_ref):
        chains = [c for cc in range(LOCAL_CHUNKS) for c in _load_chains(x_ref, g_ref, gt_ref, cc)]
        ii, jj = chains[0]["masks"][0:2]
        tms = _tri_inv_many([jnp.where(c["masks"][3], c["amat"] * c["decay"], 0.0) for c in chains], ii, jj)
        uws = [_dot(tm, jnp.concatenate([c["vb"], c["kbg"]], axis=1)) for tm, c in zip(tms, chains)]
        for c, tm, uw in zip(chains, tms, uws):
            cc, ch = c["cc"], c["ch"]
            t_ref[cc, ch] = tm
            u_ref[cc, ch] = uw[:, :GDN_DIM]
            w_ref[cc, ch] = uw[:, GDN_DIM:].astype(BF16)
            qg_ref[cc, ch] = c["qg"].astype(BF16)
            kd_ref[cc, ch] = c["kdec"].astype(BF16)
            in_ref[cc, ch] = c["intra"].astype(BF16)
            eg_ref[cc, ch:ch + 1, :] = jnp.broadcast_to(jnp.exp(c["glast"]), (1, LANES))

    lc = LOCAL_CHUNKS
    blk = lambda rows, cols: pl.BlockSpec((lc, N_CHAINS, rows, cols), lambda n: (n, 0, 0, 0))
    shapes = [_chain_shape(CHUNK, CHUNK, F32), _chain_shape(CHUNK, GDN_DIM, F32), _chain_shape(CHUNK, GDN_DIM, BF16),
              _chain_shape(CHUNK, GDN_DIM, BF16), _chain_shape(CHUNK, GDN_DIM, BF16), _chain_shape(CHUNK, CHUNK, BF16)]
    return tuple(pl.pallas_call(
        body, name="gdn_local_fwd", grid=(nc // lc,),
        in_specs=[pl.BlockSpec((lc * CHUNK, 3 * hd), lambda n: (n, 0)), pl.BlockSpec((lc * CHUNK, LANES), lambda n: (n, 0)),
                  pl.BlockSpec((lc, 16, CHUNK), lambda n: (n, 0, 0))],
        out_specs=[blk(CHUNK, CHUNK), blk(CHUNK, GDN_DIM), blk(CHUNK, GDN_DIM), blk(CHUNK, GDN_DIM),
                   blk(CHUNK, GDN_DIM), blk(CHUNK, CHUNK), pl.BlockSpec((lc, N_CHAINS, LANES), lambda n: (n, 0, 0))],
        out_shape=[s(nc) for s in shapes] + [jax.ShapeDtypeStruct((nc, N_CHAINS, LANES), F32)],
        compiler_params=_params(("arbitrary",), VMEM_LIMIT),
    )(qkvc, gb, gbt))


SCAN_CHUNKS = 8


def _dir_specs(nc, rev):
    nb = nc // SCAN_CHUNKS

    def spec(d, rows, cols, own=False):
        chunk = (lambda n: n) if (d == 0) != rev else (lambda n: nb - 1 - n)
        blk = 0 if own else d
        if rows is None:
            return pl.BlockSpec((SCAN_CHUNKS, GDN_HEADS if own else N_CHAINS, cols), lambda n: (chunk(n), 0, 0))
        return pl.BlockSpec((SCAN_CHUNKS, GDN_HEADS, rows, cols), lambda n: (chunk(n), blk, 0, 0))

    def rows_spec(d, cols):
        chunk = (lambda n: n) if (d == 0) != rev else (lambda n: nb - 1 - n)
        return pl.BlockSpec((SCAN_CHUNKS * CHUNK, cols), lambda n: (chunk(n), 0))

    def order(d):
        return list(range(SCAN_CHUNKS)) if (d == 0) != rev else list(range(SCAN_CHUNKS - 1, -1, -1))
    return spec, rows_spec, order


def _gdn_scan_fwd(u, w, qg, kd, intra, egl, t):
    nc = t // CHUNK
    hd = GDN_HEADS * GDN_DIM

    def body(*refs):
        ins, outs, state = refs[:12], refs[12:18], refs[18]
        @pl.when(pl.program_id(0) == 0)
        def _():
            state[...] = jnp.zeros_like(state)

        chains = [(d, h) for d in range(2) for h in range(GDN_HEADS)]
        states = [state[ch] for ch in range(N_CHAINS)]
        for step in range(SCAN_CHUNKS):
            at = [order(d)[step] for d in range(2)]
            pick = lambda k, d, h: ins[2 * k + d][at[d], h]
            sbs = [s.astype(BF16) for s in states]
            ws = [_dot(pick(1, d, h), sb) for (d, h), sb in zip(chains, sbs)]
            o1 = [_dot(pick(2, d, h), sb) for (d, h), sb in zip(chains, sbs)]
            vns = [(pick(0, d, h) - wsb).astype(BF16) for (d, h), wsb in zip(chains, ws)]
            o2 = [_dot(pick(4, d, h), vn) for (d, h), vn in zip(chains, vns)]
            kv = [_dot_tn(pick(3, d, h), vn) for (d, h), vn in zip(chains, vns)]
            new_states = []
            for ch, (d, h) in enumerate(chains):
                outs[d][at[d] * CHUNK:(at[d] + 1) * CHUNK, h * GDN_DIM:(h + 1) * GDN_DIM] = o1[ch] + o2[ch]
                outs[2 + d][at[d], h] = states[ch]
                outs[4 + d][at[d], h] = vns[ch]
                new_states.append(states[ch] * ins[10 + d][at[d], ch:ch + 1, :] + kv[ch])
            states = new_states
        for ch in range(N_CHAINS):
            state[ch] = states[ch]

    spec, rows_spec, order = _dir_specs(nc, False)
    pair = lambda rows, cols, own=False: [spec(0, rows, cols, own), spec(1, rows, cols, own)]
    s_shape = jax.ShapeDtypeStruct((nc, GDN_HEADS, GDN_DIM, GDN_DIM), F32)
    vn_shape = jax.ShapeDtypeStruct((nc, GDN_HEADS, CHUNK, GDN_DIM), BF16)
    return pl.pallas_call(
        body, name="gdn_scan_fwd", grid=(nc // SCAN_CHUNKS,),
        in_specs=(pair(CHUNK, GDN_DIM) + pair(CHUNK, GDN_DIM) + pair(CHUNK, GDN_DIM) + pair(CHUNK, GDN_DIM)
                  + pair(CHUNK, CHUNK) + pair(None, LANES)),
        out_specs=([rows_spec(0, hd), rows_spec(1, hd)] + pair(GDN_DIM, GDN_DIM, True)
                   + pair(CHUNK, GDN_DIM, True)),
        out_shape=[jax.ShapeDtypeStruct((t, hd), F32), jax.ShapeDtypeStruct((t, hd), F32),
                   s_shape, s_shape, vn_shape, vn_shape],
        scratch_shapes=[pltpu.VMEM((N_CHAINS, GDN_DIM, GDN_DIM), F32)],
        compiler_params=_params(("arbitrary",), VMEM_LIMIT),
    )(u, u, w, w, qg, qg, kd, kd, intra, intra, egl, egl)


def _gdn_bwd(qkvc, gb, gbt, do, saved, exchange=None):
    scan = _gdn_scan_bwd(do, saved, qkvc.shape[0])
    return _gdn_local_bwd(qkvc, gb, gbt, do, saved, scan, exchange)


def _gdn_scan_bwd(do, saved, t):
    nc = t // CHUNK
    hd = GDN_HEADS * GDN_DIM

    def body(*refs):
        ins, outs, dstate = refs[:16], refs[16:26], refs[26]
        @pl.when(pl.program_id(0) == 0)
        def _():
            dstate[...] = jnp.zeros_like(dstate)

        chains = [(d, h) for d in range(2) for h in range(GDN_HEADS)]
        dss = [dstate[ch] for ch in range(N_CHAINS)]
        for step in range(SCAN_CHUNKS):
            at = [order(d)[step] for d in range(2)]
            pick = lambda k, d, h: ins[2 * k + d][at[d], h]
            dsbs = [ds.astype(BF16) for ds in dss]
            ss = [pick(1, d, h) for d, h in chains]
            sbs = [s.astype(BF16) for s in ss]
            dos = [ins[d][at[d] * CHUNK:(at[d] + 1) * CHUNK, h * GDN_DIM:(h + 1) * GDN_DIM].astype(BF16)
                   for d, h in chains]
            dv1 = [_dot_tn(pick(5, d, h), dov) for (d, h), dov in zip(chains, dos)]
            dv2 = [_dot(pick(4, d, h), dsb) for (d, h), dsb in zip(chains, dsbs)]
            ds1 = [_dot_tn(pick(3, d, h), dov) for (d, h), dov in zip(chains, dos)]
            dkds = [_dot_nt(pick(6, d, h), dsb) for (d, h), dsb in zip(chains, dsbs)]
            dqgs = [_dot_nt(dov, sb) for dov, sb in zip(dos, sbs)]
            dvns = [(a + b).astype(BF16) for a, b in zip(dv1, dv2)]
            ds2 = [_dot_tn(pick(2, d, h), dvn) for (d, h), dvn in zip(chains, dvns)]
            dws = [_dot_nt(dvn, sb) for dvn, sb in zip(dvns, sbs)]
            new_dss = []
            for ch, (d, h) in enumerate(chains):
                egl = ins[14 + d][at[d], ch:ch + 1, :]
                outs[d][at[d], h] = dvns[ch]
                outs[2 + d][at[d], h] = (-dws[ch]).astype(BF16)
                outs[4 + d][at[d], h] = dqgs[ch]
                outs[6 + d][at[d], h] = dkds[ch]
                outs[8 + d][at[d], h:h + 1, :] = egl * jnp.sum(jnp.sum(ss[ch] * dss[ch], axis=1, keepdims=True),
                                                               axis=0, keepdims=True)
                new_dss.append(ds1[ch] + egl * dss[ch] - ds2[ch])
            dss = new_dss
        for ch in range(N_CHAINS):
            dstate[ch] = dss[ch]

    spec, rows_spec, order = _dir_specs(nc, True)
    pair = lambda rows, cols, own=False: [spec(0, rows, cols, own), spec(1, rows, cols, own)]
    s_f, s_b = saved["s"]
    vn_f, vn_b = saved["vn"]
    w, qg, kd, intra, egl = saved["w"], saved["qg"], saved["kd"], saved["intra"], saved["egl"]
    own = lambda rows, cols, dtype: jax.ShapeDtypeStruct((nc, GDN_HEADS, rows, cols), dtype)
    row_shape = jax.ShapeDtypeStruct((nc, GDN_HEADS, LANES), F32)
    return pl.pallas_call(
        body, name="gdn_scan_bwd", grid=(nc // SCAN_CHUNKS,),
        in_specs=([rows_spec(0, hd), rows_spec(1, hd)] + pair(GDN_DIM, GDN_DIM, True) + pair(CHUNK, GDN_DIM)
                  + pair(CHUNK, GDN_DIM) + pair(CHUNK, GDN_DIM) + pair(CHUNK, CHUNK) + pair(CHUNK, GDN_DIM, True)
                  + pair(None, LANES)),
        out_specs=(pair(CHUNK, GDN_DIM, True) + pair(CHUNK, GDN_DIM, True) + pair(CHUNK, GDN_DIM, True)
                   + pair(CHUNK, GDN_DIM, True) + pair(None, LANES, True)),
        out_shape=[own(CHUNK, GDN_DIM, BF16)] * 4 + [own(CHUNK, GDN_DIM, F32)] * 4 + [row_shape] * 2,
        scratch_shapes=[pltpu.VMEM((N_CHAINS, GDN_DIM, GDN_DIM), F32)],
        compiler_params=_params(("arbitrary",), VMEM_LIMIT),
    )(do, do, s_f, s_b, w, w, qg, qg, kd, kd, intra, intra, vn_f, vn_b, egl, egl)


def _dot3_nt(a, b):
    ah = a.astype(BF16)
    al = (a - ah.astype(F32)).astype(BF16)
    bh = b.astype(BF16)
    bl = (b - bh.astype(F32)).astype(BF16)
    return _dot_nt(ah, bh) + (_dot_nt(ah, bl) + _dot_nt(al, bh))


def _dot3_tn(a, b):
    ah = a.astype(BF16)
    al = (a - ah.astype(F32)).astype(BF16)
    bh = b.astype(BF16)
    bl = (b - bh.astype(F32)).astype(BF16)
    return _dot_tn(ah, bh) + (_dot_tn(ah, bl) + _dot_tn(al, bh))


def _gdn_local_bwd(qkvc, gb, gbt, do, saved, scan, exchange=None):
    t = qkvc.shape[0]
    nc = t // CHUNK
    hd = GDN_HEADS * GDN_DIM

    def body(*refs):
        x_ref, g_ref, gt_ref, do_ref, t_ref = refs[:5]
        per_dir = refs[5:17]
        dx_ref, dg_ref = refs[17:]
        chains = [c for cc in range(LOCAL_CHUNKS) for c in _load_chains(x_ref, g_ref, gt_ref, cc)]
        lane = lax.broadcasted_iota(jnp.int32, (CHUNK, LANES), 1)
        dgates = [jnp.zeros((CHUNK, LANES), F32) for _ in range(LOCAL_CHUNKS)]
        for c in chains:
            d = c["ch"] // GDN_HEADS
            vn_ref, dvn_ref, dw_ref, dqg_ref, dkd_ref, dgl_ref = per_dir[d::2]
            h, cc = c["h"], c["cc"]
            rows = slice(cc * CHUNK, (cc + 1) * CHUNK)
            c.update(tm=t_ref[cc, c["ch"]], dov=do_ref[rows, h * GDN_DIM:(h + 1) * GDN_DIM], vnew=vn_ref[cc, h],
                     dvnew=dvn_ref[cc, h], dw=dw_ref[cc, h], dqg=dqg_ref[cc, h], dkdec=dkd_ref[cc, h],
                     dglast=dgl_ref[cc, h:h + 1, 0:1])
        dintras = [_dot_nt(c["dov"], c["vnew"]) for c in chains]
        dts = [_dot_nt(c["dvnew"], c["vb"]) + _dot_nt(c["dw"], c["kbg"]) for c in chains]
        dvbs = [_dot_tn(c["tm"], c["dvnew"]) for c in chains]
        dkbgs = [_dot_tn(c["tm"], c["dw"]) for c in chains]
        tdts = [_dot3_nt(dt, c["tm"]) for dt, c in zip(dts, chains)]
        dls = [jnp.where(c["masks"][3], -_dot3_tn(c["tm"], tdt), 0.0) for tdt, c in zip(tdts, chains)]
        das = [dl * c["decay"] for dl, c in zip(dls, chains)]
        dqks = [jnp.where(c["masks"][2], di, 0.0) * c["decay"] for di, c in zip(dintras, chains)]
        dkb1 = [_dot(da, c["k"]) for da, c in zip(das, chains)]
        dk1 = [_dot_tn(da, c["kb"]) for da, c in zip(das, chains)]
        dk2 = [_dot_tn(dqk, c["q"]) for dqk, c in zip(dqks, chains)]
        dq1 = [_dot(dqk, c["k"]) for dqk, c in zip(dqks, chains)]
        grads, mms, p_gs, p_betas, p_kds = [], [], [], [], []
        for n, c in enumerate(chains):
            incl = c["masks"][2]
            dkb = dkb1[n] + dkbgs[n] * c["eg"]
            kd = c["dkdec"] * c["kdec"]
            mms.append((dls[n] * c["amat"] + jnp.where(incl, dintras[n], 0.0) * c["qk"]) * c["decay"])
            p_gs.append(c["dqg"] * c["qg"] - kd + dkbgs[n] * c["kbg"])
            p_betas.append(dkb * c["k"] + dvbs[n] * c["v"])
            p_kds.append(kd)
            grads.append((dq1[n] + c["dqg"] * c["eg"],
                          dk1[n] + dk2[n] + c["dkdec"] * c["ek"] + dkb * c["bcol"],
                          dvbs[n] * c["bcol"]))
        row_sums = [jnp.sum(mm, axis=1, keepdims=True) for mm in mms]
        col_sums = [jnp.sum(mm, axis=0, keepdims=True) for mm in mms]
        g_sums = [jnp.sum(pg, axis=1, keepdims=True) for pg in p_gs]
        dbetas = [jnp.sum(pb, axis=1, keepdims=True) for pb in p_betas]
        kd_tots = [jnp.sum(jnp.sum(pk, axis=1, keepdims=True), axis=0, keepdims=True) for pk in p_kds]
        dgcs = [rs - _row_to_col(cs, *c["masks"][0:2]) + gs for rs, cs, gs, c in zip(row_sums, col_sums, g_sums, chains)]
        dgrs = [_col_to_row(dgc, *c["masks"][0:2]) for dgc, c in zip(dgcs, chains)]
        draws = [jnp.sum(jnp.where(jnp.logical_not(c["masks"][3]), dgr, 0.0), axis=1, keepdims=True) + c["dglast"] + kt
                 for dgr, kt, c in zip(dgrs, kd_tots, chains)]
        for c, draw, dbeta in zip(chains, draws, dbetas):
            ch = c["ch"]
            dgates[c["cc"]] = dgates[c["cc"]] + jnp.where(lane == ch, draw, 0.0) + jnp.where(lane == 8 + ch, dbeta, 0.0)
        for cc in range(LOCAL_CHUNKS):
            rows = slice(cc * CHUNK, (cc + 1) * CHUNK)
            for h in range(GDN_HEADS):
                for part in range(3):
                    cols = slice(part * hd + h * GDN_DIM, part * hd + (h + 1) * GDN_DIM)
                    dx_ref[rows, cols] = grads[cc * N_CHAINS + h][part] + grads[cc * N_CHAINS + GDN_HEADS + h][part]
            dg_ref[rows, :] = dgates[cc]

    lc = LOCAL_CHUNKS
    all8 = lambda rows, cols: pl.BlockSpec((lc, N_CHAINS, rows, cols), lambda n: (n, 0, 0, 0))
    own4 = lambda rows, cols: pl.BlockSpec((lc, GDN_HEADS, rows, cols), lambda n: (n, 0, 0, 0))
    row4 = pl.BlockSpec((lc, GDN_HEADS, LANES), lambda n: (n, 0, 0))
    vn_f, vn_b = saved["vn"]
    dvn_f, dvn_b, dw_f, dw_b, dqg_f, dqg_b, dkd_f, dkd_b, dgl_f, dgl_b = scan
    return _grid_call(
        body, "gdn_local_bwd", nc // lc,
        [pl.BlockSpec((lc * CHUNK, 3 * hd), lambda n: (n, 0)), pl.BlockSpec((lc * CHUNK, LANES), lambda n: (n, 0)),
         pl.BlockSpec((lc, 16, CHUNK), lambda n: (n, 0, 0)), pl.BlockSpec((lc * CHUNK, hd), lambda n: (n, 0)),
         all8(CHUNK, CHUNK)] + [own4(CHUNK, GDN_DIM)] * 10 + [row4, row4],
        [pl.BlockSpec((lc * CHUNK, 3 * hd), lambda n: (n, 0)), pl.BlockSpec((lc * CHUNK, LANES), lambda n: (n, 0))],
        [jax.ShapeDtypeStruct((t, 3 * hd), F32), jax.ShapeDtypeStruct((t, LANES), F32)],
        (qkvc, gb, gbt, do, saved["tm"], vn_f, vn_b, dvn_f, dvn_b, dw_f, dw_b, dqg_f, dqg_b, dkd_f, dkd_b, dgl_f, dgl_b),
        exchange=exchange)


def _gdn_post_fwd(of, ob, z, gw, tm):
    t, hd = of.shape

    def body(of_ref, ob_ref, z_ref, w_ref, o_ref):
        for h in range(GDN_HEADS):
            cols = slice(h * GDN_DIM, (h + 1) * GDN_DIM)
            o = of_ref[:, cols] + ob_ref[:, cols]
            zv = z_ref[:, cols]
            o_ref[:, cols] = (o * _rstd(o) * w_ref[...] * (zv * _sigmoid(zv))).astype(BF16)

    row = pl.BlockSpec((tm, hd), lambda i: (i, 0))
    return pl.pallas_call(
        body, name="gdn_post_fwd", grid=(t // tm,),
        in_specs=[row, row, row, _resident((1, GDN_DIM))],
        out_specs=row, out_shape=jax.ShapeDtypeStruct((t, hd), BF16),
        compiler_params=_params(("arbitrary",), VMEM_LIMIT),
    )(of, ob, z, gw)


def _gdn_post_bwd(doa, of, ob, z, gw, tm):
    t, hd = of.shape

    def body(d_ref, of_ref, ob_ref, z_ref, w_ref, do_ref, dz_ref, dw_ref):
        @pl.when(pl.program_id(0) == 0)
        def _():
            dw_ref[...] = jnp.zeros_like(dw_ref)

        dw = jnp.zeros((1, GDN_DIM), F32)
        for h in range(GDN_HEADS):
            cols = slice(h * GDN_DIM, (h + 1) * GDN_DIM)
            o = of_ref[:, cols] + ob_ref[:, cols]
            zv = z_ref[:, cols]
            dv = d_ref[:, cols]
            r = _rstd(o)
            sg = _sigmoid(zv)
            on = o * r * w_ref[...]
            dz_ref[:, cols] = (dv * on * (sg * (1.0 + zv * (1.0 - sg)))).astype(BF16)
            dxr, dwh = _rms_bwd(o, r, w_ref[...], dv * (zv * sg))
            do_ref[:, cols] = dxr
            dw = dw + dwh
        dw_ref[...] += dw

    row = pl.BlockSpec((tm, hd), lambda i: (i, 0))
    return pl.pallas_call(
        body, name="gdn_post_bwd", grid=(t // tm,),
        in_specs=[row, row, row, row, _resident((1, GDN_DIM))],
        out_specs=[row, row, pl.BlockSpec((1, GDN_DIM), lambda i: (0, 0))],
        out_shape=[jax.ShapeDtypeStruct((t, hd), F32), jax.ShapeDtypeStruct((t, hd), BF16),
                   jax.ShapeDtypeStruct((1, GDN_DIM), F32)],
        compiler_params=_params(("arbitrary",), VMEM_LIMIT),
    )(doa, of, ob, z, gw)


SWA_W = SWA_HEADS * SWA_DIM
QBLK = 128
KWIN = QBLK + 2 * RADIUS
WIN_OFFSETS = (0, RADIUS, 2 * RADIUS)


def _t5_bucket(rel):
    nb = REL_BUCKETS // 2
    bucket = (rel > 0).astype(np.int32) * nb
    n = np.abs(rel)
    max_exact = nb // 2
    large = max_exact + (np.log(np.maximum(n, 1) / max_exact)
                         / math.log(REL_MAX_DISTANCE / max_exact) * (nb - max_exact)).astype(np.int32)
    large = np.minimum(large, nb - 1)
    return (bucket + np.where(n < max_exact, n, large)).astype(np.int32)


def _band_tables(dilation):
    a = np.arange(QBLK)
    b = np.arange(KWIN)
    rel = np.stack([b[None, :] - w0 - a[:, None] for w0 in WIN_OFFSETS])
    return np.where(np.abs(rel) <= RADIUS, _t5_bucket(rel * dilation), -1).astype(np.int32)


BAND_CELLS = len(WIN_OFFSETS) * QBLK * KWIN
BIAS_TILE = BAND_CELLS // 3


def _band_index():
    return jnp.asarray(np.concatenate([_band_tables(d).reshape(-1) for _, d in PATTERNS])[None, :])


def _onehot(idx, dtype):
    return (lax.broadcasted_iota(jnp.int32, (REL_BUCKETS, idx.shape[1]), 0) == idx).astype(dtype)


def _bias_tables(rel_bias, idx, tk):
    n = idx.shape[1]

    def body(rb_ref, i_ref, o_ref):
        iv = i_ref[...]
        oh = _onehot(iv, BF16)
        rest, acc = rb_ref[...], None
        for _ in range(3):
            piece = rest.astype(BF16)
            part = jnp.dot(piece, oh, preferred_element_type=F32)
            acc = part if acc is None else acc + part
            rest = rest - piece.astype(F32)
        o_ref[...] = jnp.where(iv < 0, NEG_BIG, acc)

    return pl.pallas_call(
        body, name="bias_tables", grid=(n // tk,),
        in_specs=[_resident((SWA_HEADS, REL_BUCKETS)), pl.BlockSpec((1, tk), lambda k: (0, k))],
        out_specs=pl.BlockSpec((SWA_HEADS, tk), lambda k: (0, k)),
        out_shape=jax.ShapeDtypeStruct((SWA_HEADS, n), F32),
        compiler_params=_params(("arbitrary",), VMEM_LIMIT),
    )(rel_bias.T, idx)


def _head_mean(x2, bd_ref):
    bd = bd_ref[...]
    rest, acc = x2, None
    for _ in range(3):
        piece = rest.astype(BF16)
        part = jnp.dot(piece, bd, preferred_element_type=F32)
        acc = part if acc is None else acc + part
        rest = rest - piece.astype(F32)
    return acc


VIEW_DILATIONS = tuple(d for _, d in PATTERNS if d > 1)


def _view_spec(tm, d):
    return pl.BlockSpec((tm // d, d * SWA_W), lambda i: (i, 0))


def _view_shape(t, d, dtype):
    return jax.ShapeDtypeStruct((t // d, d * SWA_W), dtype)


N_GROUPS = SWA_W // LANES


def _to_view(src_ref, idx, dst_ref, d, rows):
    for r in range(d):
        for g in range(N_GROUPS):
            cols = slice(r * SWA_W + g * LANES, r * SWA_W + (g + 1) * LANES)
            dst_ref[:, cols] = src_ref[idx, g, pl.ds(r, rows // d, stride=d), :].astype(dst_ref.dtype)


def _from_view(src_ref, dst_ref, idx, d, rows):
    for r in range(d):
        for g in range(N_GROUPS):
            cols = slice(r * SWA_W + g * LANES, r * SWA_W + (g + 1) * LANES)
            dst_ref[idx, g, pl.ds(r, rows // d, stride=d), :] = src_ref[:, cols]


def _swa_prep_fwd(qkvb, qw, kw, bd, tm):
    t = qkvb.shape[0]

    def body(x_ref, qw_ref, kw_ref, bd_ref, *rest):
        outs, sc = rest[:-1], rest[-1]
        for gidx in range(N_GROUPS):
            cols = slice(gidx * LANES, (gidx + 1) * LANES)
            xq = x_ref[:, cols]
            sc[0, gidx] = xq * lax.rsqrt(_head_mean(xq * xq, bd_ref) + EPS) * qw_ref[:, cols] * (SWA_DIM ** -0.5)
            xk = x_ref[:, SWA_W + gidx * LANES:SWA_W + (gidx + 1) * LANES]
            sc[1, gidx] = xk * lax.rsqrt(_head_mean(xk * xk, bd_ref) + EPS) * kw_ref[:, cols]
            sc[2, gidx] = x_ref[:, 2 * SWA_W + gidx * LANES:2 * SWA_W + (gidx + 1) * LANES]
            for i in range(3):
                outs[i][:, cols] = sc[i, gidx].astype(BF16)
        for i in range(3):
            for n, d in enumerate(VIEW_DILATIONS):
                _to_view(sc, i, outs[3 * (n + 1) + i], d, tm)

    return pl.pallas_call(
        body, name="swa_prep_fwd", grid=(t // tm,),
        in_specs=[pl.BlockSpec((tm, 3 * SWA_W), lambda i: (i, 0)), _resident((1, SWA_W)), _resident((1, SWA_W)),
                  _resident((LANES, LANES))],
        out_specs=[_view_spec(tm, d) for d in (1,) + VIEW_DILATIONS for _ in range(3)],
        out_shape=[_view_shape(t, d, BF16) for d in (1,) + VIEW_DILATIONS for _ in range(3)],
        scratch_shapes=[pltpu.VMEM((3, N_GROUPS, tm, LANES), F32)],
        compiler_params=_params(("arbitrary",), VMEM_LIMIT),
    )(qkvb, qw, kw, bd)


def _swa_prep_bwd(qkvb, qw, kw, bd, grads, tm):
    t = qkvb.shape[0]

    def body(x_ref, qw_ref, kw_ref, bd_ref, *rest):
        parts, (dx_ref, dqw_ref, dkw_ref, sc) = rest[:9], rest[9:]
        @pl.when(pl.program_id(0) == 0)
        def _():
            dqw_ref[...] = jnp.zeros_like(dqw_ref)
            dkw_ref[...] = jnp.zeros_like(dkw_ref)

        for i in range(3):
            for n, d in enumerate(VIEW_DILATIONS):
                _from_view(parts[3 * (n + 1) + i], sc, 2 * i + n, d, tm)
        for gidx in range(N_GROUPS):
            cols = slice(gidx * LANES, (gidx + 1) * LANES)
            for i, base, w_ref, dw_ref, scale in ((0, 0, qw_ref, dqw_ref, SWA_DIM ** -0.5),
                                                  (1, SWA_W, kw_ref, dkw_ref, 1.0)):
                xv = x_ref[:, base + gidx * LANES:base + (gidx + 1) * LANES]
                dy = (parts[i][:, cols] + sc[2 * i, gidx] + sc[2 * i + 1, gidx]) * scale
                r = lax.rsqrt(_head_mean(xv * xv, bd_ref) + EPS)
                xhat = xv * r
                dxh = dy * w_ref[:, cols]
                dx = r * (dxh - xhat * _head_mean(dxh * xhat, bd_ref))
                dx_ref[:, base + gidx * LANES:base + (gidx + 1) * LANES] = dx.astype(BF16)
                dw_ref[:, cols] += jnp.sum(dy * xhat, axis=0, keepdims=True)
            dx_ref[:, 2 * SWA_W + gidx * LANES:2 * SWA_W + (gidx + 1) * LANES] = (
                parts[2][:, cols] + sc[4, gidx] + sc[5, gidx]).astype(BF16)

    wrow = pl.BlockSpec((1, SWA_W), lambda i: (0, 0))
    return pl.pallas_call(
        body, name="swa_prep_bwd", grid=(t // tm,),
        in_specs=[pl.BlockSpec((tm, 3 * SWA_W), lambda i: (i, 0)), _resident((1, SWA_W)), _resident((1, SWA_W)),
                  _resident((LANES, LANES))] + [_view_spec(tm, d) for d in (1,) + VIEW_DILATIONS for _ in range(3)],
        out_specs=[pl.BlockSpec((tm, 3 * SWA_W), lambda i: (i, 0)), wrow, wrow],
        out_shape=[jax.ShapeDtypeStruct((t, 3 * SWA_W), BF16), jax.ShapeDtypeStruct((1, SWA_W), F32),
                   jax.ShapeDtypeStruct((1, SWA_W), F32)],
        scratch_shapes=[pltpu.VMEM((6, N_GROUPS, tm, LANES), F32)],
        compiler_params=_params(("arbitrary",), VMEM_LIMIT),
    )(qkvb, qw, kw, bd, *grads)


def _aligned(v, m):
    return v if isinstance(v, int) else pl.multiple_of(v, m)


BAND_GROUP = 2


def _band_loop(nsub, length, step, group=BAND_GROUP):
    step([(0, 0)], 0)
    if nsub > 2:
        assert (nsub - 2) % group == 0

        def inner(i, carry):
            s0 = 1 + i * group
            step([(s0 + e, pl.multiple_of((s0 + e) * QBLK - RADIUS, RADIUS)) for e in range(group)], 1)
            return carry
        lax.fori_loop(0, (nsub - 2) // group, inner, 0)
    step([(nsub - 1, length - KWIN)], 2)


def _head_select(lane, a0, a1):
    return jnp.where(lane < SWA_DIM, a0, a1)


def _swa_fwd(qv, kv, vv, bias, dilation, name):
    length = qv.shape[0]
    nsub = length // QBLK
    assert nsub >= 2 and length % QBLK == 0

    def body(q_ref, k_ref, v_ref, b_ref, o_ref, l_ref):
        lane = lax.broadcasted_iota(jnp.int32, (QBLK, LANES), 1)

        def step(blocks, var):
            items = []
            for s, ws in blocks:
                rows = pl.ds(_aligned(s * QBLK, QBLK), QBLK)
                q, kk, vw = q_ref[rows, :], k_ref[pl.ds(ws, KWIN), :], v_ref[pl.ds(ws, KWIN), :]
                for hh in range(2):
                    items.append((hh, jnp.where((lane < SWA_DIM) == (hh == 0), q, jnp.zeros_like(q)), kk, vw))
            lgs = [_dot_nt(qh, kk) + b_ref[hh, var] for hh, qh, kk, _ in items]
            ms = [jnp.max(lg, axis=-1, keepdims=True) for lg in lgs]
            ps = [jnp.exp(lg - m) for lg, m in zip(lgs, ms)]
            dens = [jnp.sum(p, axis=-1, keepdims=True) for p in ps]
            pvs = [_dot(p, it[3]) for p, it in zip(ps, items)]
            for n, (s, _) in enumerate(blocks):
                rows = pl.ds(_aligned(s * QBLK, QBLK), QBLK)
                o0, o1 = (pvs[2 * n + hh] / dens[2 * n + hh] for hh in range(2))
                l0, l1 = (ms[2 * n + hh] + jnp.log(dens[2 * n + hh]) for hh in range(2))
                o_ref[rows, :] = _head_select(lane, o0, o1)
                l_ref[rows, :] = _head_select(lane, l0, l1)

        _band_loop(nsub, length, step)

    blk = pl.BlockSpec((length, LANES), lambda hp, r: (0, r * (SWA_W // LANES) + hp))
    shp = jax.ShapeDtypeStruct(qv.shape, F32)
    return pl.pallas_call(
        body, name=name, grid=(SWA_W // LANES, dilation),
        in_specs=[blk, blk, blk, pl.BlockSpec((2, 3, QBLK, KWIN), lambda hp, r: (hp, 0, 0, 0))],
        out_specs=[blk, blk], out_shape=[shp, shp],
        compiler_params=_params(("arbitrary", "arbitrary"), VMEM_LIMIT),
    )(qv, kv, vv, bias)


def _swa_combine(os_, ls_, tm):
    t = os_[0].shape[0]

    def body(o0, o1, o2, l0, l1, l2, o_ref, ob_ref, la_ref, lb_ref, lc_ref, sc):
        for n, d in enumerate(VIEW_DILATIONS):
            _from_view((o1, o2)[n], sc, n, d, tm)
            _from_view((l1, l2)[n], sc, 2 + n, d, tm)
        for g in range(N_GROUPS):
            cols = slice(g * LANES, (g + 1) * LANES)
            la, lb, lc = l0[:, cols], sc[2, g], sc[3, g]
            m = jnp.maximum(jnp.maximum(la, lb), lc)
            tot = m + jnp.log(jnp.exp(la - m) + jnp.exp(lb - m) + jnp.exp(lc - m))
            o = jnp.exp(la - tot) * o0[:, cols] + jnp.exp(lb - tot) * sc[0, g] + jnp.exp(lc - tot) * sc[1, g]
            o_ref[:, cols] = o
            ob_ref[:, cols] = o.astype(BF16)
            la_ref[:, cols] = tot
            sc[4, g] = tot
        for n, d in enumerate(VIEW_DILATIONS):
            _to_view(sc, 4, (lb_ref, lc_ref)[n], d, tm)

    specs = [_view_spec(tm, d) for d in (1,) + VIEW_DILATIONS]
    return pl.pallas_call(
        body, name="swa_combine", grid=(t // tm,), in_specs=specs + specs, out_specs=[specs[0], specs[0]] + specs,
        out_shape=[jax.ShapeDtypeStruct((t, SWA_W), F32), jax.ShapeDtypeStruct((t, SWA_W), BF16)]
                  + [_view_shape(t, d, F32) for d in (1,) + VIEW_DILATIONS],
        scratch_shapes=[pltpu.VMEM((5, N_GROUPS, tm, LANES), F32)],
        compiler_params=_params(("arbitrary",), VMEM_LIMIT),
    )(*os_, *ls_)


def _swa_bwd_prep(do, o, bd, tm):
    t = do.shape[0]

    def body(d_ref, o_ref, bd_ref, dd1, dd4, dd16, db1, db4, db16, sc):
        for gidx in range(N_GROUPS):
            cols = slice(gidx * LANES, (gidx + 1) * LANES)
            dv = d_ref[:, cols]
            dd = _head_mean(dv * o_ref[:, cols], bd_ref) * float(SWA_DIM)
            sc[0, gidx] = dd
            sc[1, gidx] = dv
            dd1[:, cols] = dd
            db1[:, cols] = dv.astype(BF16)
        for n, d in enumerate(VIEW_DILATIONS):
            _to_view(sc, 0, (dd4, dd16)[n], d, tm)
            _to_view(sc, 1, (db4, db16)[n], d, tm)

    specs = [_view_spec(tm, d) for d in (1,) + VIEW_DILATIONS]
    return pl.pallas_call(
        body, name="swa_bwd_prep", grid=(t // tm,), in_specs=[specs[0], specs[0], _resident((LANES, LANES))],
        out_specs=specs + specs,
        out_shape=[_view_shape(t, d, F32) for d in (1,) + VIEW_DILATIONS]
                  + [_view_shape(t, d, BF16) for d in (1,) + VIEW_DILATIONS],
        scratch_shapes=[pltpu.VMEM((2, N_GROUPS, tm, LANES), F32)],
        compiler_params=_params(("arbitrary",), VMEM_LIMIT),
    )(do, o, bd)


def _swa_bwd(qv, kv, vv, dov, lv, ddv, bias_a, dilation, name):
    length = qv.shape[0]
    nsub = length // QBLK
    single = pl.Buffered(1) if dilation == 1 else None

    def body(q_ref, k_ref, v_ref, do_ref, l_ref, dd_ref, ba_ref, dq_ref, dk_ref, dv_ref, db_ref):
        @pl.when(pl.program_id(1) == 0)
        def _():
            db_ref[...] = jnp.zeros_like(db_ref)

        lane = lax.broadcasted_iota(jnp.int32, (QBLK, LANES), 1)
        lanew = lax.broadcasted_iota(jnp.int32, (KWIN, LANES), 1)

        def step(blocks, var):
            items = []
            for s, ws in blocks:
                rows = pl.ds(_aligned(s * QBLK, QBLK), QBLK)
                win = pl.ds(ws, KWIN)
                q, dov_ = q_ref[rows, :], do_ref[rows, :]
                kk, vw = k_ref[win, :], v_ref[win, :]
                lse, dd = l_ref[rows, :], dd_ref[rows, :]
                for hh in range(2):
                    mine = (lane < SWA_DIM) == (hh == 0)
                    col = slice(hh * SWA_DIM, hh * SWA_DIM + 1)
                    items.append((hh, jnp.where(mine, q, jnp.zeros_like(q)), jnp.where(mine, dov_, jnp.zeros_like(dov_)),
                                  kk, vw, lse[:, col], dd[:, col], q, dov_))
            lgs = [_dot_nt(it[1], it[3]) + ba_ref[it[0], var] for it in items]
            dps = [_dot_nt(it[2], it[4]) for it in items]
            ps = [jnp.exp(lg - it[5]) for lg, it in zip(lgs, items)]
            dss = [p * (dp - it[6]) for p, dp, it in zip(ps, dps, items)]
            dqs = [_dot(ds, it[3]) for ds, it in zip(dss, items)]
            dks = [_dot_tn(ds, it[7]) for ds, it in zip(dss, items)]
            dvs = [_dot_tn(p, it[8]) for p, it in zip(ps, items)]
            for n, (s, ws) in enumerate(blocks):
                rows = pl.ds(_aligned(s * QBLK, QBLK), QBLK)
                win = pl.ds(ws, KWIN)
                dq_ref[rows, :] = _head_select(lane, dqs[2 * n], dqs[2 * n + 1])
                dk_ref[win, :] += _head_select(lanew, dks[2 * n], dks[2 * n + 1])
                dv_ref[win, :] += _head_select(lanew, dvs[2 * n], dvs[2 * n + 1])
            for hh in range(2):
                tot = dss[hh]
                for n in range(1, len(blocks)):
                    tot = tot + dss[2 * n + hh]
                db_ref[hh, var] += tot

        dk_ref[...] = jnp.zeros_like(dk_ref)
        dv_ref[...] = jnp.zeros_like(dv_ref)
        _band_loop(nsub, length, step)

    imap = lambda hp, r: (0, r * (SWA_W // LANES) + hp)
    blk_in = pl.BlockSpec((length, LANES), imap, pipeline_mode=single)
    blk_out = pl.BlockSpec((length, LANES), imap)
    shp = jax.ShapeDtypeStruct(qv.shape, F32)
    return pl.pallas_call(
        body, name=name, grid=(SWA_W // LANES, dilation),
        in_specs=[blk_in] * 6 + [pl.BlockSpec((2, 3, QBLK, KWIN), lambda hp, r: (hp, 0, 0, 0))],
        out_specs=[blk_out, blk_out, blk_out, pl.BlockSpec((2, 3, QBLK, KWIN), lambda hp, r: (hp, 0, 0, 0))],
        out_shape=[shp, shp, shp, jax.ShapeDtypeStruct((SWA_HEADS, 3, QBLK, KWIN), F32)],
        compiler_params=_params(("arbitrary", "arbitrary"), VMEM_LIMIT),
    )(qv, kv, vv, dov, lv, ddv, bias_a)


def _bias_grad(ds2, idx, tk):
    n = ds2.shape[1]
    nk = n // tk

    def body(a_ref, i_ref, o_ref):
        @pl.when(pl.program_id(0) == 0)
        def _():
            o_ref[...] = jnp.zeros_like(o_ref)

        oh = _onehot(i_ref[...], BF16)
        rest = a_ref[...]
        acc = jnp.zeros((SWA_HEADS, REL_BUCKETS), F32)
        for _ in range(3):
            piece = rest.astype(BF16)
            acc = acc + _dot_nt(piece, oh)
            rest = rest - piece.astype(F32)
        o_ref[...] += acc

    return pl.pallas_call(
        body, name="bias_grad", grid=(nk,),
        in_specs=[pl.BlockSpec((SWA_HEADS, tk), lambda k: (0, k)), pl.BlockSpec((1, tk), lambda k: (0, k))],
        out_specs=pl.BlockSpec((SWA_HEADS, REL_BUCKETS), lambda k: (0, 0)),
        out_shape=jax.ShapeDtypeStruct((SWA_HEADS, REL_BUCKETS), F32),
        compiler_params=_params(("arbitrary",), VMEM_LIMIT),
    )(ds2, idx)


def _swa_branch_fwd(qkvb, qw_t, kw_t, rel_bias, bd, tm):
    qkv = _swa_prep_fwd(qkvb, qw_t, kw_t, bd, tm)
    tables = _bias_tables(rel_bias, _band_index(), BIAS_TILE)
    os_, ls_, tabs = [], [], []
    for n, (_, d) in enumerate(PATTERNS):
        bias = tables[:, n * BAND_CELLS:(n + 1) * BAND_CELLS].reshape(SWA_HEADS, len(WIN_OFFSETS), QBLK, KWIN)
        o_p, l_p = _swa_fwd(*qkv[3 * n:3 * n + 3], bias, d, f"swa_fwd_d{d}")
        os_.append(o_p)
        ls_.append(l_p)
        tabs.append(bias)
    o, o16, *lses = _swa_combine(os_, ls_, tm)
    return o, o16, (qkv, lses, tabs)


def _swa_branch_bwd(do, o, saved, qkvb, qw_t, kw_t, bd, tm):
    qkv, lses, tabs = saved
    prep = _swa_bwd_prep(do, o, bd, tm)
    grads, dss = [], []
    for n, ((_, d), bias) in enumerate(zip(PATTERNS, tabs)):
        dq, dk, dv, ds = _swa_bwd(*qkv[3 * n:3 * n + 3], prep[3 + n], lses[n], prep[n], bias, d, f"swa_bwd_d{d}")
        grads += [dq, dk, dv]
        dss.append(ds.reshape(SWA_HEADS, -1))
    dqkvb, dqw, dkw = _swa_prep_bwd(qkvb, qw_t, kw_t, bd, grads, tm)
    dbias = _bias_grad(jnp.concatenate(dss, axis=1), _band_index(), BIAS_TILE)
    fold = lambda w: jnp.sum(w.reshape(SWA_HEADS, SWA_DIM), axis=0)
    return dqkvb, fold(dqw), fold(dkw), dbias.T


def _mesh_pos():
    return lax.axis_index("x"), lax.axis_index("y"), lax.axis_index("c")


def _other_chips(x, y):
    return [(1 - x, y), (x, 1 - y), (1 - x, 1 - y)]


def _remote(src, dst, send_sem, recv_sem, device):
    return pltpu.make_async_remote_copy(src_ref=src, dst_ref=dst, send_sem=send_sem, recv_sem=recv_sem,
                                        device_id=device, device_id_type=MESH)


def _split_axis(shape2):
    return 0 if (shape2[0] // 2) % 16 == 0 else 1


def _half_index(shape2, c):
    axis = _split_axis(shape2)
    h = shape2[axis] // 2
    return (pl.ds(c * h, h), slice(None)) if axis == 0 else (slice(None), pl.ds(c * h, h))


def _all_gather(xs):
    n = len(xs)

    def body(*refs):
        ins, outs = refs[:n], refs[n:2 * n]
        send_sems, recv_sems = refs[2 * n:]
        x, y, c = _mesh_pos()
        me = 2 * x + y
        chips = _other_chips(x, y)
        halves = []
        sends = []
        for a in range(n):
            h = ins[a].shape[0] // 2
            mine, other = pl.ds(c * h, h), pl.ds((1 - c) * h, h)
            halves.append((mine, other))
            own = _remote(ins[a], outs[a].at[me], send_sems.at[a, 6], recv_sems.at[a, 6], (x, y, 1 - c))
            own.start()
            sends.append(own)
            for j, chip in enumerate(chips):
                cp = _remote(ins[a].at[mine], outs[a].at[me, mine], send_sems.at[a, j], recv_sems.at[a, j], (*chip, c))
                cp.start()
                sends.append(cp)
        for a in range(n):
            mine, _ = halves[a]
            for j, chip in enumerate(chips):
                src = 2 * chip[0] + chip[1]
                landed = outs[a].at[src, mine]
                _remote(landed, landed, send_sems.at[a, j], recv_sems.at[a, j], (x, y, c)).wait_recv()
                fwd = _remote(landed, landed, send_sems.at[a, 3 + j], recv_sems.at[a, 3 + j], (x, y, 1 - c))
                fwd.start()
                sends.append(fwd)
        for a in range(n):
            _, other = halves[a]
            for j, chip in enumerate(chips):
                src = 2 * chip[0] + chip[1]
                landed = outs[a].at[src, other]
                _remote(landed, landed, send_sems.at[a, 3 + j], recv_sems.at[a, 3 + j], (x, y, c)).wait_recv()
            mine_slot = outs[a].at[me]
            _remote(mine_slot, mine_slot, send_sems.at[a, 6], recv_sems.at[a, 6], (x, y, c)).wait_recv()
        for cp in sends:
            cp.wait_send()

    return list(pl.pallas_call(
        body, name="all_gather_weights",
        in_specs=[ANY] * n, out_specs=[ANY] * n,
        out_shape=[jax.ShapeDtypeStruct((N_SHARDS,) + a.shape, a.dtype) for a in xs],
        scratch_shapes=[pltpu.SemaphoreType.DMA((n, 7)), pltpu.SemaphoreType.DMA((n, 7))],
    )(*xs))


def _rs_pair(gs):
    n = len(gs)

    def body(*refs):
        ins, lands = refs[:n], refs[n:2 * n]
        send_sems, recv_sems = refs[2 * n:]
        x, y, c = _mesh_pos()
        cps = []
        for a in range(n):
            theirs = (slice(None),) + _half_index(ins[a].shape[1:], 1 - c)
            cp = _remote(ins[a].at[theirs], lands[a], send_sems.at[a], recv_sems.at[a], (x, y, 1 - c))
            cp.start()
            cps.append(cp)
        for cp in cps:
            cp.wait()

    def half_shape(g):
        dims = list(g.shape)
        dims[1 + _split_axis(g.shape[1:])] //= 2
        return tuple(dims)

    return list(pl.pallas_call(
        body, name="rs_pair", in_specs=[ANY] * n, out_specs=[ANY] * n,
        out_shape=[jax.ShapeDtypeStruct(half_shape(g), g.dtype) for g in gs],
        scratch_shapes=[pltpu.SemaphoreType.DMA((n,)), pltpu.SemaphoreType.DMA((n,))],
    )(*gs))


def _pair_exchange(gs):
    def copies(cin, cout, send_sems, recv_sems):
        x, y, c = _mesh_pos()
        return [_remote(g.at[(slice(None),) + _half_index(g.shape[1:], 1 - c)], land, send_sems.at[a, 0],
                        recv_sems.at[a, 0], (x, y, 1 - c)) for a, (g, land) in enumerate(zip(cin, cout))]

    def start(*refs):
        for cp in copies(*refs):
            cp.start()

    def finish(*refs):
        for cp in copies(*refs):
            cp.wait()

    def half_shape(g):
        dims = list(g.shape)
        dims[1 + _split_axis(g.shape[1:])] //= 2
        return tuple(dims)

    return _Exchange(tuple(gs), tuple(jax.ShapeDtypeStruct(half_shape(g), g.dtype) for g in gs), start, finish)


def _rs_chips(ss):
    n = len(ss)

    def body(*refs):
        ins, outs = refs[:n], refs[n:2 * n]
        send_sems, recv_sems = refs[2 * n:]
        x, y, c = _mesh_pos()
        me = 2 * x + y
        chips = _other_chips(x, y)
        cps = []
        for a in range(n):
            for j, chip in enumerate(chips):
                dst_chip = 2 * chip[0] + chip[1]
                cp = _remote(ins[a].at[dst_chip], outs[a].at[me], send_sems.at[a, j], recv_sems.at[a, j], (*chip, c))
                cp.start()
                cps.append(cp)
        for a in range(n):
            for j, chip in enumerate(chips):
                src = 2 * chip[0] + chip[1]
                _remote(outs[a].at[src], outs[a].at[src], send_sems.at[a, j], recv_sems.at[a, j], (x, y, c)).wait_recv()
        for cp in cps:
            cp.wait_send()

    return list(pl.pallas_call(
        body, name="rs_chips", in_specs=[ANY] * n, out_specs=[ANY] * n,
        out_shape=[jax.ShapeDtypeStruct(s.shape, s.dtype) for s in ss],
        scratch_shapes=[pltpu.SemaphoreType.DMA((n, 3)), pltpu.SemaphoreType.DMA((n, 3))],
    )(*ss))


def _rs_join(fs, axes):
    n = len(fs)

    def whole(f, axis):
        dims = list(f.shape)
        dims[axis] *= 2
        return tuple(dims)

    def body(*refs):
        ins, outs = refs[:n], refs[n:2 * n]
        send_sems, recv_sems = refs[2 * n:]
        x, y, c = _mesh_pos()
        cps = []
        for a in range(n):
            h = ins[a].shape[axes[a]]
            mine = (pl.ds(c * h, h), slice(None)) if axes[a] == 0 else (slice(None), pl.ds(c * h, h))
            cp = _remote(ins[a], outs[a].at[mine], send_sems.at[a], recv_sems.at[a], (x, y, 1 - c))
            cp.start()
            cps.append(cp)
        for cp in cps:
            cp.wait()

    outs = pl.pallas_call(
        body, name="rs_join", in_specs=[ANY] * n, out_specs=[ANY] * n,
        out_shape=[jax.ShapeDtypeStruct(whole(f, ax), f.dtype) for f, ax in zip(fs, axes)],
        scratch_shapes=[pltpu.SemaphoreType.DMA((n,)), pltpu.SemaphoreType.DMA((n,))],
    )(*fs)
    c = lax.axis_index("c")
    return [lax.dynamic_update_slice_in_dim(o, f, c * f.shape[ax], ax) for o, f, ax in zip(outs, fs, axes)]


def _gather_exchange(xs):
    def start(cin, cout, send_sems, recv_sems):
        x, y, c = _mesh_pos()
        me = 2 * x + y
        for a, (src, dst) in enumerate(zip(cin, cout)):
            mine = _half_index(src.shape, c)
            for j, chip in enumerate(_other_chips(x, y)):
                _remote(src.at[mine], dst.at[(me,) + mine], send_sems.at[a, j], recv_sems.at[a, j], (*chip, c)).start()
            _remote(src, dst.at[me], send_sems.at[a, 3], recv_sems.at[a, 3], (x, y, 1 - c)).start()

    def finish(cin, cout, send_sems, recv_sems):
        x, y, c = _mesh_pos()
        for a, dst in enumerate(cout):
            for j, chip in enumerate(_other_chips(x, y)):
                landed = dst.at[(2 * chip[0] + chip[1],) + _half_index(dst.shape[1:], c)]
                _remote(landed, landed, send_sems.at[a, j], recv_sems.at[a, j], (x, y, c)).wait()
            own = dst.at[2 * x + y]
            _remote(own, own, send_sems.at[a, 3], recv_sems.at[a, 3], (x, y, c)).wait()

    return _Exchange(tuple(xs), tuple(jax.ShapeDtypeStruct((N_SHARDS,) + a.shape, a.dtype) for a in xs), start, finish)


def _gather_forward(gs):
    n = len(gs)

    def body(*refs):
        outs = refs[n:2 * n]
        send_sems, recv_sems = refs[2 * n:]
        x, y, c = _mesh_pos()
        chips = _other_chips(x, y)
        cps = []
        for a in range(n):
            for j, chip in enumerate(chips):
                landed = outs[a].at[(2 * chip[0] + chip[1],) + _half_index(outs[a].shape[1:], c)]
                cp = _remote(landed, landed, send_sems.at[a, j], recv_sems.at[a, j], (x, y, 1 - c))
                cp.start()
                cps.append(cp)
        for a in range(n):
            for j, chip in enumerate(chips):
                other = outs[a].at[(2 * chip[0] + chip[1],) + _half_index(outs[a].shape[1:], 1 - c)]
                _remote(other, other, send_sems.at[a, j], recv_sems.at[a, j], (x, y, c)).wait_recv()
        for cp in cps:
            cp.wait_send()

    return list(pl.pallas_call(
        body, name="gather_forward", in_specs=[ANY] * n, out_specs=[ANY] * n,
        out_shape=[jax.ShapeDtypeStruct(g.shape, g.dtype) for g in gs],
        input_output_aliases={i: i for i in range(n)},
        scratch_shapes=[pltpu.SemaphoreType.DMA((n, 3)), pltpu.SemaphoreType.DMA((n, 3))],
    )(*gs))


def _scatter_exchange(ss):
    def start(cin, cout, send_sems, recv_sems):
        x, y, c = _mesh_pos()
        me = 2 * x + y
        for a, (src, dst) in enumerate(zip(cin, cout)):
            for j, chip in enumerate(_other_chips(x, y)):
                _remote(src.at[2 * chip[0] + chip[1]], dst.at[me], send_sems.at[a, j], recv_sems.at[a, j],
                        (*chip, c)).start()

    def finish(cin, cout, send_sems, recv_sems):
        x, y, c = _mesh_pos()
        for a, dst in enumerate(cout):
            for j, chip in enumerate(_other_chips(x, y)):
                slot = dst.at[2 * chip[0] + chip[1]]
                _remote(slot, slot, send_sems.at[a, j], recv_sems.at[a, j], (x, y, c)).wait()

    return _Exchange(tuple(ss), tuple(jax.ShapeDtypeStruct(s.shape, s.dtype) for s in ss), start, finish)


def _add_pairs(gs, lands, name):
    n = len(gs)

    def body(*refs):
        c = lax.axis_index("c")
        for g_ref, l_ref, o_ref in zip(refs[:n], refs[n:2 * n], refs[2 * n:]):
            mine = g_ref[(0,) + _half_index(g_ref.shape[1:], c)]
            o_ref[0] = (mine.astype(F32) + l_ref[0].astype(F32)).astype(BF16)

    whole = [pl.BlockSpec((1,) + g.shape[1:], lambda j: (j, 0, 0)) for g in gs]
    half = [pl.BlockSpec((1,) + l.shape[1:], lambda j: (j, 0, 0)) for l in lands]
    return list(pl.pallas_call(body, name=name, grid=(gs[0].shape[0],), in_specs=whole + half, out_specs=half,
                               out_shape=[jax.ShapeDtypeStruct(l.shape, BF16) for l in lands],
                               compiler_params=_params(("arbitrary",), VMEM_LIMIT))(*gs, *lands))


def _sum_slots(slots, owns, name):
    n = len(slots)

    def body(*refs):
        me = 2 * lax.axis_index("x") + lax.axis_index("y")
        for s_ref, o_ref, out_ref in zip(refs[:n], refs[n:2 * n], refs[2 * n:]):
            acc = jnp.zeros(out_ref.shape, F32)
            for s in range(N_SHARDS):
                acc = acc + jnp.where(me == s, o_ref[s], s_ref[s]).astype(F32)
            out_ref[...] = acc

    def specs(a):
        _, h, c = a.shape
        if h % 32 == 0:
            return (pl.BlockSpec((N_SHARDS, h // 2, c), lambda i: (0, i, 0)), pl.BlockSpec((h // 2, c), lambda i: (i, 0)))
        return (pl.BlockSpec((N_SHARDS, h, c // 2), lambda i: (0, 0, i)), pl.BlockSpec((h, c // 2), lambda i: (0, i)))

    in_specs = [specs(a)[0] for a in slots]
    return list(pl.pallas_call(body, name=name, grid=(2,), in_specs=in_specs + in_specs,
                               out_specs=[specs(a)[1] for a in slots],
                               out_shape=[jax.ShapeDtypeStruct(a.shape[1:], F32) for a in slots],
                               compiler_params=_params(("arbitrary",), VMEM_LIMIT))(*slots, *owns))


def _all_reduce_small(p):
    r = p.shape[0]

    def body(p_ref, o_ref, buf, send_sems, recv_sems):
        x, y, c = _mesh_pos()
        me = 4 * x + 2 * y + c
        buf[me] = p_ref[...]
        cps = []
        k = 0
        for fx in range(2):
            for fy in range(2):
                for fc in range(2):
                    if fx + fy + fc == 0:
                        continue
                    peer = (1 - x if fx else x, 1 - y if fy else y, 1 - c if fc else c)
                    peer_id = 4 * peer[0] + 2 * peer[1] + peer[2]
                    cp = _remote(p_ref, buf.at[me], send_sems.at[k], recv_sems.at[k], peer)
                    cp.start()
                    cps.append((cp, peer_id, k))
                    k += 1
        for cp, peer_id, k in cps:
            _remote(p_ref, buf.at[peer_id], send_sems.at[k], recv_sems.at[k], (x, y, c)).wait_recv()
        for cp, _, _ in cps:
            cp.wait_send()
        acc = buf[0]
        for s in range(1, 8):
            acc = acc + buf[s]
        o_ref[...] = acc

    vm = pl.BlockSpec(memory_space=pltpu.VMEM)
    return pl.pallas_call(
        body, name="all_reduce_small", in_specs=[vm], out_specs=vm,
        out_shape=jax.ShapeDtypeStruct(p.shape, F32),
        scratch_shapes=[pltpu.VMEM((8, r, LANES), F32), pltpu.SemaphoreType.DMA((7,)), pltpu.SemaphoreType.DMA((7,))],
    )(p)


def _adamw(w, g, m, v, name):
    r, c = w.shape
    row_tiles = [d for d in range(8, min(r, 256) + 1, 8) if r % d == 0]
    tr, tc = (max(row_tiles), c) if row_tiles else (r, 256 if c % 256 == 0 else c)
    c1 = 1.0 / (1.0 - ADAM_B1 ** ADAM_STEP)
    c2 = 1.0 / (1.0 - ADAM_B2 ** ADAM_STEP)

    def body(w_ref, g_ref, m_ref, v_ref, d_ref, nm_ref, nv_ref):
        gv = g_ref[...]
        nm = ADAM_B1 * m_ref[...] + (1.0 - ADAM_B1) * gv
        nv = ADAM_B2 * v_ref[...] + (1.0 - ADAM_B2) * (gv * gv)
        d_ref[...] = -ADAM_LR * ((nm * c1) / (jnp.sqrt(nv * c2) + ADAM_EPS) + ADAM_WD * w_ref[...])
        nm_ref[...] = nm
        nv_ref[...] = nv

    blk = pl.BlockSpec((tr, tc), lambda i, j: (i, j))
    shp = jax.ShapeDtypeStruct((r, c), F32)
    return pl.pallas_call(body, name=name, grid=(r // tr, c // tc), in_specs=[blk] * 4, out_specs=[blk] * 3,
                          out_shape=[shp, shp, shp],
                          compiler_params=_params(("arbitrary", "arbitrary"), VMEM_LIMIT))(w, g, m, v)


PACK_UNIT = 8 * LANES


def _pack(arrs):
    parts = []
    for a in arrs:
        f = a.reshape(-1).astype(F32)
        parts.append(jnp.pad(f, (0, (-f.shape[0]) % PACK_UNIT)).reshape(-1, LANES))
    return jnp.concatenate(parts, axis=0)


def _unpack(m, shapes):
    outs, row = [], 0
    for s in shapes:
        n = int(np.prod(s))
        rows = -(-n // PACK_UNIT) * 8
        outs.append(m[row:row + rows].reshape(-1)[:n].reshape(s))
        row += rows
    return outs


WEIGHTS = ["ffn1_norm", "ffn1_w_gate", "ffn1_w_up", "ffn1_w_down", "mix_norm", "w_in", "conv_w", "a_log", "dt_bias",
           "gdn_norm_w", "q_norm_w", "k_norm_w", "rel_bias", "w_out", "ffn2_norm", "ffn2_w_gate", "ffn2_w_up",
           "ffn2_w_down", "final_norm"]
BIG = ["ffn1_w_gate", "ffn1_w_up", "ffn1_w_down", "w_in", "w_out", "ffn2_w_gate", "ffn2_w_up", "ffn2_w_down"]
SMALL = [n for n in WEIGHTS if n not in BIG]
COL_SHARDED = ["ffn1_w_gate", "ffn1_w_up", "w_in", "ffn2_w_gate", "ffn2_w_up"]
N_IN_COLS = 3600
TM = 256
TE = 512
TK = 2048


def kernel(x, ffn1_norm, ffn1_w_gate, ffn1_w_up, ffn1_w_down, mix_norm, w_in, conv_w, a_log, dt_bias, gdn_norm_w, q_norm_w, k_norm_w, rel_bias, w_out, ffn2_norm, ffn2_w_gate, ffn2_w_up, ffn2_w_down, final_norm, loss_target, m_ffn1_norm, m_ffn1_w_gate, m_ffn1_w_up, m_ffn1_w_down, m_mix_norm, m_w_in, m_conv_w, m_a_log, m_dt_bias, m_gdn_norm_w, m_q_norm_w, m_k_norm_w, m_rel_bias, m_w_out, m_ffn2_norm, m_ffn2_w_gate, m_ffn2_w_up, m_ffn2_w_down, m_final_norm, v_ffn1_norm, v_ffn1_w_gate, v_ffn1_w_up, v_ffn1_w_down, v_mix_norm, v_w_in, v_conv_w, v_a_log, v_dt_bias, v_gdn_norm_w, v_q_norm_w, v_k_norm_w, v_rel_bias, v_w_out, v_ffn2_norm, v_ffn2_w_gate, v_ffn2_w_up, v_ffn2_w_down, v_final_norm):
    p = dict(locals())
    xs, target = x[0], loss_target[0]
    t, d = xs.shape
    nc = t // CHUNK
    tk = min(TK, t)
    me = 2 * lax.axis_index("x") + lax.axis_index("y")

    first = ["ffn1_w_gate", "ffn1_w_up", "ffn1_w_down"]
    later = [n for n in BIG if n not in first] + ["conv_w"]
    local = lambda n, a: a[0].T if n in COL_SHARDED else a[0]
    shards = {n: local(n, p[n]).astype(BF16) for n in BIG}
    shards["conv_w"] = conv_w[0]
    gw = dict(zip(first, _all_gather([shards[n] for n in first])))
    f1 = (gw["ffn1_w_gate"], gw["ffn1_w_up"], gw["ffn1_w_down"])
    (x1, xn1, g1, u1), landed = _ffn_fwd(xs, ffn1_norm, *f1, TM, "ffn1_fwd",
                                         exchange=_gather_exchange([shards[n] for n in later]))
    gw.update(zip(later, _gather_forward(landed)))
    wp = gw["w_in"].reshape(N_IN_COLS, d)
    w_out_full = gw["w_out"].reshape(d, d)
    conv_rows = conv_w.shape[1]
    cw = jnp.pad(gw["conv_w"].reshape(N_SHARDS * conv_rows, CONV_TAPS).T, ((0, 8 - CONV_TAPS), (0, 0)))
    gp = jnp.pad(jnp.stack([a_log.reshape(8), dt_bias.reshape(8)]), ((0, 6), (0, LANES - 8)))
    gdn_w = gdn_norm_w.reshape(1, GDN_DIM)
    qw_t = jnp.tile(q_norm_w.reshape(1, SWA_DIM), (1, SWA_HEADS))
    kw_t = jnp.tile(k_norm_w.reshape(1, SWA_DIM), (1, SWA_HEADS))
    bd = jnp.asarray(np.kron(np.eye(2), np.full((SWA_DIM, SWA_DIM), 1.0 / SWA_DIM)), BF16)
    f2 = (gw["ffn2_w_gate"], gw["ffn2_w_up"], gw["ffn2_w_down"])

    hn, qkva, z, ab, qkvb = _mix_in_fwd(x1, mix_norm, wp, TE)
    qkvc, gb = _gdn_prep_fwd(qkva, cw, ab, gp, TM)
    gbt = jnp.transpose(gb[:, :16].reshape(nc, CHUNK, 16), (0, 2, 1))
    o_f, o_b, gdn_saved = _gdn_fwd(qkvc, gb, gbt)
    oa = _gdn_post_fwd(o_f, o_b, z, gdn_w, TE)
    o_swa, o_swa16, swa_saved = _swa_branch_fwd(qkvb, qw_t, kw_t, rel_bias, bd, TE)
    x2 = _mix_out_fwd(x1, oa, o_swa, w_out_full, TE)
    (dx3, xn2, g2, u2, loss_part, d_final), _ = _ffn_fwd(x2, ffn2_norm, *f2, TM, "ffn2_fwd", head=(final_norm, target))

    def pair_sums(partials, tag):
        return _add_pairs(partials, _rs_pair(partials), f"rs_add_{tag}")

    (dx2, dyh2, dg2, du2, h2, d_nw2), _ = _ffn_bwd_dx(dx3, x2, ffn2_norm, g2, u2, *f2, TM, "ffn2_bwd_dx")
    dwg2 = _matmul_tn(dg2, xn2, tk, "ffn2_dwg")
    dwu2 = _matmul_tn(du2, xn2, tk, "ffn2_dwu")
    dwd2 = _matmul_tn(h2, dyh2, tk, "ffn2_dwd")
    (doa, dob, dx2b), lands_f2 = _mix_out_bwd(dx2, w_out_full, TE, exchange=_pair_exchange([dwg2, dwu2, dwd2]))
    sums_f2 = _add_pairs([dwg2, dwu2, dwd2], lands_f2, "rs_add_a")
    dwo = jnp.concatenate([_matmul_tn(oa, dx2b, tk, "w_out_dw_a")[0], _matmul_tn(o_swa16, dx2b, tk, "w_out_dw_b")[0]],
                          axis=0).reshape(N_SHARDS, d // N_SHARDS, d)
    do_g, dz, d_gdnw = _gdn_post_bwd(doa, o_f, o_b, z, gdn_w, TE)
    (dqkvc, dgates), slots_f2 = _gdn_bwd(qkvc, gb, gbt, do_g, gdn_saved, exchange=_scatter_exchange(sums_f2))
    dqkva, dab, dcw, dgp = _gdn_prep_bwd(qkva, cw, ab, gp, dqkvc, dgates, TM)
    dqkvb, d_qw, d_kw, d_rel = _swa_branch_bwd(dob, o_swa, swa_saved, qkvb, qw_t, kw_t, bd, TE)
    dpieces = (dqkva, dz, dab, dqkvb)
    dwp = [_matmul_tn(dp, hn, tk, f"w_in_dw_{i}")[0] for i, dp in enumerate(dpieces)]
    dw_in = jnp.concatenate([dwp[0], dwp[1], dwp[2][:N_GATE_COLS], dwp[3]], axis=0)
    dw_in = dw_in.reshape(N_SHARDS, N_IN_COLS // N_SHARDS, d)
    sums_mix = pair_sums([dw_in, dwo], "b")
    (dx1, d_mixnw), slots_mix = _mix_in_bwd_dx(dx2, x1, mix_norm, dpieces, wp, TE, exchange=_scatter_exchange(sums_mix))
    (gx, dyh1, dg1, du1, h1, d_nw1), _ = _ffn_bwd_dx(dx1, xs, ffn1_norm, g1, u1, *f1, TM, "ffn1_bwd_dx")
    dwg1 = _matmul_tn(dg1, xn1, tk, "ffn1_dwg")
    dwu1 = _matmul_tn(du1, xn1, tk, "ffn1_dwu")
    sums_gu = pair_sums([dwg1, dwu1], "c")
    dwd1, slots_gu = _matmul_tn(h1, dyh1, tk, "ffn1_dwd", exchange=_scatter_exchange(sums_gu))
    sums_d = pair_sums([dwd1], "d")
    slots = slots_gu + _rs_chips(sums_d) + slots_mix + slots_f2
    sums = sums_gu + sums_d + sums_mix + sums_f2
    halves = _sum_slots(slots[:4], sums[:4], "rs_sum_a") + _sum_slots(slots[4:], sums[4:], "rs_sum_b")
    g_big = dict(zip(BIG, _rs_join(halves, [_split_axis(shards[n].shape) for n in BIG])))

    small_partial = {"ffn1_norm": d_nw1, "mix_norm": d_mixnw, "a_log": dgp[0, 0:8], "dt_bias": dgp[1, 0:8],
                     "gdn_norm_w": d_gdnw, "q_norm_w": d_qw, "k_norm_w": d_kw, "rel_bias": d_rel,
                     "ffn2_norm": d_nw2, "final_norm": d_final, "conv_w": dcw[0:CONV_TAPS].T}
    red = _all_reduce_small(_pack([small_partial[n] for n in SMALL] + [loss_part[0, 0:1]]))
    full_shapes = [p[n].shape if n != "conv_w" else (N_SHARDS * conv_rows, CONV_TAPS) for n in SMALL]
    red_parts = _unpack(red, full_shapes + [(1,)])
    loss = red_parts[-1].reshape(())
    g_small = dict(zip(SMALL, red_parts[:-1]))
    g_small["conv_w"] = lax.dynamic_slice_in_dim(g_small["conv_w"], me * conv_rows, conv_rows, 0).reshape(conv_w.shape)

    grads, deltas, new_m, new_v = {}, {}, {}, {}
    for n in BIG:
        back = (lambda a: a.T[None]) if n in COL_SHARDED else (lambda a: a[None])
        grads[n] = back(g_big[n])
        dl, nm, nv = _adamw(local(n, p[n]), g_big[n], local(n, p["m_" + n]), local(n, p["v_" + n]), "adamw_" + n)
        deltas[n], new_m[n], new_v[n] = back(dl), back(nm), back(nv)
    packed = [_pack([src[n] for n in SMALL]) for src in
              ({n: p[n] for n in SMALL}, g_small, {n: p["m_" + n] for n in SMALL}, {n: p["v_" + n] for n in SMALL})]
    small_shapes = [p[n].shape for n in SMALL]
    for dst, arr in zip((deltas, new_m, new_v), _adamw(*packed, "adamw_small")):
        dst.update(zip(SMALL, _unpack(arr, small_shapes)))
    grads.update(g_small)

    return (loss, gx[None], *[grads[n] for n in WEIGHTS], *[deltas[n] for n in WEIGHTS],
            *[new_m[n] for n in WEIGHTS], *[new_v[n] for n in WEIGHTS])
```

```python
import math
from typing import Callable, NamedTuple

import numpy as np
import jax
import jax.numpy as jnp
from jax import lax
from jax.experimental import pallas as pl
from jax.experimental.pallas import tpu as pltpu

F32 = jnp.float32
BF16 = jnp.bfloat16
MESH = pl.DeviceIdType.MESH

EPS = 1e-6
NEG_BIG = -1e30
GDN_HEADS = 4
GDN_DIM = 128
CHUNK = 64
SWA_HEADS = 8
SWA_DIM = 64
PATTERNS = ((128, 1), (512, 4), (2048, 16))
RADIUS = 64
REL_BUCKETS = 32
REL_MAX_DISTANCE = 1024
CONV_TAPS = 5
N_SHARDS = 4
LANES = 128
VMEM_LIMIT = 56 * 1024 * 1024

ADAM_LR, ADAM_B1, ADAM_B2, ADAM_EPS, ADAM_WD, ADAM_STEP = 0.001, 0.9, 0.999, 1e-08, 0.01, 10


def _params(sem=None, vmem=None):
    return pltpu.CompilerParams(dimension_semantics=sem, vmem_limit_bytes=vmem)


def _resident(shape):
    nd = len(shape)
    return pl.BlockSpec(shape, lambda *_: (0,) * nd, pipeline_mode=pl.Buffered(1))


ANY = pl.BlockSpec(memory_space=pl.ANY)


class _Exchange(NamedTuple):
    arrays: tuple
    out_shape: tuple
    start: Callable
    finish: Callable


def _grid_call(body, name, nsteps, in_specs, out_specs, out_shape, operands, scratch=(), exchange=None):
    params = _params(("arbitrary",), VMEM_LIMIT)
    if exchange is None:
        res = pl.pallas_call(body, name=name, grid=(nsteps,), in_specs=list(in_specs), out_specs=list(out_specs),
                             out_shape=list(out_shape), scratch_shapes=list(scratch), compiler_params=params)(*operands)
        return list(res), []
    n_in, n_out, k, n_scr = len(in_specs), len(out_specs), len(exchange.arrays), len(scratch)

    def wrapped(*refs):
        ins, cin = refs[:n_in], refs[n_in:n_in + k]
        outs, cout = refs[n_in + k:n_in + k + n_out], refs[n_in + k + n_out:n_in + 2 * k + n_out]
        rest = refs[n_in + 2 * k + n_out:]
        scr, (send_sems, recv_sems) = rest[:n_scr], rest[n_scr:]

        @pl.when(pl.program_id(0) == 0)
        def _():
            exchange.start(cin, cout, send_sems, recv_sems)

        body(*ins, *outs, *scr)

        @pl.when(pl.program_id(0) == nsteps - 1)
        def _():
            exchange.finish(cin, cout, send_sems, recv_sems)

    res = pl.pallas_call(
        wrapped, name=name, grid=(nsteps,), in_specs=list(in_specs) + [ANY] * k, out_specs=list(out_specs) + [ANY] * k,
        out_shape=list(out_shape) + list(exchange.out_shape),
        scratch_shapes=list(scratch) + [pltpu.SemaphoreType.DMA((k, 4)), pltpu.SemaphoreType.DMA((k, 4))],
        compiler_params=params)(*operands, *exchange.arrays)
    return list(res[:n_out]), list(res[n_out:])


def _dot(a, b):
    return jnp.dot(a.astype(BF16), b.astype(BF16), preferred_element_type=F32)


def _dot_nt(a, b):
    return lax.dot_general(a.astype(BF16), b.astype(BF16), (((1,), (1,)), ((), ())), preferred_element_type=F32)


def _dot_tn(a, b):
    return lax.dot_general(a.astype(BF16), b.astype(BF16), (((0,), (0,)), ((), ())), preferred_element_type=F32)


def _sigmoid(x):
    return 1.0 / (1.0 + jnp.exp(-x))


def _rstd(xf):
    return lax.rsqrt(jnp.mean(xf * xf, axis=-1, keepdims=True) + EPS)


def _rms_bwd(xf, r, nw, dxn):
    xhat = xf * r
    dxh = dxn * nw
    dx = r * (dxh - xhat * jnp.mean(dxh * xhat, axis=-1, keepdims=True))
    return dx, jnp.sum(dxn * xhat, axis=0, keepdims=True)


def _ffn_fwd(x, nw, wg, wu, wd, tm, name, exchange=None, head=None):
    t, d = x.shape
    nj, fs, _ = wg.shape

    def body(x_ref, nw_ref, wg_ref, wu_ref, wd_ref, *rest):
        if head is None:
            y_ref, xn_ref, g_ref, u_ref = rest
        else:
            fw_ref, t_ref, y_ref, xn_ref, g_ref, u_ref, loss_ref, dfw_ref = rest

            @pl.when(pl.program_id(0) == 0)
            def _():
                loss_ref[...] = jnp.zeros_like(loss_ref)
                dfw_ref[...] = jnp.zeros_like(dfw_ref)

        xf = x_ref[...]
        xn = (xf * _rstd(xf) * nw_ref[...]).astype(BF16)
        xn_ref[...] = xn
        acc = jnp.zeros((tm, d), F32)
        for j in range(nj):
            g = _dot_nt(xn, wg_ref[j])
            u = _dot_nt(xn, wu_ref[j])
            h = (g * _sigmoid(g) * u).astype(BF16)
            acc = acc + jnp.dot(h, wd_ref[j], preferred_element_type=F32)
            g_ref[j] = g.astype(BF16)
            u_ref[j] = u.astype(BF16)
        y = xf + 0.5 * acc
        if head is None:
            y_ref[...] = y
        else:
            r = _rstd(y)
            err = y * r * fw_ref[...] - t_ref[...]
            loss_ref[...] += 0.5 * jnp.sum(jnp.mean(err * err, axis=-1, keepdims=True), axis=0, keepdims=True)
            dy, dfw = _rms_bwd(y, r, fw_ref[...], err * (1.0 / d))
            y_ref[...] = dy
            dfw_ref[...] += dfw

    row = pl.BlockSpec((tm, d), lambda i: (i, 0))
    act = pl.BlockSpec((nj, tm, fs), lambda i: (0, i, 0))
    in_specs = [row, _resident((1, d)), _resident(wg.shape), _resident(wu.shape), _resident(wd.shape)]
    out_specs = [row, row, act, act]
    out_shape = [jax.ShapeDtypeStruct((t, d), F32), jax.ShapeDtypeStruct((t, d), BF16),
                 jax.ShapeDtypeStruct((nj, t, fs), BF16), jax.ShapeDtypeStruct((nj, t, fs), BF16)]
    operands = (x, nw, wg, wu, wd)
    if head is not None:
        in_specs += [_resident((1, d)), row]
        out_specs += [pl.BlockSpec((1, LANES), lambda i: (0, 0)), pl.BlockSpec((1, d), lambda i: (0, 0))]
        out_shape += [jax.ShapeDtypeStruct((1, LANES), F32), jax.ShapeDtypeStruct((1, d), F32)]
        operands += tuple(head)
    return _grid_call(body, name, t // tm, in_specs, out_specs, out_shape, operands, exchange=exchange)


def _ffn_bwd_dx(dy, x, nw, g, u, wg, wu, wd, tm, name, exchange=None):
    t, d = x.shape
    nj, fs, _ = wg.shape

    def body(dy_ref, x_ref, nw_ref, g_ref, u_ref, wg_ref, wu_ref, wd_ref,
             dx_ref, dyh_ref, dg_ref, du_ref, h_ref, dnw_ref):
        @pl.when(pl.program_id(0) == 0)
        def _():
            dnw_ref[...] = jnp.zeros_like(dnw_ref)

        dyv = dy_ref[...]
        dyh = (0.5 * dyv).astype(BF16)
        dyh_ref[...] = dyh
        dxn = jnp.zeros((tm, d), F32)
        dh_next = _dot_nt(dyh, wd_ref[0])
        for j in range(nj):
            dh = dh_next
            gv = g_ref[j].astype(F32)
            uv = u_ref[j].astype(F32)
            sg = _sigmoid(gv)
            si = gv * sg
            dg = (dh * uv * (sg * (1.0 + gv * (1.0 - sg)))).astype(BF16)
            du = (dh * si).astype(BF16)
            if j + 1 < nj:
                dh_next = _dot_nt(dyh, wd_ref[j + 1])
            h_ref[j] = (si * uv).astype(BF16)
            dg_ref[j] = dg
            du_ref[j] = du
            dxn = dxn + _dot(dg, wg_ref[j]) + _dot(du, wu_ref[j])
        xf = x_ref[...]
        dxr, dnw = _rms_bwd(xf, _rstd(xf), nw_ref[...], dxn)
        dx_ref[...] = dyv + dxr
        dnw_ref[...] += dnw

    row = pl.BlockSpec((tm, d), lambda i: (i, 0))
    act = pl.BlockSpec((nj, tm, fs), lambda i: (0, i, 0))
    act_shape = jax.ShapeDtypeStruct((nj, t, fs), BF16)
    return _grid_call(
        body, name, t // tm,
        [row, row, _resident((1, d)), act, act, _resident(wg.shape), _resident(wu.shape), _resident(wd.shape)],
        [row, row, act, act, act, pl.BlockSpec((1, d), lambda i: (0, 0))],
        [jax.ShapeDtypeStruct((t, d), F32), jax.ShapeDtypeStruct((t, d), BF16),
         act_shape, act_shape, act_shape, jax.ShapeDtypeStruct((1, d), F32)],
        (dy, x, nw, g, u, wg, wu, wd), exchange=exchange)


def _matmul_tn(a, b, tk, name, exchange=None):
    a3, b3 = a.ndim == 3, b.ndim == 3
    nj = a.shape[0] if a3 else (b.shape[0] if b3 else 1)
    t, m = a.shape[-2:]
    n = b.shape[-1]
    nt = t // tk

    def body(a_ref, b_ref, o_ref, acc_ref):
        k = pl.program_id(0) % nt

        @pl.when(k == 0)
        def _():
            acc_ref[...] = jnp.zeros_like(acc_ref)

        acc_ref[...] += lax.dot_general(a_ref[...], b_ref[...], (((0,), (0,)), ((), ())),
                                        preferred_element_type=F32)

        @pl.when(k == nt - 1)
        def _():
            o_ref[...] = acc_ref[...].astype(o_ref.dtype)

    a_spec = (pl.BlockSpec((None, tk, m), lambda i: (i // nt, i % nt, 0)) if a3
              else pl.BlockSpec((tk, m), lambda i: (i % nt, 0)))
    b_spec = (pl.BlockSpec((None, tk, n), lambda i: (i // nt, i % nt, 0)) if b3
              else pl.BlockSpec((tk, n), lambda i: (i % nt, 0)))
    (out,), landed = _grid_call(
        body, name, nj * nt, [a_spec, b_spec], [pl.BlockSpec((None, m, n), lambda i: (i // nt, 0, 0))],
        [jax.ShapeDtypeStruct((nj, m, n), BF16)], (a, b), scratch=[pltpu.VMEM((m, n), F32)], exchange=exchange)
    return out if exchange is None else (out, landed)


N_GATE_COLS = 4 * GDN_HEADS
P_QKVA, P_Z, P_AB, P_QKVB = (0, 1536), (1536, 2048), (2048, 2048 + LANES), (2048 + N_GATE_COLS, 3600)
P_PIECES = (P_QKVA, P_Z, P_AB, P_QKVB)


def _mix_in_fwd(x1, nw, wp, tm):
    t, d = x1.shape

    def body(x_ref, nw_ref, w_ref, hn_ref, *outs):
        xf = x_ref[...]
        xn = (xf * _rstd(xf) * nw_ref[...]).astype(BF16)
        hn_ref[...] = xn
        for (a, b), o_ref in zip(P_PIECES, outs):
            o_ref[...] = _dot_nt(xn, w_ref[a:b, :])

    row = pl.BlockSpec((tm, d), lambda i: (i, 0))
    return pl.pallas_call(
        body, name="mix_in_fwd", grid=(t // tm,),
        in_specs=[row, _resident((1, d)), _resident(wp.shape)],
        out_specs=[row] + [pl.BlockSpec((tm, b - a), lambda i: (i, 0)) for a, b in P_PIECES],
        out_shape=[jax.ShapeDtypeStruct((t, d), BF16)]
                  + [jax.ShapeDtypeStruct((t, b - a), F32) for a, b in P_PIECES],
        compiler_params=_params(("arbitrary",), VMEM_LIMIT),
    )(x1, nw, wp)


def _mix_in_bwd_dx(dx, x1, nw, dpieces, wp, tm, exchange=None):
    t, d = x1.shape

    def body(dx_ref, x_ref, nw_ref, p0, p1, p2, p3, w_ref, o_ref, dnw_ref):
        @pl.when(pl.program_id(0) == 0)
        def _():
            dnw_ref[...] = jnp.zeros_like(dnw_ref)

        dh = jnp.zeros((tm, d), F32)
        for (a, b), p_ref in zip(P_PIECES, (p0, p1, p2, p3)):
            dh = dh + _dot(p_ref[...], w_ref[a:b, :])
        xf = x_ref[...]
        dxr, dnw = _rms_bwd(xf, _rstd(xf), nw_ref[...], dh)
        o_ref[...] = dx_ref[...] + dxr
        dnw_ref[...] += dnw

    row = pl.BlockSpec((tm, d), lambda i: (i, 0))
    return _grid_call(
        body, "mix_in_bwd_dx", t // tm,
        [row, row, _resident((1, d))]
        + [pl.BlockSpec((tm, b - a), lambda i: (i, 0)) for a, b in P_PIECES] + [_resident(wp.shape)],
        [row, pl.BlockSpec((1, d), lambda i: (0, 0))],
        [jax.ShapeDtypeStruct((t, d), F32), jax.ShapeDtypeStruct((1, d), F32)],
        (dx, x1, nw, *dpieces, wp), exchange=exchange)


def _mix_out_fwd(x1, oa, ob, w, tm):
    t, d = x1.shape
    half = oa.shape[1]

    def body(x_ref, oa_ref, ob_ref, w_ref, o_ref):
        o_ref[...] = (x_ref[...] + _dot(oa_ref[...], w_ref[0:half, :]) + _dot(ob_ref[...], w_ref[half:2 * half, :]))

    row = pl.BlockSpec((tm, d), lambda i: (i, 0))
    hrow = pl.BlockSpec((tm, half), lambda i: (i, 0))
    return pl.pallas_call(
        body, name="mix_out_fwd", grid=(t // tm,),
        in_specs=[row, hrow, hrow, _resident(w.shape)],
        out_specs=row, out_shape=jax.ShapeDtypeStruct((t, d), F32),
        compiler_params=_params(("arbitrary",), VMEM_LIMIT),
    )(x1, oa, ob, w)


def _mix_out_bwd(dx2, w, tm, exchange=None):
    t, d = dx2.shape
    half = w.shape[0] // 2

    def body(dx_ref, w_ref, doa_ref, dob_ref, dxb_ref):
        dxb = dx_ref[...].astype(BF16)
        dxb_ref[...] = dxb
        doa_ref[...] = _dot_nt(dxb, w_ref[0:half, :])
        dob_ref[...] = _dot_nt(dxb, w_ref[half:2 * half, :])

    row = pl.BlockSpec((tm, d), lambda i: (i, 0))
    hrow = pl.BlockSpec((tm, half), lambda i: (i, 0))
    return _grid_call(
        body, "mix_out_bwd", t // tm, [row, _resident(w.shape)], [hrow, hrow, row],
        [jax.ShapeDtypeStruct((t, half), F32), jax.ShapeDtypeStruct((t, half), F32), jax.ShapeDtypeStruct((t, d), BF16)],
        (dx2, w), exchange=exchange)


HALO = 8


def _halo_row_specs(tr, cols, nrow8):
    per = tr // HALO
    return [pl.BlockSpec((tr, cols), lambda i: (i, 0)),
            pl.BlockSpec((HALO, cols), lambda i: (jnp.maximum(i * per - 1, 0), 0)),
            pl.BlockSpec((HALO, cols), lambda i: (jnp.minimum((i + 1) * per, nrow8 - 1), 0))]


def _fill_window(win_ref, cb, xm, xp, xn, first, last):
    tr = xm.shape[0]
    cols = slice(cb * LANES, (cb + 1) * LANES)
    win_ref[cb, 0:HALO, :] = jnp.where(first, 0.0, xp[:, cols])
    win_ref[cb, HALO:HALO + tr, :] = xm[:, cols]
    win_ref[cb, HALO + tr:HALO + tr + HALO, :] = jnp.where(last, 0.0, xn[:, cols])


def _conv_taps(win_ref, cb, cw_ref, start, rows):
    cols = slice(cb * LANES, (cb + 1) * LANES)
    acc = None
    for j in range(CONV_TAPS):
        term = win_ref[cb, pl.ds(start + j - CONV_TAPS // 2, rows), :] * cw_ref[j:j + 1, cols]
        acc = term if acc is None else acc + term
    return acc


def _softplus(x):
    u = jnp.exp(-jnp.abs(x))
    w = 1.0 + u
    log1p = jnp.where(w == 1.0, u, jnp.log(w) * u / jnp.where(w == 1.0, 1.0, w - 1.0))
    return jnp.maximum(x, 0.0) + log1p


def _gdn_prep_fwd(qkva, cw, ab, gp, tr):
    t, c = qkva.shape
    nt = t // tr
    ncb = c // LANES

    def body(xm, xp, xn, cw_ref, ab_ref, gp_ref, o_ref, gb_ref, xw_ref):
        i = pl.program_id(0)
        first, last = i == 0, i == nt - 1
        for cb in range(ncb):
            cols = slice(cb * LANES, (cb + 1) * LANES)
            _fill_window(xw_ref, cb, xm, xp, xn, first, last)
            pre = _conv_taps(xw_ref, cb, cw_ref, HALO, tr)
            y = pre * _sigmoid(pre)
            if cb < 2 * GDN_HEADS:
                y = y * lax.rsqrt(jnp.sum(y * y, axis=-1, keepdims=True) + EPS)
            if cb < GDN_HEADS:
                y = y * (GDN_DIM ** -0.5)
            o_ref[:, cols] = y
        abv = ab_ref[...]
        lane = lax.broadcasted_iota(jnp.int32, abv.shape, 1)
        g = -jnp.exp(gp_ref[0:1, :]) * _softplus(abv + gp_ref[1:2, :])
        gb_ref[...] = jnp.where(lane < 8, g, jnp.where(lane < 16, _sigmoid(abv), 0.0))

    return pl.pallas_call(
        body, name="gdn_prep_fwd", grid=(nt,),
        in_specs=_halo_row_specs(tr, c, t // HALO)
                 + [_resident(cw.shape), pl.BlockSpec((tr, LANES), lambda i: (i, 0)), _resident(gp.shape)],
        out_specs=[pl.BlockSpec((tr, c), lambda i: (i, 0)), pl.BlockSpec((tr, LANES), lambda i: (i, 0))],
        out_shape=[jax.ShapeDtypeStruct((t, c), F32), jax.ShapeDtypeStruct((t, LANES), F32)],
        scratch_shapes=[pltpu.VMEM((ncb, tr + 2 * HALO, LANES), F32)],
        compiler_params=_params(("arbitrary",), VMEM_LIMIT),
    )(qkva, qkva, qkva, cw, ab, gp)


def _gdn_prep_bwd(qkva, cw, ab, gp, dy, dgates, tr):
    t, c = qkva.shape
    nt = t // tr
    ncb = c // LANES

    ext = HALO // 2
    rows_ext = tr + 2 * ext

    def body(xm, xp, xn, fm, fp, fn, cw_ref, ab_ref, gp_ref, gf_ref, dx_ref, dab_ref, dcw_ref, dgp_ref,
             xw_ref, dyw_ref, dp_ref):
        i = pl.program_id(0)
        first, last = i == 0, i == nt - 1

        @pl.when(first)
        def _():
            dcw_ref[...] = jnp.zeros_like(dcw_ref)
            dgp_ref[...] = jnp.zeros_like(dgp_ref)

        sub8 = lax.broadcasted_iota(jnp.int32, (8, LANES), 0)
        for cb in range(ncb):
            cols = slice(cb * LANES, (cb + 1) * LANES)
            _fill_window(xw_ref, cb, xm, xp, xn, first, last)
            _fill_window(dyw_ref, cb, fm, fp, fn, first, last)
            pre = _conv_taps(xw_ref, cb, cw_ref, HALO - ext, rows_ext)
            dyw = dyw_ref[cb, pl.ds(HALO - ext, rows_ext), :]
            sg = _sigmoid(pre)
            s = pre * sg
            if cb < 2 * GDN_HEADS:
                scale = (GDN_DIM ** -0.5) if cb < GDN_HEADS else 1.0
                r = lax.rsqrt(jnp.sum(s * s, axis=-1, keepdims=True) + EPS)
                dn = dyw * scale
                ds = r * dn - s * (r * r * r) * jnp.sum(dn * s, axis=-1, keepdims=True)
            else:
                ds = dyw
            dp_ref[cb] = ds * (sg * (1.0 + pre * (1.0 - sg)))
            dpre = dp_ref[cb, pl.ds(ext, tr), :]
            dx = None
            dcw = jnp.zeros((8, LANES), F32)
            for j in range(CONV_TAPS):
                off = j - CONV_TAPS // 2
                term = dp_ref[cb, pl.ds(ext - off, tr), :] * cw_ref[j:j + 1, cols]
                dx = term if dx is None else dx + term
                tap = jnp.sum(dpre * xw_ref[cb, pl.ds(HALO + off, tr), :], axis=0, keepdims=True)
                dcw = dcw + jnp.where(sub8 == j, tap, 0.0)
            dx_ref[:, cols] = dx.astype(BF16)
            dcw_ref[:, cols] += dcw

        abv = ab_ref[...]
        dgb = gf_ref[...]
        lane = lax.broadcasted_iota(jnp.int32, abv.shape, 1)
        nea = -jnp.exp(gp_ref[0:1, :])
        xs = abv + gp_ref[1:2, :]
        g = nea * _softplus(xs)
        beta = _sigmoid(abv)
        da = dgb * nea * _sigmoid(xs)
        dab = jnp.where(lane < 8, da, jnp.where(lane < 16, dgb * beta * (1.0 - beta), 0.0))
        dab_ref[...] = dab.astype(BF16)
        keep = lane[0:1, :] < 8
        dalog = jnp.where(keep, jnp.sum(dgb * g, axis=0, keepdims=True), 0.0)
        ddtb = jnp.where(keep, jnp.sum(da, axis=0, keepdims=True), 0.0)
        dgp_ref[...] += jnp.where(sub8 == 0, dalog, 0.0) + jnp.where(sub8 == 1, ddtb, 0.0)

    lrow = pl.BlockSpec((tr, LANES), lambda i: (i, 0))
    halo = _halo_row_specs(tr, c, t // HALO)
    return pl.pallas_call(
        body, name="gdn_prep_bwd", grid=(nt,),
        in_specs=halo + halo + [_resident(cw.shape), lrow, _resident(gp.shape), lrow],
        out_specs=[pl.BlockSpec((tr, c), lambda i: (i, 0)), lrow,
                   pl.BlockSpec(cw.shape, lambda i: (0, 0)), pl.BlockSpec(gp.shape, lambda i: (0, 0))],
        out_shape=[jax.ShapeDtypeStruct((t, c), BF16), jax.ShapeDtypeStruct((t, LANES), BF16),
                   jax.ShapeDtypeStruct(cw.shape, F32), jax.ShapeDtypeStruct(gp.shape, F32)],
        scratch_shapes=[pltpu.VMEM((ncb, tr + 2 * HALO, LANES), F32), pltpu.VMEM((ncb, tr + 2 * HALO, LANES), F32),
                        pltpu.VMEM((ncb, rows_ext, LANES), F32)],
        compiler_params=_params(("arbitrary",), VMEM_LIMIT),
    )(qkva, qkva, qkva, dy, dy, dy, cw, ab, gp, dgates)


def _chunk_masks(lower):
    ii = lax.broadcasted_iota(jnp.int32, (CHUNK, CHUNK), 0)
    jj = lax.broadcasted_iota(jnp.int32, (CHUNK, CHUNK), 1)
    incl = (ii >= jj) if lower else (ii <= jj)
    strict = (ii > jj) if lower else (ii < jj)
    return ii, jj, incl, strict


def _dot3(a, b):
    ah = a.astype(BF16)
    al = (a - ah.astype(F32)).astype(BF16)
    bh = b.astype(BF16)
    bl = (b - bh.astype(F32)).astype(BF16)
    d = lambda u, v: jnp.dot(u, v, preferred_element_type=F32)
    return d(ah, bh) + (d(ah, bl) + d(al, bh))


def _tri_inv_many(lmats, ii, jj):
    m16 = (ii // 16) == (jj // 16)
    m32 = (ii // 32) == (jj // 32)
    eye = jnp.where(ii == jj, 1.0, 0.0)
    l16 = [jnp.where(m16, l, 0.0) for l in lmats]
    p2 = [_dot3(a, a) for a in l16]
    p4 = [_dot3(a, a) for a in p2]
    p8 = [_dot3(a, a) for a in p4]
    xs = [eye - a for a in l16]
    for ps in (p2, p4, p8):
        xs = [x + _dot3(x, p) for x, p in zip(xs, ps)]
    for off in ([jnp.where(m32 & jnp.logical_not(m16), l, 0.0) for l in lmats],
                [jnp.where(m32, 0.0, l) for l in lmats]):
        ys = [_dot3(x, c) for x, c in zip(xs, off)]
        xs = [x - _dot3(y, x) for x, y in zip(xs, ys)]
    return xs


def _col_to_row(col, ii, jj):
    return jnp.sum(jnp.where(ii == jj, col, 0.0), axis=0, keepdims=True)


def _row_to_col(row, ii, jj):
    return jnp.sum(jnp.where(ii == jj, row, 0.0), axis=1, keepdims=True)


def _chain_common(q, k, v, graw_col, graw_row, bcol, masks):
    ii, jj, incl, strict = masks
    inclt = jnp.logical_not(strict)
    gcol = jnp.sum(jnp.where(incl, graw_row, 0.0), axis=1, keepdims=True)
    grow = jnp.sum(jnp.where(inclt, graw_col, 0.0), axis=0, keepdims=True)
    glast = jnp.sum(graw_row, axis=1, keepdims=True)
    decay = jnp.where(incl, jnp.exp(jnp.where(incl, gcol - grow, 0.0)), 0.0)
    kb = k * bcol
    vb = v * bcol
    eg = jnp.exp(gcol)
    ek = jnp.exp(glast - gcol)
    kbg = kb * eg
    amat = _dot_nt(kb, k)
    qk = _dot_nt(q, k)
    return dict(gcol=gcol, glast=glast, decay=decay, kb=kb, vb=vb, eg=eg, ek=ek, kbg=kbg, amat=amat, qk=qk,
                intra=qk * decay, qg=q * eg, kdec=k * ek)


def _gdn_fwd(qkvc, gb, gbt):
    tm, u, w, qg, kd, intra, egl = _gdn_local_fwd(qkvc, gb, gbt)
    o_f, o_b, s_f, s_b, vn_f, vn_b = _gdn_scan_fwd(u, w, qg, kd, intra, egl, qkvc.shape[0])
    return o_f, o_b, dict(tm=tm, w=w, qg=qg, kd=kd, intra=intra, egl=egl, s=(s_f, s_b), vn=(vn_f, vn_b))


N_CHAINS = 2 * GDN_HEADS


LOCAL_CHUNKS = 4


def _load_chains(x_ref, g_ref, gt_ref, cc=0):
    hd = GDN_HEADS * GDN_DIM
    rows = slice(cc * CHUNK, (cc + 1) * CHUNK)
    chains = []
    for d in range(2):
        masks = _chunk_masks(d == 0)
        for h in range(GDN_HEADS):
            ch = d * GDN_HEADS + h
            q = x_ref[rows, h * GDN_DIM:(h + 1) * GDN_DIM]
            k = x_ref[rows, hd + h * GDN_DIM:hd + (h + 1) * GDN_DIM]
            v = x_ref[rows, 2 * hd + h * GDN_DIM:2 * hd + (h + 1) * GDN_DIM]
            bcol = g_ref[rows, 8 + ch:9 + ch]
            cm = _chain_common(q, k, v, g_ref[rows, ch:ch + 1], gt_ref[cc, ch:ch + 1, :], bcol, masks)
            chains.append(dict(cm, q=q, k=k, v=v, bcol=bcol, masks=masks, ch=ch, h=h, cc=cc))
    return chains


def _chain_shape(rows, cols, dtype):
    return lambda nc: jax.ShapeDtypeStruct((nc, N_CHAINS, rows, cols), dtype)


def _gdn_local_fwd(qkvc, gb, gbt):
    t = qkvc.shape[0]
    nc = t // CHUNK
    hd = GDN_HEADS * GDN_DIM

    def body(x_ref, g_ref, gt_ref, t_ref, u_ref, w_ref, qg_ref, kd_ref, in_ref, eg_ref):
        chains = [c for cc in range(LOCAL_CHUNKS) for c in _load_chains(x_ref, g_ref, gt_ref, cc)]
        ii, jj = chains[0]["masks"][0:2]
        tms = _tri_inv_many([jnp.where(c["masks"][3], c["amat"] * c["decay"], 0.0) for c in chains], ii, jj)
        uws = [_dot(tm, jnp.concatenate([c["vb"], c["kbg"]], axis=1)) for tm, c in zip(tms, chains)]
        for c, tm, uw in zip(chains, tms, uws):
            cc, ch = c["cc"], c["ch"]
            t_ref[cc, ch] = tm
            u_ref[cc, ch] = uw[:, :GDN_DIM]
            w_ref[cc, ch] = uw[:, GDN_DIM:].astype(BF16)
            qg_ref[cc, ch] = c["qg"].astype(BF16)
            kd_ref[cc, ch] = c["kdec"].astype(BF16)
            in_ref[cc, ch] = c["intra"].astype(BF16)
            eg_ref[cc, ch:ch + 1, :] = jnp.broadcast_to(jnp.exp(c["glast"]), (1, LANES))

    lc = LOCAL_CHUNKS
    blk = lambda rows, cols: pl.BlockSpec((lc, N_CHAINS, rows, cols), lambda n: (n, 0, 0, 0))
    shapes = [_chain_shape(CHUNK, CHUNK, F32), _chain_shape(CHUNK, GDN_DIM, F32), _chain_shape(CHUNK, GDN_DIM, BF16),
              _chain_shape(CHUNK, GDN_DIM, BF16), _chain_shape(CHUNK, GDN_DIM, BF16), _chain_shape(CHUNK, CHUNK, BF16)]
    return tuple(pl.pallas_call(
        body, name="gdn_local_fwd", grid=(nc // lc,),
        in_specs=[pl.BlockSpec((lc * CHUNK, 3 * hd), lambda n: (n, 0)), pl.BlockSpec((lc * CHUNK, LANES), lambda n: (n, 0)),
                  pl.BlockSpec((lc, 16, CHUNK), lambda n: (n, 0, 0))],
        out_specs=[blk(CHUNK, CHUNK), blk(CHUNK, GDN_DIM), blk(CHUNK, GDN_DIM), blk(CHUNK, GDN_DIM),
                   blk(CHUNK, GDN_DIM), blk(CHUNK, CHUNK), pl.BlockSpec((lc, N_CHAINS, LANES), lambda n: (n, 0, 0))],
        out_shape=[s(nc) for s in shapes] + [jax.ShapeDtypeStruct((nc, N_CHAINS, LANES), F32)],
        compiler_params=_params(("arbitrary",), VMEM_LIMIT),
    )(qkvc, gb, gbt))


SCAN_CHUNKS = 8


def _dir_specs(nc, rev):
    nb = nc // SCAN_CHUNKS

    def spec(d, rows, cols, own=False):
        chunk = (lambda n: n) if (d == 0) != rev else (lambda n: nb - 1 - n)
        blk = 0 if own else d
        if rows is None:
            return pl.BlockSpec((SCAN_CHUNKS, GDN_HEADS if own else N_CHAINS, cols), lambda n: (chunk(n), 0, 0))
        return pl.BlockSpec((SCAN_CHUNKS, GDN_HEADS, rows, cols), lambda n: (chunk(n), blk, 0, 0))

    def rows_spec(d, cols):
        chunk = (lambda n: n) if (d == 0) != rev else (lambda n: nb - 1 - n)
        return pl.BlockSpec((SCAN_CHUNKS * CHUNK, cols), lambda n: (chunk(n), 0))

    def order(d):
        return list(range(SCAN_CHUNKS)) if (d == 0) != rev else list(range(SCAN_CHUNKS - 1, -1, -1))
    return spec, rows_spec, order


def _gdn_scan_fwd(u, w, qg, kd, intra, egl, t):
    nc = t // CHUNK
    hd = GDN_HEADS * GDN_DIM

    def body(*refs):
        ins, outs, state = refs[:12], refs[12:18], refs[18]
        @pl.when(pl.program_id(0) == 0)
        def _():
            state[...] = jnp.zeros_like(state)

        chains = [(d, h) for d in range(2) for h in range(GDN_HEADS)]
        states = [state[ch] for ch in range(N_CHAINS)]
        for step in range(SCAN_CHUNKS):
            at = [order(d)[step] for d in range(2)]
            pick = lambda k, d, h: ins[2 * k + d][at[d], h]
            sbs = [s.astype(BF16) for s in states]
            ws = [_dot(pick(1, d, h), sb) for (d, h), sb in zip(chains, sbs)]
            o1 = [_dot(pick(2, d, h), sb) for (d, h), sb in zip(chains, sbs)]
            vns = [(pick(0, d, h) - wsb).astype(BF16) for (d, h), wsb in zip(chains, ws)]
            o2 = [_dot(pick(4, d, h), vn) for (d, h), vn in zip(chains, vns)]
            kv = [_dot_tn(pick(3, d, h), vn) for (d, h), vn in zip(chains, vns)]
            new_states = []
            for ch, (d, h) in enumerate(chains):
                outs[d][at[d] * CHUNK:(at[d] + 1) * CHUNK, h * GDN_DIM:(h + 1) * GDN_DIM] = o1[ch] + o2[ch]
                outs[2 + d][at[d], h] = states[ch]
                outs[4 + d][at[d], h] = vns[ch]
                new_states.append(states[ch] * ins[10 + d][at[d], ch:ch + 1, :] + kv[ch])
            states = new_states
        for ch in range(N_CHAINS):
            state[ch] = states[ch]

    spec, rows_spec, order = _dir_specs(nc, False)
    pair = lambda rows, cols, own=False: [spec(0, rows, cols, own), spec(1, rows, cols, own)]
    s_shape = jax.ShapeDtypeStruct((nc, GDN_HEADS, GDN_DIM, GDN_DIM), F32)
    vn_shape = jax.ShapeDtypeStruct((nc, GDN_HEADS, CHUNK, GDN_DIM), BF16)
    return pl.pallas_call(
        body, name="gdn_scan_fwd", grid=(nc // SCAN_CHUNKS,),
        in_specs=(pair(CHUNK, GDN_DIM) + pair(CHUNK, GDN_DIM) + pair(CHUNK, GDN_DIM) + pair(CHUNK, GDN_DIM)
                  + pair(CHUNK, CHUNK) + pair(None, LANES)),
        out_specs=([rows_spec(0, hd), rows_spec(1, hd)] + pair(GDN_DIM, GDN_DIM, True)
                   + pair(CHUNK, GDN_DIM, True)),
        out_shape=[jax.ShapeDtypeStruct((t, hd), F32), jax.ShapeDtypeStruct((t, hd), F32),
                   s_shape, s_shape, vn_shape, vn_shape],
        scratch_shapes=[pltpu.VMEM((N_CHAINS, GDN_DIM, GDN_DIM), F32)],
        compiler_params=_params(("arbitrary",), VMEM_LIMIT),
    )(u, u, w, w, qg, qg, kd, kd, intra, intra, egl, egl)


def _gdn_bwd(qkvc, gb, gbt, do, saved, exchange=None):
    scan = _gdn_scan_bwd(do, saved, qkvc.shape[0])
    return _gdn_local_bwd(qkvc, gb, gbt, do, saved, scan, exchange)


def _gdn_scan_bwd(do, saved, t):
    nc = t // CHUNK
    hd = GDN_HEADS * GDN_DIM

    def body(*refs):
        ins, outs, dstate = refs[:16], refs[16:26], refs[26]
        @pl.when(pl.program_id(0) == 0)
        def _():
            dstate[...] = jnp.zeros_like(dstate)

        chains = [(d, h) for d in range(2) for h in range(GDN_HEADS)]
        dss = [dstate[ch] for ch in range(N_CHAINS)]
        for step in range(SCAN_CHUNKS):
            at = [order(d)[step] for d in range(2)]
            pick = lambda k, d, h: ins[2 * k + d][at[d], h]
            dsbs = [ds.astype(BF16) for ds in dss]
            ss = [pick(1, d, h) for d, h in chains]
            sbs = [s.astype(BF16) for s in ss]
            dos = [ins[d][at[d] * CHUNK:(at[d] + 1) * CHUNK, h * GDN_DIM:(h + 1) * GDN_DIM].astype(BF16)
                   for d, h in chains]
            dv1 = [_dot_tn(pick(5, d, h), dov) for (d, h), dov in zip(chains, dos)]
            dv2 = [_dot(pick(4, d, h), dsb) for (d, h), dsb in zip(chains, dsbs)]
            ds1 = [_dot_tn(pick(3, d, h), dov) for (d, h), dov in zip(chains, dos)]
            dkds = [_dot_nt(pick(6, d, h), dsb) for (d, h), dsb in zip(chains, dsbs)]
            dqgs = [_dot_nt(dov, sb) for dov, sb in zip(dos, sbs)]
            dvns = [(a + b).astype(BF16) for a, b in zip(dv1, dv2)]
            ds2 = [_dot_tn(pick(2, d, h), dvn) for (d, h), dvn in zip(chains, dvns)]
            dws = [_dot_nt(dvn, sb) for dvn, sb in zip(dvns, sbs)]
            new_dss = []
            for ch, (d, h) in enumerate(chains):
                egl = ins[14 + d][at[d], ch:ch + 1, :]
                outs[d][at[d], h] = dvns[ch]
                outs[2 + d][at[d], h] = (-dws[ch]).astype(BF16)
                outs[4 + d][at[d], h] = dqgs[ch]
                outs[6 + d][at[d], h] = dkds[ch]
                outs[8 + d][at[d], h:h + 1, :] = egl * jnp.sum(jnp.sum(ss[ch] * dss[ch], axis=1, keepdims=True),
                                                               axis=0, keepdims=True)
                new_dss.append(ds1[ch] + egl * dss[ch] - ds2[ch])
            dss = new_dss
        for ch in range(N_CHAINS):
            dstate[ch] = dss[ch]

    spec, rows_spec, order = _dir_specs(nc, True)
    pair = lambda rows, cols, own=False: [spec(0, rows, cols, own), spec(1, rows, cols, own)]
    s_f, s_b = saved["s"]
    vn_f, vn_b = saved["vn"]
    w, qg, kd, intra, egl = saved["w"], saved["qg"], saved["kd"], saved["intra"], saved["egl"]
    own = lambda rows, cols, dtype: jax.ShapeDtypeStruct((nc, GDN_HEADS, rows, cols), dtype)
    row_shape = jax.ShapeDtypeStruct((nc, GDN_HEADS, LANES), F32)
    return pl.pallas_call(
        body, name="gdn_scan_bwd", grid=(nc // SCAN_CHUNKS,),
        in_specs=([rows_spec(0, hd), rows_spec(1, hd)] + pair(GDN_DIM, GDN_DIM, True) + pair(CHUNK, GDN_DIM)
                  + pair(CHUNK, GDN_DIM) + pair(CHUNK, GDN_DIM) + pair(CHUNK, CHUNK) + pair(CHUNK, GDN_DIM, True)
                  + pair(None, LANES)),
        out_specs=(pair(CHUNK, GDN_DIM, True) + pair(CHUNK, GDN_DIM, True) + pair(CHUNK, GDN_DIM, True)
                   + pair(CHUNK, GDN_DIM, True) + pair(None, LANES, True)),
        out_shape=[own(CHUNK, GDN_DIM, BF16)] * 4 + [own(CHUNK, GDN_DIM, F32)] * 4 + [row_shape] * 2,
        scratch_shapes=[pltpu.VMEM((N_CHAINS, GDN_DIM, GDN_DIM), F32)],
        compiler_params=_params(("arbitrary",), VMEM_LIMIT),
    )(do, do, s_f, s_b, w, w, qg, qg, kd, kd, intra, intra, vn_f, vn_b, egl, egl)


def _dot3_nt(a, b):
    ah = a.astype(BF16)
    al = (a - ah.astype(F32)).astype(BF16)
    bh = b.astype(BF16)
    bl = (b - bh.astype(F32)).astype(BF16)
    return _dot_nt(ah, bh) + (_dot_nt(ah, bl) + _dot_nt(al, bh))


def _dot3_tn(a, b):
    ah = a.astype(BF16)
    al = (a - ah.astype(F32)).astype(BF16)
    bh = b.astype(BF16)
    bl = (b - bh.astype(F32)).astype(BF16)
    return _dot_tn(ah, bh) + (_dot_tn(ah, bl) + _dot_tn(al, bh))


def _gdn_local_bwd(qkvc, gb, gbt, do, saved, scan, exchange=None):
    t = qkvc.shape[0]
    nc = t // CHUNK
    hd = GDN_HEADS * GDN_DIM

    def body(*refs):
        x_ref, g_ref, gt_ref, do_ref, t_ref = refs[:5]
        per_dir = refs[5:17]
        dx_ref, dg_ref = refs[17:]
        chains = [c for cc in range(LOCAL_CHUNKS) for c in _load_chains(x_ref, g_ref, gt_ref, cc)]
        lane = lax.broadcasted_iota(jnp.int32, (CHUNK, LANES), 1)
        dgates = [jnp.zeros((CHUNK, LANES), F32) for _ in range(LOCAL_CHUNKS)]
        for c in chains:
            d = c["ch"] // GDN_HEADS
            vn_ref, dvn_ref, dw_ref, dqg_ref, dkd_ref, dgl_ref = per_dir[d::2]
            h, cc = c["h"], c["cc"]
            rows = slice(cc * CHUNK, (cc + 1) * CHUNK)
            c.update(tm=t_ref[cc, c["ch"]], dov=do_ref[rows, h * GDN_DIM:(h + 1) * GDN_DIM], vnew=vn_ref[cc, h],
                     dvnew=dvn_ref[cc, h], dw=dw_ref[cc, h], dqg=dqg_ref[cc, h], dkdec=dkd_ref[cc, h],
                     dglast=dgl_ref[cc, h:h + 1, 0:1])
        dintras = [_dot_nt(c["dov"], c["vnew"]) for c in chains]
        dts = [_dot_nt(c["dvnew"], c["vb"]) + _dot_nt(c["dw"], c["kbg"]) for c in chains]
        dvbs = [_dot_tn(c["tm"], c["dvnew"]) for c in chains]
        dkbgs = [_dot_tn(c["tm"], c["dw"]) for c in chains]
        tdts = [_dot3_nt(dt, c["tm"]) for dt, c in zip(dts, chains)]
        dls = [jnp.where(c["masks"][3], -_dot3_tn(c["tm"], tdt), 0.0) for tdt, c in zip(tdts, chains)]
        das = [dl * c["decay"] for dl, c in zip(dls, chains)]
        dqks = [jnp.where(c["masks"][2], di, 0.0) * c["decay"] for di, c in zip(dintras, chains)]
        dkb1 = [_dot(da, c["k"]) for da, c in zip(das, chains)]
        dk1 = [_dot_tn(da, c["kb"]) for da, c in zip(das, chains)]
        dk2 = [_dot_tn(dqk, c["q"]) for dqk, c in zip(dqks, chains)]
        dq1 = [_dot(dqk, c["k"]) for dqk, c in zip(dqks, chains)]
        grads, mms, p_gs, p_betas, p_kds = [], [], [], [], []
        for n, c in enumerate(chains):
            incl = c["masks"][2]
            dkb = dkb1[n] + dkbgs[n] * c["eg"]
            kd = c["dkdec"] * c["kdec"]
            mms.append((dls[n] * c["amat"] + jnp.where(incl, dintras[n], 0.0) * c["qk"]) * c["decay"])
            p_gs.append(c["dqg"] * c["qg"] - kd + dkbgs[n] * c["kbg"])
            p_betas.append(dkb * c["k"] + dvbs[n] * c["v"])
            p_kds.append(kd)
            grads.append((dq1[n] + c["dqg"] * c["eg"],
                          dk1[n] + dk2[n] + c["dkdec"] * c["ek"] + dkb * c["bcol"],
                          dvbs[n] * c["bcol"]))
        row_sums = [jnp.sum(mm, axis=1, keepdims=True) for mm in mms]
        col_sums = [jnp.sum(mm, axis=0, keepdims=True) for mm in mms]
        g_sums = [jnp.sum(pg, axis=1, keepdims=True) for pg in p_gs]
        dbetas = [jnp.sum(pb, axis=1, keepdims=True) for pb in p_betas]
        kd_tots = [jnp.sum(jnp.sum(pk, axis=1, keepdims=True), axis=0, keepdims=True) for pk in p_kds]
        dgcs = [rs - _row_to_col(cs, *c["masks"][0:2]) + gs for rs, cs, gs, c in zip(row_sums, col_sums, g_sums, chains)]
        dgrs = [_col_to_row(dgc, *c["masks"][0:2]) for dgc, c in zip(dgcs, chains)]
        draws = [jnp.sum(jnp.where(jnp.logical_not(c["masks"][3]), dgr, 0.0), axis=1, keepdims=True) + c["dglast"] + kt
                 for dgr, kt, c in zip(dgrs, kd_tots, chains)]
        for c, draw, dbeta in zip(chains, draws, dbetas):
            ch = c["ch"]
            dgates[c["cc"]] = dgates[c["cc"]] + jnp.where(lane == ch, draw, 0.0) + jnp.where(lane == 8 + ch, dbeta, 0.0)
        for cc in range(LOCAL_CHUNKS):
            rows = slice(cc * CHUNK, (cc + 1) * CHUNK)
            for h in range(GDN_HEADS):
                for part in range(3):
                    cols = slice(part * hd + h * GDN_DIM, part * hd + (h + 1) * GDN_DIM)
                    dx_ref[rows, cols] = grads[cc * N_CHAINS + h][part] + grads[cc * N_CHAINS + GDN_HEADS + h][part]
            dg_ref[rows, :] = dgates[cc]

    lc = LOCAL_CHUNKS
    all8 = lambda rows, cols: pl.BlockSpec((lc, N_CHAINS, rows, cols), lambda n: (n, 0, 0, 0))
    own4 = lambda rows, cols: pl.BlockSpec((lc, GDN_HEADS, rows, cols), lambda n: (n, 0, 0, 0))
    row4 = pl.BlockSpec((lc, GDN_HEADS, LANES), lambda n: (n, 0, 0))
    vn_f, vn_b = saved["vn"]
    dvn_f, dvn_b, dw_f, dw_b, dqg_f, dqg_b, dkd_f, dkd_b, dgl_f, dgl_b = scan
    return _grid_call(
        body, "gdn_local_bwd", nc // lc,
        [pl.BlockSpec((lc * CHUNK, 3 * hd), lambda n: (n, 0)), pl.BlockSpec((lc * CHUNK, LANES), lambda n: (n, 0)),
         pl.BlockSpec((lc, 16, CHUNK), lambda n: (n, 0, 0)), pl.BlockSpec((lc * CHUNK, hd), lambda n: (n, 0)),
         all8(CHUNK, CHUNK)] + [own4(CHUNK, GDN_DIM)] * 10 + [row4, row4],
        [pl.BlockSpec((lc * CHUNK, 3 * hd), lambda n: (n, 0)), pl.BlockSpec((lc * CHUNK, LANES), lambda n: (n, 0))],
        [jax.ShapeDtypeStruct((t, 3 * hd), F32), jax.ShapeDtypeStruct((t, LANES), F32)],
        (qkvc, gb, gbt, do, saved["tm"], vn_f, vn_b, dvn_f, dvn_b, dw_f, dw_b, dqg_f, dqg_b, dkd_f, dkd_b, dgl_f, dgl_b),
        exchange=exchange)


def _gdn_post_fwd(of, ob, z, gw, tm):
    t, hd = of.shape

    def body(of_ref, ob_ref, z_ref, w_ref, o_ref):
        for h in range(GDN_HEADS):
            cols = slice(h * GDN_DIM, (h + 1) * GDN_DIM)
            o = of_ref[:, cols] + ob_ref[:, cols]
            zv = z_ref[:, cols]
            o_ref[:, cols] = (o * _rstd(o) * w_ref[...] * (zv * _sigmoid(zv))).astype(BF16)

    row = pl.BlockSpec((tm, hd), lambda i: (i, 0))
    return pl.pallas_call(
        body, name="gdn_post_fwd", grid=(t // tm,),
        in_specs=[row, row, row, _resident((1, GDN_DIM))],
        out_specs=row, out_shape=jax.ShapeDtypeStruct((t, hd), BF16),
        compiler_params=_params(("arbitrary",), VMEM_LIMIT),
    )(of, ob, z, gw)


def _gdn_post_bwd(doa, of, ob, z, gw, tm):
    t, hd = of.shape

    def body(d_ref, of_ref, ob_ref, z_ref, w_ref, do_ref, dz_ref, dw_ref):
        @pl.when(pl.program_id(0) == 0)
        def _():
            dw_ref[...] = jnp.zeros_like(dw_ref)

        dw = jnp.zeros((1, GDN_DIM), F32)
        for h in range(GDN_HEADS):
            cols = slice(h * GDN_DIM, (h + 1) * GDN_DIM)
            o = of_ref[:, cols] + ob_ref[:, cols]
            zv = z_ref[:, cols]
            dv = d_ref[:, cols]
            r = _rstd(o)
            sg = _sigmoid(zv)
            on = o * r * w_ref[...]
            dz_ref[:, cols] = (dv * on * (sg * (1.0 + zv * (1.0 - sg)))).astype(BF16)
            dxr, dwh = _rms_bwd(o, r, w_ref[...], dv * (zv * sg))
            do_ref[:, cols] = dxr
            dw = dw + dwh
        dw_ref[...] += dw

    row = pl.BlockSpec((tm, hd), lambda i: (i, 0))
    return pl.pallas_call(
        body, name="gdn_post_bwd", grid=(t // tm,),
        in_specs=[row, row, row, row, _resident((1, GDN_DIM))],
        out_specs=[row, row, pl.BlockSpec((1, GDN_DIM), lambda i: (0, 0))],
        out_shape=[jax.ShapeDtypeStruct((t, hd), F32), jax.ShapeDtypeStruct((t, hd), BF16),
                   jax.ShapeDtypeStruct((1, GDN_DIM), F32)],
        compiler_params=_params(("arbitrary",), VMEM_LIMIT),
    )(doa, of, ob, z, gw)


SWA_W = SWA_HEADS * SWA_DIM
QBLK = 128
KWIN = QBLK + 2 * RADIUS
WIN_OFFSETS = (0, RADIUS, 2 * RADIUS)


def _t5_bucket(rel):
    nb = REL_BUCKETS // 2
    bucket = (rel > 0).astype(np.int32) * nb
    n = np.abs(rel)
    max_exact = nb // 2
    large = max_exact + (np.log(np.maximum(n, 1) / max_exact)
                         / math.log(REL_MAX_DISTANCE / max_exact) * (nb - max_exact)).astype(np.int32)
    large = np.minimum(large, nb - 1)
    return (bucket + np.where(n < max_exact, n, large)).astype(np.int32)


def _band_tables(dilation):
    a = np.arange(QBLK)
    b = np.arange(KWIN)
    rel = np.stack([b[None, :] - w0 - a[:, None] for w0 in WIN_OFFSETS])
    return np.where(np.abs(rel) <= RADIUS, _t5_bucket(rel * dilation), -1).astype(np.int32)


BAND_CELLS = len(WIN_OFFSETS) * QBLK * KWIN
BIAS_TILE = BAND_CELLS // 3


def _band_index():
    return jnp.asarray(np.concatenate([_band_tables(d).reshape(-1) for _, d in PATTERNS])[None, :])


def _onehot(idx, dtype):
    return (lax.broadcasted_iota(jnp.int32, (REL_BUCKETS, idx.shape[1]), 0) == idx).astype(dtype)


def _bias_tables(rel_bias, idx, tk):
    n = idx.shape[1]

    def body(rb_ref, i_ref, o_ref):
        iv = i_ref[...]
        oh = _onehot(iv, BF16)
        rest, acc = rb_ref[...], None
        for _ in range(3):
            piece = rest.astype(BF16)
            part = jnp.dot(piece, oh, preferred_element_type=F32)
            acc = part if acc is None else acc + part
            rest = rest - piece.astype(F32)
        o_ref[...] = jnp.where(iv < 0, NEG_BIG, acc)

    return pl.pallas_call(
        body, name="bias_tables", grid=(n // tk,),
        in_specs=[_resident((SWA_HEADS, REL_BUCKETS)), pl.BlockSpec((1, tk), lambda k: (0, k))],
        out_specs=pl.BlockSpec((SWA_HEADS, tk), lambda k: (0, k)),
        out_shape=jax.ShapeDtypeStruct((SWA_HEADS, n), F32),
        compiler_params=_params(("arbitrary",), VMEM_LIMIT),
    )(rel_bias.T, idx)


def _head_mean(x2, bd_ref):
    bd = bd_ref[...]
    rest, acc = x2, None
    for _ in range(3):
        piece = rest.astype(BF16)
        part = jnp.dot(piece, bd, preferred_element_type=F32)
        acc = part if acc is None else acc + part
        rest = rest - piece.astype(F32)
    return acc


VIEW_DILATIONS = tuple(d for _, d in PATTERNS if d > 1)


def _view_spec(tm, d):
    return pl.BlockSpec((tm // d, d * SWA_W), lambda i: (i, 0))


def _view_shape(t, d, dtype):
    return jax.ShapeDtypeStruct((t // d, d * SWA_W), dtype)


N_GROUPS = SWA_W // LANES


def _to_view(src_ref, idx, dst_ref, d, rows):
    for r in range(d):
        for g in range(N_GROUPS):
            cols = slice(r * SWA_W + g * LANES, r * SWA_W + (g + 1) * LANES)
            dst_ref[:, cols] = src_ref[idx, g, pl.ds(r, rows // d, stride=d), :].astype(dst_ref.dtype)


def _from_view(src_ref, dst_ref, idx, d, rows):
    for r in range(d):
        for g in range(N_GROUPS):
            cols = slice(r * SWA_W + g * LANES, r * SWA_W + (g + 1) * LANES)
            dst_ref[idx, g, pl.ds(r, rows // d, stride=d), :] = src_ref[:, cols]


def _swa_prep_fwd(qkvb, qw, kw, bd, tm):
    t = qkvb.shape[0]

    def body(x_ref, qw_ref, kw_ref, bd_ref, *rest):
        outs, sc = rest[:-1], rest[-1]
        for gidx in range(N_GROUPS):
            cols = slice(gidx * LANES, (gidx + 1) * LANES)
            xq = x_ref[:, cols]
            sc[0, gidx] = xq * lax.rsqrt(_head_mean(xq * xq, bd_ref) + EPS) * qw_ref[:, cols] * (SWA_DIM ** -0.5)
            xk = x_ref[:, SWA_W + gidx * LANES:SWA_W + (gidx + 1) * LANES]
            sc[1, gidx] = xk * lax.rsqrt(_head_mean(xk * xk, bd_ref) + EPS) * kw_ref[:, cols]
            sc[2, gidx] = x_ref[:, 2 * SWA_W + gidx * LANES:2 * SWA_W + (gidx + 1) * LANES]
            for i in range(3):
                outs[i][:, cols] = sc[i, gidx].astype(BF16)
        for i in range(3):
            for n, d in enumerate(VIEW_DILATIONS):
                _to_view(sc, i, outs[3 * (n + 1) + i], d, tm)

    return pl.pallas_call(
        body, name="swa_prep_fwd", grid=(t // tm,),
        in_specs=[pl.BlockSpec((tm, 3 * SWA_W), lambda i: (i, 0)), _resident((1, SWA_W)), _resident((1, SWA_W)),
                  _resident((LANES, LANES))],
        out_specs=[_view_spec(tm, d) for d in (1,) + VIEW_DILATIONS for _ in range(3)],
        out_shape=[_view_shape(t, d, BF16) for d in (1,) + VIEW_DILATIONS for _ in range(3)],
        scratch_shapes=[pltpu.VMEM((3, N_GROUPS, tm, LANES), F32)],
        compiler_params=_params(("arbitrary",), VMEM_LIMIT),
    )(qkvb, qw, kw, bd)


def _swa_prep_bwd(qkvb, qw, kw, bd, grads, tm):
    t = qkvb.shape[0]

    def body(x_ref, qw_ref, kw_ref, bd_ref, *rest):
        parts, (dx_ref, dqw_ref, dkw_ref, sc) = rest[:9], rest[9:]
        @pl.when(pl.program_id(0) == 0)
        def _():
            dqw_ref[...] = jnp.zeros_like(dqw_ref)
            dkw_ref[...] = jnp.zeros_like(dkw_ref)

        for i in range(3):
            for n, d in enumerate(VIEW_DILATIONS):
                _from_view(parts[3 * (n + 1) + i], sc, 2 * i + n, d, tm)
        for gidx in range(N_GROUPS):
            cols = slice(gidx * LANES, (gidx + 1) * LANES)
            for i, base, w_ref, dw_ref, scale in ((0, 0, qw_ref, dqw_ref, SWA_DIM ** -0.5),
                                                  (1, SWA_W, kw_ref, dkw_ref, 1.0)):
                xv = x_ref[:, base + gidx * LANES:base + (gidx + 1) * LANES]
                dy = (parts[i][:, cols] + sc[2 * i, gidx] + sc[2 * i + 1, gidx]) * scale
                r = lax.rsqrt(_head_mean(xv * xv, bd_ref) + EPS)
                xhat = xv * r
                dxh = dy * w_ref[:, cols]
                dx = r * (dxh - xhat * _head_mean(dxh * xhat, bd_ref))
                dx_ref[:, base + gidx * LANES:base + (gidx + 1) * LANES] = dx.astype(BF16)
                dw_ref[:, cols] += jnp.sum(dy * xhat, axis=0, keepdims=True)
            dx_ref[:, 2 * SWA_W + gidx * LANES:2 * SWA_W + (gidx + 1) * LANES] = (
                parts[2][:, cols] + sc[4, gidx] + sc[5, gidx]).astype(BF16)

    wrow = pl.BlockSpec((1, SWA_W), lambda i: (0, 0))
    return pl.pallas_call(
        body, name="swa_prep_bwd", grid=(t // tm,),
        in_specs=[pl.BlockSpec((tm, 3 * SWA_W), lambda i: (i, 0)), _resident((1, SWA_W)), _resident((1, SWA_W)),
                  _resident((LANES, LANES))] + [_view_spec(tm, d) for d in (1,) + VIEW_DILATIONS for _ in range(3)],
        out_specs=[pl.BlockSpec((tm, 3 * SWA_W), lambda i: (i, 0)), wrow, wrow],
        out_shape=[jax.ShapeDtypeStruct((t, 3 * SWA_W), BF16), jax.ShapeDtypeStruct((1, SWA_W), F32),
                   jax.ShapeDtypeStruct((1, SWA_W), F32)],
        scratch_shapes=[pltpu.VMEM((6, N_GROUPS, tm, LANES), F32)],
        compiler_params=_params(("arbitrary",), VMEM_LIMIT),
    )(qkvb, qw, kw, bd, *grads)


def _aligned(v, m):
    return v if isinstance(v, int) else pl.multiple_of(v, m)


BAND_GROUP = 2


def _band_loop(nsub, length, step, group=BAND_GROUP):
    step([(0, 0)], 0)
    if nsub > 2:
        assert (nsub - 2) % group == 0

        def inner(i, carry):
            s0 = 1 + i * group
            step([(s0 + e, pl.multiple_of((s0 + e) * QBLK - RADIUS, RADIUS)) for e in range(group)], 1)
            return carry
        lax.fori_loop(0, (nsub - 2) // group, inner, 0)
    step([(nsub - 1, length - KWIN)], 2)


def _head_select(lane, a0, a1):
    return jnp.where(lane < SWA_DIM, a0, a1)


def _swa_fwd(qv, kv, vv, bias, dilation, name):
    length = qv.shape[0]
    nsub = length // QBLK
    assert nsub >= 2 and length % QBLK == 0

    def body(q_ref, k_ref, v_ref, b_ref, o_ref, l_ref):
        lane = lax.broadcasted_iota(jnp.int32, (QBLK, LANES), 1)

        def step(blocks, var):
            items = []
            for s, ws in blocks:
                rows = pl.ds(_aligned(s * QBLK, QBLK), QBLK)
                q, kk, vw = q_ref[rows, :], k_ref[pl.ds(ws, KWIN), :], v_ref[pl.ds(ws, KWIN), :]
                for hh in range(2):
                    items.append((hh, jnp.where((lane < SWA_DIM) == (hh == 0), q, jnp.zeros_like(q)), kk, vw))
            lgs = [_dot_nt(qh, kk) + b_ref[hh, var] for hh, qh, kk, _ in items]
            ms = [jnp.max(lg, axis=-1, keepdims=True) for lg in lgs]
            ps = [jnp.exp(lg - m) for lg, m in zip(lgs, ms)]
            dens = [jnp.sum(p, axis=-1, keepdims=True) for p in ps]
            pvs = [_dot(p, it[3]) for p, it in zip(ps, items)]
            for n, (s, _) in enumerate(blocks):
                rows = pl.ds(_aligned(s * QBLK, QBLK), QBLK)
                o0, o1 = (pvs[2 * n + hh] / dens[2 * n + hh] for hh in range(2))
                l0, l1 = (ms[2 * n + hh] + jnp.log(dens[2 * n + hh]) for hh in range(2))
                o_ref[rows, :] = _head_select(lane, o0, o1)
                l_ref[rows, :] = _head_select(lane, l0, l1)

        _band_loop(nsub, length, step)

    blk = pl.BlockSpec((length, LANES), lambda hp, r: (0, r * (SWA_W // LANES) + hp))
    shp = jax.ShapeDtypeStruct(qv.shape, F32)
    return pl.pallas_call(
        body, name=name, grid=(SWA_W // LANES, dilation),
        in_specs=[blk, blk, blk, pl.BlockSpec((2, 3, QBLK, KWIN), lambda hp, r: (hp, 0, 0, 0))],
        out_specs=[blk, blk], out_shape=[shp, shp],
        compiler_params=_params(("arbitrary", "arbitrary"), VMEM_LIMIT),
    )(qv, kv, vv, bias)


def _swa_combine(os_, ls_, tm):
    t = os_[0].shape[0]

    def body(o0, o1, o2, l0, l1, l2, o_ref, ob_ref, la_ref, lb_ref, lc_ref, sc):
        for n, d in enumerate(VIEW_DILATIONS):
            _from_view((o1, o2)[n], sc, n, d, tm)
            _from_view((l1, l2)[n], sc, 2 + n, d, tm)
        for g in range(N_GROUPS):
            cols = slice(g * LANES, (g + 1) * LANES)
            la, lb, lc = l0[:, cols], sc[2, g], sc[3, g]
            m = jnp.maximum(jnp.maximum(la, lb), lc)
            tot = m + jnp.log(jnp.exp(la - m) + jnp.exp(lb - m) + jnp.exp(lc - m))
            o = jnp.exp(la - tot) * o0[:, cols] + jnp.exp(lb - tot) * sc[0, g] + jnp.exp(lc - tot) * sc[1, g]
            o_ref[:, cols] = o
            ob_ref[:, cols] = o.astype(BF16)
            la_ref[:, cols] = tot
            sc[4, g] = tot
        for n, d in enumerate(VIEW_DILATIONS):
            _to_view(sc, 4, (lb_ref, lc_ref)[n], d, tm)

    specs = [_view_spec(tm, d) for d in (1,) + VIEW_DILATIONS]
    return pl.pallas_call(
        body, name="swa_combine", grid=(t // tm,), in_specs=specs + specs, out_specs=[specs[0], specs[0]] + specs,
        out_shape=[jax.ShapeDtypeStruct((t, SWA_W), F32), jax.ShapeDtypeStruct((t, SWA_W), BF16)]
                  + [_view_shape(t, d, F32) for d in (1,) + VIEW_DILATIONS],
        scratch_shapes=[pltpu.VMEM((5, N_GROUPS, tm, LANES), F32)],
        compiler_params=_params(("arbitrary",), VMEM_LIMIT),
    )(*os_, *ls_)


def _swa_bwd_prep(do, o, bd, tm):
    t = do.shape[0]

    def body(d_ref, o_ref, bd_ref, dd1, dd4, dd16, db1, db4, db16, sc):
        for gidx in range(N_GROUPS):
            cols = slice(gidx * LANES, (gidx + 1) * LANES)
            dv = d_ref[:, cols]
            dd = _head_mean(dv * o_ref[:, cols], bd_ref) * float(SWA_DIM)
            sc[0, gidx] = dd
            sc[1, gidx] = dv
            dd1[:, cols] = dd
            db1[:, cols] = dv.astype(BF16)
        for n, d in enumerate(VIEW_DILATIONS):
            _to_view(sc, 0, (dd4, dd16)[n], d, tm)
            _to_view(sc, 1, (db4, db16)[n], d, tm)

    specs = [_view_spec(tm, d) for d in (1,) + VIEW_DILATIONS]
    return pl.pallas_call(
        body, name="swa_bwd_prep", grid=(t // tm,), in_specs=[specs[0], specs[0], _resident((LANES, LANES))],
        out_specs=specs + specs,
        out_shape=[_view_shape(t, d, F32) for d in (1,) + VIEW_DILATIONS]
                  + [_view_shape(t, d, BF16) for d in (1,) + VIEW_DILATIONS],
        scratch_shapes=[pltpu.VMEM((2, N_GROUPS, tm, LANES), F32)],
        compiler_params=_params(("arbitrary",), VMEM_LIMIT),
    )(do, o, bd)


def _swa_bwd(qv, kv, vv, dov, lv, ddv, bias_a, dilation, name):
    length = qv.shape[0]
    nsub = length // QBLK
    single = pl.Buffered(1) if dilation == 1 else None

    def body(q_ref, k_ref, v_ref, do_ref, l_ref, dd_ref, ba_ref, dq_ref, dk_ref, dv_ref, db_ref):
        @pl.when(pl.program_id(1) == 0)
        def _():
            db_ref[...] = jnp.zeros_like(db_ref)

        lane = lax.broadcasted_iota(jnp.int32, (QBLK, LANES), 1)
        lanew = lax.broadcasted_iota(jnp.int32, (KWIN, LANES), 1)

        def step(blocks, var):
            items = []
            for s, ws in blocks:
                rows = pl.ds(_aligned(s * QBLK, QBLK), QBLK)
                win = pl.ds(ws, KWIN)
                q, dov_ = q_ref[rows, :], do_ref[rows, :]
                kk, vw = k_ref[win, :], v_ref[win, :]
                lse, dd = l_ref[rows, :], dd_ref[rows, :]
                for hh in range(2):
                    mine = (lane < SWA_DIM) == (hh == 0)
                    col = slice(hh * SWA_DIM, hh * SWA_DIM + 1)
                    items.append((hh, jnp.where(mine, q, jnp.zeros_like(q)), jnp.where(mine, dov_, jnp.zeros_like(dov_)),
                                  kk, vw, lse[:, col], dd[:, col], q, dov_))
            lgs = [_dot_nt(it[1], it[3]) + ba_ref[it[0], var] for it in items]
            dps = [_dot_nt(it[2], it[4]) for it in items]
            ps = [jnp.exp(lg - it[5]) for lg, it in zip(lgs, items)]
            dss = [p * (dp - it[6]) for p, dp, it in zip(ps, dps, items)]
            dqs = [_dot(ds, it[3]) for ds, it in zip(dss, items)]
            dks = [_dot_tn(ds, it[7]) for ds, it in zip(dss, items)]
            dvs = [_dot_tn(p, it[8]) for p, it in zip(ps, items)]
            for n, (s, ws) in enumerate(blocks):
                rows = pl.ds(_aligned(s * QBLK, QBLK), QBLK)
                win = pl.ds(ws, KWIN)
                dq_ref[rows, :] = _head_select(lane, dqs[2 * n], dqs[2 * n + 1])
                dk_ref[win, :] += _head_select(lanew, dks[2 * n], dks[2 * n + 1])
                dv_ref[win, :] += _head_select(lanew, dvs[2 * n], dvs[2 * n + 1])
            for hh in range(2):
                tot = dss[hh]
                for n in range(1, len(blocks)):
                    tot = tot + dss[2 * n + hh]
                db_ref[hh, var] += tot

        dk_ref[...] = jnp.zeros_like(dk_ref)
        dv_ref[...] = jnp.zeros_like(dv_ref)
        _band_loop(nsub, length, step)

    imap = lambda hp, r: (0, r * (SWA_W // LANES) + hp)
    blk_in = pl.BlockSpec((length, LANES), imap, pipeline_mode=single)
    blk_out = pl.BlockSpec((length, LANES), imap)
    shp = jax.ShapeDtypeStruct(qv.shape, F32)
    return pl.pallas_call(
        body, name=name, grid=(SWA_W // LANES, dilation),
        in_specs=[blk_in] * 6 + [pl.BlockSpec((2, 3, QBLK, KWIN), lambda hp, r: (hp, 0, 0, 0))],
        out_specs=[blk_out, blk_out, blk_out, pl.BlockSpec((2, 3, QBLK, KWIN), lambda hp, r: (hp, 0, 0, 0))],
        out_shape=[shp, shp, shp, jax.ShapeDtypeStruct((SWA_HEADS, 3, QBLK, KWIN), F32)],
        compiler_params=_params(("arbitrary", "arbitrary"), VMEM_LIMIT),
    )(qv, kv, vv, dov, lv, ddv, bias_a)


def _bias_grad(ds2, idx, tk):
    n = ds2.shape[1]
    nk = n // tk

    def body(a_ref, i_ref, o_ref):
        @pl.when(pl.program_id(0) == 0)
        def _():
            o_ref[...] = jnp.zeros_like(o_ref)

        oh = _onehot(i_ref[...], BF16)
        rest = a_ref[...]
        acc = jnp.zeros((SWA_HEADS, REL_BUCKETS), F32)
        for _ in range(3):
            piece = rest.astype(BF16)
            acc = acc + _dot_nt(piece, oh)
            rest = rest - piece.astype(F32)
        o_ref[...] += acc

    return pl.pallas_call(
        body, name="bias_grad", grid=(nk,),
        in_specs=[pl.BlockSpec((SWA_HEADS, tk), lambda k: (0, k)), pl.BlockSpec((1, tk), lambda k: (0, k))],
        out_specs=pl.BlockSpec((SWA_HEADS, REL_BUCKETS), lambda k: (0, 0)),
        out_shape=jax.ShapeDtypeStruct((SWA_HEADS, REL_BUCKETS), F32),
        compiler_params=_params(("arbitrary",), VMEM_LIMIT),
    )(ds2, idx)


def _swa_branch_fwd(qkvb, qw_t, kw_t, rel_bias, bd, tm):
    qkv = _swa_prep_fwd(qkvb, qw_t, kw_t, bd, tm)
    tables = _bias_tables(rel_bias, _band_index(), BIAS_TILE)
    os_, ls_, tabs = [], [], []
    for n, (_, d) in enumerate(PATTERNS):
        bias = tables[:, n * BAND_CELLS:(n + 1) * BAND_CELLS].reshape(SWA_HEADS, len(WIN_OFFSETS), QBLK, KWIN)
        o_p, l_p = _swa_fwd(*qkv[3 * n:3 * n + 3], bias, d, f"swa_fwd_d{d}")
        os_.append(o_p)
        ls_.append(l_p)
        tabs.append(bias)
    o, o16, *lses = _swa_combine(os_, ls_, tm)
    return o, o16, (qkv, lses, tabs)


def _swa_branch_bwd(do, o, saved, qkvb, qw_t, kw_t, bd, tm):
    qkv, lses, tabs = saved
    prep = _swa_bwd_prep(do, o, bd, tm)
    grads, dss = [], []
    for n, ((_, d), bias) in enumerate(zip(PATTERNS, tabs)):
        dq, dk, dv, ds = _swa_bwd(*qkv[3 * n:3 * n + 3], prep[3 + n], lses[n], prep[n], bias, d, f"swa_bwd_d{d}")
        grads += [dq, dk, dv]
        dss.append(ds.reshape(SWA_HEADS, -1))
    dqkvb, dqw, dkw = _swa_prep_bwd(qkvb, qw_t, kw_t, bd, grads, tm)
    dbias = _bias_grad(jnp.concatenate(dss, axis=1), _band_index(), BIAS_TILE)
    fold = lambda w: jnp.sum(w.reshape(SWA_HEADS, SWA_DIM), axis=0)
    return dqkvb, fold(dqw), fold(dkw), dbias.T


def _mesh_pos():
    return lax.axis_index("x"), lax.axis_index("y"), lax.axis_index("c")


def _other_chips(x, y):
    return [(1 - x, y), (x, 1 - y), (1 - x, 1 - y)]


def _remote(src, dst, send_sem, recv_sem, device):
    return pltpu.make_async_remote_copy(src_ref=src, dst_ref=dst, send_sem=send_sem, recv_sem=recv_sem,
                                        device_id=device, device_id_type=MESH)


def _split_axis(shape2):
    return 0 if (shape2[0] // 2) % 16 == 0 else 1


def _half_index(shape2, c):
    axis = _split_axis(shape2)
    h = shape2[axis] // 2
    return (pl.ds(c * h, h), slice(None)) if axis == 0 else (slice(None), pl.ds(c * h, h))


def _all_gather(xs):
    n = len(xs)

    def body(*refs):
        ins, outs = refs[:n], refs[n:2 * n]
        send_sems, recv_sems = refs[2 * n:]
        x, y, c = _mesh_pos()
        me = 2 * x + y
        chips = _other_chips(x, y)
        halves = []
        sends = []
        for a in range(n):
            h = ins[a].shape[0] // 2
            mine, other = pl.ds(c * h, h), pl.ds((1 - c) * h, h)
            halves.append((mine, other))
            own = _remote(ins[a], outs[a].at[me], send_sems.at[a, 6], recv_sems.at[a, 6], (x, y, 1 - c))
            own.start()
            sends.append(own)
            for j, chip in enumerate(chips):
                cp = _remote(ins[a].at[mine], outs[a].at[me, mine], send_sems.at[a, j], recv_sems.at[a, j], (*chip, c))
                cp.start()
                sends.append(cp)
        for a in range(n):
            mine, _ = halves[a]
            for j, chip in enumerate(chips):
                src = 2 * chip[0] + chip[1]
                landed = outs[a].at[src, mine]
                _remote(landed, landed, send_sems.at[a, j], recv_sems.at[a, j], (x, y, c)).wait_recv()
                fwd = _remote(landed, landed, send_sems.at[a, 3 + j], recv_sems.at[a, 3 + j], (x, y, 1 - c))
                fwd.start()
                sends.append(fwd)
        for a in range(n):
            _, other = halves[a]
            for j, chip in enumerate(chips):
                src = 2 * chip[0] + chip[1]
                landed = outs[a].at[src, other]
                _remote(landed, landed, send_sems.at[a, 3 + j], recv_sems.at[a, 3 + j], (x, y, c)).wait_recv()
            mine_slot = outs[a].at[me]
            _remote(mine_slot, mine_slot, send_sems.at[a, 6], recv_sems.at[a, 6], (x, y, c)).wait_recv()
        for cp in sends:
            cp.wait_send()

    return list(pl.pallas_call(
        body, name="all_gather_weights",
        in_specs=[ANY] * n, out_specs=[ANY] * n,
        out_shape=[jax.ShapeDtypeStruct((N_SHARDS,) + a.shape, a.dtype) for a in xs],
        scratch_shapes=[pltpu.SemaphoreType.DMA((n, 7)), pltpu.SemaphoreType.DMA((n, 7))],
    )(*xs))


def _rs_pair(gs):
    n = len(gs)

    def body(*refs):
        ins, lands = refs[:n], refs[n:2 * n]
        send_sems, recv_sems = refs[2 * n:]
        x, y, c = _mesh_pos()
        cps = []
        for a in range(n):
            theirs = (slice(None),) + _half_index(ins[a].shape[1:], 1 - c)
            cp = _remote(ins[a].at[theirs], lands[a], send_sems.at[a], recv_sems.at[a], (x, y, 1 - c))
            cp.start()
            cps.append(cp)
        for cp in cps:
            cp.wait()

    def half_shape(g):
        dims = list(g.shape)
        dims[1 + _split_axis(g.shape[1:])] //= 2
        return tuple(dims)

    return list(pl.pallas_call(
        body, name="rs_pair", in_specs=[ANY] * n, out_specs=[ANY] * n,
        out_shape=[jax.ShapeDtypeStruct(half_shape(g), g.dtype) for g in gs],
        scratch_shapes=[pltpu.SemaphoreType.DMA((n,)), pltpu.SemaphoreType.DMA((n,))],
    )(*gs))


def _pair_exchange(gs):
    def copies(cin, cout, send_sems, recv_sems):
        x, y, c = _mesh_pos()
        return [_remote(g.at[(slice(None),) + _half_index(g.shape[1:], 1 - c)], land, send_sems.at[a, 0],
                        recv_sems.at[a, 0], (x, y, 1 - c)) for a, (g, land) in enumerate(zip(cin, cout))]

    def start(*refs):
        for cp in copies(*refs):
            cp.start()

    def finish(*refs):
        for cp in copies(*refs):
            cp.wait()

    def half_shape(g):
        dims = list(g.shape)
        dims[1 + _split_axis(g.shape[1:])] //= 2
        return tuple(dims)

    return _Exchange(tuple(gs), tuple(jax.ShapeDtypeStruct(half_shape(g), g.dtype) for g in gs), start, finish)


def _rs_chips(ss):
    n = len(ss)

    def body(*refs):
        ins, outs = refs[:n], refs[n:2 * n]
        send_sems, recv_sems = refs[2 * n:]
        x, y, c = _mesh_pos()
        me = 2 * x + y
        chips = _other_chips(x, y)
        cps = []
        for a in range(n):
            for j, chip in enumerate(chips):
                dst_chip = 2 * chip[0] + chip[1]
                cp = _remote(ins[a].at[dst_chip], outs[a].at[me], send_sems.at[a, j], recv_sems.at[a, j], (*chip, c))
                cp.start()
                cps.append(cp)
        for a in range(n):
            for j, chip in enumerate(chips):
                src = 2 * chip[0] + chip[1]
                _remote(outs[a].at[src], outs[a].at[src], send_sems.at[a, j], recv_sems.at[a, j], (x, y, c)).wait_recv()
        for cp in cps:
            cp.wait_send()

    return list(pl.pallas_call(
        body, name="rs_chips", in_specs=[ANY] * n, out_specs=[ANY] * n,
        out_shape=[jax.ShapeDtypeStruct(s.shape, s.dtype) for s in ss],
        scratch_shapes=[pltpu.SemaphoreType.DMA((n, 3)), pltpu.SemaphoreType.DMA((n, 3))],
    )(*ss))


def _rs_join(fs, axes):
    n = len(fs)

    def whole(f, axis):
        dims = list(f.shape)
        dims[axis] *= 2
        return tuple(dims)

    def body(*refs):
        ins, outs = refs[:n], refs[n:2 * n]
        send_sems, recv_sems = refs[2 * n:]
        x, y, c = _mesh_pos()
        cps = []
        for a in range(n):
            h = ins[a].shape[axes[a]]
            mine = (pl.ds(c * h, h), slice(None)) if axes[a] == 0 else (slice(None), pl.ds(c * h, h))
            cp = _remote(ins[a], outs[a].at[mine], send_sems.at[a], recv_sems.at[a], (x, y, 1 - c))
            cp.start()
            cps.append(cp)
        for cp in cps:
            cp.wait()

    outs = pl.pallas_call(
        body, name="rs_join", in_specs=[ANY] * n, out_specs=[ANY] * n,
        out_shape=[jax.ShapeDtypeStruct(whole(f, ax), f.dtype) for f, ax in zip(fs, axes)],
        scratch_shapes=[pltpu.SemaphoreType.DMA((n,)), pltpu.SemaphoreType.DMA((n,))],
    )(*fs)
    c = lax.axis_index("c")
    return [lax.dynamic_update_slice_in_dim(o, f, c * f.shape[ax], ax) for o, f, ax in zip(outs, fs, axes)]


def _gather_exchange(xs):
    def start(cin, cout, send_sems, recv_sems):
        x, y, c = _mesh_pos()
        me = 2 * x + y
        for a, (src, dst) in enumerate(zip(cin, cout)):
            mine = _half_index(src.shape, c)
            for j, chip in enumerate(_other_chips(x, y)):
                _remote(src.at[mine], dst.at[(me,) + mine], send_sems.at[a, j], recv_sems.at[a, j], (*chip, c)).start()
            _remote(src, dst.at[me], send_sems.at[a, 3], recv_sems.at[a, 3], (x, y, 1 - c)).start()

    def finish(cin, cout, send_sems, recv_sems):
        x, y, c = _mesh_pos()
        for a, dst in enumerate(cout):
            for j, chip in enumerate(_other_chips(x, y)):
                landed = dst.at[(2 * chip[0] + chip[1],) + _half_index(dst.shape[1:], c)]
                _remote(landed, landed, send_sems.at[a, j], recv_sems.at[a, j], (x, y, c)).wait()
            own = dst.at[2 * x + y]
            _remote(own, own, send_sems.at[a, 3], recv_sems.at[a, 3], (x, y, c)).wait()

    return _Exchange(tuple(xs), tuple(jax.ShapeDtypeStruct((N_SHARDS,) + a.shape, a.dtype) for a in xs), start, finish)


def _gather_forward(gs):
    n = len(gs)

    def body(*refs):
        outs = refs[n:2 * n]
        send_sems, recv_sems = refs[2 * n:]
        x, y, c = _mesh_pos()
        chips = _other_chips(x, y)
        cps = []
        for a in range(n):
            for j, chip in enumerate(chips):
                landed = outs[a].at[(2 * chip[0] + chip[1],) + _half_index(outs[a].shape[1:], c)]
                cp = _remote(landed, landed, send_sems.at[a, j], recv_sems.at[a, j], (x, y, 1 - c))
                cp.start()
                cps.append(cp)
        for a in range(n):
            for j, chip in enumerate(chips):
                other = outs[a].at[(2 * chip[0] + chip[1],) + _half_index(outs[a].shape[1:], 1 - c)]
                _remote(other, other, send_sems.at[a, j], recv_sems.at[a, j], (x, y, c)).wait_recv()
        for cp in cps:
            cp.wait_send()

    return list(pl.pallas_call(
        body, name="gather_forward", in_specs=[ANY] * n, out_specs=[ANY] * n,
        out_shape=[jax.ShapeDtypeStruct(g.shape, g.dtype) for g in gs],
        input_output_aliases={i: i for i in range(n)},
        scratch_shapes=[pltpu.SemaphoreType.DMA((n, 3)), pltpu.SemaphoreType.DMA((n, 3))],
    )(*gs))


def _scatter_exchange(ss):
    def start(cin, cout, send_sems, recv_sems):
        x, y, c = _mesh_pos()
        me = 2 * x + y
        for a, (src, dst) in enumerate(zip(cin, cout)):
            for j, chip in enumerate(_other_chips(x, y)):
                _remote(src.at[2 * chip[0] + chip[1]], dst.at[me], send_sems.at[a, j], recv_sems.at[a, j],
                        (*chip, c)).start()

    def finish(cin, cout, send_sems, recv_sems):
        x, y, c = _mesh_pos()
        for a, dst in enumerate(cout):
            for j, chip in enumerate(_other_chips(x, y)):
                slot = dst.at[2 * chip[0] + chip[1]]
                _remote(slot, slot, send_sems.at[a, j], recv_sems.at[a, j], (x, y, c)).wait()

    return _Exchange(tuple(ss), tuple(jax.ShapeDtypeStruct(s.shape, s.dtype) for s in ss), start, finish)


def _add_pairs(gs, lands, name):
    n = len(gs)

    def body(*refs):
        c = lax.axis_index("c")
        for g_ref, l_ref, o_ref in zip(refs[:n], refs[n:2 * n], refs[2 * n:]):
            mine = g_ref[(0,) + _half_index(g_ref.shape[1:], c)]
            o_ref[0] = (mine.astype(F32) + l_ref[0].astype(F32)).astype(BF16)

    whole = [pl.BlockSpec((1,) + g.shape[1:], lambda j: (j, 0, 0)) for g in gs]
    half = [pl.BlockSpec((1,) + l.shape[1:], lambda j: (j, 0, 0)) for l in lands]
    return list(pl.pallas_call(body, name=name, grid=(gs[0].shape[0],), in_specs=whole + half, out_specs=half,
                               out_shape=[jax.ShapeDtypeStruct(l.shape, BF16) for l in lands],
                               compiler_params=_params(("arbitrary",), VMEM_LIMIT))(*gs, *lands))


def _sum_slots(slots, owns, name):
    n = len(slots)

    def body(*refs):
        me = 2 * lax.axis_index("x") + lax.axis_index("y")
        for s_ref, o_ref, out_ref in zip(refs[:n], refs[n:2 * n], refs[2 * n:]):
            acc = jnp.zeros(out_ref.shape, F32)
            for s in range(N_SHARDS):
                acc = acc + jnp.where(me == s, o_ref[s], s_ref[s]).astype(F32)
            out_ref[...] = acc

    def specs(a):
        _, h, c = a.shape
        if h % 32 == 0:
            return (pl.BlockSpec((N_SHARDS, h // 2, c), lambda i: (0, i, 0)), pl.BlockSpec((h // 2, c), lambda i: (i, 0)))
        return (pl.BlockSpec((N_SHARDS, h, c // 2), lambda i: (0, 0, i)), pl.BlockSpec((h, c // 2), lambda i: (0, i)))

    in_specs = [specs(a)[0] for a in slots]
    return list(pl.pallas_call(body, name=name, grid=(2,), in_specs=in_specs + in_specs,
                               out_specs=[specs(a)[1] for a in slots],
                               out_shape=[jax.ShapeDtypeStruct(a.shape[1:], F32) for a in slots],
                               compiler_params=_params(("arbitrary",), VMEM_LIMIT))(*slots, *owns))


def _all_reduce_small(p):
    r = p.shape[0]

    def body(p_ref, o_ref, buf, send_sems, recv_sems):
        x, y, c = _mesh_pos()
        me = 4 * x + 2 * y + c
        buf[me] = p_ref[...]
        cps = []
        k = 0
        for fx in range(2):
            for fy in range(2):
                for fc in range(2):
                    if fx + fy + fc == 0:
                        continue
                    peer = (1 - x if fx else x, 1 - y if fy else y, 1 - c if fc else c)
                    peer_id = 4 * peer[0] + 2 * peer[1] + peer[2]
                    cp = _remote(p_ref, buf.at[me], send_sems.at[k], recv_sems.at[k], peer)
                    cp.start()
                    cps.append((cp, peer_id, k))
                    k += 1
        for cp, peer_id, k in cps:
            _remote(p_ref, buf.at[peer_id], send_sems.at[k], recv_sems.at[k], (x, y, c)).wait_recv()
        for cp, _, _ in cps:
            cp.wait_send()
        acc = buf[0]
        for s in range(1, 8):
            acc = acc + buf[s]
        o_ref[...] = acc

    vm = pl.BlockSpec(memory_space=pltpu.VMEM)
    return pl.pallas_call(
        body, name="all_reduce_small", in_specs=[vm], out_specs=vm,
        out_shape=jax.ShapeDtypeStruct(p.shape, F32),
        scratch_shapes=[pltpu.VMEM((8, r, LANES), F32), pltpu.SemaphoreType.DMA((7,)), pltpu.SemaphoreType.DMA((7,))],
    )(p)


def _adamw(params, name, steps):
    c1 = 1.0 / (1.0 - ADAM_B1 ** ADAM_STEP)
    c2 = 1.0 / (1.0 - ADAM_B2 ** ADAM_STEP)
    n = len(params)

    def body(*refs):
        for a in range(n):
            w_ref, g_ref, m_ref, v_ref = refs[4 * a:4 * a + 4]
            d_ref, nm_ref, nv_ref = refs[4 * n + 3 * a:4 * n + 3 * a + 3]
            gv = g_ref[...]
            nm = ADAM_B1 * m_ref[...] + (1.0 - ADAM_B1) * gv
            nv = ADAM_B2 * v_ref[...] + (1.0 - ADAM_B2) * (gv * gv)
            d_ref[...] = -ADAM_LR * ((nm * c1) / (jnp.sqrt(nv * c2) + ADAM_EPS) + ADAM_WD * w_ref[...])
            nm_ref[...] = nm
            nv_ref[...] = nv

    def spec(shape):
        r, c = shape
        if r % (8 * steps) == 0:
            return pl.BlockSpec((r // steps, c), lambda i: (i, 0))
        assert c % (LANES * steps) == 0
        return pl.BlockSpec((r, c // steps), lambda i: (0, i))

    specs = [spec(w.shape) for w, _, _, _ in params]
    res = pl.pallas_call(
        body, name=name, grid=(steps,),
        in_specs=[s for s in specs for _ in range(4)], out_specs=[s for s in specs for _ in range(3)],
        out_shape=[jax.ShapeDtypeStruct(w.shape, F32) for w, _, _, _ in params for _ in range(3)],
        compiler_params=_params(("arbitrary",), VMEM_LIMIT))(*[a for p4 in params for a in p4])
    return [tuple(res[3 * a:3 * a + 3]) for a in range(n)]


PACK_UNIT = 8 * LANES


def _pack(arrs):
    parts = []
    for a in arrs:
        f = a.reshape(-1).astype(F32)
        parts.append(jnp.pad(f, (0, (-f.shape[0]) % PACK_UNIT)).reshape(-1, LANES))
    return jnp.concatenate(parts, axis=0)


def _unpack(m, shapes):
    outs, row = [], 0
    for s in shapes:
        n = int(np.prod(s))
        rows = -(-n // PACK_UNIT) * 8
        outs.append(m[row:row + rows].reshape(-1)[:n].reshape(s))
        row += rows
    return outs


WEIGHTS = ["ffn1_norm", "ffn1_w_gate", "ffn1_w_up", "ffn1_w_down", "mix_norm", "w_in", "conv_w", "a_log", "dt_bias",
           "gdn_norm_w", "q_norm_w", "k_norm_w", "rel_bias", "w_out", "ffn2_norm", "ffn2_w_gate", "ffn2_w_up",
           "ffn2_w_down", "final_norm"]
BIG = ["ffn1_w_gate", "ffn1_w_up", "ffn1_w_down", "w_in", "w_out", "ffn2_w_gate", "ffn2_w_up", "ffn2_w_down"]
SMALL = [n for n in WEIGHTS if n not in BIG]
COL_SHARDED = ["ffn1_w_gate", "ffn1_w_up", "w_in", "ffn2_w_gate", "ffn2_w_up"]
N_IN_COLS = 3600
TM = 256
TE = 512
ADAM_PIECES = 8
TK = 2048


def kernel(x, ffn1_norm, ffn1_w_gate, ffn1_w_up, ffn1_w_down, mix_norm, w_in, conv_w, a_log, dt_bias, gdn_norm_w, q_norm_w, k_norm_w, rel_bias, w_out, ffn2_norm, ffn2_w_gate, ffn2_w_up, ffn2_w_down, final_norm, loss_target, m_ffn1_norm, m_ffn1_w_gate, m_ffn1_w_up, m_ffn1_w_down, m_mix_norm, m_w_in, m_conv_w, m_a_log, m_dt_bias, m_gdn_norm_w, m_q_norm_w, m_k_norm_w, m_rel_bias, m_w_out, m_ffn2_norm, m_ffn2_w_gate, m_ffn2_w_up, m_ffn2_w_down, m_final_norm, v_ffn1_norm, v_ffn1_w_gate, v_ffn1_w_up, v_ffn1_w_down, v_mix_norm, v_w_in, v_conv_w, v_a_log, v_dt_bias, v_gdn_norm_w, v_q_norm_w, v_k_norm_w, v_rel_bias, v_w_out, v_ffn2_norm, v_ffn2_w_gate, v_ffn2_w_up, v_ffn2_w_down, v_final_norm):
    p = dict(locals())
    xs, target = x[0], loss_target[0]
    t, d = xs.shape
    nc = t // CHUNK
    tk = min(TK, t)
    me = 2 * lax.axis_index("x") + lax.axis_index("y")

    first = ["ffn1_w_gate", "ffn1_w_up", "ffn1_w_down"]
    later = [n for n in BIG if n not in first] + ["conv_w"]
    local = lambda n, a: a[0].T if n in COL_SHARDED else a[0]
    shards = {n: local(n, p[n]).astype(BF16) for n in BIG}
    shards["conv_w"] = conv_w[0]
    gw = dict(zip(first, _all_gather([shards[n] for n in first])))
    f1 = (gw["ffn1_w_gate"], gw["ffn1_w_up"], gw["ffn1_w_down"])
    (x1, xn1, g1, u1), landed = _ffn_fwd(xs, ffn1_norm, *f1, TM, "ffn1_fwd",
                                         exchange=_gather_exchange([shards[n] for n in later]))
    gw.update(zip(later, _gather_forward(landed)))
    wp = gw["w_in"].reshape(N_IN_COLS, d)
    w_out_full = gw["w_out"].reshape(d, d)
    conv_rows = conv_w.shape[1]
    cw = jnp.pad(gw["conv_w"].reshape(N_SHARDS * conv_rows, CONV_TAPS).T, ((0, 8 - CONV_TAPS), (0, 0)))
    gp = jnp.pad(jnp.stack([a_log.reshape(8), dt_bias.reshape(8)]), ((0, 6), (0, LANES - 8)))
    gdn_w = gdn_norm_w.reshape(1, GDN_DIM)
    qw_t = jnp.tile(q_norm_w.reshape(1, SWA_DIM), (1, SWA_HEADS))
    kw_t = jnp.tile(k_norm_w.reshape(1, SWA_DIM), (1, SWA_HEADS))
    bd = jnp.asarray(np.kron(np.eye(2), np.full((SWA_DIM, SWA_DIM), 1.0 / SWA_DIM)), BF16)
    f2 = (gw["ffn2_w_gate"], gw["ffn2_w_up"], gw["ffn2_w_down"])

    hn, qkva, z, ab, qkvb = _mix_in_fwd(x1, mix_norm, wp, TE)
    qkvc, gb = _gdn_prep_fwd(qkva, cw, ab, gp, TE)
    gbt = jnp.transpose(gb[:, :16].reshape(nc, CHUNK, 16), (0, 2, 1))
    o_f, o_b, gdn_saved = _gdn_fwd(qkvc, gb, gbt)
    oa = _gdn_post_fwd(o_f, o_b, z, gdn_w, TE)
    o_swa, o_swa16, swa_saved = _swa_branch_fwd(qkvb, qw_t, kw_t, rel_bias, bd, TE)
    x2 = _mix_out_fwd(x1, oa, o_swa, w_out_full, TE)
    (dx3, xn2, g2, u2, loss_part, d_final), _ = _ffn_fwd(x2, ffn2_norm, *f2, TM, "ffn2_fwd", head=(final_norm, target))

    def pair_sums(partials, tag):
        return _add_pairs(partials, _rs_pair(partials), f"rs_add_{tag}")

    (dx2, dyh2, dg2, du2, h2, d_nw2), _ = _ffn_bwd_dx(dx3, x2, ffn2_norm, g2, u2, *f2, TM, "ffn2_bwd_dx")
    dwg2 = _matmul_tn(dg2, xn2, tk, "ffn2_dwg")
    dwu2 = _matmul_tn(du2, xn2, tk, "ffn2_dwu")
    dwd2 = _matmul_tn(h2, dyh2, tk, "ffn2_dwd")
    (doa, dob, dx2b), lands_f2 = _mix_out_bwd(dx2, w_out_full, TE, exchange=_pair_exchange([dwg2, dwu2, dwd2]))
    sums_f2 = _add_pairs([dwg2, dwu2, dwd2], lands_f2, "rs_add_a")
    dwo = jnp.concatenate([_matmul_tn(oa, dx2b, tk, "w_out_dw_a")[0], _matmul_tn(o_swa16, dx2b, tk, "w_out_dw_b")[0]],
                          axis=0).reshape(N_SHARDS, d // N_SHARDS, d)
    do_g, dz, d_gdnw = _gdn_post_bwd(doa, o_f, o_b, z, gdn_w, TE)
    (dqkvc, dgates), slots_f2 = _gdn_bwd(qkvc, gb, gbt, do_g, gdn_saved, exchange=_scatter_exchange(sums_f2))
    dqkva, dab, dcw, dgp = _gdn_prep_bwd(qkva, cw, ab, gp, dqkvc, dgates, TE)
    dqkvb, d_qw, d_kw, d_rel = _swa_branch_bwd(dob, o_swa, swa_saved, qkvb, qw_t, kw_t, bd, TE)
    dpieces = (dqkva, dz, dab, dqkvb)
    dwp = [_matmul_tn(dp, hn, tk, f"w_in_dw_{i}")[0] for i, dp in enumerate(dpieces)]
    dw_in = jnp.concatenate([dwp[0], dwp[1], dwp[2][:N_GATE_COLS], dwp[3]], axis=0)
    dw_in = dw_in.reshape(N_SHARDS, N_IN_COLS // N_SHARDS, d)
    sums_mix = pair_sums([dw_in, dwo], "b")
    (dx1, d_mixnw), slots_mix = _mix_in_bwd_dx(dx2, x1, mix_norm, dpieces, wp, TE, exchange=_scatter_exchange(sums_mix))
    (gx, dyh1, dg1, du1, h1, d_nw1), _ = _ffn_bwd_dx(dx1, xs, ffn1_norm, g1, u1, *f1, TM, "ffn1_bwd_dx")
    dwg1 = _matmul_tn(dg1, xn1, tk, "ffn1_dwg")
    dwu1 = _matmul_tn(du1, xn1, tk, "ffn1_dwu")
    sums_gu = pair_sums([dwg1, dwu1], "c")
    dwd1, slots_gu = _matmul_tn(h1, dyh1, tk, "ffn1_dwd", exchange=_scatter_exchange(sums_gu))
    sums_d = pair_sums([dwd1], "d")
    slots = slots_gu + _rs_chips(sums_d) + slots_mix + slots_f2
    sums = sums_gu + sums_d + sums_mix + sums_f2
    halves = _sum_slots(slots[:4], sums[:4], "rs_sum_a") + _sum_slots(slots[4:], sums[4:], "rs_sum_b")
    g_big = dict(zip(BIG, _rs_join(halves, [_split_axis(shards[n].shape) for n in BIG])))

    small_partial = {"ffn1_norm": d_nw1, "mix_norm": d_mixnw, "a_log": dgp[0, 0:8], "dt_bias": dgp[1, 0:8],
                     "gdn_norm_w": d_gdnw, "q_norm_w": d_qw, "k_norm_w": d_kw, "rel_bias": d_rel,
                     "ffn2_norm": d_nw2, "final_norm": d_final, "conv_w": dcw[0:CONV_TAPS].T}
    red = _all_reduce_small(_pack([small_partial[n] for n in SMALL] + [loss_part[0, 0:1]]))
    full_shapes = [p[n].shape if n != "conv_w" else (N_SHARDS * conv_rows, CONV_TAPS) for n in SMALL]
    red_parts = _unpack(red, full_shapes + [(1,)])
    loss = red_parts[-1].reshape(())
    g_small = dict(zip(SMALL, red_parts[:-1]))
    g_small["conv_w"] = lax.dynamic_slice_in_dim(g_small["conv_w"], me * conv_rows, conv_rows, 0).reshape(conv_w.shape)

    grads, deltas, new_m, new_v = {}, {}, {}, {}
    quad = lambda n: (local(n, p[n]), g_big[n], local(n, p["m_" + n]), local(n, p["v_" + n]))
    updates = (_adamw([quad(n) for n in BIG[:4]], "adamw_a", ADAM_PIECES)
               + _adamw([quad(n) for n in BIG[4:]], "adamw_b", ADAM_PIECES))
    for n, (dl, nm, nv) in zip(BIG, updates):
        back = (lambda a: a.T[None]) if n in COL_SHARDED else (lambda a: a[None])
        grads[n], deltas[n], new_m[n], new_v[n] = back(g_big[n]), back(dl), back(nm), back(nv)
    packed = [_pack([src[n] for n in SMALL]) for src in
              ({n: p[n] for n in SMALL}, g_small, {n: p["m_" + n] for n in SMALL}, {n: p["v_" + n] for n in SMALL})]
    small_shapes = [p[n].shape for n in SMALL]
    for dst, arr in zip((deltas, new_m, new_v), _adamw([tuple(packed)], "adamw_small", 1)[0]):
        dst.update(zip(SMALL, _unpack(arr, small_shapes)))
    grads.update(g_small)

    return (loss, gx[None], *[grads[n] for n in WEIGHTS], *[deltas[n] for n in WEIGHTS],
            *[new_m[n] for n in WEIGHTS], *[new_v[n] for n in WEIGHTS])
```

```python
import math
from typing import Callable, NamedTuple

import numpy as np
import jax
import jax.numpy as jnp
from jax import lax
from jax.experimental import pallas as pl
from jax.experimental.pallas import tpu as pltpu

F32 = jnp.float32
BF16 = jnp.bfloat16
MESH = pl.DeviceIdType.MESH

EPS = 1e-6
NEG_BIG = -1e30
GDN_HEADS = 4
GDN_DIM = 128
CHUNK = 64
SWA_HEADS = 8
SWA_DIM = 64
PATTERNS = ((128, 1), (512, 4), (2048, 16))
RADIUS = 64
REL_BUCKETS = 32
REL_MAX_DISTANCE = 1024
CONV_TAPS = 5
N_SHARDS = 4
LANES = 128
VMEM_LIMIT = 56 * 1024 * 1024

ADAM_LR, ADAM_B1, ADAM_B2, ADAM_EPS, ADAM_WD, ADAM_STEP = 0.001, 0.9, 0.999, 1e-08, 0.01, 10


def _params(sem=None, vmem=None):
    return pltpu.CompilerParams(dimension_semantics=sem, vmem_limit_bytes=vmem)


def _resident(shape):
    nd = len(shape)
    return pl.BlockSpec(shape, lambda *_: (0,) * nd, pipeline_mode=pl.Buffered(1))


ANY = pl.BlockSpec(memory_space=pl.ANY)


class _Exchange(NamedTuple):
    arrays: tuple
    out_shape: tuple
    start: Callable
    finish: Callable


def _grid_call(body, name, nsteps, in_specs, out_specs, out_shape, operands, scratch=(), exchange=None):
    params = _params(("arbitrary",), VMEM_LIMIT)
    if exchange is None:
        res = pl.pallas_call(body, name=name, grid=(nsteps,), in_specs=list(in_specs), out_specs=list(out_specs),
                             out_shape=list(out_shape), scratch_shapes=list(scratch), compiler_params=params)(*operands)
        return list(res), []
    n_in, n_out, k, n_scr = len(in_specs), len(out_specs), len(exchange.arrays), len(scratch)

    def wrapped(*refs):
        ins, cin = refs[:n_in], refs[n_in:n_in + k]
        outs, cout = refs[n_in + k:n_in + k + n_out], refs[n_in + k + n_out:n_in + 2 * k + n_out]
        rest = refs[n_in + 2 * k + n_out:]
        scr, (send_sems, recv_sems) = rest[:n_scr], rest[n_scr:]

        @pl.when(pl.program_id(0) == 0)
        def _():
            exchange.start(cin, cout, send_sems, recv_sems)

        body(*ins, *outs, *scr)

        @pl.when(pl.program_id(0) == nsteps - 1)
        def _():
            exchange.finish(cin, cout, send_sems, recv_sems)

    res = pl.pallas_call(
        wrapped, name=name, grid=(nsteps,), in_specs=list(in_specs) + [ANY] * k, out_specs=list(out_specs) + [ANY] * k,
        out_shape=list(out_shape) + list(exchange.out_shape),
        scratch_shapes=list(scratch) + [pltpu.SemaphoreType.DMA((k, 4)), pltpu.SemaphoreType.DMA((k, 4))],
        compiler_params=params)(*operands, *exchange.arrays)
    return list(res[:n_out]), list(res[n_out:])


def _dot(a, b):
    return jnp.dot(a.astype(BF16), b.astype(BF16), preferred_element_type=F32)


def _dot_nt(a, b):
    return lax.dot_general(a.astype(BF16), b.astype(BF16), (((1,), (1,)), ((), ())), preferred_element_type=F32)


def _dot_tn(a, b):
    return lax.dot_general(a.astype(BF16), b.astype(BF16), (((0,), (0,)), ((), ())), preferred_element_type=F32)


def _sigmoid(x):
    return 1.0 / (1.0 + jnp.exp(-x))


def _rstd(xf):
    return lax.rsqrt(jnp.mean(xf * xf, axis=-1, keepdims=True) + EPS)


def _rms_bwd(xf, r, nw, dxn):
    xhat = xf * r
    dxh = dxn * nw
    dx = r * (dxh - xhat * jnp.mean(dxh * xhat, axis=-1, keepdims=True))
    return dx, jnp.sum(dxn * xhat, axis=0, keepdims=True)


def _ffn_fwd(x, nw, wg, wu, wd, tm, name, exchange=None, head=None):
    t, d = x.shape
    nj, fs, _ = wg.shape

    def body(x_ref, nw_ref, wg_ref, wu_ref, wd_ref, *rest):
        if head is None:
            y_ref, xn_ref, g_ref, u_ref = rest
        else:
            fw_ref, t_ref, y_ref, xn_ref, g_ref, u_ref, loss_ref, dfw_ref = rest

            @pl.when(pl.program_id(0) == 0)
            def _():
                loss_ref[...] = jnp.zeros_like(loss_ref)
                dfw_ref[...] = jnp.zeros_like(dfw_ref)

        xf = x_ref[...]
        xn = (xf * _rstd(xf) * nw_ref[...]).astype(BF16)
        xn_ref[...] = xn
        acc = jnp.zeros((tm, d), F32)
        for j in range(nj):
            g = _dot_nt(xn, wg_ref[j])
            u = _dot_nt(xn, wu_ref[j])
            h = (g * _sigmoid(g) * u).astype(BF16)
            acc = acc + jnp.dot(h, wd_ref[j], preferred_element_type=F32)
            g_ref[j] = g.astype(BF16)
            u_ref[j] = u.astype(BF16)
        y = xf + 0.5 * acc
        if head is None:
            y_ref[...] = y
        else:
            r = _rstd(y)
            err = y * r * fw_ref[...] - t_ref[...]
            loss_ref[...] += 0.5 * jnp.sum(jnp.mean(err * err, axis=-1, keepdims=True), axis=0, keepdims=True)
            dy, dfw = _rms_bwd(y, r, fw_ref[...], err * (1.0 / d))
            y_ref[...] = dy
            dfw_ref[...] += dfw

    row = pl.BlockSpec((tm, d), lambda i: (i, 0))
    act = pl.BlockSpec((nj, tm, fs), lambda i: (0, i, 0))
    in_specs = [row, _resident((1, d)), _resident(wg.shape), _resident(wu.shape), _resident(wd.shape)]
    out_specs = [row, row, act, act]
    out_shape = [jax.ShapeDtypeStruct((t, d), F32), jax.ShapeDtypeStruct((t, d), BF16),
                 jax.ShapeDtypeStruct((nj, t, fs), BF16), jax.ShapeDtypeStruct((nj, t, fs), BF16)]
    operands = (x, nw, wg, wu, wd)
    if head is not None:
        in_specs += [_resident((1, d)), row]
        out_specs += [pl.BlockSpec((1, LANES), lambda i: (0, 0)), pl.BlockSpec((1, d), lambda i: (0, 0))]
        out_shape += [jax.ShapeDtypeStruct((1, LANES), F32), jax.ShapeDtypeStruct((1, d), F32)]
        operands += tuple(head)
    return _grid_call(body, name, t // tm, in_specs, out_specs, out_shape, operands, exchange=exchange)


def _ffn_bwd_dx(dy, x, nw, g, u, wg, wu, wd, tm, name, exchange=None):
    t, d = x.shape
    nj, fs, _ = wg.shape

    def body(dy_ref, x_ref, nw_ref, g_ref, u_ref, wg_ref, wu_ref, wd_ref,
             dx_ref, dyh_ref, dg_ref, du_ref, h_ref, dnw_ref):
        @pl.when(pl.program_id(0) == 0)
        def _():
            dnw_ref[...] = jnp.zeros_like(dnw_ref)

        dyv = dy_ref[...]
        dyh = (0.5 * dyv).astype(BF16)
        dyh_ref[...] = dyh
        dxn = jnp.zeros((tm, d), F32)
        dh_next = _dot_nt(dyh, wd_ref[0])
        for j in range(nj):
            dh = dh_next
            gv = g_ref[j].astype(F32)
            uv = u_ref[j].astype(F32)
            sg = _sigmoid(gv)
            si = gv * sg
            dg = (dh * uv * (sg * (1.0 + gv * (1.0 - sg)))).astype(BF16)
            du = (dh * si).astype(BF16)
            if j + 1 < nj:
                dh_next = _dot_nt(dyh, wd_ref[j + 1])
            h_ref[j] = (si * uv).astype(BF16)
            dg_ref[j] = dg
            du_ref[j] = du
            dxn = dxn + _dot(dg, wg_ref[j]) + _dot(du, wu_ref[j])
        xf = x_ref[...]
        dxr, dnw = _rms_bwd(xf, _rstd(xf), nw_ref[...], dxn)
        dx_ref[...] = dyv + dxr
        dnw_ref[...] += dnw

    row = pl.BlockSpec((tm, d), lambda i: (i, 0))
    act = pl.BlockSpec((nj, tm, fs), lambda i: (0, i, 0))
    act_shape = jax.ShapeDtypeStruct((nj, t, fs), BF16)
    return _grid_call(
        body, name, t // tm,
        [row, row, _resident((1, d)), act, act, _resident(wg.shape), _resident(wu.shape), _resident(wd.shape)],
        [row, row, act, act, act, pl.BlockSpec((1, d), lambda i: (0, 0))],
        [jax.ShapeDtypeStruct((t, d), F32), jax.ShapeDtypeStruct((t, d), BF16),
         act_shape, act_shape, act_shape, jax.ShapeDtypeStruct((1, d), F32)],
        (dy, x, nw, g, u, wg, wu, wd), exchange=exchange)


def _matmul_tn(a, b, tk, name, exchange=None):
    a3, b3 = a.ndim == 3, b.ndim == 3
    nj = a.shape[0] if a3 else (b.shape[0] if b3 else 1)
    t, m = a.shape[-2:]
    n = b.shape[-1]
    nt = t // tk

    def body(a_ref, b_ref, o_ref, acc_ref):
        k = pl.program_id(0) % nt

        @pl.when(k == 0)
        def _():
            acc_ref[...] = jnp.zeros_like(acc_ref)

        acc_ref[...] += lax.dot_general(a_ref[...], b_ref[...], (((0,), (0,)), ((), ())),
                                        preferred_element_type=F32)

        @pl.when(k == nt - 1)
        def _():
            o_ref[...] = acc_ref[...].astype(o_ref.dtype)

    a_spec = (pl.BlockSpec((None, tk, m), lambda i: (i // nt, i % nt, 0)) if a3
              else pl.BlockSpec((tk, m), lambda i: (i % nt, 0)))
    b_spec = (pl.BlockSpec((None, tk, n), lambda i: (i // nt, i % nt, 0)) if b3
              else pl.BlockSpec((tk, n), lambda i: (i % nt, 0)))
    (out,), landed = _grid_call(
        body, name, nj * nt, [a_spec, b_spec], [pl.BlockSpec((None, m, n), lambda i: (i // nt, 0, 0))],
        [jax.ShapeDtypeStruct((nj, m, n), BF16)], (a, b), scratch=[pltpu.VMEM((m, n), F32)], exchange=exchange)
    return out if exchange is None else (out, landed)


N_GATE_COLS = 4 * GDN_HEADS
P_QKVA, P_Z, P_AB, P_QKVB = (0, 1536), (1536, 2048), (2048, 2048 + LANES), (2048 + N_GATE_COLS, 3600)
P_PIECES = (P_QKVA, P_Z, P_AB, P_QKVB)


def _mix_in_fwd(x1, nw, wp, tm):
    t, d = x1.shape

    def body(x_ref, nw_ref, w_ref, hn_ref, *outs):
        xf = x_ref[...]
        xn = (xf * _rstd(xf) * nw_ref[...]).astype(BF16)
        hn_ref[...] = xn
        for (a, b), o_ref in zip(P_PIECES, outs):
            o_ref[...] = _dot_nt(xn, w_ref[a:b, :])

    row = pl.BlockSpec((tm, d), lambda i: (i, 0))
    return pl.pallas_call(
        body, name="mix_in_fwd", grid=(t // tm,),
        in_specs=[row, _resident((1, d)), _resident(wp.shape)],
        out_specs=[row] + [pl.BlockSpec((tm, b - a), lambda i: (i, 0)) for a, b in P_PIECES],
        out_shape=[jax.ShapeDtypeStruct((t, d), BF16)]
                  + [jax.ShapeDtypeStruct((t, b - a), F32) for a, b in P_PIECES],
        compiler_params=_params(("arbitrary",), VMEM_LIMIT),
    )(x1, nw, wp)


def _mix_in_bwd_dx(dx, x1, nw, dpieces, wp, tm, exchange=None):
    t, d = x1.shape

    def body(dx_ref, x_ref, nw_ref, p0, p1, p2, p3, w_ref, o_ref, dnw_ref):
        @pl.when(pl.program_id(0) == 0)
        def _():
            dnw_ref[...] = jnp.zeros_like(dnw_ref)

        dh = jnp.zeros((tm, d), F32)
        for (a, b), p_ref in zip(P_PIECES, (p0, p1, p2, p3)):
            dh = dh + _dot(p_ref[...], w_ref[a:b, :])
        xf = x_ref[...]
        dxr, dnw = _rms_bwd(xf, _rstd(xf), nw_ref[...], dh)
        o_ref[...] = dx_ref[...] + dxr
        dnw_ref[...] += dnw

    row = pl.BlockSpec((tm, d), lambda i: (i, 0))
    return _grid_call(
        body, "mix_in_bwd_dx", t // tm,
        [row, row, _resident((1, d))]
        + [pl.BlockSpec((tm, b - a), lambda i: (i, 0)) for a, b in P_PIECES] + [_resident(wp.shape)],
        [row, pl.BlockSpec((1, d), lambda i: (0, 0))],
        [jax.ShapeDtypeStruct((t, d), F32), jax.ShapeDtypeStruct((1, d), F32)],
        (dx, x1, nw, *dpieces, wp), exchange=exchange)


def _mix_out_fwd(x1, oa, ob, w, tm):
    t, d = x1.shape
    half = oa.shape[1]

    def body(x_ref, oa_ref, ob_ref, w_ref, o_ref):
        o_ref[...] = (x_ref[...] + _dot(oa_ref[...], w_ref[0:half, :]) + _dot(ob_ref[...], w_ref[half:2 * half, :]))

    row = pl.BlockSpec((tm, d), lambda i: (i, 0))
    hrow = pl.BlockSpec((tm, half), lambda i: (i, 0))
    return pl.pallas_call(
        body, name="mix_out_fwd", grid=(t // tm,),
        in_specs=[row, hrow, hrow, _resident(w.shape)],
        out_specs=row, out_shape=jax.ShapeDtypeStruct((t, d), F32),
        compiler_params=_params(("arbitrary",), VMEM_LIMIT),
    )(x1, oa, ob, w)


def _mix_out_bwd(dx2, w, tm, exchange=None):
    t, d = dx2.shape
    half = w.shape[0] // 2

    def body(dx_ref, w_ref, doa_ref, dob_ref, dxb_ref):
        dxb = dx_ref[...].astype(BF16)
        dxb_ref[...] = dxb
        doa_ref[...] = _dot_nt(dxb, w_ref[0:half, :])
        dob_ref[...] = _dot_nt(dxb, w_ref[half:2 * half, :])

    row = pl.BlockSpec((tm, d), lambda i: (i, 0))
    hrow = pl.BlockSpec((tm, half), lambda i: (i, 0))
    return _grid_call(
        body, "mix_out_bwd", t // tm, [row, _resident(w.shape)], [hrow, hrow, row],
        [jax.ShapeDtypeStruct((t, half), F32), jax.ShapeDtypeStruct((t, half), F32), jax.ShapeDtypeStruct((t, d), BF16)],
        (dx2, w), exchange=exchange)


HALO = 8


def _halo_row_specs(tr, cols, nrow8):
    per = tr // HALO
    return [pl.BlockSpec((tr, cols), lambda i: (i, 0)),
            pl.BlockSpec((HALO, cols), lambda i: (jnp.maximum(i * per - 1, 0), 0)),
            pl.BlockSpec((HALO, cols), lambda i: (jnp.minimum((i + 1) * per, nrow8 - 1), 0))]


def _fill_window(win_ref, cb, xm, xp, xn, first, last):
    tr = xm.shape[0]
    cols = slice(cb * LANES, (cb + 1) * LANES)
    win_ref[cb, 0:HALO, :] = jnp.where(first, 0.0, xp[:, cols])
    win_ref[cb, HALO:HALO + tr, :] = xm[:, cols]
    win_ref[cb, HALO + tr:HALO + tr + HALO, :] = jnp.where(last, 0.0, xn[:, cols])


def _conv_taps(win_ref, cb, cw_ref, start, rows):
    cols = slice(cb * LANES, (cb + 1) * LANES)
    acc = None
    for j in range(CONV_TAPS):
        term = win_ref[cb, pl.ds(start + j - CONV_TAPS // 2, rows), :] * cw_ref[j:j + 1, cols]
        acc = term if acc is None else acc + term
    return acc


def _softplus(x):
    u = jnp.exp(-jnp.abs(x))
    w = 1.0 + u
    log1p = jnp.where(w == 1.0, u, jnp.log(w) * u / jnp.where(w == 1.0, 1.0, w - 1.0))
    return jnp.maximum(x, 0.0) + log1p


def _gdn_prep_fwd(qkva, cw, ab, gp, tr):
    t, c = qkva.shape
    nt = t // tr
    ncb = c // LANES

    def body(xm, xp, xn, cw_ref, ab_ref, gp_ref, o_ref, gb_ref, xw_ref):
        i = pl.program_id(0)
        first, last = i == 0, i == nt - 1
        for cb in range(ncb):
            cols = slice(cb * LANES, (cb + 1) * LANES)
            _fill_window(xw_ref, cb, xm, xp, xn, first, last)
            pre = _conv_taps(xw_ref, cb, cw_ref, HALO, tr)
            y = pre * _sigmoid(pre)
            if cb < 2 * GDN_HEADS:
                y = y * lax.rsqrt(jnp.sum(y * y, axis=-1, keepdims=True) + EPS)
            if cb < GDN_HEADS:
                y = y * (GDN_DIM ** -0.5)
            o_ref[:, cols] = y
        abv = ab_ref[...]
        lane = lax.broadcasted_iota(jnp.int32, abv.shape, 1)
        g = -jnp.exp(gp_ref[0:1, :]) * _softplus(abv + gp_ref[1:2, :])
        gb_ref[...] = jnp.where(lane < 8, g, jnp.where(lane < 16, _sigmoid(abv), 0.0))

    return pl.pallas_call(
        body, name="gdn_prep_fwd", grid=(nt,),
        in_specs=_halo_row_specs(tr, c, t // HALO)
                 + [_resident(cw.shape), pl.BlockSpec((tr, LANES), lambda i: (i, 0)), _resident(gp.shape)],
        out_specs=[pl.BlockSpec((tr, c), lambda i: (i, 0)), pl.BlockSpec((tr, LANES), lambda i: (i, 0))],
        out_shape=[jax.ShapeDtypeStruct((t, c), F32), jax.ShapeDtypeStruct((t, LANES), F32)],
        scratch_shapes=[pltpu.VMEM((ncb, tr + 2 * HALO, LANES), F32)],
        compiler_params=_params(("arbitrary",), VMEM_LIMIT),
    )(qkva, qkva, qkva, cw, ab, gp)


def _gdn_prep_bwd(qkva, cw, ab, gp, dy, dgates, tr):
    t, c = qkva.shape
    nt = t // tr
    ncb = c // LANES

    ext = HALO // 2
    rows_ext = tr + 2 * ext

    def body(xm, xp, xn, fm, fp, fn, cw_ref, ab_ref, gp_ref, gf_ref, dx_ref, dab_ref, dcw_ref, dgp_ref,
             xw_ref, dyw_ref, dp_ref):
        i = pl.program_id(0)
        first, last = i == 0, i == nt - 1

        @pl.when(first)
        def _():
            dcw_ref[...] = jnp.zeros_like(dcw_ref)
            dgp_ref[...] = jnp.zeros_like(dgp_ref)

        sub8 = lax.broadcasted_iota(jnp.int32, (8, LANES), 0)
        for cb in range(ncb):
            cols = slice(cb * LANES, (cb + 1) * LANES)
            _fill_window(xw_ref, cb, xm, xp, xn, first, last)
            _fill_window(dyw_ref, cb, fm, fp, fn, first, last)
            pre = _conv_taps(xw_ref, cb, cw_ref, HALO - ext, rows_ext)
            dyw = dyw_ref[cb, pl.ds(HALO - ext, rows_ext), :]
            sg = _sigmoid(pre)
            s = pre * sg
            if cb < 2 * GDN_HEADS:
                scale = (GDN_DIM ** -0.5) if cb < GDN_HEADS else 1.0
                r = lax.rsqrt(jnp.sum(s * s, axis=-1, keepdims=True) + EPS)
                dn = dyw * scale
                ds = r * dn - s * (r * r * r) * jnp.sum(dn * s, axis=-1, keepdims=True)
            else:
                ds = dyw
            dp_ref[cb] = ds * (sg * (1.0 + pre * (1.0 - sg)))
            dpre = dp_ref[cb, pl.ds(ext, tr), :]
            dx = None
            dcw = jnp.zeros((8, LANES), F32)
            for j in range(CONV_TAPS):
                off = j - CONV_TAPS // 2
                term = dp_ref[cb, pl.ds(ext - off, tr), :] * cw_ref[j:j + 1, cols]
                dx = term if dx is None else dx + term
                tap = jnp.sum(dpre * xw_ref[cb, pl.ds(HALO + off, tr), :], axis=0, keepdims=True)
                dcw = dcw + jnp.where(sub8 == j, tap, 0.0)
            dx_ref[:, cols] = dx.astype(BF16)
            dcw_ref[:, cols] += dcw

        abv = ab_ref[...]
        dgb = gf_ref[...]
        lane = lax.broadcasted_iota(jnp.int32, abv.shape, 1)
        nea = -jnp.exp(gp_ref[0:1, :])
        xs = abv + gp_ref[1:2, :]
        g = nea * _softplus(xs)
        beta = _sigmoid(abv)
        da = dgb * nea * _sigmoid(xs)
        dab = jnp.where(lane < 8, da, jnp.where(lane < 16, dgb * beta * (1.0 - beta), 0.0))
        dab_ref[...] = dab.astype(BF16)
        keep = lane[0:1, :] < 8
        dalog = jnp.where(keep, jnp.sum(dgb * g, axis=0, keepdims=True), 0.0)
        ddtb = jnp.where(keep, jnp.sum(da, axis=0, keepdims=True), 0.0)
        dgp_ref[...] += jnp.where(sub8 == 0, dalog, 0.0) + jnp.where(sub8 == 1, ddtb, 0.0)

    lrow = pl.BlockSpec((tr, LANES), lambda i: (i, 0))
    halo = _halo_row_specs(tr, c, t // HALO)
    return pl.pallas_call(
        body, name="gdn_prep_bwd", grid=(nt,),
        in_specs=halo + halo + [_resident(cw.shape), lrow, _resident(gp.shape), lrow],
        out_specs=[pl.BlockSpec((tr, c), lambda i: (i, 0)), lrow,
                   pl.BlockSpec(cw.shape, lambda i: (0, 0)), pl.BlockSpec(gp.shape, lambda i: (0, 0))],
        out_shape=[jax.ShapeDtypeStruct((t, c), BF16), jax.ShapeDtypeStruct((t, LANES), BF16),
                   jax.ShapeDtypeStruct(cw.shape, F32), jax.ShapeDtypeStruct(gp.shape, F32)],
        scratch_shapes=[pltpu.VMEM((ncb, tr + 2 * HALO, LANES), F32), pltpu.VMEM((ncb, tr + 2 * HALO, LANES), F32),
                        pltpu.VMEM((ncb, rows_ext, LANES), F32)],
        compiler_params=_params(("arbitrary",), VMEM_LIMIT),
    )(qkva, qkva, qkva, dy, dy, dy, cw, ab, gp, dgates)


def _chunk_masks(lower):
    ii = lax.broadcasted_iota(jnp.int32, (CHUNK, CHUNK), 0)
    jj = lax.broadcasted_iota(jnp.int32, (CHUNK, CHUNK), 1)
    incl = (ii >= jj) if lower else (ii <= jj)
    strict = (ii > jj) if lower else (ii < jj)
    return ii, jj, incl, strict


def _dot3(a, b):
    ah = a.astype(BF16)
    al = (a - ah.astype(F32)).astype(BF16)
    bh = b.astype(BF16)
    bl = (b - bh.astype(F32)).astype(BF16)
    d = lambda u, v: jnp.dot(u, v, preferred_element_type=F32)
    return d(ah, bh) + (d(ah, bl) + d(al, bh))


def _tri_inv_many(lmats, ii, jj):
    m16 = (ii // 16) == (jj // 16)
    m32 = (ii // 32) == (jj // 32)
    eye = jnp.where(ii == jj, 1.0, 0.0)
    l16 = [jnp.where(m16, l, 0.0) for l in lmats]
    p2 = [_dot3(a, a) for a in l16]
    p4 = [_dot3(a, a) for a in p2]
    p8 = [_dot3(a, a) for a in p4]
    xs = [eye - a for a in l16]
    for ps in (p2, p4, p8):
        xs = [x + _dot3(x, p) for x, p in zip(xs, ps)]
    for off in ([jnp.where(m32 & jnp.logical_not(m16), l, 0.0) for l in lmats],
                [jnp.where(m32, 0.0, l) for l in lmats]):
        ys = [_dot3(x, c) for x, c in zip(xs, off)]
        xs = [x - _dot3(y, x) for x, y in zip(xs, ys)]
    return xs


def _col_to_row(col, ii, jj):
    return jnp.sum(jnp.where(ii == jj, col, 0.0), axis=0, keepdims=True)


def _row_to_col(row, ii, jj):
    return jnp.sum(jnp.where(ii == jj, row, 0.0), axis=1, keepdims=True)


def _chain_common(q, k, v, graw_col, graw_row, bcol, masks):
    ii, jj, incl, strict = masks
    inclt = jnp.logical_not(strict)
    gcol = jnp.sum(jnp.where(incl, graw_row, 0.0), axis=1, keepdims=True)
    grow = jnp.sum(jnp.where(inclt, graw_col, 0.0), axis=0, keepdims=True)
    glast = jnp.sum(graw_row, axis=1, keepdims=True)
    decay = jnp.where(incl, jnp.exp(jnp.where(incl, gcol - grow, 0.0)), 0.0)
    kb = k * bcol
    vb = v * bcol
    eg = jnp.exp(gcol)
    ek = jnp.exp(glast - gcol)
    kbg = kb * eg
    amat = _dot_nt(kb, k)
    qk = _dot_nt(q, k)
    return dict(gcol=gcol, glast=glast, decay=decay, kb=kb, vb=vb, eg=eg, ek=ek, kbg=kbg, amat=amat, qk=qk,
                intra=qk * decay, qg=q * eg, kdec=k * ek)


def _gdn_fwd(qkvc, gb, gbt):
    tm, u, w, qg, kd, intra, egl = _gdn_local_fwd(qkvc, gb, gbt)
    o_f, o_b, s_f, s_b, vn_f, vn_b = _gdn_scan_fwd(u, w, qg, kd, intra, egl, qkvc.shape[0])
    return o_f, o_b, dict(tm=tm, w=w, qg=qg, kd=kd, intra=intra, egl=egl, s=(s_f, s_b), vn=(vn_f, vn_b))


N_CHAINS = 2 * GDN_HEADS


LOCAL_CHUNKS = 4


def _load_chains(x_ref, g_ref, gt_ref, cc=0):
    hd = GDN_HEADS * GDN_DIM
    rows = slice(cc * CHUNK, (cc + 1) * CHUNK)
    chains = []
    for d in range(2):
        masks = _chunk_masks(d == 0)
        for h in range(GDN_HEADS):
            ch = d * GDN_HEADS + h
            q = x_ref[rows, h * GDN_DIM:(h + 1) * GDN_DIM]
            k = x_ref[rows, hd + h * GDN_DIM:hd + (h + 1) * GDN_DIM]
            v = x_ref[rows, 2 * hd + h * GDN_DIM:2 * hd + (h + 1) * GDN_DIM]
            bcol = g_ref[rows, 8 + ch:9 + ch]
            cm = _chain_common(q, k, v, g_ref[rows, ch:ch + 1], gt_ref[cc, ch:ch + 1, :], bcol, masks)
            chains.append(dict(cm, q=q, k=k, v=v, bcol=bcol, masks=masks, ch=ch, h=h, cc=cc))
    return chains


def _chain_shape(rows, cols, dtype):
    return lambda nc: jax.ShapeDtypeStruct((nc, N_CHAINS, rows, cols), dtype)


def _gdn_local_fwd(qkvc, gb, gbt):
    t = qkvc.shape[0]
    nc = t // CHUNK
    hd = GDN_HEADS * GDN_DIM

    def body(x_ref, g_ref, gt_ref, t_ref, u_ref, w_ref, qg_ref, kd_ref, in_ref, eg_ref):
        chains = [c for cc in range(LOCAL_CHUNKS) for c in _load_chains(x_ref, g_ref, gt_ref, cc)]
        ii, jj = chains[0]["masks"][0:2]
        tms = _tri_inv_many([jnp.where(c["masks"][3], c["amat"] * c["decay"], 0.0) for c in chains], ii, jj)
        uws = [_dot(tm, jnp.concatenate([c["vb"], c["kbg"]], axis=1)) for tm, c in zip(tms, chains)]
        for c, tm, uw in zip(chains, tms, uws):
            cc, ch = c["cc"], c["ch"]
            t_ref[cc, ch] = tm
            u_ref[cc, ch] = uw[:, :GDN_DIM]
            w_ref[cc, ch] = uw[:, GDN_DIM:].astype(BF16)
            qg_ref[cc, ch] = c["qg"].astype(BF16)
            kd_ref[cc, ch] = c["kdec"].astype(BF16)
            in_ref[cc, ch] = c["intra"].astype(BF16)
            eg_ref[cc, ch:ch + 1, :] = jnp.broadcast_to(jnp.exp(c["glast"]), (1, LANES))

    lc = LOCAL_CHUNKS
    blk = lambda rows, cols: pl.BlockSpec((lc, N_CHAINS, rows, cols), lambda n: (n, 0, 0, 0))
    shapes = [_chain_shape(CHUNK, CHUNK, F32), _chain_shape(CHUNK, GDN_DIM, F32), _chain_shape(CHUNK, GDN_DIM, BF16),
              _chain_shape(CHUNK, GDN_DIM, BF16), _chain_shape(CHUNK, GDN_DIM, BF16), _chain_shape(CHUNK, CHUNK, BF16)]
    return tuple(pl.pallas_call(
        body, name="gdn_local_fwd", grid=(nc // lc,),
        in_specs=[pl.BlockSpec((lc * CHUNK, 3 * hd), lambda n: (n, 0)), pl.BlockSpec((lc * CHUNK, LANES), lambda n: (n, 0)),
                  pl.BlockSpec((lc, 16, CHUNK), lambda n: (n, 0, 0))],
        out_specs=[blk(CHUNK, CHUNK), blk(CHUNK, GDN_DIM), blk(CHUNK, GDN_DIM), blk(CHUNK, GDN_DIM),
                   blk(CHUNK, GDN_DIM), blk(CHUNK, CHUNK), pl.BlockSpec((lc, N_CHAINS, LANES), lambda n: (n, 0, 0))],
        out_shape=[s(nc) for s in shapes] + [jax.ShapeDtypeStruct((nc, N_CHAINS, LANES), F32)],
        compiler_params=_params(("arbitrary",), VMEM_LIMIT),
    )(qkvc, gb, gbt))


SCAN_CHUNKS = 8


def _dir_specs(nc, rev):
    nb = nc // SCAN_CHUNKS

    def spec(d, rows, cols, own=False):
        chunk = (lambda n: n) if (d == 0) != rev else (lambda n: nb - 1 - n)
        blk = 0 if own else d
        if rows is None:
            return pl.BlockSpec((SCAN_CHUNKS, GDN_HEADS if own else N_CHAINS, cols), lambda n: (chunk(n), 0, 0))
        return pl.BlockSpec((SCAN_CHUNKS, GDN_HEADS, rows, cols), lambda n: (chunk(n), blk, 0, 0))

    def rows_spec(d, cols):
        chunk = (lambda n: n) if (d == 0) != rev else (lambda n: nb - 1 - n)
        return pl.BlockSpec((SCAN_CHUNKS * CHUNK, cols), lambda n: (chunk(n), 0))

    def order(d):
        return list(range(SCAN_CHUNKS)) if (d == 0) != rev else list(range(SCAN_CHUNKS - 1, -1, -1))
    return spec, rows_spec, order


def _gdn_scan_fwd(u, w, qg, kd, intra, egl, t):
    nc = t // CHUNK
    hd = GDN_HEADS * GDN_DIM

    def body(*refs):
        ins, outs, state = refs[:12], refs[12:18], refs[18]
        @pl.when(pl.program_id(0) == 0)
        def _():
            state[...] = jnp.zeros_like(state)

        chains = [(d, h) for d in range(2) for h in range(GDN_HEADS)]
        states = [state[ch] for ch in range(N_CHAINS)]
        for step in range(SCAN_CHUNKS):
            at = [order(d)[step] for d in range(2)]
            pick = lambda k, d, h: ins[2 * k + d][at[d], h]
            sbs = [s.astype(BF16) for s in states]
            ws = [_dot(pick(1, d, h), sb) for (d, h), sb in zip(chains, sbs)]
            o1 = [_dot(pick(2, d, h), sb) for (d, h), sb in zip(chains, sbs)]
            vns = [(pick(0, d, h) - wsb).astype(BF16) for (d, h), wsb in zip(chains, ws)]
            o2 = [_dot(pick(4, d, h), vn) for (d, h), vn in zip(chains, vns)]
            kv = [_dot_tn(pick(3, d, h), vn) for (d, h), vn in zip(chains, vns)]
            new_states = []
            for ch, (d, h) in enumerate(chains):
                outs[d][at[d] * CHUNK:(at[d] + 1) * CHUNK, h * GDN_DIM:(h + 1) * GDN_DIM] = o1[ch] + o2[ch]
                outs[2 + d][at[d], h] = states[ch]
                outs[4 + d][at[d], h] = vns[ch]
                new_states.append(states[ch] * ins[10 + d][at[d], ch:ch + 1, :] + kv[ch])
            states = new_states
        for ch in range(N_CHAINS):
            state[ch] = states[ch]

    spec, rows_spec, order = _dir_specs(nc, False)
    pair = lambda rows, cols, own=False: [spec(0, rows, cols, own), spec(1, rows, cols, own)]
    s_shape = jax.ShapeDtypeStruct((nc, GDN_HEADS, GDN_DIM, GDN_DIM), F32)
    vn_shape = jax.ShapeDtypeStruct((nc, GDN_HEADS, CHUNK, GDN_DIM), BF16)
    return pl.pallas_call(
        body, name="gdn_scan_fwd", grid=(nc // SCAN_CHUNKS,),
        in_specs=(pair(CHUNK, GDN_DIM) + pair(CHUNK, GDN_DIM) + pair(CHUNK, GDN_DIM) + pair(CHUNK, GDN_DIM)
                  + pair(CHUNK, CHUNK) + pair(None, LANES)),
        out_specs=([rows_spec(0, hd), rows_spec(1, hd)] + pair(GDN_DIM, GDN_DIM, True)
                   + pair(CHUNK, GDN_DIM, True)),
        out_shape=[jax.ShapeDtypeStruct((t, hd), F32), jax.ShapeDtypeStruct((t, hd), F32),
                   s_shape, s_shape, vn_shape, vn_shape],
        scratch_shapes=[pltpu.VMEM((N_CHAINS, GDN_DIM, GDN_DIM), F32)],
        compiler_params=_params(("arbitrary",), VMEM_LIMIT),
    )(u, u, w, w, qg, qg, kd, kd, intra, intra, egl, egl)


def _gdn_bwd(qkvc, gb, gbt, do, saved, exchange=None):
    scan = _gdn_scan_bwd(do, saved, qkvc.shape[0])
    return _gdn_local_bwd(qkvc, gb, gbt, do, saved, scan, exchange)


def _gdn_scan_bwd(do, saved, t):
    nc = t // CHUNK
    hd = GDN_HEADS * GDN_DIM

    def body(*refs):
        ins, outs, dstate = refs[:16], refs[16:26], refs[26]
        @pl.when(pl.program_id(0) == 0)
        def _():
            dstate[...] = jnp.zeros_like(dstate)

        chains = [(d, h) for d in range(2) for h in range(GDN_HEADS)]
        dss = [dstate[ch] for ch in range(N_CHAINS)]
        for step in range(SCAN_CHUNKS):
            at = [order(d)[step] for d in range(2)]
            pick = lambda k, d, h: ins[2 * k + d][at[d], h]
            dsbs = [ds.astype(BF16) for ds in dss]
            ss = [pick(1, d, h) for d, h in chains]
            sbs = [s.astype(BF16) for s in ss]
            dos = [ins[d][at[d] * CHUNK:(at[d] + 1) * CHUNK, h * GDN_DIM:(h + 1) * GDN_DIM].astype(BF16)
                   for d, h in chains]
            dv1 = [_dot_tn(pick(5, d, h), dov) for (d, h), dov in zip(chains, dos)]
            dv2 = [_dot(pick(4, d, h), dsb) for (d, h), dsb in zip(chains, dsbs)]
            ds1 = [_dot_tn(pick(3, d, h), dov) for (d, h), dov in zip(chains, dos)]
            dkds = [_dot_nt(pick(6, d, h), dsb) for (d, h), dsb in zip(chains, dsbs)]
            dqgs = [_dot_nt(dov, sb) for dov, sb in zip(dos, sbs)]
            dvns = [(a + b).astype(BF16) for a, b in zip(dv1, dv2)]
            ds2 = [_dot_tn(pick(2, d, h), dvn) for (d, h), dvn in zip(chains, dvns)]
            dws = [_dot_nt(dvn, sb) for dvn, sb in zip(dvns, sbs)]
            new_dss = []
            for ch, (d, h) in enumerate(chains):
                egl = ins[14 + d][at[d], ch:ch + 1, :]
                outs[d][at[d], h] = dvns[ch]
                outs[2 + d][at[d], h] = (-dws[ch]).astype(BF16)
                outs[4 + d][at[d], h] = dqgs[ch]
                outs[6 + d][at[d], h] = dkds[ch]
                outs[8 + d][at[d], h:h + 1, :] = egl * jnp.sum(jnp.sum(ss[ch] * dss[ch], axis=1, keepdims=True),
                                                               axis=0, keepdims=True)
                new_dss.append(ds1[ch] + egl * dss[ch] - ds2[ch])
            dss = new_dss
        for ch in range(N_CHAINS):
            dstate[ch] = dss[ch]

    spec, rows_spec, order = _dir_specs(nc, True)
    pair = lambda rows, cols, own=False: [spec(0, rows, cols, own), spec(1, rows, cols, own)]
    s_f, s_b = saved["s"]
    vn_f, vn_b = saved["vn"]
    w, qg, kd, intra, egl = saved["w"], saved["qg"], saved["kd"], saved["intra"], saved["egl"]
    own = lambda rows, cols, dtype: jax.ShapeDtypeStruct((nc, GDN_HEADS, rows, cols), dtype)
    row_shape = jax.ShapeDtypeStruct((nc, GDN_HEADS, LANES), F32)
    return pl.pallas_call(
        body, name="gdn_scan_bwd", grid=(nc // SCAN_CHUNKS,),
        in_specs=([rows_spec(0, hd), rows_spec(1, hd)] + pair(GDN_DIM, GDN_DIM, True) + pair(CHUNK, GDN_DIM)
                  + pair(CHUNK, GDN_DIM) + pair(CHUNK, GDN_DIM) + pair(CHUNK, CHUNK) + pair(CHUNK, GDN_DIM, True)
                  + pair(None, LANES)),
        out_specs=(pair(CHUNK, GDN_DIM, True) + pair(CHUNK, GDN_DIM, True) + pair(CHUNK, GDN_DIM, True)
                   + pair(CHUNK, GDN_DIM, True) + pair(None, LANES, True)),
        out_shape=[own(CHUNK, GDN_DIM, BF16)] * 4 + [own(CHUNK, GDN_DIM, F32)] * 4 + [row_shape] * 2,
        scratch_shapes=[pltpu.VMEM((N_CHAINS, GDN_DIM, GDN_DIM), F32)],
        compiler_params=_params(("arbitrary",), VMEM_LIMIT),
    )(do, do, s_f, s_b, w, w, qg, qg, kd, kd, intra, intra, vn_f, vn_b, egl, egl)


def _dot3_nt(a, b):
    ah = a.astype(BF16)
    al = (a - ah.astype(F32)).astype(BF16)
    bh = b.astype(BF16)
    bl = (b - bh.astype(F32)).astype(BF16)
    return _dot_nt(ah, bh) + (_dot_nt(ah, bl) + _dot_nt(al, bh))


def _dot3_tn(a, b):
    ah = a.astype(BF16)
    al = (a - ah.astype(F32)).astype(BF16)
    bh = b.astype(BF16)
    bl = (b - bh.astype(F32)).astype(BF16)
    return _dot_tn(ah, bh) + (_dot_tn(ah, bl) + _dot_tn(al, bh))


def _gdn_local_bwd(qkvc, gb, gbt, do, saved, scan, exchange=None):
    t = qkvc.shape[0]
    nc = t // CHUNK
    hd = GDN_HEADS * GDN_DIM

    def body(*refs):
        x_ref, g_ref, gt_ref, do_ref, t_ref = refs[:5]
        per_dir = refs[5:17]
        dx_ref, dg_ref = refs[17:]
        chains = [c for cc in range(LOCAL_CHUNKS) for c in _load_chains(x_ref, g_ref, gt_ref, cc)]
        lane = lax.broadcasted_iota(jnp.int32, (CHUNK, LANES), 1)
        dgates = [jnp.zeros((CHUNK, LANES), F32) for _ in range(LOCAL_CHUNKS)]
        for c in chains:
            d = c["ch"] // GDN_HEADS
            vn_ref, dvn_ref, dw_ref, dqg_ref, dkd_ref, dgl_ref = per_dir[d::2]
            h, cc = c["h"], c["cc"]
            rows = slice(cc * CHUNK, (cc + 1) * CHUNK)
            c.update(tm=t_ref[cc, c["ch"]], dov=do_ref[rows, h * GDN_DIM:(h + 1) * GDN_DIM], vnew=vn_ref[cc, h],
                     dvnew=dvn_ref[cc, h], dw=dw_ref[cc, h], dqg=dqg_ref[cc, h], dkdec=dkd_ref[cc, h],
                     dglast=dgl_ref[cc, h:h + 1, 0:1])
        dintras = [_dot_nt(c["dov"], c["vnew"]) for c in chains]
        dts = [_dot_nt(c["dvnew"], c["vb"]) + _dot_nt(c["dw"], c["kbg"]) for c in chains]
        dvbs = [_dot_tn(c["tm"], c["dvnew"]) for c in chains]
        dkbgs = [_dot_tn(c["tm"], c["dw"]) for c in chains]
        tdts = [_dot3_nt(dt, c["tm"]) for dt, c in zip(dts, chains)]
        dls = [jnp.where(c["masks"][3], -_dot3_tn(c["tm"], tdt), 0.0) for tdt, c in zip(tdts, chains)]
        das = [dl * c["decay"] for dl, c in zip(dls, chains)]
        dqks = [jnp.where(c["masks"][2], di, 0.0) * c["decay"] for di, c in zip(dintras, chains)]
        dkb1 = [_dot(da, c["k"]) for da, c in zip(das, chains)]
        dk1 = [_dot_tn(da, c["kb"]) for da, c in zip(das, chains)]
        dk2 = [_dot_tn(dqk, c["q"]) for dqk, c in zip(dqks, chains)]
        dq1 = [_dot(dqk, c["k"]) for dqk, c in zip(dqks, chains)]
        grads, mms, p_gs, p_betas, p_kds = [], [], [], [], []
        for n, c in enumerate(chains):
            incl = c["masks"][2]
            dkb = dkb1[n] + dkbgs[n] * c["eg"]
            kd = c["dkdec"] * c["kdec"]
            mms.append((dls[n] * c["amat"] + jnp.where(incl, dintras[n], 0.0) * c["qk"]) * c["decay"])
            p_gs.append(c["dqg"] * c["qg"] - kd + dkbgs[n] * c["kbg"])
            p_betas.append(dkb * c["k"] + dvbs[n] * c["v"])
            p_kds.append(kd)
            grads.append((dq1[n] + c["dqg"] * c["eg"],
                          dk1[n] + dk2[n] + c["dkdec"] * c["ek"] + dkb * c["bcol"],
                          dvbs[n] * c["bcol"]))
        row_sums = [jnp.sum(mm, axis=1, keepdims=True) for mm in mms]
        col_sums = [jnp.sum(mm, axis=0, keepdims=True) for mm in mms]
        g_sums = [jnp.sum(pg, axis=1, keepdims=True) for pg in p_gs]
        dbetas = [jnp.sum(pb, axis=1, keepdims=True) for pb in p_betas]
        kd_tots = [jnp.sum(jnp.sum(pk, axis=1, keepdims=True), axis=0, keepdims=True) for pk in p_kds]
        dgcs = [rs - _row_to_col(cs, *c["masks"][0:2]) + gs for rs, cs, gs, c in zip(row_sums, col_sums, g_sums, chains)]
        dgrs = [_col_to_row(dgc, *c["masks"][0:2]) for dgc, c in zip(dgcs, chains)]
        draws = [jnp.sum(jnp.where(jnp.logical_not(c["masks"][3]), dgr, 0.0), axis=1, keepdims=True) + c["dglast"] + kt
                 for dgr, kt, c in zip(dgrs, kd_tots, chains)]
        for c, draw, dbeta in zip(chains, draws, dbetas):
            ch = c["ch"]
            dgates[c["cc"]] = dgates[c["cc"]] + jnp.where(lane == ch, draw, 0.0) + jnp.where(lane == 8 + ch, dbeta, 0.0)
        for cc in range(LOCAL_CHUNKS):
            rows = slice(cc * CHUNK, (cc + 1) * CHUNK)
            for h in range(GDN_HEADS):
                for part in range(3):
                    cols = slice(part * hd + h * GDN_DIM, part * hd + (h + 1) * GDN_DIM)
                    dx_ref[rows, cols] = grads[cc * N_CHAINS + h][part] + grads[cc * N_CHAINS + GDN_HEADS + h][part]
            dg_ref[rows, :] = dgates[cc]

    lc = LOCAL_CHUNKS
    all8 = lambda rows, cols: pl.BlockSpec((lc, N_CHAINS, rows, cols), lambda n: (n, 0, 0, 0))
    own4 = lambda rows, cols: pl.BlockSpec((lc, GDN_HEADS, rows, cols), lambda n: (n, 0, 0, 0))
    row4 = pl.BlockSpec((lc, GDN_HEADS, LANES), lambda n: (n, 0, 0))
    vn_f, vn_b = saved["vn"]
    dvn_f, dvn_b, dw_f, dw_b, dqg_f, dqg_b, dkd_f, dkd_b, dgl_f, dgl_b = scan
    return _grid_call(
        body, "gdn_local_bwd", nc // lc,
        [pl.BlockSpec((lc * CHUNK, 3 * hd), lambda n: (n, 0)), pl.BlockSpec((lc * CHUNK, LANES), lambda n: (n, 0)),
         pl.BlockSpec((lc, 16, CHUNK), lambda n: (n, 0, 0)), pl.BlockSpec((lc * CHUNK, hd), lambda n: (n, 0)),
         all8(CHUNK, CHUNK)] + [own4(CHUNK, GDN_DIM)] * 10 + [row4, row4],
        [pl.BlockSpec((lc * CHUNK, 3 * hd), lambda n: (n, 0)), pl.BlockSpec((lc * CHUNK, LANES), lambda n: (n, 0))],
        [jax.ShapeDtypeStruct((t, 3 * hd), F32), jax.ShapeDtypeStruct((t, LANES), F32)],
        (qkvc, gb, gbt, do, saved["tm"], vn_f, vn_b, dvn_f, dvn_b, dw_f, dw_b, dqg_f, dqg_b, dkd_f, dkd_b, dgl_f, dgl_b),
        exchange=exchange)


def _gdn_post_fwd(of, ob, z, gw, tm):
    t, hd = of.shape

    def body(of_ref, ob_ref, z_ref, w_ref, o_ref):
        for h in range(GDN_HEADS):
            cols = slice(h * GDN_DIM, (h + 1) * GDN_DIM)
            o = of_ref[:, cols] + ob_ref[:, cols]
            zv = z_ref[:, cols]
            o_ref[:, cols] = (o * _rstd(o) * w_ref[...] * (zv * _sigmoid(zv))).astype(BF16)

    row = pl.BlockSpec((tm, hd), lambda i: (i, 0))
    return pl.pallas_call(
        body, name="gdn_post_fwd", grid=(t // tm,),
        in_specs=[row, row, row, _resident((1, GDN_DIM))],
        out_specs=row, out_shape=jax.ShapeDtypeStruct((t, hd), BF16),
        compiler_params=_params(("arbitrary",), VMEM_LIMIT),
    )(of, ob, z, gw)


def _gdn_post_bwd(doa, of, ob, z, gw, tm):
    t, hd = of.shape

    def body(d_ref, of_ref, ob_ref, z_ref, w_ref, do_ref, dz_ref, dw_ref):
        @pl.when(pl.program_id(0) == 0)
        def _():
            dw_ref[...] = jnp.zeros_like(dw_ref)

        dw = jnp.zeros((1, GDN_DIM), F32)
        for h in range(GDN_HEADS):
            cols = slice(h * GDN_DIM, (h + 1) * GDN_DIM)
            o = of_ref[:, cols] + ob_ref[:, cols]
            zv = z_ref[:, cols]
            dv = d_ref[:, cols]
            r = _rstd(o)
            sg = _sigmoid(zv)
            on = o * r * w_ref[...]
            dz_ref[:, cols] = (dv * on * (sg * (1.0 + zv * (1.0 - sg)))).astype(BF16)
            dxr, dwh = _rms_bwd(o, r, w_ref[...], dv * (zv * sg))
            do_ref[:, cols] = dxr
            dw = dw + dwh
        dw_ref[...] += dw

    row = pl.BlockSpec((tm, hd), lambda i: (i, 0))
    return pl.pallas_call(
        body, name="gdn_post_bwd", grid=(t // tm,),
        in_specs=[row, row, row, row, _resident((1, GDN_DIM))],
        out_specs=[row, row, pl.BlockSpec((1, GDN_DIM), lambda i: (0, 0))],
        out_shape=[jax.ShapeDtypeStruct((t, hd), F32), jax.ShapeDtypeStruct((t, hd), BF16),
                   jax.ShapeDtypeStruct((1, GDN_DIM), F32)],
        compiler_params=_params(("arbitrary",), VMEM_LIMIT),
    )(doa, of, ob, z, gw)


SWA_W = SWA_HEADS * SWA_DIM
QBLK = 128
KWIN = QBLK + 2 * RADIUS
WIN_OFFSETS = (0, RADIUS, 2 * RADIUS)


def _t5_bucket(rel):
    nb = REL_BUCKETS // 2
    bucket = (rel > 0).astype(np.int32) * nb
    n = np.abs(rel)
    max_exact = nb // 2
    large = max_exact + (np.log(np.maximum(n, 1) / max_exact)
                         / math.log(REL_MAX_DISTANCE / max_exact) * (nb - max_exact)).astype(np.int32)
    large = np.minimum(large, nb - 1)
    return (bucket + np.where(n < max_exact, n, large)).astype(np.int32)


def _band_tables(dilation):
    a = np.arange(QBLK)
    b = np.arange(KWIN)
    rel = np.stack([b[None, :] - w0 - a[:, None] for w0 in WIN_OFFSETS])
    return np.where(np.abs(rel) <= RADIUS, _t5_bucket(rel * dilation), -1).astype(np.int32)


BAND_CELLS = len(WIN_OFFSETS) * QBLK * KWIN
BIAS_TILE = BAND_CELLS // 3


def _band_index():
    return jnp.asarray(np.concatenate([_band_tables(d).reshape(-1) for _, d in PATTERNS])[None, :])


def _onehot(idx, dtype):
    return (lax.broadcasted_iota(jnp.int32, (REL_BUCKETS, idx.shape[1]), 0) == idx).astype(dtype)


def _bias_tables(rel_bias, idx, tk):
    n = idx.shape[1]

    def body(rb_ref, i_ref, o_ref):
        iv = i_ref[...]
        oh = _onehot(iv, BF16)
        rest, acc = rb_ref[...], None
        for _ in range(3):
            piece = rest.astype(BF16)
            part = jnp.dot(piece, oh, preferred_element_type=F32)
            acc = part if acc is None else acc + part
            rest = rest - piece.astype(F32)
        o_ref[...] = jnp.where(iv < 0, NEG_BIG, acc)

    return pl.pallas_call(
        body, name="bias_tables", grid=(n // tk,),
        in_specs=[_resident((SWA_HEADS, REL_BUCKETS)), pl.BlockSpec((1, tk), lambda k: (0, k))],
        out_specs=pl.BlockSpec((SWA_HEADS, tk), lambda k: (0, k)),
        out_shape=jax.ShapeDtypeStruct((SWA_HEADS, n), F32),
        compiler_params=_params(("arbitrary",), VMEM_LIMIT),
    )(rel_bias.T, idx)


def _head_mean(x2, bd_ref):
    bd = bd_ref[...]
    rest, acc = x2, None
    for _ in range(3):
        piece = rest.astype(BF16)
        part = jnp.dot(piece, bd, preferred_element_type=F32)
        acc = part if acc is None else acc + part
        rest = rest - piece.astype(F32)
    return acc


VIEW_DILATIONS = tuple(d for _, d in PATTERNS if d > 1)


def _view_spec(tm, d):
    return pl.BlockSpec((tm // d, d * SWA_W), lambda i: (i, 0))


def _view_shape(t, d, dtype):
    return jax.ShapeDtypeStruct((t // d, d * SWA_W), dtype)


N_GROUPS = SWA_W // LANES


def _to_view(src_ref, idx, dst_ref, d, rows):
    for r in range(d):
        for g in range(N_GROUPS):
            cols = slice(r * SWA_W + g * LANES, r * SWA_W + (g + 1) * LANES)
            dst_ref[:, cols] = src_ref[idx, g, pl.ds(r, rows // d, stride=d), :].astype(dst_ref.dtype)


def _from_view(src_ref, dst_ref, idx, d, rows):
    for r in range(d):
        for g in range(N_GROUPS):
            cols = slice(r * SWA_W + g * LANES, r * SWA_W + (g + 1) * LANES)
            dst_ref[idx, g, pl.ds(r, rows // d, stride=d), :] = src_ref[:, cols]


def _swa_prep_fwd(qkvb, qw, kw, bd, tm):
    t = qkvb.shape[0]

    def body(x_ref, qw_ref, kw_ref, bd_ref, *rest):
        outs, sc = rest[:-1], rest[-1]
        for gidx in range(N_GROUPS):
            cols = slice(gidx * LANES, (gidx + 1) * LANES)
            xq = x_ref[:, cols]
            sc[0, gidx] = xq * lax.rsqrt(_head_mean(xq * xq, bd_ref) + EPS) * qw_ref[:, cols] * (SWA_DIM ** -0.5)
            xk = x_ref[:, SWA_W + gidx * LANES:SWA_W + (gidx + 1) * LANES]
            sc[1, gidx] = xk * lax.rsqrt(_head_mean(xk * xk, bd_ref) + EPS) * kw_ref[:, cols]
            sc[2, gidx] = x_ref[:, 2 * SWA_W + gidx * LANES:2 * SWA_W + (gidx + 1) * LANES]
            for i in range(3):
                outs[i][:, cols] = sc[i, gidx].astype(BF16)
        for i in range(3):
            for n, d in enumerate(VIEW_DILATIONS):
                _to_view(sc, i, outs[3 * (n + 1) + i], d, tm)

    return pl.pallas_call(
        body, name="swa_prep_fwd", grid=(t // tm,),
        in_specs=[pl.BlockSpec((tm, 3 * SWA_W), lambda i: (i, 0)), _resident((1, SWA_W)), _resident((1, SWA_W)),
                  _resident((LANES, LANES))],
        out_specs=[_view_spec(tm, d) for d in (1,) + VIEW_DILATIONS for _ in range(3)],
        out_shape=[_view_shape(t, d, BF16) for d in (1,) + VIEW_DILATIONS for _ in range(3)],
        scratch_shapes=[pltpu.VMEM((3, N_GROUPS, tm, LANES), F32)],
        compiler_params=_params(("arbitrary",), VMEM_LIMIT),
    )(qkvb, qw, kw, bd)


def _swa_prep_bwd(qkvb, qw, kw, bd, grads, tm):
    t = qkvb.shape[0]

    def body(x_ref, qw_ref, kw_ref, bd_ref, *rest):
        parts, (dx_ref, dqw_ref, dkw_ref, sc) = rest[:9], rest[9:]
        @pl.when(pl.program_id(0) == 0)
        def _():
            dqw_ref[...] = jnp.zeros_like(dqw_ref)
            dkw_ref[...] = jnp.zeros_like(dkw_ref)

        for i in range(3):
            for n, d in enumerate(VIEW_DILATIONS):
                _from_view(parts[3 * (n + 1) + i], sc, 2 * i + n, d, tm)
        for gidx in range(N_GROUPS):
            cols = slice(gidx * LANES, (gidx + 1) * LANES)
            for i, base, w_ref, dw_ref, scale in ((0, 0, qw_ref, dqw_ref, SWA_DIM ** -0.5),
                                                  (1, SWA_W, kw_ref, dkw_ref, 1.0)):
                xv = x_ref[:, base + gidx * LANES:base + (gidx + 1) * LANES]
                dy = (parts[i][:, cols] + sc[2 * i, gidx] + sc[2 * i + 1, gidx]) * scale
                r = lax.rsqrt(_head_mean(xv * xv, bd_ref) + EPS)
                xhat = xv * r
                dxh = dy * w_ref[:, cols]
                dx = r * (dxh - xhat * _head_mean(dxh * xhat, bd_ref))
                dx_ref[:, base + gidx * LANES:base + (gidx + 1) * LANES] = dx.astype(BF16)
                dw_ref[:, cols] += jnp.sum(dy * xhat, axis=0, keepdims=True)
            dx_ref[:, 2 * SWA_W + gidx * LANES:2 * SWA_W + (gidx + 1) * LANES] = (
                parts[2][:, cols] + sc[4, gidx] + sc[5, gidx]).astype(BF16)

    wrow = pl.BlockSpec((1, SWA_W), lambda i: (0, 0))
    return pl.pallas_call(
        body, name="swa_prep_bwd", grid=(t // tm,),
        in_specs=[pl.BlockSpec((tm, 3 * SWA_W), lambda i: (i, 0)), _resident((1, SWA_W)), _resident((1, SWA_W)),
                  _resident((LANES, LANES))] + [_view_spec(tm, d) for d in (1,) + VIEW_DILATIONS for _ in range(3)],
        out_specs=[pl.BlockSpec((tm, 3 * SWA_W), lambda i: (i, 0)), wrow, wrow],
        out_shape=[jax.ShapeDtypeStruct((t, 3 * SWA_W), BF16), jax.ShapeDtypeStruct((1, SWA_W), F32),
                   jax.ShapeDtypeStruct((1, SWA_W), F32)],
        scratch_shapes=[pltpu.VMEM((6, N_GROUPS, tm, LANES), F32)],
        compiler_params=_params(("arbitrary",), VMEM_LIMIT),
    )(qkvb, qw, kw, bd, *grads)


def _aligned(v, m):
    return v if isinstance(v, int) else pl.multiple_of(v, m)


BAND_GROUP = 2


def _band_loop(nsub, length, step, group=BAND_GROUP):
    step([(0, 0)], 0)
    if nsub > 2:
        assert (nsub - 2) % group == 0

        def inner(i, carry):
            s0 = 1 + i * group
            step([(s0 + e, pl.multiple_of((s0 + e) * QBLK - RADIUS, RADIUS)) for e in range(group)], 1)
            return carry
        lax.fori_loop(0, (nsub - 2) // group, inner, 0)
    step([(nsub - 1, length - KWIN)], 2)


def _head_select(lane, a0, a1):
    return jnp.where(lane < SWA_DIM, a0, a1)


def _swa_fwd(qv, kv, vv, bias, dilation, name):
    length = qv.shape[0]
    nsub = length // QBLK
    assert nsub >= 2 and length % QBLK == 0

    def body(q_ref, k_ref, v_ref, b_ref, o_ref, l_ref):
        lane = lax.broadcasted_iota(jnp.int32, (QBLK, LANES), 1)

        def step(blocks, var):
            items = []
            for s, ws in blocks:
                rows = pl.ds(_aligned(s * QBLK, QBLK), QBLK)
                q, kk, vw = q_ref[rows, :], k_ref[pl.ds(ws, KWIN), :], v_ref[pl.ds(ws, KWIN), :]
                for hh in range(2):
                    items.append((hh, jnp.where((lane < SWA_DIM) == (hh == 0), q, jnp.zeros_like(q)), kk, vw))
            lgs = [_dot_nt(qh, kk) + b_ref[hh, var] for hh, qh, kk, _ in items]
            ms = [jnp.max(lg, axis=-1, keepdims=True) for lg in lgs]
            ps = [jnp.exp(lg - m) for lg, m in zip(lgs, ms)]
            dens = [jnp.sum(p, axis=-1, keepdims=True) for p in ps]
            pvs = [_dot(p, it[3]) for p, it in zip(ps, items)]
            for n, (s, _) in enumerate(blocks):
                rows = pl.ds(_aligned(s * QBLK, QBLK), QBLK)
                o0, o1 = (pvs[2 * n + hh] / dens[2 * n + hh] for hh in range(2))
                l0, l1 = (ms[2 * n + hh] + jnp.log(dens[2 * n + hh]) for hh in range(2))
                o_ref[rows, :] = _head_select(lane, o0, o1)
                l_ref[rows, :] = _head_select(lane, l0, l1)

        _band_loop(nsub, length, step)

    blk = pl.BlockSpec((length, LANES), lambda hp, r: (0, r * (SWA_W // LANES) + hp))
    shp = jax.ShapeDtypeStruct(qv.shape, F32)
    return pl.pallas_call(
        body, name=name, grid=(SWA_W // LANES, dilation),
        in_specs=[blk, blk, blk, pl.BlockSpec((2, 3, QBLK, KWIN), lambda hp, r: (hp, 0, 0, 0))],
        out_specs=[blk, blk], out_shape=[shp, shp],
        compiler_params=_params(("arbitrary", "arbitrary"), VMEM_LIMIT),
    )(qv, kv, vv, bias)


def _swa_combine(os_, ls_, tm):
    t = os_[0].shape[0]

    def body(o0, o1, o2, l0, l1, l2, o_ref, ob_ref, la_ref, lb_ref, lc_ref, sc):
        for n, d in enumerate(VIEW_DILATIONS):
            _from_view((o1, o2)[n], sc, n, d, tm)
            _from_view((l1, l2)[n], sc, 2 + n, d, tm)
        for g in range(N_GROUPS):
            cols = slice(g * LANES, (g + 1) * LANES)
            la, lb, lc = l0[:, cols], sc[2, g], sc[3, g]
            m = jnp.maximum(jnp.maximum(la, lb), lc)
            tot = m + jnp.log(jnp.exp(la - m) + jnp.exp(lb - m) + jnp.exp(lc - m))
            o = jnp.exp(la - tot) * o0[:, cols] + jnp.exp(lb - tot) * sc[0, g] + jnp.exp(lc - tot) * sc[1, g]
            o_ref[:, cols] = o
            ob_ref[:, cols] = o.astype(BF16)
            la_ref[:, cols] = tot
            sc[4, g] = tot
        for n, d in enumerate(VIEW_DILATIONS):
            _to_view(sc, 4, (lb_ref, lc_ref)[n], d, tm)

    specs = [_view_spec(tm, d) for d in (1,) + VIEW_DILATIONS]
    return pl.pallas_call(
        body, name="swa_combine", grid=(t // tm,), in_specs=specs + specs, out_specs=[specs[0], specs[0]] + specs,
        out_shape=[jax.ShapeDtypeStruct((t, SWA_W), F32), jax.ShapeDtypeStruct((t, SWA_W), BF16)]
                  + [_view_shape(t, d, F32) for d in (1,) + VIEW_DILATIONS],
        scratch_shapes=[pltpu.VMEM((5, N_GROUPS, tm, LANES), F32)],
        compiler_params=_params(("arbitrary",), VMEM_LIMIT),
    )(*os_, *ls_)


def _swa_bwd_prep(do, o, bd, tm):
    t = do.shape[0]

    def body(d_ref, o_ref, bd_ref, dd1, dd4, dd16, db1, db4, db16, sc):
        for gidx in range(N_GROUPS):
            cols = slice(gidx * LANES, (gidx + 1) * LANES)
            dv = d_ref[:, cols]
            dd = _head_mean(dv * o_ref[:, cols], bd_ref) * float(SWA_DIM)
            sc[0, gidx] = dd
            sc[1, gidx] = dv
            dd1[:, cols] = dd
            db1[:, cols] = dv.astype(BF16)
        for n, d in enumerate(VIEW_DILATIONS):
            _to_view(sc, 0, (dd4, dd16)[n], d, tm)
            _to_view(sc, 1, (db4, db16)[n], d, tm)

    specs = [_view_spec(tm, d) for d in (1,) + VIEW_DILATIONS]
    return pl.pallas_call(
        body, name="swa_bwd_prep", grid=(t // tm,), in_specs=[specs[0], specs[0], _resident((LANES, LANES))],
        out_specs=specs + specs,
        out_shape=[_view_shape(t, d, F32) for d in (1,) + VIEW_DILATIONS]
                  + [_view_shape(t, d, BF16) for d in (1,) + VIEW_DILATIONS],
        scratch_shapes=[pltpu.VMEM((2, N_GROUPS, tm, LANES), F32)],
        compiler_params=_params(("arbitrary",), VMEM_LIMIT),
    )(do, o, bd)


def _swa_bwd(qv, kv, vv, dov, lv, ddv, bias_a, dilation, name):
    length = qv.shape[0]
    nsub = length // QBLK
    single = pl.Buffered(1) if dilation == 1 else None

    def body(q_ref, k_ref, v_ref, do_ref, l_ref, dd_ref, ba_ref, dq_ref, dk_ref, dv_ref, db_ref):
        @pl.when(pl.program_id(1) == 0)
        def _():
            db_ref[...] = jnp.zeros_like(db_ref)

        lane = lax.broadcasted_iota(jnp.int32, (QBLK, LANES), 1)
        lanew = lax.broadcasted_iota(jnp.int32, (KWIN, LANES), 1)

        def step(blocks, var):
            items = []
            for s, ws in blocks:
                rows = pl.ds(_aligned(s * QBLK, QBLK), QBLK)
                win = pl.ds(ws, KWIN)
                q, dov_ = q_ref[rows, :], do_ref[rows, :]
                kk, vw = k_ref[win, :], v_ref[win, :]
                lse, dd = l_ref[rows, :], dd_ref[rows, :]
                for hh in range(2):
                    mine = (lane < SWA_DIM) == (hh == 0)
                    col = slice(hh * SWA_DIM, hh * SWA_DIM + 1)
                    items.append((hh, jnp.where(mine, q, jnp.zeros_like(q)), jnp.where(mine, dov_, jnp.zeros_like(dov_)),
                                  kk, vw, lse[:, col], dd[:, col], q, dov_))
            lgs = [_dot_nt(it[1], it[3]) + ba_ref[it[0], var] for it in items]
            dps = [_dot_nt(it[2], it[4]) for it in items]
            ps = [jnp.exp(lg - it[5]) for lg, it in zip(lgs, items)]
            dss = [p * (dp - it[6]) for p, dp, it in zip(ps, dps, items)]
            dqs = [_dot(ds, it[3]) for ds, it in zip(dss, items)]
            dks = [_dot_tn(ds, it[7]) for ds, it in zip(dss, items)]
            dvs = [_dot_tn(p, it[8]) for p, it in zip(ps, items)]
            for n, (s, ws) in enumerate(blocks):
                rows = pl.ds(_aligned(s * QBLK, QBLK), QBLK)
                win = pl.ds(ws, KWIN)
                dq_ref[rows, :] = _head_select(lane, dqs[2 * n], dqs[2 * n + 1])
                dk_ref[win, :] += _head_select(lanew, dks[2 * n], dks[2 * n + 1])
                dv_ref[win, :] += _head_select(lanew, dvs[2 * n], dvs[2 * n + 1])
            for hh in range(2):
                tot = dss[hh]
                for n in range(1, len(blocks)):
                    tot = tot + dss[2 * n + hh]
                db_ref[hh, var] += tot

        dk_ref[...] = jnp.zeros_like(dk_ref)
        dv_ref[...] = jnp.zeros_like(dv_ref)
        _band_loop(nsub, length, step)

    imap = lambda hp, r: (0, r * (SWA_W // LANES) + hp)
    blk_in = pl.BlockSpec((length, LANES), imap, pipeline_mode=single)
    blk_out = pl.BlockSpec((length, LANES), imap)
    shp = jax.ShapeDtypeStruct(qv.shape, F32)
    return pl.pallas_call(
        body, name=name, grid=(SWA_W // LANES, dilation),
        in_specs=[blk_in] * 6 + [pl.BlockSpec((2, 3, QBLK, KWIN), lambda hp, r: (hp, 0, 0, 0))],
        out_specs=[blk_out, blk_out, blk_out, pl.BlockSpec((2, 3, QBLK, KWIN), lambda hp, r: (hp, 0, 0, 0))],
        out_shape=[shp, shp, shp, jax.ShapeDtypeStruct((SWA_HEADS, 3, QBLK, KWIN), F32)],
        compiler_params=_params(("arbitrary", "arbitrary"), VMEM_LIMIT),
    )(qv, kv, vv, dov, lv, ddv, bias_a)


def _bias_grad(ds2, idx, tk):
    n = ds2.shape[1]
    nk = n // tk

    def body(a_ref, i_ref, o_ref):
        @pl.when(pl.program_id(0) == 0)
        def _():
            o_ref[...] = jnp.zeros_like(o_ref)

        oh = _onehot(i_ref[...], BF16)
        rest = a_ref[...]
        acc = jnp.zeros((SWA_HEADS, REL_BUCKETS), F32)
        for _ in range(3):
            piece = rest.astype(BF16)
            acc = acc + _dot_nt(piece, oh)
            rest = rest - piece.astype(F32)
        o_ref[...] += acc

    return pl.pallas_call(
        body, name="bias_grad", grid=(nk,),
        in_specs=[pl.BlockSpec((SWA_HEADS, tk), lambda k: (0, k)), pl.BlockSpec((1, tk), lambda k: (0, k))],
        out_specs=pl.BlockSpec((SWA_HEADS, REL_BUCKETS), lambda k: (0, 0)),
        out_shape=jax.ShapeDtypeStruct((SWA_HEADS, REL_BUCKETS), F32),
        compiler_params=_params(("arbitrary",), VMEM_LIMIT),
    )(ds2, idx)


def _swa_branch_fwd(qkvb, qw_t, kw_t, rel_bias, bd, tm):
    qkv = _swa_prep_fwd(qkvb, qw_t, kw_t, bd, tm)
    tables = _bias_tables(rel_bias, _band_index(), BIAS_TILE)
    os_, ls_, tabs = [], [], []
    for n, (_, d) in enumerate(PATTERNS):
        bias = tables[:, n * BAND_CELLS:(n + 1) * BAND_CELLS].reshape(SWA_HEADS, len(WIN_OFFSETS), QBLK, KWIN)
        o_p, l_p = _swa_fwd(*qkv[3 * n:3 * n + 3], bias, d, f"swa_fwd_d{d}")
        os_.append(o_p)
        ls_.append(l_p)
        tabs.append(bias)
    o, o16, *lses = _swa_combine(os_, ls_, tm)
    return o, o16, (qkv, lses, tabs)


def _swa_branch_bwd(do, o, saved, qkvb, qw_t, kw_t, bd, tm):
    qkv, lses, tabs = saved
    prep = _swa_bwd_prep(do, o, bd, tm)
    grads, dss = [], []
    for n, ((_, d), bias) in enumerate(zip(PATTERNS, tabs)):
        dq, dk, dv, ds = _swa_bwd(*qkv[3 * n:3 * n + 3], prep[3 + n], lses[n], prep[n], bias, d, f"swa_bwd_d{d}")
        grads += [dq, dk, dv]
        dss.append(ds.reshape(SWA_HEADS, -1))
    dqkvb, dqw, dkw = _swa_prep_bwd(qkvb, qw_t, kw_t, bd, grads, tm)
    dbias = _bias_grad(jnp.concatenate(dss, axis=1), _band_index(), BIAS_TILE)
    fold = lambda w: jnp.sum(w.reshape(SWA_HEADS, SWA_DIM), axis=0)
    return dqkvb, fold(dqw), fold(dkw), dbias.T


def _mesh_pos():
    return lax.axis_index("x"), lax.axis_index("y"), lax.axis_index("c")


def _other_chips(x, y):
    return [(1 - x, y), (x, 1 - y), (1 - x, 1 - y)]


def _remote(src, dst, send_sem, recv_sem, device):
    return pltpu.make_async_remote_copy(src_ref=src, dst_ref=dst, send_sem=send_sem, recv_sem=recv_sem,
                                        device_id=device, device_id_type=MESH)


def _split_axis(shape2):
    return 0 if (shape2[0] // 2) % 16 == 0 else 1


def _half_index(shape2, c):
    axis = _split_axis(shape2)
    h = shape2[axis] // 2
    return (pl.ds(c * h, h), slice(None)) if axis == 0 else (slice(None), pl.ds(c * h, h))


def _all_gather(xs):
    n = len(xs)

    def body(*refs):
        ins, outs = refs[:n], refs[n:2 * n]
        send_sems, recv_sems = refs[2 * n:]
        x, y, c = _mesh_pos()
        me = 2 * x + y
        chips = _other_chips(x, y)
        halves = []
        sends = []
        for a in range(n):
            h = ins[a].shape[0] // 2
            mine, other = pl.ds(c * h, h), pl.ds((1 - c) * h, h)
            halves.append((mine, other))
            own = _remote(ins[a], outs[a].at[me], send_sems.at[a, 6], recv_sems.at[a, 6], (x, y, 1 - c))
            own.start()
            sends.append(own)
            for j, chip in enumerate(chips):
                cp = _remote(ins[a].at[mine], outs[a].at[me, mine], send_sems.at[a, j], recv_sems.at[a, j], (*chip, c))
                cp.start()
                sends.append(cp)
        for a in range(n):
            mine, _ = halves[a]
            for j, chip in enumerate(chips):
                src = 2 * chip[0] + chip[1]
                landed = outs[a].at[src, mine]
                _remote(landed, landed, send_sems.at[a, j], recv_sems.at[a, j], (x, y, c)).wait_recv()
                fwd = _remote(landed, landed, send_sems.at[a, 3 + j], recv_sems.at[a, 3 + j], (x, y, 1 - c))
                fwd.start()
                sends.append(fwd)
        for a in range(n):
            _, other = halves[a]
            for j, chip in enumerate(chips):
                src = 2 * chip[0] + chip[1]
                landed = outs[a].at[src, other]
                _remote(landed, landed, send_sems.at[a, 3 + j], recv_sems.at[a, 3 + j], (x, y, c)).wait_recv()
            mine_slot = outs[a].at[me]
            _remote(mine_slot, mine_slot, send_sems.at[a, 6], recv_sems.at[a, 6], (x, y, c)).wait_recv()
        for cp in sends:
            cp.wait_send()

    return list(pl.pallas_call(
        body, name="all_gather_weights",
        in_specs=[ANY] * n, out_specs=[ANY] * n,
        out_shape=[jax.ShapeDtypeStruct((N_SHARDS,) + a.shape, a.dtype) for a in xs],
        scratch_shapes=[pltpu.SemaphoreType.DMA((n, 7)), pltpu.SemaphoreType.DMA((n, 7))],
    )(*xs))


def _rs_pair(gs):
    n = len(gs)

    def body(*refs):
        ins, lands = refs[:n], refs[n:2 * n]
        send_sems, recv_sems = refs[2 * n:]
        x, y, c = _mesh_pos()
        cps = []
        for a in range(n):
            theirs = (slice(None),) + _half_index(ins[a].shape[1:], 1 - c)
            cp = _remote(ins[a].at[theirs], lands[a], send_sems.at[a], recv_sems.at[a], (x, y, 1 - c))
            cp.start()
            cps.append(cp)
        for cp in cps:
            cp.wait()

    def half_shape(g):
        dims = list(g.shape)
        dims[1 + _split_axis(g.shape[1:])] //= 2
        return tuple(dims)

    return list(pl.pallas_call(
        body, name="rs_pair", in_specs=[ANY] * n, out_specs=[ANY] * n,
        out_shape=[jax.ShapeDtypeStruct(half_shape(g), g.dtype) for g in gs],
        scratch_shapes=[pltpu.SemaphoreType.DMA((n,)), pltpu.SemaphoreType.DMA((n,))],
    )(*gs))


def _pair_exchange(gs):
    def copies(cin, cout, send_sems, recv_sems):
        x, y, c = _mesh_pos()
        return [_remote(g.at[(slice(None),) + _half_index(g.shape[1:], 1 - c)], land, send_sems.at[a, 0],
                        recv_sems.at[a, 0], (x, y, 1 - c)) for a, (g, land) in enumerate(zip(cin, cout))]

    def start(*refs):
        for cp in copies(*refs):
            cp.start()

    def finish(*refs):
        for cp in copies(*refs):
            cp.wait()

    def half_shape(g):
        dims = list(g.shape)
        dims[1 + _split_axis(g.shape[1:])] //= 2
        return tuple(dims)

    return _Exchange(tuple(gs), tuple(jax.ShapeDtypeStruct(half_shape(g), g.dtype) for g in gs), start, finish)


def _rs_chips(ss):
    n = len(ss)

    def body(*refs):
        ins, outs = refs[:n], refs[n:2 * n]
        send_sems, recv_sems = refs[2 * n:]
        x, y, c = _mesh_pos()
        me = 2 * x + y
        chips = _other_chips(x, y)
        cps = []
        for a in range(n):
            for j, chip in enumerate(chips):
                dst_chip = 2 * chip[0] + chip[1]
                cp = _remote(ins[a].at[dst_chip], outs[a].at[me], send_sems.at[a, j], recv_sems.at[a, j], (*chip, c))
                cp.start()
                cps.append(cp)
        for a in range(n):
            for j, chip in enumerate(chips):
                src = 2 * chip[0] + chip[1]
                _remote(outs[a].at[src], outs[a].at[src], send_sems.at[a, j], recv_sems.at[a, j], (x, y, c)).wait_recv()
        for cp in cps:
            cp.wait_send()

    return list(pl.pallas_call(
        body, name="rs_chips", in_specs=[ANY] * n, out_specs=[ANY] * n,
        out_shape=[jax.ShapeDtypeStruct(s.shape, s.dtype) for s in ss],
        scratch_shapes=[pltpu.SemaphoreType.DMA((n, 3)), pltpu.SemaphoreType.DMA((n, 3))],
    )(*ss))


def _rs_join(fs, axes):
    n = len(fs)

    def whole(f, axis):
        dims = list(f.shape)
        dims[axis] *= 2
        return tuple(dims)

    def body(*refs):
        ins, outs = refs[:n], refs[n:2 * n]
        send_sems, recv_sems = refs[2 * n:]
        x, y, c = _mesh_pos()
        cps = []
        for a in range(n):
            h = ins[a].shape[axes[a]]
            mine = (pl.ds(c * h, h), slice(None)) if axes[a] == 0 else (slice(None), pl.ds(c * h, h))
            cp = _remote(ins[a], outs[a].at[mine], send_sems.at[a], recv_sems.at[a], (x, y, 1 - c))
            cp.start()
            cps.append(cp)
        for cp in cps:
            cp.wait()

    outs = pl.pallas_call(
        body, name="rs_join", in_specs=[ANY] * n, out_specs=[ANY] * n,
        out_shape=[jax.ShapeDtypeStruct(whole(f, ax), f.dtype) for f, ax in zip(fs, axes)],
        scratch_shapes=[pltpu.SemaphoreType.DMA((n,)), pltpu.SemaphoreType.DMA((n,))],
    )(*fs)
    c = lax.axis_index("c")
    return [lax.dynamic_update_slice_in_dim(o, f, c * f.shape[ax], ax) for o, f, ax in zip(outs, fs, axes)]


def _gather_exchange(xs):
    def start(cin, cout, send_sems, recv_sems):
        x, y, c = _mesh_pos()
        me = 2 * x + y
        for a, (src, dst) in enumerate(zip(cin, cout)):
            mine = _half_index(src.shape, c)
            for j, chip in enumerate(_other_chips(x, y)):
                _remote(src.at[mine], dst.at[(me,) + mine], send_sems.at[a, j], recv_sems.at[a, j], (*chip, c)).start()
            _remote(src, dst.at[me], send_sems.at[a, 3], recv_sems.at[a, 3], (x, y, 1 - c)).start()

    def finish(cin, cout, send_sems, recv_sems):
        x, y, c = _mesh_pos()
        for a, dst in enumerate(cout):
            for j, chip in enumerate(_other_chips(x, y)):
                landed = dst.at[(2 * chip[0] + chip[1],) + _half_index(dst.shape[1:], c)]
                _remote(landed, landed, send_sems.at[a, j], recv_sems.at[a, j], (x, y, c)).wait()
            own = dst.at[2 * x + y]
            _remote(own, own, send_sems.at[a, 3], recv_sems.at[a, 3], (x, y, c)).wait()

    return _Exchange(tuple(xs), tuple(jax.ShapeDtypeStruct((N_SHARDS,) + a.shape, a.dtype) for a in xs), start, finish)


def _gather_forward(gs):
    n = len(gs)

    def body(*refs):
        outs = refs[n:2 * n]
        send_sems, recv_sems = refs[2 * n:]
        x, y, c = _mesh_pos()
        chips = _other_chips(x, y)
        cps = []
        for a in range(n):
            for j, chip in enumerate(chips):
                landed = outs[a].at[(2 * chip[0] + chip[1],) + _half_index(outs[a].shape[1:], c)]
                cp = _remote(landed, landed, send_sems.at[a, j], recv_sems.at[a, j], (x, y, 1 - c))
                cp.start()
                cps.append(cp)
        for a in range(n):
            for j, chip in enumerate(chips):
                other = outs[a].at[(2 * chip[0] + chip[1],) + _half_index(outs[a].shape[1:], 1 - c)]
                _remote(other, other, send_sems.at[a, j], recv_sems.at[a, j], (x, y, c)).wait_recv()
        for cp in cps:
            cp.wait_send()

    return list(pl.pallas_call(
        body, name="gather_forward", in_specs=[ANY] * n, out_specs=[ANY] * n,
        out_shape=[jax.ShapeDtypeStruct(g.shape, g.dtype) for g in gs],
        input_output_aliases={i: i for i in range(n)},
        scratch_shapes=[pltpu.SemaphoreType.DMA((n, 3)), pltpu.SemaphoreType.DMA((n, 3))],
    )(*gs))


def _scatter_exchange(ss):
    def start(cin, cout, send_sems, recv_sems):
        x, y, c = _mesh_pos()
        me = 2 * x + y
        for a, (src, dst) in enumerate(zip(cin, cout)):
            for j, chip in enumerate(_other_chips(x, y)):
                _remote(src.at[2 * chip[0] + chip[1]], dst.at[me], send_sems.at[a, j], recv_sems.at[a, j],
                        (*chip, c)).start()

    def finish(cin, cout, send_sems, recv_sems):
        x, y, c = _mesh_pos()
        for a, dst in enumerate(cout):
            for j, chip in enumerate(_other_chips(x, y)):
                slot = dst.at[2 * chip[0] + chip[1]]
                _remote(slot, slot, send_sems.at[a, j], recv_sems.at[a, j], (x, y, c)).wait()

    return _Exchange(tuple(ss), tuple(jax.ShapeDtypeStruct(s.shape, s.dtype) for s in ss), start, finish)


def _add_pairs(gs, lands, name):
    n = len(gs)

    def body(*refs):
        c = lax.axis_index("c")
        for g_ref, l_ref, o_ref in zip(refs[:n], refs[n:2 * n], refs[2 * n:]):
            mine = g_ref[(0,) + _half_index(g_ref.shape[1:], c)]
            o_ref[0] = (mine.astype(F32) + l_ref[0].astype(F32)).astype(BF16)

    whole = [pl.BlockSpec((1,) + g.shape[1:], lambda j: (j, 0, 0)) for g in gs]
    half = [pl.BlockSpec((1,) + l.shape[1:], lambda j: (j, 0, 0)) for l in lands]
    return list(pl.pallas_call(body, name=name, grid=(gs[0].shape[0],), in_specs=whole + half, out_specs=half,
                               out_shape=[jax.ShapeDtypeStruct(l.shape, BF16) for l in lands],
                               compiler_params=_params(("arbitrary",), VMEM_LIMIT))(*gs, *lands))


def _sum_slots(slots, owns, name):
    n = len(slots)

    def body(*refs):
        me = 2 * lax.axis_index("x") + lax.axis_index("y")
        for s_ref, o_ref, out_ref in zip(refs[:n], refs[n:2 * n], refs[2 * n:]):
            acc = jnp.zeros(out_ref.shape, F32)
            for s in range(N_SHARDS):
                acc = acc + jnp.where(me == s, o_ref[s], s_ref[s]).astype(F32)
            out_ref[...] = acc

    def specs(a):
        _, h, c = a.shape
        if h % 32 == 0:
            return (pl.BlockSpec((N_SHARDS, h // 2, c), lambda i: (0, i, 0)), pl.BlockSpec((h // 2, c), lambda i: (i, 0)))
        return (pl.BlockSpec((N_SHARDS, h, c // 2), lambda i: (0, 0, i)), pl.BlockSpec((h, c // 2), lambda i: (0, i)))

    in_specs = [specs(a)[0] for a in slots]
    return list(pl.pallas_call(body, name=name, grid=(2,), in_specs=in_specs + in_specs,
                               out_specs=[specs(a)[1] for a in slots],
                               out_shape=[jax.ShapeDtypeStruct(a.shape[1:], F32) for a in slots],
                               compiler_params=_params(("arbitrary",), VMEM_LIMIT))(*slots, *owns))


def _all_reduce_small(p):
    r = p.shape[0]

    def body(p_ref, o_ref, buf, send_sems, recv_sems):
        x, y, c = _mesh_pos()
        me = 4 * x + 2 * y + c
        buf[me] = p_ref[...]
        cps = []
        k = 0
        for fx in range(2):
            for fy in range(2):
                for fc in range(2):
                    if fx + fy + fc == 0:
                        continue
                    peer = (1 - x if fx else x, 1 - y if fy else y, 1 - c if fc else c)
                    peer_id = 4 * peer[0] + 2 * peer[1] + peer[2]
                    cp = _remote(p_ref, buf.at[me], send_sems.at[k], recv_sems.at[k], peer)
                    cp.start()
                    cps.append((cp, peer_id, k))
                    k += 1
        for cp, peer_id, k in cps:
            _remote(p_ref, buf.at[peer_id], send_sems.at[k], recv_sems.at[k], (x, y, c)).wait_recv()
        for cp, _, _ in cps:
            cp.wait_send()
        acc = buf[0]
        for s in range(1, 8):
            acc = acc + buf[s]
        o_ref[...] = acc

    vm = pl.BlockSpec(memory_space=pltpu.VMEM)
    return pl.pallas_call(
        body, name="all_reduce_small", in_specs=[vm], out_specs=vm,
        out_shape=jax.ShapeDtypeStruct(p.shape, F32),
        scratch_shapes=[pltpu.VMEM((8, r, LANES), F32), pltpu.SemaphoreType.DMA((7,)), pltpu.SemaphoreType.DMA((7,))],
    )(p)


def _adamw(params, name, steps):
    c1 = 1.0 / (1.0 - ADAM_B1 ** ADAM_STEP)
    c2 = 1.0 / (1.0 - ADAM_B2 ** ADAM_STEP)
    n = len(params)

    def body(*refs):
        for a in range(n):
            w_ref, g_ref, m_ref, v_ref = refs[4 * a:4 * a + 4]
            d_ref, nm_ref, nv_ref = refs[4 * n + 3 * a:4 * n + 3 * a + 3]
            gv = g_ref[...]
            nm = ADAM_B1 * m_ref[...] + (1.0 - ADAM_B1) * gv
            nv = ADAM_B2 * v_ref[...] + (1.0 - ADAM_B2) * (gv * gv)
            d_ref[...] = -ADAM_LR * ((nm * c1) / (jnp.sqrt(nv * c2) + ADAM_EPS) + ADAM_WD * w_ref[...])
            nm_ref[...] = nm
            nv_ref[...] = nv

    def spec(shape):
        r, c = shape
        if r % (8 * steps) == 0:
            return pl.BlockSpec((r // steps, c), lambda i: (i, 0))
        assert c % (LANES * steps) == 0
        return pl.BlockSpec((r, c // steps), lambda i: (0, i))

    specs = [spec(w.shape) for w, _, _, _ in params]
    res = pl.pallas_call(
        body, name=name, grid=(steps,),
        in_specs=[s for s in specs for _ in range(4)], out_specs=[s for s in specs for _ in range(3)],
        out_shape=[jax.ShapeDtypeStruct(w.shape, F32) for w, _, _, _ in params for _ in range(3)],
        compiler_params=_params(("arbitrary",), VMEM_LIMIT))(*[a for p4 in params for a in p4])
    return [tuple(res[3 * a:3 * a + 3]) for a in range(n)]


PACK_UNIT = 8 * LANES


def _pack(arrs):
    parts = []
    for a in arrs:
        f = a.reshape(-1).astype(F32)
        parts.append(jnp.pad(f, (0, (-f.shape[0]) % PACK_UNIT)).reshape(-1, LANES))
    return jnp.concatenate(parts, axis=0)


def _unpack(m, shapes):
    outs, row = [], 0
    for s in shapes:
        n = int(np.prod(s))
        rows = -(-n // PACK_UNIT) * 8
        outs.append(m[row:row + rows].reshape(-1)[:n].reshape(s))
        row += rows
    return outs


WEIGHTS = ["ffn1_norm", "ffn1_w_gate", "ffn1_w_up", "ffn1_w_down", "mix_norm", "w_in", "conv_w", "a_log", "dt_bias",
           "gdn_norm_w", "q_norm_w", "k_norm_w", "rel_bias", "w_out", "ffn2_norm", "ffn2_w_gate", "ffn2_w_up",
           "ffn2_w_down", "final_norm"]
BIG = ["ffn1_w_gate", "ffn1_w_up", "ffn1_w_down", "w_in", "w_out", "ffn2_w_gate", "ffn2_w_up", "ffn2_w_down"]
SMALL = [n for n in WEIGHTS if n not in BIG]
COL_SHARDED = ["ffn1_w_gate", "ffn1_w_up", "w_in", "ffn2_w_gate", "ffn2_w_up"]
N_IN_COLS = 3600
TM = 256
TE = 512
ADAM_PIECES = 8
TK = 2048


def kernel(x, ffn1_norm, ffn1_w_gate, ffn1_w_up, ffn1_w_down, mix_norm, w_in, conv_w, a_log, dt_bias, gdn_norm_w, q_norm_w, k_norm_w, rel_bias, w_out, ffn2_norm, ffn2_w_gate, ffn2_w_up, ffn2_w_down, final_norm, loss_target, m_ffn1_norm, m_ffn1_w_gate, m_ffn1_w_up, m_ffn1_w_down, m_mix_norm, m_w_in, m_conv_w, m_a_log, m_dt_bias, m_gdn_norm_w, m_q_norm_w, m_k_norm_w, m_rel_bias, m_w_out, m_ffn2_norm, m_ffn2_w_gate, m_ffn2_w_up, m_ffn2_w_down, m_final_norm, v_ffn1_norm, v_ffn1_w_gate, v_ffn1_w_up, v_ffn1_w_down, v_mix_norm, v_w_in, v_conv_w, v_a_log, v_dt_bias, v_gdn_norm_w, v_q_norm_w, v_k_norm_w, v_rel_bias, v_w_out, v_ffn2_norm, v_ffn2_w_gate, v_ffn2_w_up, v_ffn2_w_down, v_final_norm):
    p = dict(locals())
    xs, target = x[0], loss_target[0]
    t, d = xs.shape
    nc = t // CHUNK
    tk = min(TK, t)
    me = 2 * lax.axis_index("x") + lax.axis_index("y")

    first = ["ffn1_w_gate", "ffn1_w_up", "ffn1_w_down"]
    later = [n for n in BIG if n not in first] + ["conv_w"]
    local = lambda n, a: a[0].T if n in COL_SHARDED else a[0]
    shards = {n: local(n, p[n]).astype(BF16) for n in BIG}
    shards["conv_w"] = conv_w[0]
    gw = dict(zip(first, _all_gather([shards[n] for n in first])))
    f1 = (gw["ffn1_w_gate"], gw["ffn1_w_up"], gw["ffn1_w_down"])
    (x1, xn1, g1, u1), landed = _ffn_fwd(xs, ffn1_norm, *f1, TM, "ffn1_fwd",
                                         exchange=_gather_exchange([shards[n] for n in later]))
    gw.update(zip(later, _gather_forward(landed)))
    wp = gw["w_in"].reshape(N_IN_COLS, d)
    w_out_full = gw["w_out"].reshape(d, d)
    conv_rows = conv_w.shape[1]
    cw = jnp.pad(gw["conv_w"].reshape(N_SHARDS * conv_rows, CONV_TAPS).T, ((0, 8 - CONV_TAPS), (0, 0)))
    gp = jnp.pad(jnp.stack([a_log.reshape(8), dt_bias.reshape(8)]), ((0, 6), (0, LANES - 8)))
    gdn_w = gdn_norm_w.reshape(1, GDN_DIM)
    qw_t = jnp.tile(q_norm_w.reshape(1, SWA_DIM), (1, SWA_HEADS))
    kw_t = jnp.tile(k_norm_w.reshape(1, SWA_DIM), (1, SWA_HEADS))
    bd = jnp.asarray(np.kron(np.eye(2), np.full((SWA_DIM, SWA_DIM), 1.0 / SWA_DIM)), BF16)
    f2 = (gw["ffn2_w_gate"], gw["ffn2_w_up"], gw["ffn2_w_down"])

    hn, qkva, z, ab, qkvb = _mix_in_fwd(x1, mix_norm, wp, TE)
    qkvc, gb = _gdn_prep_fwd(qkva, cw, ab, gp, TE)
    gbt = jnp.transpose(gb[:, :16].reshape(nc, CHUNK, 16), (0, 2, 1))
    o_f, o_b, gdn_saved = _gdn_fwd(qkvc, gb, gbt)
    oa = _gdn_post_fwd(o_f, o_b, z, gdn_w, TE)
    o_swa, o_swa16, swa_saved = _swa_branch_fwd(qkvb, qw_t, kw_t, rel_bias, bd, TE)
    x2 = _mix_out_fwd(x1, oa, o_swa, w_out_full, TE)
    (dx3, xn2, g2, u2, loss_part, d_final), _ = _ffn_fwd(x2, ffn2_norm, *f2, TM, "ffn2_fwd", head=(final_norm, target))

    def pair_sums(partials, tag):
        return _add_pairs(partials, _rs_pair(partials), f"rs_add_{tag}")

    (dx2, dyh2, dg2, du2, h2, d_nw2), _ = _ffn_bwd_dx(dx3, x2, ffn2_norm, g2, u2, *f2, TM, "ffn2_bwd_dx")
    dwg2 = _matmul_tn(dg2, xn2, tk, "ffn2_dwg")
    dwu2 = _matmul_tn(du2, xn2, tk, "ffn2_dwu")
    dwd2 = _matmul_tn(h2, dyh2, tk, "ffn2_dwd")
    (doa, dob, dx2b), lands_f2 = _mix_out_bwd(dx2, w_out_full, TE, exchange=_pair_exchange([dwg2, dwu2, dwd2]))
    sums_f2 = _add_pairs([dwg2, dwu2, dwd2], lands_f2, "rs_add_a")
    dwo = jnp.concatenate([_matmul_tn(oa, dx2b, tk, "w_out_dw_a")[0], _matmul_tn(o_swa16, dx2b, tk, "w_out_dw_b")[0]],
                          axis=0).reshape(N_SHARDS, d // N_SHARDS, d)
    do_g, dz, d_gdnw = _gdn_post_bwd(doa, o_f, o_b, z, gdn_w, TE)
    (dqkvc, dgates), slots_f2 = _gdn_bwd(qkvc, gb, gbt, do_g, gdn_saved, exchange=_scatter_exchange(sums_f2))
    dqkva, dab, dcw, dgp = _gdn_prep_bwd(qkva, cw, ab, gp, dqkvc, dgates, TM)
    dqkvb, d_qw, d_kw, d_rel = _swa_branch_bwd(dob, o_swa, swa_saved, qkvb, qw_t, kw_t, bd, TE)
    dpieces = (dqkva, dz, dab, dqkvb)
    dwp = [_matmul_tn(dp, hn, tk, f"w_in_dw_{i}")[0] for i, dp in enumerate(dpieces)]
    dw_in = jnp.concatenate([dwp[0], dwp[1], dwp[2][:N_GATE_COLS], dwp[3]], axis=0)
    dw_in = dw_in.reshape(N_SHARDS, N_IN_COLS // N_SHARDS, d)
    sums_mix = pair_sums([dw_in, dwo], "b")
    (dx1, d_mixnw), slots_mix = _mix_in_bwd_dx(dx2, x1, mix_norm, dpieces, wp, TE, exchange=_scatter_exchange(sums_mix))
    (gx, dyh1, dg1, du1, h1, d_nw1), _ = _ffn_bwd_dx(dx1, xs, ffn1_norm, g1, u1, *f1, TM, "ffn1_bwd_dx")
    dwg1 = _matmul_tn(dg1, xn1, tk, "ffn1_dwg")
    dwu1 = _matmul_tn(du1, xn1, tk, "ffn1_dwu")
    sums_gu = pair_sums([dwg1, dwu1], "c")
    dwd1, slots_gu = _matmul_tn(h1, dyh1, tk, "ffn1_dwd", exchange=_scatter_exchange(sums_gu))
    sums_d = pair_sums([dwd1], "d")
    slots = slots_gu + _rs_chips(sums_d) + slots_mix + slots_f2
    sums = sums_gu + sums_d + sums_mix + sums_f2
    halves = _sum_slots(slots[:4], sums[:4], "rs_sum_a") + _sum_slots(slots[4:], sums[4:], "rs_sum_b")
    g_big = dict(zip(BIG, _rs_join(halves, [_split_axis(shards[n].shape) for n in BIG])))

    small_partial = {"ffn1_norm": d_nw1, "mix_norm": d_mixnw, "a_log": dgp[0, 0:8], "dt_bias": dgp[1, 0:8],
                     "gdn_norm_w": d_gdnw, "q_norm_w": d_qw, "k_norm_w": d_kw, "rel_bias": d_rel,
                     "ffn2_norm": d_nw2, "final_norm": d_final, "conv_w": dcw[0:CONV_TAPS].T}
    red = _all_reduce_small(_pack([small_partial[n] for n in SMALL] + [loss_part[0, 0:1]]))
    full_shapes = [p[n].shape if n != "conv_w" else (N_SHARDS * conv_rows, CONV_TAPS) for n in SMALL]
    red_parts = _unpack(red, full_shapes + [(1,)])
    loss = red_parts[-1].reshape(())
    g_small = dict(zip(SMALL, red_parts[:-1]))
    g_small["conv_w"] = lax.dynamic_slice_in_dim(g_small["conv_w"], me * conv_rows, conv_rows, 0).reshape(conv_w.shape)

    grads, deltas, new_m, new_v = {}, {}, {}, {}
    quad = lambda n: (local(n, p[n]), g_big[n], local(n, p["m_" + n]), local(n, p["v_" + n]))
    updates = (_adamw([quad(n) for n in BIG[:4]], "adamw_a", ADAM_PIECES)
               + _adamw([quad(n) for n in BIG[4:]], "adamw_b", ADAM_PIECES))
    for n, (dl, nm, nv) in zip(BIG, updates):
        back = (lambda a: a.T[None]) if n in COL_SHARDED else (lambda a: a[None])
        grads[n], deltas[n], new_m[n], new_v[n] = back(g_big[n]), back(dl), back(nm), back(nv)
    packed = [_pack([src[n] for n in SMALL]) for src in
              ({n: p[n] for n in SMALL}, g_small, {n: p["m_" + n] for n in SMALL}, {n: p["v_" + n] for n in SMALL})]
    small_shapes = [p[n].shape for n in SMALL]
    for dst, arr in zip((deltas, new_m, new_v), _adamw([tuple(packed)], "adamw_small", 1)[0]):
        dst.update(zip(SMALL, _unpack(arr, small_shapes)))
    grads.update(g_small)

    return (loss, gx[None], *[grads[n] for n in WEIGHTS], *[deltas[n] for n in WEIGHTS],
            *[new_m[n] for n in WEIGHTS], *[new_v[n] for n in WEIGHTS])
```

```python
import math
from typing import Callable, NamedTuple

import numpy as np
import jax
import jax.numpy as jnp
from jax import lax
from jax.experimental import pallas as pl
from jax.experimental.pallas import tpu as pltpu

F32 = jnp.float32
BF16 = jnp.bfloat16
MESH = pl.DeviceIdType.MESH

EPS = 1e-6
NEG_BIG = -1e30
GDN_HEADS = 4
GDN_DIM = 128
CHUNK = 64
SWA_HEADS = 8
SWA_DIM = 64
PATTERNS = ((128, 1), (512, 4), (2048, 16))
RADIUS = 64
REL_BUCKETS = 32
REL_MAX_DISTANCE = 1024
CONV_TAPS = 5
N_SHARDS = 4
LANES = 128
VMEM_LIMIT = 56 * 1024 * 1024

ADAM_LR, ADAM_B1, ADAM_B2, ADAM_EPS, ADAM_WD, ADAM_STEP = 0.001, 0.9, 0.999, 1e-08, 0.01, 10


def _params(sem=None, vmem=None):
    return pltpu.CompilerParams(dimension_semantics=sem, vmem_limit_bytes=vmem)


def _resident(shape):
    nd = len(shape)
    return pl.BlockSpec(shape, lambda *_: (0,) * nd, pipeline_mode=pl.Buffered(1))


ANY = pl.BlockSpec(memory_space=pl.ANY)


class _Exchange(NamedTuple):
    arrays: tuple
    out_shape: tuple
    start: Callable
    finish: Callable


def _grid_call(body, name, nsteps, in_specs, out_specs, out_shape, operands, scratch=(), exchange=None):
    params = _params(("arbitrary",), VMEM_LIMIT)
    if exchange is None:
        res = pl.pallas_call(body, name=name, grid=(nsteps,), in_specs=list(in_specs), out_specs=list(out_specs),
                             out_shape=list(out_shape), scratch_shapes=list(scratch), compiler_params=params)(*operands)
        return list(res), []
    n_in, n_out, k, n_scr = len(in_specs), len(out_specs), len(exchange.arrays), len(scratch)

    def wrapped(*refs):
        ins, cin = refs[:n_in], refs[n_in:n_in + k]
        outs, cout = refs[n_in + k:n_in + k + n_out], refs[n_in + k + n_out:n_in + 2 * k + n_out]
        rest = refs[n_in + 2 * k + n_out:]
        scr, (send_sems, recv_sems) = rest[:n_scr], rest[n_scr:]

        @pl.when(pl.program_id(0) == 0)
        def _():
            exchange.start(cin, cout, send_sems, recv_sems)

        body(*ins, *outs, *scr)

        @pl.when(pl.program_id(0) == nsteps - 1)
        def _():
            exchange.finish(cin, cout, send_sems, recv_sems)

    res = pl.pallas_call(
        wrapped, name=name, grid=(nsteps,), in_specs=list(in_specs) + [ANY] * k, out_specs=list(out_specs) + [ANY] * k,
        out_shape=list(out_shape) + list(exchange.out_shape),
        scratch_shapes=list(scratch) + [pltpu.SemaphoreType.DMA((k, 4)), pltpu.SemaphoreType.DMA((k, 4))],
        compiler_params=params)(*operands, *exchange.arrays)
    return list(res[:n_out]), list(res[n_out:])


def _dot(a, b):
    return jnp.dot(a.astype(BF16), b.astype(BF16), preferred_element_type=F32)


def _dot_nt(a, b):
    return lax.dot_general(a.astype(BF16), b.astype(BF16), (((1,), (1,)), ((), ())), preferred_element_type=F32)


def _dot_tn(a, b):
    return lax.dot_general(a.astype(BF16), b.astype(BF16), (((0,), (0,)), ((), ())), preferred_element_type=F32)


def _sigmoid(x):
    return 1.0 / (1.0 + jnp.exp(-x))


def _rstd(xf):
    return lax.rsqrt(jnp.mean(xf * xf, axis=-1, keepdims=True) + EPS)


def _rms_bwd(xf, r, nw, dxn):
    xhat = xf * r
    dxh = dxn * nw
    dx = r * (dxh - xhat * jnp.mean(dxh * xhat, axis=-1, keepdims=True))
    return dx, jnp.sum(dxn * xhat, axis=0, keepdims=True)


def _ffn_fwd(x, nw, wg, wu, wd, tm, name, exchange=None, head=None):
    t, d = x.shape
    nj, fs, _ = wg.shape

    def body(x_ref, nw_ref, wg_ref, wu_ref, wd_ref, *rest):
        if head is None:
            y_ref, xn_ref, g_ref, u_ref = rest
        else:
            fw_ref, t_ref, y_ref, xn_ref, g_ref, u_ref, loss_ref, dfw_ref = rest

            @pl.when(pl.program_id(0) == 0)
            def _():
                loss_ref[...] = jnp.zeros_like(loss_ref)
                dfw_ref[...] = jnp.zeros_like(dfw_ref)

        xf = x_ref[...]
        xn = (xf * _rstd(xf) * nw_ref[...]).astype(BF16)
        xn_ref[...] = xn
        acc = jnp.zeros((tm, d), F32)
        for j in range(nj):
            g = _dot_nt(xn, wg_ref[j])
            u = _dot_nt(xn, wu_ref[j])
            h = (g * _sigmoid(g) * u).astype(BF16)
            acc = acc + jnp.dot(h, wd_ref[j], preferred_element_type=F32)
            g_ref[j] = g.astype(BF16)
            u_ref[j] = u.astype(BF16)
        y = xf + 0.5 * acc
        if head is None:
            y_ref[...] = y
        else:
            r = _rstd(y)
            err = y * r * fw_ref[...] - t_ref[...]
            loss_ref[...] += 0.5 * jnp.sum(jnp.mean(err * err, axis=-1, keepdims=True), axis=0, keepdims=True)
            dy, dfw = _rms_bwd(y, r, fw_ref[...], err * (1.0 / d))
            y_ref[...] = dy
            dfw_ref[...] += dfw

    row = pl.BlockSpec((tm, d), lambda i: (i, 0))
    act = pl.BlockSpec((nj, tm, fs), lambda i: (0, i, 0))
    in_specs = [row, _resident((1, d)), _resident(wg.shape), _resident(wu.shape), _resident(wd.shape)]
    out_specs = [row, row, act, act]
    out_shape = [jax.ShapeDtypeStruct((t, d), F32), jax.ShapeDtypeStruct((t, d), BF16),
                 jax.ShapeDtypeStruct((nj, t, fs), BF16), jax.ShapeDtypeStruct((nj, t, fs), BF16)]
    operands = (x, nw, wg, wu, wd)
    if head is not None:
        in_specs += [_resident((1, d)), row]
        out_specs += [pl.BlockSpec((1, LANES), lambda i: (0, 0)), pl.BlockSpec((1, d), lambda i: (0, 0))]
        out_shape += [jax.ShapeDtypeStruct((1, LANES), F32), jax.ShapeDtypeStruct((1, d), F32)]
        operands += tuple(head)
    return _grid_call(body, name, t // tm, in_specs, out_specs, out_shape, operands, exchange=exchange)


def _ffn_bwd_dx(dy, x, nw, g, u, wg, wu, wd, tm, name, exchange=None):
    t, d = x.shape
    nj, fs, _ = wg.shape

    def body(dy_ref, x_ref, nw_ref, g_ref, u_ref, wg_ref, wu_ref, wd_ref,
             dx_ref, dyh_ref, dg_ref, du_ref, h_ref, dnw_ref):
        @pl.when(pl.program_id(0) == 0)
        def _():
            dnw_ref[...] = jnp.zeros_like(dnw_ref)

        dyv = dy_ref[...]
        dyh = (0.5 * dyv).astype(BF16)
        dyh_ref[...] = dyh
        dxn = jnp.zeros((tm, d), F32)
        dh_next = _dot_nt(dyh, wd_ref[0])
        for j in range(nj):
            dh = dh_next
            gv = g_ref[j].astype(F32)
            uv = u_ref[j].astype(F32)
            sg = _sigmoid(gv)
            si = gv * sg
            dg = (dh * uv * (sg * (1.0 + gv * (1.0 - sg)))).astype(BF16)
            du = (dh * si).astype(BF16)
            if j + 1 < nj:
                dh_next = _dot_nt(dyh, wd_ref[j + 1])
            h_ref[j] = (si * uv).astype(BF16)
            dg_ref[j] = dg
            du_ref[j] = du
            dxn = dxn + _dot(dg, wg_ref[j]) + _dot(du, wu_ref[j])
        xf = x_ref[...]
        dxr, dnw = _rms_bwd(xf, _rstd(xf), nw_ref[...], dxn)
        dx_ref[...] = dyv + dxr
        dnw_ref[...] += dnw

    row = pl.BlockSpec((tm, d), lambda i: (i, 0))
    act = pl.BlockSpec((nj, tm, fs), lambda i: (0, i, 0))
    act_shape = jax.ShapeDtypeStruct((nj, t, fs), BF16)
    return _grid_call(
        body, name, t // tm,
        [row, row, _resident((1, d)), act, act, _resident(wg.shape), _resident(wu.shape), _resident(wd.shape)],
        [row, row, act, act, act, pl.BlockSpec((1, d), lambda i: (0, 0))],
        [jax.ShapeDtypeStruct((t, d), F32), jax.ShapeDtypeStruct((t, d), BF16),
         act_shape, act_shape, act_shape, jax.ShapeDtypeStruct((1, d), F32)],
        (dy, x, nw, g, u, wg, wu, wd), exchange=exchange)


def _matmul_tn(a, b, tk, name, exchange=None):
    a3, b3 = a.ndim == 3, b.ndim == 3
    nj = a.shape[0] if a3 else (b.shape[0] if b3 else 1)
    t, m = a.shape[-2:]
    n = b.shape[-1]
    nt = t // tk

    def body(a_ref, b_ref, o_ref, acc_ref):
        k = pl.program_id(0) % nt

        @pl.when(k == 0)
        def _():
            acc_ref[...] = jnp.zeros_like(acc_ref)

        acc_ref[...] += lax.dot_general(a_ref[...], b_ref[...], (((0,), (0,)), ((), ())),
                                        preferred_element_type=F32)

        @pl.when(k == nt - 1)
        def _():
            o_ref[...] = acc_ref[...].astype(o_ref.dtype)

    a_spec = (pl.BlockSpec((None, tk, m), lambda i: (i // nt, i % nt, 0)) if a3
              else pl.BlockSpec((tk, m), lambda i: (i % nt, 0)))
    b_spec = (pl.BlockSpec((None, tk, n), lambda i: (i // nt, i % nt, 0)) if b3
              else pl.BlockSpec((tk, n), lambda i: (i % nt, 0)))
    (out,), landed = _grid_call(
        body, name, nj * nt, [a_spec, b_spec], [pl.BlockSpec((None, m, n), lambda i: (i // nt, 0, 0))],
        [jax.ShapeDtypeStruct((nj, m, n), BF16)], (a, b), scratch=[pltpu.VMEM((m, n), F32)], exchange=exchange)
    return out if exchange is None else (out, landed)


N_GATE_COLS = 4 * GDN_HEADS
P_QKVA, P_Z, P_AB, P_QKVB = (0, 1536), (1536, 2048), (2048, 2048 + LANES), (2048 + N_GATE_COLS, 3600)
P_PIECES = (P_QKVA, P_Z, P_AB, P_QKVB)


def _mix_in_fwd(x1, nw, wp, tm):
    t, d = x1.shape

    def body(x_ref, nw_ref, w_ref, hn_ref, *outs):
        xf = x_ref[...]
        xn = (xf * _rstd(xf) * nw_ref[...]).astype(BF16)
        hn_ref[...] = xn
        for (a, b), o_ref in zip(P_PIECES, outs):
            o_ref[...] = _dot_nt(xn, w_ref[a:b, :])

    row = pl.BlockSpec((tm, d), lambda i: (i, 0))
    return pl.pallas_call(
        body, name="mix_in_fwd", grid=(t // tm,),
        in_specs=[row, _resident((1, d)), _resident(wp.shape)],
        out_specs=[row] + [pl.BlockSpec((tm, b - a), lambda i: (i, 0)) for a, b in P_PIECES],
        out_shape=[jax.ShapeDtypeStruct((t, d), BF16)]
                  + [jax.ShapeDtypeStruct((t, b - a), F32) for a, b in P_PIECES],
        compiler_params=_params(("arbitrary",), VMEM_LIMIT),
    )(x1, nw, wp)


def _mix_in_bwd_dx(dx, x1, nw, dpieces, wp, tm, exchange=None):
    t, d = x1.shape

    def body(dx_ref, x_ref, nw_ref, p0, p1, p2, p3, w_ref, o_ref, dnw_ref):
        @pl.when(pl.program_id(0) == 0)
        def _():
            dnw_ref[...] = jnp.zeros_like(dnw_ref)

        dh = jnp.zeros((tm, d), F32)
        for (a, b), p_ref in zip(P_PIECES, (p0, p1, p2, p3)):
            dh = dh + _dot(p_ref[...], w_ref[a:b, :])
        xf = x_ref[...]
        dxr, dnw = _rms_bwd(xf, _rstd(xf), nw_ref[...], dh)
        o_ref[...] = dx_ref[...] + dxr
        dnw_ref[...] += dnw

    row = pl.BlockSpec((tm, d), lambda i: (i, 0))
    return _grid_call(
        body, "mix_in_bwd_dx", t // tm,
        [row, row, _resident((1, d))]
        + [pl.BlockSpec((tm, b - a), lambda i: (i, 0)) for a, b in P_PIECES] + [_resident(wp.shape)],
        [row, pl.BlockSpec((1, d), lambda i: (0, 0))],
        [jax.ShapeDtypeStruct((t, d), F32), jax.ShapeDtypeStruct((1, d), F32)],
        (dx, x1, nw, *dpieces, wp), exchange=exchange)


def _mix_out_fwd(x1, oa, ob, w, tm):
    t, d = x1.shape
    half = oa.shape[1]

    def body(x_ref, oa_ref, ob_ref, w_ref, o_ref):
        o_ref[...] = (x_ref[...] + _dot(oa_ref[...], w_ref[0:half, :]) + _dot(ob_ref[...], w_ref[half:2 * half, :]))

    row = pl.BlockSpec((tm, d), lambda i: (i, 0))
    hrow = pl.BlockSpec((tm, half), lambda i: (i, 0))
    return pl.pallas_call(
        body, name="mix_out_fwd", grid=(t // tm,),
        in_specs=[row, hrow, hrow, _resident(w.shape)],
        out_specs=row, out_shape=jax.ShapeDtypeStruct((t, d), F32),
        compiler_params=_params(("arbitrary",), VMEM_LIMIT),
    )(x1, oa, ob, w)


def _mix_out_bwd(dx2, w, tm, exchange=None):
    t, d = dx2.shape
    half = w.shape[0] // 2

    def body(dx_ref, w_ref, doa_ref, dob_ref, dxb_ref):
        dxb = dx_ref[...].astype(BF16)
        dxb_ref[...] = dxb
        doa_ref[...] = _dot_nt(dxb, w_ref[0:half, :])
        dob_ref[...] = _dot_nt(dxb, w_ref[half:2 * half, :])

    row = pl.BlockSpec((tm, d), lambda i: (i, 0))
    hrow = pl.BlockSpec((tm, half), lambda i: (i, 0))
    return _grid_call(
        body, "mix_out_bwd", t // tm, [row, _resident(w.shape)], [hrow, hrow, row],
        [jax.ShapeDtypeStruct((t, half), F32), jax.ShapeDtypeStruct((t, half), F32), jax.ShapeDtypeStruct((t, d), BF16)],
        (dx2, w), exchange=exchange)


HALO = 8


def _halo_row_specs(tr, cols, nrow8):
    per = tr // HALO
    return [pl.BlockSpec((tr, cols), lambda i: (i, 0)),
            pl.BlockSpec((HALO, cols), lambda i: (jnp.maximum(i * per - 1, 0), 0)),
            pl.BlockSpec((HALO, cols), lambda i: (jnp.minimum((i + 1) * per, nrow8 - 1), 0))]


def _fill_window(win_ref, cb, xm, xp, xn, first, last):
    tr = xm.shape[0]
    cols = slice(cb * LANES, (cb + 1) * LANES)
    win_ref[cb, 0:HALO, :] = jnp.where(first, 0.0, xp[:, cols])
    win_ref[cb, HALO:HALO + tr, :] = xm[:, cols]
    win_ref[cb, HALO + tr:HALO + tr + HALO, :] = jnp.where(last, 0.0, xn[:, cols])


def _conv_taps(win_ref, cb, cw_ref, start, rows):
    cols = slice(cb * LANES, (cb + 1) * LANES)
    acc = None
    for j in range(CONV_TAPS):
        term = win_ref[cb, pl.ds(start + j - CONV_TAPS // 2, rows), :] * cw_ref[j:j + 1, cols]
        acc = term if acc is None else acc + term
    return acc


def _softplus(x):
    u = jnp.exp(-jnp.abs(x))
    w = 1.0 + u
    log1p = jnp.where(w == 1.0, u, jnp.log(w) * u / jnp.where(w == 1.0, 1.0, w - 1.0))
    return jnp.maximum(x, 0.0) + log1p


def _gdn_prep_fwd(qkva, cw, ab, gp, tr):
    t, c = qkva.shape
    nt = t // tr
    ncb = c // LANES

    def body(xm, xp, xn, cw_ref, ab_ref, gp_ref, o_ref, gb_ref, xw_ref):
        i = pl.program_id(0)
        first, last = i == 0, i == nt - 1
        for cb in range(ncb):
            cols = slice(cb * LANES, (cb + 1) * LANES)
            _fill_window(xw_ref, cb, xm, xp, xn, first, last)
            pre = _conv_taps(xw_ref, cb, cw_ref, HALO, tr)
            y = pre * _sigmoid(pre)
            if cb < 2 * GDN_HEADS:
                y = y * lax.rsqrt(jnp.sum(y * y, axis=-1, keepdims=True) + EPS)
            if cb < GDN_HEADS:
                y = y * (GDN_DIM ** -0.5)
            o_ref[:, cols] = y
        abv = ab_ref[...]
        lane = lax.broadcasted_iota(jnp.int32, abv.shape, 1)
        g = -jnp.exp(gp_ref[0:1, :]) * _softplus(abv + gp_ref[1:2, :])
        gb_ref[...] = jnp.where(lane < 8, g, jnp.where(lane < 16, _sigmoid(abv), 0.0))

    return pl.pallas_call(
        body, name="gdn_prep_fwd", grid=(nt,),
        in_specs=_halo_row_specs(tr, c, t // HALO)
                 + [_resident(cw.shape), pl.BlockSpec((tr, LANES), lambda i: (i, 0)), _resident(gp.shape)],
        out_specs=[pl.BlockSpec((tr, c), lambda i: (i, 0)), pl.BlockSpec((tr, LANES), lambda i: (i, 0))],
        out_shape=[jax.ShapeDtypeStruct((t, c), F32), jax.ShapeDtypeStruct((t, LANES), F32)],
        scratch_shapes=[pltpu.VMEM((ncb, tr + 2 * HALO, LANES), F32)],
        compiler_params=_params(("arbitrary",), VMEM_LIMIT),
    )(qkva, qkva, qkva, cw, ab, gp)


def _gdn_prep_bwd(qkva, cw, ab, gp, dy, dgates, tr):
    t, c = qkva.shape
    nt = t // tr
    ncb = c // LANES

    ext = HALO // 2
    rows_ext = tr + 2 * ext

    def body(xm, xp, xn, fm, fp, fn, cw_ref, ab_ref, gp_ref, gf_ref, dx_ref, dab_ref, dcw_ref, dgp_ref,
             xw_ref, dyw_ref, dp_ref):
        i = pl.program_id(0)
        first, last = i == 0, i == nt - 1

        @pl.when(first)
        def _():
            dcw_ref[...] = jnp.zeros_like(dcw_ref)
            dgp_ref[...] = jnp.zeros_like(dgp_ref)

        sub8 = lax.broadcasted_iota(jnp.int32, (8, LANES), 0)
        for cb in range(ncb):
            cols = slice(cb * LANES, (cb + 1) * LANES)
            _fill_window(xw_ref, cb, xm, xp, xn, first, last)
            _fill_window(dyw_ref, cb, fm, fp, fn, first, last)
            pre = _conv_taps(xw_ref, cb, cw_ref, HALO - ext, rows_ext)
            dyw = dyw_ref[cb, pl.ds(HALO - ext, rows_ext), :]
            sg = _sigmoid(pre)
            s = pre * sg
            if cb < 2 * GDN_HEADS:
                scale = (GDN_DIM ** -0.5) if cb < GDN_HEADS else 1.0
                r = lax.rsqrt(jnp.sum(s * s, axis=-1, keepdims=True) + EPS)
                dn = dyw * scale
                ds = r * dn - s * (r * r * r) * jnp.sum(dn * s, axis=-1, keepdims=True)
            else:
                ds = dyw
            dp_ref[cb] = ds * (sg * (1.0 + pre * (1.0 - sg)))
            dpre = dp_ref[cb, pl.ds(ext, tr), :]
            dx = None
            dcw = jnp.zeros((8, LANES), F32)
            for j in range(CONV_TAPS):
                off = j - CONV_TAPS // 2
                term = dp_ref[cb, pl.ds(ext - off, tr), :] * cw_ref[j:j + 1, cols]
                dx = term if dx is None else dx + term
                tap = jnp.sum(dpre * xw_ref[cb, pl.ds(HALO + off, tr), :], axis=0, keepdims=True)
                dcw = dcw + jnp.where(sub8 == j, tap, 0.0)
            dx_ref[:, cols] = dx.astype(BF16)
            dcw_ref[:, cols] += dcw

        abv = ab_ref[...]
        dgb = gf_ref[...]
        lane = lax.broadcasted_iota(jnp.int32, abv.shape, 1)
        nea = -jnp.exp(gp_ref[0:1, :])
        xs = abv + gp_ref[1:2, :]
        g = nea * _softplus(xs)
        beta = _sigmoid(abv)
        da = dgb * nea * _sigmoid(xs)
        dab = jnp.where(lane < 8, da, jnp.where(lane < 16, dgb * beta * (1.0 - beta), 0.0))
        dab_ref[...] = dab.astype(BF16)
        keep = lane[0:1, :] < 8
        dalog = jnp.where(keep, jnp.sum(dgb * g, axis=0, keepdims=True), 0.0)
        ddtb = jnp.where(keep, jnp.sum(da, axis=0, keepdims=True), 0.0)
        dgp_ref[...] += jnp.where(sub8 == 0, dalog, 0.0) + jnp.where(sub8 == 1, ddtb, 0.0)

    lrow = pl.BlockSpec((tr, LANES), lambda i: (i, 0))
    halo = _halo_row_specs(tr, c, t // HALO)
    return pl.pallas_call(
        body, name="gdn_prep_bwd", grid=(nt,),
        in_specs=halo + halo + [_resident(cw.shape), lrow, _resident(gp.shape), lrow],
        out_specs=[pl.BlockSpec((tr, c), lambda i: (i, 0)), lrow,
                   pl.BlockSpec(cw.shape, lambda i: (0, 0)), pl.BlockSpec(gp.shape, lambda i: (0, 0))],
        out_shape=[jax.ShapeDtypeStruct((t, c), BF16), jax.ShapeDtypeStruct((t, LANES), BF16),
                   jax.ShapeDtypeStruct(cw.shape, F32), jax.ShapeDtypeStruct(gp.shape, F32)],
        scratch_shapes=[pltpu.VMEM((ncb, tr + 2 * HALO, LANES), F32), pltpu.VMEM((ncb, tr + 2 * HALO, LANES), F32),
                        pltpu.VMEM((ncb, rows_ext, LANES), F32)],
        compiler_params=_params(("arbitrary",), VMEM_LIMIT),
    )(qkva, qkva, qkva, dy, dy, dy, cw, ab, gp, dgates)


def _chunk_masks(lower):
    ii = lax.broadcasted_iota(jnp.int32, (CHUNK, CHUNK), 0)
    jj = lax.broadcasted_iota(jnp.int32, (CHUNK, CHUNK), 1)
    incl = (ii >= jj) if lower else (ii <= jj)
    strict = (ii > jj) if lower else (ii < jj)
    return ii, jj, incl, strict


def _dot3(a, b):
    ah = a.astype(BF16)
    al = (a - ah.astype(F32)).astype(BF16)
    bh = b.astype(BF16)
    bl = (b - bh.astype(F32)).astype(BF16)
    d = lambda u, v: jnp.dot(u, v, preferred_element_type=F32)
    return d(ah, bh) + (d(ah, bl) + d(al, bh))


def _tri_inv_many(lmats, ii, jj):
    m16 = (ii // 16) == (jj // 16)
    m32 = (ii // 32) == (jj // 32)
    eye = jnp.where(ii == jj, 1.0, 0.0)
    l16 = [jnp.where(m16, l, 0.0) for l in lmats]
    p2 = [_dot3(a, a) for a in l16]
    p4 = [_dot3(a, a) for a in p2]
    p8 = [_dot3(a, a) for a in p4]
    xs = [eye - a for a in l16]
    for ps in (p2, p4, p8):
        xs = [x + _dot3(x, p) for x, p in zip(xs, ps)]
    for off in ([jnp.where(m32 & jnp.logical_not(m16), l, 0.0) for l in lmats],
                [jnp.where(m32, 0.0, l) for l in lmats]):
        ys = [_dot3(x, c) for x, c in zip(xs, off)]
        xs = [x - _dot3(y, x) for x, y in zip(xs, ys)]
    return xs


def _col_to_row(col, ii, jj):
    return jnp.sum(jnp.where(ii == jj, col, 0.0), axis=0, keepdims=True)


def _row_to_col(row, ii, jj):
    return jnp.sum(jnp.where(ii == jj, row, 0.0), axis=1, keepdims=True)


def _chain_common(q, k, v, graw_col, graw_row, bcol, masks):
    ii, jj, incl, strict = masks
    inclt = jnp.logical_not(strict)
    gcol = jnp.sum(jnp.where(incl, graw_row, 0.0), axis=1, keepdims=True)
    grow = jnp.sum(jnp.where(inclt, graw_col, 0.0), axis=0, keepdims=True)
    glast = jnp.sum(graw_row, axis=1, keepdims=True)
    decay = jnp.where(incl, jnp.exp(jnp.where(incl, gcol - grow, 0.0)), 0.0)
    kb = k * bcol
    vb = v * bcol
    eg = jnp.exp(gcol)
    ek = jnp.exp(glast - gcol)
    kbg = kb * eg
    amat = _dot_nt(kb, k)
    qk = _dot_nt(q, k)
    return dict(gcol=gcol, glast=glast, decay=decay, kb=kb, vb=vb, eg=eg, ek=ek, kbg=kbg, amat=amat, qk=qk,
                intra=qk * decay, qg=q * eg, kdec=k * ek)


def _gdn_fwd(qkvc, gb, gbt):
    tm, u, w, qg, kd, intra, egl = _gdn_local_fwd(qkvc, gb, gbt)
    o_f, o_b, s_f, s_b, vn_f, vn_b = _gdn_scan_fwd(u, w, qg, kd, intra, egl, qkvc.shape[0])
    return o_f, o_b, dict(tm=tm, w=w, qg=qg, kd=kd, intra=intra, egl=egl, s=(s_f, s_b), vn=(vn_f, vn_b))


N_CHAINS = 2 * GDN_HEADS


LOCAL_CHUNKS = 4


def _load_chains(x_ref, g_ref, gt_ref, cc=0):
    hd = GDN_HEADS * GDN_DIM
    rows = slice(cc * CHUNK, (cc + 1) * CHUNK)
    chains = []
    for d in range(2):
        masks = _chunk_masks(d == 0)
        for h in range(GDN_HEADS):
            ch = d * GDN_HEADS + h
            q = x_ref[rows, h * GDN_DIM:(h + 1) * GDN_DIM]
            k = x_ref[rows, hd + h * GDN_DIM:hd + (h + 1) * GDN_DIM]
            v = x_ref[rows, 2 * hd + h * GDN_DIM:2 * hd + (h + 1) * GDN_DIM]
            bcol = g_ref[rows, 8 + ch:9 + ch]
            cm = _chain_common(q, k, v, g_ref[rows, ch:ch + 1], gt_ref[cc, ch:ch + 1, :], bcol, masks)
            chains.append(dict(cm, q=q, k=k, v=v, bcol=bcol, masks=masks, ch=ch, h=h, cc=cc))
    return chains


def _chain_shape(rows, cols, dtype):
    return lambda nc: jax.ShapeDtypeStruct((nc, N_CHAINS, rows, cols), dtype)


def _gdn_local_fwd(qkvc, gb, gbt):
    t = qkvc.shape[0]
    nc = t // CHUNK
    hd = GDN_HEADS * GDN_DIM

    def body(x_ref, g_ref, gt_ref, t_ref, u_ref, w_ref, qg_ref, kd_ref, in_ref, eg_ref):
        chains = [c for cc in range(LOCAL_CHUNKS) for c in _load_chains(x_ref, g_ref, gt_ref, cc)]
        ii, jj = chains[0]["masks"][0:2]
        tms = _tri_inv_many([jnp.where(c["masks"][3], c["amat"] * c["decay"], 0.0) for c in chains], ii, jj)
        uws = [_dot(tm, jnp.concatenate([c["vb"], c["kbg"]], axis=1)) for tm, c in zip(tms, chains)]
        for c, tm, uw in zip(chains, tms, uws):
            cc, ch = c["cc"], c["ch"]
            t_ref[cc, ch] = tm
            u_ref[cc, ch] = uw[:, :GDN_DIM]
            w_ref[cc, ch] = uw[:, GDN_DIM:].astype(BF16)
            qg_ref[cc, ch] = c["qg"].astype(BF16)
            kd_ref[cc, ch] = c["kdec"].astype(BF16)
            in_ref[cc, ch] = c["intra"].astype(BF16)
            eg_ref[cc, ch:ch + 1, :] = jnp.broadcast_to(jnp.exp(c["glast"]), (1, LANES))

    lc = LOCAL_CHUNKS
    blk = lambda rows, cols: pl.BlockSpec((lc, N_CHAINS, rows, cols), lambda n: (n, 0, 0, 0))
    shapes = [_chain_shape(CHUNK, CHUNK, F32), _chain_shape(CHUNK, GDN_DIM, F32), _chain_shape(CHUNK, GDN_DIM, BF16),
              _chain_shape(CHUNK, GDN_DIM, BF16), _chain_shape(CHUNK, GDN_DIM, BF16), _chain_shape(CHUNK, CHUNK, BF16)]
    return tuple(pl.pallas_call(
        body, name="gdn_local_fwd", grid=(nc // lc,),
        in_specs=[pl.BlockSpec((lc * CHUNK, 3 * hd), lambda n: (n, 0)), pl.BlockSpec((lc * CHUNK, LANES), lambda n: (n, 0)),
                  pl.BlockSpec((lc, 16, CHUNK), lambda n: (n, 0, 0))],
        out_specs=[blk(CHUNK, CHUNK), blk(CHUNK, GDN_DIM), blk(CHUNK, GDN_DIM), blk(CHUNK, GDN_DIM),
                   blk(CHUNK, GDN_DIM), blk(CHUNK, CHUNK), pl.BlockSpec((lc, N_CHAINS, LANES), lambda n: (n, 0, 0))],
        out_shape=[s(nc) for s in shapes] + [jax.ShapeDtypeStruct((nc, N_CHAINS, LANES), F32)],
        compiler_params=_params(("arbitrary",), VMEM_LIMIT),
    )(qkvc, gb, gbt))


SCAN_CHUNKS = 8


def _dir_specs(nc, rev):
    nb = nc // SCAN_CHUNKS

    def spec(d, rows, cols, own=False):
        chunk = (lambda n: n) if (d == 0) != rev else (lambda n: nb - 1 - n)
        blk = 0 if own else d
        if rows is None:
            return pl.BlockSpec((SCAN_CHUNKS, GDN_HEADS if own else N_CHAINS, cols), lambda n: (chunk(n), 0, 0))
        return pl.BlockSpec((SCAN_CHUNKS, GDN_HEADS, rows, cols), lambda n: (chunk(n), blk, 0, 0))

    def rows_spec(d, cols):
        chunk = (lambda n: n) if (d == 0) != rev else (lambda n: nb - 1 - n)
        return pl.BlockSpec((SCAN_CHUNKS * CHUNK, cols), lambda n: (chunk(n), 0))

    def order(d):
        return list(range(SCAN_CHUNKS)) if (d == 0) != rev else list(range(SCAN_CHUNKS - 1, -1, -1))
    return spec, rows_spec, order


def _gdn_scan_fwd(u, w, qg, kd, intra, egl, t):
    nc = t // CHUNK
    hd = GDN_HEADS * GDN_DIM

    def body(*refs):
        ins, outs, state = refs[:12], refs[12:18], refs[18]
        @pl.when(pl.program_id(0) == 0)
        def _():
            state[...] = jnp.zeros_like(state)

        chains = [(d, h) for d in range(2) for h in range(GDN_HEADS)]
        states = [state[ch] for ch in range(N_CHAINS)]
        for step in range(SCAN_CHUNKS):
            at = [order(d)[step] for d in range(2)]
            pick = lambda k, d, h: ins[2 * k + d][at[d], h]
            sbs = [s.astype(BF16) for s in states]
            ws = [_dot(pick(1, d, h), sb) for (d, h), sb in zip(chains, sbs)]
            o1 = [_dot(pick(2, d, h), sb) for (d, h), sb in zip(chains, sbs)]
            vns = [(pick(0, d, h) - wsb).astype(BF16) for (d, h), wsb in zip(chains, ws)]
            o2 = [_dot(pick(4, d, h), vn) for (d, h), vn in zip(chains, vns)]
            kv = [_dot_tn(pick(3, d, h), vn) for (d, h), vn in zip(chains, vns)]
            new_states = []
            for ch, (d, h) in enumerate(chains):
                outs[d][at[d] * CHUNK:(at[d] + 1) * CHUNK, h * GDN_DIM:(h + 1) * GDN_DIM] = o1[ch] + o2[ch]
                outs[2 + d][at[d], h] = states[ch]
                outs[4 + d][at[d], h] = vns[ch]
                new_states.append(states[ch] * ins[10 + d][at[d], ch:ch + 1, :] + kv[ch])
            states = new_states
        for ch in range(N_CHAINS):
            state[ch] = states[ch]

    spec, rows_spec, order = _dir_specs(nc, False)
    pair = lambda rows, cols, own=False: [spec(0, rows, cols, own), spec(1, rows, cols, own)]
    s_shape = jax.ShapeDtypeStruct((nc, GDN_HEADS, GDN_DIM, GDN_DIM), F32)
    vn_shape = jax.ShapeDtypeStruct((nc, GDN_HEADS, CHUNK, GDN_DIM), BF16)
    return pl.pallas_call(
        body, name="gdn_scan_fwd", grid=(nc // SCAN_CHUNKS,),
        in_specs=(pair(CHUNK, GDN_DIM) + pair(CHUNK, GDN_DIM) + pair(CHUNK, GDN_DIM) + pair(CHUNK, GDN_DIM)
                  + pair(CHUNK, CHUNK) + pair(None, LANES)),
        out_specs=([rows_spec(0, hd), rows_spec(1, hd)] + pair(GDN_DIM, GDN_DIM, True)
                   + pair(CHUNK, GDN_DIM, True)),
        out_shape=[jax.ShapeDtypeStruct((t, hd), F32), jax.ShapeDtypeStruct((t, hd), F32),
                   s_shape, s_shape, vn_shape, vn_shape],
        scratch_shapes=[pltpu.VMEM((N_CHAINS, GDN_DIM, GDN_DIM), F32)],
        compiler_params=_params(("arbitrary",), VMEM_LIMIT),
    )(u, u, w, w, qg, qg, kd, kd, intra, intra, egl, egl)


def _gdn_bwd(qkvc, gb, gbt, do, saved, exchange=None):
    scan = _gdn_scan_bwd(do, saved, qkvc.shape[0])
    return _gdn_local_bwd(qkvc, gb, gbt, do, saved, scan, exchange)


def _gdn_scan_bwd(do, saved, t):
    nc = t // CHUNK
    hd = GDN_HEADS * GDN_DIM

    def body(*refs):
        ins, outs, dstate = refs[:16], refs[16:26], refs[26]
        @pl.when(pl.program_id(0) == 0)
        def _():
            dstate[...] = jnp.zeros_like(dstate)

        chains = [(d, h) for d in range(2) for h in range(GDN_HEADS)]
        dss = [dstate[ch] for ch in range(N_CHAINS)]
        for step in range(SCAN_CHUNKS):
            at = [order(d)[step] for d in range(2)]
            pick = lambda k, d, h: ins[2 * k + d][at[d], h]
            dsbs = [ds.astype(BF16) for ds in dss]
            ss = [pick(1, d, h) for d, h in chains]
            sbs = [s.astype(BF16) for s in ss]
            dos = [ins[d][at[d] * CHUNK:(at[d] + 1) * CHUNK, h * GDN_DIM:(h + 1) * GDN_DIM].astype(BF16)
                   for d, h in chains]
            dv1 = [_dot_tn(pick(5, d, h), dov) for (d, h), dov in zip(chains, dos)]
            dv2 = [_dot(pick(4, d, h), dsb) for (d, h), dsb in zip(chains, dsbs)]
            ds1 = [_dot_tn(pick(3, d, h), dov) for (d, h), dov in zip(chains, dos)]
            dkds = [_dot_nt(pick(6, d, h), dsb) for (d, h), dsb in zip(chains, dsbs)]
            dqgs = [_dot_nt(dov, sb) for dov, sb in zip(dos, sbs)]
            dvns = [(a + b).astype(BF16) for a, b in zip(dv1, dv2)]
            ds2 = [_dot_tn(pick(2, d, h), dvn) for (d, h), dvn in zip(chains, dvns)]
            dws = [_dot_nt(dvn, sb) for dvn, sb in zip(dvns, sbs)]
            new_dss = []
            for ch, (d, h) in enumerate(chains):
                egl = ins[14 + d][at[d], ch:ch + 1, :]
                outs[d][at[d], h] = dvns[ch]
                outs[2 + d][at[d], h] = (-dws[ch]).astype(BF16)
                outs[4 + d][at[d], h] = dqgs[ch]
                outs[6 + d][at[d], h] = dkds[ch]
                outs[8 + d][at[d], h:h + 1, :] = egl * jnp.sum(jnp.sum(ss[ch] * dss[ch], axis=1, keepdims=True),
                                                               axis=0, keepdims=True)
                new_dss.append(ds1[ch] + egl * dss[ch] - ds2[ch])
            dss = new_dss
        for ch in range(N_CHAINS):
            dstate[ch] = dss[ch]

    spec, rows_spec, order = _dir_specs(nc, True)
    pair = lambda rows, cols, own=False: [spec(0, rows, cols, own), spec(1, rows, cols, own)]
    s_f, s_b = saved["s"]
    vn_f, vn_b = saved["vn"]
    w, qg, kd, intra, egl = saved["w"], saved["qg"], saved["kd"], saved["intra"], saved["egl"]
    own = lambda rows, cols, dtype: jax.ShapeDtypeStruct((nc, GDN_HEADS, rows, cols), dtype)
    row_shape = jax.ShapeDtypeStruct((nc, GDN_HEADS, LANES), F32)
    return pl.pallas_call(
        body, name="gdn_scan_bwd", grid=(nc // SCAN_CHUNKS,),
        in_specs=([rows_spec(0, hd), rows_spec(1, hd)] + pair(GDN_DIM, GDN_DIM, True) + pair(CHUNK, GDN_DIM)
                  + pair(CHUNK, GDN_DIM) + pair(CHUNK, GDN_DIM) + pair(CHUNK, CHUNK) + pair(CHUNK, GDN_DIM, True)
                  + pair(None, LANES)),
        out_specs=(pair(CHUNK, GDN_DIM, True) + pair(CHUNK, GDN_DIM, True) + pair(CHUNK, GDN_DIM, True)
                   + pair(CHUNK, GDN_DIM, True) + pair(None, LANES, True)),
        out_shape=[own(CHUNK, GDN_DIM, BF16)] * 4 + [own(CHUNK, GDN_DIM, F32)] * 4 + [row_shape] * 2,
        scratch_shapes=[pltpu.VMEM((N_CHAINS, GDN_DIM, GDN_DIM), F32)],
        compiler_params=_params(("arbitrary",), VMEM_LIMIT),
    )(do, do, s_f, s_b, w, w, qg, qg, kd, kd, intra, intra, vn_f, vn_b, egl, egl)


def _dot3_nt(a, b):
    ah = a.astype(BF16)
    al = (a - ah.astype(F32)).astype(BF16)
    bh = b.astype(BF16)
    bl = (b - bh.astype(F32)).astype(BF16)
    return _dot_nt(ah, bh) + (_dot_nt(ah, bl) + _dot_nt(al, bh))


def _dot3_tn(a, b):
    ah = a.astype(BF16)
    al = (a - ah.astype(F32)).astype(BF16)
    bh = b.astype(BF16)
    bl = (b - bh.astype(F32)).astype(BF16)
    return _dot_tn(ah, bh) + (_dot_tn(ah, bl) + _dot_tn(al, bh))


def _gdn_local_bwd(qkvc, gb, gbt, do, saved, scan, exchange=None):
    t = qkvc.shape[0]
    nc = t // CHUNK
    hd = GDN_HEADS * GDN_DIM

    def body(*refs):
        x_ref, g_ref, gt_ref, do_ref, t_ref = refs[:5]
        per_dir = refs[5:17]
        dx_ref, dg_ref = refs[17:]
        chains = [c for cc in range(LOCAL_CHUNKS) for c in _load_chains(x_ref, g_ref, gt_ref, cc)]
        lane = lax.broadcasted_iota(jnp.int32, (CHUNK, LANES), 1)
        dgates = [jnp.zeros((CHUNK, LANES), F32) for _ in range(LOCAL_CHUNKS)]
        for c in chains:
            d = c["ch"] // GDN_HEADS
            vn_ref, dvn_ref, dw_ref, dqg_ref, dkd_ref, dgl_ref = per_dir[d::2]
            h, cc = c["h"], c["cc"]
            rows = slice(cc * CHUNK, (cc + 1) * CHUNK)
            c.update(tm=t_ref[cc, c["ch"]], dov=do_ref[rows, h * GDN_DIM:(h + 1) * GDN_DIM], vnew=vn_ref[cc, h],
                     dvnew=dvn_ref[cc, h], dw=dw_ref[cc, h], dqg=dqg_ref[cc, h], dkdec=dkd_ref[cc, h],
                     dglast=dgl_ref[cc, h:h + 1, 0:1])
        dintras = [_dot_nt(c["dov"], c["vnew"]) for c in chains]
        dts = [_dot_nt(c["dvnew"], c["vb"]) + _dot_nt(c["dw"], c["kbg"]) for c in chains]
        dvbs = [_dot_tn(c["tm"], c["dvnew"]) for c in chains]
        dkbgs = [_dot_tn(c["tm"], c["dw"]) for c in chains]
        tdts = [_dot3_nt(dt, c["tm"]) for dt, c in zip(dts, chains)]
        dls = [jnp.where(c["masks"][3], -_dot3_tn(c["tm"], tdt), 0.0) for tdt, c in zip(tdts, chains)]
        das = [dl * c["decay"] for dl, c in zip(dls, chains)]
        dqks = [jnp.where(c["masks"][2], di, 0.0) * c["decay"] for di, c in zip(dintras, chains)]
        dkb1 = [_dot(da, c["k"]) for da, c in zip(das, chains)]
        dk1 = [_dot_tn(da, c["kb"]) for da, c in zip(das, chains)]
        dk2 = [_dot_tn(dqk, c["q"]) for dqk, c in zip(dqks, chains)]
        dq1 = [_dot(dqk, c["k"]) for dqk, c in zip(dqks, chains)]
        grads, mms, p_gs, p_betas, p_kds = [], [], [], [], []
        for n, c in enumerate(chains):
            incl = c["masks"][2]
            dkb = dkb1[n] + dkbgs[n] * c["eg"]
            kd = c["dkdec"] * c["kdec"]
            mms.append((dls[n] * c["amat"] + jnp.where(incl, dintras[n], 0.0) * c["qk"]) * c["decay"])
            p_gs.append(c["dqg"] * c["qg"] - kd + dkbgs[n] * c["kbg"])
            p_betas.append(dkb * c["k"] + dvbs[n] * c["v"])
            p_kds.append(kd)
            grads.append((dq1[n] + c["dqg"] * c["eg"],
                          dk1[n] + dk2[n] + c["dkdec"] * c["ek"] + dkb * c["bcol"],
                          dvbs[n] * c["bcol"]))
        row_sums = [jnp.sum(mm, axis=1, keepdims=True) for mm in mms]
        col_sums = [jnp.sum(mm, axis=0, keepdims=True) for mm in mms]
        g_sums = [jnp.sum(pg, axis=1, keepdims=True) for pg in p_gs]
        dbetas = [jnp.sum(pb, axis=1, keepdims=True) for pb in p_betas]
        kd_tots = [jnp.sum(jnp.sum(pk, axis=1, keepdims=True), axis=0, keepdims=True) for pk in p_kds]
        dgcs = [rs - _row_to_col(cs, *c["masks"][0:2]) + gs for rs, cs, gs, c in zip(row_sums, col_sums, g_sums, chains)]
        dgrs = [_col_to_row(dgc, *c["masks"][0:2]) for dgc, c in zip(dgcs, chains)]
        draws = [jnp.sum(jnp.where(jnp.logical_not(c["masks"][3]), dgr, 0.0), axis=1, keepdims=True) + c["dglast"] + kt
                 for dgr, kt, c in zip(dgrs, kd_tots, chains)]
        for c, draw, dbeta in zip(chains, draws, dbetas):
            ch = c["ch"]
            dgates[c["cc"]] = dgates[c["cc"]] + jnp.where(lane == ch, draw, 0.0) + jnp.where(lane == 8 + ch, dbeta, 0.0)
        for cc in range(LOCAL_CHUNKS):
            rows = slice(cc * CHUNK, (cc + 1) * CHUNK)
            for h in range(GDN_HEADS):
                for part in range(3):
                    cols = slice(part * hd + h * GDN_DIM, part * hd + (h + 1) * GDN_DIM)
                    dx_ref[rows, cols] = grads[cc * N_CHAINS + h][part] + grads[cc * N_CHAINS + GDN_HEADS + h][part]
            dg_ref[rows, :] = dgates[cc]

    lc = LOCAL_CHUNKS
    all8 = lambda rows, cols: pl.BlockSpec((lc, N_CHAINS, rows, cols), lambda n: (n, 0, 0, 0))
    own4 = lambda rows, cols: pl.BlockSpec((lc, GDN_HEADS, rows, cols), lambda n: (n, 0, 0, 0))
    row4 = pl.BlockSpec((lc, GDN_HEADS, LANES), lambda n: (n, 0, 0))
    vn_f, vn_b = saved["vn"]
    dvn_f, dvn_b, dw_f, dw_b, dqg_f, dqg_b, dkd_f, dkd_b, dgl_f, dgl_b = scan
    return _grid_call(
        body, "gdn_local_bwd", nc // lc,
        [pl.BlockSpec((lc * CHUNK, 3 * hd), lambda n: (n, 0)), pl.BlockSpec((lc * CHUNK, LANES), lambda n: (n, 0)),
         pl.BlockSpec((lc, 16, CHUNK), lambda n: (n, 0, 0)), pl.BlockSpec((lc * CHUNK, hd), lambda n: (n, 0)),
         all8(CHUNK, CHUNK)] + [own4(CHUNK, GDN_DIM)] * 10 + [row4, row4],
        [pl.BlockSpec((lc * CHUNK, 3 * hd), lambda n: (n, 0)), pl.BlockSpec((lc * CHUNK, LANES), lambda n: (n, 0))],
        [jax.ShapeDtypeStruct((t, 3 * hd), F32), jax.ShapeDtypeStruct((t, LANES), F32)],
        (qkvc, gb, gbt, do, saved["tm"], vn_f, vn_b, dvn_f, dvn_b, dw_f, dw_b, dqg_f, dqg_b, dkd_f, dkd_b, dgl_f, dgl_b),
        exchange=exchange)


def _gdn_post_fwd(of, ob, z, gw, tm):
    t, hd = of.shape

    def body(of_ref, ob_ref, z_ref, w_ref, o_ref):
        for h in range(GDN_HEADS):
            cols = slice(h * GDN_DIM, (h + 1) * GDN_DIM)
            o = of_ref[:, cols] + ob_ref[:, cols]
            zv = z_ref[:, cols]
            o_ref[:, cols] = (o * _rstd(o) * w_ref[...] * (zv * _sigmoid(zv))).astype(BF16)

    row = pl.BlockSpec((tm, hd), lambda i: (i, 0))
    return pl.pallas_call(
        body, name="gdn_post_fwd", grid=(t // tm,),
        in_specs=[row, row, row, _resident((1, GDN_DIM))],
        out_specs=row, out_shape=jax.ShapeDtypeStruct((t, hd), BF16),
        compiler_params=_params(("arbitrary",), VMEM_LIMIT),
    )(of, ob, z, gw)


def _gdn_post_bwd(doa, of, ob, z, gw, tm):
    t, hd = of.shape

    def body(d_ref, of_ref, ob_ref, z_ref, w_ref, do_ref, dz_ref, dw_ref):
        @pl.when(pl.program_id(0) == 0)
        def _():
            dw_ref[...] = jnp.zeros_like(dw_ref)

        dw = jnp.zeros((1, GDN_DIM), F32)
        for h in range(GDN_HEADS):
            cols = slice(h * GDN_DIM, (h + 1) * GDN_DIM)
            o = of_ref[:, cols] + ob_ref[:, cols]
            zv = z_ref[:, cols]
            dv = d_ref[:, cols]
            r = _rstd(o)
            sg = _sigmoid(zv)
            on = o * r * w_ref[...]
            dz_ref[:, cols] = (dv * on * (sg * (1.0 + zv * (1.0 - sg)))).astype(BF16)
            dxr, dwh = _rms_bwd(o, r, w_ref[...], dv * (zv * sg))
            do_ref[:, cols] = dxr
            dw = dw + dwh
        dw_ref[...] += dw

    row = pl.BlockSpec((tm, hd), lambda i: (i, 0))
    return pl.pallas_call(
        body, name="gdn_post_bwd", grid=(t // tm,),
        in_specs=[row, row, row, row, _resident((1, GDN_DIM))],
        out_specs=[row, row, pl.BlockSpec((1, GDN_DIM), lambda i: (0, 0))],
        out_shape=[jax.ShapeDtypeStruct((t, hd), F32), jax.ShapeDtypeStruct((t, hd), BF16),
                   jax.ShapeDtypeStruct((1, GDN_DIM), F32)],
        compiler_params=_params(("arbitrary",), VMEM_LIMIT),
    )(doa, of, ob, z, gw)


SWA_W = SWA_HEADS * SWA_DIM
QBLK = 128
KWIN = QBLK + 2 * RADIUS
WIN_OFFSETS = (0, RADIUS, 2 * RADIUS)


def _t5_bucket(rel):
    nb = REL_BUCKETS // 2
    bucket = (rel > 0).astype(np.int32) * nb
    n = np.abs(rel)
    max_exact = nb // 2
    large = max_exact + (np.log(np.maximum(n, 1) / max_exact)
                         / math.log(REL_MAX_DISTANCE / max_exact) * (nb - max_exact)).astype(np.int32)
    large = np.minimum(large, nb - 1)
    return (bucket + np.where(n < max_exact, n, large)).astype(np.int32)


def _band_tables(dilation):
    a = np.arange(QBLK)
    b = np.arange(KWIN)
    rel = np.stack([b[None, :] - w0 - a[:, None] for w0 in WIN_OFFSETS])
    return np.where(np.abs(rel) <= RADIUS, _t5_bucket(rel * dilation), -1).astype(np.int32)


BAND_CELLS = len(WIN_OFFSETS) * QBLK * KWIN
BIAS_TILE = BAND_CELLS // 3


def _band_index():
    return jnp.asarray(np.concatenate([_band_tables(d).reshape(-1) for _, d in PATTERNS])[None, :])


def _onehot(idx, dtype):
    return (lax.broadcasted_iota(jnp.int32, (REL_BUCKETS, idx.shape[1]), 0) == idx).astype(dtype)


def _bias_tables(rel_bias, idx, tk):
    n = idx.shape[1]

    def body(rb_ref, i_ref, o_ref):
        iv = i_ref[...]
        oh = _onehot(iv, BF16)
        rest, acc = rb_ref[...], None
        for _ in range(3):
            piece = rest.astype(BF16)
            part = jnp.dot(piece, oh, preferred_element_type=F32)
            acc = part if acc is None else acc + part
            rest = rest - piece.astype(F32)
        o_ref[...] = jnp.where(iv < 0, NEG_BIG, acc)

    return pl.pallas_call(
        body, name="bias_tables", grid=(n // tk,),
        in_specs=[_resident((SWA_HEADS, REL_BUCKETS)), pl.BlockSpec((1, tk), lambda k: (0, k))],
        out_specs=pl.BlockSpec((SWA_HEADS, tk), lambda k: (0, k)),
        out_shape=jax.ShapeDtypeStruct((SWA_HEADS, n), F32),
        compiler_params=_params(("arbitrary",), VMEM_LIMIT),
    )(rel_bias.T, idx)


def _head_mean(x2, bd_ref):
    bd = bd_ref[...]
    rest, acc = x2, None
    for _ in range(3):
        piece = rest.astype(BF16)
        part = jnp.dot(piece, bd, preferred_element_type=F32)
        acc = part if acc is None else acc + part
        rest = rest - piece.astype(F32)
    return acc


VIEW_DILATIONS = tuple(d for _, d in PATTERNS if d > 1)


def _view_spec(tm, d):
    return pl.BlockSpec((tm // d, d * SWA_W), lambda i: (i, 0))


def _view_shape(t, d, dtype):
    return jax.ShapeDtypeStruct((t // d, d * SWA_W), dtype)


N_GROUPS = SWA_W // LANES


def _to_view(src_ref, idx, dst_ref, d, rows):
    for r in range(d):
        for g in range(N_GROUPS):
            cols = slice(r * SWA_W + g * LANES, r * SWA_W + (g + 1) * LANES)
            dst_ref[:, cols] = src_ref[idx, g, pl.ds(r, rows // d, stride=d), :].astype(dst_ref.dtype)


def _from_view(src_ref, dst_ref, idx, d, rows):
    for r in range(d):
        for g in range(N_GROUPS):
            cols = slice(r * SWA_W + g * LANES, r * SWA_W + (g + 1) * LANES)
            dst_ref[idx, g, pl.ds(r, rows // d, stride=d), :] = src_ref[:, cols]


def _swa_prep_fwd(qkvb, qw, kw, bd, tm):
    t = qkvb.shape[0]

    def body(x_ref, qw_ref, kw_ref, bd_ref, *rest):
        outs, sc = rest[:-1], rest[-1]
        for gidx in range(N_GROUPS):
            cols = slice(gidx * LANES, (gidx + 1) * LANES)
            xq = x_ref[:, cols]
            sc[0, gidx] = xq * lax.rsqrt(_head_mean(xq * xq, bd_ref) + EPS) * qw_ref[:, cols] * (SWA_DIM ** -0.5)
            xk = x_ref[:, SWA_W + gidx * LANES:SWA_W + (gidx + 1) * LANES]
            sc[1, gidx] = xk * lax.rsqrt(_head_mean(xk * xk, bd_ref) + EPS) * kw_ref[:, cols]
            sc[2, gidx] = x_ref[:, 2 * SWA_W + gidx * LANES:2 * SWA_W + (gidx + 1) * LANES]
            for i in range(3):
                outs[i][:, cols] = sc[i, gidx].astype(BF16)
        for i in range(3):
            for n, d in enumerate(VIEW_DILATIONS):
                _to_view(sc, i, outs[3 * (n + 1) + i], d, tm)

    return pl.pallas_call(
        body, name="swa_prep_fwd", grid=(t // tm,),
        in_specs=[pl.BlockSpec((tm, 3 * SWA_W), lambda i: (i, 0)), _resident((1, SWA_W)), _resident((1, SWA_W)),
                  _resident((LANES, LANES))],
        out_specs=[_view_spec(tm, d) for d in (1,) + VIEW_DILATIONS for _ in range(3)],
        out_shape=[_view_shape(t, d, BF16) for d in (1,) + VIEW_DILATIONS for _ in range(3)],
        scratch_shapes=[pltpu.VMEM((3, N_GROUPS, tm, LANES), F32)],
        compiler_params=_params(("arbitrary",), VMEM_LIMIT),
    )(qkvb, qw, kw, bd)


def _swa_prep_bwd(qkvb, qw, kw, bd, grads, tm):
    t = qkvb.shape[0]

    def body(x_ref, qw_ref, kw_ref, bd_ref, *rest):
        parts, (dx_ref, dqw_ref, dkw_ref, sc) = rest[:9], rest[9:]
        @pl.when(pl.program_id(0) == 0)
        def _():
            dqw_ref[...] = jnp.zeros_like(dqw_ref)
            dkw_ref[...] = jnp.zeros_like(dkw_ref)

        for i in range(3):
            for n, d in enumerate(VIEW_DILATIONS):
                _from_view(parts[3 * (n + 1) + i], sc, 2 * i + n, d, tm)
        for gidx in range(N_GROUPS):
            cols = slice(gidx * LANES, (gidx + 1) * LANES)
            for i, base, w_ref, dw_ref, scale in ((0, 0, qw_ref, dqw_ref, SWA_DIM ** -0.5),
                                                  (1, SWA_W, kw_ref, dkw_ref, 1.0)):
                xv = x_ref[:, base + gidx * LANES:base + (gidx + 1) * LANES]
                dy = (parts[i][:, cols] + sc[2 * i, gidx] + sc[2 * i + 1, gidx]) * scale
                r = lax.rsqrt(_head_mean(xv * xv, bd_ref) + EPS)
                xhat = xv * r
                dxh = dy * w_ref[:, cols]
                dx = r * (dxh - xhat * _head_mean(dxh * xhat, bd_ref))
                dx_ref[:, base + gidx * LANES:base + (gidx + 1) * LANES] = dx.astype(BF16)
                dw_ref[:, cols] += jnp.sum(dy * xhat, axis=0, keepdims=True)
            dx_ref[:, 2 * SWA_W + gidx * LANES:2 * SWA_W + (gidx + 1) * LANES] = (
                parts[2][:, cols] + sc[4, gidx] + sc[5, gidx]).astype(BF16)

    wrow = pl.BlockSpec((1, SWA_W), lambda i: (0, 0))
    return pl.pallas_call(
        body, name="swa_prep_bwd", grid=(t // tm,),
        in_specs=[pl.BlockSpec((tm, 3 * SWA_W), lambda i: (i, 0)), _resident((1, SWA_W)), _resident((1, SWA_W)),
                  _resident((LANES, LANES))] + [_view_spec(tm, d) for d in (1,) + VIEW_DILATIONS for _ in range(3)],
        out_specs=[pl.BlockSpec((tm, 3 * SWA_W), lambda i: (i, 0)), wrow, wrow],
        out_shape=[jax.ShapeDtypeStruct((t, 3 * SWA_W), BF16), jax.ShapeDtypeStruct((1, SWA_W), F32),
                   jax.ShapeDtypeStruct((1, SWA_W), F32)],
        scratch_shapes=[pltpu.VMEM((6, N_GROUPS, tm, LANES), F32)],
        compiler_params=_params(("arbitrary",), VMEM_LIMIT),
    )(qkvb, qw, kw, bd, *grads)


def _aligned(v, m):
    return v if isinstance(v, int) else pl.multiple_of(v, m)


BAND_GROUP = 2


def _band_loop(nsub, length, step, group=BAND_GROUP):
    step([(0, 0)], 0)
    if nsub > 2:
        assert (nsub - 2) % group == 0

        def inner(i, carry):
            s0 = 1 + i * group
            step([(s0 + e, pl.multiple_of((s0 + e) * QBLK - RADIUS, RADIUS)) for e in range(group)], 1)
            return carry
        lax.fori_loop(0, (nsub - 2) // group, inner, 0)
    step([(nsub - 1, length - KWIN)], 2)


def _head_select(lane, a0, a1):
    return jnp.where(lane < SWA_DIM, a0, a1)


def _swa_fwd(qv, kv, vv, bias, dilation, name):
    length = qv.shape[0]
    nsub = length // QBLK
    assert nsub >= 2 and length % QBLK == 0

    def body(q_ref, k_ref, v_ref, b_ref, o_ref, l_ref):
        lane = lax.broadcasted_iota(jnp.int32, (QBLK, LANES), 1)

        def step(blocks, var):
            items = []
            for s, ws in blocks:
                rows = pl.ds(_aligned(s * QBLK, QBLK), QBLK)
                q, kk, vw = q_ref[rows, :], k_ref[pl.ds(ws, KWIN), :], v_ref[pl.ds(ws, KWIN), :]
                for hh in range(2):
                    items.append((hh, jnp.where((lane < SWA_DIM) == (hh == 0), q, jnp.zeros_like(q)), kk, vw))
            lgs = [_dot_nt(qh, kk) + b_ref[hh, var] for hh, qh, kk, _ in items]
            ms = [jnp.max(lg, axis=-1, keepdims=True) for lg in lgs]
            ps = [jnp.exp(lg - m) for lg, m in zip(lgs, ms)]
            dens = [jnp.sum(p, axis=-1, keepdims=True) for p in ps]
            pvs = [_dot(p, it[3]) for p, it in zip(ps, items)]
            for n, (s, _) in enumerate(blocks):
                rows = pl.ds(_aligned(s * QBLK, QBLK), QBLK)
                o0, o1 = (pvs[2 * n + hh] / dens[2 * n + hh] for hh in range(2))
                l0, l1 = (ms[2 * n + hh] + jnp.log(dens[2 * n + hh]) for hh in range(2))
                o_ref[rows, :] = _head_select(lane, o0, o1)
                l_ref[rows, :] = _head_select(lane, l0, l1)

        _band_loop(nsub, length, step)

    blk = pl.BlockSpec((length, LANES), lambda hp, r: (0, r * (SWA_W // LANES) + hp))
    shp = jax.ShapeDtypeStruct(qv.shape, F32)
    return pl.pallas_call(
        body, name=name, grid=(SWA_W // LANES, dilation),
        in_specs=[blk, blk, blk, pl.BlockSpec((2, 3, QBLK, KWIN), lambda hp, r: (hp, 0, 0, 0))],
        out_specs=[blk, blk], out_shape=[shp, shp],
        compiler_params=_params(("arbitrary", "arbitrary"), VMEM_LIMIT),
    )(qv, kv, vv, bias)


def _swa_combine(os_, ls_, tm):
    t = os_[0].shape[0]

    def body(o0, o1, o2, l0, l1, l2, o_ref, ob_ref, la_ref, lb_ref, lc_ref, sc):
        for n, d in enumerate(VIEW_DILATIONS):
            _from_view((o1, o2)[n], sc, n, d, tm)
            _from_view((l1, l2)[n], sc, 2 + n, d, tm)
        for g in range(N_GROUPS):
            cols = slice(g * LANES, (g + 1) * LANES)
            la, lb, lc = l0[:, cols], sc[2, g], sc[3, g]
            m = jnp.maximum(jnp.maximum(la, lb), lc)
            tot = m + jnp.log(jnp.exp(la - m) + jnp.exp(lb - m) + jnp.exp(lc - m))
            o = jnp.exp(la - tot) * o0[:, cols] + jnp.exp(lb - tot) * sc[0, g] + jnp.exp(lc - tot) * sc[1, g]
            o_ref[:, cols] = o
            ob_ref[:, cols] = o.astype(BF16)
            la_ref[:, cols] = tot
            sc[4, g] = tot
        for n, d in enumerate(VIEW_DILATIONS):
            _to_view(sc, 4, (lb_ref, lc_ref)[n], d, tm)

    specs = [_view_spec(tm, d) for d in (1,) + VIEW_DILATIONS]
    return pl.pallas_call(
        body, name="swa_combine", grid=(t // tm,), in_specs=specs + specs, out_specs=[specs[0], specs[0]] + specs,
        out_shape=[jax.ShapeDtypeStruct((t, SWA_W), F32), jax.ShapeDtypeStruct((t, SWA_W), BF16)]
                  + [_view_shape(t, d, F32) for d in (1,) + VIEW_DILATIONS],
        scratch_shapes=[pltpu.VMEM((5, N_GROUPS, tm, LANES), F32)],
        compiler_params=_params(("arbitrary",), VMEM_LIMIT),
    )(*os_, *ls_)


def _swa_bwd_prep(do, o, bd, tm):
    t = do.shape[0]

    def body(d_ref, o_ref, bd_ref, dd1, dd4, dd16, db1, db4, db16, sc):
        for gidx in range(N_GROUPS):
            cols = slice(gidx * LANES, (gidx + 1) * LANES)
            dv = d_ref[:, cols]
            dd = _head_mean(dv * o_ref[:, cols], bd_ref) * float(SWA_DIM)
            sc[0, gidx] = dd
            sc[1, gidx] = dv
            dd1[:, cols] = dd
            db1[:, cols] = dv.astype(BF16)
        for n, d in enumerate(VIEW_DILATIONS):
            _to_view(sc, 0, (dd4, dd16)[n], d, tm)
            _to_view(sc, 1, (db4, db16)[n], d, tm)

    specs = [_view_spec(tm, d) for d in (1,) + VIEW_DILATIONS]
    return pl.pallas_call(
        body, name="swa_bwd_prep", grid=(t // tm,), in_specs=[specs[0], specs[0], _resident((LANES, LANES))],
        out_specs=specs + specs,
        out_shape=[_view_shape(t, d, F32) for d in (1,) + VIEW_DILATIONS]
                  + [_view_shape(t, d, BF16) for d in (1,) + VIEW_DILATIONS],
        scratch_shapes=[pltpu.VMEM((2, N_GROUPS, tm, LANES), F32)],
        compiler_params=_params(("arbitrary",), VMEM_LIMIT),
    )(do, o, bd)


def _swa_bwd(qv, kv, vv, dov, lv, ddv, bias_a, dilation, name):
    length = qv.shape[0]
    nsub = length // QBLK
    single = pl.Buffered(1) if dilation == 1 else None

    def body(q_ref, k_ref, v_ref, do_ref, l_ref, dd_ref, ba_ref, dq_ref, dk_ref, dv_ref, db_ref):
        @pl.when(pl.program_id(1) == 0)
        def _():
            db_ref[...] = jnp.zeros_like(db_ref)

        lane = lax.broadcasted_iota(jnp.int32, (QBLK, LANES), 1)
        lanew = lax.broadcasted_iota(jnp.int32, (KWIN, LANES), 1)

        def step(blocks, var):
            items = []
            for s, ws in blocks:
                rows = pl.ds(_aligned(s * QBLK, QBLK), QBLK)
                win = pl.ds(ws, KWIN)
                q, dov_ = q_ref[rows, :], do_ref[rows, :]
                kk, vw = k_ref[win, :], v_ref[win, :]
                lse, dd = l_ref[rows, :], dd_ref[rows, :]
                for hh in range(2):
                    mine = (lane < SWA_DIM) == (hh == 0)
                    col = slice(hh * SWA_DIM, hh * SWA_DIM + 1)
                    items.append((hh, jnp.where(mine, q, jnp.zeros_like(q)), jnp.where(mine, dov_, jnp.zeros_like(dov_)),
                                  kk, vw, lse[:, col], dd[:, col], q, dov_))
            lgs = [_dot_nt(it[1], it[3]) + ba_ref[it[0], var] for it in items]
            dps = [_dot_nt(it[2], it[4]) for it in items]
            ps = [jnp.exp(lg - it[5]) for lg, it in zip(lgs, items)]
            dss = [p * (dp - it[6]) for p, dp, it in zip(ps, dps, items)]
            dqs = [_dot(ds, it[3]) for ds, it in zip(dss, items)]
            dks = [_dot_tn(ds, it[7]) for ds, it in zip(dss, items)]
            dvs = [_dot_tn(p, it[8]) for p, it in zip(ps, items)]
            for n, (s, ws) in enumerate(blocks):
                rows = pl.ds(_aligned(s * QBLK, QBLK), QBLK)
                win = pl.ds(ws, KWIN)
                dq_ref[rows, :] = _head_select(lane, dqs[2 * n], dqs[2 * n + 1])
                dk_ref[win, :] += _head_select(lanew, dks[2 * n], dks[2 * n + 1])
                dv_ref[win, :] += _head_select(lanew, dvs[2 * n], dvs[2 * n + 1])
            for hh in range(2):
                tot = dss[hh]
                for n in range(1, len(blocks)):
                    tot = tot + dss[2 * n + hh]
                db_ref[hh, var] += tot

        dk_ref[...] = jnp.zeros_like(dk_ref)
        dv_ref[...] = jnp.zeros_like(dv_ref)
        _band_loop(nsub, length, step)

    imap = lambda hp, r: (0, r * (SWA_W // LANES) + hp)
    blk_in = pl.BlockSpec((length, LANES), imap, pipeline_mode=single)
    blk_out = pl.BlockSpec((length, LANES), imap)
    shp = jax.ShapeDtypeStruct(qv.shape, F32)
    return pl.pallas_call(
        body, name=name, grid=(SWA_W // LANES, dilation),
        in_specs=[blk_in] * 6 + [pl.BlockSpec((2, 3, QBLK, KWIN), lambda hp, r: (hp, 0, 0, 0))],
        out_specs=[blk_out, blk_out, blk_out, pl.BlockSpec((2, 3, QBLK, KWIN), lambda hp, r: (hp, 0, 0, 0))],
        out_shape=[shp, shp, shp, jax.ShapeDtypeStruct((SWA_HEADS, 3, QBLK, KWIN), F32)],
        compiler_params=_params(("arbitrary", "arbitrary"), VMEM_LIMIT),
    )(qv, kv, vv, dov, lv, ddv, bias_a)


def _bias_grad(ds2, idx, tk):
    n = ds2.shape[1]
    nk = n // tk

    def body(a_ref, i_ref, o_ref):
        @pl.when(pl.program_id(0) == 0)
        def _():
            o_ref[...] = jnp.zeros_like(o_ref)

        oh = _onehot(i_ref[...], BF16)
        rest = a_ref[...]
        acc = jnp.zeros((SWA_HEADS, REL_BUCKETS), F32)
        for _ in range(3):
            piece = rest.astype(BF16)
            acc = acc + _dot_nt(piece, oh)
            rest = rest - piece.astype(F32)
        o_ref[...] += acc

    return pl.pallas_call(
        body, name="bias_grad", grid=(nk,),
        in_specs=[pl.BlockSpec((SWA_HEADS, tk), lambda k: (0, k)), pl.BlockSpec((1, tk), lambda k: (0, k))],
        out_specs=pl.BlockSpec((SWA_HEADS, REL_BUCKETS), lambda k: (0, 0)),
        out_shape=jax.ShapeDtypeStruct((SWA_HEADS, REL_BUCKETS), F32),
        compiler_params=_params(("arbitrary",), VMEM_LIMIT),
    )(ds2, idx)


def _swa_branch_fwd(qkvb, qw_t, kw_t, rel_bias, bd, tm):
    qkv = _swa_prep_fwd(qkvb, qw_t, kw_t, bd, tm)
    tables = _bias_tables(rel_bias, _band_index(), BIAS_TILE)
    os_, ls_, tabs = [], [], []
    for n, (_, d) in enumerate(PATTERNS):
        bias = tables[:, n * BAND_CELLS:(n + 1) * BAND_CELLS].reshape(SWA_HEADS, len(WIN_OFFSETS), QBLK, KWIN)
        o_p, l_p = _swa_fwd(*qkv[3 * n:3 * n + 3], bias, d, f"swa_fwd_d{d}")
        os_.append(o_p)
        ls_.append(l_p)
        tabs.append(bias)
    o, o16, *lses = _swa_combine(os_, ls_, tm)
    return o, o16, (qkv, lses, tabs)


def _swa_branch_bwd(do, o, saved, qkvb, qw_t, kw_t, bd, tm):
    qkv, lses, tabs = saved
    prep = _swa_bwd_prep(do, o, bd, tm)
    grads, dss = [], []
    for n, ((_, d), bias) in enumerate(zip(PATTERNS, tabs)):
        dq, dk, dv, ds = _swa_bwd(*qkv[3 * n:3 * n + 3], prep[3 + n], lses[n], prep[n], bias, d, f"swa_bwd_d{d}")
        grads += [dq, dk, dv]
        dss.append(ds.reshape(SWA_HEADS, -1))
    dqkvb, dqw, dkw = _swa_prep_bwd(qkvb, qw_t, kw_t, bd, grads, tm)
    dbias = _bias_grad(jnp.concatenate(dss, axis=1), _band_index(), BIAS_TILE)
    fold = lambda w: jnp.sum(w.reshape(SWA_HEADS, SWA_DIM), axis=0)
    return dqkvb, fold(dqw), fold(dkw), dbias.T


def _mesh_pos():
    return lax.axis_index("x"), lax.axis_index("y"), lax.axis_index("c")


def _other_chips(x, y):
    return [(1 - x, y), (x, 1 - y), (1 - x, 1 - y)]


def _remote(src, dst, send_sem, recv_sem, device):
    return pltpu.make_async_remote_copy(src_ref=src, dst_ref=dst, send_sem=send_sem, recv_sem=recv_sem,
                                        device_id=device, device_id_type=MESH)


def _split_axis(shape2):
    return 0 if (shape2[0] // 2) % 16 == 0 else 1


def _half_index(shape2, c):
    axis = _split_axis(shape2)
    h = shape2[axis] // 2
    return (pl.ds(c * h, h), slice(None)) if axis == 0 else (slice(None), pl.ds(c * h, h))


def _all_gather(xs):
    n = len(xs)

    def body(*refs):
        ins, outs = refs[:n], refs[n:2 * n]
        send_sems, recv_sems = refs[2 * n:]
        x, y, c = _mesh_pos()
        me = 2 * x + y
        chips = _other_chips(x, y)
        halves = []
        sends = []
        for a in range(n):
            h = ins[a].shape[0] // 2
            mine, other = pl.ds(c * h, h), pl.ds((1 - c) * h, h)
            halves.append((mine, other))
            own = _remote(ins[a], outs[a].at[me], send_sems.at[a, 6], recv_sems.at[a, 6], (x, y, 1 - c))
            own.start()
            sends.append(own)
            for j, chip in enumerate(chips):
                cp = _remote(ins[a].at[mine], outs[a].at[me, mine], send_sems.at[a, j], recv_sems.at[a, j], (*chip, c))
                cp.start()
                sends.append(cp)
        for a in range(n):
            mine, _ = halves[a]
            for j, chip in enumerate(chips):
                src = 2 * chip[0] + chip[1]
                landed = outs[a].at[src, mine]
                _remote(landed, landed, send_sems.at[a, j], recv_sems.at[a, j], (x, y, c)).wait_recv()
                fwd = _remote(landed, landed, send_sems.at[a, 3 + j], recv_sems.at[a, 3 + j], (x, y, 1 - c))
                fwd.start()
                sends.append(fwd)
        for a in range(n):
            _, other = halves[a]
            for j, chip in enumerate(chips):
                src = 2 * chip[0] + chip[1]
                landed = outs[a].at[src, other]
                _remote(landed, landed, send_sems.at[a, 3 + j], recv_sems.at[a, 3 + j], (x, y, c)).wait_recv()
            mine_slot = outs[a].at[me]
            _remote(mine_slot, mine_slot, send_sems.at[a, 6], recv_sems.at[a, 6], (x, y, c)).wait_recv()
        for cp in sends:
            cp.wait_send()

    return list(pl.pallas_call(
        body, name="all_gather_weights",
        in_specs=[ANY] * n, out_specs=[ANY] * n,
        out_shape=[jax.ShapeDtypeStruct((N_SHARDS,) + a.shape, a.dtype) for a in xs],
        scratch_shapes=[pltpu.SemaphoreType.DMA((n, 7)), pltpu.SemaphoreType.DMA((n, 7))],
    )(*xs))


def _rs_pair(gs):
    n = len(gs)

    def body(*refs):
        ins, lands = refs[:n], refs[n:2 * n]
        send_sems, recv_sems = refs[2 * n:]
        x, y, c = _mesh_pos()
        cps = []
        for a in range(n):
            theirs = (slice(None),) + _half_index(ins[a].shape[1:], 1 - c)
            cp = _remote(ins[a].at[theirs], lands[a], send_sems.at[a], recv_sems.at[a], (x, y, 1 - c))
            cp.start()
            cps.append(cp)
        for cp in cps:
            cp.wait()

    def half_shape(g):
        dims = list(g.shape)
        dims[1 + _split_axis(g.shape[1:])] //= 2
        return tuple(dims)

    return list(pl.pallas_call(
        body, name="rs_pair", in_specs=[ANY] * n, out_specs=[ANY] * n,
        out_shape=[jax.ShapeDtypeStruct(half_shape(g), g.dtype) for g in gs],
        scratch_shapes=[pltpu.SemaphoreType.DMA((n,)), pltpu.SemaphoreType.DMA((n,))],
    )(*gs))


def _pair_exchange(gs):
    def copies(cin, cout, send_sems, recv_sems):
        x, y, c = _mesh_pos()
        return [_remote(g.at[(slice(None),) + _half_index(g.shape[1:], 1 - c)], land, send_sems.at[a, 0],
                        recv_sems.at[a, 0], (x, y, 1 - c)) for a, (g, land) in enumerate(zip(cin, cout))]

    def start(*refs):
        for cp in copies(*refs):
            cp.start()

    def finish(*refs):
        for cp in copies(*refs):
            cp.wait()

    def half_shape(g):
        dims = list(g.shape)
        dims[1 + _split_axis(g.shape[1:])] //= 2
        return tuple(dims)

    return _Exchange(tuple(gs), tuple(jax.ShapeDtypeStruct(half_shape(g), g.dtype) for g in gs), start, finish)


def _rs_chips(ss):
    n = len(ss)

    def body(*refs):
        ins, outs = refs[:n], refs[n:2 * n]
        send_sems, recv_sems = refs[2 * n:]
        x, y, c = _mesh_pos()
        me = 2 * x + y
        chips = _other_chips(x, y)
        cps = []
        for a in range(n):
            for j, chip in enumerate(chips):
                dst_chip = 2 * chip[0] + chip[1]
                cp = _remote(ins[a].at[dst_chip], outs[a].at[me], send_sems.at[a, j], recv_sems.at[a, j], (*chip, c))
                cp.start()
                cps.append(cp)
        for a in range(n):
            for j, chip in enumerate(chips):
                src = 2 * chip[0] + chip[1]
                _remote(outs[a].at[src], outs[a].at[src], send_sems.at[a, j], recv_sems.at[a, j], (x, y, c)).wait_recv()
        for cp in cps:
            cp.wait_send()

    return list(pl.pallas_call(
        body, name="rs_chips", in_specs=[ANY] * n, out_specs=[ANY] * n,
        out_shape=[jax.ShapeDtypeStruct(s.shape, s.dtype) for s in ss],
        scratch_shapes=[pltpu.SemaphoreType.DMA((n, 3)), pltpu.SemaphoreType.DMA((n, 3))],
    )(*ss))


def _rs_join(fs, axes):
    n = len(fs)

    def whole(f, axis):
        dims = list(f.shape)
        dims[axis] *= 2
        return tuple(dims)

    def body(*refs):
        ins, outs = refs[:n], refs[n:2 * n]
        send_sems, recv_sems = refs[2 * n:]
        x, y, c = _mesh_pos()
        cps = []
        for a in range(n):
            h = ins[a].shape[axes[a]]
            mine = (pl.ds(c * h, h), slice(None)) if axes[a] == 0 else (slice(None), pl.ds(c * h, h))
            cp = _remote(ins[a], outs[a].at[mine], send_sems.at[a], recv_sems.at[a], (x, y, 1 - c))
            cp.start()
            cps.append(cp)
        for cp in cps:
            cp.wait()

    outs = pl.pallas_call(
        body, name="rs_join", in_specs=[ANY] * n, out_specs=[ANY] * n,
        out_shape=[jax.ShapeDtypeStruct(whole(f, ax), f.dtype) for f, ax in zip(fs, axes)],
        scratch_shapes=[pltpu.SemaphoreType.DMA((n,)), pltpu.SemaphoreType.DMA((n,))],
    )(*fs)
    c = lax.axis_index("c")
    return [lax.dynamic_update_slice_in_dim(o, f, c * f.shape[ax], ax) for o, f, ax in zip(outs, fs, axes)]


def _gather_exchange(xs):
    def start(cin, cout, send_sems, recv_sems):
        x, y, c = _mesh_pos()
        me = 2 * x + y
        for a, (src, dst) in enumerate(zip(cin, cout)):
            mine = _half_index(src.shape, c)
            for j, chip in enumerate(_other_chips(x, y)):
                _remote(src.at[mine], dst.at[(me,) + mine], send_sems.at[a, j], recv_sems.at[a, j], (*chip, c)).start()
            _remote(src, dst.at[me], send_sems.at[a, 3], recv_sems.at[a, 3], (x, y, 1 - c)).start()

    def finish(cin, cout, send_sems, recv_sems):
        x, y, c = _mesh_pos()
        for a, dst in enumerate(cout):
            for j, chip in enumerate(_other_chips(x, y)):
                landed = dst.at[(2 * chip[0] + chip[1],) + _half_index(dst.shape[1:], c)]
                _remote(landed, landed, send_sems.at[a, j], recv_sems.at[a, j], (x, y, c)).wait()
            own = dst.at[2 * x + y]
            _remote(own, own, send_sems.at[a, 3], recv_sems.at[a, 3], (x, y, c)).wait()

    return _Exchange(tuple(xs), tuple(jax.ShapeDtypeStruct((N_SHARDS,) + a.shape, a.dtype) for a in xs), start, finish)


def _gather_forward(gs):
    n = len(gs)

    def body(*refs):
        outs = refs[n:2 * n]
        send_sems, recv_sems = refs[2 * n:]
        x, y, c = _mesh_pos()
        chips = _other_chips(x, y)
        cps = []
        for a in range(n):
            for j, chip in enumerate(chips):
                landed = outs[a].at[(2 * chip[0] + chip[1],) + _half_index(outs[a].shape[1:], c)]
                cp = _remote(landed, landed, send_sems.at[a, j], recv_sems.at[a, j], (x, y, 1 - c))
                cp.start()
                cps.append(cp)
        for a in range(n):
            for j, chip in enumerate(chips):
                other = outs[a].at[(2 * chip[0] + chip[1],) + _half_index(outs[a].shape[1:], 1 - c)]
                _remote(other, other, send_sems.at[a, j], recv_sems.at[a, j], (x, y, c)).wait_recv()
        for cp in cps:
            cp.wait_send()

    return list(pl.pallas_call(
        body, name="gather_forward", in_specs=[ANY] * n, out_specs=[ANY] * n,
        out_shape=[jax.ShapeDtypeStruct(g.shape, g.dtype) for g in gs],
        input_output_aliases={i: i for i in range(n)},
        scratch_shapes=[pltpu.SemaphoreType.DMA((n, 3)), pltpu.SemaphoreType.DMA((n, 3))],
    )(*gs))


def _scatter_exchange(ss):
    def start(cin, cout, send_sems, recv_sems):
        x, y, c = _mesh_pos()
        me = 2 * x + y
        for a, (src, dst) in enumerate(zip(cin, cout)):
            for j, chip in enumerate(_other_chips(x, y)):
                _remote(src.at[2 * chip[0] + chip[1]], dst.at[me], send_sems.at[a, j], recv_sems.at[a, j],
                        (*chip, c)).start()

    def finish(cin, cout, send_sems, recv_sems):
        x, y, c = _mesh_pos()
        for a, dst in enumerate(cout):
            for j, chip in enumerate(_other_chips(x, y)):
                slot = dst.at[2 * chip[0] + chip[1]]
                _remote(slot, slot, send_sems.at[a, j], recv_sems.at[a, j], (x, y, c)).wait()

    return _Exchange(tuple(ss), tuple(jax.ShapeDtypeStruct(s.shape, s.dtype) for s in ss), start, finish)


def _add_pairs(gs, lands, name):
    n = len(gs)

    def body(*refs):
        c = lax.axis_index("c")
        for g_ref, l_ref, o_ref in zip(refs[:n], refs[n:2 * n], refs[2 * n:]):
            mine = g_ref[(0,) + _half_index(g_ref.shape[1:], c)]
            o_ref[0] = (mine.astype(F32) + l_ref[0].astype(F32)).astype(BF16)

    whole = [pl.BlockSpec((1,) + g.shape[1:], lambda j: (j, 0, 0)) for g in gs]
    half = [pl.BlockSpec((1,) + l.shape[1:], lambda j: (j, 0, 0)) for l in lands]
    return list(pl.pallas_call(body, name=name, grid=(gs[0].shape[0],), in_specs=whole + half, out_specs=half,
                               out_shape=[jax.ShapeDtypeStruct(l.shape, BF16) for l in lands],
                               compiler_params=_params(("arbitrary",), VMEM_LIMIT))(*gs, *lands))


def _sum_slots(slots, owns, name):
    n = len(slots)

    def body(*refs):
        me = 2 * lax.axis_index("x") + lax.axis_index("y")
        for s_ref, o_ref, out_ref in zip(refs[:n], refs[n:2 * n], refs[2 * n:]):
            acc = jnp.zeros(out_ref.shape, F32)
            for s in range(N_SHARDS):
                acc = acc + jnp.where(me == s, o_ref[s], s_ref[s]).astype(F32)
            out_ref[...] = acc

    def specs(a):
        _, h, c = a.shape
        if h % 32 == 0:
            return (pl.BlockSpec((N_SHARDS, h // 2, c), lambda i: (0, i, 0)), pl.BlockSpec((h // 2, c), lambda i: (i, 0)))
        return (pl.BlockSpec((N_SHARDS, h, c // 2), lambda i: (0, 0, i)), pl.BlockSpec((h, c // 2), lambda i: (0, i)))

    in_specs = [specs(a)[0] for a in slots]
    return list(pl.pallas_call(body, name=name, grid=(2,), in_specs=in_specs + in_specs,
                               out_specs=[specs(a)[1] for a in slots],
                               out_shape=[jax.ShapeDtypeStruct(a.shape[1:], F32) for a in slots],
                               compiler_params=_params(("arbitrary",), VMEM_LIMIT))(*slots, *owns))


def _all_reduce_small(p):
    r = p.shape[0]

    def body(p_ref, o_ref, buf, send_sems, recv_sems):
        x, y, c = _mesh_pos()
        me = 4 * x + 2 * y + c
        buf[me] = p_ref[...]
        cps = []
        k = 0
        for fx in range(2):
            for fy in range(2):
                for fc in range(2):
                    if fx + fy + fc == 0:
                        continue
                    peer = (1 - x if fx else x, 1 - y if fy else y, 1 - c if fc else c)
                    peer_id = 4 * peer[0] + 2 * peer[1] + peer[2]
                    cp = _remote(p_ref, buf.at[me], send_sems.at[k], recv_sems.at[k], peer)
                    cp.start()
                    cps.append((cp, peer_id, k))
                    k += 1
        for cp, peer_id, k in cps:
            _remote(p_ref, buf.at[peer_id], send_sems.at[k], recv_sems.at[k], (x, y, c)).wait_recv()
        for cp, _, _ in cps:
            cp.wait_send()
        acc = buf[0]
        for s in range(1, 8):
            acc = acc + buf[s]
        o_ref[...] = acc

    vm = pl.BlockSpec(memory_space=pltpu.VMEM)
    return pl.pallas_call(
        body, name="all_reduce_small", in_specs=[vm], out_specs=vm,
        out_shape=jax.ShapeDtypeStruct(p.shape, F32),
        scratch_shapes=[pltpu.VMEM((8, r, LANES), F32), pltpu.SemaphoreType.DMA((7,)), pltpu.SemaphoreType.DMA((7,))],
    )(p)


def _adamw(params, name, steps):
    c1 = 1.0 / (1.0 - ADAM_B1 ** ADAM_STEP)
    c2 = 1.0 / (1.0 - ADAM_B2 ** ADAM_STEP)
    n = len(params)

    def body(*refs):
        for a in range(n):
            w_ref, g_ref, m_ref, v_ref = refs[4 * a:4 * a + 4]
            d_ref, nm_ref, nv_ref = refs[4 * n + 3 * a:4 * n + 3 * a + 3]
            gv = g_ref[...]
            nm = ADAM_B1 * m_ref[...] + (1.0 - ADAM_B1) * gv
            nv = ADAM_B2 * v_ref[...] + (1.0 - ADAM_B2) * (gv * gv)
            d_ref[...] = -ADAM_LR * ((nm * c1) / (jnp.sqrt(nv * c2) + ADAM_EPS) + ADAM_WD * w_ref[...])
            nm_ref[...] = nm
            nv_ref[...] = nv

    def spec(shape):
        r, c = shape
        if r % (8 * steps) == 0:
            return pl.BlockSpec((r // steps, c), lambda i: (i, 0))
        assert c % (LANES * steps) == 0
        return pl.BlockSpec((r, c // steps), lambda i: (0, i))

    specs = [spec(w.shape) for w, _, _, _ in params]
    res = pl.pallas_call(
        body, name=name, grid=(steps,),
        in_specs=[s for s in specs for _ in range(4)], out_specs=[s for s in specs for _ in range(3)],
        out_shape=[jax.ShapeDtypeStruct(w.shape, F32) for w, _, _, _ in params for _ in range(3)],
        compiler_params=_params(("arbitrary",), VMEM_LIMIT))(*[a for p4 in params for a in p4])
    return [tuple(res[3 * a:3 * a + 3]) for a in range(n)]


PACK_UNIT = 8 * LANES


def _pack(arrs):
    parts = []
    for a in arrs:
        f = a.reshape(-1).astype(F32)
        parts.append(jnp.pad(f, (0, (-f.shape[0]) % PACK_UNIT)).reshape(-1, LANES))
    return jnp.concatenate(parts, axis=0)


def _unpack(m, shapes):
    outs, row = [], 0
    for s in shapes:
        n = int(np.prod(s))
        rows = -(-n // PACK_UNIT) * 8
        outs.append(m[row:row + rows].reshape(-1)[:n].reshape(s))
        row += rows
    return outs


WEIGHTS = ["ffn1_norm", "ffn1_w_gate", "ffn1_w_up", "ffn1_w_down", "mix_norm", "w_in", "conv_w", "a_log", "dt_bias",
           "gdn_norm_w", "q_norm_w", "k_norm_w", "rel_bias", "w_out", "ffn2_norm", "ffn2_w_gate", "ffn2_w_up",
           "ffn2_w_down", "final_norm"]
BIG = ["ffn1_w_gate", "ffn1_w_up", "ffn1_w_down", "w_in", "w_out", "ffn2_w_gate", "ffn2_w_up", "ffn2_w_down"]
SMALL = [n for n in WEIGHTS if n not in BIG]
COL_SHARDED = ["ffn1_w_gate", "ffn1_w_up", "w_in", "ffn2_w_gate", "ffn2_w_up"]
N_IN_COLS = 3600
TM = 256
TE = 512
ADAM_PIECES = 8
TK = 2048


def kernel(x, ffn1_norm, ffn1_w_gate, ffn1_w_up, ffn1_w_down, mix_norm, w_in, conv_w, a_log, dt_bias, gdn_norm_w, q_norm_w, k_norm_w, rel_bias, w_out, ffn2_norm, ffn2_w_gate, ffn2_w_up, ffn2_w_down, final_norm, loss_target, m_ffn1_norm, m_ffn1_w_gate, m_ffn1_w_up, m_ffn1_w_down, m_mix_norm, m_w_in, m_conv_w, m_a_log, m_dt_bias, m_gdn_norm_w, m_q_norm_w, m_k_norm_w, m_rel_bias, m_w_out, m_ffn2_norm, m_ffn2_w_gate, m_ffn2_w_up, m_ffn2_w_down, m_final_norm, v_ffn1_norm, v_ffn1_w_gate, v_ffn1_w_up, v_ffn1_w_down, v_mix_norm, v_w_in, v_conv_w, v_a_log, v_dt_bias, v_gdn_norm_w, v_q_norm_w, v_k_norm_w, v_rel_bias, v_w_out, v_ffn2_norm, v_ffn2_w_gate, v_ffn2_w_up, v_ffn2_w_down, v_final_norm):
    p = dict(locals())
    xs, target = x[0], loss_target[0]
    t, d = xs.shape
    nc = t // CHUNK
    tk = min(TK, t)
    tkf = min(2 * TK, t)
    me = 2 * lax.axis_index("x") + lax.axis_index("y")

    first = ["ffn1_w_gate", "ffn1_w_up", "ffn1_w_down"]
    later = [n for n in BIG if n not in first] + ["conv_w"]
    local = lambda n, a: a[0].T if n in COL_SHARDED else a[0]
    shards = {n: local(n, p[n]).astype(BF16) for n in BIG}
    shards["conv_w"] = conv_w[0]
    gw = dict(zip(first, _all_gather([shards[n] for n in first])))
    f1 = (gw["ffn1_w_gate"], gw["ffn1_w_up"], gw["ffn1_w_down"])
    (x1, xn1, g1, u1), landed = _ffn_fwd(xs, ffn1_norm, *f1, TE, "ffn1_fwd",
                                         exchange=_gather_exchange([shards[n] for n in later]))
    gw.update(zip(later, _gather_forward(landed)))
    wp = gw["w_in"].reshape(N_IN_COLS, d)
    w_out_full = gw["w_out"].reshape(d, d)
    conv_rows = conv_w.shape[1]
    cw = jnp.pad(gw["conv_w"].reshape(N_SHARDS * conv_rows, CONV_TAPS).T, ((0, 8 - CONV_TAPS), (0, 0)))
    gp = jnp.pad(jnp.stack([a_log.reshape(8), dt_bias.reshape(8)]), ((0, 6), (0, LANES - 8)))
    gdn_w = gdn_norm_w.reshape(1, GDN_DIM)
    qw_t = jnp.tile(q_norm_w.reshape(1, SWA_DIM), (1, SWA_HEADS))
    kw_t = jnp.tile(k_norm_w.reshape(1, SWA_DIM), (1, SWA_HEADS))
    bd = jnp.asarray(np.kron(np.eye(2), np.full((SWA_DIM, SWA_DIM), 1.0 / SWA_DIM)), BF16)
    f2 = (gw["ffn2_w_gate"], gw["ffn2_w_up"], gw["ffn2_w_down"])

    hn, qkva, z, ab, qkvb = _mix_in_fwd(x1, mix_norm, wp, TE)
    qkvc, gb = _gdn_prep_fwd(qkva, cw, ab, gp, TE)
    gbt = jnp.transpose(gb[:, :16].reshape(nc, CHUNK, 16), (0, 2, 1))
    o_f, o_b, gdn_saved = _gdn_fwd(qkvc, gb, gbt)
    oa = _gdn_post_fwd(o_f, o_b, z, gdn_w, TE)
    o_swa, o_swa16, swa_saved = _swa_branch_fwd(qkvb, qw_t, kw_t, rel_bias, bd, TE)
    x2 = _mix_out_fwd(x1, oa, o_swa, w_out_full, TE)
    (dx3, xn2, g2, u2, loss_part, d_final), _ = _ffn_fwd(x2, ffn2_norm, *f2, TE, "ffn2_fwd", head=(final_norm, target))

    def pair_sums(partials, tag):
        return _add_pairs(partials, _rs_pair(partials), f"rs_add_{tag}")

    (dx2, dyh2, dg2, du2, h2, d_nw2), _ = _ffn_bwd_dx(dx3, x2, ffn2_norm, g2, u2, *f2, TM, "ffn2_bwd_dx")
    dwg2 = _matmul_tn(dg2, xn2, tkf, "ffn2_dwg")
    dwu2 = _matmul_tn(du2, xn2, tkf, "ffn2_dwu")
    dwd2 = _matmul_tn(h2, dyh2, tkf, "ffn2_dwd")
    (doa, dob, dx2b), lands_f2 = _mix_out_bwd(dx2, w_out_full, TE, exchange=_pair_exchange([dwg2, dwu2, dwd2]))
    sums_f2 = _add_pairs([dwg2, dwu2, dwd2], lands_f2, "rs_add_a")
    dwo = jnp.concatenate([_matmul_tn(oa, dx2b, tk, "w_out_dw_a")[0], _matmul_tn(o_swa16, dx2b, tk, "w_out_dw_b")[0]],
                          axis=0).reshape(N_SHARDS, d // N_SHARDS, d)
    do_g, dz, d_gdnw = _gdn_post_bwd(doa, o_f, o_b, z, gdn_w, TE)
    (dqkvc, dgates), slots_f2 = _gdn_bwd(qkvc, gb, gbt, do_g, gdn_saved, exchange=_scatter_exchange(sums_f2))
    dqkva, dab, dcw, dgp = _gdn_prep_bwd(qkva, cw, ab, gp, dqkvc, dgates, TM)
    dqkvb, d_qw, d_kw, d_rel = _swa_branch_bwd(dob, o_swa, swa_saved, qkvb, qw_t, kw_t, bd, TE)
    dpieces = (dqkva, dz, dab, dqkvb)
    dwp = [_matmul_tn(dp, hn, tk, f"w_in_dw_{i}")[0] for i, dp in enumerate(dpieces)]
    dw_in = jnp.concatenate([dwp[0], dwp[1], dwp[2][:N_GATE_COLS], dwp[3]], axis=0)
    dw_in = dw_in.reshape(N_SHARDS, N_IN_COLS // N_SHARDS, d)
    sums_mix = pair_sums([dw_in, dwo], "b")
    (dx1, d_mixnw), slots_mix = _mix_in_bwd_dx(dx2, x1, mix_norm, dpieces, wp, TE, exchange=_scatter_exchange(sums_mix))
    (gx, dyh1, dg1, du1, h1, d_nw1), _ = _ffn_bwd_dx(dx1, xs, ffn1_norm, g1, u1, *f1, TM, "ffn1_bwd_dx")
    dwg1 = _matmul_tn(dg1, xn1, tkf, "ffn1_dwg")
    dwu1 = _matmul_tn(du1, xn1, tkf, "ffn1_dwu")
    sums_gu = pair_sums([dwg1, dwu1], "c")
    dwd1, slots_gu = _matmul_tn(h1, dyh1, tkf, "ffn1_dwd", exchange=_scatter_exchange(sums_gu))
    sums_d = pair_sums([dwd1], "d")
    slots = slots_gu + _rs_chips(sums_d) + slots_mix + slots_f2
    sums = sums_gu + sums_d + sums_mix + sums_f2
    halves = _sum_slots(slots[:4], sums[:4], "rs_sum_a") + _sum_slots(slots[4:], sums[4:], "rs_sum_b")
    g_big = dict(zip(BIG, _rs_join(halves, [_split_axis(shards[n].shape) for n in BIG])))

    small_partial = {"ffn1_norm": d_nw1, "mix_norm": d_mixnw, "a_log": dgp[0, 0:8], "dt_bias": dgp[1, 0:8],
                     "gdn_norm_w": d_gdnw, "q_norm_w": d_qw, "k_norm_w": d_kw, "rel_bias": d_rel,
                     "ffn2_norm": d_nw2, "final_norm": d_final, "conv_w": dcw[0:CONV_TAPS].T}
    red = _all_reduce_small(_pack([small_partial[n] for n in SMALL] + [loss_part[0, 0:1]]))
    full_shapes = [p[n].shape if n != "conv_w" else (N_SHARDS * conv_rows, CONV_TAPS) for n in SMALL]
    red_parts = _unpack(red, full_shapes + [(1,)])
    loss = red_parts[-1].reshape(())
    g_small = dict(zip(SMALL, red_parts[:-1]))
    g_small["conv_w"] = lax.dynamic_slice_in_dim(g_small["conv_w"], me * conv_rows, conv_rows, 0).reshape(conv_w.shape)

    grads, deltas, new_m, new_v = {}, {}, {}, {}
    quad = lambda n: (local(n, p[n]), g_big[n], local(n, p["m_" + n]), local(n, p["v_" + n]))
    updates = (_adamw([quad(n) for n in BIG[:4]], "adamw_a", ADAM_PIECES)
               + _adamw([quad(n) for n in BIG[4:]], "adamw_b", ADAM_PIECES))
    for n, (dl, nm, nv) in zip(BIG, updates):
        back = (lambda a: a.T[None]) if n in COL_SHARDED else (lambda a: a[None])
        grads[n], deltas[n], new_m[n], new_v[n] = back(g_big[n]), back(dl), back(nm), back(nv)
    packed = [_pack([src[n] for n in SMALL]) for src in
              ({n: p[n] for n in SMALL}, g_small, {n: p["m_" + n] for n in SMALL}, {n: p["v_" + n] for n in SMALL})]
    small_shapes = [p[n].shape for n in SMALL]
    for dst, arr in zip((deltas, new_m, new_v), _adamw([tuple(packed)], "adamw_small", 1)[0]):
        dst.update(zip(SMALL, _unpack(arr, small_shapes)))
    grads.update(g_small)

    return (loss, gx[None], *[grads[n] for n in WEIGHTS], *[deltas[n] for n in WEIGHTS],
            *[new_m[n] for n in WEIGHTS], *[new_v[n] for n in WEIGHTS])
```

```python
import math
from typing import Callable, NamedTuple

import numpy as np
import jax
import jax.numpy as jnp
from jax import lax
from jax.experimental import pallas as pl
from jax.experimental.pallas import tpu as pltpu

F32 = jnp.float32
BF16 = jnp.bfloat16
MESH = pl.DeviceIdType.MESH

EPS = 1e-6
NEG_BIG = -1e30
GDN_HEADS = 4
GDN_DIM = 128
CHUNK = 64
SWA_HEADS = 8
SWA_DIM = 64
PATTERNS = ((128, 1), (512, 4), (2048, 16))
RADIUS = 64
REL_BUCKETS = 32
REL_MAX_DISTANCE = 1024
CONV_TAPS = 5
N_SHARDS = 4
LANES = 128
VMEM_LIMIT = 56 * 1024 * 1024

ADAM_LR, ADAM_B1, ADAM_B2, ADAM_EPS, ADAM_WD, ADAM_STEP = 0.001, 0.9, 0.999, 1e-08, 0.01, 10


def _params(sem=None, vmem=None):
    return pltpu.CompilerParams(dimension_semantics=sem, vmem_limit_bytes=vmem)


def _resident(shape):
    nd = len(shape)
    return pl.BlockSpec(shape, lambda *_: (0,) * nd, pipeline_mode=pl.Buffered(1))


ANY = pl.BlockSpec(memory_space=pl.ANY)


class _Exchange(NamedTuple):
    arrays: tuple
    out_shape: tuple
    start: Callable
    finish: Callable


def _grid_call(body, name, nsteps, in_specs, out_specs, out_shape, operands, scratch=(), exchange=None):
    params = _params(("arbitrary",), VMEM_LIMIT)
    if exchange is None:
        res = pl.pallas_call(body, name=name, grid=(nsteps,), in_specs=list(in_specs), out_specs=list(out_specs),
                             out_shape=list(out_shape), scratch_shapes=list(scratch), compiler_params=params)(*operands)
        return list(res), []
    n_in, n_out, k, n_scr = len(in_specs), len(out_specs), len(exchange.arrays), len(scratch)

    def wrapped(*refs):
        ins, cin = refs[:n_in], refs[n_in:n_in + k]
        outs, cout = refs[n_in + k:n_in + k + n_out], refs[n_in + k + n_out:n_in + 2 * k + n_out]
        rest = refs[n_in + 2 * k + n_out:]
        scr, (send_sems, recv_sems) = rest[:n_scr], rest[n_scr:]

        @pl.when(pl.program_id(0) == 0)
        def _():
            exchange.start(cin, cout, send_sems, recv_sems)

        body(*ins, *outs, *scr)

        @pl.when(pl.program_id(0) == nsteps - 1)
        def _():
            exchange.finish(cin, cout, send_sems, recv_sems)

    res = pl.pallas_call(
        wrapped, name=name, grid=(nsteps,), in_specs=list(in_specs) + [ANY] * k, out_specs=list(out_specs) + [ANY] * k,
        out_shape=list(out_shape) + list(exchange.out_shape),
        scratch_shapes=list(scratch) + [pltpu.SemaphoreType.DMA((k, 4)), pltpu.SemaphoreType.DMA((k, 4))],
        compiler_params=params)(*operands, *exchange.arrays)
    return list(res[:n_out]), list(res[n_out:])


def _dot(a, b):
    return jnp.dot(a.astype(BF16), b.astype(BF16), preferred_element_type=F32)


def _dot_nt(a, b):
    return lax.dot_general(a.astype(BF16), b.astype(BF16), (((1,), (1,)), ((), ())), preferred_element_type=F32)


def _dot_tn(a, b):
    return lax.dot_general(a.astype(BF16), b.astype(BF16), (((0,), (0,)), ((), ())), preferred_element_type=F32)


def _sigmoid(x):
    return 1.0 / (1.0 + jnp.exp(-x))


def _rstd(xf):
    return lax.rsqrt(jnp.mean(xf * xf, axis=-1, keepdims=True) + EPS)


def _rms_bwd(xf, r, nw, dxn):
    xhat = xf * r
    dxh = dxn * nw
    dx = r * (dxh - xhat * jnp.mean(dxh * xhat, axis=-1, keepdims=True))
    return dx, jnp.sum(dxn * xhat, axis=0, keepdims=True)


def _ffn_fwd(x, nw, wg, wu, wd, tm, name, exchange=None, head=None):
    t, d = x.shape
    nj, fs, _ = wg.shape

    def body(x_ref, nw_ref, wg_ref, wu_ref, wd_ref, *rest):
        if head is None:
            y_ref, xn_ref, g_ref, u_ref = rest
        else:
            fw_ref, t_ref, y_ref, xn_ref, g_ref, u_ref, loss_ref, dfw_ref = rest

            @pl.when(pl.program_id(0) == 0)
            def _():
                loss_ref[...] = jnp.zeros_like(loss_ref)
                dfw_ref[...] = jnp.zeros_like(dfw_ref)

        xf = x_ref[...]
        xn = (xf * _rstd(xf) * nw_ref[...]).astype(BF16)
        xn_ref[...] = xn
        acc = jnp.zeros((tm, d), F32)
        for j in range(nj):
            g = _dot_nt(xn, wg_ref[j])
            u = _dot_nt(xn, wu_ref[j])
            h = (g * _sigmoid(g) * u).astype(BF16)
            acc = acc + jnp.dot(h, wd_ref[j], preferred_element_type=F32)
            g_ref[j] = g.astype(BF16)
            u_ref[j] = u.astype(BF16)
        y = xf + 0.5 * acc
        if head is None:
            y_ref[...] = y
        else:
            r = _rstd(y)
            err = y * r * fw_ref[...] - t_ref[...]
            loss_ref[...] += 0.5 * jnp.sum(jnp.mean(err * err, axis=-1, keepdims=True), axis=0, keepdims=True)
            dy, dfw = _rms_bwd(y, r, fw_ref[...], err * (1.0 / d))
            y_ref[...] = dy
            dfw_ref[...] += dfw

    row = pl.BlockSpec((tm, d), lambda i: (i, 0))
    act = pl.BlockSpec((nj, tm, fs), lambda i: (0, i, 0))
    in_specs = [row, _resident((1, d)), _resident(wg.shape), _resident(wu.shape), _resident(wd.shape)]
    out_specs = [row, row, act, act]
    out_shape = [jax.ShapeDtypeStruct((t, d), F32), jax.ShapeDtypeStruct((t, d), BF16),
                 jax.ShapeDtypeStruct((nj, t, fs), BF16), jax.ShapeDtypeStruct((nj, t, fs), BF16)]
    operands = (x, nw, wg, wu, wd)
    if head is not None:
        in_specs += [_resident((1, d)), row]
        out_specs += [pl.BlockSpec((1, LANES), lambda i: (0, 0)), pl.BlockSpec((1, d), lambda i: (0, 0))]
        out_shape += [jax.ShapeDtypeStruct((1, LANES), F32), jax.ShapeDtypeStruct((1, d), F32)]
        operands += tuple(head)
    return _grid_call(body, name, t // tm, in_specs, out_specs, out_shape, operands, exchange=exchange)


def _ffn_bwd_dx(dy, x, nw, g, u, wg, wu, wd, tm, name, exchange=None):
    t, d = x.shape
    nj, fs, _ = wg.shape

    def body(dy_ref, x_ref, nw_ref, g_ref, u_ref, wg_ref, wu_ref, wd_ref,
             dx_ref, dyh_ref, dg_ref, du_ref, h_ref, dnw_ref):
        @pl.when(pl.program_id(0) == 0)
        def _():
            dnw_ref[...] = jnp.zeros_like(dnw_ref)

        dyv = dy_ref[...]
        dyh = (0.5 * dyv).astype(BF16)
        dyh_ref[...] = dyh
        dxn = jnp.zeros((tm, d), F32)
        dh_next = _dot_nt(dyh, wd_ref[0])
        for j in range(nj):
            dh = dh_next
            gv = g_ref[j].astype(F32)
            uv = u_ref[j].astype(F32)
            sg = _sigmoid(gv)
            si = gv * sg
            dg = (dh * uv * (sg * (1.0 + gv * (1.0 - sg)))).astype(BF16)
            du = (dh * si).astype(BF16)
            if j + 1 < nj:
                dh_next = _dot_nt(dyh, wd_ref[j + 1])
            h_ref[j] = (si * uv).astype(BF16)
            dg_ref[j] = dg
            du_ref[j] = du
            dxn = dxn + _dot(dg, wg_ref[j]) + _dot(du, wu_ref[j])
        xf = x_ref[...]
        dxr, dnw = _rms_bwd(xf, _rstd(xf), nw_ref[...], dxn)
        dx_ref[...] = dyv + dxr
        dnw_ref[...] += dnw

    row = pl.BlockSpec((tm, d), lambda i: (i, 0))
    act = pl.BlockSpec((nj, tm, fs), lambda i: (0, i, 0))
    act_shape = jax.ShapeDtypeStruct((nj, t, fs), BF16)
    return _grid_call(
        body, name, t // tm,
        [row, row, _resident((1, d)), act, act, _resident(wg.shape), _resident(wu.shape), _resident(wd.shape)],
        [row, row, act, act, act, pl.BlockSpec((1, d), lambda i: (0, 0))],
        [jax.ShapeDtypeStruct((t, d), F32), jax.ShapeDtypeStruct((t, d), BF16),
         act_shape, act_shape, act_shape, jax.ShapeDtypeStruct((1, d), F32)],
        (dy, x, nw, g, u, wg, wu, wd), exchange=exchange)


def _matmul_tn(a, b, tk, name, exchange=None):
    a3, b3 = a.ndim == 3, b.ndim == 3
    nj = a.shape[0] if a3 else (b.shape[0] if b3 else 1)
    t, m = a.shape[-2:]
    n = b.shape[-1]
    nt = t // tk

    def body(a_ref, b_ref, o_ref, acc_ref):
        k = pl.program_id(0) % nt

        @pl.when(k == 0)
        def _():
            acc_ref[...] = jnp.zeros_like(acc_ref)

        acc_ref[...] += lax.dot_general(a_ref[...], b_ref[...], (((0,), (0,)), ((), ())),
                                        preferred_element_type=F32)

        @pl.when(k == nt - 1)
        def _():
            o_ref[...] = acc_ref[...].astype(o_ref.dtype)

    a_spec = (pl.BlockSpec((None, tk, m), lambda i: (i // nt, i % nt, 0)) if a3
              else pl.BlockSpec((tk, m), lambda i: (i % nt, 0)))
    b_spec = (pl.BlockSpec((None, tk, n), lambda i: (i // nt, i % nt, 0)) if b3
              else pl.BlockSpec((tk, n), lambda i: (i % nt, 0)))
    (out,), landed = _grid_call(
        body, name, nj * nt, [a_spec, b_spec], [pl.BlockSpec((None, m, n), lambda i: (i // nt, 0, 0))],
        [jax.ShapeDtypeStruct((nj, m, n), BF16)], (a, b), scratch=[pltpu.VMEM((m, n), F32)], exchange=exchange)
    return out if exchange is None else (out, landed)


N_GATE_COLS = 4 * GDN_HEADS
P_QKVA, P_Z, P_AB, P_QKVB = (0, 1536), (1536, 2048), (2048, 2048 + LANES), (2048 + N_GATE_COLS, 3600)
P_PIECES = (P_QKVA, P_Z, P_AB, P_QKVB)


def _mix_in_fwd(x1, nw, wp, tm):
    t, d = x1.shape

    def body(x_ref, nw_ref, w_ref, hn_ref, *outs):
        xf = x_ref[...]
        xn = (xf * _rstd(xf) * nw_ref[...]).astype(BF16)
        hn_ref[...] = xn
        for (a, b), o_ref in zip(P_PIECES, outs):
            o_ref[...] = _dot_nt(xn, w_ref[a:b, :])

    row = pl.BlockSpec((tm, d), lambda i: (i, 0))
    return pl.pallas_call(
        body, name="mix_in_fwd", grid=(t // tm,),
        in_specs=[row, _resident((1, d)), _resident(wp.shape)],
        out_specs=[row] + [pl.BlockSpec((tm, b - a), lambda i: (i, 0)) for a, b in P_PIECES],
        out_shape=[jax.ShapeDtypeStruct((t, d), BF16)]
                  + [jax.ShapeDtypeStruct((t, b - a), F32) for a, b in P_PIECES],
        compiler_params=_params(("arbitrary",), VMEM_LIMIT),
    )(x1, nw, wp)


def _mix_in_bwd_dx(dx, x1, nw, dpieces, wp, tm, exchange=None):
    t, d = x1.shape

    def body(dx_ref, x_ref, nw_ref, p0, p1, p2, p3, w_ref, o_ref, dnw_ref):
        @pl.when(pl.program_id(0) == 0)
        def _():
            dnw_ref[...] = jnp.zeros_like(dnw_ref)

        dh = jnp.zeros((tm, d), F32)
        for (a, b), p_ref in zip(P_PIECES, (p0, p1, p2, p3)):
            dh = dh + _dot(p_ref[...], w_ref[a:b, :])
        xf = x_ref[...]
        dxr, dnw = _rms_bwd(xf, _rstd(xf), nw_ref[...], dh)
        o_ref[...] = dx_ref[...] + dxr
        dnw_ref[...] += dnw

    row = pl.BlockSpec((tm, d), lambda i: (i, 0))
    return _grid_call(
        body, "mix_in_bwd_dx", t // tm,
        [row, row, _resident((1, d))]
        + [pl.BlockSpec((tm, b - a), lambda i: (i, 0)) for a, b in P_PIECES] + [_resident(wp.shape)],
        [row, pl.BlockSpec((1, d), lambda i: (0, 0))],
        [jax.ShapeDtypeStruct((t, d), F32), jax.ShapeDtypeStruct((1, d), F32)],
        (dx, x1, nw, *dpieces, wp), exchange=exchange)


def _mix_out_fwd(x1, oa, ob, w, tm):
    t, d = x1.shape
    half = oa.shape[1]

    def body(x_ref, oa_ref, ob_ref, w_ref, o_ref):
        o_ref[...] = (x_ref[...] + _dot(oa_ref[...], w_ref[0:half, :]) + _dot(ob_ref[...], w_ref[half:2 * half, :]))

    row = pl.BlockSpec((tm, d), lambda i: (i, 0))
    hrow = pl.BlockSpec((tm, half), lambda i: (i, 0))
    return pl.pallas_call(
        body, name="mix_out_fwd", grid=(t // tm,),
        in_specs=[row, hrow, hrow, _resident(w.shape)],
        out_specs=row, out_shape=jax.ShapeDtypeStruct((t, d), F32),
        compiler_params=_params(("arbitrary",), VMEM_LIMIT),
    )(x1, oa, ob, w)


def _mix_out_bwd(dx2, w, tm, exchange=None):
    t, d = dx2.shape
    half = w.shape[0] // 2

    def body(dx_ref, w_ref, doa_ref, dob_ref, dxb_ref):
        dxb = dx_ref[...].astype(BF16)
        dxb_ref[...] = dxb
        doa_ref[...] = _dot_nt(dxb, w_ref[0:half, :])
        dob_ref[...] = _dot_nt(dxb, w_ref[half:2 * half, :])

    row = pl.BlockSpec((tm, d), lambda i: (i, 0))
    hrow = pl.BlockSpec((tm, half), lambda i: (i, 0))
    return _grid_call(
        body, "mix_out_bwd", t // tm, [row, _resident(w.shape)], [hrow, hrow, row],
        [jax.ShapeDtypeStruct((t, half), F32), jax.ShapeDtypeStruct((t, half), F32), jax.ShapeDtypeStruct((t, d), BF16)],
        (dx2, w), exchange=exchange)


HALO = 8


def _halo_row_specs(tr, cols, nrow8):
    per = tr // HALO
    return [pl.BlockSpec((tr, cols), lambda i: (i, 0)),
            pl.BlockSpec((HALO, cols), lambda i: (jnp.maximum(i * per - 1, 0), 0)),
            pl.BlockSpec((HALO, cols), lambda i: (jnp.minimum((i + 1) * per, nrow8 - 1), 0))]


def _fill_window(win_ref, cb, xm, xp, xn, first, last):
    tr = xm.shape[0]
    cols = slice(cb * LANES, (cb + 1) * LANES)
    win_ref[cb, 0:HALO, :] = jnp.where(first, 0.0, xp[:, cols])
    win_ref[cb, HALO:HALO + tr, :] = xm[:, cols]
    win_ref[cb, HALO + tr:HALO + tr + HALO, :] = jnp.where(last, 0.0, xn[:, cols])


def _conv_taps(win_ref, cb, cw_ref, start, rows):
    cols = slice(cb * LANES, (cb + 1) * LANES)
    acc = None
    for j in range(CONV_TAPS):
        term = win_ref[cb, pl.ds(start + j - CONV_TAPS // 2, rows), :] * cw_ref[j:j + 1, cols]
        acc = term if acc is None else acc + term
    return acc


def _softplus(x):
    u = jnp.exp(-jnp.abs(x))
    w = 1.0 + u
    log1p = jnp.where(w == 1.0, u, jnp.log(w) * u / jnp.where(w == 1.0, 1.0, w - 1.0))
    return jnp.maximum(x, 0.0) + log1p


def _gdn_prep_fwd(qkva, cw, ab, gp, tr):
    t, c = qkva.shape
    nt = t // tr
    ncb = c // LANES

    def body(xm, xp, xn, cw_ref, ab_ref, gp_ref, o_ref, gb_ref, xw_ref):
        i = pl.program_id(0)
        first, last = i == 0, i == nt - 1
        for cb in range(ncb):
            cols = slice(cb * LANES, (cb + 1) * LANES)
            _fill_window(xw_ref, cb, xm, xp, xn, first, last)
            pre = _conv_taps(xw_ref, cb, cw_ref, HALO, tr)
            y = pre * _sigmoid(pre)
            if cb < 2 * GDN_HEADS:
                y = y * lax.rsqrt(jnp.sum(y * y, axis=-1, keepdims=True) + EPS)
            if cb < GDN_HEADS:
                y = y * (GDN_DIM ** -0.5)
            o_ref[:, cols] = y
        abv = ab_ref[...]
        lane = lax.broadcasted_iota(jnp.int32, abv.shape, 1)
        g = -jnp.exp(gp_ref[0:1, :]) * _softplus(abv + gp_ref[1:2, :])
        gb_ref[...] = jnp.where(lane < 8, g, jnp.where(lane < 16, _sigmoid(abv), 0.0))

    return pl.pallas_call(
        body, name="gdn_prep_fwd", grid=(nt,),
        in_specs=_halo_row_specs(tr, c, t // HALO)
                 + [_resident(cw.shape), pl.BlockSpec((tr, LANES), lambda i: (i, 0)), _resident(gp.shape)],
        out_specs=[pl.BlockSpec((tr, c), lambda i: (i, 0)), pl.BlockSpec((tr, LANES), lambda i: (i, 0))],
        out_shape=[jax.ShapeDtypeStruct((t, c), F32), jax.ShapeDtypeStruct((t, LANES), F32)],
        scratch_shapes=[pltpu.VMEM((ncb, tr + 2 * HALO, LANES), F32)],
        compiler_params=_params(("arbitrary",), VMEM_LIMIT),
    )(qkva, qkva, qkva, cw, ab, gp)


def _gdn_prep_bwd(qkva, cw, ab, gp, dy, dgates, tr):
    t, c = qkva.shape
    nt = t // tr
    ncb = c // LANES

    ext = HALO // 2
    rows_ext = tr + 2 * ext

    def body(xm, xp, xn, fm, fp, fn, cw_ref, ab_ref, gp_ref, gf_ref, dx_ref, dab_ref, dcw_ref, dgp_ref,
             xw_ref, dyw_ref, dp_ref):
        i = pl.program_id(0)
        first, last = i == 0, i == nt - 1

        @pl.when(first)
        def _():
            dcw_ref[...] = jnp.zeros_like(dcw_ref)
            dgp_ref[...] = jnp.zeros_like(dgp_ref)

        sub8 = lax.broadcasted_iota(jnp.int32, (8, LANES), 0)
        for cb in range(ncb):
            cols = slice(cb * LANES, (cb + 1) * LANES)
            _fill_window(xw_ref, cb, xm, xp, xn, first, last)
            _fill_window(dyw_ref, cb, fm, fp, fn, first, last)
            pre = _conv_taps(xw_ref, cb, cw_ref, HALO - ext, rows_ext)
            dyw = dyw_ref[cb, pl.ds(HALO - ext, rows_ext), :]
            sg = _sigmoid(pre)
            s = pre * sg
            if cb < 2 * GDN_HEADS:
                scale = (GDN_DIM ** -0.5) if cb < GDN_HEADS else 1.0
                r = lax.rsqrt(jnp.sum(s * s, axis=-1, keepdims=True) + EPS)
                dn = dyw * scale
                ds = r * dn - s * (r * r * r) * jnp.sum(dn * s, axis=-1, keepdims=True)
            else:
                ds = dyw
            dp_ref[cb] = ds * (sg * (1.0 + pre * (1.0 - sg)))
            dpre = dp_ref[cb, pl.ds(ext, tr), :]
            dx = None
            dcw = jnp.zeros((8, LANES), F32)
            for j in range(CONV_TAPS):
                off = j - CONV_TAPS // 2
                term = dp_ref[cb, pl.ds(ext - off, tr), :] * cw_ref[j:j + 1, cols]
                dx = term if dx is None else dx + term
                tap = jnp.sum(dpre * xw_ref[cb, pl.ds(HALO + off, tr), :], axis=0, keepdims=True)
                dcw = dcw + jnp.where(sub8 == j, tap, 0.0)
            dx_ref[:, cols] = dx.astype(BF16)
            dcw_ref[:, cols] += dcw

        abv = ab_ref[...]
        dgb = gf_ref[...]
        lane = lax.broadcasted_iota(jnp.int32, abv.shape, 1)
        nea = -jnp.exp(gp_ref[0:1, :])
        xs = abv + gp_ref[1:2, :]
        g = nea * _softplus(xs)
        beta = _sigmoid(abv)
        da = dgb * nea * _sigmoid(xs)
        dab = jnp.where(lane < 8, da, jnp.where(lane < 16, dgb * beta * (1.0 - beta), 0.0))
        dab_ref[...] = dab.astype(BF16)
        keep = lane[0:1, :] < 8
        dalog = jnp.where(keep, jnp.sum(dgb * g, axis=0, keepdims=True), 0.0)
        ddtb = jnp.where(keep, jnp.sum(da, axis=0, keepdims=True), 0.0)
        dgp_ref[...] += jnp.where(sub8 == 0, dalog, 0.0) + jnp.where(sub8 == 1, ddtb, 0.0)

    lrow = pl.BlockSpec((tr, LANES), lambda i: (i, 0))
    halo = _halo_row_specs(tr, c, t // HALO)
    return pl.pallas_call(
        body, name="gdn_prep_bwd", grid=(nt,),
        in_specs=halo + halo + [_resident(cw.shape), lrow, _resident(gp.shape), lrow],
        out_specs=[pl.BlockSpec((tr, c), lambda i: (i, 0)), lrow,
                   pl.BlockSpec(cw.shape, lambda i: (0, 0)), pl.BlockSpec(gp.shape, lambda i: (0, 0))],
        out_shape=[jax.ShapeDtypeStruct((t, c), BF16), jax.ShapeDtypeStruct((t, LANES), BF16),
                   jax.ShapeDtypeStruct(cw.shape, F32), jax.ShapeDtypeStruct(gp.shape, F32)],
        scratch_shapes=[pltpu.VMEM((ncb, tr + 2 * HALO, LANES), F32), pltpu.VMEM((ncb, tr + 2 * HALO, LANES), F32),
                        pltpu.VMEM((ncb, rows_ext, LANES), F32)],
        compiler_params=_params(("arbitrary",), VMEM_LIMIT),
    )(qkva, qkva, qkva, dy, dy, dy, cw, ab, gp, dgates)


def _chunk_masks(lower):
    ii = lax.broadcasted_iota(jnp.int32, (CHUNK, CHUNK), 0)
    jj = lax.broadcasted_iota(jnp.int32, (CHUNK, CHUNK), 1)
    incl = (ii >= jj) if lower else (ii <= jj)
    strict = (ii > jj) if lower else (ii < jj)
    return ii, jj, incl, strict


def _dot3(a, b):
    ah = a.astype(BF16)
    al = (a - ah.astype(F32)).astype(BF16)
    bh = b.astype(BF16)
    bl = (b - bh.astype(F32)).astype(BF16)
    d = lambda u, v: jnp.dot(u, v, preferred_element_type=F32)
    return d(ah, bh) + (d(ah, bl) + d(al, bh))


def _tri_inv_many(lmats, ii, jj):
    m16 = (ii // 16) == (jj // 16)
    m32 = (ii // 32) == (jj // 32)
    eye = jnp.where(ii == jj, 1.0, 0.0)
    l16 = [jnp.where(m16, l, 0.0) for l in lmats]
    p2 = [_dot3(a, a) for a in l16]
    p4 = [_dot3(a, a) for a in p2]
    p8 = [_dot3(a, a) for a in p4]
    xs = [eye - a for a in l16]
    for ps in (p2, p4, p8):
        xs = [x + _dot3(x, p) for x, p in zip(xs, ps)]
    for off in ([jnp.where(m32 & jnp.logical_not(m16), l, 0.0) for l in lmats],
                [jnp.where(m32, 0.0, l) for l in lmats]):
        ys = [_dot3(x, c) for x, c in zip(xs, off)]
        xs = [x - _dot3(y, x) for x, y in zip(xs, ys)]
    return xs


def _col_to_row(col, ii, jj):
    return jnp.sum(jnp.where(ii == jj, col, 0.0), axis=0, keepdims=True)


def _row_to_col(row, ii, jj):
    return jnp.sum(jnp.where(ii == jj, row, 0.0), axis=1, keepdims=True)


def _chain_common(q, k, v, graw_col, graw_row, bcol, masks):
    ii, jj, incl, strict = masks
    inclt = jnp.logical_not(strict)
    gcol = jnp.sum(jnp.where(incl, graw_row, 0.0), axis=1, keepdims=True)
    grow = jnp.sum(jnp.where(inclt, graw_col, 0.0), axis=0, keepdims=True)
    glast = jnp.sum(graw_row, axis=1, keepdims=True)
    decay = jnp.where(incl, jnp.exp(jnp.where(incl, gcol - grow, 0.0)), 0.0)
    kb = k * bcol
    vb = v * bcol
    eg = jnp.exp(gcol)
    ek = jnp.exp(glast - gcol)
    kbg = kb * eg
    amat = _dot_nt(kb, k)
    qk = _dot_nt(q, k)
    return dict(gcol=gcol, glast=glast, decay=decay, kb=kb, vb=vb, eg=eg, ek=ek, kbg=kbg, amat=amat, qk=qk,
                intra=qk * decay, qg=q * eg, kdec=k * ek)


def _gdn_fwd(qkvc, gb, gbt):
    tm, u, w, qg, kd, intra, egl = _gdn_local_fwd(qkvc, gb, gbt)
    o_f, o_b, s_f, s_b, vn_f, vn_b = _gdn_scan_fwd(u, w, qg, kd, intra, egl, qkvc.shape[0])
    return o_f, o_b, dict(tm=tm, w=w, qg=qg, kd=kd, intra=intra, egl=egl, s=(s_f, s_b), vn=(vn_f, vn_b))


N_CHAINS = 2 * GDN_HEADS


LOCAL_CHUNKS = 4


def _load_chains(x_ref, g_ref, gt_ref, cc=0):
    hd = GDN_HEADS * GDN_DIM
    rows = slice(cc * CHUNK, (cc + 1) * CHUNK)
    chains = []
    for d in range(2):
        masks = _chunk_masks(d == 0)
        for h in range(GDN_HEADS):
            ch = d * GDN_HEADS + h
            q = x_ref[rows, h * GDN_DIM:(h + 1) * GDN_DIM]
            k = x_ref[rows, hd + h * GDN_DIM:hd + (h + 1) * GDN_DIM]
            v = x_ref[rows, 2 * hd + h * GDN_DIM:2 * hd + (h + 1) * GDN_DIM]
            bcol = g_ref[rows, 8 + ch:9 + ch]
            cm = _chain_common(q, k, v, g_ref[rows, ch:ch + 1], gt_ref[cc, ch:ch + 1, :], bcol, masks)
            chains.append(dict(cm, q=q, k=k, v=v, bcol=bcol, masks=masks, ch=ch, h=h, cc=cc))
    return chains


def _chain_shape(rows, cols, dtype):
    return lambda nc: jax.ShapeDtypeStruct((nc, N_CHAINS, rows, cols), dtype)


def _gdn_local_fwd(qkvc, gb, gbt):
    t = qkvc.shape[0]
    nc = t // CHUNK
    hd = GDN_HEADS * GDN_DIM

    def body(x_ref, g_ref, gt_ref, t_ref, u_ref, w_ref, qg_ref, kd_ref, in_ref, eg_ref):
        chains = [c for cc in range(LOCAL_CHUNKS) for c in _load_chains(x_ref, g_ref, gt_ref, cc)]
        ii, jj = chains[0]["masks"][0:2]
        tms = _tri_inv_many([jnp.where(c["masks"][3], c["amat"] * c["decay"], 0.0) for c in chains], ii, jj)
        uws = [_dot(tm, jnp.concatenate([c["vb"], c["kbg"]], axis=1)) for tm, c in zip(tms, chains)]
        for c, tm, uw in zip(chains, tms, uws):
            cc, ch = c["cc"], c["ch"]
            t_ref[cc, ch] = tm
            u_ref[cc, ch] = uw[:, :GDN_DIM]
            w_ref[cc, ch] = uw[:, GDN_DIM:].astype(BF16)
            qg_ref[cc, ch] = c["qg"].astype(BF16)
            kd_ref[cc, ch] = c["kdec"].astype(BF16)
            in_ref[cc, ch] = c["intra"].astype(BF16)
            eg_ref[cc, ch:ch + 1, :] = jnp.broadcast_to(jnp.exp(c["glast"]), (1, LANES))

    lc = LOCAL_CHUNKS
    blk = lambda rows, cols: pl.BlockSpec((lc, N_CHAINS, rows, cols), lambda n: (n, 0, 0, 0))
    shapes = [_chain_shape(CHUNK, CHUNK, F32), _chain_shape(CHUNK, GDN_DIM, F32), _chain_shape(CHUNK, GDN_DIM, BF16),
              _chain_shape(CHUNK, GDN_DIM, BF16), _chain_shape(CHUNK, GDN_DIM, BF16), _chain_shape(CHUNK, CHUNK, BF16)]
    return tuple(pl.pallas_call(
        body, name="gdn_local_fwd", grid=(nc // lc,),
        in_specs=[pl.BlockSpec((lc * CHUNK, 3 * hd), lambda n: (n, 0)), pl.BlockSpec((lc * CHUNK, LANES), lambda n: (n, 0)),
                  pl.BlockSpec((lc, 16, CHUNK), lambda n: (n, 0, 0))],
        out_specs=[blk(CHUNK, CHUNK), blk(CHUNK, GDN_DIM), blk(CHUNK, GDN_DIM), blk(CHUNK, GDN_DIM),
                   blk(CHUNK, GDN_DIM), blk(CHUNK, CHUNK), pl.BlockSpec((lc, N_CHAINS, LANES), lambda n: (n, 0, 0))],
        out_shape=[s(nc) for s in shapes] + [jax.ShapeDtypeStruct((nc, N_CHAINS, LANES), F32)],
        compiler_params=_params(("arbitrary",), VMEM_LIMIT),
    )(qkvc, gb, gbt))


SCAN_CHUNKS = 8


def _dir_specs(nc, rev):
    nb = nc // SCAN_CHUNKS

    def spec(d, rows, cols, own=False):
        chunk = (lambda n: n) if (d == 0) != rev else (lambda n: nb - 1 - n)
        blk = 0 if own else d
        if rows is None:
            return pl.BlockSpec((SCAN_CHUNKS, GDN_HEADS if own else N_CHAINS, cols), lambda n: (chunk(n), 0, 0))
        return pl.BlockSpec((SCAN_CHUNKS, GDN_HEADS, rows, cols), lambda n: (chunk(n), blk, 0, 0))

    def rows_spec(d, cols):
        chunk = (lambda n: n) if (d == 0) != rev else (lambda n: nb - 1 - n)
        return pl.BlockSpec((SCAN_CHUNKS * CHUNK, cols), lambda n: (chunk(n), 0))

    def order(d):
        return list(range(SCAN_CHUNKS)) if (d == 0) != rev else list(range(SCAN_CHUNKS - 1, -1, -1))
    return spec, rows_spec, order


def _gdn_scan_fwd(u, w, qg, kd, intra, egl, t):
    nc = t // CHUNK
    hd = GDN_HEADS * GDN_DIM

    def body(*refs):
        ins, outs, state = refs[:12], refs[12:18], refs[18]
        @pl.when(pl.program_id(0) == 0)
        def _():
            state[...] = jnp.zeros_like(state)

        chains = [(d, h) for d in range(2) for h in range(GDN_HEADS)]
        states = [state[ch] for ch in range(N_CHAINS)]
        for step in range(SCAN_CHUNKS):
            at = [order(d)[step] for d in range(2)]
            pick = lambda k, d, h: ins[2 * k + d][at[d], h]
            sbs = [s.astype(BF16) for s in states]
            ws = [_dot(pick(1, d, h), sb) for (d, h), sb in zip(chains, sbs)]
            o1 = [_dot(pick(2, d, h), sb) for (d, h), sb in zip(chains, sbs)]
            vns = [(pick(0, d, h) - wsb).astype(BF16) for (d, h), wsb in zip(chains, ws)]
            o2 = [_dot(pick(4, d, h), vn) for (d, h), vn in zip(chains, vns)]
            kv = [_dot_tn(pick(3, d, h), vn) for (d, h), vn in zip(chains, vns)]
            new_states = []
            for ch, (d, h) in enumerate(chains):
                outs[d][at[d] * CHUNK:(at[d] + 1) * CHUNK, h * GDN_DIM:(h + 1) * GDN_DIM] = o1[ch] + o2[ch]
                outs[2 + d][at[d], h] = sbs[ch]
                outs[4 + d][at[d], h] = vns[ch]
                new_states.append(states[ch] * ins[10 + d][at[d], ch:ch + 1, :] + kv[ch])
            states = new_states
        for ch in range(N_CHAINS):
            state[ch] = states[ch]

    spec, rows_spec, order = _dir_specs(nc, False)
    pair = lambda rows, cols, own=False: [spec(0, rows, cols, own), spec(1, rows, cols, own)]
    s_shape = jax.ShapeDtypeStruct((nc, GDN_HEADS, GDN_DIM, GDN_DIM), BF16)
    vn_shape = jax.ShapeDtypeStruct((nc, GDN_HEADS, CHUNK, GDN_DIM), BF16)
    return pl.pallas_call(
        body, name="gdn_scan_fwd", grid=(nc // SCAN_CHUNKS,),
        in_specs=(pair(CHUNK, GDN_DIM) + pair(CHUNK, GDN_DIM) + pair(CHUNK, GDN_DIM) + pair(CHUNK, GDN_DIM)
                  + pair(CHUNK, CHUNK) + pair(None, LANES)),
        out_specs=([rows_spec(0, hd), rows_spec(1, hd)] + pair(GDN_DIM, GDN_DIM, True)
                   + pair(CHUNK, GDN_DIM, True)),
        out_shape=[jax.ShapeDtypeStruct((t, hd), F32), jax.ShapeDtypeStruct((t, hd), F32),
                   s_shape, s_shape, vn_shape, vn_shape],
        scratch_shapes=[pltpu.VMEM((N_CHAINS, GDN_DIM, GDN_DIM), F32)],
        compiler_params=_params(("arbitrary",), VMEM_LIMIT),
    )(u, u, w, w, qg, qg, kd, kd, intra, intra, egl, egl)


def _gdn_bwd(qkvc, gb, gbt, do, saved, exchange=None):
    scan = _gdn_scan_bwd(do, saved, qkvc.shape[0])
    return _gdn_local_bwd(qkvc, gb, gbt, do, saved, scan, exchange)


def _gdn_scan_bwd(do, saved, t):
    nc = t // CHUNK
    hd = GDN_HEADS * GDN_DIM

    def body(*refs):
        ins, outs, dstate = refs[:16], refs[16:26], refs[26]
        @pl.when(pl.program_id(0) == 0)
        def _():
            dstate[...] = jnp.zeros_like(dstate)

        chains = [(d, h) for d in range(2) for h in range(GDN_HEADS)]
        dss = [dstate[ch] for ch in range(N_CHAINS)]
        for step in range(SCAN_CHUNKS):
            at = [order(d)[step] for d in range(2)]
            pick = lambda k, d, h: ins[2 * k + d][at[d], h]
            dsbs = [ds.astype(BF16) for ds in dss]
            ss = [pick(1, d, h) for d, h in chains]
            sbs = ss
            dos = [ins[d][at[d] * CHUNK:(at[d] + 1) * CHUNK, h * GDN_DIM:(h + 1) * GDN_DIM].astype(BF16)
                   for d, h in chains]
            dv1 = [_dot_tn(pick(5, d, h), dov) for (d, h), dov in zip(chains, dos)]
            dv2 = [_dot(pick(4, d, h), dsb) for (d, h), dsb in zip(chains, dsbs)]
            ds1 = [_dot_tn(pick(3, d, h), dov) for (d, h), dov in zip(chains, dos)]
            dkds = [_dot_nt(pick(6, d, h), dsb) for (d, h), dsb in zip(chains, dsbs)]
            dqgs = [_dot_nt(dov, sb) for dov, sb in zip(dos, sbs)]
            dvns = [(a + b).astype(BF16) for a, b in zip(dv1, dv2)]
            ds2 = [_dot_tn(pick(2, d, h), dvn) for (d, h), dvn in zip(chains, dvns)]
            dws = [_dot_nt(dvn, sb) for dvn, sb in zip(dvns, sbs)]
            new_dss = []
            for ch, (d, h) in enumerate(chains):
                egl = ins[14 + d][at[d], ch:ch + 1, :]
                outs[d][at[d], h] = dvns[ch]
                outs[2 + d][at[d], h] = (-dws[ch]).astype(BF16)
                outs[4 + d][at[d], h] = dqgs[ch]
                outs[6 + d][at[d], h] = dkds[ch]
                outs[8 + d][at[d], h:h + 1, :] = egl * jnp.sum(jnp.sum(ss[ch].astype(F32) * dss[ch], axis=1, keepdims=True),
                                                               axis=0, keepdims=True)
                new_dss.append(ds1[ch] + egl * dss[ch] - ds2[ch])
            dss = new_dss
        for ch in range(N_CHAINS):
            dstate[ch] = dss[ch]

    spec, rows_spec, order = _dir_specs(nc, True)
    pair = lambda rows, cols, own=False: [spec(0, rows, cols, own), spec(1, rows, cols, own)]
    s_f, s_b = saved["s"]
    vn_f, vn_b = saved["vn"]
    w, qg, kd, intra, egl = saved["w"], saved["qg"], saved["kd"], saved["intra"], saved["egl"]
    own = lambda rows, cols, dtype: jax.ShapeDtypeStruct((nc, GDN_HEADS, rows, cols), dtype)
    row_shape = jax.ShapeDtypeStruct((nc, GDN_HEADS, LANES), F32)
    return pl.pallas_call(
        body, name="gdn_scan_bwd", grid=(nc // SCAN_CHUNKS,),
        in_specs=([rows_spec(0, hd), rows_spec(1, hd)] + pair(GDN_DIM, GDN_DIM, True) + pair(CHUNK, GDN_DIM)
                  + pair(CHUNK, GDN_DIM) + pair(CHUNK, GDN_DIM) + pair(CHUNK, CHUNK) + pair(CHUNK, GDN_DIM, True)
                  + pair(None, LANES)),
        out_specs=(pair(CHUNK, GDN_DIM, True) + pair(CHUNK, GDN_DIM, True) + pair(CHUNK, GDN_DIM, True)
                   + pair(CHUNK, GDN_DIM, True) + pair(None, LANES, True)),
        out_shape=[own(CHUNK, GDN_DIM, BF16)] * 4 + [own(CHUNK, GDN_DIM, F32)] * 4 + [row_shape] * 2,
        scratch_shapes=[pltpu.VMEM((N_CHAINS, GDN_DIM, GDN_DIM), F32)],
        compiler_params=_params(("arbitrary",), VMEM_LIMIT),
    )(do, do, s_f, s_b, w, w, qg, qg, kd, kd, intra, intra, vn_f, vn_b, egl, egl)


def _dot3_nt(a, b):
    ah = a.astype(BF16)
    al = (a - ah.astype(F32)).astype(BF16)
    bh = b.astype(BF16)
    bl = (b - bh.astype(F32)).astype(BF16)
    return _dot_nt(ah, bh) + (_dot_nt(ah, bl) + _dot_nt(al, bh))


def _dot3_tn(a, b):
    ah = a.astype(BF16)
    al = (a - ah.astype(F32)).astype(BF16)
    bh = b.astype(BF16)
    bl = (b - bh.astype(F32)).astype(BF16)
    return _dot_tn(ah, bh) + (_dot_tn(ah, bl) + _dot_tn(al, bh))


def _gdn_local_bwd(qkvc, gb, gbt, do, saved, scan, exchange=None):
    t = qkvc.shape[0]
    nc = t // CHUNK
    hd = GDN_HEADS * GDN_DIM

    def body(*refs):
        x_ref, g_ref, gt_ref, do_ref, t_ref = refs[:5]
        per_dir = refs[5:17]
        dx_ref, dg_ref = refs[17:]
        chains = [c for cc in range(LOCAL_CHUNKS) for c in _load_chains(x_ref, g_ref, gt_ref, cc)]
        lane = lax.broadcasted_iota(jnp.int32, (CHUNK, LANES), 1)
        dgates = [jnp.zeros((CHUNK, LANES), F32) for _ in range(LOCAL_CHUNKS)]
        for c in chains:
            d = c["ch"] // GDN_HEADS
            vn_ref, dvn_ref, dw_ref, dqg_ref, dkd_ref, dgl_ref = per_dir[d::2]
            h, cc = c["h"], c["cc"]
            rows = slice(cc * CHUNK, (cc + 1) * CHUNK)
            c.update(tm=t_ref[cc, c["ch"]], dov=do_ref[rows, h * GDN_DIM:(h + 1) * GDN_DIM], vnew=vn_ref[cc, h],
                     dvnew=dvn_ref[cc, h], dw=dw_ref[cc, h], dqg=dqg_ref[cc, h], dkdec=dkd_ref[cc, h],
                     dglast=dgl_ref[cc, h:h + 1, 0:1])
        dintras = [_dot_nt(c["dov"], c["vnew"]) for c in chains]
        dts = [_dot_nt(c["dvnew"], c["vb"]) + _dot_nt(c["dw"], c["kbg"]) for c in chains]
        dvbs = [_dot_tn(c["tm"], c["dvnew"]) for c in chains]
        dkbgs = [_dot_tn(c["tm"], c["dw"]) for c in chains]
        tdts = [_dot3_nt(dt, c["tm"]) for dt, c in zip(dts, chains)]
        dls = [jnp.where(c["masks"][3], -_dot3_tn(c["tm"], tdt), 0.0) for tdt, c in zip(tdts, chains)]
        das = [dl * c["decay"] for dl, c in zip(dls, chains)]
        dqks = [jnp.where(c["masks"][2], di, 0.0) * c["decay"] for di, c in zip(dintras, chains)]
        dkb1 = [_dot(da, c["k"]) for da, c in zip(das, chains)]
        dk1 = [_dot_tn(da, c["kb"]) for da, c in zip(das, chains)]
        dk2 = [_dot_tn(dqk, c["q"]) for dqk, c in zip(dqks, chains)]
        dq1 = [_dot(dqk, c["k"]) for dqk, c in zip(dqks, chains)]
        grads, mms, p_gs, p_betas, p_kds = [], [], [], [], []
        for n, c in enumerate(chains):
            incl = c["masks"][2]
            dkb = dkb1[n] + dkbgs[n] * c["eg"]
            kd = c["dkdec"] * c["kdec"]
            mms.append((dls[n] * c["amat"] + jnp.where(incl, dintras[n], 0.0) * c["qk"]) * c["decay"])
            p_gs.append(c["dqg"] * c["qg"] - kd + dkbgs[n] * c["kbg"])
            p_betas.append(dkb * c["k"] + dvbs[n] * c["v"])
            p_kds.append(kd)
            grads.append((dq1[n] + c["dqg"] * c["eg"],
                          dk1[n] + dk2[n] + c["dkdec"] * c["ek"] + dkb * c["bcol"],
                          dvbs[n] * c["bcol"]))
        row_sums = [jnp.sum(mm, axis=1, keepdims=True) for mm in mms]
        col_sums = [jnp.sum(mm, axis=0, keepdims=True) for mm in mms]
        g_sums = [jnp.sum(pg, axis=1, keepdims=True) for pg in p_gs]
        dbetas = [jnp.sum(pb, axis=1, keepdims=True) for pb in p_betas]
        kd_tots = [jnp.sum(jnp.sum(pk, axis=1, keepdims=True), axis=0, keepdims=True) for pk in p_kds]
        dgcs = [rs - _row_to_col(cs, *c["masks"][0:2]) + gs for rs, cs, gs, c in zip(row_sums, col_sums, g_sums, chains)]
        dgrs = [_col_to_row(dgc, *c["masks"][0:2]) for dgc, c in zip(dgcs, chains)]
        draws = [jnp.sum(jnp.where(jnp.logical_not(c["masks"][3]), dgr, 0.0), axis=1, keepdims=True) + c["dglast"] + kt
                 for dgr, kt, c in zip(dgrs, kd_tots, chains)]
        for c, draw, dbeta in zip(chains, draws, dbetas):
            ch = c["ch"]
            dgates[c["cc"]] = dgates[c["cc"]] + jnp.where(lane == ch, draw, 0.0) + jnp.where(lane == 8 + ch, dbeta, 0.0)
        for cc in range(LOCAL_CHUNKS):
            rows = slice(cc * CHUNK, (cc + 1) * CHUNK)
            for h in range(GDN_HEADS):
                for part in range(3):
                    cols = slice(part * hd + h * GDN_DIM, part * hd + (h + 1) * GDN_DIM)
                    dx_ref[rows, cols] = grads[cc * N_CHAINS + h][part] + grads[cc * N_CHAINS + GDN_HEADS + h][part]
            dg_ref[rows, :] = dgates[cc]

    lc = LOCAL_CHUNKS
    all8 = lambda rows, cols: pl.BlockSpec((lc, N_CHAINS, rows, cols), lambda n: (n, 0, 0, 0))
    own4 = lambda rows, cols: pl.BlockSpec((lc, GDN_HEADS, rows, cols), lambda n: (n, 0, 0, 0))
    row4 = pl.BlockSpec((lc, GDN_HEADS, LANES), lambda n: (n, 0, 0))
    vn_f, vn_b = saved["vn"]
    dvn_f, dvn_b, dw_f, dw_b, dqg_f, dqg_b, dkd_f, dkd_b, dgl_f, dgl_b = scan
    return _grid_call(
        body, "gdn_local_bwd", nc // lc,
        [pl.BlockSpec((lc * CHUNK, 3 * hd), lambda n: (n, 0)), pl.BlockSpec((lc * CHUNK, LANES), lambda n: (n, 0)),
         pl.BlockSpec((lc, 16, CHUNK), lambda n: (n, 0, 0)), pl.BlockSpec((lc * CHUNK, hd), lambda n: (n, 0)),
         all8(CHUNK, CHUNK)] + [own4(CHUNK, GDN_DIM)] * 10 + [row4, row4],
        [pl.BlockSpec((lc * CHUNK, 3 * hd), lambda n: (n, 0)), pl.BlockSpec((lc * CHUNK, LANES), lambda n: (n, 0))],
        [jax.ShapeDtypeStruct((t, 3 * hd), F32), jax.ShapeDtypeStruct((t, LANES), F32)],
        (qkvc, gb, gbt, do, saved["tm"], vn_f, vn_b, dvn_f, dvn_b, dw_f, dw_b, dqg_f, dqg_b, dkd_f, dkd_b, dgl_f, dgl_b),
        exchange=exchange)


def _gdn_post_fwd(of, ob, z, gw, tm):
    t, hd = of.shape

    def body(of_ref, ob_ref, z_ref, w_ref, o_ref):
        for h in range(GDN_HEADS):
            cols = slice(h * GDN_DIM, (h + 1) * GDN_DIM)
            o = of_ref[:, cols] + ob_ref[:, cols]
            zv = z_ref[:, cols]
            o_ref[:, cols] = (o * _rstd(o) * w_ref[...] * (zv * _sigmoid(zv))).astype(BF16)

    row = pl.BlockSpec((tm, hd), lambda i: (i, 0))
    return pl.pallas_call(
        body, name="gdn_post_fwd", grid=(t // tm,),
        in_specs=[row, row, row, _resident((1, GDN_DIM))],
        out_specs=row, out_shape=jax.ShapeDtypeStruct((t, hd), BF16),
        compiler_params=_params(("arbitrary",), VMEM_LIMIT),
    )(of, ob, z, gw)


def _gdn_post_bwd(doa, of, ob, z, gw, tm):
    t, hd = of.shape

    def body(d_ref, of_ref, ob_ref, z_ref, w_ref, do_ref, dz_ref, dw_ref):
        @pl.when(pl.program_id(0) == 0)
        def _():
            dw_ref[...] = jnp.zeros_like(dw_ref)

        dw = jnp.zeros((1, GDN_DIM), F32)
        for h in range(GDN_HEADS):
            cols = slice(h * GDN_DIM, (h + 1) * GDN_DIM)
            o = of_ref[:, cols] + ob_ref[:, cols]
            zv = z_ref[:, cols]
            dv = d_ref[:, cols]
            r = _rstd(o)
            sg = _sigmoid(zv)
            on = o * r * w_ref[...]
            dz_ref[:, cols] = (dv * on * (sg * (1.0 + zv * (1.0 - sg)))).astype(BF16)
            dxr, dwh = _rms_bwd(o, r, w_ref[...], dv * (zv * sg))
            do_ref[:, cols] = dxr
            dw = dw + dwh
        dw_ref[...] += dw

    row = pl.BlockSpec((tm, hd), lambda i: (i, 0))
    return pl.pallas_call(
        body, name="gdn_post_bwd", grid=(t // tm,),
        in_specs=[row, row, row, row, _resident((1, GDN_DIM))],
        out_specs=[row, row, pl.BlockSpec((1, GDN_DIM), lambda i: (0, 0))],
        out_shape=[jax.ShapeDtypeStruct((t, hd), F32), jax.ShapeDtypeStruct((t, hd), BF16),
                   jax.ShapeDtypeStruct((1, GDN_DIM), F32)],
        compiler_params=_params(("arbitrary",), VMEM_LIMIT),
    )(doa, of, ob, z, gw)


SWA_W = SWA_HEADS * SWA_DIM
QBLK = 128
KWIN = QBLK + 2 * RADIUS
WIN_OFFSETS = (0, RADIUS, 2 * RADIUS)


def _t5_bucket(rel):
    nb = REL_BUCKETS // 2
    bucket = (rel > 0).astype(np.int32) * nb
    n = np.abs(rel)
    max_exact = nb // 2
    large = max_exact + (np.log(np.maximum(n, 1) / max_exact)
                         / math.log(REL_MAX_DISTANCE / max_exact) * (nb - max_exact)).astype(np.int32)
    large = np.minimum(large, nb - 1)
    return (bucket + np.where(n < max_exact, n, large)).astype(np.int32)


def _band_tables(dilation):
    a = np.arange(QBLK)
    b = np.arange(KWIN)
    rel = np.stack([b[None, :] - w0 - a[:, None] for w0 in WIN_OFFSETS])
    return np.where(np.abs(rel) <= RADIUS, _t5_bucket(rel * dilation), -1).astype(np.int32)


BAND_CELLS = len(WIN_OFFSETS) * QBLK * KWIN
BIAS_TILE = BAND_CELLS // 3


def _band_index():
    return jnp.asarray(np.concatenate([_band_tables(d).reshape(-1) for _, d in PATTERNS])[None, :])


def _onehot(idx, dtype):
    return (lax.broadcasted_iota(jnp.int32, (REL_BUCKETS, idx.shape[1]), 0) == idx).astype(dtype)


def _bias_tables(rel_bias, idx, tk):
    n = idx.shape[1]

    def body(rb_ref, i_ref, o_ref):
        iv = i_ref[...]
        oh = _onehot(iv, BF16)
        rest, acc = rb_ref[...], None
        for _ in range(3):
            piece = rest.astype(BF16)
            part = jnp.dot(piece, oh, preferred_element_type=F32)
            acc = part if acc is None else acc + part
            rest = rest - piece.astype(F32)
        o_ref[...] = jnp.where(iv < 0, NEG_BIG, acc)

    return pl.pallas_call(
        body, name="bias_tables", grid=(n // tk,),
        in_specs=[_resident((SWA_HEADS, REL_BUCKETS)), pl.BlockSpec((1, tk), lambda k: (0, k))],
        out_specs=pl.BlockSpec((SWA_HEADS, tk), lambda k: (0, k)),
        out_shape=jax.ShapeDtypeStruct((SWA_HEADS, n), F32),
        compiler_params=_params(("arbitrary",), VMEM_LIMIT),
    )(rel_bias.T, idx)


def _head_mean(x2, bd_ref):
    bd = bd_ref[...]
    rest, acc = x2, None
    for _ in range(3):
        piece = rest.astype(BF16)
        part = jnp.dot(piece, bd, preferred_element_type=F32)
        acc = part if acc is None else acc + part
        rest = rest - piece.astype(F32)
    return acc


VIEW_DILATIONS = tuple(d for _, d in PATTERNS if d > 1)


def _view_spec(tm, d):
    return pl.BlockSpec((tm // d, d * SWA_W), lambda i: (i, 0))


def _view_shape(t, d, dtype):
    return jax.ShapeDtypeStruct((t // d, d * SWA_W), dtype)


N_GROUPS = SWA_W // LANES


def _to_view(src_ref, idx, dst_ref, d, rows):
    for r in range(d):
        for g in range(N_GROUPS):
            cols = slice(r * SWA_W + g * LANES, r * SWA_W + (g + 1) * LANES)
            dst_ref[:, cols] = src_ref[idx, g, pl.ds(r, rows // d, stride=d), :].astype(dst_ref.dtype)


def _from_view(src_ref, dst_ref, idx, d, rows):
    for r in range(d):
        for g in range(N_GROUPS):
            cols = slice(r * SWA_W + g * LANES, r * SWA_W + (g + 1) * LANES)
            dst_ref[idx, g, pl.ds(r, rows // d, stride=d), :] = src_ref[:, cols]


def _swa_prep_fwd(qkvb, qw, kw, bd, tm):
    t = qkvb.shape[0]

    def body(x_ref, qw_ref, kw_ref, bd_ref, *rest):
        outs, sc = rest[:-1], rest[-1]
        for gidx in range(N_GROUPS):
            cols = slice(gidx * LANES, (gidx + 1) * LANES)
            xq = x_ref[:, cols]
            sc[0, gidx] = xq * lax.rsqrt(_head_mean(xq * xq, bd_ref) + EPS) * qw_ref[:, cols] * (SWA_DIM ** -0.5)
            xk = x_ref[:, SWA_W + gidx * LANES:SWA_W + (gidx + 1) * LANES]
            sc[1, gidx] = xk * lax.rsqrt(_head_mean(xk * xk, bd_ref) + EPS) * kw_ref[:, cols]
            sc[2, gidx] = x_ref[:, 2 * SWA_W + gidx * LANES:2 * SWA_W + (gidx + 1) * LANES]
            for i in range(3):
                outs[i][:, cols] = sc[i, gidx].astype(BF16)
        for i in range(3):
            for n, d in enumerate(VIEW_DILATIONS):
                _to_view(sc, i, outs[3 * (n + 1) + i], d, tm)

    return pl.pallas_call(
        body, name="swa_prep_fwd", grid=(t // tm,),
        in_specs=[pl.BlockSpec((tm, 3 * SWA_W), lambda i: (i, 0)), _resident((1, SWA_W)), _resident((1, SWA_W)),
                  _resident((LANES, LANES))],
        out_specs=[_view_spec(tm, d) for d in (1,) + VIEW_DILATIONS for _ in range(3)],
        out_shape=[_view_shape(t, d, BF16) for d in (1,) + VIEW_DILATIONS for _ in range(3)],
        scratch_shapes=[pltpu.VMEM((3, N_GROUPS, tm, LANES), F32)],
        compiler_params=_params(("arbitrary",), VMEM_LIMIT),
    )(qkvb, qw, kw, bd)


def _swa_prep_bwd(qkvb, qw, kw, bd, grads, tm):
    t = qkvb.shape[0]

    def body(x_ref, qw_ref, kw_ref, bd_ref, *rest):
        parts, (dx_ref, dqw_ref, dkw_ref, sc) = rest[:9], rest[9:]
        @pl.when(pl.program_id(0) == 0)
        def _():
            dqw_ref[...] = jnp.zeros_like(dqw_ref)
            dkw_ref[...] = jnp.zeros_like(dkw_ref)

        for i in range(3):
            for n, d in enumerate(VIEW_DILATIONS):
                _from_view(parts[3 * (n + 1) + i], sc, 2 * i + n, d, tm)
        for gidx in range(N_GROUPS):
            cols = slice(gidx * LANES, (gidx + 1) * LANES)
            for i, base, w_ref, dw_ref, scale in ((0, 0, qw_ref, dqw_ref, SWA_DIM ** -0.5),
                                                  (1, SWA_W, kw_ref, dkw_ref, 1.0)):
                xv = x_ref[:, base + gidx * LANES:base + (gidx + 1) * LANES]
                dy = (parts[i][:, cols] + sc[2 * i, gidx] + sc[2 * i + 1, gidx]) * scale
                r = lax.rsqrt(_head_mean(xv * xv, bd_ref) + EPS)
                xhat = xv * r
                dxh = dy * w_ref[:, cols]
                dx = r * (dxh - xhat * _head_mean(dxh * xhat, bd_ref))
                dx_ref[:, base + gidx * LANES:base + (gidx + 1) * LANES] = dx.astype(BF16)
                dw_ref[:, cols] += jnp.sum(dy * xhat, axis=0, keepdims=True)
            dx_ref[:, 2 * SWA_W + gidx * LANES:2 * SWA_W + (gidx + 1) * LANES] = (
                parts[2][:, cols] + sc[4, gidx] + sc[5, gidx]).astype(BF16)

    wrow = pl.BlockSpec((1, SWA_W), lambda i: (0, 0))
    return pl.pallas_call(
        body, name="swa_prep_bwd", grid=(t // tm,),
        in_specs=[pl.BlockSpec((tm, 3 * SWA_W), lambda i: (i, 0)), _resident((1, SWA_W)), _resident((1, SWA_W)),
                  _resident((LANES, LANES))] + [_view_spec(tm, d) for d in (1,) + VIEW_DILATIONS for _ in range(3)],
        out_specs=[pl.BlockSpec((tm, 3 * SWA_W), lambda i: (i, 0)), wrow, wrow],
        out_shape=[jax.ShapeDtypeStruct((t, 3 * SWA_W), BF16), jax.ShapeDtypeStruct((1, SWA_W), F32),
                   jax.ShapeDtypeStruct((1, SWA_W), F32)],
        scratch_shapes=[pltpu.VMEM((6, N_GROUPS, tm, LANES), F32)],
        compiler_params=_params(("arbitrary",), VMEM_LIMIT),
    )(qkvb, qw, kw, bd, *grads)


def _aligned(v, m):
    return v if isinstance(v, int) else pl.multiple_of(v, m)


BAND_GROUP = 2


def _band_loop(nsub, length, step, group=BAND_GROUP):
    step([(0, 0)], 0)
    if nsub > 2:
        assert (nsub - 2) % group == 0

        def inner(i, carry):
            s0 = 1 + i * group
            step([(s0 + e, pl.multiple_of((s0 + e) * QBLK - RADIUS, RADIUS)) for e in range(group)], 1)
            return carry
        lax.fori_loop(0, (nsub - 2) // group, inner, 0)
    step([(nsub - 1, length - KWIN)], 2)


def _head_select(lane, a0, a1):
    return jnp.where(lane < SWA_DIM, a0, a1)


def _swa_fwd(qv, kv, vv, bias, dilation, name):
    length = qv.shape[0]
    nsub = length // QBLK
    assert nsub >= 2 and length % QBLK == 0

    def body(q_ref, k_ref, v_ref, b_ref, o_ref, l_ref):
        lane = lax.broadcasted_iota(jnp.int32, (QBLK, LANES), 1)

        def step(blocks, var):
            items = []
            for s, ws in blocks:
                rows = pl.ds(_aligned(s * QBLK, QBLK), QBLK)
                q, kk, vw = q_ref[rows, :], k_ref[pl.ds(ws, KWIN), :], v_ref[pl.ds(ws, KWIN), :]
                for hh in range(2):
                    items.append((hh, jnp.where((lane < SWA_DIM) == (hh == 0), q, jnp.zeros_like(q)), kk, vw))
            lgs = [_dot_nt(qh, kk) + b_ref[hh, var] for hh, qh, kk, _ in items]
            ms = [jnp.max(lg, axis=-1, keepdims=True) for lg in lgs]
            ps = [jnp.exp(lg - m) for lg, m in zip(lgs, ms)]
            dens = [jnp.sum(p, axis=-1, keepdims=True) for p in ps]
            pvs = [_dot(p, it[3]) for p, it in zip(ps, items)]
            for n, (s, _) in enumerate(blocks):
                rows = pl.ds(_aligned(s * QBLK, QBLK), QBLK)
                o0, o1 = (pvs[2 * n + hh] / dens[2 * n + hh] for hh in range(2))
                l0, l1 = (ms[2 * n + hh] + jnp.log(dens[2 * n + hh]) for hh in range(2))
                o_ref[rows, :] = _head_select(lane, o0, o1)
                l_ref[rows, :] = _head_select(lane, l0, l1)

        _band_loop(nsub, length, step)

    blk = pl.BlockSpec((length, LANES), lambda hp, r: (0, r * (SWA_W // LANES) + hp))
    shp = jax.ShapeDtypeStruct(qv.shape, F32)
    return pl.pallas_call(
        body, name=name, grid=(SWA_W // LANES, dilation),
        in_specs=[blk, blk, blk, pl.BlockSpec((2, 3, QBLK, KWIN), lambda hp, r: (hp, 0, 0, 0))],
        out_specs=[blk, blk], out_shape=[shp, shp],
        compiler_params=_params(("arbitrary", "arbitrary"), VMEM_LIMIT),
    )(qv, kv, vv, bias)


def _swa_combine(os_, ls_, tm):
    t = os_[0].shape[0]

    def body(o0, o1, o2, l0, l1, l2, o_ref, ob_ref, la_ref, lb_ref, lc_ref, sc):
        for n, d in enumerate(VIEW_DILATIONS):
            _from_view((o1, o2)[n], sc, n, d, tm)
            _from_view((l1, l2)[n], sc, 2 + n, d, tm)
        for g in range(N_GROUPS):
            cols = slice(g * LANES, (g + 1) * LANES)
            la, lb, lc = l0[:, cols], sc[2, g], sc[3, g]
            m = jnp.maximum(jnp.maximum(la, lb), lc)
            tot = m + jnp.log(jnp.exp(la - m) + jnp.exp(lb - m) + jnp.exp(lc - m))
            o = jnp.exp(la - tot) * o0[:, cols] + jnp.exp(lb - tot) * sc[0, g] + jnp.exp(lc - tot) * sc[1, g]
            o_ref[:, cols] = o
            ob_ref[:, cols] = o.astype(BF16)
            la_ref[:, cols] = tot
            sc[4, g] = tot
        for n, d in enumerate(VIEW_DILATIONS):
            _to_view(sc, 4, (lb_ref, lc_ref)[n], d, tm)

    specs = [_view_spec(tm, d) for d in (1,) + VIEW_DILATIONS]
    return pl.pallas_call(
        body, name="swa_combine", grid=(t // tm,), in_specs=specs + specs, out_specs=[specs[0], specs[0]] + specs,
        out_shape=[jax.ShapeDtypeStruct((t, SWA_W), F32), jax.ShapeDtypeStruct((t, SWA_W), BF16)]
                  + [_view_shape(t, d, F32) for d in (1,) + VIEW_DILATIONS],
        scratch_shapes=[pltpu.VMEM((5, N_GROUPS, tm, LANES), F32)],
        compiler_params=_params(("arbitrary",), VMEM_LIMIT),
    )(*os_, *ls_)


def _swa_bwd_prep(do, o, bd, tm):
    t = do.shape[0]

    def body(d_ref, o_ref, bd_ref, dd1, dd4, dd16, db1, db4, db16, sc):
        for gidx in range(N_GROUPS):
            cols = slice(gidx * LANES, (gidx + 1) * LANES)
            dv = d_ref[:, cols]
            dd = _head_mean(dv * o_ref[:, cols], bd_ref) * float(SWA_DIM)
            sc[0, gidx] = dd
            sc[1, gidx] = dv
            dd1[:, cols] = dd
            db1[:, cols] = dv.astype(BF16)
        for n, d in enumerate(VIEW_DILATIONS):
            _to_view(sc, 0, (dd4, dd16)[n], d, tm)
            _to_view(sc, 1, (db4, db16)[n], d, tm)

    specs = [_view_spec(tm, d) for d in (1,) + VIEW_DILATIONS]
    return pl.pallas_call(
        body, name="swa_bwd_prep", grid=(t // tm,), in_specs=[specs[0], specs[0], _resident((LANES, LANES))],
        out_specs=specs + specs,
        out_shape=[_view_shape(t, d, F32) for d in (1,) + VIEW_DILATIONS]
                  + [_view_shape(t, d, BF16) for d in (1,) + VIEW_DILATIONS],
        scratch_shapes=[pltpu.VMEM((2, N_GROUPS, tm, LANES), F32)],
        compiler_params=_params(("arbitrary",), VMEM_LIMIT),
    )(do, o, bd)


def _swa_bwd(qv, kv, vv, dov, lv, ddv, bias_a, dilation, name):
    length = qv.shape[0]
    nsub = length // QBLK
    single = pl.Buffered(1) if dilation == 1 else None

    def body(q_ref, k_ref, v_ref, do_ref, l_ref, dd_ref, ba_ref, dq_ref, dk_ref, dv_ref, db_ref):
        @pl.when(pl.program_id(1) == 0)
        def _():
            db_ref[...] = jnp.zeros_like(db_ref)

        lane = lax.broadcasted_iota(jnp.int32, (QBLK, LANES), 1)
        lanew = lax.broadcasted_iota(jnp.int32, (KWIN, LANES), 1)

        def step(blocks, var):
            items = []
            for s, ws in blocks:
                rows = pl.ds(_aligned(s * QBLK, QBLK), QBLK)
                win = pl.ds(ws, KWIN)
                q, dov_ = q_ref[rows, :], do_ref[rows, :]
                kk, vw = k_ref[win, :], v_ref[win, :]
                lse, dd = l_ref[rows, :], dd_ref[rows, :]
                for hh in range(2):
                    mine = (lane < SWA_DIM) == (hh == 0)
                    col = slice(hh * SWA_DIM, hh * SWA_DIM + 1)
                    items.append((hh, jnp.where(mine, q, jnp.zeros_like(q)), jnp.where(mine, dov_, jnp.zeros_like(dov_)),
                                  kk, vw, lse[:, col], dd[:, col], q, dov_))
            lgs = [_dot_nt(it[1], it[3]) + ba_ref[it[0], var] for it in items]
            dps = [_dot_nt(it[2], it[4]) for it in items]
            ps = [jnp.exp(lg - it[5]) for lg, it in zip(lgs, items)]
            dss = [p * (dp - it[6]) for p, dp, it in zip(ps, dps, items)]
            dqs = [_dot(ds, it[3]) for ds, it in zip(dss, items)]
            dks = [_dot_tn(ds, it[7]) for ds, it in zip(dss, items)]
            dvs = [_dot_tn(p, it[8]) for p, it in zip(ps, items)]
            for n, (s, ws) in enumerate(blocks):
                rows = pl.ds(_aligned(s * QBLK, QBLK), QBLK)
                win = pl.ds(ws, KWIN)
                dq_ref[rows, :] = _head_select(lane, dqs[2 * n], dqs[2 * n + 1])
                dk_ref[win, :] += _head_select(lanew, dks[2 * n], dks[2 * n + 1])
                dv_ref[win, :] += _head_select(lanew, dvs[2 * n], dvs[2 * n + 1])
            for hh in range(2):
                tot = dss[hh]
                for n in range(1, len(blocks)):
                    tot = tot + dss[2 * n + hh]
                db_ref[hh, var] += tot

        dk_ref[...] = jnp.zeros_like(dk_ref)
        dv_ref[...] = jnp.zeros_like(dv_ref)
        _band_loop(nsub, length, step)

    imap = lambda hp, r: (0, r * (SWA_W // LANES) + hp)
    blk_in = pl.BlockSpec((length, LANES), imap, pipeline_mode=single)
    blk_out = pl.BlockSpec((length, LANES), imap)
    shp = jax.ShapeDtypeStruct(qv.shape, F32)
    return pl.pallas_call(
        body, name=name, grid=(SWA_W // LANES, dilation),
        in_specs=[blk_in] * 6 + [pl.BlockSpec((2, 3, QBLK, KWIN), lambda hp, r: (hp, 0, 0, 0))],
        out_specs=[blk_out, blk_out, blk_out, pl.BlockSpec((2, 3, QBLK, KWIN), lambda hp, r: (hp, 0, 0, 0))],
        out_shape=[shp, shp, shp, jax.ShapeDtypeStruct((SWA_HEADS, 3, QBLK, KWIN), F32)],
        compiler_params=_params(("arbitrary", "arbitrary"), VMEM_LIMIT),
    )(qv, kv, vv, dov, lv, ddv, bias_a)


def _bias_grad(ds2, idx, tk):
    n = ds2.shape[1]
    nk = n // tk

    def body(a_ref, i_ref, o_ref):
        @pl.when(pl.program_id(0) == 0)
        def _():
            o_ref[...] = jnp.zeros_like(o_ref)

        oh = _onehot(i_ref[...], BF16)
        rest = a_ref[...]
        acc = jnp.zeros((SWA_HEADS, REL_BUCKETS), F32)
        for _ in range(3):
            piece = rest.astype(BF16)
            acc = acc + _dot_nt(piece, oh)
            rest = rest - piece.astype(F32)
        o_ref[...] += acc

    return pl.pallas_call(
        body, name="bias_grad", grid=(nk,),
        in_specs=[pl.BlockSpec((SWA_HEADS, tk), lambda k: (0, k)), pl.BlockSpec((1, tk), lambda k: (0, k))],
        out_specs=pl.BlockSpec((SWA_HEADS, REL_BUCKETS), lambda k: (0, 0)),
        out_shape=jax.ShapeDtypeStruct((SWA_HEADS, REL_BUCKETS), F32),
        compiler_params=_params(("arbitrary",), VMEM_LIMIT),
    )(ds2, idx)


def _swa_branch_fwd(qkvb, qw_t, kw_t, rel_bias, bd, tm):
    qkv = _swa_prep_fwd(qkvb, qw_t, kw_t, bd, tm)
    tables = _bias_tables(rel_bias, _band_index(), BIAS_TILE)
    os_, ls_, tabs = [], [], []
    for n, (_, d) in enumerate(PATTERNS):
        bias = tables[:, n * BAND_CELLS:(n + 1) * BAND_CELLS].reshape(SWA_HEADS, len(WIN_OFFSETS), QBLK, KWIN)
        o_p, l_p = _swa_fwd(*qkv[3 * n:3 * n + 3], bias, d, f"swa_fwd_d{d}")
        os_.append(o_p)
        ls_.append(l_p)
        tabs.append(bias)
    o, o16, *lses = _swa_combine(os_, ls_, tm)
    return o, o16, (qkv, lses, tabs)


def _swa_branch_bwd(do, o, saved, qkvb, qw_t, kw_t, bd, tm):
    qkv, lses, tabs = saved
    prep = _swa_bwd_prep(do, o, bd, tm)
    grads, dss = [], []
    for n, ((_, d), bias) in enumerate(zip(PATTERNS, tabs)):
        dq, dk, dv, ds = _swa_bwd(*qkv[3 * n:3 * n + 3], prep[3 + n], lses[n], prep[n], bias, d, f"swa_bwd_d{d}")
        grads += [dq, dk, dv]
        dss.append(ds.reshape(SWA_HEADS, -1))
    dqkvb, dqw, dkw = _swa_prep_bwd(qkvb, qw_t, kw_t, bd, grads, tm)
    dbias = _bias_grad(jnp.concatenate(dss, axis=1), _band_index(), BIAS_TILE)
    fold = lambda w: jnp.sum(w.reshape(SWA_HEADS, SWA_DIM), axis=0)
    return dqkvb, fold(dqw), fold(dkw), dbias.T


def _mesh_pos():
    return lax.axis_index("x"), lax.axis_index("y"), lax.axis_index("c")


def _other_chips(x, y):
    return [(1 - x, y), (x, 1 - y), (1 - x, 1 - y)]


def _remote(src, dst, send_sem, recv_sem, device):
    return pltpu.make_async_remote_copy(src_ref=src, dst_ref=dst, send_sem=send_sem, recv_sem=recv_sem,
                                        device_id=device, device_id_type=MESH)


def _split_axis(shape2):
    return 0 if (shape2[0] // 2) % 16 == 0 else 1


def _half_index(shape2, c):
    axis = _split_axis(shape2)
    h = shape2[axis] // 2
    return (pl.ds(c * h, h), slice(None)) if axis == 0 else (slice(None), pl.ds(c * h, h))


def _all_gather(xs):
    n = len(xs)

    def body(*refs):
        ins, outs = refs[:n], refs[n:2 * n]
        send_sems, recv_sems = refs[2 * n:]
        x, y, c = _mesh_pos()
        me = 2 * x + y
        chips = _other_chips(x, y)
        halves = []
        sends = []
        for a in range(n):
            h = ins[a].shape[0] // 2
            mine, other = pl.ds(c * h, h), pl.ds((1 - c) * h, h)
            halves.append((mine, other))
            own = _remote(ins[a], outs[a].at[me], send_sems.at[a, 6], recv_sems.at[a, 6], (x, y, 1 - c))
            own.start()
            sends.append(own)
            for j, chip in enumerate(chips):
                cp = _remote(ins[a].at[mine], outs[a].at[me, mine], send_sems.at[a, j], recv_sems.at[a, j], (*chip, c))
                cp.start()
                sends.append(cp)
        for a in range(n):
            mine, _ = halves[a]
            for j, chip in enumerate(chips):
                src = 2 * chip[0] + chip[1]
                landed = outs[a].at[src, mine]
                _remote(landed, landed, send_sems.at[a, j], recv_sems.at[a, j], (x, y, c)).wait_recv()
                fwd = _remote(landed, landed, send_sems.at[a, 3 + j], recv_sems.at[a, 3 + j], (x, y, 1 - c))
                fwd.start()
                sends.append(fwd)
        for a in range(n):
            _, other = halves[a]
            for j, chip in enumerate(chips):
                src = 2 * chip[0] + chip[1]
                landed = outs[a].at[src, other]
                _remote(landed, landed, send_sems.at[a, 3 + j], recv_sems.at[a, 3 + j], (x, y, c)).wait_recv()
            mine_slot = outs[a].at[me]
            _remote(mine_slot, mine_slot, send_sems.at[a, 6], recv_sems.at[a, 6], (x, y, c)).wait_recv()
        for cp in sends:
            cp.wait_send()

    return list(pl.pallas_call(
        body, name="all_gather_weights",
        in_specs=[ANY] * n, out_specs=[ANY] * n,
        out_shape=[jax.ShapeDtypeStruct((N_SHARDS,) + a.shape, a.dtype) for a in xs],
        scratch_shapes=[pltpu.SemaphoreType.DMA((n, 7)), pltpu.SemaphoreType.DMA((n, 7))],
    )(*xs))


def _rs_pair(gs):
    n = len(gs)

    def body(*refs):
        ins, lands = refs[:n], refs[n:2 * n]
        send_sems, recv_sems = refs[2 * n:]
        x, y, c = _mesh_pos()
        cps = []
        for a in range(n):
            theirs = (slice(None),) + _half_index(ins[a].shape[1:], 1 - c)
            cp = _remote(ins[a].at[theirs], lands[a], send_sems.at[a], recv_sems.at[a], (x, y, 1 - c))
            cp.start()
            cps.append(cp)
        for cp in cps:
            cp.wait()

    def half_shape(g):
        dims = list(g.shape)
        dims[1 + _split_axis(g.shape[1:])] //= 2
        return tuple(dims)

    return list(pl.pallas_call(
        body, name="rs_pair", in_specs=[ANY] * n, out_specs=[ANY] * n,
        out_shape=[jax.ShapeDtypeStruct(half_shape(g), g.dtype) for g in gs],
        scratch_shapes=[pltpu.SemaphoreType.DMA((n,)), pltpu.SemaphoreType.DMA((n,))],
    )(*gs))


def _pair_exchange(gs):
    def copies(cin, cout, send_sems, recv_sems):
        x, y, c = _mesh_pos()
        return [_remote(g.at[(slice(None),) + _half_index(g.shape[1:], 1 - c)], land, send_sems.at[a, 0],
                        recv_sems.at[a, 0], (x, y, 1 - c)) for a, (g, land) in enumerate(zip(cin, cout))]

    def start(*refs):
        for cp in copies(*refs):
            cp.start()

    def finish(*refs):
        for cp in copies(*refs):
            cp.wait()

    def half_shape(g):
        dims = list(g.shape)
        dims[1 + _split_axis(g.shape[1:])] //= 2
        return tuple(dims)

    return _Exchange(tuple(gs), tuple(jax.ShapeDtypeStruct(half_shape(g), g.dtype) for g in gs), start, finish)


def _rs_chips(ss):
    n = len(ss)

    def body(*refs):
        ins, outs = refs[:n], refs[n:2 * n]
        send_sems, recv_sems = refs[2 * n:]
        x, y, c = _mesh_pos()
        me = 2 * x + y
        chips = _other_chips(x, y)
        cps = []
        for a in range(n):
            for j, chip in enumerate(chips):
                dst_chip = 2 * chip[0] + chip[1]
                cp = _remote(ins[a].at[dst_chip], outs[a].at[me], send_sems.at[a, j], recv_sems.at[a, j], (*chip, c))
                cp.start()
                cps.append(cp)
        for a in range(n):
            for j, chip in enumerate(chips):
                src = 2 * chip[0] + chip[1]
                _remote(outs[a].at[src], outs[a].at[src], send_sems.at[a, j], recv_sems.at[a, j], (x, y, c)).wait_recv()
        for cp in cps:
            cp.wait_send()

    return list(pl.pallas_call(
        body, name="rs_chips", in_specs=[ANY] * n, out_specs=[ANY] * n,
        out_shape=[jax.ShapeDtypeStruct(s.shape, s.dtype) for s in ss],
        scratch_shapes=[pltpu.SemaphoreType.DMA((n, 3)), pltpu.SemaphoreType.DMA((n, 3))],
    )(*ss))


def _rs_join(fs, axes):
    n = len(fs)

    def whole(f, axis):
        dims = list(f.shape)
        dims[axis] *= 2
        return tuple(dims)

    def body(*refs):
        ins, outs = refs[:n], refs[n:2 * n]
        send_sems, recv_sems = refs[2 * n:]
        x, y, c = _mesh_pos()
        cps = []
        for a in range(n):
            h = ins[a].shape[axes[a]]
            mine = (pl.ds(c * h, h), slice(None)) if axes[a] == 0 else (slice(None), pl.ds(c * h, h))
            cp = _remote(ins[a], outs[a].at[mine], send_sems.at[a], recv_sems.at[a], (x, y, 1 - c))
            cp.start()
            cps.append(cp)
        for cp in cps:
            cp.wait()

    outs = pl.pallas_call(
        body, name="rs_join", in_specs=[ANY] * n, out_specs=[ANY] * n,
        out_shape=[jax.ShapeDtypeStruct(whole(f, ax), f.dtype) for f, ax in zip(fs, axes)],
        scratch_shapes=[pltpu.SemaphoreType.DMA((n,)), pltpu.SemaphoreType.DMA((n,))],
    )(*fs)
    c = lax.axis_index("c")
    return [lax.dynamic_update_slice_in_dim(o, f, c * f.shape[ax], ax) for o, f, ax in zip(outs, fs, axes)]


def _gather_exchange(xs):
    def start(cin, cout, send_sems, recv_sems):
        x, y, c = _mesh_pos()
        me = 2 * x + y
        for a, (src, dst) in enumerate(zip(cin, cout)):
            mine = _half_index(src.shape, c)
            for j, chip in enumerate(_other_chips(x, y)):
                _remote(src.at[mine], dst.at[(me,) + mine], send_sems.at[a, j], recv_sems.at[a, j], (*chip, c)).start()
            _remote(src, dst.at[me], send_sems.at[a, 3], recv_sems.at[a, 3], (x, y, 1 - c)).start()

    def finish(cin, cout, send_sems, recv_sems):
        x, y, c = _mesh_pos()
        for a, dst in enumerate(cout):
            for j, chip in enumerate(_other_chips(x, y)):
                landed = dst.at[(2 * chip[0] + chip[1],) + _half_index(dst.shape[1:], c)]
                _remote(landed, landed, send_sems.at[a, j], recv_sems.at[a, j], (x, y, c)).wait()
            own = dst.at[2 * x + y]
            _remote(own, own, send_sems.at[a, 3], recv_sems.at[a, 3], (x, y, c)).wait()

    return _Exchange(tuple(xs), tuple(jax.ShapeDtypeStruct((N_SHARDS,) + a.shape, a.dtype) for a in xs), start, finish)


def _gather_forward(gs):
    n = len(gs)

    def body(*refs):
        outs = refs[n:2 * n]
        send_sems, recv_sems = refs[2 * n:]
        x, y, c = _mesh_pos()
        chips = _other_chips(x, y)
        cps = []
        for a in range(n):
            for j, chip in enumerate(chips):
                landed = outs[a].at[(2 * chip[0] + chip[1],) + _half_index(outs[a].shape[1:], c)]
                cp = _remote(landed, landed, send_sems.at[a, j], recv_sems.at[a, j], (x, y, 1 - c))
                cp.start()
                cps.append(cp)
        for a in range(n):
            for j, chip in enumerate(chips):
                other = outs[a].at[(2 * chip[0] + chip[1],) + _half_index(outs[a].shape[1:], 1 - c)]
                _remote(other, other, send_sems.at[a, j], recv_sems.at[a, j], (x, y, c)).wait_recv()
        for cp in cps:
            cp.wait_send()

    return list(pl.pallas_call(
        body, name="gather_forward", in_specs=[ANY] * n, out_specs=[ANY] * n,
        out_shape=[jax.ShapeDtypeStruct(g.shape, g.dtype) for g in gs],
        input_output_aliases={i: i for i in range(n)},
        scratch_shapes=[pltpu.SemaphoreType.DMA((n, 3)), pltpu.SemaphoreType.DMA((n, 3))],
    )(*gs))


def _scatter_exchange(ss):
    def start(cin, cout, send_sems, recv_sems):
        x, y, c = _mesh_pos()
        me = 2 * x + y
        for a, (src, dst) in enumerate(zip(cin, cout)):
            for j, chip in enumerate(_other_chips(x, y)):
                _remote(src.at[2 * chip[0] + chip[1]], dst.at[me], send_sems.at[a, j], recv_sems.at[a, j],
                        (*chip, c)).start()

    def finish(cin, cout, send_sems, recv_sems):
        x, y, c = _mesh_pos()
        for a, dst in enumerate(cout):
            for j, chip in enumerate(_other_chips(x, y)):
                slot = dst.at[2 * chip[0] + chip[1]]
                _remote(slot, slot, send_sems.at[a, j], recv_sems.at[a, j], (x, y, c)).wait()

    return _Exchange(tuple(ss), tuple(jax.ShapeDtypeStruct(s.shape, s.dtype) for s in ss), start, finish)


def _add_pairs(gs, lands, name):
    n = len(gs)

    def body(*refs):
        c = lax.axis_index("c")
        for g_ref, l_ref, o_ref in zip(refs[:n], refs[n:2 * n], refs[2 * n:]):
            mine = g_ref[(0,) + _half_index(g_ref.shape[1:], c)]
            o_ref[0] = (mine.astype(F32) + l_ref[0].astype(F32)).astype(BF16)

    whole = [pl.BlockSpec((1,) + g.shape[1:], lambda j: (j, 0, 0)) for g in gs]
    half = [pl.BlockSpec((1,) + l.shape[1:], lambda j: (j, 0, 0)) for l in lands]
    return list(pl.pallas_call(body, name=name, grid=(gs[0].shape[0],), in_specs=whole + half, out_specs=half,
                               out_shape=[jax.ShapeDtypeStruct(l.shape, BF16) for l in lands],
                               compiler_params=_params(("arbitrary",), VMEM_LIMIT))(*gs, *lands))


def _sum_slots(slots, owns, name):
    n = len(slots)

    def body(*refs):
        me = 2 * lax.axis_index("x") + lax.axis_index("y")
        for s_ref, o_ref, out_ref in zip(refs[:n], refs[n:2 * n], refs[2 * n:]):
            acc = jnp.zeros(out_ref.shape, F32)
            for s in range(N_SHARDS):
                acc = acc + jnp.where(me == s, o_ref[s], s_ref[s]).astype(F32)
            out_ref[...] = acc

    def specs(a):
        _, h, c = a.shape
        if h % 32 == 0:
            return (pl.BlockSpec((N_SHARDS, h // 2, c), lambda i: (0, i, 0)), pl.BlockSpec((h // 2, c), lambda i: (i, 0)))
        return (pl.BlockSpec((N_SHARDS, h, c // 2), lambda i: (0, 0, i)), pl.BlockSpec((h, c // 2), lambda i: (0, i)))

    in_specs = [specs(a)[0] for a in slots]
    return list(pl.pallas_call(body, name=name, grid=(2,), in_specs=in_specs + in_specs,
                               out_specs=[specs(a)[1] for a in slots],
                               out_shape=[jax.ShapeDtypeStruct(a.shape[1:], F32) for a in slots],
                               compiler_params=_params(("arbitrary",), VMEM_LIMIT))(*slots, *owns))


def _all_reduce_small(p):
    r = p.shape[0]

    def body(p_ref, o_ref, buf, send_sems, recv_sems):
        x, y, c = _mesh_pos()
        me = 4 * x + 2 * y + c
        buf[me] = p_ref[...]
        cps = []
        k = 0
        for fx in range(2):
            for fy in range(2):
                for fc in range(2):
                    if fx + fy + fc == 0:
                        continue
                    peer = (1 - x if fx else x, 1 - y if fy else y, 1 - c if fc else c)
                    peer_id = 4 * peer[0] + 2 * peer[1] + peer[2]
                    cp = _remote(p_ref, buf.at[me], send_sems.at[k], recv_sems.at[k], peer)
                    cp.start()
                    cps.append((cp, peer_id, k))
                    k += 1
        for cp, peer_id, k in cps:
            _remote(p_ref, buf.at[peer_id], send_sems.at[k], recv_sems.at[k], (x, y, c)).wait_recv()
        for cp, _, _ in cps:
            cp.wait_send()
        acc = buf[0]
        for s in range(1, 8):
            acc = acc + buf[s]
        o_ref[...] = acc

    vm = pl.BlockSpec(memory_space=pltpu.VMEM)
    return pl.pallas_call(
        body, name="all_reduce_small", in_specs=[vm], out_specs=vm,
        out_shape=jax.ShapeDtypeStruct(p.shape, F32),
        scratch_shapes=[pltpu.VMEM((8, r, LANES), F32), pltpu.SemaphoreType.DMA((7,)), pltpu.SemaphoreType.DMA((7,))],
    )(p)


def _adamw(params, name, steps):
    c1 = 1.0 / (1.0 - ADAM_B1 ** ADAM_STEP)
    c2 = 1.0 / (1.0 - ADAM_B2 ** ADAM_STEP)
    n = len(params)

    def body(*refs):
        for a in range(n):
            w_ref, g_ref, m_ref, v_ref = refs[4 * a:4 * a + 4]
            d_ref, nm_ref, nv_ref = refs[4 * n + 3 * a:4 * n + 3 * a + 3]
            gv = g_ref[...]
            nm = ADAM_B1 * m_ref[...] + (1.0 - ADAM_B1) * gv
            nv = ADAM_B2 * v_ref[...] + (1.0 - ADAM_B2) * (gv * gv)
            d_ref[...] = -ADAM_LR * ((nm * c1) / (jnp.sqrt(nv * c2) + ADAM_EPS) + ADAM_WD * w_ref[...])
            nm_ref[...] = nm
            nv_ref[...] = nv

    def spec(shape):
        r, c = shape
        if r % (8 * steps) == 0:
            return pl.BlockSpec((r // steps, c), lambda i: (i, 0))
        assert c % (LANES * steps) == 0
        return pl.BlockSpec((r, c // steps), lambda i: (0, i))

    specs = [spec(w.shape) for w, _, _, _ in params]
    res = pl.pallas_call(
        body, name=name, grid=(steps,),
        in_specs=[s for s in specs for _ in range(4)], out_specs=[s for s in specs for _ in range(3)],
        out_shape=[jax.ShapeDtypeStruct(w.shape, F32) for w, _, _, _ in params for _ in range(3)],
        compiler_params=_params(("arbitrary",), VMEM_LIMIT))(*[a for p4 in params for a in p4])
    return [tuple(res[3 * a:3 * a + 3]) for a in range(n)]


PACK_UNIT = 8 * LANES


def _pack(arrs):
    parts = []
    for a in arrs:
        f = a.reshape(-1).astype(F32)
        parts.append(jnp.pad(f, (0, (-f.shape[0]) % PACK_UNIT)).reshape(-1, LANES))
    return jnp.concatenate(parts, axis=0)


def _unpack(m, shapes):
    outs, row = [], 0
    for s in shapes:
        n = int(np.prod(s))
        rows = -(-n // PACK_UNIT) * 8
        outs.append(m[row:row + rows].reshape(-1)[:n].reshape(s))
        row += rows
    return outs


WEIGHTS = ["ffn1_norm", "ffn1_w_gate", "ffn1_w_up", "ffn1_w_down", "mix_norm", "w_in", "conv_w", "a_log", "dt_bias",
           "gdn_norm_w", "q_norm_w", "k_norm_w", "rel_bias", "w_out", "ffn2_norm", "ffn2_w_gate", "ffn2_w_up",
           "ffn2_w_down", "final_norm"]
BIG = ["ffn1_w_gate", "ffn1_w_up", "ffn1_w_down", "w_in", "w_out", "ffn2_w_gate", "ffn2_w_up", "ffn2_w_down"]
SMALL = [n for n in WEIGHTS if n not in BIG]
COL_SHARDED = ["ffn1_w_gate", "ffn1_w_up", "w_in", "ffn2_w_gate", "ffn2_w_up"]
N_IN_COLS = 3600
TM = 256
TE = 512
ADAM_PIECES = 8
TK = 2048


def kernel(x, ffn1_norm, ffn1_w_gate, ffn1_w_up, ffn1_w_down, mix_norm, w_in, conv_w, a_log, dt_bias, gdn_norm_w, q_norm_w, k_norm_w, rel_bias, w_out, ffn2_norm, ffn2_w_gate, ffn2_w_up, ffn2_w_down, final_norm, loss_target, m_ffn1_norm, m_ffn1_w_gate, m_ffn1_w_up, m_ffn1_w_down, m_mix_norm, m_w_in, m_conv_w, m_a_log, m_dt_bias, m_gdn_norm_w, m_q_norm_w, m_k_norm_w, m_rel_bias, m_w_out, m_ffn2_norm, m_ffn2_w_gate, m_ffn2_w_up, m_ffn2_w_down, m_final_norm, v_ffn1_norm, v_ffn1_w_gate, v_ffn1_w_up, v_ffn1_w_down, v_mix_norm, v_w_in, v_conv_w, v_a_log, v_dt_bias, v_gdn_norm_w, v_q_norm_w, v_k_norm_w, v_rel_bias, v_w_out, v_ffn2_norm, v_ffn2_w_gate, v_ffn2_w_up, v_ffn2_w_down, v_final_norm):
    p = dict(locals())
    xs, target = x[0], loss_target[0]
    t, d = xs.shape
    nc = t // CHUNK
    tk = min(TK, t)
    tkf = min(2 * TK, t)
    me = 2 * lax.axis_index("x") + lax.axis_index("y")

    first = ["ffn1_w_gate", "ffn1_w_up", "ffn1_w_down"]
    later = [n for n in BIG if n not in first] + ["conv_w"]
    local = lambda n, a: a[0].T if n in COL_SHARDED else a[0]
    shards = {n: local(n, p[n]).astype(BF16) for n in BIG}
    shards["conv_w"] = conv_w[0]
    gw = dict(zip(first, _all_gather([shards[n] for n in first])))
    f1 = (gw["ffn1_w_gate"], gw["ffn1_w_up"], gw["ffn1_w_down"])
    (x1, xn1, g1, u1), landed = _ffn_fwd(xs, ffn1_norm, *f1, TE, "ffn1_fwd",
                                         exchange=_gather_exchange([shards[n] for n in later]))
    gw.update(zip(later, _gather_forward(landed)))
    wp = gw["w_in"].reshape(N_IN_COLS, d)
    w_out_full = gw["w_out"].reshape(d, d)
    conv_rows = conv_w.shape[1]
    cw = jnp.pad(gw["conv_w"].reshape(N_SHARDS * conv_rows, CONV_TAPS).T, ((0, 8 - CONV_TAPS), (0, 0)))
    gp = jnp.pad(jnp.stack([a_log.reshape(8), dt_bias.reshape(8)]), ((0, 6), (0, LANES - 8)))
    gdn_w = gdn_norm_w.reshape(1, GDN_DIM)
    qw_t = jnp.tile(q_norm_w.reshape(1, SWA_DIM), (1, SWA_HEADS))
    kw_t = jnp.tile(k_norm_w.reshape(1, SWA_DIM), (1, SWA_HEADS))
    bd = jnp.asarray(np.kron(np.eye(2), np.full((SWA_DIM, SWA_DIM), 1.0 / SWA_DIM)), BF16)
    f2 = (gw["ffn2_w_gate"], gw["ffn2_w_up"], gw["ffn2_w_down"])

    hn, qkva, z, ab, qkvb = _mix_in_fwd(x1, mix_norm, wp, TE)
    qkvc, gb = _gdn_prep_fwd(qkva, cw, ab, gp, TE)
    gbt = jnp.transpose(gb[:, :16].reshape(nc, CHUNK, 16), (0, 2, 1))
    o_f, o_b, gdn_saved = _gdn_fwd(qkvc, gb, gbt)
    oa = _gdn_post_fwd(o_f, o_b, z, gdn_w, TE)
    o_swa, o_swa16, swa_saved = _swa_branch_fwd(qkvb, qw_t, kw_t, rel_bias, bd, TE)
    x2 = _mix_out_fwd(x1, oa, o_swa, w_out_full, TE)
    (dx3, xn2, g2, u2, loss_part, d_final), _ = _ffn_fwd(x2, ffn2_norm, *f2, TE, "ffn2_fwd", head=(final_norm, target))

    def pair_sums(partials, tag):
        return _add_pairs(partials, _rs_pair(partials), f"rs_add_{tag}")

    (dx2, dyh2, dg2, du2, h2, d_nw2), _ = _ffn_bwd_dx(dx3, x2, ffn2_norm, g2, u2, *f2, TM, "ffn2_bwd_dx")
    dwg2 = _matmul_tn(dg2, xn2, tkf, "ffn2_dwg")
    dwu2 = _matmul_tn(du2, xn2, tkf, "ffn2_dwu")
    dwd2 = _matmul_tn(h2, dyh2, tkf, "ffn2_dwd")
    (doa, dob, dx2b), lands_f2 = _mix_out_bwd(dx2, w_out_full, TE, exchange=_pair_exchange([dwg2, dwu2, dwd2]))
    sums_f2 = _add_pairs([dwg2, dwu2, dwd2], lands_f2, "rs_add_a")
    dwo = jnp.concatenate([_matmul_tn(oa, dx2b, tk, "w_out_dw_a")[0], _matmul_tn(o_swa16, dx2b, tk, "w_out_dw_b")[0]],
                          axis=0).reshape(N_SHARDS, d // N_SHARDS, d)
    do_g, dz, d_gdnw = _gdn_post_bwd(doa, o_f, o_b, z, gdn_w, TE)
    (dqkvc, dgates), slots_f2 = _gdn_bwd(qkvc, gb, gbt, do_g, gdn_saved, exchange=_scatter_exchange(sums_f2))
    dqkva, dab, dcw, dgp = _gdn_prep_bwd(qkva, cw, ab, gp, dqkvc, dgates, TM)
    dqkvb, d_qw, d_kw, d_rel = _swa_branch_bwd(dob, o_swa, swa_saved, qkvb, qw_t, kw_t, bd, TE)
    dpieces = (dqkva, dz, dab, dqkvb)
    dwp = [_matmul_tn(dp, hn, tk, f"w_in_dw_{i}")[0] for i, dp in enumerate(dpieces)]
    dw_in = jnp.concatenate([dwp[0], dwp[1], dwp[2][:N_GATE_COLS], dwp[3]], axis=0)
    dw_in = dw_in.reshape(N_SHARDS, N_IN_COLS // N_SHARDS, d)
    sums_mix = pair_sums([dw_in, dwo], "b")
    (dx1, d_mixnw), slots_mix = _mix_in_bwd_dx(dx2, x1, mix_norm, dpieces, wp, TE, exchange=_scatter_exchange(sums_mix))
    (gx, dyh1, dg1, du1, h1, d_nw1), _ = _ffn_bwd_dx(dx1, xs, ffn1_norm, g1, u1, *f1, TM, "ffn1_bwd_dx")
    dwg1 = _matmul_tn(dg1, xn1, tkf, "ffn1_dwg")
    dwu1 = _matmul_tn(du1, xn1, tkf, "ffn1_dwu")
    sums_gu = pair_sums([dwg1, dwu1], "c")
    dwd1, slots_gu = _matmul_tn(h1, dyh1, tkf, "ffn1_dwd", exchange=_scatter_exchange(sums_gu))
    sums_d = pair_sums([dwd1], "d")
    slots = slots_gu + _rs_chips(sums_d) + slots_mix + slots_f2
    sums = sums_gu + sums_d + sums_mix + sums_f2
    halves = _sum_slots(slots[:4], sums[:4], "rs_sum_a") + _sum_slots(slots[4:], sums[4:], "rs_sum_b")
    g_big = dict(zip(BIG, _rs_join(halves, [_split_axis(shards[n].shape) for n in BIG])))

    small_partial = {"ffn1_norm": d_nw1, "mix_norm": d_mixnw, "a_log": dgp[0, 0:8], "dt_bias": dgp[1, 0:8],
                     "gdn_norm_w": d_gdnw, "q_norm_w": d_qw, "k_norm_w": d_kw, "rel_bias": d_rel,
                     "ffn2_norm": d_nw2, "final_norm": d_final, "conv_w": dcw[0:CONV_TAPS].T}
    red = _all_reduce_small(_pack([small_partial[n] for n in SMALL] + [loss_part[0, 0:1]]))
    full_shapes = [p[n].shape if n != "conv_w" else (N_SHARDS * conv_rows, CONV_TAPS) for n in SMALL]
    red_parts = _unpack(red, full_shapes + [(1,)])
    loss = red_parts[-1].reshape(())
    g_small = dict(zip(SMALL, red_parts[:-1]))
    g_small["conv_w"] = lax.dynamic_slice_in_dim(g_small["conv_w"], me * conv_rows, conv_rows, 0).reshape(conv_w.shape)

    grads, deltas, new_m, new_v = {}, {}, {}, {}
    quad = lambda n: (local(n, p[n]), g_big[n], local(n, p["m_" + n]), local(n, p["v_" + n]))
    updates = (_adamw([quad(n) for n in BIG[:4]], "adamw_a", ADAM_PIECES)
               + _adamw([quad(n) for n in BIG[4:]], "adamw_b", ADAM_PIECES))
    for n, (dl, nm, nv) in zip(BIG, updates):
        back = (lambda a: a.T[None]) if n in COL_SHARDED else (lambda a: a[None])
        grads[n], deltas[n], new_m[n], new_v[n] = back(g_big[n]), back(dl), back(nm), back(nv)
    packed = [_pack([src[n] for n in SMALL]) for src in
              ({n: p[n] for n in SMALL}, g_small, {n: p["m_" + n] for n in SMALL}, {n: p["v_" + n] for n in SMALL})]
    small_shapes = [p[n].shape for n in SMALL]
    for dst, arr in zip((deltas, new_m, new_v), _adamw([tuple(packed)], "adamw_small", 1)[0]):
        dst.update(zip(SMALL, _unpack(arr, small_shapes)))
    grads.update(g_small)

    return (loss, gx[None], *[grads[n] for n in WEIGHTS], *[deltas[n] for n in WEIGHTS],
            *[new_m[n] for n in WEIGHTS], *[new_v[n] for n in WEIGHTS])
```

```python
import math
from typing import Callable, NamedTuple

import numpy as np
import jax
import jax.numpy as jnp
from jax import lax
from jax.experimental import pallas as pl
from jax.experimental.pallas import tpu as pltpu

F32 = jnp.float32
BF16 = jnp.bfloat16
MESH = pl.DeviceIdType.MESH

EPS = 1e-6
NEG_BIG = -1e30
GDN_HEADS = 4
GDN_DIM = 128
CHUNK = 64
SWA_HEADS = 8
SWA_DIM = 64
PATTERNS = ((128, 1), (512, 4), (2048, 16))
RADIUS = 64
REL_BUCKETS = 32
REL_MAX_DISTANCE = 1024
CONV_TAPS = 5
N_SHARDS = 4
LANES = 128
VMEM_LIMIT = 56 * 1024 * 1024

ADAM_LR, ADAM_B1, ADAM_B2, ADAM_EPS, ADAM_WD, ADAM_STEP = 0.001, 0.9, 0.999, 1e-08, 0.01, 10


def _params(sem=None, vmem=None):
    return pltpu.CompilerParams(dimension_semantics=sem, vmem_limit_bytes=vmem)


def _resident(shape):
    nd = len(shape)
    return pl.BlockSpec(shape, lambda *_: (0,) * nd, pipeline_mode=pl.Buffered(1))


ANY = pl.BlockSpec(memory_space=pl.ANY)


class _Exchange(NamedTuple):
    arrays: tuple
    out_shape: tuple
    start: Callable
    finish: Callable


def _grid_call(body, name, nsteps, in_specs, out_specs, out_shape, operands, scratch=(), exchange=None):
    params = _params(("arbitrary",), VMEM_LIMIT)
    if exchange is None:
        res = pl.pallas_call(body, name=name, grid=(nsteps,), in_specs=list(in_specs), out_specs=list(out_specs),
                             out_shape=list(out_shape), scratch_shapes=list(scratch), compiler_params=params)(*operands)
        return list(res), []
    n_in, n_out, k, n_scr = len(in_specs), len(out_specs), len(exchange.arrays), len(scratch)

    def wrapped(*refs):
        ins, cin = refs[:n_in], refs[n_in:n_in + k]
        outs, cout = refs[n_in + k:n_in + k + n_out], refs[n_in + k + n_out:n_in + 2 * k + n_out]
        rest = refs[n_in + 2 * k + n_out:]
        scr, (send_sems, recv_sems) = rest[:n_scr], rest[n_scr:]

        @pl.when(pl.program_id(0) == 0)
        def _():
            exchange.start(cin, cout, send_sems, recv_sems)

        body(*ins, *outs, *scr)

        @pl.when(pl.program_id(0) == nsteps - 1)
        def _():
            exchange.finish(cin, cout, send_sems, recv_sems)

    res = pl.pallas_call(
        wrapped, name=name, grid=(nsteps,), in_specs=list(in_specs) + [ANY] * k, out_specs=list(out_specs) + [ANY] * k,
        out_shape=list(out_shape) + list(exchange.out_shape),
        scratch_shapes=list(scratch) + [pltpu.SemaphoreType.DMA((k, 4)), pltpu.SemaphoreType.DMA((k, 4))],
        compiler_params=params)(*operands, *exchange.arrays)
    return list(res[:n_out]), list(res[n_out:])


def _dot(a, b):
    return jnp.dot(a.astype(BF16), b.astype(BF16), preferred_element_type=F32)


def _dot_nt(a, b):
    return lax.dot_general(a.astype(BF16), b.astype(BF16), (((1,), (1,)), ((), ())), preferred_element_type=F32)


def _dot_tn(a, b):
    return lax.dot_general(a.astype(BF16), b.astype(BF16), (((0,), (0,)), ((), ())), preferred_element_type=F32)


def _sigmoid(x):
    return 1.0 / (1.0 + jnp.exp(-x))


def _rstd(xf):
    return lax.rsqrt(jnp.mean(xf * xf, axis=-1, keepdims=True) + EPS)


def _rms_bwd(xf, r, nw, dxn):
    xhat = xf * r
    dxh = dxn * nw
    dx = r * (dxh - xhat * jnp.mean(dxh * xhat, axis=-1, keepdims=True))
    return dx, jnp.sum(dxn * xhat, axis=0, keepdims=True)


def _ffn_fwd(x, nw, wg, wu, wd, tm, name, exchange=None, head=None):
    t, d = x.shape
    nj, fs, _ = wg.shape

    def body(x_ref, nw_ref, wg_ref, wu_ref, wd_ref, *rest):
        if head is None:
            y_ref, xn_ref, g_ref, u_ref = rest
        else:
            fw_ref, t_ref, y_ref, xn_ref, g_ref, u_ref, loss_ref, dfw_ref = rest

            @pl.when(pl.program_id(0) == 0)
            def _():
                loss_ref[...] = jnp.zeros_like(loss_ref)
                dfw_ref[...] = jnp.zeros_like(dfw_ref)

        xf = x_ref[...]
        xn = (xf * _rstd(xf) * nw_ref[...]).astype(BF16)
        xn_ref[...] = xn
        acc = jnp.zeros((tm, d), F32)
        for j in range(nj):
            g = _dot_nt(xn, wg_ref[j])
            u = _dot_nt(xn, wu_ref[j])
            h = (g * _sigmoid(g) * u).astype(BF16)
            acc = acc + jnp.dot(h, wd_ref[j], preferred_element_type=F32)
            g_ref[j] = g.astype(BF16)
            u_ref[j] = u.astype(BF16)
        y = xf + 0.5 * acc
        if head is None:
            y_ref[...] = y
        else:
            r = _rstd(y)
            err = y * r * fw_ref[...] - t_ref[...]
            loss_ref[...] += 0.5 * jnp.sum(jnp.mean(err * err, axis=-1, keepdims=True), axis=0, keepdims=True)
            dy, dfw = _rms_bwd(y, r, fw_ref[...], err * (1.0 / d))
            y_ref[...] = dy
            dfw_ref[...] += dfw

    row = pl.BlockSpec((tm, d), lambda i: (i, 0))
    act = pl.BlockSpec((nj, tm, fs), lambda i: (0, i, 0))
    in_specs = [row, _resident((1, d)), _resident(wg.shape), _resident(wu.shape), _resident(wd.shape)]
    out_specs = [row, row, act, act]
    out_shape = [jax.ShapeDtypeStruct((t, d), F32), jax.ShapeDtypeStruct((t, d), BF16),
                 jax.ShapeDtypeStruct((nj, t, fs), BF16), jax.ShapeDtypeStruct((nj, t, fs), BF16)]
    operands = (x, nw, wg, wu, wd)
    if head is not None:
        in_specs += [_resident((1, d)), row]
        out_specs += [pl.BlockSpec((1, LANES), lambda i: (0, 0)), pl.BlockSpec((1, d), lambda i: (0, 0))]
        out_shape += [jax.ShapeDtypeStruct((1, LANES), F32), jax.ShapeDtypeStruct((1, d), F32)]
        operands += tuple(head)
    return _grid_call(body, name, t // tm, in_specs, out_specs, out_shape, operands, exchange=exchange)


def _ffn_bwd_dx(dy, x, nw, g, u, wg, wu, wd, tm, name, exchange=None):
    t, d = x.shape
    nj, fs, _ = wg.shape

    def body(dy_ref, x_ref, nw_ref, g_ref, u_ref, wg_ref, wu_ref, wd_ref,
             dx_ref, dyh_ref, dg_ref, du_ref, h_ref, dnw_ref):
        @pl.when(pl.program_id(0) == 0)
        def _():
            dnw_ref[...] = jnp.zeros_like(dnw_ref)

        dyv = dy_ref[...]
        dyh = (0.5 * dyv).astype(BF16)
        dyh_ref[...] = dyh
        dxn = jnp.zeros((tm, d), F32)
        dh_next = _dot_nt(dyh, wd_ref[0])
        for j in range(nj):
            dh = dh_next
            gv = g_ref[j].astype(F32)
            uv = u_ref[j].astype(F32)
            sg = _sigmoid(gv)
            si = gv * sg
            dg = (dh * uv * (sg * (1.0 + gv * (1.0 - sg)))).astype(BF16)
            du = (dh * si).astype(BF16)
            if j + 1 < nj:
                dh_next = _dot_nt(dyh, wd_ref[j + 1])
            h_ref[j] = (si * uv).astype(BF16)
            dg_ref[j] = dg
            du_ref[j] = du
            dxn = dxn + _dot(dg, wg_ref[j]) + _dot(du, wu_ref[j])
        xf = x_ref[...]
        dxr, dnw = _rms_bwd(xf, _rstd(xf), nw_ref[...], dxn)
        dx_ref[...] = dyv + dxr
        dnw_ref[...] += dnw

    row = pl.BlockSpec((tm, d), lambda i: (i, 0))
    act = pl.BlockSpec((nj, tm, fs), lambda i: (0, i, 0))
    act_shape = jax.ShapeDtypeStruct((nj, t, fs), BF16)
    return _grid_call(
        body, name, t // tm,
        [row, row, _resident((1, d)), act, act, _resident(wg.shape), _resident(wu.shape), _resident(wd.shape)],
        [row, row, act, act, act, pl.BlockSpec((1, d), lambda i: (0, 0))],
        [jax.ShapeDtypeStruct((t, d), F32), jax.ShapeDtypeStruct((t, d), BF16),
         act_shape, act_shape, act_shape, jax.ShapeDtypeStruct((1, d), F32)],
        (dy, x, nw, g, u, wg, wu, wd), exchange=exchange)


def _matmul_tn(a, b, tk, name, exchange=None):
    a3, b3 = a.ndim == 3, b.ndim == 3
    nj = a.shape[0] if a3 else (b.shape[0] if b3 else 1)
    t, m = a.shape[-2:]
    n = b.shape[-1]
    nt = t // tk

    def body(a_ref, b_ref, o_ref, acc_ref):
        k = pl.program_id(0) % nt

        @pl.when(k == 0)
        def _():
            acc_ref[...] = jnp.zeros_like(acc_ref)

        acc_ref[...] += lax.dot_general(a_ref[...], b_ref[...], (((0,), (0,)), ((), ())),
                                        preferred_element_type=F32)

        @pl.when(k == nt - 1)
        def _():
            o_ref[...] = acc_ref[...].astype(o_ref.dtype)

    a_spec = (pl.BlockSpec((None, tk, m), lambda i: (i // nt, i % nt, 0)) if a3
              else pl.BlockSpec((tk, m), lambda i: (i % nt, 0)))
    b_spec = (pl.BlockSpec((None, tk, n), lambda i: (i // nt, i % nt, 0)) if b3
              else pl.BlockSpec((tk, n), lambda i: (i % nt, 0)))
    (out,), landed = _grid_call(
        body, name, nj * nt, [a_spec, b_spec], [pl.BlockSpec((None, m, n), lambda i: (i // nt, 0, 0))],
        [jax.ShapeDtypeStruct((nj, m, n), BF16)], (a, b), scratch=[pltpu.VMEM((m, n), F32)], exchange=exchange)
    return out if exchange is None else (out, landed)


N_GATE_COLS = 4 * GDN_HEADS
P_QKVA, P_Z, P_AB, P_QKVB = (0, 1536), (1536, 2048), (2048, 2048 + LANES), (2048 + N_GATE_COLS, 3600)
P_PIECES = (P_QKVA, P_Z, P_AB, P_QKVB)


def _mix_in_fwd(x1, nw, wp, tm):
    t, d = x1.shape

    def body(x_ref, nw_ref, w_ref, hn_ref, *outs):
        xf = x_ref[...]
        xn = (xf * _rstd(xf) * nw_ref[...]).astype(BF16)
        hn_ref[...] = xn
        for (a, b), o_ref in zip(P_PIECES, outs):
            o_ref[...] = _dot_nt(xn, w_ref[a:b, :])

    row = pl.BlockSpec((tm, d), lambda i: (i, 0))
    return pl.pallas_call(
        body, name="mix_in_fwd", grid=(t // tm,),
        in_specs=[row, _resident((1, d)), _resident(wp.shape)],
        out_specs=[row] + [pl.BlockSpec((tm, b - a), lambda i: (i, 0)) for a, b in P_PIECES],
        out_shape=[jax.ShapeDtypeStruct((t, d), BF16)]
                  + [jax.ShapeDtypeStruct((t, b - a), F32) for a, b in P_PIECES],
        compiler_params=_params(("arbitrary",), VMEM_LIMIT),
    )(x1, nw, wp)


def _mix_in_bwd_dx(dx, x1, nw, dpieces, wp, tm, exchange=None):
    t, d = x1.shape

    def body(dx_ref, x_ref, nw_ref, p0, p1, p2, p3, w_ref, o_ref, dnw_ref):
        @pl.when(pl.program_id(0) == 0)
        def _():
            dnw_ref[...] = jnp.zeros_like(dnw_ref)

        dh = jnp.zeros((tm, d), F32)
        for (a, b), p_ref in zip(P_PIECES, (p0, p1, p2, p3)):
            dh = dh + _dot(p_ref[...], w_ref[a:b, :])
        xf = x_ref[...]
        dxr, dnw = _rms_bwd(xf, _rstd(xf), nw_ref[...], dh)
        o_ref[...] = dx_ref[...] + dxr
        dnw_ref[...] += dnw

    row = pl.BlockSpec((tm, d), lambda i: (i, 0))
    return _grid_call(
        body, "mix_in_bwd_dx", t // tm,
        [row, row, _resident((1, d))]
        + [pl.BlockSpec((tm, b - a), lambda i: (i, 0)) for a, b in P_PIECES] + [_resident(wp.shape)],
        [row, pl.BlockSpec((1, d), lambda i: (0, 0))],
        [jax.ShapeDtypeStruct((t, d), F32), jax.ShapeDtypeStruct((1, d), F32)],
        (dx, x1, nw, *dpieces, wp), exchange=exchange)


def _mix_out_fwd(x1, oa, ob, w, tm):
    t, d = x1.shape
    half = oa.shape[1]

    def body(x_ref, oa_ref, ob_ref, w_ref, o_ref):
        o_ref[...] = (x_ref[...] + _dot(oa_ref[...], w_ref[0:half, :]) + _dot(ob_ref[...], w_ref[half:2 * half, :]))

    row = pl.BlockSpec((tm, d), lambda i: (i, 0))
    hrow = pl.BlockSpec((tm, half), lambda i: (i, 0))
    return pl.pallas_call(
        body, name="mix_out_fwd", grid=(t // tm,),
        in_specs=[row, hrow, hrow, _resident(w.shape)],
        out_specs=row, out_shape=jax.ShapeDtypeStruct((t, d), F32),
        compiler_params=_params(("arbitrary",), VMEM_LIMIT),
    )(x1, oa, ob, w)


def _mix_out_bwd(dx2, w, tm, exchange=None):
    t, d = dx2.shape
    half = w.shape[0] // 2

    def body(dx_ref, w_ref, doa_ref, dob_ref, dxb_ref):
        dxb = dx_ref[...].astype(BF16)
        dxb_ref[...] = dxb
        doa_ref[...] = _dot_nt(dxb, w_ref[0:half, :])
        dob_ref[...] = _dot_nt(dxb, w_ref[half:2 * half, :])

    row = pl.BlockSpec((tm, d), lambda i: (i, 0))
    hrow = pl.BlockSpec((tm, half), lambda i: (i, 0))
    return _grid_call(
        body, "mix_out_bwd", t // tm, [row, _resident(w.shape)], [hrow, hrow, row],
        [jax.ShapeDtypeStruct((t, half), F32), jax.ShapeDtypeStruct((t, half), F32), jax.ShapeDtypeStruct((t, d), BF16)],
        (dx2, w), exchange=exchange)


HALO = 8


def _halo_row_specs(tr, cols, nrow8):
    per = tr // HALO
    return [pl.BlockSpec((tr, cols), lambda i: (i, 0)),
            pl.BlockSpec((HALO, cols), lambda i: (jnp.maximum(i * per - 1, 0), 0)),
            pl.BlockSpec((HALO, cols), lambda i: (jnp.minimum((i + 1) * per, nrow8 - 1), 0))]


def _fill_window(win_ref, cb, xm, xp, xn, first, last):
    tr = xm.shape[0]
    cols = slice(cb * LANES, (cb + 1) * LANES)
    win_ref[cb, 0:HALO, :] = jnp.where(first, 0.0, xp[:, cols])
    win_ref[cb, HALO:HALO + tr, :] = xm[:, cols]
    win_ref[cb, HALO + tr:HALO + tr + HALO, :] = jnp.where(last, 0.0, xn[:, cols])


def _conv_taps(win_ref, cb, cw_ref, start, rows):
    cols = slice(cb * LANES, (cb + 1) * LANES)
    acc = None
    for j in range(CONV_TAPS):
        term = win_ref[cb, pl.ds(start + j - CONV_TAPS // 2, rows), :] * cw_ref[j:j + 1, cols]
        acc = term if acc is None else acc + term
    return acc


def _softplus(x):
    u = jnp.exp(-jnp.abs(x))
    w = 1.0 + u
    log1p = jnp.where(w == 1.0, u, jnp.log(w) * u / jnp.where(w == 1.0, 1.0, w - 1.0))
    return jnp.maximum(x, 0.0) + log1p


def _gdn_prep_fwd(qkva, cw, ab, gp, tr):
    t, c = qkva.shape
    nt = t // tr
    ncb = c // LANES

    def body(xm, xp, xn, cw_ref, ab_ref, gp_ref, o_ref, gb_ref, xw_ref):
        i = pl.program_id(0)
        first, last = i == 0, i == nt - 1
        for cb in range(ncb):
            cols = slice(cb * LANES, (cb + 1) * LANES)
            _fill_window(xw_ref, cb, xm, xp, xn, first, last)
            pre = _conv_taps(xw_ref, cb, cw_ref, HALO, tr)
            y = pre * _sigmoid(pre)
            if cb < 2 * GDN_HEADS:
                y = y * lax.rsqrt(jnp.sum(y * y, axis=-1, keepdims=True) + EPS)
            if cb < GDN_HEADS:
                y = y * (GDN_DIM ** -0.5)
            o_ref[:, cols] = y
        abv = ab_ref[...]
        lane = lax.broadcasted_iota(jnp.int32, abv.shape, 1)
        g = -jnp.exp(gp_ref[0:1, :]) * _softplus(abv + gp_ref[1:2, :])
        gb_ref[...] = jnp.where(lane < 8, g, jnp.where(lane < 16, _sigmoid(abv), 0.0))

    return pl.pallas_call(
        body, name="gdn_prep_fwd", grid=(nt,),
        in_specs=_halo_row_specs(tr, c, t // HALO)
                 + [_resident(cw.shape), pl.BlockSpec((tr, LANES), lambda i: (i, 0)), _resident(gp.shape)],
        out_specs=[pl.BlockSpec((tr, c), lambda i: (i, 0)), pl.BlockSpec((tr, LANES), lambda i: (i, 0))],
        out_shape=[jax.ShapeDtypeStruct((t, c), F32), jax.ShapeDtypeStruct((t, LANES), F32)],
        scratch_shapes=[pltpu.VMEM((ncb, tr + 2 * HALO, LANES), F32)],
        compiler_params=_params(("arbitrary",), VMEM_LIMIT),
    )(qkva, qkva, qkva, cw, ab, gp)


def _gdn_prep_bwd(qkva, cw, ab, gp, dy, dgates, tr):
    t, c = qkva.shape
    nt = t // tr
    ncb = c // LANES

    ext = HALO // 2
    rows_ext = tr + 2 * ext

    def body(xm, xp, xn, fm, fp, fn, cw_ref, ab_ref, gp_ref, gf_ref, dx_ref, dab_ref, dcw_ref, dgp_ref,
             xw_ref, dyw_ref, dp_ref):
        i = pl.program_id(0)
        first, last = i == 0, i == nt - 1

        @pl.when(first)
        def _():
            dcw_ref[...] = jnp.zeros_like(dcw_ref)
            dgp_ref[...] = jnp.zeros_like(dgp_ref)

        sub8 = lax.broadcasted_iota(jnp.int32, (8, LANES), 0)
        for cb in range(ncb):
            cols = slice(cb * LANES, (cb + 1) * LANES)
            _fill_window(xw_ref, cb, xm, xp, xn, first, last)
            _fill_window(dyw_ref, cb, fm, fp, fn, first, last)
            pre = _conv_taps(xw_ref, cb, cw_ref, HALO - ext, rows_ext)
            dyw = dyw_ref[cb, pl.ds(HALO - ext, rows_ext), :]
            sg = _sigmoid(pre)
            s = pre * sg
            if cb < 2 * GDN_HEADS:
                scale = (GDN_DIM ** -0.5) if cb < GDN_HEADS else 1.0
                r = lax.rsqrt(jnp.sum(s * s, axis=-1, keepdims=True) + EPS)
                dn = dyw * scale
                ds = r * dn - s * (r * r * r) * jnp.sum(dn * s, axis=-1, keepdims=True)
            else:
                ds = dyw
            dp_ref[cb] = ds * (sg * (1.0 + pre * (1.0 - sg)))
            dpre = dp_ref[cb, pl.ds(ext, tr), :]
            dx = None
            dcw = jnp.zeros((8, LANES), F32)
            for j in range(CONV_TAPS):
                off = j - CONV_TAPS // 2
                term = dp_ref[cb, pl.ds(ext - off, tr), :] * cw_ref[j:j + 1, cols]
                dx = term if dx is None else dx + term
                tap = jnp.sum(dpre * xw_ref[cb, pl.ds(HALO + off, tr), :], axis=0, keepdims=True)
                dcw = dcw + jnp.where(sub8 == j, tap, 0.0)
            dx_ref[:, cols] = dx.astype(BF16)
            dcw_ref[:, cols] += dcw

        abv = ab_ref[...]
        dgb = gf_ref[...]
        lane = lax.broadcasted_iota(jnp.int32, abv.shape, 1)
        nea = -jnp.exp(gp_ref[0:1, :])
        xs = abv + gp_ref[1:2, :]
        g = nea * _softplus(xs)
        beta = _sigmoid(abv)
        da = dgb * nea * _sigmoid(xs)
        dab = jnp.where(lane < 8, da, jnp.where(lane < 16, dgb * beta * (1.0 - beta), 0.0))
        dab_ref[...] = dab.astype(BF16)
        keep = lane[0:1, :] < 8
        dalog = jnp.where(keep, jnp.sum(dgb * g, axis=0, keepdims=True), 0.0)
        ddtb = jnp.where(keep, jnp.sum(da, axis=0, keepdims=True), 0.0)
        dgp_ref[...] += jnp.where(sub8 == 0, dalog, 0.0) + jnp.where(sub8 == 1, ddtb, 0.0)

    lrow = pl.BlockSpec((tr, LANES), lambda i: (i, 0))
    halo = _halo_row_specs(tr, c, t // HALO)
    return pl.pallas_call(
        body, name="gdn_prep_bwd", grid=(nt,),
        in_specs=halo + halo + [_resident(cw.shape), lrow, _resident(gp.shape), lrow],
        out_specs=[pl.BlockSpec((tr, c), lambda i: (i, 0)), lrow,
                   pl.BlockSpec(cw.shape, lambda i: (0, 0)), pl.BlockSpec(gp.shape, lambda i: (0, 0))],
        out_shape=[jax.ShapeDtypeStruct((t, c), BF16), jax.ShapeDtypeStruct((t, LANES), BF16),
                   jax.ShapeDtypeStruct(cw.shape, F32), jax.ShapeDtypeStruct(gp.shape, F32)],
        scratch_shapes=[pltpu.VMEM((ncb, tr + 2 * HALO, LANES), F32), pltpu.VMEM((ncb, tr + 2 * HALO, LANES), F32),
                        pltpu.VMEM((ncb, rows_ext, LANES), F32)],
        compiler_params=_params(("arbitrary",), VMEM_LIMIT),
    )(qkva, qkva, qkva, dy, dy, dy, cw, ab, gp, dgates)


def _chunk_masks(lower):
    ii = lax.broadcasted_iota(jnp.int32, (CHUNK, CHUNK), 0)
    jj = lax.broadcasted_iota(jnp.int32, (CHUNK, CHUNK), 1)
    incl = (ii >= jj) if lower else (ii <= jj)
    strict = (ii > jj) if lower else (ii < jj)
    return ii, jj, incl, strict


def _dot3(a, b):
    ah = a.astype(BF16)
    al = (a - ah.astype(F32)).astype(BF16)
    bh = b.astype(BF16)
    bl = (b - bh.astype(F32)).astype(BF16)
    d = lambda u, v: jnp.dot(u, v, preferred_element_type=F32)
    return d(ah, bh) + (d(ah, bl) + d(al, bh))


def _tri_inv_many(lmats, ii, jj):
    m16 = (ii // 16) == (jj // 16)
    m32 = (ii // 32) == (jj // 32)
    eye = jnp.where(ii == jj, 1.0, 0.0)
    l16 = [jnp.where(m16, l, 0.0) for l in lmats]
    p2 = [_dot3(a, a) for a in l16]
    p4 = [_dot3(a, a) for a in p2]
    p8 = [_dot3(a, a) for a in p4]
    xs = [eye - a for a in l16]
    for ps in (p2, p4, p8):
        xs = [x + _dot3(x, p) for x, p in zip(xs, ps)]
    for off in ([jnp.where(m32 & jnp.logical_not(m16), l, 0.0) for l in lmats],
                [jnp.where(m32, 0.0, l) for l in lmats]):
        ys = [_dot3(x, c) for x, c in zip(xs, off)]
        xs = [x - _dot3(y, x) for x, y in zip(xs, ys)]
    return xs


def _col_to_row(col, ii, jj):
    return jnp.sum(jnp.where(ii == jj, col, 0.0), axis=0, keepdims=True)


def _row_to_col(row, ii, jj):
    return jnp.sum(jnp.where(ii == jj, row, 0.0), axis=1, keepdims=True)


def _chain_common(q, k, v, graw_col, graw_row, bcol, masks):
    ii, jj, incl, strict = masks
    inclt = jnp.logical_not(strict)
    gcol = jnp.sum(jnp.where(incl, graw_row, 0.0), axis=1, keepdims=True)
    grow = jnp.sum(jnp.where(inclt, graw_col, 0.0), axis=0, keepdims=True)
    glast = jnp.sum(graw_row, axis=1, keepdims=True)
    decay = jnp.where(incl, jnp.exp(jnp.where(incl, gcol - grow, 0.0)), 0.0)
    kb = k * bcol
    vb = v * bcol
    eg = jnp.exp(gcol)
    ek = jnp.exp(glast - gcol)
    kbg = kb * eg
    amat = _dot_nt(kb, k)
    qk = _dot_nt(q, k)
    return dict(gcol=gcol, glast=glast, decay=decay, kb=kb, vb=vb, eg=eg, ek=ek, kbg=kbg, amat=amat, qk=qk,
                intra=qk * decay, qg=q * eg, kdec=k * ek)


def _gdn_fwd(qkvc, gb, gbt):
    tm, u, w, qg, kd, intra, egl = _gdn_local_fwd(qkvc, gb, gbt)
    o_f, o_b, s_f, s_b, vn_f, vn_b = _gdn_scan_fwd(u, w, qg, kd, intra, egl, qkvc.shape[0])
    return o_f, o_b, dict(tm=tm, w=w, qg=qg, kd=kd, intra=intra, egl=egl, s=(s_f, s_b), vn=(vn_f, vn_b))


N_CHAINS = 2 * GDN_HEADS


LOCAL_CHUNKS = 4


def _load_chains(x_ref, g_ref, gt_ref, cc=0):
    hd = GDN_HEADS * GDN_DIM
    rows = slice(cc * CHUNK, (cc + 1) * CHUNK)
    chains = []
    for d in range(2):
        masks = _chunk_masks(d == 0)
        for h in range(GDN_HEADS):
            ch = d * GDN_HEADS + h
            q = x_ref[rows, h * GDN_DIM:(h + 1) * GDN_DIM]
            k = x_ref[rows, hd + h * GDN_DIM:hd + (h + 1) * GDN_DIM]
            v = x_ref[rows, 2 * hd + h * GDN_DIM:2 * hd + (h + 1) * GDN_DIM]
            bcol = g_ref[rows, 8 + ch:9 + ch]
            cm = _chain_common(q, k, v, g_ref[rows, ch:ch + 1], gt_ref[cc, ch:ch + 1, :], bcol, masks)
            chains.append(dict(cm, q=q, k=k, v=v, bcol=bcol, masks=masks, ch=ch, h=h, cc=cc))
    return chains


def _chain_shape(rows, cols, dtype):
    return lambda nc: jax.ShapeDtypeStruct((nc, N_CHAINS, rows, cols), dtype)


def _gdn_local_fwd(qkvc, gb, gbt):
    t = qkvc.shape[0]
    nc = t // CHUNK
    hd = GDN_HEADS * GDN_DIM

    def body(x_ref, g_ref, gt_ref, t_ref, u_ref, w_ref, qg_ref, kd_ref, in_ref, eg_ref):
        chains = [c for cc in range(LOCAL_CHUNKS) for c in _load_chains(x_ref, g_ref, gt_ref, cc)]
        ii, jj = chains[0]["masks"][0:2]
        tms = _tri_inv_many([jnp.where(c["masks"][3], c["amat"] * c["decay"], 0.0) for c in chains], ii, jj)
        uws = [_dot(tm, jnp.concatenate([c["vb"], c["kbg"]], axis=1)) for tm, c in zip(tms, chains)]
        for c, tm, uw in zip(chains, tms, uws):
            cc, ch = c["cc"], c["ch"]
            t_ref[cc, ch] = tm
            u_ref[cc, ch] = uw[:, :GDN_DIM]
            w_ref[cc, ch] = uw[:, GDN_DIM:].astype(BF16)
            qg_ref[cc, ch] = c["qg"].astype(BF16)
            kd_ref[cc, ch] = c["kdec"].astype(BF16)
            in_ref[cc, ch] = c["intra"].astype(BF16)
            eg_ref[cc, ch:ch + 1, :] = jnp.broadcast_to(jnp.exp(c["glast"]), (1, LANES))

    lc = LOCAL_CHUNKS
    blk = lambda rows, cols: pl.BlockSpec((lc, N_CHAINS, rows, cols), lambda n: (n, 0, 0, 0))
    shapes = [_chain_shape(CHUNK, CHUNK, F32), _chain_shape(CHUNK, GDN_DIM, F32), _chain_shape(CHUNK, GDN_DIM, BF16),
              _chain_shape(CHUNK, GDN_DIM, BF16), _chain_shape(CHUNK, GDN_DIM, BF16), _chain_shape(CHUNK, CHUNK, BF16)]
    return tuple(pl.pallas_call(
        body, name="gdn_local_fwd", grid=(nc // lc,),
        in_specs=[pl.BlockSpec((lc * CHUNK, 3 * hd), lambda n: (n, 0)), pl.BlockSpec((lc * CHUNK, LANES), lambda n: (n, 0)),
                  pl.BlockSpec((lc, 16, CHUNK), lambda n: (n, 0, 0))],
        out_specs=[blk(CHUNK, CHUNK), blk(CHUNK, GDN_DIM), blk(CHUNK, GDN_DIM), blk(CHUNK, GDN_DIM),
                   blk(CHUNK, GDN_DIM), blk(CHUNK, CHUNK), pl.BlockSpec((lc, N_CHAINS, LANES), lambda n: (n, 0, 0))],
        out_shape=[s(nc) for s in shapes] + [jax.ShapeDtypeStruct((nc, N_CHAINS, LANES), F32)],
        compiler_params=_params(("arbitrary",), VMEM_LIMIT),
    )(qkvc, gb, gbt))


SCAN_CHUNKS = 8


def _dir_specs(nc, rev):
    nb = nc // SCAN_CHUNKS

    def spec(d, rows, cols, own=False):
        chunk = (lambda n: n) if (d == 0) != rev else (lambda n: nb - 1 - n)
        blk = 0 if own else d
        if rows is None:
            return pl.BlockSpec((SCAN_CHUNKS, GDN_HEADS if own else N_CHAINS, cols), lambda n: (chunk(n), 0, 0))
        return pl.BlockSpec((SCAN_CHUNKS, GDN_HEADS, rows, cols), lambda n: (chunk(n), blk, 0, 0))

    def rows_spec(d, cols):
        chunk = (lambda n: n) if (d == 0) != rev else (lambda n: nb - 1 - n)
        return pl.BlockSpec((SCAN_CHUNKS * CHUNK, cols), lambda n: (chunk(n), 0))

    def order(d):
        return list(range(SCAN_CHUNKS)) if (d == 0) != rev else list(range(SCAN_CHUNKS - 1, -1, -1))
    return spec, rows_spec, order


def _gdn_scan_fwd(u, w, qg, kd, intra, egl, t):
    nc = t // CHUNK
    hd = GDN_HEADS * GDN_DIM

    def body(*refs):
        ins, outs, state = refs[:12], refs[12:18], refs[18]
        @pl.when(pl.program_id(0) == 0)
        def _():
            state[...] = jnp.zeros_like(state)

        chains = [(d, h) for d in range(2) for h in range(GDN_HEADS)]
        states = [state[ch] for ch in range(N_CHAINS)]
        for step in range(SCAN_CHUNKS):
            at = [order(d)[step] for d in range(2)]
            pick = lambda k, d, h: ins[2 * k + d][at[d], h]
            sbs = [s.astype(BF16) for s in states]
            ws = [_dot(pick(1, d, h), sb) for (d, h), sb in zip(chains, sbs)]
            o1 = [_dot(pick(2, d, h), sb) for (d, h), sb in zip(chains, sbs)]
            vns = [(pick(0, d, h) - wsb).astype(BF16) for (d, h), wsb in zip(chains, ws)]
            o2 = [_dot(pick(4, d, h), vn) for (d, h), vn in zip(chains, vns)]
            kv = [_dot_tn(pick(3, d, h), vn) for (d, h), vn in zip(chains, vns)]
            new_states = []
            for ch, (d, h) in enumerate(chains):
                outs[d][at[d] * CHUNK:(at[d] + 1) * CHUNK, h * GDN_DIM:(h + 1) * GDN_DIM] = o1[ch] + o2[ch]
                outs[2 + d][at[d], h] = sbs[ch]
                outs[4 + d][at[d], h] = vns[ch]
                new_states.append(states[ch] * ins[10 + d][at[d], ch:ch + 1, :] + kv[ch])
            states = new_states
        for ch in range(N_CHAINS):
            state[ch] = states[ch]

    spec, rows_spec, order = _dir_specs(nc, False)
    pair = lambda rows, cols, own=False: [spec(0, rows, cols, own), spec(1, rows, cols, own)]
    s_shape = jax.ShapeDtypeStruct((nc, GDN_HEADS, GDN_DIM, GDN_DIM), BF16)
    vn_shape = jax.ShapeDtypeStruct((nc, GDN_HEADS, CHUNK, GDN_DIM), BF16)
    return pl.pallas_call(
        body, name="gdn_scan_fwd", grid=(nc // SCAN_CHUNKS,),
        in_specs=(pair(CHUNK, GDN_DIM) + pair(CHUNK, GDN_DIM) + pair(CHUNK, GDN_DIM) + pair(CHUNK, GDN_DIM)
                  + pair(CHUNK, CHUNK) + pair(None, LANES)),
        out_specs=([rows_spec(0, hd), rows_spec(1, hd)] + pair(GDN_DIM, GDN_DIM, True)
                   + pair(CHUNK, GDN_DIM, True)),
        out_shape=[jax.ShapeDtypeStruct((t, hd), F32), jax.ShapeDtypeStruct((t, hd), F32),
                   s_shape, s_shape, vn_shape, vn_shape],
        scratch_shapes=[pltpu.VMEM((N_CHAINS, GDN_DIM, GDN_DIM), F32)],
        compiler_params=_params(("arbitrary",), VMEM_LIMIT),
    )(u, u, w, w, qg, qg, kd, kd, intra, intra, egl, egl)


def _gdn_bwd(qkvc, gb, gbt, do, saved, exchange=None):
    scan = _gdn_scan_bwd(do, saved, qkvc.shape[0])
    return _gdn_local_bwd(qkvc, gb, gbt, do, saved, scan, exchange)


def _gdn_scan_bwd(do, saved, t):
    nc = t // CHUNK
    hd = GDN_HEADS * GDN_DIM

    def body(*refs):
        ins, outs, dstate = refs[:16], refs[16:26], refs[26]
        @pl.when(pl.program_id(0) == 0)
        def _():
            dstate[...] = jnp.zeros_like(dstate)

        chains = [(d, h) for d in range(2) for h in range(GDN_HEADS)]
        dss = [dstate[ch] for ch in range(N_CHAINS)]
        for step in range(SCAN_CHUNKS):
            at = [order(d)[step] for d in range(2)]
            pick = lambda k, d, h: ins[2 * k + d][at[d], h]
            dsbs = [ds.astype(BF16) for ds in dss]
            ss = [pick(1, d, h) for d, h in chains]
            sbs = ss
            dos = [ins[d][at[d] * CHUNK:(at[d] + 1) * CHUNK, h * GDN_DIM:(h + 1) * GDN_DIM].astype(BF16)
                   for d, h in chains]
            dv1 = [_dot_tn(pick(5, d, h), dov) for (d, h), dov in zip(chains, dos)]
            dv2 = [_dot(pick(4, d, h), dsb) for (d, h), dsb in zip(chains, dsbs)]
            ds1 = [_dot_tn(pick(3, d, h), dov) for (d, h), dov in zip(chains, dos)]
            dkds = [_dot_nt(pick(6, d, h), dsb) for (d, h), dsb in zip(chains, dsbs)]
            dqgs = [_dot_nt(dov, sb) for dov, sb in zip(dos, sbs)]
            dvns = [(a + b).astype(BF16) for a, b in zip(dv1, dv2)]
            ds2 = [_dot_tn(pick(2, d, h), dvn) for (d, h), dvn in zip(chains, dvns)]
            dws = [_dot_nt(dvn, sb) for dvn, sb in zip(dvns, sbs)]
            new_dss = []
            for ch, (d, h) in enumerate(chains):
                egl = ins[14 + d][at[d], ch:ch + 1, :]
                outs[d][at[d], h] = dvns[ch]
                outs[2 + d][at[d], h] = (-dws[ch]).astype(BF16)
                outs[4 + d][at[d], h] = dqgs[ch]
                outs[6 + d][at[d], h] = dkds[ch]
                outs[8 + d][at[d], h:h + 1, :] = egl * jnp.sum(jnp.sum(ss[ch].astype(F32) * dss[ch], axis=1, keepdims=True),
                                                               axis=0, keepdims=True)
                new_dss.append(ds1[ch] + egl * dss[ch] - ds2[ch])
            dss = new_dss
        for ch in range(N_CHAINS):
            dstate[ch] = dss[ch]

    spec, rows_spec, order = _dir_specs(nc, True)
    pair = lambda rows, cols, own=False: [spec(0, rows, cols, own), spec(1, rows, cols, own)]
    s_f, s_b = saved["s"]
    vn_f, vn_b = saved["vn"]
    w, qg, kd, intra, egl = saved["w"], saved["qg"], saved["kd"], saved["intra"], saved["egl"]
    own = lambda rows, cols, dtype: jax.ShapeDtypeStruct((nc, GDN_HEADS, rows, cols), dtype)
    row_shape = jax.ShapeDtypeStruct((nc, GDN_HEADS, LANES), F32)
    return pl.pallas_call(
        body, name="gdn_scan_bwd", grid=(nc // SCAN_CHUNKS,),
        in_specs=([rows_spec(0, hd), rows_spec(1, hd)] + pair(GDN_DIM, GDN_DIM, True) + pair(CHUNK, GDN_DIM)
                  + pair(CHUNK, GDN_DIM) + pair(CHUNK, GDN_DIM) + pair(CHUNK, CHUNK) + pair(CHUNK, GDN_DIM, True)
                  + pair(None, LANES)),
        out_specs=(pair(CHUNK, GDN_DIM, True) + pair(CHUNK, GDN_DIM, True) + pair(CHUNK, GDN_DIM, True)
                   + pair(CHUNK, GDN_DIM, True) + pair(None, LANES, True)),
        out_shape=[own(CHUNK, GDN_DIM, BF16)] * 4 + [own(CHUNK, GDN_DIM, F32)] * 4 + [row_shape] * 2,
        scratch_shapes=[pltpu.VMEM((N_CHAINS, GDN_DIM, GDN_DIM), F32)],
        compiler_params=_params(("arbitrary",), VMEM_LIMIT),
    )(do, do, s_f, s_b, w, w, qg, qg, kd, kd, intra, intra, vn_f, vn_b, egl, egl)


def _dot3_nt(a, b):
    ah = a.astype(BF16)
    al = (a - ah.astype(F32)).astype(BF16)
    bh = b.astype(BF16)
    bl = (b - bh.astype(F32)).astype(BF16)
    return _dot_nt(ah, bh) + (_dot_nt(ah, bl) + _dot_nt(al, bh))


def _dot3_tn(a, b):
    ah = a.astype(BF16)
    al = (a - ah.astype(F32)).astype(BF16)
    bh = b.astype(BF16)
    bl = (b - bh.astype(F32)).astype(BF16)
    return _dot_tn(ah, bh) + (_dot_tn(ah, bl) + _dot_tn(al, bh))


def _gdn_local_bwd(qkvc, gb, gbt, do, saved, scan, exchange=None):
    t = qkvc.shape[0]
    nc = t // CHUNK
    hd = GDN_HEADS * GDN_DIM

    def body(*refs):
        x_ref, g_ref, gt_ref, do_ref, t_ref = refs[:5]
        per_dir = refs[5:17]
        dx_ref, dg_ref = refs[17:]
        chains = [c for cc in range(LOCAL_CHUNKS) for c in _load_chains(x_ref, g_ref, gt_ref, cc)]
        lane = lax.broadcasted_iota(jnp.int32, (CHUNK, LANES), 1)
        dgates = [jnp.zeros((CHUNK, LANES), F32) for _ in range(LOCAL_CHUNKS)]
        for c in chains:
            d = c["ch"] // GDN_HEADS
            vn_ref, dvn_ref, dw_ref, dqg_ref, dkd_ref, dgl_ref = per_dir[d::2]
            h, cc = c["h"], c["cc"]
            rows = slice(cc * CHUNK, (cc + 1) * CHUNK)
            c.update(tm=t_ref[cc, c["ch"]], dov=do_ref[rows, h * GDN_DIM:(h + 1) * GDN_DIM], vnew=vn_ref[cc, h],
                     dvnew=dvn_ref[cc, h], dw=dw_ref[cc, h], dqg=dqg_ref[cc, h], dkdec=dkd_ref[cc, h],
                     dglast=dgl_ref[cc, h:h + 1, 0:1])
        dintras = [_dot_nt(c["dov"], c["vnew"]) for c in chains]
        dts = [_dot_nt(c["dvnew"], c["vb"]) + _dot_nt(c["dw"], c["kbg"]) for c in chains]
        dvbs = [_dot_tn(c["tm"], c["dvnew"]) for c in chains]
        dkbgs = [_dot_tn(c["tm"], c["dw"]) for c in chains]
        tdts = [_dot3_nt(dt, c["tm"]) for dt, c in zip(dts, chains)]
        dls = [jnp.where(c["masks"][3], -_dot3_tn(c["tm"], tdt), 0.0) for tdt, c in zip(tdts, chains)]
        das = [dl * c["decay"] for dl, c in zip(dls, chains)]
        dqks = [jnp.where(c["masks"][2], di, 0.0) * c["decay"] for di, c in zip(dintras, chains)]
        dkb1 = [_dot(da, c["k"]) for da, c in zip(das, chains)]
        dk1 = [_dot_tn(da, c["kb"]) for da, c in zip(das, chains)]
        dk2 = [_dot_tn(dqk, c["q"]) for dqk, c in zip(dqks, chains)]
        dq1 = [_dot(dqk, c["k"]) for dqk, c in zip(dqks, chains)]
        grads, mms, p_gs, p_betas, p_kds = [], [], [], [], []
        for n, c in enumerate(chains):
            incl = c["masks"][2]
            dkb = dkb1[n] + dkbgs[n] * c["eg"]
            kd = c["dkdec"] * c["kdec"]
            mms.append((dls[n] * c["amat"] + jnp.where(incl, dintras[n], 0.0) * c["qk"]) * c["decay"])
            p_gs.append(c["dqg"] * c["qg"] - kd + dkbgs[n] * c["kbg"])
            p_betas.append(dkb * c["k"] + dvbs[n] * c["v"])
            p_kds.append(kd)
            grads.append((dq1[n] + c["dqg"] * c["eg"],
                          dk1[n] + dk2[n] + c["dkdec"] * c["ek"] + dkb * c["bcol"],
                          dvbs[n] * c["bcol"]))
        row_sums = [jnp.sum(mm, axis=1, keepdims=True) for mm in mms]
        col_sums = [jnp.sum(mm, axis=0, keepdims=True) for mm in mms]
        g_sums = [jnp.sum(pg, axis=1, keepdims=True) for pg in p_gs]
        dbetas = [jnp.sum(pb, axis=1, keepdims=True) for pb in p_betas]
        kd_tots = [jnp.sum(jnp.sum(pk, axis=1, keepdims=True), axis=0, keepdims=True) for pk in p_kds]
        dgcs = [rs - _row_to_col(cs, *c["masks"][0:2]) + gs for rs, cs, gs, c in zip(row_sums, col_sums, g_sums, chains)]
        dgrs = [_col_to_row(dgc, *c["masks"][0:2]) for dgc, c in zip(dgcs, chains)]
        draws = [jnp.sum(jnp.where(jnp.logical_not(c["masks"][3]), dgr, 0.0), axis=1, keepdims=True) + c["dglast"] + kt
                 for dgr, kt, c in zip(dgrs, kd_tots, chains)]
        for c, draw, dbeta in zip(chains, draws, dbetas):
            ch = c["ch"]
            dgates[c["cc"]] = dgates[c["cc"]] + jnp.where(lane == ch, draw, 0.0) + jnp.where(lane == 8 + ch, dbeta, 0.0)
        for cc in range(LOCAL_CHUNKS):
            rows = slice(cc * CHUNK, (cc + 1) * CHUNK)
            for h in range(GDN_HEADS):
                for part in range(3):
                    cols = slice(part * hd + h * GDN_DIM, part * hd + (h + 1) * GDN_DIM)
                    dx_ref[rows, cols] = grads[cc * N_CHAINS + h][part] + grads[cc * N_CHAINS + GDN_HEADS + h][part]
            dg_ref[rows, :] = dgates[cc]

    lc = LOCAL_CHUNKS
    all8 = lambda rows, cols: pl.BlockSpec((lc, N_CHAINS, rows, cols), lambda n: (n, 0, 0, 0))
    own4 = lambda rows, cols: pl.BlockSpec((lc, GDN_HEADS, rows, cols), lambda n: (n, 0, 0, 0))
    row4 = pl.BlockSpec((lc, GDN_HEADS, LANES), lambda n: (n, 0, 0))
    vn_f, vn_b = saved["vn"]
    dvn_f, dvn_b, dw_f, dw_b, dqg_f, dqg_b, dkd_f, dkd_b, dgl_f, dgl_b = scan
    return _grid_call(
        body, "gdn_local_bwd", nc // lc,
        [pl.BlockSpec((lc * CHUNK, 3 * hd), lambda n: (n, 0)), pl.BlockSpec((lc * CHUNK, LANES), lambda n: (n, 0)),
         pl.BlockSpec((lc, 16, CHUNK), lambda n: (n, 0, 0)), pl.BlockSpec((lc * CHUNK, hd), lambda n: (n, 0)),
         all8(CHUNK, CHUNK)] + [own4(CHUNK, GDN_DIM)] * 10 + [row4, row4],
        [pl.BlockSpec((lc * CHUNK, 3 * hd), lambda n: (n, 0)), pl.BlockSpec((lc * CHUNK, LANES), lambda n: (n, 0))],
        [jax.ShapeDtypeStruct((t, 3 * hd), F32), jax.ShapeDtypeStruct((t, LANES), F32)],
        (qkvc, gb, gbt, do, saved["tm"], vn_f, vn_b, dvn_f, dvn_b, dw_f, dw_b, dqg_f, dqg_b, dkd_f, dkd_b, dgl_f, dgl_b),
        exchange=exchange)


def _gdn_post_fwd(of, ob, z, gw, tm):
    t, hd = of.shape

    def body(of_ref, ob_ref, z_ref, w_ref, o_ref):
        for h in range(GDN_HEADS):
            cols = slice(h * GDN_DIM, (h + 1) * GDN_DIM)
            o = of_ref[:, cols] + ob_ref[:, cols]
            zv = z_ref[:, cols]
            o_ref[:, cols] = (o * _rstd(o) * w_ref[...] * (zv * _sigmoid(zv))).astype(BF16)

    row = pl.BlockSpec((tm, hd), lambda i: (i, 0))
    return pl.pallas_call(
        body, name="gdn_post_fwd", grid=(t // tm,),
        in_specs=[row, row, row, _resident((1, GDN_DIM))],
        out_specs=row, out_shape=jax.ShapeDtypeStruct((t, hd), BF16),
        compiler_params=_params(("arbitrary",), VMEM_LIMIT),
    )(of, ob, z, gw)


def _gdn_post_bwd(doa, of, ob, z, gw, tm):
    t, hd = of.shape

    def body(d_ref, of_ref, ob_ref, z_ref, w_ref, do_ref, dz_ref, dw_ref):
        @pl.when(pl.program_id(0) == 0)
        def _():
            dw_ref[...] = jnp.zeros_like(dw_ref)

        dw = jnp.zeros((1, GDN_DIM), F32)
        for h in range(GDN_HEADS):
            cols = slice(h * GDN_DIM, (h + 1) * GDN_DIM)
            o = of_ref[:, cols] + ob_ref[:, cols]
            zv = z_ref[:, cols]
            dv = d_ref[:, cols]
            r = _rstd(o)
            sg = _sigmoid(zv)
            on = o * r * w_ref[...]
            dz_ref[:, cols] = (dv * on * (sg * (1.0 + zv * (1.0 - sg)))).astype(BF16)
            dxr, dwh = _rms_bwd(o, r, w_ref[...], dv * (zv * sg))
            do_ref[:, cols] = dxr
            dw = dw + dwh
        dw_ref[...] += dw

    row = pl.BlockSpec((tm, hd), lambda i: (i, 0))
    return pl.pallas_call(
        body, name="gdn_post_bwd", grid=(t // tm,),
        in_specs=[row, row, row, row, _resident((1, GDN_DIM))],
        out_specs=[row, row, pl.BlockSpec((1, GDN_DIM), lambda i: (0, 0))],
        out_shape=[jax.ShapeDtypeStruct((t, hd), F32), jax.ShapeDtypeStruct((t, hd), BF16),
                   jax.ShapeDtypeStruct((1, GDN_DIM), F32)],
        compiler_params=_params(("arbitrary",), VMEM_LIMIT),
    )(doa, of, ob, z, gw)


SWA_W = SWA_HEADS * SWA_DIM
QBLK = 128
KWIN = QBLK + 2 * RADIUS
WIN_OFFSETS = (0, RADIUS, 2 * RADIUS)


def _t5_bucket(rel):
    nb = REL_BUCKETS // 2
    bucket = (rel > 0).astype(np.int32) * nb
    n = np.abs(rel)
    max_exact = nb // 2
    large = max_exact + (np.log(np.maximum(n, 1) / max_exact)
                         / math.log(REL_MAX_DISTANCE / max_exact) * (nb - max_exact)).astype(np.int32)
    large = np.minimum(large, nb - 1)
    return (bucket + np.where(n < max_exact, n, large)).astype(np.int32)


def _band_tables(dilation):
    a = np.arange(QBLK)
    b = np.arange(KWIN)
    rel = np.stack([b[None, :] - w0 - a[:, None] for w0 in WIN_OFFSETS])
    return np.where(np.abs(rel) <= RADIUS, _t5_bucket(rel * dilation), -1).astype(np.int32)


BAND_CELLS = len(WIN_OFFSETS) * QBLK * KWIN
BIAS_TILE = BAND_CELLS // 3


def _band_index():
    return jnp.asarray(np.concatenate([_band_tables(d).reshape(-1) for _, d in PATTERNS])[None, :])


def _onehot(idx, dtype):
    return (lax.broadcasted_iota(jnp.int32, (REL_BUCKETS, idx.shape[1]), 0) == idx).astype(dtype)


def _bias_tables(rel_bias, idx, tk):
    n = idx.shape[1]

    def body(rb_ref, i_ref, o_ref):
        iv = i_ref[...]
        oh = _onehot(iv, BF16)
        rest, acc = rb_ref[...], None
        for _ in range(3):
            piece = rest.astype(BF16)
            part = jnp.dot(piece, oh, preferred_element_type=F32)
            acc = part if acc is None else acc + part
            rest = rest - piece.astype(F32)
        o_ref[...] = jnp.where(iv < 0, NEG_BIG, acc)

    return pl.pallas_call(
        body, name="bias_tables", grid=(n // tk,),
        in_specs=[_resident((SWA_HEADS, REL_BUCKETS)), pl.BlockSpec((1, tk), lambda k: (0, k))],
        out_specs=pl.BlockSpec((SWA_HEADS, tk), lambda k: (0, k)),
        out_shape=jax.ShapeDtypeStruct((SWA_HEADS, n), F32),
        compiler_params=_params(("arbitrary",), VMEM_LIMIT),
    )(rel_bias.T, idx)


def _head_mean(x2, bd_ref):
    bd = bd_ref[...]
    rest, acc = x2, None
    for _ in range(3):
        piece = rest.astype(BF16)
        part = jnp.dot(piece, bd, preferred_element_type=F32)
        acc = part if acc is None else acc + part
        rest = rest - piece.astype(F32)
    return acc


VIEW_DILATIONS = tuple(d for _, d in PATTERNS if d > 1)


def _view_spec(tm, d):
    return pl.BlockSpec((tm // d, d * SWA_W), lambda i: (i, 0))


def _view_shape(t, d, dtype):
    return jax.ShapeDtypeStruct((t // d, d * SWA_W), dtype)


N_GROUPS = SWA_W // LANES


def _to_view(src_ref, idx, dst_ref, d, rows):
    for r in range(d):
        for g in range(N_GROUPS):
            cols = slice(r * SWA_W + g * LANES, r * SWA_W + (g + 1) * LANES)
            dst_ref[:, cols] = src_ref[idx, g, pl.ds(r, rows // d, stride=d), :].astype(dst_ref.dtype)


def _from_view(src_ref, dst_ref, idx, d, rows):
    for r in range(d):
        for g in range(N_GROUPS):
            cols = slice(r * SWA_W + g * LANES, r * SWA_W + (g + 1) * LANES)
            dst_ref[idx, g, pl.ds(r, rows // d, stride=d), :] = src_ref[:, cols]


def _swa_prep_fwd(qkvb, qw, kw, bd, tm):
    t = qkvb.shape[0]

    def body(x_ref, qw_ref, kw_ref, bd_ref, *rest):
        outs, sc = rest[:-1], rest[-1]
        for gidx in range(N_GROUPS):
            cols = slice(gidx * LANES, (gidx + 1) * LANES)
            xq = x_ref[:, cols]
            sc[0, gidx] = xq * lax.rsqrt(_head_mean(xq * xq, bd_ref) + EPS) * qw_ref[:, cols] * (SWA_DIM ** -0.5)
            xk = x_ref[:, SWA_W + gidx * LANES:SWA_W + (gidx + 1) * LANES]
            sc[1, gidx] = xk * lax.rsqrt(_head_mean(xk * xk, bd_ref) + EPS) * kw_ref[:, cols]
            sc[2, gidx] = x_ref[:, 2 * SWA_W + gidx * LANES:2 * SWA_W + (gidx + 1) * LANES]
            for i in range(3):
                outs[i][:, cols] = sc[i, gidx].astype(BF16)
        for i in range(3):
            for n, d in enumerate(VIEW_DILATIONS):
                _to_view(sc, i, outs[3 * (n + 1) + i], d, tm)

    return pl.pallas_call(
        body, name="swa_prep_fwd", grid=(t // tm,),
        in_specs=[pl.BlockSpec((tm, 3 * SWA_W), lambda i: (i, 0)), _resident((1, SWA_W)), _resident((1, SWA_W)),
                  _resident((LANES, LANES))],
        out_specs=[_view_spec(tm, d) for d in (1,) + VIEW_DILATIONS for _ in range(3)],
        out_shape=[_view_shape(t, d, BF16) for d in (1,) + VIEW_DILATIONS for _ in range(3)],
        scratch_shapes=[pltpu.VMEM((3, N_GROUPS, tm, LANES), F32)],
        compiler_params=_params(("arbitrary",), VMEM_LIMIT),
    )(qkvb, qw, kw, bd)


def _swa_prep_bwd(qkvb, qw, kw, bd, grads, tm):
    t = qkvb.shape[0]

    def body(x_ref, qw_ref, kw_ref, bd_ref, *rest):
        parts, (dx_ref, dqw_ref, dkw_ref, sc) = rest[:9], rest[9:]
        @pl.when(pl.program_id(0) == 0)
        def _():
            dqw_ref[...] = jnp.zeros_like(dqw_ref)
            dkw_ref[...] = jnp.zeros_like(dkw_ref)

        for i in range(3):
            for n, d in enumerate(VIEW_DILATIONS):
                _from_view(parts[3 * (n + 1) + i], sc, 2 * i + n, d, tm)
        for gidx in range(N_GROUPS):
            cols = slice(gidx * LANES, (gidx + 1) * LANES)
            for i, base, w_ref, dw_ref, scale in ((0, 0, qw_ref, dqw_ref, SWA_DIM ** -0.5),
                                                  (1, SWA_W, kw_ref, dkw_ref, 1.0)):
                xv = x_ref[:, base + gidx * LANES:base + (gidx + 1) * LANES]
                dy = (parts[i][:, cols] + sc[2 * i, gidx] + sc[2 * i + 1, gidx]) * scale
                r = lax.rsqrt(_head_mean(xv * xv, bd_ref) + EPS)
                xhat = xv * r
                dxh = dy * w_ref[:, cols]
                dx = r * (dxh - xhat * _head_mean(dxh * xhat, bd_ref))
                dx_ref[:, base + gidx * LANES:base + (gidx + 1) * LANES] = dx.astype(BF16)
                dw_ref[:, cols] += jnp.sum(dy * xhat, axis=0, keepdims=True)
            dx_ref[:, 2 * SWA_W + gidx * LANES:2 * SWA_W + (gidx + 1) * LANES] = (
                parts[2][:, cols] + sc[4, gidx] + sc[5, gidx]).astype(BF16)

    wrow = pl.BlockSpec((1, SWA_W), lambda i: (0, 0))
    return pl.pallas_call(
        body, name="swa_prep_bwd", grid=(t // tm,),
        in_specs=[pl.BlockSpec((tm, 3 * SWA_W), lambda i: (i, 0)), _resident((1, SWA_W)), _resident((1, SWA_W)),
                  _resident((LANES, LANES))] + [_view_spec(tm, d) for d in (1,) + VIEW_DILATIONS for _ in range(3)],
        out_specs=[pl.BlockSpec((tm, 3 * SWA_W), lambda i: (i, 0)), wrow, wrow],
        out_shape=[jax.ShapeDtypeStruct((t, 3 * SWA_W), BF16), jax.ShapeDtypeStruct((1, SWA_W), F32),
                   jax.ShapeDtypeStruct((1, SWA_W), F32)],
        scratch_shapes=[pltpu.VMEM((6, N_GROUPS, tm, LANES), F32)],
        compiler_params=_params(("arbitrary",), VMEM_LIMIT),
    )(qkvb, qw, kw, bd, *grads)


def _aligned(v, m):
    return v if isinstance(v, int) else pl.multiple_of(v, m)


BAND_GROUP = 2


def _band_loop(nsub, length, step, group=BAND_GROUP):
    step([(0, 0)], 0)
    if nsub > 2:
        assert (nsub - 2) % group == 0

        def inner(i, carry):
            s0 = 1 + i * group
            step([(s0 + e, pl.multiple_of((s0 + e) * QBLK - RADIUS, RADIUS)) for e in range(group)], 1)
            return carry
        lax.fori_loop(0, (nsub - 2) // group, inner, 0)
    step([(nsub - 1, length - KWIN)], 2)


def _head_select(lane, a0, a1):
    return jnp.where(lane < SWA_DIM, a0, a1)


def _swa_fwd(qv, kv, vv, bias, dilation, name):
    length = qv.shape[0]
    nsub = length // QBLK
    assert nsub >= 2 and length % QBLK == 0

    def body(q_ref, k_ref, v_ref, b_ref, o_ref, l_ref):
        lane = lax.broadcasted_iota(jnp.int32, (QBLK, LANES), 1)

        def step(blocks, var):
            items = []
            for s, ws in blocks:
                rows = pl.ds(_aligned(s * QBLK, QBLK), QBLK)
                q, kk, vw = q_ref[rows, :], k_ref[pl.ds(ws, KWIN), :], v_ref[pl.ds(ws, KWIN), :]
                for hh in range(2):
                    items.append((hh, jnp.where((lane < SWA_DIM) == (hh == 0), q, jnp.zeros_like(q)), kk, vw))
            lgs = [_dot_nt(qh, kk) + b_ref[hh, var] for hh, qh, kk, _ in items]
            ms = [jnp.max(lg, axis=-1, keepdims=True) for lg in lgs]
            ps = [jnp.exp(lg - m) for lg, m in zip(lgs, ms)]
            dens = [jnp.sum(p, axis=-1, keepdims=True) for p in ps]
            pvs = [_dot(p, it[3]) for p, it in zip(ps, items)]
            for n, (s, _) in enumerate(blocks):
                rows = pl.ds(_aligned(s * QBLK, QBLK), QBLK)
                o0, o1 = (pvs[2 * n + hh] / dens[2 * n + hh] for hh in range(2))
                l0, l1 = (ms[2 * n + hh] + jnp.log(dens[2 * n + hh]) for hh in range(2))
                o_ref[rows, :] = _head_select(lane, o0, o1)
                l_ref[rows, :] = _head_select(lane, l0, l1)

        _band_loop(nsub, length, step)

    blk = pl.BlockSpec((length, LANES), lambda hp, r: (0, r * (SWA_W // LANES) + hp))
    shp = jax.ShapeDtypeStruct(qv.shape, F32)
    return pl.pallas_call(
        body, name=name, grid=(SWA_W // LANES, dilation),
        in_specs=[blk, blk, blk, pl.BlockSpec((2, 3, QBLK, KWIN), lambda hp, r: (hp, 0, 0, 0))],
        out_specs=[blk, blk], out_shape=[shp, shp],
        compiler_params=_params(("arbitrary", "arbitrary"), VMEM_LIMIT),
    )(qv, kv, vv, bias)


def _swa_combine(os_, ls_, tm):
    t = os_[0].shape[0]

    def body(o0, o1, o2, l0, l1, l2, o_ref, ob_ref, la_ref, lb_ref, lc_ref, sc):
        for n, d in enumerate(VIEW_DILATIONS):
            _from_view((o1, o2)[n], sc, n, d, tm)
            _from_view((l1, l2)[n], sc, 2 + n, d, tm)
        for g in range(N_GROUPS):
            cols = slice(g * LANES, (g + 1) * LANES)
            la, lb, lc = l0[:, cols], sc[2, g], sc[3, g]
            m = jnp.maximum(jnp.maximum(la, lb), lc)
            tot = m + jnp.log(jnp.exp(la - m) + jnp.exp(lb - m) + jnp.exp(lc - m))
            o = jnp.exp(la - tot) * o0[:, cols] + jnp.exp(lb - tot) * sc[0, g] + jnp.exp(lc - tot) * sc[1, g]
            o_ref[:, cols] = o
            ob_ref[:, cols] = o.astype(BF16)
            la_ref[:, cols] = tot
            sc[4, g] = tot
        for n, d in enumerate(VIEW_DILATIONS):
            _to_view(sc, 4, (lb_ref, lc_ref)[n], d, tm)

    specs = [_view_spec(tm, d) for d in (1,) + VIEW_DILATIONS]
    return pl.pallas_call(
        body, name="swa_combine", grid=(t // tm,), in_specs=specs + specs, out_specs=[specs[0], specs[0]] + specs,
        out_shape=[jax.ShapeDtypeStruct((t, SWA_W), F32), jax.ShapeDtypeStruct((t, SWA_W), BF16)]
                  + [_view_shape(t, d, F32) for d in (1,) + VIEW_DILATIONS],
        scratch_shapes=[pltpu.VMEM((5, N_GROUPS, tm, LANES), F32)],
        compiler_params=_params(("arbitrary",), VMEM_LIMIT),
    )(*os_, *ls_)


def _swa_bwd_prep(do, o, bd, tm):
    t = do.shape[0]

    def body(d_ref, o_ref, bd_ref, dd1, dd4, dd16, db1, db4, db16, sc):
        for gidx in range(N_GROUPS):
            cols = slice(gidx * LANES, (gidx + 1) * LANES)
            dv = d_ref[:, cols]
            dd = _head_mean(dv * o_ref[:, cols], bd_ref) * float(SWA_DIM)
            sc[0, gidx] = dd
            sc[1, gidx] = dv
            dd1[:, cols] = dd
            db1[:, cols] = dv.astype(BF16)
        for n, d in enumerate(VIEW_DILATIONS):
            _to_view(sc, 0, (dd4, dd16)[n], d, tm)
            _to_view(sc, 1, (db4, db16)[n], d, tm)

    specs = [_view_spec(tm, d) for d in (1,) + VIEW_DILATIONS]
    return pl.pallas_call(
        body, name="swa_bwd_prep", grid=(t // tm,), in_specs=[specs[0], specs[0], _resident((LANES, LANES))],
        out_specs=specs + specs,
        out_shape=[_view_shape(t, d, F32) for d in (1,) + VIEW_DILATIONS]
                  + [_view_shape(t, d, BF16) for d in (1,) + VIEW_DILATIONS],
        scratch_shapes=[pltpu.VMEM((2, N_GROUPS, tm, LANES), F32)],
        compiler_params=_params(("arbitrary",), VMEM_LIMIT),
    )(do, o, bd)


def _swa_bwd(qv, kv, vv, dov, lv, ddv, bias_a, dilation, name):
    length = qv.shape[0]
    nsub = length // QBLK
    single = pl.Buffered(1) if dilation == 1 else None

    def body(q_ref, k_ref, v_ref, do_ref, l_ref, dd_ref, ba_ref, dq_ref, dk_ref, dv_ref, db_ref):
        @pl.when(pl.program_id(1) == 0)
        def _():
            db_ref[...] = jnp.zeros_like(db_ref)

        lane = lax.broadcasted_iota(jnp.int32, (QBLK, LANES), 1)
        lanew = lax.broadcasted_iota(jnp.int32, (KWIN, LANES), 1)

        def step(blocks, var):
            items = []
            for s, ws in blocks:
                rows = pl.ds(_aligned(s * QBLK, QBLK), QBLK)
                win = pl.ds(ws, KWIN)
                q, dov_ = q_ref[rows, :], do_ref[rows, :]
                kk, vw = k_ref[win, :], v_ref[win, :]
                lse, dd = l_ref[rows, :], dd_ref[rows, :]
                for hh in range(2):
                    mine = (lane < SWA_DIM) == (hh == 0)
                    col = slice(hh * SWA_DIM, hh * SWA_DIM + 1)
                    items.append((hh, jnp.where(mine, q, jnp.zeros_like(q)), jnp.where(mine, dov_, jnp.zeros_like(dov_)),
                                  kk, vw, lse[:, col], dd[:, col], q, dov_))
            lgs = [_dot_nt(it[1], it[3]) + ba_ref[it[0], var] for it in items]
            dps = [_dot_nt(it[2], it[4]) for it in items]
            ps = [jnp.exp(lg - it[5]) for lg, it in zip(lgs, items)]
            dss = [p * (dp - it[6]) for p, dp, it in zip(ps, dps, items)]
            dqs = [_dot(ds, it[3]) for ds, it in zip(dss, items)]
            dks = [_dot_tn(ds, it[7]) for ds, it in zip(dss, items)]
            dvs = [_dot_tn(p, it[8]) for p, it in zip(ps, items)]
            for n, (s, ws) in enumerate(blocks):
                rows = pl.ds(_aligned(s * QBLK, QBLK), QBLK)
                win = pl.ds(ws, KWIN)
                dq_ref[rows, :] = _head_select(lane, dqs[2 * n], dqs[2 * n + 1])
                dk_ref[win, :] += _head_select(lanew, dks[2 * n], dks[2 * n + 1])
                dv_ref[win, :] += _head_select(lanew, dvs[2 * n], dvs[2 * n + 1])
            for hh in range(2):
                tot = dss[hh]
                for n in range(1, len(blocks)):
                    tot = tot + dss[2 * n + hh]
                db_ref[hh, var] += tot

        dk_ref[...] = jnp.zeros_like(dk_ref)
        dv_ref[...] = jnp.zeros_like(dv_ref)
        _band_loop(nsub, length, step)

    imap = lambda hp, r: (0, r * (SWA_W // LANES) + hp)
    blk_in = pl.BlockSpec((length, LANES), imap, pipeline_mode=single)
    blk_out = pl.BlockSpec((length, LANES), imap)
    shp = jax.ShapeDtypeStruct(qv.shape, F32)
    return pl.pallas_call(
        body, name=name, grid=(SWA_W // LANES, dilation),
        in_specs=[blk_out] * 4 + [blk_in] * 2 + [pl.BlockSpec((2, 3, QBLK, KWIN), lambda hp, r: (hp, 0, 0, 0))],
        out_specs=[blk_out, blk_out, blk_out, pl.BlockSpec((2, 3, QBLK, KWIN), lambda hp, r: (hp, 0, 0, 0))],
        out_shape=[shp, shp, shp, jax.ShapeDtypeStruct((SWA_HEADS, 3, QBLK, KWIN), F32)],
        compiler_params=_params(("arbitrary", "arbitrary"), VMEM_LIMIT),
    )(qv, kv, vv, dov, lv, ddv, bias_a)


def _bias_grad(ds2, idx, tk):
    n = ds2.shape[1]
    nk = n // tk

    def body(a_ref, i_ref, o_ref):
        @pl.when(pl.program_id(0) == 0)
        def _():
            o_ref[...] = jnp.zeros_like(o_ref)

        oh = _onehot(i_ref[...], BF16)
        rest = a_ref[...]
        acc = jnp.zeros((SWA_HEADS, REL_BUCKETS), F32)
        for _ in range(3):
            piece = rest.astype(BF16)
            acc = acc + _dot_nt(piece, oh)
            rest = rest - piece.astype(F32)
        o_ref[...] += acc

    return pl.pallas_call(
        body, name="bias_grad", grid=(nk,),
        in_specs=[pl.BlockSpec((SWA_HEADS, tk), lambda k: (0, k)), pl.BlockSpec((1, tk), lambda k: (0, k))],
        out_specs=pl.BlockSpec((SWA_HEADS, REL_BUCKETS), lambda k: (0, 0)),
        out_shape=jax.ShapeDtypeStruct((SWA_HEADS, REL_BUCKETS), F32),
        compiler_params=_params(("arbitrary",), VMEM_LIMIT),
    )(ds2, idx)


def _swa_branch_fwd(qkvb, qw_t, kw_t, rel_bias, bd, tm):
    qkv = _swa_prep_fwd(qkvb, qw_t, kw_t, bd, tm)
    tables = _bias_tables(rel_bias, _band_index(), BIAS_TILE)
    os_, ls_, tabs = [], [], []
    for n, (_, d) in enumerate(PATTERNS):
        bias = tables[:, n * BAND_CELLS:(n + 1) * BAND_CELLS].reshape(SWA_HEADS, len(WIN_OFFSETS), QBLK, KWIN)
        o_p, l_p = _swa_fwd(*qkv[3 * n:3 * n + 3], bias, d, f"swa_fwd_d{d}")
        os_.append(o_p)
        ls_.append(l_p)
        tabs.append(bias)
    o, o16, *lses = _swa_combine(os_, ls_, tm)
    return o, o16, (qkv, lses, tabs)


def _swa_branch_bwd(do, o, saved, qkvb, qw_t, kw_t, bd, tm):
    qkv, lses, tabs = saved
    prep = _swa_bwd_prep(do, o, bd, tm)
    grads, dss = [], []
    for n, ((_, d), bias) in enumerate(zip(PATTERNS, tabs)):
        dq, dk, dv, ds = _swa_bwd(*qkv[3 * n:3 * n + 3], prep[3 + n], lses[n], prep[n], bias, d, f"swa_bwd_d{d}")
        grads += [dq, dk, dv]
        dss.append(ds.reshape(SWA_HEADS, -1))
    dqkvb, dqw, dkw = _swa_prep_bwd(qkvb, qw_t, kw_t, bd, grads, tm)
    dbias = _bias_grad(jnp.concatenate(dss, axis=1), _band_index(), BIAS_TILE)
    fold = lambda w: jnp.sum(w.reshape(SWA_HEADS, SWA_DIM), axis=0)
    return dqkvb, fold(dqw), fold(dkw), dbias.T


def _mesh_pos():
    return lax.axis_index("x"), lax.axis_index("y"), lax.axis_index("c")


def _other_chips(x, y):
    return [(1 - x, y), (x, 1 - y), (1 - x, 1 - y)]


def _remote(src, dst, send_sem, recv_sem, device):
    return pltpu.make_async_remote_copy(src_ref=src, dst_ref=dst, send_sem=send_sem, recv_sem=recv_sem,
                                        device_id=device, device_id_type=MESH)


def _split_axis(shape2):
    return 0 if (shape2[0] // 2) % 16 == 0 else 1


def _half_index(shape2, c):
    axis = _split_axis(shape2)
    h = shape2[axis] // 2
    return (pl.ds(c * h, h), slice(None)) if axis == 0 else (slice(None), pl.ds(c * h, h))


def _all_gather(xs):
    n = len(xs)

    def body(*refs):
        ins, outs = refs[:n], refs[n:2 * n]
        send_sems, recv_sems = refs[2 * n:]
        x, y, c = _mesh_pos()
        me = 2 * x + y
        chips = _other_chips(x, y)
        halves = []
        sends = []
        for a in range(n):
            h = ins[a].shape[0] // 2
            mine, other = pl.ds(c * h, h), pl.ds((1 - c) * h, h)
            halves.append((mine, other))
            own = _remote(ins[a], outs[a].at[me], send_sems.at[a, 6], recv_sems.at[a, 6], (x, y, 1 - c))
            own.start()
            sends.append(own)
            for j, chip in enumerate(chips):
                cp = _remote(ins[a].at[mine], outs[a].at[me, mine], send_sems.at[a, j], recv_sems.at[a, j], (*chip, c))
                cp.start()
                sends.append(cp)
        for a in range(n):
            mine, _ = halves[a]
            for j, chip in enumerate(chips):
                src = 2 * chip[0] + chip[1]
                landed = outs[a].at[src, mine]
                _remote(landed, landed, send_sems.at[a, j], recv_sems.at[a, j], (x, y, c)).wait_recv()
                fwd = _remote(landed, landed, send_sems.at[a, 3 + j], recv_sems.at[a, 3 + j], (x, y, 1 - c))
                fwd.start()
                sends.append(fwd)
        for a in range(n):
            _, other = halves[a]
            for j, chip in enumerate(chips):
                src = 2 * chip[0] + chip[1]
                landed = outs[a].at[src, other]
                _remote(landed, landed, send_sems.at[a, 3 + j], recv_sems.at[a, 3 + j], (x, y, c)).wait_recv()
            mine_slot = outs[a].at[me]
            _remote(mine_slot, mine_slot, send_sems.at[a, 6], recv_sems.at[a, 6], (x, y, c)).wait_recv()
        for cp in sends:
            cp.wait_send()

    return list(pl.pallas_call(
        body, name="all_gather_weights",
        in_specs=[ANY] * n, out_specs=[ANY] * n,
        out_shape=[jax.ShapeDtypeStruct((N_SHARDS,) + a.shape, a.dtype) for a in xs],
        scratch_shapes=[pltpu.SemaphoreType.DMA((n, 7)), pltpu.SemaphoreType.DMA((n, 7))],
    )(*xs))


def _rs_pair(gs):
    n = len(gs)

    def body(*refs):
        ins, lands = refs[:n], refs[n:2 * n]
        send_sems, recv_sems = refs[2 * n:]
        x, y, c = _mesh_pos()
        cps = []
        for a in range(n):
            theirs = (slice(None),) + _half_index(ins[a].shape[1:], 1 - c)
            cp = _remote(ins[a].at[theirs], lands[a], send_sems.at[a], recv_sems.at[a], (x, y, 1 - c))
            cp.start()
            cps.append(cp)
        for cp in cps:
            cp.wait()

    def half_shape(g):
        dims = list(g.shape)
        dims[1 + _split_axis(g.shape[1:])] //= 2
        return tuple(dims)

    return list(pl.pallas_call(
        body, name="rs_pair", in_specs=[ANY] * n, out_specs=[ANY] * n,
        out_shape=[jax.ShapeDtypeStruct(half_shape(g), g.dtype) for g in gs],
        scratch_shapes=[pltpu.SemaphoreType.DMA((n,)), pltpu.SemaphoreType.DMA((n,))],
    )(*gs))


def _pair_exchange(gs):
    def copies(cin, cout, send_sems, recv_sems):
        x, y, c = _mesh_pos()
        return [_remote(g.at[(slice(None),) + _half_index(g.shape[1:], 1 - c)], land, send_sems.at[a, 0],
                        recv_sems.at[a, 0], (x, y, 1 - c)) for a, (g, land) in enumerate(zip(cin, cout))]

    def start(*refs):
        for cp in copies(*refs):
            cp.start()

    def finish(*refs):
        for cp in copies(*refs):
            cp.wait()

    def half_shape(g):
        dims = list(g.shape)
        dims[1 + _split_axis(g.shape[1:])] //= 2
        return tuple(dims)

    return _Exchange(tuple(gs), tuple(jax.ShapeDtypeStruct(half_shape(g), g.dtype) for g in gs), start, finish)


def _rs_chips(ss):
    n = len(ss)

    def body(*refs):
        ins, outs = refs[:n], refs[n:2 * n]
        send_sems, recv_sems = refs[2 * n:]
        x, y, c = _mesh_pos()
        me = 2 * x + y
        chips = _other_chips(x, y)
        cps = []
        for a in range(n):
            for j, chip in enumerate(chips):
                dst_chip = 2 * chip[0] + chip[1]
                cp = _remote(ins[a].at[dst_chip], outs[a].at[me], send_sems.at[a, j], recv_sems.at[a, j], (*chip, c))
                cp.start()
                cps.append(cp)
        for a in range(n):
            for j, chip in enumerate(chips):
                src = 2 * chip[0] + chip[1]
                _remote(outs[a].at[src], outs[a].at[src], send_sems.at[a, j], recv_sems.at[a, j], (x, y, c)).wait_recv()
        for cp in cps:
            cp.wait_send()

    return list(pl.pallas_call(
        body, name="rs_chips", in_specs=[ANY] * n, out_specs=[ANY] * n,
        out_shape=[jax.ShapeDtypeStruct(s.shape, s.dtype) for s in ss],
        scratch_shapes=[pltpu.SemaphoreType.DMA((n, 3)), pltpu.SemaphoreType.DMA((n, 3))],
    )(*ss))


def _rs_join(fs, axes):
    n = len(fs)

    def whole(f, axis):
        dims = list(f.shape)
        dims[axis] *= 2
        return tuple(dims)

    def body(*refs):
        ins, outs = refs[:n], refs[n:2 * n]
        send_sems, recv_sems = refs[2 * n:]
        x, y, c = _mesh_pos()
        cps = []
        for a in range(n):
            h = ins[a].shape[axes[a]]
            mine = (pl.ds(c * h, h), slice(None)) if axes[a] == 0 else (slice(None), pl.ds(c * h, h))
            cp = _remote(ins[a], outs[a].at[mine], send_sems.at[a], recv_sems.at[a], (x, y, 1 - c))
            cp.start()
            cps.append(cp)
        for cp in cps:
            cp.wait()

    outs = pl.pallas_call(
        body, name="rs_join", in_specs=[ANY] * n, out_specs=[ANY] * n,
        out_shape=[jax.ShapeDtypeStruct(whole(f, ax), f.dtype) for f, ax in zip(fs, axes)],
        scratch_shapes=[pltpu.SemaphoreType.DMA((n,)), pltpu.SemaphoreType.DMA((n,))],
    )(*fs)
    c = lax.axis_index("c")
    return [lax.dynamic_update_slice_in_dim(o, f, c * f.shape[ax], ax) for o, f, ax in zip(outs, fs, axes)]


def _gather_exchange(xs):
    def start(cin, cout, send_sems, recv_sems):
        x, y, c = _mesh_pos()
        me = 2 * x + y
        for a, (src, dst) in enumerate(zip(cin, cout)):
            mine = _half_index(src.shape, c)
            for j, chip in enumerate(_other_chips(x, y)):
                _remote(src.at[mine], dst.at[(me,) + mine], send_sems.at[a, j], recv_sems.at[a, j], (*chip, c)).start()
            _remote(src, dst.at[me], send_sems.at[a, 3], recv_sems.at[a, 3], (x, y, 1 - c)).start()

    def finish(cin, cout, send_sems, recv_sems):
        x, y, c = _mesh_pos()
        for a, dst in enumerate(cout):
            for j, chip in enumerate(_other_chips(x, y)):
                landed = dst.at[(2 * chip[0] + chip[1],) + _half_index(dst.shape[1:], c)]
                _remote(landed, landed, send_sems.at[a, j], recv_sems.at[a, j], (x, y, c)).wait()
            own = dst.at[2 * x + y]
            _remote(own, own, send_sems.at[a, 3], recv_sems.at[a, 3], (x, y, c)).wait()

    return _Exchange(tuple(xs), tuple(jax.ShapeDtypeStruct((N_SHARDS,) + a.shape, a.dtype) for a in xs), start, finish)


def _gather_forward(gs):
    n = len(gs)

    def body(*refs):
        outs = refs[n:2 * n]
        send_sems, recv_sems = refs[2 * n:]
        x, y, c = _mesh_pos()
        chips = _other_chips(x, y)
        cps = []
        for a in range(n):
            for j, chip in enumerate(chips):
                landed = outs[a].at[(2 * chip[0] + chip[1],) + _half_index(outs[a].shape[1:], c)]
                cp = _remote(landed, landed, send_sems.at[a, j], recv_sems.at[a, j], (x, y, 1 - c))
                cp.start()
                cps.append(cp)
        for a in range(n):
            for j, chip in enumerate(chips):
                other = outs[a].at[(2 * chip[0] + chip[1],) + _half_index(outs[a].shape[1:], 1 - c)]
                _remote(other, other, send_sems.at[a, j], recv_sems.at[a, j], (x, y, c)).wait_recv()
        for cp in cps:
            cp.wait_send()

    return list(pl.pallas_call(
        body, name="gather_forward", in_specs=[ANY] * n, out_specs=[ANY] * n,
        out_shape=[jax.ShapeDtypeStruct(g.shape, g.dtype) for g in gs],
        input_output_aliases={i: i for i in range(n)},
        scratch_shapes=[pltpu.SemaphoreType.DMA((n, 3)), pltpu.SemaphoreType.DMA((n, 3))],
    )(*gs))


def _scatter_exchange(ss):
    def start(cin, cout, send_sems, recv_sems):
        x, y, c = _mesh_pos()
        me = 2 * x + y
        for a, (src, dst) in enumerate(zip(cin, cout)):
            for j, chip in enumerate(_other_chips(x, y)):
                _remote(src.at[2 * chip[0] + chip[1]], dst.at[me], send_sems.at[a, j], recv_sems.at[a, j],
                        (*chip, c)).start()

    def finish(cin, cout, send_sems, recv_sems):
        x, y, c = _mesh_pos()
        for a, dst in enumerate(cout):
            for j, chip in enumerate(_other_chips(x, y)):
                slot = dst.at[2 * chip[0] + chip[1]]
                _remote(slot, slot, send_sems.at[a, j], recv_sems.at[a, j], (x, y, c)).wait()

    return _Exchange(tuple(ss), tuple(jax.ShapeDtypeStruct(s.shape, s.dtype) for s in ss), start, finish)


def _add_pairs(gs, lands, name):
    n = len(gs)

    def body(*refs):
        c = lax.axis_index("c")
        for g_ref, l_ref, o_ref in zip(refs[:n], refs[n:2 * n], refs[2 * n:]):
            mine = g_ref[(0,) + _half_index(g_ref.shape[1:], c)]
            o_ref[0] = (mine.astype(F32) + l_ref[0].astype(F32)).astype(BF16)

    whole = [pl.BlockSpec((1,) + g.shape[1:], lambda j: (j, 0, 0)) for g in gs]
    half = [pl.BlockSpec((1,) + l.shape[1:], lambda j: (j, 0, 0)) for l in lands]
    return list(pl.pallas_call(body, name=name, grid=(gs[0].shape[0],), in_specs=whole + half, out_specs=half,
                               out_shape=[jax.ShapeDtypeStruct(l.shape, BF16) for l in lands],
                               compiler_params=_params(("arbitrary",), VMEM_LIMIT))(*gs, *lands))


def _sum_slots(slots, owns, name):
    n = len(slots)

    def body(*refs):
        me = 2 * lax.axis_index("x") + lax.axis_index("y")
        for s_ref, o_ref, out_ref in zip(refs[:n], refs[n:2 * n], refs[2 * n:]):
            acc = jnp.zeros(out_ref.shape, F32)
            for s in range(N_SHARDS):
                acc = acc + jnp.where(me == s, o_ref[s], s_ref[s]).astype(F32)
            out_ref[...] = acc

    def specs(a):
        _, h, c = a.shape
        if h % 32 == 0:
            return (pl.BlockSpec((N_SHARDS, h // 2, c), lambda i: (0, i, 0)), pl.BlockSpec((h // 2, c), lambda i: (i, 0)))
        return (pl.BlockSpec((N_SHARDS, h, c // 2), lambda i: (0, 0, i)), pl.BlockSpec((h, c // 2), lambda i: (0, i)))

    in_specs = [specs(a)[0] for a in slots]
    return list(pl.pallas_call(body, name=name, grid=(2,), in_specs=in_specs + in_specs,
                               out_specs=[specs(a)[1] for a in slots],
                               out_shape=[jax.ShapeDtypeStruct(a.shape[1:], F32) for a in slots],
                               compiler_params=_params(("arbitrary",), VMEM_LIMIT))(*slots, *owns))


def _all_reduce_small(p):
    r = p.shape[0]

    def body(p_ref, o_ref, buf, send_sems, recv_sems):
        x, y, c = _mesh_pos()
        me = 4 * x + 2 * y + c
        buf[me] = p_ref[...]
        cps = []
        k = 0
        for fx in range(2):
            for fy in range(2):
                for fc in range(2):
                    if fx + fy + fc == 0:
                        continue
                    peer = (1 - x if fx else x, 1 - y if fy else y, 1 - c if fc else c)
                    peer_id = 4 * peer[0] + 2 * peer[1] + peer[2]
                    cp = _remote(p_ref, buf.at[me], send_sems.at[k], recv_sems.at[k], peer)
                    cp.start()
                    cps.append((cp, peer_id, k))
                    k += 1
        for cp, peer_id, k in cps:
            _remote(p_ref, buf.at[peer_id], send_sems.at[k], recv_sems.at[k], (x, y, c)).wait_recv()
        for cp, _, _ in cps:
            cp.wait_send()
        acc = buf[0]
        for s in range(1, 8):
            acc = acc + buf[s]
        o_ref[...] = acc

    vm = pl.BlockSpec(memory_space=pltpu.VMEM)
    return pl.pallas_call(
        body, name="all_reduce_small", in_specs=[vm], out_specs=vm,
        out_shape=jax.ShapeDtypeStruct(p.shape, F32),
        scratch_shapes=[pltpu.VMEM((8, r, LANES), F32), pltpu.SemaphoreType.DMA((7,)), pltpu.SemaphoreType.DMA((7,))],
    )(p)


def _adamw(params, name, steps):
    c1 = 1.0 / (1.0 - ADAM_B1 ** ADAM_STEP)
    c2 = 1.0 / (1.0 - ADAM_B2 ** ADAM_STEP)
    n = len(params)

    def body(*refs):
        for a in range(n):
            w_ref, g_ref, m_ref, v_ref = refs[4 * a:4 * a + 4]
            d_ref, nm_ref, nv_ref = refs[4 * n + 3 * a:4 * n + 3 * a + 3]
            gv = g_ref[...]
            nm = ADAM_B1 * m_ref[...] + (1.0 - ADAM_B1) * gv
            nv = ADAM_B2 * v_ref[...] + (1.0 - ADAM_B2) * (gv * gv)
            d_ref[...] = -ADAM_LR * ((nm * c1) / (jnp.sqrt(nv * c2) + ADAM_EPS) + ADAM_WD * w_ref[...])
            nm_ref[...] = nm
            nv_ref[...] = nv

    def spec(shape):
        r, c = shape
        if r % (8 * steps) == 0:
            return pl.BlockSpec((r // steps, c), lambda i: (i, 0))
        assert c % (LANES * steps) == 0
        return pl.BlockSpec((r, c // steps), lambda i: (0, i))

    specs = [spec(w.shape) for w, _, _, _ in params]
    res = pl.pallas_call(
        body, name=name, grid=(steps,),
        in_specs=[s for s in specs for _ in range(4)], out_specs=[s for s in specs for _ in range(3)],
        out_shape=[jax.ShapeDtypeStruct(w.shape, F32) for w, _, _, _ in params for _ in range(3)],
        compiler_params=_params(("arbitrary",), VMEM_LIMIT))(*[a for p4 in params for a in p4])
    return [tuple(res[3 * a:3 * a + 3]) for a in range(n)]


PACK_UNIT = 8 * LANES


def _pack(arrs):
    parts = []
    for a in arrs:
        f = a.reshape(-1).astype(F32)
        parts.append(jnp.pad(f, (0, (-f.shape[0]) % PACK_UNIT)).reshape(-1, LANES))
    return jnp.concatenate(parts, axis=0)


def _unpack(m, shapes):
    outs, row = [], 0
    for s in shapes:
        n = int(np.prod(s))
        rows = -(-n // PACK_UNIT) * 8
        outs.append(m[row:row + rows].reshape(-1)[:n].reshape(s))
        row += rows
    return outs


WEIGHTS = ["ffn1_norm", "ffn1_w_gate", "ffn1_w_up", "ffn1_w_down", "mix_norm", "w_in", "conv_w", "a_log", "dt_bias",
           "gdn_norm_w", "q_norm_w", "k_norm_w", "rel_bias", "w_out", "ffn2_norm", "ffn2_w_gate", "ffn2_w_up",
           "ffn2_w_down", "final_norm"]
BIG = ["ffn1_w_gate", "ffn1_w_up", "ffn1_w_down", "w_in", "w_out", "ffn2_w_gate", "ffn2_w_up", "ffn2_w_down"]
SMALL = [n for n in WEIGHTS if n not in BIG]
COL_SHARDED = ["ffn1_w_gate", "ffn1_w_up", "w_in", "ffn2_w_gate", "ffn2_w_up"]
N_IN_COLS = 3600
TM = 256
TE = 512
ADAM_PIECES = 8
TK = 2048


def kernel(x, ffn1_norm, ffn1_w_gate, ffn1_w_up, ffn1_w_down, mix_norm, w_in, conv_w, a_log, dt_bias, gdn_norm_w, q_norm_w, k_norm_w, rel_bias, w_out, ffn2_norm, ffn2_w_gate, ffn2_w_up, ffn2_w_down, final_norm, loss_target, m_ffn1_norm, m_ffn1_w_gate, m_ffn1_w_up, m_ffn1_w_down, m_mix_norm, m_w_in, m_conv_w, m_a_log, m_dt_bias, m_gdn_norm_w, m_q_norm_w, m_k_norm_w, m_rel_bias, m_w_out, m_ffn2_norm, m_ffn2_w_gate, m_ffn2_w_up, m_ffn2_w_down, m_final_norm, v_ffn1_norm, v_ffn1_w_gate, v_ffn1_w_up, v_ffn1_w_down, v_mix_norm, v_w_in, v_conv_w, v_a_log, v_dt_bias, v_gdn_norm_w, v_q_norm_w, v_k_norm_w, v_rel_bias, v_w_out, v_ffn2_norm, v_ffn2_w_gate, v_ffn2_w_up, v_ffn2_w_down, v_final_norm):
    p = dict(locals())
    xs, target = x[0], loss_target[0]
    t, d = xs.shape
    nc = t // CHUNK
    tk = min(TK, t)
    tkf = min(2 * TK, t)
    me = 2 * lax.axis_index("x") + lax.axis_index("y")

    first = ["ffn1_w_gate", "ffn1_w_up", "ffn1_w_down"]
    later = [n for n in BIG if n not in first] + ["conv_w"]
    local = lambda n, a: a[0].T if n in COL_SHARDED else a[0]
    shards = {n: local(n, p[n]).astype(BF16) for n in BIG}
    shards["conv_w"] = conv_w[0]
    gw = dict(zip(first, _all_gather([shards[n] for n in first])))
    f1 = (gw["ffn1_w_gate"], gw["ffn1_w_up"], gw["ffn1_w_down"])
    (x1, xn1, g1, u1), landed = _ffn_fwd(xs, ffn1_norm, *f1, TE, "ffn1_fwd",
                                         exchange=_gather_exchange([shards[n] for n in later]))
    gw.update(zip(later, _gather_forward(landed)))
    wp = gw["w_in"].reshape(N_IN_COLS, d)
    w_out_full = gw["w_out"].reshape(d, d)
    conv_rows = conv_w.shape[1]
    cw = jnp.pad(gw["conv_w"].reshape(N_SHARDS * conv_rows, CONV_TAPS).T, ((0, 8 - CONV_TAPS), (0, 0)))
    gp = jnp.pad(jnp.stack([a_log.reshape(8), dt_bias.reshape(8)]), ((0, 6), (0, LANES - 8)))
    gdn_w = gdn_norm_w.reshape(1, GDN_DIM)
    qw_t = jnp.tile(q_norm_w.reshape(1, SWA_DIM), (1, SWA_HEADS))
    kw_t = jnp.tile(k_norm_w.reshape(1, SWA_DIM), (1, SWA_HEADS))
    bd = jnp.asarray(np.kron(np.eye(2), np.full((SWA_DIM, SWA_DIM), 1.0 / SWA_DIM)), BF16)
    f2 = (gw["ffn2_w_gate"], gw["ffn2_w_up"], gw["ffn2_w_down"])

    hn, qkva, z, ab, qkvb = _mix_in_fwd(x1, mix_norm, wp, TE)
    qkvc, gb = _gdn_prep_fwd(qkva, cw, ab, gp, TE)
    gbt = jnp.transpose(gb[:, :16].reshape(nc, CHUNK, 16), (0, 2, 1))
    o_f, o_b, gdn_saved = _gdn_fwd(qkvc, gb, gbt)
    oa = _gdn_post_fwd(o_f, o_b, z, gdn_w, TE)
    o_swa, o_swa16, swa_saved = _swa_branch_fwd(qkvb, qw_t, kw_t, rel_bias, bd, TE)
    x2 = _mix_out_fwd(x1, oa, o_swa, w_out_full, TE)
    (dx3, xn2, g2, u2, loss_part, d_final), _ = _ffn_fwd(x2, ffn2_norm, *f2, TE, "ffn2_fwd", head=(final_norm, target))

    def pair_sums(partials, tag):
        return _add_pairs(partials, _rs_pair(partials), f"rs_add_{tag}")

    (dx2, dyh2, dg2, du2, h2, d_nw2), _ = _ffn_bwd_dx(dx3, x2, ffn2_norm, g2, u2, *f2, TM, "ffn2_bwd_dx")
    dwg2 = _matmul_tn(dg2, xn2, tkf, "ffn2_dwg")
    dwu2 = _matmul_tn(du2, xn2, tkf, "ffn2_dwu")
    dwd2 = _matmul_tn(h2, dyh2, tkf, "ffn2_dwd")
    (doa, dob, dx2b), lands_f2 = _mix_out_bwd(dx2, w_out_full, TE, exchange=_pair_exchange([dwg2, dwu2, dwd2]))
    sums_f2 = _add_pairs([dwg2, dwu2, dwd2], lands_f2, "rs_add_a")
    dwo = jnp.concatenate([_matmul_tn(oa, dx2b, tk, "w_out_dw_a")[0], _matmul_tn(o_swa16, dx2b, tk, "w_out_dw_b")[0]],
                          axis=0).reshape(N_SHARDS, d // N_SHARDS, d)
    do_g, dz, d_gdnw = _gdn_post_bwd(doa, o_f, o_b, z, gdn_w, TE)
    (dqkvc, dgates), slots_f2 = _gdn_bwd(qkvc, gb, gbt, do_g, gdn_saved, exchange=_scatter_exchange(sums_f2))
    dqkva, dab, dcw, dgp = _gdn_prep_bwd(qkva, cw, ab, gp, dqkvc, dgates, TM)
    dqkvb, d_qw, d_kw, d_rel = _swa_branch_bwd(dob, o_swa, swa_saved, qkvb, qw_t, kw_t, bd, TE)
    dpieces = (dqkva, dz, dab, dqkvb)
    dwp = [_matmul_tn(dp, hn, tk, f"w_in_dw_{i}")[0] for i, dp in enumerate(dpieces)]
    dw_in = jnp.concatenate([dwp[0], dwp[1], dwp[2][:N_GATE_COLS], dwp[3]], axis=0)
    dw_in = dw_in.reshape(N_SHARDS, N_IN_COLS // N_SHARDS, d)
    sums_mix = pair_sums([dw_in, dwo], "b")
    (dx1, d_mixnw), slots_mix = _mix_in_bwd_dx(dx2, x1, mix_norm, dpieces, wp, TE, exchange=_scatter_exchange(sums_mix))
    (gx, dyh1, dg1, du1, h1, d_nw1), _ = _ffn_bwd_dx(dx1, xs, ffn1_norm, g1, u1, *f1, TM, "ffn1_bwd_dx")
    dwg1 = _matmul_tn(dg1, xn1, tkf, "ffn1_dwg")
    dwu1 = _matmul_tn(du1, xn1, tkf, "ffn1_dwu")
    sums_gu = pair_sums([dwg1, dwu1], "c")
    dwd1, slots_gu = _matmul_tn(h1, dyh1, tkf, "ffn1_dwd", exchange=_scatter_exchange(sums_gu))
    sums_d = pair_sums([dwd1], "d")
    slots = slots_gu + _rs_chips(sums_d) + slots_mix + slots_f2
    sums = sums_gu + sums_d + sums_mix + sums_f2
    halves = _sum_slots(slots[:4], sums[:4], "rs_sum_a") + _sum_slots(slots[4:], sums[4:], "rs_sum_b")
    g_big = dict(zip(BIG, _rs_join(halves, [_split_axis(shards[n].shape) for n in BIG])))

    small_partial = {"ffn1_norm": d_nw1, "mix_norm": d_mixnw, "a_log": dgp[0, 0:8], "dt_bias": dgp[1, 0:8],
                     "gdn_norm_w": d_gdnw, "q_norm_w": d_qw, "k_norm_w": d_kw, "rel_bias": d_rel,
                     "ffn2_norm": d_nw2, "final_norm": d_final, "conv_w": dcw[0:CONV_TAPS].T}
    red = _all_reduce_small(_pack([small_partial[n] for n in SMALL] + [loss_part[0, 0:1]]))
    full_shapes = [p[n].shape if n != "conv_w" else (N_SHARDS * conv_rows, CONV_TAPS) for n in SMALL]
    red_parts = _unpack(red, full_shapes + [(1,)])
    loss = red_parts[-1].reshape(())
    g_small = dict(zip(SMALL, red_parts[:-1]))
    g_small["conv_w"] = lax.dynamic_slice_in_dim(g_small["conv_w"], me * conv_rows, conv_rows, 0).reshape(conv_w.shape)

    grads, deltas, new_m, new_v = {}, {}, {}, {}
    quad = lambda n: (local(n, p[n]), g_big[n], local(n, p["m_" + n]), local(n, p["v_" + n]))
    updates = (_adamw([quad(n) for n in BIG[:4]], "adamw_a", ADAM_PIECES)
               + _adamw([quad(n) for n in BIG[4:]], "adamw_b", ADAM_PIECES))
    for n, (dl, nm, nv) in zip(BIG, updates):
        back = (lambda a: a.T[None]) if n in COL_SHARDED else (lambda a: a[None])
        grads[n], deltas[n], new_m[n], new_v[n] = back(g_big[n]), back(dl), back(nm), back(nv)
    packed = [_pack([src[n] for n in SMALL]) for src in
              ({n: p[n] for n in SMALL}, g_small, {n: p["m_" + n] for n in SMALL}, {n: p["v_" + n] for n in SMALL})]
    small_shapes = [p[n].shape for n in SMALL]
    for dst, arr in zip((deltas, new_m, new_v), _adamw([tuple(packed)], "adamw_small", 1)[0]):
        dst.update(zip(SMALL, _unpack(arr, small_shapes)))
    grads.update(g_small)

    return (loss, gx[None], *[grads[n] for n in WEIGHTS], *[deltas[n] for n in WEIGHTS],
            *[new_m[n] for n in WEIGHTS], *[new_v[n] for n in WEIGHTS])
```

```python
import math
from typing import Callable, NamedTuple

import numpy as np
import jax
import jax.numpy as jnp
from jax import lax
from jax.experimental import pallas as pl
from jax.experimental.pallas import tpu as pltpu

F32 = jnp.float32
BF16 = jnp.bfloat16
MESH = pl.DeviceIdType.MESH

EPS = 1e-6
NEG_BIG = -1e30
GDN_HEADS = 4
GDN_DIM = 128
CHUNK = 64
SWA_HEADS = 8
SWA_DIM = 64
PATTERNS = ((128, 1), (512, 4), (2048, 16))
RADIUS = 64
REL_BUCKETS = 32
REL_MAX_DISTANCE = 1024
CONV_TAPS = 5
N_SHARDS = 4
LANES = 128
VMEM_LIMIT = 56 * 1024 * 1024

ADAM_LR, ADAM_B1, ADAM_B2, ADAM_EPS, ADAM_WD, ADAM_STEP = 0.001, 0.9, 0.999, 1e-08, 0.01, 10


def _params(sem=None, vmem=None):
    return pltpu.CompilerParams(dimension_semantics=sem, vmem_limit_bytes=vmem)


def _resident(shape):
    nd = len(shape)
    return pl.BlockSpec(shape, lambda *_: (0,) * nd, pipeline_mode=pl.Buffered(1))


ANY = pl.BlockSpec(memory_space=pl.ANY)


class _Exchange(NamedTuple):
    arrays: tuple
    out_shape: tuple
    start: Callable
    finish: Callable


def _grid_call(body, name, nsteps, in_specs, out_specs, out_shape, operands, scratch=(), exchange=None):
    params = _params(("arbitrary",), VMEM_LIMIT)
    if exchange is None:
        res = pl.pallas_call(body, name=name, grid=(nsteps,), in_specs=list(in_specs), out_specs=list(out_specs),
                             out_shape=list(out_shape), scratch_shapes=list(scratch), compiler_params=params)(*operands)
        return list(res), []
    n_in, n_out, k, n_scr = len(in_specs), len(out_specs), len(exchange.arrays), len(scratch)

    def wrapped(*refs):
        ins, cin = refs[:n_in], refs[n_in:n_in + k]
        outs, cout = refs[n_in + k:n_in + k + n_out], refs[n_in + k + n_out:n_in + 2 * k + n_out]
        rest = refs[n_in + 2 * k + n_out:]
        scr, (send_sems, recv_sems) = rest[:n_scr], rest[n_scr:]

        @pl.when(pl.program_id(0) == 0)
        def _():
            exchange.start(cin, cout, send_sems, recv_sems)

        body(*ins, *outs, *scr)

        @pl.when(pl.program_id(0) == nsteps - 1)
        def _():
            exchange.finish(cin, cout, send_sems, recv_sems)

    res = pl.pallas_call(
        wrapped, name=name, grid=(nsteps,), in_specs=list(in_specs) + [ANY] * k, out_specs=list(out_specs) + [ANY] * k,
        out_shape=list(out_shape) + list(exchange.out_shape),
        scratch_shapes=list(scratch) + [pltpu.SemaphoreType.DMA((k, 4)), pltpu.SemaphoreType.DMA((k, 4))],
        compiler_params=params)(*operands, *exchange.arrays)
    return list(res[:n_out]), list(res[n_out:])


def _dot(a, b):
    return jnp.dot(a.astype(BF16), b.astype(BF16), preferred_element_type=F32)


def _dot_nt(a, b):
    return lax.dot_general(a.astype(BF16), b.astype(BF16), (((1,), (1,)), ((), ())), preferred_element_type=F32)


def _dot_tn(a, b):
    return lax.dot_general(a.astype(BF16), b.astype(BF16), (((0,), (0,)), ((), ())), preferred_element_type=F32)


def _sigmoid(x):
    return 1.0 / (1.0 + jnp.exp(-x))


def _rstd(xf):
    return lax.rsqrt(jnp.mean(xf * xf, axis=-1, keepdims=True) + EPS)


def _rms_bwd(xf, r, nw, dxn):
    xhat = xf * r
    dxh = dxn * nw
    dx = r * (dxh - xhat * jnp.mean(dxh * xhat, axis=-1, keepdims=True))
    return dx, jnp.sum(dxn * xhat, axis=0, keepdims=True)


def _ffn_fwd(x, nw, wg, wu, wd, tm, name, exchange=None, head=None):
    t, d = x.shape
    nj, fs, _ = wg.shape

    def body(x_ref, nw_ref, wg_ref, wu_ref, wd_ref, *rest):
        if head is None:
            y_ref, xn_ref, g_ref, u_ref = rest
        else:
            fw_ref, t_ref, y_ref, xn_ref, g_ref, u_ref, loss_ref, dfw_ref = rest

            @pl.when(pl.program_id(0) == 0)
            def _():
                loss_ref[...] = jnp.zeros_like(loss_ref)
                dfw_ref[...] = jnp.zeros_like(dfw_ref)

        xf = x_ref[...]
        xn = (xf * _rstd(xf) * nw_ref[...]).astype(BF16)
        xn_ref[...] = xn
        acc = jnp.zeros((tm, d), F32)
        for j in range(nj):
            g = _dot_nt(xn, wg_ref[j])
            u = _dot_nt(xn, wu_ref[j])
            h = (g * _sigmoid(g) * u).astype(BF16)
            acc = acc + jnp.dot(h, wd_ref[j], preferred_element_type=F32)
            g_ref[j] = g.astype(BF16)
            u_ref[j] = u.astype(BF16)
        y = xf + 0.5 * acc
        if head is None:
            y_ref[...] = y
        else:
            r = _rstd(y)
            err = y * r * fw_ref[...] - t_ref[...]
            loss_ref[...] += 0.5 * jnp.sum(jnp.mean(err * err, axis=-1, keepdims=True), axis=0, keepdims=True)
            dy, dfw = _rms_bwd(y, r, fw_ref[...], err * (1.0 / d))
            y_ref[...] = dy
            dfw_ref[...] += dfw

    row = pl.BlockSpec((tm, d), lambda i: (i, 0))
    act = pl.BlockSpec((nj, tm, fs), lambda i: (0, i, 0))
    in_specs = [row, _resident((1, d)), _resident(wg.shape), _resident(wu.shape), _resident(wd.shape)]
    out_specs = [row, row, act, act]
    out_shape = [jax.ShapeDtypeStruct((t, d), F32), jax.ShapeDtypeStruct((t, d), BF16),
                 jax.ShapeDtypeStruct((nj, t, fs), BF16), jax.ShapeDtypeStruct((nj, t, fs), BF16)]
    operands = (x, nw, wg, wu, wd)
    if head is not None:
        in_specs += [_resident((1, d)), row]
        out_specs += [pl.BlockSpec((1, LANES), lambda i: (0, 0)), pl.BlockSpec((1, d), lambda i: (0, 0))]
        out_shape += [jax.ShapeDtypeStruct((1, LANES), F32), jax.ShapeDtypeStruct((1, d), F32)]
        operands += tuple(head)
    return _grid_call(body, name, t // tm, in_specs, out_specs, out_shape, operands, exchange=exchange)


def _ffn_bwd_dx(dy, x, nw, g, u, wg, wu, wd, tm, name, exchange=None):
    t, d = x.shape
    nj, fs, _ = wg.shape

    def body(dy_ref, x_ref, nw_ref, g_ref, u_ref, wg_ref, wu_ref, wd_ref,
             dx_ref, dyh_ref, dg_ref, du_ref, h_ref, dnw_ref):
        @pl.when(pl.program_id(0) == 0)
        def _():
            dnw_ref[...] = jnp.zeros_like(dnw_ref)

        dyv = dy_ref[...]
        dyh = (0.5 * dyv).astype(BF16)
        dyh_ref[...] = dyh
        dxn = jnp.zeros((tm, d), F32)
        dh_next = _dot_nt(dyh, wd_ref[0])
        for j in range(nj):
            dh = dh_next
            gv = g_ref[j].astype(F32)
            uv = u_ref[j].astype(F32)
            sg = _sigmoid(gv)
            si = gv * sg
            dg = (dh * uv * (sg * (1.0 + gv * (1.0 - sg)))).astype(BF16)
            du = (dh * si).astype(BF16)
            if j + 1 < nj:
                dh_next = _dot_nt(dyh, wd_ref[j + 1])
            h_ref[j] = (si * uv).astype(BF16)
            dg_ref[j] = dg
            du_ref[j] = du
            dxn = dxn + _dot(dg, wg_ref[j]) + _dot(du, wu_ref[j])
        xf = x_ref[...]
        dxr, dnw = _rms_bwd(xf, _rstd(xf), nw_ref[...], dxn)
        dx_ref[...] = dyv + dxr
        dnw_ref[...] += dnw

    row = pl.BlockSpec((tm, d), lambda i: (i, 0))
    act = pl.BlockSpec((nj, tm, fs), lambda i: (0, i, 0))
    act_shape = jax.ShapeDtypeStruct((nj, t, fs), BF16)
    return _grid_call(
        body, name, t // tm,
        [row, row, _resident((1, d)), act, act, _resident(wg.shape), _resident(wu.shape), _resident(wd.shape)],
        [row, row, act, act, act, pl.BlockSpec((1, d), lambda i: (0, 0))],
        [jax.ShapeDtypeStruct((t, d), F32), jax.ShapeDtypeStruct((t, d), BF16),
         act_shape, act_shape, act_shape, jax.ShapeDtypeStruct((1, d), F32)],
        (dy, x, nw, g, u, wg, wu, wd), exchange=exchange)


def _matmul_tn(a, b, tk, name, exchange=None):
    a3, b3 = a.ndim == 3, b.ndim == 3
    nj = a.shape[0] if a3 else (b.shape[0] if b3 else 1)
    t, m = a.shape[-2:]
    n = b.shape[-1]
    nt = t // tk

    def body(a_ref, b_ref, o_ref, acc_ref):
        k = pl.program_id(0) % nt

        @pl.when(k == 0)
        def _():
            acc_ref[...] = jnp.zeros_like(acc_ref)

        acc_ref[...] += lax.dot_general(a_ref[...], b_ref[...], (((0,), (0,)), ((), ())),
                                        preferred_element_type=F32)

        @pl.when(k == nt - 1)
        def _():
            o_ref[...] = acc_ref[...].astype(o_ref.dtype)

    a_spec = (pl.BlockSpec((None, tk, m), lambda i: (i // nt, i % nt, 0)) if a3
              else pl.BlockSpec((tk, m), lambda i: (i % nt, 0)))
    b_spec = (pl.BlockSpec((None, tk, n), lambda i: (i // nt, i % nt, 0)) if b3
              else pl.BlockSpec((tk, n), lambda i: (i % nt, 0)))
    (out,), landed = _grid_call(
        body, name, nj * nt, [a_spec, b_spec], [pl.BlockSpec((None, m, n), lambda i: (i // nt, 0, 0))],
        [jax.ShapeDtypeStruct((nj, m, n), BF16)], (a, b), scratch=[pltpu.VMEM((m, n), F32)], exchange=exchange)
    return out if exchange is None else (out, landed)


N_GATE_COLS = 4 * GDN_HEADS
P_QKVA, P_Z, P_AB, P_QKVB = (0, 1536), (1536, 2048), (2048, 2048 + LANES), (2048 + N_GATE_COLS, 3600)
P_PIECES = (P_QKVA, P_Z, P_AB, P_QKVB)


def _mix_in_fwd(x1, nw, wp, tm):
    t, d = x1.shape

    def body(x_ref, nw_ref, w_ref, hn_ref, *outs):
        xf = x_ref[...]
        xn = (xf * _rstd(xf) * nw_ref[...]).astype(BF16)
        hn_ref[...] = xn
        for (a, b), o_ref in zip(P_PIECES, outs):
            o_ref[...] = _dot_nt(xn, w_ref[a:b, :])

    row = pl.BlockSpec((tm, d), lambda i: (i, 0))
    return pl.pallas_call(
        body, name="mix_in_fwd", grid=(t // tm,),
        in_specs=[row, _resident((1, d)), _resident(wp.shape)],
        out_specs=[row] + [pl.BlockSpec((tm, b - a), lambda i: (i, 0)) for a, b in P_PIECES],
        out_shape=[jax.ShapeDtypeStruct((t, d), BF16)]
                  + [jax.ShapeDtypeStruct((t, b - a), F32) for a, b in P_PIECES],
        compiler_params=_params(("arbitrary",), VMEM_LIMIT),
    )(x1, nw, wp)


def _mix_in_bwd_dx(dx, x1, nw, dpieces, wp, tm, exchange=None):
    t, d = x1.shape

    def body(dx_ref, x_ref, nw_ref, p0, p1, p2, p3, w_ref, o_ref, dnw_ref):
        @pl.when(pl.program_id(0) == 0)
        def _():
            dnw_ref[...] = jnp.zeros_like(dnw_ref)

        dh = jnp.zeros((tm, d), F32)
        for (a, b), p_ref in zip(P_PIECES, (p0, p1, p2, p3)):
            dh = dh + _dot(p_ref[...], w_ref[a:b, :])
        xf = x_ref[...]
        dxr, dnw = _rms_bwd(xf, _rstd(xf), nw_ref[...], dh)
        o_ref[...] = dx_ref[...] + dxr
        dnw_ref[...] += dnw

    row = pl.BlockSpec((tm, d), lambda i: (i, 0))
    return _grid_call(
        body, "mix_in_bwd_dx", t // tm,
        [row, row, _resident((1, d))]
        + [pl.BlockSpec((tm, b - a), lambda i: (i, 0)) for a, b in P_PIECES] + [_resident(wp.shape)],
        [row, pl.BlockSpec((1, d), lambda i: (0, 0))],
        [jax.ShapeDtypeStruct((t, d), F32), jax.ShapeDtypeStruct((1, d), F32)],
        (dx, x1, nw, *dpieces, wp), exchange=exchange)


def _mix_out_fwd(x1, oa, ob, w, tm):
    t, d = x1.shape
    half = oa.shape[1]

    def body(x_ref, oa_ref, ob_ref, w_ref, o_ref):
        o_ref[...] = (x_ref[...] + _dot(oa_ref[...], w_ref[0:half, :]) + _dot(ob_ref[...], w_ref[half:2 * half, :]))

    row = pl.BlockSpec((tm, d), lambda i: (i, 0))
    hrow = pl.BlockSpec((tm, half), lambda i: (i, 0))
    return pl.pallas_call(
        body, name="mix_out_fwd", grid=(t // tm,),
        in_specs=[row, hrow, hrow, _resident(w.shape)],
        out_specs=row, out_shape=jax.ShapeDtypeStruct((t, d), F32),
        compiler_params=_params(("arbitrary",), VMEM_LIMIT),
    )(x1, oa, ob, w)


def _mix_out_bwd(dx2, w, tm, exchange=None):
    t, d = dx2.shape
    half = w.shape[0] // 2

    def body(dx_ref, w_ref, doa_ref, dob_ref, dxb_ref):
        dxb = dx_ref[...].astype(BF16)
        dxb_ref[...] = dxb
        doa_ref[...] = _dot_nt(dxb, w_ref[0:half, :])
        dob_ref[...] = _dot_nt(dxb, w_ref[half:2 * half, :])

    row = pl.BlockSpec((tm, d), lambda i: (i, 0))
    hrow = pl.BlockSpec((tm, half), lambda i: (i, 0))
    return _grid_call(
        body, "mix_out_bwd", t // tm, [row, _resident(w.shape)], [hrow, hrow, row],
        [jax.ShapeDtypeStruct((t, half), F32), jax.ShapeDtypeStruct((t, half), F32), jax.ShapeDtypeStruct((t, d), BF16)],
        (dx2, w), exchange=exchange)


HALO = 8


def _halo_row_specs(tr, cols, nrow8):
    per = tr // HALO
    return [pl.BlockSpec((tr, cols), lambda i: (i, 0)),
            pl.BlockSpec((HALO, cols), lambda i: (jnp.maximum(i * per - 1, 0), 0)),
            pl.BlockSpec((HALO, cols), lambda i: (jnp.minimum((i + 1) * per, nrow8 - 1), 0))]


def _fill_window(win_ref, cb, xm, xp, xn, first, last):
    tr = xm.shape[0]
    cols = slice(cb * LANES, (cb + 1) * LANES)
    win_ref[cb, 0:HALO, :] = jnp.where(first, 0.0, xp[:, cols])
    win_ref[cb, HALO:HALO + tr, :] = xm[:, cols]
    win_ref[cb, HALO + tr:HALO + tr + HALO, :] = jnp.where(last, 0.0, xn[:, cols])


def _conv_taps(win_ref, cb, cw_ref, start, rows):
    cols = slice(cb * LANES, (cb + 1) * LANES)
    acc = None
    for j in range(CONV_TAPS):
        term = win_ref[cb, pl.ds(start + j - CONV_TAPS // 2, rows), :] * cw_ref[j:j + 1, cols]
        acc = term if acc is None else acc + term
    return acc


def _softplus(x):
    u = jnp.exp(-jnp.abs(x))
    w = 1.0 + u
    log1p = jnp.where(w == 1.0, u, jnp.log(w) * u / jnp.where(w == 1.0, 1.0, w - 1.0))
    return jnp.maximum(x, 0.0) + log1p


def _gdn_prep_fwd(qkva, cw, ab, gp, tr):
    t, c = qkva.shape
    nt = t // tr
    ncb = c // LANES

    def body(xm, xp, xn, cw_ref, ab_ref, gp_ref, o_ref, gb_ref, xw_ref):
        i = pl.program_id(0)
        first, last = i == 0, i == nt - 1
        for cb in range(ncb):
            cols = slice(cb * LANES, (cb + 1) * LANES)
            _fill_window(xw_ref, cb, xm, xp, xn, first, last)
            pre = _conv_taps(xw_ref, cb, cw_ref, HALO, tr)
            y = pre * _sigmoid(pre)
            if cb < 2 * GDN_HEADS:
                y = y * lax.rsqrt(jnp.sum(y * y, axis=-1, keepdims=True) + EPS)
            if cb < GDN_HEADS:
                y = y * (GDN_DIM ** -0.5)
            o_ref[:, cols] = y
        abv = ab_ref[...]
        lane = lax.broadcasted_iota(jnp.int32, abv.shape, 1)
        g = -jnp.exp(gp_ref[0:1, :]) * _softplus(abv + gp_ref[1:2, :])
        gb_ref[...] = jnp.where(lane < 8, g, jnp.where(lane < 16, _sigmoid(abv), 0.0))

    return pl.pallas_call(
        body, name="gdn_prep_fwd", grid=(nt,),
        in_specs=_halo_row_specs(tr, c, t // HALO)
                 + [_resident(cw.shape), pl.BlockSpec((tr, LANES), lambda i: (i, 0)), _resident(gp.shape)],
        out_specs=[pl.BlockSpec((tr, c), lambda i: (i, 0)), pl.BlockSpec((tr, LANES), lambda i: (i, 0))],
        out_shape=[jax.ShapeDtypeStruct((t, c), F32), jax.ShapeDtypeStruct((t, LANES), F32)],
        scratch_shapes=[pltpu.VMEM((ncb, tr + 2 * HALO, LANES), F32)],
        compiler_params=_params(("arbitrary",), VMEM_LIMIT),
    )(qkva, qkva, qkva, cw, ab, gp)


def _gdn_prep_bwd(qkva, cw, ab, gp, dy, dgates, tr):
    t, c = qkva.shape
    nt = t // tr
    ncb = c // LANES

    ext = HALO // 2
    rows_ext = tr + 2 * ext

    def body(xm, xp, xn, fm, fp, fn, cw_ref, ab_ref, gp_ref, gf_ref, dx_ref, dab_ref, dcw_ref, dgp_ref,
             xw_ref, dyw_ref, dp_ref):
        i = pl.program_id(0)
        first, last = i == 0, i == nt - 1

        @pl.when(first)
        def _():
            dcw_ref[...] = jnp.zeros_like(dcw_ref)
            dgp_ref[...] = jnp.zeros_like(dgp_ref)

        sub8 = lax.broadcasted_iota(jnp.int32, (8, LANES), 0)
        for cb in range(ncb):
            cols = slice(cb * LANES, (cb + 1) * LANES)
            _fill_window(xw_ref, cb, xm, xp, xn, first, last)
            _fill_window(dyw_ref, cb, fm, fp, fn, first, last)
            pre = _conv_taps(xw_ref, cb, cw_ref, HALO - ext, rows_ext)
            dyw = dyw_ref[cb, pl.ds(HALO - ext, rows_ext), :]
            sg = _sigmoid(pre)
            s = pre * sg
            if cb < 2 * GDN_HEADS:
                scale = (GDN_DIM ** -0.5) if cb < GDN_HEADS else 1.0
                r = lax.rsqrt(jnp.sum(s * s, axis=-1, keepdims=True) + EPS)
                dn = dyw * scale
                ds = r * dn - s * (r * r * r) * jnp.sum(dn * s, axis=-1, keepdims=True)
            else:
                ds = dyw
            dp_ref[cb] = ds * (sg * (1.0 + pre * (1.0 - sg)))
            dpre = dp_ref[cb, pl.ds(ext, tr), :]
            dx = None
            dcw = jnp.zeros((8, LANES), F32)
            for j in range(CONV_TAPS):
                off = j - CONV_TAPS // 2
                term = dp_ref[cb, pl.ds(ext - off, tr), :] * cw_ref[j:j + 1, cols]
                dx = term if dx is None else dx + term
                tap = jnp.sum(dpre * xw_ref[cb, pl.ds(HALO + off, tr), :], axis=0, keepdims=True)
                dcw = dcw + jnp.where(sub8 == j, tap, 0.0)
            dx_ref[:, cols] = dx.astype(BF16)
            dcw_ref[:, cols] += dcw

        abv = ab_ref[...]
        dgb = gf_ref[...]
        lane = lax.broadcasted_iota(jnp.int32, abv.shape, 1)
        nea = -jnp.exp(gp_ref[0:1, :])
        xs = abv + gp_ref[1:2, :]
        g = nea * _softplus(xs)
        beta = _sigmoid(abv)
        da = dgb * nea * _sigmoid(xs)
        dab = jnp.where(lane < 8, da, jnp.where(lane < 16, dgb * beta * (1.0 - beta), 0.0))
        dab_ref[...] = dab.astype(BF16)
        keep = lane[0:1, :] < 8
        dalog = jnp.where(keep, jnp.sum(dgb * g, axis=0, keepdims=True), 0.0)
        ddtb = jnp.where(keep, jnp.sum(da, axis=0, keepdims=True), 0.0)
        dgp_ref[...] += jnp.where(sub8 == 0, dalog, 0.0) + jnp.where(sub8 == 1, ddtb, 0.0)

    lrow = pl.BlockSpec((tr, LANES), lambda i: (i, 0))
    halo = _halo_row_specs(tr, c, t // HALO)
    return pl.pallas_call(
        body, name="gdn_prep_bwd", grid=(nt,),
        in_specs=halo + halo + [_resident(cw.shape), lrow, _resident(gp.shape), lrow],
        out_specs=[pl.BlockSpec((tr, c), lambda i: (i, 0)), lrow,
                   pl.BlockSpec(cw.shape, lambda i: (0, 0)), pl.BlockSpec(gp.shape, lambda i: (0, 0))],
        out_shape=[jax.ShapeDtypeStruct((t, c), BF16), jax.ShapeDtypeStruct((t, LANES), BF16),
                   jax.ShapeDtypeStruct(cw.shape, F32), jax.ShapeDtypeStruct(gp.shape, F32)],
        scratch_shapes=[pltpu.VMEM((ncb, tr + 2 * HALO, LANES), F32), pltpu.VMEM((ncb, tr + 2 * HALO, LANES), F32),
                        pltpu.VMEM((ncb, rows_ext, LANES), F32)],
        compiler_params=_params(("arbitrary",), VMEM_LIMIT),
    )(qkva, qkva, qkva, dy, dy, dy, cw, ab, gp, dgates)


def _chunk_masks(lower):
    ii = lax.broadcasted_iota(jnp.int32, (CHUNK, CHUNK), 0)
    jj = lax.broadcasted_iota(jnp.int32, (CHUNK, CHUNK), 1)
    incl = (ii >= jj) if lower else (ii <= jj)
    strict = (ii > jj) if lower else (ii < jj)
    return ii, jj, incl, strict


def _dot3(a, b):
    ah = a.astype(BF16)
    al = (a - ah.astype(F32)).astype(BF16)
    bh = b.astype(BF16)
    bl = (b - bh.astype(F32)).astype(BF16)
    d = lambda u, v: jnp.dot(u, v, preferred_element_type=F32)
    return d(ah, bh) + (d(ah, bl) + d(al, bh))


def _tri_inv_many(lmats, ii, jj):
    m16 = (ii // 16) == (jj // 16)
    m32 = (ii // 32) == (jj // 32)
    eye = jnp.where(ii == jj, 1.0, 0.0)
    l16 = [jnp.where(m16, l, 0.0) for l in lmats]
    p2 = [_dot3(a, a) for a in l16]
    p4 = [_dot3(a, a) for a in p2]
    p8 = [_dot3(a, a) for a in p4]
    xs = [eye - a for a in l16]
    for ps in (p2, p4, p8):
        xs = [x + _dot3(x, p) for x, p in zip(xs, ps)]
    for off in ([jnp.where(m32 & jnp.logical_not(m16), l, 0.0) for l in lmats],
                [jnp.where(m32, 0.0, l) for l in lmats]):
        ys = [_dot3(x, c) for x, c in zip(xs, off)]
        xs = [x - _dot3(y, x) for x, y in zip(xs, ys)]
    return xs


def _col_to_row(col, ii, jj):
    return jnp.sum(jnp.where(ii == jj, col, 0.0), axis=0, keepdims=True)


def _row_to_col(row, ii, jj):
    return jnp.sum(jnp.where(ii == jj, row, 0.0), axis=1, keepdims=True)


def _chain_common(q, k, v, graw_col, graw_row, bcol, masks):
    ii, jj, incl, strict = masks
    inclt = jnp.logical_not(strict)
    gcol = jnp.sum(jnp.where(incl, graw_row, 0.0), axis=1, keepdims=True)
    grow = jnp.sum(jnp.where(inclt, graw_col, 0.0), axis=0, keepdims=True)
    glast = jnp.sum(graw_row, axis=1, keepdims=True)
    decay = jnp.where(incl, jnp.exp(jnp.where(incl, gcol - grow, 0.0)), 0.0)
    kb = k * bcol
    vb = v * bcol
    eg = jnp.exp(gcol)
    ek = jnp.exp(glast - gcol)
    kbg = kb * eg
    amat = _dot_nt(kb, k)
    qk = _dot_nt(q, k)
    return dict(gcol=gcol, glast=glast, decay=decay, kb=kb, vb=vb, eg=eg, ek=ek, kbg=kbg, amat=amat, qk=qk,
                intra=qk * decay, qg=q * eg, kdec=k * ek)


def _gdn_fwd(qkvc, gb, gbt):
    tm, u, w, qg, kd, intra, egl = _gdn_local_fwd(qkvc, gb, gbt)
    o_f, o_b, s_f, s_b, vn_f, vn_b = _gdn_scan_fwd(u, w, qg, kd, intra, egl, qkvc.shape[0])
    return o_f, o_b, dict(tm=tm, w=w, qg=qg, kd=kd, intra=intra, egl=egl, s=(s_f, s_b), vn=(vn_f, vn_b))


N_CHAINS = 2 * GDN_HEADS


LOCAL_CHUNKS = 4


def _load_chains(x_ref, g_ref, gt_ref, cc=0):
    hd = GDN_HEADS * GDN_DIM
    rows = slice(cc * CHUNK, (cc + 1) * CHUNK)
    chains = []
    for d in range(2):
        masks = _chunk_masks(d == 0)
        for h in range(GDN_HEADS):
            ch = d * GDN_HEADS + h
            q = x_ref[rows, h * GDN_DIM:(h + 1) * GDN_DIM]
            k = x_ref[rows, hd + h * GDN_DIM:hd + (h + 1) * GDN_DIM]
            v = x_ref[rows, 2 * hd + h * GDN_DIM:2 * hd + (h + 1) * GDN_DIM]
            bcol = g_ref[rows, 8 + ch:9 + ch]
            cm = _chain_common(q, k, v, g_ref[rows, ch:ch + 1], gt_ref[cc, ch:ch + 1, :], bcol, masks)
            chains.append(dict(cm, q=q, k=k, v=v, bcol=bcol, masks=masks, ch=ch, h=h, cc=cc))
    return chains


def _chain_shape(rows, cols, dtype):
    return lambda nc: jax.ShapeDtypeStruct((nc, N_CHAINS, rows, cols), dtype)


def _gdn_local_fwd(qkvc, gb, gbt):
    t = qkvc.shape[0]
    nc = t // CHUNK
    hd = GDN_HEADS * GDN_DIM

    def body(x_ref, g_ref, gt_ref, t_ref, u_ref, w_ref, qg_ref, kd_ref, in_ref, eg_ref):
        chains = [c for cc in range(LOCAL_CHUNKS) for c in _load_chains(x_ref, g_ref, gt_ref, cc)]
        ii, jj = chains[0]["masks"][0:2]
        tms = _tri_inv_many([jnp.where(c["masks"][3], c["amat"] * c["decay"], 0.0) for c in chains], ii, jj)
        uws = [_dot(tm, jnp.concatenate([c["vb"], c["kbg"]], axis=1)) for tm, c in zip(tms, chains)]
        for c, tm, uw in zip(chains, tms, uws):
            cc, ch = c["cc"], c["ch"]
            t_ref[cc, ch] = tm
            u_ref[cc, ch] = uw[:, :GDN_DIM]
            w_ref[cc, ch] = uw[:, GDN_DIM:].astype(BF16)
            qg_ref[cc, ch] = c["qg"].astype(BF16)
            kd_ref[cc, ch] = c["kdec"].astype(BF16)
            in_ref[cc, ch] = c["intra"].astype(BF16)
            eg_ref[cc, ch:ch + 1, :] = jnp.broadcast_to(jnp.exp(c["glast"]), (1, LANES))

    lc = LOCAL_CHUNKS
    blk = lambda rows, cols: pl.BlockSpec((lc, N_CHAINS, rows, cols), lambda n: (n, 0, 0, 0))
    shapes = [_chain_shape(CHUNK, CHUNK, F32), _chain_shape(CHUNK, GDN_DIM, F32), _chain_shape(CHUNK, GDN_DIM, BF16),
              _chain_shape(CHUNK, GDN_DIM, BF16), _chain_shape(CHUNK, GDN_DIM, BF16), _chain_shape(CHUNK, CHUNK, BF16)]
    return tuple(pl.pallas_call(
        body, name="gdn_local_fwd", grid=(nc // lc,),
        in_specs=[pl.BlockSpec((lc * CHUNK, 3 * hd), lambda n: (n, 0)), pl.BlockSpec((lc * CHUNK, LANES), lambda n: (n, 0)),
                  pl.BlockSpec((lc, 16, CHUNK), lambda n: (n, 0, 0))],
        out_specs=[blk(CHUNK, CHUNK), blk(CHUNK, GDN_DIM), blk(CHUNK, GDN_DIM), blk(CHUNK, GDN_DIM),
                   blk(CHUNK, GDN_DIM), blk(CHUNK, CHUNK), pl.BlockSpec((lc, N_CHAINS, LANES), lambda n: (n, 0, 0))],
        out_shape=[s(nc) for s in shapes] + [jax.ShapeDtypeStruct((nc, N_CHAINS, LANES), F32)],
        compiler_params=_params(("arbitrary",), VMEM_LIMIT),
    )(qkvc, gb, gbt))


SCAN_CHUNKS = 8


def _dir_specs(nc, rev):
    nb = nc // SCAN_CHUNKS

    def spec(d, rows, cols, own=False):
        chunk = (lambda n: n) if (d == 0) != rev else (lambda n: nb - 1 - n)
        blk = 0 if own else d
        if rows is None:
            return pl.BlockSpec((SCAN_CHUNKS, GDN_HEADS if own else N_CHAINS, cols), lambda n: (chunk(n), 0, 0))
        return pl.BlockSpec((SCAN_CHUNKS, GDN_HEADS, rows, cols), lambda n: (chunk(n), blk, 0, 0))

    def rows_spec(d, cols):
        chunk = (lambda n: n) if (d == 0) != rev else (lambda n: nb - 1 - n)
        return pl.BlockSpec((SCAN_CHUNKS * CHUNK, cols), lambda n: (chunk(n), 0))

    def order(d):
        return list(range(SCAN_CHUNKS)) if (d == 0) != rev else list(range(SCAN_CHUNKS - 1, -1, -1))
    return spec, rows_spec, order


def _gdn_scan_fwd(u, w, qg, kd, intra, egl, t):
    nc = t // CHUNK
    hd = GDN_HEADS * GDN_DIM

    def body(*refs):
        ins, outs, state = refs[:12], refs[12:18], refs[18]
        @pl.when(pl.program_id(0) == 0)
        def _():
            state[...] = jnp.zeros_like(state)

        chains = [(d, h) for d in range(2) for h in range(GDN_HEADS)]
        states = [state[ch] for ch in range(N_CHAINS)]
        for step in range(SCAN_CHUNKS):
            at = [order(d)[step] for d in range(2)]
            pick = lambda k, d, h: ins[2 * k + d][at[d], h]
            sbs = [s.astype(BF16) for s in states]
            wo = [_dot(jnp.concatenate([pick(1, d, h), pick(2, d, h)], axis=0), sb) for (d, h), sb in zip(chains, sbs)]
            ws, o1 = [x[:CHUNK] for x in wo], [x[CHUNK:] for x in wo]
            vns = [(pick(0, d, h) - wsb).astype(BF16) for (d, h), wsb in zip(chains, ws)]
            o2 = [_dot(pick(4, d, h), vn) for (d, h), vn in zip(chains, vns)]
            kv = [_dot_tn(pick(3, d, h), vn) for (d, h), vn in zip(chains, vns)]
            new_states = []
            for ch, (d, h) in enumerate(chains):
                outs[d][at[d] * CHUNK:(at[d] + 1) * CHUNK, h * GDN_DIM:(h + 1) * GDN_DIM] = o1[ch] + o2[ch]
                outs[2 + d][at[d], h] = sbs[ch]
                outs[4 + d][at[d], h] = vns[ch]
                new_states.append(states[ch] * ins[10 + d][at[d], ch:ch + 1, :] + kv[ch])
            states = new_states
        for ch in range(N_CHAINS):
            state[ch] = states[ch]

    spec, rows_spec, order = _dir_specs(nc, False)
    pair = lambda rows, cols, own=False: [spec(0, rows, cols, own), spec(1, rows, cols, own)]
    s_shape = jax.ShapeDtypeStruct((nc, GDN_HEADS, GDN_DIM, GDN_DIM), BF16)
    vn_shape = jax.ShapeDtypeStruct((nc, GDN_HEADS, CHUNK, GDN_DIM), BF16)
    return pl.pallas_call(
        body, name="gdn_scan_fwd", grid=(nc // SCAN_CHUNKS,),
        in_specs=(pair(CHUNK, GDN_DIM) + pair(CHUNK, GDN_DIM) + pair(CHUNK, GDN_DIM) + pair(CHUNK, GDN_DIM)
                  + pair(CHUNK, CHUNK) + pair(None, LANES)),
        out_specs=([rows_spec(0, hd), rows_spec(1, hd)] + pair(GDN_DIM, GDN_DIM, True)
                   + pair(CHUNK, GDN_DIM, True)),
        out_shape=[jax.ShapeDtypeStruct((t, hd), F32), jax.ShapeDtypeStruct((t, hd), F32),
                   s_shape, s_shape, vn_shape, vn_shape],
        scratch_shapes=[pltpu.VMEM((N_CHAINS, GDN_DIM, GDN_DIM), F32)],
        compiler_params=_params(("arbitrary",), VMEM_LIMIT),
    )(u, u, w, w, qg, qg, kd, kd, intra, intra, egl, egl)


def _gdn_bwd(qkvc, gb, gbt, do, saved, exchange=None):
    scan = _gdn_scan_bwd(do, saved, qkvc.shape[0])
    return _gdn_local_bwd(qkvc, gb, gbt, do, saved, scan, exchange)


def _gdn_scan_bwd(do, saved, t):
    nc = t // CHUNK
    hd = GDN_HEADS * GDN_DIM

    def body(*refs):
        ins, outs, dstate = refs[:16], refs[16:26], refs[26]
        @pl.when(pl.program_id(0) == 0)
        def _():
            dstate[...] = jnp.zeros_like(dstate)

        chains = [(d, h) for d in range(2) for h in range(GDN_HEADS)]
        dss = [dstate[ch] for ch in range(N_CHAINS)]
        for step in range(SCAN_CHUNKS):
            at = [order(d)[step] for d in range(2)]
            pick = lambda k, d, h: ins[2 * k + d][at[d], h]
            dsbs = [ds.astype(BF16) for ds in dss]
            ss = [pick(1, d, h) for d, h in chains]
            sbs = ss
            dos = [ins[d][at[d] * CHUNK:(at[d] + 1) * CHUNK, h * GDN_DIM:(h + 1) * GDN_DIM].astype(BF16)
                   for d, h in chains]
            dv1 = [_dot_tn(pick(5, d, h), dov) for (d, h), dov in zip(chains, dos)]
            dv2 = [_dot(pick(4, d, h), dsb) for (d, h), dsb in zip(chains, dsbs)]
            ds1 = [_dot_tn(pick(3, d, h), dov) for (d, h), dov in zip(chains, dos)]
            dkds = [_dot_nt(pick(6, d, h), dsb) for (d, h), dsb in zip(chains, dsbs)]
            dqgs = [_dot_nt(dov, sb) for dov, sb in zip(dos, sbs)]
            dvns = [(a + b).astype(BF16) for a, b in zip(dv1, dv2)]
            ds2 = [_dot_tn(pick(2, d, h), dvn) for (d, h), dvn in zip(chains, dvns)]
            dws = [_dot_nt(dvn, sb) for dvn, sb in zip(dvns, sbs)]
            new_dss = []
            for ch, (d, h) in enumerate(chains):
                egl = ins[14 + d][at[d], ch:ch + 1, :]
                outs[d][at[d], h] = dvns[ch]
                outs[2 + d][at[d], h] = (-dws[ch]).astype(BF16)
                outs[4 + d][at[d], h] = dqgs[ch]
                outs[6 + d][at[d], h] = dkds[ch]
                outs[8 + d][at[d], h:h + 1, :] = egl * jnp.sum(jnp.sum(ss[ch].astype(F32) * dss[ch], axis=1, keepdims=True),
                                                               axis=0, keepdims=True)
                new_dss.append(ds1[ch] + egl * dss[ch] - ds2[ch])
            dss = new_dss
        for ch in range(N_CHAINS):
            dstate[ch] = dss[ch]

    spec, rows_spec, order = _dir_specs(nc, True)
    pair = lambda rows, cols, own=False: [spec(0, rows, cols, own), spec(1, rows, cols, own)]
    s_f, s_b = saved["s"]
    vn_f, vn_b = saved["vn"]
    w, qg, kd, intra, egl = saved["w"], saved["qg"], saved["kd"], saved["intra"], saved["egl"]
    own = lambda rows, cols, dtype: jax.ShapeDtypeStruct((nc, GDN_HEADS, rows, cols), dtype)
    row_shape = jax.ShapeDtypeStruct((nc, GDN_HEADS, LANES), F32)
    return pl.pallas_call(
        body, name="gdn_scan_bwd", grid=(nc // SCAN_CHUNKS,),
        in_specs=([rows_spec(0, hd), rows_spec(1, hd)] + pair(GDN_DIM, GDN_DIM, True) + pair(CHUNK, GDN_DIM)
                  + pair(CHUNK, GDN_DIM) + pair(CHUNK, GDN_DIM) + pair(CHUNK, CHUNK) + pair(CHUNK, GDN_DIM, True)
                  + pair(None, LANES)),
        out_specs=(pair(CHUNK, GDN_DIM, True) + pair(CHUNK, GDN_DIM, True) + pair(CHUNK, GDN_DIM, True)
                   + pair(CHUNK, GDN_DIM, True) + pair(None, LANES, True)),
        out_shape=[own(CHUNK, GDN_DIM, BF16)] * 4 + [own(CHUNK, GDN_DIM, F32)] * 4 + [row_shape] * 2,
        scratch_shapes=[pltpu.VMEM((N_CHAINS, GDN_DIM, GDN_DIM), F32)],
        compiler_params=_params(("arbitrary",), VMEM_LIMIT),
    )(do, do, s_f, s_b, w, w, qg, qg, kd, kd, intra, intra, vn_f, vn_b, egl, egl)


def _dot3_nt(a, b):
    ah = a.astype(BF16)
    al = (a - ah.astype(F32)).astype(BF16)
    bh = b.astype(BF16)
    bl = (b - bh.astype(F32)).astype(BF16)
    return _dot_nt(ah, bh) + (_dot_nt(ah, bl) + _dot_nt(al, bh))


def _dot3_tn(a, b):
    ah = a.astype(BF16)
    al = (a - ah.astype(F32)).astype(BF16)
    bh = b.astype(BF16)
    bl = (b - bh.astype(F32)).astype(BF16)
    return _dot_tn(ah, bh) + (_dot_tn(ah, bl) + _dot_tn(al, bh))


def _gdn_local_bwd(qkvc, gb, gbt, do, saved, scan, exchange=None):
    t = qkvc.shape[0]
    nc = t // CHUNK
    hd = GDN_HEADS * GDN_DIM

    def body(*refs):
        x_ref, g_ref, gt_ref, do_ref, t_ref = refs[:5]
        per_dir = refs[5:17]
        dx_ref, dg_ref = refs[17:]
        chains = [c for cc in range(LOCAL_CHUNKS) for c in _load_chains(x_ref, g_ref, gt_ref, cc)]
        lane = lax.broadcasted_iota(jnp.int32, (CHUNK, LANES), 1)
        dgates = [jnp.zeros((CHUNK, LANES), F32) for _ in range(LOCAL_CHUNKS)]
        for c in chains:
            d = c["ch"] // GDN_HEADS
            vn_ref, dvn_ref, dw_ref, dqg_ref, dkd_ref, dgl_ref = per_dir[d::2]
            h, cc = c["h"], c["cc"]
            rows = slice(cc * CHUNK, (cc + 1) * CHUNK)
            c.update(tm=t_ref[cc, c["ch"]], dov=do_ref[rows, h * GDN_DIM:(h + 1) * GDN_DIM], vnew=vn_ref[cc, h],
                     dvnew=dvn_ref[cc, h], dw=dw_ref[cc, h], dqg=dqg_ref[cc, h], dkdec=dkd_ref[cc, h],
                     dglast=dgl_ref[cc, h:h + 1, 0:1])
        dintras = [_dot_nt(c["dov"], c["vnew"]) for c in chains]
        dts = [_dot_nt(c["dvnew"], c["vb"]) + _dot_nt(c["dw"], c["kbg"]) for c in chains]
        dvbs = [_dot_tn(c["tm"], c["dvnew"]) for c in chains]
        dkbgs = [_dot_tn(c["tm"], c["dw"]) for c in chains]
        tdts = [_dot3_nt(dt, c["tm"]) for dt, c in zip(dts, chains)]
        dls = [jnp.where(c["masks"][3], -_dot3_tn(c["tm"], tdt), 0.0) for tdt, c in zip(tdts, chains)]
        das = [dl * c["decay"] for dl, c in zip(dls, chains)]
        dqks = [jnp.where(c["masks"][2], di, 0.0) * c["decay"] for di, c in zip(dintras, chains)]
        dkb1 = [_dot(da, c["k"]) for da, c in zip(das, chains)]
        dk1 = [_dot_tn(da, c["kb"]) for da, c in zip(das, chains)]
        dk2 = [_dot_tn(dqk, c["q"]) for dqk, c in zip(dqks, chains)]
        dq1 = [_dot(dqk, c["k"]) for dqk, c in zip(dqks, chains)]
        grads, mms, p_gs, p_betas, p_kds = [], [], [], [], []
        for n, c in enumerate(chains):
            incl = c["masks"][2]
            dkb = dkb1[n] + dkbgs[n] * c["eg"]
            kd = c["dkdec"] * c["kdec"]
            mms.append((dls[n] * c["amat"] + jnp.where(incl, dintras[n], 0.0) * c["qk"]) * c["decay"])
            p_gs.append(c["dqg"] * c["qg"] - kd + dkbgs[n] * c["kbg"])
            p_betas.append(dkb * c["k"] + dvbs[n] * c["v"])
            p_kds.append(kd)
            grads.append((dq1[n] + c["dqg"] * c["eg"],
                          dk1[n] + dk2[n] + c["dkdec"] * c["ek"] + dkb * c["bcol"],
                          dvbs[n] * c["bcol"]))
        row_sums = [jnp.sum(mm, axis=1, keepdims=True) for mm in mms]
        col_sums = [jnp.sum(mm, axis=0, keepdims=True) for mm in mms]
        g_sums = [jnp.sum(pg, axis=1, keepdims=True) for pg in p_gs]
        dbetas = [jnp.sum(pb, axis=1, keepdims=True) for pb in p_betas]
        kd_tots = [jnp.sum(jnp.sum(pk, axis=1, keepdims=True), axis=0, keepdims=True) for pk in p_kds]
        dgcs = [rs - _row_to_col(cs, *c["masks"][0:2]) + gs for rs, cs, gs, c in zip(row_sums, col_sums, g_sums, chains)]
        dgrs = [_col_to_row(dgc, *c["masks"][0:2]) for dgc, c in zip(dgcs, chains)]
        draws = [jnp.sum(jnp.where(jnp.logical_not(c["masks"][3]), dgr, 0.0), axis=1, keepdims=True) + c["dglast"] + kt
                 for dgr, kt, c in zip(dgrs, kd_tots, chains)]
        for c, draw, dbeta in zip(chains, draws, dbetas):
            ch = c["ch"]
            dgates[c["cc"]] = dgates[c["cc"]] + jnp.where(lane == ch, draw, 0.0) + jnp.where(lane == 8 + ch, dbeta, 0.0)
        for cc in range(LOCAL_CHUNKS):
            rows = slice(cc * CHUNK, (cc + 1) * CHUNK)
            for h in range(GDN_HEADS):
                for part in range(3):
                    cols = slice(part * hd + h * GDN_DIM, part * hd + (h + 1) * GDN_DIM)
                    dx_ref[rows, cols] = grads[cc * N_CHAINS + h][part] + grads[cc * N_CHAINS + GDN_HEADS + h][part]
            dg_ref[rows, :] = dgates[cc]

    lc = LOCAL_CHUNKS
    all8 = lambda rows, cols: pl.BlockSpec((lc, N_CHAINS, rows, cols), lambda n: (n, 0, 0, 0))
    own4 = lambda rows, cols: pl.BlockSpec((lc, GDN_HEADS, rows, cols), lambda n: (n, 0, 0, 0))
    row4 = pl.BlockSpec((lc, GDN_HEADS, LANES), lambda n: (n, 0, 0))
    vn_f, vn_b = saved["vn"]
    dvn_f, dvn_b, dw_f, dw_b, dqg_f, dqg_b, dkd_f, dkd_b, dgl_f, dgl_b = scan
    return _grid_call(
        body, "gdn_local_bwd", nc // lc,
        [pl.BlockSpec((lc * CHUNK, 3 * hd), lambda n: (n, 0)), pl.BlockSpec((lc * CHUNK, LANES), lambda n: (n, 0)),
         pl.BlockSpec((lc, 16, CHUNK), lambda n: (n, 0, 0)), pl.BlockSpec((lc * CHUNK, hd), lambda n: (n, 0)),
         all8(CHUNK, CHUNK)] + [own4(CHUNK, GDN_DIM)] * 10 + [row4, row4],
        [pl.BlockSpec((lc * CHUNK, 3 * hd), lambda n: (n, 0)), pl.BlockSpec((lc * CHUNK, LANES), lambda n: (n, 0))],
        [jax.ShapeDtypeStruct((t, 3 * hd), F32), jax.ShapeDtypeStruct((t, LANES), F32)],
        (qkvc, gb, gbt, do, saved["tm"], vn_f, vn_b, dvn_f, dvn_b, dw_f, dw_b, dqg_f, dqg_b, dkd_f, dkd_b, dgl_f, dgl_b),
        exchange=exchange)


def _gdn_post_fwd(of, ob, z, gw, tm):
    t, hd = of.shape

    def body(of_ref, ob_ref, z_ref, w_ref, o_ref):
        for h in range(GDN_HEADS):
            cols = slice(h * GDN_DIM, (h + 1) * GDN_DIM)
            o = of_ref[:, cols] + ob_ref[:, cols]
            zv = z_ref[:, cols]
            o_ref[:, cols] = (o * _rstd(o) * w_ref[...] * (zv * _sigmoid(zv))).astype(BF16)

    row = pl.BlockSpec((tm, hd), lambda i: (i, 0))
    return pl.pallas_call(
        body, name="gdn_post_fwd", grid=(t // tm,),
        in_specs=[row, row, row, _resident((1, GDN_DIM))],
        out_specs=row, out_shape=jax.ShapeDtypeStruct((t, hd), BF16),
        compiler_params=_params(("arbitrary",), VMEM_LIMIT),
    )(of, ob, z, gw)


def _gdn_post_bwd(doa, of, ob, z, gw, tm):
    t, hd = of.shape

    def body(d_ref, of_ref, ob_ref, z_ref, w_ref, do_ref, dz_ref, dw_ref):
        @pl.when(pl.program_id(0) == 0)
        def _():
            dw_ref[...] = jnp.zeros_like(dw_ref)

        dw = jnp.zeros((1, GDN_DIM), F32)
        for h in range(GDN_HEADS):
            cols = slice(h * GDN_DIM, (h + 1) * GDN_DIM)
            o = of_ref[:, cols] + ob_ref[:, cols]
            zv = z_ref[:, cols]
            dv = d_ref[:, cols]
            r = _rstd(o)
            sg = _sigmoid(zv)
            on = o * r * w_ref[...]
            dz_ref[:, cols] = (dv * on * (sg * (1.0 + zv * (1.0 - sg)))).astype(BF16)
            dxr, dwh = _rms_bwd(o, r, w_ref[...], dv * (zv * sg))
            do_ref[:, cols] = dxr
            dw = dw + dwh
        dw_ref[...] += dw

    row = pl.BlockSpec((tm, hd), lambda i: (i, 0))
    return pl.pallas_call(
        body, name="gdn_post_bwd", grid=(t // tm,),
        in_specs=[row, row, row, row, _resident((1, GDN_DIM))],
        out_specs=[row, row, pl.BlockSpec((1, GDN_DIM), lambda i: (0, 0))],
        out_shape=[jax.ShapeDtypeStruct((t, hd), F32), jax.ShapeDtypeStruct((t, hd), BF16),
                   jax.ShapeDtypeStruct((1, GDN_DIM), F32)],
        compiler_params=_params(("arbitrary",), VMEM_LIMIT),
    )(doa, of, ob, z, gw)


SWA_W = SWA_HEADS * SWA_DIM
QBLK = 128
KWIN = QBLK + 2 * RADIUS
WIN_OFFSETS = (0, RADIUS, 2 * RADIUS)


def _t5_bucket(rel):
    nb = REL_BUCKETS // 2
    bucket = (rel > 0).astype(np.int32) * nb
    n = np.abs(rel)
    max_exact = nb // 2
    large = max_exact + (np.log(np.maximum(n, 1) / max_exact)
                         / math.log(REL_MAX_DISTANCE / max_exact) * (nb - max_exact)).astype(np.int32)
    large = np.minimum(large, nb - 1)
    return (bucket + np.where(n < max_exact, n, large)).astype(np.int32)


def _band_tables(dilation):
    a = np.arange(QBLK)
    b = np.arange(KWIN)
    rel = np.stack([b[None, :] - w0 - a[:, None] for w0 in WIN_OFFSETS])
    return np.where(np.abs(rel) <= RADIUS, _t5_bucket(rel * dilation), -1).astype(np.int32)


BAND_CELLS = len(WIN_OFFSETS) * QBLK * KWIN
BIAS_TILE = BAND_CELLS // 3


def _band_index():
    return jnp.asarray(np.concatenate([_band_tables(d).reshape(-1) for _, d in PATTERNS])[None, :])


def _onehot(idx, dtype):
    return (lax.broadcasted_iota(jnp.int32, (REL_BUCKETS, idx.shape[1]), 0) == idx).astype(dtype)


def _bias_tables(rel_bias, idx, tk):
    n = idx.shape[1]

    def body(rb_ref, i_ref, o_ref):
        iv = i_ref[...]
        oh = _onehot(iv, BF16)
        rest, acc = rb_ref[...], None
        for _ in range(3):
            piece = rest.astype(BF16)
            part = jnp.dot(piece, oh, preferred_element_type=F32)
            acc = part if acc is None else acc + part
            rest = rest - piece.astype(F32)
        o_ref[...] = jnp.where(iv < 0, NEG_BIG, acc)

    return pl.pallas_call(
        body, name="bias_tables", grid=(n // tk,),
        in_specs=[_resident((SWA_HEADS, REL_BUCKETS)), pl.BlockSpec((1, tk), lambda k: (0, k))],
        out_specs=pl.BlockSpec((SWA_HEADS, tk), lambda k: (0, k)),
        out_shape=jax.ShapeDtypeStruct((SWA_HEADS, n), F32),
        compiler_params=_params(("arbitrary",), VMEM_LIMIT),
    )(rel_bias.T, idx)


def _head_mean(x2, bd_ref):
    bd = bd_ref[...]
    rest, acc = x2, None
    for _ in range(3):
        piece = rest.astype(BF16)
        part = jnp.dot(piece, bd, preferred_element_type=F32)
        acc = part if acc is None else acc + part
        rest = rest - piece.astype(F32)
    return acc


VIEW_DILATIONS = tuple(d for _, d in PATTERNS if d > 1)


def _view_spec(tm, d):
    return pl.BlockSpec((tm // d, d * SWA_W), lambda i: (i, 0))


def _view_shape(t, d, dtype):
    return jax.ShapeDtypeStruct((t // d, d * SWA_W), dtype)


N_GROUPS = SWA_W // LANES


def _to_view(src_ref, idx, dst_ref, d, rows):
    for r in range(d):
        for g in range(N_GROUPS):
            cols = slice(r * SWA_W + g * LANES, r * SWA_W + (g + 1) * LANES)
            dst_ref[:, cols] = src_ref[idx, g, pl.ds(r, rows // d, stride=d), :].astype(dst_ref.dtype)


def _from_view(src_ref, dst_ref, idx, d, rows):
    for r in range(d):
        for g in range(N_GROUPS):
            cols = slice(r * SWA_W + g * LANES, r * SWA_W + (g + 1) * LANES)
            dst_ref[idx, g, pl.ds(r, rows // d, stride=d), :] = src_ref[:, cols]


def _swa_prep_fwd(qkvb, qw, kw, bd, tm):
    t = qkvb.shape[0]

    def body(x_ref, qw_ref, kw_ref, bd_ref, *rest):
        outs, sc = rest[:-1], rest[-1]
        for gidx in range(N_GROUPS):
            cols = slice(gidx * LANES, (gidx + 1) * LANES)
            xq = x_ref[:, cols]
            sc[0, gidx] = xq * lax.rsqrt(_head_mean(xq * xq, bd_ref) + EPS) * qw_ref[:, cols] * (SWA_DIM ** -0.5)
            xk = x_ref[:, SWA_W + gidx * LANES:SWA_W + (gidx + 1) * LANES]
            sc[1, gidx] = xk * lax.rsqrt(_head_mean(xk * xk, bd_ref) + EPS) * kw_ref[:, cols]
            sc[2, gidx] = x_ref[:, 2 * SWA_W + gidx * LANES:2 * SWA_W + (gidx + 1) * LANES]
            for i in range(3):
                outs[i][:, cols] = sc[i, gidx].astype(BF16)
        for i in range(3):
            for n, d in enumerate(VIEW_DILATIONS):
                _to_view(sc, i, outs[3 * (n + 1) + i], d, tm)

    return pl.pallas_call(
        body, name="swa_prep_fwd", grid=(t // tm,),
        in_specs=[pl.BlockSpec((tm, 3 * SWA_W), lambda i: (i, 0)), _resident((1, SWA_W)), _resident((1, SWA_W)),
                  _resident((LANES, LANES))],
        out_specs=[_view_spec(tm, d) for d in (1,) + VIEW_DILATIONS for _ in range(3)],
        out_shape=[_view_shape(t, d, BF16) for d in (1,) + VIEW_DILATIONS for _ in range(3)],
        scratch_shapes=[pltpu.VMEM((3, N_GROUPS, tm, LANES), F32)],
        compiler_params=_params(("arbitrary",), VMEM_LIMIT),
    )(qkvb, qw, kw, bd)


def _swa_prep_bwd(qkvb, qw, kw, bd, grads, tm):
    t = qkvb.shape[0]

    def body(x_ref, qw_ref, kw_ref, bd_ref, *rest):
        parts, (dx_ref, dqw_ref, dkw_ref, sc) = rest[:9], rest[9:]
        @pl.when(pl.program_id(0) == 0)
        def _():
            dqw_ref[...] = jnp.zeros_like(dqw_ref)
            dkw_ref[...] = jnp.zeros_like(dkw_ref)

        for i in range(3):
            for n, d in enumerate(VIEW_DILATIONS):
                _from_view(parts[3 * (n + 1) + i], sc, 2 * i + n, d, tm)
        for gidx in range(N_GROUPS):
            cols = slice(gidx * LANES, (gidx + 1) * LANES)
            for i, base, w_ref, dw_ref, scale in ((0, 0, qw_ref, dqw_ref, SWA_DIM ** -0.5),
                                                  (1, SWA_W, kw_ref, dkw_ref, 1.0)):
                xv = x_ref[:, base + gidx * LANES:base + (gidx + 1) * LANES]
                dy = (parts[i][:, cols] + sc[2 * i, gidx] + sc[2 * i + 1, gidx]) * scale
                r = lax.rsqrt(_head_mean(xv * xv, bd_ref) + EPS)
                xhat = xv * r
                dxh = dy * w_ref[:, cols]
                dx = r * (dxh - xhat * _head_mean(dxh * xhat, bd_ref))
                dx_ref[:, base + gidx * LANES:base + (gidx + 1) * LANES] = dx.astype(BF16)
                dw_ref[:, cols] += jnp.sum(dy * xhat, axis=0, keepdims=True)
            dx_ref[:, 2 * SWA_W + gidx * LANES:2 * SWA_W + (gidx + 1) * LANES] = (
                parts[2][:, cols] + sc[4, gidx] + sc[5, gidx]).astype(BF16)

    wrow = pl.BlockSpec((1, SWA_W), lambda i: (0, 0))
    return pl.pallas_call(
        body, name="swa_prep_bwd", grid=(t // tm,),
        in_specs=[pl.BlockSpec((tm, 3 * SWA_W), lambda i: (i, 0)), _resident((1, SWA_W)), _resident((1, SWA_W)),
                  _resident((LANES, LANES))] + [_view_spec(tm, d) for d in (1,) + VIEW_DILATIONS for _ in range(3)],
        out_specs=[pl.BlockSpec((tm, 3 * SWA_W), lambda i: (i, 0)), wrow, wrow],
        out_shape=[jax.ShapeDtypeStruct((t, 3 * SWA_W), BF16), jax.ShapeDtypeStruct((1, SWA_W), F32),
                   jax.ShapeDtypeStruct((1, SWA_W), F32)],
        scratch_shapes=[pltpu.VMEM((6, N_GROUPS, tm, LANES), F32)],
        compiler_params=_params(("arbitrary",), VMEM_LIMIT),
    )(qkvb, qw, kw, bd, *grads)


def _aligned(v, m):
    return v if isinstance(v, int) else pl.multiple_of(v, m)


BAND_GROUP = 2


def _band_loop(nsub, length, step, group=BAND_GROUP):
    step([(0, 0)], 0)
    if nsub > 2:
        assert (nsub - 2) % group == 0

        def inner(i, carry):
            s0 = 1 + i * group
            step([(s0 + e, pl.multiple_of((s0 + e) * QBLK - RADIUS, RADIUS)) for e in range(group)], 1)
            return carry
        lax.fori_loop(0, (nsub - 2) // group, inner, 0)
    step([(nsub - 1, length - KWIN)], 2)


def _head_select(lane, a0, a1):
    return jnp.where(lane < SWA_DIM, a0, a1)


def _swa_fwd(qv, kv, vv, bias, dilation, name):
    length = qv.shape[0]
    nsub = length // QBLK
    assert nsub >= 2 and length % QBLK == 0

    def body(q_ref, k_ref, v_ref, b_ref, o_ref, l_ref):
        lane = lax.broadcasted_iota(jnp.int32, (QBLK, LANES), 1)

        def step(blocks, var):
            items = []
            for s, ws in blocks:
                rows = pl.ds(_aligned(s * QBLK, QBLK), QBLK)
                q, kk, vw = q_ref[rows, :], k_ref[pl.ds(ws, KWIN), :], v_ref[pl.ds(ws, KWIN), :]
                for hh in range(2):
                    items.append((hh, jnp.where((lane < SWA_DIM) == (hh == 0), q, jnp.zeros_like(q)), kk, vw))
            lgs = [_dot_nt(qh, kk) + b_ref[hh, var] for hh, qh, kk, _ in items]
            ms = [jnp.max(lg, axis=-1, keepdims=True) for lg in lgs]
            ps = [jnp.exp(lg - m) for lg, m in zip(lgs, ms)]
            dens = [jnp.sum(p, axis=-1, keepdims=True) for p in ps]
            pvs = [_dot(p, it[3]) for p, it in zip(ps, items)]
            for n, (s, _) in enumerate(blocks):
                rows = pl.ds(_aligned(s * QBLK, QBLK), QBLK)
                o0, o1 = (pvs[2 * n + hh] / dens[2 * n + hh] for hh in range(2))
                l0, l1 = (ms[2 * n + hh] + jnp.log(dens[2 * n + hh]) for hh in range(2))
                o_ref[rows, :] = _head_select(lane, o0, o1)
                l_ref[rows, :] = _head_select(lane, l0, l1)

        _band_loop(nsub, length, step)

    blk = pl.BlockSpec((length, LANES), lambda hp, r: (0, r * (SWA_W // LANES) + hp))
    shp = jax.ShapeDtypeStruct(qv.shape, F32)
    return pl.pallas_call(
        body, name=name, grid=(SWA_W // LANES, dilation),
        in_specs=[blk, blk, blk, pl.BlockSpec((2, 3, QBLK, KWIN), lambda hp, r: (hp, 0, 0, 0))],
        out_specs=[blk, blk], out_shape=[shp, shp],
        compiler_params=_params(("arbitrary", "arbitrary"), VMEM_LIMIT),
    )(qv, kv, vv, bias)


def _swa_combine(os_, ls_, tm):
    t = os_[0].shape[0]

    def body(o0, o1, o2, l0, l1, l2, o_ref, ob_ref, la_ref, lb_ref, lc_ref, sc):
        for n, d in enumerate(VIEW_DILATIONS):
            _from_view((o1, o2)[n], sc, n, d, tm)
            _from_view((l1, l2)[n], sc, 2 + n, d, tm)
        for g in range(N_GROUPS):
            cols = slice(g * LANES, (g + 1) * LANES)
            la, lb, lc = l0[:, cols], sc[2, g], sc[3, g]
            m = jnp.maximum(jnp.maximum(la, lb), lc)
            tot = m + jnp.log(jnp.exp(la - m) + jnp.exp(lb - m) + jnp.exp(lc - m))
            o = jnp.exp(la - tot) * o0[:, cols] + jnp.exp(lb - tot) * sc[0, g] + jnp.exp(lc - tot) * sc[1, g]
            o_ref[:, cols] = o
            ob_ref[:, cols] = o.astype(BF16)
            la_ref[:, cols] = tot
            sc[4, g] = tot
        for n, d in enumerate(VIEW_DILATIONS):
            _to_view(sc, 4, (lb_ref, lc_ref)[n], d, tm)

    specs = [_view_spec(tm, d) for d in (1,) + VIEW_DILATIONS]
    return pl.pallas_call(
        body, name="swa_combine", grid=(t // tm,), in_specs=specs + specs, out_specs=[specs[0], specs[0]] + specs,
        out_shape=[jax.ShapeDtypeStruct((t, SWA_W), F32), jax.ShapeDtypeStruct((t, SWA_W), BF16)]
                  + [_view_shape(t, d, F32) for d in (1,) + VIEW_DILATIONS],
        scratch_shapes=[pltpu.VMEM((5, N_GROUPS, tm, LANES), F32)],
        compiler_params=_params(("arbitrary",), VMEM_LIMIT),
    )(*os_, *ls_)


def _swa_bwd_prep(do, o, bd, tm):
    t = do.shape[0]

    def body(d_ref, o_ref, bd_ref, dd1, dd4, dd16, db1, db4, db16, sc):
        for gidx in range(N_GROUPS):
            cols = slice(gidx * LANES, (gidx + 1) * LANES)
            dv = d_ref[:, cols]
            dd = _head_mean(dv * o_ref[:, cols], bd_ref) * float(SWA_DIM)
            sc[0, gidx] = dd
            sc[1, gidx] = dv
            dd1[:, cols] = dd
            db1[:, cols] = dv.astype(BF16)
        for n, d in enumerate(VIEW_DILATIONS):
            _to_view(sc, 0, (dd4, dd16)[n], d, tm)
            _to_view(sc, 1, (db4, db16)[n], d, tm)

    specs = [_view_spec(tm, d) for d in (1,) + VIEW_DILATIONS]
    return pl.pallas_call(
        body, name="swa_bwd_prep", grid=(t // tm,), in_specs=[specs[0], specs[0], _resident((LANES, LANES))],
        out_specs=specs + specs,
        out_shape=[_view_shape(t, d, F32) for d in (1,) + VIEW_DILATIONS]
                  + [_view_shape(t, d, BF16) for d in (1,) + VIEW_DILATIONS],
        scratch_shapes=[pltpu.VMEM((2, N_GROUPS, tm, LANES), F32)],
        compiler_params=_params(("arbitrary",), VMEM_LIMIT),
    )(do, o, bd)


def _swa_bwd(qv, kv, vv, dov, lv, ddv, bias_a, dilation, name):
    length = qv.shape[0]
    nsub = length // QBLK
    single = pl.Buffered(1) if dilation == 1 else None

    def body(q_ref, k_ref, v_ref, do_ref, l_ref, dd_ref, ba_ref, dq_ref, dk_ref, dv_ref, db_ref):
        @pl.when(pl.program_id(1) == 0)
        def _():
            db_ref[...] = jnp.zeros_like(db_ref)

        lane = lax.broadcasted_iota(jnp.int32, (QBLK, LANES), 1)
        lanew = lax.broadcasted_iota(jnp.int32, (KWIN, LANES), 1)

        def step(blocks, var):
            items = []
            for s, ws in blocks:
                rows = pl.ds(_aligned(s * QBLK, QBLK), QBLK)
                win = pl.ds(ws, KWIN)
                q, dov_ = q_ref[rows, :], do_ref[rows, :]
                kk, vw = k_ref[win, :], v_ref[win, :]
                lse, dd = l_ref[rows, :], dd_ref[rows, :]
                for hh in range(2):
                    mine = (lane < SWA_DIM) == (hh == 0)
                    col = slice(hh * SWA_DIM, hh * SWA_DIM + 1)
                    items.append((hh, jnp.where(mine, q, jnp.zeros_like(q)), jnp.where(mine, dov_, jnp.zeros_like(dov_)),
                                  kk, vw, lse[:, col], dd[:, col], q, dov_))
            lgs = [_dot_nt(it[1], it[3]) + ba_ref[it[0], var] for it in items]
            dps = [_dot_nt(it[2], it[4]) for it in items]
            ps = [jnp.exp(lg - it[5]) for lg, it in zip(lgs, items)]
            dss = [p * (dp - it[6]) for p, dp, it in zip(ps, dps, items)]
            dqs = [_dot(ds, it[3]) for ds, it in zip(dss, items)]
            dks = [_dot_tn(ds, it[7]) for ds, it in zip(dss, items)]
            dvs = [_dot_tn(p, it[8]) for p, it in zip(ps, items)]
            for n, (s, ws) in enumerate(blocks):
                rows = pl.ds(_aligned(s * QBLK, QBLK), QBLK)
                win = pl.ds(ws, KWIN)
                dq_ref[rows, :] = _head_select(lane, dqs[2 * n], dqs[2 * n + 1])
                dk_ref[win, :] += _head_select(lanew, dks[2 * n], dks[2 * n + 1])
                dv_ref[win, :] += _head_select(lanew, dvs[2 * n], dvs[2 * n + 1])
            for hh in range(2):
                tot = dss[hh]
                for n in range(1, len(blocks)):
                    tot = tot + dss[2 * n + hh]
                db_ref[hh, var] += tot

        dk_ref[...] = jnp.zeros_like(dk_ref)
        dv_ref[...] = jnp.zeros_like(dv_ref)
        _band_loop(nsub, length, step)

    imap = lambda hp, r: (0, r * (SWA_W // LANES) + hp)
    blk_in = pl.BlockSpec((length, LANES), imap, pipeline_mode=single)
    blk_out = pl.BlockSpec((length, LANES), imap)
    shp = jax.ShapeDtypeStruct(qv.shape, F32)
    return pl.pallas_call(
        body, name=name, grid=(SWA_W // LANES, dilation),
        in_specs=[blk_out] * 4 + [blk_in] * 2 + [pl.BlockSpec((2, 3, QBLK, KWIN), lambda hp, r: (hp, 0, 0, 0))],
        out_specs=[blk_out, blk_out, blk_out, pl.BlockSpec((2, 3, QBLK, KWIN), lambda hp, r: (hp, 0, 0, 0))],
        out_shape=[shp, shp, shp, jax.ShapeDtypeStruct((SWA_HEADS, 3, QBLK, KWIN), F32)],
        compiler_params=_params(("arbitrary", "arbitrary"), VMEM_LIMIT),
    )(qv, kv, vv, dov, lv, ddv, bias_a)


def _bias_grad(ds2, idx, tk):
    n = ds2.shape[1]
    nk = n // tk

    def body(a_ref, i_ref, o_ref):
        @pl.when(pl.program_id(0) == 0)
        def _():
            o_ref[...] = jnp.zeros_like(o_ref)

        oh = _onehot(i_ref[...], BF16)
        rest = a_ref[...]
        acc = jnp.zeros((SWA_HEADS, REL_BUCKETS), F32)
        for _ in range(3):
            piece = rest.astype(BF16)
            acc = acc + _dot_nt(piece, oh)
            rest = rest - piece.astype(F32)
        o_ref[...] += acc

    return pl.pallas_call(
        body, name="bias_grad", grid=(nk,),
        in_specs=[pl.BlockSpec((SWA_HEADS, tk), lambda k: (0, k)), pl.BlockSpec((1, tk), lambda k: (0, k))],
        out_specs=pl.BlockSpec((SWA_HEADS, REL_BUCKETS), lambda k: (0, 0)),
        out_shape=jax.ShapeDtypeStruct((SWA_HEADS, REL_BUCKETS), F32),
        compiler_params=_params(("arbitrary",), VMEM_LIMIT),
    )(ds2, idx)


def _swa_branch_fwd(qkvb, qw_t, kw_t, rel_bias, bd, tm):
    qkv = _swa_prep_fwd(qkvb, qw_t, kw_t, bd, tm)
    tables = _bias_tables(rel_bias, _band_index(), BIAS_TILE)
    os_, ls_, tabs = [], [], []
    for n, (_, d) in enumerate(PATTERNS):
        bias = tables[:, n * BAND_CELLS:(n + 1) * BAND_CELLS].reshape(SWA_HEADS, len(WIN_OFFSETS), QBLK, KWIN)
        o_p, l_p = _swa_fwd(*qkv[3 * n:3 * n + 3], bias, d, f"swa_fwd_d{d}")
        os_.append(o_p)
        ls_.append(l_p)
        tabs.append(bias)
    o, o16, *lses = _swa_combine(os_, ls_, tm)
    return o, o16, (qkv, lses, tabs)


def _swa_branch_bwd(do, o, saved, qkvb, qw_t, kw_t, bd, tm):
    qkv, lses, tabs = saved
    prep = _swa_bwd_prep(do, o, bd, tm)
    grads, dss = [], []
    for n, ((_, d), bias) in enumerate(zip(PATTERNS, tabs)):
        dq, dk, dv, ds = _swa_bwd(*qkv[3 * n:3 * n + 3], prep[3 + n], lses[n], prep[n], bias, d, f"swa_bwd_d{d}")
        grads += [dq, dk, dv]
        dss.append(ds.reshape(SWA_HEADS, -1))
    dqkvb, dqw, dkw = _swa_prep_bwd(qkvb, qw_t, kw_t, bd, grads, tm)
    dbias = _bias_grad(jnp.concatenate(dss, axis=1), _band_index(), BIAS_TILE)
    fold = lambda w: jnp.sum(w.reshape(SWA_HEADS, SWA_DIM), axis=0)
    return dqkvb, fold(dqw), fold(dkw), dbias.T


def _mesh_pos():
    return lax.axis_index("x"), lax.axis_index("y"), lax.axis_index("c")


def _other_chips(x, y):
    return [(1 - x, y), (x, 1 - y), (1 - x, 1 - y)]


def _remote(src, dst, send_sem, recv_sem, device):
    return pltpu.make_async_remote_copy(src_ref=src, dst_ref=dst, send_sem=send_sem, recv_sem=recv_sem,
                                        device_id=device, device_id_type=MESH)


def _split_axis(shape2):
    return 0 if (shape2[0] // 2) % 16 == 0 else 1


def _half_index(shape2, c):
    axis = _split_axis(shape2)
    h = shape2[axis] // 2
    return (pl.ds(c * h, h), slice(None)) if axis == 0 else (slice(None), pl.ds(c * h, h))


def _all_gather(xs):
    n = len(xs)

    def body(*refs):
        ins, outs = refs[:n], refs[n:2 * n]
        send_sems, recv_sems = refs[2 * n:]
        x, y, c = _mesh_pos()
        me = 2 * x + y
        chips = _other_chips(x, y)
        halves = []
        sends = []
        for a in range(n):
            h = ins[a].shape[0] // 2
            mine, other = pl.ds(c * h, h), pl.ds((1 - c) * h, h)
            halves.append((mine, other))
            own = _remote(ins[a], outs[a].at[me], send_sems.at[a, 6], recv_sems.at[a, 6], (x, y, 1 - c))
            own.start()
            sends.append(own)
            for j, chip in enumerate(chips):
                cp = _remote(ins[a].at[mine], outs[a].at[me, mine], send_sems.at[a, j], recv_sems.at[a, j], (*chip, c))
                cp.start()
                sends.append(cp)
        for a in range(n):
            mine, _ = halves[a]
            for j, chip in enumerate(chips):
                src = 2 * chip[0] + chip[1]
                landed = outs[a].at[src, mine]
                _remote(landed, landed, send_sems.at[a, j], recv_sems.at[a, j], (x, y, c)).wait_recv()
                fwd = _remote(landed, landed, send_sems.at[a, 3 + j], recv_sems.at[a, 3 + j], (x, y, 1 - c))
                fwd.start()
                sends.append(fwd)
        for a in range(n):
            _, other = halves[a]
            for j, chip in enumerate(chips):
                src = 2 * chip[0] + chip[1]
                landed = outs[a].at[src, other]
                _remote(landed, landed, send_sems.at[a, 3 + j], recv_sems.at[a, 3 + j], (x, y, c)).wait_recv()
            mine_slot = outs[a].at[me]
            _remote(mine_slot, mine_slot, send_sems.at[a, 6], recv_sems.at[a, 6], (x, y, c)).wait_recv()
        for cp in sends:
            cp.wait_send()

    return list(pl.pallas_call(
        body, name="all_gather_weights",
        in_specs=[ANY] * n, out_specs=[ANY] * n,
        out_shape=[jax.ShapeDtypeStruct((N_SHARDS,) + a.shape, a.dtype) for a in xs],
        scratch_shapes=[pltpu.SemaphoreType.DMA((n, 7)), pltpu.SemaphoreType.DMA((n, 7))],
    )(*xs))


def _rs_pair(gs):
    n = len(gs)

    def body(*refs):
        ins, lands = refs[:n], refs[n:2 * n]
        send_sems, recv_sems = refs[2 * n:]
        x, y, c = _mesh_pos()
        cps = []
        for a in range(n):
            theirs = (slice(None),) + _half_index(ins[a].shape[1:], 1 - c)
            cp = _remote(ins[a].at[theirs], lands[a], send_sems.at[a], recv_sems.at[a], (x, y, 1 - c))
            cp.start()
            cps.append(cp)
        for cp in cps:
            cp.wait()

    def half_shape(g):
        dims = list(g.shape)
        dims[1 + _split_axis(g.shape[1:])] //= 2
        return tuple(dims)

    return list(pl.pallas_call(
        body, name="rs_pair", in_specs=[ANY] * n, out_specs=[ANY] * n,
        out_shape=[jax.ShapeDtypeStruct(half_shape(g), g.dtype) for g in gs],
        scratch_shapes=[pltpu.SemaphoreType.DMA((n,)), pltpu.SemaphoreType.DMA((n,))],
    )(*gs))


def _pair_exchange(gs):
    def copies(cin, cout, send_sems, recv_sems):
        x, y, c = _mesh_pos()
        return [_remote(g.at[(slice(None),) + _half_index(g.shape[1:], 1 - c)], land, send_sems.at[a, 0],
                        recv_sems.at[a, 0], (x, y, 1 - c)) for a, (g, land) in enumerate(zip(cin, cout))]

    def start(*refs):
        for cp in copies(*refs):
            cp.start()

    def finish(*refs):
        for cp in copies(*refs):
            cp.wait()

    def half_shape(g):
        dims = list(g.shape)
        dims[1 + _split_axis(g.shape[1:])] //= 2
        return tuple(dims)

    return _Exchange(tuple(gs), tuple(jax.ShapeDtypeStruct(half_shape(g), g.dtype) for g in gs), start, finish)


def _rs_chips(ss):
    n = len(ss)

    def body(*refs):
        ins, outs = refs[:n], refs[n:2 * n]
        send_sems, recv_sems = refs[2 * n:]
        x, y, c = _mesh_pos()
        me = 2 * x + y
        chips = _other_chips(x, y)
        cps = []
        for a in range(n):
            for j, chip in enumerate(chips):
                dst_chip = 2 * chip[0] + chip[1]
                cp = _remote(ins[a].at[dst_chip], outs[a].at[me], send_sems.at[a, j], recv_sems.at[a, j], (*chip, c))
                cp.start()
                cps.append(cp)
        for a in range(n):
            for j, chip in enumerate(chips):
                src = 2 * chip[0] + chip[1]
                _remote(outs[a].at[src], outs[a].at[src], send_sems.at[a, j], recv_sems.at[a, j], (x, y, c)).wait_recv()
        for cp in cps:
            cp.wait_send()

    return list(pl.pallas_call(
        body, name="rs_chips", in_specs=[ANY] * n, out_specs=[ANY] * n,
        out_shape=[jax.ShapeDtypeStruct(s.shape, s.dtype) for s in ss],
        scratch_shapes=[pltpu.SemaphoreType.DMA((n, 3)), pltpu.SemaphoreType.DMA((n, 3))],
    )(*ss))


def _rs_join(fs, axes):
    n = len(fs)

    def whole(f, axis):
        dims = list(f.shape)
        dims[axis] *= 2
        return tuple(dims)

    def body(*refs):
        ins, outs = refs[:n], refs[n:2 * n]
        send_sems, recv_sems = refs[2 * n:]
        x, y, c = _mesh_pos()
        cps = []
        for a in range(n):
            h = ins[a].shape[axes[a]]
            mine = (pl.ds(c * h, h), slice(None)) if axes[a] == 0 else (slice(None), pl.ds(c * h, h))
            cp = _remote(ins[a], outs[a].at[mine], send_sems.at[a], recv_sems.at[a], (x, y, 1 - c))
            cp.start()
            cps.append(cp)
        for cp in cps:
            cp.wait()

    outs = pl.pallas_call(
        body, name="rs_join", in_specs=[ANY] * n, out_specs=[ANY] * n,
        out_shape=[jax.ShapeDtypeStruct(whole(f, ax), f.dtype) for f, ax in zip(fs, axes)],
        scratch_shapes=[pltpu.SemaphoreType.DMA((n,)), pltpu.SemaphoreType.DMA((n,))],
    )(*fs)
    c = lax.axis_index("c")
    return [lax.dynamic_update_slice_in_dim(o, f, c * f.shape[ax], ax) for o, f, ax in zip(outs, fs, axes)]


def _gather_exchange(xs):
    def start(cin, cout, send_sems, recv_sems):
        x, y, c = _mesh_pos()
        me = 2 * x + y
        for a, (src, dst) in enumerate(zip(cin, cout)):
            mine = _half_index(src.shape, c)
            for j, chip in enumerate(_other_chips(x, y)):
                _remote(src.at[mine], dst.at[(me,) + mine], send_sems.at[a, j], recv_sems.at[a, j], (*chip, c)).start()
            _remote(src, dst.at[me], send_sems.at[a, 3], recv_sems.at[a, 3], (x, y, 1 - c)).start()

    def finish(cin, cout, send_sems, recv_sems):
        x, y, c = _mesh_pos()
        for a, dst in enumerate(cout):
            for j, chip in enumerate(_other_chips(x, y)):
                landed = dst.at[(2 * chip[0] + chip[1],) + _half_index(dst.shape[1:], c)]
                _remote(landed, landed, send_sems.at[a, j], recv_sems.at[a, j], (x, y, c)).wait()
            own = dst.at[2 * x + y]
            _remote(own, own, send_sems.at[a, 3], recv_sems.at[a, 3], (x, y, c)).wait()

    return _Exchange(tuple(xs), tuple(jax.ShapeDtypeStruct((N_SHARDS,) + a.shape, a.dtype) for a in xs), start, finish)


def _gather_forward(gs):
    n = len(gs)

    def body(*refs):
        outs = refs[n:2 * n]
        send_sems, recv_sems = refs[2 * n:]
        x, y, c = _mesh_pos()
        chips = _other_chips(x, y)
        cps = []
        for a in range(n):
            for j, chip in enumerate(chips):
                landed = outs[a].at[(2 * chip[0] + chip[1],) + _half_index(outs[a].shape[1:], c)]
                cp = _remote(landed, landed, send_sems.at[a, j], recv_sems.at[a, j], (x, y, 1 - c))
                cp.start()
                cps.append(cp)
        for a in range(n):
            for j, chip in enumerate(chips):
                other = outs[a].at[(2 * chip[0] + chip[1],) + _half_index(outs[a].shape[1:], 1 - c)]
                _remote(other, other, send_sems.at[a, j], recv_sems.at[a, j], (x, y, c)).wait_recv()
        for cp in cps:
            cp.wait_send()

    return list(pl.pallas_call(
        body, name="gather_forward", in_specs=[ANY] * n, out_specs=[ANY] * n,
        out_shape=[jax.ShapeDtypeStruct(g.shape, g.dtype) for g in gs],
        input_output_aliases={i: i for i in range(n)},
        scratch_shapes=[pltpu.SemaphoreType.DMA((n, 3)), pltpu.SemaphoreType.DMA((n, 3))],
    )(*gs))


def _scatter_exchange(ss):
    def start(cin, cout, send_sems, recv_sems):
        x, y, c = _mesh_pos()
        me = 2 * x + y
        for a, (src, dst) in enumerate(zip(cin, cout)):
            for j, chip in enumerate(_other_chips(x, y)):
                _remote(src.at[2 * chip[0] + chip[1]], dst.at[me], send_sems.at[a, j], recv_sems.at[a, j],
                        (*chip, c)).start()

    def finish(cin, cout, send_sems, recv_sems):
        x, y, c = _mesh_pos()
        for a, dst in enumerate(cout):
            for j, chip in enumerate(_other_chips(x, y)):
                slot = dst.at[2 * chip[0] + chip[1]]
                _remote(slot, slot, send_sems.at[a, j], recv_sems.at[a, j], (x, y, c)).wait()

    return _Exchange(tuple(ss), tuple(jax.ShapeDtypeStruct(s.shape, s.dtype) for s in ss), start, finish)


def _add_pairs(gs, lands, name):
    n = len(gs)

    def body(*refs):
        c = lax.axis_index("c")
        for g_ref, l_ref, o_ref in zip(refs[:n], refs[n:2 * n], refs[2 * n:]):
            mine = g_ref[(0,) + _half_index(g_ref.shape[1:], c)]
            o_ref[0] = (mine.astype(F32) + l_ref[0].astype(F32)).astype(BF16)

    whole = [pl.BlockSpec((1,) + g.shape[1:], lambda j: (j, 0, 0)) for g in gs]
    half = [pl.BlockSpec((1,) + l.shape[1:], lambda j: (j, 0, 0)) for l in lands]
    return list(pl.pallas_call(body, name=name, grid=(gs[0].shape[0],), in_specs=whole + half, out_specs=half,
                               out_shape=[jax.ShapeDtypeStruct(l.shape, BF16) for l in lands],
                               compiler_params=_params(("arbitrary",), VMEM_LIMIT))(*gs, *lands))


def _sum_slots(slots, owns, name):
    n = len(slots)

    def body(*refs):
        me = 2 * lax.axis_index("x") + lax.axis_index("y")
        for s_ref, o_ref, out_ref in zip(refs[:n], refs[n:2 * n], refs[2 * n:]):
            acc = jnp.zeros(out_ref.shape, F32)
            for s in range(N_SHARDS):
                acc = acc + jnp.where(me == s, o_ref[s], s_ref[s]).astype(F32)
            out_ref[...] = acc

    def specs(a):
        _, h, c = a.shape
        if h % 32 == 0:
            return (pl.BlockSpec((N_SHARDS, h // 2, c), lambda i: (0, i, 0)), pl.BlockSpec((h // 2, c), lambda i: (i, 0)))
        return (pl.BlockSpec((N_SHARDS, h, c // 2), lambda i: (0, 0, i)), pl.BlockSpec((h, c // 2), lambda i: (0, i)))

    in_specs = [specs(a)[0] for a in slots]
    return list(pl.pallas_call(body, name=name, grid=(2,), in_specs=in_specs + in_specs,
                               out_specs=[specs(a)[1] for a in slots],
                               out_shape=[jax.ShapeDtypeStruct(a.shape[1:], F32) for a in slots],
                               compiler_params=_params(("arbitrary",), VMEM_LIMIT))(*slots, *owns))


def _all_reduce_small(p):
    r = p.shape[0]

    def body(p_ref, o_ref, buf, send_sems, recv_sems):
        x, y, c = _mesh_pos()
        me = 4 * x + 2 * y + c
        buf[me] = p_ref[...]
        cps = []
        k = 0
        for fx in range(2):
            for fy in range(2):
                for fc in range(2):
                    if fx + fy + fc == 0:
                        continue
                    peer = (1 - x if fx else x, 1 - y if fy else y, 1 - c if fc else c)
                    peer_id = 4 * peer[0] + 2 * peer[1] + peer[2]
                    cp = _remote(p_ref, buf.at[me], send_sems.at[k], recv_sems.at[k], peer)
                    cp.start()
                    cps.append((cp, peer_id, k))
                    k += 1
        for cp, peer_id, k in cps:
            _remote(p_ref, buf.at[peer_id], send_sems.at[k], recv_sems.at[k], (x, y, c)).wait_recv()
        for cp, _, _ in cps:
            cp.wait_send()
        acc = buf[0]
        for s in range(1, 8):
            acc = acc + buf[s]
        o_ref[...] = acc

    vm = pl.BlockSpec(memory_space=pltpu.VMEM)
    return pl.pallas_call(
        body, name="all_reduce_small", in_specs=[vm], out_specs=vm,
        out_shape=jax.ShapeDtypeStruct(p.shape, F32),
        scratch_shapes=[pltpu.VMEM((8, r, LANES), F32), pltpu.SemaphoreType.DMA((7,)), pltpu.SemaphoreType.DMA((7,))],
    )(p)


def _adamw(params, name, steps):
    c1 = 1.0 / (1.0 - ADAM_B1 ** ADAM_STEP)
    c2 = 1.0 / (1.0 - ADAM_B2 ** ADAM_STEP)
    n = len(params)

    def body(*refs):
        for a in range(n):
            w_ref, g_ref, m_ref, v_ref = refs[4 * a:4 * a + 4]
            d_ref, nm_ref, nv_ref = refs[4 * n + 3 * a:4 * n + 3 * a + 3]
            gv = g_ref[...]
            nm = ADAM_B1 * m_ref[...] + (1.0 - ADAM_B1) * gv
            nv = ADAM_B2 * v_ref[...] + (1.0 - ADAM_B2) * (gv * gv)
            d_ref[...] = -ADAM_LR * ((nm * c1) / (jnp.sqrt(nv * c2) + ADAM_EPS) + ADAM_WD * w_ref[...])
            nm_ref[...] = nm
            nv_ref[...] = nv

    def spec(shape):
        r, c = shape
        if r % (8 * steps) == 0:
            return pl.BlockSpec((r // steps, c), lambda i: (i, 0))
        assert c % (LANES * steps) == 0
        return pl.BlockSpec((r, c // steps), lambda i: (0, i))

    specs = [spec(w.shape) for w, _, _, _ in params]
    res = pl.pallas_call(
        body, name=name, grid=(steps,),
        in_specs=[s for s in specs for _ in range(4)], out_specs=[s for s in specs for _ in range(3)],
        out_shape=[jax.ShapeDtypeStruct(w.shape, F32) for w, _, _, _ in params for _ in range(3)],
        compiler_params=_params(("arbitrary",), VMEM_LIMIT))(*[a for p4 in params for a in p4])
    return [tuple(res[3 * a:3 * a + 3]) for a in range(n)]


PACK_UNIT = 8 * LANES


def _pack(arrs):
    parts = []
    for a in arrs:
        f = a.reshape(-1).astype(F32)
        parts.append(jnp.pad(f, (0, (-f.shape[0]) % PACK_UNIT)).reshape(-1, LANES))
    return jnp.concatenate(parts, axis=0)


def _unpack(m, shapes):
    outs, row = [], 0
    for s in shapes:
        n = int(np.prod(s))
        rows = -(-n // PACK_UNIT) * 8
        outs.append(m[row:row + rows].reshape(-1)[:n].reshape(s))
        row += rows
    return outs


WEIGHTS = ["ffn1_norm", "ffn1_w_gate", "ffn1_w_up", "ffn1_w_down", "mix_norm", "w_in", "conv_w", "a_log", "dt_bias",
           "gdn_norm_w", "q_norm_w", "k_norm_w", "rel_bias", "w_out", "ffn2_norm", "ffn2_w_gate", "ffn2_w_up",
           "ffn2_w_down", "final_norm"]
BIG = ["ffn1_w_gate", "ffn1_w_up", "ffn1_w_down", "w_in", "w_out", "ffn2_w_gate", "ffn2_w_up", "ffn2_w_down"]
SMALL = [n for n in WEIGHTS if n not in BIG]
COL_SHARDED = ["ffn1_w_gate", "ffn1_w_up", "w_in", "ffn2_w_gate", "ffn2_w_up"]
N_IN_COLS = 3600
TM = 256
TE = 512
ADAM_PIECES = 8
TK = 2048


def kernel(x, ffn1_norm, ffn1_w_gate, ffn1_w_up, ffn1_w_down, mix_norm, w_in, conv_w, a_log, dt_bias, gdn_norm_w, q_norm_w, k_norm_w, rel_bias, w_out, ffn2_norm, ffn2_w_gate, ffn2_w_up, ffn2_w_down, final_norm, loss_target, m_ffn1_norm, m_ffn1_w_gate, m_ffn1_w_up, m_ffn1_w_down, m_mix_norm, m_w_in, m_conv_w, m_a_log, m_dt_bias, m_gdn_norm_w, m_q_norm_w, m_k_norm_w, m_rel_bias, m_w_out, m_ffn2_norm, m_ffn2_w_gate, m_ffn2_w_up, m_ffn2_w_down, m_final_norm, v_ffn1_norm, v_ffn1_w_gate, v_ffn1_w_up, v_ffn1_w_down, v_mix_norm, v_w_in, v_conv_w, v_a_log, v_dt_bias, v_gdn_norm_w, v_q_norm_w, v_k_norm_w, v_rel_bias, v_w_out, v_ffn2_norm, v_ffn2_w_gate, v_ffn2_w_up, v_ffn2_w_down, v_final_norm):
    p = dict(locals())
    xs, target = x[0], loss_target[0]
    t, d = xs.shape
    nc = t // CHUNK
    tk = min(TK, t)
    tkf = min(2 * TK, t)
    me = 2 * lax.axis_index("x") + lax.axis_index("y")

    first = ["ffn1_w_gate", "ffn1_w_up", "ffn1_w_down"]
    later = [n for n in BIG if n not in first] + ["conv_w"]
    local = lambda n, a: a[0].T if n in COL_SHARDED else a[0]
    shards = {n: local(n, p[n]).astype(BF16) for n in BIG}
    shards["conv_w"] = conv_w[0]
    gw = dict(zip(first, _all_gather([shards[n] for n in first])))
    f1 = (gw["ffn1_w_gate"], gw["ffn1_w_up"], gw["ffn1_w_down"])
    (x1, xn1, g1, u1), landed = _ffn_fwd(xs, ffn1_norm, *f1, TE, "ffn1_fwd",
                                         exchange=_gather_exchange([shards[n] for n in later]))
    gw.update(zip(later, _gather_forward(landed)))
    wp = gw["w_in"].reshape(N_IN_COLS, d)
    w_out_full = gw["w_out"].reshape(d, d)
    conv_rows = conv_w.shape[1]
    cw = jnp.pad(gw["conv_w"].reshape(N_SHARDS * conv_rows, CONV_TAPS).T, ((0, 8 - CONV_TAPS), (0, 0)))
    gp = jnp.pad(jnp.stack([a_log.reshape(8), dt_bias.reshape(8)]), ((0, 6), (0, LANES - 8)))
    gdn_w = gdn_norm_w.reshape(1, GDN_DIM)
    qw_t = jnp.tile(q_norm_w.reshape(1, SWA_DIM), (1, SWA_HEADS))
    kw_t = jnp.tile(k_norm_w.reshape(1, SWA_DIM), (1, SWA_HEADS))
    bd = jnp.asarray(np.kron(np.eye(2), np.full((SWA_DIM, SWA_DIM), 1.0 / SWA_DIM)), BF16)
    f2 = (gw["ffn2_w_gate"], gw["ffn2_w_up"], gw["ffn2_w_down"])

    hn, qkva, z, ab, qkvb = _mix_in_fwd(x1, mix_norm, wp, TE)
    qkvc, gb = _gdn_prep_fwd(qkva, cw, ab, gp, TE)
    gbt = jnp.transpose(gb[:, :16].reshape(nc, CHUNK, 16), (0, 2, 1))
    o_f, o_b, gdn_saved = _gdn_fwd(qkvc, gb, gbt)
    oa = _gdn_post_fwd(o_f, o_b, z, gdn_w, TE)
    o_swa, o_swa16, swa_saved = _swa_branch_fwd(qkvb, qw_t, kw_t, rel_bias, bd, TE)
    x2 = _mix_out_fwd(x1, oa, o_swa, w_out_full, TE)
    (dx3, xn2, g2, u2, loss_part, d_final), _ = _ffn_fwd(x2, ffn2_norm, *f2, TE, "ffn2_fwd", head=(final_norm, target))

    def pair_sums(partials, tag):
        return _add_pairs(partials, _rs_pair(partials), f"rs_add_{tag}")

    (dx2, dyh2, dg2, du2, h2, d_nw2), _ = _ffn_bwd_dx(dx3, x2, ffn2_norm, g2, u2, *f2, TM, "ffn2_bwd_dx")
    dwg2 = _matmul_tn(dg2, xn2, tkf, "ffn2_dwg")
    dwu2 = _matmul_tn(du2, xn2, tkf, "ffn2_dwu")
    dwd2 = _matmul_tn(h2, dyh2, tkf, "ffn2_dwd")
    (doa, dob, dx2b), lands_f2 = _mix_out_bwd(dx2, w_out_full, TE, exchange=_pair_exchange([dwg2, dwu2, dwd2]))
    sums_f2 = _add_pairs([dwg2, dwu2, dwd2], lands_f2, "rs_add_a")
    dwo = jnp.concatenate([_matmul_tn(oa, dx2b, tk, "w_out_dw_a")[0], _matmul_tn(o_swa16, dx2b, tk, "w_out_dw_b")[0]],
                          axis=0).reshape(N_SHARDS, d // N_SHARDS, d)
    do_g, dz, d_gdnw = _gdn_post_bwd(doa, o_f, o_b, z, gdn_w, TE)
    (dqkvc, dgates), slots_f2 = _gdn_bwd(qkvc, gb, gbt, do_g, gdn_saved, exchange=_scatter_exchange(sums_f2))
    dqkva, dab, dcw, dgp = _gdn_prep_bwd(qkva, cw, ab, gp, dqkvc, dgates, TM)
    dqkvb, d_qw, d_kw, d_rel = _swa_branch_bwd(dob, o_swa, swa_saved, qkvb, qw_t, kw_t, bd, TE)
    dpieces = (dqkva, dz, dab, dqkvb)
    dwp = [_matmul_tn(dp, hn, tk, f"w_in_dw_{i}")[0] for i, dp in enumerate(dpieces)]
    dw_in = jnp.concatenate([dwp[0], dwp[1], dwp[2][:N_GATE_COLS], dwp[3]], axis=0)
    dw_in = dw_in.reshape(N_SHARDS, N_IN_COLS // N_SHARDS, d)
    sums_mix = pair_sums([dw_in, dwo], "b")
    (dx1, d_mixnw), slots_mix = _mix_in_bwd_dx(dx2, x1, mix_norm, dpieces, wp, TE, exchange=_scatter_exchange(sums_mix))
    (gx, dyh1, dg1, du1, h1, d_nw1), _ = _ffn_bwd_dx(dx1, xs, ffn1_norm, g1, u1, *f1, TM, "ffn1_bwd_dx")
    dwg1 = _matmul_tn(dg1, xn1, tkf, "ffn1_dwg")
    dwu1 = _matmul_tn(du1, xn1, tkf, "ffn1_dwu")
    sums_gu = pair_sums([dwg1, dwu1], "c")
    dwd1, slots_gu = _matmul_tn(h1, dyh1, tkf, "ffn1_dwd", exchange=_scatter_exchange(sums_gu))
    sums_d = pair_sums([dwd1], "d")
    slots = slots_gu + _rs_chips(sums_d) + slots_mix + slots_f2
    sums = sums_gu + sums_d + sums_mix + sums_f2
    halves = _sum_slots(slots[:4], sums[:4], "rs_sum_a") + _sum_slots(slots[4:], sums[4:], "rs_sum_b")
    g_big = dict(zip(BIG, _rs_join(halves, [_split_axis(shards[n].shape) for n in BIG])))

    small_partial = {"ffn1_norm": d_nw1, "mix_norm": d_mixnw, "a_log": dgp[0, 0:8], "dt_bias": dgp[1, 0:8],
                     "gdn_norm_w": d_gdnw, "q_norm_w": d_qw, "k_norm_w": d_kw, "rel_bias": d_rel,
                     "ffn2_norm": d_nw2, "final_norm": d_final, "conv_w": dcw[0:CONV_TAPS].T}
    red = _all_reduce_small(_pack([small_partial[n] for n in SMALL] + [loss_part[0, 0:1]]))
    full_shapes = [p[n].shape if n != "conv_w" else (N_SHARDS * conv_rows, CONV_TAPS) for n in SMALL]
    red_parts = _unpack(red, full_shapes + [(1,)])
    loss = red_parts[-1].reshape(())
    g_small = dict(zip(SMALL, red_parts[:-1]))
    g_small["conv_w"] = lax.dynamic_slice_in_dim(g_small["conv_w"], me * conv_rows, conv_rows, 0).reshape(conv_w.shape)

    grads, deltas, new_m, new_v = {}, {}, {}, {}
    quad = lambda n: (local(n, p[n]), g_big[n], local(n, p["m_" + n]), local(n, p["v_" + n]))
    updates = (_adamw([quad(n) for n in BIG[:4]], "adamw_a", ADAM_PIECES)
               + _adamw([quad(n) for n in BIG[4:]], "adamw_b", ADAM_PIECES))
    for n, (dl, nm, nv) in zip(BIG, updates):
        back = (lambda a: a.T[None]) if n in COL_SHARDED else (lambda a: a[None])
        grads[n], deltas[n], new_m[n], new_v[n] = back(g_big[n]), back(dl), back(nm), back(nv)
    packed = [_pack([src[n] for n in SMALL]) for src in
              ({n: p[n] for n in SMALL}, g_small, {n: p["m_" + n] for n in SMALL}, {n: p["v_" + n] for n in SMALL})]
    small_shapes = [p[n].shape for n in SMALL]
    for dst, arr in zip((deltas, new_m, new_v), _adamw([tuple(packed)], "adamw_small", 1)[0]):
        dst.update(zip(SMALL, _unpack(arr, small_shapes)))
    grads.update(g_small)

    return (loss, gx[None], *[grads[n] for n in WEIGHTS], *[deltas[n] for n in WEIGHTS],
            *[new_m[n] for n in WEIGHTS], *[new_v[n] for n in WEIGHTS])
```

```python
import math
from typing import Callable, NamedTuple

import numpy as np
import jax
import jax.numpy as jnp
from jax import lax
from jax.experimental import pallas as pl
from jax.experimental.pallas import tpu as pltpu

F32 = jnp.float32
BF16 = jnp.bfloat16
MESH = pl.DeviceIdType.MESH

EPS = 1e-6
NEG_BIG = -1e30
GDN_HEADS = 4
GDN_DIM = 128
CHUNK = 64
SWA_HEADS = 8
SWA_DIM = 64
PATTERNS = ((128, 1), (512, 4), (2048, 16))
RADIUS = 64
REL_BUCKETS = 32
REL_MAX_DISTANCE = 1024
CONV_TAPS = 5
N_SHARDS = 4
LANES = 128
VMEM_LIMIT = 56 * 1024 * 1024

ADAM_LR, ADAM_B1, ADAM_B2, ADAM_EPS, ADAM_WD, ADAM_STEP = 0.001, 0.9, 0.999, 1e-08, 0.01, 10


def _params(sem=None, vmem=None):
    return pltpu.CompilerParams(dimension_semantics=sem, vmem_limit_bytes=vmem)


def _resident(shape):
    nd = len(shape)
    return pl.BlockSpec(shape, lambda *_: (0,) * nd, pipeline_mode=pl.Buffered(1))


ANY = pl.BlockSpec(memory_space=pl.ANY)
RING_SLOTS = 3


class _Exchange(NamedTuple):
    arrays: tuple
    out_shape: tuple
    start: Callable
    finish: Callable


def _grid_call(body, name, nsteps, in_specs, out_specs, out_shape, operands, scratch=(), exchange=None):
    params = _params(("arbitrary",), VMEM_LIMIT)
    if exchange is None:
        res = pl.pallas_call(body, name=name, grid=(nsteps,), in_specs=list(in_specs), out_specs=list(out_specs),
                             out_shape=list(out_shape), scratch_shapes=list(scratch), compiler_params=params)(*operands)
        return list(res), []
    n_in, n_out, k, n_scr = len(in_specs), len(out_specs), len(exchange.arrays), len(scratch)

    def wrapped(*refs):
        ins, cin = refs[:n_in], refs[n_in:n_in + k]
        outs, cout = refs[n_in + k:n_in + k + n_out], refs[n_in + k + n_out:n_in + 2 * k + n_out]
        rest = refs[n_in + 2 * k + n_out:]
        scr, (send_sems, recv_sems) = rest[:n_scr], rest[n_scr:]

        @pl.when(pl.program_id(0) == 0)
        def _():
            exchange.start(cin, cout, send_sems, recv_sems)

        body(*ins, *outs, *scr)

        @pl.when(pl.program_id(0) == nsteps - 1)
        def _():
            exchange.finish(cin, cout, send_sems, recv_sems)

    res = pl.pallas_call(
        wrapped, name=name, grid=(nsteps,), in_specs=list(in_specs) + [ANY] * k, out_specs=list(out_specs) + [ANY] * k,
        out_shape=list(out_shape) + list(exchange.out_shape),
        scratch_shapes=list(scratch) + [pltpu.SemaphoreType.DMA((k, 4)), pltpu.SemaphoreType.DMA((k, 4))],
        compiler_params=params)(*operands, *exchange.arrays)
    return list(res[:n_out]), list(res[n_out:])


def _dot(a, b):
    return jnp.dot(a.astype(BF16), b.astype(BF16), preferred_element_type=F32)


def _dot_nt(a, b):
    return lax.dot_general(a.astype(BF16), b.astype(BF16), (((1,), (1,)), ((), ())), preferred_element_type=F32)


def _dot_tn(a, b):
    return lax.dot_general(a.astype(BF16), b.astype(BF16), (((0,), (0,)), ((), ())), preferred_element_type=F32)


def _sigmoid(x):
    return 1.0 / (1.0 + jnp.exp(-x))


def _rstd(xf):
    return lax.rsqrt(jnp.mean(xf * xf, axis=-1, keepdims=True) + EPS)


def _rms_bwd(xf, r, nw, dxn):
    xhat = xf * r
    dxh = dxn * nw
    dx = r * (dxh - xhat * jnp.mean(dxh * xhat, axis=-1, keepdims=True))
    return dx, jnp.sum(dxn * xhat, axis=0, keepdims=True)


def _ffn_fwd(x, nw, wg, wu, wd, tm, name, exchange=None, head=None):
    t, d = x.shape
    nj, fs, _ = wg.shape

    def body(x_ref, nw_ref, wg_ref, wu_ref, wd_ref, *rest):
        if head is None:
            y_ref, xn_ref, g_ref, u_ref = rest
        else:
            fw_ref, t_ref, y_ref, xn_ref, g_ref, u_ref, loss_ref, dfw_ref = rest

            @pl.when(pl.program_id(0) == 0)
            def _():
                loss_ref[...] = jnp.zeros_like(loss_ref)
                dfw_ref[...] = jnp.zeros_like(dfw_ref)

        xf = x_ref[...]
        xn = (xf * _rstd(xf) * nw_ref[...]).astype(BF16)
        xn_ref[...] = xn
        acc = jnp.zeros((tm, d), F32)
        for j in range(nj):
            g = _dot_nt(xn, wg_ref[j])
            u = _dot_nt(xn, wu_ref[j])
            h = (g * _sigmoid(g) * u).astype(BF16)
            acc = acc + jnp.dot(h, wd_ref[j], preferred_element_type=F32)
            g_ref[j] = g.astype(BF16)
            u_ref[j] = u.astype(BF16)
        y = xf + 0.5 * acc
        if head is None:
            y_ref[...] = y
        else:
            r = _rstd(y)
            err = y * r * fw_ref[...] - t_ref[...]
            loss_ref[...] += 0.5 * jnp.sum(jnp.mean(err * err, axis=-1, keepdims=True), axis=0, keepdims=True)
            dy, dfw = _rms_bwd(y, r, fw_ref[...], err * (1.0 / d))
            y_ref[...] = dy
            dfw_ref[...] += dfw

    row = pl.BlockSpec((tm, d), lambda i: (i, 0))
    act = pl.BlockSpec((nj, tm, fs), lambda i: (0, i, 0))
    in_specs = [row, _resident((1, d)), _resident(wg.shape), _resident(wu.shape), _resident(wd.shape)]
    out_specs = [row, row, act, act]
    out_shape = [jax.ShapeDtypeStruct((t, d), F32), jax.ShapeDtypeStruct((t, d), BF16),
                 jax.ShapeDtypeStruct((nj, t, fs), BF16), jax.ShapeDtypeStruct((nj, t, fs), BF16)]
    operands = (x, nw, wg, wu, wd)
    if head is not None:
        in_specs += [_resident((1, d)), row]
        out_specs += [pl.BlockSpec((1, LANES), lambda i: (0, 0)), pl.BlockSpec((1, d), lambda i: (0, 0))]
        out_shape += [jax.ShapeDtypeStruct((1, LANES), F32), jax.ShapeDtypeStruct((1, d), F32)]
        operands += tuple(head)
    return _grid_call(body, name, t // tm, in_specs, out_specs, out_shape, operands, exchange=exchange)


def _ffn_bwd_dx(dy, x, nw, g, u, wg, wu, wd, tm, name, exchange=None):
    t, d = x.shape
    nj, fs, _ = wg.shape

    def body(dy_ref, x_ref, nw_ref, g_ref, u_ref, wg_ref, wu_ref, wd_ref,
             dx_ref, dyh_ref, dg_ref, du_ref, h_ref, dnw_ref):
        @pl.when(pl.program_id(0) == 0)
        def _():
            dnw_ref[...] = jnp.zeros_like(dnw_ref)

        dyv = dy_ref[...]
        dyh = (0.5 * dyv).astype(BF16)
        dyh_ref[...] = dyh
        dxn = jnp.zeros((tm, d), F32)
        dh_next = _dot_nt(dyh, wd_ref[0])
        for j in range(nj):
            dh = dh_next
            gv = g_ref[j].astype(F32)
            uv = u_ref[j].astype(F32)
            sg = _sigmoid(gv)
            si = gv * sg
            dg = (dh * uv * (sg * (1.0 + gv * (1.0 - sg)))).astype(BF16)
            du = (dh * si).astype(BF16)
            if j + 1 < nj:
                dh_next = _dot_nt(dyh, wd_ref[j + 1])
            h_ref[j] = (si * uv).astype(BF16)
            dg_ref[j] = dg
            du_ref[j] = du
            dxn = dxn + _dot(dg, wg_ref[j]) + _dot(du, wu_ref[j])
        xf = x_ref[...]
        dxr, dnw = _rms_bwd(xf, _rstd(xf), nw_ref[...], dxn)
        dx_ref[...] = dyv + dxr
        dnw_ref[...] += dnw

    row = pl.BlockSpec((tm, d), lambda i: (i, 0))
    act = pl.BlockSpec((nj, tm, fs), lambda i: (0, i, 0))
    act_shape = jax.ShapeDtypeStruct((nj, t, fs), BF16)
    return _grid_call(
        body, name, t // tm,
        [row, row, _resident((1, d)), act, act, _resident(wg.shape), _resident(wu.shape), _resident(wd.shape)],
        [row, row, act, act, act, pl.BlockSpec((1, d), lambda i: (0, 0))],
        [jax.ShapeDtypeStruct((t, d), F32), jax.ShapeDtypeStruct((t, d), BF16),
         act_shape, act_shape, act_shape, jax.ShapeDtypeStruct((1, d), F32)],
        (dy, x, nw, g, u, wg, wu, wd), exchange=exchange)


def _matmul_tn(a, b, tk, name, exchange=None):
    a3, b3 = a.ndim == 3, b.ndim == 3
    nj = a.shape[0] if a3 else (b.shape[0] if b3 else 1)
    t, m = a.shape[-2:]
    n = b.shape[-1]
    nt = t // tk

    def body(a_ref, b_ref, o_ref, acc_ref):
        k = pl.program_id(0) % nt

        @pl.when(k == 0)
        def _():
            acc_ref[...] = jnp.zeros_like(acc_ref)

        acc_ref[...] += lax.dot_general(a_ref[...], b_ref[...], (((0,), (0,)), ((), ())),
                                        preferred_element_type=F32)

        @pl.when(k == nt - 1)
        def _():
            o_ref[...] = acc_ref[...].astype(o_ref.dtype)

    a_spec = (pl.BlockSpec((None, tk, m), lambda i: (i // nt, i % nt, 0)) if a3
              else pl.BlockSpec((tk, m), lambda i: (i % nt, 0)))
    b_spec = (pl.BlockSpec((None, tk, n), lambda i: (i // nt, i % nt, 0)) if b3
              else pl.BlockSpec((tk, n), lambda i: (i % nt, 0)))
    (out,), landed = _grid_call(
        body, name, nj * nt, [a_spec, b_spec], [pl.BlockSpec((None, m, n), lambda i: (i // nt, 0, 0))],
        [jax.ShapeDtypeStruct((nj, m, n), BF16)], (a, b), scratch=[pltpu.VMEM((m, n), F32)], exchange=exchange)
    return out if exchange is None else (out, landed)


N_GATE_COLS = 4 * GDN_HEADS
P_QKVA, P_Z, P_AB, P_QKVB = (0, 1536), (1536, 2048), (2048, 2048 + LANES), (2048 + N_GATE_COLS, 3600)
P_PIECES = (P_QKVA, P_Z, P_AB, P_QKVB)


def _mix_in_fwd(x1, nw, wp, tm):
    t, d = x1.shape

    def body(x_ref, nw_ref, w_ref, hn_ref, *outs):
        xf = x_ref[...]
        xn = (xf * _rstd(xf) * nw_ref[...]).astype(BF16)
        hn_ref[...] = xn
        for (a, b), o_ref in zip(P_PIECES, outs):
            o_ref[...] = _dot_nt(xn, w_ref[a:b, :])

    row = pl.BlockSpec((tm, d), lambda i: (i, 0))
    return pl.pallas_call(
        body, name="mix_in_fwd", grid=(t // tm,),
        in_specs=[row, _resident((1, d)), _resident(wp.shape)],
        out_specs=[row] + [pl.BlockSpec((tm, b - a), lambda i: (i, 0)) for a, b in P_PIECES],
        out_shape=[jax.ShapeDtypeStruct((t, d), BF16)]
                  + [jax.ShapeDtypeStruct((t, b - a), F32) for a, b in P_PIECES],
        compiler_params=_params(("arbitrary",), VMEM_LIMIT),
    )(x1, nw, wp)


def _mix_in_bwd_dx(dx, x1, nw, dpieces, wp, tm, exchange=None):
    t, d = x1.shape

    def body(dx_ref, x_ref, nw_ref, p0, p1, p2, p3, w_ref, o_ref, dnw_ref):
        @pl.when(pl.program_id(0) == 0)
        def _():
            dnw_ref[...] = jnp.zeros_like(dnw_ref)

        dh = jnp.zeros((tm, d), F32)
        for (a, b), p_ref in zip(P_PIECES, (p0, p1, p2, p3)):
            dh = dh + _dot(p_ref[...], w_ref[a:b, :])
        xf = x_ref[...]
        dxr, dnw = _rms_bwd(xf, _rstd(xf), nw_ref[...], dh)
        o_ref[...] = dx_ref[...] + dxr
        dnw_ref[...] += dnw

    row = pl.BlockSpec((tm, d), lambda i: (i, 0))
    return _grid_call(
        body, "mix_in_bwd_dx", t // tm,
        [row, row, _resident((1, d))]
        + [pl.BlockSpec((tm, b - a), lambda i: (i, 0)) for a, b in P_PIECES] + [_resident(wp.shape)],
        [row, pl.BlockSpec((1, d), lambda i: (0, 0))],
        [jax.ShapeDtypeStruct((t, d), F32), jax.ShapeDtypeStruct((1, d), F32)],
        (dx, x1, nw, *dpieces, wp), exchange=exchange)


def _mix_out_fwd(x1, oa, ob, w, tm):
    t, d = x1.shape
    half = oa.shape[1]
    n = t // tm
    ahead = RING_SLOTS - 1

    def body(oa_ref, ob_ref, w_ref, x_hbm, o_ref, ring, sems):
        s = pl.program_id(0)

        def fetch(step):
            slot = step % RING_SLOTS
            return pltpu.make_async_copy(x_hbm.at[pl.ds(_aligned(step * tm, tm), tm), :], ring.at[slot], sems.at[slot])

        @pl.when(s == 0)
        def _():
            for first in range(ahead):
                fetch(first).start()

        @pl.when(s + ahead < n)
        def _():
            fetch(s + ahead).start()

        fetch(s).wait()
        o_ref[...] = (ring[s % RING_SLOTS] + _dot(oa_ref[...], w_ref[0:half, :])
                      + _dot(ob_ref[...], w_ref[half:2 * half, :]))

    assert n >= ahead
    hrow = pl.BlockSpec((tm, half), lambda i: (i, 0))
    return pl.pallas_call(
        body, name="mix_out_fwd", grid=(n,),
        in_specs=[hrow, hrow, _resident(w.shape), ANY],
        out_specs=pl.BlockSpec((tm, d), lambda i: (i, 0)), out_shape=jax.ShapeDtypeStruct((t, d), F32),
        scratch_shapes=[pltpu.VMEM((RING_SLOTS, tm, d), F32), pltpu.SemaphoreType.DMA((RING_SLOTS,))],
        compiler_params=_params(("arbitrary",), VMEM_LIMIT),
    )(oa, ob, w, x1)


def _mix_out_bwd(dx2, w, tm, exchange=None):
    t, d = dx2.shape
    half = w.shape[0] // 2

    def body(dx_ref, w_ref, doa_ref, dob_ref, dxb_ref):
        dxb = dx_ref[...].astype(BF16)
        dxb_ref[...] = dxb
        doa_ref[...] = _dot_nt(dxb, w_ref[0:half, :])
        dob_ref[...] = _dot_nt(dxb, w_ref[half:2 * half, :])

    row = pl.BlockSpec((tm, d), lambda i: (i, 0))
    hrow = pl.BlockSpec((tm, half), lambda i: (i, 0))
    return _grid_call(
        body, "mix_out_bwd", t // tm, [row, _resident(w.shape)], [hrow, hrow, row],
        [jax.ShapeDtypeStruct((t, half), F32), jax.ShapeDtypeStruct((t, half), F32), jax.ShapeDtypeStruct((t, d), BF16)],
        (dx2, w), exchange=exchange)


HALO = 8


def _halo_row_specs(tr, cols, nrow8):
    per = tr // HALO
    return [pl.BlockSpec((tr, cols), lambda i: (i, 0)),
            pl.BlockSpec((HALO, cols), lambda i: (jnp.maximum(i * per - 1, 0), 0)),
            pl.BlockSpec((HALO, cols), lambda i: (jnp.minimum((i + 1) * per, nrow8 - 1), 0))]


def _fill_window(win_ref, cb, xm, xp, xn, first, last):
    tr = xm.shape[0]
    cols = slice(cb * LANES, (cb + 1) * LANES)
    win_ref[cb, 0:HALO, :] = jnp.where(first, 0.0, xp[:, cols])
    win_ref[cb, HALO:HALO + tr, :] = xm[:, cols]
    win_ref[cb, HALO + tr:HALO + tr + HALO, :] = jnp.where(last, 0.0, xn[:, cols])


def _conv_taps(win_ref, cb, cw_ref, start, rows):
    cols = slice(cb * LANES, (cb + 1) * LANES)
    acc = None
    for j in range(CONV_TAPS):
        term = win_ref[cb, pl.ds(start + j - CONV_TAPS // 2, rows), :] * cw_ref[j:j + 1, cols]
        acc = term if acc is None else acc + term
    return acc


def _softplus(x):
    u = jnp.exp(-jnp.abs(x))
    w = 1.0 + u
    log1p = jnp.where(w == 1.0, u, jnp.log(w) * u / jnp.where(w == 1.0, 1.0, w - 1.0))
    return jnp.maximum(x, 0.0) + log1p


def _gdn_prep_fwd(qkva, cw, ab, gp, tr):
    t, c = qkva.shape
    nt = t // tr
    ncb = c // LANES

    def body(xm, xp, xn, cw_ref, ab_ref, gp_ref, o_ref, gb_ref, xw_ref):
        i = pl.program_id(0)
        first, last = i == 0, i == nt - 1
        for cb in range(ncb):
            cols = slice(cb * LANES, (cb + 1) * LANES)
            _fill_window(xw_ref, cb, xm, xp, xn, first, last)
            pre = _conv_taps(xw_ref, cb, cw_ref, HALO, tr)
            y = pre * _sigmoid(pre)
            if cb < 2 * GDN_HEADS:
                y = y * lax.rsqrt(jnp.sum(y * y, axis=-1, keepdims=True) + EPS)
            if cb < GDN_HEADS:
                y = y * (GDN_DIM ** -0.5)
            o_ref[:, cols] = y
        abv = ab_ref[...]
        lane = lax.broadcasted_iota(jnp.int32, abv.shape, 1)
        g = -jnp.exp(gp_ref[0:1, :]) * _softplus(abv + gp_ref[1:2, :])
        gb_ref[...] = jnp.where(lane < 8, g, jnp.where(lane < 16, _sigmoid(abv), 0.0))

    return pl.pallas_call(
        body, name="gdn_prep_fwd", grid=(nt,),
        in_specs=_halo_row_specs(tr, c, t // HALO)
                 + [_resident(cw.shape), pl.BlockSpec((tr, LANES), lambda i: (i, 0)), _resident(gp.shape)],
        out_specs=[pl.BlockSpec((tr, c), lambda i: (i, 0)), pl.BlockSpec((tr, LANES), lambda i: (i, 0))],
        out_shape=[jax.ShapeDtypeStruct((t, c), F32), jax.ShapeDtypeStruct((t, LANES), F32)],
        scratch_shapes=[pltpu.VMEM((ncb, tr + 2 * HALO, LANES), F32)],
        compiler_params=_params(("arbitrary",), VMEM_LIMIT),
    )(qkva, qkva, qkva, cw, ab, gp)


def _gdn_prep_bwd(qkva, cw, ab, gp, dy, dgates, tr):
    t, c = qkva.shape
    nt = t // tr
    ncb = c // LANES

    ext = HALO // 2
    rows_ext = tr + 2 * ext

    def body(xm, xp, xn, fm, fp, fn, cw_ref, ab_ref, gp_ref, gf_ref, dx_ref, dab_ref, dcw_ref, dgp_ref,
             xw_ref, dyw_ref, dp_ref):
        i = pl.program_id(0)
        first, last = i == 0, i == nt - 1

        @pl.when(first)
        def _():
            dcw_ref[...] = jnp.zeros_like(dcw_ref)
            dgp_ref[...] = jnp.zeros_like(dgp_ref)

        sub8 = lax.broadcasted_iota(jnp.int32, (8, LANES), 0)
        for cb in range(ncb):
            cols = slice(cb * LANES, (cb + 1) * LANES)
            _fill_window(xw_ref, cb, xm, xp, xn, first, last)
            _fill_window(dyw_ref, cb, fm, fp, fn, first, last)
            pre = _conv_taps(xw_ref, cb, cw_ref, HALO - ext, rows_ext)
            dyw = dyw_ref[cb, pl.ds(HALO - ext, rows_ext), :]
            sg = _sigmoid(pre)
            s = pre * sg
            if cb < 2 * GDN_HEADS:
                scale = (GDN_DIM ** -0.5) if cb < GDN_HEADS else 1.0
                r = lax.rsqrt(jnp.sum(s * s, axis=-1, keepdims=True) + EPS)
                dn = dyw * scale
                ds = r * dn - s * (r * r * r) * jnp.sum(dn * s, axis=-1, keepdims=True)
            else:
                ds = dyw
            dp_ref[cb] = ds * (sg * (1.0 + pre * (1.0 - sg)))
            dpre = dp_ref[cb, pl.ds(ext, tr), :]
            dx = None
            dcw = jnp.zeros((8, LANES), F32)
            for j in range(CONV_TAPS):
                off = j - CONV_TAPS // 2
                term = dp_ref[cb, pl.ds(ext - off, tr), :] * cw_ref[j:j + 1, cols]
                dx = term if dx is None else dx + term
                tap = jnp.sum(dpre * xw_ref[cb, pl.ds(HALO + off, tr), :], axis=0, keepdims=True)
                dcw = dcw + jnp.where(sub8 == j, tap, 0.0)
            dx_ref[:, cols] = dx.astype(BF16)
            dcw_ref[:, cols] += dcw

        abv = ab_ref[...]
        dgb = gf_ref[...]
        lane = lax.broadcasted_iota(jnp.int32, abv.shape, 1)
        nea = -jnp.exp(gp_ref[0:1, :])
        xs = abv + gp_ref[1:2, :]
        g = nea * _softplus(xs)
        beta = _sigmoid(abv)
        da = dgb * nea * _sigmoid(xs)
        dab = jnp.where(lane < 8, da, jnp.where(lane < 16, dgb * beta * (1.0 - beta), 0.0))
        dab_ref[...] = dab.astype(BF16)
        keep = lane[0:1, :] < 8
        dalog = jnp.where(keep, jnp.sum(dgb * g, axis=0, keepdims=True), 0.0)
        ddtb = jnp.where(keep, jnp.sum(da, axis=0, keepdims=True), 0.0)
        dgp_ref[...] += jnp.where(sub8 == 0, dalog, 0.0) + jnp.where(sub8 == 1, ddtb, 0.0)

    lrow = pl.BlockSpec((tr, LANES), lambda i: (i, 0))
    halo = _halo_row_specs(tr, c, t // HALO)
    return pl.pallas_call(
        body, name="gdn_prep_bwd", grid=(nt,),
        in_specs=halo + halo + [_resident(cw.shape), lrow, _resident(gp.shape), lrow],
        out_specs=[pl.BlockSpec((tr, c), lambda i: (i, 0)), lrow,
                   pl.BlockSpec(cw.shape, lambda i: (0, 0)), pl.BlockSpec(gp.shape, lambda i: (0, 0))],
        out_shape=[jax.ShapeDtypeStruct((t, c), BF16), jax.ShapeDtypeStruct((t, LANES), BF16),
                   jax.ShapeDtypeStruct(cw.shape, F32), jax.ShapeDtypeStruct(gp.shape, F32)],
        scratch_shapes=[pltpu.VMEM((ncb, tr + 2 * HALO, LANES), F32), pltpu.VMEM((ncb, tr + 2 * HALO, LANES), F32),
                        pltpu.VMEM((ncb, rows_ext, LANES), F32)],
        compiler_params=_params(("arbitrary",), VMEM_LIMIT),
    )(qkva, qkva, qkva, dy, dy, dy, cw, ab, gp, dgates)


def _chunk_masks(lower):
    ii = lax.broadcasted_iota(jnp.int32, (CHUNK, CHUNK), 0)
    jj = lax.broadcasted_iota(jnp.int32, (CHUNK, CHUNK), 1)
    incl = (ii >= jj) if lower else (ii <= jj)
    strict = (ii > jj) if lower else (ii < jj)
    return ii, jj, incl, strict


def _dot3(a, b):
    ah = a.astype(BF16)
    al = (a - ah.astype(F32)).astype(BF16)
    bh = b.astype(BF16)
    bl = (b - bh.astype(F32)).astype(BF16)
    d = lambda u, v: jnp.dot(u, v, preferred_element_type=F32)
    return d(ah, bh) + (d(ah, bl) + d(al, bh))


def _tri_inv_many(lmats, ii, jj):
    m16 = (ii // 16) == (jj // 16)
    m32 = (ii // 32) == (jj // 32)
    eye = jnp.where(ii == jj, 1.0, 0.0)
    l16 = [jnp.where(m16, l, 0.0) for l in lmats]
    p2 = [_dot3(a, a) for a in l16]
    p4 = [_dot3(a, a) for a in p2]
    p8 = [_dot3(a, a) for a in p4]
    xs = [eye - a for a in l16]
    for ps in (p2, p4, p8):
        xs = [x + _dot3(x, p) for x, p in zip(xs, ps)]
    for off in ([jnp.where(m32 & jnp.logical_not(m16), l, 0.0) for l in lmats],
                [jnp.where(m32, 0.0, l) for l in lmats]):
        ys = [_dot3(x, c) for x, c in zip(xs, off)]
        xs = [x - _dot3(y, x) for x, y in zip(xs, ys)]
    return xs


def _col_to_row(col, ii, jj):
    return jnp.sum(jnp.where(ii == jj, col, 0.0), axis=0, keepdims=True)


def _row_to_col(row, ii, jj):
    return jnp.sum(jnp.where(ii == jj, row, 0.0), axis=1, keepdims=True)


def _chain_common(q, k, v, graw_col, graw_row, bcol, masks):
    ii, jj, incl, strict = masks
    inclt = jnp.logical_not(strict)
    gcol = jnp.sum(jnp.where(incl, graw_row, 0.0), axis=1, keepdims=True)
    grow = jnp.sum(jnp.where(inclt, graw_col, 0.0), axis=0, keepdims=True)
    glast = jnp.sum(graw_row, axis=1, keepdims=True)
    decay = jnp.where(incl, jnp.exp(jnp.where(incl, gcol - grow, 0.0)), 0.0)
    kb = k * bcol
    vb = v * bcol
    eg = jnp.exp(gcol)
    ek = jnp.exp(glast - gcol)
    kbg = kb * eg
    amat = _dot_nt(kb, k)
    qk = _dot_nt(q, k)
    return dict(gcol=gcol, glast=glast, decay=decay, kb=kb, vb=vb, eg=eg, ek=ek, kbg=kbg, amat=amat, qk=qk,
                intra=qk * decay, qg=q * eg, kdec=k * ek)


def _gdn_fwd(qkvc, gb, gbt):
    tm, u, w, qg, kd, intra, egl = _gdn_local_fwd(qkvc, gb, gbt)
    o_f, o_b, s_f, s_b, vn_f, vn_b = _gdn_scan_fwd(u, w, qg, kd, intra, egl, qkvc.shape[0])
    return o_f, o_b, dict(tm=tm, w=w, qg=qg, kd=kd, intra=intra, egl=egl, s=(s_f, s_b), vn=(vn_f, vn_b))


N_CHAINS = 2 * GDN_HEADS


LOCAL_CHUNKS = 4


def _load_chains(x_ref, g_ref, gt_ref, cc=0):
    hd = GDN_HEADS * GDN_DIM
    rows = slice(cc * CHUNK, (cc + 1) * CHUNK)
    chains = []
    for d in range(2):
        masks = _chunk_masks(d == 0)
        for h in range(GDN_HEADS):
            ch = d * GDN_HEADS + h
            q = x_ref[rows, h * GDN_DIM:(h + 1) * GDN_DIM]
            k = x_ref[rows, hd + h * GDN_DIM:hd + (h + 1) * GDN_DIM]
            v = x_ref[rows, 2 * hd + h * GDN_DIM:2 * hd + (h + 1) * GDN_DIM]
            bcol = g_ref[rows, 8 + ch:9 + ch]
            cm = _chain_common(q, k, v, g_ref[rows, ch:ch + 1], gt_ref[cc, ch:ch + 1, :], bcol, masks)
            chains.append(dict(cm, q=q, k=k, v=v, bcol=bcol, masks=masks, ch=ch, h=h, cc=cc))
    return chains


def _chain_shape(rows, cols, dtype):
    return lambda nc: jax.ShapeDtypeStruct((nc, N_CHAINS, rows, cols), dtype)


def _gdn_local_fwd(qkvc, gb, gbt):
    t = qkvc.shape[0]
    nc = t // CHUNK
    hd = GDN_HEADS * GDN_DIM

    def body(x_ref, g_ref, gt_ref, t_ref, u_ref, w_ref, qg_ref, kd_ref, in_ref, eg_ref):
        chains = [c for cc in range(LOCAL_CHUNKS) for c in _load_chains(x_ref, g_ref, gt_ref, cc)]
        ii, jj = chains[0]["masks"][0:2]
        tms = _tri_inv_many([jnp.where(c["masks"][3], c["amat"] * c["decay"], 0.0) for c in chains], ii, jj)
        uws = [_dot(tm, jnp.concatenate([c["vb"], c["kbg"]], axis=1)) for tm, c in zip(tms, chains)]
        for c, tm, uw in zip(chains, tms, uws):
            cc, ch = c["cc"], c["ch"]
            t_ref[cc, ch] = tm
            u_ref[cc, ch] = uw[:, :GDN_DIM]
            w_ref[cc, ch] = uw[:, GDN_DIM:].astype(BF16)
            qg_ref[cc, ch] = c["qg"].astype(BF16)
            kd_ref[cc, ch] = c["kdec"].astype(BF16)
            in_ref[cc, ch] = c["intra"].astype(BF16)
            eg_ref[cc, ch:ch + 1, :] = jnp.broadcast_to(jnp.exp(c["glast"]), (1, LANES))

    lc = LOCAL_CHUNKS
    blk = lambda rows, cols: pl.BlockSpec((lc, N_CHAINS, rows, cols), lambda n: (n, 0, 0, 0))
    shapes = [_chain_shape(CHUNK, CHUNK, F32), _chain_shape(CHUNK, GDN_DIM, F32), _chain_shape(CHUNK, GDN_DIM, BF16),
              _chain_shape(CHUNK, GDN_DIM, BF16), _chain_shape(CHUNK, GDN_DIM, BF16), _chain_shape(CHUNK, CHUNK, BF16)]
    return tuple(pl.pallas_call(
        body, name="gdn_local_fwd", grid=(nc // lc,),
        in_specs=[pl.BlockSpec((lc * CHUNK, 3 * hd), lambda n: (n, 0)), pl.BlockSpec((lc * CHUNK, LANES), lambda n: (n, 0)),
                  pl.BlockSpec((lc, 16, CHUNK), lambda n: (n, 0, 0))],
        out_specs=[blk(CHUNK, CHUNK), blk(CHUNK, GDN_DIM), blk(CHUNK, GDN_DIM), blk(CHUNK, GDN_DIM),
                   blk(CHUNK, GDN_DIM), blk(CHUNK, CHUNK), pl.BlockSpec((lc, N_CHAINS, LANES), lambda n: (n, 0, 0))],
        out_shape=[s(nc) for s in shapes] + [jax.ShapeDtypeStruct((nc, N_CHAINS, LANES), F32)],
        compiler_params=_params(("arbitrary",), VMEM_LIMIT),
    )(qkvc, gb, gbt))


SCAN_CHUNKS = 8


def _dir_specs(nc, rev):
    nb = nc // SCAN_CHUNKS

    def spec(d, rows, cols, own=False):
        chunk = (lambda n: n) if (d == 0) != rev else (lambda n: nb - 1 - n)
        blk = 0 if own else d
        if rows is None:
            return pl.BlockSpec((SCAN_CHUNKS, GDN_HEADS if own else N_CHAINS, cols), lambda n: (chunk(n), 0, 0))
        return pl.BlockSpec((SCAN_CHUNKS, GDN_HEADS, rows, cols), lambda n: (chunk(n), blk, 0, 0))

    def rows_spec(d, cols):
        chunk = (lambda n: n) if (d == 0) != rev else (lambda n: nb - 1 - n)
        return pl.BlockSpec((SCAN_CHUNKS * CHUNK, cols), lambda n: (chunk(n), 0))

    def order(d):
        return list(range(SCAN_CHUNKS)) if (d == 0) != rev else list(range(SCAN_CHUNKS - 1, -1, -1))
    return spec, rows_spec, order


def _gdn_scan_fwd(u, w, qg, kd, intra, egl, t):
    nc = t // CHUNK
    hd = GDN_HEADS * GDN_DIM

    def body(*refs):
        ins, outs, state = refs[:12], refs[12:18], refs[18]
        @pl.when(pl.program_id(0) == 0)
        def _():
            state[...] = jnp.zeros_like(state)

        chains = [(d, h) for d in range(2) for h in range(GDN_HEADS)]
        states = [state[ch] for ch in range(N_CHAINS)]
        for step in range(SCAN_CHUNKS):
            at = [order(d)[step] for d in range(2)]
            pick = lambda k, d, h: ins[2 * k + d][at[d], h]
            sbs = [s.astype(BF16) for s in states]
            ws = [_dot(pick(1, d, h), sb) for (d, h), sb in zip(chains, sbs)]
            o1 = [_dot(pick(2, d, h), sb) for (d, h), sb in zip(chains, sbs)]
            vns = [(pick(0, d, h) - wsb).astype(BF16) for (d, h), wsb in zip(chains, ws)]
            o2 = [_dot(pick(4, d, h), vn) for (d, h), vn in zip(chains, vns)]
            kv = [_dot_tn(pick(3, d, h), vn) for (d, h), vn in zip(chains, vns)]
            new_states = []
            for ch, (d, h) in enumerate(chains):
                outs[d][at[d] * CHUNK:(at[d] + 1) * CHUNK, h * GDN_DIM:(h + 1) * GDN_DIM] = o1[ch] + o2[ch]
                outs[2 + d][at[d], h] = sbs[ch]
                outs[4 + d][at[d], h] = vns[ch]
                new_states.append(states[ch] * ins[10 + d][at[d], ch:ch + 1, :] + kv[ch])
            states = new_states
        for ch in range(N_CHAINS):
            state[ch] = states[ch]

    spec, rows_spec, order = _dir_specs(nc, False)
    pair = lambda rows, cols, own=False: [spec(0, rows, cols, own), spec(1, rows, cols, own)]
    s_shape = jax.ShapeDtypeStruct((nc, GDN_HEADS, GDN_DIM, GDN_DIM), BF16)
    vn_shape = jax.ShapeDtypeStruct((nc, GDN_HEADS, CHUNK, GDN_DIM), BF16)
    return pl.pallas_call(
        body, name="gdn_scan_fwd", grid=(nc // SCAN_CHUNKS,),
        in_specs=(pair(CHUNK, GDN_DIM) + pair(CHUNK, GDN_DIM) + pair(CHUNK, GDN_DIM) + pair(CHUNK, GDN_DIM)
                  + pair(CHUNK, CHUNK) + pair(None, LANES)),
        out_specs=([rows_spec(0, hd), rows_spec(1, hd)] + pair(GDN_DIM, GDN_DIM, True)
                   + pair(CHUNK, GDN_DIM, True)),
        out_shape=[jax.ShapeDtypeStruct((t, hd), F32), jax.ShapeDtypeStruct((t, hd), F32),
                   s_shape, s_shape, vn_shape, vn_shape],
        scratch_shapes=[pltpu.VMEM((N_CHAINS, GDN_DIM, GDN_DIM), F32)],
        compiler_params=_params(("arbitrary",), VMEM_LIMIT),
    )(u, u, w, w, qg, qg, kd, kd, intra, intra, egl, egl)


def _gdn_bwd(qkvc, gb, gbt, do, saved, exchange=None):
    scan = _gdn_scan_bwd(do, saved, qkvc.shape[0])
    return _gdn_local_bwd(qkvc, gb, gbt, do, saved, scan, exchange)


def _gdn_scan_bwd(do, saved, t):
    nc = t // CHUNK
    hd = GDN_HEADS * GDN_DIM

    def body(*refs):
        ins, outs, dstate = refs[:16], refs[16:26], refs[26]
        @pl.when(pl.program_id(0) == 0)
        def _():
            dstate[...] = jnp.zeros_like(dstate)

        chains = [(d, h) for d in range(2) for h in range(GDN_HEADS)]
        dss = [dstate[ch] for ch in range(N_CHAINS)]
        for step in range(SCAN_CHUNKS):
            at = [order(d)[step] for d in range(2)]
            pick = lambda k, d, h: ins[2 * k + d][at[d], h]
            dsbs = [ds.astype(BF16) for ds in dss]
            ss = [pick(1, d, h) for d, h in chains]
            sbs = ss
            dos = [ins[d][at[d] * CHUNK:(at[d] + 1) * CHUNK, h * GDN_DIM:(h + 1) * GDN_DIM].astype(BF16)
                   for d, h in chains]
            dv1 = [_dot_tn(pick(5, d, h), dov) for (d, h), dov in zip(chains, dos)]
            dv2 = [_dot(pick(4, d, h), dsb) for (d, h), dsb in zip(chains, dsbs)]
            ds1 = [_dot_tn(pick(3, d, h), dov) for (d, h), dov in zip(chains, dos)]
            dkds = [_dot_nt(pick(6, d, h), dsb) for (d, h), dsb in zip(chains, dsbs)]
            dqgs = [_dot_nt(dov, sb) for dov, sb in zip(dos, sbs)]
            dvns = [(a + b).astype(BF16) for a, b in zip(dv1, dv2)]
            ds2 = [_dot_tn(pick(2, d, h), dvn) for (d, h), dvn in zip(chains, dvns)]
            dws = [_dot_nt(dvn, sb) for dvn, sb in zip(dvns, sbs)]
            new_dss = []
            for ch, (d, h) in enumerate(chains):
                egl = ins[14 + d][at[d], ch:ch + 1, :]
                outs[d][at[d], h] = dvns[ch]
                outs[2 + d][at[d], h] = (-dws[ch]).astype(BF16)
                outs[4 + d][at[d], h] = dqgs[ch]
                outs[6 + d][at[d], h] = dkds[ch]
                outs[8 + d][at[d], h:h + 1, :] = egl * jnp.sum(jnp.sum(ss[ch].astype(F32) * dss[ch], axis=1, keepdims=True),
                                                               axis=0, keepdims=True)
                new_dss.append(ds1[ch] + egl * dss[ch] - ds2[ch])
            dss = new_dss
        for ch in range(N_CHAINS):
            dstate[ch] = dss[ch]

    spec, rows_spec, order = _dir_specs(nc, True)
    pair = lambda rows, cols, own=False: [spec(0, rows, cols, own), spec(1, rows, cols, own)]
    s_f, s_b = saved["s"]
    vn_f, vn_b = saved["vn"]
    w, qg, kd, intra, egl = saved["w"], saved["qg"], saved["kd"], saved["intra"], saved["egl"]
    own = lambda rows, cols, dtype: jax.ShapeDtypeStruct((nc, GDN_HEADS, rows, cols), dtype)
    row_shape = jax.ShapeDtypeStruct((nc, GDN_HEADS, LANES), F32)
    return pl.pallas_call(
        body, name="gdn_scan_bwd", grid=(nc // SCAN_CHUNKS,),
        in_specs=([rows_spec(0, hd), rows_spec(1, hd)] + pair(GDN_DIM, GDN_DIM, True) + pair(CHUNK, GDN_DIM)
                  + pair(CHUNK, GDN_DIM) + pair(CHUNK, GDN_DIM) + pair(CHUNK, CHUNK) + pair(CHUNK, GDN_DIM, True)
                  + pair(None, LANES)),
        out_specs=(pair(CHUNK, GDN_DIM, True) + pair(CHUNK, GDN_DIM, True) + pair(CHUNK, GDN_DIM, True)
                   + pair(CHUNK, GDN_DIM, True) + pair(None, LANES, True)),
        out_shape=[own(CHUNK, GDN_DIM, BF16)] * 4 + [own(CHUNK, GDN_DIM, F32)] * 4 + [row_shape] * 2,
        scratch_shapes=[pltpu.VMEM((N_CHAINS, GDN_DIM, GDN_DIM), F32)],
        compiler_params=_params(("arbitrary",), VMEM_LIMIT),
    )(do, do, s_f, s_b, w, w, qg, qg, kd, kd, intra, intra, vn_f, vn_b, egl, egl)


def _dot3_nt(a, b):
    ah = a.astype(BF16)
    al = (a - ah.astype(F32)).astype(BF16)
    bh = b.astype(BF16)
    bl = (b - bh.astype(F32)).astype(BF16)
    return _dot_nt(ah, bh) + (_dot_nt(ah, bl) + _dot_nt(al, bh))


def _dot3_tn(a, b):
    ah = a.astype(BF16)
    al = (a - ah.astype(F32)).astype(BF16)
    bh = b.astype(BF16)
    bl = (b - bh.astype(F32)).astype(BF16)
    return _dot_tn(ah, bh) + (_dot_tn(ah, bl) + _dot_tn(al, bh))


def _gdn_local_bwd(qkvc, gb, gbt, do, saved, scan, exchange=None):
    t = qkvc.shape[0]
    nc = t // CHUNK
    hd = GDN_HEADS * GDN_DIM

    def body(*refs):
        x_ref, g_ref, gt_ref, do_ref, t_ref = refs[:5]
        per_dir = refs[5:17]
        dx_ref, dg_ref = refs[17:]
        chains = [c for cc in range(LOCAL_CHUNKS) for c in _load_chains(x_ref, g_ref, gt_ref, cc)]
        lane = lax.broadcasted_iota(jnp.int32, (CHUNK, LANES), 1)
        dgates = [jnp.zeros((CHUNK, LANES), F32) for _ in range(LOCAL_CHUNKS)]
        for c in chains:
            d = c["ch"] // GDN_HEADS
            vn_ref, dvn_ref, dw_ref, dqg_ref, dkd_ref, dgl_ref = per_dir[d::2]
            h, cc = c["h"], c["cc"]
            rows = slice(cc * CHUNK, (cc + 1) * CHUNK)
            c.update(tm=t_ref[cc, c["ch"]], dov=do_ref[rows, h * GDN_DIM:(h + 1) * GDN_DIM], vnew=vn_ref[cc, h],
                     dvnew=dvn_ref[cc, h], dw=dw_ref[cc, h], dqg=dqg_ref[cc, h], dkdec=dkd_ref[cc, h],
                     dglast=dgl_ref[cc, h:h + 1, 0:1])
        dintras = [_dot_nt(c["dov"], c["vnew"]) for c in chains]
        dts = [_dot_nt(c["dvnew"], c["vb"]) + _dot_nt(c["dw"], c["kbg"]) for c in chains]
        dvbs = [_dot_tn(c["tm"], c["dvnew"]) for c in chains]
        dkbgs = [_dot_tn(c["tm"], c["dw"]) for c in chains]
        tdts = [_dot3_nt(dt, c["tm"]) for dt, c in zip(dts, chains)]
        dls = [jnp.where(c["masks"][3], -_dot3_tn(c["tm"], tdt), 0.0) for tdt, c in zip(tdts, chains)]
        das = [dl * c["decay"] for dl, c in zip(dls, chains)]
        dqks = [jnp.where(c["masks"][2], di, 0.0) * c["decay"] for di, c in zip(dintras, chains)]
        dkb1 = [_dot(da, c["k"]) for da, c in zip(das, chains)]
        dk1 = [_dot_tn(da, c["kb"]) for da, c in zip(das, chains)]
        dk2 = [_dot_tn(dqk, c["q"]) for dqk, c in zip(dqks, chains)]
        dq1 = [_dot(dqk, c["k"]) for dqk, c in zip(dqks, chains)]
        grads, mms, p_gs, p_betas, p_kds = [], [], [], [], []
        for n, c in enumerate(chains):
            incl = c["masks"][2]
            dkb = dkb1[n] + dkbgs[n] * c["eg"]
            kd = c["dkdec"] * c["kdec"]
            mms.append((dls[n] * c["amat"] + jnp.where(incl, dintras[n], 0.0) * c["qk"]) * c["decay"])
            p_gs.append(c["dqg"] * c["qg"] - kd + dkbgs[n] * c["kbg"])
            p_betas.append(dkb * c["k"] + dvbs[n] * c["v"])
            p_kds.append(kd)
            grads.append((dq1[n] + c["dqg"] * c["eg"],
                          dk1[n] + dk2[n] + c["dkdec"] * c["ek"] + dkb * c["bcol"],
                          dvbs[n] * c["bcol"]))
        row_sums = [jnp.sum(mm, axis=1, keepdims=True) for mm in mms]
        col_sums = [jnp.sum(mm, axis=0, keepdims=True) for mm in mms]
        g_sums = [jnp.sum(pg, axis=1, keepdims=True) for pg in p_gs]
        dbetas = [jnp.sum(pb, axis=1, keepdims=True) for pb in p_betas]
        kd_tots = [jnp.sum(jnp.sum(pk, axis=1, keepdims=True), axis=0, keepdims=True) for pk in p_kds]
        dgcs = [rs - _row_to_col(cs, *c["masks"][0:2]) + gs for rs, cs, gs, c in zip(row_sums, col_sums, g_sums, chains)]
        dgrs = [_col_to_row(dgc, *c["masks"][0:2]) for dgc, c in zip(dgcs, chains)]
        draws = [jnp.sum(jnp.where(jnp.logical_not(c["masks"][3]), dgr, 0.0), axis=1, keepdims=True) + c["dglast"] + kt
                 for dgr, kt, c in zip(dgrs, kd_tots, chains)]
        for c, draw, dbeta in zip(chains, draws, dbetas):
            ch = c["ch"]
            dgates[c["cc"]] = dgates[c["cc"]] + jnp.where(lane == ch, draw, 0.0) + jnp.where(lane == 8 + ch, dbeta, 0.0)
        for cc in range(LOCAL_CHUNKS):
            rows = slice(cc * CHUNK, (cc + 1) * CHUNK)
            for h in range(GDN_HEADS):
                for part in range(3):
                    cols = slice(part * hd + h * GDN_DIM, part * hd + (h + 1) * GDN_DIM)
                    dx_ref[rows, cols] = grads[cc * N_CHAINS + h][part] + grads[cc * N_CHAINS + GDN_HEADS + h][part]
            dg_ref[rows, :] = dgates[cc]

    lc = LOCAL_CHUNKS
    all8 = lambda rows, cols: pl.BlockSpec((lc, N_CHAINS, rows, cols), lambda n: (n, 0, 0, 0))
    own4 = lambda rows, cols: pl.BlockSpec((lc, GDN_HEADS, rows, cols), lambda n: (n, 0, 0, 0))
    row4 = pl.BlockSpec((lc, GDN_HEADS, LANES), lambda n: (n, 0, 0))
    vn_f, vn_b = saved["vn"]
    dvn_f, dvn_b, dw_f, dw_b, dqg_f, dqg_b, dkd_f, dkd_b, dgl_f, dgl_b = scan
    return _grid_call(
        body, "gdn_local_bwd", nc // lc,
        [pl.BlockSpec((lc * CHUNK, 3 * hd), lambda n: (n, 0)), pl.BlockSpec((lc * CHUNK, LANES), lambda n: (n, 0)),
         pl.BlockSpec((lc, 16, CHUNK), lambda n: (n, 0, 0)), pl.BlockSpec((lc * CHUNK, hd), lambda n: (n, 0)),
         all8(CHUNK, CHUNK)] + [own4(CHUNK, GDN_DIM)] * 10 + [row4, row4],
        [pl.BlockSpec((lc * CHUNK, 3 * hd), lambda n: (n, 0)), pl.BlockSpec((lc * CHUNK, LANES), lambda n: (n, 0))],
        [jax.ShapeDtypeStruct((t, 3 * hd), F32), jax.ShapeDtypeStruct((t, LANES), F32)],
        (qkvc, gb, gbt, do, saved["tm"], vn_f, vn_b, dvn_f, dvn_b, dw_f, dw_b, dqg_f, dqg_b, dkd_f, dkd_b, dgl_f, dgl_b),
        exchange=exchange)


def _gdn_post_fwd(of, ob, z, gw, tm):
    t, hd = of.shape

    def body(of_ref, ob_ref, z_ref, w_ref, o_ref):
        for h in range(GDN_HEADS):
            cols = slice(h * GDN_DIM, (h + 1) * GDN_DIM)
            o = of_ref[:, cols] + ob_ref[:, cols]
            zv = z_ref[:, cols]
            o_ref[:, cols] = (o * _rstd(o) * w_ref[...] * (zv * _sigmoid(zv))).astype(BF16)

    row = pl.BlockSpec((tm, hd), lambda i: (i, 0))
    return pl.pallas_call(
        body, name="gdn_post_fwd", grid=(t // tm,),
        in_specs=[row, row, row, _resident((1, GDN_DIM))],
        out_specs=row, out_shape=jax.ShapeDtypeStruct((t, hd), BF16),
        compiler_params=_params(("arbitrary",), VMEM_LIMIT),
    )(of, ob, z, gw)


def _gdn_post_bwd(doa, of, ob, z, gw, tm):
    t, hd = of.shape

    def body(d_ref, of_ref, ob_ref, z_ref, w_ref, do_ref, dz_ref, dw_ref):
        @pl.when(pl.program_id(0) == 0)
        def _():
            dw_ref[...] = jnp.zeros_like(dw_ref)

        dw = jnp.zeros((1, GDN_DIM), F32)
        for h in range(GDN_HEADS):
            cols = slice(h * GDN_DIM, (h + 1) * GDN_DIM)
            o = of_ref[:, cols] + ob_ref[:, cols]
            zv = z_ref[:, cols]
            dv = d_ref[:, cols]
            r = _rstd(o)
            sg = _sigmoid(zv)
            on = o * r * w_ref[...]
            dz_ref[:, cols] = (dv * on * (sg * (1.0 + zv * (1.0 - sg)))).astype(BF16)
            dxr, dwh = _rms_bwd(o, r, w_ref[...], dv * (zv * sg))
            do_ref[:, cols] = dxr
            dw = dw + dwh
        dw_ref[...] += dw

    row = pl.BlockSpec((tm, hd), lambda i: (i, 0))
    return pl.pallas_call(
        body, name="gdn_post_bwd", grid=(t // tm,),
        in_specs=[row, row, row, row, _resident((1, GDN_DIM))],
        out_specs=[row, row, pl.BlockSpec((1, GDN_DIM), lambda i: (0, 0))],
        out_shape=[jax.ShapeDtypeStruct((t, hd), F32), jax.ShapeDtypeStruct((t, hd), BF16),
                   jax.ShapeDtypeStruct((1, GDN_DIM), F32)],
        compiler_params=_params(("arbitrary",), VMEM_LIMIT),
    )(doa, of, ob, z, gw)


SWA_W = SWA_HEADS * SWA_DIM
QBLK = 128
KWIN = QBLK + 2 * RADIUS
WIN_OFFSETS = (0, RADIUS, 2 * RADIUS)


def _t5_bucket(rel):
    nb = REL_BUCKETS // 2
    bucket = (rel > 0).astype(np.int32) * nb
    n = np.abs(rel)
    max_exact = nb // 2
    large = max_exact + (np.log(np.maximum(n, 1) / max_exact)
                         / math.log(REL_MAX_DISTANCE / max_exact) * (nb - max_exact)).astype(np.int32)
    large = np.minimum(large, nb - 1)
    return (bucket + np.where(n < max_exact, n, large)).astype(np.int32)


def _band_tables(dilation):
    a = np.arange(QBLK)
    b = np.arange(KWIN)
    rel = np.stack([b[None, :] - w0 - a[:, None] for w0 in WIN_OFFSETS])
    return np.where(np.abs(rel) <= RADIUS, _t5_bucket(rel * dilation), -1).astype(np.int32)


BAND_CELLS = len(WIN_OFFSETS) * QBLK * KWIN
BIAS_TILE = BAND_CELLS // 3


def _band_index():
    return jnp.asarray(np.concatenate([_band_tables(d).reshape(-1) for _, d in PATTERNS])[None, :])


def _onehot(idx, dtype):
    return (lax.broadcasted_iota(jnp.int32, (REL_BUCKETS, idx.shape[1]), 0) == idx).astype(dtype)


def _bias_tables(rel_bias, idx, tk):
    n = idx.shape[1]

    def body(rb_ref, i_ref, o_ref):
        iv = i_ref[...]
        oh = _onehot(iv, BF16)
        rest, acc = rb_ref[...], None
        for _ in range(3):
            piece = rest.astype(BF16)
            part = jnp.dot(piece, oh, preferred_element_type=F32)
            acc = part if acc is None else acc + part
            rest = rest - piece.astype(F32)
        o_ref[...] = jnp.where(iv < 0, NEG_BIG, acc)

    return pl.pallas_call(
        body, name="bias_tables", grid=(n // tk,),
        in_specs=[_resident((SWA_HEADS, REL_BUCKETS)), pl.BlockSpec((1, tk), lambda k: (0, k))],
        out_specs=pl.BlockSpec((SWA_HEADS, tk), lambda k: (0, k)),
        out_shape=jax.ShapeDtypeStruct((SWA_HEADS, n), F32),
        compiler_params=_params(("arbitrary",), VMEM_LIMIT),
    )(rel_bias.T, idx)


def _head_mean(x2, bd_ref):
    bd = bd_ref[...]
    rest, acc = x2, None
    for _ in range(3):
        piece = rest.astype(BF16)
        part = jnp.dot(piece, bd, preferred_element_type=F32)
        acc = part if acc is None else acc + part
        rest = rest - piece.astype(F32)
    return acc


VIEW_DILATIONS = tuple(d for _, d in PATTERNS if d > 1)


def _view_spec(tm, d):
    return pl.BlockSpec((tm // d, d * SWA_W), lambda i: (i, 0))


def _view_shape(t, d, dtype):
    return jax.ShapeDtypeStruct((t // d, d * SWA_W), dtype)


N_GROUPS = SWA_W // LANES


def _to_view(src_ref, idx, dst_ref, d, rows):
    for r in range(d):
        for g in range(N_GROUPS):
            cols = slice(r * SWA_W + g * LANES, r * SWA_W + (g + 1) * LANES)
            dst_ref[:, cols] = src_ref[idx, g, pl.ds(r, rows // d, stride=d), :].astype(dst_ref.dtype)


def _from_view(src_ref, dst_ref, idx, d, rows):
    for r in range(d):
        for g in range(N_GROUPS):
            cols = slice(r * SWA_W + g * LANES, r * SWA_W + (g + 1) * LANES)
            dst_ref[idx, g, pl.ds(r, rows // d, stride=d), :] = src_ref[:, cols]


def _swa_prep_fwd(qkvb, qw, kw, bd, tm):
    t = qkvb.shape[0]

    def body(x_ref, qw_ref, kw_ref, bd_ref, *rest):
        outs, sc = rest[:-1], rest[-1]
        for gidx in range(N_GROUPS):
            cols = slice(gidx * LANES, (gidx + 1) * LANES)
            xq = x_ref[:, cols]
            sc[0, gidx] = xq * lax.rsqrt(_head_mean(xq * xq, bd_ref) + EPS) * qw_ref[:, cols] * (SWA_DIM ** -0.5)
            xk = x_ref[:, SWA_W + gidx * LANES:SWA_W + (gidx + 1) * LANES]
            sc[1, gidx] = xk * lax.rsqrt(_head_mean(xk * xk, bd_ref) + EPS) * kw_ref[:, cols]
            sc[2, gidx] = x_ref[:, 2 * SWA_W + gidx * LANES:2 * SWA_W + (gidx + 1) * LANES]
            for i in range(3):
                outs[i][:, cols] = sc[i, gidx].astype(BF16)
        for i in range(3):
            for n, d in enumerate(VIEW_DILATIONS):
                _to_view(sc, i, outs[3 * (n + 1) + i], d, tm)

    return pl.pallas_call(
        body, name="swa_prep_fwd", grid=(t // tm,),
        in_specs=[pl.BlockSpec((tm, 3 * SWA_W), lambda i: (i, 0)), _resident((1, SWA_W)), _resident((1, SWA_W)),
                  _resident((LANES, LANES))],
        out_specs=[_view_spec(tm, d) for d in (1,) + VIEW_DILATIONS for _ in range(3)],
        out_shape=[_view_shape(t, d, BF16) for d in (1,) + VIEW_DILATIONS for _ in range(3)],
        scratch_shapes=[pltpu.VMEM((3, N_GROUPS, tm, LANES), F32)],
        compiler_params=_params(("arbitrary",), VMEM_LIMIT),
    )(qkvb, qw, kw, bd)


def _swa_prep_bwd(qkvb, qw, kw, bd, grads, tm):
    t = qkvb.shape[0]

    def body(x_ref, qw_ref, kw_ref, bd_ref, *rest):
        parts, (dx_ref, dqw_ref, dkw_ref, sc) = rest[:9], rest[9:]
        @pl.when(pl.program_id(0) == 0)
        def _():
            dqw_ref[...] = jnp.zeros_like(dqw_ref)
            dkw_ref[...] = jnp.zeros_like(dkw_ref)

        for i in range(3):
            for n, d in enumerate(VIEW_DILATIONS):
                _from_view(parts[3 * (n + 1) + i], sc, 2 * i + n, d, tm)
        for gidx in range(N_GROUPS):
            cols = slice(gidx * LANES, (gidx + 1) * LANES)
            for i, base, w_ref, dw_ref, scale in ((0, 0, qw_ref, dqw_ref, SWA_DIM ** -0.5),
                                                  (1, SWA_W, kw_ref, dkw_ref, 1.0)):
                xv = x_ref[:, base + gidx * LANES:base + (gidx + 1) * LANES]
                dy = (parts[i][:, cols] + sc[2 * i, gidx] + sc[2 * i + 1, gidx]) * scale
                r = lax.rsqrt(_head_mean(xv * xv, bd_ref) + EPS)
                xhat = xv * r
                dxh = dy * w_ref[:, cols]
                dx = r * (dxh - xhat * _head_mean(dxh * xhat, bd_ref))
                dx_ref[:, base + gidx * LANES:base + (gidx + 1) * LANES] = dx.astype(BF16)
                dw_ref[:, cols] += jnp.sum(dy * xhat, axis=0, keepdims=True)
            dx_ref[:, 2 * SWA_W + gidx * LANES:2 * SWA_W + (gidx + 1) * LANES] = (
                parts[2][:, cols] + sc[4, gidx] + sc[5, gidx]).astype(BF16)

    wrow = pl.BlockSpec((1, SWA_W), lambda i: (0, 0))
    return pl.pallas_call(
        body, name="swa_prep_bwd", grid=(t // tm,),
        in_specs=[pl.BlockSpec((tm, 3 * SWA_W), lambda i: (i, 0)), _resident((1, SWA_W)), _resident((1, SWA_W)),
                  _resident((LANES, LANES))] + [_view_spec(tm, d) for d in (1,) + VIEW_DILATIONS for _ in range(3)],
        out_specs=[pl.BlockSpec((tm, 3 * SWA_W), lambda i: (i, 0)), wrow, wrow],
        out_shape=[jax.ShapeDtypeStruct((t, 3 * SWA_W), BF16), jax.ShapeDtypeStruct((1, SWA_W), F32),
                   jax.ShapeDtypeStruct((1, SWA_W), F32)],
        scratch_shapes=[pltpu.VMEM((6, N_GROUPS, tm, LANES), F32)],
        compiler_params=_params(("arbitrary",), VMEM_LIMIT),
    )(qkvb, qw, kw, bd, *grads)


def _aligned(v, m):
    return v if isinstance(v, int) else pl.multiple_of(v, m)


BAND_GROUP = 2


def _band_loop(nsub, length, step, group=BAND_GROUP):
    step([(0, 0)], 0)
    if nsub > 2:
        assert (nsub - 2) % group == 0

        def inner(i, carry):
            s0 = 1 + i * group
            step([(s0 + e, pl.multiple_of((s0 + e) * QBLK - RADIUS, RADIUS)) for e in range(group)], 1)
            return carry
        lax.fori_loop(0, (nsub - 2) // group, inner, 0)
    step([(nsub - 1, length - KWIN)], 2)


def _head_select(lane, a0, a1):
    return jnp.where(lane < SWA_DIM, a0, a1)


def _swa_fwd(qv, kv, vv, bias, dilation, name):
    length = qv.shape[0]
    nsub = length // QBLK
    assert nsub >= 2 and length % QBLK == 0

    def body(q_ref, k_ref, v_ref, b_ref, o_ref, l_ref):
        lane = lax.broadcasted_iota(jnp.int32, (QBLK, LANES), 1)

        def step(blocks, var):
            items = []
            for s, ws in blocks:
                rows = pl.ds(_aligned(s * QBLK, QBLK), QBLK)
                q, kk, vw = q_ref[rows, :], k_ref[pl.ds(ws, KWIN), :], v_ref[pl.ds(ws, KWIN), :]
                for hh in range(2):
                    items.append((hh, jnp.where((lane < SWA_DIM) == (hh == 0), q, jnp.zeros_like(q)), kk, vw))
            lgs = [_dot_nt(qh, kk) + b_ref[hh, var] for hh, qh, kk, _ in items]
            ms = [jnp.max(lg, axis=-1, keepdims=True) for lg in lgs]
            ps = [jnp.exp(lg - m) for lg, m in zip(lgs, ms)]
            dens = [jnp.sum(p, axis=-1, keepdims=True) for p in ps]
            pvs = [_dot(p, it[3]) for p, it in zip(ps, items)]
            for n, (s, _) in enumerate(blocks):
                rows = pl.ds(_aligned(s * QBLK, QBLK), QBLK)
                o0, o1 = (pvs[2 * n + hh] / dens[2 * n + hh] for hh in range(2))
                l0, l1 = (ms[2 * n + hh] + jnp.log(dens[2 * n + hh]) for hh in range(2))
                o_ref[rows, :] = _head_select(lane, o0, o1)
                l_ref[rows, :] = _head_select(lane, l0, l1)

        _band_loop(nsub, length, step)

    blk = pl.BlockSpec((length, LANES), lambda hp, r: (0, r * (SWA_W // LANES) + hp))
    shp = jax.ShapeDtypeStruct(qv.shape, F32)
    return pl.pallas_call(
        body, name=name, grid=(SWA_W // LANES, dilation),
        in_specs=[blk, blk, blk, pl.BlockSpec((2, 3, QBLK, KWIN), lambda hp, r: (hp, 0, 0, 0))],
        out_specs=[blk, blk], out_shape=[shp, shp],
        compiler_params=_params(("arbitrary", "arbitrary"), VMEM_LIMIT),
    )(qv, kv, vv, bias)


def _swa_combine(os_, ls_, tm):
    t = os_[0].shape[0]

    def body(o0, o1, o2, l0, l1, l2, o_ref, ob_ref, la_ref, lb_ref, lc_ref, sc):
        for n, d in enumerate(VIEW_DILATIONS):
            _from_view((o1, o2)[n], sc, n, d, tm)
            _from_view((l1, l2)[n], sc, 2 + n, d, tm)
        for g in range(N_GROUPS):
            cols = slice(g * LANES, (g + 1) * LANES)
            la, lb, lc = l0[:, cols], sc[2, g], sc[3, g]
            m = jnp.maximum(jnp.maximum(la, lb), lc)
            tot = m + jnp.log(jnp.exp(la - m) + jnp.exp(lb - m) + jnp.exp(lc - m))
            o = jnp.exp(la - tot) * o0[:, cols] + jnp.exp(lb - tot) * sc[0, g] + jnp.exp(lc - tot) * sc[1, g]
            o_ref[:, cols] = o
            ob_ref[:, cols] = o.astype(BF16)
            la_ref[:, cols] = tot
            sc[4, g] = tot
        for n, d in enumerate(VIEW_DILATIONS):
            _to_view(sc, 4, (lb_ref, lc_ref)[n], d, tm)

    specs = [_view_spec(tm, d) for d in (1,) + VIEW_DILATIONS]
    return pl.pallas_call(
        body, name="swa_combine", grid=(t // tm,), in_specs=specs + specs, out_specs=[specs[0], specs[0]] + specs,
        out_shape=[jax.ShapeDtypeStruct((t, SWA_W), F32), jax.ShapeDtypeStruct((t, SWA_W), BF16)]
                  + [_view_shape(t, d, F32) for d in (1,) + VIEW_DILATIONS],
        scratch_shapes=[pltpu.VMEM((5, N_GROUPS, tm, LANES), F32)],
        compiler_params=_params(("arbitrary",), VMEM_LIMIT),
    )(*os_, *ls_)


def _swa_bwd_prep(do, o, bd, tm):
    t = do.shape[0]

    def body(d_ref, o_ref, bd_ref, dd1, dd4, dd16, db1, db4, db16, sc):
        for gidx in range(N_GROUPS):
            cols = slice(gidx * LANES, (gidx + 1) * LANES)
            dv = d_ref[:, cols]
            dd = _head_mean(dv * o_ref[:, cols], bd_ref) * float(SWA_DIM)
            sc[0, gidx] = dd
            sc[1, gidx] = dv
            dd1[:, cols] = dd
            db1[:, cols] = dv.astype(BF16)
        for n, d in enumerate(VIEW_DILATIONS):
            _to_view(sc, 0, (dd4, dd16)[n], d, tm)
            _to_view(sc, 1, (db4, db16)[n], d, tm)

    specs = [_view_spec(tm, d) for d in (1,) + VIEW_DILATIONS]
    return pl.pallas_call(
        body, name="swa_bwd_prep", grid=(t // tm,), in_specs=[specs[0], specs[0], _resident((LANES, LANES))],
        out_specs=specs + specs,
        out_shape=[_view_shape(t, d, F32) for d in (1,) + VIEW_DILATIONS]
                  + [_view_shape(t, d, BF16) for d in (1,) + VIEW_DILATIONS],
        scratch_shapes=[pltpu.VMEM((2, N_GROUPS, tm, LANES), F32)],
        compiler_params=_params(("arbitrary",), VMEM_LIMIT),
    )(do, o, bd)


def _swa_bwd(qv, kv, vv, dov, lv, ddv, bias_a, dilation, name):
    length = qv.shape[0]
    nsub = length // QBLK
    single = pl.Buffered(1) if dilation == 1 else None

    def body(q_ref, k_ref, v_ref, do_ref, l_ref, dd_ref, ba_ref, dq_ref, dk_ref, dv_ref, db_ref):
        @pl.when(pl.program_id(1) == 0)
        def _():
            db_ref[...] = jnp.zeros_like(db_ref)

        lane = lax.broadcasted_iota(jnp.int32, (QBLK, LANES), 1)
        lanew = lax.broadcasted_iota(jnp.int32, (KWIN, LANES), 1)

        def step(blocks, var):
            items = []
            for s, ws in blocks:
                rows = pl.ds(_aligned(s * QBLK, QBLK), QBLK)
                win = pl.ds(ws, KWIN)
                q, dov_ = q_ref[rows, :], do_ref[rows, :]
                kk, vw = k_ref[win, :], v_ref[win, :]
                lse, dd = l_ref[rows, :], dd_ref[rows, :]
                for hh in range(2):
                    mine = (lane < SWA_DIM) == (hh == 0)
                    col = slice(hh * SWA_DIM, hh * SWA_DIM + 1)
                    items.append((hh, jnp.where(mine, q, jnp.zeros_like(q)), jnp.where(mine, dov_, jnp.zeros_like(dov_)),
                                  kk, vw, lse[:, col], dd[:, col], q, dov_))
            lgs = [_dot_nt(it[1], it[3]) + ba_ref[it[0], var] for it in items]
            dps = [_dot_nt(it[2], it[4]) for it in items]
            ps = [jnp.exp(lg - it[5]) for lg, it in zip(lgs, items)]
            dss = [p * (dp - it[6]) for p, dp, it in zip(ps, dps, items)]
            dqs = [_dot(ds, it[3]) for ds, it in zip(dss, items)]
            dks = [_dot_tn(ds, it[7]) for ds, it in zip(dss, items)]
            dvs = [_dot_tn(p, it[8]) for p, it in zip(ps, items)]
            for n, (s, ws) in enumerate(blocks):
                rows = pl.ds(_aligned(s * QBLK, QBLK), QBLK)
                win = pl.ds(ws, KWIN)
                dq_ref[rows, :] = _head_select(lane, dqs[2 * n], dqs[2 * n + 1])
                dk_ref[win, :] += _head_select(lanew, dks[2 * n], dks[2 * n + 1])
                dv_ref[win, :] += _head_select(lanew, dvs[2 * n], dvs[2 * n + 1])
            for hh in range(2):
                tot = dss[hh]
                for n in range(1, len(blocks)):
                    tot = tot + dss[2 * n + hh]
                db_ref[hh, var] += tot

        dk_ref[...] = jnp.zeros_like(dk_ref)
        dv_ref[...] = jnp.zeros_like(dv_ref)
        _band_loop(nsub, length, step)

    imap = lambda hp, r: (0, r * (SWA_W // LANES) + hp)
    blk_in = pl.BlockSpec((length, LANES), imap, pipeline_mode=single)
    blk_out = pl.BlockSpec((length, LANES), imap)
    shp = jax.ShapeDtypeStruct(qv.shape, F32)
    return pl.pallas_call(
        body, name=name, grid=(SWA_W // LANES, dilation),
        in_specs=[blk_out] * 4 + [blk_in] * 2 + [pl.BlockSpec((2, 3, QBLK, KWIN), lambda hp, r: (hp, 0, 0, 0))],
        out_specs=[blk_out, blk_out, blk_out, pl.BlockSpec((2, 3, QBLK, KWIN), lambda hp, r: (hp, 0, 0, 0))],
        out_shape=[shp, shp, shp, jax.ShapeDtypeStruct((SWA_HEADS, 3, QBLK, KWIN), F32)],
        compiler_params=_params(("arbitrary", "arbitrary"), VMEM_LIMIT),
    )(qv, kv, vv, dov, lv, ddv, bias_a)


def _bias_grad(ds2, idx, tk):
    n = ds2.shape[1]
    nk = n // tk

    def body(a_ref, i_ref, o_ref):
        @pl.when(pl.program_id(0) == 0)
        def _():
            o_ref[...] = jnp.zeros_like(o_ref)

        oh = _onehot(i_ref[...], BF16)
        rest = a_ref[...]
        acc = jnp.zeros((SWA_HEADS, REL_BUCKETS), F32)
        for _ in range(3):
            piece = rest.astype(BF16)
            acc = acc + _dot_nt(piece, oh)
            rest = rest - piece.astype(F32)
        o_ref[...] += acc

    return pl.pallas_call(
        body, name="bias_grad", grid=(nk,),
        in_specs=[pl.BlockSpec((SWA_HEADS, tk), lambda k: (0, k)), pl.BlockSpec((1, tk), lambda k: (0, k))],
        out_specs=pl.BlockSpec((SWA_HEADS, REL_BUCKETS), lambda k: (0, 0)),
        out_shape=jax.ShapeDtypeStruct((SWA_HEADS, REL_BUCKETS), F32),
        compiler_params=_params(("arbitrary",), VMEM_LIMIT),
    )(ds2, idx)


def _swa_branch_fwd(qkvb, qw_t, kw_t, rel_bias, bd, tm):
    qkv = _swa_prep_fwd(qkvb, qw_t, kw_t, bd, tm)
    tables = _bias_tables(rel_bias, _band_index(), BIAS_TILE)
    os_, ls_, tabs = [], [], []
    for n, (_, d) in enumerate(PATTERNS):
        bias = tables[:, n * BAND_CELLS:(n + 1) * BAND_CELLS].reshape(SWA_HEADS, len(WIN_OFFSETS), QBLK, KWIN)
        o_p, l_p = _swa_fwd(*qkv[3 * n:3 * n + 3], bias, d, f"swa_fwd_d{d}")
        os_.append(o_p)
        ls_.append(l_p)
        tabs.append(bias)
    o, o16, *lses = _swa_combine(os_, ls_, tm)
    return o, o16, (qkv, lses, tabs)


def _swa_branch_bwd(do, o, saved, qkvb, qw_t, kw_t, bd, tm):
    qkv, lses, tabs = saved
    prep = _swa_bwd_prep(do, o, bd, tm)
    grads, dss = [], []
    for n, ((_, d), bias) in enumerate(zip(PATTERNS, tabs)):
        dq, dk, dv, ds = _swa_bwd(*qkv[3 * n:3 * n + 3], prep[3 + n], lses[n], prep[n], bias, d, f"swa_bwd_d{d}")
        grads += [dq, dk, dv]
        dss.append(ds.reshape(SWA_HEADS, -1))
    dqkvb, dqw, dkw = _swa_prep_bwd(qkvb, qw_t, kw_t, bd, grads, tm)
    dbias = _bias_grad(jnp.concatenate(dss, axis=1), _band_index(), BIAS_TILE)
    fold = lambda w: jnp.sum(w.reshape(SWA_HEADS, SWA_DIM), axis=0)
    return dqkvb, fold(dqw), fold(dkw), dbias.T


def _mesh_pos():
    return lax.axis_index("x"), lax.axis_index("y"), lax.axis_index("c")


def _other_chips(x, y):
    return [(1 - x, y), (x, 1 - y), (1 - x, 1 - y)]


def _remote(src, dst, send_sem, recv_sem, device):
    return pltpu.make_async_remote_copy(src_ref=src, dst_ref=dst, send_sem=send_sem, recv_sem=recv_sem,
                                        device_id=device, device_id_type=MESH)


def _split_axis(shape2):
    return 0 if (shape2[0] // 2) % 16 == 0 else 1


def _half_index(shape2, c):
    axis = _split_axis(shape2)
    h = shape2[axis] // 2
    return (pl.ds(c * h, h), slice(None)) if axis == 0 else (slice(None), pl.ds(c * h, h))


def _all_gather(xs):
    n = len(xs)

    def body(*refs):
        ins, outs = refs[:n], refs[n:2 * n]
        send_sems, recv_sems = refs[2 * n:]
        x, y, c = _mesh_pos()
        me = 2 * x + y
        chips = _other_chips(x, y)
        halves = []
        sends = []
        for a in range(n):
            h = ins[a].shape[0] // 2
            mine, other = pl.ds(c * h, h), pl.ds((1 - c) * h, h)
            halves.append((mine, other))
            own = _remote(ins[a], outs[a].at[me], send_sems.at[a, 6], recv_sems.at[a, 6], (x, y, 1 - c))
            own.start()
            sends.append(own)
            for j, chip in enumerate(chips):
                cp = _remote(ins[a].at[mine], outs[a].at[me, mine], send_sems.at[a, j], recv_sems.at[a, j], (*chip, c))
                cp.start()
                sends.append(cp)
        for a in range(n):
            mine, _ = halves[a]
            for j, chip in enumerate(chips):
                src = 2 * chip[0] + chip[1]
                landed = outs[a].at[src, mine]
                _remote(landed, landed, send_sems.at[a, j], recv_sems.at[a, j], (x, y, c)).wait_recv()
                fwd = _remote(landed, landed, send_sems.at[a, 3 + j], recv_sems.at[a, 3 + j], (x, y, 1 - c))
                fwd.start()
                sends.append(fwd)
        for a in range(n):
            _, other = halves[a]
            for j, chip in enumerate(chips):
                src = 2 * chip[0] + chip[1]
                landed = outs[a].at[src, other]
                _remote(landed, landed, send_sems.at[a, 3 + j], recv_sems.at[a, 3 + j], (x, y, c)).wait_recv()
            mine_slot = outs[a].at[me]
            _remote(mine_slot, mine_slot, send_sems.at[a, 6], recv_sems.at[a, 6], (x, y, c)).wait_recv()
        for cp in sends:
            cp.wait_send()

    return list(pl.pallas_call(
        body, name="all_gather_weights",
        in_specs=[ANY] * n, out_specs=[ANY] * n,
        out_shape=[jax.ShapeDtypeStruct((N_SHARDS,) + a.shape, a.dtype) for a in xs],
        scratch_shapes=[pltpu.SemaphoreType.DMA((n, 7)), pltpu.SemaphoreType.DMA((n, 7))],
    )(*xs))


def _rs_pair(gs):
    n = len(gs)

    def body(*refs):
        ins, lands = refs[:n], refs[n:2 * n]
        send_sems, recv_sems = refs[2 * n:]
        x, y, c = _mesh_pos()
        cps = []
        for a in range(n):
            theirs = (slice(None),) + _half_index(ins[a].shape[1:], 1 - c)
            cp = _remote(ins[a].at[theirs], lands[a], send_sems.at[a], recv_sems.at[a], (x, y, 1 - c))
            cp.start()
            cps.append(cp)
        for cp in cps:
            cp.wait()

    def half_shape(g):
        dims = list(g.shape)
        dims[1 + _split_axis(g.shape[1:])] //= 2
        return tuple(dims)

    return list(pl.pallas_call(
        body, name="rs_pair", in_specs=[ANY] * n, out_specs=[ANY] * n,
        out_shape=[jax.ShapeDtypeStruct(half_shape(g), g.dtype) for g in gs],
        scratch_shapes=[pltpu.SemaphoreType.DMA((n,)), pltpu.SemaphoreType.DMA((n,))],
    )(*gs))


def _pair_exchange(gs):
    def copies(cin, cout, send_sems, recv_sems):
        x, y, c = _mesh_pos()
        return [_remote(g.at[(slice(None),) + _half_index(g.shape[1:], 1 - c)], land, send_sems.at[a, 0],
                        recv_sems.at[a, 0], (x, y, 1 - c)) for a, (g, land) in enumerate(zip(cin, cout))]

    def start(*refs):
        for cp in copies(*refs):
            cp.start()

    def finish(*refs):
        for cp in copies(*refs):
            cp.wait()

    def half_shape(g):
        dims = list(g.shape)
        dims[1 + _split_axis(g.shape[1:])] //= 2
        return tuple(dims)

    return _Exchange(tuple(gs), tuple(jax.ShapeDtypeStruct(half_shape(g), g.dtype) for g in gs), start, finish)


def _rs_chips(ss):
    n = len(ss)

    def body(*refs):
        ins, outs = refs[:n], refs[n:2 * n]
        send_sems, recv_sems = refs[2 * n:]
        x, y, c = _mesh_pos()
        me = 2 * x + y
        chips = _other_chips(x, y)
        cps = []
        for a in range(n):
            for j, chip in enumerate(chips):
                dst_chip = 2 * chip[0] + chip[1]
                cp = _remote(ins[a].at[dst_chip], outs[a].at[me], send_sems.at[a, j], recv_sems.at[a, j], (*chip, c))
                cp.start()
                cps.append(cp)
        for a in range(n):
            for j, chip in enumerate(chips):
                src = 2 * chip[0] + chip[1]
                _remote(outs[a].at[src], outs[a].at[src], send_sems.at[a, j], recv_sems.at[a, j], (x, y, c)).wait_recv()
        for cp in cps:
            cp.wait_send()

    return list(pl.pallas_call(
        body, name="rs_chips", in_specs=[ANY] * n, out_specs=[ANY] * n,
        out_shape=[jax.ShapeDtypeStruct(s.shape, s.dtype) for s in ss],
        scratch_shapes=[pltpu.SemaphoreType.DMA((n, 3)), pltpu.SemaphoreType.DMA((n, 3))],
    )(*ss))


def _rs_join(fs, axes):
    n = len(fs)

    def whole(f, axis):
        dims = list(f.shape)
        dims[axis] *= 2
        return tuple(dims)

    def body(*refs):
        ins, outs = refs[:n], refs[n:2 * n]
        send_sems, recv_sems = refs[2 * n:]
        x, y, c = _mesh_pos()
        cps = []
        for a in range(n):
            h = ins[a].shape[axes[a]]
            mine = (pl.ds(c * h, h), slice(None)) if axes[a] == 0 else (slice(None), pl.ds(c * h, h))
            cp = _remote(ins[a], outs[a].at[mine], send_sems.at[a], recv_sems.at[a], (x, y, 1 - c))
            cp.start()
            cps.append(cp)
        for cp in cps:
            cp.wait()

    outs = pl.pallas_call(
        body, name="rs_join", in_specs=[ANY] * n, out_specs=[ANY] * n,
        out_shape=[jax.ShapeDtypeStruct(whole(f, ax), f.dtype) for f, ax in zip(fs, axes)],
        scratch_shapes=[pltpu.SemaphoreType.DMA((n,)), pltpu.SemaphoreType.DMA((n,))],
    )(*fs)
    c = lax.axis_index("c")
    return [lax.dynamic_update_slice_in_dim(o, f, c * f.shape[ax], ax) for o, f, ax in zip(outs, fs, axes)]


def _gather_exchange(xs):
    def start(cin, cout, send_sems, recv_sems):
        x, y, c = _mesh_pos()
        me = 2 * x + y
        for a, (src, dst) in enumerate(zip(cin, cout)):
            mine = _half_index(src.shape, c)
            for j, chip in enumerate(_other_chips(x, y)):
                _remote(src.at[mine], dst.at[(me,) + mine], send_sems.at[a, j], recv_sems.at[a, j], (*chip, c)).start()
            _remote(src, dst.at[me], send_sems.at[a, 3], recv_sems.at[a, 3], (x, y, 1 - c)).start()

    def finish(cin, cout, send_sems, recv_sems):
        x, y, c = _mesh_pos()
        for a, dst in enumerate(cout):
            for j, chip in enumerate(_other_chips(x, y)):
                landed = dst.at[(2 * chip[0] + chip[1],) + _half_index(dst.shape[1:], c)]
                _remote(landed, landed, send_sems.at[a, j], recv_sems.at[a, j], (x, y, c)).wait()
            own = dst.at[2 * x + y]
            _remote(own, own, send_sems.at[a, 3], recv_sems.at[a, 3], (x, y, c)).wait()

    return _Exchange(tuple(xs), tuple(jax.ShapeDtypeStruct((N_SHARDS,) + a.shape, a.dtype) for a in xs), start, finish)


def _gather_forward(gs):
    n = len(gs)

    def body(*refs):
        outs = refs[n:2 * n]
        send_sems, recv_sems = refs[2 * n:]
        x, y, c = _mesh_pos()
        chips = _other_chips(x, y)
        cps = []
        for a in range(n):
            for j, chip in enumerate(chips):
                landed = outs[a].at[(2 * chip[0] + chip[1],) + _half_index(outs[a].shape[1:], c)]
                cp = _remote(landed, landed, send_sems.at[a, j], recv_sems.at[a, j], (x, y, 1 - c))
                cp.start()
                cps.append(cp)
        for a in range(n):
            for j, chip in enumerate(chips):
                other = outs[a].at[(2 * chip[0] + chip[1],) + _half_index(outs[a].shape[1:], 1 - c)]
                _remote(other, other, send_sems.at[a, j], recv_sems.at[a, j], (x, y, c)).wait_recv()
        for cp in cps:
            cp.wait_send()

    return list(pl.pallas_call(
        body, name="gather_forward", in_specs=[ANY] * n, out_specs=[ANY] * n,
        out_shape=[jax.ShapeDtypeStruct(g.shape, g.dtype) for g in gs],
        input_output_aliases={i: i for i in range(n)},
        scratch_shapes=[pltpu.SemaphoreType.DMA((n, 3)), pltpu.SemaphoreType.DMA((n, 3))],
    )(*gs))


def _scatter_exchange(ss):
    def start(cin, cout, send_sems, recv_sems):
        x, y, c = _mesh_pos()
        me = 2 * x + y
        for a, (src, dst) in enumerate(zip(cin, cout)):
            for j, chip in enumerate(_other_chips(x, y)):
                _remote(src.at[2 * chip[0] + chip[1]], dst.at[me], send_sems.at[a, j], recv_sems.at[a, j],
                        (*chip, c)).start()

    def finish(cin, cout, send_sems, recv_sems):
        x, y, c = _mesh_pos()
        for a, dst in enumerate(cout):
            for j, chip in enumerate(_other_chips(x, y)):
                slot = dst.at[2 * chip[0] + chip[1]]
                _remote(slot, slot, send_sems.at[a, j], recv_sems.at[a, j], (x, y, c)).wait()

    return _Exchange(tuple(ss), tuple(jax.ShapeDtypeStruct(s.shape, s.dtype) for s in ss), start, finish)


def _add_pairs(gs, lands, name):
    n = len(gs)

    def body(*refs):
        c = lax.axis_index("c")
        for g_ref, l_ref, o_ref in zip(refs[:n], refs[n:2 * n], refs[2 * n:]):
            mine = g_ref[(0,) + _half_index(g_ref.shape[1:], c)]
            o_ref[0] = (mine.astype(F32) + l_ref[0].astype(F32)).astype(BF16)

    whole = [pl.BlockSpec((1,) + g.shape[1:], lambda j: (j, 0, 0)) for g in gs]
    half = [pl.BlockSpec((1,) + l.shape[1:], lambda j: (j, 0, 0)) for l in lands]
    return list(pl.pallas_call(body, name=name, grid=(gs[0].shape[0],), in_specs=whole + half, out_specs=half,
                               out_shape=[jax.ShapeDtypeStruct(l.shape, BF16) for l in lands],
                               compiler_params=_params(("arbitrary",), VMEM_LIMIT))(*gs, *lands))


def _sum_slots(slots, owns, name):
    n = len(slots)

    def body(*refs):
        me = 2 * lax.axis_index("x") + lax.axis_index("y")
        for s_ref, o_ref, out_ref in zip(refs[:n], refs[n:2 * n], refs[2 * n:]):
            acc = jnp.zeros(out_ref.shape, F32)
            for s in range(N_SHARDS):
                acc = acc + jnp.where(me == s, o_ref[s], s_ref[s]).astype(F32)
            out_ref[...] = acc

    def specs(a):
        _, h, c = a.shape
        if h % 32 == 0:
            return (pl.BlockSpec((N_SHARDS, h // 2, c), lambda i: (0, i, 0)), pl.BlockSpec((h // 2, c), lambda i: (i, 0)))
        return (pl.BlockSpec((N_SHARDS, h, c // 2), lambda i: (0, 0, i)), pl.BlockSpec((h, c // 2), lambda i: (0, i)))

    in_specs = [specs(a)[0] for a in slots]
    return list(pl.pallas_call(body, name=name, grid=(2,), in_specs=in_specs + in_specs,
                               out_specs=[specs(a)[1] for a in slots],
                               out_shape=[jax.ShapeDtypeStruct(a.shape[1:], F32) for a in slots],
                               compiler_params=_params(("arbitrary",), VMEM_LIMIT))(*slots, *owns))


def _all_reduce_small(p):
    r = p.shape[0]

    def body(p_ref, o_ref, buf, send_sems, recv_sems):
        x, y, c = _mesh_pos()
        me = 4 * x + 2 * y + c
        buf[me] = p_ref[...]
        cps = []
        k = 0
        for fx in range(2):
            for fy in range(2):
                for fc in range(2):
                    if fx + fy + fc == 0:
                        continue
                    peer = (1 - x if fx else x, 1 - y if fy else y, 1 - c if fc else c)
                    peer_id = 4 * peer[0] + 2 * peer[1] + peer[2]
                    cp = _remote(p_ref, buf.at[me], send_sems.at[k], recv_sems.at[k], peer)
                    cp.start()
                    cps.append((cp, peer_id, k))
                    k += 1
        for cp, peer_id, k in cps:
            _remote(p_ref, buf.at[peer_id], send_sems.at[k], recv_sems.at[k], (x, y, c)).wait_recv()
        for cp, _, _ in cps:
            cp.wait_send()
        acc = buf[0]
        for s in range(1, 8):
            acc = acc + buf[s]
        o_ref[...] = acc

    vm = pl.BlockSpec(memory_space=pltpu.VMEM)
    return pl.pallas_call(
        body, name="all_reduce_small", in_specs=[vm], out_specs=vm,
        out_shape=jax.ShapeDtypeStruct(p.shape, F32),
        scratch_shapes=[pltpu.VMEM((8, r, LANES), F32), pltpu.SemaphoreType.DMA((7,)), pltpu.SemaphoreType.DMA((7,))],
    )(p)


def _adamw(params, name, steps):
    c1 = 1.0 / (1.0 - ADAM_B1 ** ADAM_STEP)
    c2 = 1.0 / (1.0 - ADAM_B2 ** ADAM_STEP)
    n = len(params)

    def body(*refs):
        for a in range(n):
            w_ref, g_ref, m_ref, v_ref = refs[4 * a:4 * a + 4]
            d_ref, nm_ref, nv_ref = refs[4 * n + 3 * a:4 * n + 3 * a + 3]
            gv = g_ref[...]
            nm = ADAM_B1 * m_ref[...] + (1.0 - ADAM_B1) * gv
            nv = ADAM_B2 * v_ref[...] + (1.0 - ADAM_B2) * (gv * gv)
            d_ref[...] = -ADAM_LR * ((nm * c1) / (jnp.sqrt(nv * c2) + ADAM_EPS) + ADAM_WD * w_ref[...])
            nm_ref[...] = nm
            nv_ref[...] = nv

    def spec(shape):
        r, c = shape
        if r % (8 * steps) == 0:
            return pl.BlockSpec((r // steps, c), lambda i: (i, 0))
        assert c % (LANES * steps) == 0
        return pl.BlockSpec((r, c // steps), lambda i: (0, i))

    specs = [spec(w.shape) for w, _, _, _ in params]
    res = pl.pallas_call(
        body, name=name, grid=(steps,),
        in_specs=[s for s in specs for _ in range(4)], out_specs=[s for s in specs for _ in range(3)],
        out_shape=[jax.ShapeDtypeStruct(w.shape, F32) for w, _, _, _ in params for _ in range(3)],
        compiler_params=_params(("arbitrary",), VMEM_LIMIT))(*[a for p4 in params for a in p4])
    return [tuple(res[3 * a:3 * a + 3]) for a in range(n)]


PACK_UNIT = 8 * LANES


def _pack(arrs):
    parts = []
    for a in arrs:
        f = a.reshape(-1).astype(F32)
        parts.append(jnp.pad(f, (0, (-f.shape[0]) % PACK_UNIT)).reshape(-1, LANES))
    return jnp.concatenate(parts, axis=0)


def _unpack(m, shapes):
    outs, row = [], 0
    for s in shapes:
        n = int(np.prod(s))
        rows = -(-n // PACK_UNIT) * 8
        outs.append(m[row:row + rows].reshape(-1)[:n].reshape(s))
        row += rows
    return outs


WEIGHTS = ["ffn1_norm", "ffn1_w_gate", "ffn1_w_up", "ffn1_w_down", "mix_norm", "w_in", "conv_w", "a_log", "dt_bias",
           "gdn_norm_w", "q_norm_w", "k_norm_w", "rel_bias", "w_out", "ffn2_norm", "ffn2_w_gate", "ffn2_w_up",
           "ffn2_w_down", "final_norm"]
BIG = ["ffn1_w_gate", "ffn1_w_up", "ffn1_w_down", "w_in", "w_out", "ffn2_w_gate", "ffn2_w_up", "ffn2_w_down"]
SMALL = [n for n in WEIGHTS if n not in BIG]
COL_SHARDED = ["ffn1_w_gate", "ffn1_w_up", "w_in", "ffn2_w_gate", "ffn2_w_up"]
N_IN_COLS = 3600
TM = 256
TE = 512
ADAM_PIECES = 8
TK = 2048


def kernel(x, ffn1_norm, ffn1_w_gate, ffn1_w_up, ffn1_w_down, mix_norm, w_in, conv_w, a_log, dt_bias, gdn_norm_w, q_norm_w, k_norm_w, rel_bias, w_out, ffn2_norm, ffn2_w_gate, ffn2_w_up, ffn2_w_down, final_norm, loss_target, m_ffn1_norm, m_ffn1_w_gate, m_ffn1_w_up, m_ffn1_w_down, m_mix_norm, m_w_in, m_conv_w, m_a_log, m_dt_bias, m_gdn_norm_w, m_q_norm_w, m_k_norm_w, m_rel_bias, m_w_out, m_ffn2_norm, m_ffn2_w_gate, m_ffn2_w_up, m_ffn2_w_down, m_final_norm, v_ffn1_norm, v_ffn1_w_gate, v_ffn1_w_up, v_ffn1_w_down, v_mix_norm, v_w_in, v_conv_w, v_a_log, v_dt_bias, v_gdn_norm_w, v_q_norm_w, v_k_norm_w, v_rel_bias, v_w_out, v_ffn2_norm, v_ffn2_w_gate, v_ffn2_w_up, v_ffn2_w_down, v_final_norm):
    p = dict(locals())
    xs, target = x[0], loss_target[0]
    t, d = xs.shape
    nc = t // CHUNK
    tk = min(TK, t)
    tkf = min(2 * TK, t)
    me = 2 * lax.axis_index("x") + lax.axis_index("y")

    first = ["ffn1_w_gate", "ffn1_w_up", "ffn1_w_down"]
    later = [n for n in BIG if n not in first] + ["conv_w"]
    local = lambda n, a: a[0].T if n in COL_SHARDED else a[0]
    shards = {n: local(n, p[n]).astype(BF16) for n in BIG}
    shards["conv_w"] = conv_w[0]
    gw = dict(zip(first, _all_gather([shards[n] for n in first])))
    f1 = (gw["ffn1_w_gate"], gw["ffn1_w_up"], gw["ffn1_w_down"])
    (x1, xn1, g1, u1), landed = _ffn_fwd(xs, ffn1_norm, *f1, TE, "ffn1_fwd",
                                         exchange=_gather_exchange([shards[n] for n in later]))
    gw.update(zip(later, _gather_forward(landed)))
    wp = gw["w_in"].reshape(N_IN_COLS, d)
    w_out_full = gw["w_out"].reshape(d, d)
    conv_rows = conv_w.shape[1]
    cw = jnp.pad(gw["conv_w"].reshape(N_SHARDS * conv_rows, CONV_TAPS).T, ((0, 8 - CONV_TAPS), (0, 0)))
    gp = jnp.pad(jnp.stack([a_log.reshape(8), dt_bias.reshape(8)]), ((0, 6), (0, LANES - 8)))
    gdn_w = gdn_norm_w.reshape(1, GDN_DIM)
    qw_t = jnp.tile(q_norm_w.reshape(1, SWA_DIM), (1, SWA_HEADS))
    kw_t = jnp.tile(k_norm_w.reshape(1, SWA_DIM), (1, SWA_HEADS))
    bd = jnp.asarray(np.kron(np.eye(2), np.full((SWA_DIM, SWA_DIM), 1.0 / SWA_DIM)), BF16)
    f2 = (gw["ffn2_w_gate"], gw["ffn2_w_up"], gw["ffn2_w_down"])

    hn, qkva, z, ab, qkvb = _mix_in_fwd(x1, mix_norm, wp, TE)
    qkvc, gb = _gdn_prep_fwd(qkva, cw, ab, gp, TE)
    gbt = jnp.transpose(gb[:, :16].reshape(nc, CHUNK, 16), (0, 2, 1))
    o_f, o_b, gdn_saved = _gdn_fwd(qkvc, gb, gbt)
    oa = _gdn_post_fwd(o_f, o_b, z, gdn_w, TE)
    o_swa, o_swa16, swa_saved = _swa_branch_fwd(qkvb, qw_t, kw_t, rel_bias, bd, TE)
    x2 = _mix_out_fwd(x1, oa, o_swa, w_out_full, TE)
    (dx3, xn2, g2, u2, loss_part, d_final), _ = _ffn_fwd(x2, ffn2_norm, *f2, TE, "ffn2_fwd", head=(final_norm, target))

    def pair_sums(partials, tag):
        return _add_pairs(partials, _rs_pair(partials), f"rs_add_{tag}")

    (dx2, dyh2, dg2, du2, h2, d_nw2), _ = _ffn_bwd_dx(dx3, x2, ffn2_norm, g2, u2, *f2, TM, "ffn2_bwd_dx")
    dwg2 = _matmul_tn(dg2, xn2, tkf, "ffn2_dwg")
    dwu2 = _matmul_tn(du2, xn2, tkf, "ffn2_dwu")
    dwd2 = _matmul_tn(h2, dyh2, tkf, "ffn2_dwd")
    (doa, dob, dx2b), lands_f2 = _mix_out_bwd(dx2, w_out_full, TE, exchange=_pair_exchange([dwg2, dwu2, dwd2]))
    sums_f2 = _add_pairs([dwg2, dwu2, dwd2], lands_f2, "rs_add_a")
    dwo = jnp.concatenate([_matmul_tn(oa, dx2b, tk, "w_out_dw_a")[0], _matmul_tn(o_swa16, dx2b, tk, "w_out_dw_b")[0]],
                          axis=0).reshape(N_SHARDS, d // N_SHARDS, d)
    do_g, dz, d_gdnw = _gdn_post_bwd(doa, o_f, o_b, z, gdn_w, TE)
    (dqkvc, dgates), slots_f2 = _gdn_bwd(qkvc, gb, gbt, do_g, gdn_saved, exchange=_scatter_exchange(sums_f2))
    dqkva, dab, dcw, dgp = _gdn_prep_bwd(qkva, cw, ab, gp, dqkvc, dgates, TM)
    dqkvb, d_qw, d_kw, d_rel = _swa_branch_bwd(dob, o_swa, swa_saved, qkvb, qw_t, kw_t, bd, TE)
    dpieces = (dqkva, dz, dab, dqkvb)
    dwp = [_matmul_tn(dp, hn, tk, f"w_in_dw_{i}")[0] for i, dp in enumerate(dpieces)]
    dw_in = jnp.concatenate([dwp[0], dwp[1], dwp[2][:N_GATE_COLS], dwp[3]], axis=0)
    dw_in = dw_in.reshape(N_SHARDS, N_IN_COLS // N_SHARDS, d)
    sums_mix = pair_sums([dw_in, dwo], "b")
    (dx1, d_mixnw), slots_mix = _mix_in_bwd_dx(dx2, x1, mix_norm, dpieces, wp, TE, exchange=_scatter_exchange(sums_mix))
    (gx, dyh1, dg1, du1, h1, d_nw1), _ = _ffn_bwd_dx(dx1, xs, ffn1_norm, g1, u1, *f1, TM, "ffn1_bwd_dx")
    dwg1 = _matmul_tn(dg1, xn1, tkf, "ffn1_dwg")
    dwu1 = _matmul_tn(du1, xn1, tkf, "ffn1_dwu")
    sums_gu = pair_sums([dwg1, dwu1], "c")
    dwd1, slots_gu = _matmul_tn(h1, dyh1, tkf, "ffn1_dwd", exchange=_scatter_exchange(sums_gu))
    sums_d = pair_sums([dwd1], "d")
    slots = slots_gu + _rs_chips(sums_d) + slots_mix + slots_f2
    sums = sums_gu + sums_d + sums_mix + sums_f2
    halves = _sum_slots(slots[:4], sums[:4], "rs_sum_a") + _sum_slots(slots[4:], sums[4:], "rs_sum_b")
    g_big = dict(zip(BIG, _rs_join(halves, [_split_axis(shards[n].shape) for n in BIG])))

    small_partial = {"ffn1_norm": d_nw1, "mix_norm": d_mixnw, "a_log": dgp[0, 0:8], "dt_bias": dgp[1, 0:8],
                     "gdn_norm_w": d_gdnw, "q_norm_w": d_qw, "k_norm_w": d_kw, "rel_bias": d_rel,
                     "ffn2_norm": d_nw2, "final_norm": d_final, "conv_w": dcw[0:CONV_TAPS].T}
    red = _all_reduce_small(_pack([small_partial[n] for n in SMALL] + [loss_part[0, 0:1]]))
    full_shapes = [p[n].shape if n != "conv_w" else (N_SHARDS * conv_rows, CONV_TAPS) for n in SMALL]
    red_parts = _unpack(red, full_shapes + [(1,)])
    loss = red_parts[-1].reshape(())
    g_small = dict(zip(SMALL, red_parts[:-1]))
    g_small["conv_w"] = lax.dynamic_slice_in_dim(g_small["conv_w"], me * conv_rows, conv_rows, 0).reshape(conv_w.shape)

    grads, deltas, new_m, new_v = {}, {}, {}, {}
    quad = lambda n: (local(n, p[n]), g_big[n], local(n, p["m_" + n]), local(n, p["v_" + n]))
    updates = (_adamw([quad(n) for n in BIG[:4]], "adamw_a", ADAM_PIECES)
               + _adamw([quad(n) for n in BIG[4:]], "adamw_b", ADAM_PIECES))
    for n, (dl, nm, nv) in zip(BIG, updates):
        back = (lambda a: a.T[None]) if n in COL_SHARDED else (lambda a: a[None])
        grads[n], deltas[n], new_m[n], new_v[n] = back(g_big[n]), back(dl), back(nm), back(nv)
    packed = [_pack([src[n] for n in SMALL]) for src in
              ({n: p[n] for n in SMALL}, g_small, {n: p["m_" + n] for n in SMALL}, {n: p["v_" + n] for n in SMALL})]
    small_shapes = [p[n].shape for n in SMALL]
    for dst, arr in zip((deltas, new_m, new_v), _adamw([tuple(packed)], "adamw_small", 1)[0]):
        dst.update(zip(SMALL, _unpack(arr, small_shapes)))
    grads.update(g_small)

    return (loss, gx[None], *[grads[n] for n in WEIGHTS], *[deltas[n] for n in WEIGHTS],
            *[new_m[n] for n in WEIGHTS], *[new_v[n] for n in WEIGHTS])
```
